```python
import jax, jax.numpy as jnp
from jax import lax
import numpy as np

D_MODEL = 1024
BATCH = 8
SEQ = 2048
DEPTH = 1

GRID_W = 64
CTX_LEN = 256
N_HEADS = 8
QK_NOPE_DIM = 64
QK_ROPE_DIM = 32
V_HEAD_DIM = 64
Q_LORA_RANK = 384
KV_LORA_RANK = 256
ROPE_THETA = 10000.0
CONV_DIM = 512
CONV_WIDTH = 3
D_FF = 2816
Q_BLOCK = 128
EPS = 1e-6
ATTN_DIM = N_HEADS * V_HEAD_DIM
KV_END = KV_LORA_RANK
KR_END = KV_END + QK_ROPE_DIM
Q_END = KR_END + Q_LORA_RANK
CX_END = Q_END + CONV_DIM
CB_END = CX_END + CONV_DIM
CC_END = CB_END + CONV_DIM
GA_END = CC_END + D_MODEL
GC_END = GA_END + D_MODEL
IN_COLS = GC_END

kernel_name = 'hybrid_mla_shortconv_convffn_dit_block'


def rmsnorm(x, g):
    xf = x.astype(jnp.float32)
    y = xf * lax.rsqrt(jnp.mean(xf * xf, axis=-1, keepdims=True) + EPS)
    return (y * g.astype(jnp.float32)).astype(x.dtype)


def modulate(h, shift, scale):
    return h * (1 + scale[:, None, :]) + shift[:, None, :]


def dwconv(x, w, b):
    T = x.shape[1]
    pad = CONV_WIDTH // 2
    xp = jnp.pad(x, ((0, 0), (pad, CONV_WIDTH - 1 - pad), (0, 0)))
    y = b
    for k in range(CONV_WIDTH):
        y = y + xp[:, k:k + T] * w[k]
    return y


def axial_angles(rows):
    row = jnp.repeat(jnp.arange(rows), GRID_W).astype(jnp.float32)
    col = jnp.tile(jnp.arange(GRID_W), rows).astype(jnp.float32)
    axis_dim = QK_ROPE_DIM // 2
    inv = ROPE_THETA ** (-jnp.arange(0, axis_dim, 2, dtype=jnp.float32) / axis_dim)
    return jnp.concatenate([row[:, None] * inv, col[:, None] * inv], axis=-1)


def rope2d(x, ang):
    nf = QK_ROPE_DIM // 4
    xr = x.reshape(*x.shape[:-1], 2, 2, nf)
    x1, x2 = xr[..., 0, :], xr[..., 1, :]
    a = ang.reshape(ang.shape[0], 1, 2, nf)
    cos, sin = jnp.cos(a).astype(x.dtype), jnp.sin(a).astype(x.dtype)
    out = jnp.stack([x1 * cos - x2 * sin, x1 * sin + x2 * cos], axis=-2)
    return out.reshape(x.shape)


def mla_kv(p, kv_norm_g, w_ukv, ang):
    B, T = p.shape[:2]
    ckv = rmsnorm(p[..., :KV_END], kv_norm_g)
    kv = (ckv @ w_ukv).reshape(B, T, N_HEADS, QK_NOPE_DIM + V_HEAD_DIM)
    k_nope, v = kv[..., :QK_NOPE_DIM], kv[..., QK_NOPE_DIM:]
    k_rope = p[..., KV_END:KR_END][:, :, None, :]
    if ang is not None:
        k_rope = rope2d(k_rope, ang)
    return k_nope, k_rope[:, :, 0], v


def mla_query(p, q_norm_g, w_uq, ang):
    B, T = p.shape[:2]
    cq = rmsnorm(p[..., KR_END:Q_END], q_norm_g)
    q = (cq @ w_uq).reshape(B, T, N_HEADS, QK_NOPE_DIM + QK_ROPE_DIM)
    q_nope, q_rope = q[..., :QK_NOPE_DIM], q[..., QK_NOPE_DIM:]
    if ang is not None:
        q_rope = rope2d(q_rope, ang)
    return q_nope, q_rope


def mla_attend(q_nope, q_rope, k_nope, k_rope, v):
    scale = (QK_NOPE_DIM + QK_ROPE_DIM) ** -0.5
    s = (jnp.einsum('bqhd,bkhd->bhqk', q_nope, k_nope)
         + jnp.einsum('bqhr,bkr->bhqk', q_rope, k_rope))
    pr = jax.nn.softmax(s.astype(jnp.float32) * scale, axis=-1).astype(v.dtype)
    return jnp.einsum('bhqk,bkhd->bqhd', pr, v)


def mla_latent(q_nope, q_rope, k_nope, k_rope, v):
    B, T = q_nope.shape[:2]
    nb = T // Q_BLOCK

    def blocks(a):
        return a.reshape(B, nb, Q_BLOCK, *a.shape[2:]).swapaxes(0, 1)

    o = lax.map(lambda qs: mla_attend(qs[0], qs[1], k_nope, k_rope, v), (blocks(q_nope), blocks(q_rope)))
    return o.swapaxes(0, 1).reshape(B, T, ATTN_DIM)


def gated_merge(p, o_attn, conv_w, conv_b, w_attn_out, w_conv_out, w_o):
    x_in = p[..., Q_END:CX_END]
    b_gate = p[..., CX_END:CB_END]
    c_gate = p[..., CB_END:CC_END]
    y_conv = (b_gate * dwconv(c_gate * x_in, conv_w, conv_b)) @ w_conv_out
    y_attn = o_attn @ w_attn_out
    g_attn = jax.nn.sigmoid(p[..., CC_END:GA_END])
    g_conv = jax.nn.sigmoid(p[..., GA_END:GC_END])
    return (g_attn * y_attn + g_conv * y_conv) @ w_o


def conv_ffn(h, w_up, conv_w, conv_b, w_down):
    u = dwconv(h @ w_up, conv_w, conv_b)
    gate, val = jnp.split(u, 2, axis=-1)
    return (jax.nn.silu(gate) * val) @ w_down


def trunk_layer(x, ctx, mod_lat, mod_ctx, ang, update_ctx, norm1_g, w_in, q_norm_g, kv_norm_g,
                w_uq, w_ukv, conv_w, conv_b, w_attn_out, w_conv_out, w_o, norm2_g, w_up,
                ffn_conv_w, ffn_conv_b, w_down):
    sh1, sc1, g1, sh2, sc2, g2 = jnp.split(mod_lat, 6, axis=-1)
    csh1, csc1, cg1, csh2, csc2, cg2 = jnp.split(mod_ctx, 6, axis=-1)
    h = modulate(rmsnorm(x, norm1_g), sh1, sc1)
    hc = modulate(rmsnorm(ctx, norm1_g), csh1, csc1)
    p = h @ w_in
    pc = hc @ w_in if update_ctx else hc @ w_in[:, :KR_END]
    kn_c, kr_c, v_c = mla_kv(pc, kv_norm_g, w_ukv, None)
    kn_l, kr_l, v_l = mla_kv(p, kv_norm_g, w_ukv, ang)
    qn, qr = mla_query(p, q_norm_g, w_uq, ang)
    o = mla_latent(qn, qr, jnp.concatenate([kn_c, kn_l], axis=1),
                   jnp.concatenate([kr_c, kr_l], axis=1), jnp.concatenate([v_c, v_l], axis=1))
    x = x + g1[:, None, :] * gated_merge(p, o, conv_w, conv_b, w_attn_out, w_conv_out, w_o)
    x = x + g2[:, None, :] * conv_ffn(modulate(rmsnorm(x, norm2_g), sh2, sc2), w_up, ffn_conv_w, ffn_conv_b, w_down)
    if update_ctx:
        qn_c, qr_c = mla_query(pc, q_norm_g, w_uq, None)
        oc = mla_attend(qn_c, qr_c, kn_c, kr_c, v_c).reshape(ctx.shape[0], ctx.shape[1], ATTN_DIM)
        ctx = ctx + cg1[:, None, :] * gated_merge(pc, oc, conv_w, conv_b, w_attn_out, w_conv_out, w_o)
        ctx = ctx + cg2[:, None, :] * conv_ffn(modulate(rmsnorm(ctx, norm2_g), csh2, csc2), w_up, ffn_conv_w, ffn_conv_b, w_down)
    return x, ctx


def _fwd_setup_inputs(seed: int = 0) -> dict:
    key = jax.random.key(seed)
    ks = jax.random.split(key, 24)
    L, D = DEPTH, D_MODEL

    def nrm(k, shape, fan_in, gain=1.0):
        return jax.random.normal(k, shape, jnp.float32) * (gain * fan_in ** -0.5)

    def norm_gain(k, shape):
        return 1.0 + 0.05 * jax.random.normal(k, shape, jnp.float32)

    def bias(k, shape):
        return 0.02 * jax.random.normal(k, shape, jnp.float32)

    return {
        'x': jax.random.normal(ks[0], (BATCH, SEQ, D), jnp.float32),
        'c': jax.random.normal(ks[1], (BATCH, D), jnp.float32),
        'ctx': jax.random.normal(ks[2], (BATCH, CTX_LEN, D), jnp.float32),
        'c_ctx': jax.random.normal(ks[3], (D,), jnp.float32),
        'w_ada': nrm(ks[4], (L, D, 6 * D), D, 0.5),
        'b_ada': bias(ks[5], (L, 6 * D)),
        'norm1_g': norm_gain(ks[6], (L, D)),
        'w_in': nrm(ks[7], (L, D, IN_COLS), D),
        'q_norm_g': norm_gain(ks[8], (L, Q_LORA_RANK)),
        'kv_norm_g': norm_gain(ks[9], (L, KV_LORA_RANK)),
        'w_uq': nrm(ks[10], (L, Q_LORA_RANK, N_HEADS * (QK_NOPE_DIM + QK_ROPE_DIM)), Q_LORA_RANK),
        'w_ukv': nrm(ks[11], (L, KV_LORA_RANK, N_HEADS * (QK_NOPE_DIM + V_HEAD_DIM)), KV_LORA_RANK),
        'conv_w': nrm(ks[12], (L, CONV_WIDTH, CONV_DIM), CONV_WIDTH),
        'conv_b': bias(ks[13], (L, CONV_DIM)),
        'w_attn_out': nrm(ks[14], (L, ATTN_DIM, D), ATTN_DIM),
        'w_conv_out': nrm(ks[15], (L, CONV_DIM, D), CONV_DIM),
        'w_o': nrm(ks[16], (L, D, D), D),
        'norm2_g': norm_gain(ks[17], (L, D)),
        'w_up': nrm(ks[18], (L, D, 2 * D_FF), D),
        'ffn_conv_w': nrm(ks[19], (L, CONV_WIDTH, 2 * D_FF), CONV_WIDTH),
        'ffn_conv_b': bias(ks[20], (L, 2 * D_FF)),
        'w_down': nrm(ks[21], (L, D_FF, D), D_FF),
        'final_g': norm_gain(ks[22], (D,)),
    }


def _fwd_reference(x, c, ctx, c_ctx, w_ada, b_ada, norm1_g, w_in, q_norm_g, kv_norm_g, w_uq, w_ukv,
              conv_w, conv_b, w_attn_out, w_conv_out, w_o, norm2_g, w_up, ffn_conv_w, ffn_conv_b,
              w_down, final_g):
    rows = x.shape[1] // GRID_W
    ang = axial_angles(rows)
    sc = jax.nn.silu(c)
    sc_ctx = jax.nn.silu(c_ctx)[None, :]
    for i in range(DEPTH):
        mod_lat = sc @ w_ada[i] + b_ada[i]
        mod_ctx = sc_ctx @ w_ada[i] + b_ada[i]
        x, ctx = trunk_layer(x, ctx, mod_lat, mod_ctx, ang, i < DEPTH - 1, norm1_g[i], w_in[i],
                             q_norm_g[i], kv_norm_g[i], w_uq[i], w_ukv[i], conv_w[i], conv_b[i],
                             w_attn_out[i], w_conv_out[i], w_o[i], norm2_g[i], w_up[i],
                             ffn_conv_w[i], ffn_conv_b[i], w_down[i])
    return rmsnorm(x, final_g)


import jax as _jax
import jax.numpy as _jnp

TWIN_FORMAT = 'train_step'
FWD_PARAMS = ['x', 'c', 'ctx', 'c_ctx', 'w_ada', 'b_ada', 'norm1_g', 'w_in', 'q_norm_g', 'kv_norm_g', 'w_uq', 'w_ukv', 'conv_w', 'conv_b', 'w_attn_out', 'w_conv_out', 'w_o', 'norm2_g', 'w_up', 'ffn_conv_w', 'ffn_conv_b', 'w_down', 'final_g']
TWIN_WEIGHTS = ['c_ctx', 'w_ada', 'b_ada', 'norm1_g', 'w_in', 'q_norm_g', 'kv_norm_g', 'w_uq', 'w_ukv', 'conv_w', 'conv_b', 'w_attn_out', 'w_conv_out', 'w_o', 'norm2_g', 'w_up', 'ffn_conv_w', 'ffn_conv_b', 'w_down', 'final_g']
TWIN_DIFF_INPUT = 'x'
TWIN_INPUTS = ['x', 'c', 'ctx', 'c_ctx', 'w_ada', 'b_ada', 'norm1_g', 'w_in', 'q_norm_g', 'kv_norm_g', 'w_uq', 'w_ukv', 'conv_w', 'conv_b', 'w_attn_out', 'w_conv_out', 'w_o', 'norm2_g', 'w_up', 'ffn_conv_w', 'ffn_conv_b', 'w_down', 'final_g', 'loss_target', 'm_c_ctx', 'm_w_ada', 'm_b_ada', 'm_norm1_g', 'm_w_in', 'm_q_norm_g', 'm_kv_norm_g', 'm_w_uq', 'm_w_ukv', 'm_conv_w', 'm_conv_b', 'm_w_attn_out', 'm_w_conv_out', 'm_w_o', 'm_norm2_g', 'm_w_up', 'm_ffn_conv_w', 'm_ffn_conv_b', 'm_w_down', 'm_final_g', 'v_c_ctx', 'v_w_ada', 'v_b_ada', 'v_norm1_g', 'v_w_in', 'v_q_norm_g', 'v_kv_norm_g', 'v_w_uq', 'v_w_ukv', 'v_conv_w', 'v_conv_b', 'v_w_attn_out', 'v_w_conv_out', 'v_w_o', 'v_norm2_g', 'v_w_up', 'v_ffn_conv_w', 'v_ffn_conv_b', 'v_w_down', 'v_final_g']
TWIN_OUTPUTS = ['loss', 'grad_x', 'grad_c_ctx', 'grad_w_ada', 'grad_b_ada', 'grad_norm1_g', 'grad_w_in', 'grad_q_norm_g', 'grad_kv_norm_g', 'grad_w_uq', 'grad_w_ukv', 'grad_conv_w', 'grad_conv_b', 'grad_w_attn_out', 'grad_w_conv_out', 'grad_w_o', 'grad_norm2_g', 'grad_w_up', 'grad_ffn_conv_w', 'grad_ffn_conv_b', 'grad_w_down', 'grad_final_g', 'delta_c_ctx', 'delta_w_ada', 'delta_b_ada', 'delta_norm1_g', 'delta_w_in', 'delta_q_norm_g', 'delta_kv_norm_g', 'delta_w_uq', 'delta_w_ukv', 'delta_conv_w', 'delta_conv_b', 'delta_w_attn_out', 'delta_w_conv_out', 'delta_w_o', 'delta_norm2_g', 'delta_w_up', 'delta_ffn_conv_w', 'delta_ffn_conv_b', 'delta_w_down', 'delta_final_g', 'new_m_c_ctx', 'new_m_w_ada', 'new_m_b_ada', 'new_m_norm1_g', 'new_m_w_in', 'new_m_q_norm_g', 'new_m_kv_norm_g', 'new_m_w_uq', 'new_m_w_ukv', 'new_m_conv_w', 'new_m_conv_b', 'new_m_w_attn_out', 'new_m_w_conv_out', 'new_m_w_o', 'new_m_norm2_g', 'new_m_w_up', 'new_m_ffn_conv_w', 'new_m_ffn_conv_b', 'new_m_w_down', 'new_m_final_g', 'new_v_c_ctx', 'new_v_w_ada', 'new_v_b_ada', 'new_v_norm1_g', 'new_v_w_in', 'new_v_q_norm_g', 'new_v_kv_norm_g', 'new_v_w_uq', 'new_v_w_ukv', 'new_v_conv_w', 'new_v_conv_b', 'new_v_w_attn_out', 'new_v_w_conv_out', 'new_v_w_o', 'new_v_norm2_g', 'new_v_w_up', 'new_v_ffn_conv_w', 'new_v_ffn_conv_b', 'new_v_w_down', 'new_v_final_g']
TWIN_LEAF_KINDS = {'loss': 'loss', 'grad_x': 'grad_x', 'grad_c_ctx': 'grad_w', 'grad_w_ada': 'grad_w', 'grad_b_ada': 'grad_w', 'grad_norm1_g': 'grad_w', 'grad_w_in': 'grad_w', 'grad_q_norm_g': 'grad_w', 'grad_kv_norm_g': 'grad_w', 'grad_w_uq': 'grad_w', 'grad_w_ukv': 'grad_w', 'grad_conv_w': 'grad_w', 'grad_conv_b': 'grad_w', 'grad_w_attn_out': 'grad_w', 'grad_w_conv_out': 'grad_w', 'grad_w_o': 'grad_w', 'grad_norm2_g': 'grad_w', 'grad_w_up': 'grad_w', 'grad_ffn_conv_w': 'grad_w', 'grad_ffn_conv_b': 'grad_w', 'grad_w_down': 'grad_w', 'grad_final_g': 'grad_w', 'delta_c_ctx': 'delta_w', 'delta_w_ada': 'delta_w', 'delta_b_ada': 'delta_w', 'delta_norm1_g': 'delta_w', 'delta_w_in': 'delta_w', 'delta_q_norm_g': 'delta_w', 'delta_kv_norm_g': 'delta_w', 'delta_w_uq': 'delta_w', 'delta_w_ukv': 'delta_w', 'delta_conv_w': 'delta_w', 'delta_conv_b': 'delta_w', 'delta_w_attn_out': 'delta_w', 'delta_w_conv_out': 'delta_w', 'delta_w_o': 'delta_w', 'delta_norm2_g': 'delta_w', 'delta_w_up': 'delta_w', 'delta_ffn_conv_w': 'delta_w', 'delta_ffn_conv_b': 'delta_w', 'delta_w_down': 'delta_w', 'delta_final_g': 'delta_w', 'new_m_c_ctx': 'new_m', 'new_m_w_ada': 'new_m', 'new_m_b_ada': 'new_m', 'new_m_norm1_g': 'new_m', 'new_m_w_in': 'new_m', 'new_m_q_norm_g': 'new_m', 'new_m_kv_norm_g': 'new_m', 'new_m_w_uq': 'new_m', 'new_m_w_ukv': 'new_m', 'new_m_conv_w': 'new_m', 'new_m_conv_b': 'new_m', 'new_m_w_attn_out': 'new_m', 'new_m_w_conv_out': 'new_m', 'new_m_w_o': 'new_m', 'new_m_norm2_g': 'new_m', 'new_m_w_up': 'new_m', 'new_m_ffn_conv_w': 'new_m', 'new_m_ffn_conv_b': 'new_m', 'new_m_w_down': 'new_m', 'new_m_final_g': 'new_m', 'new_v_c_ctx': 'new_v', 'new_v_w_ada': 'new_v', 'new_v_b_ada': 'new_v', 'new_v_norm1_g': 'new_v', 'new_v_w_in': 'new_v', 'new_v_q_norm_g': 'new_v', 'new_v_kv_norm_g': 'new_v', 'new_v_w_uq': 'new_v', 'new_v_w_ukv': 'new_v', 'new_v_conv_w': 'new_v', 'new_v_conv_b': 'new_v', 'new_v_w_attn_out': 'new_v', 'new_v_w_conv_out': 'new_v', 'new_v_w_o': 'new_v', 'new_v_norm2_g': 'new_v', 'new_v_w_up': 'new_v', 'new_v_ffn_conv_w': 'new_v', 'new_v_ffn_conv_b': 'new_v', 'new_v_w_down': 'new_v', 'new_v_final_g': 'new_v'}


def _forward(args):
    return _fwd_reference(*[args[k] for k in FWD_PARAMS])


def _output_shape():
    out = _jax.eval_shape(lambda: _forward(_fwd_setup_inputs(0)))
    return out.shape, out.dtype

N_MICROBATCH = 1
ADAM_LR = 0.001
ADAM_B1 = 0.9
ADAM_B2 = 0.999
ADAM_EPS = 1e-08
ADAM_WD = 0.01
ADAM_STEP = 10
PER_EXAMPLE_BATCH_AXIS = {'x': 0, 'c': 0, 'ctx': 0, 'loss_target': 0}
SHARED_INPUTS = []
_WEIGHT_DTYPES = {'c_ctx': _jnp.float32, 'w_ada': _jnp.float32, 'b_ada': _jnp.float32, 'norm1_g': _jnp.float32, 'w_in': _jnp.float32, 'q_norm_g': _jnp.float32, 'kv_norm_g': _jnp.float32, 'w_uq': _jnp.float32, 'w_ukv': _jnp.float32, 'conv_w': _jnp.float32, 'conv_b': _jnp.float32, 'w_attn_out': _jnp.float32, 'w_conv_out': _jnp.float32, 'w_o': _jnp.float32, 'norm2_g': _jnp.float32, 'w_up': _jnp.float32, 'ffn_conv_w': _jnp.float32, 'ffn_conv_b': _jnp.float32, 'w_down': _jnp.float32, 'final_g': _jnp.float32}
MOMENT_SCALE = {'c_ctx': 2.842182e-03, 'w_ada': 4.921653e-02, 'b_ada': 8.636144e-02, 'norm1_g': 5.016824e-02, 'w_in': 2.396684e-02, 'q_norm_g': 3.486034e-03, 'kv_norm_g': 1.257115e-02, 'w_uq': 2.331117e-03, 'w_ukv': 5.808044e-03, 'conv_w': 4.131820e-02, 'conv_b': 3.564001e-02, 'w_attn_out': 5.773980e-03, 'w_conv_out': 2.746912e-02, 'w_o': 2.798304e-02, 'norm2_g': 3.910042e-02, 'w_up': 1.698447e-02, 'ffn_conv_w': 1.712331e-02, 'ffn_conv_b': 1.569818e-02, 'w_down': 2.802218e-02, 'final_g': 1.602417e+01}


def _to_microbatches(a, axis):
    t = _jnp.moveaxis(a, axis, 0)
    t = t.reshape((N_MICROBATCH, t.shape[0] // N_MICROBATCH) + t.shape[1:])
    return _jnp.moveaxis(t, 1, axis + 1)


def setup_inputs(seed: int = 0) -> dict:
    inp = _fwd_setup_inputs(seed)
    key = _jax.random.fold_in(_jax.random.key(seed), 7919)
    shape, _ = _output_shape()
    out = dict(inp)
    out["loss_target"] = _jax.random.normal(_jax.random.fold_in(key, 0), shape, _jnp.float32)
    for i, name in enumerate(TWIN_WEIGHTS):
        w = inp[name].astype(_jnp.float32)
        if MOMENT_SCALE is None:
            s = _jnp.sqrt(_jnp.mean(_jnp.square(w)) + 1e-30)
        else:
            s = MOMENT_SCALE[name]
        km, kv = _jax.random.split(_jax.random.fold_in(key, i + 1))
        out[name] = w
        out["m_" + name] = s * _jax.random.normal(km, w.shape, _jnp.float32)
        out["v_" + name] = (s * s) * _jax.random.uniform(kv, w.shape, _jnp.float32, 0.5, 1.5)
    if N_MICROBATCH > 1:
        for name, axis in PER_EXAMPLE_BATCH_AXIS.items():
            out[name] = _to_microbatches(out[name], axis)
    return {'x': out['x'], 'c': out['c'], 'ctx': out['ctx'], 'c_ctx': out['c_ctx'], 'w_ada': out['w_ada'], 'b_ada': out['b_ada'], 'norm1_g': out['norm1_g'], 'w_in': out['w_in'], 'q_norm_g': out['q_norm_g'], 'kv_norm_g': out['kv_norm_g'], 'w_uq': out['w_uq'], 'w_ukv': out['w_ukv'], 'conv_w': out['conv_w'], 'conv_b': out['conv_b'], 'w_attn_out': out['w_attn_out'], 'w_conv_out': out['w_conv_out'], 'w_o': out['w_o'], 'norm2_g': out['norm2_g'], 'w_up': out['w_up'], 'ffn_conv_w': out['ffn_conv_w'], 'ffn_conv_b': out['ffn_conv_b'], 'w_down': out['w_down'], 'final_g': out['final_g'], 'loss_target': out['loss_target'], 'm_c_ctx': out['m_c_ctx'], 'm_w_ada': out['m_w_ada'], 'm_b_ada': out['m_b_ada'], 'm_norm1_g': out['m_norm1_g'], 'm_w_in': out['m_w_in'], 'm_q_norm_g': out['m_q_norm_g'], 'm_kv_norm_g': out['m_kv_norm_g'], 'm_w_uq': out['m_w_uq'], 'm_w_ukv': out['m_w_ukv'], 'm_conv_w': out['m_conv_w'], 'm_conv_b': out['m_conv_b'], 'm_w_attn_out': out['m_w_attn_out'], 'm_w_conv_out': out['m_w_conv_out'], 'm_w_o': out['m_w_o'], 'm_norm2_g': out['m_norm2_g'], 'm_w_up': out['m_w_up'], 'm_ffn_conv_w': out['m_ffn_conv_w'], 'm_ffn_conv_b': out['m_ffn_conv_b'], 'm_w_down': out['m_w_down'], 'm_final_g': out['m_final_g'], 'v_c_ctx': out['v_c_ctx'], 'v_w_ada': out['v_w_ada'], 'v_b_ada': out['v_b_ada'], 'v_norm1_g': out['v_norm1_g'], 'v_w_in': out['v_w_in'], 'v_q_norm_g': out['v_q_norm_g'], 'v_kv_norm_g': out['v_kv_norm_g'], 'v_w_uq': out['v_w_uq'], 'v_w_ukv': out['v_w_ukv'], 'v_conv_w': out['v_conv_w'], 'v_conv_b': out['v_conv_b'], 'v_w_attn_out': out['v_w_attn_out'], 'v_w_conv_out': out['v_w_conv_out'], 'v_w_o': out['v_w_o'], 'v_norm2_g': out['v_norm2_g'], 'v_w_up': out['v_w_up'], 'v_ffn_conv_w': out['v_ffn_conv_w'], 'v_ffn_conv_b': out['v_ffn_conv_b'], 'v_w_down': out['v_w_down'], 'v_final_g': out['v_final_g']}


def _loss(weights, diff, rest, loss_target):
    with _jax.named_scope("forward"):
        args = {**rest, TWIN_DIFF_INPUT: diff, **{k: w.astype(_WEIGHT_DTYPES[k]) for k, w in weights.items()}}
        y = _forward(args)
    with _jax.named_scope("loss_head"):
        err = _jnp.square(y.astype(_jnp.float32) - loss_target)
        return 0.5 * _jnp.sum(_jnp.mean(err, axis=-1)) if err.ndim else 0.5 * err


def _adamw(w, g, m, v):
    m = ADAM_B1 * m + (1.0 - ADAM_B1) * g
    v = ADAM_B2 * v + (1.0 - ADAM_B2) * _jnp.square(g)
    m_hat = m / (1.0 - ADAM_B1 ** ADAM_STEP)
    v_hat = v / (1.0 - ADAM_B2 ** ADAM_STEP)
    delta = -ADAM_LR * (m_hat / (_jnp.sqrt(v_hat) + ADAM_EPS) + ADAM_WD * w)
    return delta, m, v


def reference(x, c, ctx, c_ctx, w_ada, b_ada, norm1_g, w_in, q_norm_g, kv_norm_g, w_uq, w_ukv, conv_w, conv_b, w_attn_out, w_conv_out, w_o, norm2_g, w_up, ffn_conv_w, ffn_conv_b, w_down, final_g, loss_target, m_c_ctx, m_w_ada, m_b_ada, m_norm1_g, m_w_in, m_q_norm_g, m_kv_norm_g, m_w_uq, m_w_ukv, m_conv_w, m_conv_b, m_w_attn_out, m_w_conv_out, m_w_o, m_norm2_g, m_w_up, m_ffn_conv_w, m_ffn_conv_b, m_w_down, m_final_g, v_c_ctx, v_w_ada, v_b_ada, v_norm1_g, v_w_in, v_q_norm_g, v_kv_norm_g, v_w_uq, v_w_ukv, v_conv_w, v_conv_b, v_w_attn_out, v_w_conv_out, v_w_o, v_norm2_g, v_w_up, v_ffn_conv_w, v_ffn_conv_b, v_w_down, v_final_g):
    given = dict(x=x, c=c, ctx=ctx, c_ctx=c_ctx, w_ada=w_ada, b_ada=b_ada, norm1_g=norm1_g, w_in=w_in, q_norm_g=q_norm_g, kv_norm_g=kv_norm_g, w_uq=w_uq, w_ukv=w_ukv, conv_w=conv_w, conv_b=conv_b, w_attn_out=w_attn_out, w_conv_out=w_conv_out, w_o=w_o, norm2_g=norm2_g, w_up=w_up, ffn_conv_w=ffn_conv_w, ffn_conv_b=ffn_conv_b, w_down=w_down, final_g=final_g, loss_target=loss_target, m_c_ctx=m_c_ctx, m_w_ada=m_w_ada, m_b_ada=m_b_ada, m_norm1_g=m_norm1_g, m_w_in=m_w_in, m_q_norm_g=m_q_norm_g, m_kv_norm_g=m_kv_norm_g, m_w_uq=m_w_uq, m_w_ukv=m_w_ukv, m_conv_w=m_conv_w, m_conv_b=m_conv_b, m_w_attn_out=m_w_attn_out, m_w_conv_out=m_w_conv_out, m_w_o=m_w_o, m_norm2_g=m_norm2_g, m_w_up=m_w_up, m_ffn_conv_w=m_ffn_conv_w, m_ffn_conv_b=m_ffn_conv_b, m_w_down=m_w_down, m_final_g=m_final_g, v_c_ctx=v_c_ctx, v_w_ada=v_w_ada, v_b_ada=v_b_ada, v_norm1_g=v_norm1_g, v_w_in=v_w_in, v_q_norm_g=v_q_norm_g, v_kv_norm_g=v_kv_norm_g, v_w_uq=v_w_uq, v_w_ukv=v_w_ukv, v_conv_w=v_conv_w, v_conv_b=v_conv_b, v_w_attn_out=v_w_attn_out, v_w_conv_out=v_w_conv_out, v_w_o=v_w_o, v_norm2_g=v_norm2_g, v_w_up=v_w_up, v_ffn_conv_w=v_ffn_conv_w, v_ffn_conv_b=v_ffn_conv_b, v_w_down=v_w_down, v_final_g=v_final_g)
    weights = {n: given[n] for n in TWIN_WEIGHTS}
    shared = {n: given[n] for n in SHARED_INPUTS}
    per_example = {n: given[n] for n in ['x', 'c', 'ctx']}
    grad_fn = _jax.value_and_grad(_loss, argnums=(0, 1))

    def one_microbatch(ex, loss_target):
        ex = dict(ex)
        diff = ex.pop(TWIN_DIFF_INPUT)
        return grad_fn(weights, diff, {**shared, **ex}, loss_target)

    if N_MICROBATCH == 1:
        loss, (grad_w, grad_x) = one_microbatch(per_example, given["loss_target"])
    else:
        def body(carry, xs):
            loss_sum, grad_sum = carry
            l_k, (gw_k, gx_k) = one_microbatch(xs[0], xs[1])
            with _jax.named_scope("update"):
                return (loss_sum + l_k, _jax.tree.map(_jnp.add, grad_sum, gw_k)), gx_k

        init = (_jnp.zeros((), _jnp.float32), _jax.tree.map(_jnp.zeros_like, weights))
        (loss, grad_w), grad_x = _jax.lax.scan(body, init, (per_example, given["loss_target"]))
    with _jax.named_scope("update"):
        delta_w, new_m, new_v = {}, {}, {}
        for n in TWIN_WEIGHTS:
            delta_w[n], new_m[n], new_v[n] = _adamw(weights[n], grad_w[n], given["m_" + n], given["v_" + n])
    return (loss, grad_x, *[grad_w[n] for n in TWIN_WEIGHTS], *[delta_w[n] for n in TWIN_WEIGHTS],
            *[new_m[n] for n in TWIN_WEIGHTS], *[new_v[n] for n in TWIN_WEIGHTS])
```

```python
import functools
import numpy as np
import jax
import jax.numpy as jnp
from jax import lax
from jax.experimental import pallas as pl
from jax.experimental.pallas import tpu as pltpu

F32 = jnp.float32
BF = jnp.bfloat16
MESH = pl.DeviceIdType.MESH

D = 1024
T = 2048
TC = 256
TKV = T + TC
GRID_W = 64
NH = 8
DN = 64
DR = 32
DV = 64
QL = 384
KVL = 256
CONV = 512
DFF = 2816
EPS = 1e-6
ROPE_THETA = 10000.0
SCALE = (DN + DR) ** -0.5
NDEV = 8
HP = 128
ROPE_LO = DN

O_GA, O_GC, O_CX, O_CB, O_CC, O_KV, O_Q = 0, 1024, 2048, 2560, 3072, 3584, 4096
NIN = 4608

LR, B1, B2, AEPS, WD, STEP = 0.001, 0.9, 0.999, 1e-08, 0.01, 10


def _pick(n, target, mult=128):
    best = None
    for d in range(mult, min(n, target) + 1, mult):
        if n % d == 0:
            best = d
    return best if best is not None else n


def _swap_idx():
    j = np.arange(DR)
    axis, half, f = j // 16, (j % 16) // 8, j % 8
    return axis * 16 + (1 - half) * 8 + f


def mm(a, b, *, ta=False, tb=False, out_dtype=F32, name, tm=512, tn=512, tk=2048):
    if ta:
        K, M = a.shape
    else:
        M, K = a.shape
    if tb:
        N, K2 = b.shape
    else:
        K2, N = b.shape
    assert K == K2, (a.shape, b.shape, ta, tb)
    tm = _pick(M, tm, 128 if ta else 16)
    tn = _pick(N, tn, 128)
    tk = _pick(K, tk, 128)
    nk = K // tk
    ca = 0 if ta else 1
    cb = 1 if tb else 0

    def body(a_ref, b_ref, o_ref, acc):
        k = pl.program_id(2)
        part = lax.dot_general(a_ref[...].astype(BF), b_ref[...].astype(BF),
                               (((ca,), (cb,)), ((), ())), preferred_element_type=F32)
        if nk == 1:
            o_ref[...] = part.astype(o_ref.dtype)
        else:
            @pl.when(k == 0)
            def _():
                acc[...] = part

            @pl.when(k > 0)
            def _():
                acc[...] += part

            @pl.when(k == nk - 1)
            def _():
                o_ref[...] = acc[...].astype(o_ref.dtype)

    a_spec = pl.BlockSpec((tk, tm), lambda i, j, k: (k, i)) if ta else pl.BlockSpec((tm, tk), lambda i, j, k: (i, k))
    b_spec = pl.BlockSpec((tn, tk), lambda i, j, k: (j, k)) if tb else pl.BlockSpec((tk, tn), lambda i, j, k: (k, j))
    return pl.pallas_call(
        body, name=name, grid=(M // tm, N // tn, nk),
        in_specs=[a_spec, b_spec], out_specs=pl.BlockSpec((tm, tn), lambda i, j, k: (i, j)),
        out_shape=jax.ShapeDtypeStruct((M, N), out_dtype),
        scratch_shapes=[pltpu.VMEM((tm, tn) if nk > 1 else (8, 128), F32)],
        compiler_params=pltpu.CompilerParams(dimension_semantics=("parallel", "parallel", "arbitrary")),
    )(a, b)


def _row(width):
    return pl.BlockSpec((1, width), lambda i: (0, 0))


def normmod(x, g, sc, sh, *, name, tm=256):
    R = x.shape[0]

    def body(x_ref, g_ref, sc_ref, sh_ref, h_ref):
        xv = x_ref[...]
        r = lax.rsqrt(jnp.mean(xv * xv, axis=-1, keepdims=True) + EPS)
        h_ref[...] = ((xv * r * g_ref[...]) * (1.0 + sc_ref[...]) + sh_ref[...]).astype(BF)

    blk = pl.BlockSpec((tm, D), lambda i: (i, 0))
    return pl.pallas_call(
        body, name=name, grid=(R // tm,), in_specs=[blk, _row(D), _row(D), _row(D)], out_specs=blk,
        out_shape=jax.ShapeDtypeStruct((R, D), BF),
        compiler_params=pltpu.CompilerParams(dimension_semantics=("parallel",)),
    )(x, g, sc, sh)


def resid_normmod(x, a, gate, g, sc, sh, *, name, tm=256):
    R = x.shape[0]

    def body(x_ref, a_ref, gate_ref, g_ref, sc_ref, sh_ref, x1_ref, h_ref):
        xv = x_ref[...] + gate_ref[...] * a_ref[...]
        x1_ref[...] = xv
        r = lax.rsqrt(jnp.mean(xv * xv, axis=-1, keepdims=True) + EPS)
        h_ref[...] = ((xv * r * g_ref[...]) * (1.0 + sc_ref[...]) + sh_ref[...]).astype(BF)

    blk = pl.BlockSpec((tm, D), lambda i: (i, 0))
    return pl.pallas_call(
        body, name=name, grid=(R // tm,), in_specs=[blk, blk, _row(D), _row(D), _row(D), _row(D)],
        out_specs=[blk, blk],
        out_shape=[jax.ShapeDtypeStruct((R, D), F32), jax.ShapeDtypeStruct((R, D), BF)],
        compiler_params=pltpu.CompilerParams(dimension_semantics=("parallel",)),
    )(x, a, gate, g, sc, sh)


def kvprep(pc, p, kvg, wkv2, ck, sk, *, name, tm=256):
    assert tm == TC
    nb = TKV // tm
    kvcol = O_KV // 512

    def body(pc_ref, p_ref, g_ref, w_ref, ck_ref, sk_ref, k_ref, v_ref, ckv_ref):
        i = pl.program_id(0)
        t = jnp.where(i == 0, pc_ref[...], p_ref[...])
        pk = t[:, :KVL]
        r = lax.rsqrt(jnp.mean(pk * pk, axis=-1, keepdims=True) + EPS)
        ckv = (pk * r * g_ref[...]).astype(BF)
        ckv_ref[...] = ckv
        kv2 = jnp.dot(ckv, w_ref[...], preferred_element_type=F32)
        krr = t[:, KVL:KVL + HP] * ck_ref[...] + t[:, KVL + HP:KVL + 2 * HP] * sk_ref[...]
        k_ref[...] = (kv2[:, :NH * HP] + jnp.concatenate([krr] * NH, axis=1)).astype(BF)
        v_ref[...] = kv2[:, NH * HP:].astype(BF)

    return pl.pallas_call(
        body, name=name, grid=(nb,),
        in_specs=[pl.BlockSpec((tm, 512), lambda i: (0, 0)),
                  pl.BlockSpec((tm, 512), lambda i: (jnp.maximum(i - 1, 0), kvcol)),
                  _row(KVL), pl.BlockSpec((KVL, NH * HP + NH * DV), lambda i: (0, 0)),
                  pl.BlockSpec((tm, HP), lambda i: (i, 0)), pl.BlockSpec((tm, HP), lambda i: (i, 0))],
        out_specs=[pl.BlockSpec((tm, NH * HP), lambda i: (i, 0)), pl.BlockSpec((tm, NH * DV), lambda i: (i, 0)),
                   pl.BlockSpec((tm, KVL), lambda i: (i, 0))],
        out_shape=[jax.ShapeDtypeStruct((TKV, NH * HP), BF), jax.ShapeDtypeStruct((TKV, NH * DV), BF),
                   jax.ShapeDtypeStruct((TKV, KVL), BF)],
        compiler_params=pltpu.CompilerParams(dimension_semantics=("parallel",)),
    )(pc, p, kvg, wkv2, ck, sk)


def qprep(p, qg, wq2, cq_t, sq_t, *, name, tm=256):
    qcol = O_Q // 512

    def body(p_ref, g_ref, w_ref, c_ref, s_ref, q_ref, cq_ref):
        pq = p_ref[...]
        r = lax.rsqrt(jnp.sum(pq * pq, axis=-1, keepdims=True) * (1.0 / QL) + EPS)
        cq = (pq * r * g_ref[...]).astype(BF)
        cq_ref[...] = cq
        q2 = jnp.dot(cq, w_ref[...], preferred_element_type=F32)
        cc = jnp.concatenate([c_ref[...]] * NH, axis=1)
        ss = jnp.concatenate([s_ref[...]] * NH, axis=1)
        q_ref[...] = (q2[:, :NH * HP] * cc + q2[:, NH * HP:] * ss).astype(BF)

    return pl.pallas_call(
        body, name=name, grid=(T // tm,),
        in_specs=[pl.BlockSpec((tm, 512), lambda i: (i, qcol)), _row(512),
                  pl.BlockSpec((512, 2 * NH * HP), lambda i: (0, 0)),
                  pl.BlockSpec((tm, HP), lambda i: (i, 0)), pl.BlockSpec((tm, HP), lambda i: (i, 0))],
        out_specs=[pl.BlockSpec((tm, NH * HP), lambda i: (i, 0)), pl.BlockSpec((tm, 512), lambda i: (i, 0))],
        out_shape=[jax.ShapeDtypeStruct((T, NH * HP), BF), jax.ShapeDtypeStruct((T, 512), BF)],
        compiler_params=pltpu.CompilerParams(dimension_semantics=("parallel",)),
    )(p, qg, wq2, cq_t, sq_t)


def _head_mask(h):
    lanes = lax.broadcasted_iota(jnp.int32, (1, 2 * DV), 1)
    return (lanes // DV) == (h % 2)


def attn_fwd(q, k, v, *, name, tq=256):
    def body(q_ref, k_ref, v_ref, o_ref):
        h = pl.program_id(1)
        s = lax.dot_general(q_ref[...], k_ref[...], (((1,), (1,)), ((), ())), preferred_element_type=F32) * SCALE
        m = jnp.max(s, axis=-1, keepdims=True)
        e = jnp.exp(s - m)
        pr = (e * (1.0 / jnp.sum(e, axis=-1, keepdims=True))).astype(BF)
        vm = jnp.where(_head_mask(h), v_ref[...], jnp.zeros_like(v_ref[...]))
        o2 = jnp.dot(pr, vm, preferred_element_type=F32).astype(BF)

        @pl.when(h % 2 == 0)
        def _():
            o_ref[...] = o2

        @pl.when(h % 2 == 1)
        def _():
            o_ref[...] = o_ref[...] + o2

    return pl.pallas_call(
        body, name=name, grid=(T // tq, NH),
        in_specs=[pl.BlockSpec((tq, HP), lambda i, h: (i, h)), pl.BlockSpec((TKV, HP), lambda i, h: (0, h)),
                  pl.BlockSpec((TKV, 2 * DV), lambda i, h: (0, h // 2))],
        out_specs=pl.BlockSpec((tq, 2 * DV), lambda i, h: (i, h // 2)),
        out_shape=jax.ShapeDtypeStruct((T, NH * DV), BF),
        compiler_params=pltpu.CompilerParams(dimension_semantics=("parallel", "arbitrary")),
    )(q, k, v)


def _shift_dn(x):
    n = x.shape[0]
    rows = lax.broadcasted_iota(jnp.int32, (n, 1), 0)
    return jnp.where(rows == 0, 0.0, pltpu.roll(x, 1, axis=0))


def _shift_up(x):
    n = x.shape[0]
    rows = lax.broadcasted_iota(jnp.int32, (n, 1), 0)
    return jnp.where(rows == n - 1, 0.0, pltpu.roll(x, n - 1, axis=0))


def _conv(x, w_ref, b_ref):
    return b_ref[...] + _shift_dn(x) * w_ref[0:1, :] + x * w_ref[1:2, :] + _shift_up(x) * w_ref[2:3, :]


def _conv_t(dy, w_ref):
    return _shift_up(dy) * w_ref[0:1, :] + dy * w_ref[1:2, :] + _shift_dn(dy) * w_ref[2:3, :]


def _conv_wgrad(dw_ref, dy, x):
    dw_ref[0:1, :] = jnp.sum(dy * _shift_dn(x), axis=0, keepdims=True)
    dw_ref[1:2, :] = jnp.sum(dy * x, axis=0, keepdims=True)
    dw_ref[2:3, :] = jnp.sum(dy * _shift_up(x), axis=0, keepdims=True)


def convz(p, cw, cb, *, name, tc=256):
    ox, ob, oc = O_CX // tc, O_CB // tc, O_CC // tc

    def body(x_ref, b_ref, c_ref, w_ref, bias_ref, z_ref):
        z_ref[...] = (b_ref[...] * _conv(c_ref[...] * x_ref[...], w_ref, bias_ref)).astype(BF)

    return pl.pallas_call(
        body, name=name, grid=(CONV // tc,),
        in_specs=[pl.BlockSpec((T, tc), lambda j: (0, ox + j)), pl.BlockSpec((T, tc), lambda j: (0, ob + j)),
                  pl.BlockSpec((T, tc), lambda j: (0, oc + j)), pl.BlockSpec((3, tc), lambda j: (0, j)),
                  pl.BlockSpec((1, tc), lambda j: (0, j))],
        out_specs=pl.BlockSpec((T, tc), lambda j: (0, j)),
        out_shape=jax.ShapeDtypeStruct((T, CONV), BF),
        compiler_params=pltpu.CompilerParams(dimension_semantics=("parallel",)),
    )(p, p, p, cw, cb)


def gate_merge(p, ya, yc, *, name, tm=256):
    def body(ga_ref, gc_ref, ya_ref, yc_ref, o_ref):
        o_ref[...] = (jax.nn.sigmoid(ga_ref[...]) * ya_ref[...] + jax.nn.sigmoid(gc_ref[...]) * yc_ref[...]).astype(BF)

    blk = pl.BlockSpec((tm, D), lambda i: (i, 0))
    return pl.pallas_call(
        body, name=name, grid=(T // tm,),
        in_specs=[pl.BlockSpec((tm, D), lambda i: (i, O_GA // D)), pl.BlockSpec((tm, D), lambda i: (i, O_GC // D)), blk, blk],
        out_specs=blk, out_shape=jax.ShapeDtypeStruct((T, D), BF),
        compiler_params=pltpu.CompilerParams(dimension_semantics=("parallel",)),
    )(p, p, ya, yc)


def ffn_act(u0, cw, cb, *, name, tc=256):
    nb = DFF // tc

    def body(g_ref, v_ref, wg_ref, wv_ref, bg_ref, bv_ref, f_ref):
        ug = _conv(g_ref[...], wg_ref, bg_ref)
        uv = _conv(v_ref[...], wv_ref, bv_ref)
        f_ref[...] = (ug * jax.nn.sigmoid(ug) * uv).astype(BF)

    return pl.pallas_call(
        body, name=name, grid=(nb,),
        in_specs=[pl.BlockSpec((T, tc), lambda j: (0, j)), pl.BlockSpec((T, tc), lambda j: (0, nb + j)),
                  pl.BlockSpec((3, tc), lambda j: (0, j)), pl.BlockSpec((3, tc), lambda j: (0, nb + j)),
                  pl.BlockSpec((1, tc), lambda j: (0, j)), pl.BlockSpec((1, tc), lambda j: (0, nb + j))],
        out_specs=pl.BlockSpec((T, tc), lambda j: (0, j)),
        out_shape=jax.ShapeDtypeStruct((T, DFF), BF),
        compiler_params=pltpu.CompilerParams(dimension_semantics=("parallel",)),
    )(u0, u0, cw, cw, cb, cb)


def final_loss(x1, d, g2, fg, tgt, *, name, tm=256):
    def body(x1_ref, d_ref, g2_ref, fg_ref, t_ref, dx_ref, dd_ref, dfg_ref, loss_ref):
        i = pl.program_id(0)
        xv = x1_ref[...] + g2_ref[...] * d_ref[...]
        r = lax.rsqrt(jnp.mean(xv * xv, axis=-1, keepdims=True) + EPS)
        xh = xv * r
        diff = xh * fg_ref[...] - t_ref[...]
        part = 0.5 * jnp.sum(jnp.mean(diff * diff, axis=-1, keepdims=True), axis=0, keepdims=True)
        dy = diff * (1.0 / D)
        a = dy * fg_ref[...]
        dx = r * (a - xh * jnp.mean(a * xh, axis=-1, keepdims=True))
        dx_ref[...] = dx
        dd_ref[...] = (dx * g2_ref[...]).astype(BF)
        dfg = jnp.sum(dy * xh, axis=0, keepdims=True)

        @pl.when(i == 0)
        def _():
            dfg_ref[...] = dfg
            loss_ref[...] = jnp.broadcast_to(part, (1, 128))

        @pl.when(i > 0)
        def _():
            dfg_ref[...] += dfg
            loss_ref[...] += jnp.broadcast_to(part, (1, 128))

    blk = pl.BlockSpec((tm, D), lambda i: (i, 0))
    return pl.pallas_call(
        body, name=name, grid=(T // tm,), in_specs=[blk, blk, _row(D), _row(D), blk],
        out_specs=[blk, blk, _row(D), _row(128)],
        out_shape=[jax.ShapeDtypeStruct((T, D), F32), jax.ShapeDtypeStruct((T, D), BF),
                   jax.ShapeDtypeStruct((1, D), F32), jax.ShapeDtypeStruct((1, 128), F32)],
        compiler_params=pltpu.CompilerParams(dimension_semantics=("arbitrary",)),
    )(x1, d, g2, fg, tgt)


def normmod_bwd(x, dh, g, sc, dres, gsrc, gate, *, name, tm=256):
    R = x.shape[0]
    has_res = dres is not None

    def body(*refs):
        if has_res:
            x_ref, dh_ref, g_ref, sc_ref, dres_ref, gsrc_ref, gate_ref, dx_ref, dxg_ref, st_ref = refs
        else:
            x_ref, dh_ref, g_ref, sc_ref, st_ref = refs
        i = pl.program_id(0)
        xv = x_ref[...]
        r = lax.rsqrt(jnp.mean(xv * xv, axis=-1, keepdims=True) + EPS)
        xh = xv * r
        dhv = dh_ref[...]
        n = xh * g_ref[...]
        dn = dhv * (1.0 + sc_ref[...])
        a = dn * g_ref[...]
        rows = [jnp.sum(dhv, axis=0, keepdims=True), jnp.sum(dhv * n, axis=0, keepdims=True),
                jnp.sum(dn * xh, axis=0, keepdims=True)]
        if has_res:
            dr = dres_ref[...]
            dx = dr + r * (a - xh * jnp.mean(a * xh, axis=-1, keepdims=True))
            dx_ref[...] = dx
            dxg_ref[...] = (dx * gate_ref[...]).astype(BF)
            rows.append(jnp.sum(dr * gsrc_ref[...], axis=0, keepdims=True))
        else:
            rows.append(jnp.zeros((1, D), F32))

        @pl.when(i == 0)
        def _():
            for k, row in enumerate(rows):
                st_ref[k:k + 1, :] = row

        @pl.when(i > 0)
        def _():
            for k, row in enumerate(rows):
                st_ref[k:k + 1, :] += row

    blk = pl.BlockSpec((tm, D), lambda i: (i, 0))
    st_spec = pl.BlockSpec((4, D), lambda i: (0, 0))
    st_shape = jax.ShapeDtypeStruct((4, D), F32)
    cp = pltpu.CompilerParams(dimension_semantics=("arbitrary",))
    if has_res:
        return pl.pallas_call(
            body, name=name, grid=(R // tm,), in_specs=[blk, blk, _row(D), _row(D), blk, blk, _row(D)],
            out_specs=[blk, blk, st_spec],
            out_shape=[jax.ShapeDtypeStruct((R, D), F32), jax.ShapeDtypeStruct((R, D), BF), st_shape],
            compiler_params=cp,
        )(x, dh, g, sc, dres, gsrc, gate)
    return pl.pallas_call(
        body, name=name, grid=(R // tm,), in_specs=[blk, blk, _row(D), _row(D)],
        out_specs=st_spec, out_shape=st_shape, compiler_params=cp,
    )(x, dh, g, sc)


def ffn_act_bwd(u0, df, cw, cb, *, name, tc=256):
    nb = DFF // tc

    def body(g_ref, v_ref, df_ref, wg_ref, wv_ref, bg_ref, bv_ref,
             dg_ref, dv_ref, dwg_ref, dwv_ref, dbg_ref, dbv_ref):
        xg, xv = g_ref[...], v_ref[...]
        ug = _conv(xg, wg_ref, bg_ref)
        uv = _conv(xv, wv_ref, bv_ref)
        sig = jax.nn.sigmoid(ug)
        dfv = df_ref[...]
        dug = dfv * uv * (sig * (1.0 + ug * (1.0 - sig)))
        duv = dfv * (ug * sig)
        dg_ref[...] = _conv_t(dug, wg_ref).astype(BF)
        dv_ref[...] = _conv_t(duv, wv_ref).astype(BF)
        _conv_wgrad(dwg_ref, dug, xg)
        _conv_wgrad(dwv_ref, duv, xv)
        dbg_ref[...] = jnp.sum(dug, axis=0, keepdims=True)
        dbv_ref[...] = jnp.sum(duv, axis=0, keepdims=True)

    lo = lambda r: pl.BlockSpec((r, tc), lambda j: (0, j))
    hi = lambda r: pl.BlockSpec((r, tc), lambda j: (0, nb + j))
    outs = pl.pallas_call(
        body, name=name, grid=(nb,),
        in_specs=[lo(T), hi(T), lo(T), lo(3), hi(3), lo(1), hi(1)],
        out_specs=[lo(T), lo(T), lo(3), lo(3), lo(1), lo(1)],
        out_shape=[jax.ShapeDtypeStruct((T, DFF), BF), jax.ShapeDtypeStruct((T, DFF), BF),
                   jax.ShapeDtypeStruct((3, DFF), F32), jax.ShapeDtypeStruct((3, DFF), F32),
                   jax.ShapeDtypeStruct((1, DFF), F32), jax.ShapeDtypeStruct((1, DFF), F32)],
        compiler_params=pltpu.CompilerParams(dimension_semantics=("parallel",)),
    )(u0, u0, df, cw, cw, cb, cb)
    return outs


def gate_merge_bwd(p, ya, yc, dm, *, name, tm=256):
    def body(ga_ref, gc_ref, ya_ref, yc_ref, dm_ref, dya_ref, dyc_ref, dga_ref, dgc_ref):
        sa, sc_ = jax.nn.sigmoid(ga_ref[...]), jax.nn.sigmoid(gc_ref[...])
        dmv = dm_ref[...]
        dya_ref[...] = (dmv * sa).astype(BF)
        dyc_ref[...] = (dmv * sc_).astype(BF)
        dga_ref[...] = (dmv * ya_ref[...] * (sa * (1.0 - sa))).astype(BF)
        dgc_ref[...] = (dmv * yc_ref[...] * (sc_ * (1.0 - sc_))).astype(BF)

    blk = pl.BlockSpec((tm, D), lambda i: (i, 0))
    sh = jax.ShapeDtypeStruct((T, D), BF)
    return pl.pallas_call(
        body, name=name, grid=(T // tm,),
        in_specs=[pl.BlockSpec((tm, D), lambda i: (i, O_GA // D)), pl.BlockSpec((tm, D), lambda i: (i, O_GC // D)), blk, blk, blk],
        out_specs=[blk, blk, blk, blk], out_shape=[sh, sh, sh, sh],
        compiler_params=pltpu.CompilerParams(dimension_semantics=("parallel",)),
    )(p, p, ya, yc, dm)


def convz_bwd(p, dz, cw, cb, *, name, tc=256):
    ox, ob, oc = O_CX // tc, O_CB // tc, O_CC // tc

    def body(x_ref, b_ref, c_ref, dz_ref, w_ref, bias_ref, dx_ref, db_ref, dc_ref, dw_ref, dbias_ref):
        xv, bv, cv = x_ref[...], b_ref[...], c_ref[...]
        ci = cv * xv
        dwc = _conv(ci, w_ref, bias_ref)
        dzv = dz_ref[...]
        db_ref[...] = (dzv * dwc).astype(BF)
        ddw = dzv * bv
        dci = _conv_t(ddw, w_ref)
        dx_ref[...] = (dci * cv).astype(BF)
        dc_ref[...] = (dci * xv).astype(BF)
        _conv_wgrad(dw_ref, ddw, ci)
        dbias_ref[...] = jnp.sum(ddw, axis=0, keepdims=True)

    own = lambda r: pl.BlockSpec((r, tc), lambda j: (0, j))
    sh = jax.ShapeDtypeStruct((T, CONV), BF)
    return pl.pallas_call(
        body, name=name, grid=(CONV // tc,),
        in_specs=[pl.BlockSpec((T, tc), lambda j: (0, ox + j)), pl.BlockSpec((T, tc), lambda j: (0, ob + j)),
                  pl.BlockSpec((T, tc), lambda j: (0, oc + j)), own(T), own(3), own(1)],
        out_specs=[own(T), own(T), own(T), own(3), own(1)],
        out_shape=[sh, sh, sh, jax.ShapeDtypeStruct((3, CONV), F32), jax.ShapeDtypeStruct((1, CONV), F32)],
        compiler_params=pltpu.CompilerParams(dimension_semantics=("parallel",)),
    )(p, p, p, dz, cw, cb)


def attn_bwd(q, k, v, do, *, name, tq=256):
    def body(q_ref, k_ref, v_ref, do_ref, dq_ref, dk_ref, dv_ref):
        h, i = pl.program_id(0), pl.program_id(1)
        qv, kv = q_ref[...], k_ref[...]
        s = lax.dot_general(qv, kv, (((1,), (1,)), ((), ())), preferred_element_type=F32) * SCALE
        m = jnp.max(s, axis=-1, keepdims=True)
        e = jnp.exp(s - m)
        pr = e * (1.0 / jnp.sum(e, axis=-1, keepdims=True))
        mask = _head_mask(h)
        vm = jnp.where(mask, v_ref[...], jnp.zeros_like(v_ref[...]))
        dom = jnp.where(mask, do_ref[...], jnp.zeros_like(do_ref[...]))
        dp = lax.dot_general(dom, vm, (((1,), (1,)), ((), ())), preferred_element_type=F32)
        ds = (pr * (dp - jnp.sum(pr * dp, axis=-1, keepdims=True)) * SCALE).astype(BF)
        dq_ref[...] = jnp.dot(ds, kv, preferred_element_type=F32)
        dk = lax.dot_general(ds, qv, (((0,), (0,)), ((), ())), preferred_element_type=F32)
        dv = lax.dot_general(pr.astype(BF), dom, (((0,), (0,)), ((), ())), preferred_element_type=F32)

        @pl.when(i == 0)
        def _():
            dk_ref[...] = dk

        @pl.when(i > 0)
        def _():
            dk_ref[...] += dk

        @pl.when((i == 0) & (h % 2 == 0))
        def _():
            dv_ref[...] = dv

        @pl.when((i > 0) | (h % 2 == 1))
        def _():
            dv_ref[...] += dv

    return pl.pallas_call(
        body, name=name, grid=(NH, T // tq),
        in_specs=[pl.BlockSpec((tq, HP), lambda h, i: (i, h)), pl.BlockSpec((TKV, HP), lambda h, i: (0, h)),
                  pl.BlockSpec((TKV, 2 * DV), lambda h, i: (0, h // 2)), pl.BlockSpec((tq, 2 * DV), lambda h, i: (i, h // 2))],
        out_specs=[pl.BlockSpec((tq, HP), lambda h, i: (i, h)), pl.BlockSpec((TKV, HP), lambda h, i: (0, h)),
                   pl.BlockSpec((TKV, 2 * DV), lambda h, i: (0, h // 2))],
        out_shape=[jax.ShapeDtypeStruct((T, NH * HP), F32), jax.ShapeDtypeStruct((TKV, NH * HP), F32),
                   jax.ShapeDtypeStruct((TKV, NH * DV), F32)],
        compiler_params=pltpu.CompilerParams(dimension_semantics=("arbitrary", "arbitrary")),
    )(q, k, v, do)


def qprep_bwd(p, dq, qg, wq2, cq_t, sq_t, *, name, tm=256):
    qcol = O_Q // 512

    def body(p_ref, dq_ref, g_ref, w_ref, c_ref, s_ref, dp_ref, dq2_ref, dg_ref):
        i = pl.program_id(0)
        dqv = dq_ref[...]
        cc = jnp.concatenate([c_ref[...]] * NH, axis=1)
        ss = jnp.concatenate([s_ref[...]] * NH, axis=1)
        dq2 = jnp.concatenate([dqv * cc, dqv * ss], axis=1).astype(BF)
        dq2_ref[...] = dq2
        dcq = lax.dot_general(dq2, w_ref[...], (((1,), (1,)), ((), ())), preferred_element_type=F32)
        pq = p_ref[...]
        r = lax.rsqrt(jnp.sum(pq * pq, axis=-1, keepdims=True) * (1.0 / QL) + EPS)
        xh = pq * r
        a = dcq * g_ref[...]
        dp_ref[...] = (r * (a - xh * (jnp.sum(a * xh, axis=-1, keepdims=True) * (1.0 / QL)))).astype(BF)
        dg = jnp.sum(dcq * xh, axis=0, keepdims=True)

        @pl.when(i == 0)
        def _():
            dg_ref[...] = dg

        @pl.when(i > 0)
        def _():
            dg_ref[...] += dg

    return pl.pallas_call(
        body, name=name, grid=(T // tm,),
        in_specs=[pl.BlockSpec((tm, 512), lambda i: (i, qcol)), pl.BlockSpec((tm, NH * HP), lambda i: (i, 0)), _row(512),
                  pl.BlockSpec((512, 2 * NH * HP), lambda i: (0, 0)),
                  pl.BlockSpec((tm, HP), lambda i: (i, 0)), pl.BlockSpec((tm, HP), lambda i: (i, 0))],
        out_specs=[pl.BlockSpec((tm, 512), lambda i: (i, 0)), pl.BlockSpec((tm, 2 * NH * HP), lambda i: (i, 0)), _row(512)],
        out_shape=[jax.ShapeDtypeStruct((T, 512), BF), jax.ShapeDtypeStruct((T, 2 * NH * HP), BF),
                   jax.ShapeDtypeStruct((1, 512), F32)],
        compiler_params=pltpu.CompilerParams(dimension_semantics=("arbitrary",)),
    )(p, dq, qg, wq2, cq_t, sq_t)


def kvprep_bwd(pc, p, dk, dv, kvg, wkv2, ck, sk, *, name, tm=256):
    assert tm == TC
    nb = TKV // tm
    kvcol = O_KV // 512

    def body(pc_ref, p_ref, dk_ref, dv_ref, g_ref, w_ref, ck_ref, sk_ref, dp_ref, dkv2_ref, dg_ref):
        i = pl.program_id(0)
        t = jnp.where(i == 0, pc_ref[...], p_ref[...])
        pk = t[:, :KVL]
        r = lax.rsqrt(jnp.mean(pk * pk, axis=-1, keepdims=True) + EPS)
        xh = pk * r
        dkv = dk_ref[...]
        dkv2 = jnp.concatenate([dkv, dv_ref[...]], axis=1).astype(BF)
        dkv2_ref[...] = dkv2
        dckv = lax.dot_general(dkv2, w_ref[...], (((1,), (1,)), ((), ())), preferred_element_type=F32)
        a = dckv * g_ref[...]
        dpk = r * (a - xh * jnp.mean(a * xh, axis=-1, keepdims=True))
        dkr = dkv[:, 0:HP]
        for hh in range(1, NH):
            dkr = dkr + dkv[:, hh * HP:(hh + 1) * HP]
        dp_ref[...] = jnp.concatenate([dpk, dkr * ck_ref[...], dkr * sk_ref[...]], axis=1).astype(BF)
        dg = jnp.sum(dckv * xh, axis=0, keepdims=True)

        @pl.when(i == 0)
        def _():
            dg_ref[...] = dg

        @pl.when(i > 0)
        def _():
            dg_ref[...] += dg

    rb = lambda w: pl.BlockSpec((tm, w), lambda i: (i, 0))
    return pl.pallas_call(
        body, name=name, grid=(nb,),
        in_specs=[pl.BlockSpec((tm, 512), lambda i: (0, 0)),
                  pl.BlockSpec((tm, 512), lambda i: (jnp.maximum(i - 1, 0), kvcol)),
                  rb(NH * HP), rb(NH * DV), _row(KVL), pl.BlockSpec((KVL, NH * HP + NH * DV), lambda i: (0, 0)),
                  rb(HP), rb(HP)],
        out_specs=[rb(512), rb(NH * HP + NH * DV), _row(KVL)],
        out_shape=[jax.ShapeDtypeStruct((TKV, 512), BF), jax.ShapeDtypeStruct((TKV, NH * HP + NH * DV), BF),
                   jax.ShapeDtypeStruct((1, KVL), F32)],
        compiler_params=pltpu.CompilerParams(dimension_semantics=("arbitrary",)),
    )(pc, p, dk, dv, kvg, wkv2, ck, sk)


def _rope_tables():
    t = jnp.arange(T)
    row = (t // GRID_W).astype(F32)
    col = (t % GRID_W).astype(F32)
    axis_dim = DR // 2
    inv = ROPE_THETA ** (-jnp.arange(0, axis_dim, 2, dtype=F32) / axis_dim)
    ar, ac = row[:, None] * inv, col[:, None] * inv
    cosv = jnp.concatenate([jnp.cos(ar), jnp.cos(ar), jnp.cos(ac), jnp.cos(ac)], axis=1)
    sinv = jnp.concatenate([-jnp.sin(ar), jnp.sin(ar), -jnp.sin(ac), jnp.sin(ac)], axis=1)
    z64, z32 = jnp.zeros((T, DN), F32), jnp.zeros((T, HP - DN - DR), F32)
    ck_lat = jnp.concatenate([z64, cosv, z32], axis=1)
    sk_lat = jnp.concatenate([z64, sinv, z32], axis=1)
    lane = jnp.arange(HP)
    rope_lane = ((lane >= DN) & (lane < DN + DR)).astype(F32)[None, :]
    ck = jnp.concatenate([jnp.broadcast_to(rope_lane, (TC, HP)), ck_lat], axis=0)
    sk = jnp.concatenate([jnp.zeros((TC, HP), F32), sk_lat], axis=0)
    cq = jnp.concatenate([jnp.ones((T, DN), F32), cosv, z32], axis=1)
    return ck, sk, cq, sk_lat


def _w_in_internal(w_in):
    sw = _swap_idx()
    kv, kr, q = w_in[:, 0:256], w_in[:, 256:288], w_in[:, 288:672]
    cx, cb, cc = w_in[:, 672:1184], w_in[:, 1184:1696], w_in[:, 1696:2208]
    ga, gc = w_in[:, 2208:3232], w_in[:, 3232:4256]
    z = lambda n: jnp.zeros((D, n), w_in.dtype)
    kra = jnp.concatenate([z(DN), kr, z(HP - DN - DR)], axis=1)
    krb = jnp.concatenate([z(DN), kr[:, sw], z(HP - DN - DR)], axis=1)
    return jnp.concatenate([ga, gc, cx, cb, cc, kv, kra, krb, q, z(128)], axis=1)


def _w_in_external(dwi):
    sw = _swap_idx()
    ga, gc = dwi[:, O_GA:O_GA + 1024], dwi[:, O_GC:O_GC + 1024]
    cx, cb, cc = dwi[:, O_CX:O_CX + 512], dwi[:, O_CB:O_CB + 512], dwi[:, O_CC:O_CC + 512]
    kv = dwi[:, O_KV:O_KV + 256]
    kra = dwi[:, O_KV + 256 + DN:O_KV + 256 + DN + DR]
    krb = dwi[:, O_KV + 384 + DN:O_KV + 384 + DN + DR]
    kr = kra + krb[:, sw]
    q = dwi[:, O_Q:O_Q + QL]
    return jnp.concatenate([kv, kr, q, cx, cb, cc, ga, gc], axis=1)


def _w_q_internal(w_uq):
    sw = _swap_idx()
    w = w_uq.reshape(QL, NH, DN + DR)
    nope, rope = w[:, :, :DN], w[:, :, DN:]
    z = lambda n: jnp.zeros((QL, NH, n), w_uq.dtype)
    qa = jnp.concatenate([nope, rope, z(HP - DN - DR)], axis=2).reshape(QL, NH * HP)
    qb = jnp.concatenate([z(DN), rope[:, :, sw], z(HP - DN - DR)], axis=2).reshape(QL, NH * HP)
    w2 = jnp.concatenate([qa, qb], axis=1)
    return jnp.concatenate([w2, jnp.zeros((512 - QL, 2 * NH * HP), w_uq.dtype)], axis=0)


def _w_q_external(dw2):
    sw = _swap_idx()
    a = dw2[:QL, :NH * HP].reshape(QL, NH, HP)
    b = dw2[:QL, NH * HP:].reshape(QL, NH, HP)
    nope = a[:, :, :DN]
    rope = a[:, :, DN:DN + DR] + b[:, :, DN:DN + DR][:, :, sw]
    return jnp.concatenate([nope, rope], axis=2).reshape(QL, NH * (DN + DR))


def _w_kv_internal(w_ukv):
    w = w_ukv.reshape(KVL, NH, DN + DV)
    kn, v = w[:, :, :DN], w[:, :, DN:]
    kpart = jnp.concatenate([kn, jnp.zeros((KVL, NH, HP - DN), w_ukv.dtype)], axis=2).reshape(KVL, NH * HP)
    return jnp.concatenate([kpart, v.reshape(KVL, NH * DV)], axis=1)


def _w_kv_external(dw2):
    kn = dw2[:, :NH * HP].reshape(KVL, NH, HP)[:, :, :DN]
    v = dw2[:, NH * HP:].reshape(KVL, NH, DV)
    return jnp.concatenate([kn, v], axis=2).reshape(KVL, NH * (DN + DV))


def _local_step(x, ctx, tgt, mod_lat, mod_ctx, n1g, qg, kvg, n2g, fg, conv_w, conv_b, ffn_w, ffn_b,
                win, wq2, wkv2, wao, wco, wo, wup, wdn):
    sh1, sc1, g1, sh2, sc2, g2 = [mod_lat[:, i * D:(i + 1) * D] for i in range(6)]
    csh1, csc1 = mod_ctx[:, 0:D], mod_ctx[:, D:2 * D]
    ck, sk, cq_t, sq_t = _rope_tables()
    qg_p = jnp.concatenate([qg, jnp.zeros((1, 512 - QL), F32)], axis=1)
    win_kv = win[:, O_KV:O_KV + 512]

    h = normmod(x, n1g, sc1, sh1, name="normmod1")
    hc = normmod(ctx, n1g, csc1, csh1, name="normmod1_ctx")
    p = mm(h, win, name="in_proj")
    pc = mm(hc, win_kv, name="in_proj_ctx")
    kh, vh, ckv = kvprep(pc, p, kvg, wkv2, ck, sk, name="kvprep")
    qr, cq = qprep(p, qg_p, wq2, cq_t, sq_t, name="qprep")
    o = attn_fwd(qr, kh, vh, name="attn_fwd")
    z = convz(p, conv_w, conv_b, name="convz")
    ya = mm(o, wao, name="attn_out")
    yc = mm(z, wco, name="conv_out")
    merged = gate_merge(p, ya, yc, name="gate_merge")
    a_out = mm(merged, wo, name="o_proj")
    x1, h2 = resid_normmod(x, a_out, g1, n2g, sc2, sh2, name="resid_normmod2")
    u0 = mm(h2, wup, name="up_proj")
    f = ffn_act(u0, ffn_w, ffn_b, name="ffn_act")
    dn = mm(f, wdn, name="down_proj")
    dx2, dd, dfg, loss = final_loss(x1, dn, g2, fg, tgt, name="final_loss")

    df = mm(dd, wdn, tb=True, name="down_proj_dx")
    dwdn = mm(f, dd, ta=True, out_dtype=BF, name="down_proj_dw")
    dug, duv, dfwg, dfwv, dfbg, dfbv = ffn_act_bwd(u0, df, ffn_w, ffn_b, name="ffn_act_bwd")
    du0 = jnp.concatenate([dug, duv], axis=1)
    dh2 = mm(du0, wup, tb=True, name="up_proj_dx")
    dwup = mm(h2, du0, ta=True, out_dtype=BF, name="up_proj_dw")
    dx1, da, st2 = normmod_bwd(x1, dh2, n2g, sc2, dx2, dn, g1, name="normmod2_bwd")

    dmerged = mm(da, wo, tb=True, name="o_proj_dx")
    dwo = mm(merged, da, ta=True, out_dtype=BF, name="o_proj_dw")
    dya, dyc, dga, dgc = gate_merge_bwd(p, ya, yc, dmerged, name="gate_merge_bwd")
    do = mm(dya, wao, tb=True, out_dtype=BF, name="attn_out_dx")
    dwao = mm(o, dya, ta=True, out_dtype=BF, name="attn_out_dw")
    dz = mm(dyc, wco, tb=True, name="conv_out_dx")
    dwco = mm(z, dyc, ta=True, out_dtype=BF, name="conv_out_dw")
    dcx, dcb, dcc, dconv_w, dconv_b = convz_bwd(p, dz, conv_w, conv_b, name="convz_bwd")
    dq, dk, dv = attn_bwd(qr, kh, vh, do, name="attn_bwd")
    dpq, dq2, dqg = qprep_bwd(p, dq, qg_p, wq2, cq_t, sq_t, name="qprep_bwd")
    dwq2 = mm(cq, dq2, ta=True, name="q_up_dw")
    dpkv, dkv2, dkvg = kvprep_bwd(pc, p, dk, dv, kvg, wkv2, ck, sk, name="kvprep_bwd")
    dwkv2 = mm(ckv, dkv2, ta=True, name="kv_up_dw")

    dp = jnp.concatenate([dga, dgc, dcx, dcb, dcc, dpkv[TC:], dpq], axis=1)
    dh = mm(dp, win, tb=True, name="in_proj_dx")
    dhc = mm(dpkv[:TC], win_kv, tb=True, name="in_proj_ctx_dx")
    dwin_a = mm(h, dp[:, :O_KV], ta=True, name="in_proj_dw")
    dwin_q = mm(h, dpq, ta=True, name="in_proj_q_dw")
    dwin_kv = mm(jnp.concatenate([hc, h], axis=0), dpkv, ta=True, name="in_proj_kv_dw")
    dwin = jnp.concatenate([dwin_a, dwin_kv, dwin_q], axis=1)
    dx, _, st1 = normmod_bwd(x, dh, n1g, sc1, dx1, a_out, g1, name="normmod1_bwd")
    stc = normmod_bwd(ctx, dhc, n1g, csc1, None, None, None, name="normmod1_ctx_bwd")

    zrow = jnp.zeros((1, D), F32)
    dmod_lat = jnp.concatenate([st1[0:1], st1[1:2], st1[3:4], st2[0:1], st2[1:2], st2[3:4]], axis=1)
    dmod_ctx = jnp.concatenate([stc[0:1], stc[1:2], zrow, zrow, zrow, zrow], axis=1)
    return dict(
        loss=loss, dx=dx, dmod_lat=dmod_lat, dmod_ctx=dmod_ctx,
        dn1g=st1[2:3] + stc[2:3], dqg=dqg, dkvg=dkvg, dn2g=st2[2:3], dfg=dfg,
        dconv_w=dconv_w, dconv_b=dconv_b,
        dffn_w=jnp.concatenate([dfwg, dfwv], axis=1), dffn_b=jnp.concatenate([dfbg, dfbv], axis=1),
        dwin=dwin, dwq2=dwq2, dwkv2=dwkv2, dwao=dwao, dwco=dwco, dwo=dwo, dwup=dwup, dwdn=dwdn)


def _me():
    x, y, c = lax.axis_index("x"), lax.axis_index("y"), lax.axis_index("c")
    return x, y, c, 4 * x + 2 * y + c


def _peer(x, y, c, k):
    px = 1 - x if k & 4 else x
    py = 1 - y if k & 2 else y
    pc = 1 - c if k & 1 else c
    return (px, py, pc), 4 * px + 2 * py + pc


def _exchange_tiles(src_of_peer, buf, send_sem, recv_sem):
    x, y, c, me = _me()
    for k in range(1, NDEV):
        dev, lin = _peer(x, y, c, k)
        pltpu.make_async_remote_copy(src_ref=src_of_peer(lin), dst_ref=buf.at[me], send_sem=send_sem, recv_sem=recv_sem,
                                     device_id=dev, device_id_type=MESH).start()
    seven = buf.at[pl.ds(0, NDEV - 1)]
    pltpu.make_async_remote_copy(src_ref=seven, dst_ref=seven, send_sem=send_sem, recv_sem=recv_sem,
                                 device_id=(x, y, c), device_id_type=MESH).wait()


def _silu(z):
    return z * jax.nn.sigmoid(z)


def ada_fwd(c, c_ctx, ffn_w, conv_w, w_shard, b_shard, *, name):
    nsh = w_shard.shape[1]

    def body(c_ref, cc_ref, fw_ref, cw_ref, w_ref, b_ref, s_ref, m_ref, mine, res, sems):
        x, y, c, me = _me()
        mine[0:1, :] = _silu(c_ref[...])
        mine[1:2, :] = _silu(cc_ref[...])
        mine[2:5, :] = fw_ref[...]
        mine[5:8, :] = cw_ref[...]
        s_ref[me] = mine[...]
        _exchange_tiles(lambda lin: mine, s_ref, sems.at[0], sems.at[1])
        sall = s_ref[...].reshape(NDEV * 8, D).astype(BF)
        r = jnp.dot(sall, w_ref[...].astype(BF), preferred_element_type=F32) + b_ref[...]
        res[...] = r.reshape(NDEV, 8, nsh)
        m_ref[me] = res[me]
        _exchange_tiles(lambda lin: res.at[lin], m_ref, sems.at[2], sems.at[3])

    vm = pl.BlockSpec(memory_space=pltpu.VMEM)
    return pl.pallas_call(
        body, name=name, in_specs=[vm] * 6, out_specs=[vm, vm],
        out_shape=[jax.ShapeDtypeStruct((NDEV, 8, D), F32), jax.ShapeDtypeStruct((NDEV, 8, nsh), F32)],
        scratch_shapes=[pltpu.VMEM((8, D), F32), pltpu.VMEM((NDEV, 8, nsh), F32), pltpu.SemaphoreType.DMA((4,))],
    )(c, c_ctx, ffn_w, conv_w, w_shard, b_shard)


P_DML, P_DMC, P_N1, P_QG, P_KVG, P_CB, P_N2, P_FB, P_FG, P_CW, P_FW, P_LOSS, P_ROWS = 0, 6, 12, 13, 14, 15, 16, 17, 23, 24, 27, 45, 48


def sync_small(r, *, name):
    ins = [r["dmod_lat"], r["dmod_ctx"], r["dn1g"], r["dqg"], r["dkvg"], r["dconv_b"], r["dn2g"], r["dffn_b"], r["dfg"],
           r["dconv_w"], r["dffn_w"], r["loss"]]

    def put_wide(p, row0, ref, k, n):
        for j in range(-(-n // D)):
            w = min(D, n - j * D)
            p[row0 + j:row0 + j + 1, 0:w] = ref[k:k + 1, j * D:j * D + w]

    def body(dml, dmc, n1, qg, kvg, cb, n2, fb, fg, cw, fw, loss, a_ref, sum_ref, p, sems):
        x, y, c, me = _me()
        p[...] = jnp.zeros_like(p)
        put_wide(p, P_DML, dml, 0, 6 * D)
        put_wide(p, P_DMC, dmc, 0, 6 * D)
        put_wide(p, P_N1, n1, 0, D)
        put_wide(p, P_QG, qg, 0, 512)
        put_wide(p, P_KVG, kvg, 0, KVL)
        put_wide(p, P_CB, cb, 0, CONV)
        put_wide(p, P_N2, n2, 0, D)
        put_wide(p, P_FB, fb, 0, 2 * DFF)
        put_wide(p, P_FG, fg, 0, D)
        for k in range(3):
            put_wide(p, P_CW + k, cw, k, CONV)
            put_wide(p, P_FW + 6 * k, fw, k, 2 * DFF)
        put_wide(p, P_LOSS, loss, 0, 128)
        a_ref[me] = p[...]
        _exchange_tiles(lambda lin: p, a_ref, sems.at[0], sems.at[1])
        acc = a_ref[0]
        for k in range(1, NDEV):
            acc = acc + a_ref[k]
        sum_ref[...] = acc

    vm = pl.BlockSpec(memory_space=pltpu.VMEM)
    return pl.pallas_call(
        body, name=name, in_specs=[vm] * len(ins), out_specs=[vm, vm],
        out_shape=[jax.ShapeDtypeStruct((NDEV, P_ROWS, D), F32), jax.ShapeDtypeStruct((P_ROWS, D), F32)],
        scratch_shapes=[pltpu.VMEM((P_ROWS, D), F32), pltpu.SemaphoreType.DMA((2,))],
    )(*ins)


def ada_bwd(s_all, dml, dmc, w_shard, c_ctx, *, name):
    nsh = w_shard.shape[1]

    def body(s_ref, dml_ref, dmc_ref, w_ref, c_ref, dw_ref, gc_ref, s16, dm16, part, buf, sems):
        x, y, c, me = _me()
        s16[...] = jnp.zeros_like(s16)
        dm16[...] = jnp.zeros_like(dm16)
        for k in range(NDEV):
            s16[k:k + 1, :] = s_ref[k, 0:1, :]
        s16[8:9, :] = s_ref[0, 1:2, :]
        dm16[0:8, :] = dml_ref[...]
        dm16[8:9, :] = dmc_ref[...]
        dw_ref[...] = lax.dot_general(s16[...].astype(BF), dm16[...].astype(BF), (((0,), (0,)), ((), ())),
                                      preferred_element_type=F32)
        part[...] = lax.dot_general(dm16[8:16, :].astype(BF), w_ref[...].astype(BF), (((1,), (1,)), ((), ())),
                                    preferred_element_type=F32)
        buf[me] = part[...]
        _exchange_tiles(lambda lin: part, buf, sems.at[0], sems.at[1])
        acc = buf[0]
        for k in range(1, NDEV):
            acc = acc + buf[k]
        z = c_ref[...]
        sg = jax.nn.sigmoid(z)
        gc_ref[...] = acc * (sg * (1.0 + z * (1.0 - sg)))

    vm = pl.BlockSpec(memory_space=pltpu.VMEM)
    return pl.pallas_call(
        body, name=name, in_specs=[vm] * 5, out_specs=[vm, vm],
        out_shape=[jax.ShapeDtypeStruct((D, nsh), F32), jax.ShapeDtypeStruct((8, D), F32)],
        scratch_shapes=[pltpu.VMEM((16, D), F32), pltpu.VMEM((16, nsh), F32), pltpu.VMEM((8, D), F32),
                        pltpu.VMEM((NDEV, 8, D), F32), pltpu.SemaphoreType.DMA((2,))],
    )(s_all, dml, dmc, w_shard, c_ctx)


def all_gather_pack(pack, *, name):
    R = pack.shape[0]

    def body(x_ref, out_ref, send_sems, recv_sems, local_sem):
        x, y, c, me = _me()
        sibling = (x, y, 1 - c)
        chips = [(1 - x, y), (x, 1 - y), (1 - x, 1 - y)]

        def slot(px, py, pc):
            return out_ref.at[4 * px + 2 * py + pc]

        def copy(k, block, to, src=None):
            return pltpu.make_async_remote_copy(
                src_ref=slot(*block) if src is None else src, dst_ref=slot(*block),
                send_sem=send_sems.at[k], recv_sem=recv_sems.at[k], device_id=to, device_id_type=MESH)

        mine = pltpu.make_async_copy(x_ref, slot(x, y, c), local_sem)
        mine.start()
        first = [copy(0, (x, y, c), sibling, src=x_ref)]
        first += [copy(1 + j, (x, y, c), (*chip, c), src=x_ref) for j, chip in enumerate(chips)]
        for cp in first:
            cp.start()
        passed = [copy(4 + j, (*chip, c), sibling) for j, chip in enumerate(chips)]
        for j, chip in enumerate(chips):
            copy(1 + j, (*chip, c), (x, y, c)).wait_recv()
            passed[j].start()
        copy(0, sibling, (x, y, c)).wait_recv()
        for j, chip in enumerate(chips):
            copy(4 + j, (*chip, 1 - c), (x, y, c)).wait_recv()
        for cp in first + passed:
            cp.wait_send()
        mine.wait()

    hbm = pl.BlockSpec(memory_space=pl.ANY)
    return pl.pallas_call(
        body, name=name, in_specs=[hbm], out_specs=hbm,
        out_shape=jax.ShapeDtypeStruct((NDEV, R, 128), pack.dtype),
        scratch_shapes=[pltpu.SemaphoreType.DMA((7,)), pltpu.SemaphoreType.DMA((7,)), pltpu.SemaphoreType.DMA],
    )(pack)


def reduce_scatter_send(pg, *, name):
    R = pg.shape[1]

    def body(g_ref, out_ref, send_sems, recv_sems, local_sem):
        x, y, c, me = _me()
        mine = pltpu.make_async_copy(g_ref.at[me], out_ref.at[me], local_sem)
        mine.start()
        cps = []
        for k in range(1, NDEV):
            dev, lin = _peer(x, y, c, k)
            cp = pltpu.make_async_remote_copy(src_ref=g_ref.at[lin], dst_ref=out_ref.at[me], send_sem=send_sems.at[k - 1],
                                              recv_sem=recv_sems.at[k - 1], device_id=dev, device_id_type=MESH)
            cp.start()
            cps.append(cp)
        for k in range(1, NDEV):
            dev, lin = _peer(x, y, c, k)
            pltpu.make_async_remote_copy(src_ref=g_ref.at[me], dst_ref=out_ref.at[lin], send_sem=send_sems.at[k - 1],
                                         recv_sem=recv_sems.at[k - 1], device_id=dev, device_id_type=MESH).wait_recv()
        for cp in cps:
            cp.wait_send()
        mine.wait()

    hbm = pl.BlockSpec(memory_space=pl.ANY)
    return pl.pallas_call(
        body, name=name, in_specs=[hbm], out_specs=hbm,
        out_shape=jax.ShapeDtypeStruct((NDEV, R, 128), pg.dtype),
        scratch_shapes=[pltpu.SemaphoreType.DMA((7,)), pltpu.SemaphoreType.DMA((7,)), pltpu.SemaphoreType.DMA],
    )(pg)


def sum_slots(rbuf, *, name, tr=1024):
    R = rbuf.shape[1]
    tr = _pick(R, tr, 16)

    def body(r_ref, o_ref):
        acc = r_ref[0].astype(F32)
        for k in range(1, NDEV):
            acc = acc + r_ref[k].astype(F32)
        o_ref[...] = acc

    return pl.pallas_call(
        body, name=name, grid=(R // tr,), in_specs=[pl.BlockSpec((NDEV, tr, 128), lambda i: (0, i, 0))],
        out_specs=pl.BlockSpec((tr, 128), lambda i: (i, 0)), out_shape=jax.ShapeDtypeStruct((R, 128), F32),
        compiler_params=pltpu.CompilerParams(dimension_semantics=("parallel",)),
    )(rbuf)


def _adamw_refs(w_ref, g_ref, m_ref, v_ref, d_ref, nm_ref, nv_ref):
    gv = g_ref[...]
    nm = B1 * m_ref[...] + (1.0 - B1) * gv
    nv = B2 * v_ref[...] + (1.0 - B2) * (gv * gv)
    m_hat = nm / (1.0 - B1 ** STEP)
    v_hat = nv / (1.0 - B2 ** STEP)
    d_ref[...] = -LR * (m_hat / (jnp.sqrt(v_hat) + AEPS) + WD * w_ref[...])
    nm_ref[...] = nm
    nv_ref[...] = nv


def adamw_many(ws, gs, ms, vs, *, name):
    n = len(ws)

    def body(*refs):
        for k in range(n):
            _adamw_refs(refs[k], refs[n + k], refs[2 * n + k], refs[3 * n + k],
                        refs[4 * n + k], refs[5 * n + k], refs[6 * n + k])

    vm = pl.BlockSpec(memory_space=pltpu.VMEM)
    sh = [jax.ShapeDtypeStruct(w.shape, F32) for w in ws]
    outs = pl.pallas_call(body, name=name, in_specs=[vm] * (4 * n), out_specs=[vm] * (3 * n), out_shape=sh * 3,
                          )(*ws, *gs, *ms, *vs)
    return outs[:n], outs[n:2 * n], outs[2 * n:]


def adamw(w, g, m, v, *, name, tr=256):
    R, C = w.shape
    tr = _pick(R, tr, 8)

    def body(w_ref, g_ref, m_ref, v_ref, d_ref, nm_ref, nv_ref):
        _adamw_refs(w_ref, g_ref, m_ref, v_ref, d_ref, nm_ref, nv_ref)

    blk = pl.BlockSpec((tr, C), lambda i: (i, 0))
    sh = jax.ShapeDtypeStruct((R, C), F32)
    return pl.pallas_call(
        body, name=name, grid=(R // tr,), in_specs=[blk, blk, blk, blk], out_specs=[blk, blk, blk],
        out_shape=[sh, sh, sh], compiler_params=pltpu.CompilerParams(dimension_semantics=("parallel",)),
    )(w, g, m, v)


_BIG = [("w_in", (D, 4256), 1), ("w_uq", (QL, 768), 1), ("w_ukv", (KVL, 1024), 1), ("w_attn_out", (512, D), 1),
        ("w_conv_out", (512, D), 1), ("w_o", (D, D), 0), ("w_up", (D, 2 * DFF), 1), ("w_down", (DFF, D), 0)]


def _shard_shape(full, axis):
    s = list(full)
    s[axis] //= NDEV
    return tuple(s)


def _pack_rows():
    return [int(np.prod(full)) // NDEV // 128 for _, full, _ in _BIG]


def _unpack_full(gathered):
    out, off = {}, 0
    for (nm, full, axis), rows in zip(_BIG, _pack_rows()):
        seg = gathered[:, off:off + rows].reshape((NDEV,) + _shard_shape(full, axis))
        off += rows
        out[nm] = seg.reshape(full) if axis == 0 else seg.transpose(1, 0, 2).reshape(full)
    return out


def _pack_grads(grads):
    segs = []
    for (nm, full, axis), rows in zip(_BIG, _pack_rows()):
        g = grads[nm]
        sh = _shard_shape(full, axis)
        blk = g.reshape((NDEV,) + sh) if axis == 0 else g.reshape(full[0], NDEV, sh[1]).transpose(1, 0, 2)
        segs.append(blk.reshape(NDEV, rows, 128))
    segs.append(jnp.zeros((NDEV, PACK_ROWS - sum(_pack_rows()), 128), segs[0].dtype))
    return jnp.concatenate(segs, axis=1)


PACK_ROWS = 15360


def _padc(a, n=D):
    return jnp.pad(a, ((0, 0), (0, n - a.shape[1])))


def kernel(x, c, ctx, c_ctx, w_ada, b_ada, norm1_g, w_in, q_norm_g, kv_norm_g, w_uq, w_ukv, conv_w, conv_b, w_attn_out, w_conv_out, w_o, norm2_g, w_up, ffn_conv_w, ffn_conv_b, w_down, final_g, loss_target, m_c_ctx, m_w_ada, m_b_ada, m_norm1_g, m_w_in, m_q_norm_g, m_kv_norm_g, m_w_uq, m_w_ukv, m_conv_w, m_conv_b, m_w_attn_out, m_w_conv_out, m_w_o, m_norm2_g, m_w_up, m_ffn_conv_w, m_ffn_conv_b, m_w_down, m_final_g, v_c_ctx, v_w_ada, v_b_ada, v_norm1_g, v_w_in, v_q_norm_g, v_kv_norm_g, v_w_uq, v_w_ukv, v_conv_w, v_conv_b, v_w_attn_out, v_w_conv_out, v_w_o, v_norm2_g, v_w_up, v_ffn_conv_w, v_ffn_conv_b, v_w_down, v_final_g):
    me = 4 * lax.axis_index("x") + 2 * lax.axis_index("y") + lax.axis_index("c")
    W = dict(c_ctx=c_ctx, w_ada=w_ada, b_ada=b_ada, norm1_g=norm1_g, w_in=w_in, q_norm_g=q_norm_g, kv_norm_g=kv_norm_g,
             w_uq=w_uq, w_ukv=w_ukv, conv_w=conv_w, conv_b=conv_b, w_attn_out=w_attn_out, w_conv_out=w_conv_out, w_o=w_o,
             norm2_g=norm2_g, w_up=w_up, ffn_conv_w=ffn_conv_w, ffn_conv_b=ffn_conv_b, w_down=w_down, final_g=final_g)
    M = dict(c_ctx=m_c_ctx, w_ada=m_w_ada, b_ada=m_b_ada, norm1_g=m_norm1_g, w_in=m_w_in, q_norm_g=m_q_norm_g,
             kv_norm_g=m_kv_norm_g, w_uq=m_w_uq, w_ukv=m_w_ukv, conv_w=m_conv_w, conv_b=m_conv_b, w_attn_out=m_w_attn_out,
             w_conv_out=m_w_conv_out, w_o=m_w_o, norm2_g=m_norm2_g, w_up=m_w_up, ffn_conv_w=m_ffn_conv_w,
             ffn_conv_b=m_ffn_conv_b, w_down=m_w_down, final_g=m_final_g)
    V = dict(c_ctx=v_c_ctx, w_ada=v_w_ada, b_ada=v_b_ada, norm1_g=v_norm1_g, w_in=v_w_in, q_norm_g=v_q_norm_g,
             kv_norm_g=v_kv_norm_g, w_uq=v_w_uq, w_ukv=v_w_ukv, conv_w=v_conv_w, conv_b=v_conv_b, w_attn_out=v_w_attn_out,
             w_conv_out=v_w_conv_out, w_o=v_w_o, norm2_g=v_norm2_g, w_up=v_w_up, ffn_conv_w=v_ffn_conv_w,
             ffn_conv_b=v_ffn_conv_b, w_down=v_w_down, final_g=v_final_g)
    names = list(W)
    as2d = lambda a: a.reshape(1, -1) if a.ndim == 1 else a.reshape(a.shape[-2], a.shape[-1])
    W2 = {k: as2d(a) for k, a in W.items()}
    M2 = {k: as2d(a) for k, a in M.items()}
    V2 = {k: as2d(a) for k, a in V.items()}
    nsh = W2["w_ada"].shape[1]

    b_sh = lax.dynamic_slice(W2["b_ada"], (0, me * nsh), (1, nsh))
    s_all, m_all = ada_fwd(c, W2["c_ctx"], _padc(W2["ffn_conv_w"]), _padc(W2["conv_w"]), W2["w_ada"], b_sh, name="ada_fwd")
    mod_lat = m_all[:, 0, :].reshape(1, 6 * D)
    mod_ctx = m_all[:, 1, :].reshape(1, 6 * D)
    ffn_w_full = s_all[:, 2:5, :2 * DFF // NDEV].transpose(1, 0, 2).reshape(3, 2 * DFF)
    conv_w_full = s_all[:, 5:8, :CONV // NDEV].transpose(1, 0, 2).reshape(3, CONV)

    segs = [W2[nm].astype(BF).reshape(-1, 128) for nm, _, _ in _BIG]
    segs.append(jnp.zeros((PACK_ROWS - sum(_pack_rows()), 128), BF))
    gathered = all_gather_pack(jnp.concatenate(segs, axis=0), name="all_gather_weights")
    full = _unpack_full(gathered)

    r = _local_step(x[0], ctx[0], loss_target[0], mod_lat, mod_ctx, W2["norm1_g"], W2["q_norm_g"], W2["kv_norm_g"],
                    W2["norm2_g"], W2["final_g"], conv_w_full, W2["conv_b"], ffn_w_full, W2["ffn_conv_b"],
                    _w_in_internal(full["w_in"]), _w_q_internal(full["w_uq"]), _w_kv_internal(full["w_ukv"]),
                    full["w_attn_out"], full["w_conv_out"], full["w_o"], full["w_up"], full["w_down"])

    gfull = dict(w_in=_w_in_external(r["dwin"]).astype(BF), w_uq=_w_q_external(r["dwq2"]).astype(BF),
                 w_ukv=_w_kv_external(r["dwkv2"]).astype(BF), w_attn_out=r["dwao"], w_conv_out=r["dwco"], w_o=r["dwo"],
                 w_up=r["dwup"], w_down=r["dwdn"])
    rbuf = reduce_scatter_send(_pack_grads(gfull), name="reduce_scatter_grads")
    gsum = sum_slots(rbuf, name="sum_grads")
    G = {}
    off = 0
    for (nm, fshape, axis), rows in zip(_BIG, _pack_rows()):
        G[nm] = gsum[off:off + rows].reshape(_shard_shape(fshape, axis))
        off += rows

    a_buf, ssum = sync_small(r, name="sync_small")
    loss = ssum[P_LOSS, 0]
    G["norm1_g"] = ssum[P_N1:P_N1 + 1]
    G["q_norm_g"] = ssum[P_QG:P_QG + 1, :QL]
    G["kv_norm_g"] = ssum[P_KVG:P_KVG + 1, :KVL]
    G["conv_b"] = ssum[P_CB:P_CB + 1, :CONV]
    G["norm2_g"] = ssum[P_N2:P_N2 + 1]
    G["ffn_conv_b"] = ssum[P_FB:P_FB + 6].reshape(1, 6 * D)[:, :2 * DFF]
    G["final_g"] = ssum[P_FG:P_FG + 1]
    G["conv_w"] = lax.dynamic_slice(ssum[P_CW:P_CW + 3, :CONV], (0, me * (CONV // NDEV)), (3, CONV // NDEV))
    G["ffn_conv_w"] = lax.dynamic_slice(ssum[P_FW:P_FW + 18].reshape(3, 6 * D), (0, me * (2 * DFF // NDEV)),
                                        (3, 2 * DFF // NDEV))
    G["b_ada"] = (ssum[P_DML:P_DML + 6] + ssum[P_DMC:P_DMC + 6]).reshape(1, 6 * D)

    dml = lax.dynamic_slice(a_buf[:, P_DML:P_DML + 6, :].reshape(NDEV, 6 * D), (0, me * nsh), (NDEV, nsh))
    dmc = lax.dynamic_slice(ssum[P_DMC:P_DMC + 6].reshape(1, 6 * D), (0, me * nsh), (1, nsh))
    G["w_ada"], gcc = ada_bwd(s_all, dml, dmc, W2["w_ada"], W2["c_ctx"], name="ada_bwd")
    G["c_ctx"] = gcc[0:1]

    DL, NM, NV = {}, {}, {}
    for nm in ["w_ada"] + [b[0] for b in _BIG]:
        DL[nm], NM[nm], NV[nm] = adamw(W2[nm], G[nm], M2[nm], V2[nm], name="adamw_" + nm)
    small = ["c_ctx", "b_ada", "norm1_g", "q_norm_g", "kv_norm_g", "conv_b", "norm2_g", "ffn_conv_b", "final_g", "conv_w",
             "ffn_conv_w"]
    ds, nms, nvs = adamw_many([W2[k] for k in small], [G[k] for k in small], [M2[k] for k in small],
                              [V2[k] for k in small], name="adamw_small")
    for k, nm in enumerate(small):
        DL[nm], NM[nm], NV[nm] = ds[k], nms[k], nvs[k]

    outs = [loss, r["dx"][None]]
    for grp in (G, DL, NM, NV):
        outs += [grp[nm].reshape(W[nm].shape) for nm in names]
    return tuple(outs)
```

```python
import functools
import numpy as np
import jax
import jax.numpy as jnp
from jax import lax
from jax.experimental import pallas as pl
from jax.experimental.pallas import tpu as pltpu

F32 = jnp.float32
BF = jnp.bfloat16
MESH = pl.DeviceIdType.MESH

D = 1024
T = 2048
TC = 256
TKV = T + TC
GRID_W = 64
NH = 8
DN = 64
DR = 32
DV = 64
QL = 384
KVL = 256
CONV = 512
DFF = 2816
EPS = 1e-6
ROPE_THETA = 10000.0
SCALE = (DN + DR) ** -0.5
NDEV = 8
HP = 128

O_GA, O_GC, O_KV, O_Q, O_CV = 0, 1024, 2048, 2560, 3072
NIN = 4608
CVB = 256
N_IN = 4256
SH_IN = N_IN // NDEV

LR, B1, B2, AEPS, WD, STEP = 0.001, 0.9, 0.999, 1e-08, 0.01, 10


def _pick(n, target, mult=128):
    best = None
    for d in range(mult, min(n, target) + 1, mult):
        if n % d == 0:
            best = d
    return best if best is not None else n


def _swap_start(g):
    return 8 * (g ^ 1)


def mm(a, b, *, ta=False, tb=False, out_dtype=F32, name, tm=512, tn=512, tk=2048, M=None, N=None, K=None,
       a_off=(0, 0), b_off=(0, 0), a_stack=False, b_stack=False, o_stack=False):
    def dims(arr, stack):
        return (arr.shape[1], 2 * arr.shape[2]) if stack else arr.shape

    ar, ac = dims(a, a_stack)
    br, bc = dims(b, b_stack)
    M = M or ((ac if ta else ar) - a_off[1 if ta else 0])
    K = K or ((ar if ta else ac) - a_off[0 if ta else 1])
    N = N or ((br if tb else bc) - b_off[0 if tb else 1])
    tm = _pick(M, tm, 128 if ta else 16)
    tn = _pick(N // 2 if (o_stack or (b_stack and not tb)) else N, tn, 128)
    tk = _pick(K // 2 if ((a_stack and not ta) or (b_stack and tb)) else K, tk, 128)
    nk = K // tk
    ca = 0 if ta else 1
    cb = 1 if tb else 0

    def body(a_ref, b_ref, o_ref, acc):
        k = pl.program_id(2)
        part = lax.dot_general(a_ref[...].astype(BF), b_ref[...].astype(BF),
                               (((ca,), (cb,)), ((), ())), preferred_element_type=F32)
        if nk == 1:
            o_ref[...] = part.astype(o_ref.dtype)
        else:
            @pl.when(k == 0)
            def _():
                acc[...] = part

            @pl.when(k > 0)
            def _():
                acc[...] += part

            @pl.when(k == nk - 1)
            def _():
                o_ref[...] = acc[...].astype(o_ref.dtype)

    def spec(blk, rc, off, stack, ncols):
        assert off[0] % blk[0] == 0 and off[1] % blk[1] == 0, (name, blk, off)
        ro, co = off[0] // blk[0], off[1] // blk[1]
        if not stack:
            return pl.BlockSpec(blk, lambda i, j, k: (rc(i, j, k)[0] + ro, rc(i, j, k)[1] + co))
        nhb = ncols // 2 // blk[1]
        return pl.BlockSpec((None,) + blk,
                            lambda i, j, k: ((rc(i, j, k)[1] + co) // nhb, rc(i, j, k)[0] + ro, (rc(i, j, k)[1] + co) % nhb))

    a_spec = spec((tk, tm), lambda i, j, k: (k, i), a_off, a_stack, ac) if ta else \
        spec((tm, tk), lambda i, j, k: (i, k), a_off, a_stack, ac)
    b_spec = spec((tn, tk), lambda i, j, k: (j, k), b_off, b_stack, bc) if tb else \
        spec((tk, tn), lambda i, j, k: (k, j), b_off, b_stack, bc)
    o_spec = spec((tm, tn), lambda i, j, k: (i, j), (0, 0), o_stack, N)
    o_shape = (2, M, N // 2) if o_stack else (M, N)
    return pl.pallas_call(
        body, name=name, grid=(M // tm, N // tn, nk),
        in_specs=[a_spec, b_spec], out_specs=o_spec, out_shape=jax.ShapeDtypeStruct(o_shape, out_dtype),
        scratch_shapes=[pltpu.VMEM((tm, tn) if nk > 1 else (8, 128), F32)],
        compiler_params=pltpu.CompilerParams(dimension_semantics=("parallel", "parallel", "arbitrary")),
    )(a, b)


def _row(width):
    return pl.BlockSpec((1, width), lambda *_: (0, 0))


NLAT = T // TC


def normmod_cat(ctx, x, g, csc, csh, sc, sh, *, name, tm=256):
    assert tm == TC

    def body(c_ref, x_ref, g_ref, csc_ref, csh_ref, sc_ref, sh_ref, h_ref):
        last = pl.program_id(0) == NLAT
        xv = jnp.where(last, c_ref[...], x_ref[...])
        scv = jnp.where(last, csc_ref[...], sc_ref[...])
        shv = jnp.where(last, csh_ref[...], sh_ref[...])
        r = lax.rsqrt(jnp.mean(xv * xv, axis=-1, keepdims=True) + EPS)
        h_ref[...] = ((xv * r * g_ref[...]) * (1.0 + scv) + shv).astype(BF)

    return pl.pallas_call(
        body, name=name, grid=(TKV // tm,),
        in_specs=[pl.BlockSpec((tm, D), lambda i: (0, 0)), pl.BlockSpec((tm, D), lambda i: (jnp.minimum(i, NLAT - 1), 0)),
                  _row(D), _row(D), _row(D), _row(D), _row(D)],
        out_specs=pl.BlockSpec((tm, D), lambda i: (i, 0)), out_shape=jax.ShapeDtypeStruct((TKV, D), BF),
        compiler_params=pltpu.CompilerParams(dimension_semantics=("parallel",)),
    )(ctx, x, g, csc, csh, sc, sh)


def resid_normmod(x, a, gate, g, sc, sh, *, name, tm=256):
    R = x.shape[0]

    def body(x_ref, a_ref, gate_ref, g_ref, sc_ref, sh_ref, x1_ref, h_ref):
        xv = x_ref[...] + gate_ref[...] * a_ref[...]
        x1_ref[...] = xv
        r = lax.rsqrt(jnp.mean(xv * xv, axis=-1, keepdims=True) + EPS)
        h_ref[...] = ((xv * r * g_ref[...]) * (1.0 + sc_ref[...]) + sh_ref[...]).astype(BF)

    blk = pl.BlockSpec((tm, D), lambda i: (i, 0))
    return pl.pallas_call(
        body, name=name, grid=(R // tm,), in_specs=[blk, blk, _row(D), _row(D), _row(D), _row(D)],
        out_specs=[blk, blk],
        out_shape=[jax.ShapeDtypeStruct((R, D), F32), jax.ShapeDtypeStruct((R, D), BF)],
        compiler_params=pltpu.CompilerParams(dimension_semantics=("parallel",)),
    )(x, a, gate, g, sc, sh)


def kvprep(pc, p, kvg, wkv2, ck, sk, *, name, tm=256):
    assert tm == TC
    nb = TKV // tm
    kvcol = O_KV // 512

    def body(pc_ref, p_ref, g_ref, w_ref, ck_ref, sk_ref, k_ref, v_ref, ckv_ref):
        i = pl.program_id(0)
        t = jnp.where(i == NLAT, pc_ref[...], p_ref[...])
        pk = t[:, :KVL]
        r = lax.rsqrt(jnp.mean(pk * pk, axis=-1, keepdims=True) + EPS)
        ckv = (pk * r * g_ref[...]).astype(BF)
        ckv_ref[...] = ckv
        kv2 = jnp.dot(ckv, w_ref[...], preferred_element_type=F32)
        krr = t[:, KVL:KVL + HP] * ck_ref[...] + t[:, KVL + HP:KVL + 2 * HP] * sk_ref[...]
        k_ref[...] = (kv2[:, :NH * HP] + jnp.concatenate([krr] * NH, axis=1)).astype(BF)
        v_ref[...] = kv2[:, NH * HP:].astype(BF)

    return pl.pallas_call(
        body, name=name, grid=(nb,),
        in_specs=[pl.BlockSpec((tm, 512), lambda i: (0, 0)),
                  pl.BlockSpec((tm, 512), lambda i: (jnp.minimum(i, NLAT - 1), kvcol)),
                  _row(KVL), pl.BlockSpec((KVL, NH * HP + NH * DV), lambda i: (0, 0)),
                  pl.BlockSpec((tm, HP), lambda i: (i, 0)), pl.BlockSpec((tm, HP), lambda i: (i, 0))],
        out_specs=[pl.BlockSpec((tm, NH * HP), lambda i: (i, 0)), pl.BlockSpec((tm, NH * DV), lambda i: (i, 0)),
                   pl.BlockSpec((tm, KVL), lambda i: (i, 0))],
        out_shape=[jax.ShapeDtypeStruct((TKV, NH * HP), BF), jax.ShapeDtypeStruct((TKV, NH * DV), BF),
                   jax.ShapeDtypeStruct((TKV, KVL), BF)],
        compiler_params=pltpu.CompilerParams(dimension_semantics=("parallel",)),
    )(pc, p, kvg, wkv2, ck, sk)


def qprep(p, qg, wq2, cq_t, sq_t, *, name, tm=256):
    qcol = O_Q // 512

    def body(p_ref, g_ref, w_ref, c_ref, s_ref, q_ref, cq_ref):
        pq = p_ref[...]
        r = lax.rsqrt(jnp.sum(pq * pq, axis=-1, keepdims=True) * (1.0 / QL) + EPS)
        cq = (pq * r * g_ref[...]).astype(BF)
        cq_ref[...] = cq
        q2 = jnp.dot(cq, w_ref[...], preferred_element_type=F32)
        cc = jnp.concatenate([c_ref[...]] * NH, axis=1)
        ss = jnp.concatenate([s_ref[...]] * NH, axis=1)
        q_ref[...] = (q2[:, :NH * HP] * cc + q2[:, NH * HP:] * ss).astype(BF)

    return pl.pallas_call(
        body, name=name, grid=(T // tm,),
        in_specs=[pl.BlockSpec((tm, 512), lambda i: (i, qcol)), _row(512),
                  pl.BlockSpec((512, 2 * NH * HP), lambda i: (0, 0)),
                  pl.BlockSpec((tm, HP), lambda i: (i, 0)), pl.BlockSpec((tm, HP), lambda i: (i, 0))],
        out_specs=[pl.BlockSpec((tm, NH * HP), lambda i: (i, 0)), pl.BlockSpec((tm, 512), lambda i: (i, 0))],
        out_shape=[jax.ShapeDtypeStruct((T, NH * HP), BF), jax.ShapeDtypeStruct((T, 512), BF)],
        compiler_params=pltpu.CompilerParams(dimension_semantics=("parallel",)),
    )(p, qg, wq2, cq_t, sq_t)


def _head_mask(h):
    lanes = lax.broadcasted_iota(jnp.int32, (1, 2 * DV), 1)
    return (lanes // DV) == (h % 2)


def attn_fwd(q, k, v, *, name, tq=256):
    def body(q_ref, k_ref, v_ref, o_ref):
        h = pl.program_id(1)
        s = lax.dot_general(q_ref[...], k_ref[...], (((1,), (1,)), ((), ())), preferred_element_type=F32) * SCALE
        m = jnp.max(s, axis=-1, keepdims=True)
        e = jnp.exp(s - m)
        pr = (e * (1.0 / jnp.sum(e, axis=-1, keepdims=True))).astype(BF)
        vm = jnp.where(_head_mask(h), v_ref[...], jnp.zeros_like(v_ref[...]))
        o2 = jnp.dot(pr, vm, preferred_element_type=F32).astype(BF)

        @pl.when(h % 2 == 0)
        def _():
            o_ref[...] = o2

        @pl.when(h % 2 == 1)
        def _():
            o_ref[...] = o_ref[...] + o2

    return pl.pallas_call(
        body, name=name, grid=(T // tq, NH),
        in_specs=[pl.BlockSpec((tq, HP), lambda i, h: (i, h)), pl.BlockSpec((TKV, HP), lambda i, h: (0, h)),
                  pl.BlockSpec((TKV, 2 * DV), lambda i, h: (0, h // 2))],
        out_specs=pl.BlockSpec((tq, 2 * DV), lambda i, h: (i, h // 2)),
        out_shape=jax.ShapeDtypeStruct((T, NH * DV), BF),
        compiler_params=pltpu.CompilerParams(dimension_semantics=("parallel", "arbitrary")),
    )(q, k, v)


def _shift_dn(x):
    n = x.shape[0]
    rows = lax.broadcasted_iota(jnp.int32, (n, 1), 0)
    return jnp.where(rows == 0, 0.0, pltpu.roll(x, 1, axis=0))


def _shift_up(x):
    n = x.shape[0]
    rows = lax.broadcasted_iota(jnp.int32, (n, 1), 0)
    return jnp.where(rows == n - 1, 0.0, pltpu.roll(x, n - 1, axis=0))


def _conv(x, w_ref, b_ref):
    return b_ref[...] + _shift_dn(x) * w_ref[0:1, :] + x * w_ref[1:2, :] + _shift_up(x) * w_ref[2:3, :]


def _conv_t(dy, w_ref):
    return _shift_up(dy) * w_ref[0:1, :] + dy * w_ref[1:2, :] + _shift_dn(dy) * w_ref[2:3, :]


def _conv_wgrad(dw_ref, dy, x):
    dw_ref[0:1, :] = jnp.sum(dy * _shift_dn(x), axis=0, keepdims=True)
    dw_ref[1:2, :] = jnp.sum(dy * x, axis=0, keepdims=True)
    dw_ref[2:3, :] = jnp.sum(dy * _shift_up(x), axis=0, keepdims=True)


def convz(p, cw, cb, *, name):
    o0 = O_CV // (3 * CVB)

    def body(p_ref, w_ref, bias_ref, z_ref):
        xv, bv, cv = p_ref[:, 0:CVB], p_ref[:, CVB:2 * CVB], p_ref[:, 2 * CVB:3 * CVB]
        z_ref[...] = (bv * _conv(cv * xv, w_ref, bias_ref)).astype(BF)

    return pl.pallas_call(
        body, name=name, grid=(CONV // CVB,),
        in_specs=[pl.BlockSpec((T, 3 * CVB), lambda j: (0, o0 + j)), pl.BlockSpec((3, CVB), lambda j: (0, j)),
                  pl.BlockSpec((1, CVB), lambda j: (0, j))],
        out_specs=pl.BlockSpec((T, CVB), lambda j: (0, j)),
        out_shape=jax.ShapeDtypeStruct((T, CONV), BF),
        compiler_params=pltpu.CompilerParams(dimension_semantics=("parallel",)),
    )(p, cw, cb)


def gate_merge(p, ya, yc, *, name, tm=256):
    def body(ga_ref, gc_ref, ya_ref, yc_ref, o_ref):
        o_ref[...] = (jax.nn.sigmoid(ga_ref[...]) * ya_ref[...] + jax.nn.sigmoid(gc_ref[...]) * yc_ref[...]).astype(BF)

    blk = pl.BlockSpec((tm, D), lambda i: (i, 0))
    return pl.pallas_call(
        body, name=name, grid=(T // tm,),
        in_specs=[pl.BlockSpec((tm, D), lambda i: (i, O_GA // D)), pl.BlockSpec((tm, D), lambda i: (i, O_GC // D)), blk, blk],
        out_specs=blk, out_shape=jax.ShapeDtypeStruct((T, D), BF),
        compiler_params=pltpu.CompilerParams(dimension_semantics=("parallel",)),
    )(p, p, ya, yc)


def ffn_act(u0, cw, cb, *, name, tc=256):
    nb = DFF // tc

    def body(u_ref, wg_ref, wv_ref, bg_ref, bv_ref, f_ref):
        ug = _conv(u_ref[0], wg_ref, bg_ref)
        uv = _conv(u_ref[1], wv_ref, bv_ref)
        f_ref[...] = (ug * jax.nn.sigmoid(ug) * uv).astype(BF)

    return pl.pallas_call(
        body, name=name, grid=(nb,),
        in_specs=[pl.BlockSpec((2, T, tc), lambda j: (0, 0, j)),
                  pl.BlockSpec((3, tc), lambda j: (0, j)), pl.BlockSpec((3, tc), lambda j: (0, nb + j)),
                  pl.BlockSpec((1, tc), lambda j: (0, j)), pl.BlockSpec((1, tc), lambda j: (0, nb + j))],
        out_specs=pl.BlockSpec((T, tc), lambda j: (0, j)),
        out_shape=jax.ShapeDtypeStruct((T, DFF), BF),
        compiler_params=pltpu.CompilerParams(dimension_semantics=("parallel",)),
    )(u0, cw, cw, cb, cb)


def final_loss(x1, d, g2, fg, tgt, *, name, tm=256):
    def body(x1_ref, d_ref, g2_ref, fg_ref, t_ref, dx_ref, dd_ref, dfg_ref, loss_ref):
        i = pl.program_id(0)
        xv = x1_ref[...] + g2_ref[...] * d_ref[...]
        r = lax.rsqrt(jnp.mean(xv * xv, axis=-1, keepdims=True) + EPS)
        xh = xv * r
        diff = xh * fg_ref[...] - t_ref[...]
        part = 0.5 * jnp.sum(jnp.mean(diff * diff, axis=-1, keepdims=True), axis=0, keepdims=True)
        dy = diff * (1.0 / D)
        a = dy * fg_ref[...]
        dx = r * (a - xh * jnp.mean(a * xh, axis=-1, keepdims=True))
        dx_ref[...] = dx
        dd_ref[...] = (dx * g2_ref[...]).astype(BF)
        dfg = jnp.sum(dy * xh, axis=0, keepdims=True)

        @pl.when(i == 0)
        def _():
            dfg_ref[...] = dfg
            loss_ref[...] = jnp.broadcast_to(part, (1, 128))

        @pl.when(i > 0)
        def _():
            dfg_ref[...] += dfg
            loss_ref[...] += jnp.broadcast_to(part, (1, 128))

    blk = pl.BlockSpec((tm, D), lambda i: (i, 0))
    return pl.pallas_call(
        body, name=name, grid=(T // tm,), in_specs=[blk, blk, _row(D), _row(D), blk],
        out_specs=[blk, blk, _row(D), _row(128)],
        out_shape=[jax.ShapeDtypeStruct((T, D), F32), jax.ShapeDtypeStruct((T, D), BF),
                   jax.ShapeDtypeStruct((1, D), F32), jax.ShapeDtypeStruct((1, 128), F32)],
        compiler_params=pltpu.CompilerParams(dimension_semantics=("arbitrary",)),
    )(x1, d, g2, fg, tgt)


def normmod_bwd(x, dh, g, sc, dres, gsrc, gate, *, name, tm=256):
    R = x.shape[0]
    has_res = dres is not None

    def body(*refs):
        if has_res:
            x_ref, dh_ref, g_ref, sc_ref, dres_ref, gsrc_ref, gate_ref, dx_ref, dxg_ref, st_ref = refs
        else:
            x_ref, dh_ref, g_ref, sc_ref, st_ref = refs
        i = pl.program_id(0)
        xv = x_ref[...]
        r = lax.rsqrt(jnp.mean(xv * xv, axis=-1, keepdims=True) + EPS)
        xh = xv * r
        dhv = dh_ref[...]
        n = xh * g_ref[...]
        dn = dhv * (1.0 + sc_ref[...])
        a = dn * g_ref[...]
        rows = [jnp.sum(dhv, axis=0, keepdims=True), jnp.sum(dhv * n, axis=0, keepdims=True),
                jnp.sum(dn * xh, axis=0, keepdims=True)]
        if has_res:
            dr = dres_ref[...]
            dx = dr + r * (a - xh * jnp.mean(a * xh, axis=-1, keepdims=True))
            dx_ref[...] = dx
            dxg_ref[...] = (dx * gate_ref[...]).astype(BF)
            rows.append(jnp.sum(dr * gsrc_ref[...], axis=0, keepdims=True))
        else:
            rows.append(jnp.zeros((1, D), F32))

        @pl.when(i == 0)
        def _():
            for k, row in enumerate(rows):
                st_ref[k:k + 1, :] = row

        @pl.when(i > 0)
        def _():
            for k, row in enumerate(rows):
                st_ref[k:k + 1, :] += row

    blk = pl.BlockSpec((tm, D), lambda i: (i, 0))
    st_spec = pl.BlockSpec((4, D), lambda i: (0, 0))
    st_shape = jax.ShapeDtypeStruct((4, D), F32)
    cp = pltpu.CompilerParams(dimension_semantics=("arbitrary",))
    if has_res:
        return pl.pallas_call(
            body, name=name, grid=(R // tm,), in_specs=[blk, blk, _row(D), _row(D), blk, blk, _row(D)],
            out_specs=[blk, blk, st_spec],
            out_shape=[jax.ShapeDtypeStruct((R, D), F32), jax.ShapeDtypeStruct((R, D), BF), st_shape],
            compiler_params=cp,
        )(x, dh, g, sc, dres, gsrc, gate)
    return pl.pallas_call(
        body, name=name, grid=(R // tm,), in_specs=[blk, blk, _row(D), _row(D)],
        out_specs=st_spec, out_shape=st_shape, compiler_params=cp,
    )(x, dh, g, sc)


def ffn_act_bwd(u0, df, cw, cb, *, name, tc=256):
    nb = DFF // tc

    def body(u_ref, df_ref, wg_ref, wv_ref, bg_ref, bv_ref, du_ref, dw_ref, db_ref):
        xg, xv = u_ref[0], u_ref[1]
        ug = _conv(xg, wg_ref, bg_ref)
        uv = _conv(xv, wv_ref, bv_ref)
        sig = jax.nn.sigmoid(ug)
        dfv = df_ref[...]
        dug = dfv * uv * (sig * (1.0 + ug * (1.0 - sig)))
        duv = dfv * (ug * sig)
        du_ref[0] = _conv_t(dug, wg_ref).astype(BF)
        du_ref[1] = _conv_t(duv, wv_ref).astype(BF)
        _conv_wgrad(dw_ref.at[0], dug, xg)
        _conv_wgrad(dw_ref.at[1], duv, xv)
        db_ref[0] = jnp.sum(dug, axis=0, keepdims=True)
        db_ref[1] = jnp.sum(duv, axis=0, keepdims=True)

    lo = lambda r: pl.BlockSpec((r, tc), lambda j: (0, j))
    hi = lambda r: pl.BlockSpec((r, tc), lambda j: (0, nb + j))
    st = lambda r: pl.BlockSpec((2, r, tc), lambda j: (0, 0, j))
    return pl.pallas_call(
        body, name=name, grid=(nb,),
        in_specs=[st(T), lo(T), lo(3), hi(3), lo(1), hi(1)],
        out_specs=[st(T), st(3), st(1)],
        out_shape=[jax.ShapeDtypeStruct((2, T, DFF), BF), jax.ShapeDtypeStruct((2, 3, DFF), F32),
                   jax.ShapeDtypeStruct((2, 1, DFF), F32)],
        compiler_params=pltpu.CompilerParams(dimension_semantics=("parallel",)),
    )(u0, df, cw, cw, cb, cb)


def gate_merge_bwd(p, ya, yc, dm, *, name, tm=256):
    def body(ga_ref, gc_ref, ya_ref, yc_ref, dm_ref, dya_ref, dyc_ref, dp_ref):
        sa, sc_ = jax.nn.sigmoid(ga_ref[...]), jax.nn.sigmoid(gc_ref[...])
        dmv = dm_ref[...]
        dya_ref[...] = (dmv * sa).astype(BF)
        dyc_ref[...] = (dmv * sc_).astype(BF)
        dp_ref[:, 0:D] = (dmv * ya_ref[...] * (sa * (1.0 - sa))).astype(BF)
        dp_ref[:, D:2 * D] = (dmv * yc_ref[...] * (sc_ * (1.0 - sc_))).astype(BF)

    blk = pl.BlockSpec((tm, D), lambda i: (i, 0))
    sh = jax.ShapeDtypeStruct((T, D), BF)
    return pl.pallas_call(
        body, name=name, grid=(T // tm,),
        in_specs=[pl.BlockSpec((tm, D), lambda i: (i, O_GA // D)), pl.BlockSpec((tm, D), lambda i: (i, O_GC // D)), blk, blk, blk],
        out_specs=[blk, blk, pl.BlockSpec((tm, 2 * D), lambda i: (i, 0))],
        out_shape=[sh, sh, jax.ShapeDtypeStruct((T, NIN), BF)],
        compiler_params=pltpu.CompilerParams(dimension_semantics=("parallel",)),
    )(p, p, ya, yc, dm)


def convz_bwd(p, dz, cw, cb, dp, *, name):
    o0 = O_CV // (3 * CVB)

    def body(p_ref, dz_ref, w_ref, bias_ref, dp_in, dp_ref, dw_ref, dbias_ref):
        xv, bv, cv = p_ref[:, 0:CVB], p_ref[:, CVB:2 * CVB], p_ref[:, 2 * CVB:3 * CVB]
        ci = cv * xv
        dwc = _conv(ci, w_ref, bias_ref)
        dzv = dz_ref[...]
        ddw = dzv * bv
        dci = _conv_t(ddw, w_ref)
        dp_ref[:, 0:CVB] = (dci * cv).astype(BF)
        dp_ref[:, CVB:2 * CVB] = (dzv * dwc).astype(BF)
        dp_ref[:, 2 * CVB:3 * CVB] = (dci * xv).astype(BF)
        _conv_wgrad(dw_ref, ddw, ci)
        dbias_ref[...] = jnp.sum(ddw, axis=0, keepdims=True)

    own = lambda r: pl.BlockSpec((r, CVB), lambda j: (0, j))
    return pl.pallas_call(
        body, name=name, grid=(CONV // CVB,),
        in_specs=[pl.BlockSpec((T, 3 * CVB), lambda j: (0, o0 + j)), own(T), own(3), own(1),
                  pl.BlockSpec(memory_space=pl.ANY)],
        out_specs=[pl.BlockSpec((T, 3 * CVB), lambda j: (0, o0 + j)), own(3), own(1)],
        out_shape=[jax.ShapeDtypeStruct((T, NIN), BF), jax.ShapeDtypeStruct((3, CONV), F32),
                   jax.ShapeDtypeStruct((1, CONV), F32)],
        input_output_aliases={4: 0},
        compiler_params=pltpu.CompilerParams(dimension_semantics=("parallel",)),
    )(p, dz, cw, cb, dp)


def attn_bwd(q, k, v, do, *, name, tq=256):
    def body(q_ref, k_ref, v_ref, do_ref, dq_ref, dk_ref, dv_ref):
        h, i = pl.program_id(0), pl.program_id(1)
        qv, kv = q_ref[...], k_ref[...]
        s = lax.dot_general(qv, kv, (((1,), (1,)), ((), ())), preferred_element_type=F32) * SCALE
        m = jnp.max(s, axis=-1, keepdims=True)
        e = jnp.exp(s - m)
        pr = e * (1.0 / jnp.sum(e, axis=-1, keepdims=True))
        mask = _head_mask(h)
        vm = jnp.where(mask, v_ref[...], jnp.zeros_like(v_ref[...]))
        dom = jnp.where(mask, do_ref[...], jnp.zeros_like(do_ref[...]))
        dp = lax.dot_general(dom, vm, (((1,), (1,)), ((), ())), preferred_element_type=F32)
        ds = (pr * (dp - jnp.sum(pr * dp, axis=-1, keepdims=True)) * SCALE).astype(BF)
        dq_ref[...] = jnp.dot(ds, kv, preferred_element_type=F32)
        dk = lax.dot_general(ds, qv, (((0,), (0,)), ((), ())), preferred_element_type=F32)
        dv = lax.dot_general(pr.astype(BF), dom, (((0,), (0,)), ((), ())), preferred_element_type=F32)

        @pl.when(i == 0)
        def _():
            dk_ref[...] = dk

        @pl.when(i > 0)
        def _():
            dk_ref[...] += dk

        @pl.when((i == 0) & (h % 2 == 0))
        def _():
            dv_ref[...] = dv

        @pl.when((i > 0) | (h % 2 == 1))
        def _():
            dv_ref[...] += dv

    return pl.pallas_call(
        body, name=name, grid=(NH, T // tq),
        in_specs=[pl.BlockSpec((tq, HP), lambda h, i: (i, h)), pl.BlockSpec((TKV, HP), lambda h, i: (0, h)),
                  pl.BlockSpec((TKV, 2 * DV), lambda h, i: (0, h // 2)), pl.BlockSpec((tq, 2 * DV), lambda h, i: (i, h // 2))],
        out_specs=[pl.BlockSpec((tq, HP), lambda h, i: (i, h)), pl.BlockSpec((TKV, HP), lambda h, i: (0, h)),
                   pl.BlockSpec((TKV, 2 * DV), lambda h, i: (0, h // 2))],
        out_shape=[jax.ShapeDtypeStruct((T, NH * HP), F32), jax.ShapeDtypeStruct((TKV, NH * HP), F32),
                   jax.ShapeDtypeStruct((TKV, NH * DV), F32)],
        compiler_params=pltpu.CompilerParams(dimension_semantics=("arbitrary", "arbitrary")),
    )(q, k, v, do)


def qprep_bwd(p, dq, qg, wq2, cq_t, sq_t, dp, *, name, tm=256):
    qcol = O_Q // 512

    def body(p_ref, dq_ref, g_ref, w_ref, c_ref, s_ref, dp_in, dp_ref, dq2_ref, dg_ref):
        i = pl.program_id(0)
        dqv = dq_ref[...]
        cc = jnp.concatenate([c_ref[...]] * NH, axis=1)
        ss = jnp.concatenate([s_ref[...]] * NH, axis=1)
        dq2 = jnp.concatenate([dqv * cc, dqv * ss], axis=1).astype(BF)
        dq2_ref[...] = dq2
        dcq = lax.dot_general(dq2, w_ref[...], (((1,), (1,)), ((), ())), preferred_element_type=F32)
        pq = p_ref[...]
        r = lax.rsqrt(jnp.sum(pq * pq, axis=-1, keepdims=True) * (1.0 / QL) + EPS)
        xh = pq * r
        a = dcq * g_ref[...]
        dp_ref[...] = (r * (a - xh * (jnp.sum(a * xh, axis=-1, keepdims=True) * (1.0 / QL)))).astype(BF)
        dg = jnp.sum(dcq * xh, axis=0, keepdims=True)

        @pl.when(i == 0)
        def _():
            dg_ref[...] = dg

        @pl.when(i > 0)
        def _():
            dg_ref[...] += dg

    return pl.pallas_call(
        body, name=name, grid=(T // tm,),
        in_specs=[pl.BlockSpec((tm, 512), lambda i: (i, qcol)), pl.BlockSpec((tm, NH * HP), lambda i: (i, 0)), _row(512),
                  pl.BlockSpec((512, 2 * NH * HP), lambda i: (0, 0)),
                  pl.BlockSpec((tm, HP), lambda i: (i, 0)), pl.BlockSpec((tm, HP), lambda i: (i, 0)),
                  pl.BlockSpec(memory_space=pl.ANY)],
        out_specs=[pl.BlockSpec((tm, 512), lambda i: (i, qcol)), pl.BlockSpec((tm, 2 * NH * HP), lambda i: (i, 0)), _row(512)],
        out_shape=[jax.ShapeDtypeStruct((T, NIN), BF), jax.ShapeDtypeStruct((T, 2 * NH * HP), BF),
                   jax.ShapeDtypeStruct((1, 512), F32)],
        input_output_aliases={6: 0},
        compiler_params=pltpu.CompilerParams(dimension_semantics=("arbitrary",)),
    )(p, dq, qg, wq2, cq_t, sq_t, dp)


def kvprep_bwd(pc, p, dk, dv, kvg, wkv2, ck, sk, dp, *, name, tm=256):
    assert tm == TC
    nb = TKV // tm
    kvcol = O_KV // 512

    def body(pc_ref, p_ref, dk_ref, dv_ref, g_ref, w_ref, ck_ref, sk_ref, dp_in, dp_ref, dpc_ref, dkv2_ref, dg_ref):
        i = pl.program_id(0)
        t = jnp.where(i == NLAT, pc_ref[...], p_ref[...])
        pk = t[:, :KVL]
        r = lax.rsqrt(jnp.mean(pk * pk, axis=-1, keepdims=True) + EPS)
        xh = pk * r
        dkv = dk_ref[...]
        dkv2 = jnp.concatenate([dkv, dv_ref[...]], axis=1).astype(BF)
        dkv2_ref[...] = dkv2
        dckv = lax.dot_general(dkv2, w_ref[...], (((1,), (1,)), ((), ())), preferred_element_type=F32)
        a = dckv * g_ref[...]
        dpk = r * (a - xh * jnp.mean(a * xh, axis=-1, keepdims=True))
        dkr = dkv[:, 0:HP]
        for hh in range(1, NH):
            dkr = dkr + dkv[:, hh * HP:(hh + 1) * HP]
        res = jnp.concatenate([dpk, dkr * ck_ref[...], dkr * sk_ref[...]], axis=1).astype(BF)
        dg = jnp.sum(dckv * xh, axis=0, keepdims=True)

        @pl.when(i == 0)
        def _():
            dg_ref[...] = dg

        @pl.when(i > 0)
        def _():
            dg_ref[...] += dg

        @pl.when(i < NLAT)
        def _():
            dp_ref[...] = res

        @pl.when(i == NLAT)
        def _():
            dpc_ref[...] = res

    rb = lambda w: pl.BlockSpec((tm, w), lambda i: (i, 0))
    return pl.pallas_call(
        body, name=name, grid=(nb,),
        in_specs=[pl.BlockSpec((tm, 512), lambda i: (0, 0)),
                  pl.BlockSpec((tm, 512), lambda i: (jnp.minimum(i, NLAT - 1), kvcol)),
                  rb(NH * HP), rb(NH * DV), _row(KVL), pl.BlockSpec((KVL, NH * HP + NH * DV), lambda i: (0, 0)),
                  rb(HP), rb(HP), pl.BlockSpec(memory_space=pl.ANY)],
        out_specs=[pl.BlockSpec((tm, 512), lambda i: (jnp.minimum(i, NLAT - 1), kvcol)),
                   pl.BlockSpec((tm, 512), lambda i: (0, 0)), rb(NH * HP + NH * DV), _row(KVL)],
        out_shape=[jax.ShapeDtypeStruct((T, NIN), BF), jax.ShapeDtypeStruct((TC, 512), BF),
                   jax.ShapeDtypeStruct((TKV, NH * HP + NH * DV), BF), jax.ShapeDtypeStruct((1, KVL), F32)],
        input_output_aliases={8: 0},
        compiler_params=pltpu.CompilerParams(dimension_semantics=("arbitrary",)),
    )(pc, p, dk, dv, kvg, wkv2, ck, sk, dp)


def _pieces(src, width, n):
    out, c = [], src
    while c < src + width:
        k = c // n
        w = min(src + width, (k + 1) * n) - c
        out.append((k, c - k * n, c - src, w))
        c += w
    return out


def _win_moves():
    mv = [(2208, 1024, O_GA), (3232, 1024, O_GC), (0, KVL, O_KV), (256, DR, O_KV + KVL + DN), (288, QL, O_Q)]
    mv += [(256 + _swap_start(g), 8, O_KV + KVL + HP + DN + 8 * g) for g in range(4)]
    for j in range(CONV // CVB):
        base = O_CV + 3 * CVB * j
        mv += [(672 + CVB * j, CVB, base), (1184 + CVB * j, CVB, base + CVB), (1696 + CVB * j, CVB, base + 2 * CVB)]
    return mv


_WIN_ZERO = [(O_KV + KVL, DN), (O_KV + KVL + DN + DR, HP - DN - DR), (O_KV + KVL + HP, DN),
             (O_KV + KVL + HP + DN + DR, HP - DN - DR), (O_Q + QL, 512 - QL)]


def build_win(g, *, name, tm=256):
    def body(g_ref, o_ref):
        for src, w, dst in _win_moves():
            for k, a, off, pw in _pieces(src, w, SH_IN):
                o_ref[:, dst + off:dst + off + pw] = g_ref[k, :, a:a + pw]
        for c0, w in _WIN_ZERO:
            o_ref[:, c0:c0 + w] = jnp.zeros((tm, w), o_ref.dtype)

    return pl.pallas_call(
        body, name=name, grid=(D // tm,), in_specs=[pl.BlockSpec((NDEV, tm, SH_IN), lambda i: (0, i, 0))],
        out_specs=pl.BlockSpec((tm, NIN), lambda i: (i, 0)), out_shape=jax.ShapeDtypeStruct((D, NIN), g.dtype),
        compiler_params=pltpu.CompilerParams(dimension_semantics=("parallel",)),
    )(g)


def shard_win_grad(dw, dwc, *, name, tm=256):
    def body(dw_ref, dwc_ref, o_ref, kvs):
        kvs[...] = dw_ref[:, O_KV:O_KV + 512] + dwc_ref[...]

        def src(col, w):
            if O_KV <= col < O_KV + 512:
                return kvs[:, col - O_KV:col - O_KV + w]
            return dw_ref[:, col:col + w]

        for s, w, dst in _win_moves():
            if w == 8 or s == 256:
                continue
            for k, a, off, pw in _pieces(s, w, SH_IN):
                o_ref[k, :, a:a + pw] = src(dst + off, pw).astype(o_ref.dtype)
        for g in range(4):
            val = src(O_KV + KVL + DN + 8 * g, 8) + src(O_KV + KVL + HP + DN + _swap_start(g), 8)
            o_ref[0, :, 256 + 8 * g:256 + 8 * g + 8] = val.astype(o_ref.dtype)

    return pl.pallas_call(
        body, name=name, grid=(D // tm,),
        in_specs=[pl.BlockSpec((tm, NIN), lambda i: (i, 0)), pl.BlockSpec((tm, 512), lambda i: (i, 0))],
        out_specs=pl.BlockSpec((NDEV, tm, SH_IN), lambda i: (0, i, 0)),
        out_shape=jax.ShapeDtypeStruct((NDEV, D, SH_IN), BF),
        scratch_shapes=[pltpu.VMEM((tm, 512), F32)],
        compiler_params=pltpu.CompilerParams(dimension_semantics=("parallel",)),
    )(dw, dwc)


def build_wq_wkv(gq, gkv, *, name):
    def body(gq_ref, gkv_ref, q_ref, kv_ref):
        q_ref[...] = jnp.zeros_like(q_ref)
        kv_ref[...] = jnp.zeros_like(kv_ref)
        for h in range(NH):
            q_ref[0:QL, h * HP:h * HP + DN + DR] = gq_ref[h]
            for g in range(4):
                c0 = NH * HP + h * HP + DN + 8 * g
                q_ref[0:QL, c0:c0 + 8] = gq_ref[h, :, DN + _swap_start(g):DN + _swap_start(g) + 8]
            kv_ref[:, h * HP:h * HP + DN] = gkv_ref[h, :, 0:DN]
            kv_ref[:, NH * HP + h * DV:NH * HP + (h + 1) * DV] = gkv_ref[h, :, DN:DN + DV]

    vm = pl.BlockSpec(memory_space=pltpu.VMEM)
    return pl.pallas_call(
        body, name=name, in_specs=[vm, vm], out_specs=[vm, vm],
        out_shape=[jax.ShapeDtypeStruct((512, 2 * NH * HP), gq.dtype), jax.ShapeDtypeStruct((KVL, NH * HP + NH * DV), gq.dtype)],
    )(gq, gkv)


def shard_wq_wkv_grad(dwq2, dwkv2, *, name):
    def body(q_ref, kv_ref, gq_ref, gkv_ref):
        for h in range(NH):
            gq_ref[h, :, 0:DN] = q_ref[0:QL, h * HP:h * HP + DN].astype(BF)
            for g in range(4):
                a = q_ref[0:QL, h * HP + DN + 8 * g:h * HP + DN + 8 * g + 8]
                c0 = NH * HP + h * HP + DN + _swap_start(g)
                gq_ref[h, :, DN + 8 * g:DN + 8 * g + 8] = (a + q_ref[0:QL, c0:c0 + 8]).astype(BF)
            gkv_ref[h, :, 0:DN] = kv_ref[:, h * HP:h * HP + DN].astype(BF)
            gkv_ref[h, :, DN:DN + DV] = kv_ref[:, NH * HP + h * DV:NH * HP + (h + 1) * DV].astype(BF)

    vm = pl.BlockSpec(memory_space=pltpu.VMEM)
    return pl.pallas_call(
        body, name=name, in_specs=[vm, vm], out_specs=[vm, vm],
        out_shape=[jax.ShapeDtypeStruct((NDEV, QL, (DN + DR)), BF), jax.ShapeDtypeStruct((NDEV, KVL, DN + DV), BF)],
    )(dwq2, dwkv2)


def unshard_cols(g, *, name, tm=256):
    _, K, n = g.shape
    tm = _pick(K, tm, 16)

    def body(g_ref, o_ref):
        for k in range(NDEV):
            o_ref[:, k * n:(k + 1) * n] = g_ref[k]

    return pl.pallas_call(
        body, name=name, grid=(K // tm,), in_specs=[pl.BlockSpec((NDEV, tm, n), lambda i: (0, i, 0))],
        out_specs=pl.BlockSpec((tm, NDEV * n), lambda i: (i, 0)), out_shape=jax.ShapeDtypeStruct((K, NDEV * n), g.dtype),
        compiler_params=pltpu.CompilerParams(dimension_semantics=("parallel",)),
    )(g)


def shard_cols(w, *, name, tm=256):
    K, n8 = w.shape
    n = n8 // NDEV
    tm = _pick(K, tm, 16)

    def body(w_ref, o_ref):
        for k in range(NDEV):
            o_ref[k] = w_ref[:, k * n:(k + 1) * n]

    return pl.pallas_call(
        body, name=name, grid=(K // tm,), in_specs=[pl.BlockSpec((tm, n8), lambda i: (i, 0))],
        out_specs=pl.BlockSpec((NDEV, tm, n), lambda i: (0, i, 0)), out_shape=jax.ShapeDtypeStruct((NDEV, K, n), w.dtype),
        compiler_params=pltpu.CompilerParams(dimension_semantics=("parallel",)),
    )(w)


def _rope_tables():
    t = np.arange(T)
    row = (t // GRID_W).astype(np.float32)
    col = (t % GRID_W).astype(np.float32)
    axis_dim = DR // 2
    inv = (np.float32(ROPE_THETA) ** (-np.arange(0, axis_dim, 2, dtype=np.float32) / np.float32(axis_dim))).astype(np.float32)
    ar, ac = (row[:, None] * inv).astype(np.float32), (col[:, None] * inv).astype(np.float32)
    cosv = np.concatenate([np.cos(ar), np.cos(ar), np.cos(ac), np.cos(ac)], axis=1).astype(np.float32)
    sinv = np.concatenate([-np.sin(ar), np.sin(ar), -np.sin(ac), np.sin(ac)], axis=1).astype(np.float32)
    ck = np.zeros((TKV, HP), np.float32)
    sk = np.zeros((TKV, HP), np.float32)
    ck[T:, DN:DN + DR] = 1.0
    ck[:T, DN:DN + DR] = cosv
    sk[:T, DN:DN + DR] = sinv
    cq = np.zeros((T, HP), np.float32)
    cq[:, :DN] = 1.0
    cq[:, DN:DN + DR] = cosv
    return jnp.asarray(ck), jnp.asarray(sk), jnp.asarray(cq), jnp.asarray(sk[:T])


def _local_step(x, ctx, tgt, mod_lat, mod_ctx, n1g, qg, kvg, n2g, fg, conv_w, conv_b, ffn_w, ffn_b,
                win, wq2, wkv2, wao, wco, wo, wup, wdn):
    sh1, sc1, g1, sh2, sc2, g2 = [mod_lat[:, i * D:(i + 1) * D] for i in range(6)]
    csh1, csc1 = mod_ctx[:, 0:D], mod_ctx[:, D:2 * D]
    ck, sk, cq_t, sq_t = _rope_tables()
    qg_p = jnp.pad(qg, ((0, 0), (0, 512 - QL)))

    hcat = normmod_cat(ctx, x, n1g, csc1, csh1, sc1, sh1, name="normmod1")
    p = mm(hcat, win, M=T, name="in_proj")
    pc = mm(hcat, win, M=TC, N=512, a_off=(T, 0), b_off=(0, O_KV), name="in_proj_ctx")
    kh, vh, ckv = kvprep(pc, p, kvg, wkv2, ck, sk, name="kvprep")
    qr, cq = qprep(p, qg_p, wq2, cq_t, sq_t, name="qprep")
    o = attn_fwd(qr, kh, vh, name="attn_fwd")
    z = convz(p, conv_w, conv_b, name="convz")
    ya = mm(o, wao, name="attn_out")
    yc = mm(z, wco, name="conv_out")
    merged = gate_merge(p, ya, yc, name="gate_merge")
    a_out = mm(merged, wo, name="o_proj")
    x1, h2 = resid_normmod(x, a_out, g1, n2g, sc2, sh2, name="resid_normmod2")
    u0 = mm(h2, wup, o_stack=True, name="up_proj")
    f = ffn_act(u0, ffn_w, ffn_b, name="ffn_act")
    dn = mm(f, wdn, name="down_proj")
    dx2, dd, dfg, loss = final_loss(x1, dn, g2, fg, tgt, name="final_loss")

    df = mm(dd, wdn, tb=True, name="down_proj_dx")
    dwdn = mm(f, dd, ta=True, out_dtype=BF, name="down_proj_dw")
    du0, dffn_w, dffn_b = ffn_act_bwd(u0, df, ffn_w, ffn_b, name="ffn_act_bwd")
    dh2 = mm(du0, wup, tb=True, a_stack=True, name="up_proj_dx")
    dwup = mm(h2, du0, ta=True, b_stack=True, out_dtype=BF, name="up_proj_dw")
    dx1, da, st2 = normmod_bwd(x1, dh2, n2g, sc2, dx2, dn, g1, name="normmod2_bwd")

    dmerged = mm(da, wo, tb=True, name="o_proj_dx")
    dwo = mm(merged, da, ta=True, out_dtype=BF, name="o_proj_dw")
    dya, dyc, dp = gate_merge_bwd(p, ya, yc, dmerged, name="gate_merge_bwd")
    do = mm(dya, wao, tb=True, out_dtype=BF, name="attn_out_dx")
    dwao = mm(o, dya, ta=True, out_dtype=BF, name="attn_out_dw")
    dz = mm(dyc, wco, tb=True, name="conv_out_dx")
    dwco = mm(z, dyc, ta=True, out_dtype=BF, name="conv_out_dw")
    dp, dconv_w, dconv_b = convz_bwd(p, dz, conv_w, conv_b, dp, name="convz_bwd")
    dq, dk, dv = attn_bwd(qr, kh, vh, do, name="attn_bwd")
    dp, dq2, dqg = qprep_bwd(p, dq, qg_p, wq2, cq_t, sq_t, dp, name="qprep_bwd")
    dwq2 = mm(cq, dq2, ta=True, name="q_up_dw")
    dp, dpc, dkv2, dkvg = kvprep_bwd(pc, p, dk, dv, kvg, wkv2, ck, sk, dp, name="kvprep_bwd")
    dwkv2 = mm(ckv, dkv2, ta=True, name="kv_up_dw")

    dh = mm(dp, win, tb=True, name="in_proj_dx")
    dhc = mm(dpc, win, tb=True, N=D, K=512, b_off=(0, O_KV), name="in_proj_ctx_dx")
    dwin = mm(hcat, dp, ta=True, K=T, name="in_proj_dw")
    dwin_c = mm(hcat, dpc, ta=True, K=TC, a_off=(T, 0), name="in_proj_ctx_dw")
    dx, _, st1 = normmod_bwd(x, dh, n1g, sc1, dx1, a_out, g1, name="normmod1_bwd")
    stc = normmod_bwd(ctx, dhc, n1g, csc1, None, None, None, name="normmod1_ctx_bwd")

    zrow = jnp.zeros((1, D), F32)
    dmod_lat = jnp.concatenate([st1[0:1], st1[1:2], st1[3:4], st2[0:1], st2[1:2], st2[3:4]], axis=1)
    dmod_ctx = jnp.concatenate([stc[0:1], stc[1:2], zrow, zrow, zrow, zrow], axis=1)
    return dict(
        loss=loss, dx=dx, dmod_lat=dmod_lat, dmod_ctx=dmod_ctx,
        dn1g=st1[2:3] + stc[2:3], dqg=dqg, dkvg=dkvg, dn2g=st2[2:3], dfg=dfg,
        dconv_w=dconv_w, dconv_b=dconv_b, dffn_w=dffn_w, dffn_b=dffn_b,
        dwin=dwin, dwin_c=dwin_c, dwq2=dwq2, dwkv2=dwkv2, dwao=dwao, dwco=dwco, dwo=dwo, dwup=dwup, dwdn=dwdn)


def _me():
    x, y, c = lax.axis_index("x"), lax.axis_index("y"), lax.axis_index("c")
    return x, y, c, 4 * x + 2 * y + c


def _peer(x, y, c, k):
    px = 1 - x if k & 4 else x
    py = 1 - y if k & 2 else y
    pc = 1 - c if k & 1 else c
    return (px, py, pc), 4 * px + 2 * py + pc


def _exchange_tiles(src_of_peer, buf, send_sem, recv_sem):
    x, y, c, me = _me()
    for k in range(1, NDEV):
        dev, lin = _peer(x, y, c, k)
        pltpu.make_async_remote_copy(src_ref=src_of_peer(lin), dst_ref=buf.at[me], send_sem=send_sem, recv_sem=recv_sem,
                                     device_id=dev, device_id_type=MESH).start()
    seven = buf.at[pl.ds(0, NDEV - 1)]
    pltpu.make_async_remote_copy(src_ref=seven, dst_ref=seven, send_sem=send_sem, recv_sem=recv_sem,
                                 device_id=(x, y, c), device_id_type=MESH).wait()


def _silu(z):
    return z * jax.nn.sigmoid(z)


def ada_fwd(c, c_ctx, ffn_w, conv_w, w_shard, b_shard, *, name):
    nsh = w_shard.shape[1]

    def body(c_ref, cc_ref, fw_ref, cw_ref, w_ref, b_ref, s_ref, m_ref, mine, res, sems):
        x, y, c, me = _me()
        mine[0:1, :] = _silu(c_ref[...])
        mine[1:2, :] = _silu(cc_ref[...])
        mine[2:5, :] = fw_ref[...]
        mine[5:8, :] = cw_ref[...]
        s_ref[me] = mine[...]
        _exchange_tiles(lambda lin: mine, s_ref, sems.at[0], sems.at[1])
        sall = s_ref[...].reshape(NDEV * 8, D).astype(BF)
        r = jnp.dot(sall, w_ref[...].astype(BF), preferred_element_type=F32) + b_ref[...]
        res[...] = r.reshape(NDEV, 8, nsh)
        m_ref[me] = res[me]
        _exchange_tiles(lambda lin: res.at[lin], m_ref, sems.at[2], sems.at[3])

    vm = pl.BlockSpec(memory_space=pltpu.VMEM)
    return pl.pallas_call(
        body, name=name, in_specs=[vm] * 6, out_specs=[vm, vm],
        out_shape=[jax.ShapeDtypeStruct((NDEV, 8, D), F32), jax.ShapeDtypeStruct((NDEV, 8, nsh), F32)],
        scratch_shapes=[pltpu.VMEM((8, D), F32), pltpu.VMEM((NDEV, 8, nsh), F32), pltpu.SemaphoreType.DMA((4,))],
    )(c, c_ctx, ffn_w, conv_w, w_shard, b_shard)


P_DML, P_DMC, P_N1, P_QG, P_KVG, P_CB, P_N2, P_FB, P_FG, P_CW, P_FW, P_LOSS, P_ROWS = 0, 6, 12, 13, 14, 15, 16, 17, 23, 24, 27, 45, 48
FROWS = 3


def sync_small(r, *, name):
    ins = [r["dmod_lat"], r["dmod_ctx"], r["dn1g"], r["dqg"], r["dkvg"], r["dconv_b"], r["dn2g"], r["dffn_b"], r["dfg"],
           r["dconv_w"], r["dffn_w"], r["loss"]]

    def put_wide(p, row0, row, n):
        for j in range(-(-n // D)):
            w = min(D, n - j * D)
            p[row0 + j:row0 + j + 1, 0:w] = row[:, j * D:j * D + w]

    def body(dml, dmc, n1, qg, kvg, cb, n2, fb, fg, cw, fw, loss, a_ref, sum_ref, p, sems):
        x, y, c, me = _me()
        p[...] = jnp.zeros_like(p)
        put_wide(p, P_DML, dml, 6 * D)
        put_wide(p, P_DMC, dmc, 6 * D)
        put_wide(p, P_N1, n1, D)
        put_wide(p, P_QG, qg, 512)
        put_wide(p, P_KVG, kvg, KVL)
        put_wide(p, P_CB, cb, CONV)
        put_wide(p, P_N2, n2, D)
        put_wide(p, P_FG, fg, D)
        put_wide(p, P_LOSS, loss, 128)
        for s in range(2):
            put_wide(p, P_FB + FROWS * s, fb.at[s], DFF)
        for k in range(3):
            put_wide(p, P_CW + k, cw.at[k:k + 1], CONV)
            for s in range(2):
                put_wide(p, P_FW + FROWS * (2 * k + s), fw.at[s, k:k + 1], DFF)
        a_ref[me] = p[...]
        _exchange_tiles(lambda lin: p, a_ref, sems.at[0], sems.at[1])
        acc = a_ref[0]
        for k in range(1, NDEV):
            acc = acc + a_ref[k]
        sum_ref[...] = acc

    vm = pl.BlockSpec(memory_space=pltpu.VMEM)
    return pl.pallas_call(
        body, name=name, in_specs=[vm] * len(ins), out_specs=[vm, vm],
        out_shape=[jax.ShapeDtypeStruct((NDEV, P_ROWS, D), F32), jax.ShapeDtypeStruct((P_ROWS, D), F32)],
        scratch_shapes=[pltpu.VMEM((P_ROWS, D), F32), pltpu.SemaphoreType.DMA((2,))],
    )(*ins)


def ada_bwd(s_all, dml, dmc, w_shard, c_ctx, *, name):
    nsh = w_shard.shape[1]

    def body(s_ref, dml_ref, dmc_ref, w_ref, c_ref, dw_ref, gc_ref, s16, dm16, part, buf, sems):
        x, y, c, me = _me()
        s16[...] = jnp.zeros_like(s16)
        dm16[...] = jnp.zeros_like(dm16)
        for k in range(NDEV):
            s16[k:k + 1, :] = s_ref[k, 0:1, :]
        s16[8:9, :] = s_ref[0, 1:2, :]
        dm16[0:8, :] = dml_ref[...]
        dm16[8:9, :] = dmc_ref[...]
        dw_ref[...] = lax.dot_general(s16[...].astype(BF), dm16[...].astype(BF), (((0,), (0,)), ((), ())),
                                      preferred_element_type=F32)
        part[...] = lax.dot_general(dm16[8:16, :].astype(BF), w_ref[...].astype(BF), (((1,), (1,)), ((), ())),
                                    preferred_element_type=F32)
        buf[me] = part[...]
        _exchange_tiles(lambda lin: part, buf, sems.at[0], sems.at[1])
        acc = buf[0]
        for k in range(1, NDEV):
            acc = acc + buf[k]
        z = c_ref[...]
        sg = jax.nn.sigmoid(z)
        gc_ref[...] = acc * (sg * (1.0 + z * (1.0 - sg)))

    vm = pl.BlockSpec(memory_space=pltpu.VMEM)
    return pl.pallas_call(
        body, name=name, in_specs=[vm] * 5, out_specs=[vm, vm],
        out_shape=[jax.ShapeDtypeStruct((D, nsh), F32), jax.ShapeDtypeStruct((8, D), F32)],
        scratch_shapes=[pltpu.VMEM((16, D), F32), pltpu.VMEM((16, nsh), F32), pltpu.VMEM((8, D), F32),
                        pltpu.VMEM((NDEV, 8, D), F32), pltpu.SemaphoreType.DMA((2,))],
    )(s_all, dml, dmc, w_shard, c_ctx)


def all_gather_shards(shards, *, name):
    nt = len(shards)

    def body(*refs):
        ins, outs = refs[:nt], refs[nt:2 * nt]
        send_sems, recv_sems, local_sems = refs[2 * nt:]
        x, y, c, me = _me()
        sibling = (x, y, 1 - c)
        chips = [(1 - x, y), (x, 1 - y), (1 - x, 1 - y)]

        def copy(t, k, block, to, src=None):
            slot = outs[t].at[4 * block[0] + 2 * block[1] + block[2]]
            return pltpu.make_async_remote_copy(
                src_ref=slot if src is None else src, dst_ref=slot,
                send_sem=send_sems.at[t, k], recv_sem=recv_sems.at[t, k], device_id=to, device_id_type=MESH)

        mine = [pltpu.make_async_copy(ins[t], outs[t].at[me], local_sems.at[t]) for t in range(nt)]
        for cp in mine:
            cp.start()
        first = []
        for t in range(nt):
            first.append(copy(t, 0, (x, y, c), sibling, src=ins[t]))
            first += [copy(t, 1 + j, (x, y, c), (*chip, c), src=ins[t]) for j, chip in enumerate(chips)]
        for cp in first:
            cp.start()
        passed = []
        for t in range(nt):
            for j, chip in enumerate(chips):
                copy(t, 1 + j, (*chip, c), (x, y, c)).wait_recv()
                cp = copy(t, 4 + j, (*chip, c), sibling)
                cp.start()
                passed.append(cp)
        for t in range(nt):
            copy(t, 0, sibling, (x, y, c)).wait_recv()
            for j, chip in enumerate(chips):
                copy(t, 4 + j, (*chip, 1 - c), (x, y, c)).wait_recv()
        for cp in first + passed:
            cp.wait_send()
        for cp in mine:
            cp.wait()

    hbm = pl.BlockSpec(memory_space=pl.ANY)
    return pl.pallas_call(
        body, name=name, in_specs=[hbm] * nt, out_specs=[hbm] * nt,
        out_shape=[jax.ShapeDtypeStruct((NDEV,) + s.shape, s.dtype) for s in shards],
        scratch_shapes=[pltpu.SemaphoreType.DMA((nt, 7)), pltpu.SemaphoreType.DMA((nt, 7)), pltpu.SemaphoreType.DMA((nt,))],
    )(*shards)


def reduce_scatter_send(parts, *, name):
    nt = len(parts)

    def body(*refs):
        ins, outs = refs[:nt], refs[nt:2 * nt]
        send_sems, recv_sems, local_sems = refs[2 * nt:]
        x, y, c, me = _me()
        mine = [pltpu.make_async_copy(ins[t].at[me], outs[t].at[me], local_sems.at[t]) for t in range(nt)]
        for cp in mine:
            cp.start()
        cps = []
        for t in range(nt):
            for k in range(1, NDEV):
                dev, lin = _peer(x, y, c, k)
                cp = pltpu.make_async_remote_copy(src_ref=ins[t].at[lin], dst_ref=outs[t].at[me],
                                                  send_sem=send_sems.at[t, k - 1], recv_sem=recv_sems.at[t, k - 1],
                                                  device_id=dev, device_id_type=MESH)
                cp.start()
                cps.append(cp)
        for t in range(nt):
            for k in range(1, NDEV):
                dev, lin = _peer(x, y, c, k)
                pltpu.make_async_remote_copy(src_ref=ins[t].at[me], dst_ref=outs[t].at[lin],
                                             send_sem=send_sems.at[t, k - 1], recv_sem=recv_sems.at[t, k - 1],
                                             device_id=dev, device_id_type=MESH).wait_recv()
        for cp in cps:
            cp.wait_send()
        for cp in mine:
            cp.wait()

    hbm = pl.BlockSpec(memory_space=pl.ANY)
    return pl.pallas_call(
        body, name=name, in_specs=[hbm] * nt, out_specs=[hbm] * nt,
        out_shape=[jax.ShapeDtypeStruct(a.shape, a.dtype) for a in parts],
        scratch_shapes=[pltpu.SemaphoreType.DMA((nt, 7)), pltpu.SemaphoreType.DMA((nt, 7)), pltpu.SemaphoreType.DMA((nt,))],
    )(*parts)


def _adamw_math(w, g, m, v):
    nm = B1 * m + (1.0 - B1) * g
    nv = B2 * v + (1.0 - B2) * (g * g)
    m_hat = nm / (1.0 - B1 ** STEP)
    v_hat = nv / (1.0 - B2 ** STEP)
    return -LR * (m_hat / (jnp.sqrt(v_hat) + AEPS) + WD * w), nm, nv


def adamw_many(ws, gs, ms, vs, *, name):
    n = len(ws)

    def body(*refs):
        for k in range(n):
            d, nm, nv = _adamw_math(refs[k][...], refs[n + k][...], refs[2 * n + k][...], refs[3 * n + k][...])
            refs[4 * n + k][...] = d
            refs[5 * n + k][...] = nm
            refs[6 * n + k][...] = nv

    vm = pl.BlockSpec(memory_space=pltpu.VMEM)
    sh = [jax.ShapeDtypeStruct(w.shape, F32) for w in ws]
    outs = pl.pallas_call(body, name=name, in_specs=[vm] * (4 * n), out_specs=[vm] * (3 * n), out_shape=sh * 3,
                          )(*ws, *gs, *ms, *vs)
    return outs[:n], outs[n:2 * n], outs[2 * n:]


def adamw(w, g, m, v, *, name, tr=256):
    R, C = w.shape
    tr = _pick(R, tr, 8)

    def body(w_ref, g_ref, m_ref, v_ref, d_ref, nm_ref, nv_ref):
        d_ref[...], nm_ref[...], nv_ref[...] = _adamw_math(w_ref[...], g_ref[...], m_ref[...], v_ref[...])

    blk = pl.BlockSpec((tr, C), lambda i: (i, 0))
    sh = jax.ShapeDtypeStruct((R, C), F32)
    return pl.pallas_call(
        body, name=name, grid=(R // tr,), in_specs=[blk, blk, blk, blk], out_specs=[blk, blk, blk],
        out_shape=[sh, sh, sh], compiler_params=pltpu.CompilerParams(dimension_semantics=("parallel",)),
    )(w, g, m, v)


def adamw_slots(w, slots, m, v, *, name, tr=256):
    R, C = w.shape
    tr = _pick(R, tr, 16)

    def body(w_ref, s_ref, m_ref, v_ref, g_ref, d_ref, nm_ref, nv_ref):
        g = s_ref[0].astype(F32)
        for k in range(1, NDEV):
            g = g + s_ref[k].astype(F32)
        g_ref[...] = g
        d_ref[...], nm_ref[...], nv_ref[...] = _adamw_math(w_ref[...], g, m_ref[...], v_ref[...])

    blk = pl.BlockSpec((tr, C), lambda i: (i, 0))
    sh = jax.ShapeDtypeStruct((R, C), F32)
    return pl.pallas_call(
        body, name=name, grid=(R // tr,), in_specs=[blk, pl.BlockSpec((NDEV, tr, C), lambda i: (0, i, 0)), blk, blk],
        out_specs=[blk, blk, blk, blk], out_shape=[sh, sh, sh, sh],
        compiler_params=pltpu.CompilerParams(dimension_semantics=("parallel",)),
    )(w, slots, m, v)


def _padc(a, n=D):
    return jnp.pad(a, ((0, 0), (0, n - a.shape[1])))


_BIG = ["w_in", "w_uq", "w_ukv", "w_attn_out", "w_conv_out", "w_o", "w_up", "w_down"]


def kernel(x, c, ctx, c_ctx, w_ada, b_ada, norm1_g, w_in, q_norm_g, kv_norm_g, w_uq, w_ukv, conv_w, conv_b, w_attn_out, w_conv_out, w_o, norm2_g, w_up, ffn_conv_w, ffn_conv_b, w_down, final_g, loss_target, m_c_ctx, m_w_ada, m_b_ada, m_norm1_g, m_w_in, m_q_norm_g, m_kv_norm_g, m_w_uq, m_w_ukv, m_conv_w, m_conv_b, m_w_attn_out, m_w_conv_out, m_w_o, m_norm2_g, m_w_up, m_ffn_conv_w, m_ffn_conv_b, m_w_down, m_final_g, v_c_ctx, v_w_ada, v_b_ada, v_norm1_g, v_w_in, v_q_norm_g, v_kv_norm_g, v_w_uq, v_w_ukv, v_conv_w, v_conv_b, v_w_attn_out, v_w_conv_out, v_w_o, v_norm2_g, v_w_up, v_ffn_conv_w, v_ffn_conv_b, v_w_down, v_final_g):
    me = 4 * lax.axis_index("x") + 2 * lax.axis_index("y") + lax.axis_index("c")
    W = dict(c_ctx=c_ctx, w_ada=w_ada, b_ada=b_ada, norm1_g=norm1_g, w_in=w_in, q_norm_g=q_norm_g, kv_norm_g=kv_norm_g,
             w_uq=w_uq, w_ukv=w_ukv, conv_w=conv_w, conv_b=conv_b, w_attn_out=w_attn_out, w_conv_out=w_conv_out, w_o=w_o,
             norm2_g=norm2_g, w_up=w_up, ffn_conv_w=ffn_conv_w, ffn_conv_b=ffn_conv_b, w_down=w_down, final_g=final_g)
    M = dict(c_ctx=m_c_ctx, w_ada=m_w_ada, b_ada=m_b_ada, norm1_g=m_norm1_g, w_in=m_w_in, q_norm_g=m_q_norm_g,
             kv_norm_g=m_kv_norm_g, w_uq=m_w_uq, w_ukv=m_w_ukv, conv_w=m_conv_w, conv_b=m_conv_b, w_attn_out=m_w_attn_out,
             w_conv_out=m_w_conv_out, w_o=m_w_o, norm2_g=m_norm2_g, w_up=m_w_up, ffn_conv_w=m_ffn_conv_w,
             ffn_conv_b=m_ffn_conv_b, w_down=m_w_down, final_g=m_final_g)
    V = dict(c_ctx=v_c_ctx, w_ada=v_w_ada, b_ada=v_b_ada, norm1_g=v_norm1_g, w_in=v_w_in, q_norm_g=v_q_norm_g,
             kv_norm_g=v_kv_norm_g, w_uq=v_w_uq, w_ukv=v_w_ukv, conv_w=v_conv_w, conv_b=v_conv_b, w_attn_out=v_w_attn_out,
             w_conv_out=v_w_conv_out, w_o=v_w_o, norm2_g=v_norm2_g, w_up=v_w_up, ffn_conv_w=v_ffn_conv_w,
             ffn_conv_b=v_ffn_conv_b, w_down=v_w_down, final_g=v_final_g)
    names = list(W)
    as2d = lambda a: a.reshape(1, -1) if a.ndim == 1 else a.reshape(a.shape[-2], a.shape[-1])
    W2 = {k: as2d(a) for k, a in W.items()}
    M2 = {k: as2d(a) for k, a in M.items()}
    V2 = {k: as2d(a) for k, a in V.items()}
    nsh = W2["w_ada"].shape[1]

    b_sh = lax.dynamic_slice(W2["b_ada"], (0, me * nsh), (1, nsh))
    s_all, m_all = ada_fwd(c, W2["c_ctx"], _padc(W2["ffn_conv_w"]), _padc(W2["conv_w"]), W2["w_ada"], b_sh, name="ada_fwd")
    mod_lat = m_all[:, 0, :].reshape(1, 6 * D)
    mod_ctx = m_all[:, 1, :].reshape(1, 6 * D)
    ffn_w_full = s_all[:, 2:5, :2 * DFF // NDEV].transpose(1, 0, 2).reshape(3, 2 * DFF)
    conv_w_full = s_all[:, 5:8, :CONV // NDEV].transpose(1, 0, 2).reshape(3, CONV)

    gathered = dict(zip(_BIG, all_gather_shards([W2[nm].astype(BF) for nm in _BIG], name="all_gather_weights")))
    win = build_win(gathered["w_in"], name="build_win")
    wq2, wkv2 = build_wq_wkv(gathered["w_uq"], gathered["w_ukv"], name="build_wq_wkv")
    wao = unshard_cols(gathered["w_attn_out"], name="unshard_w_attn_out")
    wco = unshard_cols(gathered["w_conv_out"], name="unshard_w_conv_out")
    wup = unshard_cols(gathered["w_up"], name="unshard_w_up")
    wo = gathered["w_o"].reshape(D, D)
    wdn = gathered["w_down"].reshape(DFF, D)

    r = _local_step(x[0], ctx[0], loss_target[0], mod_lat, mod_ctx, W2["norm1_g"], W2["q_norm_g"], W2["kv_norm_g"],
                    W2["norm2_g"], W2["final_g"], conv_w_full, W2["conv_b"], ffn_w_full, W2["ffn_conv_b"],
                    win, wq2, wkv2, wao, wco, wo, wup, wdn)

    pq, pkv = shard_wq_wkv_grad(r["dwq2"], r["dwkv2"], name="shard_wq_wkv_grad")
    parts = dict(w_in=shard_win_grad(r["dwin"], r["dwin_c"], name="shard_win_grad"), w_uq=pq, w_ukv=pkv,
                 w_attn_out=shard_cols(r["dwao"], name="shard_w_attn_out"),
                 w_conv_out=shard_cols(r["dwco"], name="shard_w_conv_out"),
                 w_o=r["dwo"].reshape(NDEV, D // NDEV, D), w_up=shard_cols(r["dwup"], name="shard_w_up"),
                 w_down=r["dwdn"].reshape(NDEV, DFF // NDEV, D))
    slots = dict(zip(_BIG, reduce_scatter_send([parts[nm] for nm in _BIG], name="reduce_scatter_grads")))

    a_buf, ssum = sync_small(r, name="sync_small")
    G = {}
    loss = ssum[P_LOSS, 0]
    G["norm1_g"] = ssum[P_N1:P_N1 + 1]
    G["q_norm_g"] = ssum[P_QG:P_QG + 1, :QL]
    G["kv_norm_g"] = ssum[P_KVG:P_KVG + 1, :KVL]
    G["conv_b"] = ssum[P_CB:P_CB + 1, :CONV]
    G["norm2_g"] = ssum[P_N2:P_N2 + 1]
    G["ffn_conv_b"] = ssum[P_FB:P_FB + 2 * FROWS].reshape(1, 2, FROWS * D)[:, :, :DFF].reshape(1, 2 * DFF)
    G["final_g"] = ssum[P_FG:P_FG + 1]
    G["conv_w"] = lax.dynamic_slice(ssum[P_CW:P_CW + 3, :CONV], (0, me * (CONV // NDEV)), (3, CONV // NDEV))
    fw_full = ssum[P_FW:P_FW + 6 * FROWS].reshape(3, 2, FROWS * D)[:, :, :DFF].reshape(3, 2 * DFF)
    G["ffn_conv_w"] = lax.dynamic_slice(fw_full, (0, me * (2 * DFF // NDEV)), (3, 2 * DFF // NDEV))
    G["b_ada"] = (ssum[P_DML:P_DML + 6] + ssum[P_DMC:P_DMC + 6]).reshape(1, 6 * D)

    dml = lax.dynamic_slice(a_buf[:, P_DML:P_DML + 6, :].reshape(NDEV, 6 * D), (0, me * nsh), (NDEV, nsh))
    dmc = lax.dynamic_slice(ssum[P_DMC:P_DMC + 6].reshape(1, 6 * D), (0, me * nsh), (1, nsh))
    G["w_ada"], gcc = ada_bwd(s_all, dml, dmc, W2["w_ada"], W2["c_ctx"], name="ada_bwd")
    G["c_ctx"] = gcc[0:1]

    DL, NM, NV = {}, {}, {}
    DL["w_ada"], NM["w_ada"], NV["w_ada"] = adamw(W2["w_ada"], G["w_ada"], M2["w_ada"], V2["w_ada"], name="adamw_w_ada")
    for nm in _BIG:
        G[nm], DL[nm], NM[nm], NV[nm] = adamw_slots(W2[nm], slots[nm], M2[nm], V2[nm], name="adamw_" + nm)
    small = ["c_ctx", "b_ada", "norm1_g", "q_norm_g", "kv_norm_g", "conv_b", "norm2_g", "ffn_conv_b", "final_g", "conv_w",
             "ffn_conv_w"]
    ds, nms, nvs = adamw_many([W2[k] for k in small], [G[k] for k in small], [M2[k] for k in small],
                              [V2[k] for k in small], name="adamw_small")
    for k, nm in enumerate(small):
        DL[nm], NM[nm], NV[nm] = ds[k], nms[k], nvs[k]

    outs = [loss, r["dx"][None]]
    for grp in (G, DL, NM, NV):
        outs += [grp[nm].reshape(W[nm].shape) for nm in names]
    return tuple(outs)
```

```python
import functools
import numpy as np
import jax
import jax.numpy as jnp
from jax import lax
from jax.experimental import pallas as pl
from jax.experimental.pallas import tpu as pltpu

F32 = jnp.float32
BF = jnp.bfloat16
MESH = pl.DeviceIdType.MESH

D = 1024
T = 2048
TC = 256
TKV = T + TC
GRID_W = 64
NH = 8
DN = 64
DR = 32
DV = 64
QL = 384
KVL = 256
CONV = 512
DFF = 2816
EPS = 1e-6
ROPE_THETA = 10000.0
SCALE = (DN + DR) ** -0.5
NDEV = 8
HP = 128

O_GA, O_GC, O_KV, O_Q, O_CV = 0, 1024, 2048, 2560, 3072
NIN = 4608
CVB = 256
N_IN = 4256
SH_IN = N_IN // NDEV

LR, B1, B2, AEPS, WD, STEP = 0.001, 0.9, 0.999, 1e-08, 0.01, 10


def _pick(n, target, mult=128):
    best = None
    for d in range(mult, min(n, target) + 1, mult):
        if n % d == 0:
            best = d
    return best if best is not None else n


def _swap_start(g):
    return 8 * (g ^ 1)


def mm(a, b, *, ta=False, tb=False, out_dtype=F32, name, tm=512, tn=512, tk=2048, M=None, N=None, K=None,
       a_off=(0, 0), b_off=(0, 0), a_stack=False, b_stack=False, o_stack=False):
    def dims(arr, stack):
        return (arr.shape[1], 2 * arr.shape[2]) if stack else arr.shape

    ar, ac = dims(a, a_stack)
    br, bc = dims(b, b_stack)
    M = M or ((ac if ta else ar) - a_off[1 if ta else 0])
    K = K or ((ar if ta else ac) - a_off[0 if ta else 1])
    N = N or ((br if tb else bc) - b_off[0 if tb else 1])
    tm = _pick(M, tm, 128 if ta else 16)
    tn = _pick(N // 2 if (o_stack or (b_stack and not tb)) else N, tn, 128)
    tk = _pick(K // 2 if ((a_stack and not ta) or (b_stack and tb)) else K, tk, 128)
    nk = K // tk
    ca = 0 if ta else 1
    cb = 1 if tb else 0

    def body(a_ref, b_ref, o_ref, acc):
        k = pl.program_id(2)
        part = lax.dot_general(a_ref[...].astype(BF), b_ref[...].astype(BF),
                               (((ca,), (cb,)), ((), ())), preferred_element_type=F32)
        if nk == 1:
            o_ref[...] = part.astype(o_ref.dtype)
        else:
            @pl.when(k == 0)
            def _():
                acc[...] = part

            @pl.when(k > 0)
            def _():
                acc[...] += part

            @pl.when(k == nk - 1)
            def _():
                o_ref[...] = acc[...].astype(o_ref.dtype)

    def spec(blk, rc, off, stack, ncols):
        assert off[0] % blk[0] == 0 and off[1] % blk[1] == 0, (name, blk, off)
        ro, co = off[0] // blk[0], off[1] // blk[1]
        if not stack:
            return pl.BlockSpec(blk, lambda i, j, k: (rc(i, j, k)[0] + ro, rc(i, j, k)[1] + co))
        nhb = ncols // 2 // blk[1]
        return pl.BlockSpec((None,) + blk,
                            lambda i, j, k: ((rc(i, j, k)[1] + co) // nhb, rc(i, j, k)[0] + ro, (rc(i, j, k)[1] + co) % nhb))

    a_spec = spec((tk, tm), lambda i, j, k: (k, i), a_off, a_stack, ac) if ta else \
        spec((tm, tk), lambda i, j, k: (i, k), a_off, a_stack, ac)
    b_spec = spec((tn, tk), lambda i, j, k: (j, k), b_off, b_stack, bc) if tb else \
        spec((tk, tn), lambda i, j, k: (k, j), b_off, b_stack, bc)
    o_spec = spec((tm, tn), lambda i, j, k: (i, j), (0, 0), o_stack, N)
    o_shape = (2, M, N // 2) if o_stack else (M, N)
    return pl.pallas_call(
        body, name=name, grid=(M // tm, N // tn, nk),
        in_specs=[a_spec, b_spec], out_specs=o_spec, out_shape=jax.ShapeDtypeStruct(o_shape, out_dtype),
        scratch_shapes=[pltpu.VMEM((tm, tn) if nk > 1 else (8, 128), F32)],
        compiler_params=pltpu.CompilerParams(dimension_semantics=("parallel", "parallel", "arbitrary")),
    )(a, b)


def _row(width):
    return pl.BlockSpec((1, width), lambda *_: (0, 0))


NLAT = T // TC


def normmod_cat(ctx, x, g, csc, csh, sc, sh, *, name, tm=256):
    assert tm == TC

    def body(c_ref, x_ref, g_ref, csc_ref, csh_ref, sc_ref, sh_ref, h_ref):
        last = pl.program_id(0) == NLAT
        xv = jnp.where(last, c_ref[...], x_ref[...])
        scv = jnp.where(last, csc_ref[...], sc_ref[...])
        shv = jnp.where(last, csh_ref[...], sh_ref[...])
        r = lax.rsqrt(jnp.mean(xv * xv, axis=-1, keepdims=True) + EPS)
        h_ref[...] = ((xv * r * g_ref[...]) * (1.0 + scv) + shv).astype(BF)

    return pl.pallas_call(
        body, name=name, grid=(TKV // tm,),
        in_specs=[pl.BlockSpec((tm, D), lambda i: (0, 0)), pl.BlockSpec((tm, D), lambda i: (jnp.minimum(i, NLAT - 1), 0)),
                  _row(D), _row(D), _row(D), _row(D), _row(D)],
        out_specs=pl.BlockSpec((tm, D), lambda i: (i, 0)), out_shape=jax.ShapeDtypeStruct((TKV, D), BF),
        compiler_params=pltpu.CompilerParams(dimension_semantics=("parallel",)),
    )(ctx, x, g, csc, csh, sc, sh)


def resid_normmod(x, a, gate, g, sc, sh, *, name, tm=256):
    R = x.shape[0]

    def body(x_ref, a_ref, gate_ref, g_ref, sc_ref, sh_ref, x1_ref, h_ref):
        xv = x_ref[...] + gate_ref[...] * a_ref[...]
        x1_ref[...] = xv
        r = lax.rsqrt(jnp.mean(xv * xv, axis=-1, keepdims=True) + EPS)
        h_ref[...] = ((xv * r * g_ref[...]) * (1.0 + sc_ref[...]) + sh_ref[...]).astype(BF)

    blk = pl.BlockSpec((tm, D), lambda i: (i, 0))
    return pl.pallas_call(
        body, name=name, grid=(R // tm,), in_specs=[blk, blk, _row(D), _row(D), _row(D), _row(D)],
        out_specs=[blk, blk],
        out_shape=[jax.ShapeDtypeStruct((R, D), F32), jax.ShapeDtypeStruct((R, D), BF)],
        compiler_params=pltpu.CompilerParams(dimension_semantics=("parallel",)),
    )(x, a, gate, g, sc, sh)


def kvprep(pc, p, kvg, wkv2, ck, sk, *, name, tm=256):
    assert tm == TC
    nb = TKV // tm
    kvcol = O_KV // 512

    def body(pc_ref, p_ref, g_ref, w_ref, ck_ref, sk_ref, k_ref, v_ref, ckv_ref):
        i = pl.program_id(0)
        t = jnp.where(i == NLAT, pc_ref[...], p_ref[...])
        pk = t[:, :KVL]
        r = lax.rsqrt(jnp.mean(pk * pk, axis=-1, keepdims=True) + EPS)
        ckv = (pk * r * g_ref[...]).astype(BF)
        ckv_ref[...] = ckv
        kv2 = jnp.dot(ckv, w_ref[...], preferred_element_type=F32)
        krr = t[:, KVL:KVL + HP] * ck_ref[...] + t[:, KVL + HP:KVL + 2 * HP] * sk_ref[...]
        k_ref[...] = (kv2[:, :NH * HP] + jnp.concatenate([krr] * NH, axis=1)).astype(BF)
        v_ref[...] = kv2[:, NH * HP:].astype(BF)

    return pl.pallas_call(
        body, name=name, grid=(nb,),
        in_specs=[pl.BlockSpec((tm, 512), lambda i: (0, 0)),
                  pl.BlockSpec((tm, 512), lambda i: (jnp.minimum(i, NLAT - 1), kvcol)),
                  _row(KVL), pl.BlockSpec((KVL, NH * HP + NH * DV), lambda i: (0, 0)),
                  pl.BlockSpec((tm, HP), lambda i: (i, 0)), pl.BlockSpec((tm, HP), lambda i: (i, 0))],
        out_specs=[pl.BlockSpec((tm, NH * HP), lambda i: (i, 0)), pl.BlockSpec((tm, NH * DV), lambda i: (i, 0)),
                   pl.BlockSpec((tm, KVL), lambda i: (i, 0))],
        out_shape=[jax.ShapeDtypeStruct((TKV, NH * HP), BF), jax.ShapeDtypeStruct((TKV, NH * DV), BF),
                   jax.ShapeDtypeStruct((TKV, KVL), BF)],
        compiler_params=pltpu.CompilerParams(dimension_semantics=("parallel",)),
    )(pc, p, kvg, wkv2, ck, sk)


def qprep(p, qg, wq2, cq_t, sq_t, *, name, tm=256):
    qcol = O_Q // 512

    def body(p_ref, g_ref, w_ref, c_ref, s_ref, q_ref, cq_ref):
        pq = p_ref[...]
        r = lax.rsqrt(jnp.sum(pq * pq, axis=-1, keepdims=True) * (1.0 / QL) + EPS)
        cq = (pq * r * g_ref[...]).astype(BF)
        cq_ref[...] = cq
        q2 = jnp.dot(cq, w_ref[...], preferred_element_type=F32)
        cc = jnp.concatenate([c_ref[...]] * NH, axis=1)
        ss = jnp.concatenate([s_ref[...]] * NH, axis=1)
        q_ref[...] = (q2[:, :NH * HP] * cc + q2[:, NH * HP:] * ss).astype(BF)

    return pl.pallas_call(
        body, name=name, grid=(T // tm,),
        in_specs=[pl.BlockSpec((tm, 512), lambda i: (i, qcol)), _row(512),
                  pl.BlockSpec((512, 2 * NH * HP), lambda i: (0, 0)),
                  pl.BlockSpec((tm, HP), lambda i: (i, 0)), pl.BlockSpec((tm, HP), lambda i: (i, 0))],
        out_specs=[pl.BlockSpec((tm, NH * HP), lambda i: (i, 0)), pl.BlockSpec((tm, 512), lambda i: (i, 0))],
        out_shape=[jax.ShapeDtypeStruct((T, NH * HP), BF), jax.ShapeDtypeStruct((T, 512), BF)],
        compiler_params=pltpu.CompilerParams(dimension_semantics=("parallel",)),
    )(p, qg, wq2, cq_t, sq_t)


def _head_mask(h):
    lanes = lax.broadcasted_iota(jnp.int32, (1, 2 * DV), 1)
    return (lanes // DV) == (h % 2)


def attn_fwd(q, k, v, *, name, tq=256):
    def body(q_ref, k_ref, v_ref, o_ref):
        h = pl.program_id(1)
        s = lax.dot_general(q_ref[...], k_ref[...], (((1,), (1,)), ((), ())), preferred_element_type=F32) * SCALE
        m = jnp.max(s, axis=-1, keepdims=True)
        e = jnp.exp(s - m)
        pr = (e * (1.0 / jnp.sum(e, axis=-1, keepdims=True))).astype(BF)
        vm = jnp.where(_head_mask(h), v_ref[...], jnp.zeros_like(v_ref[...]))
        o2 = jnp.dot(pr, vm, preferred_element_type=F32).astype(BF)

        @pl.when(h % 2 == 0)
        def _():
            o_ref[...] = o2

        @pl.when(h % 2 == 1)
        def _():
            o_ref[...] = o_ref[...] + o2

    return pl.pallas_call(
        body, name=name, grid=(T // tq, NH),
        in_specs=[pl.BlockSpec((tq, HP), lambda i, h: (i, h)), pl.BlockSpec((TKV, HP), lambda i, h: (0, h)),
                  pl.BlockSpec((TKV, 2 * DV), lambda i, h: (0, h // 2))],
        out_specs=pl.BlockSpec((tq, 2 * DV), lambda i, h: (i, h // 2)),
        out_shape=jax.ShapeDtypeStruct((T, NH * DV), BF),
        compiler_params=pltpu.CompilerParams(dimension_semantics=("parallel", "arbitrary")),
    )(q, k, v)


def _shift_dn(x):
    n = x.shape[0]
    rows = lax.broadcasted_iota(jnp.int32, (n, 1), 0)
    return jnp.where(rows == 0, 0.0, pltpu.roll(x, 1, axis=0))


def _shift_up(x):
    n = x.shape[0]
    rows = lax.broadcasted_iota(jnp.int32, (n, 1), 0)
    return jnp.where(rows == n - 1, 0.0, pltpu.roll(x, n - 1, axis=0))


def _conv(x, w_ref, b_ref):
    return b_ref[...] + _shift_dn(x) * w_ref[0:1, :] + x * w_ref[1:2, :] + _shift_up(x) * w_ref[2:3, :]


def _conv_t(dy, w_ref):
    return _shift_up(dy) * w_ref[0:1, :] + dy * w_ref[1:2, :] + _shift_dn(dy) * w_ref[2:3, :]


def _conv_wgrad(dw_ref, dy, x):
    dw_ref[0:1, :] = jnp.sum(dy * _shift_dn(x), axis=0, keepdims=True)
    dw_ref[1:2, :] = jnp.sum(dy * x, axis=0, keepdims=True)
    dw_ref[2:3, :] = jnp.sum(dy * _shift_up(x), axis=0, keepdims=True)


def convz(p, cw, cb, *, name):
    o0 = O_CV // (3 * CVB)

    def body(p_ref, w_ref, bias_ref, z_ref):
        xv, bv, cv = p_ref[:, 0:CVB], p_ref[:, CVB:2 * CVB], p_ref[:, 2 * CVB:3 * CVB]
        z_ref[...] = (bv * _conv(cv * xv, w_ref, bias_ref)).astype(BF)

    return pl.pallas_call(
        body, name=name, grid=(CONV // CVB,),
        in_specs=[pl.BlockSpec((T, 3 * CVB), lambda j: (0, o0 + j)), pl.BlockSpec((3, CVB), lambda j: (0, j)),
                  pl.BlockSpec((1, CVB), lambda j: (0, j))],
        out_specs=pl.BlockSpec((T, CVB), lambda j: (0, j)),
        out_shape=jax.ShapeDtypeStruct((T, CONV), BF),
        compiler_params=pltpu.CompilerParams(dimension_semantics=("parallel",)),
    )(p, cw, cb)


def gate_merge(p, ya, yc, *, name, tm=256):
    def body(ga_ref, gc_ref, ya_ref, yc_ref, o_ref):
        o_ref[...] = (jax.nn.sigmoid(ga_ref[...]) * ya_ref[...] + jax.nn.sigmoid(gc_ref[...]) * yc_ref[...]).astype(BF)

    blk = pl.BlockSpec((tm, D), lambda i: (i, 0))
    return pl.pallas_call(
        body, name=name, grid=(T // tm,),
        in_specs=[pl.BlockSpec((tm, D), lambda i: (i, O_GA // D)), pl.BlockSpec((tm, D), lambda i: (i, O_GC // D)), blk, blk],
        out_specs=blk, out_shape=jax.ShapeDtypeStruct((T, D), BF),
        compiler_params=pltpu.CompilerParams(dimension_semantics=("parallel",)),
    )(p, p, ya, yc)


def ffn_act(u0, cw, cb, *, name, tc=256):
    nb = DFF // tc

    def body(u_ref, wg_ref, wv_ref, bg_ref, bv_ref, f_ref):
        ug = _conv(u_ref[0], wg_ref, bg_ref)
        uv = _conv(u_ref[1], wv_ref, bv_ref)
        f_ref[...] = (ug * jax.nn.sigmoid(ug) * uv).astype(BF)

    return pl.pallas_call(
        body, name=name, grid=(nb,),
        in_specs=[pl.BlockSpec((2, T, tc), lambda j: (0, 0, j)),
                  pl.BlockSpec((3, tc), lambda j: (0, j)), pl.BlockSpec((3, tc), lambda j: (0, nb + j)),
                  pl.BlockSpec((1, tc), lambda j: (0, j)), pl.BlockSpec((1, tc), lambda j: (0, nb + j))],
        out_specs=pl.BlockSpec((T, tc), lambda j: (0, j)),
        out_shape=jax.ShapeDtypeStruct((T, DFF), BF),
        compiler_params=pltpu.CompilerParams(dimension_semantics=("parallel",)),
    )(u0, cw, cw, cb, cb)


def final_loss(x1, d, g2, fg, tgt, *, name, tm=256):
    def body(x1_ref, d_ref, g2_ref, fg_ref, t_ref, dx_ref, dd_ref, dfg_ref, loss_ref):
        i = pl.program_id(0)
        xv = x1_ref[...] + g2_ref[...] * d_ref[...]
        r = lax.rsqrt(jnp.mean(xv * xv, axis=-1, keepdims=True) + EPS)
        xh = xv * r
        diff = xh * fg_ref[...] - t_ref[...]
        part = 0.5 * jnp.sum(jnp.mean(diff * diff, axis=-1, keepdims=True), axis=0, keepdims=True)
        dy = diff * (1.0 / D)
        a = dy * fg_ref[...]
        dx = r * (a - xh * jnp.mean(a * xh, axis=-1, keepdims=True))
        dx_ref[...] = dx
        dd_ref[...] = (dx * g2_ref[...]).astype(BF)
        dfg = jnp.sum(dy * xh, axis=0, keepdims=True)

        @pl.when(i == 0)
        def _():
            dfg_ref[...] = dfg
            loss_ref[...] = jnp.broadcast_to(part, (1, 128))

        @pl.when(i > 0)
        def _():
            dfg_ref[...] += dfg
            loss_ref[...] += jnp.broadcast_to(part, (1, 128))

    blk = pl.BlockSpec((tm, D), lambda i: (i, 0))
    return pl.pallas_call(
        body, name=name, grid=(T // tm,), in_specs=[blk, blk, _row(D), _row(D), blk],
        out_specs=[blk, blk, _row(D), _row(128)],
        out_shape=[jax.ShapeDtypeStruct((T, D), F32), jax.ShapeDtypeStruct((T, D), BF),
                   jax.ShapeDtypeStruct((1, D), F32), jax.ShapeDtypeStruct((1, 128), F32)],
        compiler_params=pltpu.CompilerParams(dimension_semantics=("arbitrary",)),
    )(x1, d, g2, fg, tgt)


def normmod_bwd(x, dh, g, sc, dres, gsrc, gate, *, name, tm=256):
    R = x.shape[0]
    has_res = dres is not None

    def body(*refs):
        if has_res:
            x_ref, dh_ref, g_ref, sc_ref, dres_ref, gsrc_ref, gate_ref, dx_ref, dxg_ref, st_ref = refs
        else:
            x_ref, dh_ref, g_ref, sc_ref, st_ref = refs
        i = pl.program_id(0)
        xv = x_ref[...]
        r = lax.rsqrt(jnp.mean(xv * xv, axis=-1, keepdims=True) + EPS)
        xh = xv * r
        dhv = dh_ref[...]
        n = xh * g_ref[...]
        dn = dhv * (1.0 + sc_ref[...])
        a = dn * g_ref[...]
        rows = [jnp.sum(dhv, axis=0, keepdims=True), jnp.sum(dhv * n, axis=0, keepdims=True),
                jnp.sum(dn * xh, axis=0, keepdims=True)]
        if has_res:
            dr = dres_ref[...]
            dx = dr + r * (a - xh * jnp.mean(a * xh, axis=-1, keepdims=True))
            dx_ref[...] = dx
            dxg_ref[...] = (dx * gate_ref[...]).astype(BF)
            rows.append(jnp.sum(dr * gsrc_ref[...], axis=0, keepdims=True))
        else:
            rows.append(jnp.zeros((1, D), F32))

        @pl.when(i == 0)
        def _():
            for k, row in enumerate(rows):
                st_ref[k:k + 1, :] = row

        @pl.when(i > 0)
        def _():
            for k, row in enumerate(rows):
                st_ref[k:k + 1, :] += row

    blk = pl.BlockSpec((tm, D), lambda i: (i, 0))
    st_spec = pl.BlockSpec((4, D), lambda i: (0, 0))
    st_shape = jax.ShapeDtypeStruct((4, D), F32)
    cp = pltpu.CompilerParams(dimension_semantics=("arbitrary",))
    if has_res:
        return pl.pallas_call(
            body, name=name, grid=(R // tm,), in_specs=[blk, blk, _row(D), _row(D), blk, blk, _row(D)],
            out_specs=[blk, blk, st_spec],
            out_shape=[jax.ShapeDtypeStruct((R, D), F32), jax.ShapeDtypeStruct((R, D), BF), st_shape],
            compiler_params=cp,
        )(x, dh, g, sc, dres, gsrc, gate)
    return pl.pallas_call(
        body, name=name, grid=(R // tm,), in_specs=[blk, blk, _row(D), _row(D)],
        out_specs=st_spec, out_shape=st_shape, compiler_params=cp,
    )(x, dh, g, sc)


def ffn_act_bwd(u0, df, cw, cb, *, name, tc=256):
    nb = DFF // tc

    def body(u_ref, df_ref, wg_ref, wv_ref, bg_ref, bv_ref, du_ref, dw_ref, db_ref):
        xg, xv = u_ref[0], u_ref[1]
        ug = _conv(xg, wg_ref, bg_ref)
        uv = _conv(xv, wv_ref, bv_ref)
        sig = jax.nn.sigmoid(ug)
        dfv = df_ref[...]
        dug = dfv * uv * (sig * (1.0 + ug * (1.0 - sig)))
        duv = dfv * (ug * sig)
        du_ref[0] = _conv_t(dug, wg_ref).astype(BF)
        du_ref[1] = _conv_t(duv, wv_ref).astype(BF)
        _conv_wgrad(dw_ref.at[0], dug, xg)
        _conv_wgrad(dw_ref.at[1], duv, xv)
        db_ref[0] = jnp.sum(dug, axis=0, keepdims=True)
        db_ref[1] = jnp.sum(duv, axis=0, keepdims=True)

    lo = lambda r: pl.BlockSpec((r, tc), lambda j: (0, j))
    hi = lambda r: pl.BlockSpec((r, tc), lambda j: (0, nb + j))
    st = lambda r: pl.BlockSpec((2, r, tc), lambda j: (0, 0, j))
    return pl.pallas_call(
        body, name=name, grid=(nb,),
        in_specs=[st(T), lo(T), lo(3), hi(3), lo(1), hi(1)],
        out_specs=[st(T), st(3), st(1)],
        out_shape=[jax.ShapeDtypeStruct((2, T, DFF), BF), jax.ShapeDtypeStruct((2, 3, DFF), F32),
                   jax.ShapeDtypeStruct((2, 1, DFF), F32)],
        compiler_params=pltpu.CompilerParams(dimension_semantics=("parallel",)),
    )(u0, df, cw, cw, cb, cb)


def gate_merge_bwd(p, ya, yc, dm, *, name, tm=256):
    def body(ga_ref, gc_ref, ya_ref, yc_ref, dm_ref, dya_ref, dyc_ref, dp_ref):
        sa, sc_ = jax.nn.sigmoid(ga_ref[...]), jax.nn.sigmoid(gc_ref[...])
        dmv = dm_ref[...]
        dya_ref[...] = (dmv * sa).astype(BF)
        dyc_ref[...] = (dmv * sc_).astype(BF)
        dp_ref[:, 0:D] = (dmv * ya_ref[...] * (sa * (1.0 - sa))).astype(BF)
        dp_ref[:, D:2 * D] = (dmv * yc_ref[...] * (sc_ * (1.0 - sc_))).astype(BF)

    blk = pl.BlockSpec((tm, D), lambda i: (i, 0))
    sh = jax.ShapeDtypeStruct((T, D), BF)
    return pl.pallas_call(
        body, name=name, grid=(T // tm,),
        in_specs=[pl.BlockSpec((tm, D), lambda i: (i, O_GA // D)), pl.BlockSpec((tm, D), lambda i: (i, O_GC // D)), blk, blk, blk],
        out_specs=[blk, blk, pl.BlockSpec((tm, 2 * D), lambda i: (i, 0))],
        out_shape=[sh, sh, jax.ShapeDtypeStruct((T, NIN), BF)],
        compiler_params=pltpu.CompilerParams(dimension_semantics=("parallel",)),
    )(p, p, ya, yc, dm)


def convz_bwd(p, dz, cw, cb, dp, *, name):
    o0 = O_CV // (3 * CVB)

    def body(p_ref, dz_ref, w_ref, bias_ref, dp_in, dp_ref, dw_ref, dbias_ref):
        xv, bv, cv = p_ref[:, 0:CVB], p_ref[:, CVB:2 * CVB], p_ref[:, 2 * CVB:3 * CVB]
        ci = cv * xv
        dwc = _conv(ci, w_ref, bias_ref)
        dzv = dz_ref[...]
        ddw = dzv * bv
        dci = _conv_t(ddw, w_ref)
        dp_ref[:, 0:CVB] = (dci * cv).astype(BF)
        dp_ref[:, CVB:2 * CVB] = (dzv * dwc).astype(BF)
        dp_ref[:, 2 * CVB:3 * CVB] = (dci * xv).astype(BF)
        _conv_wgrad(dw_ref, ddw, ci)
        dbias_ref[...] = jnp.sum(ddw, axis=0, keepdims=True)

    own = lambda r: pl.BlockSpec((r, CVB), lambda j: (0, j))
    return pl.pallas_call(
        body, name=name, grid=(CONV // CVB,),
        in_specs=[pl.BlockSpec((T, 3 * CVB), lambda j: (0, o0 + j)), own(T), own(3), own(1),
                  pl.BlockSpec(memory_space=pl.ANY)],
        out_specs=[pl.BlockSpec((T, 3 * CVB), lambda j: (0, o0 + j)), own(3), own(1)],
        out_shape=[jax.ShapeDtypeStruct((T, NIN), BF), jax.ShapeDtypeStruct((3, CONV), F32),
                   jax.ShapeDtypeStruct((1, CONV), F32)],
        input_output_aliases={4: 0},
        compiler_params=pltpu.CompilerParams(dimension_semantics=("parallel",)),
    )(p, dz, cw, cb, dp)


def attn_bwd(q, k, v, do, *, name, tq=256):
    def body(q_ref, k_ref, v_ref, do_ref, dq_ref, dk_ref, dv_ref):
        h, i = pl.program_id(0), pl.program_id(1)
        qv, kv = q_ref[...], k_ref[...]
        s = lax.dot_general(qv, kv, (((1,), (1,)), ((), ())), preferred_element_type=F32) * SCALE
        m = jnp.max(s, axis=-1, keepdims=True)
        e = jnp.exp(s - m)
        pr = e * (1.0 / jnp.sum(e, axis=-1, keepdims=True))
        mask = _head_mask(h)
        vm = jnp.where(mask, v_ref[...], jnp.zeros_like(v_ref[...]))
        dom = jnp.where(mask, do_ref[...], jnp.zeros_like(do_ref[...]))
        dp = lax.dot_general(dom, vm, (((1,), (1,)), ((), ())), preferred_element_type=F32)
        ds = (pr * (dp - jnp.sum(pr * dp, axis=-1, keepdims=True)) * SCALE).astype(BF)
        dq_ref[...] = jnp.dot(ds, kv, preferred_element_type=F32)
        dk = lax.dot_general(ds, qv, (((0,), (0,)), ((), ())), preferred_element_type=F32)
        dv = lax.dot_general(pr.astype(BF), dom, (((0,), (0,)), ((), ())), preferred_element_type=F32)

        @pl.when(i == 0)
        def _():
            dk_ref[...] = dk

        @pl.when(i > 0)
        def _():
            dk_ref[...] += dk

        @pl.when((i == 0) & (h % 2 == 0))
        def _():
            dv_ref[...] = dv

        @pl.when((i > 0) | (h % 2 == 1))
        def _():
            dv_ref[...] += dv

    return pl.pallas_call(
        body, name=name, grid=(NH, T // tq),
        in_specs=[pl.BlockSpec((tq, HP), lambda h, i: (i, h)), pl.BlockSpec((TKV, HP), lambda h, i: (0, h)),
                  pl.BlockSpec((TKV, 2 * DV), lambda h, i: (0, h // 2)), pl.BlockSpec((tq, 2 * DV), lambda h, i: (i, h // 2))],
        out_specs=[pl.BlockSpec((tq, HP), lambda h, i: (i, h)), pl.BlockSpec((TKV, HP), lambda h, i: (0, h)),
                   pl.BlockSpec((TKV, 2 * DV), lambda h, i: (0, h // 2))],
        out_shape=[jax.ShapeDtypeStruct((T, NH * HP), F32), jax.ShapeDtypeStruct((TKV, NH * HP), F32),
                   jax.ShapeDtypeStruct((TKV, NH * DV), F32)],
        compiler_params=pltpu.CompilerParams(dimension_semantics=("arbitrary", "arbitrary")),
    )(q, k, v, do)


def qprep_bwd(p, dq, qg, wq2, cq_t, sq_t, dp, *, name, tm=256):
    qcol = O_Q // 512

    def body(p_ref, dq_ref, g_ref, w_ref, c_ref, s_ref, dp_in, dp_ref, dq2_ref, dg_ref):
        i = pl.program_id(0)
        dqv = dq_ref[...]
        cc = jnp.concatenate([c_ref[...]] * NH, axis=1)
        ss = jnp.concatenate([s_ref[...]] * NH, axis=1)
        dq2 = jnp.concatenate([dqv * cc, dqv * ss], axis=1).astype(BF)
        dq2_ref[...] = dq2
        dcq = lax.dot_general(dq2, w_ref[...], (((1,), (1,)), ((), ())), preferred_element_type=F32)
        pq = p_ref[...]
        r = lax.rsqrt(jnp.sum(pq * pq, axis=-1, keepdims=True) * (1.0 / QL) + EPS)
        xh = pq * r
        a = dcq * g_ref[...]
        dp_ref[...] = (r * (a - xh * (jnp.sum(a * xh, axis=-1, keepdims=True) * (1.0 / QL)))).astype(BF)
        dg = jnp.sum(dcq * xh, axis=0, keepdims=True)

        @pl.when(i == 0)
        def _():
            dg_ref[...] = dg

        @pl.when(i > 0)
        def _():
            dg_ref[...] += dg

    return pl.pallas_call(
        body, name=name, grid=(T // tm,),
        in_specs=[pl.BlockSpec((tm, 512), lambda i: (i, qcol)), pl.BlockSpec((tm, NH * HP), lambda i: (i, 0)), _row(512),
                  pl.BlockSpec((512, 2 * NH * HP), lambda i: (0, 0)),
                  pl.BlockSpec((tm, HP), lambda i: (i, 0)), pl.BlockSpec((tm, HP), lambda i: (i, 0)),
                  pl.BlockSpec(memory_space=pl.ANY)],
        out_specs=[pl.BlockSpec((tm, 512), lambda i: (i, qcol)), pl.BlockSpec((tm, 2 * NH * HP), lambda i: (i, 0)), _row(512)],
        out_shape=[jax.ShapeDtypeStruct((T, NIN), BF), jax.ShapeDtypeStruct((T, 2 * NH * HP), BF),
                   jax.ShapeDtypeStruct((1, 512), F32)],
        input_output_aliases={6: 0},
        compiler_params=pltpu.CompilerParams(dimension_semantics=("arbitrary",)),
    )(p, dq, qg, wq2, cq_t, sq_t, dp)


def kvprep_bwd(pc, p, dk, dv, kvg, wkv2, ck, sk, dp, *, name, tm=256):
    assert tm == TC
    nb = TKV // tm
    kvcol = O_KV // 512

    def body(pc_ref, p_ref, dk_ref, dv_ref, g_ref, w_ref, ck_ref, sk_ref, dp_in, dp_ref, dpc_ref, dkv2_ref, dg_ref):
        i = pl.program_id(0)
        t = jnp.where(i == NLAT, pc_ref[...], p_ref[...])
        pk = t[:, :KVL]
        r = lax.rsqrt(jnp.mean(pk * pk, axis=-1, keepdims=True) + EPS)
        xh = pk * r
        dkv = dk_ref[...]
        dkv2 = jnp.concatenate([dkv, dv_ref[...]], axis=1).astype(BF)
        dkv2_ref[...] = dkv2
        dckv = lax.dot_general(dkv2, w_ref[...], (((1,), (1,)), ((), ())), preferred_element_type=F32)
        a = dckv * g_ref[...]
        dpk = r * (a - xh * jnp.mean(a * xh, axis=-1, keepdims=True))
        dkr = dkv[:, 0:HP]
        for hh in range(1, NH):
            dkr = dkr + dkv[:, hh * HP:(hh + 1) * HP]
        res = jnp.concatenate([dpk, dkr * ck_ref[...], dkr * sk_ref[...]], axis=1).astype(BF)
        dg = jnp.sum(dckv * xh, axis=0, keepdims=True)

        @pl.when(i == 0)
        def _():
            dg_ref[...] = dg

        @pl.when(i > 0)
        def _():
            dg_ref[...] += dg

        @pl.when(i < NLAT)
        def _():
            dp_ref[...] = res

        @pl.when(i == NLAT)
        def _():
            dpc_ref[...] = res

    rb = lambda w: pl.BlockSpec((tm, w), lambda i: (i, 0))
    return pl.pallas_call(
        body, name=name, grid=(nb,),
        in_specs=[pl.BlockSpec((tm, 512), lambda i: (0, 0)),
                  pl.BlockSpec((tm, 512), lambda i: (jnp.minimum(i, NLAT - 1), kvcol)),
                  rb(NH * HP), rb(NH * DV), _row(KVL), pl.BlockSpec((KVL, NH * HP + NH * DV), lambda i: (0, 0)),
                  rb(HP), rb(HP), pl.BlockSpec(memory_space=pl.ANY)],
        out_specs=[pl.BlockSpec((tm, 512), lambda i: (jnp.minimum(i, NLAT - 1), kvcol)),
                   pl.BlockSpec((tm, 512), lambda i: (0, 0)), rb(NH * HP + NH * DV), _row(KVL)],
        out_shape=[jax.ShapeDtypeStruct((T, NIN), BF), jax.ShapeDtypeStruct((TC, 512), BF),
                   jax.ShapeDtypeStruct((TKV, NH * HP + NH * DV), BF), jax.ShapeDtypeStruct((1, KVL), F32)],
        input_output_aliases={8: 0},
        compiler_params=pltpu.CompilerParams(dimension_semantics=("arbitrary",)),
    )(pc, p, dk, dv, kvg, wkv2, ck, sk, dp)


def _pieces(src, width, n):
    out, c = [], src
    while c < src + width:
        k = c // n
        w = min(src + width, (k + 1) * n) - c
        out.append((k, c - k * n, c - src, w))
        c += w
    return out


def _win_moves():
    mv = [(2208, 1024, O_GA), (3232, 1024, O_GC), (0, KVL, O_KV), (256, DR, O_KV + KVL + DN), (288, QL, O_Q)]
    mv += [(256 + _swap_start(g), 8, O_KV + KVL + HP + DN + 8 * g) for g in range(4)]
    for j in range(CONV // CVB):
        base = O_CV + 3 * CVB * j
        mv += [(672 + CVB * j, CVB, base), (1184 + CVB * j, CVB, base + CVB), (1696 + CVB * j, CVB, base + 2 * CVB)]
    return mv


_WIN_ZERO = [(O_KV + KVL, DN), (O_KV + KVL + DN + DR, HP - DN - DR), (O_KV + KVL + HP, DN),
             (O_KV + KVL + HP + DN + DR, HP - DN - DR), (O_Q + QL, 512 - QL)]


def build_win(g, *, name, tm=256):
    def body(g_ref, o_ref):
        for src, w, dst in _win_moves():
            for k, a, off, pw in _pieces(src, w, SH_IN):
                o_ref[:, dst + off:dst + off + pw] = g_ref[k, :, a:a + pw]
        for c0, w in _WIN_ZERO:
            o_ref[:, c0:c0 + w] = jnp.zeros((tm, w), o_ref.dtype)

    return pl.pallas_call(
        body, name=name, grid=(D // tm,), in_specs=[pl.BlockSpec((NDEV, tm, SH_IN), lambda i: (0, i, 0))],
        out_specs=pl.BlockSpec((tm, NIN), lambda i: (i, 0)), out_shape=jax.ShapeDtypeStruct((D, NIN), g.dtype),
        compiler_params=pltpu.CompilerParams(dimension_semantics=("parallel",)),
    )(g)


def shard_win_grad(dw, dwc, *, name, tm=256):
    def body(dw_ref, dwc_ref, o_ref, kvs):
        kvs[...] = dw_ref[:, O_KV:O_KV + 512] + dwc_ref[...]

        def src(col, w):
            if O_KV <= col < O_KV + 512:
                return kvs[:, col - O_KV:col - O_KV + w]
            return dw_ref[:, col:col + w]

        for s, w, dst in _win_moves():
            if w == 8 or s == 256:
                continue
            for k, a, off, pw in _pieces(s, w, SH_IN):
                o_ref[k, :, a:a + pw] = src(dst + off, pw).astype(o_ref.dtype)
        for g in range(4):
            val = src(O_KV + KVL + DN + 8 * g, 8) + src(O_KV + KVL + HP + DN + _swap_start(g), 8)
            o_ref[0, :, 256 + 8 * g:256 + 8 * g + 8] = val.astype(o_ref.dtype)

    return pl.pallas_call(
        body, name=name, grid=(D // tm,),
        in_specs=[pl.BlockSpec((tm, NIN), lambda i: (i, 0)), pl.BlockSpec((tm, 512), lambda i: (i, 0))],
        out_specs=pl.BlockSpec((NDEV, tm, SH_IN), lambda i: (0, i, 0)),
        out_shape=jax.ShapeDtypeStruct((NDEV, D, SH_IN), BF),
        scratch_shapes=[pltpu.VMEM((tm, 512), F32)],
        compiler_params=pltpu.CompilerParams(dimension_semantics=("parallel",)),
    )(dw, dwc)


def build_wq_wkv(gq, gkv, *, name):
    def body(gq_ref, gkv_ref, q_ref, kv_ref):
        q_ref[...] = jnp.zeros_like(q_ref)
        kv_ref[...] = jnp.zeros_like(kv_ref)
        for h in range(NH):
            q_ref[0:QL, h * HP:h * HP + DN + DR] = gq_ref[h]
            for g in range(4):
                c0 = NH * HP + h * HP + DN + 8 * g
                q_ref[0:QL, c0:c0 + 8] = gq_ref[h, :, DN + _swap_start(g):DN + _swap_start(g) + 8]
            kv_ref[:, h * HP:h * HP + DN] = gkv_ref[h, :, 0:DN]
            kv_ref[:, NH * HP + h * DV:NH * HP + (h + 1) * DV] = gkv_ref[h, :, DN:DN + DV]

    vm = pl.BlockSpec(memory_space=pltpu.VMEM)
    return pl.pallas_call(
        body, name=name, in_specs=[vm, vm], out_specs=[vm, vm],
        out_shape=[jax.ShapeDtypeStruct((512, 2 * NH * HP), gq.dtype), jax.ShapeDtypeStruct((KVL, NH * HP + NH * DV), gq.dtype)],
    )(gq, gkv)


def shard_wq_wkv_grad(dwq2, dwkv2, *, name):
    def body(q_ref, kv_ref, gq_ref, gkv_ref):
        for h in range(NH):
            gq_ref[h, :, 0:DN] = q_ref[0:QL, h * HP:h * HP + DN].astype(BF)
            for g in range(4):
                a = q_ref[0:QL, h * HP + DN + 8 * g:h * HP + DN + 8 * g + 8]
                c0 = NH * HP + h * HP + DN + _swap_start(g)
                gq_ref[h, :, DN + 8 * g:DN + 8 * g + 8] = (a + q_ref[0:QL, c0:c0 + 8]).astype(BF)
            gkv_ref[h, :, 0:DN] = kv_ref[:, h * HP:h * HP + DN].astype(BF)
            gkv_ref[h, :, DN:DN + DV] = kv_ref[:, NH * HP + h * DV:NH * HP + (h + 1) * DV].astype(BF)

    vm = pl.BlockSpec(memory_space=pltpu.VMEM)
    return pl.pallas_call(
        body, name=name, in_specs=[vm, vm], out_specs=[vm, vm],
        out_shape=[jax.ShapeDtypeStruct((NDEV, QL, (DN + DR)), BF), jax.ShapeDtypeStruct((NDEV, KVL, DN + DV), BF)],
    )(dwq2, dwkv2)


def unshard_cols(g, *, name, tm=256):
    _, K, n = g.shape
    tm = _pick(K, tm, 16)

    def body(g_ref, o_ref):
        for k in range(NDEV):
            o_ref[:, k * n:(k + 1) * n] = g_ref[k]

    return pl.pallas_call(
        body, name=name, grid=(K // tm,), in_specs=[pl.BlockSpec((NDEV, tm, n), lambda i: (0, i, 0))],
        out_specs=pl.BlockSpec((tm, NDEV * n), lambda i: (i, 0)), out_shape=jax.ShapeDtypeStruct((K, NDEV * n), g.dtype),
        compiler_params=pltpu.CompilerParams(dimension_semantics=("parallel",)),
    )(g)


def shard_cols(w, *, name, tm=256):
    K, n8 = w.shape
    n = n8 // NDEV
    tm = _pick(K, tm, 16)

    def body(w_ref, o_ref):
        for k in range(NDEV):
            o_ref[k] = w_ref[:, k * n:(k + 1) * n]

    return pl.pallas_call(
        body, name=name, grid=(K // tm,), in_specs=[pl.BlockSpec((tm, n8), lambda i: (i, 0))],
        out_specs=pl.BlockSpec((NDEV, tm, n), lambda i: (0, i, 0)), out_shape=jax.ShapeDtypeStruct((NDEV, K, n), w.dtype),
        compiler_params=pltpu.CompilerParams(dimension_semantics=("parallel",)),
    )(w)


def _rope_tables():
    t = np.arange(T)
    row = (t // GRID_W).astype(np.float32)
    col = (t % GRID_W).astype(np.float32)
    axis_dim = DR // 2
    inv = (np.float32(ROPE_THETA) ** (-np.arange(0, axis_dim, 2, dtype=np.float32) / np.float32(axis_dim))).astype(np.float32)
    ar, ac = (row[:, None] * inv).astype(np.float32), (col[:, None] * inv).astype(np.float32)
    cosv = np.concatenate([np.cos(ar), np.cos(ar), np.cos(ac), np.cos(ac)], axis=1).astype(np.float32)
    sinv = np.concatenate([-np.sin(ar), np.sin(ar), -np.sin(ac), np.sin(ac)], axis=1).astype(np.float32)
    ck = np.zeros((TKV, HP), np.float32)
    sk = np.zeros((TKV, HP), np.float32)
    ck[T:, DN:DN + DR] = 1.0
    ck[:T, DN:DN + DR] = cosv
    sk[:T, DN:DN + DR] = sinv
    cq = np.zeros((T, HP), np.float32)
    cq[:, :DN] = 1.0
    cq[:, DN:DN + DR] = cosv
    return jnp.asarray(ck), jnp.asarray(sk), jnp.asarray(cq), jnp.asarray(sk[:T])


def _tie(a, token):
    return a if token is None else lax.optimization_barrier((a, token))[0]


def _local_step(x, ctx, tgt, mod_lat, mod_ctx, n1g, qg, kvg, n2g, fg, conv_w, conv_b, ffn_w, ffn_b, get_w, put_g):
    sh1, sc1, g1, sh2, sc2, g2 = [mod_lat[:, i * D:(i + 1) * D] for i in range(6)]
    csh1, csc1 = mod_ctx[:, 0:D], mod_ctx[:, D:2 * D]
    ck, sk, cq_t, sq_t = _rope_tables()
    qg_p = jnp.pad(qg, ((0, 0), (0, 512 - QL)))

    hcat = normmod_cat(ctx, x, n1g, csc1, csh1, sc1, sh1, name="normmod1")
    win = get_w("in", hcat)
    p = mm(hcat, win, M=T, name="in_proj")
    pc = mm(hcat, win, M=TC, N=512, a_off=(T, 0), b_off=(0, O_KV), name="in_proj_ctx")
    wq2, wkv2, wao, wco, wo = get_w("mid", p)
    kh, vh, ckv = kvprep(pc, p, kvg, wkv2, ck, sk, name="kvprep")
    qr, cq = qprep(p, qg_p, wq2, cq_t, sq_t, name="qprep")
    o = attn_fwd(qr, kh, vh, name="attn_fwd")
    z = convz(p, conv_w, conv_b, name="convz")
    ya = mm(o, wao, name="attn_out")
    yc = mm(z, wco, name="conv_out")
    merged = gate_merge(p, ya, yc, name="gate_merge")
    a_out = mm(merged, wo, name="o_proj")
    x1, h2 = resid_normmod(x, a_out, g1, n2g, sc2, sh2, name="resid_normmod2")
    wup, wdn = get_w("ffn", h2)
    u0 = mm(h2, wup, o_stack=True, tn=1408, name="up_proj")
    f = ffn_act(u0, ffn_w, ffn_b, name="ffn_act")
    dn = mm(f, wdn, name="down_proj")
    dx2, dd, dfg, loss = final_loss(x1, dn, g2, fg, tgt, name="final_loss")

    df = mm(dd, wdn, tb=True, name="down_proj_dx")
    dwdn = mm(f, dd, ta=True, out_dtype=BF, name="down_proj_dw")
    du0, dffn_w, dffn_b = ffn_act_bwd(u0, df, ffn_w, ffn_b, name="ffn_act_bwd")
    dwup = mm(h2, du0, ta=True, b_stack=True, out_dtype=BF, tn=1408, name="up_proj_dw")
    tok = put_g("ffn", dict(dwup=dwup, dwdn=dwdn))
    dh2 = mm(_tie(du0, tok), wup, tb=True, a_stack=True, name="up_proj_dx")
    dx1, da, st2 = normmod_bwd(x1, dh2, n2g, sc2, dx2, dn, g1, name="normmod2_bwd")

    dmerged = mm(da, wo, tb=True, name="o_proj_dx")
    dwo = mm(merged, da, ta=True, out_dtype=BF, name="o_proj_dw")
    dya, dyc, dp = gate_merge_bwd(p, ya, yc, dmerged, name="gate_merge_bwd")
    do = mm(dya, wao, tb=True, out_dtype=BF, name="attn_out_dx")
    dwao = mm(o, dya, ta=True, out_dtype=BF, name="attn_out_dw")
    dz = mm(dyc, wco, tb=True, name="conv_out_dx")
    dwco = mm(z, dyc, ta=True, out_dtype=BF, name="conv_out_dw")
    dp, dconv_w, dconv_b = convz_bwd(p, dz, conv_w, conv_b, dp, name="convz_bwd")
    dq, dk, dv = attn_bwd(qr, kh, vh, do, name="attn_bwd")
    dp, dq2, dqg = qprep_bwd(p, dq, qg_p, wq2, cq_t, sq_t, dp, name="qprep_bwd")
    dwq2 = mm(cq, dq2, ta=True, name="q_up_dw")
    dp, dpc, dkv2, dkvg = kvprep_bwd(pc, p, dk, dv, kvg, wkv2, ck, sk, dp, name="kvprep_bwd")
    dwkv2 = mm(ckv, dkv2, ta=True, name="kv_up_dw")
    tok = put_g("mid", dict(dwq2=dwq2, dwkv2=dwkv2, dwao=dwao, dwco=dwco, dwo=dwo))

    dp = _tie(dp, tok)
    dwin = mm(hcat, dp, ta=True, K=T, name="in_proj_dw")
    dwin_c = mm(hcat, dpc, ta=True, K=TC, a_off=(T, 0), name="in_proj_ctx_dw")
    tok = put_g("in", dict(dwin=dwin, dwin_c=dwin_c))
    dh = mm(_tie(dp, tok), win, tb=True, name="in_proj_dx")
    dhc = mm(dpc, win, tb=True, N=D, K=512, b_off=(0, O_KV), name="in_proj_ctx_dx")
    dx, _, st1 = normmod_bwd(x, dh, n1g, sc1, dx1, a_out, g1, name="normmod1_bwd")
    stc = normmod_bwd(ctx, dhc, n1g, csc1, None, None, None, name="normmod1_ctx_bwd")

    zrow = jnp.zeros((1, D), F32)
    dmod_lat = jnp.concatenate([st1[0:1], st1[1:2], st1[3:4], st2[0:1], st2[1:2], st2[3:4]], axis=1)
    dmod_ctx = jnp.concatenate([stc[0:1], stc[1:2], zrow, zrow, zrow, zrow], axis=1)
    return dict(
        loss=loss, dx=dx, dmod_lat=dmod_lat, dmod_ctx=dmod_ctx,
        dn1g=st1[2:3] + stc[2:3], dqg=dqg, dkvg=dkvg, dn2g=st2[2:3], dfg=dfg,
        dconv_w=dconv_w, dconv_b=dconv_b, dffn_w=dffn_w, dffn_b=dffn_b)


def _me():
    x, y, c = lax.axis_index("x"), lax.axis_index("y"), lax.axis_index("c")
    return x, y, c, 4 * x + 2 * y + c


def _peer(x, y, c, k):
    px = 1 - x if k & 4 else x
    py = 1 - y if k & 2 else y
    pc = 1 - c if k & 1 else c
    return (px, py, pc), 4 * px + 2 * py + pc


def _exchange_tiles(src_of_peer, buf, send_sem, recv_sem):
    x, y, c, me = _me()
    for k in range(1, NDEV):
        dev, lin = _peer(x, y, c, k)
        pltpu.make_async_remote_copy(src_ref=src_of_peer(lin), dst_ref=buf.at[me], send_sem=send_sem, recv_sem=recv_sem,
                                     device_id=dev, device_id_type=MESH).start()
    seven = buf.at[pl.ds(0, NDEV - 1)]
    pltpu.make_async_remote_copy(src_ref=seven, dst_ref=seven, send_sem=send_sem, recv_sem=recv_sem,
                                 device_id=(x, y, c), device_id_type=MESH).wait()


def _silu(z):
    return z * jax.nn.sigmoid(z)


def ada_fwd(c, c_ctx, ffn_w, conv_w, w_shard, b_shard, *, name):
    nsh = w_shard.shape[1]

    def body(c_ref, cc_ref, fw_ref, cw_ref, w_ref, b_ref, s_ref, m_ref, mine, res, sems):
        x, y, c, me = _me()
        mine[0:1, :] = _silu(c_ref[...])
        mine[1:2, :] = _silu(cc_ref[...])
        mine[2:5, :] = fw_ref[...]
        mine[5:8, :] = cw_ref[...]
        s_ref[me] = mine[...]
        _exchange_tiles(lambda lin: mine, s_ref, sems.at[0], sems.at[1])
        sall = s_ref[...].reshape(NDEV * 8, D).astype(BF)
        r = jnp.dot(sall, w_ref[...].astype(BF), preferred_element_type=F32) + b_ref[...]
        res[...] = r.reshape(NDEV, 8, nsh)
        m_ref[me] = res[me]
        _exchange_tiles(lambda lin: res.at[lin], m_ref, sems.at[2], sems.at[3])

    vm = pl.BlockSpec(memory_space=pltpu.VMEM)
    return pl.pallas_call(
        body, name=name, in_specs=[vm] * 6, out_specs=[vm, vm],
        out_shape=[jax.ShapeDtypeStruct((NDEV, 8, D), F32), jax.ShapeDtypeStruct((NDEV, 8, nsh), F32)],
        scratch_shapes=[pltpu.VMEM((8, D), F32), pltpu.VMEM((NDEV, 8, nsh), F32), pltpu.SemaphoreType.DMA((4,))],
    )(c, c_ctx, ffn_w, conv_w, w_shard, b_shard)


P_DML, P_DMC, P_N1, P_QG, P_KVG, P_CB, P_N2, P_FB, P_FG, P_CW, P_FW, P_LOSS, P_ROWS = 0, 6, 12, 13, 14, 15, 16, 17, 23, 24, 27, 45, 48
FROWS = 3


def sync_small(r, *, name):
    ins = [r["dmod_lat"], r["dmod_ctx"], r["dn1g"], r["dqg"], r["dkvg"], r["dconv_b"], r["dn2g"], r["dffn_b"], r["dfg"],
           r["dconv_w"], r["dffn_w"], r["loss"]]

    def put_wide(p, row0, row, n):
        for j in range(-(-n // D)):
            w = min(D, n - j * D)
            p[row0 + j:row0 + j + 1, 0:w] = row[:, j * D:j * D + w]

    def body(dml, dmc, n1, qg, kvg, cb, n2, fb, fg, cw, fw, loss, a_ref, sum_ref, p, sems):
        x, y, c, me = _me()
        p[...] = jnp.zeros_like(p)
        put_wide(p, P_DML, dml, 6 * D)
        put_wide(p, P_DMC, dmc, 6 * D)
        put_wide(p, P_N1, n1, D)
        put_wide(p, P_QG, qg, 512)
        put_wide(p, P_KVG, kvg, KVL)
        put_wide(p, P_CB, cb, CONV)
        put_wide(p, P_N2, n2, D)
        put_wide(p, P_FG, fg, D)
        put_wide(p, P_LOSS, loss, 128)
        for s in range(2):
            put_wide(p, P_FB + FROWS * s, fb.at[s], DFF)
        for k in range(3):
            put_wide(p, P_CW + k, cw.at[k:k + 1], CONV)
            for s in range(2):
                put_wide(p, P_FW + FROWS * (2 * k + s), fw.at[s, k:k + 1], DFF)
        a_ref[me] = p[...]
        _exchange_tiles(lambda lin: p, a_ref, sems.at[0], sems.at[1])
        acc = a_ref[0]
        for k in range(1, NDEV):
            acc = acc + a_ref[k]
        sum_ref[...] = acc

    vm = pl.BlockSpec(memory_space=pltpu.VMEM)
    return pl.pallas_call(
        body, name=name, in_specs=[vm] * len(ins), out_specs=[vm, vm],
        out_shape=[jax.ShapeDtypeStruct((NDEV, P_ROWS, D), F32), jax.ShapeDtypeStruct((P_ROWS, D), F32)],
        scratch_shapes=[pltpu.VMEM((P_ROWS, D), F32), pltpu.SemaphoreType.DMA((2,))],
    )(*ins)


def ada_bwd(s_all, dml, dmc, w_shard, c_ctx, *, name):
    nsh = w_shard.shape[1]

    def body(s_ref, dml_ref, dmc_ref, w_ref, c_ref, dw_ref, gc_ref, s16, dm16, part, buf, sems):
        x, y, c, me = _me()
        s16[...] = jnp.zeros_like(s16)
        dm16[...] = jnp.zeros_like(dm16)
        for k in range(NDEV):
            s16[k:k + 1, :] = s_ref[k, 0:1, :]
        s16[8:9, :] = s_ref[0, 1:2, :]
        dm16[0:8, :] = dml_ref[...]
        dm16[8:9, :] = dmc_ref[...]
        dw_ref[...] = lax.dot_general(s16[...].astype(BF), dm16[...].astype(BF), (((0,), (0,)), ((), ())),
                                      preferred_element_type=F32)
        part[...] = lax.dot_general(dm16[8:16, :].astype(BF), w_ref[...].astype(BF), (((1,), (1,)), ((), ())),
                                    preferred_element_type=F32)
        buf[me] = part[...]
        _exchange_tiles(lambda lin: part, buf, sems.at[0], sems.at[1])
        acc = buf[0]
        for k in range(1, NDEV):
            acc = acc + buf[k]
        z = c_ref[...]
        sg = jax.nn.sigmoid(z)
        gc_ref[...] = acc * (sg * (1.0 + z * (1.0 - sg)))

    vm = pl.BlockSpec(memory_space=pltpu.VMEM)
    return pl.pallas_call(
        body, name=name, in_specs=[vm] * 5, out_specs=[vm, vm],
        out_shape=[jax.ShapeDtypeStruct((D, nsh), F32), jax.ShapeDtypeStruct((8, D), F32)],
        scratch_shapes=[pltpu.VMEM((16, D), F32), pltpu.VMEM((16, nsh), F32), pltpu.VMEM((8, D), F32),
                        pltpu.VMEM((NDEV, 8, D), F32), pltpu.SemaphoreType.DMA((2,))],
    )(s_all, dml, dmc, w_shard, c_ctx)


HBM_SPEC = pl.BlockSpec(memory_space=pltpu.HBM)
SEM_SPEC = pl.BlockSpec(memory_space=pltpu.SEMAPHORE)
EFFECT = pltpu.SideEffectType.DATAFLOW_SIDE_EFFECTING


def _exchange_copies(srcs, lands, send, recv, per_peer):
    x, y, c, me = _me()
    cps = []
    for t in range(len(srcs)):
        for k in range(1, NDEV):
            dev, lin = _peer(x, y, c, k)
            cps.append(pltpu.make_async_remote_copy(
                src_ref=srcs[t].at[lin] if per_peer else srcs[t], dst_ref=lands[t].at[me],
                send_sem=send.at[7 * t + k - 1], recv_sem=recv.at[7 * t + k - 1], device_id=dev, device_id_type=MESH))
    return cps


def exchange_start(srcs, *, per_peer, name):
    nt = len(srcs)
    land_shapes = [(a.shape if per_peer else (NDEV,) + a.shape) for a in srcs]

    def body(*refs):
        src, land, send, recv = refs[:nt], refs[nt:2 * nt], refs[2 * nt], refs[2 * nt + 1]
        for cp in _exchange_copies(src, land, send, recv, per_peer):
            cp.start()
        refs[-1][...] = jnp.zeros_like(refs[-1])

    hb = lambda a: pltpu.with_memory_space_constraint(a, pltpu.HBM)
    outs = pl.pallas_call(
        body, name=name,
        out_shape=(pltpu.SemaphoreType.DMA((7 * nt,)), pltpu.SemaphoreType.DMA((7 * nt,)),
                   *[pltpu.HBM(a.shape, a.dtype) for a in srcs], *[pltpu.HBM(s, a.dtype) for s, a in zip(land_shapes, srcs)],
                   jax.ShapeDtypeStruct((8, 128), F32)),
        in_specs=[HBM_SPEC] * (2 * nt),
        out_specs=(SEM_SPEC, SEM_SPEC, *([HBM_SPEC] * (2 * nt)), pl.BlockSpec(memory_space=pltpu.VMEM)),
        input_output_aliases={i: 2 + i for i in range(2 * nt)},
        compiler_params=pltpu.CompilerParams(has_side_effects=EFFECT),
    )(*[hb(a) for a in srcs], *[hb(lax.empty(s, a.dtype)) for s, a in zip(land_shapes, srcs)])
    return dict(send=outs[0], recv=outs[1], src=list(outs[2:2 + nt]), land=list(outs[2 + nt:2 + 2 * nt]), token=outs[-1],
                per_peer=per_peer)


def exchange_wait(h, after, *, name):
    nt = len(h["src"])
    per_peer = h["per_peer"]

    def body(*refs):
        src, land, send, recv = refs[:nt], refs[nt:2 * nt], refs[2 * nt], refs[2 * nt + 1]
        for cp in _exchange_copies(src, land, send, recv, per_peer):
            cp.wait_send()
            cp.wait_recv()

    outs = pl.pallas_call(
        body, name=name,
        out_shape=(*[pltpu.HBM(a.shape, a.dtype) for a in h["src"]], *[pltpu.HBM(a.shape, a.dtype) for a in h["land"]]),
        in_specs=[HBM_SPEC] * (2 * nt) + [SEM_SPEC, SEM_SPEC, pl.BlockSpec(memory_space=pl.ANY)],
        out_specs=tuple([HBM_SPEC] * (2 * nt)),
        input_output_aliases={i: i for i in range(2 * nt)},
        compiler_params=pltpu.CompilerParams(has_side_effects=EFFECT),
    )(*h["src"], *h["land"], h["send"], h["recv"], after)
    return list(outs[:nt]), list(outs[nt:])


def _gathered(h, after, me, *, name):
    srcs, lands = exchange_wait(h, after, name=name)
    return [lax.dynamic_update_slice(l, s[None], (me,) + (0,) * s.ndim) for s, l in zip(srcs, lands)]


def _scattered(h, after, me, *, name):
    srcs, lands = exchange_wait(h, after, name=name)
    return [lax.dynamic_update_slice(l, lax.dynamic_slice_in_dim(s, me, 1, 0), (me,) + (0,) * (s.ndim - 1))
            for s, l in zip(srcs, lands)]


def _adamw_math(w, g, m, v):
    nm = B1 * m + (1.0 - B1) * g
    nv = B2 * v + (1.0 - B2) * (g * g)
    m_hat = nm / (1.0 - B1 ** STEP)
    v_hat = nv / (1.0 - B2 ** STEP)
    return -LR * (m_hat / (jnp.sqrt(v_hat) + AEPS) + WD * w), nm, nv


def adamw_many(ws, gs, ms, vs, *, name):
    n = len(ws)

    def body(*refs):
        for k in range(n):
            d, nm, nv = _adamw_math(refs[k][...], refs[n + k][...], refs[2 * n + k][...], refs[3 * n + k][...])
            refs[4 * n + k][...] = d
            refs[5 * n + k][...] = nm
            refs[6 * n + k][...] = nv

    vm = pl.BlockSpec(memory_space=pltpu.VMEM)
    sh = [jax.ShapeDtypeStruct(w.shape, F32) for w in ws]
    outs = pl.pallas_call(body, name=name, in_specs=[vm] * (4 * n), out_specs=[vm] * (3 * n), out_shape=sh * 3,
                          )(*ws, *gs, *ms, *vs)
    return outs[:n], outs[n:2 * n], outs[2 * n:]


def adamw(w, g, m, v, *, name, tr=256):
    R, C = w.shape
    tr = _pick(R, tr, 8)

    def body(w_ref, g_ref, m_ref, v_ref, d_ref, nm_ref, nv_ref):
        d_ref[...], nm_ref[...], nv_ref[...] = _adamw_math(w_ref[...], g_ref[...], m_ref[...], v_ref[...])

    blk = pl.BlockSpec((tr, C), lambda i: (i, 0))
    sh = jax.ShapeDtypeStruct((R, C), F32)
    return pl.pallas_call(
        body, name=name, grid=(R // tr,), in_specs=[blk, blk, blk, blk], out_specs=[blk, blk, blk],
        out_shape=[sh, sh, sh], compiler_params=pltpu.CompilerParams(dimension_semantics=("parallel",)),
    )(w, g, m, v)


def adamw_slots(w, slots, m, v, *, name, tr=256):
    R, C = w.shape
    tr = _pick(R, tr, 16)

    def body(w_ref, s_ref, m_ref, v_ref, g_ref, d_ref, nm_ref, nv_ref):
        g = s_ref[0].astype(F32)
        for k in range(1, NDEV):
            g = g + s_ref[k].astype(F32)
        g_ref[...] = g
        d_ref[...], nm_ref[...], nv_ref[...] = _adamw_math(w_ref[...], g, m_ref[...], v_ref[...])

    blk = pl.BlockSpec((tr, C), lambda i: (i, 0))
    sh = jax.ShapeDtypeStruct((R, C), F32)
    return pl.pallas_call(
        body, name=name, grid=(R // tr,), in_specs=[blk, pl.BlockSpec((NDEV, tr, C), lambda i: (0, i, 0)), blk, blk],
        out_specs=[blk, blk, blk, blk], out_shape=[sh, sh, sh, sh],
        compiler_params=pltpu.CompilerParams(dimension_semantics=("parallel",)),
    )(w, slots, m, v)


def _padc(a, n=D):
    return jnp.pad(a, ((0, 0), (0, n - a.shape[1])))


_BIG = ["w_in", "w_uq", "w_ukv", "w_attn_out", "w_conv_out", "w_o", "w_up", "w_down"]


def kernel(x, c, ctx, c_ctx, w_ada, b_ada, norm1_g, w_in, q_norm_g, kv_norm_g, w_uq, w_ukv, conv_w, conv_b, w_attn_out, w_conv_out, w_o, norm2_g, w_up, ffn_conv_w, ffn_conv_b, w_down, final_g, loss_target, m_c_ctx, m_w_ada, m_b_ada, m_norm1_g, m_w_in, m_q_norm_g, m_kv_norm_g, m_w_uq, m_w_ukv, m_conv_w, m_conv_b, m_w_attn_out, m_w_conv_out, m_w_o, m_norm2_g, m_w_up, m_ffn_conv_w, m_ffn_conv_b, m_w_down, m_final_g, v_c_ctx, v_w_ada, v_b_ada, v_norm1_g, v_w_in, v_q_norm_g, v_kv_norm_g, v_w_uq, v_w_ukv, v_conv_w, v_conv_b, v_w_attn_out, v_w_conv_out, v_w_o, v_norm2_g, v_w_up, v_ffn_conv_w, v_ffn_conv_b, v_w_down, v_final_g):
    me = 4 * lax.axis_index("x") + 2 * lax.axis_index("y") + lax.axis_index("c")
    W = dict(c_ctx=c_ctx, w_ada=w_ada, b_ada=b_ada, norm1_g=norm1_g, w_in=w_in, q_norm_g=q_norm_g, kv_norm_g=kv_norm_g,
             w_uq=w_uq, w_ukv=w_ukv, conv_w=conv_w, conv_b=conv_b, w_attn_out=w_attn_out, w_conv_out=w_conv_out, w_o=w_o,
             norm2_g=norm2_g, w_up=w_up, ffn_conv_w=ffn_conv_w, ffn_conv_b=ffn_conv_b, w_down=w_down, final_g=final_g)
    M = dict(c_ctx=m_c_ctx, w_ada=m_w_ada, b_ada=m_b_ada, norm1_g=m_norm1_g, w_in=m_w_in, q_norm_g=m_q_norm_g,
             kv_norm_g=m_kv_norm_g, w_uq=m_w_uq, w_ukv=m_w_ukv, conv_w=m_conv_w, conv_b=m_conv_b, w_attn_out=m_w_attn_out,
             w_conv_out=m_w_conv_out, w_o=m_w_o, norm2_g=m_norm2_g, w_up=m_w_up, ffn_conv_w=m_ffn_conv_w,
             ffn_conv_b=m_ffn_conv_b, w_down=m_w_down, final_g=m_final_g)
    V = dict(c_ctx=v_c_ctx, w_ada=v_w_ada, b_ada=v_b_ada, norm1_g=v_norm1_g, w_in=v_w_in, q_norm_g=v_q_norm_g,
             kv_norm_g=v_kv_norm_g, w_uq=v_w_uq, w_ukv=v_w_ukv, conv_w=v_conv_w, conv_b=v_conv_b, w_attn_out=v_w_attn_out,
             w_conv_out=v_w_conv_out, w_o=v_w_o, norm2_g=v_norm2_g, w_up=v_w_up, ffn_conv_w=v_ffn_conv_w,
             ffn_conv_b=v_ffn_conv_b, w_down=v_w_down, final_g=v_final_g)
    names = list(W)
    as2d = lambda a: a.reshape(1, -1) if a.ndim == 1 else a.reshape(a.shape[-2], a.shape[-1])
    W2 = {k: as2d(a) for k, a in W.items()}
    M2 = {k: as2d(a) for k, a in M.items()}
    V2 = {k: as2d(a) for k, a in V.items()}
    nsh = W2["w_ada"].shape[1]

    stage_w = {"in": ["w_in"], "mid": ["w_uq", "w_ukv", "w_attn_out", "w_conv_out", "w_o"], "ffn": ["w_up", "w_down"]}
    ag = {st: exchange_start([W2[nm].astype(BF) for nm in nms], per_peer=False, name="ag_start_" + st)
          for st, nms in stage_w.items()}

    b_sh = lax.dynamic_slice(W2["b_ada"], (0, me * nsh), (1, nsh))
    c_in = c
    for st in stage_w:
        c_in = _tie(c_in, ag[st]["token"])
    s_all, m_all = ada_fwd(c_in, W2["c_ctx"], _padc(W2["ffn_conv_w"]), _padc(W2["conv_w"]), W2["w_ada"], b_sh,
                           name="ada_fwd")
    mod_lat = m_all[:, 0, :].reshape(1, 6 * D)
    mod_ctx = m_all[:, 1, :].reshape(1, 6 * D)
    ffn_w_full = s_all[:, 2:5, :2 * DFF // NDEV].transpose(1, 0, 2).reshape(3, 2 * DFF)
    conv_w_full = s_all[:, 5:8, :CONV // NDEV].transpose(1, 0, 2).reshape(3, CONV)

    def get_w(stage, after):
        g = dict(zip(stage_w[stage], _gathered(ag[stage], after, me, name="ag_wait_" + stage)))
        if stage == "in":
            return build_win(g["w_in"], name="build_win")
        if stage == "mid":
            wq2, wkv2 = build_wq_wkv(g["w_uq"], g["w_ukv"], name="build_wq_wkv")
            return (wq2, wkv2, unshard_cols(g["w_attn_out"], name="unshard_w_attn_out"),
                    unshard_cols(g["w_conv_out"], name="unshard_w_conv_out"), g["w_o"].reshape(D, D))
        return unshard_cols(g["w_up"], name="unshard_w_up"), g["w_down"].reshape(DFF, D)

    rs = {}

    def put_g(stage, g):
        if stage == "in":
            parts = [shard_win_grad(g["dwin"], g["dwin_c"], name="shard_win_grad")]
        elif stage == "mid":
            pq, pkv = shard_wq_wkv_grad(g["dwq2"], g["dwkv2"], name="shard_wq_wkv_grad")
            parts = [pq, pkv, shard_cols(g["dwao"], name="shard_w_attn_out"),
                     shard_cols(g["dwco"], name="shard_w_conv_out"), g["dwo"].reshape(NDEV, D // NDEV, D)]
        else:
            parts = [shard_cols(g["dwup"], name="shard_w_up"), g["dwdn"].reshape(NDEV, DFF // NDEV, D)]
        rs[stage] = exchange_start(parts, per_peer=True, name="rs_start_" + stage)
        return rs[stage]["token"]

    r = _local_step(x[0], ctx[0], loss_target[0], mod_lat, mod_ctx, W2["norm1_g"], W2["q_norm_g"], W2["kv_norm_g"],
                    W2["norm2_g"], W2["final_g"], conv_w_full, W2["conv_b"], ffn_w_full, W2["ffn_conv_b"], get_w, put_g)

    a_buf, ssum = sync_small(r, name="sync_small")
    G = {}
    loss = ssum[P_LOSS, 0]
    G["norm1_g"] = ssum[P_N1:P_N1 + 1]
    G["q_norm_g"] = ssum[P_QG:P_QG + 1, :QL]
    G["kv_norm_g"] = ssum[P_KVG:P_KVG + 1, :KVL]
    G["conv_b"] = ssum[P_CB:P_CB + 1, :CONV]
    G["norm2_g"] = ssum[P_N2:P_N2 + 1]
    G["ffn_conv_b"] = ssum[P_FB:P_FB + 2 * FROWS].reshape(1, 2, FROWS * D)[:, :, :DFF].reshape(1, 2 * DFF)
    G["final_g"] = ssum[P_FG:P_FG + 1]
    G["conv_w"] = lax.dynamic_slice(ssum[P_CW:P_CW + 3, :CONV], (0, me * (CONV // NDEV)), (3, CONV // NDEV))
    fw_full = ssum[P_FW:P_FW + 6 * FROWS].reshape(3, 2, FROWS * D)[:, :, :DFF].reshape(3, 2 * DFF)
    G["ffn_conv_w"] = lax.dynamic_slice(fw_full, (0, me * (2 * DFF // NDEV)), (3, 2 * DFF // NDEV))
    G["b_ada"] = (ssum[P_DML:P_DML + 6] + ssum[P_DMC:P_DMC + 6]).reshape(1, 6 * D)

    dml = lax.dynamic_slice(a_buf[:, P_DML:P_DML + 6, :].reshape(NDEV, 6 * D), (0, me * nsh), (NDEV, nsh))
    dmc = lax.dynamic_slice(ssum[P_DMC:P_DMC + 6].reshape(1, 6 * D), (0, me * nsh), (1, nsh))
    G["w_ada"], gcc = ada_bwd(s_all, dml, dmc, W2["w_ada"], W2["c_ctx"], name="ada_bwd")
    G["c_ctx"] = gcc[0:1]

    DL, NM, NV = {}, {}, {}
    DL["w_ada"], NM["w_ada"], NV["w_ada"] = adamw(W2["w_ada"], G["w_ada"], M2["w_ada"], V2["w_ada"], name="adamw_w_ada")
    small = ["c_ctx", "b_ada", "norm1_g", "q_norm_g", "kv_norm_g", "conv_b", "norm2_g", "ffn_conv_b", "final_g", "conv_w",
             "ffn_conv_w"]
    ds, nms, nvs = adamw_many([W2[k] for k in small], [G[k] for k in small], [M2[k] for k in small],
                              [V2[k] for k in small], name="adamw_small")
    for k, nm in enumerate(small):
        DL[nm], NM[nm], NV[nm] = ds[k], nms[k], nvs[k]
    after = DL["w_ada"]
    for st in ("ffn", "mid", "in"):
        slots = _scattered(rs[st], after, me, name="rs_wait_" + st)
        for nm, sl in zip(stage_w[st], slots):
            G[nm], DL[nm], NM[nm], NV[nm] = adamw_slots(W2[nm], sl, M2[nm], V2[nm], name="adamw_" + nm)
            after = DL[nm]

    outs = [loss, r["dx"][None]]
    for grp in (G, DL, NM, NV):
        outs += [grp[nm].reshape(W[nm].shape) for nm in names]
    return tuple(outs)
```

```python
import functools
import numpy as np
import jax
import jax.numpy as jnp
from jax import lax
from jax.experimental import pallas as pl
from jax.experimental.pallas import tpu as pltpu

F32 = jnp.float32
BF = jnp.bfloat16
MESH = pl.DeviceIdType.MESH

D = 1024
T = 2048
TC = 256
TKV = T + TC
GRID_W = 64
NH = 8
DN = 64
DR = 32
DV = 64
QL = 384
KVL = 256
CONV = 512
DFF = 2816
EPS = 1e-6
ROPE_THETA = 10000.0
SCALE = (DN + DR) ** -0.5
NDEV = 8
HP = 128

O_GA, O_GC, O_KV, O_Q, O_CV = 0, 1024, 2048, 2560, 3072
NIN = 4608
CVB = 256
N_IN = 4256
SH_IN = N_IN // NDEV

LR, B1, B2, AEPS, WD, STEP = 0.001, 0.9, 0.999, 1e-08, 0.01, 10


def _pick(n, target, mult=128):
    best = None
    for d in range(mult, min(n, target) + 1, mult):
        if n % d == 0:
            best = d
    return best if best is not None else n


def _swap_start(g):
    return 8 * (g ^ 1)


def mm(a, b, *, ta=False, tb=False, out_dtype=F32, name, tm=512, tn=512, tk=2048, M=None, N=None, K=None,
       a_off=(0, 0), b_off=(0, 0), a_stack=False, b_stack=False, o_stack=False, dep=None):
    def dims(arr, stack):
        return (arr.shape[1], 2 * arr.shape[2]) if stack else arr.shape

    ar, ac = dims(a, a_stack)
    br, bc = dims(b, b_stack)
    M = M or ((ac if ta else ar) - a_off[1 if ta else 0])
    K = K or ((ar if ta else ac) - a_off[0 if ta else 1])
    N = N or ((br if tb else bc) - b_off[0 if tb else 1])
    tm = _pick(M, tm, 128 if ta else 16)
    tn = _pick(N // 2 if (o_stack or (b_stack and not tb)) else N, tn, 128)
    tk = _pick(K // 2 if ((a_stack and not ta) or (b_stack and tb)) else K, tk, 128)
    nk = K // tk
    ca = 0 if ta else 1
    cb = 1 if tb else 0

    def body(a_ref, b_ref, *rest):
        o_ref, acc = rest[-2:]
        k = pl.program_id(2)
        part = lax.dot_general(a_ref[...].astype(BF), b_ref[...].astype(BF),
                               (((ca,), (cb,)), ((), ())), preferred_element_type=F32)
        if nk == 1:
            o_ref[...] = part.astype(o_ref.dtype)
        else:
            @pl.when(k == 0)
            def _():
                acc[...] = part

            @pl.when(k > 0)
            def _():
                acc[...] += part

            @pl.when(k == nk - 1)
            def _():
                o_ref[...] = acc[...].astype(o_ref.dtype)

    def spec(blk, rc, off, stack, ncols):
        assert off[0] % blk[0] == 0 and off[1] % blk[1] == 0, (name, blk, off)
        ro, co = off[0] // blk[0], off[1] // blk[1]
        if not stack:
            return pl.BlockSpec(blk, lambda i, j, k: (rc(i, j, k)[0] + ro, rc(i, j, k)[1] + co))
        nhb = ncols // 2 // blk[1]
        return pl.BlockSpec((None,) + blk,
                            lambda i, j, k: ((rc(i, j, k)[1] + co) // nhb, rc(i, j, k)[0] + ro, (rc(i, j, k)[1] + co) % nhb))

    a_spec = spec((tk, tm), lambda i, j, k: (k, i), a_off, a_stack, ac) if ta else \
        spec((tm, tk), lambda i, j, k: (i, k), a_off, a_stack, ac)
    b_spec = spec((tn, tk), lambda i, j, k: (j, k), b_off, b_stack, bc) if tb else \
        spec((tk, tn), lambda i, j, k: (k, j), b_off, b_stack, bc)
    o_spec = spec((tm, tn), lambda i, j, k: (i, j), (0, 0), o_stack, N)
    o_shape = (2, M, N // 2) if o_stack else (M, N)
    deps = [] if dep is None else [dep]
    return pl.pallas_call(
        body, name=name, grid=(M // tm, N // tn, nk),
        in_specs=[a_spec, b_spec] + [pl.BlockSpec(memory_space=pl.ANY)] * len(deps),
        out_specs=o_spec, out_shape=jax.ShapeDtypeStruct(o_shape, out_dtype),
        scratch_shapes=[pltpu.VMEM((tm, tn) if nk > 1 else (8, 128), F32)],
        compiler_params=pltpu.CompilerParams(dimension_semantics=("parallel", "parallel", "arbitrary")),
    )(a, b, *deps)


def _row(width):
    return pl.BlockSpec((1, width), lambda *_: (0, 0))


NLAT = T // TC


def normmod_cat(ctx, x, g, csc, csh, sc, sh, *, name, tm=256):
    assert tm == TC

    def body(c_ref, x_ref, g_ref, csc_ref, csh_ref, sc_ref, sh_ref, h_ref):
        last = pl.program_id(0) == NLAT
        xv = jnp.where(last, c_ref[...], x_ref[...])
        scv = jnp.where(last, csc_ref[...], sc_ref[...])
        shv = jnp.where(last, csh_ref[...], sh_ref[...])
        r = lax.rsqrt(jnp.mean(xv * xv, axis=-1, keepdims=True) + EPS)
        h_ref[...] = ((xv * r * g_ref[...]) * (1.0 + scv) + shv).astype(BF)

    return pl.pallas_call(
        body, name=name, grid=(TKV // tm,),
        in_specs=[pl.BlockSpec((tm, D), lambda i: (0, 0)), pl.BlockSpec((tm, D), lambda i: (jnp.minimum(i, NLAT - 1), 0)),
                  _row(D), _row(D), _row(D), _row(D), _row(D)],
        out_specs=pl.BlockSpec((tm, D), lambda i: (i, 0)), out_shape=jax.ShapeDtypeStruct((TKV, D), BF),
        compiler_params=pltpu.CompilerParams(dimension_semantics=("parallel",)),
    )(ctx, x, g, csc, csh, sc, sh)


def resid_normmod(x, a, gate, g, sc, sh, *, name, tm=256):
    R = x.shape[0]

    def body(x_ref, a_ref, gate_ref, g_ref, sc_ref, sh_ref, x1_ref, h_ref):
        xv = x_ref[...] + gate_ref[...] * a_ref[...]
        x1_ref[...] = xv
        r = lax.rsqrt(jnp.mean(xv * xv, axis=-1, keepdims=True) + EPS)
        h_ref[...] = ((xv * r * g_ref[...]) * (1.0 + sc_ref[...]) + sh_ref[...]).astype(BF)

    blk = pl.BlockSpec((tm, D), lambda i: (i, 0))
    return pl.pallas_call(
        body, name=name, grid=(R // tm,), in_specs=[blk, blk, _row(D), _row(D), _row(D), _row(D)],
        out_specs=[blk, blk],
        out_shape=[jax.ShapeDtypeStruct((R, D), F32), jax.ShapeDtypeStruct((R, D), BF)],
        compiler_params=pltpu.CompilerParams(dimension_semantics=("parallel",)),
    )(x, a, gate, g, sc, sh)


def kvprep(pc, p, kvg, wkv2, ck, sk, *, name, tm=256):
    assert tm == TC
    nb = TKV // tm
    kvcol = O_KV // 512

    def body(pc_ref, p_ref, g_ref, w_ref, ck_ref, sk_ref, k_ref, v_ref, ckv_ref):
        i = pl.program_id(0)
        t = jnp.where(i == NLAT, pc_ref[...], p_ref[...])
        pk = t[:, :KVL]
        r = lax.rsqrt(jnp.mean(pk * pk, axis=-1, keepdims=True) + EPS)
        ckv = (pk * r * g_ref[...]).astype(BF)
        ckv_ref[...] = ckv
        kv2 = jnp.dot(ckv, w_ref[...], preferred_element_type=F32)
        krr = t[:, KVL:KVL + HP] * ck_ref[...] + t[:, KVL + HP:KVL + 2 * HP] * sk_ref[...]
        k_ref[...] = (kv2[:, :NH * HP] + jnp.concatenate([krr] * NH, axis=1)).astype(BF)
        v_ref[...] = kv2[:, NH * HP:].astype(BF)

    return pl.pallas_call(
        body, name=name, grid=(nb,),
        in_specs=[pl.BlockSpec((tm, 512), lambda i: (0, 0)),
                  pl.BlockSpec((tm, 512), lambda i: (jnp.minimum(i, NLAT - 1), kvcol)),
                  _row(KVL), pl.BlockSpec((KVL, NH * HP + NH * DV), lambda i: (0, 0)),
                  pl.BlockSpec((tm, HP), lambda i: (i, 0)), pl.BlockSpec((tm, HP), lambda i: (i, 0))],
        out_specs=[pl.BlockSpec((tm, NH * HP), lambda i: (i, 0)), pl.BlockSpec((tm, NH * DV), lambda i: (i, 0)),
                   pl.BlockSpec((tm, KVL), lambda i: (i, 0))],
        out_shape=[jax.ShapeDtypeStruct((TKV, NH * HP), BF), jax.ShapeDtypeStruct((TKV, NH * DV), BF),
                   jax.ShapeDtypeStruct((TKV, KVL), BF)],
        compiler_params=pltpu.CompilerParams(dimension_semantics=("parallel",)),
    )(pc, p, kvg, wkv2, ck, sk)


def qprep(p, qg, wq2, cq_t, sq_t, *, name, tm=256):
    qcol = O_Q // 512

    def body(p_ref, g_ref, w_ref, c_ref, s_ref, q_ref, cq_ref):
        pq = p_ref[...]
        r = lax.rsqrt(jnp.sum(pq * pq, axis=-1, keepdims=True) * (1.0 / QL) + EPS)
        cq = (pq * r * g_ref[...]).astype(BF)
        cq_ref[...] = cq
        q2 = jnp.dot(cq, w_ref[...], preferred_element_type=F32)
        cc = jnp.concatenate([c_ref[...]] * NH, axis=1)
        ss = jnp.concatenate([s_ref[...]] * NH, axis=1)
        q_ref[...] = (q2[:, :NH * HP] * cc + q2[:, NH * HP:] * ss).astype(BF)

    return pl.pallas_call(
        body, name=name, grid=(T // tm,),
        in_specs=[pl.BlockSpec((tm, 512), lambda i: (i, qcol)), _row(512),
                  pl.BlockSpec((512, 2 * NH * HP), lambda i: (0, 0)),
                  pl.BlockSpec((tm, HP), lambda i: (i, 0)), pl.BlockSpec((tm, HP), lambda i: (i, 0))],
        out_specs=[pl.BlockSpec((tm, NH * HP), lambda i: (i, 0)), pl.BlockSpec((tm, 512), lambda i: (i, 0))],
        out_shape=[jax.ShapeDtypeStruct((T, NH * HP), BF), jax.ShapeDtypeStruct((T, 512), BF)],
        compiler_params=pltpu.CompilerParams(dimension_semantics=("parallel",)),
    )(p, qg, wq2, cq_t, sq_t)


def _head_mask(h):
    lanes = lax.broadcasted_iota(jnp.int32, (1, 2 * DV), 1)
    return (lanes // DV) == (h % 2)


def attn_fwd(q, k, v, *, name, tq=256):
    def body(q_ref, k_ref, v_ref, o_ref):
        h = pl.program_id(1)
        s = lax.dot_general(q_ref[...], k_ref[...], (((1,), (1,)), ((), ())), preferred_element_type=F32) * SCALE
        m = jnp.max(s, axis=-1, keepdims=True)
        e = jnp.exp(s - m)
        pr = (e * (1.0 / jnp.sum(e, axis=-1, keepdims=True))).astype(BF)
        vm = jnp.where(_head_mask(h), v_ref[...], jnp.zeros_like(v_ref[...]))
        o2 = jnp.dot(pr, vm, preferred_element_type=F32).astype(BF)

        @pl.when(h % 2 == 0)
        def _():
            o_ref[...] = o2

        @pl.when(h % 2 == 1)
        def _():
            o_ref[...] = o_ref[...] + o2

    return pl.pallas_call(
        body, name=name, grid=(T // tq, NH),
        in_specs=[pl.BlockSpec((tq, HP), lambda i, h: (i, h)), pl.BlockSpec((TKV, HP), lambda i, h: (0, h)),
                  pl.BlockSpec((TKV, 2 * DV), lambda i, h: (0, h // 2))],
        out_specs=pl.BlockSpec((tq, 2 * DV), lambda i, h: (i, h // 2)),
        out_shape=jax.ShapeDtypeStruct((T, NH * DV), BF),
        compiler_params=pltpu.CompilerParams(dimension_semantics=("parallel", "arbitrary")),
    )(q, k, v)


def _shift_dn(x):
    n = x.shape[0]
    rows = lax.broadcasted_iota(jnp.int32, (n, 1), 0)
    return jnp.where(rows == 0, 0.0, pltpu.roll(x, 1, axis=0))


def _shift_up(x):
    n = x.shape[0]
    rows = lax.broadcasted_iota(jnp.int32, (n, 1), 0)
    return jnp.where(rows == n - 1, 0.0, pltpu.roll(x, n - 1, axis=0))


def _conv(x, w_ref, b_ref):
    return b_ref[...] + _shift_dn(x) * w_ref[0:1, :] + x * w_ref[1:2, :] + _shift_up(x) * w_ref[2:3, :]


def _conv_t(dy, w_ref):
    return _shift_up(dy) * w_ref[0:1, :] + dy * w_ref[1:2, :] + _shift_dn(dy) * w_ref[2:3, :]


def _conv_wgrad(dw_ref, dy, x):
    dw_ref[0:1, :] = jnp.sum(dy * _shift_dn(x), axis=0, keepdims=True)
    dw_ref[1:2, :] = jnp.sum(dy * x, axis=0, keepdims=True)
    dw_ref[2:3, :] = jnp.sum(dy * _shift_up(x), axis=0, keepdims=True)


def convz(p, cw, cb, *, name):
    o0 = O_CV // (3 * CVB)

    def body(p_ref, w_ref, bias_ref, z_ref):
        xv, bv, cv = p_ref[:, 0:CVB], p_ref[:, CVB:2 * CVB], p_ref[:, 2 * CVB:3 * CVB]
        z_ref[...] = (bv * _conv(cv * xv, w_ref, bias_ref)).astype(BF)

    return pl.pallas_call(
        body, name=name, grid=(CONV // CVB,),
        in_specs=[pl.BlockSpec((T, 3 * CVB), lambda j: (0, o0 + j)), pl.BlockSpec((3, CVB), lambda j: (0, j)),
                  pl.BlockSpec((1, CVB), lambda j: (0, j))],
        out_specs=pl.BlockSpec((T, CVB), lambda j: (0, j)),
        out_shape=jax.ShapeDtypeStruct((T, CONV), BF),
        compiler_params=pltpu.CompilerParams(dimension_semantics=("parallel",)),
    )(p, cw, cb)


def gate_merge(p, ya, yc, *, name, tm=256):
    def body(ga_ref, gc_ref, ya_ref, yc_ref, o_ref):
        o_ref[...] = (jax.nn.sigmoid(ga_ref[...]) * ya_ref[...] + jax.nn.sigmoid(gc_ref[...]) * yc_ref[...]).astype(BF)

    blk = pl.BlockSpec((tm, D), lambda i: (i, 0))
    return pl.pallas_call(
        body, name=name, grid=(T // tm,),
        in_specs=[pl.BlockSpec((tm, D), lambda i: (i, O_GA // D)), pl.BlockSpec((tm, D), lambda i: (i, O_GC // D)), blk, blk],
        out_specs=blk, out_shape=jax.ShapeDtypeStruct((T, D), BF),
        compiler_params=pltpu.CompilerParams(dimension_semantics=("parallel",)),
    )(p, p, ya, yc)


def ffn_act(u0, cw, cb, *, name, tc=256):
    nb = DFF // tc

    def body(u_ref, wg_ref, wv_ref, bg_ref, bv_ref, f_ref):
        ug = _conv(u_ref[0], wg_ref, bg_ref)
        uv = _conv(u_ref[1], wv_ref, bv_ref)
        f_ref[...] = (ug * jax.nn.sigmoid(ug) * uv).astype(BF)

    return pl.pallas_call(
        body, name=name, grid=(nb,),
        in_specs=[pl.BlockSpec((2, T, tc), lambda j: (0, 0, j)),
                  pl.BlockSpec((3, tc), lambda j: (0, j)), pl.BlockSpec((3, tc), lambda j: (0, nb + j)),
                  pl.BlockSpec((1, tc), lambda j: (0, j)), pl.BlockSpec((1, tc), lambda j: (0, nb + j))],
        out_specs=pl.BlockSpec((T, tc), lambda j: (0, j)),
        out_shape=jax.ShapeDtypeStruct((T, DFF), BF),
        compiler_params=pltpu.CompilerParams(dimension_semantics=("parallel",)),
    )(u0, cw, cw, cb, cb)


def final_loss(x1, d, g2, fg, tgt, *, name, tm=256):
    def body(x1_ref, d_ref, g2_ref, fg_ref, t_ref, dx_ref, dd_ref, dfg_ref, loss_ref):
        i = pl.program_id(0)
        xv = x1_ref[...] + g2_ref[...] * d_ref[...]
        r = lax.rsqrt(jnp.mean(xv * xv, axis=-1, keepdims=True) + EPS)
        xh = xv * r
        diff = xh * fg_ref[...] - t_ref[...]
        part = 0.5 * jnp.sum(jnp.mean(diff * diff, axis=-1, keepdims=True), axis=0, keepdims=True)
        dy = diff * (1.0 / D)
        a = dy * fg_ref[...]
        dx = r * (a - xh * jnp.mean(a * xh, axis=-1, keepdims=True))
        dx_ref[...] = dx
        dd_ref[...] = (dx * g2_ref[...]).astype(BF)
        dfg = jnp.sum(dy * xh, axis=0, keepdims=True)

        @pl.when(i == 0)
        def _():
            dfg_ref[...] = dfg
            loss_ref[...] = jnp.broadcast_to(part, (1, 128))

        @pl.when(i > 0)
        def _():
            dfg_ref[...] += dfg
            loss_ref[...] += jnp.broadcast_to(part, (1, 128))

    blk = pl.BlockSpec((tm, D), lambda i: (i, 0))
    return pl.pallas_call(
        body, name=name, grid=(T // tm,), in_specs=[blk, blk, _row(D), _row(D), blk],
        out_specs=[blk, blk, _row(D), _row(128)],
        out_shape=[jax.ShapeDtypeStruct((T, D), F32), jax.ShapeDtypeStruct((T, D), BF),
                   jax.ShapeDtypeStruct((1, D), F32), jax.ShapeDtypeStruct((1, 128), F32)],
        compiler_params=pltpu.CompilerParams(dimension_semantics=("arbitrary",)),
    )(x1, d, g2, fg, tgt)


def normmod_bwd(x, dh, g, sc, dres, gsrc, gate, *, name, tm=256):
    R = x.shape[0]
    has_res = dres is not None

    def body(*refs):
        if has_res:
            x_ref, dh_ref, g_ref, sc_ref, dres_ref, gsrc_ref, gate_ref, dx_ref, dxg_ref, st_ref = refs
        else:
            x_ref, dh_ref, g_ref, sc_ref, st_ref = refs
        i = pl.program_id(0)
        xv = x_ref[...]
        r = lax.rsqrt(jnp.mean(xv * xv, axis=-1, keepdims=True) + EPS)
        xh = xv * r
        dhv = dh_ref[...]
        n = xh * g_ref[...]
        dn = dhv * (1.0 + sc_ref[...])
        a = dn * g_ref[...]
        rows = [jnp.sum(dhv, axis=0, keepdims=True), jnp.sum(dhv * n, axis=0, keepdims=True),
                jnp.sum(dn * xh, axis=0, keepdims=True)]
        if has_res:
            dr = dres_ref[...]
            dx = dr + r * (a - xh * jnp.mean(a * xh, axis=-1, keepdims=True))
            dx_ref[...] = dx
            dxg_ref[...] = (dx * gate_ref[...]).astype(BF)
            rows.append(jnp.sum(dr * gsrc_ref[...], axis=0, keepdims=True))
        else:
            rows.append(jnp.zeros((1, D), F32))

        @pl.when(i == 0)
        def _():
            for k, row in enumerate(rows):
                st_ref[k:k + 1, :] = row

        @pl.when(i > 0)
        def _():
            for k, row in enumerate(rows):
                st_ref[k:k + 1, :] += row

    blk = pl.BlockSpec((tm, D), lambda i: (i, 0))
    st_spec = pl.BlockSpec((4, D), lambda i: (0, 0))
    st_shape = jax.ShapeDtypeStruct((4, D), F32)
    cp = pltpu.CompilerParams(dimension_semantics=("arbitrary",))
    if has_res:
        return pl.pallas_call(
            body, name=name, grid=(R // tm,), in_specs=[blk, blk, _row(D), _row(D), blk, blk, _row(D)],
            out_specs=[blk, blk, st_spec],
            out_shape=[jax.ShapeDtypeStruct((R, D), F32), jax.ShapeDtypeStruct((R, D), BF), st_shape],
            compiler_params=cp,
        )(x, dh, g, sc, dres, gsrc, gate)
    return pl.pallas_call(
        body, name=name, grid=(R // tm,), in_specs=[blk, blk, _row(D), _row(D)],
        out_specs=st_spec, out_shape=st_shape, compiler_params=cp,
    )(x, dh, g, sc)


def ffn_act_bwd(u0, df, cw, cb, *, name, tc=256):
    nb = DFF // tc

    def body(u_ref, df_ref, wg_ref, wv_ref, bg_ref, bv_ref, du_ref, dw_ref, db_ref):
        xg, xv = u_ref[0], u_ref[1]
        ug = _conv(xg, wg_ref, bg_ref)
        uv = _conv(xv, wv_ref, bv_ref)
        sig = jax.nn.sigmoid(ug)
        dfv = df_ref[...]
        dug = dfv * uv * (sig * (1.0 + ug * (1.0 - sig)))
        duv = dfv * (ug * sig)
        du_ref[0] = _conv_t(dug, wg_ref).astype(BF)
        du_ref[1] = _conv_t(duv, wv_ref).astype(BF)
        _conv_wgrad(dw_ref.at[0], dug, xg)
        _conv_wgrad(dw_ref.at[1], duv, xv)
        db_ref[0] = jnp.sum(dug, axis=0, keepdims=True)
        db_ref[1] = jnp.sum(duv, axis=0, keepdims=True)

    lo = lambda r: pl.BlockSpec((r, tc), lambda j: (0, j))
    hi = lambda r: pl.BlockSpec((r, tc), lambda j: (0, nb + j))
    st = lambda r: pl.BlockSpec((2, r, tc), lambda j: (0, 0, j))
    return pl.pallas_call(
        body, name=name, grid=(nb,),
        in_specs=[st(T), lo(T), lo(3), hi(3), lo(1), hi(1)],
        out_specs=[st(T), st(3), st(1)],
        out_shape=[jax.ShapeDtypeStruct((2, T, DFF), BF), jax.ShapeDtypeStruct((2, 3, DFF), F32),
                   jax.ShapeDtypeStruct((2, 1, DFF), F32)],
        compiler_params=pltpu.CompilerParams(dimension_semantics=("parallel",)),
    )(u0, df, cw, cw, cb, cb)


def gate_merge_bwd(p, ya, yc, dm, *, name, tm=256):
    def body(ga_ref, gc_ref, ya_ref, yc_ref, dm_ref, dya_ref, dyc_ref, dp_ref):
        sa, sc_ = jax.nn.sigmoid(ga_ref[...]), jax.nn.sigmoid(gc_ref[...])
        dmv = dm_ref[...]
        dya_ref[...] = (dmv * sa).astype(BF)
        dyc_ref[...] = (dmv * sc_).astype(BF)
        dp_ref[:, 0:D] = (dmv * ya_ref[...] * (sa * (1.0 - sa))).astype(BF)
        dp_ref[:, D:2 * D] = (dmv * yc_ref[...] * (sc_ * (1.0 - sc_))).astype(BF)

    blk = pl.BlockSpec((tm, D), lambda i: (i, 0))
    sh = jax.ShapeDtypeStruct((T, D), BF)
    return pl.pallas_call(
        body, name=name, grid=(T // tm,),
        in_specs=[pl.BlockSpec((tm, D), lambda i: (i, O_GA // D)), pl.BlockSpec((tm, D), lambda i: (i, O_GC // D)), blk, blk, blk],
        out_specs=[blk, blk, pl.BlockSpec((tm, 2 * D), lambda i: (i, 0))],
        out_shape=[sh, sh, jax.ShapeDtypeStruct((T, NIN), BF)],
        compiler_params=pltpu.CompilerParams(dimension_semantics=("parallel",)),
    )(p, p, ya, yc, dm)


def convz_bwd(p, dz, cw, cb, dp, *, name):
    o0 = O_CV // (3 * CVB)

    def body(p_ref, dz_ref, w_ref, bias_ref, dp_in, dp_ref, dw_ref, dbias_ref):
        xv, bv, cv = p_ref[:, 0:CVB], p_ref[:, CVB:2 * CVB], p_ref[:, 2 * CVB:3 * CVB]
        ci = cv * xv
        dwc = _conv(ci, w_ref, bias_ref)
        dzv = dz_ref[...]
        ddw = dzv * bv
        dci = _conv_t(ddw, w_ref)
        dp_ref[:, 0:CVB] = (dci * cv).astype(BF)
        dp_ref[:, CVB:2 * CVB] = (dzv * dwc).astype(BF)
        dp_ref[:, 2 * CVB:3 * CVB] = (dci * xv).astype(BF)
        _conv_wgrad(dw_ref, ddw, ci)
        dbias_ref[...] = jnp.sum(ddw, axis=0, keepdims=True)

    own = lambda r: pl.BlockSpec((r, CVB), lambda j: (0, j))
    return pl.pallas_call(
        body, name=name, grid=(CONV // CVB,),
        in_specs=[pl.BlockSpec((T, 3 * CVB), lambda j: (0, o0 + j)), own(T), own(3), own(1),
                  pl.BlockSpec(memory_space=pl.ANY)],
        out_specs=[pl.BlockSpec((T, 3 * CVB), lambda j: (0, o0 + j)), own(3), own(1)],
        out_shape=[jax.ShapeDtypeStruct((T, NIN), BF), jax.ShapeDtypeStruct((3, CONV), F32),
                   jax.ShapeDtypeStruct((1, CONV), F32)],
        input_output_aliases={4: 0},
        compiler_params=pltpu.CompilerParams(dimension_semantics=("parallel",)),
    )(p, dz, cw, cb, dp)


def attn_bwd(q, k, v, do, *, name, tq=256):
    def body(q_ref, k_ref, v_ref, do_ref, dq_ref, dk_ref, dv_ref):
        h, i = pl.program_id(0), pl.program_id(1)
        qv, kv = q_ref[...], k_ref[...]
        s = lax.dot_general(qv, kv, (((1,), (1,)), ((), ())), preferred_element_type=F32) * SCALE
        m = jnp.max(s, axis=-1, keepdims=True)
        e = jnp.exp(s - m)
        pr = e * (1.0 / jnp.sum(e, axis=-1, keepdims=True))
        mask = _head_mask(h)
        vm = jnp.where(mask, v_ref[...], jnp.zeros_like(v_ref[...]))
        dom = jnp.where(mask, do_ref[...], jnp.zeros_like(do_ref[...]))
        dp = lax.dot_general(dom, vm, (((1,), (1,)), ((), ())), preferred_element_type=F32)
        ds = (pr * (dp - jnp.sum(pr * dp, axis=-1, keepdims=True)) * SCALE).astype(BF)
        dq_ref[...] = jnp.dot(ds, kv, preferred_element_type=F32)
        dk = lax.dot_general(ds, qv, (((0,), (0,)), ((), ())), preferred_element_type=F32)
        dv = lax.dot_general(pr.astype(BF), dom, (((0,), (0,)), ((), ())), preferred_element_type=F32)

        @pl.when(i == 0)
        def _():
            dk_ref[...] = dk

        @pl.when(i > 0)
        def _():
            dk_ref[...] += dk

        @pl.when((i == 0) & (h % 2 == 0))
        def _():
            dv_ref[...] = dv

        @pl.when((i > 0) | (h % 2 == 1))
        def _():
            dv_ref[...] += dv

    return pl.pallas_call(
        body, name=name, grid=(NH, T // tq),
        in_specs=[pl.BlockSpec((tq, HP), lambda h, i: (i, h)), pl.BlockSpec((TKV, HP), lambda h, i: (0, h)),
                  pl.BlockSpec((TKV, 2 * DV), lambda h, i: (0, h // 2)), pl.BlockSpec((tq, 2 * DV), lambda h, i: (i, h // 2))],
        out_specs=[pl.BlockSpec((tq, HP), lambda h, i: (i, h)), pl.BlockSpec((TKV, HP), lambda h, i: (0, h)),
                   pl.BlockSpec((TKV, 2 * DV), lambda h, i: (0, h // 2))],
        out_shape=[jax.ShapeDtypeStruct((T, NH * HP), F32), jax.ShapeDtypeStruct((TKV, NH * HP), F32),
                   jax.ShapeDtypeStruct((TKV, NH * DV), F32)],
        compiler_params=pltpu.CompilerParams(dimension_semantics=("arbitrary", "arbitrary")),
    )(q, k, v, do)


def qprep_bwd(p, dq, qg, wq2, cq_t, sq_t, dp, *, name, tm=256):
    qcol = O_Q // 512

    def body(p_ref, dq_ref, g_ref, w_ref, c_ref, s_ref, dp_in, dp_ref, dq2_ref, dg_ref):
        i = pl.program_id(0)
        dqv = dq_ref[...]
        cc = jnp.concatenate([c_ref[...]] * NH, axis=1)
        ss = jnp.concatenate([s_ref[...]] * NH, axis=1)
        dq2 = jnp.concatenate([dqv * cc, dqv * ss], axis=1).astype(BF)
        dq2_ref[...] = dq2
        dcq = lax.dot_general(dq2, w_ref[...], (((1,), (1,)), ((), ())), preferred_element_type=F32)
        pq = p_ref[...]
        r = lax.rsqrt(jnp.sum(pq * pq, axis=-1, keepdims=True) * (1.0 / QL) + EPS)
        xh = pq * r
        a = dcq * g_ref[...]
        dp_ref[...] = (r * (a - xh * (jnp.sum(a * xh, axis=-1, keepdims=True) * (1.0 / QL)))).astype(BF)
        dg = jnp.sum(dcq * xh, axis=0, keepdims=True)

        @pl.when(i == 0)
        def _():
            dg_ref[...] = dg

        @pl.when(i > 0)
        def _():
            dg_ref[...] += dg

    return pl.pallas_call(
        body, name=name, grid=(T // tm,),
        in_specs=[pl.BlockSpec((tm, 512), lambda i: (i, qcol)), pl.BlockSpec((tm, NH * HP), lambda i: (i, 0)), _row(512),
                  pl.BlockSpec((512, 2 * NH * HP), lambda i: (0, 0)),
                  pl.BlockSpec((tm, HP), lambda i: (i, 0)), pl.BlockSpec((tm, HP), lambda i: (i, 0)),
                  pl.BlockSpec(memory_space=pl.ANY)],
        out_specs=[pl.BlockSpec((tm, 512), lambda i: (i, qcol)), pl.BlockSpec((tm, 2 * NH * HP), lambda i: (i, 0)), _row(512)],
        out_shape=[jax.ShapeDtypeStruct((T, NIN), BF), jax.ShapeDtypeStruct((T, 2 * NH * HP), BF),
                   jax.ShapeDtypeStruct((1, 512), F32)],
        input_output_aliases={6: 0},
        compiler_params=pltpu.CompilerParams(dimension_semantics=("arbitrary",)),
    )(p, dq, qg, wq2, cq_t, sq_t, dp)


def kvprep_bwd(pc, p, dk, dv, kvg, wkv2, ck, sk, dp, *, name, tm=256):
    assert tm == TC
    nb = TKV // tm
    kvcol = O_KV // 512

    def body(pc_ref, p_ref, dk_ref, dv_ref, g_ref, w_ref, ck_ref, sk_ref, dp_in, dp_ref, dpc_ref, dkv2_ref, dg_ref):
        i = pl.program_id(0)
        t = jnp.where(i == NLAT, pc_ref[...], p_ref[...])
        pk = t[:, :KVL]
        r = lax.rsqrt(jnp.mean(pk * pk, axis=-1, keepdims=True) + EPS)
        xh = pk * r
        dkv = dk_ref[...]
        dkv2 = jnp.concatenate([dkv, dv_ref[...]], axis=1).astype(BF)
        dkv2_ref[...] = dkv2
        dckv = lax.dot_general(dkv2, w_ref[...], (((1,), (1,)), ((), ())), preferred_element_type=F32)
        a = dckv * g_ref[...]
        dpk = r * (a - xh * jnp.mean(a * xh, axis=-1, keepdims=True))
        dkr = dkv[:, 0:HP]
        for hh in range(1, NH):
            dkr = dkr + dkv[:, hh * HP:(hh + 1) * HP]
        res = jnp.concatenate([dpk, dkr * ck_ref[...], dkr * sk_ref[...]], axis=1).astype(BF)
        dg = jnp.sum(dckv * xh, axis=0, keepdims=True)

        @pl.when(i == 0)
        def _():
            dg_ref[...] = dg

        @pl.when(i > 0)
        def _():
            dg_ref[...] += dg

        @pl.when(i < NLAT)
        def _():
            dp_ref[...] = res

        @pl.when(i == NLAT)
        def _():
            dpc_ref[...] = res

    rb = lambda w: pl.BlockSpec((tm, w), lambda i: (i, 0))
    return pl.pallas_call(
        body, name=name, grid=(nb,),
        in_specs=[pl.BlockSpec((tm, 512), lambda i: (0, 0)),
                  pl.BlockSpec((tm, 512), lambda i: (jnp.minimum(i, NLAT - 1), kvcol)),
                  rb(NH * HP), rb(NH * DV), _row(KVL), pl.BlockSpec((KVL, NH * HP + NH * DV), lambda i: (0, 0)),
                  rb(HP), rb(HP), pl.BlockSpec(memory_space=pl.ANY)],
        out_specs=[pl.BlockSpec((tm, 512), lambda i: (jnp.minimum(i, NLAT - 1), kvcol)),
                   pl.BlockSpec((tm, 512), lambda i: (0, 0)), rb(NH * HP + NH * DV), _row(KVL)],
        out_shape=[jax.ShapeDtypeStruct((T, NIN), BF), jax.ShapeDtypeStruct((TC, 512), BF),
                   jax.ShapeDtypeStruct((TKV, NH * HP + NH * DV), BF), jax.ShapeDtypeStruct((1, KVL), F32)],
        input_output_aliases={8: 0},
        compiler_params=pltpu.CompilerParams(dimension_semantics=("arbitrary",)),
    )(pc, p, dk, dv, kvg, wkv2, ck, sk, dp)


def _pieces(src, width, n):
    out, c = [], src
    while c < src + width:
        k = c // n
        w = min(src + width, (k + 1) * n) - c
        out.append((k, c - k * n, c - src, w))
        c += w
    return out


def _win_moves():
    mv = [(2208, 1024, O_GA), (3232, 1024, O_GC), (0, KVL, O_KV), (256, DR, O_KV + KVL + DN), (288, QL, O_Q)]
    mv += [(256 + _swap_start(g), 8, O_KV + KVL + HP + DN + 8 * g) for g in range(4)]
    for j in range(CONV // CVB):
        base = O_CV + 3 * CVB * j
        mv += [(672 + CVB * j, CVB, base), (1184 + CVB * j, CVB, base + CVB), (1696 + CVB * j, CVB, base + 2 * CVB)]
    return mv


_WIN_ZERO = [(O_KV + KVL, DN), (O_KV + KVL + DN + DR, HP - DN - DR), (O_KV + KVL + HP, DN),
             (O_KV + KVL + HP + DN + DR, HP - DN - DR), (O_Q + QL, 512 - QL)]


def build_win(g, *, name, tm=256):
    def body(g_ref, o_ref):
        for src, w, dst in _win_moves():
            for k, a, off, pw in _pieces(src, w, SH_IN):
                o_ref[:, dst + off:dst + off + pw] = g_ref[k, :, a:a + pw]
        for c0, w in _WIN_ZERO:
            o_ref[:, c0:c0 + w] = jnp.zeros((tm, w), o_ref.dtype)

    return pl.pallas_call(
        body, name=name, grid=(D // tm,), in_specs=[pl.BlockSpec((NDEV, tm, SH_IN), lambda i: (0, i, 0))],
        out_specs=pl.BlockSpec((tm, NIN), lambda i: (i, 0)), out_shape=jax.ShapeDtypeStruct((D, NIN), g.dtype),
        compiler_params=pltpu.CompilerParams(dimension_semantics=("parallel",)),
    )(g)


def shard_win_grad(dw, dwc, *, name, tm=256):
    def body(dw_ref, dwc_ref, o_ref, kvs):
        kvs[...] = dw_ref[:, O_KV:O_KV + 512] + dwc_ref[...]

        def src(col, w):
            if O_KV <= col < O_KV + 512:
                return kvs[:, col - O_KV:col - O_KV + w]
            return dw_ref[:, col:col + w]

        for s, w, dst in _win_moves():
            if w == 8 or s == 256:
                continue
            for k, a, off, pw in _pieces(s, w, SH_IN):
                o_ref[k, :, a:a + pw] = src(dst + off, pw).astype(o_ref.dtype)
        for g in range(4):
            val = src(O_KV + KVL + DN + 8 * g, 8) + src(O_KV + KVL + HP + DN + _swap_start(g), 8)
            o_ref[0, :, 256 + 8 * g:256 + 8 * g + 8] = val.astype(o_ref.dtype)

    return pl.pallas_call(
        body, name=name, grid=(D // tm,),
        in_specs=[pl.BlockSpec((tm, NIN), lambda i: (i, 0)), pl.BlockSpec((tm, 512), lambda i: (i, 0))],
        out_specs=pl.BlockSpec((NDEV, tm, SH_IN), lambda i: (0, i, 0)),
        out_shape=jax.ShapeDtypeStruct((NDEV, D, SH_IN), BF),
        scratch_shapes=[pltpu.VMEM((tm, 512), F32)],
        compiler_params=pltpu.CompilerParams(dimension_semantics=("parallel",)),
    )(dw, dwc)


def build_wq_wkv(gq, gkv, *, name):
    def body(gq_ref, gkv_ref, q_ref, kv_ref):
        q_ref[...] = jnp.zeros_like(q_ref)
        kv_ref[...] = jnp.zeros_like(kv_ref)
        for h in range(NH):
            q_ref[0:QL, h * HP:h * HP + DN + DR] = gq_ref[h]
            for g in range(4):
                c0 = NH * HP + h * HP + DN + 8 * g
                q_ref[0:QL, c0:c0 + 8] = gq_ref[h, :, DN + _swap_start(g):DN + _swap_start(g) + 8]
            kv_ref[:, h * HP:h * HP + DN] = gkv_ref[h, :, 0:DN]
            kv_ref[:, NH * HP + h * DV:NH * HP + (h + 1) * DV] = gkv_ref[h, :, DN:DN + DV]

    vm = pl.BlockSpec(memory_space=pltpu.VMEM)
    return pl.pallas_call(
        body, name=name, in_specs=[vm, vm], out_specs=[vm, vm],
        out_shape=[jax.ShapeDtypeStruct((512, 2 * NH * HP), gq.dtype), jax.ShapeDtypeStruct((KVL, NH * HP + NH * DV), gq.dtype)],
    )(gq, gkv)


def shard_wq_wkv_grad(dwq2, dwkv2, *, name):
    def body(q_ref, kv_ref, gq_ref, gkv_ref):
        for h in range(NH):
            gq_ref[h, :, 0:DN] = q_ref[0:QL, h * HP:h * HP + DN].astype(BF)
            for g in range(4):
                a = q_ref[0:QL, h * HP + DN + 8 * g:h * HP + DN + 8 * g + 8]
                c0 = NH * HP + h * HP + DN + _swap_start(g)
                gq_ref[h, :, DN + 8 * g:DN + 8 * g + 8] = (a + q_ref[0:QL, c0:c0 + 8]).astype(BF)
            gkv_ref[h, :, 0:DN] = kv_ref[:, h * HP:h * HP + DN].astype(BF)
            gkv_ref[h, :, DN:DN + DV] = kv_ref[:, NH * HP + h * DV:NH * HP + (h + 1) * DV].astype(BF)

    vm = pl.BlockSpec(memory_space=pltpu.VMEM)
    return pl.pallas_call(
        body, name=name, in_specs=[vm, vm], out_specs=[vm, vm],
        out_shape=[jax.ShapeDtypeStruct((NDEV, QL, (DN + DR)), BF), jax.ShapeDtypeStruct((NDEV, KVL, DN + DV), BF)],
    )(dwq2, dwkv2)


def unshard_cols(g, *, name, tm=256):
    _, K, n = g.shape
    tm = _pick(K, tm, 16)

    def body(g_ref, o_ref):
        for k in range(NDEV):
            o_ref[:, k * n:(k + 1) * n] = g_ref[k]

    return pl.pallas_call(
        body, name=name, grid=(K // tm,), in_specs=[pl.BlockSpec((NDEV, tm, n), lambda i: (0, i, 0))],
        out_specs=pl.BlockSpec((tm, NDEV * n), lambda i: (i, 0)), out_shape=jax.ShapeDtypeStruct((K, NDEV * n), g.dtype),
        compiler_params=pltpu.CompilerParams(dimension_semantics=("parallel",)),
    )(g)


def shard_cols(w, *, name, tm=256):
    K, n8 = w.shape
    n = n8 // NDEV
    tm = _pick(K, tm, 16)

    def body(w_ref, o_ref):
        for k in range(NDEV):
            o_ref[k] = w_ref[:, k * n:(k + 1) * n]

    return pl.pallas_call(
        body, name=name, grid=(K // tm,), in_specs=[pl.BlockSpec((tm, n8), lambda i: (i, 0))],
        out_specs=pl.BlockSpec((NDEV, tm, n), lambda i: (0, i, 0)), out_shape=jax.ShapeDtypeStruct((NDEV, K, n), w.dtype),
        compiler_params=pltpu.CompilerParams(dimension_semantics=("parallel",)),
    )(w)


def _rope_tables():
    t = np.arange(T)
    row = (t // GRID_W).astype(np.float32)
    col = (t % GRID_W).astype(np.float32)
    axis_dim = DR // 2
    inv = (np.float32(ROPE_THETA) ** (-np.arange(0, axis_dim, 2, dtype=np.float32) / np.float32(axis_dim))).astype(np.float32)
    ar, ac = (row[:, None] * inv).astype(np.float32), (col[:, None] * inv).astype(np.float32)
    cosv = np.concatenate([np.cos(ar), np.cos(ar), np.cos(ac), np.cos(ac)], axis=1).astype(np.float32)
    sinv = np.concatenate([-np.sin(ar), np.sin(ar), -np.sin(ac), np.sin(ac)], axis=1).astype(np.float32)
    ck = np.zeros((TKV, HP), np.float32)
    sk = np.zeros((TKV, HP), np.float32)
    ck[T:, DN:DN + DR] = 1.0
    ck[:T, DN:DN + DR] = cosv
    sk[:T, DN:DN + DR] = sinv
    cq = np.zeros((T, HP), np.float32)
    cq[:, :DN] = 1.0
    cq[:, DN:DN + DR] = cosv
    return jnp.asarray(ck), jnp.asarray(sk), jnp.asarray(cq), jnp.asarray(sk[:T])


def _local_step(x, ctx, tgt, mod_lat, mod_ctx, n1g, qg, kvg, n2g, fg, conv_w, conv_b, ffn_w, ffn_b, get_w, put_g):
    sh1, sc1, g1, sh2, sc2, g2 = [mod_lat[:, i * D:(i + 1) * D] for i in range(6)]
    csh1, csc1 = mod_ctx[:, 0:D], mod_ctx[:, D:2 * D]
    ck, sk, cq_t, sq_t = _rope_tables()
    qg_p = jnp.pad(qg, ((0, 0), (0, 512 - QL)))

    hcat = normmod_cat(ctx, x, n1g, csc1, csh1, sc1, sh1, name="normmod1")
    win = get_w("in", hcat)
    p = mm(hcat, win, M=T, name="in_proj")
    pc = mm(hcat, win, M=TC, N=512, a_off=(T, 0), b_off=(0, O_KV), name="in_proj_ctx")
    wq2, wkv2, wao, wco, wo = get_w("mid", p)
    kh, vh, ckv = kvprep(pc, p, kvg, wkv2, ck, sk, name="kvprep")
    qr, cq = qprep(p, qg_p, wq2, cq_t, sq_t, name="qprep")
    o = attn_fwd(qr, kh, vh, name="attn_fwd")
    z = convz(p, conv_w, conv_b, name="convz")
    ya = mm(o, wao, name="attn_out")
    yc = mm(z, wco, name="conv_out")
    merged = gate_merge(p, ya, yc, name="gate_merge")
    a_out = mm(merged, wo, name="o_proj")
    x1, h2 = resid_normmod(x, a_out, g1, n2g, sc2, sh2, name="resid_normmod2")
    wup, wdn = get_w("ffn", h2)
    u0 = mm(h2, wup, o_stack=True, tn=1408, name="up_proj")
    f = ffn_act(u0, ffn_w, ffn_b, name="ffn_act")
    dn = mm(f, wdn, name="down_proj")
    dx2, dd, dfg, loss = final_loss(x1, dn, g2, fg, tgt, name="final_loss")

    df = mm(dd, wdn, tb=True, name="down_proj_dx")
    dwdn = mm(f, dd, ta=True, out_dtype=BF, name="down_proj_dw")
    du0, dffn_w, dffn_b = ffn_act_bwd(u0, df, ffn_w, ffn_b, name="ffn_act_bwd")
    dwup = mm(h2, du0, ta=True, b_stack=True, out_dtype=BF, tn=1408, name="up_proj_dw")
    tok = put_g("ffn", dict(dwup=dwup, dwdn=dwdn))
    dh2 = mm(du0, wup, tb=True, a_stack=True, dep=tok, name="up_proj_dx")
    dx1, da, st2 = normmod_bwd(x1, dh2, n2g, sc2, dx2, dn, g1, name="normmod2_bwd")

    dmerged = mm(da, wo, tb=True, name="o_proj_dx")
    dwo = mm(merged, da, ta=True, out_dtype=BF, name="o_proj_dw")
    dya, dyc, dp = gate_merge_bwd(p, ya, yc, dmerged, name="gate_merge_bwd")
    do = mm(dya, wao, tb=True, out_dtype=BF, name="attn_out_dx")
    dwao = mm(o, dya, ta=True, out_dtype=BF, name="attn_out_dw")
    dz = mm(dyc, wco, tb=True, name="conv_out_dx")
    dwco = mm(z, dyc, ta=True, out_dtype=BF, name="conv_out_dw")
    dp, dconv_w, dconv_b = convz_bwd(p, dz, conv_w, conv_b, dp, name="convz_bwd")
    dq, dk, dv = attn_bwd(qr, kh, vh, do, name="attn_bwd")
    dp, dq2, dqg = qprep_bwd(p, dq, qg_p, wq2, cq_t, sq_t, dp, name="qprep_bwd")
    dwq2 = mm(cq, dq2, ta=True, name="q_up_dw")
    dp, dpc, dkv2, dkvg = kvprep_bwd(pc, p, dk, dv, kvg, wkv2, ck, sk, dp, name="kvprep_bwd")
    dwkv2 = mm(ckv, dkv2, ta=True, name="kv_up_dw")
    tok = put_g("mid", dict(dwq2=dwq2, dwkv2=dwkv2, dwao=dwao, dwco=dwco, dwo=dwo))

    dwin = mm(hcat, dp, ta=True, K=T, dep=tok, name="in_proj_dw")
    dwin_c = mm(hcat, dpc, ta=True, K=TC, a_off=(T, 0), name="in_proj_ctx_dw")
    tok = put_g("in", dict(dwin=dwin, dwin_c=dwin_c))
    dh = mm(dp, win, tb=True, dep=tok, name="in_proj_dx")
    dhc = mm(dpc, win, tb=True, N=D, K=512, b_off=(0, O_KV), name="in_proj_ctx_dx")
    dx, _, st1 = normmod_bwd(x, dh, n1g, sc1, dx1, a_out, g1, name="normmod1_bwd")
    stc = normmod_bwd(ctx, dhc, n1g, csc1, None, None, None, name="normmod1_ctx_bwd")

    zrow = jnp.zeros((1, D), F32)
    dmod_lat = jnp.concatenate([st1[0:1], st1[1:2], st1[3:4], st2[0:1], st2[1:2], st2[3:4]], axis=1)
    dmod_ctx = jnp.concatenate([stc[0:1], stc[1:2], zrow, zrow, zrow, zrow], axis=1)
    return dict(
        loss=loss, dx=dx, dmod_lat=dmod_lat, dmod_ctx=dmod_ctx,
        dn1g=st1[2:3] + stc[2:3], dqg=dqg, dkvg=dkvg, dn2g=st2[2:3], dfg=dfg,
        dconv_w=dconv_w, dconv_b=dconv_b, dffn_w=dffn_w, dffn_b=dffn_b)


def _me():
    x, y, c = lax.axis_index("x"), lax.axis_index("y"), lax.axis_index("c")
    return x, y, c, 4 * x + 2 * y + c


def _peer(x, y, c, k):
    px = 1 - x if k & 4 else x
    py = 1 - y if k & 2 else y
    pc = 1 - c if k & 1 else c
    return (px, py, pc), 4 * px + 2 * py + pc


def _exchange_tiles(src_of_peer, buf, send_sem, recv_sem):
    x, y, c, me = _me()
    for k in range(1, NDEV):
        dev, lin = _peer(x, y, c, k)
        pltpu.make_async_remote_copy(src_ref=src_of_peer(lin), dst_ref=buf.at[me], send_sem=send_sem, recv_sem=recv_sem,
                                     device_id=dev, device_id_type=MESH).start()
    seven = buf.at[pl.ds(0, NDEV - 1)]
    pltpu.make_async_remote_copy(src_ref=seven, dst_ref=seven, send_sem=send_sem, recv_sem=recv_sem,
                                 device_id=(x, y, c), device_id_type=MESH).wait()


def _silu(z):
    return z * jax.nn.sigmoid(z)


def ada_fwd(c, c_ctx, ffn_w, conv_w, w_shard, b_shard, deps, *, name):
    nsh = w_shard.shape[1]
    deps = [d for d in deps if d is not None]

    def body(c_ref, cc_ref, fw_ref, cw_ref, w_ref, b_ref, *rest):
        s_ref, m_ref, mine, res, sems = rest[len(deps):]
        x, y, c, me = _me()
        mine[0:1, :] = _silu(c_ref[...])
        mine[1:2, :] = _silu(cc_ref[...])
        mine[2:5, :] = fw_ref[...]
        mine[5:8, :] = cw_ref[...]
        s_ref[me] = mine[...]
        _exchange_tiles(lambda lin: mine, s_ref, sems.at[0], sems.at[1])
        sall = s_ref[...].reshape(NDEV * 8, D).astype(BF)
        r = jnp.dot(sall, w_ref[...].astype(BF), preferred_element_type=F32) + b_ref[...]
        res[...] = r.reshape(NDEV, 8, nsh)
        m_ref[me] = res[me]
        _exchange_tiles(lambda lin: res.at[lin], m_ref, sems.at[2], sems.at[3])

    vm = pl.BlockSpec(memory_space=pltpu.VMEM)
    return pl.pallas_call(
        body, name=name, in_specs=[vm] * 6 + [pl.BlockSpec(memory_space=pl.ANY)] * len(deps), out_specs=[vm, vm],
        out_shape=[jax.ShapeDtypeStruct((NDEV, 8, D), F32), jax.ShapeDtypeStruct((NDEV, 8, nsh), F32)],
        scratch_shapes=[pltpu.VMEM((8, D), F32), pltpu.VMEM((NDEV, 8, nsh), F32), pltpu.SemaphoreType.DMA((4,))],
    )(c, c_ctx, ffn_w, conv_w, w_shard, b_shard, *deps)


P_DML, P_DMC, P_N1, P_QG, P_KVG, P_CB, P_N2, P_FB, P_FG, P_CW, P_FW, P_LOSS, P_ROWS = 0, 6, 12, 13, 14, 15, 16, 17, 23, 24, 27, 45, 48
FROWS = 3


def sync_small(r, *, name):
    ins = [r["dmod_lat"], r["dmod_ctx"], r["dn1g"], r["dqg"], r["dkvg"], r["dconv_b"], r["dn2g"], r["dffn_b"], r["dfg"],
           r["dconv_w"], r["dffn_w"], r["loss"]]

    def put_wide(p, row0, row, n):
        for j in range(-(-n // D)):
            w = min(D, n - j * D)
            p[row0 + j:row0 + j + 1, 0:w] = row[:, j * D:j * D + w]

    def body(dml, dmc, n1, qg, kvg, cb, n2, fb, fg, cw, fw, loss, a_ref, sum_ref, p, sems):
        x, y, c, me = _me()
        p[...] = jnp.zeros_like(p)
        put_wide(p, P_DML, dml, 6 * D)
        put_wide(p, P_DMC, dmc, 6 * D)
        put_wide(p, P_N1, n1, D)
        put_wide(p, P_QG, qg, 512)
        put_wide(p, P_KVG, kvg, KVL)
        put_wide(p, P_CB, cb, CONV)
        put_wide(p, P_N2, n2, D)
        put_wide(p, P_FG, fg, D)
        put_wide(p, P_LOSS, loss, 128)
        for s in range(2):
            put_wide(p, P_FB + FROWS * s, fb.at[s], DFF)
        for k in range(3):
            put_wide(p, P_CW + k, cw.at[k:k + 1], CONV)
            for s in range(2):
                put_wide(p, P_FW + FROWS * (2 * k + s), fw.at[s, k:k + 1], DFF)
        a_ref[me] = p[...]
        _exchange_tiles(lambda lin: p, a_ref, sems.at[0], sems.at[1])
        acc = a_ref[0]
        for k in range(1, NDEV):
            acc = acc + a_ref[k]
        sum_ref[...] = acc

    vm = pl.BlockSpec(memory_space=pltpu.VMEM)
    return pl.pallas_call(
        body, name=name, in_specs=[vm] * len(ins), out_specs=[vm, vm],
        out_shape=[jax.ShapeDtypeStruct((NDEV, P_ROWS, D), F32), jax.ShapeDtypeStruct((P_ROWS, D), F32)],
        scratch_shapes=[pltpu.VMEM((P_ROWS, D), F32), pltpu.SemaphoreType.DMA((2,))],
    )(*ins)


def ada_bwd(s_all, dml, dmc, w_shard, c_ctx, *, name):
    nsh = w_shard.shape[1]

    def body(s_ref, dml_ref, dmc_ref, w_ref, c_ref, dw_ref, gc_ref, s16, dm16, part, buf, sems):
        x, y, c, me = _me()
        s16[...] = jnp.zeros_like(s16)
        dm16[...] = jnp.zeros_like(dm16)
        for k in range(NDEV):
            s16[k:k + 1, :] = s_ref[k, 0:1, :]
        s16[8:9, :] = s_ref[0, 1:2, :]
        dm16[0:8, :] = dml_ref[...]
        dm16[8:9, :] = dmc_ref[...]
        dw_ref[...] = lax.dot_general(s16[...].astype(BF), dm16[...].astype(BF), (((0,), (0,)), ((), ())),
                                      preferred_element_type=F32)
        part[...] = lax.dot_general(dm16[8:16, :].astype(BF), w_ref[...].astype(BF), (((1,), (1,)), ((), ())),
                                    preferred_element_type=F32)
        buf[me] = part[...]
        _exchange_tiles(lambda lin: part, buf, sems.at[0], sems.at[1])
        acc = buf[0]
        for k in range(1, NDEV):
            acc = acc + buf[k]
        z = c_ref[...]
        sg = jax.nn.sigmoid(z)
        gc_ref[...] = acc * (sg * (1.0 + z * (1.0 - sg)))

    vm = pl.BlockSpec(memory_space=pltpu.VMEM)
    return pl.pallas_call(
        body, name=name, in_specs=[vm] * 5, out_specs=[vm, vm],
        out_shape=[jax.ShapeDtypeStruct((D, nsh), F32), jax.ShapeDtypeStruct((8, D), F32)],
        scratch_shapes=[pltpu.VMEM((16, D), F32), pltpu.VMEM((16, nsh), F32), pltpu.VMEM((8, D), F32),
                        pltpu.VMEM((NDEV, 8, D), F32), pltpu.SemaphoreType.DMA((2,))],
    )(s_all, dml, dmc, w_shard, c_ctx)


HBM_SPEC = pl.BlockSpec(memory_space=pltpu.HBM)
SEM_SPEC = pl.BlockSpec(memory_space=pltpu.SEMAPHORE)
EFFECT = pltpu.SideEffectType.DATAFLOW_SIDE_EFFECTING


def _exchange_copies(srcs, lands, send, recv, per_peer):
    x, y, c, me = _me()
    cps = []
    for t in range(len(srcs)):
        for k in range(1, NDEV):
            dev, lin = _peer(x, y, c, k)
            cps.append(pltpu.make_async_remote_copy(
                src_ref=srcs[t].at[lin] if per_peer else srcs[t], dst_ref=lands[t].at[me],
                send_sem=send.at[7 * t + k - 1], recv_sem=recv.at[7 * t + k - 1], device_id=dev, device_id_type=MESH))
    return cps


def exchange_start(srcs, *, per_peer, name, dep=None):
    nt = len(srcs)
    land_shapes = [(a.shape if per_peer else (NDEV,) + a.shape) for a in srcs]
    deps = [] if dep is None else [dep]

    def body(*refs):
        src, land = refs[:nt], refs[nt:2 * nt]
        send, recv = refs[2 * nt + len(deps)], refs[2 * nt + len(deps) + 1]
        for cp in _exchange_copies(src, land, send, recv, per_peer):
            cp.start()
        refs[-1][...] = jnp.zeros_like(refs[-1])

    hb = lambda a: pltpu.with_memory_space_constraint(a, pltpu.HBM)
    outs = pl.pallas_call(
        body, name=name,
        out_shape=(pltpu.SemaphoreType.DMA((7 * nt,)), pltpu.SemaphoreType.DMA((7 * nt,)),
                   *[pltpu.HBM(a.shape, a.dtype) for a in srcs], *[pltpu.HBM(s, a.dtype) for s, a in zip(land_shapes, srcs)],
                   jax.ShapeDtypeStruct((8, 128), F32)),
        in_specs=[HBM_SPEC] * (2 * nt) + [pl.BlockSpec(memory_space=pl.ANY)] * len(deps),
        out_specs=(SEM_SPEC, SEM_SPEC, *([HBM_SPEC] * (2 * nt)), pl.BlockSpec(memory_space=pltpu.VMEM)),
        input_output_aliases={i: 2 + i for i in range(2 * nt)},
        compiler_params=pltpu.CompilerParams(has_side_effects=EFFECT),
    )(*[hb(a) for a in srcs], *[hb(lax.empty(s, a.dtype)) for s, a in zip(land_shapes, srcs)], *deps)
    return dict(send=outs[0], recv=outs[1], src=list(outs[2:2 + nt]), land=list(outs[2 + nt:2 + 2 * nt]), token=outs[-1],
                per_peer=per_peer)


def exchange_wait(h, after, *, name):
    nt = len(h["src"])
    per_peer = h["per_peer"]

    def body(*refs):
        src, land, send, recv = refs[:nt], refs[nt:2 * nt], refs[2 * nt], refs[2 * nt + 1]
        for cp in _exchange_copies(src, land, send, recv, per_peer):
            cp.wait_send()
            cp.wait_recv()

    outs = pl.pallas_call(
        body, name=name,
        out_shape=(*[pltpu.HBM(a.shape, a.dtype) for a in h["src"]], *[pltpu.HBM(a.shape, a.dtype) for a in h["land"]]),
        in_specs=[HBM_SPEC] * (2 * nt) + [SEM_SPEC, SEM_SPEC, pl.BlockSpec(memory_space=pl.ANY)],
        out_specs=tuple([HBM_SPEC] * (2 * nt)),
        input_output_aliases={i: i for i in range(2 * nt)},
        compiler_params=pltpu.CompilerParams(has_side_effects=EFFECT),
    )(*h["src"], *h["land"], h["send"], h["recv"], after)
    return list(outs[:nt]), list(outs[nt:])


def _gathered(h, after, me, *, name):
    srcs, lands = exchange_wait(h, after, name=name)
    return [lax.dynamic_update_slice(l, s[None], (me,) + (0,) * s.ndim) for s, l in zip(srcs, lands)]


def _scattered(h, after, me, *, name):
    srcs, lands = exchange_wait(h, after, name=name)
    return [lax.dynamic_update_slice(l, lax.dynamic_slice_in_dim(s, me, 1, 0), (me,) + (0,) * (s.ndim - 1))
            for s, l in zip(srcs, lands)]


def _adamw_math(w, g, m, v):
    nm = B1 * m + (1.0 - B1) * g
    nv = B2 * v + (1.0 - B2) * (g * g)
    m_hat = nm / (1.0 - B1 ** STEP)
    v_hat = nv / (1.0 - B2 ** STEP)
    return -LR * (m_hat / (jnp.sqrt(v_hat) + AEPS) + WD * w), nm, nv


def adamw_many(ws, gs, ms, vs, *, name):
    n = len(ws)

    def body(*refs):
        for k in range(n):
            d, nm, nv = _adamw_math(refs[k][...], refs[n + k][...], refs[2 * n + k][...], refs[3 * n + k][...])
            refs[4 * n + k][...] = d
            refs[5 * n + k][...] = nm
            refs[6 * n + k][...] = nv

    vm = pl.BlockSpec(memory_space=pltpu.VMEM)
    sh = [jax.ShapeDtypeStruct(w.shape, F32) for w in ws]
    outs = pl.pallas_call(body, name=name, in_specs=[vm] * (4 * n), out_specs=[vm] * (3 * n), out_shape=sh * 3,
                          )(*ws, *gs, *ms, *vs)
    return outs[:n], outs[n:2 * n], outs[2 * n:]


def adamw(w, g, m, v, *, name, tr=256):
    R, C = w.shape
    tr = _pick(R, tr, 8)

    def body(w_ref, g_ref, m_ref, v_ref, d_ref, nm_ref, nv_ref):
        d_ref[...], nm_ref[...], nv_ref[...] = _adamw_math(w_ref[...], g_ref[...], m_ref[...], v_ref[...])

    blk = pl.BlockSpec((tr, C), lambda i: (i, 0))
    sh = jax.ShapeDtypeStruct((R, C), F32)
    return pl.pallas_call(
        body, name=name, grid=(R // tr,), in_specs=[blk, blk, blk, blk], out_specs=[blk, blk, blk],
        out_shape=[sh, sh, sh], compiler_params=pltpu.CompilerParams(dimension_semantics=("parallel",)),
    )(w, g, m, v)


def adamw_slots(w, slots, m, v, *, name, tr=256):
    R, C = w.shape
    tr = _pick(R, tr, 16)

    def body(w_ref, s_ref, m_ref, v_ref, g_ref, d_ref, nm_ref, nv_ref):
        g = s_ref[0].astype(F32)
        for k in range(1, NDEV):
            g = g + s_ref[k].astype(F32)
        g_ref[...] = g
        d_ref[...], nm_ref[...], nv_ref[...] = _adamw_math(w_ref[...], g, m_ref[...], v_ref[...])

    blk = pl.BlockSpec((tr, C), lambda i: (i, 0))
    sh = jax.ShapeDtypeStruct((R, C), F32)
    return pl.pallas_call(
        body, name=name, grid=(R // tr,), in_specs=[blk, pl.BlockSpec((NDEV, tr, C), lambda i: (0, i, 0)), blk, blk],
        out_specs=[blk, blk, blk, blk], out_shape=[sh, sh, sh, sh],
        compiler_params=pltpu.CompilerParams(dimension_semantics=("parallel",)),
    )(w, slots, m, v)


def _padc(a, n=D):
    return jnp.pad(a, ((0, 0), (0, n - a.shape[1])))


def kernel(x, c, ctx, c_ctx, w_ada, b_ada, norm1_g, w_in, q_norm_g, kv_norm_g, w_uq, w_ukv, conv_w, conv_b, w_attn_out, w_conv_out, w_o, norm2_g, w_up, ffn_conv_w, ffn_conv_b, w_down, final_g, loss_target, m_c_ctx, m_w_ada, m_b_ada, m_norm1_g, m_w_in, m_q_norm_g, m_kv_norm_g, m_w_uq, m_w_ukv, m_conv_w, m_conv_b, m_w_attn_out, m_w_conv_out, m_w_o, m_norm2_g, m_w_up, m_ffn_conv_w, m_ffn_conv_b, m_w_down, m_final_g, v_c_ctx, v_w_ada, v_b_ada, v_norm1_g, v_w_in, v_q_norm_g, v_kv_norm_g, v_w_uq, v_w_ukv, v_conv_w, v_conv_b, v_w_attn_out, v_w_conv_out, v_w_o, v_norm2_g, v_w_up, v_ffn_conv_w, v_ffn_conv_b, v_w_down, v_final_g):
    me = 4 * lax.axis_index("x") + 2 * lax.axis_index("y") + lax.axis_index("c")
    W = dict(c_ctx=c_ctx, w_ada=w_ada, b_ada=b_ada, norm1_g=norm1_g, w_in=w_in, q_norm_g=q_norm_g, kv_norm_g=kv_norm_g,
             w_uq=w_uq, w_ukv=w_ukv, conv_w=conv_w, conv_b=conv_b, w_attn_out=w_attn_out, w_conv_out=w_conv_out, w_o=w_o,
             norm2_g=norm2_g, w_up=w_up, ffn_conv_w=ffn_conv_w, ffn_conv_b=ffn_conv_b, w_down=w_down, final_g=final_g)
    M = dict(c_ctx=m_c_ctx, w_ada=m_w_ada, b_ada=m_b_ada, norm1_g=m_norm1_g, w_in=m_w_in, q_norm_g=m_q_norm_g,
             kv_norm_g=m_kv_norm_g, w_uq=m_w_uq, w_ukv=m_w_ukv, conv_w=m_conv_w, conv_b=m_conv_b, w_attn_out=m_w_attn_out,
             w_conv_out=m_w_conv_out, w_o=m_w_o, norm2_g=m_norm2_g, w_up=m_w_up, ffn_conv_w=m_ffn_conv_w,
             ffn_conv_b=m_ffn_conv_b, w_down=m_w_down, final_g=m_final_g)
    V = dict(c_ctx=v_c_ctx, w_ada=v_w_ada, b_ada=v_b_ada, norm1_g=v_norm1_g, w_in=v_w_in, q_norm_g=v_q_norm_g,
             kv_norm_g=v_kv_norm_g, w_uq=v_w_uq, w_ukv=v_w_ukv, conv_w=v_conv_w, conv_b=v_conv_b, w_attn_out=v_w_attn_out,
             w_conv_out=v_w_conv_out, w_o=v_w_o, norm2_g=v_norm2_g, w_up=v_w_up, ffn_conv_w=v_ffn_conv_w,
             ffn_conv_b=v_ffn_conv_b, w_down=v_w_down, final_g=v_final_g)
    names = list(W)
    as2d = lambda a: a.reshape(1, -1) if a.ndim == 1 else a.reshape(a.shape[-2], a.shape[-1])
    W2 = {k: as2d(a) for k, a in W.items()}
    M2 = {k: as2d(a) for k, a in M.items()}
    V2 = {k: as2d(a) for k, a in V.items()}
    nsh = W2["w_ada"].shape[1]

    stage_w = {"in": ["w_in"], "mid": ["w_uq", "w_ukv", "w_attn_out", "w_conv_out", "w_o"], "ffn": ["w_up", "w_down"]}
    ag, tok = {}, None
    for st, nms in stage_w.items():
        ag[st] = exchange_start([W2[nm].astype(BF) for nm in nms], per_peer=False, dep=tok, name="ag_start_" + st)
        tok = ag[st]["token"]

    b_sh = lax.dynamic_slice(W2["b_ada"], (0, me * nsh), (1, nsh))
    s_all, m_all = ada_fwd(c, W2["c_ctx"], _padc(W2["ffn_conv_w"]), _padc(W2["conv_w"]), W2["w_ada"], b_sh,
                           [ag[st]["token"] for st in stage_w], name="ada_fwd")
    mod_lat = m_all[:, 0, :].reshape(1, 6 * D)
    mod_ctx = m_all[:, 1, :].reshape(1, 6 * D)
    ffn_w_full = s_all[:, 2:5, :2 * DFF // NDEV].transpose(1, 0, 2).reshape(3, 2 * DFF)
    conv_w_full = s_all[:, 5:8, :CONV // NDEV].transpose(1, 0, 2).reshape(3, CONV)

    def get_w(stage, after):
        g = dict(zip(stage_w[stage], _gathered(ag[stage], after, me, name="ag_wait_" + stage)))
        if stage == "in":
            return build_win(g["w_in"], name="build_win")
        if stage == "mid":
            wq2, wkv2 = build_wq_wkv(g["w_uq"], g["w_ukv"], name="build_wq_wkv")
            return (wq2, wkv2, unshard_cols(g["w_attn_out"], name="unshard_w_attn_out"),
                    unshard_cols(g["w_conv_out"], name="unshard_w_conv_out"), g["w_o"].reshape(D, D))
        return unshard_cols(g["w_up"], name="unshard_w_up"), g["w_down"].reshape(DFF, D)

    rs = {}

    def put_g(stage, g):
        if stage == "in":
            parts = [shard_win_grad(g["dwin"], g["dwin_c"], name="shard_win_grad")]
        elif stage == "mid":
            pq, pkv = shard_wq_wkv_grad(g["dwq2"], g["dwkv2"], name="shard_wq_wkv_grad")
            parts = [pq, pkv, shard_cols(g["dwao"], name="shard_w_attn_out"),
                     shard_cols(g["dwco"], name="shard_w_conv_out"), g["dwo"].reshape(NDEV, D // NDEV, D)]
        else:
            parts = [shard_cols(g["dwup"], name="shard_w_up"), g["dwdn"].reshape(NDEV, DFF // NDEV, D)]
        rs[stage] = exchange_start(parts, per_peer=True, name="rs_start_" + stage)
        return rs[stage]["token"]

    r = _local_step(x[0], ctx[0], loss_target[0], mod_lat, mod_ctx, W2["norm1_g"], W2["q_norm_g"], W2["kv_norm_g"],
                    W2["norm2_g"], W2["final_g"], conv_w_full, W2["conv_b"], ffn_w_full, W2["ffn_conv_b"], get_w, put_g)

    a_buf, ssum = sync_small(r, name="sync_small")
    G = {}
    loss = ssum[P_LOSS, 0]
    G["norm1_g"] = ssum[P_N1:P_N1 + 1]
    G["q_norm_g"] = ssum[P_QG:P_QG + 1, :QL]
    G["kv_norm_g"] = ssum[P_KVG:P_KVG + 1, :KVL]
    G["conv_b"] = ssum[P_CB:P_CB + 1, :CONV]
    G["norm2_g"] = ssum[P_N2:P_N2 + 1]
    G["ffn_conv_b"] = ssum[P_FB:P_FB + 2 * FROWS].reshape(1, 2, FROWS * D)[:, :, :DFF].reshape(1, 2 * DFF)
    G["final_g"] = ssum[P_FG:P_FG + 1]
    G["conv_w"] = lax.dynamic_slice(ssum[P_CW:P_CW + 3, :CONV], (0, me * (CONV // NDEV)), (3, CONV // NDEV))
    fw_full = ssum[P_FW:P_FW + 6 * FROWS].reshape(3, 2, FROWS * D)[:, :, :DFF].reshape(3, 2 * DFF)
    G["ffn_conv_w"] = lax.dynamic_slice(fw_full, (0, me * (2 * DFF // NDEV)), (3, 2 * DFF // NDEV))
    G["b_ada"] = (ssum[P_DML:P_DML + 6] + ssum[P_DMC:P_DMC + 6]).reshape(1, 6 * D)

    dml = lax.dynamic_slice(a_buf[:, P_DML:P_DML + 6, :].reshape(NDEV, 6 * D), (0, me * nsh), (NDEV, nsh))
    dmc = lax.dynamic_slice(ssum[P_DMC:P_DMC + 6].reshape(1, 6 * D), (0, me * nsh), (1, nsh))
    G["w_ada"], gcc = ada_bwd(s_all, dml, dmc, W2["w_ada"], W2["c_ctx"], name="ada_bwd")
    G["c_ctx"] = gcc[0:1]

    DL, NM, NV = {}, {}, {}
    DL["w_ada"], NM["w_ada"], NV["w_ada"] = adamw(W2["w_ada"], G["w_ada"], M2["w_ada"], V2["w_ada"], name="adamw_w_ada")
    small = ["c_ctx", "b_ada", "norm1_g", "q_norm_g", "kv_norm_g", "conv_b", "norm2_g", "ffn_conv_b", "final_g", "conv_w",
             "ffn_conv_w"]
    ds, nms, nvs = adamw_many([W2[k] for k in small], [G[k] for k in small], [M2[k] for k in small],
                              [V2[k] for k in small], name="adamw_small")
    for k, nm in enumerate(small):
        DL[nm], NM[nm], NV[nm] = ds[k], nms[k], nvs[k]
    after = DL["w_ada"]
    for st in ("ffn", "mid", "in"):
        slots = _scattered(rs[st], after, me, name="rs_wait_" + st)
        for nm, sl in zip(stage_w[st], slots):
            G[nm], DL[nm], NM[nm], NV[nm] = adamw_slots(W2[nm], sl, M2[nm], V2[nm], name="adamw_" + nm)
            after = DL[nm]

    outs = [loss, r["dx"][None]]
    for grp in (G, DL, NM, NV):
        outs += [grp[nm].reshape(W[nm].shape) for nm in names]
    return tuple(outs)
```

```python
import functools
import numpy as np
import jax
import jax.numpy as jnp
from jax import lax
from jax.experimental import pallas as pl
from jax.experimental.pallas import tpu as pltpu

F32 = jnp.float32
BF = jnp.bfloat16
MESH = pl.DeviceIdType.MESH

D = 1024
T = 2048
TC = 256
TKV = T + TC
GRID_W = 64
NH = 8
DN = 64
DR = 32
DV = 64
QL = 384
KVL = 256
CONV = 512
DFF = 2816
EPS = 1e-6
ROPE_THETA = 10000.0
SCALE = (DN + DR) ** -0.5
NDEV = 8
HP = 128

O_GA, O_GC, O_KV, O_Q, O_CV = 0, 1024, 2048, 2560, 3072
NIN = 4608
CVB = 256
N_IN = 4256
SH_IN = N_IN // NDEV

LR, B1, B2, AEPS, WD, STEP = 0.001, 0.9, 0.999, 1e-08, 0.01, 10


def _pick(n, target, mult=128):
    best = None
    for d in range(mult, min(n, target) + 1, mult):
        if n % d == 0:
            best = d
    return best if best is not None else n


def _swap_start(g):
    return 8 * (g ^ 1)


def mm(a, b, *, ta=False, tb=False, out_dtype=F32, name, tm=512, tn=512, tk=2048, M=None, N=None, K=None,
       a_off=(0, 0), b_off=(0, 0), a_stack=False, b_stack=False, o_stack=False, dep=None):
    def dims(arr, stack):
        return (arr.shape[1], 2 * arr.shape[2]) if stack else arr.shape

    ar, ac = dims(a, a_stack)
    br, bc = dims(b, b_stack)
    M = M or ((ac if ta else ar) - a_off[1 if ta else 0])
    K = K or ((ar if ta else ac) - a_off[0 if ta else 1])
    N = N or ((br if tb else bc) - b_off[0 if tb else 1])
    tm = _pick(M, tm, 128 if ta else 16)
    tn = _pick(N // 2 if (o_stack or (b_stack and not tb)) else N, tn, 128)
    tk = _pick(K // 2 if ((a_stack and not ta) or (b_stack and tb)) else K, tk, 128)
    nk = K // tk
    ca = 0 if ta else 1
    cb = 1 if tb else 0

    def body(a_ref, b_ref, *rest):
        o_ref, acc = rest[-2:]
        k = pl.program_id(2)
        part = lax.dot_general(a_ref[...].astype(BF), b_ref[...].astype(BF),
                               (((ca,), (cb,)), ((), ())), preferred_element_type=F32)
        if nk == 1:
            o_ref[...] = part.astype(o_ref.dtype)
        else:
            @pl.when(k == 0)
            def _():
                acc[...] = part

            @pl.when(k > 0)
            def _():
                acc[...] += part

            @pl.when(k == nk - 1)
            def _():
                o_ref[...] = acc[...].astype(o_ref.dtype)

    def spec(blk, rc, off, stack, ncols):
        assert off[0] % blk[0] == 0 and off[1] % blk[1] == 0, (name, blk, off)
        ro, co = off[0] // blk[0], off[1] // blk[1]
        if not stack:
            return pl.BlockSpec(blk, lambda i, j, k: (rc(i, j, k)[0] + ro, rc(i, j, k)[1] + co))
        nhb = ncols // 2 // blk[1]
        return pl.BlockSpec((None,) + blk,
                            lambda i, j, k: ((rc(i, j, k)[1] + co) // nhb, rc(i, j, k)[0] + ro, (rc(i, j, k)[1] + co) % nhb))

    a_spec = spec((tk, tm), lambda i, j, k: (k, i), a_off, a_stack, ac) if ta else \
        spec((tm, tk), lambda i, j, k: (i, k), a_off, a_stack, ac)
    b_spec = spec((tn, tk), lambda i, j, k: (j, k), b_off, b_stack, bc) if tb else \
        spec((tk, tn), lambda i, j, k: (k, j), b_off, b_stack, bc)
    o_spec = spec((tm, tn), lambda i, j, k: (i, j), (0, 0), o_stack, N)
    o_shape = (2, M, N // 2) if o_stack else (M, N)
    deps = [] if dep is None else [dep]
    return pl.pallas_call(
        body, name=name, grid=(M // tm, N // tn, nk),
        in_specs=[a_spec, b_spec] + [pl.BlockSpec(memory_space=pl.ANY)] * len(deps),
        out_specs=o_spec, out_shape=jax.ShapeDtypeStruct(o_shape, out_dtype),
        scratch_shapes=[pltpu.VMEM((tm, tn) if nk > 1 else (8, 128), F32)],
        compiler_params=pltpu.CompilerParams(dimension_semantics=("parallel", "parallel", "arbitrary")),
    )(a, b, *deps)


def _row(width):
    return pl.BlockSpec((1, width), lambda *_: (0, 0))


NLAT = T // TC


def normmod_cat(ctx, x, g, csc, csh, sc, sh, dep, *, name, tm=256):
    assert tm == TC

    def body(c_ref, x_ref, g_ref, csc_ref, csh_ref, sc_ref, sh_ref, dep_ref, h_ref):
        last = pl.program_id(0) == NLAT
        xv = jnp.where(last, c_ref[...], x_ref[...])
        scv = jnp.where(last, csc_ref[...], sc_ref[...])
        shv = jnp.where(last, csh_ref[...], sh_ref[...])
        r = lax.rsqrt(jnp.mean(xv * xv, axis=-1, keepdims=True) + EPS)
        h_ref[...] = ((xv * r * g_ref[...]) * (1.0 + scv) + shv).astype(BF)

    return pl.pallas_call(
        body, name=name, grid=(TKV // tm,),
        in_specs=[pl.BlockSpec((tm, D), lambda i: (0, 0)), pl.BlockSpec((tm, D), lambda i: (jnp.minimum(i, NLAT - 1), 0)),
                  _row(D), _row(D), _row(D), _row(D), _row(D), pl.BlockSpec(memory_space=pl.ANY)],
        out_specs=pl.BlockSpec((tm, D), lambda i: (i, 0)), out_shape=jax.ShapeDtypeStruct((TKV, D), BF),
        compiler_params=pltpu.CompilerParams(dimension_semantics=("parallel",)),
    )(ctx, x, g, csc, csh, sc, sh, dep)


def resid_normmod(x, a, gate, g, sc, sh, *, name, tm=256):
    R = x.shape[0]

    def body(x_ref, a_ref, gate_ref, g_ref, sc_ref, sh_ref, x1_ref, h_ref):
        xv = x_ref[...] + gate_ref[...] * a_ref[...]
        x1_ref[...] = xv
        r = lax.rsqrt(jnp.mean(xv * xv, axis=-1, keepdims=True) + EPS)
        h_ref[...] = ((xv * r * g_ref[...]) * (1.0 + sc_ref[...]) + sh_ref[...]).astype(BF)

    blk = pl.BlockSpec((tm, D), lambda i: (i, 0))
    return pl.pallas_call(
        body, name=name, grid=(R // tm,), in_specs=[blk, blk, _row(D), _row(D), _row(D), _row(D)],
        out_specs=[blk, blk],
        out_shape=[jax.ShapeDtypeStruct((R, D), F32), jax.ShapeDtypeStruct((R, D), BF)],
        compiler_params=pltpu.CompilerParams(dimension_semantics=("parallel",)),
    )(x, a, gate, g, sc, sh)


def kvprep(pc, p, kvg, wkv2, ck, sk, *, name, tm=256):
    assert tm == TC
    nb = TKV // tm
    kvcol = O_KV // 512

    def body(pc_ref, p_ref, g_ref, w_ref, ck_ref, sk_ref, k_ref, v_ref, ckv_ref):
        i = pl.program_id(0)
        t = jnp.where(i == NLAT, pc_ref[...], p_ref[...])
        pk = t[:, :KVL]
        r = lax.rsqrt(jnp.mean(pk * pk, axis=-1, keepdims=True) + EPS)
        ckv = (pk * r * g_ref[...]).astype(BF)
        ckv_ref[...] = ckv
        kv2 = jnp.dot(ckv, w_ref[...], preferred_element_type=F32)
        krr = t[:, KVL:KVL + HP] * ck_ref[...] + t[:, KVL + HP:KVL + 2 * HP] * sk_ref[...]
        k_ref[...] = (kv2[:, :NH * HP] + jnp.concatenate([krr] * NH, axis=1)).astype(BF)
        v_ref[...] = kv2[:, NH * HP:].astype(BF)

    return pl.pallas_call(
        body, name=name, grid=(nb,),
        in_specs=[pl.BlockSpec((tm, 512), lambda i: (0, 0)),
                  pl.BlockSpec((tm, 512), lambda i: (jnp.minimum(i, NLAT - 1), kvcol)),
                  _row(KVL), pl.BlockSpec((KVL, NH * HP + NH * DV), lambda i: (0, 0)),
                  pl.BlockSpec((tm, HP), lambda i: (i, 0)), pl.BlockSpec((tm, HP), lambda i: (i, 0))],
        out_specs=[pl.BlockSpec((tm, NH * HP), lambda i: (i, 0)), pl.BlockSpec((tm, NH * DV), lambda i: (i, 0)),
                   pl.BlockSpec((tm, KVL), lambda i: (i, 0))],
        out_shape=[jax.ShapeDtypeStruct((TKV, NH * HP), BF), jax.ShapeDtypeStruct((TKV, NH * DV), BF),
                   jax.ShapeDtypeStruct((TKV, KVL), BF)],
        compiler_params=pltpu.CompilerParams(dimension_semantics=("parallel",)),
    )(pc, p, kvg, wkv2, ck, sk)


def qprep(p, qg, wq2, cq_t, sq_t, *, name, tm=256):
    qcol = O_Q // 512

    def body(p_ref, g_ref, w_ref, c_ref, s_ref, q_ref, cq_ref):
        pq = p_ref[...]
        r = lax.rsqrt(jnp.sum(pq * pq, axis=-1, keepdims=True) * (1.0 / QL) + EPS)
        cq = (pq * r * g_ref[...]).astype(BF)
        cq_ref[...] = cq
        q2 = jnp.dot(cq, w_ref[...], preferred_element_type=F32)
        cc = jnp.concatenate([c_ref[...]] * NH, axis=1)
        ss = jnp.concatenate([s_ref[...]] * NH, axis=1)
        q_ref[...] = (q2[:, :NH * HP] * cc + q2[:, NH * HP:] * ss).astype(BF)

    return pl.pallas_call(
        body, name=name, grid=(T // tm,),
        in_specs=[pl.BlockSpec((tm, 512), lambda i: (i, qcol)), _row(512),
                  pl.BlockSpec((512, 2 * NH * HP), lambda i: (0, 0)),
                  pl.BlockSpec((tm, HP), lambda i: (i, 0)), pl.BlockSpec((tm, HP), lambda i: (i, 0))],
        out_specs=[pl.BlockSpec((tm, NH * HP), lambda i: (i, 0)), pl.BlockSpec((tm, 512), lambda i: (i, 0))],
        out_shape=[jax.ShapeDtypeStruct((T, NH * HP), BF), jax.ShapeDtypeStruct((T, 512), BF)],
        compiler_params=pltpu.CompilerParams(dimension_semantics=("parallel",)),
    )(p, qg, wq2, cq_t, sq_t)


def _head_mask(h):
    lanes = lax.broadcasted_iota(jnp.int32, (1, 2 * DV), 1)
    return (lanes // DV) == (h % 2)


def attn_fwd(q, k, v, *, name, tq=256):
    def body(q_ref, k_ref, v_ref, o_ref):
        h = pl.program_id(1)
        s = lax.dot_general(q_ref[...], k_ref[...], (((1,), (1,)), ((), ())), preferred_element_type=F32) * SCALE
        m = jnp.max(s, axis=-1, keepdims=True)
        e = jnp.exp(s - m)
        pr = (e * (1.0 / jnp.sum(e, axis=-1, keepdims=True))).astype(BF)
        vm = jnp.where(_head_mask(h), v_ref[...], jnp.zeros_like(v_ref[...]))
        o2 = jnp.dot(pr, vm, preferred_element_type=F32).astype(BF)

        @pl.when(h % 2 == 0)
        def _():
            o_ref[...] = o2

        @pl.when(h % 2 == 1)
        def _():
            o_ref[...] = o_ref[...] + o2

    return pl.pallas_call(
        body, name=name, grid=(T // tq, NH),
        in_specs=[pl.BlockSpec((tq, HP), lambda i, h: (i, h)), pl.BlockSpec((TKV, HP), lambda i, h: (0, h)),
                  pl.BlockSpec((TKV, 2 * DV), lambda i, h: (0, h // 2))],
        out_specs=pl.BlockSpec((tq, 2 * DV), lambda i, h: (i, h // 2)),
        out_shape=jax.ShapeDtypeStruct((T, NH * DV), BF),
        compiler_params=pltpu.CompilerParams(dimension_semantics=("parallel", "arbitrary")),
    )(q, k, v)


def _shift_dn(x):
    n = x.shape[0]
    rows = lax.broadcasted_iota(jnp.int32, (n, 1), 0)
    return jnp.where(rows == 0, 0.0, pltpu.roll(x, 1, axis=0))


def _shift_up(x):
    n = x.shape[0]
    rows = lax.broadcasted_iota(jnp.int32, (n, 1), 0)
    return jnp.where(rows == n - 1, 0.0, pltpu.roll(x, n - 1, axis=0))


def _conv(x, w_ref, b_ref):
    return b_ref[...] + _shift_dn(x) * w_ref[0:1, :] + x * w_ref[1:2, :] + _shift_up(x) * w_ref[2:3, :]


def _conv_t(dy, w_ref):
    return _shift_up(dy) * w_ref[0:1, :] + dy * w_ref[1:2, :] + _shift_dn(dy) * w_ref[2:3, :]


def _conv_wgrad(dw_ref, dy, x):
    dw_ref[0:1, :] = jnp.sum(dy * _shift_dn(x), axis=0, keepdims=True)
    dw_ref[1:2, :] = jnp.sum(dy * x, axis=0, keepdims=True)
    dw_ref[2:3, :] = jnp.sum(dy * _shift_up(x), axis=0, keepdims=True)


def convz(p, cw, cb, *, name):
    o0 = O_CV // (3 * CVB)

    def body(p_ref, w_ref, bias_ref, z_ref):
        xv, bv, cv = p_ref[:, 0:CVB], p_ref[:, CVB:2 * CVB], p_ref[:, 2 * CVB:3 * CVB]
        z_ref[...] = (bv * _conv(cv * xv, w_ref, bias_ref)).astype(BF)

    return pl.pallas_call(
        body, name=name, grid=(CONV // CVB,),
        in_specs=[pl.BlockSpec((T, 3 * CVB), lambda j: (0, o0 + j)), pl.BlockSpec((3, CVB), lambda j: (0, j)),
                  pl.BlockSpec((1, CVB), lambda j: (0, j))],
        out_specs=pl.BlockSpec((T, CVB), lambda j: (0, j)),
        out_shape=jax.ShapeDtypeStruct((T, CONV), BF),
        compiler_params=pltpu.CompilerParams(dimension_semantics=("parallel",)),
    )(p, cw, cb)


def gate_merge(p, ya, yc, *, name, tm=256):
    def body(ga_ref, gc_ref, ya_ref, yc_ref, o_ref):
        o_ref[...] = (jax.nn.sigmoid(ga_ref[...]) * ya_ref[...] + jax.nn.sigmoid(gc_ref[...]) * yc_ref[...]).astype(BF)

    blk = pl.BlockSpec((tm, D), lambda i: (i, 0))
    return pl.pallas_call(
        body, name=name, grid=(T // tm,),
        in_specs=[pl.BlockSpec((tm, D), lambda i: (i, O_GA // D)), pl.BlockSpec((tm, D), lambda i: (i, O_GC // D)), blk, blk],
        out_specs=blk, out_shape=jax.ShapeDtypeStruct((T, D), BF),
        compiler_params=pltpu.CompilerParams(dimension_semantics=("parallel",)),
    )(p, p, ya, yc)


def ffn_act(u0, cw, cb, *, name, tc=256):
    nb = DFF // tc

    def body(u_ref, wg_ref, wv_ref, bg_ref, bv_ref, f_ref):
        ug = _conv(u_ref[0], wg_ref, bg_ref)
        uv = _conv(u_ref[1], wv_ref, bv_ref)
        f_ref[...] = (ug * jax.nn.sigmoid(ug) * uv).astype(BF)

    return pl.pallas_call(
        body, name=name, grid=(nb,),
        in_specs=[pl.BlockSpec((2, T, tc), lambda j: (0, 0, j)),
                  pl.BlockSpec((3, tc), lambda j: (0, j)), pl.BlockSpec((3, tc), lambda j: (0, nb + j)),
                  pl.BlockSpec((1, tc), lambda j: (0, j)), pl.BlockSpec((1, tc), lambda j: (0, nb + j))],
        out_specs=pl.BlockSpec((T, tc), lambda j: (0, j)),
        out_shape=jax.ShapeDtypeStruct((T, DFF), BF),
        compiler_params=pltpu.CompilerParams(dimension_semantics=("parallel",)),
    )(u0, cw, cw, cb, cb)


def final_loss(x1, d, g2, fg, tgt, *, name, tm=256):
    def body(x1_ref, d_ref, g2_ref, fg_ref, t_ref, dx_ref, dd_ref, dfg_ref, loss_ref):
        i = pl.program_id(0)
        xv = x1_ref[...] + g2_ref[...] * d_ref[...]
        r = lax.rsqrt(jnp.mean(xv * xv, axis=-1, keepdims=True) + EPS)
        xh = xv * r
        diff = xh * fg_ref[...] - t_ref[...]
        part = 0.5 * jnp.sum(jnp.mean(diff * diff, axis=-1, keepdims=True), axis=0, keepdims=True)
        dy = diff * (1.0 / D)
        a = dy * fg_ref[...]
        dx = r * (a - xh * jnp.mean(a * xh, axis=-1, keepdims=True))
        dx_ref[...] = dx
        dd_ref[...] = (dx * g2_ref[...]).astype(BF)
        dfg = jnp.sum(dy * xh, axis=0, keepdims=True)

        @pl.when(i == 0)
        def _():
            dfg_ref[...] = dfg
            loss_ref[...] = jnp.broadcast_to(part, (1, 128))

        @pl.when(i > 0)
        def _():
            dfg_ref[...] += dfg
            loss_ref[...] += jnp.broadcast_to(part, (1, 128))

    blk = pl.BlockSpec((tm, D), lambda i: (i, 0))
    return pl.pallas_call(
        body, name=name, grid=(T // tm,), in_specs=[blk, blk, _row(D), _row(D), blk],
        out_specs=[blk, blk, _row(D), _row(128)],
        out_shape=[jax.ShapeDtypeStruct((T, D), F32), jax.ShapeDtypeStruct((T, D), BF),
                   jax.ShapeDtypeStruct((1, D), F32), jax.ShapeDtypeStruct((1, 128), F32)],
        compiler_params=pltpu.CompilerParams(dimension_semantics=("arbitrary",)),
    )(x1, d, g2, fg, tgt)


def normmod_bwd(x, dh, g, sc, dres, gsrc, gate, *, name, tm=256):
    R = x.shape[0]
    has_res = dres is not None

    def body(*refs):
        if has_res:
            x_ref, dh_ref, g_ref, sc_ref, dres_ref, gsrc_ref, gate_ref, dx_ref, dxg_ref, st_ref = refs
        else:
            x_ref, dh_ref, g_ref, sc_ref, st_ref = refs
        i = pl.program_id(0)
        xv = x_ref[...]
        r = lax.rsqrt(jnp.mean(xv * xv, axis=-1, keepdims=True) + EPS)
        xh = xv * r
        dhv = dh_ref[...]
        n = xh * g_ref[...]
        dn = dhv * (1.0 + sc_ref[...])
        a = dn * g_ref[...]
        rows = [jnp.sum(dhv, axis=0, keepdims=True), jnp.sum(dhv * n, axis=0, keepdims=True),
                jnp.sum(dn * xh, axis=0, keepdims=True)]
        if has_res:
            dr = dres_ref[...]
            dx = dr + r * (a - xh * jnp.mean(a * xh, axis=-1, keepdims=True))
            dx_ref[...] = dx
            dxg_ref[...] = (dx * gate_ref[...]).astype(BF)
            rows.append(jnp.sum(dr * gsrc_ref[...], axis=0, keepdims=True))
        else:
            rows.append(jnp.zeros((1, D), F32))

        @pl.when(i == 0)
        def _():
            for k, row in enumerate(rows):
                st_ref[k:k + 1, :] = row

        @pl.when(i > 0)
        def _():
            for k, row in enumerate(rows):
                st_ref[k:k + 1, :] += row

    blk = pl.BlockSpec((tm, D), lambda i: (i, 0))
    st_spec = pl.BlockSpec((4, D), lambda i: (0, 0))
    st_shape = jax.ShapeDtypeStruct((4, D), F32)
    cp = pltpu.CompilerParams(dimension_semantics=("arbitrary",))
    if has_res:
        return pl.pallas_call(
            body, name=name, grid=(R // tm,), in_specs=[blk, blk, _row(D), _row(D), blk, blk, _row(D)],
            out_specs=[blk, blk, st_spec],
            out_shape=[jax.ShapeDtypeStruct((R, D), F32), jax.ShapeDtypeStruct((R, D), BF), st_shape],
            compiler_params=cp,
        )(x, dh, g, sc, dres, gsrc, gate)
    return pl.pallas_call(
        body, name=name, grid=(R // tm,), in_specs=[blk, blk, _row(D), _row(D)],
        out_specs=st_spec, out_shape=st_shape, compiler_params=cp,
    )(x, dh, g, sc)


def ffn_act_bwd(u0, df, cw, cb, *, name, tc=256):
    nb = DFF // tc

    def body(u_ref, df_ref, wg_ref, wv_ref, bg_ref, bv_ref, du_ref, dw_ref, db_ref):
        xg, xv = u_ref[0], u_ref[1]
        ug = _conv(xg, wg_ref, bg_ref)
        uv = _conv(xv, wv_ref, bv_ref)
        sig = jax.nn.sigmoid(ug)
        dfv = df_ref[...]
        dug = dfv * uv * (sig * (1.0 + ug * (1.0 - sig)))
        duv = dfv * (ug * sig)
        du_ref[0] = _conv_t(dug, wg_ref).astype(BF)
        du_ref[1] = _conv_t(duv, wv_ref).astype(BF)
        _conv_wgrad(dw_ref.at[0], dug, xg)
        _conv_wgrad(dw_ref.at[1], duv, xv)
        db_ref[0] = jnp.sum(dug, axis=0, keepdims=True)
        db_ref[1] = jnp.sum(duv, axis=0, keepdims=True)

    lo = lambda r: pl.BlockSpec((r, tc), lambda j: (0, j))
    hi = lambda r: pl.BlockSpec((r, tc), lambda j: (0, nb + j))
    st = lambda r: pl.BlockSpec((2, r, tc), lambda j: (0, 0, j))
    return pl.pallas_call(
        body, name=name, grid=(nb,),
        in_specs=[st(T), lo(T), lo(3), hi(3), lo(1), hi(1)],
        out_specs=[st(T), st(3), st(1)],
        out_shape=[jax.ShapeDtypeStruct((2, T, DFF), BF), jax.ShapeDtypeStruct((2, 3, DFF), F32),
                   jax.ShapeDtypeStruct((2, 1, DFF), F32)],
        compiler_params=pltpu.CompilerParams(dimension_semantics=("parallel",)),
    )(u0, df, cw, cw, cb, cb)


def gate_merge_bwd(p, ya, yc, dm, *, name, tm=256):
    def body(ga_ref, gc_ref, ya_ref, yc_ref, dm_ref, dya_ref, dyc_ref, dp_ref):
        sa, sc_ = jax.nn.sigmoid(ga_ref[...]), jax.nn.sigmoid(gc_ref[...])
        dmv = dm_ref[...]
        dya_ref[...] = (dmv * sa).astype(BF)
        dyc_ref[...] = (dmv * sc_).astype(BF)
        dp_ref[:, 0:D] = (dmv * ya_ref[...] * (sa * (1.0 - sa))).astype(BF)
        dp_ref[:, D:2 * D] = (dmv * yc_ref[...] * (sc_ * (1.0 - sc_))).astype(BF)

    blk = pl.BlockSpec((tm, D), lambda i: (i, 0))
    sh = jax.ShapeDtypeStruct((T, D), BF)
    return pl.pallas_call(
        body, name=name, grid=(T // tm,),
        in_specs=[pl.BlockSpec((tm, D), lambda i: (i, O_GA // D)), pl.BlockSpec((tm, D), lambda i: (i, O_GC // D)), blk, blk, blk],
        out_specs=[blk, blk, pl.BlockSpec((tm, 2 * D), lambda i: (i, 0))],
        out_shape=[sh, sh, jax.ShapeDtypeStruct((T, NIN), BF)],
        compiler_params=pltpu.CompilerParams(dimension_semantics=("parallel",)),
    )(p, p, ya, yc, dm)


def convz_bwd(p, dz, cw, cb, dp, *, name):
    o0 = O_CV // (3 * CVB)

    def body(p_ref, dz_ref, w_ref, bias_ref, dp_in, dp_ref, dw_ref, dbias_ref):
        xv, bv, cv = p_ref[:, 0:CVB], p_ref[:, CVB:2 * CVB], p_ref[:, 2 * CVB:3 * CVB]
        ci = cv * xv
        dwc = _conv(ci, w_ref, bias_ref)
        dzv = dz_ref[...]
        ddw = dzv * bv
        dci = _conv_t(ddw, w_ref)
        dp_ref[:, 0:CVB] = (dci * cv).astype(BF)
        dp_ref[:, CVB:2 * CVB] = (dzv * dwc).astype(BF)
        dp_ref[:, 2 * CVB:3 * CVB] = (dci * xv).astype(BF)
        _conv_wgrad(dw_ref, ddw, ci)
        dbias_ref[...] = jnp.sum(ddw, axis=0, keepdims=True)

    own = lambda r: pl.BlockSpec((r, CVB), lambda j: (0, j))
    return pl.pallas_call(
        body, name=name, grid=(CONV // CVB,),
        in_specs=[pl.BlockSpec((T, 3 * CVB), lambda j: (0, o0 + j)), own(T), own(3), own(1),
                  pl.BlockSpec(memory_space=pl.ANY)],
        out_specs=[pl.BlockSpec((T, 3 * CVB), lambda j: (0, o0 + j)), own(3), own(1)],
        out_shape=[jax.ShapeDtypeStruct((T, NIN), BF), jax.ShapeDtypeStruct((3, CONV), F32),
                   jax.ShapeDtypeStruct((1, CONV), F32)],
        input_output_aliases={4: 0},
        compiler_params=pltpu.CompilerParams(dimension_semantics=("parallel",)),
    )(p, dz, cw, cb, dp)


def attn_bwd(q, k, v, do, *, name, tq=256):
    def body(q_ref, k_ref, v_ref, do_ref, dq_ref, dk_ref, dv_ref):
        h, i = pl.program_id(0), pl.program_id(1)
        qv, kv = q_ref[...], k_ref[...]
        s = lax.dot_general(qv, kv, (((1,), (1,)), ((), ())), preferred_element_type=F32) * SCALE
        m = jnp.max(s, axis=-1, keepdims=True)
        e = jnp.exp(s - m)
        pr = e * (1.0 / jnp.sum(e, axis=-1, keepdims=True))
        mask = _head_mask(h)
        vm = jnp.where(mask, v_ref[...], jnp.zeros_like(v_ref[...]))
        dom = jnp.where(mask, do_ref[...], jnp.zeros_like(do_ref[...]))
        dp = lax.dot_general(dom, vm, (((1,), (1,)), ((), ())), preferred_element_type=F32)
        ds = (pr * (dp - jnp.sum(pr * dp, axis=-1, keepdims=True)) * SCALE).astype(BF)
        dq_ref[...] = jnp.dot(ds, kv, preferred_element_type=F32)
        dk = lax.dot_general(ds, qv, (((0,), (0,)), ((), ())), preferred_element_type=F32)
        dv = lax.dot_general(pr.astype(BF), dom, (((0,), (0,)), ((), ())), preferred_element_type=F32)

        @pl.when(i == 0)
        def _():
            dk_ref[...] = dk

        @pl.when(i > 0)
        def _():
            dk_ref[...] += dk

        @pl.when((i == 0) & (h % 2 == 0))
        def _():
            dv_ref[...] = dv

        @pl.when((i > 0) | (h % 2 == 1))
        def _():
            dv_ref[...] += dv

    return pl.pallas_call(
        body, name=name, grid=(NH, T // tq),
        in_specs=[pl.BlockSpec((tq, HP), lambda h, i: (i, h)), pl.BlockSpec((TKV, HP), lambda h, i: (0, h)),
                  pl.BlockSpec((TKV, 2 * DV), lambda h, i: (0, h // 2)), pl.BlockSpec((tq, 2 * DV), lambda h, i: (i, h // 2))],
        out_specs=[pl.BlockSpec((tq, HP), lambda h, i: (i, h)), pl.BlockSpec((TKV, HP), lambda h, i: (0, h)),
                   pl.BlockSpec((TKV, 2 * DV), lambda h, i: (0, h // 2))],
        out_shape=[jax.ShapeDtypeStruct((T, NH * HP), F32), jax.ShapeDtypeStruct((TKV, NH * HP), F32),
                   jax.ShapeDtypeStruct((TKV, NH * DV), F32)],
        compiler_params=pltpu.CompilerParams(dimension_semantics=("arbitrary", "arbitrary")),
    )(q, k, v, do)


def qprep_bwd(p, dq, qg, wq2, cq_t, sq_t, dp, *, name, tm=256):
    qcol = O_Q // 512

    def body(p_ref, dq_ref, g_ref, w_ref, c_ref, s_ref, dp_in, dp_ref, dq2_ref, dg_ref):
        i = pl.program_id(0)
        dqv = dq_ref[...]
        cc = jnp.concatenate([c_ref[...]] * NH, axis=1)
        ss = jnp.concatenate([s_ref[...]] * NH, axis=1)
        dq2 = jnp.concatenate([dqv * cc, dqv * ss], axis=1).astype(BF)
        dq2_ref[...] = dq2
        dcq = lax.dot_general(dq2, w_ref[...], (((1,), (1,)), ((), ())), preferred_element_type=F32)
        pq = p_ref[...]
        r = lax.rsqrt(jnp.sum(pq * pq, axis=-1, keepdims=True) * (1.0 / QL) + EPS)
        xh = pq * r
        a = dcq * g_ref[...]
        dp_ref[...] = (r * (a - xh * (jnp.sum(a * xh, axis=-1, keepdims=True) * (1.0 / QL)))).astype(BF)
        dg = jnp.sum(dcq * xh, axis=0, keepdims=True)

        @pl.when(i == 0)
        def _():
            dg_ref[...] = dg

        @pl.when(i > 0)
        def _():
            dg_ref[...] += dg

    return pl.pallas_call(
        body, name=name, grid=(T // tm,),
        in_specs=[pl.BlockSpec((tm, 512), lambda i: (i, qcol)), pl.BlockSpec((tm, NH * HP), lambda i: (i, 0)), _row(512),
                  pl.BlockSpec((512, 2 * NH * HP), lambda i: (0, 0)),
                  pl.BlockSpec((tm, HP), lambda i: (i, 0)), pl.BlockSpec((tm, HP), lambda i: (i, 0)),
                  pl.BlockSpec(memory_space=pl.ANY)],
        out_specs=[pl.BlockSpec((tm, 512), lambda i: (i, qcol)), pl.BlockSpec((tm, 2 * NH * HP), lambda i: (i, 0)), _row(512)],
        out_shape=[jax.ShapeDtypeStruct((T, NIN), BF), jax.ShapeDtypeStruct((T, 2 * NH * HP), BF),
                   jax.ShapeDtypeStruct((1, 512), F32)],
        input_output_aliases={6: 0},
        compiler_params=pltpu.CompilerParams(dimension_semantics=("arbitrary",)),
    )(p, dq, qg, wq2, cq_t, sq_t, dp)


def kvprep_bwd(pc, p, dk, dv, kvg, wkv2, ck, sk, dp, *, name, tm=256):
    assert tm == TC
    nb = TKV // tm
    kvcol = O_KV // 512

    def body(pc_ref, p_ref, dk_ref, dv_ref, g_ref, w_ref, ck_ref, sk_ref, dp_in, dp_ref, dpc_ref, dkv2_ref, dg_ref):
        i = pl.program_id(0)
        t = jnp.where(i == NLAT, pc_ref[...], p_ref[...])
        pk = t[:, :KVL]
        r = lax.rsqrt(jnp.mean(pk * pk, axis=-1, keepdims=True) + EPS)
        xh = pk * r
        dkv = dk_ref[...]
        dkv2 = jnp.concatenate([dkv, dv_ref[...]], axis=1).astype(BF)
        dkv2_ref[...] = dkv2
        dckv = lax.dot_general(dkv2, w_ref[...], (((1,), (1,)), ((), ())), preferred_element_type=F32)
        a = dckv * g_ref[...]
        dpk = r * (a - xh * jnp.mean(a * xh, axis=-1, keepdims=True))
        dkr = dkv[:, 0:HP]
        for hh in range(1, NH):
            dkr = dkr + dkv[:, hh * HP:(hh + 1) * HP]
        res = jnp.concatenate([dpk, dkr * ck_ref[...], dkr * sk_ref[...]], axis=1).astype(BF)
        dg = jnp.sum(dckv * xh, axis=0, keepdims=True)

        @pl.when(i == 0)
        def _():
            dg_ref[...] = dg

        @pl.when(i > 0)
        def _():
            dg_ref[...] += dg

        @pl.when(i < NLAT)
        def _():
            dp_ref[...] = res

        @pl.when(i == NLAT)
        def _():
            dpc_ref[...] = res

    rb = lambda w: pl.BlockSpec((tm, w), lambda i: (i, 0))
    return pl.pallas_call(
        body, name=name, grid=(nb,),
        in_specs=[pl.BlockSpec((tm, 512), lambda i: (0, 0)),
                  pl.BlockSpec((tm, 512), lambda i: (jnp.minimum(i, NLAT - 1), kvcol)),
                  rb(NH * HP), rb(NH * DV), _row(KVL), pl.BlockSpec((KVL, NH * HP + NH * DV), lambda i: (0, 0)),
                  rb(HP), rb(HP), pl.BlockSpec(memory_space=pl.ANY)],
        out_specs=[pl.BlockSpec((tm, 512), lambda i: (jnp.minimum(i, NLAT - 1), kvcol)),
                   pl.BlockSpec((tm, 512), lambda i: (0, 0)), rb(NH * HP + NH * DV), _row(KVL)],
        out_shape=[jax.ShapeDtypeStruct((T, NIN), BF), jax.ShapeDtypeStruct((TC, 512), BF),
                   jax.ShapeDtypeStruct((TKV, NH * HP + NH * DV), BF), jax.ShapeDtypeStruct((1, KVL), F32)],
        input_output_aliases={8: 0},
        compiler_params=pltpu.CompilerParams(dimension_semantics=("arbitrary",)),
    )(pc, p, dk, dv, kvg, wkv2, ck, sk, dp)


def _pieces(src, width, n):
    out, c = [], src
    while c < src + width:
        k = c // n
        w = min(src + width, (k + 1) * n) - c
        out.append((k, c - k * n, c - src, w))
        c += w
    return out


def _win_moves():
    mv = [(2208, 1024, O_GA), (3232, 1024, O_GC), (0, KVL, O_KV), (256, DR, O_KV + KVL + DN), (288, QL, O_Q)]
    mv += [(256 + _swap_start(g), 8, O_KV + KVL + HP + DN + 8 * g) for g in range(4)]
    for j in range(CONV // CVB):
        base = O_CV + 3 * CVB * j
        mv += [(672 + CVB * j, CVB, base), (1184 + CVB * j, CVB, base + CVB), (1696 + CVB * j, CVB, base + 2 * CVB)]
    return mv


_WIN_ZERO = [(O_KV + KVL, DN), (O_KV + KVL + DN + DR, HP - DN - DR), (O_KV + KVL + HP, DN),
             (O_KV + KVL + HP + DN + DR, HP - DN - DR), (O_Q + QL, 512 - QL)]


def build_win(g, *, name, tm=256):
    def body(g_ref, o_ref):
        for src, w, dst in _win_moves():
            for k, a, off, pw in _pieces(src, w, SH_IN):
                o_ref[:, dst + off:dst + off + pw] = g_ref[k, :, a:a + pw]
        for c0, w in _WIN_ZERO:
            o_ref[:, c0:c0 + w] = jnp.zeros((tm, w), o_ref.dtype)

    return pl.pallas_call(
        body, name=name, grid=(D // tm,), in_specs=[pl.BlockSpec((NDEV, tm, SH_IN), lambda i: (0, i, 0))],
        out_specs=pl.BlockSpec((tm, NIN), lambda i: (i, 0)), out_shape=jax.ShapeDtypeStruct((D, NIN), g.dtype),
        compiler_params=pltpu.CompilerParams(dimension_semantics=("parallel",)),
    )(g)


def shard_win_grad(dw, dwc, *, name, tm=256):
    def body(dw_ref, dwc_ref, o_ref, kvs):
        kvs[...] = dw_ref[:, O_KV:O_KV + 512] + dwc_ref[...]

        def src(col, w):
            if O_KV <= col < O_KV + 512:
                return kvs[:, col - O_KV:col - O_KV + w]
            return dw_ref[:, col:col + w]

        for s, w, dst in _win_moves():
            if w == 8 or s == 256:
                continue
            for k, a, off, pw in _pieces(s, w, SH_IN):
                o_ref[k, :, a:a + pw] = src(dst + off, pw).astype(o_ref.dtype)
        for g in range(4):
            val = src(O_KV + KVL + DN + 8 * g, 8) + src(O_KV + KVL + HP + DN + _swap_start(g), 8)
            o_ref[0, :, 256 + 8 * g:256 + 8 * g + 8] = val.astype(o_ref.dtype)

    return pl.pallas_call(
        body, name=name, grid=(D // tm,),
        in_specs=[pl.BlockSpec((tm, NIN), lambda i: (i, 0)), pl.BlockSpec((tm, 512), lambda i: (i, 0))],
        out_specs=pl.BlockSpec((NDEV, tm, SH_IN), lambda i: (0, i, 0)),
        out_shape=jax.ShapeDtypeStruct((NDEV, D, SH_IN), BF),
        scratch_shapes=[pltpu.VMEM((tm, 512), F32)],
        compiler_params=pltpu.CompilerParams(dimension_semantics=("parallel",)),
    )(dw, dwc)


def build_wq_wkv(gq, gkv, *, name):
    def body(gq_ref, gkv_ref, q_ref, kv_ref):
        q_ref[...] = jnp.zeros_like(q_ref)
        kv_ref[...] = jnp.zeros_like(kv_ref)
        for h in range(NH):
            q_ref[0:QL, h * HP:h * HP + DN + DR] = gq_ref[h]
            for g in range(4):
                c0 = NH * HP + h * HP + DN + 8 * g
                q_ref[0:QL, c0:c0 + 8] = gq_ref[h, :, DN + _swap_start(g):DN + _swap_start(g) + 8]
            kv_ref[:, h * HP:h * HP + DN] = gkv_ref[h, :, 0:DN]
            kv_ref[:, NH * HP + h * DV:NH * HP + (h + 1) * DV] = gkv_ref[h, :, DN:DN + DV]

    vm = pl.BlockSpec(memory_space=pltpu.VMEM)
    return pl.pallas_call(
        body, name=name, in_specs=[vm, vm], out_specs=[vm, vm],
        out_shape=[jax.ShapeDtypeStruct((512, 2 * NH * HP), gq.dtype), jax.ShapeDtypeStruct((KVL, NH * HP + NH * DV), gq.dtype)],
    )(gq, gkv)


def shard_wq_wkv_grad(dwq2, dwkv2, *, name):
    def body(q_ref, kv_ref, gq_ref, gkv_ref):
        for h in range(NH):
            gq_ref[h, :, 0:DN] = q_ref[0:QL, h * HP:h * HP + DN].astype(BF)
            for g in range(4):
                a = q_ref[0:QL, h * HP + DN + 8 * g:h * HP + DN + 8 * g + 8]
                c0 = NH * HP + h * HP + DN + _swap_start(g)
                gq_ref[h, :, DN + 8 * g:DN + 8 * g + 8] = (a + q_ref[0:QL, c0:c0 + 8]).astype(BF)
            gkv_ref[h, :, 0:DN] = kv_ref[:, h * HP:h * HP + DN].astype(BF)
            gkv_ref[h, :, DN:DN + DV] = kv_ref[:, NH * HP + h * DV:NH * HP + (h + 1) * DV].astype(BF)

    vm = pl.BlockSpec(memory_space=pltpu.VMEM)
    return pl.pallas_call(
        body, name=name, in_specs=[vm, vm], out_specs=[vm, vm],
        out_shape=[jax.ShapeDtypeStruct((NDEV, QL, (DN + DR)), BF), jax.ShapeDtypeStruct((NDEV, KVL, DN + DV), BF)],
    )(dwq2, dwkv2)


def unshard_cols(g, *, name, tm=256):
    _, K, n = g.shape
    tm = _pick(K, tm, 16)

    def body(g_ref, o_ref):
        for k in range(NDEV):
            o_ref[:, k * n:(k + 1) * n] = g_ref[k]

    return pl.pallas_call(
        body, name=name, grid=(K // tm,), in_specs=[pl.BlockSpec((NDEV, tm, n), lambda i: (0, i, 0))],
        out_specs=pl.BlockSpec((tm, NDEV * n), lambda i: (i, 0)), out_shape=jax.ShapeDtypeStruct((K, NDEV * n), g.dtype),
        compiler_params=pltpu.CompilerParams(dimension_semantics=("parallel",)),
    )(g)


def shard_cols(w, *, name, tm=256):
    K, n8 = w.shape
    n = n8 // NDEV
    tm = _pick(K, tm, 16)

    def body(w_ref, o_ref):
        for k in range(NDEV):
            o_ref[k] = w_ref[:, k * n:(k + 1) * n]

    return pl.pallas_call(
        body, name=name, grid=(K // tm,), in_specs=[pl.BlockSpec((tm, n8), lambda i: (i, 0))],
        out_specs=pl.BlockSpec((NDEV, tm, n), lambda i: (0, i, 0)), out_shape=jax.ShapeDtypeStruct((NDEV, K, n), w.dtype),
        compiler_params=pltpu.CompilerParams(dimension_semantics=("parallel",)),
    )(w)


def _rope_tables():
    t = np.arange(T)
    row = (t // GRID_W).astype(np.float32)
    col = (t % GRID_W).astype(np.float32)
    axis_dim = DR // 2
    inv = (np.float32(ROPE_THETA) ** (-np.arange(0, axis_dim, 2, dtype=np.float32) / np.float32(axis_dim))).astype(np.float32)
    ar, ac = (row[:, None] * inv).astype(np.float32), (col[:, None] * inv).astype(np.float32)
    cosv = np.concatenate([np.cos(ar), np.cos(ar), np.cos(ac), np.cos(ac)], axis=1).astype(np.float32)
    sinv = np.concatenate([-np.sin(ar), np.sin(ar), -np.sin(ac), np.sin(ac)], axis=1).astype(np.float32)
    ck = np.zeros((TKV, HP), np.float32)
    sk = np.zeros((TKV, HP), np.float32)
    ck[T:, DN:DN + DR] = 1.0
    ck[:T, DN:DN + DR] = cosv
    sk[:T, DN:DN + DR] = sinv
    cq = np.zeros((T, HP), np.float32)
    cq[:, :DN] = 1.0
    cq[:, DN:DN + DR] = cosv
    return jnp.asarray(ck), jnp.asarray(sk), jnp.asarray(cq), jnp.asarray(sk[:T])


def _local_step(x, ctx, tgt, mod_lat, mod_ctx, n1g, qg, kvg, n2g, fg, conv_w, conv_b, ffn_w, ffn_b, get_w, put_g, dep0):
    sh1, sc1, g1, sh2, sc2, g2 = [mod_lat[:, i * D:(i + 1) * D] for i in range(6)]
    csh1, csc1 = mod_ctx[:, 0:D], mod_ctx[:, D:2 * D]
    ck, sk, cq_t, sq_t = _rope_tables()
    qg_p = jnp.pad(qg, ((0, 0), (0, 512 - QL)))

    hcat = normmod_cat(ctx, x, n1g, csc1, csh1, sc1, sh1, dep0, name="normmod1")
    win = get_w("in", hcat)
    p = mm(hcat, win, M=T, name="in_proj")
    pc = mm(hcat, win, M=TC, N=512, a_off=(T, 0), b_off=(0, O_KV), name="in_proj_ctx")
    wq2, wkv2, wao, wco, wo = get_w("mid", p)
    kh, vh, ckv = kvprep(pc, p, kvg, wkv2, ck, sk, name="kvprep")
    qr, cq = qprep(p, qg_p, wq2, cq_t, sq_t, name="qprep")
    o = attn_fwd(qr, kh, vh, name="attn_fwd")
    z = convz(p, conv_w, conv_b, name="convz")
    ya = mm(o, wao, name="attn_out")
    yc = mm(z, wco, name="conv_out")
    merged = gate_merge(p, ya, yc, name="gate_merge")
    a_out = mm(merged, wo, name="o_proj")
    x1, h2 = resid_normmod(x, a_out, g1, n2g, sc2, sh2, name="resid_normmod2")
    wup, wdn = get_w("ffn", h2)
    u0 = mm(h2, wup, o_stack=True, tn=1408, name="up_proj")
    f = ffn_act(u0, ffn_w, ffn_b, name="ffn_act")
    dn = mm(f, wdn, name="down_proj")
    dx2, dd, dfg, loss = final_loss(x1, dn, g2, fg, tgt, name="final_loss")

    df = mm(dd, wdn, tb=True, name="down_proj_dx")
    dwdn = mm(f, dd, ta=True, out_dtype=BF, name="down_proj_dw")
    du0, dffn_w, dffn_b = ffn_act_bwd(u0, df, ffn_w, ffn_b, name="ffn_act_bwd")
    dwup = mm(h2, du0, ta=True, b_stack=True, out_dtype=BF, tn=1408, name="up_proj_dw")
    tok = put_g("ffn", dict(dwup=dwup, dwdn=dwdn))
    dh2 = mm(du0, wup, tb=True, a_stack=True, dep=tok, name="up_proj_dx")
    dx1, da, st2 = normmod_bwd(x1, dh2, n2g, sc2, dx2, dn, g1, name="normmod2_bwd")

    dmerged = mm(da, wo, tb=True, name="o_proj_dx")
    dwo = mm(merged, da, ta=True, out_dtype=BF, name="o_proj_dw")
    dya, dyc, dp = gate_merge_bwd(p, ya, yc, dmerged, name="gate_merge_bwd")
    do = mm(dya, wao, tb=True, out_dtype=BF, name="attn_out_dx")
    dwao = mm(o, dya, ta=True, out_dtype=BF, name="attn_out_dw")
    dwco = mm(z, dyc, ta=True, out_dtype=BF, name="conv_out_dw")
    tok = put_g("mid", dict(dwao=dwao, dwco=dwco, dwo=dwo))
    dz = mm(dyc, wco, tb=True, dep=tok, name="conv_out_dx")
    dp, dconv_w, dconv_b = convz_bwd(p, dz, conv_w, conv_b, dp, name="convz_bwd")
    dq, dk, dv = attn_bwd(qr, kh, vh, do, name="attn_bwd")
    dp, dq2, dqg = qprep_bwd(p, dq, qg_p, wq2, cq_t, sq_t, dp, name="qprep_bwd")
    dwq2 = mm(cq, dq2, ta=True, name="q_up_dw")
    dp, dpc, dkv2, dkvg = kvprep_bwd(pc, p, dk, dv, kvg, wkv2, ck, sk, dp, name="kvprep_bwd")
    dwkv2 = mm(ckv, dkv2, ta=True, name="kv_up_dw")
    tok = put_g("qkv", dict(dwq2=dwq2, dwkv2=dwkv2))

    dwin = mm(hcat, dp, ta=True, K=T, dep=tok, name="in_proj_dw")
    dwin_c = mm(hcat, dpc, ta=True, K=TC, a_off=(T, 0), name="in_proj_ctx_dw")
    tok = put_g("in", dict(dwin=dwin, dwin_c=dwin_c))
    dh = mm(dp, win, tb=True, dep=tok, name="in_proj_dx")
    dhc = mm(dpc, win, tb=True, N=D, K=512, b_off=(0, O_KV), name="in_proj_ctx_dx")
    dx, _, st1 = normmod_bwd(x, dh, n1g, sc1, dx1, a_out, g1, name="normmod1_bwd")
    stc = normmod_bwd(ctx, dhc, n1g, csc1, None, None, None, name="normmod1_ctx_bwd")

    zrow = jnp.zeros((1, D), F32)
    dmod_lat = jnp.concatenate([st1[0:1], st1[1:2], st1[3:4], st2[0:1], st2[1:2], st2[3:4]], axis=1)
    dmod_ctx = jnp.concatenate([stc[0:1], stc[1:2], zrow, zrow, zrow, zrow], axis=1)
    return dict(
        loss=loss, dx=dx, dmod_lat=dmod_lat, dmod_ctx=dmod_ctx,
        dn1g=st1[2:3] + stc[2:3], dqg=dqg, dkvg=dkvg, dn2g=st2[2:3], dfg=dfg,
        dconv_w=dconv_w, dconv_b=dconv_b, dffn_w=dffn_w, dffn_b=dffn_b)


def _me():
    x, y, c = lax.axis_index("x"), lax.axis_index("y"), lax.axis_index("c")
    return x, y, c, 4 * x + 2 * y + c


def _peer(x, y, c, k):
    px = 1 - x if k & 4 else x
    py = 1 - y if k & 2 else y
    pc = 1 - c if k & 1 else c
    return (px, py, pc), 4 * px + 2 * py + pc


def _exchange_tiles(src_of_peer, buf, send_sem, recv_sem):
    x, y, c, me = _me()
    for k in range(1, NDEV):
        dev, lin = _peer(x, y, c, k)
        pltpu.make_async_remote_copy(src_ref=src_of_peer(lin), dst_ref=buf.at[me], send_sem=send_sem, recv_sem=recv_sem,
                                     device_id=dev, device_id_type=MESH).start()
    seven = buf.at[pl.ds(0, NDEV - 1)]
    pltpu.make_async_remote_copy(src_ref=seven, dst_ref=seven, send_sem=send_sem, recv_sem=recv_sem,
                                 device_id=(x, y, c), device_id_type=MESH).wait()


def _silu(z):
    return z * jax.nn.sigmoid(z)


def ada_fwd(c, c_ctx, ffn_w, conv_w, w_shard, b_shard, deps, *, name):
    nsh = w_shard.shape[1]
    deps = [d for d in deps if d is not None]

    def body(c_ref, cc_ref, fw_ref, cw_ref, w_ref, b_ref, *rest):
        s_ref, m_ref, mine, res, sems = rest[len(deps):]
        x, y, c, me = _me()
        mine[0:1, :] = _silu(c_ref[...])
        mine[1:2, :] = _silu(cc_ref[...])
        mine[2:5, :] = fw_ref[...]
        mine[5:8, :] = cw_ref[...]
        s_ref[me] = mine[...]
        _exchange_tiles(lambda lin: mine, s_ref, sems.at[0], sems.at[1])
        sall = s_ref[...].reshape(NDEV * 8, D).astype(BF)
        r = jnp.dot(sall, w_ref[...].astype(BF), preferred_element_type=F32) + b_ref[...]
        res[...] = r.reshape(NDEV, 8, nsh)
        m_ref[me] = res[me]
        _exchange_tiles(lambda lin: res.at[lin], m_ref, sems.at[2], sems.at[3])

    vm = pl.BlockSpec(memory_space=pltpu.VMEM)
    return pl.pallas_call(
        body, name=name, in_specs=[vm] * 6 + [pl.BlockSpec(memory_space=pl.ANY)] * len(deps), out_specs=[vm, vm],
        out_shape=[jax.ShapeDtypeStruct((NDEV, 8, D), F32), jax.ShapeDtypeStruct((NDEV, 8, nsh), F32)],
        scratch_shapes=[pltpu.VMEM((8, D), F32), pltpu.VMEM((NDEV, 8, nsh), F32), pltpu.SemaphoreType.DMA((4,))],
    )(c, c_ctx, ffn_w, conv_w, w_shard, b_shard, *deps)


P_DML, P_DMC, P_N1, P_QG, P_KVG, P_CB, P_N2, P_FB, P_FG, P_CW, P_FW, P_LOSS, P_ROWS = 0, 6, 12, 13, 14, 15, 16, 17, 23, 24, 27, 45, 48
FROWS = 3


def sync_small(r, dep, *, name):
    ins = [r["dmod_lat"], r["dmod_ctx"], r["dn1g"], r["dqg"], r["dkvg"], r["dconv_b"], r["dn2g"], r["dffn_b"], r["dfg"],
           r["dconv_w"], r["dffn_w"], r["loss"]]

    def put_wide(p, row0, row, n):
        for j in range(-(-n // D)):
            w = min(D, n - j * D)
            p[row0 + j:row0 + j + 1, 0:w] = row[:, j * D:j * D + w]

    def body(dml, dmc, n1, qg, kvg, cb, n2, fb, fg, cw, fw, loss, dep_ref, a_ref, sum_ref, p, sems):
        x, y, c, me = _me()
        p[...] = jnp.zeros_like(p)
        put_wide(p, P_DML, dml, 6 * D)
        put_wide(p, P_DMC, dmc, 6 * D)
        put_wide(p, P_N1, n1, D)
        put_wide(p, P_QG, qg, 512)
        put_wide(p, P_KVG, kvg, KVL)
        put_wide(p, P_CB, cb, CONV)
        put_wide(p, P_N2, n2, D)
        put_wide(p, P_FG, fg, D)
        put_wide(p, P_LOSS, loss, 128)
        for s in range(2):
            put_wide(p, P_FB + FROWS * s, fb.at[s], DFF)
        for k in range(3):
            put_wide(p, P_CW + k, cw.at[k:k + 1], CONV)
            for s in range(2):
                put_wide(p, P_FW + FROWS * (2 * k + s), fw.at[s, k:k + 1], DFF)
        a_ref[me] = p[...]
        _exchange_tiles(lambda lin: p, a_ref, sems.at[0], sems.at[1])
        acc = a_ref[0]
        for k in range(1, NDEV):
            acc = acc + a_ref[k]
        sum_ref[...] = acc

    vm = pl.BlockSpec(memory_space=pltpu.VMEM)
    return pl.pallas_call(
        body, name=name, in_specs=[vm] * len(ins) + [pl.BlockSpec(memory_space=pl.ANY)], out_specs=[vm, vm],
        out_shape=[jax.ShapeDtypeStruct((NDEV, P_ROWS, D), F32), jax.ShapeDtypeStruct((P_ROWS, D), F32)],
        scratch_shapes=[pltpu.VMEM((P_ROWS, D), F32), pltpu.SemaphoreType.DMA((2,))],
    )(*ins, dep)


def ada_bwd(s_all, dml, dmc, w_shard, c_ctx, *, name):
    nsh = w_shard.shape[1]

    def body(s_ref, dml_ref, dmc_ref, w_ref, c_ref, dw_ref, gc_ref, s16, dm16, part, buf, sems):
        x, y, c, me = _me()
        s16[...] = jnp.zeros_like(s16)
        dm16[...] = jnp.zeros_like(dm16)
        for k in range(NDEV):
            s16[k:k + 1, :] = s_ref[k, 0:1, :]
        s16[8:9, :] = s_ref[0, 1:2, :]
        dm16[0:8, :] = dml_ref[...]
        dm16[8:9, :] = dmc_ref[...]
        dw_ref[...] = lax.dot_general(s16[...].astype(BF), dm16[...].astype(BF), (((0,), (0,)), ((), ())),
                                      preferred_element_type=F32)
        part[...] = lax.dot_general(dm16[8:16, :].astype(BF), w_ref[...].astype(BF), (((1,), (1,)), ((), ())),
                                    preferred_element_type=F32)
        buf[me] = part[...]
        _exchange_tiles(lambda lin: part, buf, sems.at[0], sems.at[1])
        acc = buf[0]
        for k in range(1, NDEV):
            acc = acc + buf[k]
        z = c_ref[...]
        sg = jax.nn.sigmoid(z)
        gc_ref[...] = acc * (sg * (1.0 + z * (1.0 - sg)))

    vm = pl.BlockSpec(memory_space=pltpu.VMEM)
    return pl.pallas_call(
        body, name=name, in_specs=[vm] * 5, out_specs=[vm, vm],
        out_shape=[jax.ShapeDtypeStruct((D, nsh), F32), jax.ShapeDtypeStruct((8, D), F32)],
        scratch_shapes=[pltpu.VMEM((16, D), F32), pltpu.VMEM((16, nsh), F32), pltpu.VMEM((8, D), F32),
                        pltpu.VMEM((NDEV, 8, D), F32), pltpu.SemaphoreType.DMA((2,))],
    )(s_all, dml, dmc, w_shard, c_ctx)


HBM_SPEC = pl.BlockSpec(memory_space=pltpu.HBM)
SEM_SPEC = pl.BlockSpec(memory_space=pltpu.SEMAPHORE)
EFFECT = pltpu.SideEffectType.DATAFLOW_SIDE_EFFECTING


def _exchange_copies(srcs, lands, send, recv, per_peer):
    x, y, c, me = _me()
    cps = []
    for t in range(len(srcs)):
        for k in range(1, NDEV):
            dev, lin = _peer(x, y, c, k)
            cps.append(pltpu.make_async_remote_copy(
                src_ref=srcs[t].at[lin] if per_peer else srcs[t], dst_ref=lands[t].at[me],
                send_sem=send.at[7 * t + k - 1], recv_sem=recv.at[7 * t + k - 1], device_id=dev, device_id_type=MESH))
    return cps


def exchange_start(srcs, *, per_peer, name, dep=None):
    nt = len(srcs)
    land_shapes = [(a.shape if per_peer else (NDEV,) + a.shape) for a in srcs]
    deps = [] if dep is None else [dep]

    def body(*refs):
        src, land = refs[:nt], refs[nt:2 * nt]
        send, recv = refs[2 * nt + len(deps)], refs[2 * nt + len(deps) + 1]
        for cp in _exchange_copies(src, land, send, recv, per_peer):
            cp.start()
        refs[-1][...] = jnp.zeros_like(refs[-1])

    hb = lambda a: pltpu.with_memory_space_constraint(a, pltpu.HBM)
    outs = pl.pallas_call(
        body, name=name,
        out_shape=(pltpu.SemaphoreType.DMA((7 * nt,)), pltpu.SemaphoreType.DMA((7 * nt,)),
                   *[pltpu.HBM(a.shape, a.dtype) for a in srcs], *[pltpu.HBM(s, a.dtype) for s, a in zip(land_shapes, srcs)],
                   jax.ShapeDtypeStruct((8, 128), F32)),
        in_specs=[HBM_SPEC] * (2 * nt) + [pl.BlockSpec(memory_space=pl.ANY)] * len(deps),
        out_specs=(SEM_SPEC, SEM_SPEC, *([HBM_SPEC] * (2 * nt)), pl.BlockSpec(memory_space=pltpu.VMEM)),
        input_output_aliases={i: 2 + i for i in range(2 * nt)},
        compiler_params=pltpu.CompilerParams(has_side_effects=EFFECT),
    )(*[hb(a) for a in srcs], *[hb(lax.empty(s, a.dtype)) for s, a in zip(land_shapes, srcs)], *deps)
    return dict(send=outs[0], recv=outs[1], src=list(outs[2:2 + nt]), land=list(outs[2 + nt:2 + 2 * nt]), token=outs[-1],
                per_peer=per_peer)


def exchange_wait(h, after, *, name):
    nt = len(h["src"])
    per_peer = h["per_peer"]

    def body(*refs):
        src, land, send, recv = refs[:nt], refs[nt:2 * nt], refs[2 * nt], refs[2 * nt + 1]
        for cp in _exchange_copies(src, land, send, recv, per_peer):
            cp.wait_send()
            cp.wait_recv()

    outs = pl.pallas_call(
        body, name=name,
        out_shape=(*[pltpu.HBM(a.shape, a.dtype) for a in h["src"]], *[pltpu.HBM(a.shape, a.dtype) for a in h["land"]]),
        in_specs=[HBM_SPEC] * (2 * nt) + [SEM_SPEC, SEM_SPEC, pl.BlockSpec(memory_space=pl.ANY)],
        out_specs=tuple([HBM_SPEC] * (2 * nt)),
        input_output_aliases={i: i for i in range(2 * nt)},
        compiler_params=pltpu.CompilerParams(has_side_effects=EFFECT),
    )(*h["src"], *h["land"], h["send"], h["recv"], after)
    return list(outs[:nt]), list(outs[nt:])


def _gathered(h, after, me, *, name):
    srcs, lands = exchange_wait(h, after, name=name)
    return [lax.dynamic_update_slice(l, s[None], (me,) + (0,) * s.ndim) for s, l in zip(srcs, lands)]


def _scattered(h, after, me, *, name):
    srcs, lands = exchange_wait(h, after, name=name)
    return [lax.dynamic_update_slice(l, lax.dynamic_slice_in_dim(s, me, 1, 0), (me,) + (0,) * (s.ndim - 1))
            for s, l in zip(srcs, lands)]


def _adamw_math(w, g, m, v):
    nm = B1 * m + (1.0 - B1) * g
    nv = B2 * v + (1.0 - B2) * (g * g)
    m_hat = nm / (1.0 - B1 ** STEP)
    v_hat = nv / (1.0 - B2 ** STEP)
    return -LR * (m_hat / (jnp.sqrt(v_hat) + AEPS) + WD * w), nm, nv


def adamw_many(ws, gs, ms, vs, *, name):
    n = len(ws)

    def body(*refs):
        for k in range(n):
            d, nm, nv = _adamw_math(refs[k][...], refs[n + k][...], refs[2 * n + k][...], refs[3 * n + k][...])
            refs[4 * n + k][...] = d
            refs[5 * n + k][...] = nm
            refs[6 * n + k][...] = nv

    vm = pl.BlockSpec(memory_space=pltpu.VMEM)
    sh = [jax.ShapeDtypeStruct(w.shape, F32) for w in ws]
    outs = pl.pallas_call(body, name=name, in_specs=[vm] * (4 * n), out_specs=[vm] * (3 * n), out_shape=sh * 3,
                          )(*ws, *gs, *ms, *vs)
    return outs[:n], outs[n:2 * n], outs[2 * n:]


def adamw(w, g, m, v, *, name, tr=256):
    R, C = w.shape
    tr = _pick(R, tr, 8)

    def body(w_ref, g_ref, m_ref, v_ref, d_ref, nm_ref, nv_ref):
        d_ref[...], nm_ref[...], nv_ref[...] = _adamw_math(w_ref[...], g_ref[...], m_ref[...], v_ref[...])

    blk = pl.BlockSpec((tr, C), lambda i: (i, 0))
    sh = jax.ShapeDtypeStruct((R, C), F32)
    return pl.pallas_call(
        body, name=name, grid=(R // tr,), in_specs=[blk, blk, blk, blk], out_specs=[blk, blk, blk],
        out_shape=[sh, sh, sh], compiler_params=pltpu.CompilerParams(dimension_semantics=("parallel",)),
    )(w, g, m, v)


def adamw_slots(w, slots, m, v, *, name, tr=256):
    R, C = w.shape
    tr = _pick(R, tr, 16)

    def body(w_ref, s_ref, m_ref, v_ref, g_ref, d_ref, nm_ref, nv_ref):
        g = s_ref[0].astype(F32)
        for k in range(1, NDEV):
            g = g + s_ref[k].astype(F32)
        g_ref[...] = g
        d_ref[...], nm_ref[...], nv_ref[...] = _adamw_math(w_ref[...], g, m_ref[...], v_ref[...])

    blk = pl.BlockSpec((tr, C), lambda i: (i, 0))
    sh = jax.ShapeDtypeStruct((R, C), F32)
    return pl.pallas_call(
        body, name=name, grid=(R // tr,), in_specs=[blk, pl.BlockSpec((NDEV, tr, C), lambda i: (0, i, 0)), blk, blk],
        out_specs=[blk, blk, blk, blk], out_shape=[sh, sh, sh, sh],
        compiler_params=pltpu.CompilerParams(dimension_semantics=("parallel",)),
    )(w, slots, m, v)


def _padc(a, n=D):
    return jnp.pad(a, ((0, 0), (0, n - a.shape[1])))


def kernel(x, c, ctx, c_ctx, w_ada, b_ada, norm1_g, w_in, q_norm_g, kv_norm_g, w_uq, w_ukv, conv_w, conv_b, w_attn_out, w_conv_out, w_o, norm2_g, w_up, ffn_conv_w, ffn_conv_b, w_down, final_g, loss_target, m_c_ctx, m_w_ada, m_b_ada, m_norm1_g, m_w_in, m_q_norm_g, m_kv_norm_g, m_w_uq, m_w_ukv, m_conv_w, m_conv_b, m_w_attn_out, m_w_conv_out, m_w_o, m_norm2_g, m_w_up, m_ffn_conv_w, m_ffn_conv_b, m_w_down, m_final_g, v_c_ctx, v_w_ada, v_b_ada, v_norm1_g, v_w_in, v_q_norm_g, v_kv_norm_g, v_w_uq, v_w_ukv, v_conv_w, v_conv_b, v_w_attn_out, v_w_conv_out, v_w_o, v_norm2_g, v_w_up, v_ffn_conv_w, v_ffn_conv_b, v_w_down, v_final_g):
    me = 4 * lax.axis_index("x") + 2 * lax.axis_index("y") + lax.axis_index("c")
    W = dict(c_ctx=c_ctx, w_ada=w_ada, b_ada=b_ada, norm1_g=norm1_g, w_in=w_in, q_norm_g=q_norm_g, kv_norm_g=kv_norm_g,
             w_uq=w_uq, w_ukv=w_ukv, conv_w=conv_w, conv_b=conv_b, w_attn_out=w_attn_out, w_conv_out=w_conv_out, w_o=w_o,
             norm2_g=norm2_g, w_up=w_up, ffn_conv_w=ffn_conv_w, ffn_conv_b=ffn_conv_b, w_down=w_down, final_g=final_g)
    M = dict(c_ctx=m_c_ctx, w_ada=m_w_ada, b_ada=m_b_ada, norm1_g=m_norm1_g, w_in=m_w_in, q_norm_g=m_q_norm_g,
             kv_norm_g=m_kv_norm_g, w_uq=m_w_uq, w_ukv=m_w_ukv, conv_w=m_conv_w, conv_b=m_conv_b, w_attn_out=m_w_attn_out,
             w_conv_out=m_w_conv_out, w_o=m_w_o, norm2_g=m_norm2_g, w_up=m_w_up, ffn_conv_w=m_ffn_conv_w,
             ffn_conv_b=m_ffn_conv_b, w_down=m_w_down, final_g=m_final_g)
    V = dict(c_ctx=v_c_ctx, w_ada=v_w_ada, b_ada=v_b_ada, norm1_g=v_norm1_g, w_in=v_w_in, q_norm_g=v_q_norm_g,
             kv_norm_g=v_kv_norm_g, w_uq=v_w_uq, w_ukv=v_w_ukv, conv_w=v_conv_w, conv_b=v_conv_b, w_attn_out=v_w_attn_out,
             w_conv_out=v_w_conv_out, w_o=v_w_o, norm2_g=v_norm2_g, w_up=v_w_up, ffn_conv_w=v_ffn_conv_w,
             ffn_conv_b=v_ffn_conv_b, w_down=v_w_down, final_g=v_final_g)
    names = list(W)
    as2d = lambda a: a.reshape(1, -1) if a.ndim == 1 else a.reshape(a.shape[-2], a.shape[-1])
    W2 = {k: as2d(a) for k, a in W.items()}
    M2 = {k: as2d(a) for k, a in M.items()}
    V2 = {k: as2d(a) for k, a in V.items()}
    nsh = W2["w_ada"].shape[1]

    b_sh = lax.dynamic_slice(W2["b_ada"], (0, me * nsh), (1, nsh))
    s_all, m_all = ada_fwd(c, W2["c_ctx"], _padc(W2["ffn_conv_w"]), _padc(W2["conv_w"]), W2["w_ada"], b_sh, [],
                           name="ada_fwd")
    mod_lat = m_all[:, 0, :].reshape(1, 6 * D)
    mod_ctx = m_all[:, 1, :].reshape(1, 6 * D)
    ffn_w_full = s_all[:, 2:5, :2 * DFF // NDEV].transpose(1, 0, 2).reshape(3, 2 * DFF)
    conv_w_full = s_all[:, 5:8, :CONV // NDEV].transpose(1, 0, 2).reshape(3, CONV)

    stage_w = {"in": ["w_in"], "mid": ["w_uq", "w_ukv", "w_attn_out", "w_conv_out", "w_o"], "ffn": ["w_up", "w_down"]}
    ag, tok = {}, m_all
    for st, nms in stage_w.items():
        ag[st] = exchange_start([W2[nm].astype(BF) for nm in nms], per_peer=False, dep=tok, name="ag_start_" + st)
        tok = ag[st]["token"]

    def get_w(stage, after):
        g = dict(zip(stage_w[stage], _gathered(ag[stage], after, me, name="ag_wait_" + stage)))
        if stage == "in":
            return build_win(g["w_in"], name="build_win")
        if stage == "mid":
            wq2, wkv2 = build_wq_wkv(g["w_uq"], g["w_ukv"], name="build_wq_wkv")
            return (wq2, wkv2, unshard_cols(g["w_attn_out"], name="unshard_w_attn_out"),
                    unshard_cols(g["w_conv_out"], name="unshard_w_conv_out"), g["w_o"].reshape(D, D))
        return unshard_cols(g["w_up"], name="unshard_w_up"), g["w_down"].reshape(DFF, D)

    stage_g = {"ffn": ["w_up", "w_down"], "mid": ["w_attn_out", "w_conv_out", "w_o"], "qkv": ["w_uq", "w_ukv"],
               "in": ["w_in"]}
    rs = {}

    def put_g(stage, g):
        if stage == "in":
            parts = [shard_win_grad(g["dwin"], g["dwin_c"], name="shard_win_grad")]
        elif stage == "mid":
            parts = [shard_cols(g["dwao"], name="shard_w_attn_out"), shard_cols(g["dwco"], name="shard_w_conv_out"),
                     g["dwo"].reshape(NDEV, D // NDEV, D)]
        elif stage == "qkv":
            parts = list(shard_wq_wkv_grad(g["dwq2"], g["dwkv2"], name="shard_wq_wkv_grad"))
        else:
            parts = [shard_cols(g["dwup"], name="shard_w_up"), g["dwdn"].reshape(NDEV, DFF // NDEV, D)]
        rs[stage] = exchange_start(parts, per_peer=True, name="rs_start_" + stage)
        return rs[stage]["token"]

    r = _local_step(x[0], ctx[0], loss_target[0], mod_lat, mod_ctx, W2["norm1_g"], W2["q_norm_g"], W2["kv_norm_g"],
                    W2["norm2_g"], W2["final_g"], conv_w_full, W2["conv_b"], ffn_w_full, W2["ffn_conv_b"], get_w, put_g,
                    ag["ffn"]["token"])

    G, DL, NM, NV = {}, {}, {}, {}

    def finish(stage, after):
        for nm, sl in zip(stage_g[stage], _scattered(rs[stage], after, me, name="rs_wait_" + stage)):
            G[nm], DL[nm], NM[nm], NV[nm] = adamw_slots(W2[nm], sl, M2[nm], V2[nm], name="adamw_" + nm)
            after = DL[nm]
        return after

    after = r["dx"]
    for st in ("ffn", "mid", "qkv"):
        after = finish(st, after)

    a_buf, ssum = sync_small(r, after, name="sync_small")
    loss = ssum[P_LOSS, 0]
    G["norm1_g"] = ssum[P_N1:P_N1 + 1]
    G["q_norm_g"] = ssum[P_QG:P_QG + 1, :QL]
    G["kv_norm_g"] = ssum[P_KVG:P_KVG + 1, :KVL]
    G["conv_b"] = ssum[P_CB:P_CB + 1, :CONV]
    G["norm2_g"] = ssum[P_N2:P_N2 + 1]
    G["ffn_conv_b"] = ssum[P_FB:P_FB + 2 * FROWS].reshape(1, 2, FROWS * D)[:, :, :DFF].reshape(1, 2 * DFF)
    G["final_g"] = ssum[P_FG:P_FG + 1]
    G["conv_w"] = lax.dynamic_slice(ssum[P_CW:P_CW + 3, :CONV], (0, me * (CONV // NDEV)), (3, CONV // NDEV))
    fw_full = ssum[P_FW:P_FW + 6 * FROWS].reshape(3, 2, FROWS * D)[:, :, :DFF].reshape(3, 2 * DFF)
    G["ffn_conv_w"] = lax.dynamic_slice(fw_full, (0, me * (2 * DFF // NDEV)), (3, 2 * DFF // NDEV))
    G["b_ada"] = (ssum[P_DML:P_DML + 6] + ssum[P_DMC:P_DMC + 6]).reshape(1, 6 * D)

    dml = lax.dynamic_slice(a_buf[:, P_DML:P_DML + 6, :].reshape(NDEV, 6 * D), (0, me * nsh), (NDEV, nsh))
    dmc = lax.dynamic_slice(ssum[P_DMC:P_DMC + 6].reshape(1, 6 * D), (0, me * nsh), (1, nsh))
    G["w_ada"], gcc = ada_bwd(s_all, dml, dmc, W2["w_ada"], W2["c_ctx"], name="ada_bwd")
    G["c_ctx"] = gcc[0:1]

    DL["w_ada"], NM["w_ada"], NV["w_ada"] = adamw(W2["w_ada"], G["w_ada"], M2["w_ada"], V2["w_ada"], name="adamw_w_ada")
    small = ["c_ctx", "b_ada", "norm1_g", "q_norm_g", "kv_norm_g", "conv_b", "norm2_g", "ffn_conv_b", "final_g", "conv_w",
             "ffn_conv_w"]
    ds, nms, nvs = adamw_many([W2[k] for k in small], [G[k] for k in small], [M2[k] for k in small],
                              [V2[k] for k in small], name="adamw_small")
    for k, nm in enumerate(small):
        DL[nm], NM[nm], NV[nm] = ds[k], nms[k], nvs[k]
    finish("in", ds[0])

    outs = [loss, r["dx"][None]]
    for grp in (G, DL, NM, NV):
        outs += [grp[nm].reshape(W[nm].shape) for nm in names]
    return tuple(outs)
```

```python
import functools
import numpy as np
import jax
import jax.numpy as jnp
from jax import lax
from jax.experimental import pallas as pl
from jax.experimental.pallas import tpu as pltpu

F32 = jnp.float32
BF = jnp.bfloat16
MESH = pl.DeviceIdType.MESH

D = 1024
T = 2048
TC = 256
TKV = T + TC
GRID_W = 64
NH = 8
DN = 64
DR = 32
DV = 64
QL = 384
KVL = 256
CONV = 512
DFF = 2816
EPS = 1e-6
ROPE_THETA = 10000.0
SCALE = (DN + DR) ** -0.5
NDEV = 8
HP = 128

O_GA, O_GC, O_KV, O_Q, O_CV = 0, 1024, 2048, 2560, 3072
NIN = 4608
CVB = 256
N_IN = 4256
SH_IN = N_IN // NDEV

LR, B1, B2, AEPS, WD, STEP = 0.001, 0.9, 0.999, 1e-08, 0.01, 10


def _pick(n, target, mult=128):
    best = None
    for d in range(mult, min(n, target) + 1, mult):
        if n % d == 0:
            best = d
    return best if best is not None else n


def _swap_start(g):
    return 8 * (g ^ 1)


def mm(a, b, *, ta=False, tb=False, out_dtype=F32, name, tm=1024, tn=1024, tk=2048, M=None, N=None, K=None,
       a_off=(0, 0), b_off=(0, 0), a_stack=False, b_stack=False, o_stack=False, dep=None):
    def dims(arr, stack):
        return (arr.shape[1], 2 * arr.shape[2]) if stack else arr.shape

    ar, ac = dims(a, a_stack)
    br, bc = dims(b, b_stack)
    M = M or ((ac if ta else ar) - a_off[1 if ta else 0])
    K = K or ((ar if ta else ac) - a_off[0 if ta else 1])
    N = N or ((br if tb else bc) - b_off[0 if tb else 1])
    tm = _pick(M, tm, 128 if ta else 16)
    tn = _pick(N // 2 if (o_stack or (b_stack and not tb)) else N, tn, 128)
    tk = _pick(K // 2 if ((a_stack and not ta) or (b_stack and tb)) else K, tk, 128)
    nk = K // tk
    ca = 0 if ta else 1
    cb = 1 if tb else 0

    def body(a_ref, b_ref, *rest):
        o_ref, acc = rest[-2:]
        k = pl.program_id(2)
        part = lax.dot_general(a_ref[...].astype(BF), b_ref[...].astype(BF),
                               (((ca,), (cb,)), ((), ())), preferred_element_type=F32)
        if nk == 1:
            o_ref[...] = part.astype(o_ref.dtype)
        else:
            @pl.when(k == 0)
            def _():
                acc[...] = part

            @pl.when(k > 0)
            def _():
                acc[...] += part

            @pl.when(k == nk - 1)
            def _():
                o_ref[...] = acc[...].astype(o_ref.dtype)

    def spec(blk, rc, off, stack, ncols):
        assert off[0] % blk[0] == 0 and off[1] % blk[1] == 0, (name, blk, off)
        ro, co = off[0] // blk[0], off[1] // blk[1]
        if not stack:
            return pl.BlockSpec(blk, lambda i, j, k: (rc(i, j, k)[0] + ro, rc(i, j, k)[1] + co))
        nhb = ncols // 2 // blk[1]
        return pl.BlockSpec((None,) + blk,
                            lambda i, j, k: ((rc(i, j, k)[1] + co) // nhb, rc(i, j, k)[0] + ro, (rc(i, j, k)[1] + co) % nhb))

    a_spec = spec((tk, tm), lambda i, j, k: (k, i), a_off, a_stack, ac) if ta else \
        spec((tm, tk), lambda i, j, k: (i, k), a_off, a_stack, ac)
    b_spec = spec((tn, tk), lambda i, j, k: (j, k), b_off, b_stack, bc) if tb else \
        spec((tk, tn), lambda i, j, k: (k, j), b_off, b_stack, bc)
    o_spec = spec((tm, tn), lambda i, j, k: (i, j), (0, 0), o_stack, N)
    o_shape = (2, M, N // 2) if o_stack else (M, N)
    deps = [] if dep is None else [dep]
    return pl.pallas_call(
        body, name=name, grid=(M // tm, N // tn, nk),
        in_specs=[a_spec, b_spec] + [pl.BlockSpec(memory_space=pl.ANY)] * len(deps),
        out_specs=o_spec, out_shape=jax.ShapeDtypeStruct(o_shape, out_dtype),
        scratch_shapes=[pltpu.VMEM((tm, tn) if nk > 1 else (8, 128), F32)],
        compiler_params=pltpu.CompilerParams(dimension_semantics=("parallel", "parallel", "arbitrary")),
    )(a, b, *deps)


def _row(width):
    return pl.BlockSpec((1, width), lambda *_: (0, 0))


NLAT = T // TC


def normmod_cat(ctx, x, g, csc, csh, sc, sh, dep, *, name, tm=256):
    assert tm == TC

    def body(c_ref, x_ref, g_ref, csc_ref, csh_ref, sc_ref, sh_ref, dep_ref, h_ref):
        last = pl.program_id(0) == NLAT
        xv = jnp.where(last, c_ref[...], x_ref[...])
        scv = jnp.where(last, csc_ref[...], sc_ref[...])
        shv = jnp.where(last, csh_ref[...], sh_ref[...])
        r = lax.rsqrt(jnp.mean(xv * xv, axis=-1, keepdims=True) + EPS)
        h_ref[...] = ((xv * r * g_ref[...]) * (1.0 + scv) + shv).astype(BF)

    return pl.pallas_call(
        body, name=name, grid=(TKV // tm,),
        in_specs=[pl.BlockSpec((tm, D), lambda i: (0, 0)), pl.BlockSpec((tm, D), lambda i: (jnp.minimum(i, NLAT - 1), 0)),
                  _row(D), _row(D), _row(D), _row(D), _row(D), pl.BlockSpec(memory_space=pl.ANY)],
        out_specs=pl.BlockSpec((tm, D), lambda i: (i, 0)), out_shape=jax.ShapeDtypeStruct((TKV, D), BF),
        compiler_params=pltpu.CompilerParams(dimension_semantics=("parallel",)),
    )(ctx, x, g, csc, csh, sc, sh, dep)


def resid_normmod(x, a, gate, g, sc, sh, *, name, tm=256):
    R = x.shape[0]

    def body(x_ref, a_ref, gate_ref, g_ref, sc_ref, sh_ref, x1_ref, h_ref):
        xv = x_ref[...] + gate_ref[...] * a_ref[...]
        x1_ref[...] = xv
        r = lax.rsqrt(jnp.mean(xv * xv, axis=-1, keepdims=True) + EPS)
        h_ref[...] = ((xv * r * g_ref[...]) * (1.0 + sc_ref[...]) + sh_ref[...]).astype(BF)

    blk = pl.BlockSpec((tm, D), lambda i: (i, 0))
    return pl.pallas_call(
        body, name=name, grid=(R // tm,), in_specs=[blk, blk, _row(D), _row(D), _row(D), _row(D)],
        out_specs=[blk, blk],
        out_shape=[jax.ShapeDtypeStruct((R, D), F32), jax.ShapeDtypeStruct((R, D), BF)],
        compiler_params=pltpu.CompilerParams(dimension_semantics=("parallel",)),
    )(x, a, gate, g, sc, sh)


def kvprep(pc, p, kvg, wkv2, ck, sk, *, name, tm=256):
    assert tm == TC
    nb = TKV // tm
    kvcol = O_KV // 512

    def body(pc_ref, p_ref, g_ref, w_ref, ck_ref, sk_ref, k_ref, v_ref, ckv_ref):
        i = pl.program_id(0)
        t = jnp.where(i == NLAT, pc_ref[...], p_ref[...])
        pk = t[:, :KVL]
        r = lax.rsqrt(jnp.mean(pk * pk, axis=-1, keepdims=True) + EPS)
        ckv = (pk * r * g_ref[...]).astype(BF)
        ckv_ref[...] = ckv
        kv2 = jnp.dot(ckv, w_ref[...], preferred_element_type=F32)
        krr = t[:, KVL:KVL + HP] * ck_ref[...] + t[:, KVL + HP:KVL + 2 * HP] * sk_ref[...]
        k_ref[...] = (kv2[:, :NH * HP] + jnp.concatenate([krr] * NH, axis=1)).astype(BF)
        v_ref[...] = kv2[:, NH * HP:].astype(BF)

    return pl.pallas_call(
        body, name=name, grid=(nb,),
        in_specs=[pl.BlockSpec((tm, 512), lambda i: (0, 0)),
                  pl.BlockSpec((tm, 512), lambda i: (jnp.minimum(i, NLAT - 1), kvcol)),
                  _row(KVL), pl.BlockSpec((KVL, NH * HP + NH * DV), lambda i: (0, 0)),
                  pl.BlockSpec((tm, HP), lambda i: (i, 0)), pl.BlockSpec((tm, HP), lambda i: (i, 0))],
        out_specs=[pl.BlockSpec((tm, NH * HP), lambda i: (i, 0)), pl.BlockSpec((tm, NH * DV), lambda i: (i, 0)),
                   pl.BlockSpec((tm, KVL), lambda i: (i, 0))],
        out_shape=[jax.ShapeDtypeStruct((TKV, NH * HP), BF), jax.ShapeDtypeStruct((TKV, NH * DV), BF),
                   jax.ShapeDtypeStruct((TKV, KVL), BF)],
        compiler_params=pltpu.CompilerParams(dimension_semantics=("parallel",)),
    )(pc, p, kvg, wkv2, ck, sk)


def qprep(p, qg, wq2, cq_t, sq_t, *, name, tm=256):
    qcol = O_Q // 512

    def body(p_ref, g_ref, w_ref, c_ref, s_ref, q_ref, cq_ref):
        pq = p_ref[...]
        r = lax.rsqrt(jnp.sum(pq * pq, axis=-1, keepdims=True) * (1.0 / QL) + EPS)
        cq = (pq * r * g_ref[...]).astype(BF)
        cq_ref[...] = cq
        q2 = jnp.dot(cq, w_ref[...], preferred_element_type=F32)
        cc = jnp.concatenate([c_ref[...]] * NH, axis=1)
        ss = jnp.concatenate([s_ref[...]] * NH, axis=1)
        q_ref[...] = (q2[:, :NH * HP] * cc + q2[:, NH * HP:] * ss).astype(BF)

    return pl.pallas_call(
        body, name=name, grid=(T // tm,),
        in_specs=[pl.BlockSpec((tm, 512), lambda i: (i, qcol)), _row(512),
                  pl.BlockSpec((512, 2 * NH * HP), lambda i: (0, 0)),
                  pl.BlockSpec((tm, HP), lambda i: (i, 0)), pl.BlockSpec((tm, HP), lambda i: (i, 0))],
        out_specs=[pl.BlockSpec((tm, NH * HP), lambda i: (i, 0)), pl.BlockSpec((tm, 512), lambda i: (i, 0))],
        out_shape=[jax.ShapeDtypeStruct((T, NH * HP), BF), jax.ShapeDtypeStruct((T, 512), BF)],
        compiler_params=pltpu.CompilerParams(dimension_semantics=("parallel",)),
    )(p, qg, wq2, cq_t, sq_t)


def _head_mask(h):
    lanes = lax.broadcasted_iota(jnp.int32, (1, 2 * DV), 1)
    return (lanes // DV) == (h % 2)


def attn_fwd(q, k, v, *, name, tq=256):
    def body(q_ref, k_ref, v_ref, o_ref):
        h = pl.program_id(1)
        s = lax.dot_general(q_ref[...], k_ref[...], (((1,), (1,)), ((), ())), preferred_element_type=F32) * SCALE
        m = jnp.max(s, axis=-1, keepdims=True)
        e = jnp.exp(s - m)
        pr = (e * (1.0 / jnp.sum(e, axis=-1, keepdims=True))).astype(BF)
        vm = jnp.where(_head_mask(h), v_ref[...], jnp.zeros_like(v_ref[...]))
        o2 = jnp.dot(pr, vm, preferred_element_type=F32).astype(BF)

        @pl.when(h % 2 == 0)
        def _():
            o_ref[...] = o2

        @pl.when(h % 2 == 1)
        def _():
            o_ref[...] = o_ref[...] + o2

    return pl.pallas_call(
        body, name=name, grid=(T // tq, NH),
        in_specs=[pl.BlockSpec((tq, HP), lambda i, h: (i, h)), pl.BlockSpec((TKV, HP), lambda i, h: (0, h)),
                  pl.BlockSpec((TKV, 2 * DV), lambda i, h: (0, h // 2))],
        out_specs=pl.BlockSpec((tq, 2 * DV), lambda i, h: (i, h // 2)),
        out_shape=jax.ShapeDtypeStruct((T, NH * DV), BF),
        compiler_params=pltpu.CompilerParams(dimension_semantics=("parallel", "arbitrary")),
    )(q, k, v)


def _shift_dn(x):
    n = x.shape[0]
    rows = lax.broadcasted_iota(jnp.int32, (n, 1), 0)
    return jnp.where(rows == 0, 0.0, pltpu.roll(x, 1, axis=0))


def _shift_up(x):
    n = x.shape[0]
    rows = lax.broadcasted_iota(jnp.int32, (n, 1), 0)
    return jnp.where(rows == n - 1, 0.0, pltpu.roll(x, n - 1, axis=0))


def _conv(x, w_ref, b_ref):
    return b_ref[...] + _shift_dn(x) * w_ref[0:1, :] + x * w_ref[1:2, :] + _shift_up(x) * w_ref[2:3, :]


def _conv_t(dy, w_ref):
    return _shift_up(dy) * w_ref[0:1, :] + dy * w_ref[1:2, :] + _shift_dn(dy) * w_ref[2:3, :]


def _conv_wgrad(dw_ref, dy, x):
    dw_ref[0:1, :] = jnp.sum(dy * _shift_dn(x), axis=0, keepdims=True)
    dw_ref[1:2, :] = jnp.sum(dy * x, axis=0, keepdims=True)
    dw_ref[2:3, :] = jnp.sum(dy * _shift_up(x), axis=0, keepdims=True)


def convz(p, cw, cb, *, name):
    o0 = O_CV // (3 * CVB)

    def body(p_ref, w_ref, bias_ref, z_ref):
        xv, bv, cv = p_ref[:, 0:CVB], p_ref[:, CVB:2 * CVB], p_ref[:, 2 * CVB:3 * CVB]
        z_ref[...] = (bv * _conv(cv * xv, w_ref, bias_ref)).astype(BF)

    return pl.pallas_call(
        body, name=name, grid=(CONV // CVB,),
        in_specs=[pl.BlockSpec((T, 3 * CVB), lambda j: (0, o0 + j)), pl.BlockSpec((3, CVB), lambda j: (0, j)),
                  pl.BlockSpec((1, CVB), lambda j: (0, j))],
        out_specs=pl.BlockSpec((T, CVB), lambda j: (0, j)),
        out_shape=jax.ShapeDtypeStruct((T, CONV), BF),
        compiler_params=pltpu.CompilerParams(dimension_semantics=("parallel",)),
    )(p, cw, cb)


def gate_merge(p, ya, yc, *, name, tm=256):
    def body(ga_ref, gc_ref, ya_ref, yc_ref, o_ref):
        o_ref[...] = (jax.nn.sigmoid(ga_ref[...]) * ya_ref[...] + jax.nn.sigmoid(gc_ref[...]) * yc_ref[...]).astype(BF)

    blk = pl.BlockSpec((tm, D), lambda i: (i, 0))
    return pl.pallas_call(
        body, name=name, grid=(T // tm,),
        in_specs=[pl.BlockSpec((tm, D), lambda i: (i, O_GA // D)), pl.BlockSpec((tm, D), lambda i: (i, O_GC // D)), blk, blk],
        out_specs=blk, out_shape=jax.ShapeDtypeStruct((T, D), BF),
        compiler_params=pltpu.CompilerParams(dimension_semantics=("parallel",)),
    )(p, p, ya, yc)


def ffn_act(u0, cw, cb, *, name, tc=256):
    nb = DFF // tc

    def body(u_ref, wg_ref, wv_ref, bg_ref, bv_ref, f_ref):
        ug = _conv(u_ref[0], wg_ref, bg_ref)
        uv = _conv(u_ref[1], wv_ref, bv_ref)
        f_ref[...] = (ug * jax.nn.sigmoid(ug) * uv).astype(BF)

    return pl.pallas_call(
        body, name=name, grid=(nb,),
        in_specs=[pl.BlockSpec((2, T, tc), lambda j: (0, 0, j)),
                  pl.BlockSpec((3, tc), lambda j: (0, j)), pl.BlockSpec((3, tc), lambda j: (0, nb + j)),
                  pl.BlockSpec((1, tc), lambda j: (0, j)), pl.BlockSpec((1, tc), lambda j: (0, nb + j))],
        out_specs=pl.BlockSpec((T, tc), lambda j: (0, j)),
        out_shape=jax.ShapeDtypeStruct((T, DFF), BF),
        compiler_params=pltpu.CompilerParams(dimension_semantics=("parallel",)),
    )(u0, cw, cw, cb, cb)


def final_loss(x1, d, g2, fg, tgt, *, name, tm=256):
    def body(x1_ref, d_ref, g2_ref, fg_ref, t_ref, dx_ref, dd_ref, dfg_ref, loss_ref):
        i = pl.program_id(0)
        xv = x1_ref[...] + g2_ref[...] * d_ref[...]
        r = lax.rsqrt(jnp.mean(xv * xv, axis=-1, keepdims=True) + EPS)
        xh = xv * r
        diff = xh * fg_ref[...] - t_ref[...]
        part = 0.5 * jnp.sum(jnp.mean(diff * diff, axis=-1, keepdims=True), axis=0, keepdims=True)
        dy = diff * (1.0 / D)
        a = dy * fg_ref[...]
        dx = r * (a - xh * jnp.mean(a * xh, axis=-1, keepdims=True))
        dx_ref[...] = dx
        dd_ref[...] = (dx * g2_ref[...]).astype(BF)
        dfg = jnp.sum(dy * xh, axis=0, keepdims=True)

        @pl.when(i == 0)
        def _():
            dfg_ref[...] = dfg
            loss_ref[...] = jnp.broadcast_to(part, (1, 128))

        @pl.when(i > 0)
        def _():
            dfg_ref[...] += dfg
            loss_ref[...] += jnp.broadcast_to(part, (1, 128))

    blk = pl.BlockSpec((tm, D), lambda i: (i, 0))
    return pl.pallas_call(
        body, name=name, grid=(T // tm,), in_specs=[blk, blk, _row(D), _row(D), blk],
        out_specs=[blk, blk, _row(D), _row(128)],
        out_shape=[jax.ShapeDtypeStruct((T, D), F32), jax.ShapeDtypeStruct((T, D), BF),
                   jax.ShapeDtypeStruct((1, D), F32), jax.ShapeDtypeStruct((1, 128), F32)],
        compiler_params=pltpu.CompilerParams(dimension_semantics=("arbitrary",)),
    )(x1, d, g2, fg, tgt)


def normmod_bwd(x, dh, g, sc, dres, gsrc, gate, *, name, tm=256):
    R = x.shape[0]
    has_res = dres is not None

    def body(*refs):
        if has_res:
            x_ref, dh_ref, g_ref, sc_ref, dres_ref, gsrc_ref, gate_ref, dx_ref, dxg_ref, st_ref = refs
        else:
            x_ref, dh_ref, g_ref, sc_ref, st_ref = refs
        i = pl.program_id(0)
        xv = x_ref[...]
        r = lax.rsqrt(jnp.mean(xv * xv, axis=-1, keepdims=True) + EPS)
        xh = xv * r
        dhv = dh_ref[...]
        n = xh * g_ref[...]
        dn = dhv * (1.0 + sc_ref[...])
        a = dn * g_ref[...]
        rows = [jnp.sum(dhv, axis=0, keepdims=True), jnp.sum(dhv * n, axis=0, keepdims=True),
                jnp.sum(dn * xh, axis=0, keepdims=True)]
        if has_res:
            dr = dres_ref[...]
            dx = dr + r * (a - xh * jnp.mean(a * xh, axis=-1, keepdims=True))
            dx_ref[...] = dx
            dxg_ref[...] = (dx * gate_ref[...]).astype(BF)
            rows.append(jnp.sum(dr * gsrc_ref[...], axis=0, keepdims=True))
        else:
            rows.append(jnp.zeros((1, D), F32))

        @pl.when(i == 0)
        def _():
            for k, row in enumerate(rows):
                st_ref[k:k + 1, :] = row

        @pl.when(i > 0)
        def _():
            for k, row in enumerate(rows):
                st_ref[k:k + 1, :] += row

    blk = pl.BlockSpec((tm, D), lambda i: (i, 0))
    st_spec = pl.BlockSpec((4, D), lambda i: (0, 0))
    st_shape = jax.ShapeDtypeStruct((4, D), F32)
    cp = pltpu.CompilerParams(dimension_semantics=("arbitrary",))
    if has_res:
        return pl.pallas_call(
            body, name=name, grid=(R // tm,), in_specs=[blk, blk, _row(D), _row(D), blk, blk, _row(D)],
            out_specs=[blk, blk, st_spec],
            out_shape=[jax.ShapeDtypeStruct((R, D), F32), jax.ShapeDtypeStruct((R, D), BF), st_shape],
            compiler_params=cp,
        )(x, dh, g, sc, dres, gsrc, gate)
    return pl.pallas_call(
        body, name=name, grid=(R // tm,), in_specs=[blk, blk, _row(D), _row(D)],
        out_specs=st_spec, out_shape=st_shape, compiler_params=cp,
    )(x, dh, g, sc)


def ffn_act_bwd(u0, df, cw, cb, *, name, tc=256):
    nb = DFF // tc

    def body(u_ref, df_ref, wg_ref, wv_ref, bg_ref, bv_ref, du_ref, dw_ref, db_ref):
        xg, xv = u_ref[0], u_ref[1]
        ug = _conv(xg, wg_ref, bg_ref)
        uv = _conv(xv, wv_ref, bv_ref)
        sig = jax.nn.sigmoid(ug)
        dfv = df_ref[...]
        dug = dfv * uv * (sig * (1.0 + ug * (1.0 - sig)))
        duv = dfv * (ug * sig)
        du_ref[0] = _conv_t(dug, wg_ref).astype(BF)
        du_ref[1] = _conv_t(duv, wv_ref).astype(BF)
        _conv_wgrad(dw_ref.at[0], dug, xg)
        _conv_wgrad(dw_ref.at[1], duv, xv)
        db_ref[0] = jnp.sum(dug, axis=0, keepdims=True)
        db_ref[1] = jnp.sum(duv, axis=0, keepdims=True)

    lo = lambda r: pl.BlockSpec((r, tc), lambda j: (0, j))
    hi = lambda r: pl.BlockSpec((r, tc), lambda j: (0, nb + j))
    st = lambda r: pl.BlockSpec((2, r, tc), lambda j: (0, 0, j))
    return pl.pallas_call(
        body, name=name, grid=(nb,),
        in_specs=[st(T), lo(T), lo(3), hi(3), lo(1), hi(1)],
        out_specs=[st(T), st(3), st(1)],
        out_shape=[jax.ShapeDtypeStruct((2, T, DFF), BF), jax.ShapeDtypeStruct((2, 3, DFF), F32),
                   jax.ShapeDtypeStruct((2, 1, DFF), F32)],
        compiler_params=pltpu.CompilerParams(dimension_semantics=("parallel",)),
    )(u0, df, cw, cw, cb, cb)


def gate_merge_bwd(p, ya, yc, dm, *, name, tm=256):
    def body(ga_ref, gc_ref, ya_ref, yc_ref, dm_ref, dya_ref, dyc_ref, dp_ref):
        sa, sc_ = jax.nn.sigmoid(ga_ref[...]), jax.nn.sigmoid(gc_ref[...])
        dmv = dm_ref[...]
        dya_ref[...] = (dmv * sa).astype(BF)
        dyc_ref[...] = (dmv * sc_).astype(BF)
        dp_ref[:, 0:D] = (dmv * ya_ref[...] * (sa * (1.0 - sa))).astype(BF)
        dp_ref[:, D:2 * D] = (dmv * yc_ref[...] * (sc_ * (1.0 - sc_))).astype(BF)

    blk = pl.BlockSpec((tm, D), lambda i: (i, 0))
    sh = jax.ShapeDtypeStruct((T, D), BF)
    return pl.pallas_call(
        body, name=name, grid=(T // tm,),
        in_specs=[pl.BlockSpec((tm, D), lambda i: (i, O_GA // D)), pl.BlockSpec((tm, D), lambda i: (i, O_GC // D)), blk, blk, blk],
        out_specs=[blk, blk, pl.BlockSpec((tm, 2 * D), lambda i: (i, 0))],
        out_shape=[sh, sh, jax.ShapeDtypeStruct((T, NIN), BF)],
        compiler_params=pltpu.CompilerParams(dimension_semantics=("parallel",)),
    )(p, p, ya, yc, dm)


def convz_bwd(p, dz, cw, cb, dp, *, name):
    o0 = O_CV // (3 * CVB)

    def body(p_ref, dz_ref, w_ref, bias_ref, dp_in, dp_ref, dw_ref, dbias_ref):
        xv, bv, cv = p_ref[:, 0:CVB], p_ref[:, CVB:2 * CVB], p_ref[:, 2 * CVB:3 * CVB]
        ci = cv * xv
        dwc = _conv(ci, w_ref, bias_ref)
        dzv = dz_ref[...]
        ddw = dzv * bv
        dci = _conv_t(ddw, w_ref)
        dp_ref[:, 0:CVB] = (dci * cv).astype(BF)
        dp_ref[:, CVB:2 * CVB] = (dzv * dwc).astype(BF)
        dp_ref[:, 2 * CVB:3 * CVB] = (dci * xv).astype(BF)
        _conv_wgrad(dw_ref, ddw, ci)
        dbias_ref[...] = jnp.sum(ddw, axis=0, keepdims=True)

    own = lambda r: pl.BlockSpec((r, CVB), lambda j: (0, j))
    return pl.pallas_call(
        body, name=name, grid=(CONV // CVB,),
        in_specs=[pl.BlockSpec((T, 3 * CVB), lambda j: (0, o0 + j)), own(T), own(3), own(1),
                  pl.BlockSpec(memory_space=pl.ANY)],
        out_specs=[pl.BlockSpec((T, 3 * CVB), lambda j: (0, o0 + j)), own(3), own(1)],
        out_shape=[jax.ShapeDtypeStruct((T, NIN), BF), jax.ShapeDtypeStruct((3, CONV), F32),
                   jax.ShapeDtypeStruct((1, CONV), F32)],
        input_output_aliases={4: 0},
        compiler_params=pltpu.CompilerParams(dimension_semantics=("parallel",)),
    )(p, dz, cw, cb, dp)


def attn_bwd(q, k, v, do, *, name, tq=256):
    def body(q_ref, k_ref, v_ref, do_ref, dq_ref, dk_ref, dv_ref):
        h, i = pl.program_id(0), pl.program_id(1)
        qv, kv = q_ref[...], k_ref[...]
        s = lax.dot_general(qv, kv, (((1,), (1,)), ((), ())), preferred_element_type=F32) * SCALE
        m = jnp.max(s, axis=-1, keepdims=True)
        e = jnp.exp(s - m)
        pr = e * (1.0 / jnp.sum(e, axis=-1, keepdims=True))
        mask = _head_mask(h)
        vm = jnp.where(mask, v_ref[...], jnp.zeros_like(v_ref[...]))
        dom = jnp.where(mask, do_ref[...], jnp.zeros_like(do_ref[...]))
        dp = lax.dot_general(dom, vm, (((1,), (1,)), ((), ())), preferred_element_type=F32)
        ds = (pr * (dp - jnp.sum(pr * dp, axis=-1, keepdims=True)) * SCALE).astype(BF)
        dq_ref[...] = jnp.dot(ds, kv, preferred_element_type=F32)
        dk = lax.dot_general(ds, qv, (((0,), (0,)), ((), ())), preferred_element_type=F32)
        dv = lax.dot_general(pr.astype(BF), dom, (((0,), (0,)), ((), ())), preferred_element_type=F32)

        @pl.when(i == 0)
        def _():
            dk_ref[...] = dk

        @pl.when(i > 0)
        def _():
            dk_ref[...] += dk

        @pl.when((i == 0) & (h % 2 == 0))
        def _():
            dv_ref[...] = dv

        @pl.when((i > 0) | (h % 2 == 1))
        def _():
            dv_ref[...] += dv

    return pl.pallas_call(
        body, name=name, grid=(NH, T // tq),
        in_specs=[pl.BlockSpec((tq, HP), lambda h, i: (i, h)), pl.BlockSpec((TKV, HP), lambda h, i: (0, h)),
                  pl.BlockSpec((TKV, 2 * DV), lambda h, i: (0, h // 2)), pl.BlockSpec((tq, 2 * DV), lambda h, i: (i, h // 2))],
        out_specs=[pl.BlockSpec((tq, HP), lambda h, i: (i, h)), pl.BlockSpec((TKV, HP), lambda h, i: (0, h)),
                   pl.BlockSpec((TKV, 2 * DV), lambda h, i: (0, h // 2))],
        out_shape=[jax.ShapeDtypeStruct((T, NH * HP), F32), jax.ShapeDtypeStruct((TKV, NH * HP), F32),
                   jax.ShapeDtypeStruct((TKV, NH * DV), F32)],
        compiler_params=pltpu.CompilerParams(dimension_semantics=("arbitrary", "arbitrary")),
    )(q, k, v, do)


def qprep_bwd(p, dq, qg, wq2, cq_t, sq_t, dp, *, name, tm=256):
    qcol = O_Q // 512

    def body(p_ref, dq_ref, g_ref, w_ref, c_ref, s_ref, dp_in, dp_ref, dq2_ref, dg_ref):
        i = pl.program_id(0)
        dqv = dq_ref[...]
        cc = jnp.concatenate([c_ref[...]] * NH, axis=1)
        ss = jnp.concatenate([s_ref[...]] * NH, axis=1)
        dq2 = jnp.concatenate([dqv * cc, dqv * ss], axis=1).astype(BF)
        dq2_ref[...] = dq2
        dcq = lax.dot_general(dq2, w_ref[...], (((1,), (1,)), ((), ())), preferred_element_type=F32)
        pq = p_ref[...]
        r = lax.rsqrt(jnp.sum(pq * pq, axis=-1, keepdims=True) * (1.0 / QL) + EPS)
        xh = pq * r
        a = dcq * g_ref[...]
        dp_ref[...] = (r * (a - xh * (jnp.sum(a * xh, axis=-1, keepdims=True) * (1.0 / QL)))).astype(BF)
        dg = jnp.sum(dcq * xh, axis=0, keepdims=True)

        @pl.when(i == 0)
        def _():
            dg_ref[...] = dg

        @pl.when(i > 0)
        def _():
            dg_ref[...] += dg

    return pl.pallas_call(
        body, name=name, grid=(T // tm,),
        in_specs=[pl.BlockSpec((tm, 512), lambda i: (i, qcol)), pl.BlockSpec((tm, NH * HP), lambda i: (i, 0)), _row(512),
                  pl.BlockSpec((512, 2 * NH * HP), lambda i: (0, 0)),
                  pl.BlockSpec((tm, HP), lambda i: (i, 0)), pl.BlockSpec((tm, HP), lambda i: (i, 0)),
                  pl.BlockSpec(memory_space=pl.ANY)],
        out_specs=[pl.BlockSpec((tm, 512), lambda i: (i, qcol)), pl.BlockSpec((tm, 2 * NH * HP), lambda i: (i, 0)), _row(512)],
        out_shape=[jax.ShapeDtypeStruct((T, NIN), BF), jax.ShapeDtypeStruct((T, 2 * NH * HP), BF),
                   jax.ShapeDtypeStruct((1, 512), F32)],
        input_output_aliases={6: 0},
        compiler_params=pltpu.CompilerParams(dimension_semantics=("arbitrary",)),
    )(p, dq, qg, wq2, cq_t, sq_t, dp)


def kvprep_bwd(pc, p, dk, dv, kvg, wkv2, ck, sk, dp, *, name, tm=256):
    assert tm == TC
    nb = TKV // tm
    kvcol = O_KV // 512

    def body(pc_ref, p_ref, dk_ref, dv_ref, g_ref, w_ref, ck_ref, sk_ref, dp_in, dp_ref, dpc_ref, dkv2_ref, dg_ref):
        i = pl.program_id(0)
        t = jnp.where(i == NLAT, pc_ref[...], p_ref[...])
        pk = t[:, :KVL]
        r = lax.rsqrt(jnp.mean(pk * pk, axis=-1, keepdims=True) + EPS)
        xh = pk * r
        dkv = dk_ref[...]
        dkv2 = jnp.concatenate([dkv, dv_ref[...]], axis=1).astype(BF)
        dkv2_ref[...] = dkv2
        dckv = lax.dot_general(dkv2, w_ref[...], (((1,), (1,)), ((), ())), preferred_element_type=F32)
        a = dckv * g_ref[...]
        dpk = r * (a - xh * jnp.mean(a * xh, axis=-1, keepdims=True))
        dkr = dkv[:, 0:HP]
        for hh in range(1, NH):
            dkr = dkr + dkv[:, hh * HP:(hh + 1) * HP]
        res = jnp.concatenate([dpk, dkr * ck_ref[...], dkr * sk_ref[...]], axis=1).astype(BF)
        dg = jnp.sum(dckv * xh, axis=0, keepdims=True)

        @pl.when(i == 0)
        def _():
            dg_ref[...] = dg

        @pl.when(i > 0)
        def _():
            dg_ref[...] += dg

        @pl.when(i < NLAT)
        def _():
            dp_ref[...] = res

        @pl.when(i == NLAT)
        def _():
            dpc_ref[...] = res

    rb = lambda w: pl.BlockSpec((tm, w), lambda i: (i, 0))
    return pl.pallas_call(
        body, name=name, grid=(nb,),
        in_specs=[pl.BlockSpec((tm, 512), lambda i: (0, 0)),
                  pl.BlockSpec((tm, 512), lambda i: (jnp.minimum(i, NLAT - 1), kvcol)),
                  rb(NH * HP), rb(NH * DV), _row(KVL), pl.BlockSpec((KVL, NH * HP + NH * DV), lambda i: (0, 0)),
                  rb(HP), rb(HP), pl.BlockSpec(memory_space=pl.ANY)],
        out_specs=[pl.BlockSpec((tm, 512), lambda i: (jnp.minimum(i, NLAT - 1), kvcol)),
                   pl.BlockSpec((tm, 512), lambda i: (0, 0)), rb(NH * HP + NH * DV), _row(KVL)],
        out_shape=[jax.ShapeDtypeStruct((T, NIN), BF), jax.ShapeDtypeStruct((TC, 512), BF),
                   jax.ShapeDtypeStruct((TKV, NH * HP + NH * DV), BF), jax.ShapeDtypeStruct((1, KVL), F32)],
        input_output_aliases={8: 0},
        compiler_params=pltpu.CompilerParams(dimension_semantics=("arbitrary",)),
    )(pc, p, dk, dv, kvg, wkv2, ck, sk, dp)


def _pieces(src, width, n):
    out, c = [], src
    while c < src + width:
        k = c // n
        w = min(src + width, (k + 1) * n) - c
        out.append((k, c - k * n, c - src, w))
        c += w
    return out


def _win_moves():
    mv = [(2208, 1024, O_GA), (3232, 1024, O_GC), (0, KVL, O_KV), (256, DR, O_KV + KVL + DN), (288, QL, O_Q)]
    mv += [(256 + _swap_start(g), 8, O_KV + KVL + HP + DN + 8 * g) for g in range(4)]
    for j in range(CONV // CVB):
        base = O_CV + 3 * CVB * j
        mv += [(672 + CVB * j, CVB, base), (1184 + CVB * j, CVB, base + CVB), (1696 + CVB * j, CVB, base + 2 * CVB)]
    return mv


_WIN_ZERO = [(O_KV + KVL, DN), (O_KV + KVL + DN + DR, HP - DN - DR), (O_KV + KVL + HP, DN),
             (O_KV + KVL + HP + DN + DR, HP - DN - DR), (O_Q + QL, 512 - QL)]


def build_win(g, *, name, tm=256):
    def body(g_ref, o_ref):
        for src, w, dst in _win_moves():
            for k, a, off, pw in _pieces(src, w, SH_IN):
                o_ref[:, dst + off:dst + off + pw] = g_ref[k, :, a:a + pw]
        for c0, w in _WIN_ZERO:
            o_ref[:, c0:c0 + w] = jnp.zeros((tm, w), o_ref.dtype)

    return pl.pallas_call(
        body, name=name, grid=(D // tm,), in_specs=[pl.BlockSpec((NDEV, tm, SH_IN), lambda i: (0, i, 0))],
        out_specs=pl.BlockSpec((tm, NIN), lambda i: (i, 0)), out_shape=jax.ShapeDtypeStruct((D, NIN), g.dtype),
        compiler_params=pltpu.CompilerParams(dimension_semantics=("parallel",)),
    )(g)


def shard_win_grad(dw, dwc, *, name, tm=256):
    def body(dw_ref, dwc_ref, o_ref, kvs):
        kvs[...] = dw_ref[:, O_KV:O_KV + 512] + dwc_ref[...]

        def src(col, w):
            if O_KV <= col < O_KV + 512:
                return kvs[:, col - O_KV:col - O_KV + w]
            return dw_ref[:, col:col + w]

        for s, w, dst in _win_moves():
            if w == 8 or s == 256:
                continue
            for k, a, off, pw in _pieces(s, w, SH_IN):
                o_ref[k, :, a:a + pw] = src(dst + off, pw).astype(o_ref.dtype)
        for g in range(4):
            val = src(O_KV + KVL + DN + 8 * g, 8) + src(O_KV + KVL + HP + DN + _swap_start(g), 8)
            o_ref[0, :, 256 + 8 * g:256 + 8 * g + 8] = val.astype(o_ref.dtype)

    return pl.pallas_call(
        body, name=name, grid=(D // tm,),
        in_specs=[pl.BlockSpec((tm, NIN), lambda i: (i, 0)), pl.BlockSpec((tm, 512), lambda i: (i, 0))],
        out_specs=pl.BlockSpec((NDEV, tm, SH_IN), lambda i: (0, i, 0)),
        out_shape=jax.ShapeDtypeStruct((NDEV, D, SH_IN), BF),
        scratch_shapes=[pltpu.VMEM((tm, 512), F32)],
        compiler_params=pltpu.CompilerParams(dimension_semantics=("parallel",)),
    )(dw, dwc)


def build_wq_wkv(gq, gkv, *, name):
    def body(gq_ref, gkv_ref, q_ref, kv_ref):
        q_ref[...] = jnp.zeros_like(q_ref)
        kv_ref[...] = jnp.zeros_like(kv_ref)
        for h in range(NH):
            q_ref[0:QL, h * HP:h * HP + DN + DR] = gq_ref[h]
            for g in range(4):
                c0 = NH * HP + h * HP + DN + 8 * g
                q_ref[0:QL, c0:c0 + 8] = gq_ref[h, :, DN + _swap_start(g):DN + _swap_start(g) + 8]
            kv_ref[:, h * HP:h * HP + DN] = gkv_ref[h, :, 0:DN]
            kv_ref[:, NH * HP + h * DV:NH * HP + (h + 1) * DV] = gkv_ref[h, :, DN:DN + DV]

    vm = pl.BlockSpec(memory_space=pltpu.VMEM)
    return pl.pallas_call(
        body, name=name, in_specs=[vm, vm], out_specs=[vm, vm],
        out_shape=[jax.ShapeDtypeStruct((512, 2 * NH * HP), gq.dtype), jax.ShapeDtypeStruct((KVL, NH * HP + NH * DV), gq.dtype)],
    )(gq, gkv)


def shard_wq_wkv_grad(dwq2, dwkv2, *, name):
    def body(q_ref, kv_ref, gq_ref, gkv_ref):
        for h in range(NH):
            gq_ref[h, :, 0:DN] = q_ref[0:QL, h * HP:h * HP + DN].astype(BF)
            for g in range(4):
                a = q_ref[0:QL, h * HP + DN + 8 * g:h * HP + DN + 8 * g + 8]
                c0 = NH * HP + h * HP + DN + _swap_start(g)
                gq_ref[h, :, DN + 8 * g:DN + 8 * g + 8] = (a + q_ref[0:QL, c0:c0 + 8]).astype(BF)
            gkv_ref[h, :, 0:DN] = kv_ref[:, h * HP:h * HP + DN].astype(BF)
            gkv_ref[h, :, DN:DN + DV] = kv_ref[:, NH * HP + h * DV:NH * HP + (h + 1) * DV].astype(BF)

    vm = pl.BlockSpec(memory_space=pltpu.VMEM)
    return pl.pallas_call(
        body, name=name, in_specs=[vm, vm], out_specs=[vm, vm],
        out_shape=[jax.ShapeDtypeStruct((NDEV, QL, (DN + DR)), BF), jax.ShapeDtypeStruct((NDEV, KVL, DN + DV), BF)],
    )(dwq2, dwkv2)


def unshard_cols(g, *, name, tm=256):
    _, K, n = g.shape
    tm = _pick(K, tm, 16)

    def body(g_ref, o_ref):
        for k in range(NDEV):
            o_ref[:, k * n:(k + 1) * n] = g_ref[k]

    return pl.pallas_call(
        body, name=name, grid=(K // tm,), in_specs=[pl.BlockSpec((NDEV, tm, n), lambda i: (0, i, 0))],
        out_specs=pl.BlockSpec((tm, NDEV * n), lambda i: (i, 0)), out_shape=jax.ShapeDtypeStruct((K, NDEV * n), g.dtype),
        compiler_params=pltpu.CompilerParams(dimension_semantics=("parallel",)),
    )(g)


def shard_cols(w, *, name, tm=256):
    K, n8 = w.shape
    n = n8 // NDEV
    tm = _pick(K, tm, 16)

    def body(w_ref, o_ref):
        for k in range(NDEV):
            o_ref[k] = w_ref[:, k * n:(k + 1) * n]

    return pl.pallas_call(
        body, name=name, grid=(K // tm,), in_specs=[pl.BlockSpec((tm, n8), lambda i: (i, 0))],
        out_specs=pl.BlockSpec((NDEV, tm, n), lambda i: (0, i, 0)), out_shape=jax.ShapeDtypeStruct((NDEV, K, n), w.dtype),
        compiler_params=pltpu.CompilerParams(dimension_semantics=("parallel",)),
    )(w)


def _rope_tables():
    t = np.arange(T)
    row = (t // GRID_W).astype(np.float32)
    col = (t % GRID_W).astype(np.float32)
    axis_dim = DR // 2
    inv = (np.float32(ROPE_THETA) ** (-np.arange(0, axis_dim, 2, dtype=np.float32) / np.float32(axis_dim))).astype(np.float32)
    ar, ac = (row[:, None] * inv).astype(np.float32), (col[:, None] * inv).astype(np.float32)
    cosv = np.concatenate([np.cos(ar), np.cos(ar), np.cos(ac), np.cos(ac)], axis=1).astype(np.float32)
    sinv = np.concatenate([-np.sin(ar), np.sin(ar), -np.sin(ac), np.sin(ac)], axis=1).astype(np.float32)
    ck = np.zeros((TKV, HP), np.float32)
    sk = np.zeros((TKV, HP), np.float32)
    ck[T:, DN:DN + DR] = 1.0
    ck[:T, DN:DN + DR] = cosv
    sk[:T, DN:DN + DR] = sinv
    cq = np.zeros((T, HP), np.float32)
    cq[:, :DN] = 1.0
    cq[:, DN:DN + DR] = cosv
    return jnp.asarray(ck), jnp.asarray(sk), jnp.asarray(cq), jnp.asarray(sk[:T])


def _local_step(x, ctx, tgt, mod_lat, mod_ctx, n1g, qg, kvg, n2g, fg, conv_w, conv_b, ffn_w, ffn_b, get_w, put_g, dep0):
    sh1, sc1, g1, sh2, sc2, g2 = [mod_lat[:, i * D:(i + 1) * D] for i in range(6)]
    csh1, csc1 = mod_ctx[:, 0:D], mod_ctx[:, D:2 * D]
    ck, sk, cq_t, sq_t = _rope_tables()
    qg_p = jnp.pad(qg, ((0, 0), (0, 512 - QL)))

    hcat = normmod_cat(ctx, x, n1g, csc1, csh1, sc1, sh1, dep0, name="normmod1")
    win = get_w("in", hcat)
    p = mm(hcat, win, M=T, tn=768, name="in_proj")
    pc = mm(hcat, win, M=TC, N=512, a_off=(T, 0), b_off=(0, O_KV), name="in_proj_ctx")
    wq2, wkv2, wao, wco, wo = get_w("mid", p)
    kh, vh, ckv = kvprep(pc, p, kvg, wkv2, ck, sk, name="kvprep")
    qr, cq = qprep(p, qg_p, wq2, cq_t, sq_t, name="qprep")
    o = attn_fwd(qr, kh, vh, name="attn_fwd")
    z = convz(p, conv_w, conv_b, name="convz")
    ya = mm(o, wao, name="attn_out")
    yc = mm(z, wco, name="conv_out")
    merged = gate_merge(p, ya, yc, name="gate_merge")
    a_out = mm(merged, wo, name="o_proj")
    x1, h2 = resid_normmod(x, a_out, g1, n2g, sc2, sh2, name="resid_normmod2")
    wup, wdn = get_w("ffn", h2)
    u0 = mm(h2, wup, o_stack=True, tn=1408, name="up_proj")
    f = ffn_act(u0, ffn_w, ffn_b, name="ffn_act")
    dn = mm(f, wdn, tm=512, tk=DFF, name="down_proj")
    dx2, dd, dfg, loss = final_loss(x1, dn, g2, fg, tgt, name="final_loss")

    df = mm(dd, wdn, tb=True, tn=1408, name="down_proj_dx")
    dwdn = mm(f, dd, ta=True, out_dtype=BF, tm=1408, name="down_proj_dw")
    du0, dffn_w, dffn_b = ffn_act_bwd(u0, df, ffn_w, ffn_b, name="ffn_act_bwd")
    dwup = mm(h2, du0, ta=True, b_stack=True, out_dtype=BF, tn=1408, name="up_proj_dw")
    tok = put_g("ffn", dict(dwup=dwup, dwdn=dwdn))
    dh2 = mm(du0, wup, tb=True, a_stack=True, dep=tok, name="up_proj_dx")
    dx1, da, st2 = normmod_bwd(x1, dh2, n2g, sc2, dx2, dn, g1, name="normmod2_bwd")

    dmerged = mm(da, wo, tb=True, name="o_proj_dx")
    dwo = mm(merged, da, ta=True, out_dtype=BF, tn=512, name="o_proj_dw")
    dya, dyc, dp = gate_merge_bwd(p, ya, yc, dmerged, name="gate_merge_bwd")
    do = mm(dya, wao, tb=True, out_dtype=BF, name="attn_out_dx")
    dwao = mm(o, dya, ta=True, out_dtype=BF, tn=512, name="attn_out_dw")
    dwco = mm(z, dyc, ta=True, out_dtype=BF, tn=512, name="conv_out_dw")
    tok = put_g("mid", dict(dwao=dwao, dwco=dwco, dwo=dwo))
    dz = mm(dyc, wco, tb=True, dep=tok, name="conv_out_dx")
    dp, dconv_w, dconv_b = convz_bwd(p, dz, conv_w, conv_b, dp, name="convz_bwd")
    dq, dk, dv = attn_bwd(qr, kh, vh, do, name="attn_bwd")
    dp, dq2, dqg = qprep_bwd(p, dq, qg_p, wq2, cq_t, sq_t, dp, name="qprep_bwd")
    dwq2 = mm(cq, dq2, ta=True, name="q_up_dw")
    dp, dpc, dkv2, dkvg = kvprep_bwd(pc, p, dk, dv, kvg, wkv2, ck, sk, dp, name="kvprep_bwd")
    dwkv2 = mm(ckv, dkv2, ta=True, name="kv_up_dw")
    tok = put_g("qkv", dict(dwq2=dwq2, dwkv2=dwkv2))

    dwin = mm(hcat, dp, ta=True, K=T, tn=768, dep=tok, name="in_proj_dw")
    dwin_c = mm(hcat, dpc, ta=True, K=TC, a_off=(T, 0), name="in_proj_ctx_dw")
    tok = put_g("in", dict(dwin=dwin, dwin_c=dwin_c))
    dh = mm(dp, win, tb=True, dep=tok, name="in_proj_dx")
    dhc = mm(dpc, win, tb=True, N=D, K=512, b_off=(0, O_KV), name="in_proj_ctx_dx")
    dx, _, st1 = normmod_bwd(x, dh, n1g, sc1, dx1, a_out, g1, name="normmod1_bwd")
    stc = normmod_bwd(ctx, dhc, n1g, csc1, None, None, None, name="normmod1_ctx_bwd")

    zrow = jnp.zeros((1, D), F32)
    dmod_lat = jnp.concatenate([st1[0:1], st1[1:2], st1[3:4], st2[0:1], st2[1:2], st2[3:4]], axis=1)
    dmod_ctx = jnp.concatenate([stc[0:1], stc[1:2], zrow, zrow, zrow, zrow], axis=1)
    return dict(
        loss=loss, dx=dx, dmod_lat=dmod_lat, dmod_ctx=dmod_ctx,
        dn1g=st1[2:3] + stc[2:3], dqg=dqg, dkvg=dkvg, dn2g=st2[2:3], dfg=dfg,
        dconv_w=dconv_w, dconv_b=dconv_b, dffn_w=dffn_w, dffn_b=dffn_b)


def _me():
    x, y, c = lax.axis_index("x"), lax.axis_index("y"), lax.axis_index("c")
    return x, y, c, 4 * x + 2 * y + c


def _peer(x, y, c, k):
    px = 1 - x if k & 4 else x
    py = 1 - y if k & 2 else y
    pc = 1 - c if k & 1 else c
    return (px, py, pc), 4 * px + 2 * py + pc


def _exchange_tiles(src_of_peer, buf, send_sem, recv_sem):
    x, y, c, me = _me()
    for k in range(1, NDEV):
        dev, lin = _peer(x, y, c, k)
        pltpu.make_async_remote_copy(src_ref=src_of_peer(lin), dst_ref=buf.at[me], send_sem=send_sem, recv_sem=recv_sem,
                                     device_id=dev, device_id_type=MESH).start()
    seven = buf.at[pl.ds(0, NDEV - 1)]
    pltpu.make_async_remote_copy(src_ref=seven, dst_ref=seven, send_sem=send_sem, recv_sem=recv_sem,
                                 device_id=(x, y, c), device_id_type=MESH).wait()


def _silu(z):
    return z * jax.nn.sigmoid(z)


def ada_fwd(c, c_ctx, ffn_w, conv_w, w_shard, b_shard, deps, *, name):
    nsh = w_shard.shape[1]
    deps = [d for d in deps if d is not None]

    def body(c_ref, cc_ref, fw_ref, cw_ref, w_ref, b_ref, *rest):
        s_ref, m_ref, mine, res, sems = rest[len(deps):]
        x, y, c, me = _me()
        mine[0:1, :] = _silu(c_ref[...])
        mine[1:2, :] = _silu(cc_ref[...])
        mine[2:5, :] = fw_ref[...]
        mine[5:8, :] = cw_ref[...]
        s_ref[me] = mine[...]
        _exchange_tiles(lambda lin: mine, s_ref, sems.at[0], sems.at[1])
        sall = s_ref[...].reshape(NDEV * 8, D).astype(BF)
        r = jnp.dot(sall, w_ref[...].astype(BF), preferred_element_type=F32) + b_ref[...]
        res[...] = r.reshape(NDEV, 8, nsh)
        m_ref[me] = res[me]
        _exchange_tiles(lambda lin: res.at[lin], m_ref, sems.at[2], sems.at[3])

    vm = pl.BlockSpec(memory_space=pltpu.VMEM)
    return pl.pallas_call(
        body, name=name, in_specs=[vm] * 6 + [pl.BlockSpec(memory_space=pl.ANY)] * len(deps), out_specs=[vm, vm],
        out_shape=[jax.ShapeDtypeStruct((NDEV, 8, D), F32), jax.ShapeDtypeStruct((NDEV, 8, nsh), F32)],
        scratch_shapes=[pltpu.VMEM((8, D), F32), pltpu.VMEM((NDEV, 8, nsh), F32), pltpu.SemaphoreType.DMA((4,))],
    )(c, c_ctx, ffn_w, conv_w, w_shard, b_shard, *deps)


P_DML, P_DMC, P_N1, P_QG, P_KVG, P_CB, P_N2, P_FB, P_FG, P_CW, P_FW, P_LOSS, P_ROWS = 0, 6, 12, 13, 14, 15, 16, 17, 23, 24, 27, 45, 48
FROWS = 3


def sync_small(r, dep, *, name):
    ins = [r["dmod_lat"], r["dmod_ctx"], r["dn1g"], r["dqg"], r["dkvg"], r["dconv_b"], r["dn2g"], r["dffn_b"], r["dfg"],
           r["dconv_w"], r["dffn_w"], r["loss"]]

    def put_wide(p, row0, row, n):
        for j in range(-(-n // D)):
            w = min(D, n - j * D)
            p[row0 + j:row0 + j + 1, 0:w] = row[:, j * D:j * D + w]

    def body(dml, dmc, n1, qg, kvg, cb, n2, fb, fg, cw, fw, loss, dep_ref, a_ref, sum_ref, p, sems):
        x, y, c, me = _me()
        p[...] = jnp.zeros_like(p)
        put_wide(p, P_DML, dml, 6 * D)
        put_wide(p, P_DMC, dmc, 6 * D)
        put_wide(p, P_N1, n1, D)
        put_wide(p, P_QG, qg, 512)
        put_wide(p, P_KVG, kvg, KVL)
        put_wide(p, P_CB, cb, CONV)
        put_wide(p, P_N2, n2, D)
        put_wide(p, P_FG, fg, D)
        put_wide(p, P_LOSS, loss, 128)
        for s in range(2):
            put_wide(p, P_FB + FROWS * s, fb.at[s], DFF)
        for k in range(3):
            put_wide(p, P_CW + k, cw.at[k:k + 1], CONV)
            for s in range(2):
                put_wide(p, P_FW + FROWS * (2 * k + s), fw.at[s, k:k + 1], DFF)
        a_ref[me] = p[...]
        _exchange_tiles(lambda lin: p, a_ref, sems.at[0], sems.at[1])
        acc = a_ref[0]
        for k in range(1, NDEV):
            acc = acc + a_ref[k]
        sum_ref[...] = acc

    vm = pl.BlockSpec(memory_space=pltpu.VMEM)
    return pl.pallas_call(
        body, name=name, in_specs=[vm] * len(ins) + [pl.BlockSpec(memory_space=pl.ANY)], out_specs=[vm, vm],
        out_shape=[jax.ShapeDtypeStruct((NDEV, P_ROWS, D), F32), jax.ShapeDtypeStruct((P_ROWS, D), F32)],
        scratch_shapes=[pltpu.VMEM((P_ROWS, D), F32), pltpu.SemaphoreType.DMA((2,))],
    )(*ins, dep)


def ada_bwd(s_all, dml, dmc, w_shard, c_ctx, *, name):
    nsh = w_shard.shape[1]

    def body(s_ref, dml_ref, dmc_ref, w_ref, c_ref, dw_ref, gc_ref, s16, dm16, part, buf, sems):
        x, y, c, me = _me()
        s16[...] = jnp.zeros_like(s16)
        dm16[...] = jnp.zeros_like(dm16)
        for k in range(NDEV):
            s16[k:k + 1, :] = s_ref[k, 0:1, :]
        s16[8:9, :] = s_ref[0, 1:2, :]
        dm16[0:8, :] = dml_ref[...]
        dm16[8:9, :] = dmc_ref[...]
        dw_ref[...] = lax.dot_general(s16[...].astype(BF), dm16[...].astype(BF), (((0,), (0,)), ((), ())),
                                      preferred_element_type=F32)
        part[...] = lax.dot_general(dm16[8:16, :].astype(BF), w_ref[...].astype(BF), (((1,), (1,)), ((), ())),
                                    preferred_element_type=F32)
        buf[me] = part[...]
        _exchange_tiles(lambda lin: part, buf, sems.at[0], sems.at[1])
        acc = buf[0]
        for k in range(1, NDEV):
            acc = acc + buf[k]
        z = c_ref[...]
        sg = jax.nn.sigmoid(z)
        gc_ref[...] = acc * (sg * (1.0 + z * (1.0 - sg)))

    vm = pl.BlockSpec(memory_space=pltpu.VMEM)
    return pl.pallas_call(
        body, name=name, in_specs=[vm] * 5, out_specs=[vm, vm],
        out_shape=[jax.ShapeDtypeStruct((D, nsh), F32), jax.ShapeDtypeStruct((8, D), F32)],
        scratch_shapes=[pltpu.VMEM((16, D), F32), pltpu.VMEM((16, nsh), F32), pltpu.VMEM((8, D), F32),
                        pltpu.VMEM((NDEV, 8, D), F32), pltpu.SemaphoreType.DMA((2,))],
    )(s_all, dml, dmc, w_shard, c_ctx)


HBM_SPEC = pl.BlockSpec(memory_space=pltpu.HBM)
SEM_SPEC = pl.BlockSpec(memory_space=pltpu.SEMAPHORE)
EFFECT = pltpu.SideEffectType.DATAFLOW_SIDE_EFFECTING


def _exchange_copies(srcs, lands, send, recv, per_peer):
    x, y, c, me = _me()
    cps = []
    for t in range(len(srcs)):
        for k in range(1, NDEV):
            dev, lin = _peer(x, y, c, k)
            cps.append(pltpu.make_async_remote_copy(
                src_ref=srcs[t].at[lin] if per_peer else srcs[t], dst_ref=lands[t].at[me],
                send_sem=send.at[7 * t + k - 1], recv_sem=recv.at[7 * t + k - 1], device_id=dev, device_id_type=MESH))
    return cps


def _own_copies(srcs, lands, own, per_peer):
    me = _me()[3]
    return [pltpu.make_async_copy(srcs[t].at[me] if per_peer else srcs[t], lands[t].at[me], own.at[t])
            for t in range(len(srcs))]


def exchange_start(srcs, *, per_peer, name, dep=None):
    nt = len(srcs)
    land_shapes = [(a.shape if per_peer else (NDEV,) + a.shape) for a in srcs]
    deps = [] if dep is None else [dep]

    def body(*refs):
        src, land = refs[:nt], refs[nt:2 * nt]
        send, recv, own = refs[2 * nt + len(deps):2 * nt + len(deps) + 3]
        for cp in _exchange_copies(src, land, send, recv, per_peer) + _own_copies(src, land, own, per_peer):
            cp.start()
        refs[-1][...] = jnp.zeros_like(refs[-1])

    hb = lambda a: pltpu.with_memory_space_constraint(a, pltpu.HBM)
    outs = pl.pallas_call(
        body, name=name,
        out_shape=(pltpu.SemaphoreType.DMA((7 * nt,)), pltpu.SemaphoreType.DMA((7 * nt,)), pltpu.SemaphoreType.DMA((nt,)),
                   *[pltpu.HBM(a.shape, a.dtype) for a in srcs], *[pltpu.HBM(s, a.dtype) for s, a in zip(land_shapes, srcs)],
                   jax.ShapeDtypeStruct((8, 128), F32)),
        in_specs=[HBM_SPEC] * (2 * nt) + [pl.BlockSpec(memory_space=pl.ANY)] * len(deps),
        out_specs=(SEM_SPEC, SEM_SPEC, SEM_SPEC, *([HBM_SPEC] * (2 * nt)), pl.BlockSpec(memory_space=pltpu.VMEM)),
        input_output_aliases={i: 3 + i for i in range(2 * nt)},
        compiler_params=pltpu.CompilerParams(has_side_effects=EFFECT),
    )(*[hb(a) for a in srcs], *[hb(lax.empty(s, a.dtype)) for s, a in zip(land_shapes, srcs)], *deps)
    return dict(send=outs[0], recv=outs[1], own=outs[2], src=list(outs[3:3 + nt]), land=list(outs[3 + nt:3 + 2 * nt]),
                token=outs[-1], per_peer=per_peer)


def exchange_wait(h, after, *, name):
    nt = len(h["src"])
    per_peer = h["per_peer"]

    def body(*refs):
        src, land, send, recv, own = refs[:nt], refs[nt:2 * nt], refs[2 * nt], refs[2 * nt + 1], refs[2 * nt + 2]
        for cp in _exchange_copies(src, land, send, recv, per_peer):
            cp.wait_send()
            cp.wait_recv()
        for cp in _own_copies(src, land, own, per_peer):
            cp.wait()

    outs = pl.pallas_call(
        body, name=name,
        out_shape=(*[pltpu.HBM(a.shape, a.dtype) for a in h["src"]], *[pltpu.HBM(a.shape, a.dtype) for a in h["land"]]),
        in_specs=[HBM_SPEC] * (2 * nt) + [SEM_SPEC, SEM_SPEC, SEM_SPEC, pl.BlockSpec(memory_space=pl.ANY)],
        out_specs=tuple([HBM_SPEC] * (2 * nt)),
        input_output_aliases={i: i for i in range(2 * nt)},
        compiler_params=pltpu.CompilerParams(has_side_effects=EFFECT),
    )(*h["src"], *h["land"], h["send"], h["recv"], h["own"], after)
    return list(outs[nt:])


def _adamw_math(w, g, m, v):
    nm = B1 * m + (1.0 - B1) * g
    nv = B2 * v + (1.0 - B2) * (g * g)
    m_hat = nm / (1.0 - B1 ** STEP)
    v_hat = nv / (1.0 - B2 ** STEP)
    return -LR * (m_hat / (jnp.sqrt(v_hat) + AEPS) + WD * w), nm, nv


def adamw_many(ws, gs, ms, vs, *, name):
    n = len(ws)

    def body(*refs):
        for k in range(n):
            d, nm, nv = _adamw_math(refs[k][...], refs[n + k][...], refs[2 * n + k][...], refs[3 * n + k][...])
            refs[4 * n + k][...] = d
            refs[5 * n + k][...] = nm
            refs[6 * n + k][...] = nv

    vm = pl.BlockSpec(memory_space=pltpu.VMEM)
    sh = [jax.ShapeDtypeStruct(w.shape, F32) for w in ws]
    outs = pl.pallas_call(body, name=name, in_specs=[vm] * (4 * n), out_specs=[vm] * (3 * n), out_shape=sh * 3,
                          )(*ws, *gs, *ms, *vs)
    return outs[:n], outs[n:2 * n], outs[2 * n:]


def adamw(w, g, m, v, *, name, tr=256):
    R, C = w.shape
    tr = _pick(R, tr, 8)

    def body(w_ref, g_ref, m_ref, v_ref, d_ref, nm_ref, nv_ref):
        d_ref[...], nm_ref[...], nv_ref[...] = _adamw_math(w_ref[...], g_ref[...], m_ref[...], v_ref[...])

    blk = pl.BlockSpec((tr, C), lambda i: (i, 0))
    sh = jax.ShapeDtypeStruct((R, C), F32)
    return pl.pallas_call(
        body, name=name, grid=(R // tr,), in_specs=[blk, blk, blk, blk], out_specs=[blk, blk, blk],
        out_shape=[sh, sh, sh], compiler_params=pltpu.CompilerParams(dimension_semantics=("parallel",)),
    )(w, g, m, v)


def adamw_slots(w, slots, m, v, *, name, tr=256):
    R, C = w.shape
    tr = _pick(R, tr, 16)

    def body(w_ref, s_ref, m_ref, v_ref, g_ref, d_ref, nm_ref, nv_ref):
        g = s_ref[0].astype(F32)
        for k in range(1, NDEV):
            g = g + s_ref[k].astype(F32)
        g_ref[...] = g
        d_ref[...], nm_ref[...], nv_ref[...] = _adamw_math(w_ref[...], g, m_ref[...], v_ref[...])

    blk = pl.BlockSpec((tr, C), lambda i: (i, 0))
    sh = jax.ShapeDtypeStruct((R, C), F32)
    return pl.pallas_call(
        body, name=name, grid=(R // tr,), in_specs=[blk, pl.BlockSpec((NDEV, tr, C), lambda i: (0, i, 0)), blk, blk],
        out_specs=[blk, blk, blk, blk], out_shape=[sh, sh, sh, sh],
        compiler_params=pltpu.CompilerParams(dimension_semantics=("parallel",)),
    )(w, slots, m, v)


def _padc(a, n=D):
    return jnp.pad(a, ((0, 0), (0, n - a.shape[1])))


def kernel(x, c, ctx, c_ctx, w_ada, b_ada, norm1_g, w_in, q_norm_g, kv_norm_g, w_uq, w_ukv, conv_w, conv_b, w_attn_out, w_conv_out, w_o, norm2_g, w_up, ffn_conv_w, ffn_conv_b, w_down, final_g, loss_target, m_c_ctx, m_w_ada, m_b_ada, m_norm1_g, m_w_in, m_q_norm_g, m_kv_norm_g, m_w_uq, m_w_ukv, m_conv_w, m_conv_b, m_w_attn_out, m_w_conv_out, m_w_o, m_norm2_g, m_w_up, m_ffn_conv_w, m_ffn_conv_b, m_w_down, m_final_g, v_c_ctx, v_w_ada, v_b_ada, v_norm1_g, v_w_in, v_q_norm_g, v_kv_norm_g, v_w_uq, v_w_ukv, v_conv_w, v_conv_b, v_w_attn_out, v_w_conv_out, v_w_o, v_norm2_g, v_w_up, v_ffn_conv_w, v_ffn_conv_b, v_w_down, v_final_g):
    me = 4 * lax.axis_index("x") + 2 * lax.axis_index("y") + lax.axis_index("c")
    W = dict(c_ctx=c_ctx, w_ada=w_ada, b_ada=b_ada, norm1_g=norm1_g, w_in=w_in, q_norm_g=q_norm_g, kv_norm_g=kv_norm_g,
             w_uq=w_uq, w_ukv=w_ukv, conv_w=conv_w, conv_b=conv_b, w_attn_out=w_attn_out, w_conv_out=w_conv_out, w_o=w_o,
             norm2_g=norm2_g, w_up=w_up, ffn_conv_w=ffn_conv_w, ffn_conv_b=ffn_conv_b, w_down=w_down, final_g=final_g)
    M = dict(c_ctx=m_c_ctx, w_ada=m_w_ada, b_ada=m_b_ada, norm1_g=m_norm1_g, w_in=m_w_in, q_norm_g=m_q_norm_g,
             kv_norm_g=m_kv_norm_g, w_uq=m_w_uq, w_ukv=m_w_ukv, conv_w=m_conv_w, conv_b=m_conv_b, w_attn_out=m_w_attn_out,
             w_conv_out=m_w_conv_out, w_o=m_w_o, norm2_g=m_norm2_g, w_up=m_w_up, ffn_conv_w=m_ffn_conv_w,
             ffn_conv_b=m_ffn_conv_b, w_down=m_w_down, final_g=m_final_g)
    V = dict(c_ctx=v_c_ctx, w_ada=v_w_ada, b_ada=v_b_ada, norm1_g=v_norm1_g, w_in=v_w_in, q_norm_g=v_q_norm_g,
             kv_norm_g=v_kv_norm_g, w_uq=v_w_uq, w_ukv=v_w_ukv, conv_w=v_conv_w, conv_b=v_conv_b, w_attn_out=v_w_attn_out,
             w_conv_out=v_w_conv_out, w_o=v_w_o, norm2_g=v_norm2_g, w_up=v_w_up, ffn_conv_w=v_ffn_conv_w,
             ffn_conv_b=v_ffn_conv_b, w_down=v_w_down, final_g=v_final_g)
    names = list(W)
    as2d = lambda a: a.reshape(1, -1) if a.ndim == 1 else a.reshape(a.shape[-2], a.shape[-1])
    W2 = {k: as2d(a) for k, a in W.items()}
    M2 = {k: as2d(a) for k, a in M.items()}
    V2 = {k: as2d(a) for k, a in V.items()}
    nsh = W2["w_ada"].shape[1]

    b_sh = lax.dynamic_slice(W2["b_ada"], (0, me * nsh), (1, nsh))
    s_all, m_all = ada_fwd(c, W2["c_ctx"], _padc(W2["ffn_conv_w"]), _padc(W2["conv_w"]), W2["w_ada"], b_sh, [],
                           name="ada_fwd")
    mod_lat = m_all[:, 0, :].reshape(1, 6 * D)
    mod_ctx = m_all[:, 1, :].reshape(1, 6 * D)
    ffn_w_full = s_all[:, 2:5, :2 * DFF // NDEV].transpose(1, 0, 2).reshape(3, 2 * DFF)
    conv_w_full = s_all[:, 5:8, :CONV // NDEV].transpose(1, 0, 2).reshape(3, CONV)

    stage_w = {"in": ["w_in"], "mid": ["w_uq", "w_ukv", "w_attn_out", "w_conv_out", "w_o"], "ffn": ["w_up", "w_down"]}
    ag, tok = {}, m_all
    for st, nms in stage_w.items():
        ag[st] = exchange_start([W2[nm].astype(BF) for nm in nms], per_peer=False, dep=tok, name="ag_start_" + st)
        tok = ag[st]["token"]

    def get_w(stage, after):
        g = dict(zip(stage_w[stage], exchange_wait(ag[stage], after, name="ag_wait_" + stage)))
        if stage == "in":
            return build_win(g["w_in"], name="build_win")
        if stage == "mid":
            wq2, wkv2 = build_wq_wkv(g["w_uq"], g["w_ukv"], name="build_wq_wkv")
            return (wq2, wkv2, unshard_cols(g["w_attn_out"], name="unshard_w_attn_out"),
                    unshard_cols(g["w_conv_out"], name="unshard_w_conv_out"), g["w_o"].reshape(D, D))
        return unshard_cols(g["w_up"], name="unshard_w_up"), g["w_down"].reshape(DFF, D)

    stage_g = {"ffn": ["w_up", "w_down"], "mid": ["w_attn_out", "w_conv_out", "w_o"], "qkv": ["w_uq", "w_ukv"],
               "in": ["w_in"]}
    rs = {}

    def put_g(stage, g):
        if stage == "in":
            parts = [shard_win_grad(g["dwin"], g["dwin_c"], name="shard_win_grad")]
        elif stage == "mid":
            parts = [shard_cols(g["dwao"], name="shard_w_attn_out"), shard_cols(g["dwco"], name="shard_w_conv_out"),
                     g["dwo"].reshape(NDEV, D // NDEV, D)]
        elif stage == "qkv":
            parts = list(shard_wq_wkv_grad(g["dwq2"], g["dwkv2"], name="shard_wq_wkv_grad"))
        else:
            parts = [shard_cols(g["dwup"], name="shard_w_up"), g["dwdn"].reshape(NDEV, DFF // NDEV, D)]
        rs[stage] = exchange_start(parts, per_peer=True, name="rs_start_" + stage)
        return rs[stage]["token"]

    r = _local_step(x[0], ctx[0], loss_target[0], mod_lat, mod_ctx, W2["norm1_g"], W2["q_norm_g"], W2["kv_norm_g"],
                    W2["norm2_g"], W2["final_g"], conv_w_full, W2["conv_b"], ffn_w_full, W2["ffn_conv_b"], get_w, put_g,
                    ag["ffn"]["token"])

    G, DL, NM, NV = {}, {}, {}, {}

    def finish(stage, after):
        for nm, sl in zip(stage_g[stage], exchange_wait(rs[stage], after, name="rs_wait_" + stage)):
            G[nm], DL[nm], NM[nm], NV[nm] = adamw_slots(W2[nm], sl, M2[nm], V2[nm], name="adamw_" + nm)
            after = DL[nm]
        return after

    after = r["dx"]
    for st in ("ffn", "mid", "qkv"):
        after = finish(st, after)

    a_buf, ssum = sync_small(r, after, name="sync_small")
    loss = ssum[P_LOSS, 0]
    G["norm1_g"] = ssum[P_N1:P_N1 + 1]
    G["q_norm_g"] = ssum[P_QG:P_QG + 1, :QL]
    G["kv_norm_g"] = ssum[P_KVG:P_KVG + 1, :KVL]
    G["conv_b"] = ssum[P_CB:P_CB + 1, :CONV]
    G["norm2_g"] = ssum[P_N2:P_N2 + 1]
    G["ffn_conv_b"] = ssum[P_FB:P_FB + 2 * FROWS].reshape(1, 2, FROWS * D)[:, :, :DFF].reshape(1, 2 * DFF)
    G["final_g"] = ssum[P_FG:P_FG + 1]
    G["conv_w"] = lax.dynamic_slice(ssum[P_CW:P_CW + 3, :CONV], (0, me * (CONV // NDEV)), (3, CONV // NDEV))
    fw_full = ssum[P_FW:P_FW + 6 * FROWS].reshape(3, 2, FROWS * D)[:, :, :DFF].reshape(3, 2 * DFF)
    G["ffn_conv_w"] = lax.dynamic_slice(fw_full, (0, me * (2 * DFF // NDEV)), (3, 2 * DFF // NDEV))
    G["b_ada"] = (ssum[P_DML:P_DML + 6] + ssum[P_DMC:P_DMC + 6]).reshape(1, 6 * D)

    dml = lax.dynamic_slice(a_buf[:, P_DML:P_DML + 6, :].reshape(NDEV, 6 * D), (0, me * nsh), (NDEV, nsh))
    dmc = lax.dynamic_slice(ssum[P_DMC:P_DMC + 6].reshape(1, 6 * D), (0, me * nsh), (1, nsh))
    G["w_ada"], gcc = ada_bwd(s_all, dml, dmc, W2["w_ada"], W2["c_ctx"], name="ada_bwd")
    G["c_ctx"] = gcc[0:1]

    DL["w_ada"], NM["w_ada"], NV["w_ada"] = adamw(W2["w_ada"], G["w_ada"], M2["w_ada"], V2["w_ada"], name="adamw_w_ada")
    small = ["c_ctx", "b_ada", "norm1_g", "q_norm_g", "kv_norm_g", "conv_b", "norm2_g", "ffn_conv_b", "final_g", "conv_w",
             "ffn_conv_w"]
    ds, nms, nvs = adamw_many([W2[k] for k in small], [G[k] for k in small], [M2[k] for k in small],
                              [V2[k] for k in small], name="adamw_small")
    for k, nm in enumerate(small):
        DL[nm], NM[nm], NV[nm] = ds[k], nms[k], nvs[k]
    finish("in", ds[0])

    outs = [loss, r["dx"][None]]
    for grp in (G, DL, NM, NV):
        outs += [grp[nm].reshape(W[nm].shape) for nm in names]
    return tuple(outs)
```

```python
import functools
import numpy as np
import jax
import jax.numpy as jnp
from jax import lax
from jax.experimental import pallas as pl
from jax.experimental.pallas import tpu as pltpu

F32 = jnp.float32
BF = jnp.bfloat16
MESH = pl.DeviceIdType.MESH

D = 1024
T = 2048
TC = 256
TKV = T + TC
GRID_W = 64
NH = 8
DN = 64
DR = 32
DV = 64
QL = 384
KVL = 256
CONV = 512
DFF = 2816
EPS = 1e-6
ROPE_THETA = 10000.0
SCALE = (DN + DR) ** -0.5
NDEV = 8
HP = 128

O_GA, O_GC, O_KV, O_Q, O_CV = 0, 1024, 2048, 2560, 3072
NIN = 4608
CVB = 256
N_IN = 4256
SH_IN = N_IN // NDEV

LR, B1, B2, AEPS, WD, STEP = 0.001, 0.9, 0.999, 1e-08, 0.01, 10


def _pick(n, target, mult=128):
    best = None
    for d in range(mult, min(n, target) + 1, mult):
        if n % d == 0:
            best = d
    return best if best is not None else n


def _swap_start(g):
    return 8 * (g ^ 1)


def mm(a, b, *, ta=False, tb=False, out_dtype=F32, name, tm=1024, tn=1024, tk=2048, M=None, N=None, K=None,
       a_off=(0, 0), b_off=(0, 0), a_stack=False, b_stack=False, o_stack=False, dep=None):
    def dims(arr, stack):
        return (arr.shape[1], 2 * arr.shape[2]) if stack else arr.shape

    ar, ac = dims(a, a_stack)
    br, bc = dims(b, b_stack)
    M = M or ((ac if ta else ar) - a_off[1 if ta else 0])
    K = K or ((ar if ta else ac) - a_off[0 if ta else 1])
    N = N or ((br if tb else bc) - b_off[0 if tb else 1])
    tm = _pick(M, tm, 128 if ta else 16)
    tn = _pick(N // 2 if (o_stack or (b_stack and not tb)) else N, tn, 128)
    tk = _pick(K // 2 if ((a_stack and not ta) or (b_stack and tb)) else K, tk, 128)
    nk = K // tk
    ca = 0 if ta else 1
    cb = 1 if tb else 0

    def body(a_ref, b_ref, *rest):
        o_ref, acc = rest[-2:]
        k = pl.program_id(2)
        part = lax.dot_general(a_ref[...].astype(BF), b_ref[...].astype(BF),
                               (((ca,), (cb,)), ((), ())), preferred_element_type=F32)
        if nk == 1:
            o_ref[...] = part.astype(o_ref.dtype)
        else:
            @pl.when(k == 0)
            def _():
                acc[...] = part

            @pl.when(k > 0)
            def _():
                acc[...] += part

            @pl.when(k == nk - 1)
            def _():
                o_ref[...] = acc[...].astype(o_ref.dtype)

    def spec(blk, rc, off, stack, ncols):
        assert off[0] % blk[0] == 0 and off[1] % blk[1] == 0, (name, blk, off)
        ro, co = off[0] // blk[0], off[1] // blk[1]
        if not stack:
            return pl.BlockSpec(blk, lambda i, j, k: (rc(i, j, k)[0] + ro, rc(i, j, k)[1] + co))
        nhb = ncols // 2 // blk[1]
        return pl.BlockSpec((None,) + blk,
                            lambda i, j, k: ((rc(i, j, k)[1] + co) // nhb, rc(i, j, k)[0] + ro, (rc(i, j, k)[1] + co) % nhb))

    a_spec = spec((tk, tm), lambda i, j, k: (k, i), a_off, a_stack, ac) if ta else \
        spec((tm, tk), lambda i, j, k: (i, k), a_off, a_stack, ac)
    b_spec = spec((tn, tk), lambda i, j, k: (j, k), b_off, b_stack, bc) if tb else \
        spec((tk, tn), lambda i, j, k: (k, j), b_off, b_stack, bc)
    o_spec = spec((tm, tn), lambda i, j, k: (i, j), (0, 0), o_stack, N)
    o_shape = (2, M, N // 2) if o_stack else (M, N)
    deps = [] if dep is None else [dep]
    return pl.pallas_call(
        body, name=name, grid=(M // tm, N // tn, nk),
        in_specs=[a_spec, b_spec] + [pl.BlockSpec(memory_space=pl.ANY)] * len(deps),
        out_specs=o_spec, out_shape=jax.ShapeDtypeStruct(o_shape, out_dtype),
        scratch_shapes=[pltpu.VMEM((tm, tn) if nk > 1 else (8, 128), F32)],
        compiler_params=pltpu.CompilerParams(dimension_semantics=("parallel", "parallel", "arbitrary")),
    )(a, b, *deps)


def _row(width):
    return pl.BlockSpec((1, width), lambda *_: (0, 0))


NLAT = T // TC


def normmod_cat(ctx, x, g, csc, csh, sc, sh, dep, *, name, tm=256):
    assert tm == TC

    def body(c_ref, x_ref, g_ref, csc_ref, csh_ref, sc_ref, sh_ref, dep_ref, h_ref):
        last = pl.program_id(0) == NLAT
        xv = jnp.where(last, c_ref[...], x_ref[...])
        scv = jnp.where(last, csc_ref[...], sc_ref[...])
        shv = jnp.where(last, csh_ref[...], sh_ref[...])
        r = lax.rsqrt(jnp.mean(xv * xv, axis=-1, keepdims=True) + EPS)
        h_ref[...] = ((xv * r * g_ref[...]) * (1.0 + scv) + shv).astype(BF)

    return pl.pallas_call(
        body, name=name, grid=(TKV // tm,),
        in_specs=[pl.BlockSpec((tm, D), lambda i: (0, 0)), pl.BlockSpec((tm, D), lambda i: (jnp.minimum(i, NLAT - 1), 0)),
                  _row(D), _row(D), _row(D), _row(D), _row(D), pl.BlockSpec(memory_space=pl.ANY)],
        out_specs=pl.BlockSpec((tm, D), lambda i: (i, 0)), out_shape=jax.ShapeDtypeStruct((TKV, D), BF),
        compiler_params=pltpu.CompilerParams(dimension_semantics=("parallel",)),
    )(ctx, x, g, csc, csh, sc, sh, dep)


def resid_normmod(x, a, gate, g, sc, sh, *, name, tm=256):
    R = x.shape[0]

    def body(x_ref, a_ref, gate_ref, g_ref, sc_ref, sh_ref, x1_ref, h_ref):
        xv = x_ref[...] + gate_ref[...] * a_ref[...]
        x1_ref[...] = xv
        r = lax.rsqrt(jnp.mean(xv * xv, axis=-1, keepdims=True) + EPS)
        h_ref[...] = ((xv * r * g_ref[...]) * (1.0 + sc_ref[...]) + sh_ref[...]).astype(BF)

    blk = pl.BlockSpec((tm, D), lambda i: (i, 0))
    return pl.pallas_call(
        body, name=name, grid=(R // tm,), in_specs=[blk, blk, _row(D), _row(D), _row(D), _row(D)],
        out_specs=[blk, blk],
        out_shape=[jax.ShapeDtypeStruct((R, D), F32), jax.ShapeDtypeStruct((R, D), BF)],
        compiler_params=pltpu.CompilerParams(dimension_semantics=("parallel",)),
    )(x, a, gate, g, sc, sh)


def kvprep(pc, p, kvg, wkv2, ck, sk, *, name, tm=256):
    assert tm == TC
    nb = TKV // tm
    kvcol = O_KV // 512

    def body(pc_ref, p_ref, g_ref, w_ref, ck_ref, sk_ref, k_ref, v_ref, ckv_ref):
        i = pl.program_id(0)
        t = jnp.where(i == NLAT, pc_ref[...], p_ref[...])
        pk = t[:, :KVL]
        r = lax.rsqrt(jnp.mean(pk * pk, axis=-1, keepdims=True) + EPS)
        ckv = (pk * r * g_ref[...]).astype(BF)
        ckv_ref[...] = ckv
        kv2 = jnp.dot(ckv, w_ref[...], preferred_element_type=F32)
        krr = t[:, KVL:KVL + HP] * ck_ref[...] + t[:, KVL + HP:KVL + 2 * HP] * sk_ref[...]
        k_ref[...] = (kv2[:, :NH * HP] + jnp.concatenate([krr] * NH, axis=1)).astype(BF)
        v_ref[...] = kv2[:, NH * HP:].astype(BF)

    return pl.pallas_call(
        body, name=name, grid=(nb,),
        in_specs=[pl.BlockSpec((tm, 512), lambda i: (0, 0)),
                  pl.BlockSpec((tm, 512), lambda i: (jnp.minimum(i, NLAT - 1), kvcol)),
                  _row(KVL), pl.BlockSpec((KVL, NH * HP + NH * DV), lambda i: (0, 0)),
                  pl.BlockSpec((tm, HP), lambda i: (i, 0)), pl.BlockSpec((tm, HP), lambda i: (i, 0))],
        out_specs=[pl.BlockSpec((tm, NH * HP), lambda i: (i, 0)), pl.BlockSpec((tm, NH * DV), lambda i: (i, 0)),
                   pl.BlockSpec((tm, KVL), lambda i: (i, 0))],
        out_shape=[jax.ShapeDtypeStruct((TKV, NH * HP), BF), jax.ShapeDtypeStruct((TKV, NH * DV), BF),
                   jax.ShapeDtypeStruct((TKV, KVL), BF)],
        compiler_params=pltpu.CompilerParams(dimension_semantics=("parallel",)),
    )(pc, p, kvg, wkv2, ck, sk)


def qprep(p, qg, wq2, cq_t, sq_t, *, name, tm=256):
    qcol = O_Q // 512

    def body(p_ref, g_ref, w_ref, c_ref, s_ref, q_ref, cq_ref):
        pq = p_ref[...]
        r = lax.rsqrt(jnp.sum(pq * pq, axis=-1, keepdims=True) * (1.0 / QL) + EPS)
        cq = (pq * r * g_ref[...]).astype(BF)
        cq_ref[...] = cq
        q2 = jnp.dot(cq, w_ref[...], preferred_element_type=F32)
        cc = jnp.concatenate([c_ref[...]] * NH, axis=1)
        ss = jnp.concatenate([s_ref[...]] * NH, axis=1)
        q_ref[...] = (q2[:, :NH * HP] * cc + q2[:, NH * HP:] * ss).astype(BF)

    return pl.pallas_call(
        body, name=name, grid=(T // tm,),
        in_specs=[pl.BlockSpec((tm, 512), lambda i: (i, qcol)), _row(512),
                  pl.BlockSpec((512, 2 * NH * HP), lambda i: (0, 0)),
                  pl.BlockSpec((tm, HP), lambda i: (i, 0)), pl.BlockSpec((tm, HP), lambda i: (i, 0))],
        out_specs=[pl.BlockSpec((tm, NH * HP), lambda i: (i, 0)), pl.BlockSpec((tm, 512), lambda i: (i, 0))],
        out_shape=[jax.ShapeDtypeStruct((T, NH * HP), BF), jax.ShapeDtypeStruct((T, 512), BF)],
        compiler_params=pltpu.CompilerParams(dimension_semantics=("parallel",)),
    )(p, qg, wq2, cq_t, sq_t)


def _head_mask(h):
    lanes = lax.broadcasted_iota(jnp.int32, (1, 2 * DV), 1)
    return (lanes // DV) == (h % 2)


KC = 256
NKC = TKV // KC
LOG2E = 1.4426950408889634


def _scores_pass(q, k_ref, s_scr):
    m = None
    for c in range(NKC):
        s = lax.dot_general(q, k_ref[c * KC:(c + 1) * KC, :], (((1,), (1,)), ((), ())),
                            preferred_element_type=F32) * (SCALE * LOG2E)
        s_scr[:, c * KC:(c + 1) * KC] = s
        mc = jnp.max(s, axis=-1, keepdims=True)
        m = mc if m is None else jnp.maximum(m, mc)
    return m


def attn_fwd(q, k, v, *, name, tq=256):
    def body(q_ref, k_ref, v_ref, o_ref, s_scr):
        h = pl.program_id(1)
        m = _scores_pass(q_ref[...], k_ref, s_scr)
        l = jnp.zeros((tq, 1), F32)
        acc = jnp.zeros((tq, 2 * DV), F32)
        for c in range(NKC):
            e = jnp.exp2(s_scr[:, c * KC:(c + 1) * KC] - m)
            l = l + jnp.sum(e, axis=-1, keepdims=True)
            acc = acc + jnp.dot(e.astype(BF), v_ref[c * KC:(c + 1) * KC, :], preferred_element_type=F32)
        o2 = jnp.where(_head_mask(h), acc * (1.0 / l), 0.0).astype(BF)

        @pl.when(h % 2 == 0)
        def _():
            o_ref[...] = o2

        @pl.when(h % 2 == 1)
        def _():
            o_ref[...] = o_ref[...] + o2

    return pl.pallas_call(
        body, name=name, grid=(T // tq, NH),
        in_specs=[pl.BlockSpec((tq, HP), lambda i, h: (i, h)), pl.BlockSpec((TKV, HP), lambda i, h: (0, h)),
                  pl.BlockSpec((TKV, 2 * DV), lambda i, h: (0, h // 2))],
        out_specs=pl.BlockSpec((tq, 2 * DV), lambda i, h: (i, h // 2)),
        out_shape=jax.ShapeDtypeStruct((T, NH * DV), BF),
        scratch_shapes=[pltpu.VMEM((tq, TKV), F32)],
        compiler_params=pltpu.CompilerParams(dimension_semantics=("parallel", "arbitrary")),
    )(q, k, v)


def _shift_dn(x):
    n = x.shape[0]
    rows = lax.broadcasted_iota(jnp.int32, (n, 1), 0)
    return jnp.where(rows == 0, 0.0, pltpu.roll(x, 1, axis=0))


def _shift_up(x):
    n = x.shape[0]
    rows = lax.broadcasted_iota(jnp.int32, (n, 1), 0)
    return jnp.where(rows == n - 1, 0.0, pltpu.roll(x, n - 1, axis=0))


def _conv(x, w_ref, b_ref):
    return b_ref[...] + _shift_dn(x) * w_ref[0:1, :] + x * w_ref[1:2, :] + _shift_up(x) * w_ref[2:3, :]


def _conv_t(dy, w_ref):
    return _shift_up(dy) * w_ref[0:1, :] + dy * w_ref[1:2, :] + _shift_dn(dy) * w_ref[2:3, :]


def _conv_wgrad(dw_ref, dy, x):
    dw_ref[0:1, :] = jnp.sum(dy * _shift_dn(x), axis=0, keepdims=True)
    dw_ref[1:2, :] = jnp.sum(dy * x, axis=0, keepdims=True)
    dw_ref[2:3, :] = jnp.sum(dy * _shift_up(x), axis=0, keepdims=True)


def convz(p, cw, cb, *, name):
    o0 = O_CV // (3 * CVB)

    def body(p_ref, w_ref, bias_ref, z_ref):
        xv, bv, cv = p_ref[:, 0:CVB], p_ref[:, CVB:2 * CVB], p_ref[:, 2 * CVB:3 * CVB]
        z_ref[...] = (bv * _conv(cv * xv, w_ref, bias_ref)).astype(BF)

    return pl.pallas_call(
        body, name=name, grid=(CONV // CVB,),
        in_specs=[pl.BlockSpec((T, 3 * CVB), lambda j: (0, o0 + j)), pl.BlockSpec((3, CVB), lambda j: (0, j)),
                  pl.BlockSpec((1, CVB), lambda j: (0, j))],
        out_specs=pl.BlockSpec((T, CVB), lambda j: (0, j)),
        out_shape=jax.ShapeDtypeStruct((T, CONV), BF),
        compiler_params=pltpu.CompilerParams(dimension_semantics=("parallel",)),
    )(p, cw, cb)


def gate_merge(p, ya, yc, *, name, tm=256):
    def body(ga_ref, gc_ref, ya_ref, yc_ref, o_ref):
        o_ref[...] = (jax.nn.sigmoid(ga_ref[...]) * ya_ref[...] + jax.nn.sigmoid(gc_ref[...]) * yc_ref[...]).astype(BF)

    blk = pl.BlockSpec((tm, D), lambda i: (i, 0))
    return pl.pallas_call(
        body, name=name, grid=(T // tm,),
        in_specs=[pl.BlockSpec((tm, D), lambda i: (i, O_GA // D)), pl.BlockSpec((tm, D), lambda i: (i, O_GC // D)), blk, blk],
        out_specs=blk, out_shape=jax.ShapeDtypeStruct((T, D), BF),
        compiler_params=pltpu.CompilerParams(dimension_semantics=("parallel",)),
    )(p, p, ya, yc)


def ffn_act(u0, cw, cb, *, name, tc=256):
    nb = DFF // tc

    def body(u_ref, wg_ref, wv_ref, bg_ref, bv_ref, f_ref):
        ug = _conv(u_ref[0], wg_ref, bg_ref)
        uv = _conv(u_ref[1], wv_ref, bv_ref)
        f_ref[...] = (ug * jax.nn.sigmoid(ug) * uv).astype(BF)

    return pl.pallas_call(
        body, name=name, grid=(nb,),
        in_specs=[pl.BlockSpec((2, T, tc), lambda j: (0, 0, j)),
                  pl.BlockSpec((3, tc), lambda j: (0, j)), pl.BlockSpec((3, tc), lambda j: (0, nb + j)),
                  pl.BlockSpec((1, tc), lambda j: (0, j)), pl.BlockSpec((1, tc), lambda j: (0, nb + j))],
        out_specs=pl.BlockSpec((T, tc), lambda j: (0, j)),
        out_shape=jax.ShapeDtypeStruct((T, DFF), BF),
        compiler_params=pltpu.CompilerParams(dimension_semantics=("parallel",)),
    )(u0, cw, cw, cb, cb)


def final_loss(x1, d, g2, fg, tgt, *, name, tm=256):
    def body(x1_ref, d_ref, g2_ref, fg_ref, t_ref, dx_ref, dd_ref, dfg_ref, loss_ref):
        i = pl.program_id(0)
        xv = x1_ref[...] + g2_ref[...] * d_ref[...]
        r = lax.rsqrt(jnp.mean(xv * xv, axis=-1, keepdims=True) + EPS)
        xh = xv * r
        diff = xh * fg_ref[...] - t_ref[...]
        part = 0.5 * jnp.sum(jnp.mean(diff * diff, axis=-1, keepdims=True), axis=0, keepdims=True)
        dy = diff * (1.0 / D)
        a = dy * fg_ref[...]
        dx = r * (a - xh * jnp.mean(a * xh, axis=-1, keepdims=True))
        dx_ref[...] = dx
        dd_ref[...] = (dx * g2_ref[...]).astype(BF)
        dfg = jnp.sum(dy * xh, axis=0, keepdims=True)

        @pl.when(i == 0)
        def _():
            dfg_ref[...] = dfg
            loss_ref[...] = jnp.broadcast_to(part, (1, 128))

        @pl.when(i > 0)
        def _():
            dfg_ref[...] += dfg
            loss_ref[...] += jnp.broadcast_to(part, (1, 128))

    blk = pl.BlockSpec((tm, D), lambda i: (i, 0))
    return pl.pallas_call(
        body, name=name, grid=(T // tm,), in_specs=[blk, blk, _row(D), _row(D), blk],
        out_specs=[blk, blk, _row(D), _row(128)],
        out_shape=[jax.ShapeDtypeStruct((T, D), F32), jax.ShapeDtypeStruct((T, D), BF),
                   jax.ShapeDtypeStruct((1, D), F32), jax.ShapeDtypeStruct((1, 128), F32)],
        compiler_params=pltpu.CompilerParams(dimension_semantics=("arbitrary",)),
    )(x1, d, g2, fg, tgt)


def normmod_bwd(x, dh, g, sc, dres, gsrc, gate, *, name, tm=256):
    R = x.shape[0]
    has_res = dres is not None

    def body(*refs):
        if has_res:
            x_ref, dh_ref, g_ref, sc_ref, dres_ref, gsrc_ref, gate_ref, dx_ref, dxg_ref, st_ref = refs
        else:
            x_ref, dh_ref, g_ref, sc_ref, st_ref = refs
        i = pl.program_id(0)
        xv = x_ref[...]
        r = lax.rsqrt(jnp.mean(xv * xv, axis=-1, keepdims=True) + EPS)
        xh = xv * r
        dhv = dh_ref[...]
        n = xh * g_ref[...]
        dn = dhv * (1.0 + sc_ref[...])
        a = dn * g_ref[...]
        rows = [jnp.sum(dhv, axis=0, keepdims=True), jnp.sum(dhv * n, axis=0, keepdims=True),
                jnp.sum(dn * xh, axis=0, keepdims=True)]
        if has_res:
            dr = dres_ref[...]
            dx = dr + r * (a - xh * jnp.mean(a * xh, axis=-1, keepdims=True))
            dx_ref[...] = dx
            dxg_ref[...] = (dx * gate_ref[...]).astype(BF)
            rows.append(jnp.sum(dr * gsrc_ref[...], axis=0, keepdims=True))
        else:
            rows.append(jnp.zeros((1, D), F32))

        @pl.when(i == 0)
        def _():
            for k, row in enumerate(rows):
                st_ref[k:k + 1, :] = row

        @pl.when(i > 0)
        def _():
            for k, row in enumerate(rows):
                st_ref[k:k + 1, :] += row

    blk = pl.BlockSpec((tm, D), lambda i: (i, 0))
    st_spec = pl.BlockSpec((4, D), lambda i: (0, 0))
    st_shape = jax.ShapeDtypeStruct((4, D), F32)
    cp = pltpu.CompilerParams(dimension_semantics=("arbitrary",))
    if has_res:
        return pl.pallas_call(
            body, name=name, grid=(R // tm,), in_specs=[blk, blk, _row(D), _row(D), blk, blk, _row(D)],
            out_specs=[blk, blk, st_spec],
            out_shape=[jax.ShapeDtypeStruct((R, D), F32), jax.ShapeDtypeStruct((R, D), BF), st_shape],
            compiler_params=cp,
        )(x, dh, g, sc, dres, gsrc, gate)
    return pl.pallas_call(
        body, name=name, grid=(R // tm,), in_specs=[blk, blk, _row(D), _row(D)],
        out_specs=st_spec, out_shape=st_shape, compiler_params=cp,
    )(x, dh, g, sc)


def ffn_act_bwd(u0, df, cw, cb, *, name, tc=256):
    nb = DFF // tc

    def body(u_ref, df_ref, wg_ref, wv_ref, bg_ref, bv_ref, du_ref, dw_ref, db_ref):
        xg, xv = u_ref[0], u_ref[1]
        ug = _conv(xg, wg_ref, bg_ref)
        uv = _conv(xv, wv_ref, bv_ref)
        sig = jax.nn.sigmoid(ug)
        dfv = df_ref[...]
        dug = dfv * uv * (sig * (1.0 + ug * (1.0 - sig)))
        duv = dfv * (ug * sig)
        du_ref[0] = _conv_t(dug, wg_ref).astype(BF)
        du_ref[1] = _conv_t(duv, wv_ref).astype(BF)
        _conv_wgrad(dw_ref.at[0], dug, xg)
        _conv_wgrad(dw_ref.at[1], duv, xv)
        db_ref[0] = jnp.sum(dug, axis=0, keepdims=True)
        db_ref[1] = jnp.sum(duv, axis=0, keepdims=True)

    lo = lambda r: pl.BlockSpec((r, tc), lambda j: (0, j))
    hi = lambda r: pl.BlockSpec((r, tc), lambda j: (0, nb + j))
    st = lambda r: pl.BlockSpec((2, r, tc), lambda j: (0, 0, j))
    return pl.pallas_call(
        body, name=name, grid=(nb,),
        in_specs=[st(T), lo(T), lo(3), hi(3), lo(1), hi(1)],
        out_specs=[st(T), st(3), st(1)],
        out_shape=[jax.ShapeDtypeStruct((2, T, DFF), BF), jax.ShapeDtypeStruct((2, 3, DFF), F32),
                   jax.ShapeDtypeStruct((2, 1, DFF), F32)],
        compiler_params=pltpu.CompilerParams(dimension_semantics=("parallel",)),
    )(u0, df, cw, cw, cb, cb)


def gate_merge_bwd(p, ya, yc, dm, *, name, tm=256):
    def body(ga_ref, gc_ref, ya_ref, yc_ref, dm_ref, dya_ref, dyc_ref, dp_ref):
        sa, sc_ = jax.nn.sigmoid(ga_ref[...]), jax.nn.sigmoid(gc_ref[...])
        dmv = dm_ref[...]
        dya_ref[...] = (dmv * sa).astype(BF)
        dyc_ref[...] = (dmv * sc_).astype(BF)
        dp_ref[:, 0:D] = (dmv * ya_ref[...] * (sa * (1.0 - sa))).astype(BF)
        dp_ref[:, D:2 * D] = (dmv * yc_ref[...] * (sc_ * (1.0 - sc_))).astype(BF)

    blk = pl.BlockSpec((tm, D), lambda i: (i, 0))
    sh = jax.ShapeDtypeStruct((T, D), BF)
    return pl.pallas_call(
        body, name=name, grid=(T // tm,),
        in_specs=[pl.BlockSpec((tm, D), lambda i: (i, O_GA // D)), pl.BlockSpec((tm, D), lambda i: (i, O_GC // D)), blk, blk, blk],
        out_specs=[blk, blk, pl.BlockSpec((tm, 2 * D), lambda i: (i, 0))],
        out_shape=[sh, sh, jax.ShapeDtypeStruct((T, NIN), BF)],
        compiler_params=pltpu.CompilerParams(dimension_semantics=("parallel",)),
    )(p, p, ya, yc, dm)


def convz_bwd(p, dz, cw, cb, dp, *, name):
    o0 = O_CV // (3 * CVB)

    def body(p_ref, dz_ref, w_ref, bias_ref, dp_in, dp_ref, dw_ref, dbias_ref):
        xv, bv, cv = p_ref[:, 0:CVB], p_ref[:, CVB:2 * CVB], p_ref[:, 2 * CVB:3 * CVB]
        ci = cv * xv
        dwc = _conv(ci, w_ref, bias_ref)
        dzv = dz_ref[...]
        ddw = dzv * bv
        dci = _conv_t(ddw, w_ref)
        dp_ref[:, 0:CVB] = (dci * cv).astype(BF)
        dp_ref[:, CVB:2 * CVB] = (dzv * dwc).astype(BF)
        dp_ref[:, 2 * CVB:3 * CVB] = (dci * xv).astype(BF)
        _conv_wgrad(dw_ref, ddw, ci)
        dbias_ref[...] = jnp.sum(ddw, axis=0, keepdims=True)

    own = lambda r: pl.BlockSpec((r, CVB), lambda j: (0, j))
    return pl.pallas_call(
        body, name=name, grid=(CONV // CVB,),
        in_specs=[pl.BlockSpec((T, 3 * CVB), lambda j: (0, o0 + j)), own(T), own(3), own(1),
                  pl.BlockSpec(memory_space=pl.ANY)],
        out_specs=[pl.BlockSpec((T, 3 * CVB), lambda j: (0, o0 + j)), own(3), own(1)],
        out_shape=[jax.ShapeDtypeStruct((T, NIN), BF), jax.ShapeDtypeStruct((3, CONV), F32),
                   jax.ShapeDtypeStruct((1, CONV), F32)],
        input_output_aliases={4: 0},
        compiler_params=pltpu.CompilerParams(dimension_semantics=("parallel",)),
    )(p, dz, cw, cb, dp)


def attn_bwd(q, k, v, do, *, name, tq=256):
    def body(q_ref, k_ref, v_ref, do_ref, dq_ref, dk_ref, dv_ref, s_scr, dp_scr):
        h, i = pl.program_id(0), pl.program_id(1)

        @pl.when(i == 0)
        def _():
            dk_ref[...] = jnp.zeros_like(dk_ref)

        @pl.when((i == 0) & (h % 2 == 0))
        def _():
            dv_ref[...] = jnp.zeros_like(dv_ref)

        qv = q_ref[...]
        dom = jnp.where(_head_mask(h), do_ref[...], jnp.zeros_like(do_ref[...]))
        m = _scores_pass(qv, k_ref, s_scr)
        l = jnp.zeros((tq, 1), F32)
        dsum = jnp.zeros((tq, 1), F32)
        for c in range(NKC):
            cols = slice(c * KC, (c + 1) * KC)
            e = jnp.exp2(s_scr[:, cols] - m)
            s_scr[:, cols] = e
            dp = lax.dot_general(dom, v_ref[cols, :], (((1,), (1,)), ((), ())), preferred_element_type=F32)
            dp_scr[:, cols] = dp
            l = l + jnp.sum(e, axis=-1, keepdims=True)
            dsum = dsum + jnp.sum(e * dp, axis=-1, keepdims=True)
        inv = 1.0 / l
        delta = dsum * inv
        dos = (dom.astype(F32) * inv).astype(BF)
        dq = jnp.zeros((tq, HP), F32)
        for c in range(NKC):
            cols = slice(c * KC, (c + 1) * KC)
            e = s_scr[:, cols]
            ds = (e * (dp_scr[:, cols] - delta) * (inv * SCALE)).astype(BF)
            dq = dq + jnp.dot(ds, k_ref[cols, :], preferred_element_type=F32)
            dk_ref[cols, :] += lax.dot_general(ds, qv, (((0,), (0,)), ((), ())), preferred_element_type=F32)
            dv_ref[cols, :] += lax.dot_general(e.astype(BF), dos, (((0,), (0,)), ((), ())), preferred_element_type=F32)
        dq_ref[...] = dq

    return pl.pallas_call(
        body, name=name, grid=(NH, T // tq),
        in_specs=[pl.BlockSpec((tq, HP), lambda h, i: (i, h)), pl.BlockSpec((TKV, HP), lambda h, i: (0, h)),
                  pl.BlockSpec((TKV, 2 * DV), lambda h, i: (0, h // 2)), pl.BlockSpec((tq, 2 * DV), lambda h, i: (i, h // 2))],
        out_specs=[pl.BlockSpec((tq, HP), lambda h, i: (i, h)), pl.BlockSpec((TKV, HP), lambda h, i: (0, h)),
                   pl.BlockSpec((TKV, 2 * DV), lambda h, i: (0, h // 2))],
        out_shape=[jax.ShapeDtypeStruct((T, NH * HP), F32), jax.ShapeDtypeStruct((TKV, NH * HP), F32),
                   jax.ShapeDtypeStruct((TKV, NH * DV), F32)],
        scratch_shapes=[pltpu.VMEM((tq, TKV), F32), pltpu.VMEM((tq, TKV), F32)],
        compiler_params=pltpu.CompilerParams(dimension_semantics=("arbitrary", "arbitrary")),
    )(q, k, v, do)


def qprep_bwd(p, dq, qg, wq2, cq_t, sq_t, dp, *, name, tm=256):
    qcol = O_Q // 512

    def body(p_ref, dq_ref, g_ref, w_ref, c_ref, s_ref, dp_in, dp_ref, dq2_ref, dg_ref):
        i = pl.program_id(0)
        dqv = dq_ref[...]
        cc = jnp.concatenate([c_ref[...]] * NH, axis=1)
        ss = jnp.concatenate([s_ref[...]] * NH, axis=1)
        dq2 = jnp.concatenate([dqv * cc, dqv * ss], axis=1).astype(BF)
        dq2_ref[...] = dq2
        dcq = lax.dot_general(dq2, w_ref[...], (((1,), (1,)), ((), ())), preferred_element_type=F32)
        pq = p_ref[...]
        r = lax.rsqrt(jnp.sum(pq * pq, axis=-1, keepdims=True) * (1.0 / QL) + EPS)
        xh = pq * r
        a = dcq * g_ref[...]
        dp_ref[...] = (r * (a - xh * (jnp.sum(a * xh, axis=-1, keepdims=True) * (1.0 / QL)))).astype(BF)
        dg = jnp.sum(dcq * xh, axis=0, keepdims=True)

        @pl.when(i == 0)
        def _():
            dg_ref[...] = dg

        @pl.when(i > 0)
        def _():
            dg_ref[...] += dg

    return pl.pallas_call(
        body, name=name, grid=(T // tm,),
        in_specs=[pl.BlockSpec((tm, 512), lambda i: (i, qcol)), pl.BlockSpec((tm, NH * HP), lambda i: (i, 0)), _row(512),
                  pl.BlockSpec((512, 2 * NH * HP), lambda i: (0, 0)),
                  pl.BlockSpec((tm, HP), lambda i: (i, 0)), pl.BlockSpec((tm, HP), lambda i: (i, 0)),
                  pl.BlockSpec(memory_space=pl.ANY)],
        out_specs=[pl.BlockSpec((tm, 512), lambda i: (i, qcol)), pl.BlockSpec((tm, 2 * NH * HP), lambda i: (i, 0)), _row(512)],
        out_shape=[jax.ShapeDtypeStruct((T, NIN), BF), jax.ShapeDtypeStruct((T, 2 * NH * HP), BF),
                   jax.ShapeDtypeStruct((1, 512), F32)],
        input_output_aliases={6: 0},
        compiler_params=pltpu.CompilerParams(dimension_semantics=("arbitrary",)),
    )(p, dq, qg, wq2, cq_t, sq_t, dp)


def kvprep_bwd(pc, p, dk, dv, kvg, wkv2, ck, sk, dp, *, name, tm=256):
    assert tm == TC
    nb = TKV // tm
    kvcol = O_KV // 512

    def body(pc_ref, p_ref, dk_ref, dv_ref, g_ref, w_ref, ck_ref, sk_ref, dp_in, dp_ref, dpc_ref, dkv2_ref, dg_ref):
        i = pl.program_id(0)
        t = jnp.where(i == NLAT, pc_ref[...], p_ref[...])
        pk = t[:, :KVL]
        r = lax.rsqrt(jnp.mean(pk * pk, axis=-1, keepdims=True) + EPS)
        xh = pk * r
        dkv = dk_ref[...]
        dkv2 = jnp.concatenate([dkv, dv_ref[...]], axis=1).astype(BF)
        dkv2_ref[...] = dkv2
        dckv = lax.dot_general(dkv2, w_ref[...], (((1,), (1,)), ((), ())), preferred_element_type=F32)
        a = dckv * g_ref[...]
        dpk = r * (a - xh * jnp.mean(a * xh, axis=-1, keepdims=True))
        dkr = dkv[:, 0:HP]
        for hh in range(1, NH):
            dkr = dkr + dkv[:, hh * HP:(hh + 1) * HP]
        res = jnp.concatenate([dpk, dkr * ck_ref[...], dkr * sk_ref[...]], axis=1).astype(BF)
        dg = jnp.sum(dckv * xh, axis=0, keepdims=True)

        @pl.when(i == 0)
        def _():
            dg_ref[...] = dg

        @pl.when(i > 0)
        def _():
            dg_ref[...] += dg

        @pl.when(i < NLAT)
        def _():
            dp_ref[...] = res

        @pl.when(i == NLAT)
        def _():
            dpc_ref[...] = res

    rb = lambda w: pl.BlockSpec((tm, w), lambda i: (i, 0))
    return pl.pallas_call(
        body, name=name, grid=(nb,),
        in_specs=[pl.BlockSpec((tm, 512), lambda i: (0, 0)),
                  pl.BlockSpec((tm, 512), lambda i: (jnp.minimum(i, NLAT - 1), kvcol)),
                  rb(NH * HP), rb(NH * DV), _row(KVL), pl.BlockSpec((KVL, NH * HP + NH * DV), lambda i: (0, 0)),
                  rb(HP), rb(HP), pl.BlockSpec(memory_space=pl.ANY)],
        out_specs=[pl.BlockSpec((tm, 512), lambda i: (jnp.minimum(i, NLAT - 1), kvcol)),
                   pl.BlockSpec((tm, 512), lambda i: (0, 0)), rb(NH * HP + NH * DV), _row(KVL)],
        out_shape=[jax.ShapeDtypeStruct((T, NIN), BF), jax.ShapeDtypeStruct((TC, 512), BF),
                   jax.ShapeDtypeStruct((TKV, NH * HP + NH * DV), BF), jax.ShapeDtypeStruct((1, KVL), F32)],
        input_output_aliases={8: 0},
        compiler_params=pltpu.CompilerParams(dimension_semantics=("arbitrary",)),
    )(pc, p, dk, dv, kvg, wkv2, ck, sk, dp)


def _pieces(src, width, n):
    out, c = [], src
    while c < src + width:
        k = c // n
        w = min(src + width, (k + 1) * n) - c
        out.append((k, c - k * n, c - src, w))
        c += w
    return out


def _win_moves():
    mv = [(2208, 1024, O_GA), (3232, 1024, O_GC), (0, KVL, O_KV), (256, DR, O_KV + KVL + DN), (288, QL, O_Q)]
    mv += [(256 + _swap_start(g), 8, O_KV + KVL + HP + DN + 8 * g) for g in range(4)]
    for j in range(CONV // CVB):
        base = O_CV + 3 * CVB * j
        mv += [(672 + CVB * j, CVB, base), (1184 + CVB * j, CVB, base + CVB), (1696 + CVB * j, CVB, base + 2 * CVB)]
    return mv


_WIN_ZERO = [(O_KV + KVL, DN), (O_KV + KVL + DN + DR, HP - DN - DR), (O_KV + KVL + HP, DN),
             (O_KV + KVL + HP + DN + DR, HP - DN - DR), (O_Q + QL, 512 - QL)]


def build_win(g, *, name, tm=256):
    def body(g_ref, o_ref):
        for src, w, dst in _win_moves():
            for k, a, off, pw in _pieces(src, w, SH_IN):
                o_ref[:, dst + off:dst + off + pw] = g_ref[k, :, a:a + pw]
        for c0, w in _WIN_ZERO:
            o_ref[:, c0:c0 + w] = jnp.zeros((tm, w), o_ref.dtype)

    return pl.pallas_call(
        body, name=name, grid=(D // tm,), in_specs=[pl.BlockSpec((NDEV, tm, SH_IN), lambda i: (0, i, 0))],
        out_specs=pl.BlockSpec((tm, NIN), lambda i: (i, 0)), out_shape=jax.ShapeDtypeStruct((D, NIN), g.dtype),
        compiler_params=pltpu.CompilerParams(dimension_semantics=("parallel",)),
    )(g)


def shard_win_grad(dw, dwc, *, name, tm=256):
    def body(dw_ref, dwc_ref, o_ref, kvs):
        kvs[...] = dw_ref[:, O_KV:O_KV + 512] + dwc_ref[...]

        def src(col, w):
            if O_KV <= col < O_KV + 512:
                return kvs[:, col - O_KV:col - O_KV + w]
            return dw_ref[:, col:col + w]

        for s, w, dst in _win_moves():
            if w == 8 or s == 256:
                continue
            for k, a, off, pw in _pieces(s, w, SH_IN):
                o_ref[k, :, a:a + pw] = src(dst + off, pw).astype(o_ref.dtype)
        for g in range(4):
            val = src(O_KV + KVL + DN + 8 * g, 8) + src(O_KV + KVL + HP + DN + _swap_start(g), 8)
            o_ref[0, :, 256 + 8 * g:256 + 8 * g + 8] = val.astype(o_ref.dtype)

    return pl.pallas_call(
        body, name=name, grid=(D // tm,),
        in_specs=[pl.BlockSpec((tm, NIN), lambda i: (i, 0)), pl.BlockSpec((tm, 512), lambda i: (i, 0))],
        out_specs=pl.BlockSpec((NDEV, tm, SH_IN), lambda i: (0, i, 0)),
        out_shape=jax.ShapeDtypeStruct((NDEV, D, SH_IN), BF),
        scratch_shapes=[pltpu.VMEM((tm, 512), F32)],
        compiler_params=pltpu.CompilerParams(dimension_semantics=("parallel",)),
    )(dw, dwc)


def build_wq_wkv(gq, gkv, *, name):
    def body(gq_ref, gkv_ref, q_ref, kv_ref):
        q_ref[...] = jnp.zeros_like(q_ref)
        kv_ref[...] = jnp.zeros_like(kv_ref)
        for h in range(NH):
            q_ref[0:QL, h * HP:h * HP + DN + DR] = gq_ref[h]
            for g in range(4):
                c0 = NH * HP + h * HP + DN + 8 * g
                q_ref[0:QL, c0:c0 + 8] = gq_ref[h, :, DN + _swap_start(g):DN + _swap_start(g) + 8]
            kv_ref[:, h * HP:h * HP + DN] = gkv_ref[h, :, 0:DN]
            kv_ref[:, NH * HP + h * DV:NH * HP + (h + 1) * DV] = gkv_ref[h, :, DN:DN + DV]

    vm = pl.BlockSpec(memory_space=pltpu.VMEM)
    return pl.pallas_call(
        body, name=name, in_specs=[vm, vm], out_specs=[vm, vm],
        out_shape=[jax.ShapeDtypeStruct((512, 2 * NH * HP), gq.dtype), jax.ShapeDtypeStruct((KVL, NH * HP + NH * DV), gq.dtype)],
    )(gq, gkv)


def shard_wq_wkv_grad(dwq2, dwkv2, *, name):
    def body(q_ref, kv_ref, gq_ref, gkv_ref):
        for h in range(NH):
            gq_ref[h, :, 0:DN] = q_ref[0:QL, h * HP:h * HP + DN].astype(BF)
            for g in range(4):
                a = q_ref[0:QL, h * HP + DN + 8 * g:h * HP + DN + 8 * g + 8]
                c0 = NH * HP + h * HP + DN + _swap_start(g)
                gq_ref[h, :, DN + 8 * g:DN + 8 * g + 8] = (a + q_ref[0:QL, c0:c0 + 8]).astype(BF)
            gkv_ref[h, :, 0:DN] = kv_ref[:, h * HP:h * HP + DN].astype(BF)
            gkv_ref[h, :, DN:DN + DV] = kv_ref[:, NH * HP + h * DV:NH * HP + (h + 1) * DV].astype(BF)

    vm = pl.BlockSpec(memory_space=pltpu.VMEM)
    return pl.pallas_call(
        body, name=name, in_specs=[vm, vm], out_specs=[vm, vm],
        out_shape=[jax.ShapeDtypeStruct((NDEV, QL, (DN + DR)), BF), jax.ShapeDtypeStruct((NDEV, KVL, DN + DV), BF)],
    )(dwq2, dwkv2)


def unshard_cols(g, *, name, tm=256):
    _, K, n = g.shape
    tm = _pick(K, tm, 16)

    def body(g_ref, o_ref):
        for k in range(NDEV):
            o_ref[:, k * n:(k + 1) * n] = g_ref[k]

    return pl.pallas_call(
        body, name=name, grid=(K // tm,), in_specs=[pl.BlockSpec((NDEV, tm, n), lambda i: (0, i, 0))],
        out_specs=pl.BlockSpec((tm, NDEV * n), lambda i: (i, 0)), out_shape=jax.ShapeDtypeStruct((K, NDEV * n), g.dtype),
        compiler_params=pltpu.CompilerParams(dimension_semantics=("parallel",)),
    )(g)


def shard_cols(w, *, name, tm=256):
    K, n8 = w.shape
    n = n8 // NDEV
    tm = _pick(K, tm, 16)

    def body(w_ref, o_ref):
        for k in range(NDEV):
            o_ref[k] = w_ref[:, k * n:(k + 1) * n]

    return pl.pallas_call(
        body, name=name, grid=(K // tm,), in_specs=[pl.BlockSpec((tm, n8), lambda i: (i, 0))],
        out_specs=pl.BlockSpec((NDEV, tm, n), lambda i: (0, i, 0)), out_shape=jax.ShapeDtypeStruct((NDEV, K, n), w.dtype),
        compiler_params=pltpu.CompilerParams(dimension_semantics=("parallel",)),
    )(w)


def _rope_tables():
    t = np.arange(T)
    row = (t // GRID_W).astype(np.float32)
    col = (t % GRID_W).astype(np.float32)
    axis_dim = DR // 2
    inv = (np.float32(ROPE_THETA) ** (-np.arange(0, axis_dim, 2, dtype=np.float32) / np.float32(axis_dim))).astype(np.float32)
    ar, ac = (row[:, None] * inv).astype(np.float32), (col[:, None] * inv).astype(np.float32)
    cosv = np.concatenate([np.cos(ar), np.cos(ar), np.cos(ac), np.cos(ac)], axis=1).astype(np.float32)
    sinv = np.concatenate([-np.sin(ar), np.sin(ar), -np.sin(ac), np.sin(ac)], axis=1).astype(np.float32)
    ck = np.zeros((TKV, HP), np.float32)
    sk = np.zeros((TKV, HP), np.float32)
    ck[T:, DN:DN + DR] = 1.0
    ck[:T, DN:DN + DR] = cosv
    sk[:T, DN:DN + DR] = sinv
    cq = np.zeros((T, HP), np.float32)
    cq[:, :DN] = 1.0
    cq[:, DN:DN + DR] = cosv
    return jnp.asarray(ck), jnp.asarray(sk), jnp.asarray(cq), jnp.asarray(sk[:T])


def _local_step(x, ctx, tgt, mod_lat, mod_ctx, n1g, qg, kvg, n2g, fg, conv_w, conv_b, ffn_w, ffn_b, get_w, put_g, dep0):
    sh1, sc1, g1, sh2, sc2, g2 = [mod_lat[:, i * D:(i + 1) * D] for i in range(6)]
    csh1, csc1 = mod_ctx[:, 0:D], mod_ctx[:, D:2 * D]
    ck, sk, cq_t, sq_t = _rope_tables()
    qg_p = jnp.pad(qg, ((0, 0), (0, 512 - QL)))

    hcat = normmod_cat(ctx, x, n1g, csc1, csh1, sc1, sh1, dep0, name="normmod1")
    win = get_w("in", hcat)
    p = mm(hcat, win, M=T, tn=768, name="in_proj")
    pc = mm(hcat, win, M=TC, N=512, a_off=(T, 0), b_off=(0, O_KV), name="in_proj_ctx")
    wq2, wkv2, wao, wco, wo = get_w("mid", p)
    kh, vh, ckv = kvprep(pc, p, kvg, wkv2, ck, sk, name="kvprep")
    qr, cq = qprep(p, qg_p, wq2, cq_t, sq_t, name="qprep")
    o = attn_fwd(qr, kh, vh, name="attn_fwd")
    z = convz(p, conv_w, conv_b, name="convz")
    ya = mm(o, wao, name="attn_out")
    yc = mm(z, wco, name="conv_out")
    merged = gate_merge(p, ya, yc, name="gate_merge")
    a_out = mm(merged, wo, name="o_proj")
    x1, h2 = resid_normmod(x, a_out, g1, n2g, sc2, sh2, name="resid_normmod2")
    wup, wdn = get_w("ffn", h2)
    u0 = mm(h2, wup, o_stack=True, tn=1408, name="up_proj")
    f = ffn_act(u0, ffn_w, ffn_b, name="ffn_act")
    dn = mm(f, wdn, tm=512, tk=DFF, name="down_proj")
    dx2, dd, dfg, loss = final_loss(x1, dn, g2, fg, tgt, name="final_loss")

    df = mm(dd, wdn, tb=True, tn=1408, name="down_proj_dx")
    dwdn = mm(f, dd, ta=True, out_dtype=BF, tm=1408, name="down_proj_dw")
    du0, dffn_w, dffn_b = ffn_act_bwd(u0, df, ffn_w, ffn_b, name="ffn_act_bwd")
    dwup = mm(h2, du0, ta=True, b_stack=True, out_dtype=BF, tn=1408, name="up_proj_dw")
    tok = put_g("ffn", dict(dwup=dwup, dwdn=dwdn))
    dh2 = mm(du0, wup, tb=True, a_stack=True, dep=tok, name="up_proj_dx")
    dx1, da, st2 = normmod_bwd(x1, dh2, n2g, sc2, dx2, dn, g1, name="normmod2_bwd")

    dmerged = mm(da, wo, tb=True, name="o_proj_dx")
    dwo = mm(merged, da, ta=True, out_dtype=BF, tn=512, name="o_proj_dw")
    dya, dyc, dp = gate_merge_bwd(p, ya, yc, dmerged, name="gate_merge_bwd")
    do = mm(dya, wao, tb=True, out_dtype=BF, name="attn_out_dx")
    dwao = mm(o, dya, ta=True, out_dtype=BF, tn=512, name="attn_out_dw")
    dwco = mm(z, dyc, ta=True, out_dtype=BF, tn=512, name="conv_out_dw")
    tok = put_g("mid", dict(dwao=dwao, dwco=dwco, dwo=dwo))
    dz = mm(dyc, wco, tb=True, dep=tok, name="conv_out_dx")
    dp, dconv_w, dconv_b = convz_bwd(p, dz, conv_w, conv_b, dp, name="convz_bwd")
    dq, dk, dv = attn_bwd(qr, kh, vh, do, name="attn_bwd")
    dp, dq2, dqg = qprep_bwd(p, dq, qg_p, wq2, cq_t, sq_t, dp, name="qprep_bwd")
    dwq2 = mm(cq, dq2, ta=True, name="q_up_dw")
    dp, dpc, dkv2, dkvg = kvprep_bwd(pc, p, dk, dv, kvg, wkv2, ck, sk, dp, name="kvprep_bwd")
    dwkv2 = mm(ckv, dkv2, ta=True, name="kv_up_dw")
    tok = put_g("qkv", dict(dwq2=dwq2, dwkv2=dwkv2))

    dwin = mm(hcat, dp, ta=True, K=T, tn=768, dep=tok, name="in_proj_dw")
    dwin_c = mm(hcat, dpc, ta=True, K=TC, a_off=(T, 0), name="in_proj_ctx_dw")
    tok = put_g("in", dict(dwin=dwin, dwin_c=dwin_c))
    dh = mm(dp, win, tb=True, dep=tok, name="in_proj_dx")
    dhc = mm(dpc, win, tb=True, N=D, K=512, b_off=(0, O_KV), name="in_proj_ctx_dx")
    dx, _, st1 = normmod_bwd(x, dh, n1g, sc1, dx1, a_out, g1, name="normmod1_bwd")
    stc = normmod_bwd(ctx, dhc, n1g, csc1, None, None, None, name="normmod1_ctx_bwd")

    zrow = jnp.zeros((1, D), F32)
    dmod_lat = jnp.concatenate([st1[0:1], st1[1:2], st1[3:4], st2[0:1], st2[1:2], st2[3:4]], axis=1)
    dmod_ctx = jnp.concatenate([stc[0:1], stc[1:2], zrow, zrow, zrow, zrow], axis=1)
    return dict(
        loss=loss, dx=dx, dmod_lat=dmod_lat, dmod_ctx=dmod_ctx,
        dn1g=st1[2:3] + stc[2:3], dqg=dqg, dkvg=dkvg, dn2g=st2[2:3], dfg=dfg,
        dconv_w=dconv_w, dconv_b=dconv_b, dffn_w=dffn_w, dffn_b=dffn_b)


def _me():
    x, y, c = lax.axis_index("x"), lax.axis_index("y"), lax.axis_index("c")
    return x, y, c, 4 * x + 2 * y + c


def _peer(x, y, c, k):
    px = 1 - x if k & 4 else x
    py = 1 - y if k & 2 else y
    pc = 1 - c if k & 1 else c
    return (px, py, pc), 4 * px + 2 * py + pc


def _exchange_tiles(src_of_peer, buf, send_sem, recv_sem):
    x, y, c, me = _me()
    for k in range(1, NDEV):
        dev, lin = _peer(x, y, c, k)
        pltpu.make_async_remote_copy(src_ref=src_of_peer(lin), dst_ref=buf.at[me], send_sem=send_sem, recv_sem=recv_sem,
                                     device_id=dev, device_id_type=MESH).start()
    seven = buf.at[pl.ds(0, NDEV - 1)]
    pltpu.make_async_remote_copy(src_ref=seven, dst_ref=seven, send_sem=send_sem, recv_sem=recv_sem,
                                 device_id=(x, y, c), device_id_type=MESH).wait()


def _silu(z):
    return z * jax.nn.sigmoid(z)


def ada_fwd(c, c_ctx, ffn_w, conv_w, w_shard, b_shard, deps, *, name):
    nsh = w_shard.shape[1]
    deps = [d for d in deps if d is not None]

    def body(c_ref, cc_ref, fw_ref, cw_ref, w_ref, b_ref, *rest):
        s_ref, m_ref, mine, res, sems = rest[len(deps):]
        x, y, c, me = _me()
        mine[0:1, :] = _silu(c_ref[...])
        mine[1:2, :] = _silu(cc_ref[...])
        mine[2:5, :] = fw_ref[...]
        mine[5:8, :] = cw_ref[...]
        s_ref[me] = mine[...]
        _exchange_tiles(lambda lin: mine, s_ref, sems.at[0], sems.at[1])
        sall = s_ref[...].reshape(NDEV * 8, D).astype(BF)
        r = jnp.dot(sall, w_ref[...].astype(BF), preferred_element_type=F32) + b_ref[...]
        res[...] = r.reshape(NDEV, 8, nsh)
        m_ref[me] = res[me]
        _exchange_tiles(lambda lin: res.at[lin], m_ref, sems.at[2], sems.at[3])

    vm = pl.BlockSpec(memory_space=pltpu.VMEM)
    return pl.pallas_call(
        body, name=name, in_specs=[vm] * 6 + [pl.BlockSpec(memory_space=pl.ANY)] * len(deps), out_specs=[vm, vm],
        out_shape=[jax.ShapeDtypeStruct((NDEV, 8, D), F32), jax.ShapeDtypeStruct((NDEV, 8, nsh), F32)],
        scratch_shapes=[pltpu.VMEM((8, D), F32), pltpu.VMEM((NDEV, 8, nsh), F32), pltpu.SemaphoreType.DMA((4,))],
    )(c, c_ctx, ffn_w, conv_w, w_shard, b_shard, *deps)


P_DML, P_DMC, P_N1, P_QG, P_KVG, P_CB, P_N2, P_FB, P_FG, P_CW, P_FW, P_LOSS, P_ROWS = 0, 6, 12, 13, 14, 15, 16, 17, 23, 24, 27, 45, 48
FROWS = 3


def sync_small(r, deps, *, name):
    ins = [r["dmod_lat"], r["dmod_ctx"], r["dn1g"], r["dqg"], r["dkvg"], r["dconv_b"], r["dn2g"], r["dffn_b"], r["dfg"],
           r["dconv_w"], r["dffn_w"], r["loss"]]

    def put_wide(p, row0, row, n):
        for j in range(-(-n // D)):
            w = min(D, n - j * D)
            p[row0 + j:row0 + j + 1, 0:w] = row[:, j * D:j * D + w]

    def body(dml, dmc, n1, qg, kvg, cb, n2, fb, fg, cw, fw, loss, *rest):
        a_ref, sum_ref, p, sems = rest[len(deps):]
        x, y, c, me = _me()
        p[...] = jnp.zeros_like(p)
        put_wide(p, P_DML, dml, 6 * D)
        put_wide(p, P_DMC, dmc, 6 * D)
        put_wide(p, P_N1, n1, D)
        put_wide(p, P_QG, qg, 512)
        put_wide(p, P_KVG, kvg, KVL)
        put_wide(p, P_CB, cb, CONV)
        put_wide(p, P_N2, n2, D)
        put_wide(p, P_FG, fg, D)
        put_wide(p, P_LOSS, loss, 128)
        for s in range(2):
            put_wide(p, P_FB + FROWS * s, fb.at[s], DFF)
        for k in range(3):
            put_wide(p, P_CW + k, cw.at[k:k + 1], CONV)
            for s in range(2):
                put_wide(p, P_FW + FROWS * (2 * k + s), fw.at[s, k:k + 1], DFF)
        a_ref[me] = p[...]
        _exchange_tiles(lambda lin: p, a_ref, sems.at[0], sems.at[1])
        acc = a_ref[0]
        for k in range(1, NDEV):
            acc = acc + a_ref[k]
        sum_ref[...] = acc

    vm = pl.BlockSpec(memory_space=pltpu.VMEM)
    return pl.pallas_call(
        body, name=name, in_specs=[vm] * len(ins) + [pl.BlockSpec(memory_space=pl.ANY)] * len(deps), out_specs=[vm, vm],
        out_shape=[jax.ShapeDtypeStruct((NDEV, P_ROWS, D), F32), jax.ShapeDtypeStruct((P_ROWS, D), F32)],
        scratch_shapes=[pltpu.VMEM((P_ROWS, D), F32), pltpu.SemaphoreType.DMA((2,))],
    )(*ins, *deps)


def ada_bwd(s_all, dml, dmc, w_shard, c_ctx, *, name):
    nsh = w_shard.shape[1]

    def body(s_ref, dml_ref, dmc_ref, w_ref, c_ref, dw_ref, gc_ref, s16, dm16, part, buf, sems):
        x, y, c, me = _me()
        s16[...] = jnp.zeros_like(s16)
        dm16[...] = jnp.zeros_like(dm16)
        for k in range(NDEV):
            s16[k:k + 1, :] = s_ref[k, 0:1, :]
        s16[8:9, :] = s_ref[0, 1:2, :]
        dm16[0:8, :] = dml_ref[...]
        dm16[8:9, :] = dmc_ref[...]
        dw_ref[...] = lax.dot_general(s16[...].astype(BF), dm16[...].astype(BF), (((0,), (0,)), ((), ())),
                                      preferred_element_type=F32)
        part[...] = lax.dot_general(dm16[8:16, :].astype(BF), w_ref[...].astype(BF), (((1,), (1,)), ((), ())),
                                    preferred_element_type=F32)
        buf[me] = part[...]
        _exchange_tiles(lambda lin: part, buf, sems.at[0], sems.at[1])
        acc = buf[0]
        for k in range(1, NDEV):
            acc = acc + buf[k]
        z = c_ref[...]
        sg = jax.nn.sigmoid(z)
        gc_ref[...] = acc * (sg * (1.0 + z * (1.0 - sg)))

    vm = pl.BlockSpec(memory_space=pltpu.VMEM)
    return pl.pallas_call(
        body, name=name, in_specs=[vm] * 5, out_specs=[vm, vm],
        out_shape=[jax.ShapeDtypeStruct((D, nsh), F32), jax.ShapeDtypeStruct((8, D), F32)],
        scratch_shapes=[pltpu.VMEM((16, D), F32), pltpu.VMEM((16, nsh), F32), pltpu.VMEM((8, D), F32),
                        pltpu.VMEM((NDEV, 8, D), F32), pltpu.SemaphoreType.DMA((2,))],
    )(s_all, dml, dmc, w_shard, c_ctx)


HBM_SPEC = pl.BlockSpec(memory_space=pltpu.HBM)
SEM_SPEC = pl.BlockSpec(memory_space=pltpu.SEMAPHORE)
EFFECT = pltpu.SideEffectType.DATAFLOW_SIDE_EFFECTING


def _exchange_copies(srcs, lands, send, recv, per_peer):
    x, y, c, me = _me()
    cps = []
    for t in range(len(srcs)):
        for k in range(1, NDEV):
            dev, lin = _peer(x, y, c, k)
            cps.append(pltpu.make_async_remote_copy(
                src_ref=srcs[t].at[lin] if per_peer else srcs[t], dst_ref=lands[t].at[me],
                send_sem=send.at[7 * t + k - 1], recv_sem=recv.at[7 * t + k - 1], device_id=dev, device_id_type=MESH))
    return cps


def _own_copies(srcs, lands, own, per_peer):
    me = _me()[3]
    return [pltpu.make_async_copy(srcs[t].at[me] if per_peer else srcs[t], lands[t].at[me], own.at[t])
            for t in range(len(srcs))]


def exchange_start(srcs, *, per_peer, name, dep=None):
    nt = len(srcs)
    land_shapes = [(a.shape if per_peer else (NDEV,) + a.shape) for a in srcs]
    deps = [] if dep is None else [dep]

    def body(*refs):
        src, land = refs[:nt], refs[nt:2 * nt]
        send, recv, own = refs[2 * nt + len(deps):2 * nt + len(deps) + 3]
        for cp in _exchange_copies(src, land, send, recv, per_peer) + _own_copies(src, land, own, per_peer):
            cp.start()
        refs[-1][...] = jnp.zeros_like(refs[-1])

    hb = lambda a: pltpu.with_memory_space_constraint(a, pltpu.HBM)
    outs = pl.pallas_call(
        body, name=name,
        out_shape=(pltpu.SemaphoreType.DMA((7 * nt,)), pltpu.SemaphoreType.DMA((7 * nt,)), pltpu.SemaphoreType.DMA((nt,)),
                   *[pltpu.HBM(a.shape, a.dtype) for a in srcs], *[pltpu.HBM(s, a.dtype) for s, a in zip(land_shapes, srcs)],
                   jax.ShapeDtypeStruct((8, 128), F32)),
        in_specs=[HBM_SPEC] * (2 * nt) + [pl.BlockSpec(memory_space=pl.ANY)] * len(deps),
        out_specs=(SEM_SPEC, SEM_SPEC, SEM_SPEC, *([HBM_SPEC] * (2 * nt)), pl.BlockSpec(memory_space=pltpu.VMEM)),
        input_output_aliases={i: 3 + i for i in range(2 * nt)},
        compiler_params=pltpu.CompilerParams(has_side_effects=EFFECT),
    )(*[hb(a) for a in srcs], *[hb(lax.empty(s, a.dtype)) for s, a in zip(land_shapes, srcs)], *deps)
    return dict(send=outs[0], recv=outs[1], own=outs[2], src=list(outs[3:3 + nt]), land=list(outs[3 + nt:3 + 2 * nt]),
                token=outs[-1], per_peer=per_peer)


def exchange_wait(h, after, *, name):
    nt = len(h["src"])
    per_peer = h["per_peer"]

    def body(*refs):
        src, land, send, recv, own = refs[:nt], refs[nt:2 * nt], refs[2 * nt], refs[2 * nt + 1], refs[2 * nt + 2]
        for cp in _exchange_copies(src, land, send, recv, per_peer):
            cp.wait_send()
            cp.wait_recv()
        for cp in _own_copies(src, land, own, per_peer):
            cp.wait()

    outs = pl.pallas_call(
        body, name=name,
        out_shape=(*[pltpu.HBM(a.shape, a.dtype) for a in h["src"]], *[pltpu.HBM(a.shape, a.dtype) for a in h["land"]]),
        in_specs=[HBM_SPEC] * (2 * nt) + [SEM_SPEC, SEM_SPEC, SEM_SPEC, pl.BlockSpec(memory_space=pl.ANY)],
        out_specs=tuple([HBM_SPEC] * (2 * nt)),
        input_output_aliases={i: i for i in range(2 * nt)},
        compiler_params=pltpu.CompilerParams(has_side_effects=EFFECT),
    )(*h["src"], *h["land"], h["send"], h["recv"], h["own"], after)
    return list(outs[nt:])


def _adamw_math(w, g, m, v):
    nm = B1 * m + (1.0 - B1) * g
    nv = B2 * v + (1.0 - B2) * (g * g)
    m_hat = nm / (1.0 - B1 ** STEP)
    v_hat = nv / (1.0 - B2 ** STEP)
    return -LR * (m_hat / (jnp.sqrt(v_hat) + AEPS) + WD * w), nm, nv


def adamw_many(ws, gs, ms, vs, *, name):
    n = len(ws)

    def body(*refs):
        for k in range(n):
            d, nm, nv = _adamw_math(refs[k][...], refs[n + k][...], refs[2 * n + k][...], refs[3 * n + k][...])
            refs[4 * n + k][...] = d
            refs[5 * n + k][...] = nm
            refs[6 * n + k][...] = nv

    vm = pl.BlockSpec(memory_space=pltpu.VMEM)
    sh = [jax.ShapeDtypeStruct(w.shape, F32) for w in ws]
    outs = pl.pallas_call(body, name=name, in_specs=[vm] * (4 * n), out_specs=[vm] * (3 * n), out_shape=sh * 3,
                          )(*ws, *gs, *ms, *vs)
    return outs[:n], outs[n:2 * n], outs[2 * n:]


def adamw(w, g, m, v, *, name, tr=256):
    R, C = w.shape
    tr = _pick(R, tr, 8)

    def body(w_ref, g_ref, m_ref, v_ref, d_ref, nm_ref, nv_ref):
        d_ref[...], nm_ref[...], nv_ref[...] = _adamw_math(w_ref[...], g_ref[...], m_ref[...], v_ref[...])

    blk = pl.BlockSpec((tr, C), lambda i: (i, 0))
    sh = jax.ShapeDtypeStruct((R, C), F32)
    return pl.pallas_call(
        body, name=name, grid=(R // tr,), in_specs=[blk, blk, blk, blk], out_specs=[blk, blk, blk],
        out_shape=[sh, sh, sh], compiler_params=pltpu.CompilerParams(dimension_semantics=("parallel",)),
    )(w, g, m, v)


def adamw_slots(w, slots, m, v, *, name, tr=256):
    R, C = w.shape
    tr = _pick(R, tr, 16)

    def body(w_ref, s_ref, m_ref, v_ref, g_ref, d_ref, nm_ref, nv_ref):
        g = s_ref[0].astype(F32)
        for k in range(1, NDEV):
            g = g + s_ref[k].astype(F32)
        g_ref[...] = g
        d_ref[...], nm_ref[...], nv_ref[...] = _adamw_math(w_ref[...], g, m_ref[...], v_ref[...])

    blk = pl.BlockSpec((tr, C), lambda i: (i, 0))
    sh = jax.ShapeDtypeStruct((R, C), F32)
    return pl.pallas_call(
        body, name=name, grid=(R // tr,), in_specs=[blk, pl.BlockSpec((NDEV, tr, C), lambda i: (0, i, 0)), blk, blk],
        out_specs=[blk, blk, blk, blk], out_shape=[sh, sh, sh, sh],
        compiler_params=pltpu.CompilerParams(dimension_semantics=("parallel",)),
    )(w, slots, m, v)


def _padc(a, n=D):
    return jnp.pad(a, ((0, 0), (0, n - a.shape[1])))


def kernel(x, c, ctx, c_ctx, w_ada, b_ada, norm1_g, w_in, q_norm_g, kv_norm_g, w_uq, w_ukv, conv_w, conv_b, w_attn_out, w_conv_out, w_o, norm2_g, w_up, ffn_conv_w, ffn_conv_b, w_down, final_g, loss_target, m_c_ctx, m_w_ada, m_b_ada, m_norm1_g, m_w_in, m_q_norm_g, m_kv_norm_g, m_w_uq, m_w_ukv, m_conv_w, m_conv_b, m_w_attn_out, m_w_conv_out, m_w_o, m_norm2_g, m_w_up, m_ffn_conv_w, m_ffn_conv_b, m_w_down, m_final_g, v_c_ctx, v_w_ada, v_b_ada, v_norm1_g, v_w_in, v_q_norm_g, v_kv_norm_g, v_w_uq, v_w_ukv, v_conv_w, v_conv_b, v_w_attn_out, v_w_conv_out, v_w_o, v_norm2_g, v_w_up, v_ffn_conv_w, v_ffn_conv_b, v_w_down, v_final_g):
    me = 4 * lax.axis_index("x") + 2 * lax.axis_index("y") + lax.axis_index("c")
    W = dict(c_ctx=c_ctx, w_ada=w_ada, b_ada=b_ada, norm1_g=norm1_g, w_in=w_in, q_norm_g=q_norm_g, kv_norm_g=kv_norm_g,
             w_uq=w_uq, w_ukv=w_ukv, conv_w=conv_w, conv_b=conv_b, w_attn_out=w_attn_out, w_conv_out=w_conv_out, w_o=w_o,
             norm2_g=norm2_g, w_up=w_up, ffn_conv_w=ffn_conv_w, ffn_conv_b=ffn_conv_b, w_down=w_down, final_g=final_g)
    M = dict(c_ctx=m_c_ctx, w_ada=m_w_ada, b_ada=m_b_ada, norm1_g=m_norm1_g, w_in=m_w_in, q_norm_g=m_q_norm_g,
             kv_norm_g=m_kv_norm_g, w_uq=m_w_uq, w_ukv=m_w_ukv, conv_w=m_conv_w, conv_b=m_conv_b, w_attn_out=m_w_attn_out,
             w_conv_out=m_w_conv_out, w_o=m_w_o, norm2_g=m_norm2_g, w_up=m_w_up, ffn_conv_w=m_ffn_conv_w,
             ffn_conv_b=m_ffn_conv_b, w_down=m_w_down, final_g=m_final_g)
    V = dict(c_ctx=v_c_ctx, w_ada=v_w_ada, b_ada=v_b_ada, norm1_g=v_norm1_g, w_in=v_w_in, q_norm_g=v_q_norm_g,
             kv_norm_g=v_kv_norm_g, w_uq=v_w_uq, w_ukv=v_w_ukv, conv_w=v_conv_w, conv_b=v_conv_b, w_attn_out=v_w_attn_out,
             w_conv_out=v_w_conv_out, w_o=v_w_o, norm2_g=v_norm2_g, w_up=v_w_up, ffn_conv_w=v_ffn_conv_w,
             ffn_conv_b=v_ffn_conv_b, w_down=v_w_down, final_g=v_final_g)
    names = list(W)
    as2d = lambda a: a.reshape(1, -1) if a.ndim == 1 else a.reshape(a.shape[-2], a.shape[-1])
    W2 = {k: as2d(a) for k, a in W.items()}
    M2 = {k: as2d(a) for k, a in M.items()}
    V2 = {k: as2d(a) for k, a in V.items()}
    nsh = W2["w_ada"].shape[1]

    b_sh = lax.dynamic_slice(W2["b_ada"], (0, me * nsh), (1, nsh))
    s_all, m_all = ada_fwd(c, W2["c_ctx"], _padc(W2["ffn_conv_w"]), _padc(W2["conv_w"]), W2["w_ada"], b_sh, [],
                           name="ada_fwd")
    mod_lat = m_all[:, 0, :].reshape(1, 6 * D)
    mod_ctx = m_all[:, 1, :].reshape(1, 6 * D)
    ffn_w_full = s_all[:, 2:5, :2 * DFF // NDEV].transpose(1, 0, 2).reshape(3, 2 * DFF)
    conv_w_full = s_all[:, 5:8, :CONV // NDEV].transpose(1, 0, 2).reshape(3, CONV)

    stage_w = {"in": ["w_in"], "mid": ["w_uq", "w_ukv", "w_attn_out", "w_conv_out", "w_o"], "ffn": ["w_up", "w_down"]}
    ag, tok = {}, m_all
    for st, nms in stage_w.items():
        ag[st] = exchange_start([W2[nm].astype(BF) for nm in nms], per_peer=False, dep=tok, name="ag_start_" + st)
        tok = ag[st]["token"]

    def get_w(stage, after):
        g = dict(zip(stage_w[stage], exchange_wait(ag[stage], after, name="ag_wait_" + stage)))
        if stage == "in":
            return build_win(g["w_in"], name="build_win")
        if stage == "mid":
            wq2, wkv2 = build_wq_wkv(g["w_uq"], g["w_ukv"], name="build_wq_wkv")
            return (wq2, wkv2, unshard_cols(g["w_attn_out"], name="unshard_w_attn_out"),
                    unshard_cols(g["w_conv_out"], name="unshard_w_conv_out"), g["w_o"].reshape(D, D))
        return unshard_cols(g["w_up"], name="unshard_w_up"), g["w_down"].reshape(DFF, D)

    stage_g = {"ffn": ["w_up", "w_down"], "mid": ["w_attn_out", "w_conv_out", "w_o"], "qkv": ["w_uq", "w_ukv"],
               "in": ["w_in"]}
    rs = {}

    def put_g(stage, g):
        if stage == "in":
            parts = [shard_win_grad(g["dwin"], g["dwin_c"], name="shard_win_grad")]
        elif stage == "mid":
            parts = [shard_cols(g["dwao"], name="shard_w_attn_out"), shard_cols(g["dwco"], name="shard_w_conv_out"),
                     g["dwo"].reshape(NDEV, D // NDEV, D)]
        elif stage == "qkv":
            parts = list(shard_wq_wkv_grad(g["dwq2"], g["dwkv2"], name="shard_wq_wkv_grad"))
        else:
            parts = [shard_cols(g["dwup"], name="shard_w_up"), g["dwdn"].reshape(NDEV, DFF // NDEV, D)]
        rs[stage] = exchange_start(parts, per_peer=True, name="rs_start_" + stage)
        return rs[stage]["token"]

    r = _local_step(x[0], ctx[0], loss_target[0], mod_lat, mod_ctx, W2["norm1_g"], W2["q_norm_g"], W2["kv_norm_g"],
                    W2["norm2_g"], W2["final_g"], conv_w_full, W2["conv_b"], ffn_w_full, W2["ffn_conv_b"], get_w, put_g,
                    ag["ffn"]["token"])

    G, DL, NM, NV = {}, {}, {}, {}

    def finish(stage, after):
        for nm, sl in zip(stage_g[stage], exchange_wait(rs[stage], after, name="rs_wait_" + stage)):
            G[nm], DL[nm], NM[nm], NV[nm] = adamw_slots(W2[nm], sl, M2[nm], V2[nm], name="adamw_" + nm)
            after = DL[nm]
        return after

    after = r["dx"]
    for st in ("ffn", "mid", "qkv"):
        after = finish(st, after)

    a_buf, ssum = sync_small(r, [DL[nm] for st in ("ffn", "mid", "qkv") for nm in stage_g[st]], name="sync_small")
    loss = ssum[P_LOSS, 0]
    G["norm1_g"] = ssum[P_N1:P_N1 + 1]
    G["q_norm_g"] = ssum[P_QG:P_QG + 1, :QL]
    G["kv_norm_g"] = ssum[P_KVG:P_KVG + 1, :KVL]
    G["conv_b"] = ssum[P_CB:P_CB + 1, :CONV]
    G["norm2_g"] = ssum[P_N2:P_N2 + 1]
    G["ffn_conv_b"] = ssum[P_FB:P_FB + 2 * FROWS].reshape(1, 2, FROWS * D)[:, :, :DFF].reshape(1, 2 * DFF)
    G["final_g"] = ssum[P_FG:P_FG + 1]
    G["conv_w"] = lax.dynamic_slice(ssum[P_CW:P_CW + 3, :CONV], (0, me * (CONV // NDEV)), (3, CONV // NDEV))
    fw_full = ssum[P_FW:P_FW + 6 * FROWS].reshape(3, 2, FROWS * D)[:, :, :DFF].reshape(3, 2 * DFF)
    G["ffn_conv_w"] = lax.dynamic_slice(fw_full, (0, me * (2 * DFF // NDEV)), (3, 2 * DFF // NDEV))
    G["b_ada"] = (ssum[P_DML:P_DML + 6] + ssum[P_DMC:P_DMC + 6]).reshape(1, 6 * D)

    dml = lax.dynamic_slice(a_buf[:, P_DML:P_DML + 6, :].reshape(NDEV, 6 * D), (0, me * nsh), (NDEV, nsh))
    dmc = lax.dynamic_slice(ssum[P_DMC:P_DMC + 6].reshape(1, 6 * D), (0, me * nsh), (1, nsh))
    G["w_ada"], gcc = ada_bwd(s_all, dml, dmc, W2["w_ada"], W2["c_ctx"], name="ada_bwd")
    G["c_ctx"] = gcc[0:1]

    DL["w_ada"], NM["w_ada"], NV["w_ada"] = adamw(W2["w_ada"], G["w_ada"], M2["w_ada"], V2["w_ada"], name="adamw_w_ada")
    small = ["c_ctx", "b_ada", "norm1_g", "q_norm_g", "kv_norm_g", "conv_b", "norm2_g", "ffn_conv_b", "final_g", "conv_w",
             "ffn_conv_w"]
    ds, nms, nvs = adamw_many([W2[k] for k in small], [G[k] for k in small], [M2[k] for k in small],
                              [V2[k] for k in small], name="adamw_small")
    for k, nm in enumerate(small):
        DL[nm], NM[nm], NV[nm] = ds[k], nms[k], nvs[k]
    finish("in", ds[0])

    outs = [loss, r["dx"][None]]
    for grp in (G, DL, NM, NV):
        outs += [grp[nm].reshape(W[nm].shape) for nm in names]
    return tuple(outs)
```

```python
import functools
import numpy as np
import jax
import jax.numpy as jnp
from jax import lax
from jax.experimental import pallas as pl
from jax.experimental.pallas import tpu as pltpu

F32 = jnp.float32
BF = jnp.bfloat16
MESH = pl.DeviceIdType.MESH

D = 1024
T = 2048
TC = 256
TKV = T + TC
GRID_W = 64
NH = 8
DN = 64
DR = 32
DV = 64
QL = 384
KVL = 256
CONV = 512
DFF = 2816
EPS = 1e-6
ROPE_THETA = 10000.0
SCALE = (DN + DR) ** -0.5
NDEV = 8
HP = 128

O_GA, O_GC, O_KV, O_Q, O_CV = 0, 1024, 2048, 2560, 3072
NIN = 4608
CVB = 256
N_IN = 4256
SH_IN = N_IN // NDEV

LR, B1, B2, AEPS, WD, STEP = 0.001, 0.9, 0.999, 1e-08, 0.01, 10


def _pick(n, target, mult=128):
    best = None
    for d in range(mult, min(n, target) + 1, mult):
        if n % d == 0:
            best = d
    return best if best is not None else n


def _swap_start(g):
    return 8 * (g ^ 1)


def mm(a, b, *, ta=False, tb=False, out_dtype=F32, name, tm=1024, tn=1024, tk=2048, M=None, N=None, K=None,
       a_off=(0, 0), b_off=(0, 0), a_stack=False, b_stack=False, o_stack=False, dep=None):
    def dims(arr, stack):
        return (arr.shape[1], 2 * arr.shape[2]) if stack else arr.shape

    ar, ac = dims(a, a_stack)
    br, bc = dims(b, b_stack)
    M = M or ((ac if ta else ar) - a_off[1 if ta else 0])
    K = K or ((ar if ta else ac) - a_off[0 if ta else 1])
    N = N or ((br if tb else bc) - b_off[0 if tb else 1])
    tm = _pick(M, tm, 128 if ta else 16)
    tn = _pick(N // 2 if (o_stack or (b_stack and not tb)) else N, tn, 128)
    tk = _pick(K // 2 if ((a_stack and not ta) or (b_stack and tb)) else K, tk, 128)
    nk = K // tk
    ca = 0 if ta else 1
    cb = 1 if tb else 0

    def body(a_ref, b_ref, *rest):
        o_ref, acc = rest[-2:]
        k = pl.program_id(2)
        part = lax.dot_general(a_ref[...].astype(BF), b_ref[...].astype(BF),
                               (((ca,), (cb,)), ((), ())), preferred_element_type=F32)
        if nk == 1:
            o_ref[...] = part.astype(o_ref.dtype)
        else:
            @pl.when(k == 0)
            def _():
                acc[...] = part

            @pl.when(k > 0)
            def _():
                acc[...] += part

            @pl.when(k == nk - 1)
            def _():
                o_ref[...] = acc[...].astype(o_ref.dtype)

    def spec(blk, rc, off, stack, ncols):
        assert off[0] % blk[0] == 0 and off[1] % blk[1] == 0, (name, blk, off)
        ro, co = off[0] // blk[0], off[1] // blk[1]
        if not stack:
            return pl.BlockSpec(blk, lambda i, j, k: (rc(i, j, k)[0] + ro, rc(i, j, k)[1] + co))
        nhb = ncols // 2 // blk[1]
        return pl.BlockSpec((None,) + blk,
                            lambda i, j, k: ((rc(i, j, k)[1] + co) // nhb, rc(i, j, k)[0] + ro, (rc(i, j, k)[1] + co) % nhb))

    a_spec = spec((tk, tm), lambda i, j, k: (k, i), a_off, a_stack, ac) if ta else \
        spec((tm, tk), lambda i, j, k: (i, k), a_off, a_stack, ac)
    b_spec = spec((tn, tk), lambda i, j, k: (j, k), b_off, b_stack, bc) if tb else \
        spec((tk, tn), lambda i, j, k: (k, j), b_off, b_stack, bc)
    o_spec = spec((tm, tn), lambda i, j, k: (i, j), (0, 0), o_stack, N)
    o_shape = (2, M, N // 2) if o_stack else (M, N)
    deps = [] if dep is None else [dep]
    return pl.pallas_call(
        body, name=name, grid=(M // tm, N // tn, nk),
        in_specs=[a_spec, b_spec] + [pl.BlockSpec(memory_space=pl.ANY)] * len(deps),
        out_specs=o_spec, out_shape=jax.ShapeDtypeStruct(o_shape, out_dtype),
        scratch_shapes=[pltpu.VMEM((tm, tn) if nk > 1 else (8, 128), F32)],
        compiler_params=pltpu.CompilerParams(dimension_semantics=("parallel", "parallel", "arbitrary")),
    )(a, b, *deps)


def _row(width):
    return pl.BlockSpec((1, width), lambda *_: (0, 0))


NLAT = T // TC


def normmod_cat(ctx, x, g, csc, csh, sc, sh, dep, *, name, tm=256):
    assert tm == TC

    def body(c_ref, x_ref, g_ref, csc_ref, csh_ref, sc_ref, sh_ref, dep_ref, h_ref):
        last = pl.program_id(0) == NLAT
        xv = jnp.where(last, c_ref[...], x_ref[...])
        scv = jnp.where(last, csc_ref[...], sc_ref[...])
        shv = jnp.where(last, csh_ref[...], sh_ref[...])
        r = lax.rsqrt(jnp.mean(xv * xv, axis=-1, keepdims=True) + EPS)
        h_ref[...] = ((xv * r * g_ref[...]) * (1.0 + scv) + shv).astype(BF)

    return pl.pallas_call(
        body, name=name, grid=(TKV // tm,),
        in_specs=[pl.BlockSpec((tm, D), lambda i: (0, 0)), pl.BlockSpec((tm, D), lambda i: (jnp.minimum(i, NLAT - 1), 0)),
                  _row(D), _row(D), _row(D), _row(D), _row(D), pl.BlockSpec(memory_space=pl.ANY)],
        out_specs=pl.BlockSpec((tm, D), lambda i: (i, 0)), out_shape=jax.ShapeDtypeStruct((TKV, D), BF),
        compiler_params=pltpu.CompilerParams(dimension_semantics=("parallel",)),
    )(ctx, x, g, csc, csh, sc, sh, dep)


def resid_normmod(x, a, gate, g, sc, sh, *, name, tm=256):
    R = x.shape[0]

    def body(x_ref, a_ref, gate_ref, g_ref, sc_ref, sh_ref, x1_ref, h_ref):
        xv = x_ref[...] + gate_ref[...] * a_ref[...]
        x1_ref[...] = xv
        r = lax.rsqrt(jnp.mean(xv * xv, axis=-1, keepdims=True) + EPS)
        h_ref[...] = ((xv * r * g_ref[...]) * (1.0 + sc_ref[...]) + sh_ref[...]).astype(BF)

    blk = pl.BlockSpec((tm, D), lambda i: (i, 0))
    return pl.pallas_call(
        body, name=name, grid=(R // tm,), in_specs=[blk, blk, _row(D), _row(D), _row(D), _row(D)],
        out_specs=[blk, blk],
        out_shape=[jax.ShapeDtypeStruct((R, D), F32), jax.ShapeDtypeStruct((R, D), BF)],
        compiler_params=pltpu.CompilerParams(dimension_semantics=("parallel",)),
    )(x, a, gate, g, sc, sh)


def kvprep(pc, p, kvg, wkv2, ck, sk, *, name, tm=256):
    assert tm == TC
    nb = TKV // tm
    kvcol = O_KV // 512

    def body(pc_ref, p_ref, g_ref, w_ref, ck_ref, sk_ref, k_ref, v_ref, ckv_ref):
        i = pl.program_id(0)
        t = jnp.where(i == NLAT, pc_ref[...], p_ref[...])
        pk = t[:, :KVL]
        r = lax.rsqrt(jnp.mean(pk * pk, axis=-1, keepdims=True) + EPS)
        ckv = (pk * r * g_ref[...]).astype(BF)
        ckv_ref[...] = ckv
        kv2 = jnp.dot(ckv, w_ref[...], preferred_element_type=F32)
        krr = t[:, KVL:KVL + HP] * ck_ref[...] + t[:, KVL + HP:KVL + 2 * HP] * sk_ref[...]
        k_ref[...] = (kv2[:, :NH * HP] + jnp.concatenate([krr] * NH, axis=1)).astype(BF)
        v_ref[...] = kv2[:, NH * HP:].astype(BF)

    return pl.pallas_call(
        body, name=name, grid=(nb,),
        in_specs=[pl.BlockSpec((tm, 512), lambda i: (0, 0)),
                  pl.BlockSpec((tm, 512), lambda i: (jnp.minimum(i, NLAT - 1), kvcol)),
                  _row(KVL), pl.BlockSpec((KVL, NH * HP + NH * DV), lambda i: (0, 0)),
                  pl.BlockSpec((tm, HP), lambda i: (i, 0)), pl.BlockSpec((tm, HP), lambda i: (i, 0))],
        out_specs=[pl.BlockSpec((tm, NH * HP), lambda i: (i, 0)), pl.BlockSpec((tm, NH * DV), lambda i: (i, 0)),
                   pl.BlockSpec((tm, KVL), lambda i: (i, 0))],
        out_shape=[jax.ShapeDtypeStruct((TKV, NH * HP), BF), jax.ShapeDtypeStruct((TKV, NH * DV), BF),
                   jax.ShapeDtypeStruct((TKV, KVL), BF)],
        compiler_params=pltpu.CompilerParams(dimension_semantics=("parallel",)),
    )(pc, p, kvg, wkv2, ck, sk)


def qprep(p, qg, wq2, cq_t, sq_t, *, name, tm=256):
    qcol = O_Q // 512

    def body(p_ref, g_ref, w_ref, c_ref, s_ref, q_ref, cq_ref):
        pq = p_ref[...]
        r = lax.rsqrt(jnp.sum(pq * pq, axis=-1, keepdims=True) * (1.0 / QL) + EPS)
        cq = (pq * r * g_ref[...]).astype(BF)
        cq_ref[...] = cq
        q2 = jnp.dot(cq, w_ref[...], preferred_element_type=F32)
        cc = jnp.concatenate([c_ref[...]] * NH, axis=1)
        ss = jnp.concatenate([s_ref[...]] * NH, axis=1)
        q_ref[...] = (q2[:, :NH * HP] * cc + q2[:, NH * HP:] * ss).astype(BF)

    return pl.pallas_call(
        body, name=name, grid=(T // tm,),
        in_specs=[pl.BlockSpec((tm, 512), lambda i: (i, qcol)), _row(512),
                  pl.BlockSpec((512, 2 * NH * HP), lambda i: (0, 0)),
                  pl.BlockSpec((tm, HP), lambda i: (i, 0)), pl.BlockSpec((tm, HP), lambda i: (i, 0))],
        out_specs=[pl.BlockSpec((tm, NH * HP), lambda i: (i, 0)), pl.BlockSpec((tm, 512), lambda i: (i, 0))],
        out_shape=[jax.ShapeDtypeStruct((T, NH * HP), BF), jax.ShapeDtypeStruct((T, 512), BF)],
        compiler_params=pltpu.CompilerParams(dimension_semantics=("parallel",)),
    )(p, qg, wq2, cq_t, sq_t)


def _head_mask(h):
    lanes = lax.broadcasted_iota(jnp.int32, (1, 2 * DV), 1)
    return (lanes // DV) == (h % 2)


KC = 256
NKC = TKV // KC
LOG2E = 1.4426950408889634


def _scores_pass(q, k_ref, s_scr):
    m = None
    for c in range(NKC):
        s = lax.dot_general(q, k_ref[c * KC:(c + 1) * KC, :], (((1,), (1,)), ((), ())),
                            preferred_element_type=F32) * (SCALE * LOG2E)
        s_scr[:, c * KC:(c + 1) * KC] = s
        mc = jnp.max(s, axis=-1, keepdims=True)
        m = mc if m is None else jnp.maximum(m, mc)
    return m


def attn_fwd(q, k, v, *, name, tq=256):
    def body(q_ref, k_ref, v_ref, o_ref, s_scr):
        h = pl.program_id(1)
        m = _scores_pass(q_ref[...], k_ref, s_scr)
        l = jnp.zeros((tq, 1), F32)
        acc = jnp.zeros((tq, 2 * DV), F32)
        for c in range(NKC):
            e = jnp.exp2(s_scr[:, c * KC:(c + 1) * KC] - m)
            l = l + jnp.sum(e, axis=-1, keepdims=True)
            acc = acc + jnp.dot(e.astype(BF), v_ref[c * KC:(c + 1) * KC, :], preferred_element_type=F32)
        o2 = jnp.where(_head_mask(h), acc * (1.0 / l), 0.0).astype(BF)

        @pl.when(h % 2 == 0)
        def _():
            o_ref[...] = o2

        @pl.when(h % 2 == 1)
        def _():
            o_ref[...] = o_ref[...] + o2

    return pl.pallas_call(
        body, name=name, grid=(T // tq, NH),
        in_specs=[pl.BlockSpec((tq, HP), lambda i, h: (i, h)), pl.BlockSpec((TKV, HP), lambda i, h: (0, h)),
                  pl.BlockSpec((TKV, 2 * DV), lambda i, h: (0, h // 2))],
        out_specs=pl.BlockSpec((tq, 2 * DV), lambda i, h: (i, h // 2)),
        out_shape=jax.ShapeDtypeStruct((T, NH * DV), BF),
        scratch_shapes=[pltpu.VMEM((tq, TKV), F32)],
        compiler_params=pltpu.CompilerParams(dimension_semantics=("parallel", "arbitrary")),
    )(q, k, v)


def _shift_dn(x):
    n = x.shape[0]
    rows = lax.broadcasted_iota(jnp.int32, (n, 1), 0)
    return jnp.where(rows == 0, 0.0, pltpu.roll(x, 1, axis=0))


def _shift_up(x):
    n = x.shape[0]
    rows = lax.broadcasted_iota(jnp.int32, (n, 1), 0)
    return jnp.where(rows == n - 1, 0.0, pltpu.roll(x, n - 1, axis=0))


def _conv(x, w_ref, b_ref):
    return b_ref[...] + _shift_dn(x) * w_ref[0:1, :] + x * w_ref[1:2, :] + _shift_up(x) * w_ref[2:3, :]


def _conv_t(dy, w_ref):
    return _shift_up(dy) * w_ref[0:1, :] + dy * w_ref[1:2, :] + _shift_dn(dy) * w_ref[2:3, :]


def _conv_wgrad(dw_ref, dy, x):
    dw_ref[0:1, :] = jnp.sum(dy * _shift_dn(x), axis=0, keepdims=True)
    dw_ref[1:2, :] = jnp.sum(dy * x, axis=0, keepdims=True)
    dw_ref[2:3, :] = jnp.sum(dy * _shift_up(x), axis=0, keepdims=True)


def convz(p, cw, cb, *, name):
    o0 = O_CV // (3 * CVB)

    def body(p_ref, w_ref, bias_ref, z_ref):
        xv, bv, cv = p_ref[:, 0:CVB], p_ref[:, CVB:2 * CVB], p_ref[:, 2 * CVB:3 * CVB]
        z_ref[...] = (bv * _conv(cv * xv, w_ref, bias_ref)).astype(BF)

    return pl.pallas_call(
        body, name=name, grid=(CONV // CVB,),
        in_specs=[pl.BlockSpec((T, 3 * CVB), lambda j: (0, o0 + j)), pl.BlockSpec((3, CVB), lambda j: (0, j)),
                  pl.BlockSpec((1, CVB), lambda j: (0, j))],
        out_specs=pl.BlockSpec((T, CVB), lambda j: (0, j)),
        out_shape=jax.ShapeDtypeStruct((T, CONV), BF),
        compiler_params=pltpu.CompilerParams(dimension_semantics=("parallel",)),
    )(p, cw, cb)


def gate_merge(p, ya, yc, *, name, tm=256):
    def body(ga_ref, gc_ref, ya_ref, yc_ref, o_ref):
        o_ref[...] = (jax.nn.sigmoid(ga_ref[...]) * ya_ref[...] + jax.nn.sigmoid(gc_ref[...]) * yc_ref[...]).astype(BF)

    blk = pl.BlockSpec((tm, D), lambda i: (i, 0))
    return pl.pallas_call(
        body, name=name, grid=(T // tm,),
        in_specs=[pl.BlockSpec((tm, D), lambda i: (i, O_GA // D)), pl.BlockSpec((tm, D), lambda i: (i, O_GC // D)), blk, blk],
        out_specs=blk, out_shape=jax.ShapeDtypeStruct((T, D), BF),
        compiler_params=pltpu.CompilerParams(dimension_semantics=("parallel",)),
    )(p, p, ya, yc)


def ffn_act(u0, cw, cb, *, name, tc=256):
    nb = DFF // tc

    def body(u_ref, wg_ref, wv_ref, bg_ref, bv_ref, f_ref):
        ug = _conv(u_ref[0], wg_ref, bg_ref)
        uv = _conv(u_ref[1], wv_ref, bv_ref)
        f_ref[...] = (ug * jax.nn.sigmoid(ug) * uv).astype(BF)

    return pl.pallas_call(
        body, name=name, grid=(nb,),
        in_specs=[pl.BlockSpec((2, T, tc), lambda j: (0, 0, j)),
                  pl.BlockSpec((3, tc), lambda j: (0, j)), pl.BlockSpec((3, tc), lambda j: (0, nb + j)),
                  pl.BlockSpec((1, tc), lambda j: (0, j)), pl.BlockSpec((1, tc), lambda j: (0, nb + j))],
        out_specs=pl.BlockSpec((T, tc), lambda j: (0, j)),
        out_shape=jax.ShapeDtypeStruct((T, DFF), BF),
        compiler_params=pltpu.CompilerParams(dimension_semantics=("parallel",)),
    )(u0, cw, cw, cb, cb)


def final_loss(x1, d, g2, fg, tgt, *, name, tm=256):
    def body(x1_ref, d_ref, g2_ref, fg_ref, t_ref, dx_ref, dd_ref, dfg_ref, loss_ref):
        i = pl.program_id(0)
        xv = x1_ref[...] + g2_ref[...] * d_ref[...]
        r = lax.rsqrt(jnp.mean(xv * xv, axis=-1, keepdims=True) + EPS)
        xh = xv * r
        diff = xh * fg_ref[...] - t_ref[...]
        part = 0.5 * jnp.sum(jnp.mean(diff * diff, axis=-1, keepdims=True), axis=0, keepdims=True)
        dy = diff * (1.0 / D)
        a = dy * fg_ref[...]
        dx = r * (a - xh * jnp.mean(a * xh, axis=-1, keepdims=True))
        dx_ref[...] = dx
        dd_ref[...] = (dx * g2_ref[...]).astype(BF)
        dfg = jnp.sum(dy * xh, axis=0, keepdims=True)

        @pl.when(i == 0)
        def _():
            dfg_ref[...] = dfg
            loss_ref[...] = jnp.broadcast_to(part, (1, 128))

        @pl.when(i > 0)
        def _():
            dfg_ref[...] += dfg
            loss_ref[...] += jnp.broadcast_to(part, (1, 128))

    blk = pl.BlockSpec((tm, D), lambda i: (i, 0))
    return pl.pallas_call(
        body, name=name, grid=(T // tm,), in_specs=[blk, blk, _row(D), _row(D), blk],
        out_specs=[blk, blk, _row(D), _row(128)],
        out_shape=[jax.ShapeDtypeStruct((T, D), F32), jax.ShapeDtypeStruct((T, D), BF),
                   jax.ShapeDtypeStruct((1, D), F32), jax.ShapeDtypeStruct((1, 128), F32)],
        compiler_params=pltpu.CompilerParams(dimension_semantics=("arbitrary",)),
    )(x1, d, g2, fg, tgt)


def normmod_bwd(x, dh, g, sc, dres, gsrc, gate, *, name, tm=256):
    R = x.shape[0]
    has_res = dres is not None

    def body(*refs):
        if has_res:
            x_ref, dh_ref, g_ref, sc_ref, dres_ref, gsrc_ref, gate_ref, dx_ref, dxg_ref, st_ref = refs
        else:
            x_ref, dh_ref, g_ref, sc_ref, st_ref = refs
        i = pl.program_id(0)
        xv = x_ref[...]
        r = lax.rsqrt(jnp.mean(xv * xv, axis=-1, keepdims=True) + EPS)
        xh = xv * r
        dhv = dh_ref[...]
        n = xh * g_ref[...]
        dn = dhv * (1.0 + sc_ref[...])
        a = dn * g_ref[...]
        rows = [jnp.sum(dhv, axis=0, keepdims=True), jnp.sum(dhv * n, axis=0, keepdims=True),
                jnp.sum(dn * xh, axis=0, keepdims=True)]
        if has_res:
            dr = dres_ref[...]
            dx = dr + r * (a - xh * jnp.mean(a * xh, axis=-1, keepdims=True))
            dx_ref[...] = dx
            dxg_ref[...] = (dx * gate_ref[...]).astype(BF)
            rows.append(jnp.sum(dr * gsrc_ref[...], axis=0, keepdims=True))
        else:
            rows.append(jnp.zeros((1, D), F32))

        @pl.when(i == 0)
        def _():
            for k, row in enumerate(rows):
                st_ref[k:k + 1, :] = row

        @pl.when(i > 0)
        def _():
            for k, row in enumerate(rows):
                st_ref[k:k + 1, :] += row

    blk = pl.BlockSpec((tm, D), lambda i: (i, 0))
    st_spec = pl.BlockSpec((4, D), lambda i: (0, 0))
    st_shape = jax.ShapeDtypeStruct((4, D), F32)
    cp = pltpu.CompilerParams(dimension_semantics=("arbitrary",))
    if has_res:
        return pl.pallas_call(
            body, name=name, grid=(R // tm,), in_specs=[blk, blk, _row(D), _row(D), blk, blk, _row(D)],
            out_specs=[blk, blk, st_spec],
            out_shape=[jax.ShapeDtypeStruct((R, D), F32), jax.ShapeDtypeStruct((R, D), BF), st_shape],
            compiler_params=cp,
        )(x, dh, g, sc, dres, gsrc, gate)
    return pl.pallas_call(
        body, name=name, grid=(R // tm,), in_specs=[blk, blk, _row(D), _row(D)],
        out_specs=st_spec, out_shape=st_shape, compiler_params=cp,
    )(x, dh, g, sc)


def ffn_act_bwd(u0, df, cw, cb, *, name, tc=256):
    nb = DFF // tc

    def body(u_ref, df_ref, wg_ref, wv_ref, bg_ref, bv_ref, du_ref, dw_ref, db_ref):
        xg, xv = u_ref[0], u_ref[1]
        ug = _conv(xg, wg_ref, bg_ref)
        uv = _conv(xv, wv_ref, bv_ref)
        sig = jax.nn.sigmoid(ug)
        dfv = df_ref[...]
        dug = dfv * uv * (sig * (1.0 + ug * (1.0 - sig)))
        duv = dfv * (ug * sig)
        du_ref[0] = _conv_t(dug, wg_ref).astype(BF)
        du_ref[1] = _conv_t(duv, wv_ref).astype(BF)
        _conv_wgrad(dw_ref.at[0], dug, xg)
        _conv_wgrad(dw_ref.at[1], duv, xv)
        db_ref[0] = jnp.sum(dug, axis=0, keepdims=True)
        db_ref[1] = jnp.sum(duv, axis=0, keepdims=True)

    lo = lambda r: pl.BlockSpec((r, tc), lambda j: (0, j))
    hi = lambda r: pl.BlockSpec((r, tc), lambda j: (0, nb + j))
    st = lambda r: pl.BlockSpec((2, r, tc), lambda j: (0, 0, j))
    return pl.pallas_call(
        body, name=name, grid=(nb,),
        in_specs=[st(T), lo(T), lo(3), hi(3), lo(1), hi(1)],
        out_specs=[st(T), st(3), st(1)],
        out_shape=[jax.ShapeDtypeStruct((2, T, DFF), BF), jax.ShapeDtypeStruct((2, 3, DFF), F32),
                   jax.ShapeDtypeStruct((2, 1, DFF), F32)],
        compiler_params=pltpu.CompilerParams(dimension_semantics=("parallel",)),
    )(u0, df, cw, cw, cb, cb)


def gate_merge_bwd(p, ya, yc, dm, *, name, tm=256):
    def body(ga_ref, gc_ref, ya_ref, yc_ref, dm_ref, dya_ref, dyc_ref, dp_ref):
        sa, sc_ = jax.nn.sigmoid(ga_ref[...]), jax.nn.sigmoid(gc_ref[...])
        dmv = dm_ref[...]
        dya_ref[...] = (dmv * sa).astype(BF)
        dyc_ref[...] = (dmv * sc_).astype(BF)
        dp_ref[:, 0:D] = (dmv * ya_ref[...] * (sa * (1.0 - sa))).astype(BF)
        dp_ref[:, D:2 * D] = (dmv * yc_ref[...] * (sc_ * (1.0 - sc_))).astype(BF)

    blk = pl.BlockSpec((tm, D), lambda i: (i, 0))
    sh = jax.ShapeDtypeStruct((T, D), BF)
    return pl.pallas_call(
        body, name=name, grid=(T // tm,),
        in_specs=[pl.BlockSpec((tm, D), lambda i: (i, O_GA // D)), pl.BlockSpec((tm, D), lambda i: (i, O_GC // D)), blk, blk, blk],
        out_specs=[blk, blk, pl.BlockSpec((tm, 2 * D), lambda i: (i, 0))],
        out_shape=[sh, sh, jax.ShapeDtypeStruct((T, NIN), BF)],
        compiler_params=pltpu.CompilerParams(dimension_semantics=("parallel",)),
    )(p, p, ya, yc, dm)


def convz_bwd(p, dz, cw, cb, dp, *, name):
    o0 = O_CV // (3 * CVB)

    def body(p_ref, dz_ref, w_ref, bias_ref, dp_in, dp_ref, dw_ref, dbias_ref):
        xv, bv, cv = p_ref[:, 0:CVB], p_ref[:, CVB:2 * CVB], p_ref[:, 2 * CVB:3 * CVB]
        ci = cv * xv
        dwc = _conv(ci, w_ref, bias_ref)
        dzv = dz_ref[...]
        ddw = dzv * bv
        dci = _conv_t(ddw, w_ref)
        dp_ref[:, 0:CVB] = (dci * cv).astype(BF)
        dp_ref[:, CVB:2 * CVB] = (dzv * dwc).astype(BF)
        dp_ref[:, 2 * CVB:3 * CVB] = (dci * xv).astype(BF)
        _conv_wgrad(dw_ref, ddw, ci)
        dbias_ref[...] = jnp.sum(ddw, axis=0, keepdims=True)

    own = lambda r: pl.BlockSpec((r, CVB), lambda j: (0, j))
    return pl.pallas_call(
        body, name=name, grid=(CONV // CVB,),
        in_specs=[pl.BlockSpec((T, 3 * CVB), lambda j: (0, o0 + j)), own(T), own(3), own(1),
                  pl.BlockSpec(memory_space=pl.ANY)],
        out_specs=[pl.BlockSpec((T, 3 * CVB), lambda j: (0, o0 + j)), own(3), own(1)],
        out_shape=[jax.ShapeDtypeStruct((T, NIN), BF), jax.ShapeDtypeStruct((3, CONV), F32),
                   jax.ShapeDtypeStruct((1, CONV), F32)],
        input_output_aliases={4: 0},
        compiler_params=pltpu.CompilerParams(dimension_semantics=("parallel",)),
    )(p, dz, cw, cb, dp)


def attn_bwd(q, k, v, do, *, name, tq=256):
    def body(q_ref, k_ref, v_ref, do_ref, dq_ref, dk_ref, dv_ref, s_scr, dp_scr):
        h, i = pl.program_id(0), pl.program_id(1)

        @pl.when(i == 0)
        def _():
            dk_ref[...] = jnp.zeros_like(dk_ref)

        @pl.when((i == 0) & (h % 2 == 0))
        def _():
            dv_ref[...] = jnp.zeros_like(dv_ref)

        qv = q_ref[...]
        dom = jnp.where(_head_mask(h), do_ref[...], jnp.zeros_like(do_ref[...]))
        m = _scores_pass(qv, k_ref, s_scr)
        l = jnp.zeros((tq, 1), F32)
        dsum = jnp.zeros((tq, 1), F32)
        for c in range(NKC):
            cols = slice(c * KC, (c + 1) * KC)
            e = jnp.exp2(s_scr[:, cols] - m)
            s_scr[:, cols] = e
            dp = lax.dot_general(dom, v_ref[cols, :], (((1,), (1,)), ((), ())), preferred_element_type=F32)
            dp_scr[:, cols] = dp
            l = l + jnp.sum(e, axis=-1, keepdims=True)
            dsum = dsum + jnp.sum(e * dp, axis=-1, keepdims=True)
        inv = 1.0 / l
        delta = dsum * inv
        dos = (dom.astype(F32) * inv).astype(BF)
        dq = jnp.zeros((tq, HP), F32)
        for c in range(NKC):
            cols = slice(c * KC, (c + 1) * KC)
            e = s_scr[:, cols]
            ds = (e * (dp_scr[:, cols] - delta) * (inv * SCALE)).astype(BF)
            dq = dq + jnp.dot(ds, k_ref[cols, :], preferred_element_type=F32)
            dk_ref[cols, :] += lax.dot_general(ds, qv, (((0,), (0,)), ((), ())), preferred_element_type=F32)
            dv_ref[cols, :] += lax.dot_general(e.astype(BF), dos, (((0,), (0,)), ((), ())), preferred_element_type=F32)
        dq_ref[...] = dq

    return pl.pallas_call(
        body, name=name, grid=(NH, T // tq),
        in_specs=[pl.BlockSpec((tq, HP), lambda h, i: (i, h)), pl.BlockSpec((TKV, HP), lambda h, i: (0, h)),
                  pl.BlockSpec((TKV, 2 * DV), lambda h, i: (0, h // 2)), pl.BlockSpec((tq, 2 * DV), lambda h, i: (i, h // 2))],
        out_specs=[pl.BlockSpec((tq, HP), lambda h, i: (i, h)), pl.BlockSpec((TKV, HP), lambda h, i: (0, h)),
                   pl.BlockSpec((TKV, 2 * DV), lambda h, i: (0, h // 2))],
        out_shape=[jax.ShapeDtypeStruct((T, NH * HP), F32), jax.ShapeDtypeStruct((TKV, NH * HP), F32),
                   jax.ShapeDtypeStruct((TKV, NH * DV), F32)],
        scratch_shapes=[pltpu.VMEM((tq, TKV), F32), pltpu.VMEM((tq, TKV), F32)],
        compiler_params=pltpu.CompilerParams(dimension_semantics=("arbitrary", "arbitrary")),
    )(q, k, v, do)


def qprep_bwd(p, dq, qg, wq2, cq_t, sq_t, dp, *, name, tm=256):
    qcol = O_Q // 512

    def body(p_ref, dq_ref, g_ref, w_ref, c_ref, s_ref, dp_in, dp_ref, dq2_ref, dg_ref):
        i = pl.program_id(0)
        dqv = dq_ref[...]
        cc = jnp.concatenate([c_ref[...]] * NH, axis=1)
        ss = jnp.concatenate([s_ref[...]] * NH, axis=1)
        dq2 = jnp.concatenate([dqv * cc, dqv * ss], axis=1).astype(BF)
        dq2_ref[...] = dq2
        dcq = lax.dot_general(dq2, w_ref[...], (((1,), (1,)), ((), ())), preferred_element_type=F32)
        pq = p_ref[...]
        r = lax.rsqrt(jnp.sum(pq * pq, axis=-1, keepdims=True) * (1.0 / QL) + EPS)
        xh = pq * r
        a = dcq * g_ref[...]
        dp_ref[...] = (r * (a - xh * (jnp.sum(a * xh, axis=-1, keepdims=True) * (1.0 / QL)))).astype(BF)
        dg = jnp.sum(dcq * xh, axis=0, keepdims=True)

        @pl.when(i == 0)
        def _():
            dg_ref[...] = dg

        @pl.when(i > 0)
        def _():
            dg_ref[...] += dg

    return pl.pallas_call(
        body, name=name, grid=(T // tm,),
        in_specs=[pl.BlockSpec((tm, 512), lambda i: (i, qcol)), pl.BlockSpec((tm, NH * HP), lambda i: (i, 0)), _row(512),
                  pl.BlockSpec((512, 2 * NH * HP), lambda i: (0, 0)),
                  pl.BlockSpec((tm, HP), lambda i: (i, 0)), pl.BlockSpec((tm, HP), lambda i: (i, 0)),
                  pl.BlockSpec(memory_space=pl.ANY)],
        out_specs=[pl.BlockSpec((tm, 512), lambda i: (i, qcol)), pl.BlockSpec((tm, 2 * NH * HP), lambda i: (i, 0)), _row(512)],
        out_shape=[jax.ShapeDtypeStruct((T, NIN), BF), jax.ShapeDtypeStruct((T, 2 * NH * HP), BF),
                   jax.ShapeDtypeStruct((1, 512), F32)],
        input_output_aliases={6: 0},
        compiler_params=pltpu.CompilerParams(dimension_semantics=("arbitrary",)),
    )(p, dq, qg, wq2, cq_t, sq_t, dp)


def kvprep_bwd(pc, p, dk, dv, kvg, wkv2, ck, sk, dp, *, name, tm=256):
    assert tm == TC
    nb = TKV // tm
    kvcol = O_KV // 512

    def body(pc_ref, p_ref, dk_ref, dv_ref, g_ref, w_ref, ck_ref, sk_ref, dp_in, dp_ref, dpc_ref, dkv2_ref, dg_ref):
        i = pl.program_id(0)
        t = jnp.where(i == NLAT, pc_ref[...], p_ref[...])
        pk = t[:, :KVL]
        r = lax.rsqrt(jnp.mean(pk * pk, axis=-1, keepdims=True) + EPS)
        xh = pk * r
        dkv = dk_ref[...]
        dkv2 = jnp.concatenate([dkv, dv_ref[...]], axis=1).astype(BF)
        dkv2_ref[...] = dkv2
        dckv = lax.dot_general(dkv2, w_ref[...], (((1,), (1,)), ((), ())), preferred_element_type=F32)
        a = dckv * g_ref[...]
        dpk = r * (a - xh * jnp.mean(a * xh, axis=-1, keepdims=True))
        dkr = dkv[:, 0:HP]
        for hh in range(1, NH):
            dkr = dkr + dkv[:, hh * HP:(hh + 1) * HP]
        res = jnp.concatenate([dpk, dkr * ck_ref[...], dkr * sk_ref[...]], axis=1).astype(BF)
        dg = jnp.sum(dckv * xh, axis=0, keepdims=True)

        @pl.when(i == 0)
        def _():
            dg_ref[...] = dg

        @pl.when(i > 0)
        def _():
            dg_ref[...] += dg

        @pl.when(i < NLAT)
        def _():
            dp_ref[...] = res

        @pl.when(i == NLAT)
        def _():
            dpc_ref[...] = res

    rb = lambda w: pl.BlockSpec((tm, w), lambda i: (i, 0))
    return pl.pallas_call(
        body, name=name, grid=(nb,),
        in_specs=[pl.BlockSpec((tm, 512), lambda i: (0, 0)),
                  pl.BlockSpec((tm, 512), lambda i: (jnp.minimum(i, NLAT - 1), kvcol)),
                  rb(NH * HP), rb(NH * DV), _row(KVL), pl.BlockSpec((KVL, NH * HP + NH * DV), lambda i: (0, 0)),
                  rb(HP), rb(HP), pl.BlockSpec(memory_space=pl.ANY)],
        out_specs=[pl.BlockSpec((tm, 512), lambda i: (jnp.minimum(i, NLAT - 1), kvcol)),
                   pl.BlockSpec((tm, 512), lambda i: (0, 0)), rb(NH * HP + NH * DV), _row(KVL)],
        out_shape=[jax.ShapeDtypeStruct((T, NIN), BF), jax.ShapeDtypeStruct((TC, 512), BF),
                   jax.ShapeDtypeStruct((TKV, NH * HP + NH * DV), BF), jax.ShapeDtypeStruct((1, KVL), F32)],
        input_output_aliases={8: 0},
        compiler_params=pltpu.CompilerParams(dimension_semantics=("arbitrary",)),
    )(pc, p, dk, dv, kvg, wkv2, ck, sk, dp)


def _pieces(src, width, n):
    out, c = [], src
    while c < src + width:
        k = c // n
        w = min(src + width, (k + 1) * n) - c
        out.append((k, c - k * n, c - src, w))
        c += w
    return out


def _win_moves():
    mv = [(2208, 1024, O_GA), (3232, 1024, O_GC), (0, KVL, O_KV), (256, DR, O_KV + KVL + DN), (288, QL, O_Q)]
    mv += [(256 + _swap_start(g), 8, O_KV + KVL + HP + DN + 8 * g) for g in range(4)]
    for j in range(CONV // CVB):
        base = O_CV + 3 * CVB * j
        mv += [(672 + CVB * j, CVB, base), (1184 + CVB * j, CVB, base + CVB), (1696 + CVB * j, CVB, base + 2 * CVB)]
    return mv


_WIN_ZERO = [(O_KV + KVL, DN), (O_KV + KVL + DN + DR, HP - DN - DR), (O_KV + KVL + HP, DN),
             (O_KV + KVL + HP + DN + DR, HP - DN - DR), (O_Q + QL, 512 - QL)]


def build_win(g, *, name, tm=256):
    def body(g_ref, o_ref):
        for src, w, dst in _win_moves():
            for k, a, off, pw in _pieces(src, w, SH_IN):
                o_ref[:, dst + off:dst + off + pw] = g_ref[k, :, a:a + pw]
        for c0, w in _WIN_ZERO:
            o_ref[:, c0:c0 + w] = jnp.zeros((tm, w), o_ref.dtype)

    return pl.pallas_call(
        body, name=name, grid=(D // tm,), in_specs=[pl.BlockSpec((NDEV, tm, SH_IN), lambda i: (0, i, 0))],
        out_specs=pl.BlockSpec((tm, NIN), lambda i: (i, 0)), out_shape=jax.ShapeDtypeStruct((D, NIN), g.dtype),
        compiler_params=pltpu.CompilerParams(dimension_semantics=("parallel",)),
    )(g)


def shard_win_grad(dwt, dwct, *, name, tc=256):
    def body(dw_ref, dwc_ref, o_ref, kvs):
        kvs[...] = dw_ref[O_KV:O_KV + 512, :] + dwc_ref[...]

        def src(row, w):
            if O_KV <= row < O_KV + 512:
                return kvs[row - O_KV:row - O_KV + w, :]
            return dw_ref[row:row + w, :]

        for s, w, dst in _win_moves():
            if w == 8 or s == 256:
                continue
            for k, a, off, pw in _pieces(s, w, SH_IN):
                o_ref[k, a:a + pw, :] = src(dst + off, pw).astype(o_ref.dtype)
        for g in range(4):
            val = src(O_KV + KVL + DN + 8 * g, 8) + src(O_KV + KVL + HP + DN + _swap_start(g), 8)
            o_ref[0, 256 + 8 * g:256 + 8 * g + 8, :] = val.astype(o_ref.dtype)

    return pl.pallas_call(
        body, name=name, grid=(D // tc,),
        in_specs=[pl.BlockSpec((NIN, tc), lambda j: (0, j)), pl.BlockSpec((512, tc), lambda j: (0, j))],
        out_specs=pl.BlockSpec((NDEV, SH_IN, tc), lambda j: (0, 0, j)),
        out_shape=jax.ShapeDtypeStruct((NDEV, SH_IN, D), BF),
        scratch_shapes=[pltpu.VMEM((512, tc), F32)],
        compiler_params=pltpu.CompilerParams(dimension_semantics=("parallel",)),
    )(dwt, dwct)


def build_wq_wkv(gq, gkv, *, name):
    def body(gq_ref, gkv_ref, q_ref, kv_ref):
        q_ref[...] = jnp.zeros_like(q_ref)
        kv_ref[...] = jnp.zeros_like(kv_ref)
        for h in range(NH):
            q_ref[0:QL, h * HP:h * HP + DN + DR] = gq_ref[h]
            for g in range(4):
                c0 = NH * HP + h * HP + DN + 8 * g
                q_ref[0:QL, c0:c0 + 8] = gq_ref[h, :, DN + _swap_start(g):DN + _swap_start(g) + 8]
            kv_ref[:, h * HP:h * HP + DN] = gkv_ref[h, :, 0:DN]
            kv_ref[:, NH * HP + h * DV:NH * HP + (h + 1) * DV] = gkv_ref[h, :, DN:DN + DV]

    vm = pl.BlockSpec(memory_space=pltpu.VMEM)
    return pl.pallas_call(
        body, name=name, in_specs=[vm, vm], out_specs=[vm, vm],
        out_shape=[jax.ShapeDtypeStruct((512, 2 * NH * HP), gq.dtype), jax.ShapeDtypeStruct((KVL, NH * HP + NH * DV), gq.dtype)],
    )(gq, gkv)


def shard_wq_wkv_grad(dwq2, dwkv2, *, name):
    def body(q_ref, kv_ref, gq_ref, gkv_ref):
        for h in range(NH):
            gq_ref[h, :, 0:DN] = q_ref[0:QL, h * HP:h * HP + DN].astype(BF)
            for g in range(4):
                a = q_ref[0:QL, h * HP + DN + 8 * g:h * HP + DN + 8 * g + 8]
                c0 = NH * HP + h * HP + DN + _swap_start(g)
                gq_ref[h, :, DN + 8 * g:DN + 8 * g + 8] = (a + q_ref[0:QL, c0:c0 + 8]).astype(BF)
            gkv_ref[h, :, 0:DN] = kv_ref[:, h * HP:h * HP + DN].astype(BF)
            gkv_ref[h, :, DN:DN + DV] = kv_ref[:, NH * HP + h * DV:NH * HP + (h + 1) * DV].astype(BF)

    vm = pl.BlockSpec(memory_space=pltpu.VMEM)
    return pl.pallas_call(
        body, name=name, in_specs=[vm, vm], out_specs=[vm, vm],
        out_shape=[jax.ShapeDtypeStruct((NDEV, QL, (DN + DR)), BF), jax.ShapeDtypeStruct((NDEV, KVL, DN + DV), BF)],
    )(dwq2, dwkv2)


def unshard_cols(g, *, name, tm=256):
    _, K, n = g.shape
    tm = _pick(K, tm, 16)

    def body(g_ref, o_ref):
        for k in range(NDEV):
            o_ref[:, k * n:(k + 1) * n] = g_ref[k]

    return pl.pallas_call(
        body, name=name, grid=(K // tm,), in_specs=[pl.BlockSpec((NDEV, tm, n), lambda i: (0, i, 0))],
        out_specs=pl.BlockSpec((tm, NDEV * n), lambda i: (i, 0)), out_shape=jax.ShapeDtypeStruct((K, NDEV * n), g.dtype),
        compiler_params=pltpu.CompilerParams(dimension_semantics=("parallel",)),
    )(g)


def shard_cols(w, *, name, tm=256):
    K, n8 = w.shape
    n = n8 // NDEV
    tm = _pick(K, tm, 16)

    def body(w_ref, o_ref):
        for k in range(NDEV):
            o_ref[k] = w_ref[:, k * n:(k + 1) * n]

    return pl.pallas_call(
        body, name=name, grid=(K // tm,), in_specs=[pl.BlockSpec((tm, n8), lambda i: (i, 0))],
        out_specs=pl.BlockSpec((NDEV, tm, n), lambda i: (0, i, 0)), out_shape=jax.ShapeDtypeStruct((NDEV, K, n), w.dtype),
        compiler_params=pltpu.CompilerParams(dimension_semantics=("parallel",)),
    )(w)


def _rope_tables():
    t = np.arange(T)
    row = (t // GRID_W).astype(np.float32)
    col = (t % GRID_W).astype(np.float32)
    axis_dim = DR // 2
    inv = (np.float32(ROPE_THETA) ** (-np.arange(0, axis_dim, 2, dtype=np.float32) / np.float32(axis_dim))).astype(np.float32)
    ar, ac = (row[:, None] * inv).astype(np.float32), (col[:, None] * inv).astype(np.float32)
    cosv = np.concatenate([np.cos(ar), np.cos(ar), np.cos(ac), np.cos(ac)], axis=1).astype(np.float32)
    sinv = np.concatenate([-np.sin(ar), np.sin(ar), -np.sin(ac), np.sin(ac)], axis=1).astype(np.float32)
    ck = np.zeros((TKV, HP), np.float32)
    sk = np.zeros((TKV, HP), np.float32)
    ck[T:, DN:DN + DR] = 1.0
    ck[:T, DN:DN + DR] = cosv
    sk[:T, DN:DN + DR] = sinv
    cq = np.zeros((T, HP), np.float32)
    cq[:, :DN] = 1.0
    cq[:, DN:DN + DR] = cosv
    return jnp.asarray(ck), jnp.asarray(sk), jnp.asarray(cq), jnp.asarray(sk[:T])


def _local_step(x, ctx, tgt, mod_lat, mod_ctx, n1g, qg, kvg, n2g, fg, conv_w, conv_b, ffn_w, ffn_b, get_w, put_g, dep0):
    sh1, sc1, g1, sh2, sc2, g2 = [mod_lat[:, i * D:(i + 1) * D] for i in range(6)]
    csh1, csc1 = mod_ctx[:, 0:D], mod_ctx[:, D:2 * D]
    ck, sk, cq_t, sq_t = _rope_tables()
    qg_p = jnp.pad(qg, ((0, 0), (0, 512 - QL)))

    hcat = normmod_cat(ctx, x, n1g, csc1, csh1, sc1, sh1, dep0, name="normmod1")
    win = get_w("in", hcat)
    p = mm(hcat, win, M=T, tn=768, name="in_proj")
    pc = mm(hcat, win, M=TC, N=512, a_off=(T, 0), b_off=(0, O_KV), name="in_proj_ctx")
    wq2, wkv2, wao, wco, wo = get_w("mid", p)
    kh, vh, ckv = kvprep(pc, p, kvg, wkv2, ck, sk, name="kvprep")
    qr, cq = qprep(p, qg_p, wq2, cq_t, sq_t, name="qprep")
    o = attn_fwd(qr, kh, vh, name="attn_fwd")
    z = convz(p, conv_w, conv_b, name="convz")
    ya = mm(o, wao, name="attn_out")
    yc = mm(z, wco, name="conv_out")
    merged = gate_merge(p, ya, yc, name="gate_merge")
    a_out = mm(merged, wo, name="o_proj")
    x1, h2 = resid_normmod(x, a_out, g1, n2g, sc2, sh2, name="resid_normmod2")
    wup, wdn = get_w("ffn", h2)
    u0 = mm(h2, wup, tb=True, o_stack=True, tn=1408, name="up_proj")
    f = ffn_act(u0, ffn_w, ffn_b, name="ffn_act")
    dn = mm(f, wdn, tm=512, tk=DFF, name="down_proj")
    dx2, dd, dfg, loss = final_loss(x1, dn, g2, fg, tgt, name="final_loss")

    df = mm(dd, wdn, tb=True, tn=1408, name="down_proj_dx")
    dwdn = mm(f, dd, ta=True, out_dtype=BF, tm=1408, name="down_proj_dw")
    du0, dffn_w, dffn_b = ffn_act_bwd(u0, df, ffn_w, ffn_b, name="ffn_act_bwd")
    dwup = mm(du0, h2, ta=True, a_stack=True, out_dtype=BF, tm=1408, name="up_proj_dw")
    tok = put_g("ffn", dict(dwup=dwup, dwdn=dwdn))
    dh2 = mm(du0, wup, a_stack=True, dep=tok, name="up_proj_dx")
    dx1, da, st2 = normmod_bwd(x1, dh2, n2g, sc2, dx2, dn, g1, name="normmod2_bwd")

    dmerged = mm(da, wo, tb=True, name="o_proj_dx")
    dwo = mm(merged, da, ta=True, out_dtype=BF, tn=512, name="o_proj_dw")
    dya, dyc, dp = gate_merge_bwd(p, ya, yc, dmerged, name="gate_merge_bwd")
    do = mm(dya, wao, tb=True, out_dtype=BF, name="attn_out_dx")
    dwao = mm(o, dya, ta=True, out_dtype=BF, tn=512, name="attn_out_dw")
    dwco = mm(z, dyc, ta=True, out_dtype=BF, tn=512, name="conv_out_dw")
    tok = put_g("mid", dict(dwao=dwao, dwco=dwco, dwo=dwo))
    dz = mm(dyc, wco, tb=True, dep=tok, name="conv_out_dx")
    dp, dconv_w, dconv_b = convz_bwd(p, dz, conv_w, conv_b, dp, name="convz_bwd")
    dq, dk, dv = attn_bwd(qr, kh, vh, do, name="attn_bwd")
    dp, dq2, dqg = qprep_bwd(p, dq, qg_p, wq2, cq_t, sq_t, dp, name="qprep_bwd")
    dwq2 = mm(cq, dq2, ta=True, name="q_up_dw")
    dp, dpc, dkv2, dkvg = kvprep_bwd(pc, p, dk, dv, kvg, wkv2, ck, sk, dp, name="kvprep_bwd")
    dwkv2 = mm(ckv, dkv2, ta=True, name="kv_up_dw")
    tok = put_g("qkv", dict(dwq2=dwq2, dwkv2=dwkv2))

    dwin = mm(dp, hcat, ta=True, K=T, tm=768, dep=tok, name="in_proj_dw")
    dwin_c = mm(dpc, hcat, ta=True, K=TC, b_off=(T, 0), name="in_proj_ctx_dw")
    tok = put_g("in", dict(dwin=dwin, dwin_c=dwin_c))
    dh = mm(dp, win, tb=True, dep=tok, name="in_proj_dx")
    dhc = mm(dpc, win, tb=True, N=D, K=512, b_off=(0, O_KV), name="in_proj_ctx_dx")
    dx, _, st1 = normmod_bwd(x, dh, n1g, sc1, dx1, a_out, g1, name="normmod1_bwd")
    stc = normmod_bwd(ctx, dhc, n1g, csc1, None, None, None, name="normmod1_ctx_bwd")

    zrow = jnp.zeros((1, D), F32)
    dmod_lat = jnp.concatenate([st1[0:1], st1[1:2], st1[3:4], st2[0:1], st2[1:2], st2[3:4]], axis=1)
    dmod_ctx = jnp.concatenate([stc[0:1], stc[1:2], zrow, zrow, zrow, zrow], axis=1)
    return dict(
        loss=loss, dx=dx, dmod_lat=dmod_lat, dmod_ctx=dmod_ctx,
        dn1g=st1[2:3] + stc[2:3], dqg=dqg, dkvg=dkvg, dn2g=st2[2:3], dfg=dfg,
        dconv_w=dconv_w, dconv_b=dconv_b, dffn_w=dffn_w, dffn_b=dffn_b)


def _me():
    x, y, c = lax.axis_index("x"), lax.axis_index("y"), lax.axis_index("c")
    return x, y, c, 4 * x + 2 * y + c


def _peer(x, y, c, k):
    px = 1 - x if k & 4 else x
    py = 1 - y if k & 2 else y
    pc = 1 - c if k & 1 else c
    return (px, py, pc), 4 * px + 2 * py + pc


def _exchange_tiles(src_of_peer, buf, send_sem, recv_sem):
    x, y, c, me = _me()
    for k in range(1, NDEV):
        dev, lin = _peer(x, y, c, k)
        pltpu.make_async_remote_copy(src_ref=src_of_peer(lin), dst_ref=buf.at[me], send_sem=send_sem, recv_sem=recv_sem,
                                     device_id=dev, device_id_type=MESH).start()
    seven = buf.at[pl.ds(0, NDEV - 1)]
    pltpu.make_async_remote_copy(src_ref=seven, dst_ref=seven, send_sem=send_sem, recv_sem=recv_sem,
                                 device_id=(x, y, c), device_id_type=MESH).wait()


def _silu(z):
    return z * jax.nn.sigmoid(z)


def ada_fwd(c, c_ctx, ffn_w, conv_w, w_shard, b_shard, deps, *, name):
    nsh = w_shard.shape[1]
    deps = [d for d in deps if d is not None]

    def body(c_ref, cc_ref, fw_ref, cw_ref, w_ref, b_ref, *rest):
        s_ref, m_ref, mine, res, sems = rest[len(deps):]
        x, y, c, me = _me()
        mine[0:1, :] = _silu(c_ref[...])
        mine[1:2, :] = _silu(cc_ref[...])
        mine[2:5, :] = fw_ref[...]
        mine[5:8, :] = cw_ref[...]
        s_ref[me] = mine[...]
        _exchange_tiles(lambda lin: mine, s_ref, sems.at[0], sems.at[1])
        sall = s_ref[...].reshape(NDEV * 8, D).astype(BF)
        r = jnp.dot(sall, w_ref[...].astype(BF), preferred_element_type=F32) + b_ref[...]
        res[...] = r.reshape(NDEV, 8, nsh)
        m_ref[me] = res[me]
        _exchange_tiles(lambda lin: res.at[lin], m_ref, sems.at[2], sems.at[3])

    vm = pl.BlockSpec(memory_space=pltpu.VMEM)
    return pl.pallas_call(
        body, name=name, in_specs=[vm] * 6 + [pl.BlockSpec(memory_space=pl.ANY)] * len(deps), out_specs=[vm, vm],
        out_shape=[jax.ShapeDtypeStruct((NDEV, 8, D), F32), jax.ShapeDtypeStruct((NDEV, 8, nsh), F32)],
        scratch_shapes=[pltpu.VMEM((8, D), F32), pltpu.VMEM((NDEV, 8, nsh), F32), pltpu.SemaphoreType.DMA((4,))],
    )(c, c_ctx, ffn_w, conv_w, w_shard, b_shard, *deps)


P_DML, P_DMC, P_N1, P_QG, P_KVG, P_CB, P_N2, P_FB, P_FG, P_CW, P_FW, P_LOSS, P_ROWS = 0, 6, 12, 13, 14, 15, 16, 17, 23, 24, 27, 45, 48
FROWS = 3


def sync_small(r, deps, *, name):
    ins = [r["dmod_lat"], r["dmod_ctx"], r["dn1g"], r["dqg"], r["dkvg"], r["dconv_b"], r["dn2g"], r["dffn_b"], r["dfg"],
           r["dconv_w"], r["dffn_w"], r["loss"]]

    def put_wide(p, row0, row, n):
        for j in range(-(-n // D)):
            w = min(D, n - j * D)
            p[row0 + j:row0 + j + 1, 0:w] = row[:, j * D:j * D + w]

    def body(dml, dmc, n1, qg, kvg, cb, n2, fb, fg, cw, fw, loss, *rest):
        a_ref, sum_ref, p, sems = rest[len(deps):]
        x, y, c, me = _me()
        p[...] = jnp.zeros_like(p)
        put_wide(p, P_DML, dml, 6 * D)
        put_wide(p, P_DMC, dmc, 6 * D)
        put_wide(p, P_N1, n1, D)
        put_wide(p, P_QG, qg, 512)
        put_wide(p, P_KVG, kvg, KVL)
        put_wide(p, P_CB, cb, CONV)
        put_wide(p, P_N2, n2, D)
        put_wide(p, P_FG, fg, D)
        put_wide(p, P_LOSS, loss, 128)
        for s in range(2):
            put_wide(p, P_FB + FROWS * s, fb.at[s], DFF)
        for k in range(3):
            put_wide(p, P_CW + k, cw.at[k:k + 1], CONV)
            for s in range(2):
                put_wide(p, P_FW + FROWS * (2 * k + s), fw.at[s, k:k + 1], DFF)
        a_ref[me] = p[...]
        _exchange_tiles(lambda lin: p, a_ref, sems.at[0], sems.at[1])
        acc = a_ref[0]
        for k in range(1, NDEV):
            acc = acc + a_ref[k]
        sum_ref[...] = acc

    vm = pl.BlockSpec(memory_space=pltpu.VMEM)
    return pl.pallas_call(
        body, name=name, in_specs=[vm] * len(ins) + [pl.BlockSpec(memory_space=pl.ANY)] * len(deps), out_specs=[vm, vm],
        out_shape=[jax.ShapeDtypeStruct((NDEV, P_ROWS, D), F32), jax.ShapeDtypeStruct((P_ROWS, D), F32)],
        scratch_shapes=[pltpu.VMEM((P_ROWS, D), F32), pltpu.SemaphoreType.DMA((2,))],
    )(*ins, *deps)


def ada_bwd(s_all, dml, dmc, w_shard, c_ctx, *, name):
    nsh = w_shard.shape[1]

    def body(s_ref, dml_ref, dmc_ref, w_ref, c_ref, dw_ref, gc_ref, s16, dm16, part, buf, sems):
        x, y, c, me = _me()
        s16[...] = jnp.zeros_like(s16)
        dm16[...] = jnp.zeros_like(dm16)
        for k in range(NDEV):
            s16[k:k + 1, :] = s_ref[k, 0:1, :]
        s16[8:9, :] = s_ref[0, 1:2, :]
        dm16[0:8, :] = dml_ref[...]
        dm16[8:9, :] = dmc_ref[...]
        dw_ref[...] = lax.dot_general(s16[...].astype(BF), dm16[...].astype(BF), (((0,), (0,)), ((), ())),
                                      preferred_element_type=F32)
        part[...] = lax.dot_general(dm16[8:16, :].astype(BF), w_ref[...].astype(BF), (((1,), (1,)), ((), ())),
                                    preferred_element_type=F32)
        buf[me] = part[...]
        _exchange_tiles(lambda lin: part, buf, sems.at[0], sems.at[1])
        acc = buf[0]
        for k in range(1, NDEV):
            acc = acc + buf[k]
        z = c_ref[...]
        sg = jax.nn.sigmoid(z)
        gc_ref[...] = acc * (sg * (1.0 + z * (1.0 - sg)))

    vm = pl.BlockSpec(memory_space=pltpu.VMEM)
    return pl.pallas_call(
        body, name=name, in_specs=[vm] * 5, out_specs=[vm, vm],
        out_shape=[jax.ShapeDtypeStruct((D, nsh), F32), jax.ShapeDtypeStruct((8, D), F32)],
        scratch_shapes=[pltpu.VMEM((16, D), F32), pltpu.VMEM((16, nsh), F32), pltpu.VMEM((8, D), F32),
                        pltpu.VMEM((NDEV, 8, D), F32), pltpu.SemaphoreType.DMA((2,))],
    )(s_all, dml, dmc, w_shard, c_ctx)


HBM_SPEC = pl.BlockSpec(memory_space=pltpu.HBM)
SEM_SPEC = pl.BlockSpec(memory_space=pltpu.SEMAPHORE)
EFFECT = pltpu.SideEffectType.DATAFLOW_SIDE_EFFECTING


def _exchange_copies(srcs, lands, send, recv, per_peer):
    x, y, c, me = _me()
    cps = []
    for t in range(len(srcs)):
        for k in range(1, NDEV):
            dev, lin = _peer(x, y, c, k)
            cps.append(pltpu.make_async_remote_copy(
                src_ref=srcs[t].at[lin] if per_peer else srcs[t], dst_ref=lands[t].at[me],
                send_sem=send.at[7 * t + k - 1], recv_sem=recv.at[7 * t + k - 1], device_id=dev, device_id_type=MESH))
    return cps


def _own_copies(srcs, lands, own, per_peer):
    me = _me()[3]
    return [pltpu.make_async_copy(srcs[t].at[me] if per_peer else srcs[t], lands[t].at[me], own.at[t])
            for t in range(len(srcs))]


def exchange_start(srcs, *, per_peer, name, dep=None):
    nt = len(srcs)
    land_shapes = [(a.shape if per_peer else (NDEV,) + a.shape) for a in srcs]
    deps = [] if dep is None else [dep]

    def body(*refs):
        src, land = refs[:nt], refs[nt:2 * nt]
        send, recv, own = refs[2 * nt + len(deps):2 * nt + len(deps) + 3]
        for cp in _exchange_copies(src, land, send, recv, per_peer) + _own_copies(src, land, own, per_peer):
            cp.start()
        refs[-1][...] = jnp.zeros_like(refs[-1])

    hb = lambda a: pltpu.with_memory_space_constraint(a, pltpu.HBM)
    outs = pl.pallas_call(
        body, name=name,
        out_shape=(pltpu.SemaphoreType.DMA((7 * nt,)), pltpu.SemaphoreType.DMA((7 * nt,)), pltpu.SemaphoreType.DMA((nt,)),
                   *[pltpu.HBM(a.shape, a.dtype) for a in srcs], *[pltpu.HBM(s, a.dtype) for s, a in zip(land_shapes, srcs)],
                   jax.ShapeDtypeStruct((8, 128), F32)),
        in_specs=[HBM_SPEC] * (2 * nt) + [pl.BlockSpec(memory_space=pl.ANY)] * len(deps),
        out_specs=(SEM_SPEC, SEM_SPEC, SEM_SPEC, *([HBM_SPEC] * (2 * nt)), pl.BlockSpec(memory_space=pltpu.VMEM)),
        input_output_aliases={i: 3 + i for i in range(2 * nt)},
        compiler_params=pltpu.CompilerParams(has_side_effects=EFFECT),
    )(*[hb(a) for a in srcs], *[hb(lax.empty(s, a.dtype)) for s, a in zip(land_shapes, srcs)], *deps)
    return dict(send=outs[0], recv=outs[1], own=outs[2], src=list(outs[3:3 + nt]), land=list(outs[3 + nt:3 + 2 * nt]),
                token=outs[-1], per_peer=per_peer)


def exchange_wait(h, after, *, name):
    nt = len(h["src"])
    per_peer = h["per_peer"]

    def body(*refs):
        src, land, send, recv, own = refs[:nt], refs[nt:2 * nt], refs[2 * nt], refs[2 * nt + 1], refs[2 * nt + 2]
        for cp in _exchange_copies(src, land, send, recv, per_peer):
            cp.wait_send()
            cp.wait_recv()
        for cp in _own_copies(src, land, own, per_peer):
            cp.wait()

    outs = pl.pallas_call(
        body, name=name,
        out_shape=(*[pltpu.HBM(a.shape, a.dtype) for a in h["src"]], *[pltpu.HBM(a.shape, a.dtype) for a in h["land"]]),
        in_specs=[HBM_SPEC] * (2 * nt) + [SEM_SPEC, SEM_SPEC, SEM_SPEC, pl.BlockSpec(memory_space=pl.ANY)],
        out_specs=tuple([HBM_SPEC] * (2 * nt)),
        input_output_aliases={i: i for i in range(2 * nt)},
        compiler_params=pltpu.CompilerParams(has_side_effects=EFFECT),
    )(*h["src"], *h["land"], h["send"], h["recv"], h["own"], after)
    return list(outs[nt:])


def _adamw_math(w, g, m, v):
    nm = B1 * m + (1.0 - B1) * g
    nv = B2 * v + (1.0 - B2) * (g * g)
    m_hat = nm / (1.0 - B1 ** STEP)
    v_hat = nv / (1.0 - B2 ** STEP)
    return -LR * (m_hat / (jnp.sqrt(v_hat) + AEPS) + WD * w), nm, nv


def adamw_many(ws, gs, ms, vs, *, name):
    n = len(ws)

    def body(*refs):
        for k in range(n):
            d, nm, nv = _adamw_math(refs[k][...], refs[n + k][...], refs[2 * n + k][...], refs[3 * n + k][...])
            refs[4 * n + k][...] = d
            refs[5 * n + k][...] = nm
            refs[6 * n + k][...] = nv

    vm = pl.BlockSpec(memory_space=pltpu.VMEM)
    sh = [jax.ShapeDtypeStruct(w.shape, F32) for w in ws]
    outs = pl.pallas_call(body, name=name, in_specs=[vm] * (4 * n), out_specs=[vm] * (3 * n), out_shape=sh * 3,
                          )(*ws, *gs, *ms, *vs)
    return outs[:n], outs[n:2 * n], outs[2 * n:]


def adamw(w, g, m, v, *, name, tr=256):
    R, C = w.shape
    tr = _pick(R, tr, 8)

    def body(w_ref, g_ref, m_ref, v_ref, d_ref, nm_ref, nv_ref):
        d_ref[...], nm_ref[...], nv_ref[...] = _adamw_math(w_ref[...], g_ref[...], m_ref[...], v_ref[...])

    blk = pl.BlockSpec((tr, C), lambda i: (i, 0))
    sh = jax.ShapeDtypeStruct((R, C), F32)
    return pl.pallas_call(
        body, name=name, grid=(R // tr,), in_specs=[blk, blk, blk, blk], out_specs=[blk, blk, blk],
        out_shape=[sh, sh, sh], compiler_params=pltpu.CompilerParams(dimension_semantics=("parallel",)),
    )(w, g, m, v)


def adamw_slots(w, slots, m, v, *, name, tr=256):
    unit = w.ndim == 3
    R, C = w.shape[0], w.shape[-1]
    if R % 16 == 0:
        tr = _pick(R, tr, 16)
    else:
        tr = 144

    def body(w_ref, s_ref, m_ref, v_ref, g_ref, d_ref, nm_ref, nv_ref):
        g = s_ref[0].astype(F32)
        for k in range(1, NDEV):
            g = g + s_ref[k].astype(F32)
        g_ref[...] = g
        d_ref[...], nm_ref[...], nv_ref[...] = _adamw_math(w_ref[...], g, m_ref[...], v_ref[...])

    blk = pl.BlockSpec((tr, None, C), lambda i: (i, 0, 0)) if unit else pl.BlockSpec((tr, C), lambda i: (i, 0))
    sh = jax.ShapeDtypeStruct(w.shape, F32)
    return pl.pallas_call(
        body, name=name, grid=(pl.cdiv(R, tr),), in_specs=[blk, pl.BlockSpec((NDEV, tr, C), lambda i: (0, i, 0)), blk, blk],
        out_specs=[blk, blk, blk, blk], out_shape=[sh, sh, sh, sh],
        compiler_params=pltpu.CompilerParams(dimension_semantics=("parallel",)),
    )(w, slots, m, v)


def _padc(a, n=D):
    return jnp.pad(a, ((0, 0), (0, n - a.shape[1])))


def kernel(x, c, ctx, c_ctx, w_ada, b_ada, norm1_g, w_in, q_norm_g, kv_norm_g, w_uq, w_ukv, conv_w, conv_b, w_attn_out, w_conv_out, w_o, norm2_g, w_up, ffn_conv_w, ffn_conv_b, w_down, final_g, loss_target, m_c_ctx, m_w_ada, m_b_ada, m_norm1_g, m_w_in, m_q_norm_g, m_kv_norm_g, m_w_uq, m_w_ukv, m_conv_w, m_conv_b, m_w_attn_out, m_w_conv_out, m_w_o, m_norm2_g, m_w_up, m_ffn_conv_w, m_ffn_conv_b, m_w_down, m_final_g, v_c_ctx, v_w_ada, v_b_ada, v_norm1_g, v_w_in, v_q_norm_g, v_kv_norm_g, v_w_uq, v_w_ukv, v_conv_w, v_conv_b, v_w_attn_out, v_w_conv_out, v_w_o, v_norm2_g, v_w_up, v_ffn_conv_w, v_ffn_conv_b, v_w_down, v_final_g):
    me = 4 * lax.axis_index("x") + 2 * lax.axis_index("y") + lax.axis_index("c")
    W = dict(c_ctx=c_ctx, w_ada=w_ada, b_ada=b_ada, norm1_g=norm1_g, w_in=w_in, q_norm_g=q_norm_g, kv_norm_g=kv_norm_g,
             w_uq=w_uq, w_ukv=w_ukv, conv_w=conv_w, conv_b=conv_b, w_attn_out=w_attn_out, w_conv_out=w_conv_out, w_o=w_o,
             norm2_g=norm2_g, w_up=w_up, ffn_conv_w=ffn_conv_w, ffn_conv_b=ffn_conv_b, w_down=w_down, final_g=final_g)
    M = dict(c_ctx=m_c_ctx, w_ada=m_w_ada, b_ada=m_b_ada, norm1_g=m_norm1_g, w_in=m_w_in, q_norm_g=m_q_norm_g,
             kv_norm_g=m_kv_norm_g, w_uq=m_w_uq, w_ukv=m_w_ukv, conv_w=m_conv_w, conv_b=m_conv_b, w_attn_out=m_w_attn_out,
             w_conv_out=m_w_conv_out, w_o=m_w_o, norm2_g=m_norm2_g, w_up=m_w_up, ffn_conv_w=m_ffn_conv_w,
             ffn_conv_b=m_ffn_conv_b, w_down=m_w_down, final_g=m_final_g)
    V = dict(c_ctx=v_c_ctx, w_ada=v_w_ada, b_ada=v_b_ada, norm1_g=v_norm1_g, w_in=v_w_in, q_norm_g=v_q_norm_g,
             kv_norm_g=v_kv_norm_g, w_uq=v_w_uq, w_ukv=v_w_ukv, conv_w=v_conv_w, conv_b=v_conv_b, w_attn_out=v_w_attn_out,
             w_conv_out=v_w_conv_out, w_o=v_w_o, norm2_g=v_norm2_g, w_up=v_w_up, ffn_conv_w=v_ffn_conv_w,
             ffn_conv_b=v_ffn_conv_b, w_down=v_w_down, final_g=v_final_g)
    names = list(W)
    transposed = ("w_up",)
    as2d = lambda k, a: (a.reshape(1, -1) if a.ndim == 1 else
                         a[0].T if k in transposed else a.reshape(a.shape[-2], a.shape[-1]))
    W2 = {k: as2d(k, a) for k, a in W.items()}
    M2 = {k: as2d(k, a) for k, a in M.items()}
    V2 = {k: as2d(k, a) for k, a in V.items()}
    unit3 = lambda a: jnp.transpose(a, (2, 0, 1))
    W3, M3, V3 = unit3(W["w_in"]), unit3(M["w_in"]), unit3(V["w_in"])
    nsh = W2["w_ada"].shape[1]

    b_sh = lax.dynamic_slice(W2["b_ada"], (0, me * nsh), (1, nsh))
    s_all, m_all = ada_fwd(c, W2["c_ctx"], _padc(W2["ffn_conv_w"]), _padc(W2["conv_w"]), W2["w_ada"], b_sh, [],
                           name="ada_fwd")
    mod_lat = m_all[:, 0, :].reshape(1, 6 * D)
    mod_ctx = m_all[:, 1, :].reshape(1, 6 * D)
    ffn_w_full = s_all[:, 2:5, :2 * DFF // NDEV].transpose(1, 0, 2).reshape(3, 2 * DFF)
    conv_w_full = s_all[:, 5:8, :CONV // NDEV].transpose(1, 0, 2).reshape(3, CONV)

    stage_w = {"in": ["w_in"], "mid": ["w_uq", "w_ukv", "w_attn_out", "w_conv_out", "w_o"], "ffn": ["w_up", "w_down"]}
    ag, tok = {}, m_all
    for st, nms in stage_w.items():
        ag[st] = exchange_start([W2[nm].astype(BF) for nm in nms], per_peer=False, dep=tok, name="ag_start_" + st)
        tok = ag[st]["token"]

    def get_w(stage, after):
        g = dict(zip(stage_w[stage], exchange_wait(ag[stage], after, name="ag_wait_" + stage)))
        if stage == "in":
            return build_win(g["w_in"], name="build_win")
        if stage == "mid":
            wq2, wkv2 = build_wq_wkv(g["w_uq"], g["w_ukv"], name="build_wq_wkv")
            return (wq2, wkv2, unshard_cols(g["w_attn_out"], name="unshard_w_attn_out"),
                    unshard_cols(g["w_conv_out"], name="unshard_w_conv_out"), g["w_o"].reshape(D, D))
        return g["w_up"].reshape(2 * DFF, D), g["w_down"].reshape(DFF, D)

    stage_g = {"ffn": ["w_up", "w_down"], "mid": ["w_attn_out", "w_conv_out", "w_o"], "qkv": ["w_uq", "w_ukv"],
               "in": ["w_in"]}
    rs = {}

    def put_g(stage, g):
        if stage == "in":
            parts = [shard_win_grad(g["dwin"], g["dwin_c"], name="shard_win_grad")]
        elif stage == "mid":
            parts = [shard_cols(g["dwao"], name="shard_w_attn_out"), shard_cols(g["dwco"], name="shard_w_conv_out"),
                     g["dwo"].reshape(NDEV, D // NDEV, D)]
        elif stage == "qkv":
            parts = list(shard_wq_wkv_grad(g["dwq2"], g["dwkv2"], name="shard_wq_wkv_grad"))
        else:
            parts = [g["dwup"].reshape(NDEV, 2 * DFF // NDEV, D), g["dwdn"].reshape(NDEV, DFF // NDEV, D)]
        rs[stage] = exchange_start(parts, per_peer=True, name="rs_start_" + stage)
        return rs[stage]["token"]

    r = _local_step(x[0], ctx[0], loss_target[0], mod_lat, mod_ctx, W2["norm1_g"], W2["q_norm_g"], W2["kv_norm_g"],
                    W2["norm2_g"], W2["final_g"], conv_w_full, W2["conv_b"], ffn_w_full, W2["ffn_conv_b"], get_w, put_g,
                    ag["ffn"]["token"])

    G, DL, NM, NV = {}, {}, {}, {}

    def finish(stage, after):
        for nm, sl in zip(stage_g[stage], exchange_wait(rs[stage], after, name="rs_wait_" + stage)):
            wmv = (W3, M3, V3) if nm == "w_in" else (W2[nm], M2[nm], V2[nm])
            G[nm], DL[nm], NM[nm], NV[nm] = adamw_slots(wmv[0], sl, wmv[1], wmv[2], name="adamw_" + nm)
            after = DL[nm]
        return after

    after = r["dx"]
    for st in ("ffn", "mid", "qkv"):
        after = finish(st, after)

    a_buf, ssum = sync_small(r, [DL[nm] for st in ("ffn", "mid", "qkv") for nm in stage_g[st]], name="sync_small")
    loss = ssum[P_LOSS, 0]
    G["norm1_g"] = ssum[P_N1:P_N1 + 1]
    G["q_norm_g"] = ssum[P_QG:P_QG + 1, :QL]
    G["kv_norm_g"] = ssum[P_KVG:P_KVG + 1, :KVL]
    G["conv_b"] = ssum[P_CB:P_CB + 1, :CONV]
    G["norm2_g"] = ssum[P_N2:P_N2 + 1]
    G["ffn_conv_b"] = ssum[P_FB:P_FB + 2 * FROWS].reshape(1, 2, FROWS * D)[:, :, :DFF].reshape(1, 2 * DFF)
    G["final_g"] = ssum[P_FG:P_FG + 1]
    G["conv_w"] = lax.dynamic_slice(ssum[P_CW:P_CW + 3, :CONV], (0, me * (CONV // NDEV)), (3, CONV // NDEV))
    fw_full = ssum[P_FW:P_FW + 6 * FROWS].reshape(3, 2, FROWS * D)[:, :, :DFF].reshape(3, 2 * DFF)
    G["ffn_conv_w"] = lax.dynamic_slice(fw_full, (0, me * (2 * DFF // NDEV)), (3, 2 * DFF // NDEV))
    G["b_ada"] = (ssum[P_DML:P_DML + 6] + ssum[P_DMC:P_DMC + 6]).reshape(1, 6 * D)

    dml = lax.dynamic_slice(a_buf[:, P_DML:P_DML + 6, :].reshape(NDEV, 6 * D), (0, me * nsh), (NDEV, nsh))
    dmc = lax.dynamic_slice(ssum[P_DMC:P_DMC + 6].reshape(1, 6 * D), (0, me * nsh), (1, nsh))
    G["w_ada"], gcc = ada_bwd(s_all, dml, dmc, W2["w_ada"], W2["c_ctx"], name="ada_bwd")
    G["c_ctx"] = gcc[0:1]

    DL["w_ada"], NM["w_ada"], NV["w_ada"] = adamw(W2["w_ada"], G["w_ada"], M2["w_ada"], V2["w_ada"], name="adamw_w_ada")
    small = ["c_ctx", "b_ada", "norm1_g", "q_norm_g", "kv_norm_g", "conv_b", "norm2_g", "ffn_conv_b", "final_g", "conv_w",
             "ffn_conv_w"]
    ds, nms, nvs = adamw_many([W2[k] for k in small], [G[k] for k in small], [M2[k] for k in small],
                              [V2[k] for k in small], name="adamw_small")
    for k, nm in enumerate(small):
        DL[nm], NM[nm], NV[nm] = ds[k], nms[k], nvs[k]
    finish("in", ds[0])

    outs = [loss, r["dx"][None]]
    for grp in (G, DL, NM, NV):
        outs += [grp[nm].T[None] if nm in transposed else
                 jnp.transpose(grp[nm], (1, 2, 0)) if nm == "w_in" else grp[nm].reshape(W[nm].shape) for nm in names]
    return tuple(outs)
```

```python
import functools
import numpy as np
import jax
import jax.numpy as jnp
from jax import lax
from jax.experimental import pallas as pl
from jax.experimental.pallas import tpu as pltpu

F32 = jnp.float32
BF = jnp.bfloat16
MESH = pl.DeviceIdType.MESH

D = 1024
T = 2048
TC = 256
TKV = T + TC
GRID_W = 64
NH = 8
DN = 64
DR = 32
DV = 64
QL = 384
KVL = 256
CONV = 512
DFF = 2816
EPS = 1e-6
ROPE_THETA = 10000.0
SCALE = (DN + DR) ** -0.5
NDEV = 8
HP = 128

O_GA, O_GC, O_KV, O_Q, O_CV = 0, 1024, 2048, 2560, 3072
NIN = 4608
CVB = 256
N_IN = 4256
SH_IN = N_IN // NDEV

LR, B1, B2, AEPS, WD, STEP = 0.001, 0.9, 0.999, 1e-08, 0.01, 10


def _pick(n, target, mult=128):
    best = None
    for d in range(mult, min(n, target) + 1, mult):
        if n % d == 0:
            best = d
    return best if best is not None else n


def _swap_start(g):
    return 8 * (g ^ 1)


def mm(a, b, *, ta=False, tb=False, out_dtype=F32, name, tm=1024, tn=1024, tk=2048, M=None, N=None, K=None,
       a_off=(0, 0), b_off=(0, 0), a_stack=False, b_stack=False, o_stack=False, dep=None):
    def dims(arr, stack):
        return (arr.shape[1], 2 * arr.shape[2]) if stack else arr.shape

    ar, ac = dims(a, a_stack)
    br, bc = dims(b, b_stack)
    M = M or ((ac if ta else ar) - a_off[1 if ta else 0])
    K = K or ((ar if ta else ac) - a_off[0 if ta else 1])
    N = N or ((br if tb else bc) - b_off[0 if tb else 1])
    tm = _pick(M, tm, 128 if ta else 16)
    tn = _pick(N // 2 if (o_stack or (b_stack and not tb)) else N, tn, 128)
    tk = _pick(K // 2 if ((a_stack and not ta) or (b_stack and tb)) else K, tk, 128)
    nk = K // tk
    ca = 0 if ta else 1
    cb = 1 if tb else 0

    def body(a_ref, b_ref, *rest):
        o_ref, acc = rest[-2:]
        k = pl.program_id(2)
        part = lax.dot_general(a_ref[...].astype(BF), b_ref[...].astype(BF),
                               (((ca,), (cb,)), ((), ())), preferred_element_type=F32)
        if nk == 1:
            o_ref[...] = part.astype(o_ref.dtype)
        else:
            @pl.when(k == 0)
            def _():
                acc[...] = part

            @pl.when(k > 0)
            def _():
                acc[...] += part

            @pl.when(k == nk - 1)
            def _():
                o_ref[...] = acc[...].astype(o_ref.dtype)

    def spec(blk, rc, off, stack, ncols):
        assert off[0] % blk[0] == 0 and off[1] % blk[1] == 0, (name, blk, off)
        ro, co = off[0] // blk[0], off[1] // blk[1]
        if not stack:
            return pl.BlockSpec(blk, lambda i, j, k: (rc(i, j, k)[0] + ro, rc(i, j, k)[1] + co))
        nhb = ncols // 2 // blk[1]
        return pl.BlockSpec((None,) + blk,
                            lambda i, j, k: ((rc(i, j, k)[1] + co) // nhb, rc(i, j, k)[0] + ro, (rc(i, j, k)[1] + co) % nhb))

    a_spec = spec((tk, tm), lambda i, j, k: (k, i), a_off, a_stack, ac) if ta else \
        spec((tm, tk), lambda i, j, k: (i, k), a_off, a_stack, ac)
    b_spec = spec((tn, tk), lambda i, j, k: (j, k), b_off, b_stack, bc) if tb else \
        spec((tk, tn), lambda i, j, k: (k, j), b_off, b_stack, bc)
    o_spec = spec((tm, tn), lambda i, j, k: (i, j), (0, 0), o_stack, N)
    o_shape = (2, M, N // 2) if o_stack else (M, N)
    deps = [] if dep is None else [dep]
    return pl.pallas_call(
        body, name=name, grid=(M // tm, N // tn, nk),
        in_specs=[a_spec, b_spec] + [pl.BlockSpec(memory_space=pl.ANY)] * len(deps),
        out_specs=o_spec, out_shape=jax.ShapeDtypeStruct(o_shape, out_dtype),
        scratch_shapes=[pltpu.VMEM((tm, tn) if nk > 1 else (8, 128), F32)],
        compiler_params=pltpu.CompilerParams(dimension_semantics=("parallel", "parallel", "arbitrary")),
    )(a, b, *deps)


def _row(width):
    return pl.BlockSpec((1, width), lambda *_: (0, 0))


NLAT = T // TC


def normmod_cat(ctx, x, g, csc, csh, sc, sh, dep, *, name, tm=256):
    assert tm == TC

    def body(c_ref, x_ref, g_ref, csc_ref, csh_ref, sc_ref, sh_ref, dep_ref, h_ref):
        last = pl.program_id(0) == NLAT
        xv = jnp.where(last, c_ref[...], x_ref[...])
        scv = jnp.where(last, csc_ref[...], sc_ref[...])
        shv = jnp.where(last, csh_ref[...], sh_ref[...])
        r = lax.rsqrt(jnp.mean(xv * xv, axis=-1, keepdims=True) + EPS)
        h_ref[...] = ((xv * r * g_ref[...]) * (1.0 + scv) + shv).astype(BF)

    return pl.pallas_call(
        body, name=name, grid=(TKV // tm,),
        in_specs=[pl.BlockSpec((tm, D), lambda i: (0, 0)), pl.BlockSpec((tm, D), lambda i: (jnp.minimum(i, NLAT - 1), 0)),
                  _row(D), _row(D), _row(D), _row(D), _row(D), pl.BlockSpec(memory_space=pl.ANY)],
        out_specs=pl.BlockSpec((tm, D), lambda i: (i, 0)), out_shape=jax.ShapeDtypeStruct((TKV, D), BF),
        compiler_params=pltpu.CompilerParams(dimension_semantics=("parallel",)),
    )(ctx, x, g, csc, csh, sc, sh, dep)


def resid_normmod(x, a, gate, g, sc, sh, *, name, tm=256):
    R = x.shape[0]

    def body(x_ref, a_ref, gate_ref, g_ref, sc_ref, sh_ref, x1_ref, h_ref):
        xv = x_ref[...] + gate_ref[...] * a_ref[...]
        x1_ref[...] = xv
        r = lax.rsqrt(jnp.mean(xv * xv, axis=-1, keepdims=True) + EPS)
        h_ref[...] = ((xv * r * g_ref[...]) * (1.0 + sc_ref[...]) + sh_ref[...]).astype(BF)

    blk = pl.BlockSpec((tm, D), lambda i: (i, 0))
    return pl.pallas_call(
        body, name=name, grid=(R // tm,), in_specs=[blk, blk, _row(D), _row(D), _row(D), _row(D)],
        out_specs=[blk, blk],
        out_shape=[jax.ShapeDtypeStruct((R, D), F32), jax.ShapeDtypeStruct((R, D), BF)],
        compiler_params=pltpu.CompilerParams(dimension_semantics=("parallel",)),
    )(x, a, gate, g, sc, sh)


def kvprep(pc, p, kvg, wkv2, ck, sk, *, name, tm=256):
    assert tm == TC
    nb = TKV // tm
    kvcol = O_KV // 512

    def body(pc_ref, p_ref, g_ref, w_ref, ck_ref, sk_ref, k_ref, v_ref, ckv_ref):
        i = pl.program_id(0)
        t = jnp.where(i == NLAT, pc_ref[...], p_ref[...])
        pk = t[:, :KVL]
        r = lax.rsqrt(jnp.mean(pk * pk, axis=-1, keepdims=True) + EPS)
        ckv = (pk * r * g_ref[...]).astype(BF)
        ckv_ref[...] = ckv
        kv2 = jnp.dot(ckv, w_ref[...], preferred_element_type=F32)
        krr = t[:, KVL:KVL + HP] * ck_ref[...] + t[:, KVL + HP:KVL + 2 * HP] * sk_ref[...]
        k_ref[...] = (kv2[:, :NH * HP] + jnp.concatenate([krr] * NH, axis=1)).astype(BF)
        v_ref[...] = kv2[:, NH * HP:].astype(BF)

    return pl.pallas_call(
        body, name=name, grid=(nb,),
        in_specs=[pl.BlockSpec((tm, 512), lambda i: (0, 0)),
                  pl.BlockSpec((tm, 512), lambda i: (jnp.minimum(i, NLAT - 1), kvcol)),
                  _row(KVL), pl.BlockSpec((KVL, NH * HP + NH * DV), lambda i: (0, 0)),
                  pl.BlockSpec((tm, HP), lambda i: (i, 0)), pl.BlockSpec((tm, HP), lambda i: (i, 0))],
        out_specs=[pl.BlockSpec((tm, NH * HP), lambda i: (i, 0)), pl.BlockSpec((tm, NH * DV), lambda i: (i, 0)),
                   pl.BlockSpec((tm, KVL), lambda i: (i, 0))],
        out_shape=[jax.ShapeDtypeStruct((TKV, NH * HP), BF), jax.ShapeDtypeStruct((TKV, NH * DV), BF),
                   jax.ShapeDtypeStruct((TKV, KVL), BF)],
        compiler_params=pltpu.CompilerParams(dimension_semantics=("parallel",)),
    )(pc, p, kvg, wkv2, ck, sk)


def qprep(p, qg, wq2, cq_t, sq_t, *, name, tm=256):
    qcol = O_Q // 512

    def body(p_ref, g_ref, w_ref, c_ref, s_ref, q_ref, cq_ref):
        pq = p_ref[...]
        r = lax.rsqrt(jnp.sum(pq * pq, axis=-1, keepdims=True) * (1.0 / QL) + EPS)
        cq = (pq * r * g_ref[...]).astype(BF)
        cq_ref[...] = cq
        q2 = jnp.dot(cq, w_ref[...], preferred_element_type=F32)
        cc = jnp.concatenate([c_ref[...]] * NH, axis=1)
        ss = jnp.concatenate([s_ref[...]] * NH, axis=1)
        q_ref[...] = (q2[:, :NH * HP] * cc + q2[:, NH * HP:] * ss).astype(BF)

    return pl.pallas_call(
        body, name=name, grid=(T // tm,),
        in_specs=[pl.BlockSpec((tm, 512), lambda i: (i, qcol)), _row(512),
                  pl.BlockSpec((512, 2 * NH * HP), lambda i: (0, 0)),
                  pl.BlockSpec((tm, HP), lambda i: (i, 0)), pl.BlockSpec((tm, HP), lambda i: (i, 0))],
        out_specs=[pl.BlockSpec((tm, NH * HP), lambda i: (i, 0)), pl.BlockSpec((tm, 512), lambda i: (i, 0))],
        out_shape=[jax.ShapeDtypeStruct((T, NH * HP), BF), jax.ShapeDtypeStruct((T, 512), BF)],
        compiler_params=pltpu.CompilerParams(dimension_semantics=("parallel",)),
    )(p, qg, wq2, cq_t, sq_t)


def _head_mask(h):
    lanes = lax.broadcasted_iota(jnp.int32, (1, 2 * DV), 1)
    return (lanes // DV) == (h % 2)


KC = 256
NKC = TKV // KC
LOG2E = 1.4426950408889634


def _scores_pass(q, k_ref, s_scr):
    m = None
    for c in range(NKC):
        s = lax.dot_general(q, k_ref[c * KC:(c + 1) * KC, :], (((1,), (1,)), ((), ())),
                            preferred_element_type=F32) * (SCALE * LOG2E)
        s_scr[:, c * KC:(c + 1) * KC] = s
        mc = jnp.max(s, axis=-1, keepdims=True)
        m = mc if m is None else jnp.maximum(m, mc)
    return m


def attn_fwd(q, k, v, *, name, tq=256):
    def body(q_ref, k_ref, v_ref, o_ref, s_scr):
        h = pl.program_id(1)
        m = _scores_pass(q_ref[...], k_ref, s_scr)
        l = jnp.zeros((tq, 1), F32)
        acc = jnp.zeros((tq, 2 * DV), F32)
        for c in range(NKC):
            e = jnp.exp2(s_scr[:, c * KC:(c + 1) * KC] - m)
            l = l + jnp.sum(e, axis=-1, keepdims=True)
            acc = acc + jnp.dot(e.astype(BF), v_ref[c * KC:(c + 1) * KC, :], preferred_element_type=F32)
        o2 = jnp.where(_head_mask(h), acc * (1.0 / l), 0.0).astype(BF)

        @pl.when(h % 2 == 0)
        def _():
            o_ref[...] = o2

        @pl.when(h % 2 == 1)
        def _():
            o_ref[...] = o_ref[...] + o2

    return pl.pallas_call(
        body, name=name, grid=(T // tq, NH),
        in_specs=[pl.BlockSpec((tq, HP), lambda i, h: (i, h)), pl.BlockSpec((TKV, HP), lambda i, h: (0, h)),
                  pl.BlockSpec((TKV, 2 * DV), lambda i, h: (0, h // 2))],
        out_specs=pl.BlockSpec((tq, 2 * DV), lambda i, h: (i, h // 2)),
        out_shape=jax.ShapeDtypeStruct((T, NH * DV), BF),
        scratch_shapes=[pltpu.VMEM((tq, TKV), F32)],
        compiler_params=pltpu.CompilerParams(dimension_semantics=("parallel", "arbitrary")),
    )(q, k, v)


def _shift_dn(x):
    n = x.shape[0]
    rows = lax.broadcasted_iota(jnp.int32, (n, 1), 0)
    return jnp.where(rows == 0, 0.0, pltpu.roll(x, 1, axis=0))


def _shift_up(x):
    n = x.shape[0]
    rows = lax.broadcasted_iota(jnp.int32, (n, 1), 0)
    return jnp.where(rows == n - 1, 0.0, pltpu.roll(x, n - 1, axis=0))


def _conv(x, w_ref, b_ref):
    return b_ref[...] + _shift_dn(x) * w_ref[0:1, :] + x * w_ref[1:2, :] + _shift_up(x) * w_ref[2:3, :]


def _conv_t(dy, w_ref):
    return _shift_up(dy) * w_ref[0:1, :] + dy * w_ref[1:2, :] + _shift_dn(dy) * w_ref[2:3, :]


def _conv_wgrad(dw_ref, dy, x):
    dw_ref[0:1, :] = jnp.sum(dy * _shift_dn(x), axis=0, keepdims=True)
    dw_ref[1:2, :] = jnp.sum(dy * x, axis=0, keepdims=True)
    dw_ref[2:3, :] = jnp.sum(dy * _shift_up(x), axis=0, keepdims=True)


def convz(p, cw, cb, *, name):
    o0 = O_CV // (3 * CVB)

    def body(p_ref, w_ref, bias_ref, z_ref):
        xv, bv, cv = p_ref[:, 0:CVB], p_ref[:, CVB:2 * CVB], p_ref[:, 2 * CVB:3 * CVB]
        z_ref[...] = (bv * _conv(cv * xv, w_ref, bias_ref)).astype(BF)

    return pl.pallas_call(
        body, name=name, grid=(CONV // CVB,),
        in_specs=[pl.BlockSpec((T, 3 * CVB), lambda j: (0, o0 + j)), pl.BlockSpec((3, CVB), lambda j: (0, j)),
                  pl.BlockSpec((1, CVB), lambda j: (0, j))],
        out_specs=pl.BlockSpec((T, CVB), lambda j: (0, j)),
        out_shape=jax.ShapeDtypeStruct((T, CONV), BF),
        compiler_params=pltpu.CompilerParams(dimension_semantics=("parallel",)),
    )(p, cw, cb)


def gate_merge(p, ya, yc, *, name, tm=256):
    def body(ga_ref, gc_ref, ya_ref, yc_ref, o_ref):
        o_ref[...] = (jax.nn.sigmoid(ga_ref[...]) * ya_ref[...] + jax.nn.sigmoid(gc_ref[...]) * yc_ref[...]).astype(BF)

    blk = pl.BlockSpec((tm, D), lambda i: (i, 0))
    return pl.pallas_call(
        body, name=name, grid=(T // tm,),
        in_specs=[pl.BlockSpec((tm, D), lambda i: (i, O_GA // D)), pl.BlockSpec((tm, D), lambda i: (i, O_GC // D)), blk, blk],
        out_specs=blk, out_shape=jax.ShapeDtypeStruct((T, D), BF),
        compiler_params=pltpu.CompilerParams(dimension_semantics=("parallel",)),
    )(p, p, ya, yc)


def ffn_act(u0, cw, cb, *, name, tc=256):
    nb = DFF // tc

    def body(u_ref, wg_ref, wv_ref, bg_ref, bv_ref, f_ref):
        ug = _conv(u_ref[0], wg_ref, bg_ref)
        uv = _conv(u_ref[1], wv_ref, bv_ref)
        f_ref[...] = (ug * jax.nn.sigmoid(ug) * uv).astype(BF)

    return pl.pallas_call(
        body, name=name, grid=(nb,),
        in_specs=[pl.BlockSpec((2, T, tc), lambda j: (0, 0, j)),
                  pl.BlockSpec((3, tc), lambda j: (0, j)), pl.BlockSpec((3, tc), lambda j: (0, nb + j)),
                  pl.BlockSpec((1, tc), lambda j: (0, j)), pl.BlockSpec((1, tc), lambda j: (0, nb + j))],
        out_specs=pl.BlockSpec((T, tc), lambda j: (0, j)),
        out_shape=jax.ShapeDtypeStruct((T, DFF), BF),
        compiler_params=pltpu.CompilerParams(dimension_semantics=("parallel",)),
    )(u0, cw, cw, cb, cb)


def final_loss(x1, d, g2, fg, tgt, *, name, tm=256):
    def body(x1_ref, d_ref, g2_ref, fg_ref, t_ref, dx_ref, dd_ref, dfg_ref, loss_ref):
        i = pl.program_id(0)
        xv = x1_ref[...] + g2_ref[...] * d_ref[...]
        r = lax.rsqrt(jnp.mean(xv * xv, axis=-1, keepdims=True) + EPS)
        xh = xv * r
        diff = xh * fg_ref[...] - t_ref[...]
        part = 0.5 * jnp.sum(jnp.mean(diff * diff, axis=-1, keepdims=True), axis=0, keepdims=True)
        dy = diff * (1.0 / D)
        a = dy * fg_ref[...]
        dx = r * (a - xh * jnp.mean(a * xh, axis=-1, keepdims=True))
        dx_ref[...] = dx
        dd_ref[...] = (dx * g2_ref[...]).astype(BF)
        dfg = jnp.sum(dy * xh, axis=0, keepdims=True)

        @pl.when(i == 0)
        def _():
            dfg_ref[...] = dfg
            loss_ref[...] = jnp.broadcast_to(part, (1, 128))

        @pl.when(i > 0)
        def _():
            dfg_ref[...] += dfg
            loss_ref[...] += jnp.broadcast_to(part, (1, 128))

    blk = pl.BlockSpec((tm, D), lambda i: (i, 0))
    return pl.pallas_call(
        body, name=name, grid=(T // tm,), in_specs=[blk, blk, _row(D), _row(D), blk],
        out_specs=[blk, blk, _row(D), _row(128)],
        out_shape=[jax.ShapeDtypeStruct((T, D), F32), jax.ShapeDtypeStruct((T, D), BF),
                   jax.ShapeDtypeStruct((1, D), F32), jax.ShapeDtypeStruct((1, 128), F32)],
        compiler_params=pltpu.CompilerParams(dimension_semantics=("arbitrary",)),
    )(x1, d, g2, fg, tgt)


def normmod_bwd(x, dh, g, sc, dres, gsrc, gate, *, name, tm=256):
    R = x.shape[0]
    has_res = dres is not None

    def body(*refs):
        if has_res:
            x_ref, dh_ref, g_ref, sc_ref, dres_ref, gsrc_ref, gate_ref, dx_ref, dxg_ref, st_ref = refs
        else:
            x_ref, dh_ref, g_ref, sc_ref, st_ref = refs
        i = pl.program_id(0)
        xv = x_ref[...]
        r = lax.rsqrt(jnp.mean(xv * xv, axis=-1, keepdims=True) + EPS)
        xh = xv * r
        dhv = dh_ref[...]
        n = xh * g_ref[...]
        dn = dhv * (1.0 + sc_ref[...])
        a = dn * g_ref[...]
        rows = [jnp.sum(dhv, axis=0, keepdims=True), jnp.sum(dhv * n, axis=0, keepdims=True),
                jnp.sum(dn * xh, axis=0, keepdims=True)]
        if has_res:
            dr = dres_ref[...]
            dx = dr + r * (a - xh * jnp.mean(a * xh, axis=-1, keepdims=True))
            dx_ref[...] = dx
            dxg_ref[...] = (dx * gate_ref[...]).astype(BF)
            rows.append(jnp.sum(dr * gsrc_ref[...], axis=0, keepdims=True))
        else:
            rows.append(jnp.zeros((1, D), F32))

        @pl.when(i == 0)
        def _():
            for k, row in enumerate(rows):
                st_ref[k:k + 1, :] = row

        @pl.when(i > 0)
        def _():
            for k, row in enumerate(rows):
                st_ref[k:k + 1, :] += row

    blk = pl.BlockSpec((tm, D), lambda i: (i, 0))
    st_spec = pl.BlockSpec((4, D), lambda i: (0, 0))
    st_shape = jax.ShapeDtypeStruct((4, D), F32)
    cp = pltpu.CompilerParams(dimension_semantics=("arbitrary",))
    if has_res:
        return pl.pallas_call(
            body, name=name, grid=(R // tm,), in_specs=[blk, blk, _row(D), _row(D), blk, blk, _row(D)],
            out_specs=[blk, blk, st_spec],
            out_shape=[jax.ShapeDtypeStruct((R, D), F32), jax.ShapeDtypeStruct((R, D), BF), st_shape],
            compiler_params=cp,
        )(x, dh, g, sc, dres, gsrc, gate)
    return pl.pallas_call(
        body, name=name, grid=(R // tm,), in_specs=[blk, blk, _row(D), _row(D)],
        out_specs=st_spec, out_shape=st_shape, compiler_params=cp,
    )(x, dh, g, sc)


def ffn_act_bwd(u0, df, cw, cb, *, name, tc=256):
    nb = DFF // tc

    def body(u_ref, df_ref, wg_ref, wv_ref, bg_ref, bv_ref, du_ref, dw_ref, db_ref):
        xg, xv = u_ref[0], u_ref[1]
        ug = _conv(xg, wg_ref, bg_ref)
        uv = _conv(xv, wv_ref, bv_ref)
        sig = jax.nn.sigmoid(ug)
        dfv = df_ref[...]
        dug = dfv * uv * (sig * (1.0 + ug * (1.0 - sig)))
        duv = dfv * (ug * sig)
        du_ref[0] = _conv_t(dug, wg_ref).astype(BF)
        du_ref[1] = _conv_t(duv, wv_ref).astype(BF)
        _conv_wgrad(dw_ref.at[0], dug, xg)
        _conv_wgrad(dw_ref.at[1], duv, xv)
        db_ref[0] = jnp.sum(dug, axis=0, keepdims=True)
        db_ref[1] = jnp.sum(duv, axis=0, keepdims=True)

    lo = lambda r: pl.BlockSpec((r, tc), lambda j: (0, j))
    hi = lambda r: pl.BlockSpec((r, tc), lambda j: (0, nb + j))
    st = lambda r: pl.BlockSpec((2, r, tc), lambda j: (0, 0, j))
    return pl.pallas_call(
        body, name=name, grid=(nb,),
        in_specs=[st(T), lo(T), lo(3), hi(3), lo(1), hi(1)],
        out_specs=[st(T), st(3), st(1)],
        out_shape=[jax.ShapeDtypeStruct((2, T, DFF), BF), jax.ShapeDtypeStruct((2, 3, DFF), F32),
                   jax.ShapeDtypeStruct((2, 1, DFF), F32)],
        compiler_params=pltpu.CompilerParams(dimension_semantics=("parallel",)),
    )(u0, df, cw, cw, cb, cb)


def gate_merge_bwd(p, ya, yc, dm, *, name, tm=256):
    def body(ga_ref, gc_ref, ya_ref, yc_ref, dm_ref, dya_ref, dyc_ref, dp_ref):
        sa, sc_ = jax.nn.sigmoid(ga_ref[...]), jax.nn.sigmoid(gc_ref[...])
        dmv = dm_ref[...]
        dya_ref[...] = (dmv * sa).astype(BF)
        dyc_ref[...] = (dmv * sc_).astype(BF)
        dp_ref[:, 0:D] = (dmv * ya_ref[...] * (sa * (1.0 - sa))).astype(BF)
        dp_ref[:, D:2 * D] = (dmv * yc_ref[...] * (sc_ * (1.0 - sc_))).astype(BF)

    blk = pl.BlockSpec((tm, D), lambda i: (i, 0))
    sh = jax.ShapeDtypeStruct((T, D), BF)
    return pl.pallas_call(
        body, name=name, grid=(T // tm,),
        in_specs=[pl.BlockSpec((tm, D), lambda i: (i, O_GA // D)), pl.BlockSpec((tm, D), lambda i: (i, O_GC // D)), blk, blk, blk],
        out_specs=[blk, blk, pl.BlockSpec((tm, 2 * D), lambda i: (i, 0))],
        out_shape=[sh, sh, jax.ShapeDtypeStruct((T, NIN), BF)],
        compiler_params=pltpu.CompilerParams(dimension_semantics=("parallel",)),
    )(p, p, ya, yc, dm)


def convz_bwd(p, dz, cw, cb, dp, *, name):
    o0 = O_CV // (3 * CVB)

    def body(p_ref, dz_ref, w_ref, bias_ref, dp_in, dp_ref, dw_ref, dbias_ref):
        xv, bv, cv = p_ref[:, 0:CVB], p_ref[:, CVB:2 * CVB], p_ref[:, 2 * CVB:3 * CVB]
        ci = cv * xv
        dwc = _conv(ci, w_ref, bias_ref)
        dzv = dz_ref[...]
        ddw = dzv * bv
        dci = _conv_t(ddw, w_ref)
        dp_ref[:, 0:CVB] = (dci * cv).astype(BF)
        dp_ref[:, CVB:2 * CVB] = (dzv * dwc).astype(BF)
        dp_ref[:, 2 * CVB:3 * CVB] = (dci * xv).astype(BF)
        _conv_wgrad(dw_ref, ddw, ci)
        dbias_ref[...] = jnp.sum(ddw, axis=0, keepdims=True)

    own = lambda r: pl.BlockSpec((r, CVB), lambda j: (0, j))
    return pl.pallas_call(
        body, name=name, grid=(CONV // CVB,),
        in_specs=[pl.BlockSpec((T, 3 * CVB), lambda j: (0, o0 + j)), own(T), own(3), own(1),
                  pl.BlockSpec(memory_space=pl.ANY)],
        out_specs=[pl.BlockSpec((T, 3 * CVB), lambda j: (0, o0 + j)), own(3), own(1)],
        out_shape=[jax.ShapeDtypeStruct((T, NIN), BF), jax.ShapeDtypeStruct((3, CONV), F32),
                   jax.ShapeDtypeStruct((1, CONV), F32)],
        input_output_aliases={4: 0},
        compiler_params=pltpu.CompilerParams(dimension_semantics=("parallel",)),
    )(p, dz, cw, cb, dp)


def attn_bwd(q, k, v, do, *, name, tq=256):
    def body(q_ref, k_ref, v_ref, do_ref, dq_ref, dk_ref, dv_ref, s_scr, dp_scr):
        h, i = pl.program_id(0), pl.program_id(1)

        @pl.when(i == 0)
        def _():
            dk_ref[...] = jnp.zeros_like(dk_ref)

        @pl.when((i == 0) & (h % 2 == 0))
        def _():
            dv_ref[...] = jnp.zeros_like(dv_ref)

        qv = q_ref[...]
        dom = jnp.where(_head_mask(h), do_ref[...], jnp.zeros_like(do_ref[...]))
        m = _scores_pass(qv, k_ref, s_scr)
        l = jnp.zeros((tq, 1), F32)
        dsum = jnp.zeros((tq, 1), F32)
        for c in range(NKC):
            cols = slice(c * KC, (c + 1) * KC)
            e = jnp.exp2(s_scr[:, cols] - m)
            s_scr[:, cols] = e
            dp = lax.dot_general(dom, v_ref[cols, :], (((1,), (1,)), ((), ())), preferred_element_type=F32)
            dp_scr[:, cols] = dp
            l = l + jnp.sum(e, axis=-1, keepdims=True)
            dsum = dsum + jnp.sum(e * dp, axis=-1, keepdims=True)
        inv = 1.0 / l
        delta = dsum * inv
        dos = (dom.astype(F32) * inv).astype(BF)
        dq = jnp.zeros((tq, HP), F32)
        for c in range(NKC):
            cols = slice(c * KC, (c + 1) * KC)
            e = s_scr[:, cols]
            ds = (e * (dp_scr[:, cols] - delta) * (inv * SCALE)).astype(BF)
            dq = dq + jnp.dot(ds, k_ref[cols, :], preferred_element_type=F32)
            dk_ref[cols, :] += lax.dot_general(ds, qv, (((0,), (0,)), ((), ())), preferred_element_type=F32)
            dv_ref[cols, :] += lax.dot_general(e.astype(BF), dos, (((0,), (0,)), ((), ())), preferred_element_type=F32)
        dq_ref[...] = dq

    return pl.pallas_call(
        body, name=name, grid=(NH, T // tq),
        in_specs=[pl.BlockSpec((tq, HP), lambda h, i: (i, h)), pl.BlockSpec((TKV, HP), lambda h, i: (0, h)),
                  pl.BlockSpec((TKV, 2 * DV), lambda h, i: (0, h // 2)), pl.BlockSpec((tq, 2 * DV), lambda h, i: (i, h // 2))],
        out_specs=[pl.BlockSpec((tq, HP), lambda h, i: (i, h)), pl.BlockSpec((TKV, HP), lambda h, i: (0, h)),
                   pl.BlockSpec((TKV, 2 * DV), lambda h, i: (0, h // 2))],
        out_shape=[jax.ShapeDtypeStruct((T, NH * HP), F32), jax.ShapeDtypeStruct((TKV, NH * HP), F32),
                   jax.ShapeDtypeStruct((TKV, NH * DV), F32)],
        scratch_shapes=[pltpu.VMEM((tq, TKV), F32), pltpu.VMEM((tq, TKV), F32)],
        compiler_params=pltpu.CompilerParams(dimension_semantics=("arbitrary", "arbitrary")),
    )(q, k, v, do)


def qprep_bwd(p, dq, qg, wq2, cq_t, sq_t, dp, *, name, tm=256):
    qcol = O_Q // 512

    def body(p_ref, dq_ref, g_ref, w_ref, c_ref, s_ref, dp_in, dp_ref, dq2_ref, dg_ref):
        i = pl.program_id(0)
        dqv = dq_ref[...]
        cc = jnp.concatenate([c_ref[...]] * NH, axis=1)
        ss = jnp.concatenate([s_ref[...]] * NH, axis=1)
        dq2 = jnp.concatenate([dqv * cc, dqv * ss], axis=1).astype(BF)
        dq2_ref[...] = dq2
        dcq = lax.dot_general(dq2, w_ref[...], (((1,), (1,)), ((), ())), preferred_element_type=F32)
        pq = p_ref[...]
        r = lax.rsqrt(jnp.sum(pq * pq, axis=-1, keepdims=True) * (1.0 / QL) + EPS)
        xh = pq * r
        a = dcq * g_ref[...]
        dp_ref[...] = (r * (a - xh * (jnp.sum(a * xh, axis=-1, keepdims=True) * (1.0 / QL)))).astype(BF)
        dg = jnp.sum(dcq * xh, axis=0, keepdims=True)

        @pl.when(i == 0)
        def _():
            dg_ref[...] = dg

        @pl.when(i > 0)
        def _():
            dg_ref[...] += dg

    return pl.pallas_call(
        body, name=name, grid=(T // tm,),
        in_specs=[pl.BlockSpec((tm, 512), lambda i: (i, qcol)), pl.BlockSpec((tm, NH * HP), lambda i: (i, 0)), _row(512),
                  pl.BlockSpec((512, 2 * NH * HP), lambda i: (0, 0)),
                  pl.BlockSpec((tm, HP), lambda i: (i, 0)), pl.BlockSpec((tm, HP), lambda i: (i, 0)),
                  pl.BlockSpec(memory_space=pl.ANY)],
        out_specs=[pl.BlockSpec((tm, 512), lambda i: (i, qcol)), pl.BlockSpec((tm, 2 * NH * HP), lambda i: (i, 0)), _row(512)],
        out_shape=[jax.ShapeDtypeStruct((T, NIN), BF), jax.ShapeDtypeStruct((T, 2 * NH * HP), BF),
                   jax.ShapeDtypeStruct((1, 512), F32)],
        input_output_aliases={6: 0},
        compiler_params=pltpu.CompilerParams(dimension_semantics=("arbitrary",)),
    )(p, dq, qg, wq2, cq_t, sq_t, dp)


def kvprep_bwd(pc, p, dk, dv, kvg, wkv2, ck, sk, dp, *, name, tm=256):
    assert tm == TC
    nb = TKV // tm
    kvcol = O_KV // 512

    def body(pc_ref, p_ref, dk_ref, dv_ref, g_ref, w_ref, ck_ref, sk_ref, dp_in, dp_ref, dpc_ref, dkv2_ref, dg_ref):
        i = pl.program_id(0)
        t = jnp.where(i == NLAT, pc_ref[...], p_ref[...])
        pk = t[:, :KVL]
        r = lax.rsqrt(jnp.mean(pk * pk, axis=-1, keepdims=True) + EPS)
        xh = pk * r
        dkv = dk_ref[...]
        dkv2 = jnp.concatenate([dkv, dv_ref[...]], axis=1).astype(BF)
        dkv2_ref[...] = dkv2
        dckv = lax.dot_general(dkv2, w_ref[...], (((1,), (1,)), ((), ())), preferred_element_type=F32)
        a = dckv * g_ref[...]
        dpk = r * (a - xh * jnp.mean(a * xh, axis=-1, keepdims=True))
        dkr = dkv[:, 0:HP]
        for hh in range(1, NH):
            dkr = dkr + dkv[:, hh * HP:(hh + 1) * HP]
        res = jnp.concatenate([dpk, dkr * ck_ref[...], dkr * sk_ref[...]], axis=1).astype(BF)
        dg = jnp.sum(dckv * xh, axis=0, keepdims=True)

        @pl.when(i == 0)
        def _():
            dg_ref[...] = dg

        @pl.when(i > 0)
        def _():
            dg_ref[...] += dg

        @pl.when(i < NLAT)
        def _():
            dp_ref[...] = res

        @pl.when(i == NLAT)
        def _():
            dpc_ref[...] = res

    rb = lambda w: pl.BlockSpec((tm, w), lambda i: (i, 0))
    return pl.pallas_call(
        body, name=name, grid=(nb,),
        in_specs=[pl.BlockSpec((tm, 512), lambda i: (0, 0)),
                  pl.BlockSpec((tm, 512), lambda i: (jnp.minimum(i, NLAT - 1), kvcol)),
                  rb(NH * HP), rb(NH * DV), _row(KVL), pl.BlockSpec((KVL, NH * HP + NH * DV), lambda i: (0, 0)),
                  rb(HP), rb(HP), pl.BlockSpec(memory_space=pl.ANY)],
        out_specs=[pl.BlockSpec((tm, 512), lambda i: (jnp.minimum(i, NLAT - 1), kvcol)),
                   pl.BlockSpec((tm, 512), lambda i: (0, 0)), rb(NH * HP + NH * DV), _row(KVL)],
        out_shape=[jax.ShapeDtypeStruct((T, NIN), BF), jax.ShapeDtypeStruct((TC, 512), BF),
                   jax.ShapeDtypeStruct((TKV, NH * HP + NH * DV), BF), jax.ShapeDtypeStruct((1, KVL), F32)],
        input_output_aliases={8: 0},
        compiler_params=pltpu.CompilerParams(dimension_semantics=("arbitrary",)),
    )(pc, p, dk, dv, kvg, wkv2, ck, sk, dp)


def _pieces(src, width, n):
    out, c = [], src
    while c < src + width:
        k = c // n
        w = min(src + width, (k + 1) * n) - c
        out.append((k, c - k * n, c - src, w))
        c += w
    return out


def _win_moves():
    mv = [(2208, 1024, O_GA), (3232, 1024, O_GC), (0, KVL, O_KV), (256, DR, O_KV + KVL + DN), (288, QL, O_Q)]
    mv += [(256 + _swap_start(g), 8, O_KV + KVL + HP + DN + 8 * g) for g in range(4)]
    for j in range(CONV // CVB):
        base = O_CV + 3 * CVB * j
        mv += [(672 + CVB * j, CVB, base), (1184 + CVB * j, CVB, base + CVB), (1696 + CVB * j, CVB, base + 2 * CVB)]
    return mv


_WIN_ZERO = [(O_KV + KVL, DN), (O_KV + KVL + DN + DR, HP - DN - DR), (O_KV + KVL + HP, DN),
             (O_KV + KVL + HP + DN + DR, HP - DN - DR), (O_Q + QL, 512 - QL)]


def build_win(g, *, name, tm=256):
    def body(g_ref, o_ref):
        for src, w, dst in _win_moves():
            for k, a, off, pw in _pieces(src, w, SH_IN):
                o_ref[:, dst + off:dst + off + pw] = g_ref[k, :, a:a + pw]
        for c0, w in _WIN_ZERO:
            o_ref[:, c0:c0 + w] = jnp.zeros((tm, w), o_ref.dtype)

    return pl.pallas_call(
        body, name=name, grid=(D // tm,), in_specs=[pl.BlockSpec((NDEV, tm, SH_IN), lambda i: (0, i, 0))],
        out_specs=pl.BlockSpec((tm, NIN), lambda i: (i, 0)), out_shape=jax.ShapeDtypeStruct((D, NIN), g.dtype),
        compiler_params=pltpu.CompilerParams(dimension_semantics=("parallel",)),
    )(g)


def shard_win_grad(dwt, dwct, *, name, tc=256):
    def body(dw_ref, dwc_ref, o_ref, kvs):
        kvs[...] = dw_ref[O_KV:O_KV + 512, :] + dwc_ref[...]

        def src(row, w):
            if O_KV <= row < O_KV + 512:
                return kvs[row - O_KV:row - O_KV + w, :]
            return dw_ref[row:row + w, :]

        for s, w, dst in _win_moves():
            if w == 8 or s == 256:
                continue
            for k, a, off, pw in _pieces(s, w, SH_IN):
                o_ref[k, a:a + pw, :] = src(dst + off, pw).astype(o_ref.dtype)
        for g in range(4):
            val = src(O_KV + KVL + DN + 8 * g, 8) + src(O_KV + KVL + HP + DN + _swap_start(g), 8)
            o_ref[0, 256 + 8 * g:256 + 8 * g + 8, :] = val.astype(o_ref.dtype)

    return pl.pallas_call(
        body, name=name, grid=(D // tc,),
        in_specs=[pl.BlockSpec((NIN, tc), lambda j: (0, j)), pl.BlockSpec((512, tc), lambda j: (0, j))],
        out_specs=pl.BlockSpec((NDEV, SH_IN, tc), lambda j: (0, 0, j)),
        out_shape=jax.ShapeDtypeStruct((NDEV, SH_IN, D), BF),
        scratch_shapes=[pltpu.VMEM((512, tc), F32)],
        compiler_params=pltpu.CompilerParams(dimension_semantics=("parallel",)),
    )(dwt, dwct)


def build_wq_wkv(gq, gkv, *, name):
    def body(gq_ref, gkv_ref, q_ref, kv_ref):
        q_ref[...] = jnp.zeros_like(q_ref)
        kv_ref[...] = jnp.zeros_like(kv_ref)
        for h in range(NH):
            q_ref[0:QL, h * HP:h * HP + DN + DR] = gq_ref[h]
            for g in range(4):
                c0 = NH * HP + h * HP + DN + 8 * g
                q_ref[0:QL, c0:c0 + 8] = gq_ref[h, :, DN + _swap_start(g):DN + _swap_start(g) + 8]
            kv_ref[:, h * HP:h * HP + DN] = gkv_ref[h, :, 0:DN]
            kv_ref[:, NH * HP + h * DV:NH * HP + (h + 1) * DV] = gkv_ref[h, :, DN:DN + DV]

    vm = pl.BlockSpec(memory_space=pltpu.VMEM)
    return pl.pallas_call(
        body, name=name, in_specs=[vm, vm], out_specs=[vm, vm],
        out_shape=[jax.ShapeDtypeStruct((512, 2 * NH * HP), gq.dtype), jax.ShapeDtypeStruct((KVL, NH * HP + NH * DV), gq.dtype)],
    )(gq, gkv)


def shard_wq_wkv_grad(dwq2, dwkv2, *, name):
    def body(q_ref, kv_ref, gq_ref, gkv_ref):
        for h in range(NH):
            gq_ref[h, :, 0:DN] = q_ref[0:QL, h * HP:h * HP + DN].astype(BF)
            for g in range(4):
                a = q_ref[0:QL, h * HP + DN + 8 * g:h * HP + DN + 8 * g + 8]
                c0 = NH * HP + h * HP + DN + _swap_start(g)
                gq_ref[h, :, DN + 8 * g:DN + 8 * g + 8] = (a + q_ref[0:QL, c0:c0 + 8]).astype(BF)
            gkv_ref[h, :, 0:DN] = kv_ref[:, h * HP:h * HP + DN].astype(BF)
            gkv_ref[h, :, DN:DN + DV] = kv_ref[:, NH * HP + h * DV:NH * HP + (h + 1) * DV].astype(BF)

    vm = pl.BlockSpec(memory_space=pltpu.VMEM)
    return pl.pallas_call(
        body, name=name, in_specs=[vm, vm], out_specs=[vm, vm],
        out_shape=[jax.ShapeDtypeStruct((NDEV, QL, (DN + DR)), BF), jax.ShapeDtypeStruct((NDEV, KVL, DN + DV), BF)],
    )(dwq2, dwkv2)


def unshard_cols(g, *, name, tm=256):
    _, K, n = g.shape
    tm = _pick(K, tm, 16)

    def body(g_ref, o_ref):
        for k in range(NDEV):
            o_ref[:, k * n:(k + 1) * n] = g_ref[k]

    return pl.pallas_call(
        body, name=name, grid=(K // tm,), in_specs=[pl.BlockSpec((NDEV, tm, n), lambda i: (0, i, 0))],
        out_specs=pl.BlockSpec((tm, NDEV * n), lambda i: (i, 0)), out_shape=jax.ShapeDtypeStruct((K, NDEV * n), g.dtype),
        compiler_params=pltpu.CompilerParams(dimension_semantics=("parallel",)),
    )(g)


def shard_cols(w, *, name, tm=256):
    K, n8 = w.shape
    n = n8 // NDEV
    tm = _pick(K, tm, 16)

    def body(w_ref, o_ref):
        for k in range(NDEV):
            o_ref[k] = w_ref[:, k * n:(k + 1) * n]

    return pl.pallas_call(
        body, name=name, grid=(K // tm,), in_specs=[pl.BlockSpec((tm, n8), lambda i: (i, 0))],
        out_specs=pl.BlockSpec((NDEV, tm, n), lambda i: (0, i, 0)), out_shape=jax.ShapeDtypeStruct((NDEV, K, n), w.dtype),
        compiler_params=pltpu.CompilerParams(dimension_semantics=("parallel",)),
    )(w)


def _rope_tables():
    t = np.arange(T)
    row = (t // GRID_W).astype(np.float32)
    col = (t % GRID_W).astype(np.float32)
    axis_dim = DR // 2
    inv = (np.float32(ROPE_THETA) ** (-np.arange(0, axis_dim, 2, dtype=np.float32) / np.float32(axis_dim))).astype(np.float32)
    ar, ac = (row[:, None] * inv).astype(np.float32), (col[:, None] * inv).astype(np.float32)
    cosv = np.concatenate([np.cos(ar), np.cos(ar), np.cos(ac), np.cos(ac)], axis=1).astype(np.float32)
    sinv = np.concatenate([-np.sin(ar), np.sin(ar), -np.sin(ac), np.sin(ac)], axis=1).astype(np.float32)
    ck = np.zeros((TKV, HP), np.float32)
    sk = np.zeros((TKV, HP), np.float32)
    ck[T:, DN:DN + DR] = 1.0
    ck[:T, DN:DN + DR] = cosv
    sk[:T, DN:DN + DR] = sinv
    cq = np.zeros((T, HP), np.float32)
    cq[:, :DN] = 1.0
    cq[:, DN:DN + DR] = cosv
    return jnp.asarray(ck), jnp.asarray(sk), jnp.asarray(cq), jnp.asarray(sk[:T])


def _local_step(x, ctx, tgt, mod_lat, mod_ctx, n1g, qg, kvg, n2g, fg, conv_w, conv_b, ffn_w, ffn_b, get_w, put_g, dep0):
    sh1, sc1, g1, sh2, sc2, g2 = [mod_lat[:, i * D:(i + 1) * D] for i in range(6)]
    csh1, csc1 = mod_ctx[:, 0:D], mod_ctx[:, D:2 * D]
    ck, sk, cq_t, sq_t = _rope_tables()
    qg_p = jnp.pad(qg, ((0, 0), (0, 512 - QL)))

    hcat = normmod_cat(ctx, x, n1g, csc1, csh1, sc1, sh1, dep0, name="normmod1")
    win = get_w("in", hcat)
    p = mm(hcat, win, M=T, tn=768, name="in_proj")
    pc = mm(hcat, win, M=TC, N=512, a_off=(T, 0), b_off=(0, O_KV), name="in_proj_ctx")
    wq2, wkv2, wao, wco, wo = get_w("mid", p)
    kh, vh, ckv = kvprep(pc, p, kvg, wkv2, ck, sk, name="kvprep")
    qr, cq = qprep(p, qg_p, wq2, cq_t, sq_t, name="qprep")
    o = attn_fwd(qr, kh, vh, name="attn_fwd")
    z = convz(p, conv_w, conv_b, name="convz")
    ya = mm(o, wao, name="attn_out")
    yc = mm(z, wco, name="conv_out")
    merged = gate_merge(p, ya, yc, name="gate_merge")
    a_out = mm(merged, wo, name="o_proj")
    x1, h2 = resid_normmod(x, a_out, g1, n2g, sc2, sh2, name="resid_normmod2")
    wup, wdn = get_w("ffn", h2)
    u0 = mm(h2, wup, tb=True, o_stack=True, tn=1408, name="up_proj")
    f = ffn_act(u0, ffn_w, ffn_b, name="ffn_act")
    dn = mm(f, wdn, tm=512, tk=DFF, name="down_proj")
    dx2, dd, dfg, loss = final_loss(x1, dn, g2, fg, tgt, name="final_loss")

    df = mm(dd, wdn, tb=True, tn=1408, name="down_proj_dx")
    dwdn = mm(f, dd, ta=True, out_dtype=BF, tm=1408, name="down_proj_dw")
    du0, dffn_w, dffn_b = ffn_act_bwd(u0, df, ffn_w, ffn_b, name="ffn_act_bwd")
    dwup = mm(du0, h2, ta=True, a_stack=True, out_dtype=BF, tm=1408, name="up_proj_dw")
    tok = put_g("ffn", dict(dwup=dwup, dwdn=dwdn))
    dh2 = mm(du0, wup, a_stack=True, dep=tok, name="up_proj_dx")
    dx1, da, st2 = normmod_bwd(x1, dh2, n2g, sc2, dx2, dn, g1, name="normmod2_bwd")

    dmerged = mm(da, wo, tb=True, name="o_proj_dx")
    dwo = mm(merged, da, ta=True, out_dtype=BF, tn=512, name="o_proj_dw")
    dya, dyc, dp = gate_merge_bwd(p, ya, yc, dmerged, name="gate_merge_bwd")
    do = mm(dya, wao, tb=True, out_dtype=BF, name="attn_out_dx")
    dwao = mm(o, dya, ta=True, out_dtype=BF, tn=512, name="attn_out_dw")
    dwco = mm(z, dyc, ta=True, out_dtype=BF, tn=512, name="conv_out_dw")
    tok = put_g("mid", dict(dwao=dwao, dwco=dwco, dwo=dwo))
    dz = mm(dyc, wco, tb=True, dep=tok, name="conv_out_dx")
    dp, dconv_w, dconv_b = convz_bwd(p, dz, conv_w, conv_b, dp, name="convz_bwd")
    dq, dk, dv = attn_bwd(qr, kh, vh, do, name="attn_bwd")
    dp, dq2, dqg = qprep_bwd(p, dq, qg_p, wq2, cq_t, sq_t, dp, name="qprep_bwd")
    dwq2 = mm(cq, dq2, ta=True, name="q_up_dw")
    dp, dpc, dkv2, dkvg = kvprep_bwd(pc, p, dk, dv, kvg, wkv2, ck, sk, dp, name="kvprep_bwd")
    dwkv2 = mm(ckv, dkv2, ta=True, name="kv_up_dw")
    tok = put_g("qkv", dict(dwq2=dwq2, dwkv2=dwkv2))

    dwin = mm(dp, hcat, ta=True, K=T, tm=768, dep=tok, name="in_proj_dw")
    dwin_c = mm(dpc, hcat, ta=True, K=TC, b_off=(T, 0), name="in_proj_ctx_dw")
    tok = put_g("in", dict(dwin=dwin, dwin_c=dwin_c))
    dh = mm(dp, win, tb=True, dep=tok, name="in_proj_dx")
    dhc = mm(dpc, win, tb=True, N=D, K=512, b_off=(0, O_KV), name="in_proj_ctx_dx")
    dx, _, st1 = normmod_bwd(x, dh, n1g, sc1, dx1, a_out, g1, name="normmod1_bwd")
    stc = normmod_bwd(ctx, dhc, n1g, csc1, None, None, None, name="normmod1_ctx_bwd")

    zrow = jnp.zeros((1, D), F32)
    dmod_lat = jnp.concatenate([st1[0:1], st1[1:2], st1[3:4], st2[0:1], st2[1:2], st2[3:4]], axis=1)
    dmod_ctx = jnp.concatenate([stc[0:1], stc[1:2], zrow, zrow, zrow, zrow], axis=1)
    return dict(
        loss=loss, dx=dx, dmod_lat=dmod_lat, dmod_ctx=dmod_ctx,
        dn1g=st1[2:3] + stc[2:3], dqg=dqg, dkvg=dkvg, dn2g=st2[2:3], dfg=dfg,
        dconv_w=dconv_w, dconv_b=dconv_b, dffn_w=dffn_w, dffn_b=dffn_b)


def _me():
    x, y, c = lax.axis_index("x"), lax.axis_index("y"), lax.axis_index("c")
    return x, y, c, 4 * x + 2 * y + c


def _peer(x, y, c, k):
    px = 1 - x if k & 4 else x
    py = 1 - y if k & 2 else y
    pc = 1 - c if k & 1 else c
    return (px, py, pc), 4 * px + 2 * py + pc


def _exchange_tiles(src_of_peer, buf, send_sem, recv_sem):
    x, y, c, me = _me()
    for k in range(1, NDEV):
        dev, lin = _peer(x, y, c, k)
        pltpu.make_async_remote_copy(src_ref=src_of_peer(lin), dst_ref=buf.at[me], send_sem=send_sem, recv_sem=recv_sem,
                                     device_id=dev, device_id_type=MESH).start()
    seven = buf.at[pl.ds(0, NDEV - 1)]
    pltpu.make_async_remote_copy(src_ref=seven, dst_ref=seven, send_sem=send_sem, recv_sem=recv_sem,
                                 device_id=(x, y, c), device_id_type=MESH).wait()


def _silu(z):
    return z * jax.nn.sigmoid(z)


def ada_fwd(c, c_ctx, ffn_w, conv_w, w_shard, b_shard, deps, *, name):
    nsh = w_shard.shape[1]
    deps = [d for d in deps if d is not None]

    def body(c_ref, cc_ref, fw_ref, cw_ref, w_ref, b_ref, *rest):
        s_ref, m_ref, mine, res, sems = rest[len(deps):]
        x, y, c, me = _me()
        mine[0:1, :] = _silu(c_ref[...])
        mine[1:2, :] = _silu(cc_ref[...])
        mine[2:5, :] = fw_ref[...]
        mine[5:8, :] = cw_ref[...]
        s_ref[me] = mine[...]
        _exchange_tiles(lambda lin: mine, s_ref, sems.at[0], sems.at[1])
        sall = s_ref[...].reshape(NDEV * 8, D).astype(BF)
        r = jnp.dot(sall, w_ref[...].astype(BF), preferred_element_type=F32) + b_ref[...]
        res[...] = r.reshape(NDEV, 8, nsh)
        m_ref[me] = res[me]
        _exchange_tiles(lambda lin: res.at[lin], m_ref, sems.at[2], sems.at[3])

    vm = pl.BlockSpec(memory_space=pltpu.VMEM)
    return pl.pallas_call(
        body, name=name, in_specs=[vm] * 6 + [pl.BlockSpec(memory_space=pl.ANY)] * len(deps), out_specs=[vm, vm],
        out_shape=[jax.ShapeDtypeStruct((NDEV, 8, D), F32), jax.ShapeDtypeStruct((NDEV, 8, nsh), F32)],
        scratch_shapes=[pltpu.VMEM((8, D), F32), pltpu.VMEM((NDEV, 8, nsh), F32), pltpu.SemaphoreType.DMA((4,))],
    )(c, c_ctx, ffn_w, conv_w, w_shard, b_shard, *deps)


P_DML, P_DMC, P_N1, P_QG, P_KVG, P_CB, P_N2, P_FB, P_FG, P_CW, P_FW, P_LOSS, P_ROWS = 0, 6, 12, 13, 14, 15, 16, 17, 23, 24, 27, 45, 48
FROWS = 3


def sync_small(r, deps, *, name):
    ins = [r["dmod_lat"], r["dmod_ctx"], r["dn1g"], r["dqg"], r["dkvg"], r["dconv_b"], r["dn2g"], r["dffn_b"], r["dfg"],
           r["dconv_w"], r["dffn_w"], r["loss"]]

    def put_wide(p, row0, row, n):
        for j in range(-(-n // D)):
            w = min(D, n - j * D)
            p[row0 + j:row0 + j + 1, 0:w] = row[:, j * D:j * D + w]

    def body(dml, dmc, n1, qg, kvg, cb, n2, fb, fg, cw, fw, loss, *rest):
        a_ref, sum_ref, p, sems = rest[len(deps):]
        x, y, c, me = _me()
        p[...] = jnp.zeros_like(p)
        put_wide(p, P_DML, dml, 6 * D)
        put_wide(p, P_DMC, dmc, 6 * D)
        put_wide(p, P_N1, n1, D)
        put_wide(p, P_QG, qg, 512)
        put_wide(p, P_KVG, kvg, KVL)
        put_wide(p, P_CB, cb, CONV)
        put_wide(p, P_N2, n2, D)
        put_wide(p, P_FG, fg, D)
        put_wide(p, P_LOSS, loss, 128)
        for s in range(2):
            put_wide(p, P_FB + FROWS * s, fb.at[s], DFF)
        for k in range(3):
            put_wide(p, P_CW + k, cw.at[k:k + 1], CONV)
            for s in range(2):
                put_wide(p, P_FW + FROWS * (2 * k + s), fw.at[s, k:k + 1], DFF)
        a_ref[me] = p[...]
        _exchange_tiles(lambda lin: p, a_ref, sems.at[0], sems.at[1])
        acc = a_ref[0]
        for k in range(1, NDEV):
            acc = acc + a_ref[k]
        sum_ref[...] = acc

    vm = pl.BlockSpec(memory_space=pltpu.VMEM)
    return pl.pallas_call(
        body, name=name, in_specs=[vm] * len(ins) + [pl.BlockSpec(memory_space=pl.ANY)] * len(deps), out_specs=[vm, vm],
        out_shape=[jax.ShapeDtypeStruct((NDEV, P_ROWS, D), F32), jax.ShapeDtypeStruct((P_ROWS, D), F32)],
        scratch_shapes=[pltpu.VMEM((P_ROWS, D), F32), pltpu.SemaphoreType.DMA((2,))],
    )(*ins, *deps)


def ada_bwd(s_all, dml, dmc, w_shard, c_ctx, *, name):
    nsh = w_shard.shape[1]

    def body(s_ref, dml_ref, dmc_ref, w_ref, c_ref, dw_ref, gc_ref, s16, dm16, part, buf, sems):
        x, y, c, me = _me()
        s16[...] = jnp.zeros_like(s16)
        dm16[...] = jnp.zeros_like(dm16)
        for k in range(NDEV):
            s16[k:k + 1, :] = s_ref[k, 0:1, :]
        s16[8:9, :] = s_ref[0, 1:2, :]
        dm16[0:8, :] = dml_ref[...]
        dm16[8:9, :] = dmc_ref[...]
        dw_ref[...] = lax.dot_general(s16[...].astype(BF), dm16[...].astype(BF), (((0,), (0,)), ((), ())),
                                      preferred_element_type=F32)
        part[...] = lax.dot_general(dm16[8:16, :].astype(BF), w_ref[...].astype(BF), (((1,), (1,)), ((), ())),
                                    preferred_element_type=F32)
        buf[me] = part[...]
        _exchange_tiles(lambda lin: part, buf, sems.at[0], sems.at[1])
        acc = buf[0]
        for k in range(1, NDEV):
            acc = acc + buf[k]
        z = c_ref[...]
        sg = jax.nn.sigmoid(z)
        gc_ref[...] = acc * (sg * (1.0 + z * (1.0 - sg)))

    vm = pl.BlockSpec(memory_space=pltpu.VMEM)
    return pl.pallas_call(
        body, name=name, in_specs=[vm] * 5, out_specs=[vm, vm],
        out_shape=[jax.ShapeDtypeStruct((D, nsh), F32), jax.ShapeDtypeStruct((8, D), F32)],
        scratch_shapes=[pltpu.VMEM((16, D), F32), pltpu.VMEM((16, nsh), F32), pltpu.VMEM((8, D), F32),
                        pltpu.VMEM((NDEV, 8, D), F32), pltpu.SemaphoreType.DMA((2,))],
    )(s_all, dml, dmc, w_shard, c_ctx)


HBM_SPEC = pl.BlockSpec(memory_space=pltpu.HBM)
SEM_SPEC = pl.BlockSpec(memory_space=pltpu.SEMAPHORE)
EFFECT = pltpu.SideEffectType.DATAFLOW_SIDE_EFFECTING


ALL_PEERS = tuple(range(1, NDEV))
FIRST_HOP = (1, 2, 4, 6)
RELAY = (2, 4, 6)


def _exchange_copies(srcs, lands, send, recv, per_peer, peers):
    x, y, c, me = _me()
    n = len(peers)
    cps = []
    for t in range(len(srcs)):
        for j, k in enumerate(peers):
            dev, lin = _peer(x, y, c, k)
            cps.append(pltpu.make_async_remote_copy(
                src_ref=srcs[t].at[lin] if per_peer else srcs[t], dst_ref=lands[t].at[me],
                send_sem=send.at[n * t + j], recv_sem=recv.at[n * t + j], device_id=dev, device_id_type=MESH))
    return cps


def _relay_copies(lands, send, recv):
    x, y, c, me = _me()
    n = len(RELAY)
    cps = []
    for t in range(len(lands)):
        for j, k in enumerate(RELAY):
            slot = lands[t].at[_peer(x, y, c, k)[1]]
            cps.append(pltpu.make_async_remote_copy(
                src_ref=slot, dst_ref=slot, send_sem=send.at[n * t + j], recv_sem=recv.at[n * t + j],
                device_id=(x, y, 1 - c), device_id_type=MESH))
    return cps


def _own_copies(srcs, lands, own, per_peer):
    me = _me()[3]
    return [pltpu.make_async_copy(srcs[t].at[me] if per_peer else srcs[t], lands[t].at[me], own.at[t])
            for t in range(len(srcs))]


def exchange_start(srcs, *, per_peer, name, dep=None, peers=ALL_PEERS):
    nt = len(srcs)
    ns = len(peers) * nt
    land_shapes = [(a.shape if per_peer else (NDEV,) + a.shape) for a in srcs]
    deps = [] if dep is None else [dep]

    def body(*refs):
        src, land = refs[:nt], refs[nt:2 * nt]
        send, recv, own = refs[2 * nt + len(deps):2 * nt + len(deps) + 3]
        for cp in _exchange_copies(src, land, send, recv, per_peer, peers) + _own_copies(src, land, own, per_peer):
            cp.start()
        refs[-1][...] = jnp.zeros_like(refs[-1])

    hb = lambda a: pltpu.with_memory_space_constraint(a, pltpu.HBM)
    outs = pl.pallas_call(
        body, name=name,
        out_shape=(pltpu.SemaphoreType.DMA((ns,)), pltpu.SemaphoreType.DMA((ns,)), pltpu.SemaphoreType.DMA((nt,)),
                   *[pltpu.HBM(a.shape, a.dtype) for a in srcs], *[pltpu.HBM(s, a.dtype) for s, a in zip(land_shapes, srcs)],
                   jax.ShapeDtypeStruct((8, 128), F32)),
        in_specs=[HBM_SPEC] * (2 * nt) + [pl.BlockSpec(memory_space=pl.ANY)] * len(deps),
        out_specs=(SEM_SPEC, SEM_SPEC, SEM_SPEC, *([HBM_SPEC] * (2 * nt)), pl.BlockSpec(memory_space=pltpu.VMEM)),
        input_output_aliases={i: 3 + i for i in range(2 * nt)},
        compiler_params=pltpu.CompilerParams(has_side_effects=EFFECT),
    )(*[hb(a) for a in srcs], *[hb(lax.empty(s, a.dtype)) for s, a in zip(land_shapes, srcs)], *deps)
    return dict(send=outs[0], recv=outs[1], own=outs[2], src=list(outs[3:3 + nt]), land=list(outs[3 + nt:3 + 2 * nt]),
                token=outs[-1], per_peer=per_peer, peers=peers)


def exchange_wait(h, after, *, name):
    nt = len(h["src"])
    per_peer, peers = h["per_peer"], h["peers"]

    def body(*refs):
        src, land, send, recv, own = refs[:nt], refs[nt:2 * nt], refs[2 * nt], refs[2 * nt + 1], refs[2 * nt + 2]
        for cp in _exchange_copies(src, land, send, recv, per_peer, peers):
            cp.wait_send()
            cp.wait_recv()
        for cp in _own_copies(src, land, own, per_peer):
            cp.wait()

    outs = pl.pallas_call(
        body, name=name,
        out_shape=(*[pltpu.HBM(a.shape, a.dtype) for a in h["src"]], *[pltpu.HBM(a.shape, a.dtype) for a in h["land"]]),
        in_specs=[HBM_SPEC] * (2 * nt) + [SEM_SPEC, SEM_SPEC, SEM_SPEC, pl.BlockSpec(memory_space=pl.ANY)],
        out_specs=tuple([HBM_SPEC] * (2 * nt)),
        input_output_aliases={i: i for i in range(2 * nt)},
        compiler_params=pltpu.CompilerParams(has_side_effects=EFFECT),
    )(*h["src"], *h["land"], h["send"], h["recv"], h["own"], after)
    return list(outs[nt:])


def relay_start(lands, *, name):
    nt = len(lands)
    ns = len(RELAY) * nt

    def body(*refs):
        for cp in _relay_copies(refs[:nt], refs[nt], refs[nt + 1]):
            cp.start()

    outs = pl.pallas_call(
        body, name=name,
        out_shape=(pltpu.SemaphoreType.DMA((ns,)), pltpu.SemaphoreType.DMA((ns,)),
                   *[pltpu.HBM(a.shape, a.dtype) for a in lands]),
        in_specs=[HBM_SPEC] * nt, out_specs=(SEM_SPEC, SEM_SPEC, *([HBM_SPEC] * nt)),
        input_output_aliases={i: 2 + i for i in range(nt)},
        compiler_params=pltpu.CompilerParams(has_side_effects=EFFECT),
    )(*lands)
    return dict(send=outs[0], recv=outs[1], land=list(outs[2:]))


def relay_wait(h, *, name):
    nt = len(h["land"])

    def body(*refs):
        for cp in _relay_copies(refs[:nt], refs[nt], refs[nt + 1]):
            cp.wait_send()
            cp.wait_recv()

    outs = pl.pallas_call(
        body, name=name, out_shape=tuple(pltpu.HBM(a.shape, a.dtype) for a in h["land"]),
        in_specs=[HBM_SPEC] * nt + [SEM_SPEC, SEM_SPEC], out_specs=tuple([HBM_SPEC] * nt),
        input_output_aliases={i: i for i in range(nt)},
        compiler_params=pltpu.CompilerParams(has_side_effects=EFFECT),
    )(*h["land"], h["send"], h["recv"])
    return list(outs)


def _adamw_math(w, g, m, v):
    nm = B1 * m + (1.0 - B1) * g
    nv = B2 * v + (1.0 - B2) * (g * g)
    m_hat = nm / (1.0 - B1 ** STEP)
    v_hat = nv / (1.0 - B2 ** STEP)
    return -LR * (m_hat / (jnp.sqrt(v_hat) + AEPS) + WD * w), nm, nv


def adamw_many(ws, gs, ms, vs, *, name):
    n = len(ws)

    def body(*refs):
        for k in range(n):
            d, nm, nv = _adamw_math(refs[k][...], refs[n + k][...], refs[2 * n + k][...], refs[3 * n + k][...])
            refs[4 * n + k][...] = d
            refs[5 * n + k][...] = nm
            refs[6 * n + k][...] = nv

    vm = pl.BlockSpec(memory_space=pltpu.VMEM)
    sh = [jax.ShapeDtypeStruct(w.shape, F32) for w in ws]
    outs = pl.pallas_call(body, name=name, in_specs=[vm] * (4 * n), out_specs=[vm] * (3 * n), out_shape=sh * 3,
                          )(*ws, *gs, *ms, *vs)
    return outs[:n], outs[n:2 * n], outs[2 * n:]


def adamw(w, g, m, v, *, name, tr=256):
    R, C = w.shape
    tr = _pick(R, tr, 8)

    def body(w_ref, g_ref, m_ref, v_ref, d_ref, nm_ref, nv_ref):
        d_ref[...], nm_ref[...], nv_ref[...] = _adamw_math(w_ref[...], g_ref[...], m_ref[...], v_ref[...])

    blk = pl.BlockSpec((tr, C), lambda i: (i, 0))
    sh = jax.ShapeDtypeStruct((R, C), F32)
    return pl.pallas_call(
        body, name=name, grid=(R // tr,), in_specs=[blk, blk, blk, blk], out_specs=[blk, blk, blk],
        out_shape=[sh, sh, sh], compiler_params=pltpu.CompilerParams(dimension_semantics=("parallel",)),
    )(w, g, m, v)


def adamw_slots(w, slots, m, v, *, name, tr=256):
    unit = w.ndim == 3
    R, C = w.shape[0], w.shape[-1]
    if R % 16 == 0:
        tr = _pick(R, tr, 16)
    else:
        tr = 144

    def body(w_ref, s_ref, m_ref, v_ref, g_ref, d_ref, nm_ref, nv_ref):
        g = s_ref[0].astype(F32)
        for k in range(1, NDEV):
            g = g + s_ref[k].astype(F32)
        g_ref[...] = g
        d_ref[...], nm_ref[...], nv_ref[...] = _adamw_math(w_ref[...], g, m_ref[...], v_ref[...])

    blk = pl.BlockSpec((tr, None, C), lambda i: (i, 0, 0)) if unit else pl.BlockSpec((tr, C), lambda i: (i, 0))
    sh = jax.ShapeDtypeStruct(w.shape, F32)
    return pl.pallas_call(
        body, name=name, grid=(pl.cdiv(R, tr),), in_specs=[blk, pl.BlockSpec((NDEV, tr, C), lambda i: (0, i, 0)), blk, blk],
        out_specs=[blk, blk, blk, blk], out_shape=[sh, sh, sh, sh],
        compiler_params=pltpu.CompilerParams(dimension_semantics=("parallel",)),
    )(w, slots, m, v)


def _padc(a, n=D):
    return jnp.pad(a, ((0, 0), (0, n - a.shape[1])))


def kernel(x, c, ctx, c_ctx, w_ada, b_ada, norm1_g, w_in, q_norm_g, kv_norm_g, w_uq, w_ukv, conv_w, conv_b, w_attn_out, w_conv_out, w_o, norm2_g, w_up, ffn_conv_w, ffn_conv_b, w_down, final_g, loss_target, m_c_ctx, m_w_ada, m_b_ada, m_norm1_g, m_w_in, m_q_norm_g, m_kv_norm_g, m_w_uq, m_w_ukv, m_conv_w, m_conv_b, m_w_attn_out, m_w_conv_out, m_w_o, m_norm2_g, m_w_up, m_ffn_conv_w, m_ffn_conv_b, m_w_down, m_final_g, v_c_ctx, v_w_ada, v_b_ada, v_norm1_g, v_w_in, v_q_norm_g, v_kv_norm_g, v_w_uq, v_w_ukv, v_conv_w, v_conv_b, v_w_attn_out, v_w_conv_out, v_w_o, v_norm2_g, v_w_up, v_ffn_conv_w, v_ffn_conv_b, v_w_down, v_final_g):
    me = 4 * lax.axis_index("x") + 2 * lax.axis_index("y") + lax.axis_index("c")
    W = dict(c_ctx=c_ctx, w_ada=w_ada, b_ada=b_ada, norm1_g=norm1_g, w_in=w_in, q_norm_g=q_norm_g, kv_norm_g=kv_norm_g,
             w_uq=w_uq, w_ukv=w_ukv, conv_w=conv_w, conv_b=conv_b, w_attn_out=w_attn_out, w_conv_out=w_conv_out, w_o=w_o,
             norm2_g=norm2_g, w_up=w_up, ffn_conv_w=ffn_conv_w, ffn_conv_b=ffn_conv_b, w_down=w_down, final_g=final_g)
    M = dict(c_ctx=m_c_ctx, w_ada=m_w_ada, b_ada=m_b_ada, norm1_g=m_norm1_g, w_in=m_w_in, q_norm_g=m_q_norm_g,
             kv_norm_g=m_kv_norm_g, w_uq=m_w_uq, w_ukv=m_w_ukv, conv_w=m_conv_w, conv_b=m_conv_b, w_attn_out=m_w_attn_out,
             w_conv_out=m_w_conv_out, w_o=m_w_o, norm2_g=m_norm2_g, w_up=m_w_up, ffn_conv_w=m_ffn_conv_w,
             ffn_conv_b=m_ffn_conv_b, w_down=m_w_down, final_g=m_final_g)
    V = dict(c_ctx=v_c_ctx, w_ada=v_w_ada, b_ada=v_b_ada, norm1_g=v_norm1_g, w_in=v_w_in, q_norm_g=v_q_norm_g,
             kv_norm_g=v_kv_norm_g, w_uq=v_w_uq, w_ukv=v_w_ukv, conv_w=v_conv_w, conv_b=v_conv_b, w_attn_out=v_w_attn_out,
             w_conv_out=v_w_conv_out, w_o=v_w_o, norm2_g=v_norm2_g, w_up=v_w_up, ffn_conv_w=v_ffn_conv_w,
             ffn_conv_b=v_ffn_conv_b, w_down=v_w_down, final_g=v_final_g)
    names = list(W)
    transposed = ("w_up",)
    as2d = lambda k, a: (a.reshape(1, -1) if a.ndim == 1 else
                         a[0].T if k in transposed else a.reshape(a.shape[-2], a.shape[-1]))
    W2 = {k: as2d(k, a) for k, a in W.items()}
    M2 = {k: as2d(k, a) for k, a in M.items()}
    V2 = {k: as2d(k, a) for k, a in V.items()}
    unit3 = lambda a: jnp.transpose(a, (2, 0, 1))
    W3, M3, V3 = unit3(W["w_in"]), unit3(M["w_in"]), unit3(V["w_in"])
    nsh = W2["w_ada"].shape[1]

    b_sh = lax.dynamic_slice(W2["b_ada"], (0, me * nsh), (1, nsh))
    s_all, m_all = ada_fwd(c, W2["c_ctx"], _padc(W2["ffn_conv_w"]), _padc(W2["conv_w"]), W2["w_ada"], b_sh, [],
                           name="ada_fwd")
    mod_lat = m_all[:, 0, :].reshape(1, 6 * D)
    mod_ctx = m_all[:, 1, :].reshape(1, 6 * D)
    ffn_w_full = s_all[:, 2:5, :2 * DFF // NDEV].transpose(1, 0, 2).reshape(3, 2 * DFF)
    conv_w_full = s_all[:, 5:8, :CONV // NDEV].transpose(1, 0, 2).reshape(3, CONV)

    stage_w = {"in": ["w_in"], "mid": ["w_uq", "w_ukv", "w_attn_out", "w_conv_out", "w_o"], "ffn": ["w_up", "w_down"]}
    two_level = ("in", "mid")
    ag, tok = {}, m_all
    for st, nms in stage_w.items():
        ag[st] = exchange_start([W2[nm].astype(BF) for nm in nms], per_peer=False, dep=tok, name="ag_start_" + st,
                                peers=FIRST_HOP if st in two_level else ALL_PEERS)
        tok = ag[st]["token"]

    def get_w(stage, after):
        lands = exchange_wait(ag[stage], after, name="ag_wait_" + stage)
        if stage in two_level:
            lands = relay_wait(relay_start(lands, name="ag_relay_" + stage), name="ag_relay_wait_" + stage)
        g = dict(zip(stage_w[stage], lands))
        if stage == "in":
            return build_win(g["w_in"], name="build_win")
        if stage == "mid":
            wq2, wkv2 = build_wq_wkv(g["w_uq"], g["w_ukv"], name="build_wq_wkv")
            return (wq2, wkv2, unshard_cols(g["w_attn_out"], name="unshard_w_attn_out"),
                    unshard_cols(g["w_conv_out"], name="unshard_w_conv_out"), g["w_o"].reshape(D, D))
        return g["w_up"].reshape(2 * DFF, D), g["w_down"].reshape(DFF, D)

    stage_g = {"ffn": ["w_up", "w_down"], "mid": ["w_attn_out", "w_conv_out", "w_o"], "qkv": ["w_uq", "w_ukv"],
               "in": ["w_in"]}
    rs = {}

    def put_g(stage, g):
        if stage == "in":
            parts = [shard_win_grad(g["dwin"], g["dwin_c"], name="shard_win_grad")]
        elif stage == "mid":
            parts = [shard_cols(g["dwao"], name="shard_w_attn_out"), shard_cols(g["dwco"], name="shard_w_conv_out"),
                     g["dwo"].reshape(NDEV, D // NDEV, D)]
        elif stage == "qkv":
            parts = list(shard_wq_wkv_grad(g["dwq2"], g["dwkv2"], name="shard_wq_wkv_grad"))
        else:
            parts = [g["dwup"].reshape(NDEV, 2 * DFF // NDEV, D), g["dwdn"].reshape(NDEV, DFF // NDEV, D)]
        rs[stage] = exchange_start(parts, per_peer=True, name="rs_start_" + stage)
        return rs[stage]["token"]

    r = _local_step(x[0], ctx[0], loss_target[0], mod_lat, mod_ctx, W2["norm1_g"], W2["q_norm_g"], W2["kv_norm_g"],
                    W2["norm2_g"], W2["final_g"], conv_w_full, W2["conv_b"], ffn_w_full, W2["ffn_conv_b"], get_w, put_g,
                    ag["ffn"]["token"])

    G, DL, NM, NV = {}, {}, {}, {}

    def finish(stage, after):
        for nm, sl in zip(stage_g[stage], exchange_wait(rs[stage], after, name="rs_wait_" + stage)):
            wmv = (W3, M3, V3) if nm == "w_in" else (W2[nm], M2[nm], V2[nm])
            G[nm], DL[nm], NM[nm], NV[nm] = adamw_slots(wmv[0], sl, wmv[1], wmv[2], name="adamw_" + nm)
            after = DL[nm]
        return after

    after = r["dx"]
    for st in ("ffn", "mid", "qkv"):
        after = finish(st, after)

    a_buf, ssum = sync_small(r, [DL[nm] for st in ("ffn", "mid", "qkv") for nm in stage_g[st]], name="sync_small")
    loss = ssum[P_LOSS, 0]
    G["norm1_g"] = ssum[P_N1:P_N1 + 1]
    G["q_norm_g"] = ssum[P_QG:P_QG + 1, :QL]
    G["kv_norm_g"] = ssum[P_KVG:P_KVG + 1, :KVL]
    G["conv_b"] = ssum[P_CB:P_CB + 1, :CONV]
    G["norm2_g"] = ssum[P_N2:P_N2 + 1]
    G["ffn_conv_b"] = ssum[P_FB:P_FB + 2 * FROWS].reshape(1, 2, FROWS * D)[:, :, :DFF].reshape(1, 2 * DFF)
    G["final_g"] = ssum[P_FG:P_FG + 1]
    G["conv_w"] = lax.dynamic_slice(ssum[P_CW:P_CW + 3, :CONV], (0, me * (CONV // NDEV)), (3, CONV // NDEV))
    fw_full = ssum[P_FW:P_FW + 6 * FROWS].reshape(3, 2, FROWS * D)[:, :, :DFF].reshape(3, 2 * DFF)
    G["ffn_conv_w"] = lax.dynamic_slice(fw_full, (0, me * (2 * DFF // NDEV)), (3, 2 * DFF // NDEV))
    G["b_ada"] = (ssum[P_DML:P_DML + 6] + ssum[P_DMC:P_DMC + 6]).reshape(1, 6 * D)

    dml = lax.dynamic_slice(a_buf[:, P_DML:P_DML + 6, :].reshape(NDEV, 6 * D), (0, me * nsh), (NDEV, nsh))
    dmc = lax.dynamic_slice(ssum[P_DMC:P_DMC + 6].reshape(1, 6 * D), (0, me * nsh), (1, nsh))
    G["w_ada"], gcc = ada_bwd(s_all, dml, dmc, W2["w_ada"], W2["c_ctx"], name="ada_bwd")
    G["c_ctx"] = gcc[0:1]

    DL["w_ada"], NM["w_ada"], NV["w_ada"] = adamw(W2["w_ada"], G["w_ada"], M2["w_ada"], V2["w_ada"], name="adamw_w_ada")
    small = ["c_ctx", "b_ada", "norm1_g", "q_norm_g", "kv_norm_g", "conv_b", "norm2_g", "ffn_conv_b", "final_g", "conv_w",
             "ffn_conv_w"]
    ds, nms, nvs = adamw_many([W2[k] for k in small], [G[k] for k in small], [M2[k] for k in small],
                              [V2[k] for k in small], name="adamw_small")
    for k, nm in enumerate(small):
        DL[nm], NM[nm], NV[nm] = ds[k], nms[k], nvs[k]
    finish("in", ds[0])

    outs = [loss, r["dx"][None]]
    for grp in (G, DL, NM, NV):
        outs += [grp[nm].T[None] if nm in transposed else
                 jnp.transpose(grp[nm], (1, 2, 0)) if nm == "w_in" else grp[nm].reshape(W[nm].shape) for nm in names]
    return tuple(outs)
```

```python
import functools
import numpy as np
import jax
import jax.numpy as jnp
from jax import lax
from jax.experimental import pallas as pl
from jax.experimental.pallas import tpu as pltpu

F32 = jnp.float32
BF = jnp.bfloat16
MESH = pl.DeviceIdType.MESH

D = 1024
T = 2048
TC = 256
TKV = T + TC
GRID_W = 64
NH = 8
DN = 64
DR = 32
DV = 64
QL = 384
KVL = 256
CONV = 512
DFF = 2816
EPS = 1e-6
ROPE_THETA = 10000.0
SCALE = (DN + DR) ** -0.5
NDEV = 8
HP = 128

O_GA, O_GC, O_KV, O_Q, O_CV = 0, 1024, 2048, 2560, 3072
NIN = 4608
CVB = 256
N_IN = 4256
SH_IN = N_IN // NDEV

LR, B1, B2, AEPS, WD, STEP = 0.001, 0.9, 0.999, 1e-08, 0.01, 10


def _pick(n, target, mult=128):
    best = None
    for d in range(mult, min(n, target) + 1, mult):
        if n % d == 0:
            best = d
    return best if best is not None else n


def _swap_start(g):
    return 8 * (g ^ 1)


def mm(a, b, *, ta=False, tb=False, out_dtype=F32, name, tm=1024, tn=1024, tk=2048, M=None, N=None, K=None,
       a_off=(0, 0), b_off=(0, 0), a_stack=False, b_stack=False, o_stack=False, dep=None):
    def dims(arr, stack):
        return (arr.shape[1], 2 * arr.shape[2]) if stack else arr.shape

    ar, ac = dims(a, a_stack)
    br, bc = dims(b, b_stack)
    M = M or ((ac if ta else ar) - a_off[1 if ta else 0])
    K = K or ((ar if ta else ac) - a_off[0 if ta else 1])
    N = N or ((br if tb else bc) - b_off[0 if tb else 1])
    tm = _pick(M, tm, 128 if ta else 16)
    tn = _pick(N // 2 if (o_stack or (b_stack and not tb)) else N, tn, 128)
    tk = _pick(K // 2 if ((a_stack and not ta) or (b_stack and tb)) else K, tk, 128)
    nk = K // tk
    ca = 0 if ta else 1
    cb = 1 if tb else 0

    def body(a_ref, b_ref, *rest):
        o_ref, acc = rest[-2:]
        k = pl.program_id(2)
        part = lax.dot_general(a_ref[...].astype(BF), b_ref[...].astype(BF),
                               (((ca,), (cb,)), ((), ())), preferred_element_type=F32)
        if nk == 1:
            o_ref[...] = part.astype(o_ref.dtype)
        else:
            @pl.when(k == 0)
            def _():
                acc[...] = part

            @pl.when(k > 0)
            def _():
                acc[...] += part

            @pl.when(k == nk - 1)
            def _():
                o_ref[...] = acc[...].astype(o_ref.dtype)

    def spec(blk, rc, off, stack, ncols):
        assert off[0] % blk[0] == 0 and off[1] % blk[1] == 0, (name, blk, off)
        ro, co = off[0] // blk[0], off[1] // blk[1]
        if not stack:
            return pl.BlockSpec(blk, lambda i, j, k: (rc(i, j, k)[0] + ro, rc(i, j, k)[1] + co))
        nhb = ncols // 2 // blk[1]
        return pl.BlockSpec((None,) + blk,
                            lambda i, j, k: ((rc(i, j, k)[1] + co) // nhb, rc(i, j, k)[0] + ro, (rc(i, j, k)[1] + co) % nhb))

    a_spec = spec((tk, tm), lambda i, j, k: (k, i), a_off, a_stack, ac) if ta else \
        spec((tm, tk), lambda i, j, k: (i, k), a_off, a_stack, ac)
    b_spec = spec((tn, tk), lambda i, j, k: (j, k), b_off, b_stack, bc) if tb else \
        spec((tk, tn), lambda i, j, k: (k, j), b_off, b_stack, bc)
    o_spec = spec((tm, tn), lambda i, j, k: (i, j), (0, 0), o_stack, N)
    o_shape = (2, M, N // 2) if o_stack else (M, N)
    deps = [] if dep is None else [dep]
    return pl.pallas_call(
        body, name=name, grid=(M // tm, N // tn, nk),
        in_specs=[a_spec, b_spec] + [pl.BlockSpec(memory_space=pl.ANY)] * len(deps),
        out_specs=o_spec, out_shape=jax.ShapeDtypeStruct(o_shape, out_dtype),
        scratch_shapes=[pltpu.VMEM((tm, tn) if nk > 1 else (8, 128), F32)],
        compiler_params=pltpu.CompilerParams(dimension_semantics=("parallel", "parallel", "arbitrary")),
    )(a, b, *deps)


def _row(width):
    return pl.BlockSpec((1, width), lambda *_: (0, 0))


NLAT = T // TC


def normmod_cat(ctx, x, g, csc, csh, sc, sh, dep, *, name, tm=256):
    assert tm == TC

    def body(c_ref, x_ref, g_ref, csc_ref, csh_ref, sc_ref, sh_ref, dep_ref, h_ref):
        last = pl.program_id(0) == NLAT
        xv = jnp.where(last, c_ref[...], x_ref[...])
        scv = jnp.where(last, csc_ref[...], sc_ref[...])
        shv = jnp.where(last, csh_ref[...], sh_ref[...])
        r = lax.rsqrt(jnp.mean(xv * xv, axis=-1, keepdims=True) + EPS)
        h_ref[...] = ((xv * r * g_ref[...]) * (1.0 + scv) + shv).astype(BF)

    return pl.pallas_call(
        body, name=name, grid=(TKV // tm,),
        in_specs=[pl.BlockSpec((tm, D), lambda i: (0, 0)), pl.BlockSpec((tm, D), lambda i: (jnp.minimum(i, NLAT - 1), 0)),
                  _row(D), _row(D), _row(D), _row(D), _row(D), pl.BlockSpec(memory_space=pl.ANY)],
        out_specs=pl.BlockSpec((tm, D), lambda i: (i, 0)), out_shape=jax.ShapeDtypeStruct((TKV, D), BF),
        compiler_params=pltpu.CompilerParams(dimension_semantics=("parallel",)),
    )(ctx, x, g, csc, csh, sc, sh, dep)


def resid_normmod(x, a, gate, g, sc, sh, *, name, tm=256):
    R = x.shape[0]

    def body(x_ref, a_ref, gate_ref, g_ref, sc_ref, sh_ref, x1_ref, h_ref):
        xv = x_ref[...] + gate_ref[...] * a_ref[...]
        x1_ref[...] = xv
        r = lax.rsqrt(jnp.mean(xv * xv, axis=-1, keepdims=True) + EPS)
        h_ref[...] = ((xv * r * g_ref[...]) * (1.0 + sc_ref[...]) + sh_ref[...]).astype(BF)

    blk = pl.BlockSpec((tm, D), lambda i: (i, 0))
    return pl.pallas_call(
        body, name=name, grid=(R // tm,), in_specs=[blk, blk, _row(D), _row(D), _row(D), _row(D)],
        out_specs=[blk, blk],
        out_shape=[jax.ShapeDtypeStruct((R, D), F32), jax.ShapeDtypeStruct((R, D), BF)],
        compiler_params=pltpu.CompilerParams(dimension_semantics=("parallel",)),
    )(x, a, gate, g, sc, sh)


def kvprep(pc, p, kvg, wkv2, ck, sk, *, name, tm=256):
    assert tm == TC
    nb = TKV // tm
    kvcol = O_KV // 512

    def body(pc_ref, p_ref, g_ref, w_ref, ck_ref, sk_ref, k_ref, v_ref, ckv_ref):
        i = pl.program_id(0)
        t = jnp.where(i == NLAT, pc_ref[...], p_ref[...])
        pk = t[:, :KVL]
        r = lax.rsqrt(jnp.mean(pk * pk, axis=-1, keepdims=True) + EPS)
        ckv = (pk * r * g_ref[...]).astype(BF)
        ckv_ref[...] = ckv
        kv2 = jnp.dot(ckv, w_ref[...], preferred_element_type=F32)
        krr = t[:, KVL:KVL + HP] * ck_ref[...] + t[:, KVL + HP:KVL + 2 * HP] * sk_ref[...]
        k_ref[...] = (kv2[:, :NH * HP] + jnp.concatenate([krr] * NH, axis=1)).astype(BF)
        v_ref[...] = kv2[:, NH * HP:].astype(BF)

    return pl.pallas_call(
        body, name=name, grid=(nb,),
        in_specs=[pl.BlockSpec((tm, 512), lambda i: (0, 0)),
                  pl.BlockSpec((tm, 512), lambda i: (jnp.minimum(i, NLAT - 1), kvcol)),
                  _row(KVL), pl.BlockSpec((KVL, NH * HP + NH * DV), lambda i: (0, 0)),
                  pl.BlockSpec((tm, HP), lambda i: (i, 0)), pl.BlockSpec((tm, HP), lambda i: (i, 0))],
        out_specs=[pl.BlockSpec((tm, NH * HP), lambda i: (i, 0)), pl.BlockSpec((tm, NH * DV), lambda i: (i, 0)),
                   pl.BlockSpec((tm, KVL), lambda i: (i, 0))],
        out_shape=[jax.ShapeDtypeStruct((TKV, NH * HP), BF), jax.ShapeDtypeStruct((TKV, NH * DV), BF),
                   jax.ShapeDtypeStruct((TKV, KVL), BF)],
        compiler_params=pltpu.CompilerParams(dimension_semantics=("parallel",)),
    )(pc, p, kvg, wkv2, ck, sk)


def qprep(p, qg, wq2, cq_t, sq_t, *, name, tm=256):
    qcol = O_Q // 512

    def body(p_ref, g_ref, w_ref, c_ref, s_ref, q_ref, cq_ref):
        pq = p_ref[...]
        r = lax.rsqrt(jnp.sum(pq * pq, axis=-1, keepdims=True) * (1.0 / QL) + EPS)
        cq = (pq * r * g_ref[...]).astype(BF)
        cq_ref[...] = cq
        q2 = jnp.dot(cq, w_ref[...], preferred_element_type=F32)
        cc = jnp.concatenate([c_ref[...]] * NH, axis=1)
        ss = jnp.concatenate([s_ref[...]] * NH, axis=1)
        q_ref[...] = (q2[:, :NH * HP] * cc + q2[:, NH * HP:] * ss).astype(BF)

    return pl.pallas_call(
        body, name=name, grid=(T // tm,),
        in_specs=[pl.BlockSpec((tm, 512), lambda i: (i, qcol)), _row(512),
                  pl.BlockSpec((512, 2 * NH * HP), lambda i: (0, 0)),
                  pl.BlockSpec((tm, HP), lambda i: (i, 0)), pl.BlockSpec((tm, HP), lambda i: (i, 0))],
        out_specs=[pl.BlockSpec((tm, NH * HP), lambda i: (i, 0)), pl.BlockSpec((tm, 512), lambda i: (i, 0))],
        out_shape=[jax.ShapeDtypeStruct((T, NH * HP), BF), jax.ShapeDtypeStruct((T, 512), BF)],
        compiler_params=pltpu.CompilerParams(dimension_semantics=("parallel",)),
    )(p, qg, wq2, cq_t, sq_t)


def _head_mask(h):
    lanes = lax.broadcasted_iota(jnp.int32, (1, 2 * DV), 1)
    return (lanes // DV) == (h % 2)


KC = 256
NKC = TKV // KC
LOG2E = 1.4426950408889634


def _scores_pass(q, k_ref, s_scr):
    m = None
    for c in range(NKC):
        s = lax.dot_general(q, k_ref[c * KC:(c + 1) * KC, :], (((1,), (1,)), ((), ())),
                            preferred_element_type=F32) * (SCALE * LOG2E)
        s_scr[:, c * KC:(c + 1) * KC] = s
        mc = jnp.max(s, axis=-1, keepdims=True)
        m = mc if m is None else jnp.maximum(m, mc)
    return m


def attn_fwd(q, k, v, *, name, tq=256):
    def body(q_ref, k_ref, v_ref, o_ref, s_scr):
        h = pl.program_id(1)
        m = _scores_pass(q_ref[...], k_ref, s_scr)
        l = jnp.zeros((tq, 1), F32)
        acc = jnp.zeros((tq, 2 * DV), F32)
        for c in range(NKC):
            e = jnp.exp2(s_scr[:, c * KC:(c + 1) * KC] - m)
            l = l + jnp.sum(e, axis=-1, keepdims=True)
            acc = acc + jnp.dot(e.astype(BF), v_ref[c * KC:(c + 1) * KC, :], preferred_element_type=F32)
        o2 = jnp.where(_head_mask(h), acc * (1.0 / l), 0.0).astype(BF)

        @pl.when(h % 2 == 0)
        def _():
            o_ref[...] = o2

        @pl.when(h % 2 == 1)
        def _():
            o_ref[...] = o_ref[...] + o2

    return pl.pallas_call(
        body, name=name, grid=(T // tq, NH),
        in_specs=[pl.BlockSpec((tq, HP), lambda i, h: (i, h)), pl.BlockSpec((TKV, HP), lambda i, h: (0, h)),
                  pl.BlockSpec((TKV, 2 * DV), lambda i, h: (0, h // 2))],
        out_specs=pl.BlockSpec((tq, 2 * DV), lambda i, h: (i, h // 2)),
        out_shape=jax.ShapeDtypeStruct((T, NH * DV), BF),
        scratch_shapes=[pltpu.VMEM((tq, TKV), F32)],
        compiler_params=pltpu.CompilerParams(dimension_semantics=("parallel", "arbitrary")),
    )(q, k, v)


def _shift_dn(x):
    n = x.shape[0]
    rows = lax.broadcasted_iota(jnp.int32, (n, 1), 0)
    return jnp.where(rows == 0, 0.0, pltpu.roll(x, 1, axis=0))


def _shift_up(x):
    n = x.shape[0]
    rows = lax.broadcasted_iota(jnp.int32, (n, 1), 0)
    return jnp.where(rows == n - 1, 0.0, pltpu.roll(x, n - 1, axis=0))


def _conv(x, w_ref, b_ref):
    return b_ref[...] + _shift_dn(x) * w_ref[0:1, :] + x * w_ref[1:2, :] + _shift_up(x) * w_ref[2:3, :]


def _conv_t(dy, w_ref):
    return _shift_up(dy) * w_ref[0:1, :] + dy * w_ref[1:2, :] + _shift_dn(dy) * w_ref[2:3, :]


def _conv_wgrad(dw_ref, dy, x):
    dw_ref[0:1, :] = jnp.sum(dy * _shift_dn(x), axis=0, keepdims=True)
    dw_ref[1:2, :] = jnp.sum(dy * x, axis=0, keepdims=True)
    dw_ref[2:3, :] = jnp.sum(dy * _shift_up(x), axis=0, keepdims=True)


def convz(p, cw, cb, *, name):
    o0 = O_CV // (3 * CVB)

    def body(p_ref, w_ref, bias_ref, z_ref):
        xv, bv, cv = p_ref[:, 0:CVB], p_ref[:, CVB:2 * CVB], p_ref[:, 2 * CVB:3 * CVB]
        z_ref[...] = (bv * _conv(cv * xv, w_ref, bias_ref)).astype(BF)

    return pl.pallas_call(
        body, name=name, grid=(CONV // CVB,),
        in_specs=[pl.BlockSpec((T, 3 * CVB), lambda j: (0, o0 + j)), pl.BlockSpec((3, CVB), lambda j: (0, j)),
                  pl.BlockSpec((1, CVB), lambda j: (0, j))],
        out_specs=pl.BlockSpec((T, CVB), lambda j: (0, j)),
        out_shape=jax.ShapeDtypeStruct((T, CONV), BF),
        compiler_params=pltpu.CompilerParams(dimension_semantics=("parallel",)),
    )(p, cw, cb)


def gate_merge(p, ya, yc, *, name, tm=256):
    def body(ga_ref, gc_ref, ya_ref, yc_ref, o_ref):
        o_ref[...] = (jax.nn.sigmoid(ga_ref[...]) * ya_ref[...] + jax.nn.sigmoid(gc_ref[...]) * yc_ref[...]).astype(BF)

    blk = pl.BlockSpec((tm, D), lambda i: (i, 0))
    return pl.pallas_call(
        body, name=name, grid=(T // tm,),
        in_specs=[pl.BlockSpec((tm, D), lambda i: (i, O_GA // D)), pl.BlockSpec((tm, D), lambda i: (i, O_GC // D)), blk, blk],
        out_specs=blk, out_shape=jax.ShapeDtypeStruct((T, D), BF),
        compiler_params=pltpu.CompilerParams(dimension_semantics=("parallel",)),
    )(p, p, ya, yc)


CONV_HALO = 8
CONV_ROWS = 256


def _row_chunks(n, chunk, carry):
    carry = chunk(0, True, False, carry)
    carry = lax.fori_loop(1, n // CONV_ROWS - 1, lambda c, a: chunk(c * CONV_ROWS, False, False, a), carry)
    return chunk(n - CONV_ROWS, False, True, carry)


def _ext_rows(ref, r0, first, last):
    n, w = ref.shape
    zero = jnp.zeros((CONV_HALO, w), ref.dtype)
    if first:
        return jnp.concatenate([zero, ref[0:CONV_ROWS + CONV_HALO, :]], axis=0)
    if last:
        return jnp.concatenate([ref[n - CONV_ROWS - CONV_HALO:n, :], zero], axis=0)
    return ref[pl.ds(pl.multiple_of(r0 - CONV_HALO, 8), CONV_ROWS + 2 * CONV_HALO), :]


def _center_rows(r0, first, last):
    return slice(r0, r0 + CONV_ROWS) if (first or last) else pl.ds(pl.multiple_of(r0, 8), CONV_ROWS)


def _roll_dn(x):
    return pltpu.roll(x, 1, axis=0)


def _roll_up(x):
    return pltpu.roll(x, x.shape[0] - 1, axis=0)


_CTR = slice(CONV_HALO, CONV_HALO + CONV_ROWS)


def ffn_act(u0, cw, cb, *, name, tc=128):
    nb = DFF // tc

    def body(u_ref, wg_ref, wv_ref, bg_ref, bv_ref, f_ref):
        wg = [wg_ref[k:k + 1, :] for k in range(3)]
        wv = [wv_ref[k:k + 1, :] for k in range(3)]
        bg, bv = bg_ref[...], bv_ref[...]

        def chunk(r0, first, last, carry):
            xg, xv = _ext_rows(u_ref.at[0], r0, first, last), _ext_rows(u_ref.at[1], r0, first, last)
            ug = bg + _roll_dn(xg) * wg[0] + xg * wg[1] + _roll_up(xg) * wg[2]
            uv = bv + _roll_dn(xv) * wv[0] + xv * wv[1] + _roll_up(xv) * wv[2]
            f_ref[_center_rows(r0, first, last), :] = (ug * jax.nn.sigmoid(ug) * uv)[_CTR].astype(BF)
            return carry

        _row_chunks(T, chunk, 0)

    return pl.pallas_call(
        body, name=name, grid=(nb,),
        in_specs=[pl.BlockSpec((2, T, tc), lambda j: (0, 0, j)),
                  pl.BlockSpec((3, tc), lambda j: (0, j)), pl.BlockSpec((3, tc), lambda j: (0, nb + j)),
                  pl.BlockSpec((1, tc), lambda j: (0, j)), pl.BlockSpec((1, tc), lambda j: (0, nb + j))],
        out_specs=pl.BlockSpec((T, tc), lambda j: (0, j)),
        out_shape=jax.ShapeDtypeStruct((T, DFF), BF),
        compiler_params=pltpu.CompilerParams(dimension_semantics=("parallel",)),
    )(u0, cw, cw, cb, cb)


def final_loss(x1, d, g2, fg, tgt, *, name, tm=256):
    def body(x1_ref, d_ref, g2_ref, fg_ref, t_ref, dx_ref, dd_ref, dfg_ref, loss_ref):
        i = pl.program_id(0)
        xv = x1_ref[...] + g2_ref[...] * d_ref[...]
        r = lax.rsqrt(jnp.mean(xv * xv, axis=-1, keepdims=True) + EPS)
        xh = xv * r
        diff = xh * fg_ref[...] - t_ref[...]
        part = 0.5 * jnp.sum(jnp.mean(diff * diff, axis=-1, keepdims=True), axis=0, keepdims=True)
        dy = diff * (1.0 / D)
        a = dy * fg_ref[...]
        dx = r * (a - xh * jnp.mean(a * xh, axis=-1, keepdims=True))
        dx_ref[...] = dx
        dd_ref[...] = (dx * g2_ref[...]).astype(BF)
        dfg = jnp.sum(dy * xh, axis=0, keepdims=True)

        @pl.when(i == 0)
        def _():
            dfg_ref[...] = dfg
            loss_ref[...] = jnp.broadcast_to(part, (1, 128))

        @pl.when(i > 0)
        def _():
            dfg_ref[...] += dfg
            loss_ref[...] += jnp.broadcast_to(part, (1, 128))

    blk = pl.BlockSpec((tm, D), lambda i: (i, 0))
    return pl.pallas_call(
        body, name=name, grid=(T // tm,), in_specs=[blk, blk, _row(D), _row(D), blk],
        out_specs=[blk, blk, _row(D), _row(128)],
        out_shape=[jax.ShapeDtypeStruct((T, D), F32), jax.ShapeDtypeStruct((T, D), BF),
                   jax.ShapeDtypeStruct((1, D), F32), jax.ShapeDtypeStruct((1, 128), F32)],
        compiler_params=pltpu.CompilerParams(dimension_semantics=("arbitrary",)),
    )(x1, d, g2, fg, tgt)


def normmod_bwd(x, dh, g, sc, dres, gsrc, gate, *, name, tm=256):
    R = x.shape[0]
    has_res = dres is not None

    def body(*refs):
        if has_res:
            x_ref, dh_ref, g_ref, sc_ref, dres_ref, gsrc_ref, gate_ref, dx_ref, dxg_ref, st_ref = refs
        else:
            x_ref, dh_ref, g_ref, sc_ref, st_ref = refs
        i = pl.program_id(0)
        xv = x_ref[...]
        r = lax.rsqrt(jnp.mean(xv * xv, axis=-1, keepdims=True) + EPS)
        xh = xv * r
        dhv = dh_ref[...]
        n = xh * g_ref[...]
        dn = dhv * (1.0 + sc_ref[...])
        a = dn * g_ref[...]
        rows = [jnp.sum(dhv, axis=0, keepdims=True), jnp.sum(dhv * n, axis=0, keepdims=True),
                jnp.sum(dn * xh, axis=0, keepdims=True)]
        if has_res:
            dr = dres_ref[...]
            dx = dr + r * (a - xh * jnp.mean(a * xh, axis=-1, keepdims=True))
            dx_ref[...] = dx
            dxg_ref[...] = (dx * gate_ref[...]).astype(BF)
            rows.append(jnp.sum(dr * gsrc_ref[...], axis=0, keepdims=True))
        else:
            rows.append(jnp.zeros((1, D), F32))

        @pl.when(i == 0)
        def _():
            for k, row in enumerate(rows):
                st_ref[k:k + 1, :] = row

        @pl.when(i > 0)
        def _():
            for k, row in enumerate(rows):
                st_ref[k:k + 1, :] += row

    blk = pl.BlockSpec((tm, D), lambda i: (i, 0))
    st_spec = pl.BlockSpec((4, D), lambda i: (0, 0))
    st_shape = jax.ShapeDtypeStruct((4, D), F32)
    cp = pltpu.CompilerParams(dimension_semantics=("arbitrary",))
    if has_res:
        return pl.pallas_call(
            body, name=name, grid=(R // tm,), in_specs=[blk, blk, _row(D), _row(D), blk, blk, _row(D)],
            out_specs=[blk, blk, st_spec],
            out_shape=[jax.ShapeDtypeStruct((R, D), F32), jax.ShapeDtypeStruct((R, D), BF), st_shape],
            compiler_params=cp,
        )(x, dh, g, sc, dres, gsrc, gate)
    return pl.pallas_call(
        body, name=name, grid=(R // tm,), in_specs=[blk, blk, _row(D), _row(D)],
        out_specs=st_spec, out_shape=st_shape, compiler_params=cp,
    )(x, dh, g, sc)


def ffn_act_bwd(u0, df, cw, cb, *, name, tc=128):
    nb = DFF // tc

    def body(u_ref, df_ref, wg_ref, wv_ref, bg_ref, bv_ref, du_ref, dw_ref, db_ref):
        wg = [wg_ref[k:k + 1, :] for k in range(3)]
        wv = [wv_ref[k:k + 1, :] for k in range(3)]
        bg, bv = bg_ref[...], bv_ref[...]

        def chunk(r0, first, last, acc):
            xg, xv = _ext_rows(u_ref.at[0], r0, first, last), _ext_rows(u_ref.at[1], r0, first, last)
            dfe = _ext_rows(df_ref, r0, first, last)
            xg_d, xg_u, xv_d, xv_u = _roll_dn(xg), _roll_up(xg), _roll_dn(xv), _roll_up(xv)
            ug = bg + xg_d * wg[0] + xg * wg[1] + xg_u * wg[2]
            uv = bv + xv_d * wv[0] + xv * wv[1] + xv_u * wv[2]
            sig = jax.nn.sigmoid(ug)
            dug = dfe * uv * (sig * (1.0 + ug * (1.0 - sig)))
            duv = dfe * (ug * sig)
            rows = _center_rows(r0, first, last)
            du_ref[0, rows, :] = (_roll_up(dug) * wg[0] + dug * wg[1] + _roll_dn(dug) * wg[2])[_CTR].astype(BF)
            du_ref[1, rows, :] = (_roll_up(duv) * wv[0] + duv * wv[1] + _roll_dn(duv) * wv[2])[_CTR].astype(BF)
            terms = [dug * xg_d, dug * xg, dug * xg_u, dug, duv * xv_d, duv * xv, duv * xv_u, duv]
            return tuple(a + jnp.sum(t[_CTR], axis=0, keepdims=True) for a, t in zip(acc, terms))

        acc = _row_chunks(T, chunk, tuple(jnp.zeros((1, tc), F32) for _ in range(8)))
        for k in range(3):
            dw_ref[0, k:k + 1, :] = acc[k]
            dw_ref[1, k:k + 1, :] = acc[4 + k]
        db_ref[0] = acc[3]
        db_ref[1] = acc[7]

    lo = lambda r: pl.BlockSpec((r, tc), lambda j: (0, j))
    hi = lambda r: pl.BlockSpec((r, tc), lambda j: (0, nb + j))
    st = lambda r: pl.BlockSpec((2, r, tc), lambda j: (0, 0, j))
    return pl.pallas_call(
        body, name=name, grid=(nb,),
        in_specs=[st(T), lo(T), lo(3), hi(3), lo(1), hi(1)],
        out_specs=[st(T), st(3), st(1)],
        out_shape=[jax.ShapeDtypeStruct((2, T, DFF), BF), jax.ShapeDtypeStruct((2, 3, DFF), F32),
                   jax.ShapeDtypeStruct((2, 1, DFF), F32)],
        compiler_params=pltpu.CompilerParams(dimension_semantics=("parallel",)),
    )(u0, df, cw, cw, cb, cb)


def gate_merge_bwd(p, ya, yc, dm, *, name, tm=256):
    def body(ga_ref, gc_ref, ya_ref, yc_ref, dm_ref, dya_ref, dyc_ref, dp_ref):
        sa, sc_ = jax.nn.sigmoid(ga_ref[...]), jax.nn.sigmoid(gc_ref[...])
        dmv = dm_ref[...]
        dya_ref[...] = (dmv * sa).astype(BF)
        dyc_ref[...] = (dmv * sc_).astype(BF)
        dp_ref[:, 0:D] = (dmv * ya_ref[...] * (sa * (1.0 - sa))).astype(BF)
        dp_ref[:, D:2 * D] = (dmv * yc_ref[...] * (sc_ * (1.0 - sc_))).astype(BF)

    blk = pl.BlockSpec((tm, D), lambda i: (i, 0))
    sh = jax.ShapeDtypeStruct((T, D), BF)
    return pl.pallas_call(
        body, name=name, grid=(T // tm,),
        in_specs=[pl.BlockSpec((tm, D), lambda i: (i, O_GA // D)), pl.BlockSpec((tm, D), lambda i: (i, O_GC // D)), blk, blk, blk],
        out_specs=[blk, blk, pl.BlockSpec((tm, 2 * D), lambda i: (i, 0))],
        out_shape=[sh, sh, jax.ShapeDtypeStruct((T, NIN), BF)],
        compiler_params=pltpu.CompilerParams(dimension_semantics=("parallel",)),
    )(p, p, ya, yc, dm)


def convz_bwd(p, dz, cw, cb, dp, *, name):
    o0 = O_CV // (3 * CVB)

    def body(p_ref, dz_ref, w_ref, bias_ref, dp_in, dp_ref, dw_ref, dbias_ref):
        xv, bv, cv = p_ref[:, 0:CVB], p_ref[:, CVB:2 * CVB], p_ref[:, 2 * CVB:3 * CVB]
        ci = cv * xv
        dwc = _conv(ci, w_ref, bias_ref)
        dzv = dz_ref[...]
        ddw = dzv * bv
        dci = _conv_t(ddw, w_ref)
        dp_ref[:, 0:CVB] = (dci * cv).astype(BF)
        dp_ref[:, CVB:2 * CVB] = (dzv * dwc).astype(BF)
        dp_ref[:, 2 * CVB:3 * CVB] = (dci * xv).astype(BF)
        _conv_wgrad(dw_ref, ddw, ci)
        dbias_ref[...] = jnp.sum(ddw, axis=0, keepdims=True)

    own = lambda r: pl.BlockSpec((r, CVB), lambda j: (0, j))
    return pl.pallas_call(
        body, name=name, grid=(CONV // CVB,),
        in_specs=[pl.BlockSpec((T, 3 * CVB), lambda j: (0, o0 + j)), own(T), own(3), own(1),
                  pl.BlockSpec(memory_space=pl.ANY)],
        out_specs=[pl.BlockSpec((T, 3 * CVB), lambda j: (0, o0 + j)), own(3), own(1)],
        out_shape=[jax.ShapeDtypeStruct((T, NIN), BF), jax.ShapeDtypeStruct((3, CONV), F32),
                   jax.ShapeDtypeStruct((1, CONV), F32)],
        input_output_aliases={4: 0},
        compiler_params=pltpu.CompilerParams(dimension_semantics=("parallel",)),
    )(p, dz, cw, cb, dp)


def attn_bwd(q, k, v, do, *, name, tq=256):
    def body(q_ref, k_ref, v_ref, do_ref, dq_ref, dk_ref, dv_ref, s_scr, dp_scr):
        h, i = pl.program_id(0), pl.program_id(1)

        @pl.when(i == 0)
        def _():
            dk_ref[...] = jnp.zeros_like(dk_ref)

        @pl.when((i == 0) & (h % 2 == 0))
        def _():
            dv_ref[...] = jnp.zeros_like(dv_ref)

        qv = q_ref[...]
        dom = jnp.where(_head_mask(h), do_ref[...], jnp.zeros_like(do_ref[...]))
        m = _scores_pass(qv, k_ref, s_scr)
        l = jnp.zeros((tq, 1), F32)
        dsum = jnp.zeros((tq, 1), F32)
        for c in range(NKC):
            cols = slice(c * KC, (c + 1) * KC)
            e = jnp.exp2(s_scr[:, cols] - m)
            s_scr[:, cols] = e
            dp = lax.dot_general(dom, v_ref[cols, :], (((1,), (1,)), ((), ())), preferred_element_type=F32)
            dp_scr[:, cols] = dp
            l = l + jnp.sum(e, axis=-1, keepdims=True)
            dsum = dsum + jnp.sum(e * dp, axis=-1, keepdims=True)
        inv = 1.0 / l
        delta = dsum * inv
        dos = (dom.astype(F32) * inv).astype(BF)
        dq = jnp.zeros((tq, HP), F32)
        for c in range(NKC):
            cols = slice(c * KC, (c + 1) * KC)
            e = s_scr[:, cols]
            ds = (e * (dp_scr[:, cols] - delta) * (inv * SCALE)).astype(BF)
            dq = dq + jnp.dot(ds, k_ref[cols, :], preferred_element_type=F32)
            dk_ref[cols, :] += lax.dot_general(ds, qv, (((0,), (0,)), ((), ())), preferred_element_type=F32)
            dv_ref[cols, :] += lax.dot_general(e.astype(BF), dos, (((0,), (0,)), ((), ())), preferred_element_type=F32)
        dq_ref[...] = dq

    return pl.pallas_call(
        body, name=name, grid=(NH, T // tq),
        in_specs=[pl.BlockSpec((tq, HP), lambda h, i: (i, h)), pl.BlockSpec((TKV, HP), lambda h, i: (0, h)),
                  pl.BlockSpec((TKV, 2 * DV), lambda h, i: (0, h // 2)), pl.BlockSpec((tq, 2 * DV), lambda h, i: (i, h // 2))],
        out_specs=[pl.BlockSpec((tq, HP), lambda h, i: (i, h)), pl.BlockSpec((TKV, HP), lambda h, i: (0, h)),
                   pl.BlockSpec((TKV, 2 * DV), lambda h, i: (0, h // 2))],
        out_shape=[jax.ShapeDtypeStruct((T, NH * HP), F32), jax.ShapeDtypeStruct((TKV, NH * HP), F32),
                   jax.ShapeDtypeStruct((TKV, NH * DV), F32)],
        scratch_shapes=[pltpu.VMEM((tq, TKV), F32), pltpu.VMEM((tq, TKV), F32)],
        compiler_params=pltpu.CompilerParams(dimension_semantics=("arbitrary", "arbitrary")),
    )(q, k, v, do)


def qprep_bwd(p, dq, qg, wq2, cq_t, sq_t, dp, *, name, tm=256):
    qcol = O_Q // 512

    def body(p_ref, dq_ref, g_ref, w_ref, c_ref, s_ref, dp_in, dp_ref, dq2_ref, dg_ref):
        i = pl.program_id(0)
        dqv = dq_ref[...]
        cc = jnp.concatenate([c_ref[...]] * NH, axis=1)
        ss = jnp.concatenate([s_ref[...]] * NH, axis=1)
        dq2 = jnp.concatenate([dqv * cc, dqv * ss], axis=1).astype(BF)
        dq2_ref[...] = dq2
        dcq = lax.dot_general(dq2, w_ref[...], (((1,), (1,)), ((), ())), preferred_element_type=F32)
        pq = p_ref[...]
        r = lax.rsqrt(jnp.sum(pq * pq, axis=-1, keepdims=True) * (1.0 / QL) + EPS)
        xh = pq * r
        a = dcq * g_ref[...]
        dp_ref[...] = (r * (a - xh * (jnp.sum(a * xh, axis=-1, keepdims=True) * (1.0 / QL)))).astype(BF)
        dg = jnp.sum(dcq * xh, axis=0, keepdims=True)

        @pl.when(i == 0)
        def _():
            dg_ref[...] = dg

        @pl.when(i > 0)
        def _():
            dg_ref[...] += dg

    return pl.pallas_call(
        body, name=name, grid=(T // tm,),
        in_specs=[pl.BlockSpec((tm, 512), lambda i: (i, qcol)), pl.BlockSpec((tm, NH * HP), lambda i: (i, 0)), _row(512),
                  pl.BlockSpec((512, 2 * NH * HP), lambda i: (0, 0)),
                  pl.BlockSpec((tm, HP), lambda i: (i, 0)), pl.BlockSpec((tm, HP), lambda i: (i, 0)),
                  pl.BlockSpec(memory_space=pl.ANY)],
        out_specs=[pl.BlockSpec((tm, 512), lambda i: (i, qcol)), pl.BlockSpec((tm, 2 * NH * HP), lambda i: (i, 0)), _row(512)],
        out_shape=[jax.ShapeDtypeStruct((T, NIN), BF), jax.ShapeDtypeStruct((T, 2 * NH * HP), BF),
                   jax.ShapeDtypeStruct((1, 512), F32)],
        input_output_aliases={6: 0},
        compiler_params=pltpu.CompilerParams(dimension_semantics=("arbitrary",)),
    )(p, dq, qg, wq2, cq_t, sq_t, dp)


def kvprep_bwd(pc, p, dk, dv, kvg, wkv2, ck, sk, dp, *, name, tm=256):
    assert tm == TC
    nb = TKV // tm
    kvcol = O_KV // 512

    def body(pc_ref, p_ref, dk_ref, dv_ref, g_ref, w_ref, ck_ref, sk_ref, dp_in, dp_ref, dpc_ref, dkv2_ref, dg_ref):
        i = pl.program_id(0)
        t = jnp.where(i == NLAT, pc_ref[...], p_ref[...])
        pk = t[:, :KVL]
        r = lax.rsqrt(jnp.mean(pk * pk, axis=-1, keepdims=True) + EPS)
        xh = pk * r
        dkv = dk_ref[...]
        dkv2 = jnp.concatenate([dkv, dv_ref[...]], axis=1).astype(BF)
        dkv2_ref[...] = dkv2
        dckv = lax.dot_general(dkv2, w_ref[...], (((1,), (1,)), ((), ())), preferred_element_type=F32)
        a = dckv * g_ref[...]
        dpk = r * (a - xh * jnp.mean(a * xh, axis=-1, keepdims=True))
        dkr = dkv[:, 0:HP]
        for hh in range(1, NH):
            dkr = dkr + dkv[:, hh * HP:(hh + 1) * HP]
        res = jnp.concatenate([dpk, dkr * ck_ref[...], dkr * sk_ref[...]], axis=1).astype(BF)
        dg = jnp.sum(dckv * xh, axis=0, keepdims=True)

        @pl.when(i == 0)
        def _():
            dg_ref[...] = dg

        @pl.when(i > 0)
        def _():
            dg_ref[...] += dg

        @pl.when(i < NLAT)
        def _():
            dp_ref[...] = res

        @pl.when(i == NLAT)
        def _():
            dpc_ref[...] = res

    rb = lambda w: pl.BlockSpec((tm, w), lambda i: (i, 0))
    return pl.pallas_call(
        body, name=name, grid=(nb,),
        in_specs=[pl.BlockSpec((tm, 512), lambda i: (0, 0)),
                  pl.BlockSpec((tm, 512), lambda i: (jnp.minimum(i, NLAT - 1), kvcol)),
                  rb(NH * HP), rb(NH * DV), _row(KVL), pl.BlockSpec((KVL, NH * HP + NH * DV), lambda i: (0, 0)),
                  rb(HP), rb(HP), pl.BlockSpec(memory_space=pl.ANY)],
        out_specs=[pl.BlockSpec((tm, 512), lambda i: (jnp.minimum(i, NLAT - 1), kvcol)),
                   pl.BlockSpec((tm, 512), lambda i: (0, 0)), rb(NH * HP + NH * DV), _row(KVL)],
        out_shape=[jax.ShapeDtypeStruct((T, NIN), BF), jax.ShapeDtypeStruct((TC, 512), BF),
                   jax.ShapeDtypeStruct((TKV, NH * HP + NH * DV), BF), jax.ShapeDtypeStruct((1, KVL), F32)],
        input_output_aliases={8: 0},
        compiler_params=pltpu.CompilerParams(dimension_semantics=("arbitrary",)),
    )(pc, p, dk, dv, kvg, wkv2, ck, sk, dp)


def _pieces(src, width, n):
    out, c = [], src
    while c < src + width:
        k = c // n
        w = min(src + width, (k + 1) * n) - c
        out.append((k, c - k * n, c - src, w))
        c += w
    return out


def _win_moves():
    mv = [(2208, 1024, O_GA), (3232, 1024, O_GC), (0, KVL, O_KV), (256, DR, O_KV + KVL + DN), (288, QL, O_Q)]
    mv += [(256 + _swap_start(g), 8, O_KV + KVL + HP + DN + 8 * g) for g in range(4)]
    for j in range(CONV // CVB):
        base = O_CV + 3 * CVB * j
        mv += [(672 + CVB * j, CVB, base), (1184 + CVB * j, CVB, base + CVB), (1696 + CVB * j, CVB, base + 2 * CVB)]
    return mv


_WIN_ZERO = [(O_KV + KVL, DN), (O_KV + KVL + DN + DR, HP - DN - DR), (O_KV + KVL + HP, DN),
             (O_KV + KVL + HP + DN + DR, HP - DN - DR), (O_Q + QL, 512 - QL)]


def build_win(g, *, name, tm=256):
    def body(g_ref, o_ref):
        for src, w, dst in _win_moves():
            for k, a, off, pw in _pieces(src, w, SH_IN):
                o_ref[:, dst + off:dst + off + pw] = g_ref[k, :, a:a + pw]
        for c0, w in _WIN_ZERO:
            o_ref[:, c0:c0 + w] = jnp.zeros((tm, w), o_ref.dtype)

    return pl.pallas_call(
        body, name=name, grid=(D // tm,), in_specs=[pl.BlockSpec((NDEV, tm, SH_IN), lambda i: (0, i, 0))],
        out_specs=pl.BlockSpec((tm, NIN), lambda i: (i, 0)), out_shape=jax.ShapeDtypeStruct((D, NIN), g.dtype),
        compiler_params=pltpu.CompilerParams(dimension_semantics=("parallel",)),
    )(g)


def shard_win_grad(dwt, dwct, *, name, tc=256):
    def body(dw_ref, dwc_ref, o_ref, kvs):
        kvs[...] = dw_ref[O_KV:O_KV + 512, :] + dwc_ref[...]

        def src(row, w):
            if O_KV <= row < O_KV + 512:
                return kvs[row - O_KV:row - O_KV + w, :]
            return dw_ref[row:row + w, :]

        for s, w, dst in _win_moves():
            if w == 8 or s == 256:
                continue
            for k, a, off, pw in _pieces(s, w, SH_IN):
                o_ref[k, a:a + pw, :] = src(dst + off, pw).astype(o_ref.dtype)
        for g in range(4):
            val = src(O_KV + KVL + DN + 8 * g, 8) + src(O_KV + KVL + HP + DN + _swap_start(g), 8)
            o_ref[0, 256 + 8 * g:256 + 8 * g + 8, :] = val.astype(o_ref.dtype)

    return pl.pallas_call(
        body, name=name, grid=(D // tc,),
        in_specs=[pl.BlockSpec((NIN, tc), lambda j: (0, j)), pl.BlockSpec((512, tc), lambda j: (0, j))],
        out_specs=pl.BlockSpec((NDEV, SH_IN, tc), lambda j: (0, 0, j)),
        out_shape=jax.ShapeDtypeStruct((NDEV, SH_IN, D), BF),
        scratch_shapes=[pltpu.VMEM((512, tc), F32)],
        compiler_params=pltpu.CompilerParams(dimension_semantics=("parallel",)),
    )(dwt, dwct)


def build_wq_wkv(gq, gkv, *, name):
    def body(gq_ref, gkv_ref, q_ref, kv_ref):
        q_ref[...] = jnp.zeros_like(q_ref)
        kv_ref[...] = jnp.zeros_like(kv_ref)
        for h in range(NH):
            q_ref[0:QL, h * HP:h * HP + DN + DR] = gq_ref[h]
            for g in range(4):
                c0 = NH * HP + h * HP + DN + 8 * g
                q_ref[0:QL, c0:c0 + 8] = gq_ref[h, :, DN + _swap_start(g):DN + _swap_start(g) + 8]
            kv_ref[:, h * HP:h * HP + DN] = gkv_ref[h, :, 0:DN]
            kv_ref[:, NH * HP + h * DV:NH * HP + (h + 1) * DV] = gkv_ref[h, :, DN:DN + DV]

    vm = pl.BlockSpec(memory_space=pltpu.VMEM)
    return pl.pallas_call(
        body, name=name, in_specs=[vm, vm], out_specs=[vm, vm],
        out_shape=[jax.ShapeDtypeStruct((512, 2 * NH * HP), gq.dtype), jax.ShapeDtypeStruct((KVL, NH * HP + NH * DV), gq.dtype)],
    )(gq, gkv)


def shard_wq_wkv_grad(dwq2, dwkv2, *, name):
    def body(q_ref, kv_ref, gq_ref, gkv_ref):
        for h in range(NH):
            gq_ref[h, :, 0:DN] = q_ref[0:QL, h * HP:h * HP + DN].astype(BF)
            for g in range(4):
                a = q_ref[0:QL, h * HP + DN + 8 * g:h * HP + DN + 8 * g + 8]
                c0 = NH * HP + h * HP + DN + _swap_start(g)
                gq_ref[h, :, DN + 8 * g:DN + 8 * g + 8] = (a + q_ref[0:QL, c0:c0 + 8]).astype(BF)
            gkv_ref[h, :, 0:DN] = kv_ref[:, h * HP:h * HP + DN].astype(BF)
            gkv_ref[h, :, DN:DN + DV] = kv_ref[:, NH * HP + h * DV:NH * HP + (h + 1) * DV].astype(BF)

    vm = pl.BlockSpec(memory_space=pltpu.VMEM)
    return pl.pallas_call(
        body, name=name, in_specs=[vm, vm], out_specs=[vm, vm],
        out_shape=[jax.ShapeDtypeStruct((NDEV, QL, (DN + DR)), BF), jax.ShapeDtypeStruct((NDEV, KVL, DN + DV), BF)],
    )(dwq2, dwkv2)


def unshard_cols(g, *, name, tm=256):
    _, K, n = g.shape
    tm = _pick(K, tm, 16)

    def body(g_ref, o_ref):
        for k in range(NDEV):
            o_ref[:, k * n:(k + 1) * n] = g_ref[k]

    return pl.pallas_call(
        body, name=name, grid=(K // tm,), in_specs=[pl.BlockSpec((NDEV, tm, n), lambda i: (0, i, 0))],
        out_specs=pl.BlockSpec((tm, NDEV * n), lambda i: (i, 0)), out_shape=jax.ShapeDtypeStruct((K, NDEV * n), g.dtype),
        compiler_params=pltpu.CompilerParams(dimension_semantics=("parallel",)),
    )(g)


def shard_cols(w, *, name, tm=256):
    K, n8 = w.shape
    n = n8 // NDEV
    tm = _pick(K, tm, 16)

    def body(w_ref, o_ref):
        for k in range(NDEV):
            o_ref[k] = w_ref[:, k * n:(k + 1) * n]

    return pl.pallas_call(
        body, name=name, grid=(K // tm,), in_specs=[pl.BlockSpec((tm, n8), lambda i: (i, 0))],
        out_specs=pl.BlockSpec((NDEV, tm, n), lambda i: (0, i, 0)), out_shape=jax.ShapeDtypeStruct((NDEV, K, n), w.dtype),
        compiler_params=pltpu.CompilerParams(dimension_semantics=("parallel",)),
    )(w)


def _rope_tables():
    t = np.arange(T)
    row = (t // GRID_W).astype(np.float32)
    col = (t % GRID_W).astype(np.float32)
    axis_dim = DR // 2
    inv = (np.float32(ROPE_THETA) ** (-np.arange(0, axis_dim, 2, dtype=np.float32) / np.float32(axis_dim))).astype(np.float32)
    ar, ac = (row[:, None] * inv).astype(np.float32), (col[:, None] * inv).astype(np.float32)
    cosv = np.concatenate([np.cos(ar), np.cos(ar), np.cos(ac), np.cos(ac)], axis=1).astype(np.float32)
    sinv = np.concatenate([-np.sin(ar), np.sin(ar), -np.sin(ac), np.sin(ac)], axis=1).astype(np.float32)
    ck = np.zeros((TKV, HP), np.float32)
    sk = np.zeros((TKV, HP), np.float32)
    ck[T:, DN:DN + DR] = 1.0
    ck[:T, DN:DN + DR] = cosv
    sk[:T, DN:DN + DR] = sinv
    cq = np.zeros((T, HP), np.float32)
    cq[:, :DN] = 1.0
    cq[:, DN:DN + DR] = cosv
    return jnp.asarray(ck), jnp.asarray(sk), jnp.asarray(cq), jnp.asarray(sk[:T])


def _local_step(x, ctx, tgt, mod_lat, mod_ctx, n1g, qg, kvg, n2g, fg, conv_w, conv_b, ffn_w, ffn_b, get_w, put_g, dep0):
    sh1, sc1, g1, sh2, sc2, g2 = [mod_lat[:, i * D:(i + 1) * D] for i in range(6)]
    csh1, csc1 = mod_ctx[:, 0:D], mod_ctx[:, D:2 * D]
    ck, sk, cq_t, sq_t = _rope_tables()
    qg_p = jnp.pad(qg, ((0, 0), (0, 512 - QL)))

    hcat = normmod_cat(ctx, x, n1g, csc1, csh1, sc1, sh1, dep0, name="normmod1")
    win = get_w("in", hcat)
    p = mm(hcat, win, M=T, tn=768, name="in_proj")
    pc = mm(hcat, win, M=TC, N=512, a_off=(T, 0), b_off=(0, O_KV), name="in_proj_ctx")
    wq2, wkv2, wao, wco, wo = get_w("mid", p)
    kh, vh, ckv = kvprep(pc, p, kvg, wkv2, ck, sk, name="kvprep")
    qr, cq = qprep(p, qg_p, wq2, cq_t, sq_t, name="qprep")
    o = attn_fwd(qr, kh, vh, name="attn_fwd")
    z = convz(p, conv_w, conv_b, name="convz")
    ya = mm(o, wao, name="attn_out")
    yc = mm(z, wco, name="conv_out")
    merged = gate_merge(p, ya, yc, name="gate_merge")
    a_out = mm(merged, wo, name="o_proj")
    x1, h2 = resid_normmod(x, a_out, g1, n2g, sc2, sh2, name="resid_normmod2")
    wup, wdn = get_w("ffn", h2)
    u0 = mm(h2, wup, tb=True, o_stack=True, tn=1408, name="up_proj")
    f = ffn_act(u0, ffn_w, ffn_b, name="ffn_act")
    dn = mm(f, wdn, tm=512, tk=DFF, name="down_proj")
    dx2, dd, dfg, loss = final_loss(x1, dn, g2, fg, tgt, name="final_loss")

    df = mm(dd, wdn, tb=True, tn=1408, name="down_proj_dx")
    dwdn = mm(f, dd, ta=True, out_dtype=BF, tm=1408, name="down_proj_dw")
    du0, dffn_w, dffn_b = ffn_act_bwd(u0, df, ffn_w, ffn_b, name="ffn_act_bwd")
    dwup = mm(du0, h2, ta=True, a_stack=True, out_dtype=BF, tm=1408, name="up_proj_dw")
    tok = put_g("ffn", dict(dwup=dwup, dwdn=dwdn))
    dh2 = mm(du0, wup, a_stack=True, dep=tok, name="up_proj_dx")
    dx1, da, st2 = normmod_bwd(x1, dh2, n2g, sc2, dx2, dn, g1, name="normmod2_bwd")

    dmerged = mm(da, wo, tb=True, name="o_proj_dx")
    dwo = mm(merged, da, ta=True, out_dtype=BF, tn=512, name="o_proj_dw")
    dya, dyc, dp = gate_merge_bwd(p, ya, yc, dmerged, name="gate_merge_bwd")
    do = mm(dya, wao, tb=True, out_dtype=BF, name="attn_out_dx")
    dwao = mm(o, dya, ta=True, out_dtype=BF, tn=512, name="attn_out_dw")
    dwco = mm(z, dyc, ta=True, out_dtype=BF, tn=512, name="conv_out_dw")
    tok = put_g("mid", dict(dwao=dwao, dwco=dwco, dwo=dwo))
    dz = mm(dyc, wco, tb=True, dep=tok, name="conv_out_dx")
    dp, dconv_w, dconv_b = convz_bwd(p, dz, conv_w, conv_b, dp, name="convz_bwd")
    dq, dk, dv = attn_bwd(qr, kh, vh, do, name="attn_bwd")
    dp, dq2, dqg = qprep_bwd(p, dq, qg_p, wq2, cq_t, sq_t, dp, name="qprep_bwd")
    dwq2 = mm(cq, dq2, ta=True, name="q_up_dw")
    dp, dpc, dkv2, dkvg = kvprep_bwd(pc, p, dk, dv, kvg, wkv2, ck, sk, dp, name="kvprep_bwd")
    dwkv2 = mm(ckv, dkv2, ta=True, name="kv_up_dw")
    tok = put_g("qkv", dict(dwq2=dwq2, dwkv2=dwkv2))

    dwin = mm(dp, hcat, ta=True, K=T, tm=768, dep=tok, name="in_proj_dw")
    dwin_c = mm(dpc, hcat, ta=True, K=TC, b_off=(T, 0), name="in_proj_ctx_dw")
    tok = put_g("in", dict(dwin=dwin, dwin_c=dwin_c))
    dh = mm(dp, win, tb=True, dep=tok, name="in_proj_dx")
    dhc = mm(dpc, win, tb=True, N=D, K=512, b_off=(0, O_KV), name="in_proj_ctx_dx")
    dx, _, st1 = normmod_bwd(x, dh, n1g, sc1, dx1, a_out, g1, name="normmod1_bwd")
    stc = normmod_bwd(ctx, dhc, n1g, csc1, None, None, None, name="normmod1_ctx_bwd")

    zrow = jnp.zeros((1, D), F32)
    dmod_lat = jnp.concatenate([st1[0:1], st1[1:2], st1[3:4], st2[0:1], st2[1:2], st2[3:4]], axis=1)
    dmod_ctx = jnp.concatenate([stc[0:1], stc[1:2], zrow, zrow, zrow, zrow], axis=1)
    return dict(
        loss=loss, dx=dx, dmod_lat=dmod_lat, dmod_ctx=dmod_ctx,
        dn1g=st1[2:3] + stc[2:3], dqg=dqg, dkvg=dkvg, dn2g=st2[2:3], dfg=dfg,
        dconv_w=dconv_w, dconv_b=dconv_b, dffn_w=dffn_w, dffn_b=dffn_b)


def _me():
    x, y, c = lax.axis_index("x"), lax.axis_index("y"), lax.axis_index("c")
    return x, y, c, 4 * x + 2 * y + c


def _peer(x, y, c, k):
    px = 1 - x if k & 4 else x
    py = 1 - y if k & 2 else y
    pc = 1 - c if k & 1 else c
    return (px, py, pc), 4 * px + 2 * py + pc


def _exchange_tiles(src_of_peer, buf, send_sem, recv_sem):
    x, y, c, me = _me()
    for k in range(1, NDEV):
        dev, lin = _peer(x, y, c, k)
        pltpu.make_async_remote_copy(src_ref=src_of_peer(lin), dst_ref=buf.at[me], send_sem=send_sem, recv_sem=recv_sem,
                                     device_id=dev, device_id_type=MESH).start()
    seven = buf.at[pl.ds(0, NDEV - 1)]
    pltpu.make_async_remote_copy(src_ref=seven, dst_ref=seven, send_sem=send_sem, recv_sem=recv_sem,
                                 device_id=(x, y, c), device_id_type=MESH).wait()


def _silu(z):
    return z * jax.nn.sigmoid(z)


def ada_fwd(c, c_ctx, ffn_w, conv_w, w_shard, b_shard, deps, *, name):
    nsh = w_shard.shape[1]
    deps = [d for d in deps if d is not None]

    def body(c_ref, cc_ref, fw_ref, cw_ref, w_ref, b_ref, *rest):
        s_ref, m_ref, mine, res, sems = rest[len(deps):]
        x, y, c, me = _me()
        mine[0:1, :] = _silu(c_ref[...])
        mine[1:2, :] = _silu(cc_ref[...])
        mine[2:5, :] = fw_ref[...]
        mine[5:8, :] = cw_ref[...]
        s_ref[me] = mine[...]
        _exchange_tiles(lambda lin: mine, s_ref, sems.at[0], sems.at[1])
        sall = s_ref[...].reshape(NDEV * 8, D).astype(BF)
        r = jnp.dot(sall, w_ref[...].astype(BF), preferred_element_type=F32) + b_ref[...]
        res[...] = r.reshape(NDEV, 8, nsh)
        m_ref[me] = res[me]
        _exchange_tiles(lambda lin: res.at[lin], m_ref, sems.at[2], sems.at[3])

    vm = pl.BlockSpec(memory_space=pltpu.VMEM)
    return pl.pallas_call(
        body, name=name, in_specs=[vm] * 6 + [pl.BlockSpec(memory_space=pl.ANY)] * len(deps), out_specs=[vm, vm],
        out_shape=[jax.ShapeDtypeStruct((NDEV, 8, D), F32), jax.ShapeDtypeStruct((NDEV, 8, nsh), F32)],
        scratch_shapes=[pltpu.VMEM((8, D), F32), pltpu.VMEM((NDEV, 8, nsh), F32), pltpu.SemaphoreType.DMA((4,))],
    )(c, c_ctx, ffn_w, conv_w, w_shard, b_shard, *deps)


P_DML, P_DMC, P_N1, P_QG, P_KVG, P_CB, P_N2, P_FB, P_FG, P_CW, P_FW, P_LOSS, P_ROWS = 0, 6, 12, 13, 14, 15, 16, 17, 23, 24, 27, 45, 48
FROWS = 3


def sync_small(r, deps, *, name):
    ins = [r["dmod_lat"], r["dmod_ctx"], r["dn1g"], r["dqg"], r["dkvg"], r["dconv_b"], r["dn2g"], r["dffn_b"], r["dfg"],
           r["dconv_w"], r["dffn_w"], r["loss"]]

    def put_wide(p, row0, row, n):
        for j in range(-(-n // D)):
            w = min(D, n - j * D)
            p[row0 + j:row0 + j + 1, 0:w] = row[:, j * D:j * D + w]

    def body(dml, dmc, n1, qg, kvg, cb, n2, fb, fg, cw, fw, loss, *rest):
        a_ref, sum_ref, p, sems = rest[len(deps):]
        x, y, c, me = _me()
        p[...] = jnp.zeros_like(p)
        put_wide(p, P_DML, dml, 6 * D)
        put_wide(p, P_DMC, dmc, 6 * D)
        put_wide(p, P_N1, n1, D)
        put_wide(p, P_QG, qg, 512)
        put_wide(p, P_KVG, kvg, KVL)
        put_wide(p, P_CB, cb, CONV)
        put_wide(p, P_N2, n2, D)
        put_wide(p, P_FG, fg, D)
        put_wide(p, P_LOSS, loss, 128)
        for s in range(2):
            put_wide(p, P_FB + FROWS * s, fb.at[s], DFF)
        for k in range(3):
            put_wide(p, P_CW + k, cw.at[k:k + 1], CONV)
            for s in range(2):
                put_wide(p, P_FW + FROWS * (2 * k + s), fw.at[s, k:k + 1], DFF)
        a_ref[me] = p[...]
        _exchange_tiles(lambda lin: p, a_ref, sems.at[0], sems.at[1])
        acc = a_ref[0]
        for k in range(1, NDEV):
            acc = acc + a_ref[k]
        sum_ref[...] = acc

    vm = pl.BlockSpec(memory_space=pltpu.VMEM)
    return pl.pallas_call(
        body, name=name, in_specs=[vm] * len(ins) + [pl.BlockSpec(memory_space=pl.ANY)] * len(deps), out_specs=[vm, vm],
        out_shape=[jax.ShapeDtypeStruct((NDEV, P_ROWS, D), F32), jax.ShapeDtypeStruct((P_ROWS, D), F32)],
        scratch_shapes=[pltpu.VMEM((P_ROWS, D), F32), pltpu.SemaphoreType.DMA((2,))],
    )(*ins, *deps)


def ada_bwd(s_all, dml, dmc, w_shard, c_ctx, *, name):
    nsh = w_shard.shape[1]

    def body(s_ref, dml_ref, dmc_ref, w_ref, c_ref, dw_ref, gc_ref, s16, dm16, part, buf, sems):
        x, y, c, me = _me()
        s16[...] = jnp.zeros_like(s16)
        dm16[...] = jnp.zeros_like(dm16)
        for k in range(NDEV):
            s16[k:k + 1, :] = s_ref[k, 0:1, :]
        s16[8:9, :] = s_ref[0, 1:2, :]
        dm16[0:8, :] = dml_ref[...]
        dm16[8:9, :] = dmc_ref[...]
        dw_ref[...] = lax.dot_general(s16[...].astype(BF), dm16[...].astype(BF), (((0,), (0,)), ((), ())),
                                      preferred_element_type=F32)
        part[...] = lax.dot_general(dm16[8:16, :].astype(BF), w_ref[...].astype(BF), (((1,), (1,)), ((), ())),
                                    preferred_element_type=F32)
        buf[me] = part[...]
        _exchange_tiles(lambda lin: part, buf, sems.at[0], sems.at[1])
        acc = buf[0]
        for k in range(1, NDEV):
            acc = acc + buf[k]
        z = c_ref[...]
        sg = jax.nn.sigmoid(z)
        gc_ref[...] = acc * (sg * (1.0 + z * (1.0 - sg)))

    vm = pl.BlockSpec(memory_space=pltpu.VMEM)
    return pl.pallas_call(
        body, name=name, in_specs=[vm] * 5, out_specs=[vm, vm],
        out_shape=[jax.ShapeDtypeStruct((D, nsh), F32), jax.ShapeDtypeStruct((8, D), F32)],
        scratch_shapes=[pltpu.VMEM((16, D), F32), pltpu.VMEM((16, nsh), F32), pltpu.VMEM((8, D), F32),
                        pltpu.VMEM((NDEV, 8, D), F32), pltpu.SemaphoreType.DMA((2,))],
    )(s_all, dml, dmc, w_shard, c_ctx)


HBM_SPEC = pl.BlockSpec(memory_space=pltpu.HBM)
SEM_SPEC = pl.BlockSpec(memory_space=pltpu.SEMAPHORE)
EFFECT = pltpu.SideEffectType.DATAFLOW_SIDE_EFFECTING


ALL_PEERS = tuple(range(1, NDEV))
FIRST_HOP = (1, 2, 4, 6)
RELAY = (2, 4, 6)


def _exchange_copies(srcs, lands, send, recv, per_peer, peers):
    x, y, c, me = _me()
    n = len(peers)
    cps = []
    for t in range(len(srcs)):
        for j, k in enumerate(peers):
            dev, lin = _peer(x, y, c, k)
            cps.append(pltpu.make_async_remote_copy(
                src_ref=srcs[t].at[lin] if per_peer else srcs[t], dst_ref=lands[t].at[me],
                send_sem=send.at[n * t + j], recv_sem=recv.at[n * t + j], device_id=dev, device_id_type=MESH))
    return cps


def _relay_copies(lands, send, recv):
    x, y, c, me = _me()
    n = len(RELAY)
    cps = []
    for t in range(len(lands)):
        for j, k in enumerate(RELAY):
            slot = lands[t].at[_peer(x, y, c, k)[1]]
            cps.append(pltpu.make_async_remote_copy(
                src_ref=slot, dst_ref=slot, send_sem=send.at[n * t + j], recv_sem=recv.at[n * t + j],
                device_id=(x, y, 1 - c), device_id_type=MESH))
    return cps


def _own_copies(srcs, lands, own, per_peer):
    me = _me()[3]
    return [pltpu.make_async_copy(srcs[t].at[me] if per_peer else srcs[t], lands[t].at[me], own.at[t])
            for t in range(len(srcs))]


def exchange_start(srcs, *, per_peer, name, dep=None, peers=ALL_PEERS):
    nt = len(srcs)
    ns = len(peers) * nt
    land_shapes = [(a.shape if per_peer else (NDEV,) + a.shape) for a in srcs]
    deps = [] if dep is None else [dep]

    def body(*refs):
        src, land = refs[:nt], refs[nt:2 * nt]
        send, recv, own = refs[2 * nt + len(deps):2 * nt + len(deps) + 3]
        for cp in _exchange_copies(src, land, send, recv, per_peer, peers) + _own_copies(src, land, own, per_peer):
            cp.start()
        refs[-1][...] = jnp.zeros_like(refs[-1])

    hb = lambda a: pltpu.with_memory_space_constraint(a, pltpu.HBM)
    outs = pl.pallas_call(
        body, name=name,
        out_shape=(pltpu.SemaphoreType.DMA((ns,)), pltpu.SemaphoreType.DMA((ns,)), pltpu.SemaphoreType.DMA((nt,)),
                   *[pltpu.HBM(a.shape, a.dtype) for a in srcs], *[pltpu.HBM(s, a.dtype) for s, a in zip(land_shapes, srcs)],
                   jax.ShapeDtypeStruct((8, 128), F32)),
        in_specs=[HBM_SPEC] * (2 * nt) + [pl.BlockSpec(memory_space=pl.ANY)] * len(deps),
        out_specs=(SEM_SPEC, SEM_SPEC, SEM_SPEC, *([HBM_SPEC] * (2 * nt)), pl.BlockSpec(memory_space=pltpu.VMEM)),
        input_output_aliases={i: 3 + i for i in range(2 * nt)},
        compiler_params=pltpu.CompilerParams(has_side_effects=EFFECT),
    )(*[hb(a) for a in srcs], *[hb(lax.empty(s, a.dtype)) for s, a in zip(land_shapes, srcs)], *deps)
    return dict(send=outs[0], recv=outs[1], own=outs[2], src=list(outs[3:3 + nt]), land=list(outs[3 + nt:3 + 2 * nt]),
                token=outs[-1], per_peer=per_peer, peers=peers)


def exchange_wait(h, after, *, name):
    nt = len(h["src"])
    per_peer, peers = h["per_peer"], h["peers"]

    def body(*refs):
        src, land, send, recv, own = refs[:nt], refs[nt:2 * nt], refs[2 * nt], refs[2 * nt + 1], refs[2 * nt + 2]
        for cp in _exchange_copies(src, land, send, recv, per_peer, peers):
            cp.wait_send()
            cp.wait_recv()
        for cp in _own_copies(src, land, own, per_peer):
            cp.wait()

    outs = pl.pallas_call(
        body, name=name,
        out_shape=(*[pltpu.HBM(a.shape, a.dtype) for a in h["src"]], *[pltpu.HBM(a.shape, a.dtype) for a in h["land"]]),
        in_specs=[HBM_SPEC] * (2 * nt) + [SEM_SPEC, SEM_SPEC, SEM_SPEC, pl.BlockSpec(memory_space=pl.ANY)],
        out_specs=tuple([HBM_SPEC] * (2 * nt)),
        input_output_aliases={i: i for i in range(2 * nt)},
        compiler_params=pltpu.CompilerParams(has_side_effects=EFFECT),
    )(*h["src"], *h["land"], h["send"], h["recv"], h["own"], after)
    return list(outs[nt:])


def relay_start(lands, *, name):
    nt = len(lands)
    ns = len(RELAY) * nt

    def body(*refs):
        for cp in _relay_copies(refs[:nt], refs[nt], refs[nt + 1]):
            cp.start()

    outs = pl.pallas_call(
        body, name=name,
        out_shape=(pltpu.SemaphoreType.DMA((ns,)), pltpu.SemaphoreType.DMA((ns,)),
                   *[pltpu.HBM(a.shape, a.dtype) for a in lands]),
        in_specs=[HBM_SPEC] * nt, out_specs=(SEM_SPEC, SEM_SPEC, *([HBM_SPEC] * nt)),
        input_output_aliases={i: 2 + i for i in range(nt)},
        compiler_params=pltpu.CompilerParams(has_side_effects=EFFECT),
    )(*lands)
    return dict(send=outs[0], recv=outs[1], land=list(outs[2:]))


def relay_wait(h, *, name):
    nt = len(h["land"])

    def body(*refs):
        for cp in _relay_copies(refs[:nt], refs[nt], refs[nt + 1]):
            cp.wait_send()
            cp.wait_recv()

    outs = pl.pallas_call(
        body, name=name, out_shape=tuple(pltpu.HBM(a.shape, a.dtype) for a in h["land"]),
        in_specs=[HBM_SPEC] * nt + [SEM_SPEC, SEM_SPEC], out_specs=tuple([HBM_SPEC] * nt),
        input_output_aliases={i: i for i in range(nt)},
        compiler_params=pltpu.CompilerParams(has_side_effects=EFFECT),
    )(*h["land"], h["send"], h["recv"])
    return list(outs)


def _adamw_math(w, g, m, v):
    nm = B1 * m + (1.0 - B1) * g
    nv = B2 * v + (1.0 - B2) * (g * g)
    m_hat = nm / (1.0 - B1 ** STEP)
    v_hat = nv / (1.0 - B2 ** STEP)
    return -LR * (m_hat / (jnp.sqrt(v_hat) + AEPS) + WD * w), nm, nv


def adamw_many(ws, gs, ms, vs, *, name):
    n = len(ws)

    def body(*refs):
        for k in range(n):
            d, nm, nv = _adamw_math(refs[k][...], refs[n + k][...], refs[2 * n + k][...], refs[3 * n + k][...])
            refs[4 * n + k][...] = d
            refs[5 * n + k][...] = nm
            refs[6 * n + k][...] = nv

    vm = pl.BlockSpec(memory_space=pltpu.VMEM)
    sh = [jax.ShapeDtypeStruct(w.shape, F32) for w in ws]
    outs = pl.pallas_call(body, name=name, in_specs=[vm] * (4 * n), out_specs=[vm] * (3 * n), out_shape=sh * 3,
                          )(*ws, *gs, *ms, *vs)
    return outs[:n], outs[n:2 * n], outs[2 * n:]


def adamw(w, g, m, v, *, name, tr=256):
    R, C = w.shape
    tr = _pick(R, tr, 8)

    def body(w_ref, g_ref, m_ref, v_ref, d_ref, nm_ref, nv_ref):
        d_ref[...], nm_ref[...], nv_ref[...] = _adamw_math(w_ref[...], g_ref[...], m_ref[...], v_ref[...])

    blk = pl.BlockSpec((tr, C), lambda i: (i, 0))
    sh = jax.ShapeDtypeStruct((R, C), F32)
    return pl.pallas_call(
        body, name=name, grid=(R // tr,), in_specs=[blk, blk, blk, blk], out_specs=[blk, blk, blk],
        out_shape=[sh, sh, sh], compiler_params=pltpu.CompilerParams(dimension_semantics=("parallel",)),
    )(w, g, m, v)


def adamw_slots(w, slots, m, v, *, name, tr=256):
    unit = w.ndim == 3
    R, C = w.shape[0], w.shape[-1]
    if R % 16 == 0:
        tr = _pick(R, tr, 16)
    else:
        tr = 144

    def body(w_ref, s_ref, m_ref, v_ref, g_ref, d_ref, nm_ref, nv_ref):
        g = s_ref[0].astype(F32)
        for k in range(1, NDEV):
            g = g + s_ref[k].astype(F32)
        g_ref[...] = g
        d_ref[...], nm_ref[...], nv_ref[...] = _adamw_math(w_ref[...], g, m_ref[...], v_ref[...])

    blk = pl.BlockSpec((tr, None, C), lambda i: (i, 0, 0)) if unit else pl.BlockSpec((tr, C), lambda i: (i, 0))
    sh = jax.ShapeDtypeStruct(w.shape, F32)
    return pl.pallas_call(
        body, name=name, grid=(pl.cdiv(R, tr),), in_specs=[blk, pl.BlockSpec((NDEV, tr, C), lambda i: (0, i, 0)), blk, blk],
        out_specs=[blk, blk, blk, blk], out_shape=[sh, sh, sh, sh],
        compiler_params=pltpu.CompilerParams(dimension_semantics=("parallel",)),
    )(w, slots, m, v)


def _padc(a, n=D):
    return jnp.pad(a, ((0, 0), (0, n - a.shape[1])))


def kernel(x, c, ctx, c_ctx, w_ada, b_ada, norm1_g, w_in, q_norm_g, kv_norm_g, w_uq, w_ukv, conv_w, conv_b, w_attn_out, w_conv_out, w_o, norm2_g, w_up, ffn_conv_w, ffn_conv_b, w_down, final_g, loss_target, m_c_ctx, m_w_ada, m_b_ada, m_norm1_g, m_w_in, m_q_norm_g, m_kv_norm_g, m_w_uq, m_w_ukv, m_conv_w, m_conv_b, m_w_attn_out, m_w_conv_out, m_w_o, m_norm2_g, m_w_up, m_ffn_conv_w, m_ffn_conv_b, m_w_down, m_final_g, v_c_ctx, v_w_ada, v_b_ada, v_norm1_g, v_w_in, v_q_norm_g, v_kv_norm_g, v_w_uq, v_w_ukv, v_conv_w, v_conv_b, v_w_attn_out, v_w_conv_out, v_w_o, v_norm2_g, v_w_up, v_ffn_conv_w, v_ffn_conv_b, v_w_down, v_final_g):
    me = 4 * lax.axis_index("x") + 2 * lax.axis_index("y") + lax.axis_index("c")
    W = dict(c_ctx=c_ctx, w_ada=w_ada, b_ada=b_ada, norm1_g=norm1_g, w_in=w_in, q_norm_g=q_norm_g, kv_norm_g=kv_norm_g,
             w_uq=w_uq, w_ukv=w_ukv, conv_w=conv_w, conv_b=conv_b, w_attn_out=w_attn_out, w_conv_out=w_conv_out, w_o=w_o,
             norm2_g=norm2_g, w_up=w_up, ffn_conv_w=ffn_conv_w, ffn_conv_b=ffn_conv_b, w_down=w_down, final_g=final_g)
    M = dict(c_ctx=m_c_ctx, w_ada=m_w_ada, b_ada=m_b_ada, norm1_g=m_norm1_g, w_in=m_w_in, q_norm_g=m_q_norm_g,
             kv_norm_g=m_kv_norm_g, w_uq=m_w_uq, w_ukv=m_w_ukv, conv_w=m_conv_w, conv_b=m_conv_b, w_attn_out=m_w_attn_out,
             w_conv_out=m_w_conv_out, w_o=m_w_o, norm2_g=m_norm2_g, w_up=m_w_up, ffn_conv_w=m_ffn_conv_w,
             ffn_conv_b=m_ffn_conv_b, w_down=m_w_down, final_g=m_final_g)
    V = dict(c_ctx=v_c_ctx, w_ada=v_w_ada, b_ada=v_b_ada, norm1_g=v_norm1_g, w_in=v_w_in, q_norm_g=v_q_norm_g,
             kv_norm_g=v_kv_norm_g, w_uq=v_w_uq, w_ukv=v_w_ukv, conv_w=v_conv_w, conv_b=v_conv_b, w_attn_out=v_w_attn_out,
             w_conv_out=v_w_conv_out, w_o=v_w_o, norm2_g=v_norm2_g, w_up=v_w_up, ffn_conv_w=v_ffn_conv_w,
             ffn_conv_b=v_ffn_conv_b, w_down=v_w_down, final_g=v_final_g)
    names = list(W)
    transposed = ("w_up",)
    as2d = lambda k, a: (a.reshape(1, -1) if a.ndim == 1 else
                         a[0].T if k in transposed else a.reshape(a.shape[-2], a.shape[-1]))
    W2 = {k: as2d(k, a) for k, a in W.items()}
    M2 = {k: as2d(k, a) for k, a in M.items()}
    V2 = {k: as2d(k, a) for k, a in V.items()}
    unit3 = lambda a: jnp.transpose(a, (2, 0, 1))
    W3, M3, V3 = unit3(W["w_in"]), unit3(M["w_in"]), unit3(V["w_in"])
    nsh = W2["w_ada"].shape[1]

    b_sh = lax.dynamic_slice(W2["b_ada"], (0, me * nsh), (1, nsh))
    s_all, m_all = ada_fwd(c, W2["c_ctx"], _padc(W2["ffn_conv_w"]), _padc(W2["conv_w"]), W2["w_ada"], b_sh, [],
                           name="ada_fwd")
    mod_lat = m_all[:, 0, :].reshape(1, 6 * D)
    mod_ctx = m_all[:, 1, :].reshape(1, 6 * D)
    ffn_w_full = s_all[:, 2:5, :2 * DFF // NDEV].transpose(1, 0, 2).reshape(3, 2 * DFF)
    conv_w_full = s_all[:, 5:8, :CONV // NDEV].transpose(1, 0, 2).reshape(3, CONV)

    stage_w = {"in": ["w_in"], "mid": ["w_uq", "w_ukv", "w_attn_out", "w_conv_out", "w_o"], "ffn": ["w_up", "w_down"]}
    two_level = ("in", "mid")
    ag, tok = {}, m_all
    for st, nms in stage_w.items():
        ag[st] = exchange_start([W2[nm].astype(BF) for nm in nms], per_peer=False, dep=tok, name="ag_start_" + st,
                                peers=FIRST_HOP if st in two_level else ALL_PEERS)
        tok = ag[st]["token"]

    def get_w(stage, after):
        lands = exchange_wait(ag[stage], after, name="ag_wait_" + stage)
        if stage in two_level:
            lands = relay_wait(relay_start(lands, name="ag_relay_" + stage), name="ag_relay_wait_" + stage)
        g = dict(zip(stage_w[stage], lands))
        if stage == "in":
            return build_win(g["w_in"], name="build_win")
        if stage == "mid":
            wq2, wkv2 = build_wq_wkv(g["w_uq"], g["w_ukv"], name="build_wq_wkv")
            return (wq2, wkv2, unshard_cols(g["w_attn_out"], name="unshard_w_attn_out"),
                    unshard_cols(g["w_conv_out"], name="unshard_w_conv_out"), g["w_o"].reshape(D, D))
        return g["w_up"].reshape(2 * DFF, D), g["w_down"].reshape(DFF, D)

    stage_g = {"ffn": ["w_up", "w_down"], "mid": ["w_attn_out", "w_conv_out", "w_o"], "qkv": ["w_uq", "w_ukv"],
               "in": ["w_in"]}
    rs = {}

    def put_g(stage, g):
        if stage == "in":
            parts = [shard_win_grad(g["dwin"], g["dwin_c"], name="shard_win_grad")]
        elif stage == "mid":
            parts = [shard_cols(g["dwao"], name="shard_w_attn_out"), shard_cols(g["dwco"], name="shard_w_conv_out"),
                     g["dwo"].reshape(NDEV, D // NDEV, D)]
        elif stage == "qkv":
            parts = list(shard_wq_wkv_grad(g["dwq2"], g["dwkv2"], name="shard_wq_wkv_grad"))
        else:
            parts = [g["dwup"].reshape(NDEV, 2 * DFF // NDEV, D), g["dwdn"].reshape(NDEV, DFF // NDEV, D)]
        rs[stage] = exchange_start(parts, per_peer=True, name="rs_start_" + stage)
        return rs[stage]["token"]

    r = _local_step(x[0], ctx[0], loss_target[0], mod_lat, mod_ctx, W2["norm1_g"], W2["q_norm_g"], W2["kv_norm_g"],
                    W2["norm2_g"], W2["final_g"], conv_w_full, W2["conv_b"], ffn_w_full, W2["ffn_conv_b"], get_w, put_g,
                    ag["ffn"]["token"])

    G, DL, NM, NV = {}, {}, {}, {}

    def finish(stage, after):
        for nm, sl in zip(stage_g[stage], exchange_wait(rs[stage], after, name="rs_wait_" + stage)):
            wmv = (W3, M3, V3) if nm == "w_in" else (W2[nm], M2[nm], V2[nm])
            G[nm], DL[nm], NM[nm], NV[nm] = adamw_slots(wmv[0], sl, wmv[1], wmv[2], name="adamw_" + nm)
            after = DL[nm]
        return after

    after = r["dx"]
    for st in ("ffn", "mid", "qkv"):
        after = finish(st, after)

    a_buf, ssum = sync_small(r, [DL[nm] for st in ("ffn", "mid", "qkv") for nm in stage_g[st]], name="sync_small")
    loss = ssum[P_LOSS, 0]
    G["norm1_g"] = ssum[P_N1:P_N1 + 1]
    G["q_norm_g"] = ssum[P_QG:P_QG + 1, :QL]
    G["kv_norm_g"] = ssum[P_KVG:P_KVG + 1, :KVL]
    G["conv_b"] = ssum[P_CB:P_CB + 1, :CONV]
    G["norm2_g"] = ssum[P_N2:P_N2 + 1]
    G["ffn_conv_b"] = ssum[P_FB:P_FB + 2 * FROWS].reshape(1, 2, FROWS * D)[:, :, :DFF].reshape(1, 2 * DFF)
    G["final_g"] = ssum[P_FG:P_FG + 1]
    G["conv_w"] = lax.dynamic_slice(ssum[P_CW:P_CW + 3, :CONV], (0, me * (CONV // NDEV)), (3, CONV // NDEV))
    fw_full = ssum[P_FW:P_FW + 6 * FROWS].reshape(3, 2, FROWS * D)[:, :, :DFF].reshape(3, 2 * DFF)
    G["ffn_conv_w"] = lax.dynamic_slice(fw_full, (0, me * (2 * DFF // NDEV)), (3, 2 * DFF // NDEV))
    G["b_ada"] = (ssum[P_DML:P_DML + 6] + ssum[P_DMC:P_DMC + 6]).reshape(1, 6 * D)

    dml = lax.dynamic_slice(a_buf[:, P_DML:P_DML + 6, :].reshape(NDEV, 6 * D), (0, me * nsh), (NDEV, nsh))
    dmc = lax.dynamic_slice(ssum[P_DMC:P_DMC + 6].reshape(1, 6 * D), (0, me * nsh), (1, nsh))
    G["w_ada"], gcc = ada_bwd(s_all, dml, dmc, W2["w_ada"], W2["c_ctx"], name="ada_bwd")
    G["c_ctx"] = gcc[0:1]

    DL["w_ada"], NM["w_ada"], NV["w_ada"] = adamw(W2["w_ada"], G["w_ada"], M2["w_ada"], V2["w_ada"], name="adamw_w_ada")
    small = ["c_ctx", "b_ada", "norm1_g", "q_norm_g", "kv_norm_g", "conv_b", "norm2_g", "ffn_conv_b", "final_g", "conv_w",
             "ffn_conv_w"]
    ds, nms, nvs = adamw_many([W2[k] for k in small], [G[k] for k in small], [M2[k] for k in small],
                              [V2[k] for k in small], name="adamw_small")
    for k, nm in enumerate(small):
        DL[nm], NM[nm], NV[nm] = ds[k], nms[k], nvs[k]
    finish("in", ds[0])

    outs = [loss, r["dx"][None]]
    for grp in (G, DL, NM, NV):
        outs += [grp[nm].T[None] if nm in transposed else
                 jnp.transpose(grp[nm], (1, 2, 0)) if nm == "w_in" else grp[nm].reshape(W[nm].shape) for nm in names]
    return tuple(outs)
```

```python
import functools
import numpy as np
import jax
import jax.numpy as jnp
from jax import lax
from jax.experimental import pallas as pl
from jax.experimental.pallas import tpu as pltpu

F32 = jnp.float32
BF = jnp.bfloat16
MESH = pl.DeviceIdType.MESH

D = 1024
T = 2048
TC = 256
TKV = T + TC
GRID_W = 64
NH = 8
DN = 64
DR = 32
DV = 64
QL = 384
KVL = 256
CONV = 512
DFF = 2816
EPS = 1e-6
ROPE_THETA = 10000.0
SCALE = (DN + DR) ** -0.5
NDEV = 8
HP = 128

O_GA, O_GC, O_KV, O_Q, O_CV = 0, 1024, 2048, 2560, 3072
NIN = 4608
CVB = 256
N_IN = 4256
SH_IN = N_IN // NDEV

LR, B1, B2, AEPS, WD, STEP = 0.001, 0.9, 0.999, 1e-08, 0.01, 10


def _pick(n, target, mult=128):
    best = None
    for d in range(mult, min(n, target) + 1, mult):
        if n % d == 0:
            best = d
    return best if best is not None else n


def _swap_start(g):
    return 8 * (g ^ 1)


def mm(a, b, *, ta=False, tb=False, out_dtype=F32, name, tm=1024, tn=1024, tk=2048, M=None, N=None, K=None,
       a_off=(0, 0), b_off=(0, 0), a_stack=False, b_stack=False, o_stack=False, dep=None):
    def dims(arr, stack):
        return (arr.shape[1], 2 * arr.shape[2]) if stack else arr.shape

    ar, ac = dims(a, a_stack)
    br, bc = dims(b, b_stack)
    M = M or ((ac if ta else ar) - a_off[1 if ta else 0])
    K = K or ((ar if ta else ac) - a_off[0 if ta else 1])
    N = N or ((br if tb else bc) - b_off[0 if tb else 1])
    tm = _pick(M, tm, 128 if ta else 16)
    tn = _pick(N // 2 if (o_stack or (b_stack and not tb)) else N, tn, 128)
    tk = _pick(K // 2 if ((a_stack and not ta) or (b_stack and tb)) else K, tk, 128)
    nk = K // tk
    ca = 0 if ta else 1
    cb = 1 if tb else 0

    def body(a_ref, b_ref, *rest):
        o_ref, acc = rest[-2:]
        k = pl.program_id(2)
        part = lax.dot_general(a_ref[...].astype(BF), b_ref[...].astype(BF),
                               (((ca,), (cb,)), ((), ())), preferred_element_type=F32)
        if nk == 1:
            o_ref[...] = part.astype(o_ref.dtype)
        else:
            @pl.when(k == 0)
            def _():
                acc[...] = part

            @pl.when(k > 0)
            def _():
                acc[...] += part

            @pl.when(k == nk - 1)
            def _():
                o_ref[...] = acc[...].astype(o_ref.dtype)

    def spec(blk, rc, off, stack, ncols):
        assert off[0] % blk[0] == 0 and off[1] % blk[1] == 0, (name, blk, off)
        ro, co = off[0] // blk[0], off[1] // blk[1]
        if not stack:
            return pl.BlockSpec(blk, lambda i, j, k: (rc(i, j, k)[0] + ro, rc(i, j, k)[1] + co))
        nhb = ncols // 2 // blk[1]
        return pl.BlockSpec((None,) + blk,
                            lambda i, j, k: ((rc(i, j, k)[1] + co) // nhb, rc(i, j, k)[0] + ro, (rc(i, j, k)[1] + co) % nhb))

    a_spec = spec((tk, tm), lambda i, j, k: (k, i), a_off, a_stack, ac) if ta else \
        spec((tm, tk), lambda i, j, k: (i, k), a_off, a_stack, ac)
    b_spec = spec((tn, tk), lambda i, j, k: (j, k), b_off, b_stack, bc) if tb else \
        spec((tk, tn), lambda i, j, k: (k, j), b_off, b_stack, bc)
    o_spec = spec((tm, tn), lambda i, j, k: (i, j), (0, 0), o_stack, N)
    o_shape = (2, M, N // 2) if o_stack else (M, N)
    deps = [] if dep is None else [dep]
    return pl.pallas_call(
        body, name=name, grid=(M // tm, N // tn, nk),
        in_specs=[a_spec, b_spec] + [pl.BlockSpec(memory_space=pl.ANY)] * len(deps),
        out_specs=o_spec, out_shape=jax.ShapeDtypeStruct(o_shape, out_dtype),
        scratch_shapes=[pltpu.VMEM((tm, tn) if nk > 1 else (8, 128), F32)],
        compiler_params=pltpu.CompilerParams(dimension_semantics=("parallel", "parallel", "arbitrary")),
    )(a, b, *deps)


def _row(width):
    return pl.BlockSpec((1, width), lambda *_: (0, 0))


NLAT = T // TC


def normmod_cat(ctx, x, g, csc, csh, sc, sh, dep, *, name, tm=256):
    assert tm == TC

    def body(c_ref, x_ref, g_ref, csc_ref, csh_ref, sc_ref, sh_ref, dep_ref, h_ref):
        last = pl.program_id(0) == NLAT
        xv = jnp.where(last, c_ref[...], x_ref[...])
        scv = jnp.where(last, csc_ref[...], sc_ref[...])
        shv = jnp.where(last, csh_ref[...], sh_ref[...])
        r = lax.rsqrt(jnp.mean(xv * xv, axis=-1, keepdims=True) + EPS)
        h_ref[...] = ((xv * r * g_ref[...]) * (1.0 + scv) + shv).astype(BF)

    return pl.pallas_call(
        body, name=name, grid=(TKV // tm,),
        in_specs=[pl.BlockSpec((tm, D), lambda i: (0, 0)), pl.BlockSpec((tm, D), lambda i: (jnp.minimum(i, NLAT - 1), 0)),
                  _row(D), _row(D), _row(D), _row(D), _row(D), pl.BlockSpec(memory_space=pl.ANY)],
        out_specs=pl.BlockSpec((tm, D), lambda i: (i, 0)), out_shape=jax.ShapeDtypeStruct((TKV, D), BF),
        compiler_params=pltpu.CompilerParams(dimension_semantics=("parallel",)),
    )(ctx, x, g, csc, csh, sc, sh, dep)


def resid_normmod(x, a, gate, g, sc, sh, *, name, tm=256):
    R = x.shape[0]

    def body(x_ref, a_ref, gate_ref, g_ref, sc_ref, sh_ref, x1_ref, h_ref):
        xv = x_ref[...] + gate_ref[...] * a_ref[...]
        x1_ref[...] = xv
        r = lax.rsqrt(jnp.mean(xv * xv, axis=-1, keepdims=True) + EPS)
        h_ref[...] = ((xv * r * g_ref[...]) * (1.0 + sc_ref[...]) + sh_ref[...]).astype(BF)

    blk = pl.BlockSpec((tm, D), lambda i: (i, 0))
    return pl.pallas_call(
        body, name=name, grid=(R // tm,), in_specs=[blk, blk, _row(D), _row(D), _row(D), _row(D)],
        out_specs=[blk, blk],
        out_shape=[jax.ShapeDtypeStruct((R, D), F32), jax.ShapeDtypeStruct((R, D), BF)],
        compiler_params=pltpu.CompilerParams(dimension_semantics=("parallel",)),
    )(x, a, gate, g, sc, sh)


def kvprep(pc, p, kvg, wkv2, ck, sk, *, name, tm=256):
    assert tm == TC
    nb = TKV // tm
    kvcol = O_KV // 512

    def body(pc_ref, p_ref, g_ref, w_ref, ck_ref, sk_ref, k_ref, v_ref, ckv_ref):
        i = pl.program_id(0)
        t = jnp.where(i == NLAT, pc_ref[...], p_ref[...])
        pk = t[:, :KVL]
        r = lax.rsqrt(jnp.mean(pk * pk, axis=-1, keepdims=True) + EPS)
        ckv = (pk * r * g_ref[...]).astype(BF)
        ckv_ref[...] = ckv
        kv2 = jnp.dot(ckv, w_ref[...], preferred_element_type=F32)
        krr = t[:, KVL:KVL + HP] * ck_ref[...] + t[:, KVL + HP:KVL + 2 * HP] * sk_ref[...]
        k_ref[...] = (kv2[:, :NH * HP] + jnp.concatenate([krr] * NH, axis=1)).astype(BF)
        v_ref[...] = kv2[:, NH * HP:].astype(BF)

    return pl.pallas_call(
        body, name=name, grid=(nb,),
        in_specs=[pl.BlockSpec((tm, 512), lambda i: (0, 0)),
                  pl.BlockSpec((tm, 512), lambda i: (jnp.minimum(i, NLAT - 1), kvcol)),
                  _row(KVL), pl.BlockSpec((KVL, NH * HP + NH * DV), lambda i: (0, 0)),
                  pl.BlockSpec((tm, HP), lambda i: (i, 0)), pl.BlockSpec((tm, HP), lambda i: (i, 0))],
        out_specs=[pl.BlockSpec((tm, NH * HP), lambda i: (i, 0)), pl.BlockSpec((tm, NH * DV), lambda i: (i, 0)),
                   pl.BlockSpec((tm, KVL), lambda i: (i, 0))],
        out_shape=[jax.ShapeDtypeStruct((TKV, NH * HP), BF), jax.ShapeDtypeStruct((TKV, NH * DV), BF),
                   jax.ShapeDtypeStruct((TKV, KVL), BF)],
        compiler_params=pltpu.CompilerParams(dimension_semantics=("parallel",)),
    )(pc, p, kvg, wkv2, ck, sk)


def qprep(p, qg, wq2, cq_t, sq_t, *, name, tm=256):
    qcol = O_Q // 512

    def body(p_ref, g_ref, w_ref, c_ref, s_ref, q_ref, cq_ref):
        pq = p_ref[...]
        r = lax.rsqrt(jnp.sum(pq * pq, axis=-1, keepdims=True) * (1.0 / QL) + EPS)
        cq = (pq * r * g_ref[...]).astype(BF)
        cq_ref[...] = cq
        q2 = jnp.dot(cq, w_ref[...], preferred_element_type=F32)
        cc = jnp.concatenate([c_ref[...]] * NH, axis=1)
        ss = jnp.concatenate([s_ref[...]] * NH, axis=1)
        q_ref[...] = (q2[:, :NH * HP] * cc + q2[:, NH * HP:] * ss).astype(BF)

    return pl.pallas_call(
        body, name=name, grid=(T // tm,),
        in_specs=[pl.BlockSpec((tm, 512), lambda i: (i, qcol)), _row(512),
                  pl.BlockSpec((512, 2 * NH * HP), lambda i: (0, 0)),
                  pl.BlockSpec((tm, HP), lambda i: (i, 0)), pl.BlockSpec((tm, HP), lambda i: (i, 0))],
        out_specs=[pl.BlockSpec((tm, NH * HP), lambda i: (i, 0)), pl.BlockSpec((tm, 512), lambda i: (i, 0))],
        out_shape=[jax.ShapeDtypeStruct((T, NH * HP), BF), jax.ShapeDtypeStruct((T, 512), BF)],
        compiler_params=pltpu.CompilerParams(dimension_semantics=("parallel",)),
    )(p, qg, wq2, cq_t, sq_t)


def _head_mask(h):
    lanes = lax.broadcasted_iota(jnp.int32, (1, 2 * DV), 1)
    return (lanes // DV) == (h % 2)


LOG2E = 1.4426950408889634


def _scores_pass(q, k_ref, s_scr, kc):
    m = None
    for c in range(TKV // kc):
        s = lax.dot_general(q, k_ref[c * kc:(c + 1) * kc, :], (((1,), (1,)), ((), ())),
                            preferred_element_type=F32) * (SCALE * LOG2E)
        s_scr[:, c * kc:(c + 1) * kc] = s
        mc = jnp.max(s, axis=-1, keepdims=True)
        m = mc if m is None else jnp.maximum(m, mc)
    return m


def attn_fwd(q, k, v, *, name, tq=256, kc=256):
    def body(q_ref, k_ref, v_ref, o_ref, s_scr):
        h = pl.program_id(1)
        m = _scores_pass(q_ref[...], k_ref, s_scr, kc)
        l = jnp.zeros((tq, 1), F32)
        acc = jnp.zeros((tq, 2 * DV), F32)
        for c in range(TKV // kc):
            e = jnp.exp2(s_scr[:, c * kc:(c + 1) * kc] - m)
            l = l + jnp.sum(e, axis=-1, keepdims=True)
            acc = acc + jnp.dot(e.astype(BF), v_ref[c * kc:(c + 1) * kc, :], preferred_element_type=F32)
        o2 = jnp.where(_head_mask(h), acc * (1.0 / l), 0.0).astype(BF)

        @pl.when(h % 2 == 0)
        def _():
            o_ref[...] = o2

        @pl.when(h % 2 == 1)
        def _():
            o_ref[...] = o_ref[...] + o2

    return pl.pallas_call(
        body, name=name, grid=(T // tq, NH),
        in_specs=[pl.BlockSpec((tq, HP), lambda i, h: (i, h)), pl.BlockSpec((TKV, HP), lambda i, h: (0, h)),
                  pl.BlockSpec((TKV, 2 * DV), lambda i, h: (0, h // 2))],
        out_specs=pl.BlockSpec((tq, 2 * DV), lambda i, h: (i, h // 2)),
        out_shape=jax.ShapeDtypeStruct((T, NH * DV), BF),
        scratch_shapes=[pltpu.VMEM((tq, TKV), F32)],
        compiler_params=pltpu.CompilerParams(dimension_semantics=("parallel", "arbitrary")),
    )(q, k, v)


def _shift_dn(x):
    n = x.shape[0]
    rows = lax.broadcasted_iota(jnp.int32, (n, 1), 0)
    return jnp.where(rows == 0, 0.0, pltpu.roll(x, 1, axis=0))


def _shift_up(x):
    n = x.shape[0]
    rows = lax.broadcasted_iota(jnp.int32, (n, 1), 0)
    return jnp.where(rows == n - 1, 0.0, pltpu.roll(x, n - 1, axis=0))


def _conv(x, w_ref, b_ref):
    return b_ref[...] + _shift_dn(x) * w_ref[0:1, :] + x * w_ref[1:2, :] + _shift_up(x) * w_ref[2:3, :]


def _conv_t(dy, w_ref):
    return _shift_up(dy) * w_ref[0:1, :] + dy * w_ref[1:2, :] + _shift_dn(dy) * w_ref[2:3, :]


def _conv_wgrad(dw_ref, dy, x):
    dw_ref[0:1, :] = jnp.sum(dy * _shift_dn(x), axis=0, keepdims=True)
    dw_ref[1:2, :] = jnp.sum(dy * x, axis=0, keepdims=True)
    dw_ref[2:3, :] = jnp.sum(dy * _shift_up(x), axis=0, keepdims=True)


def convz(p, cw, cb, *, name):
    o0 = O_CV // (3 * CVB)

    def body(p_ref, w_ref, bias_ref, z_ref):
        xv, bv, cv = p_ref[:, 0:CVB], p_ref[:, CVB:2 * CVB], p_ref[:, 2 * CVB:3 * CVB]
        z_ref[...] = (bv * _conv(cv * xv, w_ref, bias_ref)).astype(BF)

    return pl.pallas_call(
        body, name=name, grid=(CONV // CVB,),
        in_specs=[pl.BlockSpec((T, 3 * CVB), lambda j: (0, o0 + j)), pl.BlockSpec((3, CVB), lambda j: (0, j)),
                  pl.BlockSpec((1, CVB), lambda j: (0, j))],
        out_specs=pl.BlockSpec((T, CVB), lambda j: (0, j)),
        out_shape=jax.ShapeDtypeStruct((T, CONV), BF),
        compiler_params=pltpu.CompilerParams(dimension_semantics=("parallel",)),
    )(p, cw, cb)


def gate_merge(p, ya, yc, *, name, tm=256):
    def body(ga_ref, gc_ref, ya_ref, yc_ref, o_ref):
        o_ref[...] = (jax.nn.sigmoid(ga_ref[...]) * ya_ref[...] + jax.nn.sigmoid(gc_ref[...]) * yc_ref[...]).astype(BF)

    blk = pl.BlockSpec((tm, D), lambda i: (i, 0))
    return pl.pallas_call(
        body, name=name, grid=(T // tm,),
        in_specs=[pl.BlockSpec((tm, D), lambda i: (i, O_GA // D)), pl.BlockSpec((tm, D), lambda i: (i, O_GC // D)), blk, blk],
        out_specs=blk, out_shape=jax.ShapeDtypeStruct((T, D), BF),
        compiler_params=pltpu.CompilerParams(dimension_semantics=("parallel",)),
    )(p, p, ya, yc)


CONV_HALO = 8
CONV_ROWS = 256


def _row_chunks(n, chunk, carry):
    carry = chunk(0, True, False, carry)
    carry = lax.fori_loop(1, n // CONV_ROWS - 1, lambda c, a: chunk(c * CONV_ROWS, False, False, a), carry)
    return chunk(n - CONV_ROWS, False, True, carry)


def _ext_rows(ref, r0, first, last):
    n, w = ref.shape
    zero = jnp.zeros((CONV_HALO, w), ref.dtype)
    if first:
        return jnp.concatenate([zero, ref[0:CONV_ROWS + CONV_HALO, :]], axis=0)
    if last:
        return jnp.concatenate([ref[n - CONV_ROWS - CONV_HALO:n, :], zero], axis=0)
    return ref[pl.ds(pl.multiple_of(r0 - CONV_HALO, 8), CONV_ROWS + 2 * CONV_HALO), :]


def _center_rows(r0, first, last):
    return slice(r0, r0 + CONV_ROWS) if (first or last) else pl.ds(pl.multiple_of(r0, 8), CONV_ROWS)


def _roll_dn(x):
    return pltpu.roll(x, 1, axis=0)


def _roll_up(x):
    return pltpu.roll(x, x.shape[0] - 1, axis=0)


_CTR = slice(CONV_HALO, CONV_HALO + CONV_ROWS)


def ffn_act(u0, cw, cb, *, name, tc=256):
    nb = DFF // tc

    def body(u_ref, wg_ref, wv_ref, bg_ref, bv_ref, f_ref):
        wg = [wg_ref[k:k + 1, :] for k in range(3)]
        wv = [wv_ref[k:k + 1, :] for k in range(3)]
        bg, bv = bg_ref[...], bv_ref[...]

        def chunk(r0, first, last, carry):
            xg, xv = _ext_rows(u_ref.at[0], r0, first, last), _ext_rows(u_ref.at[1], r0, first, last)
            ug = bg + _roll_dn(xg) * wg[0] + xg * wg[1] + _roll_up(xg) * wg[2]
            uv = bv + _roll_dn(xv) * wv[0] + xv * wv[1] + _roll_up(xv) * wv[2]
            f_ref[_center_rows(r0, first, last), :] = (ug * jax.nn.sigmoid(ug) * uv)[_CTR].astype(BF)
            return carry

        _row_chunks(T, chunk, 0)

    return pl.pallas_call(
        body, name=name, grid=(nb,),
        in_specs=[pl.BlockSpec((2, T, tc), lambda j: (0, 0, j)),
                  pl.BlockSpec((3, tc), lambda j: (0, j)), pl.BlockSpec((3, tc), lambda j: (0, nb + j)),
                  pl.BlockSpec((1, tc), lambda j: (0, j)), pl.BlockSpec((1, tc), lambda j: (0, nb + j))],
        out_specs=pl.BlockSpec((T, tc), lambda j: (0, j)),
        out_shape=jax.ShapeDtypeStruct((T, DFF), BF),
        compiler_params=pltpu.CompilerParams(dimension_semantics=("parallel",)),
    )(u0, cw, cw, cb, cb)


def final_loss(x1, d, g2, fg, tgt, *, name, tm=256):
    def body(x1_ref, d_ref, g2_ref, fg_ref, t_ref, dx_ref, dd_ref, dfg_ref, loss_ref):
        i = pl.program_id(0)
        xv = x1_ref[...] + g2_ref[...] * d_ref[...]
        r = lax.rsqrt(jnp.mean(xv * xv, axis=-1, keepdims=True) + EPS)
        xh = xv * r
        diff = xh * fg_ref[...] - t_ref[...]
        part = 0.5 * jnp.sum(jnp.mean(diff * diff, axis=-1, keepdims=True), axis=0, keepdims=True)
        dy = diff * (1.0 / D)
        a = dy * fg_ref[...]
        dx = r * (a - xh * jnp.mean(a * xh, axis=-1, keepdims=True))
        dx_ref[...] = dx
        dd_ref[...] = (dx * g2_ref[...]).astype(BF)
        dfg = jnp.sum(dy * xh, axis=0, keepdims=True)

        @pl.when(i == 0)
        def _():
            dfg_ref[...] = dfg
            loss_ref[...] = jnp.broadcast_to(part, (1, 128))

        @pl.when(i > 0)
        def _():
            dfg_ref[...] += dfg
            loss_ref[...] += jnp.broadcast_to(part, (1, 128))

    blk = pl.BlockSpec((tm, D), lambda i: (i, 0))
    return pl.pallas_call(
        body, name=name, grid=(T // tm,), in_specs=[blk, blk, _row(D), _row(D), blk],
        out_specs=[blk, blk, _row(D), _row(128)],
        out_shape=[jax.ShapeDtypeStruct((T, D), F32), jax.ShapeDtypeStruct((T, D), BF),
                   jax.ShapeDtypeStruct((1, D), F32), jax.ShapeDtypeStruct((1, 128), F32)],
        compiler_params=pltpu.CompilerParams(dimension_semantics=("arbitrary",)),
    )(x1, d, g2, fg, tgt)


def normmod_bwd(x, dh, g, sc, dres, gsrc, gate, *, name, tm=256):
    R = x.shape[0]
    has_res = dres is not None

    def body(*refs):
        if has_res:
            x_ref, dh_ref, g_ref, sc_ref, dres_ref, gsrc_ref, gate_ref, dx_ref, dxg_ref, st_ref = refs
        else:
            x_ref, dh_ref, g_ref, sc_ref, st_ref = refs
        i = pl.program_id(0)
        xv = x_ref[...]
        r = lax.rsqrt(jnp.mean(xv * xv, axis=-1, keepdims=True) + EPS)
        xh = xv * r
        dhv = dh_ref[...]
        n = xh * g_ref[...]
        dn = dhv * (1.0 + sc_ref[...])
        a = dn * g_ref[...]
        rows = [jnp.sum(dhv, axis=0, keepdims=True), jnp.sum(dhv * n, axis=0, keepdims=True),
                jnp.sum(dn * xh, axis=0, keepdims=True)]
        if has_res:
            dr = dres_ref[...]
            dx = dr + r * (a - xh * jnp.mean(a * xh, axis=-1, keepdims=True))
            dx_ref[...] = dx
            dxg_ref[...] = (dx * gate_ref[...]).astype(BF)
            rows.append(jnp.sum(dr * gsrc_ref[...], axis=0, keepdims=True))
        else:
            rows.append(jnp.zeros((1, D), F32))

        @pl.when(i == 0)
        def _():
            for k, row in enumerate(rows):
                st_ref[k:k + 1, :] = row

        @pl.when(i > 0)
        def _():
            for k, row in enumerate(rows):
                st_ref[k:k + 1, :] += row

    blk = pl.BlockSpec((tm, D), lambda i: (i, 0))
    st_spec = pl.BlockSpec((4, D), lambda i: (0, 0))
    st_shape = jax.ShapeDtypeStruct((4, D), F32)
    cp = pltpu.CompilerParams(dimension_semantics=("arbitrary",))
    if has_res:
        return pl.pallas_call(
            body, name=name, grid=(R // tm,), in_specs=[blk, blk, _row(D), _row(D), blk, blk, _row(D)],
            out_specs=[blk, blk, st_spec],
            out_shape=[jax.ShapeDtypeStruct((R, D), F32), jax.ShapeDtypeStruct((R, D), BF), st_shape],
            compiler_params=cp,
        )(x, dh, g, sc, dres, gsrc, gate)
    return pl.pallas_call(
        body, name=name, grid=(R // tm,), in_specs=[blk, blk, _row(D), _row(D)],
        out_specs=st_spec, out_shape=st_shape, compiler_params=cp,
    )(x, dh, g, sc)


def ffn_act_bwd(u0, df, cw, cb, *, name, tc=128):
    nb = DFF // tc

    def body(u_ref, df_ref, wg_ref, wv_ref, bg_ref, bv_ref, du_ref, dw_ref, db_ref):
        wg = [wg_ref[k:k + 1, :] for k in range(3)]
        wv = [wv_ref[k:k + 1, :] for k in range(3)]
        bg, bv = bg_ref[...], bv_ref[...]

        def chunk(r0, first, last, acc):
            xg, xv = _ext_rows(u_ref.at[0], r0, first, last), _ext_rows(u_ref.at[1], r0, first, last)
            dfe = _ext_rows(df_ref, r0, first, last)
            xg_d, xg_u, xv_d, xv_u = _roll_dn(xg), _roll_up(xg), _roll_dn(xv), _roll_up(xv)
            ug = bg + xg_d * wg[0] + xg * wg[1] + xg_u * wg[2]
            uv = bv + xv_d * wv[0] + xv * wv[1] + xv_u * wv[2]
            sig = jax.nn.sigmoid(ug)
            dug = dfe * uv * (sig * (1.0 + ug * (1.0 - sig)))
            duv = dfe * (ug * sig)
            rows = _center_rows(r0, first, last)
            du_ref[0, rows, :] = (_roll_up(dug) * wg[0] + dug * wg[1] + _roll_dn(dug) * wg[2])[_CTR].astype(BF)
            du_ref[1, rows, :] = (_roll_up(duv) * wv[0] + duv * wv[1] + _roll_dn(duv) * wv[2])[_CTR].astype(BF)
            terms = [dug * xg_d, dug * xg, dug * xg_u, dug, duv * xv_d, duv * xv, duv * xv_u, duv]
            return tuple(a + jnp.sum(t[_CTR], axis=0, keepdims=True) for a, t in zip(acc, terms))

        acc = _row_chunks(T, chunk, tuple(jnp.zeros((1, tc), F32) for _ in range(8)))
        for k in range(3):
            dw_ref[0, k:k + 1, :] = acc[k]
            dw_ref[1, k:k + 1, :] = acc[4 + k]
        db_ref[0] = acc[3]
        db_ref[1] = acc[7]

    lo = lambda r: pl.BlockSpec((r, tc), lambda j: (0, j))
    hi = lambda r: pl.BlockSpec((r, tc), lambda j: (0, nb + j))
    st = lambda r: pl.BlockSpec((2, r, tc), lambda j: (0, 0, j))
    return pl.pallas_call(
        body, name=name, grid=(nb,),
        in_specs=[st(T), lo(T), lo(3), hi(3), lo(1), hi(1)],
        out_specs=[st(T), st(3), st(1)],
        out_shape=[jax.ShapeDtypeStruct((2, T, DFF), BF), jax.ShapeDtypeStruct((2, 3, DFF), F32),
                   jax.ShapeDtypeStruct((2, 1, DFF), F32)],
        compiler_params=pltpu.CompilerParams(dimension_semantics=("parallel",)),
    )(u0, df, cw, cw, cb, cb)


def gate_merge_bwd(p, ya, yc, dm, *, name, tm=256):
    def body(ga_ref, gc_ref, ya_ref, yc_ref, dm_ref, dya_ref, dyc_ref, dp_ref):
        sa, sc_ = jax.nn.sigmoid(ga_ref[...]), jax.nn.sigmoid(gc_ref[...])
        dmv = dm_ref[...]
        dya_ref[...] = (dmv * sa).astype(BF)
        dyc_ref[...] = (dmv * sc_).astype(BF)
        dp_ref[:, 0:D] = (dmv * ya_ref[...] * (sa * (1.0 - sa))).astype(BF)
        dp_ref[:, D:2 * D] = (dmv * yc_ref[...] * (sc_ * (1.0 - sc_))).astype(BF)

    blk = pl.BlockSpec((tm, D), lambda i: (i, 0))
    sh = jax.ShapeDtypeStruct((T, D), BF)
    return pl.pallas_call(
        body, name=name, grid=(T // tm,),
        in_specs=[pl.BlockSpec((tm, D), lambda i: (i, O_GA // D)), pl.BlockSpec((tm, D), lambda i: (i, O_GC // D)), blk, blk, blk],
        out_specs=[blk, blk, pl.BlockSpec((tm, 2 * D), lambda i: (i, 0))],
        out_shape=[sh, sh, jax.ShapeDtypeStruct((T, NIN), BF)],
        compiler_params=pltpu.CompilerParams(dimension_semantics=("parallel",)),
    )(p, p, ya, yc, dm)


def convz_bwd(p, dz, cw, cb, dp, *, name):
    o0 = O_CV // (3 * CVB)

    def body(p_ref, dz_ref, w_ref, bias_ref, dp_in, dp_ref, dw_ref, dbias_ref):
        xv, bv, cv = p_ref[:, 0:CVB], p_ref[:, CVB:2 * CVB], p_ref[:, 2 * CVB:3 * CVB]
        ci = cv * xv
        dwc = _conv(ci, w_ref, bias_ref)
        dzv = dz_ref[...]
        ddw = dzv * bv
        dci = _conv_t(ddw, w_ref)
        dp_ref[:, 0:CVB] = (dci * cv).astype(BF)
        dp_ref[:, CVB:2 * CVB] = (dzv * dwc).astype(BF)
        dp_ref[:, 2 * CVB:3 * CVB] = (dci * xv).astype(BF)
        _conv_wgrad(dw_ref, ddw, ci)
        dbias_ref[...] = jnp.sum(ddw, axis=0, keepdims=True)

    own = lambda r: pl.BlockSpec((r, CVB), lambda j: (0, j))
    return pl.pallas_call(
        body, name=name, grid=(CONV // CVB,),
        in_specs=[pl.BlockSpec((T, 3 * CVB), lambda j: (0, o0 + j)), own(T), own(3), own(1),
                  pl.BlockSpec(memory_space=pl.ANY)],
        out_specs=[pl.BlockSpec((T, 3 * CVB), lambda j: (0, o0 + j)), own(3), own(1)],
        out_shape=[jax.ShapeDtypeStruct((T, NIN), BF), jax.ShapeDtypeStruct((3, CONV), F32),
                   jax.ShapeDtypeStruct((1, CONV), F32)],
        input_output_aliases={4: 0},
        compiler_params=pltpu.CompilerParams(dimension_semantics=("parallel",)),
    )(p, dz, cw, cb, dp)


def attn_bwd(q, k, v, do, *, name, tq=1024, kc=768):
    NKC, KC = TKV // kc, kc

    def body(q_ref, k_ref, v_ref, do_ref, dq_ref, dk_ref, dv_ref, s_scr, dp_scr):
        h, i = pl.program_id(0), pl.program_id(1)

        @pl.when(i == 0)
        def _():
            dk_ref[...] = jnp.zeros_like(dk_ref)

        @pl.when((i == 0) & (h % 2 == 0))
        def _():
            dv_ref[...] = jnp.zeros_like(dv_ref)

        qv = q_ref[...]
        dom = jnp.where(_head_mask(h), do_ref[...], jnp.zeros_like(do_ref[...]))
        m = _scores_pass(qv, k_ref, s_scr, kc)
        l = jnp.zeros((tq, 1), F32)
        dsum = jnp.zeros((tq, 1), F32)
        for c in range(NKC):
            cols = slice(c * KC, (c + 1) * KC)
            e = jnp.exp2(s_scr[:, cols] - m)
            s_scr[:, cols] = e
            dp = lax.dot_general(dom, v_ref[cols, :], (((1,), (1,)), ((), ())), preferred_element_type=F32)
            dp_scr[:, cols] = dp
            l = l + jnp.sum(e, axis=-1, keepdims=True)
            dsum = dsum + jnp.sum(e * dp, axis=-1, keepdims=True)
        inv = 1.0 / l
        delta = dsum * inv
        dos = (dom.astype(F32) * inv).astype(BF)
        dq = jnp.zeros((tq, HP), F32)
        for c in range(NKC):
            cols = slice(c * KC, (c + 1) * KC)
            e = s_scr[:, cols]
            ds = (e * (dp_scr[:, cols] - delta) * (inv * SCALE)).astype(BF)
            dq = dq + jnp.dot(ds, k_ref[cols, :], preferred_element_type=F32)
            dk_ref[cols, :] += lax.dot_general(ds, qv, (((0,), (0,)), ((), ())), preferred_element_type=F32)
            dv_ref[cols, :] += lax.dot_general(e.astype(BF), dos, (((0,), (0,)), ((), ())), preferred_element_type=F32)
        dq_ref[...] = dq

    return pl.pallas_call(
        body, name=name, grid=(NH, T // tq),
        in_specs=[pl.BlockSpec((tq, HP), lambda h, i: (i, h)), pl.BlockSpec((TKV, HP), lambda h, i: (0, h)),
                  pl.BlockSpec((TKV, 2 * DV), lambda h, i: (0, h // 2)), pl.BlockSpec((tq, 2 * DV), lambda h, i: (i, h // 2))],
        out_specs=[pl.BlockSpec((tq, HP), lambda h, i: (i, h)), pl.BlockSpec((TKV, HP), lambda h, i: (0, h)),
                   pl.BlockSpec((TKV, 2 * DV), lambda h, i: (0, h // 2))],
        out_shape=[jax.ShapeDtypeStruct((T, NH * HP), F32), jax.ShapeDtypeStruct((TKV, NH * HP), F32),
                   jax.ShapeDtypeStruct((TKV, NH * DV), F32)],
        scratch_shapes=[pltpu.VMEM((tq, TKV), F32), pltpu.VMEM((tq, TKV), F32)],
        compiler_params=pltpu.CompilerParams(dimension_semantics=("arbitrary", "arbitrary")),
    )(q, k, v, do)


def qprep_bwd(p, dq, qg, wq2, cq_t, sq_t, dp, *, name, tm=256):
    qcol = O_Q // 512

    def body(p_ref, dq_ref, g_ref, w_ref, c_ref, s_ref, dp_in, dp_ref, dq2_ref, dg_ref):
        i = pl.program_id(0)
        dqv = dq_ref[...]
        cc = jnp.concatenate([c_ref[...]] * NH, axis=1)
        ss = jnp.concatenate([s_ref[...]] * NH, axis=1)
        dq2 = jnp.concatenate([dqv * cc, dqv * ss], axis=1).astype(BF)
        dq2_ref[...] = dq2
        dcq = lax.dot_general(dq2, w_ref[...], (((1,), (1,)), ((), ())), preferred_element_type=F32)
        pq = p_ref[...]
        r = lax.rsqrt(jnp.sum(pq * pq, axis=-1, keepdims=True) * (1.0 / QL) + EPS)
        xh = pq * r
        a = dcq * g_ref[...]
        dp_ref[...] = (r * (a - xh * (jnp.sum(a * xh, axis=-1, keepdims=True) * (1.0 / QL)))).astype(BF)
        dg = jnp.sum(dcq * xh, axis=0, keepdims=True)

        @pl.when(i == 0)
        def _():
            dg_ref[...] = dg

        @pl.when(i > 0)
        def _():
            dg_ref[...] += dg

    return pl.pallas_call(
        body, name=name, grid=(T // tm,),
        in_specs=[pl.BlockSpec((tm, 512), lambda i: (i, qcol)), pl.BlockSpec((tm, NH * HP), lambda i: (i, 0)), _row(512),
                  pl.BlockSpec((512, 2 * NH * HP), lambda i: (0, 0)),
                  pl.BlockSpec((tm, HP), lambda i: (i, 0)), pl.BlockSpec((tm, HP), lambda i: (i, 0)),
                  pl.BlockSpec(memory_space=pl.ANY)],
        out_specs=[pl.BlockSpec((tm, 512), lambda i: (i, qcol)), pl.BlockSpec((tm, 2 * NH * HP), lambda i: (i, 0)), _row(512)],
        out_shape=[jax.ShapeDtypeStruct((T, NIN), BF), jax.ShapeDtypeStruct((T, 2 * NH * HP), BF),
                   jax.ShapeDtypeStruct((1, 512), F32)],
        input_output_aliases={6: 0},
        compiler_params=pltpu.CompilerParams(dimension_semantics=("arbitrary",)),
    )(p, dq, qg, wq2, cq_t, sq_t, dp)


def kvprep_bwd(pc, p, dk, dv, kvg, wkv2, ck, sk, dp, *, name, tm=256):
    assert tm == TC
    nb = TKV // tm
    kvcol = O_KV // 512

    def body(pc_ref, p_ref, dk_ref, dv_ref, g_ref, w_ref, ck_ref, sk_ref, dp_in, dp_ref, dpc_ref, dkv2_ref, dg_ref):
        i = pl.program_id(0)
        t = jnp.where(i == NLAT, pc_ref[...], p_ref[...])
        pk = t[:, :KVL]
        r = lax.rsqrt(jnp.mean(pk * pk, axis=-1, keepdims=True) + EPS)
        xh = pk * r
        dkv = dk_ref[...]
        dkv2 = jnp.concatenate([dkv, dv_ref[...]], axis=1).astype(BF)
        dkv2_ref[...] = dkv2
        dckv = lax.dot_general(dkv2, w_ref[...], (((1,), (1,)), ((), ())), preferred_element_type=F32)
        a = dckv * g_ref[...]
        dpk = r * (a - xh * jnp.mean(a * xh, axis=-1, keepdims=True))
        dkr = dkv[:, 0:HP]
        for hh in range(1, NH):
            dkr = dkr + dkv[:, hh * HP:(hh + 1) * HP]
        res = jnp.concatenate([dpk, dkr * ck_ref[...], dkr * sk_ref[...]], axis=1).astype(BF)
        dg = jnp.sum(dckv * xh, axis=0, keepdims=True)

        @pl.when(i == 0)
        def _():
            dg_ref[...] = dg

        @pl.when(i > 0)
        def _():
            dg_ref[...] += dg

        @pl.when(i < NLAT)
        def _():
            dp_ref[...] = res

        @pl.when(i == NLAT)
        def _():
            dpc_ref[...] = res

    rb = lambda w: pl.BlockSpec((tm, w), lambda i: (i, 0))
    return pl.pallas_call(
        body, name=name, grid=(nb,),
        in_specs=[pl.BlockSpec((tm, 512), lambda i: (0, 0)),
                  pl.BlockSpec((tm, 512), lambda i: (jnp.minimum(i, NLAT - 1), kvcol)),
                  rb(NH * HP), rb(NH * DV), _row(KVL), pl.BlockSpec((KVL, NH * HP + NH * DV), lambda i: (0, 0)),
                  rb(HP), rb(HP), pl.BlockSpec(memory_space=pl.ANY)],
        out_specs=[pl.BlockSpec((tm, 512), lambda i: (jnp.minimum(i, NLAT - 1), kvcol)),
                   pl.BlockSpec((tm, 512), lambda i: (0, 0)), rb(NH * HP + NH * DV), _row(KVL)],
        out_shape=[jax.ShapeDtypeStruct((T, NIN), BF), jax.ShapeDtypeStruct((TC, 512), BF),
                   jax.ShapeDtypeStruct((TKV, NH * HP + NH * DV), BF), jax.ShapeDtypeStruct((1, KVL), F32)],
        input_output_aliases={8: 0},
        compiler_params=pltpu.CompilerParams(dimension_semantics=("arbitrary",)),
    )(pc, p, dk, dv, kvg, wkv2, ck, sk, dp)


def _pieces(src, width, n):
    out, c = [], src
    while c < src + width:
        k = c // n
        w = min(src + width, (k + 1) * n) - c
        out.append((k, c - k * n, c - src, w))
        c += w
    return out


def _win_moves():
    mv = [(2208, 1024, O_GA), (3232, 1024, O_GC), (0, KVL, O_KV), (256, DR, O_KV + KVL + DN), (288, QL, O_Q)]
    mv += [(256 + _swap_start(g), 8, O_KV + KVL + HP + DN + 8 * g) for g in range(4)]
    for j in range(CONV // CVB):
        base = O_CV + 3 * CVB * j
        mv += [(672 + CVB * j, CVB, base), (1184 + CVB * j, CVB, base + CVB), (1696 + CVB * j, CVB, base + 2 * CVB)]
    return mv


_WIN_ZERO = [(O_KV + KVL, DN), (O_KV + KVL + DN + DR, HP - DN - DR), (O_KV + KVL + HP, DN),
             (O_KV + KVL + HP + DN + DR, HP - DN - DR), (O_Q + QL, 512 - QL)]


def build_win(g, *, name, tm=256):
    def body(g_ref, o_ref):
        for src, w, dst in _win_moves():
            for k, a, off, pw in _pieces(src, w, SH_IN):
                o_ref[:, dst + off:dst + off + pw] = g_ref[k, :, a:a + pw]
        for c0, w in _WIN_ZERO:
            o_ref[:, c0:c0 + w] = jnp.zeros((tm, w), o_ref.dtype)

    return pl.pallas_call(
        body, name=name, grid=(D // tm,), in_specs=[pl.BlockSpec((NDEV, tm, SH_IN), lambda i: (0, i, 0))],
        out_specs=pl.BlockSpec((tm, NIN), lambda i: (i, 0)), out_shape=jax.ShapeDtypeStruct((D, NIN), g.dtype),
        compiler_params=pltpu.CompilerParams(dimension_semantics=("parallel",)),
    )(g)


def shard_win_grad(dwt, dwct, *, name, tc=256):
    def body(dw_ref, dwc_ref, o_ref, kvs):
        kvs[...] = dw_ref[O_KV:O_KV + 512, :] + dwc_ref[...]

        def src(row, w):
            if O_KV <= row < O_KV + 512:
                return kvs[row - O_KV:row - O_KV + w, :]
            return dw_ref[row:row + w, :]

        for s, w, dst in _win_moves():
            if w == 8 or s == 256:
                continue
            for k, a, off, pw in _pieces(s, w, SH_IN):
                o_ref[k, a:a + pw, :] = src(dst + off, pw).astype(o_ref.dtype)
        for g in range(4):
            val = src(O_KV + KVL + DN + 8 * g, 8) + src(O_KV + KVL + HP + DN + _swap_start(g), 8)
            o_ref[0, 256 + 8 * g:256 + 8 * g + 8, :] = val.astype(o_ref.dtype)

    return pl.pallas_call(
        body, name=name, grid=(D // tc,),
        in_specs=[pl.BlockSpec((NIN, tc), lambda j: (0, j)), pl.BlockSpec((512, tc), lambda j: (0, j))],
        out_specs=pl.BlockSpec((NDEV, SH_IN, tc), lambda j: (0, 0, j)),
        out_shape=jax.ShapeDtypeStruct((NDEV, SH_IN, D), BF),
        scratch_shapes=[pltpu.VMEM((512, tc), F32)],
        compiler_params=pltpu.CompilerParams(dimension_semantics=("parallel",)),
    )(dwt, dwct)


def build_wq_wkv(gq, gkv, *, name):
    def body(gq_ref, gkv_ref, q_ref, kv_ref):
        q_ref[...] = jnp.zeros_like(q_ref)
        kv_ref[...] = jnp.zeros_like(kv_ref)
        for h in range(NH):
            q_ref[0:QL, h * HP:h * HP + DN + DR] = gq_ref[h]
            for g in range(4):
                c0 = NH * HP + h * HP + DN + 8 * g
                q_ref[0:QL, c0:c0 + 8] = gq_ref[h, :, DN + _swap_start(g):DN + _swap_start(g) + 8]
            kv_ref[:, h * HP:h * HP + DN] = gkv_ref[h, :, 0:DN]
            kv_ref[:, NH * HP + h * DV:NH * HP + (h + 1) * DV] = gkv_ref[h, :, DN:DN + DV]

    vm = pl.BlockSpec(memory_space=pltpu.VMEM)
    return pl.pallas_call(
        body, name=name, in_specs=[vm, vm], out_specs=[vm, vm],
        out_shape=[jax.ShapeDtypeStruct((512, 2 * NH * HP), gq.dtype), jax.ShapeDtypeStruct((KVL, NH * HP + NH * DV), gq.dtype)],
    )(gq, gkv)


def shard_wq_wkv_grad(dwq2, dwkv2, *, name):
    def body(q_ref, kv_ref, gq_ref, gkv_ref):
        for h in range(NH):
            gq_ref[h, :, 0:DN] = q_ref[0:QL, h * HP:h * HP + DN].astype(BF)
            for g in range(4):
                a = q_ref[0:QL, h * HP + DN + 8 * g:h * HP + DN + 8 * g + 8]
                c0 = NH * HP + h * HP + DN + _swap_start(g)
                gq_ref[h, :, DN + 8 * g:DN + 8 * g + 8] = (a + q_ref[0:QL, c0:c0 + 8]).astype(BF)
            gkv_ref[h, :, 0:DN] = kv_ref[:, h * HP:h * HP + DN].astype(BF)
            gkv_ref[h, :, DN:DN + DV] = kv_ref[:, NH * HP + h * DV:NH * HP + (h + 1) * DV].astype(BF)

    vm = pl.BlockSpec(memory_space=pltpu.VMEM)
    return pl.pallas_call(
        body, name=name, in_specs=[vm, vm], out_specs=[vm, vm],
        out_shape=[jax.ShapeDtypeStruct((NDEV, QL, (DN + DR)), BF), jax.ShapeDtypeStruct((NDEV, KVL, DN + DV), BF)],
    )(dwq2, dwkv2)


def unshard_cols(g, *, name, tm=256):
    _, K, n = g.shape
    tm = _pick(K, tm, 16)

    def body(g_ref, o_ref):
        for k in range(NDEV):
            o_ref[:, k * n:(k + 1) * n] = g_ref[k]

    return pl.pallas_call(
        body, name=name, grid=(K // tm,), in_specs=[pl.BlockSpec((NDEV, tm, n), lambda i: (0, i, 0))],
        out_specs=pl.BlockSpec((tm, NDEV * n), lambda i: (i, 0)), out_shape=jax.ShapeDtypeStruct((K, NDEV * n), g.dtype),
        compiler_params=pltpu.CompilerParams(dimension_semantics=("parallel",)),
    )(g)


def shard_cols(w, *, name, tm=256):
    K, n8 = w.shape
    n = n8 // NDEV
    tm = _pick(K, tm, 16)

    def body(w_ref, o_ref):
        for k in range(NDEV):
            o_ref[k] = w_ref[:, k * n:(k + 1) * n]

    return pl.pallas_call(
        body, name=name, grid=(K // tm,), in_specs=[pl.BlockSpec((tm, n8), lambda i: (i, 0))],
        out_specs=pl.BlockSpec((NDEV, tm, n), lambda i: (0, i, 0)), out_shape=jax.ShapeDtypeStruct((NDEV, K, n), w.dtype),
        compiler_params=pltpu.CompilerParams(dimension_semantics=("parallel",)),
    )(w)


def _rope_tables():
    t = np.arange(T)
    row = (t // GRID_W).astype(np.float32)
    col = (t % GRID_W).astype(np.float32)
    axis_dim = DR // 2
    inv = (np.float32(ROPE_THETA) ** (-np.arange(0, axis_dim, 2, dtype=np.float32) / np.float32(axis_dim))).astype(np.float32)
    ar, ac = (row[:, None] * inv).astype(np.float32), (col[:, None] * inv).astype(np.float32)
    cosv = np.concatenate([np.cos(ar), np.cos(ar), np.cos(ac), np.cos(ac)], axis=1).astype(np.float32)
    sinv = np.concatenate([-np.sin(ar), np.sin(ar), -np.sin(ac), np.sin(ac)], axis=1).astype(np.float32)
    ck = np.zeros((TKV, HP), np.float32)
    sk = np.zeros((TKV, HP), np.float32)
    ck[T:, DN:DN + DR] = 1.0
    ck[:T, DN:DN + DR] = cosv
    sk[:T, DN:DN + DR] = sinv
    cq = np.zeros((T, HP), np.float32)
    cq[:, :DN] = 1.0
    cq[:, DN:DN + DR] = cosv
    return jnp.asarray(ck), jnp.asarray(sk), jnp.asarray(cq), jnp.asarray(sk[:T])


def _local_step(x, ctx, tgt, mod_lat, mod_ctx, n1g, qg, kvg, n2g, fg, conv_w, conv_b, ffn_w, ffn_b, get_w, put_g, dep0):
    sh1, sc1, g1, sh2, sc2, g2 = [mod_lat[:, i * D:(i + 1) * D] for i in range(6)]
    csh1, csc1 = mod_ctx[:, 0:D], mod_ctx[:, D:2 * D]
    ck, sk, cq_t, sq_t = _rope_tables()
    qg_p = jnp.pad(qg, ((0, 0), (0, 512 - QL)))

    hcat = normmod_cat(ctx, x, n1g, csc1, csh1, sc1, sh1, dep0, name="normmod1")
    win = get_w("in", hcat)
    p = mm(hcat, win, M=T, tn=768, name="in_proj")
    pc = mm(hcat, win, M=TC, N=512, a_off=(T, 0), b_off=(0, O_KV), name="in_proj_ctx")
    wq2, wkv2, wao, wco, wo = get_w("mid", p)
    kh, vh, ckv = kvprep(pc, p, kvg, wkv2, ck, sk, name="kvprep")
    qr, cq = qprep(p, qg_p, wq2, cq_t, sq_t, name="qprep")
    o = attn_fwd(qr, kh, vh, name="attn_fwd")
    z = convz(p, conv_w, conv_b, name="convz")
    ya = mm(o, wao, name="attn_out")
    yc = mm(z, wco, name="conv_out")
    merged = gate_merge(p, ya, yc, name="gate_merge")
    a_out = mm(merged, wo, name="o_proj")
    x1, h2 = resid_normmod(x, a_out, g1, n2g, sc2, sh2, name="resid_normmod2")
    wup, wdn = get_w("ffn", h2)
    u0 = mm(h2, wup, tb=True, o_stack=True, tn=1408, name="up_proj")
    f = ffn_act(u0, ffn_w, ffn_b, name="ffn_act")
    dn = mm(f, wdn, tm=512, tk=DFF, name="down_proj")
    dx2, dd, dfg, loss = final_loss(x1, dn, g2, fg, tgt, name="final_loss")

    df = mm(dd, wdn, tb=True, tn=1408, name="down_proj_dx")
    dwdn = mm(f, dd, ta=True, out_dtype=BF, tm=1408, name="down_proj_dw")
    du0, dffn_w, dffn_b = ffn_act_bwd(u0, df, ffn_w, ffn_b, name="ffn_act_bwd")
    dwup = mm(du0, h2, ta=True, a_stack=True, out_dtype=BF, tm=1408, name="up_proj_dw")
    tok = put_g("ffn", dict(dwup=dwup, dwdn=dwdn))
    dh2 = mm(du0, wup, a_stack=True, dep=tok, name="up_proj_dx")
    dx1, da, st2 = normmod_bwd(x1, dh2, n2g, sc2, dx2, dn, g1, name="normmod2_bwd")

    dmerged = mm(da, wo, tb=True, name="o_proj_dx")
    dwo = mm(merged, da, ta=True, out_dtype=BF, tn=512, name="o_proj_dw")
    dya, dyc, dp = gate_merge_bwd(p, ya, yc, dmerged, name="gate_merge_bwd")
    do = mm(dya, wao, tb=True, out_dtype=BF, name="attn_out_dx")
    dwao = mm(o, dya, ta=True, out_dtype=BF, tn=512, name="attn_out_dw")
    dwco = mm(z, dyc, ta=True, out_dtype=BF, tn=512, name="conv_out_dw")
    tok = put_g("mid", dict(dwao=dwao, dwco=dwco, dwo=dwo))
    dz = mm(dyc, wco, tb=True, dep=tok, name="conv_out_dx")
    dp, dconv_w, dconv_b = convz_bwd(p, dz, conv_w, conv_b, dp, name="convz_bwd")
    dq, dk, dv = attn_bwd(qr, kh, vh, do, name="attn_bwd")
    dp, dq2, dqg = qprep_bwd(p, dq, qg_p, wq2, cq_t, sq_t, dp, name="qprep_bwd")
    dwq2 = mm(cq, dq2, ta=True, name="q_up_dw")
    dp, dpc, dkv2, dkvg = kvprep_bwd(pc, p, dk, dv, kvg, wkv2, ck, sk, dp, name="kvprep_bwd")
    dwkv2 = mm(ckv, dkv2, ta=True, name="kv_up_dw")
    tok = put_g("qkv", dict(dwq2=dwq2, dwkv2=dwkv2))

    dwin = mm(dp, hcat, ta=True, K=T, tm=768, dep=tok, name="in_proj_dw")
    dwin_c = mm(dpc, hcat, ta=True, K=TC, b_off=(T, 0), name="in_proj_ctx_dw")
    tok = put_g("in", dict(dwin=dwin, dwin_c=dwin_c))
    dh = mm(dp, win, tb=True, dep=tok, name="in_proj_dx")
    dhc = mm(dpc, win, tb=True, N=D, K=512, b_off=(0, O_KV), name="in_proj_ctx_dx")
    dx, _, st1 = normmod_bwd(x, dh, n1g, sc1, dx1, a_out, g1, name="normmod1_bwd")
    stc = normmod_bwd(ctx, dhc, n1g, csc1, None, None, None, name="normmod1_ctx_bwd")

    zrow = jnp.zeros((1, D), F32)
    dmod_lat = jnp.concatenate([st1[0:1], st1[1:2], st1[3:4], st2[0:1], st2[1:2], st2[3:4]], axis=1)
    dmod_ctx = jnp.concatenate([stc[0:1], stc[1:2], zrow, zrow, zrow, zrow], axis=1)
    return dict(
        loss=loss, dx=dx, dmod_lat=dmod_lat, dmod_ctx=dmod_ctx,
        dn1g=st1[2:3] + stc[2:3], dqg=dqg, dkvg=dkvg, dn2g=st2[2:3], dfg=dfg,
        dconv_w=dconv_w, dconv_b=dconv_b, dffn_w=dffn_w, dffn_b=dffn_b)


def _me():
    x, y, c = lax.axis_index("x"), lax.axis_index("y"), lax.axis_index("c")
    return x, y, c, 4 * x + 2 * y + c


def _peer(x, y, c, k):
    px = 1 - x if k & 4 else x
    py = 1 - y if k & 2 else y
    pc = 1 - c if k & 1 else c
    return (px, py, pc), 4 * px + 2 * py + pc


def _exchange_tiles(src_of_peer, buf, send_sem, recv_sem):
    x, y, c, me = _me()
    for k in range(1, NDEV):
        dev, lin = _peer(x, y, c, k)
        pltpu.make_async_remote_copy(src_ref=src_of_peer(lin), dst_ref=buf.at[me], send_sem=send_sem, recv_sem=recv_sem,
                                     device_id=dev, device_id_type=MESH).start()
    seven = buf.at[pl.ds(0, NDEV - 1)]
    pltpu.make_async_remote_copy(src_ref=seven, dst_ref=seven, send_sem=send_sem, recv_sem=recv_sem,
                                 device_id=(x, y, c), device_id_type=MESH).wait()


def _silu(z):
    return z * jax.nn.sigmoid(z)


def ada_fwd(c, c_ctx, ffn_w, conv_w, w_shard, b_shard, deps, *, name):
    nsh = w_shard.shape[1]
    deps = [d for d in deps if d is not None]

    def body(c_ref, cc_ref, fw_ref, cw_ref, w_ref, b_ref, *rest):
        s_ref, m_ref, mine, res, sems = rest[len(deps):]
        x, y, c, me = _me()
        mine[0:1, :] = _silu(c_ref[...])
        mine[1:2, :] = _silu(cc_ref[...])
        mine[2:5, :] = fw_ref[...]
        mine[5:8, :] = cw_ref[...]
        s_ref[me] = mine[...]
        _exchange_tiles(lambda lin: mine, s_ref, sems.at[0], sems.at[1])
        sall = s_ref[...].reshape(NDEV * 8, D).astype(BF)
        r = jnp.dot(sall, w_ref[...].astype(BF), preferred_element_type=F32) + b_ref[...]
        res[...] = r.reshape(NDEV, 8, nsh)
        m_ref[me] = res[me]
        _exchange_tiles(lambda lin: res.at[lin], m_ref, sems.at[2], sems.at[3])

    vm = pl.BlockSpec(memory_space=pltpu.VMEM)
    return pl.pallas_call(
        body, name=name, in_specs=[vm] * 6 + [pl.BlockSpec(memory_space=pl.ANY)] * len(deps), out_specs=[vm, vm],
        out_shape=[jax.ShapeDtypeStruct((NDEV, 8, D), F32), jax.ShapeDtypeStruct((NDEV, 8, nsh), F32)],
        scratch_shapes=[pltpu.VMEM((8, D), F32), pltpu.VMEM((NDEV, 8, nsh), F32), pltpu.SemaphoreType.DMA((4,))],
    )(c, c_ctx, ffn_w, conv_w, w_shard, b_shard, *deps)


P_DML, P_DMC, P_N1, P_QG, P_KVG, P_CB, P_N2, P_FB, P_FG, P_CW, P_FW, P_LOSS, P_ROWS = 0, 6, 12, 13, 14, 15, 16, 17, 23, 24, 27, 45, 48
FROWS = 3


def sync_small(r, deps, *, name):
    ins = [r["dmod_lat"], r["dmod_ctx"], r["dn1g"], r["dqg"], r["dkvg"], r["dconv_b"], r["dn2g"], r["dffn_b"], r["dfg"],
           r["dconv_w"], r["dffn_w"], r["loss"]]

    def put_wide(p, row0, row, n):
        for j in range(-(-n // D)):
            w = min(D, n - j * D)
            p[row0 + j:row0 + j + 1, 0:w] = row[:, j * D:j * D + w]

    def body(dml, dmc, n1, qg, kvg, cb, n2, fb, fg, cw, fw, loss, *rest):
        a_ref, sum_ref, p, sems = rest[len(deps):]
        x, y, c, me = _me()
        p[...] = jnp.zeros_like(p)
        put_wide(p, P_DML, dml, 6 * D)
        put_wide(p, P_DMC, dmc, 6 * D)
        put_wide(p, P_N1, n1, D)
        put_wide(p, P_QG, qg, 512)
        put_wide(p, P_KVG, kvg, KVL)
        put_wide(p, P_CB, cb, CONV)
        put_wide(p, P_N2, n2, D)
        put_wide(p, P_FG, fg, D)
        put_wide(p, P_LOSS, loss, 128)
        for s in range(2):
            put_wide(p, P_FB + FROWS * s, fb.at[s], DFF)
        for k in range(3):
            put_wide(p, P_CW + k, cw.at[k:k + 1], CONV)
            for s in range(2):
                put_wide(p, P_FW + FROWS * (2 * k + s), fw.at[s, k:k + 1], DFF)
        a_ref[me] = p[...]
        _exchange_tiles(lambda lin: p, a_ref, sems.at[0], sems.at[1])
        acc = a_ref[0]
        for k in range(1, NDEV):
            acc = acc + a_ref[k]
        sum_ref[...] = acc

    vm = pl.BlockSpec(memory_space=pltpu.VMEM)
    return pl.pallas_call(
        body, name=name, in_specs=[vm] * len(ins) + [pl.BlockSpec(memory_space=pl.ANY)] * len(deps), out_specs=[vm, vm],
        out_shape=[jax.ShapeDtypeStruct((NDEV, P_ROWS, D), F32), jax.ShapeDtypeStruct((P_ROWS, D), F32)],
        scratch_shapes=[pltpu.VMEM((P_ROWS, D), F32), pltpu.SemaphoreType.DMA((2,))],
    )(*ins, *deps)


def ada_bwd(s_all, dml, dmc, w_shard, c_ctx, *, name):
    nsh = w_shard.shape[1]

    def body(s_ref, dml_ref, dmc_ref, w_ref, c_ref, dw_ref, gc_ref, s16, dm16, part, buf, sems):
        x, y, c, me = _me()
        s16[...] = jnp.zeros_like(s16)
        dm16[...] = jnp.zeros_like(dm16)
        for k in range(NDEV):
            s16[k:k + 1, :] = s_ref[k, 0:1, :]
        s16[8:9, :] = s_ref[0, 1:2, :]
        dm16[0:8, :] = dml_ref[...]
        dm16[8:9, :] = dmc_ref[...]
        dw_ref[...] = lax.dot_general(s16[...].astype(BF), dm16[...].astype(BF), (((0,), (0,)), ((), ())),
                                      preferred_element_type=F32)
        part[...] = lax.dot_general(dm16[8:16, :].astype(BF), w_ref[...].astype(BF), (((1,), (1,)), ((), ())),
                                    preferred_element_type=F32)
        buf[me] = part[...]
        _exchange_tiles(lambda lin: part, buf, sems.at[0], sems.at[1])
        acc = buf[0]
        for k in range(1, NDEV):
            acc = acc + buf[k]
        z = c_ref[...]
        sg = jax.nn.sigmoid(z)
        gc_ref[...] = acc * (sg * (1.0 + z * (1.0 - sg)))

    vm = pl.BlockSpec(memory_space=pltpu.VMEM)
    return pl.pallas_call(
        body, name=name, in_specs=[vm] * 5, out_specs=[vm, vm],
        out_shape=[jax.ShapeDtypeStruct((D, nsh), F32), jax.ShapeDtypeStruct((8, D), F32)],
        scratch_shapes=[pltpu.VMEM((16, D), F32), pltpu.VMEM((16, nsh), F32), pltpu.VMEM((8, D), F32),
                        pltpu.VMEM((NDEV, 8, D), F32), pltpu.SemaphoreType.DMA((2,))],
    )(s_all, dml, dmc, w_shard, c_ctx)


HBM_SPEC = pl.BlockSpec(memory_space=pltpu.HBM)
SEM_SPEC = pl.BlockSpec(memory_space=pltpu.SEMAPHORE)
EFFECT = pltpu.SideEffectType.DATAFLOW_SIDE_EFFECTING


ALL_PEERS = tuple(range(1, NDEV))
FIRST_HOP = (1, 2, 4, 6)
RELAY = (2, 4, 6)


def _exchange_copies(srcs, lands, send, recv, per_peer, peers):
    x, y, c, me = _me()
    n = len(peers)
    cps = []
    for t in range(len(srcs)):
        for j, k in enumerate(peers):
            dev, lin = _peer(x, y, c, k)
            cps.append(pltpu.make_async_remote_copy(
                src_ref=srcs[t].at[lin] if per_peer else srcs[t], dst_ref=lands[t].at[me],
                send_sem=send.at[n * t + j], recv_sem=recv.at[n * t + j], device_id=dev, device_id_type=MESH))
    return cps


def _relay_copies(lands, send, recv):
    x, y, c, me = _me()
    n = len(RELAY)
    cps = []
    for t in range(len(lands)):
        for j, k in enumerate(RELAY):
            slot = lands[t].at[_peer(x, y, c, k)[1]]
            cps.append(pltpu.make_async_remote_copy(
                src_ref=slot, dst_ref=slot, send_sem=send.at[n * t + j], recv_sem=recv.at[n * t + j],
                device_id=(x, y, 1 - c), device_id_type=MESH))
    return cps


def _own_copies(srcs, lands, own, per_peer):
    me = _me()[3]
    return [pltpu.make_async_copy(srcs[t].at[me] if per_peer else srcs[t], lands[t].at[me], own.at[t])
            for t in range(len(srcs))]


def exchange_start(srcs, *, per_peer, name, dep=None, peers=ALL_PEERS):
    nt = len(srcs)
    ns = len(peers) * nt
    land_shapes = [(a.shape if per_peer else (NDEV,) + a.shape) for a in srcs]
    deps = [] if dep is None else [dep]

    def body(*refs):
        src, land = refs[:nt], refs[nt:2 * nt]
        send, recv, own = refs[2 * nt + len(deps):2 * nt + len(deps) + 3]
        for cp in _exchange_copies(src, land, send, recv, per_peer, peers) + _own_copies(src, land, own, per_peer):
            cp.start()
        refs[-1][...] = jnp.zeros_like(refs[-1])

    hb = lambda a: pltpu.with_memory_space_constraint(a, pltpu.HBM)
    outs = pl.pallas_call(
        body, name=name,
        out_shape=(pltpu.SemaphoreType.DMA((ns,)), pltpu.SemaphoreType.DMA((ns,)), pltpu.SemaphoreType.DMA((nt,)),
                   *[pltpu.HBM(a.shape, a.dtype) for a in srcs], *[pltpu.HBM(s, a.dtype) for s, a in zip(land_shapes, srcs)],
                   jax.ShapeDtypeStruct((8, 128), F32)),
        in_specs=[HBM_SPEC] * (2 * nt) + [pl.BlockSpec(memory_space=pl.ANY)] * len(deps),
        out_specs=(SEM_SPEC, SEM_SPEC, SEM_SPEC, *([HBM_SPEC] * (2 * nt)), pl.BlockSpec(memory_space=pltpu.VMEM)),
        input_output_aliases={i: 3 + i for i in range(2 * nt)},
        compiler_params=pltpu.CompilerParams(has_side_effects=EFFECT),
    )(*[hb(a) for a in srcs], *[hb(lax.empty(s, a.dtype)) for s, a in zip(land_shapes, srcs)], *deps)
    return dict(send=outs[0], recv=outs[1], own=outs[2], src=list(outs[3:3 + nt]), land=list(outs[3 + nt:3 + 2 * nt]),
                token=outs[-1], per_peer=per_peer, peers=peers)


def exchange_wait(h, after, *, name):
    nt = len(h["src"])
    per_peer, peers = h["per_peer"], h["peers"]

    def body(*refs):
        src, land, send, recv, own = refs[:nt], refs[nt:2 * nt], refs[2 * nt], refs[2 * nt + 1], refs[2 * nt + 2]
        for cp in _exchange_copies(src, land, send, recv, per_peer, peers):
            cp.wait_send()
            cp.wait_recv()
        for cp in _own_copies(src, land, own, per_peer):
            cp.wait()

    outs = pl.pallas_call(
        body, name=name,
        out_shape=(*[pltpu.HBM(a.shape, a.dtype) for a in h["src"]], *[pltpu.HBM(a.shape, a.dtype) for a in h["land"]]),
        in_specs=[HBM_SPEC] * (2 * nt) + [SEM_SPEC, SEM_SPEC, SEM_SPEC, pl.BlockSpec(memory_space=pl.ANY)],
        out_specs=tuple([HBM_SPEC] * (2 * nt)),
        input_output_aliases={i: i for i in range(2 * nt)},
        compiler_params=pltpu.CompilerParams(has_side_effects=EFFECT),
    )(*h["src"], *h["land"], h["send"], h["recv"], h["own"], after)
    return list(outs[nt:])


def relay_start(lands, *, name):
    nt = len(lands)
    ns = len(RELAY) * nt

    def body(*refs):
        for cp in _relay_copies(refs[:nt], refs[nt], refs[nt + 1]):
            cp.start()

    outs = pl.pallas_call(
        body, name=name,
        out_shape=(pltpu.SemaphoreType.DMA((ns,)), pltpu.SemaphoreType.DMA((ns,)),
                   *[pltpu.HBM(a.shape, a.dtype) for a in lands]),
        in_specs=[HBM_SPEC] * nt, out_specs=(SEM_SPEC, SEM_SPEC, *([HBM_SPEC] * nt)),
        input_output_aliases={i: 2 + i for i in range(nt)},
        compiler_params=pltpu.CompilerParams(has_side_effects=EFFECT),
    )(*lands)
    return dict(send=outs[0], recv=outs[1], land=list(outs[2:]))


def relay_wait(h, *, name):
    nt = len(h["land"])

    def body(*refs):
        for cp in _relay_copies(refs[:nt], refs[nt], refs[nt + 1]):
            cp.wait_send()
            cp.wait_recv()

    outs = pl.pallas_call(
        body, name=name, out_shape=tuple(pltpu.HBM(a.shape, a.dtype) for a in h["land"]),
        in_specs=[HBM_SPEC] * nt + [SEM_SPEC, SEM_SPEC], out_specs=tuple([HBM_SPEC] * nt),
        input_output_aliases={i: i for i in range(nt)},
        compiler_params=pltpu.CompilerParams(has_side_effects=EFFECT),
    )(*h["land"], h["send"], h["recv"])
    return list(outs)


def _adamw_math(w, g, m, v):
    nm = B1 * m + (1.0 - B1) * g
    nv = B2 * v + (1.0 - B2) * (g * g)
    m_hat = nm / (1.0 - B1 ** STEP)
    v_hat = nv / (1.0 - B2 ** STEP)
    return -LR * (m_hat / (jnp.sqrt(v_hat) + AEPS) + WD * w), nm, nv


def adamw_many(ws, gs, ms, vs, *, name):
    n = len(ws)

    def body(*refs):
        for k in range(n):
            d, nm, nv = _adamw_math(refs[k][...], refs[n + k][...], refs[2 * n + k][...], refs[3 * n + k][...])
            refs[4 * n + k][...] = d
            refs[5 * n + k][...] = nm
            refs[6 * n + k][...] = nv

    vm = pl.BlockSpec(memory_space=pltpu.VMEM)
    sh = [jax.ShapeDtypeStruct(w.shape, F32) for w in ws]
    outs = pl.pallas_call(body, name=name, in_specs=[vm] * (4 * n), out_specs=[vm] * (3 * n), out_shape=sh * 3,
                          )(*ws, *gs, *ms, *vs)
    return outs[:n], outs[n:2 * n], outs[2 * n:]


def adamw(w, g, m, v, *, name, tr=256):
    R, C = w.shape
    tr = _pick(R, tr, 8)

    def body(w_ref, g_ref, m_ref, v_ref, d_ref, nm_ref, nv_ref):
        d_ref[...], nm_ref[...], nv_ref[...] = _adamw_math(w_ref[...], g_ref[...], m_ref[...], v_ref[...])

    blk = pl.BlockSpec((tr, C), lambda i: (i, 0))
    sh = jax.ShapeDtypeStruct((R, C), F32)
    return pl.pallas_call(
        body, name=name, grid=(R // tr,), in_specs=[blk, blk, blk, blk], out_specs=[blk, blk, blk],
        out_shape=[sh, sh, sh], compiler_params=pltpu.CompilerParams(dimension_semantics=("parallel",)),
    )(w, g, m, v)


def adamw_slots(w, slots, m, v, *, name, tr=256):
    unit = w.ndim == 3
    R, C = w.shape[0], w.shape[-1]
    if R % 16 == 0:
        tr = _pick(R, tr, 16)
    else:
        tr = 144

    def body(w_ref, s_ref, m_ref, v_ref, g_ref, d_ref, nm_ref, nv_ref):
        g = s_ref[0].astype(F32)
        for k in range(1, NDEV):
            g = g + s_ref[k].astype(F32)
        g_ref[...] = g
        d_ref[...], nm_ref[...], nv_ref[...] = _adamw_math(w_ref[...], g, m_ref[...], v_ref[...])

    blk = pl.BlockSpec((tr, None, C), lambda i: (i, 0, 0)) if unit else pl.BlockSpec((tr, C), lambda i: (i, 0))
    sh = jax.ShapeDtypeStruct(w.shape, F32)
    return pl.pallas_call(
        body, name=name, grid=(pl.cdiv(R, tr),), in_specs=[blk, pl.BlockSpec((NDEV, tr, C), lambda i: (0, i, 0)), blk, blk],
        out_specs=[blk, blk, blk, blk], out_shape=[sh, sh, sh, sh],
        compiler_params=pltpu.CompilerParams(dimension_semantics=("parallel",)),
    )(w, slots, m, v)


def _padc(a, n=D):
    return jnp.pad(a, ((0, 0), (0, n - a.shape[1])))


def kernel(x, c, ctx, c_ctx, w_ada, b_ada, norm1_g, w_in, q_norm_g, kv_norm_g, w_uq, w_ukv, conv_w, conv_b, w_attn_out, w_conv_out, w_o, norm2_g, w_up, ffn_conv_w, ffn_conv_b, w_down, final_g, loss_target, m_c_ctx, m_w_ada, m_b_ada, m_norm1_g, m_w_in, m_q_norm_g, m_kv_norm_g, m_w_uq, m_w_ukv, m_conv_w, m_conv_b, m_w_attn_out, m_w_conv_out, m_w_o, m_norm2_g, m_w_up, m_ffn_conv_w, m_ffn_conv_b, m_w_down, m_final_g, v_c_ctx, v_w_ada, v_b_ada, v_norm1_g, v_w_in, v_q_norm_g, v_kv_norm_g, v_w_uq, v_w_ukv, v_conv_w, v_conv_b, v_w_attn_out, v_w_conv_out, v_w_o, v_norm2_g, v_w_up, v_ffn_conv_w, v_ffn_conv_b, v_w_down, v_final_g):
    me = 4 * lax.axis_index("x") + 2 * lax.axis_index("y") + lax.axis_index("c")
    W = dict(c_ctx=c_ctx, w_ada=w_ada, b_ada=b_ada, norm1_g=norm1_g, w_in=w_in, q_norm_g=q_norm_g, kv_norm_g=kv_norm_g,
             w_uq=w_uq, w_ukv=w_ukv, conv_w=conv_w, conv_b=conv_b, w_attn_out=w_attn_out, w_conv_out=w_conv_out, w_o=w_o,
             norm2_g=norm2_g, w_up=w_up, ffn_conv_w=ffn_conv_w, ffn_conv_b=ffn_conv_b, w_down=w_down, final_g=final_g)
    M = dict(c_ctx=m_c_ctx, w_ada=m_w_ada, b_ada=m_b_ada, norm1_g=m_norm1_g, w_in=m_w_in, q_norm_g=m_q_norm_g,
             kv_norm_g=m_kv_norm_g, w_uq=m_w_uq, w_ukv=m_w_ukv, conv_w=m_conv_w, conv_b=m_conv_b, w_attn_out=m_w_attn_out,
             w_conv_out=m_w_conv_out, w_o=m_w_o, norm2_g=m_norm2_g, w_up=m_w_up, ffn_conv_w=m_ffn_conv_w,
             ffn_conv_b=m_ffn_conv_b, w_down=m_w_down, final_g=m_final_g)
    V = dict(c_ctx=v_c_ctx, w_ada=v_w_ada, b_ada=v_b_ada, norm1_g=v_norm1_g, w_in=v_w_in, q_norm_g=v_q_norm_g,
             kv_norm_g=v_kv_norm_g, w_uq=v_w_uq, w_ukv=v_w_ukv, conv_w=v_conv_w, conv_b=v_conv_b, w_attn_out=v_w_attn_out,
             w_conv_out=v_w_conv_out, w_o=v_w_o, norm2_g=v_norm2_g, w_up=v_w_up, ffn_conv_w=v_ffn_conv_w,
             ffn_conv_b=v_ffn_conv_b, w_down=v_w_down, final_g=v_final_g)
    names = list(W)
    transposed = ("w_up",)
    as2d = lambda k, a: (a.reshape(1, -1) if a.ndim == 1 else
                         a[0].T if k in transposed else a.reshape(a.shape[-2], a.shape[-1]))
    W2 = {k: as2d(k, a) for k, a in W.items()}
    M2 = {k: as2d(k, a) for k, a in M.items()}
    V2 = {k: as2d(k, a) for k, a in V.items()}
    unit3 = lambda a: jnp.transpose(a, (2, 0, 1))
    W3, M3, V3 = unit3(W["w_in"]), unit3(M["w_in"]), unit3(V["w_in"])
    nsh = W2["w_ada"].shape[1]

    b_sh = lax.dynamic_slice(W2["b_ada"], (0, me * nsh), (1, nsh))
    s_all, m_all = ada_fwd(c, W2["c_ctx"], _padc(W2["ffn_conv_w"]), _padc(W2["conv_w"]), W2["w_ada"], b_sh, [],
                           name="ada_fwd")
    mod_lat = m_all[:, 0, :].reshape(1, 6 * D)
    mod_ctx = m_all[:, 1, :].reshape(1, 6 * D)
    ffn_w_full = s_all[:, 2:5, :2 * DFF // NDEV].transpose(1, 0, 2).reshape(3, 2 * DFF)
    conv_w_full = s_all[:, 5:8, :CONV // NDEV].transpose(1, 0, 2).reshape(3, CONV)

    stage_w = {"in": ["w_in"], "mid": ["w_uq", "w_ukv", "w_attn_out", "w_conv_out", "w_o"], "ffn": ["w_up", "w_down"]}
    two_level = ("in", "mid")
    ag, tok = {}, m_all
    for st, nms in stage_w.items():
        ag[st] = exchange_start([W2[nm].astype(BF) for nm in nms], per_peer=False, dep=tok, name="ag_start_" + st,
                                peers=FIRST_HOP if st in two_level else ALL_PEERS)
        tok = ag[st]["token"]

    def get_w(stage, after):
        lands = exchange_wait(ag[stage], after, name="ag_wait_" + stage)
        if stage in two_level:
            lands = relay_wait(relay_start(lands, name="ag_relay_" + stage), name="ag_relay_wait_" + stage)
        g = dict(zip(stage_w[stage], lands))
        if stage == "in":
            return build_win(g["w_in"], name="build_win")
        if stage == "mid":
            wq2, wkv2 = build_wq_wkv(g["w_uq"], g["w_ukv"], name="build_wq_wkv")
            return (wq2, wkv2, unshard_cols(g["w_attn_out"], name="unshard_w_attn_out"),
                    unshard_cols(g["w_conv_out"], name="unshard_w_conv_out"), g["w_o"].reshape(D, D))
        return g["w_up"].reshape(2 * DFF, D), g["w_down"].reshape(DFF, D)

    stage_g = {"ffn": ["w_up", "w_down"], "mid": ["w_attn_out", "w_conv_out", "w_o"], "qkv": ["w_uq", "w_ukv"],
               "in": ["w_in"]}
    rs = {}

    def put_g(stage, g):
        if stage == "in":
            parts = [shard_win_grad(g["dwin"], g["dwin_c"], name="shard_win_grad")]
        elif stage == "mid":
            parts = [shard_cols(g["dwao"], name="shard_w_attn_out"), shard_cols(g["dwco"], name="shard_w_conv_out"),
                     g["dwo"].reshape(NDEV, D // NDEV, D)]
        elif stage == "qkv":
            parts = list(shard_wq_wkv_grad(g["dwq2"], g["dwkv2"], name="shard_wq_wkv_grad"))
        else:
            parts = [g["dwup"].reshape(NDEV, 2 * DFF // NDEV, D), g["dwdn"].reshape(NDEV, DFF // NDEV, D)]
        rs[stage] = exchange_start(parts, per_peer=True, name="rs_start_" + stage)
        return rs[stage]["token"]

    r = _local_step(x[0], ctx[0], loss_target[0], mod_lat, mod_ctx, W2["norm1_g"], W2["q_norm_g"], W2["kv_norm_g"],
                    W2["norm2_g"], W2["final_g"], conv_w_full, W2["conv_b"], ffn_w_full, W2["ffn_conv_b"], get_w, put_g,
                    ag["ffn"]["token"])

    G, DL, NM, NV = {}, {}, {}, {}

    def finish(stage, after):
        for nm, sl in zip(stage_g[stage], exchange_wait(rs[stage], after, name="rs_wait_" + stage)):
            wmv = (W3, M3, V3) if nm == "w_in" else (W2[nm], M2[nm], V2[nm])
            G[nm], DL[nm], NM[nm], NV[nm] = adamw_slots(wmv[0], sl, wmv[1], wmv[2], name="adamw_" + nm)
            after = DL[nm]
        return after

    after = r["dx"]
    for st in ("ffn", "mid", "qkv"):
        after = finish(st, after)

    a_buf, ssum = sync_small(r, [DL[nm] for st in ("ffn", "mid", "qkv") for nm in stage_g[st]], name="sync_small")
    loss = ssum[P_LOSS, 0]
    G["norm1_g"] = ssum[P_N1:P_N1 + 1]
    G["q_norm_g"] = ssum[P_QG:P_QG + 1, :QL]
    G["kv_norm_g"] = ssum[P_KVG:P_KVG + 1, :KVL]
    G["conv_b"] = ssum[P_CB:P_CB + 1, :CONV]
    G["norm2_g"] = ssum[P_N2:P_N2 + 1]
    G["ffn_conv_b"] = ssum[P_FB:P_FB + 2 * FROWS].reshape(1, 2, FROWS * D)[:, :, :DFF].reshape(1, 2 * DFF)
    G["final_g"] = ssum[P_FG:P_FG + 1]
    G["conv_w"] = lax.dynamic_slice(ssum[P_CW:P_CW + 3, :CONV], (0, me * (CONV // NDEV)), (3, CONV // NDEV))
    fw_full = ssum[P_FW:P_FW + 6 * FROWS].reshape(3, 2, FROWS * D)[:, :, :DFF].reshape(3, 2 * DFF)
    G["ffn_conv_w"] = lax.dynamic_slice(fw_full, (0, me * (2 * DFF // NDEV)), (3, 2 * DFF // NDEV))
    G["b_ada"] = (ssum[P_DML:P_DML + 6] + ssum[P_DMC:P_DMC + 6]).reshape(1, 6 * D)

    dml = lax.dynamic_slice(a_buf[:, P_DML:P_DML + 6, :].reshape(NDEV, 6 * D), (0, me * nsh), (NDEV, nsh))
    dmc = lax.dynamic_slice(ssum[P_DMC:P_DMC + 6].reshape(1, 6 * D), (0, me * nsh), (1, nsh))
    G["w_ada"], gcc = ada_bwd(s_all, dml, dmc, W2["w_ada"], W2["c_ctx"], name="ada_bwd")
    G["c_ctx"] = gcc[0:1]

    DL["w_ada"], NM["w_ada"], NV["w_ada"] = adamw(W2["w_ada"], G["w_ada"], M2["w_ada"], V2["w_ada"], name="adamw_w_ada")
    small = ["c_ctx", "b_ada", "norm1_g", "q_norm_g", "kv_norm_g", "conv_b", "norm2_g", "ffn_conv_b", "final_g", "conv_w",
             "ffn_conv_w"]
    ds, nms, nvs = adamw_many([W2[k] for k in small], [G[k] for k in small], [M2[k] for k in small],
                              [V2[k] for k in small], name="adamw_small")
    for k, nm in enumerate(small):
        DL[nm], NM[nm], NV[nm] = ds[k], nms[k], nvs[k]
    finish("in", ds[0])

    outs = [loss, r["dx"][None]]
    for grp in (G, DL, NM, NV):
        outs += [grp[nm].T[None] if nm in transposed else
                 jnp.transpose(grp[nm], (1, 2, 0)) if nm == "w_in" else grp[nm].reshape(W[nm].shape) for nm in names]
    return tuple(outs)
```

```python
import functools
import numpy as np
import jax
import jax.numpy as jnp
from jax import lax
from jax.experimental import pallas as pl
from jax.experimental.pallas import tpu as pltpu

F32 = jnp.float32
BF = jnp.bfloat16
MESH = pl.DeviceIdType.MESH

D = 1024
T = 2048
TC = 256
TKV = T + TC
GRID_W = 64
NH = 8
DN = 64
DR = 32
DV = 64
QL = 384
KVL = 256
CONV = 512
DFF = 2816
EPS = 1e-6
ROPE_THETA = 10000.0
SCALE = (DN + DR) ** -0.5
NDEV = 8
HP = 128

O_GA, O_GC, O_KV, O_Q, O_CV = 0, 1024, 2048, 2560, 3072
NIN = 4608
CVB = 256
N_IN = 4256
SH_IN = N_IN // NDEV

LR, B1, B2, AEPS, WD, STEP = 0.001, 0.9, 0.999, 1e-08, 0.01, 10


def _pick(n, target, mult=128):
    best = None
    for d in range(mult, min(n, target) + 1, mult):
        if n % d == 0:
            best = d
    return best if best is not None else n


def _swap_start(g):
    return 8 * (g ^ 1)


def mm(a, b, *, ta=False, tb=False, out_dtype=F32, name, tm=1024, tn=1024, tk=2048, M=None, N=None, K=None,
       a_off=(0, 0), b_off=(0, 0), a_stack=False, b_stack=False, o_stack=False, dep=None):
    def dims(arr, stack):
        return (arr.shape[1], 2 * arr.shape[2]) if stack else arr.shape

    ar, ac = dims(a, a_stack)
    br, bc = dims(b, b_stack)
    M = M or ((ac if ta else ar) - a_off[1 if ta else 0])
    K = K or ((ar if ta else ac) - a_off[0 if ta else 1])
    N = N or ((br if tb else bc) - b_off[0 if tb else 1])
    tm = _pick(M, tm, 128 if ta else 16)
    tn = _pick(N // 2 if (o_stack or (b_stack and not tb)) else N, tn, 128)
    tk = _pick(K // 2 if ((a_stack and not ta) or (b_stack and tb)) else K, tk, 128)
    nk = K // tk
    ca = 0 if ta else 1
    cb = 1 if tb else 0

    def body(a_ref, b_ref, *rest):
        o_ref, acc = rest[-2:]
        k = pl.program_id(2)
        part = lax.dot_general(a_ref[...].astype(BF), b_ref[...].astype(BF),
                               (((ca,), (cb,)), ((), ())), preferred_element_type=F32)
        if nk == 1:
            o_ref[...] = part.astype(o_ref.dtype)
        else:
            @pl.when(k == 0)
            def _():
                acc[...] = part

            @pl.when(k > 0)
            def _():
                acc[...] += part

            @pl.when(k == nk - 1)
            def _():
                o_ref[...] = acc[...].astype(o_ref.dtype)

    def spec(blk, rc, off, stack, ncols):
        assert off[0] % blk[0] == 0 and off[1] % blk[1] == 0, (name, blk, off)
        ro, co = off[0] // blk[0], off[1] // blk[1]
        if not stack:
            return pl.BlockSpec(blk, lambda i, j, k: (rc(i, j, k)[0] + ro, rc(i, j, k)[1] + co))
        nhb = ncols // 2 // blk[1]
        return pl.BlockSpec((None,) + blk,
                            lambda i, j, k: ((rc(i, j, k)[1] + co) // nhb, rc(i, j, k)[0] + ro, (rc(i, j, k)[1] + co) % nhb))

    a_spec = spec((tk, tm), lambda i, j, k: (k, i), a_off, a_stack, ac) if ta else \
        spec((tm, tk), lambda i, j, k: (i, k), a_off, a_stack, ac)
    b_spec = spec((tn, tk), lambda i, j, k: (j, k), b_off, b_stack, bc) if tb else \
        spec((tk, tn), lambda i, j, k: (k, j), b_off, b_stack, bc)
    o_spec = spec((tm, tn), lambda i, j, k: (i, j), (0, 0), o_stack, N)
    o_shape = (2, M, N // 2) if o_stack else (M, N)
    deps = [] if dep is None else [dep]
    return pl.pallas_call(
        body, name=name, grid=(M // tm, N // tn, nk),
        in_specs=[a_spec, b_spec] + [pl.BlockSpec(memory_space=pl.ANY)] * len(deps),
        out_specs=o_spec, out_shape=jax.ShapeDtypeStruct(o_shape, out_dtype),
        scratch_shapes=[pltpu.VMEM((tm, tn) if nk > 1 else (8, 128), F32)],
        compiler_params=pltpu.CompilerParams(dimension_semantics=("parallel", "parallel", "arbitrary")),
    )(a, b, *deps)


def _row(width):
    return pl.BlockSpec((1, width), lambda *_: (0, 0))


NLAT = T // TC


def normmod_cat(ctx, x, g, csc, csh, sc, sh, dep, *, name, tm=256):
    assert tm == TC

    def body(c_ref, x_ref, g_ref, csc_ref, csh_ref, sc_ref, sh_ref, dep_ref, h_ref):
        last = pl.program_id(0) == NLAT
        xv = jnp.where(last, c_ref[...], x_ref[...])
        scv = jnp.where(last, csc_ref[...], sc_ref[...])
        shv = jnp.where(last, csh_ref[...], sh_ref[...])
        r = lax.rsqrt(jnp.mean(xv * xv, axis=-1, keepdims=True) + EPS)
        h_ref[...] = ((xv * r * g_ref[...]) * (1.0 + scv) + shv).astype(BF)

    return pl.pallas_call(
        body, name=name, grid=(TKV // tm,),
        in_specs=[pl.BlockSpec((tm, D), lambda i: (0, 0)), pl.BlockSpec((tm, D), lambda i: (jnp.minimum(i, NLAT - 1), 0)),
                  _row(D), _row(D), _row(D), _row(D), _row(D), pl.BlockSpec(memory_space=pl.ANY)],
        out_specs=pl.BlockSpec((tm, D), lambda i: (i, 0)), out_shape=jax.ShapeDtypeStruct((TKV, D), BF),
        compiler_params=pltpu.CompilerParams(dimension_semantics=("parallel",)),
    )(ctx, x, g, csc, csh, sc, sh, dep)


def resid_normmod(x, a, gate, g, sc, sh, *, name, tm=512):
    R = x.shape[0]

    def body(x_ref, a_ref, gate_ref, g_ref, sc_ref, sh_ref, x1_ref, h_ref):
        xv = x_ref[...] + gate_ref[...] * a_ref[...]
        x1_ref[...] = xv
        r = lax.rsqrt(jnp.mean(xv * xv, axis=-1, keepdims=True) + EPS)
        h_ref[...] = ((xv * r * g_ref[...]) * (1.0 + sc_ref[...]) + sh_ref[...]).astype(BF)

    blk = pl.BlockSpec((tm, D), lambda i: (i, 0))
    return pl.pallas_call(
        body, name=name, grid=(R // tm,), in_specs=[blk, blk, _row(D), _row(D), _row(D), _row(D)],
        out_specs=[blk, blk],
        out_shape=[jax.ShapeDtypeStruct((R, D), F32), jax.ShapeDtypeStruct((R, D), BF)],
        compiler_params=pltpu.CompilerParams(dimension_semantics=("parallel",)),
    )(x, a, gate, g, sc, sh)


def kvprep(pc, p, kvg, wkv2, ck, sk, *, name, tm=256):
    assert tm == TC
    nb = TKV // tm
    kvcol = O_KV // 512

    def body(pc_ref, p_ref, g_ref, w_ref, ck_ref, sk_ref, k_ref, v_ref, ckv_ref):
        i = pl.program_id(0)
        t = jnp.where(i == NLAT, pc_ref[...], p_ref[...])
        pk = t[:, :KVL]
        r = lax.rsqrt(jnp.mean(pk * pk, axis=-1, keepdims=True) + EPS)
        ckv = (pk * r * g_ref[...]).astype(BF)
        ckv_ref[...] = ckv
        kv2 = jnp.dot(ckv, w_ref[...], preferred_element_type=F32)
        krr = t[:, KVL:KVL + HP] * ck_ref[...] + t[:, KVL + HP:KVL + 2 * HP] * sk_ref[...]
        k_ref[...] = (kv2[:, :NH * HP] + jnp.concatenate([krr] * NH, axis=1)).astype(BF)
        v_ref[...] = kv2[:, NH * HP:].astype(BF)

    return pl.pallas_call(
        body, name=name, grid=(nb,),
        in_specs=[pl.BlockSpec((tm, 512), lambda i: (0, 0)),
                  pl.BlockSpec((tm, 512), lambda i: (jnp.minimum(i, NLAT - 1), kvcol)),
                  _row(KVL), pl.BlockSpec((KVL, NH * HP + NH * DV), lambda i: (0, 0)),
                  pl.BlockSpec((tm, HP), lambda i: (i, 0)), pl.BlockSpec((tm, HP), lambda i: (i, 0))],
        out_specs=[pl.BlockSpec((tm, NH * HP), lambda i: (i, 0)), pl.BlockSpec((tm, NH * DV), lambda i: (i, 0)),
                   pl.BlockSpec((tm, KVL), lambda i: (i, 0))],
        out_shape=[jax.ShapeDtypeStruct((TKV, NH * HP), BF), jax.ShapeDtypeStruct((TKV, NH * DV), BF),
                   jax.ShapeDtypeStruct((TKV, KVL), BF)],
        compiler_params=pltpu.CompilerParams(dimension_semantics=("parallel",)),
    )(pc, p, kvg, wkv2, ck, sk)


def qprep(p, qg, wq2, cq_t, sq_t, *, name, tm=256):
    qcol = O_Q // 512

    def body(p_ref, g_ref, w_ref, c_ref, s_ref, q_ref, cq_ref):
        pq = p_ref[...]
        r = lax.rsqrt(jnp.sum(pq * pq, axis=-1, keepdims=True) * (1.0 / QL) + EPS)
        cq = (pq * r * g_ref[...]).astype(BF)
        cq_ref[...] = cq
        q2 = jnp.dot(cq, w_ref[...], preferred_element_type=F32)
        cc = jnp.concatenate([c_ref[...]] * NH, axis=1)
        ss = jnp.concatenate([s_ref[...]] * NH, axis=1)
        q_ref[...] = (q2[:, :NH * HP] * cc + q2[:, NH * HP:] * ss).astype(BF)

    return pl.pallas_call(
        body, name=name, grid=(T // tm,),
        in_specs=[pl.BlockSpec((tm, 512), lambda i: (i, qcol)), _row(512),
                  pl.BlockSpec((512, 2 * NH * HP), lambda i: (0, 0)),
                  pl.BlockSpec((tm, HP), lambda i: (i, 0)), pl.BlockSpec((tm, HP), lambda i: (i, 0))],
        out_specs=[pl.BlockSpec((tm, NH * HP), lambda i: (i, 0)), pl.BlockSpec((tm, 512), lambda i: (i, 0))],
        out_shape=[jax.ShapeDtypeStruct((T, NH * HP), BF), jax.ShapeDtypeStruct((T, 512), BF)],
        compiler_params=pltpu.CompilerParams(dimension_semantics=("parallel",)),
    )(p, qg, wq2, cq_t, sq_t)


def _head_mask(h):
    lanes = lax.broadcasted_iota(jnp.int32, (1, 2 * DV), 1)
    return (lanes // DV) == (h % 2)


LOG2E = 1.4426950408889634


def _scores_pass(q, k_ref, s_scr, kc):
    m = None
    for c in range(TKV // kc):
        s = lax.dot_general(q, k_ref[c * kc:(c + 1) * kc, :], (((1,), (1,)), ((), ())),
                            preferred_element_type=F32) * (SCALE * LOG2E)
        s_scr[:, c * kc:(c + 1) * kc] = s
        mc = jnp.max(s, axis=-1, keepdims=True)
        m = mc if m is None else jnp.maximum(m, mc)
    return m


def attn_fwd(q, k, v, *, name, tq=512, kc=1152):
    def body(q_ref, k_ref, v_ref, o_ref, s_scr):
        h = pl.program_id(1)
        m = _scores_pass(q_ref[...], k_ref, s_scr, kc)
        l = jnp.zeros((tq, 1), F32)
        acc = jnp.zeros((tq, 2 * DV), F32)
        for c in range(TKV // kc):
            e = jnp.exp2(s_scr[:, c * kc:(c + 1) * kc] - m)
            l = l + jnp.sum(e, axis=-1, keepdims=True)
            acc = acc + jnp.dot(e.astype(BF), v_ref[c * kc:(c + 1) * kc, :], preferred_element_type=F32)
        o2 = jnp.where(_head_mask(h), acc * (1.0 / l), 0.0).astype(BF)

        @pl.when(h % 2 == 0)
        def _():
            o_ref[...] = o2

        @pl.when(h % 2 == 1)
        def _():
            o_ref[...] = o_ref[...] + o2

    return pl.pallas_call(
        body, name=name, grid=(T // tq, NH),
        in_specs=[pl.BlockSpec((tq, HP), lambda i, h: (i, h)), pl.BlockSpec((TKV, HP), lambda i, h: (0, h)),
                  pl.BlockSpec((TKV, 2 * DV), lambda i, h: (0, h // 2))],
        out_specs=pl.BlockSpec((tq, 2 * DV), lambda i, h: (i, h // 2)),
        out_shape=jax.ShapeDtypeStruct((T, NH * DV), BF),
        scratch_shapes=[pltpu.VMEM((tq, TKV), F32)],
        compiler_params=pltpu.CompilerParams(dimension_semantics=("parallel", "arbitrary")),
    )(q, k, v)


def _shift_dn(x):
    n = x.shape[0]
    rows = lax.broadcasted_iota(jnp.int32, (n, 1), 0)
    return jnp.where(rows == 0, 0.0, pltpu.roll(x, 1, axis=0))


def _shift_up(x):
    n = x.shape[0]
    rows = lax.broadcasted_iota(jnp.int32, (n, 1), 0)
    return jnp.where(rows == n - 1, 0.0, pltpu.roll(x, n - 1, axis=0))


def _conv(x, w_ref, b_ref):
    return b_ref[...] + _shift_dn(x) * w_ref[0:1, :] + x * w_ref[1:2, :] + _shift_up(x) * w_ref[2:3, :]


def _conv_t(dy, w_ref):
    return _shift_up(dy) * w_ref[0:1, :] + dy * w_ref[1:2, :] + _shift_dn(dy) * w_ref[2:3, :]


def _conv_wgrad(dw_ref, dy, x):
    dw_ref[0:1, :] = jnp.sum(dy * _shift_dn(x), axis=0, keepdims=True)
    dw_ref[1:2, :] = jnp.sum(dy * x, axis=0, keepdims=True)
    dw_ref[2:3, :] = jnp.sum(dy * _shift_up(x), axis=0, keepdims=True)


def convz(p, cw, cb, *, name):
    o0 = O_CV // (3 * CVB)

    def body(p_ref, w_ref, bias_ref, z_ref):
        xv, bv, cv = p_ref[:, 0:CVB], p_ref[:, CVB:2 * CVB], p_ref[:, 2 * CVB:3 * CVB]
        z_ref[...] = (bv * _conv(cv * xv, w_ref, bias_ref)).astype(BF)

    return pl.pallas_call(
        body, name=name, grid=(CONV // CVB,),
        in_specs=[pl.BlockSpec((T, 3 * CVB), lambda j: (0, o0 + j)), pl.BlockSpec((3, CVB), lambda j: (0, j)),
                  pl.BlockSpec((1, CVB), lambda j: (0, j))],
        out_specs=pl.BlockSpec((T, CVB), lambda j: (0, j)),
        out_shape=jax.ShapeDtypeStruct((T, CONV), BF),
        compiler_params=pltpu.CompilerParams(dimension_semantics=("parallel",)),
    )(p, cw, cb)


def gate_merge(p, ya, yc, *, name, tm=512):
    def body(ga_ref, gc_ref, ya_ref, yc_ref, o_ref):
        o_ref[...] = (jax.nn.sigmoid(ga_ref[...]) * ya_ref[...] + jax.nn.sigmoid(gc_ref[...]) * yc_ref[...]).astype(BF)

    blk = pl.BlockSpec((tm, D), lambda i: (i, 0))
    return pl.pallas_call(
        body, name=name, grid=(T // tm,),
        in_specs=[pl.BlockSpec((tm, D), lambda i: (i, O_GA // D)), pl.BlockSpec((tm, D), lambda i: (i, O_GC // D)), blk, blk],
        out_specs=blk, out_shape=jax.ShapeDtypeStruct((T, D), BF),
        compiler_params=pltpu.CompilerParams(dimension_semantics=("parallel",)),
    )(p, p, ya, yc)


CONV_HALO = 8
CONV_ROWS = 256


def _row_chunks(n, chunk, carry):
    carry = chunk(0, True, False, carry)
    carry = lax.fori_loop(1, n // CONV_ROWS - 1, lambda c, a: chunk(c * CONV_ROWS, False, False, a), carry)
    return chunk(n - CONV_ROWS, False, True, carry)


def _ext_rows(ref, r0, first, last):
    n, w = ref.shape
    zero = jnp.zeros((CONV_HALO, w), ref.dtype)
    if first:
        return jnp.concatenate([zero, ref[0:CONV_ROWS + CONV_HALO, :]], axis=0)
    if last:
        return jnp.concatenate([ref[n - CONV_ROWS - CONV_HALO:n, :], zero], axis=0)
    return ref[pl.ds(pl.multiple_of(r0 - CONV_HALO, 8), CONV_ROWS + 2 * CONV_HALO), :]


def _center_rows(r0, first, last):
    return slice(r0, r0 + CONV_ROWS) if (first or last) else pl.ds(pl.multiple_of(r0, 8), CONV_ROWS)


def _roll_dn(x):
    return pltpu.roll(x, 1, axis=0)


def _roll_up(x):
    return pltpu.roll(x, x.shape[0] - 1, axis=0)


_CTR = slice(CONV_HALO, CONV_HALO + CONV_ROWS)


def ffn_act(u0, cw, cb, *, name, tc=256):
    nb = DFF // tc

    def body(u_ref, wg_ref, wv_ref, bg_ref, bv_ref, f_ref):
        wg = [wg_ref[k:k + 1, :] for k in range(3)]
        wv = [wv_ref[k:k + 1, :] for k in range(3)]
        bg, bv = bg_ref[...], bv_ref[...]

        def chunk(r0, first, last, carry):
            xg, xv = _ext_rows(u_ref.at[0], r0, first, last), _ext_rows(u_ref.at[1], r0, first, last)
            ug = bg + _roll_dn(xg) * wg[0] + xg * wg[1] + _roll_up(xg) * wg[2]
            uv = bv + _roll_dn(xv) * wv[0] + xv * wv[1] + _roll_up(xv) * wv[2]
            f_ref[_center_rows(r0, first, last), :] = (ug * jax.nn.sigmoid(ug) * uv)[_CTR].astype(BF)
            return carry

        _row_chunks(T, chunk, 0)

    return pl.pallas_call(
        body, name=name, grid=(nb,),
        in_specs=[pl.BlockSpec((2, T, tc), lambda j: (0, 0, j)),
                  pl.BlockSpec((3, tc), lambda j: (0, j)), pl.BlockSpec((3, tc), lambda j: (0, nb + j)),
                  pl.BlockSpec((1, tc), lambda j: (0, j)), pl.BlockSpec((1, tc), lambda j: (0, nb + j))],
        out_specs=pl.BlockSpec((T, tc), lambda j: (0, j)),
        out_shape=jax.ShapeDtypeStruct((T, DFF), BF),
        compiler_params=pltpu.CompilerParams(dimension_semantics=("parallel",)),
    )(u0, cw, cw, cb, cb)


def final_loss(x1, d, g2, fg, tgt, *, name, tm=512):
    def body(x1_ref, d_ref, g2_ref, fg_ref, t_ref, dx_ref, dd_ref, dfg_ref, loss_ref):
        i = pl.program_id(0)
        xv = x1_ref[...] + g2_ref[...] * d_ref[...]
        r = lax.rsqrt(jnp.mean(xv * xv, axis=-1, keepdims=True) + EPS)
        xh = xv * r
        diff = xh * fg_ref[...] - t_ref[...]
        part = 0.5 * jnp.sum(jnp.mean(diff * diff, axis=-1, keepdims=True), axis=0, keepdims=True)
        dy = diff * (1.0 / D)
        a = dy * fg_ref[...]
        dx = r * (a - xh * jnp.mean(a * xh, axis=-1, keepdims=True))
        dx_ref[...] = dx
        dd_ref[...] = (dx * g2_ref[...]).astype(BF)
        dfg = jnp.sum(dy * xh, axis=0, keepdims=True)

        @pl.when(i == 0)
        def _():
            dfg_ref[...] = dfg
            loss_ref[...] = jnp.broadcast_to(part, (1, 128))

        @pl.when(i > 0)
        def _():
            dfg_ref[...] += dfg
            loss_ref[...] += jnp.broadcast_to(part, (1, 128))

    blk = pl.BlockSpec((tm, D), lambda i: (i, 0))
    return pl.pallas_call(
        body, name=name, grid=(T // tm,), in_specs=[blk, blk, _row(D), _row(D), blk],
        out_specs=[blk, blk, _row(D), _row(128)],
        out_shape=[jax.ShapeDtypeStruct((T, D), F32), jax.ShapeDtypeStruct((T, D), BF),
                   jax.ShapeDtypeStruct((1, D), F32), jax.ShapeDtypeStruct((1, 128), F32)],
        compiler_params=pltpu.CompilerParams(dimension_semantics=("arbitrary",)),
    )(x1, d, g2, fg, tgt)


def normmod_bwd(x, dh, g, sc, dres, gsrc, gate, *, name, tm=512):
    R = x.shape[0]
    tm = min(tm, R)
    has_res = dres is not None

    def body(*refs):
        if has_res:
            x_ref, dh_ref, g_ref, sc_ref, dres_ref, gsrc_ref, gate_ref, dx_ref, dxg_ref, st_ref = refs
        else:
            x_ref, dh_ref, g_ref, sc_ref, st_ref = refs
        i = pl.program_id(0)
        xv = x_ref[...]
        r = lax.rsqrt(jnp.mean(xv * xv, axis=-1, keepdims=True) + EPS)
        xh = xv * r
        dhv = dh_ref[...]
        n = xh * g_ref[...]
        dn = dhv * (1.0 + sc_ref[...])
        a = dn * g_ref[...]
        rows = [jnp.sum(dhv, axis=0, keepdims=True), jnp.sum(dhv * n, axis=0, keepdims=True),
                jnp.sum(dn * xh, axis=0, keepdims=True)]
        if has_res:
            dr = dres_ref[...]
            dx = dr + r * (a - xh * jnp.mean(a * xh, axis=-1, keepdims=True))
            dx_ref[...] = dx
            dxg_ref[...] = (dx * gate_ref[...]).astype(BF)
            rows.append(jnp.sum(dr * gsrc_ref[...], axis=0, keepdims=True))
        else:
            rows.append(jnp.zeros((1, D), F32))

        @pl.when(i == 0)
        def _():
            for k, row in enumerate(rows):
                st_ref[k:k + 1, :] = row

        @pl.when(i > 0)
        def _():
            for k, row in enumerate(rows):
                st_ref[k:k + 1, :] += row

    blk = pl.BlockSpec((tm, D), lambda i: (i, 0))
    st_spec = pl.BlockSpec((4, D), lambda i: (0, 0))
    st_shape = jax.ShapeDtypeStruct((4, D), F32)
    cp = pltpu.CompilerParams(dimension_semantics=("arbitrary",))
    if has_res:
        return pl.pallas_call(
            body, name=name, grid=(R // tm,), in_specs=[blk, blk, _row(D), _row(D), blk, blk, _row(D)],
            out_specs=[blk, blk, st_spec],
            out_shape=[jax.ShapeDtypeStruct((R, D), F32), jax.ShapeDtypeStruct((R, D), BF), st_shape],
            compiler_params=cp,
        )(x, dh, g, sc, dres, gsrc, gate)
    return pl.pallas_call(
        body, name=name, grid=(R // tm,), in_specs=[blk, blk, _row(D), _row(D)],
        out_specs=st_spec, out_shape=st_shape, compiler_params=cp,
    )(x, dh, g, sc)


def ffn_act_bwd(u0, df, cw, cb, *, name, tc=128):
    nb = DFF // tc

    def body(u_ref, df_ref, wg_ref, wv_ref, bg_ref, bv_ref, du_ref, dw_ref, db_ref):
        wg = [wg_ref[k:k + 1, :] for k in range(3)]
        wv = [wv_ref[k:k + 1, :] for k in range(3)]
        bg, bv = bg_ref[...], bv_ref[...]

        def chunk(r0, first, last, acc):
            xg, xv = _ext_rows(u_ref.at[0], r0, first, last), _ext_rows(u_ref.at[1], r0, first, last)
            dfe = _ext_rows(df_ref, r0, first, last)
            xg_d, xg_u, xv_d, xv_u = _roll_dn(xg), _roll_up(xg), _roll_dn(xv), _roll_up(xv)
            ug = bg + xg_d * wg[0] + xg * wg[1] + xg_u * wg[2]
            uv = bv + xv_d * wv[0] + xv * wv[1] + xv_u * wv[2]
            sig = jax.nn.sigmoid(ug)
            dug = dfe * uv * (sig * (1.0 + ug * (1.0 - sig)))
            duv = dfe * (ug * sig)
            rows = _center_rows(r0, first, last)
            du_ref[0, rows, :] = (_roll_up(dug) * wg[0] + dug * wg[1] + _roll_dn(dug) * wg[2])[_CTR].astype(BF)
            du_ref[1, rows, :] = (_roll_up(duv) * wv[0] + duv * wv[1] + _roll_dn(duv) * wv[2])[_CTR].astype(BF)
            terms = [dug * xg_d, dug * xg, dug * xg_u, dug, duv * xv_d, duv * xv, duv * xv_u, duv]
            return tuple(a + jnp.sum(t[_CTR], axis=0, keepdims=True) for a, t in zip(acc, terms))

        acc = _row_chunks(T, chunk, tuple(jnp.zeros((1, tc), F32) for _ in range(8)))
        for k in range(3):
            dw_ref[0, k:k + 1, :] = acc[k]
            dw_ref[1, k:k + 1, :] = acc[4 + k]
        db_ref[0] = acc[3]
        db_ref[1] = acc[7]

    lo = lambda r: pl.BlockSpec((r, tc), lambda j: (0, j))
    hi = lambda r: pl.BlockSpec((r, tc), lambda j: (0, nb + j))
    st = lambda r: pl.BlockSpec((2, r, tc), lambda j: (0, 0, j))
    return pl.pallas_call(
        body, name=name, grid=(nb,),
        in_specs=[st(T), lo(T), lo(3), hi(3), lo(1), hi(1)],
        out_specs=[st(T), st(3), st(1)],
        out_shape=[jax.ShapeDtypeStruct((2, T, DFF), BF), jax.ShapeDtypeStruct((2, 3, DFF), F32),
                   jax.ShapeDtypeStruct((2, 1, DFF), F32)],
        compiler_params=pltpu.CompilerParams(dimension_semantics=("parallel",)),
    )(u0, df, cw, cw, cb, cb)


def gate_merge_bwd(p, ya, yc, dm, *, name, tm=512):
    def body(ga_ref, gc_ref, ya_ref, yc_ref, dm_ref, dya_ref, dyc_ref, dp_ref):
        sa, sc_ = jax.nn.sigmoid(ga_ref[...]), jax.nn.sigmoid(gc_ref[...])
        dmv = dm_ref[...]
        dya_ref[...] = (dmv * sa).astype(BF)
        dyc_ref[...] = (dmv * sc_).astype(BF)
        dp_ref[:, 0:D] = (dmv * ya_ref[...] * (sa * (1.0 - sa))).astype(BF)
        dp_ref[:, D:2 * D] = (dmv * yc_ref[...] * (sc_ * (1.0 - sc_))).astype(BF)

    blk = pl.BlockSpec((tm, D), lambda i: (i, 0))
    sh = jax.ShapeDtypeStruct((T, D), BF)
    return pl.pallas_call(
        body, name=name, grid=(T // tm,),
        in_specs=[pl.BlockSpec((tm, D), lambda i: (i, O_GA // D)), pl.BlockSpec((tm, D), lambda i: (i, O_GC // D)), blk, blk, blk],
        out_specs=[blk, blk, pl.BlockSpec((tm, 2 * D), lambda i: (i, 0))],
        out_shape=[sh, sh, jax.ShapeDtypeStruct((T, NIN), BF)],
        compiler_params=pltpu.CompilerParams(dimension_semantics=("parallel",)),
    )(p, p, ya, yc, dm)


def convz_bwd(p, dz, cw, cb, dp, *, name):
    o0 = O_CV // (3 * CVB)

    def body(p_ref, dz_ref, w_ref, bias_ref, dp_in, dp_ref, dw_ref, dbias_ref):
        xv, bv, cv = p_ref[:, 0:CVB], p_ref[:, CVB:2 * CVB], p_ref[:, 2 * CVB:3 * CVB]
        ci = cv * xv
        dwc = _conv(ci, w_ref, bias_ref)
        dzv = dz_ref[...]
        ddw = dzv * bv
        dci = _conv_t(ddw, w_ref)
        dp_ref[:, 0:CVB] = (dci * cv).astype(BF)
        dp_ref[:, CVB:2 * CVB] = (dzv * dwc).astype(BF)
        dp_ref[:, 2 * CVB:3 * CVB] = (dci * xv).astype(BF)
        _conv_wgrad(dw_ref, ddw, ci)
        dbias_ref[...] = jnp.sum(ddw, axis=0, keepdims=True)

    own = lambda r: pl.BlockSpec((r, CVB), lambda j: (0, j))
    return pl.pallas_call(
        body, name=name, grid=(CONV // CVB,),
        in_specs=[pl.BlockSpec((T, 3 * CVB), lambda j: (0, o0 + j)), own(T), own(3), own(1),
                  pl.BlockSpec(memory_space=pl.ANY)],
        out_specs=[pl.BlockSpec((T, 3 * CVB), lambda j: (0, o0 + j)), own(3), own(1)],
        out_shape=[jax.ShapeDtypeStruct((T, NIN), BF), jax.ShapeDtypeStruct((3, CONV), F32),
                   jax.ShapeDtypeStruct((1, CONV), F32)],
        input_output_aliases={4: 0},
        compiler_params=pltpu.CompilerParams(dimension_semantics=("parallel",)),
    )(p, dz, cw, cb, dp)


def attn_bwd(q, k, v, do, *, name, tq=1024, kc=768):
    NKC, KC = TKV // kc, kc

    def body(q_ref, k_ref, v_ref, do_ref, dq_ref, dk_ref, dv_ref, s_scr, dp_scr):
        h, i = pl.program_id(0), pl.program_id(1)

        @pl.when(i == 0)
        def _():
            dk_ref[...] = jnp.zeros_like(dk_ref)

        @pl.when((i == 0) & (h % 2 == 0))
        def _():
            dv_ref[...] = jnp.zeros_like(dv_ref)

        qv = q_ref[...]
        dom = jnp.where(_head_mask(h), do_ref[...], jnp.zeros_like(do_ref[...]))
        m = _scores_pass(qv, k_ref, s_scr, kc)
        l = jnp.zeros((tq, 1), F32)
        dsum = jnp.zeros((tq, 1), F32)
        for c in range(NKC):
            cols = slice(c * KC, (c + 1) * KC)
            e = jnp.exp2(s_scr[:, cols] - m)
            s_scr[:, cols] = e
            dp = lax.dot_general(dom, v_ref[cols, :], (((1,), (1,)), ((), ())), preferred_element_type=F32)
            dp_scr[:, cols] = dp
            l = l + jnp.sum(e, axis=-1, keepdims=True)
            dsum = dsum + jnp.sum(e * dp, axis=-1, keepdims=True)
        inv = 1.0 / l
        delta = dsum * inv
        dos = (dom.astype(F32) * inv).astype(BF)
        dq = jnp.zeros((tq, HP), F32)
        for c in range(NKC):
            cols = slice(c * KC, (c + 1) * KC)
            e = s_scr[:, cols]
            ds = (e * (dp_scr[:, cols] - delta) * (inv * SCALE)).astype(BF)
            dq = dq + jnp.dot(ds, k_ref[cols, :], preferred_element_type=F32)
            dk_ref[cols, :] += lax.dot_general(ds, qv, (((0,), (0,)), ((), ())), preferred_element_type=F32)
            dv_ref[cols, :] += lax.dot_general(e.astype(BF), dos, (((0,), (0,)), ((), ())), preferred_element_type=F32)
        dq_ref[...] = dq

    return pl.pallas_call(
        body, name=name, grid=(NH, T // tq),
        in_specs=[pl.BlockSpec((tq, HP), lambda h, i: (i, h)), pl.BlockSpec((TKV, HP), lambda h, i: (0, h)),
                  pl.BlockSpec((TKV, 2 * DV), lambda h, i: (0, h // 2)), pl.BlockSpec((tq, 2 * DV), lambda h, i: (i, h // 2))],
        out_specs=[pl.BlockSpec((tq, HP), lambda h, i: (i, h)), pl.BlockSpec((TKV, HP), lambda h, i: (0, h)),
                   pl.BlockSpec((TKV, 2 * DV), lambda h, i: (0, h // 2))],
        out_shape=[jax.ShapeDtypeStruct((T, NH * HP), F32), jax.ShapeDtypeStruct((TKV, NH * HP), F32),
                   jax.ShapeDtypeStruct((TKV, NH * DV), F32)],
        scratch_shapes=[pltpu.VMEM((tq, TKV), F32), pltpu.VMEM((tq, TKV), F32)],
        compiler_params=pltpu.CompilerParams(dimension_semantics=("arbitrary", "arbitrary")),
    )(q, k, v, do)


def qprep_bwd(p, dq, qg, wq2, cq_t, sq_t, dp, *, name, tm=256):
    qcol = O_Q // 512

    def body(p_ref, dq_ref, g_ref, w_ref, c_ref, s_ref, dp_in, dp_ref, dq2_ref, dg_ref):
        i = pl.program_id(0)
        dqv = dq_ref[...]
        cc = jnp.concatenate([c_ref[...]] * NH, axis=1)
        ss = jnp.concatenate([s_ref[...]] * NH, axis=1)
        dq2 = jnp.concatenate([dqv * cc, dqv * ss], axis=1).astype(BF)
        dq2_ref[...] = dq2
        dcq = lax.dot_general(dq2, w_ref[...], (((1,), (1,)), ((), ())), preferred_element_type=F32)
        pq = p_ref[...]
        r = lax.rsqrt(jnp.sum(pq * pq, axis=-1, keepdims=True) * (1.0 / QL) + EPS)
        xh = pq * r
        a = dcq * g_ref[...]
        dp_ref[...] = (r * (a - xh * (jnp.sum(a * xh, axis=-1, keepdims=True) * (1.0 / QL)))).astype(BF)
        dg = jnp.sum(dcq * xh, axis=0, keepdims=True)

        @pl.when(i == 0)
        def _():
            dg_ref[...] = dg

        @pl.when(i > 0)
        def _():
            dg_ref[...] += dg

    return pl.pallas_call(
        body, name=name, grid=(T // tm,),
        in_specs=[pl.BlockSpec((tm, 512), lambda i: (i, qcol)), pl.BlockSpec((tm, NH * HP), lambda i: (i, 0)), _row(512),
                  pl.BlockSpec((512, 2 * NH * HP), lambda i: (0, 0)),
                  pl.BlockSpec((tm, HP), lambda i: (i, 0)), pl.BlockSpec((tm, HP), lambda i: (i, 0)),
                  pl.BlockSpec(memory_space=pl.ANY)],
        out_specs=[pl.BlockSpec((tm, 512), lambda i: (i, qcol)), pl.BlockSpec((tm, 2 * NH * HP), lambda i: (i, 0)), _row(512)],
        out_shape=[jax.ShapeDtypeStruct((T, NIN), BF), jax.ShapeDtypeStruct((T, 2 * NH * HP), BF),
                   jax.ShapeDtypeStruct((1, 512), F32)],
        input_output_aliases={6: 0},
        compiler_params=pltpu.CompilerParams(dimension_semantics=("arbitrary",)),
    )(p, dq, qg, wq2, cq_t, sq_t, dp)


def kvprep_bwd(pc, p, dk, dv, kvg, wkv2, ck, sk, dp, *, name, tm=256):
    assert tm == TC
    nb = TKV // tm
    kvcol = O_KV // 512

    def body(pc_ref, p_ref, dk_ref, dv_ref, g_ref, w_ref, ck_ref, sk_ref, dp_in, dp_ref, dpc_ref, dkv2_ref, dg_ref):
        i = pl.program_id(0)
        t = jnp.where(i == NLAT, pc_ref[...], p_ref[...])
        pk = t[:, :KVL]
        r = lax.rsqrt(jnp.mean(pk * pk, axis=-1, keepdims=True) + EPS)
        xh = pk * r
        dkv = dk_ref[...]
        dkv2 = jnp.concatenate([dkv, dv_ref[...]], axis=1).astype(BF)
        dkv2_ref[...] = dkv2
        dckv = lax.dot_general(dkv2, w_ref[...], (((1,), (1,)), ((), ())), preferred_element_type=F32)
        a = dckv * g_ref[...]
        dpk = r * (a - xh * jnp.mean(a * xh, axis=-1, keepdims=True))
        dkr = dkv[:, 0:HP]
        for hh in range(1, NH):
            dkr = dkr + dkv[:, hh * HP:(hh + 1) * HP]
        res = jnp.concatenate([dpk, dkr * ck_ref[...], dkr * sk_ref[...]], axis=1).astype(BF)
        dg = jnp.sum(dckv * xh, axis=0, keepdims=True)

        @pl.when(i == 0)
        def _():
            dg_ref[...] = dg

        @pl.when(i > 0)
        def _():
            dg_ref[...] += dg

        @pl.when(i < NLAT)
        def _():
            dp_ref[...] = res

        @pl.when(i == NLAT)
        def _():
            dpc_ref[...] = res

    rb = lambda w: pl.BlockSpec((tm, w), lambda i: (i, 0))
    return pl.pallas_call(
        body, name=name, grid=(nb,),
        in_specs=[pl.BlockSpec((tm, 512), lambda i: (0, 0)),
                  pl.BlockSpec((tm, 512), lambda i: (jnp.minimum(i, NLAT - 1), kvcol)),
                  rb(NH * HP), rb(NH * DV), _row(KVL), pl.BlockSpec((KVL, NH * HP + NH * DV), lambda i: (0, 0)),
                  rb(HP), rb(HP), pl.BlockSpec(memory_space=pl.ANY)],
        out_specs=[pl.BlockSpec((tm, 512), lambda i: (jnp.minimum(i, NLAT - 1), kvcol)),
                   pl.BlockSpec((tm, 512), lambda i: (0, 0)), rb(NH * HP + NH * DV), _row(KVL)],
        out_shape=[jax.ShapeDtypeStruct((T, NIN), BF), jax.ShapeDtypeStruct((TC, 512), BF),
                   jax.ShapeDtypeStruct((TKV, NH * HP + NH * DV), BF), jax.ShapeDtypeStruct((1, KVL), F32)],
        input_output_aliases={8: 0},
        compiler_params=pltpu.CompilerParams(dimension_semantics=("arbitrary",)),
    )(pc, p, dk, dv, kvg, wkv2, ck, sk, dp)


def _pieces(src, width, n):
    out, c = [], src
    while c < src + width:
        k = c // n
        w = min(src + width, (k + 1) * n) - c
        out.append((k, c - k * n, c - src, w))
        c += w
    return out


def _win_moves():
    mv = [(2208, 1024, O_GA), (3232, 1024, O_GC), (0, KVL, O_KV), (256, DR, O_KV + KVL + DN), (288, QL, O_Q)]
    mv += [(256 + _swap_start(g), 8, O_KV + KVL + HP + DN + 8 * g) for g in range(4)]
    for j in range(CONV // CVB):
        base = O_CV + 3 * CVB * j
        mv += [(672 + CVB * j, CVB, base), (1184 + CVB * j, CVB, base + CVB), (1696 + CVB * j, CVB, base + 2 * CVB)]
    return mv


_WIN_ZERO = [(O_KV + KVL, DN), (O_KV + KVL + DN + DR, HP - DN - DR), (O_KV + KVL + HP, DN),
             (O_KV + KVL + HP + DN + DR, HP - DN - DR), (O_Q + QL, 512 - QL)]


def build_win(g, *, name, tm=256):
    def body(g_ref, o_ref):
        for src, w, dst in _win_moves():
            for k, a, off, pw in _pieces(src, w, SH_IN):
                o_ref[:, dst + off:dst + off + pw] = g_ref[k, :, a:a + pw]
        for c0, w in _WIN_ZERO:
            o_ref[:, c0:c0 + w] = jnp.zeros((tm, w), o_ref.dtype)

    return pl.pallas_call(
        body, name=name, grid=(D // tm,), in_specs=[pl.BlockSpec((NDEV, tm, SH_IN), lambda i: (0, i, 0))],
        out_specs=pl.BlockSpec((tm, NIN), lambda i: (i, 0)), out_shape=jax.ShapeDtypeStruct((D, NIN), g.dtype),
        compiler_params=pltpu.CompilerParams(dimension_semantics=("parallel",)),
    )(g)


def shard_win_grad(dwt, dwct, *, name, tc=256):
    def body(dw_ref, dwc_ref, o_ref, kvs):
        kvs[...] = dw_ref[O_KV:O_KV + 512, :] + dwc_ref[...]

        def src(row, w):
            if O_KV <= row < O_KV + 512:
                return kvs[row - O_KV:row - O_KV + w, :]
            return dw_ref[row:row + w, :]

        for s, w, dst in _win_moves():
            if w == 8 or s == 256:
                continue
            for k, a, off, pw in _pieces(s, w, SH_IN):
                o_ref[k, a:a + pw, :] = src(dst + off, pw).astype(o_ref.dtype)
        for g in range(4):
            val = src(O_KV + KVL + DN + 8 * g, 8) + src(O_KV + KVL + HP + DN + _swap_start(g), 8)
            o_ref[0, 256 + 8 * g:256 + 8 * g + 8, :] = val.astype(o_ref.dtype)

    return pl.pallas_call(
        body, name=name, grid=(D // tc,),
        in_specs=[pl.BlockSpec((NIN, tc), lambda j: (0, j)), pl.BlockSpec((512, tc), lambda j: (0, j))],
        out_specs=pl.BlockSpec((NDEV, SH_IN, tc), lambda j: (0, 0, j)),
        out_shape=jax.ShapeDtypeStruct((NDEV, SH_IN, D), BF),
        scratch_shapes=[pltpu.VMEM((512, tc), F32)],
        compiler_params=pltpu.CompilerParams(dimension_semantics=("parallel",)),
    )(dwt, dwct)


def build_wq_wkv(gq, gkv, *, name):
    def body(gq_ref, gkv_ref, q_ref, kv_ref):
        q_ref[...] = jnp.zeros_like(q_ref)
        kv_ref[...] = jnp.zeros_like(kv_ref)
        for h in range(NH):
            q_ref[0:QL, h * HP:h * HP + DN + DR] = gq_ref[h]
            for g in range(4):
                c0 = NH * HP + h * HP + DN + 8 * g
                q_ref[0:QL, c0:c0 + 8] = gq_ref[h, :, DN + _swap_start(g):DN + _swap_start(g) + 8]
            kv_ref[:, h * HP:h * HP + DN] = gkv_ref[h, :, 0:DN]
            kv_ref[:, NH * HP + h * DV:NH * HP + (h + 1) * DV] = gkv_ref[h, :, DN:DN + DV]

    vm = pl.BlockSpec(memory_space=pltpu.VMEM)
    return pl.pallas_call(
        body, name=name, in_specs=[vm, vm], out_specs=[vm, vm],
        out_shape=[jax.ShapeDtypeStruct((512, 2 * NH * HP), gq.dtype), jax.ShapeDtypeStruct((KVL, NH * HP + NH * DV), gq.dtype)],
    )(gq, gkv)


def shard_wq_wkv_grad(dwq2, dwkv2, *, name):
    def body(q_ref, kv_ref, gq_ref, gkv_ref):
        for h in range(NH):
            gq_ref[h, :, 0:DN] = q_ref[0:QL, h * HP:h * HP + DN].astype(BF)
            for g in range(4):
                a = q_ref[0:QL, h * HP + DN + 8 * g:h * HP + DN + 8 * g + 8]
                c0 = NH * HP + h * HP + DN + _swap_start(g)
                gq_ref[h, :, DN + 8 * g:DN + 8 * g + 8] = (a + q_ref[0:QL, c0:c0 + 8]).astype(BF)
            gkv_ref[h, :, 0:DN] = kv_ref[:, h * HP:h * HP + DN].astype(BF)
            gkv_ref[h, :, DN:DN + DV] = kv_ref[:, NH * HP + h * DV:NH * HP + (h + 1) * DV].astype(BF)

    vm = pl.BlockSpec(memory_space=pltpu.VMEM)
    return pl.pallas_call(
        body, name=name, in_specs=[vm, vm], out_specs=[vm, vm],
        out_shape=[jax.ShapeDtypeStruct((NDEV, QL, (DN + DR)), BF), jax.ShapeDtypeStruct((NDEV, KVL, DN + DV), BF)],
    )(dwq2, dwkv2)


def unshard_cols(g, *, name, tm=256):
    _, K, n = g.shape
    tm = _pick(K, tm, 16)

    def body(g_ref, o_ref):
        for k in range(NDEV):
            o_ref[:, k * n:(k + 1) * n] = g_ref[k]

    return pl.pallas_call(
        body, name=name, grid=(K // tm,), in_specs=[pl.BlockSpec((NDEV, tm, n), lambda i: (0, i, 0))],
        out_specs=pl.BlockSpec((tm, NDEV * n), lambda i: (i, 0)), out_shape=jax.ShapeDtypeStruct((K, NDEV * n), g.dtype),
        compiler_params=pltpu.CompilerParams(dimension_semantics=("parallel",)),
    )(g)


def shard_cols(w, *, name, tm=256):
    K, n8 = w.shape
    n = n8 // NDEV
    tm = _pick(K, tm, 16)

    def body(w_ref, o_ref):
        for k in range(NDEV):
            o_ref[k] = w_ref[:, k * n:(k + 1) * n]

    return pl.pallas_call(
        body, name=name, grid=(K // tm,), in_specs=[pl.BlockSpec((tm, n8), lambda i: (i, 0))],
        out_specs=pl.BlockSpec((NDEV, tm, n), lambda i: (0, i, 0)), out_shape=jax.ShapeDtypeStruct((NDEV, K, n), w.dtype),
        compiler_params=pltpu.CompilerParams(dimension_semantics=("parallel",)),
    )(w)


def _rope_tables():
    t = np.arange(T)
    row = (t // GRID_W).astype(np.float32)
    col = (t % GRID_W).astype(np.float32)
    axis_dim = DR // 2
    inv = (np.float32(ROPE_THETA) ** (-np.arange(0, axis_dim, 2, dtype=np.float32) / np.float32(axis_dim))).astype(np.float32)
    ar, ac = (row[:, None] * inv).astype(np.float32), (col[:, None] * inv).astype(np.float32)
    cosv = np.concatenate([np.cos(ar), np.cos(ar), np.cos(ac), np.cos(ac)], axis=1).astype(np.float32)
    sinv = np.concatenate([-np.sin(ar), np.sin(ar), -np.sin(ac), np.sin(ac)], axis=1).astype(np.float32)
    ck = np.zeros((TKV, HP), np.float32)
    sk = np.zeros((TKV, HP), np.float32)
    ck[T:, DN:DN + DR] = 1.0
    ck[:T, DN:DN + DR] = cosv
    sk[:T, DN:DN + DR] = sinv
    cq = np.zeros((T, HP), np.float32)
    cq[:, :DN] = 1.0
    cq[:, DN:DN + DR] = cosv
    return jnp.asarray(ck), jnp.asarray(sk), jnp.asarray(cq), jnp.asarray(sk[:T])


def _local_step(x, ctx, tgt, mod_lat, mod_ctx, n1g, qg, kvg, n2g, fg, conv_w, conv_b, ffn_w, ffn_b, get_w, put_g, dep0):
    sh1, sc1, g1, sh2, sc2, g2 = [mod_lat[:, i * D:(i + 1) * D] for i in range(6)]
    csh1, csc1 = mod_ctx[:, 0:D], mod_ctx[:, D:2 * D]
    ck, sk, cq_t, sq_t = _rope_tables()
    qg_p = jnp.pad(qg, ((0, 0), (0, 512 - QL)))

    hcat = normmod_cat(ctx, x, n1g, csc1, csh1, sc1, sh1, dep0, name="normmod1")
    win = get_w("in", hcat)
    p = mm(hcat, win, M=T, tn=768, name="in_proj")
    pc = mm(hcat, win, M=TC, N=512, a_off=(T, 0), b_off=(0, O_KV), name="in_proj_ctx")
    wq2, wkv2, wao, wco, wo = get_w("mid", p)
    kh, vh, ckv = kvprep(pc, p, kvg, wkv2, ck, sk, name="kvprep")
    qr, cq = qprep(p, qg_p, wq2, cq_t, sq_t, name="qprep")
    o = attn_fwd(qr, kh, vh, name="attn_fwd")
    z = convz(p, conv_w, conv_b, name="convz")
    ya = mm(o, wao, name="attn_out")
    yc = mm(z, wco, name="conv_out")
    merged = gate_merge(p, ya, yc, name="gate_merge")
    a_out = mm(merged, wo, name="o_proj")
    x1, h2 = resid_normmod(x, a_out, g1, n2g, sc2, sh2, name="resid_normmod2")
    wup, wdn = get_w("ffn", h2)
    u0 = mm(h2, wup, tb=True, o_stack=True, tn=1408, name="up_proj")
    f = ffn_act(u0, ffn_w, ffn_b, name="ffn_act")
    dn = mm(f, wdn, tm=512, tk=DFF, name="down_proj")
    dx2, dd, dfg, loss = final_loss(x1, dn, g2, fg, tgt, name="final_loss")

    df = mm(dd, wdn, tb=True, tn=1408, name="down_proj_dx")
    dwdn = mm(f, dd, ta=True, out_dtype=BF, tm=1408, name="down_proj_dw")
    du0, dffn_w, dffn_b = ffn_act_bwd(u0, df, ffn_w, ffn_b, name="ffn_act_bwd")
    dwup = mm(du0, h2, ta=True, a_stack=True, out_dtype=BF, tm=1408, name="up_proj_dw")
    tok = put_g("ffn", dict(dwup=dwup, dwdn=dwdn))
    dh2 = mm(du0, wup, a_stack=True, dep=tok, name="up_proj_dx")
    dx1, da, st2 = normmod_bwd(x1, dh2, n2g, sc2, dx2, dn, g1, name="normmod2_bwd")

    dmerged = mm(da, wo, tb=True, name="o_proj_dx")
    dwo = mm(merged, da, ta=True, out_dtype=BF, tn=512, name="o_proj_dw")
    dya, dyc, dp = gate_merge_bwd(p, ya, yc, dmerged, name="gate_merge_bwd")
    do = mm(dya, wao, tb=True, out_dtype=BF, name="attn_out_dx")
    dwao = mm(o, dya, ta=True, out_dtype=BF, tn=512, name="attn_out_dw")
    dwco = mm(z, dyc, ta=True, out_dtype=BF, tn=512, name="conv_out_dw")
    tok = put_g("mid", dict(dwao=dwao, dwco=dwco, dwo=dwo))
    dz = mm(dyc, wco, tb=True, dep=tok, name="conv_out_dx")
    dp, dconv_w, dconv_b = convz_bwd(p, dz, conv_w, conv_b, dp, name="convz_bwd")
    dq, dk, dv = attn_bwd(qr, kh, vh, do, name="attn_bwd")
    dp, dq2, dqg = qprep_bwd(p, dq, qg_p, wq2, cq_t, sq_t, dp, name="qprep_bwd")
    dwq2 = mm(cq, dq2, ta=True, name="q_up_dw")
    dp, dpc, dkv2, dkvg = kvprep_bwd(pc, p, dk, dv, kvg, wkv2, ck, sk, dp, name="kvprep_bwd")
    dwkv2 = mm(ckv, dkv2, ta=True, name="kv_up_dw")
    tok = put_g("qkv", dict(dwq2=dwq2, dwkv2=dwkv2))

    dwin = mm(dp, hcat, ta=True, K=T, tm=768, dep=tok, name="in_proj_dw")
    dwin_c = mm(dpc, hcat, ta=True, K=TC, b_off=(T, 0), name="in_proj_ctx_dw")
    tok = put_g("in", dict(dwin=dwin, dwin_c=dwin_c))
    dh = mm(dp, win, tb=True, dep=tok, name="in_proj_dx")
    dhc = mm(dpc, win, tb=True, N=D, K=512, b_off=(0, O_KV), name="in_proj_ctx_dx")
    dx, _, st1 = normmod_bwd(x, dh, n1g, sc1, dx1, a_out, g1, name="normmod1_bwd")
    stc = normmod_bwd(ctx, dhc, n1g, csc1, None, None, None, name="normmod1_ctx_bwd")

    zrow = jnp.zeros((1, D), F32)
    dmod_lat = jnp.concatenate([st1[0:1], st1[1:2], st1[3:4], st2[0:1], st2[1:2], st2[3:4]], axis=1)
    dmod_ctx = jnp.concatenate([stc[0:1], stc[1:2], zrow, zrow, zrow, zrow], axis=1)
    return dict(
        loss=loss, dx=dx, dmod_lat=dmod_lat, dmod_ctx=dmod_ctx,
        dn1g=st1[2:3] + stc[2:3], dqg=dqg, dkvg=dkvg, dn2g=st2[2:3], dfg=dfg,
        dconv_w=dconv_w, dconv_b=dconv_b, dffn_w=dffn_w, dffn_b=dffn_b)


def _me():
    x, y, c = lax.axis_index("x"), lax.axis_index("y"), lax.axis_index("c")
    return x, y, c, 4 * x + 2 * y + c


def _peer(x, y, c, k):
    px = 1 - x if k & 4 else x
    py = 1 - y if k & 2 else y
    pc = 1 - c if k & 1 else c
    return (px, py, pc), 4 * px + 2 * py + pc


def _exchange_tiles(src_of_peer, buf, send_sem, recv_sem):
    x, y, c, me = _me()
    for k in range(1, NDEV):
        dev, lin = _peer(x, y, c, k)
        pltpu.make_async_remote_copy(src_ref=src_of_peer(lin), dst_ref=buf.at[me], send_sem=send_sem, recv_sem=recv_sem,
                                     device_id=dev, device_id_type=MESH).start()
    seven = buf.at[pl.ds(0, NDEV - 1)]
    pltpu.make_async_remote_copy(src_ref=seven, dst_ref=seven, send_sem=send_sem, recv_sem=recv_sem,
                                 device_id=(x, y, c), device_id_type=MESH).wait()


def _silu(z):
    return z * jax.nn.sigmoid(z)


def ada_fwd(c, c_ctx, ffn_w, conv_w, w_shard, b_shard, deps, *, name):
    nsh = w_shard.shape[1]
    deps = [d for d in deps if d is not None]

    def body(c_ref, cc_ref, fw_ref, cw_ref, w_ref, b_ref, *rest):
        s_ref, m_ref, mine, res, sems = rest[len(deps):]
        x, y, c, me = _me()
        mine[0:1, :] = _silu(c_ref[...])
        mine[1:2, :] = _silu(cc_ref[...])
        mine[2:5, :] = fw_ref[...]
        mine[5:8, :] = cw_ref[...]
        s_ref[me] = mine[...]
        _exchange_tiles(lambda lin: mine, s_ref, sems.at[0], sems.at[1])
        sall = s_ref[...].reshape(NDEV * 8, D).astype(BF)
        r = jnp.dot(sall, w_ref[...].astype(BF), preferred_element_type=F32) + b_ref[...]
        res[...] = r.reshape(NDEV, 8, nsh)
        m_ref[me] = res[me]
        _exchange_tiles(lambda lin: res.at[lin], m_ref, sems.at[2], sems.at[3])

    vm = pl.BlockSpec(memory_space=pltpu.VMEM)
    return pl.pallas_call(
        body, name=name, in_specs=[vm] * 6 + [pl.BlockSpec(memory_space=pl.ANY)] * len(deps), out_specs=[vm, vm],
        out_shape=[jax.ShapeDtypeStruct((NDEV, 8, D), F32), jax.ShapeDtypeStruct((NDEV, 8, nsh), F32)],
        scratch_shapes=[pltpu.VMEM((8, D), F32), pltpu.VMEM((NDEV, 8, nsh), F32), pltpu.SemaphoreType.DMA((4,))],
    )(c, c_ctx, ffn_w, conv_w, w_shard, b_shard, *deps)


P_DML, P_DMC, P_N1, P_QG, P_KVG, P_CB, P_N2, P_FB, P_FG, P_CW, P_FW, P_LOSS, P_ROWS = 0, 6, 12, 13, 14, 15, 16, 17, 23, 24, 27, 45, 48
FROWS = 3


def sync_small(r, deps, *, name):
    ins = [r["dmod_lat"], r["dmod_ctx"], r["dn1g"], r["dqg"], r["dkvg"], r["dconv_b"], r["dn2g"], r["dffn_b"], r["dfg"],
           r["dconv_w"], r["dffn_w"], r["loss"]]

    def put_wide(p, row0, row, n):
        for j in range(-(-n // D)):
            w = min(D, n - j * D)
            p[row0 + j:row0 + j + 1, 0:w] = row[:, j * D:j * D + w]

    def body(dml, dmc, n1, qg, kvg, cb, n2, fb, fg, cw, fw, loss, *rest):
        a_ref, sum_ref, p, sems = rest[len(deps):]
        x, y, c, me = _me()
        p[...] = jnp.zeros_like(p)
        put_wide(p, P_DML, dml, 6 * D)
        put_wide(p, P_DMC, dmc, 6 * D)
        put_wide(p, P_N1, n1, D)
        put_wide(p, P_QG, qg, 512)
        put_wide(p, P_KVG, kvg, KVL)
        put_wide(p, P_CB, cb, CONV)
        put_wide(p, P_N2, n2, D)
        put_wide(p, P_FG, fg, D)
        put_wide(p, P_LOSS, loss, 128)
        for s in range(2):
            put_wide(p, P_FB + FROWS * s, fb.at[s], DFF)
        for k in range(3):
            put_wide(p, P_CW + k, cw.at[k:k + 1], CONV)
            for s in range(2):
                put_wide(p, P_FW + FROWS * (2 * k + s), fw.at[s, k:k + 1], DFF)
        a_ref[me] = p[...]
        _exchange_tiles(lambda lin: p, a_ref, sems.at[0], sems.at[1])
        acc = a_ref[0]
        for k in range(1, NDEV):
            acc = acc + a_ref[k]
        sum_ref[...] = acc

    vm = pl.BlockSpec(memory_space=pltpu.VMEM)
    return pl.pallas_call(
        body, name=name, in_specs=[vm] * len(ins) + [pl.BlockSpec(memory_space=pl.ANY)] * len(deps), out_specs=[vm, vm],
        out_shape=[jax.ShapeDtypeStruct((NDEV, P_ROWS, D), F32), jax.ShapeDtypeStruct((P_ROWS, D), F32)],
        scratch_shapes=[pltpu.VMEM((P_ROWS, D), F32), pltpu.SemaphoreType.DMA((2,))],
    )(*ins, *deps)


def ada_bwd(s_all, dml, dmc, w_shard, c_ctx, *, name):
    nsh = w_shard.shape[1]

    def body(s_ref, dml_ref, dmc_ref, w_ref, c_ref, dw_ref, gc_ref, s16, dm16, part, buf, sems):
        x, y, c, me = _me()
        s16[...] = jnp.zeros_like(s16)
        dm16[...] = jnp.zeros_like(dm16)
        for k in range(NDEV):
            s16[k:k + 1, :] = s_ref[k, 0:1, :]
        s16[8:9, :] = s_ref[0, 1:2, :]
        dm16[0:8, :] = dml_ref[...]
        dm16[8:9, :] = dmc_ref[...]
        dw_ref[...] = lax.dot_general(s16[...].astype(BF), dm16[...].astype(BF), (((0,), (0,)), ((), ())),
                                      preferred_element_type=F32)
        part[...] = lax.dot_general(dm16[8:16, :].astype(BF), w_ref[...].astype(BF), (((1,), (1,)), ((), ())),
                                    preferred_element_type=F32)
        buf[me] = part[...]
        _exchange_tiles(lambda lin: part, buf, sems.at[0], sems.at[1])
        acc = buf[0]
        for k in range(1, NDEV):
            acc = acc + buf[k]
        z = c_ref[...]
        sg = jax.nn.sigmoid(z)
        gc_ref[...] = acc * (sg * (1.0 + z * (1.0 - sg)))

    vm = pl.BlockSpec(memory_space=pltpu.VMEM)
    return pl.pallas_call(
        body, name=name, in_specs=[vm] * 5, out_specs=[vm, vm],
        out_shape=[jax.ShapeDtypeStruct((D, nsh), F32), jax.ShapeDtypeStruct((8, D), F32)],
        scratch_shapes=[pltpu.VMEM((16, D), F32), pltpu.VMEM((16, nsh), F32), pltpu.VMEM((8, D), F32),
                        pltpu.VMEM((NDEV, 8, D), F32), pltpu.SemaphoreType.DMA((2,))],
    )(s_all, dml, dmc, w_shard, c_ctx)


HBM_SPEC = pl.BlockSpec(memory_space=pltpu.HBM)
SEM_SPEC = pl.BlockSpec(memory_space=pltpu.SEMAPHORE)
EFFECT = pltpu.SideEffectType.DATAFLOW_SIDE_EFFECTING


ALL_PEERS = tuple(range(1, NDEV))
FIRST_HOP = (1, 2, 4, 6)
RELAY = (2, 4, 6)


def _exchange_copies(srcs, lands, send, recv, per_peer, peers):
    x, y, c, me = _me()
    n = len(peers)
    cps = []
    for t in range(len(srcs)):
        for j, k in enumerate(peers):
            dev, lin = _peer(x, y, c, k)
            cps.append(pltpu.make_async_remote_copy(
                src_ref=srcs[t].at[lin] if per_peer else srcs[t], dst_ref=lands[t].at[me],
                send_sem=send.at[n * t + j], recv_sem=recv.at[n * t + j], device_id=dev, device_id_type=MESH))
    return cps


def _relay_copies(lands, send, recv):
    x, y, c, me = _me()
    n = len(RELAY)
    cps = []
    for t in range(len(lands)):
        for j, k in enumerate(RELAY):
            slot = lands[t].at[_peer(x, y, c, k)[1]]
            cps.append(pltpu.make_async_remote_copy(
                src_ref=slot, dst_ref=slot, send_sem=send.at[n * t + j], recv_sem=recv.at[n * t + j],
                device_id=(x, y, 1 - c), device_id_type=MESH))
    return cps


def _own_copies(srcs, lands, own, per_peer):
    me = _me()[3]
    return [pltpu.make_async_copy(srcs[t].at[me] if per_peer else srcs[t], lands[t].at[me], own.at[t])
            for t in range(len(srcs))]


def exchange_start(srcs, *, per_peer, name, dep=None, peers=ALL_PEERS):
    nt = len(srcs)
    ns = len(peers) * nt
    land_shapes = [(a.shape if per_peer else (NDEV,) + a.shape) for a in srcs]
    deps = [] if dep is None else [dep]

    def body(*refs):
        src, land = refs[:nt], refs[nt:2 * nt]
        send, recv, own = refs[2 * nt + len(deps):2 * nt + len(deps) + 3]
        for cp in _exchange_copies(src, land, send, recv, per_peer, peers) + _own_copies(src, land, own, per_peer):
            cp.start()
        refs[-1][...] = jnp.zeros_like(refs[-1])

    hb = lambda a: pltpu.with_memory_space_constraint(a, pltpu.HBM)
    outs = pl.pallas_call(
        body, name=name,
        out_shape=(pltpu.SemaphoreType.DMA((ns,)), pltpu.SemaphoreType.DMA((ns,)), pltpu.SemaphoreType.DMA((nt,)),
                   *[pltpu.HBM(a.shape, a.dtype) for a in srcs], *[pltpu.HBM(s, a.dtype) for s, a in zip(land_shapes, srcs)],
                   jax.ShapeDtypeStruct((8, 128), F32)),
        in_specs=[HBM_SPEC] * (2 * nt) + [pl.BlockSpec(memory_space=pl.ANY)] * len(deps),
        out_specs=(SEM_SPEC, SEM_SPEC, SEM_SPEC, *([HBM_SPEC] * (2 * nt)), pl.BlockSpec(memory_space=pltpu.VMEM)),
        input_output_aliases={i: 3 + i for i in range(2 * nt)},
        compiler_params=pltpu.CompilerParams(has_side_effects=EFFECT),
    )(*[hb(a) for a in srcs], *[hb(lax.empty(s, a.dtype)) for s, a in zip(land_shapes, srcs)], *deps)
    return dict(send=outs[0], recv=outs[1], own=outs[2], src=list(outs[3:3 + nt]), land=list(outs[3 + nt:3 + 2 * nt]),
                token=outs[-1], per_peer=per_peer, peers=peers)


def exchange_wait(h, after, *, name):
    nt = len(h["src"])
    per_peer, peers = h["per_peer"], h["peers"]

    def body(*refs):
        src, land, send, recv, own = refs[:nt], refs[nt:2 * nt], refs[2 * nt], refs[2 * nt + 1], refs[2 * nt + 2]
        for cp in _exchange_copies(src, land, send, recv, per_peer, peers):
            cp.wait_send()
            cp.wait_recv()
        for cp in _own_copies(src, land, own, per_peer):
            cp.wait()

    outs = pl.pallas_call(
        body, name=name,
        out_shape=(*[pltpu.HBM(a.shape, a.dtype) for a in h["src"]], *[pltpu.HBM(a.shape, a.dtype) for a in h["land"]]),
        in_specs=[HBM_SPEC] * (2 * nt) + [SEM_SPEC, SEM_SPEC, SEM_SPEC, pl.BlockSpec(memory_space=pl.ANY)],
        out_specs=tuple([HBM_SPEC] * (2 * nt)),
        input_output_aliases={i: i for i in range(2 * nt)},
        compiler_params=pltpu.CompilerParams(has_side_effects=EFFECT),
    )(*h["src"], *h["land"], h["send"], h["recv"], h["own"], after)
    return list(outs[nt:])


def relay_start(lands, *, name):
    nt = len(lands)
    ns = len(RELAY) * nt

    def body(*refs):
        for cp in _relay_copies(refs[:nt], refs[nt], refs[nt + 1]):
            cp.start()

    outs = pl.pallas_call(
        body, name=name,
        out_shape=(pltpu.SemaphoreType.DMA((ns,)), pltpu.SemaphoreType.DMA((ns,)),
                   *[pltpu.HBM(a.shape, a.dtype) for a in lands]),
        in_specs=[HBM_SPEC] * nt, out_specs=(SEM_SPEC, SEM_SPEC, *([HBM_SPEC] * nt)),
        input_output_aliases={i: 2 + i for i in range(nt)},
        compiler_params=pltpu.CompilerParams(has_side_effects=EFFECT),
    )(*lands)
    return dict(send=outs[0], recv=outs[1], land=list(outs[2:]))


def relay_wait(h, *, name):
    nt = len(h["land"])

    def body(*refs):
        for cp in _relay_copies(refs[:nt], refs[nt], refs[nt + 1]):
            cp.wait_send()
            cp.wait_recv()

    outs = pl.pallas_call(
        body, name=name, out_shape=tuple(pltpu.HBM(a.shape, a.dtype) for a in h["land"]),
        in_specs=[HBM_SPEC] * nt + [SEM_SPEC, SEM_SPEC], out_specs=tuple([HBM_SPEC] * nt),
        input_output_aliases={i: i for i in range(nt)},
        compiler_params=pltpu.CompilerParams(has_side_effects=EFFECT),
    )(*h["land"], h["send"], h["recv"])
    return list(outs)


def _adamw_math(w, g, m, v):
    nm = B1 * m + (1.0 - B1) * g
    nv = B2 * v + (1.0 - B2) * (g * g)
    m_hat = nm / (1.0 - B1 ** STEP)
    v_hat = nv / (1.0 - B2 ** STEP)
    return -LR * (m_hat / (jnp.sqrt(v_hat) + AEPS) + WD * w), nm, nv


def adamw_many(ws, gs, ms, vs, *, name):
    n = len(ws)

    def body(*refs):
        for k in range(n):
            d, nm, nv = _adamw_math(refs[k][...], refs[n + k][...], refs[2 * n + k][...], refs[3 * n + k][...])
            refs[4 * n + k][...] = d
            refs[5 * n + k][...] = nm
            refs[6 * n + k][...] = nv

    vm = pl.BlockSpec(memory_space=pltpu.VMEM)
    sh = [jax.ShapeDtypeStruct(w.shape, F32) for w in ws]
    outs = pl.pallas_call(body, name=name, in_specs=[vm] * (4 * n), out_specs=[vm] * (3 * n), out_shape=sh * 3,
                          )(*ws, *gs, *ms, *vs)
    return outs[:n], outs[n:2 * n], outs[2 * n:]


def adamw(w, g, m, v, *, name, tr=256):
    R, C = w.shape
    tr = _pick(R, tr, 8)

    def body(w_ref, g_ref, m_ref, v_ref, d_ref, nm_ref, nv_ref):
        d_ref[...], nm_ref[...], nv_ref[...] = _adamw_math(w_ref[...], g_ref[...], m_ref[...], v_ref[...])

    blk = pl.BlockSpec((tr, C), lambda i: (i, 0))
    sh = jax.ShapeDtypeStruct((R, C), F32)
    return pl.pallas_call(
        body, name=name, grid=(R // tr,), in_specs=[blk, blk, blk, blk], out_specs=[blk, blk, blk],
        out_shape=[sh, sh, sh], compiler_params=pltpu.CompilerParams(dimension_semantics=("parallel",)),
    )(w, g, m, v)


def adamw_slots(w, slots, m, v, *, name, tr=256):
    unit = w.ndim == 3
    R, C = w.shape[0], w.shape[-1]
    if R % 16 == 0:
        tr = _pick(R, tr, 16)
    else:
        tr = 144

    def body(w_ref, s_ref, m_ref, v_ref, g_ref, d_ref, nm_ref, nv_ref):
        g = s_ref[0].astype(F32)
        for k in range(1, NDEV):
            g = g + s_ref[k].astype(F32)
        g_ref[...] = g
        d_ref[...], nm_ref[...], nv_ref[...] = _adamw_math(w_ref[...], g, m_ref[...], v_ref[...])

    blk = pl.BlockSpec((tr, None, C), lambda i: (i, 0, 0)) if unit else pl.BlockSpec((tr, C), lambda i: (i, 0))
    sh = jax.ShapeDtypeStruct(w.shape, F32)
    return pl.pallas_call(
        body, name=name, grid=(pl.cdiv(R, tr),), in_specs=[blk, pl.BlockSpec((NDEV, tr, C), lambda i: (0, i, 0)), blk, blk],
        out_specs=[blk, blk, blk, blk], out_shape=[sh, sh, sh, sh],
        compiler_params=pltpu.CompilerParams(dimension_semantics=("parallel",)),
    )(w, slots, m, v)


def _padc(a, n=D):
    return jnp.pad(a, ((0, 0), (0, n - a.shape[1])))


def kernel(x, c, ctx, c_ctx, w_ada, b_ada, norm1_g, w_in, q_norm_g, kv_norm_g, w_uq, w_ukv, conv_w, conv_b, w_attn_out, w_conv_out, w_o, norm2_g, w_up, ffn_conv_w, ffn_conv_b, w_down, final_g, loss_target, m_c_ctx, m_w_ada, m_b_ada, m_norm1_g, m_w_in, m_q_norm_g, m_kv_norm_g, m_w_uq, m_w_ukv, m_conv_w, m_conv_b, m_w_attn_out, m_w_conv_out, m_w_o, m_norm2_g, m_w_up, m_ffn_conv_w, m_ffn_conv_b, m_w_down, m_final_g, v_c_ctx, v_w_ada, v_b_ada, v_norm1_g, v_w_in, v_q_norm_g, v_kv_norm_g, v_w_uq, v_w_ukv, v_conv_w, v_conv_b, v_w_attn_out, v_w_conv_out, v_w_o, v_norm2_g, v_w_up, v_ffn_conv_w, v_ffn_conv_b, v_w_down, v_final_g):
    me = 4 * lax.axis_index("x") + 2 * lax.axis_index("y") + lax.axis_index("c")
    W = dict(c_ctx=c_ctx, w_ada=w_ada, b_ada=b_ada, norm1_g=norm1_g, w_in=w_in, q_norm_g=q_norm_g, kv_norm_g=kv_norm_g,
             w_uq=w_uq, w_ukv=w_ukv, conv_w=conv_w, conv_b=conv_b, w_attn_out=w_attn_out, w_conv_out=w_conv_out, w_o=w_o,
             norm2_g=norm2_g, w_up=w_up, ffn_conv_w=ffn_conv_w, ffn_conv_b=ffn_conv_b, w_down=w_down, final_g=final_g)
    M = dict(c_ctx=m_c_ctx, w_ada=m_w_ada, b_ada=m_b_ada, norm1_g=m_norm1_g, w_in=m_w_in, q_norm_g=m_q_norm_g,
             kv_norm_g=m_kv_norm_g, w_uq=m_w_uq, w_ukv=m_w_ukv, conv_w=m_conv_w, conv_b=m_conv_b, w_attn_out=m_w_attn_out,
             w_conv_out=m_w_conv_out, w_o=m_w_o, norm2_g=m_norm2_g, w_up=m_w_up, ffn_conv_w=m_ffn_conv_w,
             ffn_conv_b=m_ffn_conv_b, w_down=m_w_down, final_g=m_final_g)
    V = dict(c_ctx=v_c_ctx, w_ada=v_w_ada, b_ada=v_b_ada, norm1_g=v_norm1_g, w_in=v_w_in, q_norm_g=v_q_norm_g,
             kv_norm_g=v_kv_norm_g, w_uq=v_w_uq, w_ukv=v_w_ukv, conv_w=v_conv_w, conv_b=v_conv_b, w_attn_out=v_w_attn_out,
             w_conv_out=v_w_conv_out, w_o=v_w_o, norm2_g=v_norm2_g, w_up=v_w_up, ffn_conv_w=v_ffn_conv_w,
             ffn_conv_b=v_ffn_conv_b, w_down=v_w_down, final_g=v_final_g)
    names = list(W)
    transposed = ("w_up",)
    as2d = lambda k, a: (a.reshape(1, -1) if a.ndim == 1 else
                         a[0].T if k in transposed else a.reshape(a.shape[-2], a.shape[-1]))
    W2 = {k: as2d(k, a) for k, a in W.items()}
    M2 = {k: as2d(k, a) for k, a in M.items()}
    V2 = {k: as2d(k, a) for k, a in V.items()}
    unit3 = lambda a: jnp.transpose(a, (2, 0, 1))
    W3, M3, V3 = unit3(W["w_in"]), unit3(M["w_in"]), unit3(V["w_in"])
    nsh = W2["w_ada"].shape[1]

    b_sh = lax.dynamic_slice(W2["b_ada"], (0, me * nsh), (1, nsh))
    s_all, m_all = ada_fwd(c, W2["c_ctx"], _padc(W2["ffn_conv_w"]), _padc(W2["conv_w"]), W2["w_ada"], b_sh, [],
                           name="ada_fwd")
    mod_lat = m_all[:, 0, :].reshape(1, 6 * D)
    mod_ctx = m_all[:, 1, :].reshape(1, 6 * D)
    ffn_w_full = s_all[:, 2:5, :2 * DFF // NDEV].transpose(1, 0, 2).reshape(3, 2 * DFF)
    conv_w_full = s_all[:, 5:8, :CONV // NDEV].transpose(1, 0, 2).reshape(3, CONV)

    stage_w = {"in": ["w_in"], "mid": ["w_uq", "w_ukv", "w_attn_out", "w_conv_out", "w_o"], "ffn": ["w_up", "w_down"]}
    two_level = ("in", "mid")
    ag, tok = {}, m_all
    for st, nms in stage_w.items():
        ag[st] = exchange_start([W2[nm].astype(BF) for nm in nms], per_peer=False, dep=tok, name="ag_start_" + st,
                                peers=FIRST_HOP if st in two_level else ALL_PEERS)
        tok = ag[st]["token"]

    def get_w(stage, after):
        lands = exchange_wait(ag[stage], after, name="ag_wait_" + stage)
        if stage in two_level:
            lands = relay_wait(relay_start(lands, name="ag_relay_" + stage), name="ag_relay_wait_" + stage)
        g = dict(zip(stage_w[stage], lands))
        if stage == "in":
            return build_win(g["w_in"], name="build_win")
        if stage == "mid":
            wq2, wkv2 = build_wq_wkv(g["w_uq"], g["w_ukv"], name="build_wq_wkv")
            return (wq2, wkv2, unshard_cols(g["w_attn_out"], name="unshard_w_attn_out"),
                    unshard_cols(g["w_conv_out"], name="unshard_w_conv_out"), g["w_o"].reshape(D, D))
        return g["w_up"].reshape(2 * DFF, D), g["w_down"].reshape(DFF, D)

    stage_g = {"ffn": ["w_up", "w_down"], "mid": ["w_attn_out", "w_conv_out", "w_o"], "qkv": ["w_uq", "w_ukv"],
               "in": ["w_in"]}
    rs = {}

    def put_g(stage, g):
        if stage == "in":
            parts = [shard_win_grad(g["dwin"], g["dwin_c"], name="shard_win_grad")]
        elif stage == "mid":
            parts = [shard_cols(g["dwao"], name="shard_w_attn_out"), shard_cols(g["dwco"], name="shard_w_conv_out"),
                     g["dwo"].reshape(NDEV, D // NDEV, D)]
        elif stage == "qkv":
            parts = list(shard_wq_wkv_grad(g["dwq2"], g["dwkv2"], name="shard_wq_wkv_grad"))
        else:
            parts = [g["dwup"].reshape(NDEV, 2 * DFF // NDEV, D), g["dwdn"].reshape(NDEV, DFF // NDEV, D)]
        rs[stage] = exchange_start(parts, per_peer=True, name="rs_start_" + stage)
        return rs[stage]["token"]

    r = _local_step(x[0], ctx[0], loss_target[0], mod_lat, mod_ctx, W2["norm1_g"], W2["q_norm_g"], W2["kv_norm_g"],
                    W2["norm2_g"], W2["final_g"], conv_w_full, W2["conv_b"], ffn_w_full, W2["ffn_conv_b"], get_w, put_g,
                    ag["ffn"]["token"])

    G, DL, NM, NV = {}, {}, {}, {}

    def finish(stage, after):
        for nm, sl in zip(stage_g[stage], exchange_wait(rs[stage], after, name="rs_wait_" + stage)):
            wmv = (W3, M3, V3) if nm == "w_in" else (W2[nm], M2[nm], V2[nm])
            G[nm], DL[nm], NM[nm], NV[nm] = adamw_slots(wmv[0], sl, wmv[1], wmv[2], name="adamw_" + nm)
            after = DL[nm]
        return after

    after = r["dx"]
    for st in ("ffn", "mid", "qkv"):
        after = finish(st, after)

    a_buf, ssum = sync_small(r, [DL[nm] for st in ("ffn", "mid", "qkv") for nm in stage_g[st]], name="sync_small")
    loss = ssum[P_LOSS, 0]
    G["norm1_g"] = ssum[P_N1:P_N1 + 1]
    G["q_norm_g"] = ssum[P_QG:P_QG + 1, :QL]
    G["kv_norm_g"] = ssum[P_KVG:P_KVG + 1, :KVL]
    G["conv_b"] = ssum[P_CB:P_CB + 1, :CONV]
    G["norm2_g"] = ssum[P_N2:P_N2 + 1]
    G["ffn_conv_b"] = ssum[P_FB:P_FB + 2 * FROWS].reshape(1, 2, FROWS * D)[:, :, :DFF].reshape(1, 2 * DFF)
    G["final_g"] = ssum[P_FG:P_FG + 1]
    G["conv_w"] = lax.dynamic_slice(ssum[P_CW:P_CW + 3, :CONV], (0, me * (CONV // NDEV)), (3, CONV // NDEV))
    fw_full = ssum[P_FW:P_FW + 6 * FROWS].reshape(3, 2, FROWS * D)[:, :, :DFF].reshape(3, 2 * DFF)
    G["ffn_conv_w"] = lax.dynamic_slice(fw_full, (0, me * (2 * DFF // NDEV)), (3, 2 * DFF // NDEV))
    G["b_ada"] = (ssum[P_DML:P_DML + 6] + ssum[P_DMC:P_DMC + 6]).reshape(1, 6 * D)

    dml = lax.dynamic_slice(a_buf[:, P_DML:P_DML + 6, :].reshape(NDEV, 6 * D), (0, me * nsh), (NDEV, nsh))
    dmc = lax.dynamic_slice(ssum[P_DMC:P_DMC + 6].reshape(1, 6 * D), (0, me * nsh), (1, nsh))
    G["w_ada"], gcc = ada_bwd(s_all, dml, dmc, W2["w_ada"], W2["c_ctx"], name="ada_bwd")
    G["c_ctx"] = gcc[0:1]

    DL["w_ada"], NM["w_ada"], NV["w_ada"] = adamw(W2["w_ada"], G["w_ada"], M2["w_ada"], V2["w_ada"], name="adamw_w_ada")
    small = ["c_ctx", "b_ada", "norm1_g", "q_norm_g", "kv_norm_g", "conv_b", "norm2_g", "ffn_conv_b", "final_g", "conv_w",
             "ffn_conv_w"]
    ds, nms, nvs = adamw_many([W2[k] for k in small], [G[k] for k in small], [M2[k] for k in small],
                              [V2[k] for k in small], name="adamw_small")
    for k, nm in enumerate(small):
        DL[nm], NM[nm], NV[nm] = ds[k], nms[k], nvs[k]
    finish("in", ds[0])

    outs = [loss, r["dx"][None]]
    for grp in (G, DL, NM, NV):
        outs += [grp[nm].T[None] if nm in transposed else
                 jnp.transpose(grp[nm], (1, 2, 0)) if nm == "w_in" else grp[nm].reshape(W[nm].shape) for nm in names]
    return tuple(outs)
```

```python
import functools
import numpy as np
import jax
import jax.numpy as jnp
from jax import lax
from jax.experimental import pallas as pl
from jax.experimental.pallas import tpu as pltpu

F32 = jnp.float32
BF = jnp.bfloat16
MESH = pl.DeviceIdType.MESH

D = 1024
T = 2048
TC = 256
TKV = T + TC
GRID_W = 64
NH = 8
DN = 64
DR = 32
DV = 64
QL = 384
KVL = 256
CONV = 512
DFF = 2816
EPS = 1e-6
ROPE_THETA = 10000.0
SCALE = (DN + DR) ** -0.5
NDEV = 8
HP = 128

O_GA, O_GC, O_KV, O_Q, O_CV = 0, 1024, 2048, 2560, 3072
NIN = 4608
CVB = 256
N_IN = 4256
SH_IN = N_IN // NDEV

LR, B1, B2, AEPS, WD, STEP = 0.001, 0.9, 0.999, 1e-08, 0.01, 10


def _pick(n, target, mult=128):
    best = None
    for d in range(mult, min(n, target) + 1, mult):
        if n % d == 0:
            best = d
    return best if best is not None else n


def _swap_start(g):
    return 8 * (g ^ 1)


def mm(a, b, *, ta=False, tb=False, out_dtype=F32, name, tm=1024, tn=1024, tk=2048, M=None, N=None, K=None,
       a_off=(0, 0), b_off=(0, 0), a_stack=False, b_stack=False, o_stack=False, dep=None):
    def dims(arr, stack):
        return (arr.shape[1], 2 * arr.shape[2]) if stack else arr.shape

    ar, ac = dims(a, a_stack)
    br, bc = dims(b, b_stack)
    M = M or ((ac if ta else ar) - a_off[1 if ta else 0])
    K = K or ((ar if ta else ac) - a_off[0 if ta else 1])
    N = N or ((br if tb else bc) - b_off[0 if tb else 1])
    tm = _pick(M, tm, 128 if ta else 16)
    tn = _pick(N // 2 if (o_stack or (b_stack and not tb)) else N, tn, 128)
    tk = _pick(K // 2 if ((a_stack and not ta) or (b_stack and tb)) else K, tk, 128)
    nk = K // tk
    ca = 0 if ta else 1
    cb = 1 if tb else 0

    def body(a_ref, b_ref, *rest):
        o_ref, acc = rest[-2:]
        k = pl.program_id(2)
        part = lax.dot_general(a_ref[...].astype(BF), b_ref[...].astype(BF),
                               (((ca,), (cb,)), ((), ())), preferred_element_type=F32)
        if nk == 1:
            o_ref[...] = part.astype(o_ref.dtype)
        else:
            @pl.when(k == 0)
            def _():
                acc[...] = part

            @pl.when(k > 0)
            def _():
                acc[...] += part

            @pl.when(k == nk - 1)
            def _():
                o_ref[...] = acc[...].astype(o_ref.dtype)

    def spec(blk, rc, off, stack, ncols):
        assert off[0] % blk[0] == 0 and off[1] % blk[1] == 0, (name, blk, off)
        ro, co = off[0] // blk[0], off[1] // blk[1]
        if not stack:
            return pl.BlockSpec(blk, lambda i, j, k: (rc(i, j, k)[0] + ro, rc(i, j, k)[1] + co))
        nhb = ncols // 2 // blk[1]
        return pl.BlockSpec((None,) + blk,
                            lambda i, j, k: ((rc(i, j, k)[1] + co) // nhb, rc(i, j, k)[0] + ro, (rc(i, j, k)[1] + co) % nhb))

    a_spec = spec((tk, tm), lambda i, j, k: (k, i), a_off, a_stack, ac) if ta else \
        spec((tm, tk), lambda i, j, k: (i, k), a_off, a_stack, ac)
    b_spec = spec((tn, tk), lambda i, j, k: (j, k), b_off, b_stack, bc) if tb else \
        spec((tk, tn), lambda i, j, k: (k, j), b_off, b_stack, bc)
    o_spec = spec((tm, tn), lambda i, j, k: (i, j), (0, 0), o_stack, N)
    o_shape = (2, M, N // 2) if o_stack else (M, N)
    deps = [] if dep is None else [dep]
    return pl.pallas_call(
        body, name=name, grid=(M // tm, N // tn, nk),
        in_specs=[a_spec, b_spec] + [pl.BlockSpec(memory_space=pl.ANY)] * len(deps),
        out_specs=o_spec, out_shape=jax.ShapeDtypeStruct(o_shape, out_dtype),
        scratch_shapes=[pltpu.VMEM((tm, tn) if nk > 1 else (8, 128), F32)],
        compiler_params=pltpu.CompilerParams(dimension_semantics=("parallel", "parallel", "arbitrary")),
    )(a, b, *deps)


def _row(width):
    return pl.BlockSpec((1, width), lambda *_: (0, 0))


NLAT = T // TC


def normmod_cat(ctx, x, g, csc, csh, sc, sh, dep, *, name, tm=256):
    assert tm == TC

    def body(c_ref, x_ref, g_ref, csc_ref, csh_ref, sc_ref, sh_ref, dep_ref, h_ref):
        last = pl.program_id(0) == NLAT
        xv = jnp.where(last, c_ref[...], x_ref[...])
        scv = jnp.where(last, csc_ref[...], sc_ref[...])
        shv = jnp.where(last, csh_ref[...], sh_ref[...])
        r = lax.rsqrt(jnp.mean(xv * xv, axis=-1, keepdims=True) + EPS)
        h_ref[...] = ((xv * r * g_ref[...]) * (1.0 + scv) + shv).astype(BF)

    return pl.pallas_call(
        body, name=name, grid=(TKV // tm,),
        in_specs=[pl.BlockSpec((tm, D), lambda i: (0, 0)), pl.BlockSpec((tm, D), lambda i: (jnp.minimum(i, NLAT - 1), 0)),
                  _row(D), _row(D), _row(D), _row(D), _row(D), pl.BlockSpec(memory_space=pl.ANY)],
        out_specs=pl.BlockSpec((tm, D), lambda i: (i, 0)), out_shape=jax.ShapeDtypeStruct((TKV, D), BF),
        compiler_params=pltpu.CompilerParams(dimension_semantics=("parallel",)),
    )(ctx, x, g, csc, csh, sc, sh, dep)


def resid_normmod(x, a, gate, g, sc, sh, *, name, tm=512):
    R = x.shape[0]

    def body(x_ref, a_ref, gate_ref, g_ref, sc_ref, sh_ref, x1_ref, h_ref):
        xv = x_ref[...] + gate_ref[...] * a_ref[...]
        x1_ref[...] = xv
        r = lax.rsqrt(jnp.mean(xv * xv, axis=-1, keepdims=True) + EPS)
        h_ref[...] = ((xv * r * g_ref[...]) * (1.0 + sc_ref[...]) + sh_ref[...]).astype(BF)

    blk = pl.BlockSpec((tm, D), lambda i: (i, 0))
    return pl.pallas_call(
        body, name=name, grid=(R // tm,), in_specs=[blk, blk, _row(D), _row(D), _row(D), _row(D)],
        out_specs=[blk, blk],
        out_shape=[jax.ShapeDtypeStruct((R, D), F32), jax.ShapeDtypeStruct((R, D), BF)],
        compiler_params=pltpu.CompilerParams(dimension_semantics=("parallel",)),
    )(x, a, gate, g, sc, sh)


def kvprep(pc, p, kvg, wkv2, ck, sk, *, name, tm=256):
    assert tm == TC
    nb = TKV // tm
    kvcol = O_KV // 512

    def body(pc_ref, p_ref, g_ref, w_ref, ck_ref, sk_ref, k_ref, v_ref, ckv_ref):
        i = pl.program_id(0)
        t = jnp.where(i == NLAT, pc_ref[...], p_ref[...])
        pk = t[:, :KVL]
        r = lax.rsqrt(jnp.mean(pk * pk, axis=-1, keepdims=True) + EPS)
        ckv = (pk * r * g_ref[...]).astype(BF)
        ckv_ref[...] = ckv
        kv2 = jnp.dot(ckv, w_ref[...], preferred_element_type=F32)
        krr = t[:, KVL:KVL + HP] * ck_ref[...] + t[:, KVL + HP:KVL + 2 * HP] * sk_ref[...]
        k_ref[...] = (kv2[:, :NH * HP] + jnp.concatenate([krr] * NH, axis=1)).astype(BF)
        v_ref[...] = kv2[:, NH * HP:].astype(BF)

    return pl.pallas_call(
        body, name=name, grid=(nb,),
        in_specs=[pl.BlockSpec((tm, 512), lambda i: (0, 0)),
                  pl.BlockSpec((tm, 512), lambda i: (jnp.minimum(i, NLAT - 1), kvcol)),
                  _row(KVL), pl.BlockSpec((KVL, NH * HP + NH * DV), lambda i: (0, 0)),
                  pl.BlockSpec((tm, HP), lambda i: (i, 0)), pl.BlockSpec((tm, HP), lambda i: (i, 0))],
        out_specs=[pl.BlockSpec((tm, NH * HP), lambda i: (i, 0)), pl.BlockSpec((tm, NH * DV), lambda i: (i, 0)),
                   pl.BlockSpec((tm, KVL), lambda i: (i, 0))],
        out_shape=[jax.ShapeDtypeStruct((TKV, NH * HP), BF), jax.ShapeDtypeStruct((TKV, NH * DV), BF),
                   jax.ShapeDtypeStruct((TKV, KVL), BF)],
        compiler_params=pltpu.CompilerParams(dimension_semantics=("parallel",)),
    )(pc, p, kvg, wkv2, ck, sk)


def qprep(p, qg, wq2, cq_t, sq_t, *, name, tm=256):
    qcol = O_Q // 512

    def body(p_ref, g_ref, w_ref, c_ref, s_ref, q_ref, cq_ref):
        pq = p_ref[...]
        r = lax.rsqrt(jnp.sum(pq * pq, axis=-1, keepdims=True) * (1.0 / QL) + EPS)
        cq = (pq * r * g_ref[...]).astype(BF)
        cq_ref[...] = cq
        q2 = jnp.dot(cq, w_ref[...], preferred_element_type=F32)
        cc = jnp.concatenate([c_ref[...]] * NH, axis=1)
        ss = jnp.concatenate([s_ref[...]] * NH, axis=1)
        q_ref[...] = (q2[:, :NH * HP] * cc + q2[:, NH * HP:] * ss).astype(BF)

    return pl.pallas_call(
        body, name=name, grid=(T // tm,),
        in_specs=[pl.BlockSpec((tm, 512), lambda i: (i, qcol)), _row(512),
                  pl.BlockSpec((512, 2 * NH * HP), lambda i: (0, 0)),
                  pl.BlockSpec((tm, HP), lambda i: (i, 0)), pl.BlockSpec((tm, HP), lambda i: (i, 0))],
        out_specs=[pl.BlockSpec((tm, NH * HP), lambda i: (i, 0)), pl.BlockSpec((tm, 512), lambda i: (i, 0))],
        out_shape=[jax.ShapeDtypeStruct((T, NH * HP), BF), jax.ShapeDtypeStruct((T, 512), BF)],
        compiler_params=pltpu.CompilerParams(dimension_semantics=("parallel",)),
    )(p, qg, wq2, cq_t, sq_t)


def _head_mask(h):
    lanes = lax.broadcasted_iota(jnp.int32, (1, 2 * DV), 1)
    return (lanes // DV) == (h % 2)


LOG2E = 1.4426950408889634


def _scores_pass(q, k_ref, s_scr, kc):
    m = None
    for c in range(TKV // kc):
        s = lax.dot_general(q, k_ref[c * kc:(c + 1) * kc, :], (((1,), (1,)), ((), ())),
                            preferred_element_type=F32) * (SCALE * LOG2E)
        s_scr[:, c * kc:(c + 1) * kc] = s
        mc = jnp.max(s, axis=-1, keepdims=True)
        m = mc if m is None else jnp.maximum(m, mc)
    return m


def attn_fwd(q, k, v, *, name, tq=512, kc=1152):
    def body(q_ref, k_ref, v_ref, o_ref, s_scr):
        h = pl.program_id(1)
        m = _scores_pass(q_ref[...], k_ref, s_scr, kc)
        l = jnp.zeros((tq, 1), F32)
        acc = jnp.zeros((tq, 2 * DV), F32)
        for c in range(TKV // kc):
            e = jnp.exp2(s_scr[:, c * kc:(c + 1) * kc] - m)
            l = l + jnp.sum(e, axis=-1, keepdims=True)
            acc = acc + jnp.dot(e.astype(BF), v_ref[c * kc:(c + 1) * kc, :], preferred_element_type=F32)
        o2 = jnp.where(_head_mask(h), acc * (1.0 / l), 0.0).astype(BF)

        @pl.when(h % 2 == 0)
        def _():
            o_ref[...] = o2

        @pl.when(h % 2 == 1)
        def _():
            o_ref[...] = o_ref[...] + o2

    return pl.pallas_call(
        body, name=name, grid=(T // tq, NH),
        in_specs=[pl.BlockSpec((tq, HP), lambda i, h: (i, h)), pl.BlockSpec((TKV, HP), lambda i, h: (0, h)),
                  pl.BlockSpec((TKV, 2 * DV), lambda i, h: (0, h // 2))],
        out_specs=pl.BlockSpec((tq, 2 * DV), lambda i, h: (i, h // 2)),
        out_shape=jax.ShapeDtypeStruct((T, NH * DV), BF),
        scratch_shapes=[pltpu.VMEM((tq, TKV), F32)],
        compiler_params=pltpu.CompilerParams(dimension_semantics=("parallel", "arbitrary")),
    )(q, k, v)


def _shift_dn(x):
    n = x.shape[0]
    rows = lax.broadcasted_iota(jnp.int32, (n, 1), 0)
    return jnp.where(rows == 0, 0.0, pltpu.roll(x, 1, axis=0))


def _shift_up(x):
    n = x.shape[0]
    rows = lax.broadcasted_iota(jnp.int32, (n, 1), 0)
    return jnp.where(rows == n - 1, 0.0, pltpu.roll(x, n - 1, axis=0))


def _conv(x, w_ref, b_ref):
    return b_ref[...] + _shift_dn(x) * w_ref[0:1, :] + x * w_ref[1:2, :] + _shift_up(x) * w_ref[2:3, :]


def _conv_t(dy, w_ref):
    return _shift_up(dy) * w_ref[0:1, :] + dy * w_ref[1:2, :] + _shift_dn(dy) * w_ref[2:3, :]


def _conv_wgrad(dw_ref, dy, x):
    dw_ref[0:1, :] = jnp.sum(dy * _shift_dn(x), axis=0, keepdims=True)
    dw_ref[1:2, :] = jnp.sum(dy * x, axis=0, keepdims=True)
    dw_ref[2:3, :] = jnp.sum(dy * _shift_up(x), axis=0, keepdims=True)


def convz(p, cw, cb, *, name):
    o0 = O_CV // (3 * CVB)

    def body(p_ref, w_ref, bias_ref, z_ref):
        xv, bv, cv = p_ref[:, 0:CVB], p_ref[:, CVB:2 * CVB], p_ref[:, 2 * CVB:3 * CVB]
        z_ref[...] = (bv * _conv(cv * xv, w_ref, bias_ref)).astype(BF)

    return pl.pallas_call(
        body, name=name, grid=(CONV // CVB,),
        in_specs=[pl.BlockSpec((T, 3 * CVB), lambda j: (0, o0 + j)), pl.BlockSpec((3, CVB), lambda j: (0, j)),
                  pl.BlockSpec((1, CVB), lambda j: (0, j))],
        out_specs=pl.BlockSpec((T, CVB), lambda j: (0, j)),
        out_shape=jax.ShapeDtypeStruct((T, CONV), BF),
        compiler_params=pltpu.CompilerParams(dimension_semantics=("parallel",)),
    )(p, cw, cb)


def gate_merge(p, ya, yc, *, name, tm=512):
    def body(ga_ref, gc_ref, ya_ref, yc_ref, o_ref):
        o_ref[...] = (jax.nn.sigmoid(ga_ref[...]) * ya_ref[...] + jax.nn.sigmoid(gc_ref[...]) * yc_ref[...]).astype(BF)

    blk = pl.BlockSpec((tm, D), lambda i: (i, 0))
    return pl.pallas_call(
        body, name=name, grid=(T // tm,),
        in_specs=[pl.BlockSpec((tm, D), lambda i: (i, O_GA // D)), pl.BlockSpec((tm, D), lambda i: (i, O_GC // D)), blk, blk],
        out_specs=blk, out_shape=jax.ShapeDtypeStruct((T, D), BF),
        compiler_params=pltpu.CompilerParams(dimension_semantics=("parallel",)),
    )(p, p, ya, yc)


CONV_HALO = 8
CONV_ROWS = 256


def _row_chunks(n, chunk, carry):
    carry = chunk(0, True, False, carry)
    carry = lax.fori_loop(1, n // CONV_ROWS - 1, lambda c, a: chunk(c * CONV_ROWS, False, False, a), carry)
    return chunk(n - CONV_ROWS, False, True, carry)


def _ext_rows(ref, r0, first, last):
    n, w = ref.shape
    zero = jnp.zeros((CONV_HALO, w), ref.dtype)
    if first:
        return jnp.concatenate([zero, ref[0:CONV_ROWS + CONV_HALO, :]], axis=0)
    if last:
        return jnp.concatenate([ref[n - CONV_ROWS - CONV_HALO:n, :], zero], axis=0)
    return ref[pl.ds(pl.multiple_of(r0 - CONV_HALO, 8), CONV_ROWS + 2 * CONV_HALO), :]


def _center_rows(r0, first, last):
    return slice(r0, r0 + CONV_ROWS) if (first or last) else pl.ds(pl.multiple_of(r0, 8), CONV_ROWS)


def _roll_dn(x):
    return pltpu.roll(x, 1, axis=0)


def _roll_up(x):
    return pltpu.roll(x, x.shape[0] - 1, axis=0)


_CTR = slice(CONV_HALO, CONV_HALO + CONV_ROWS)


def ffn_act(u0, cw, cb, *, name, tc=256):
    nb = DFF // tc

    def body(u_ref, wg_ref, wv_ref, bg_ref, bv_ref, f_ref):
        wg = [wg_ref[k:k + 1, :] for k in range(3)]
        wv = [wv_ref[k:k + 1, :] for k in range(3)]
        bg, bv = bg_ref[...], bv_ref[...]

        def chunk(r0, first, last, carry):
            xg, xv = _ext_rows(u_ref.at[0], r0, first, last), _ext_rows(u_ref.at[1], r0, first, last)
            ug = bg + _roll_dn(xg) * wg[0] + xg * wg[1] + _roll_up(xg) * wg[2]
            uv = bv + _roll_dn(xv) * wv[0] + xv * wv[1] + _roll_up(xv) * wv[2]
            f_ref[_center_rows(r0, first, last), :] = (ug * jax.nn.sigmoid(ug) * uv)[_CTR].astype(BF)
            return carry

        _row_chunks(T, chunk, 0)

    return pl.pallas_call(
        body, name=name, grid=(nb,),
        in_specs=[pl.BlockSpec((2, T, tc), lambda j: (0, 0, j)),
                  pl.BlockSpec((3, tc), lambda j: (0, j)), pl.BlockSpec((3, tc), lambda j: (0, nb + j)),
                  pl.BlockSpec((1, tc), lambda j: (0, j)), pl.BlockSpec((1, tc), lambda j: (0, nb + j))],
        out_specs=pl.BlockSpec((T, tc), lambda j: (0, j)),
        out_shape=jax.ShapeDtypeStruct((T, DFF), BF),
        compiler_params=pltpu.CompilerParams(dimension_semantics=("parallel",)),
    )(u0, cw, cw, cb, cb)


def final_loss(x1, d, g2, fg, tgt, *, name, tm=512):
    def body(x1_ref, d_ref, g2_ref, fg_ref, t_ref, dx_ref, dd_ref, dfg_ref, loss_ref):
        i = pl.program_id(0)
        xv = x1_ref[...] + g2_ref[...] * d_ref[...]
        r = lax.rsqrt(jnp.mean(xv * xv, axis=-1, keepdims=True) + EPS)
        xh = xv * r
        diff = xh * fg_ref[...] - t_ref[...]
        part = 0.5 * jnp.sum(jnp.mean(diff * diff, axis=-1, keepdims=True), axis=0, keepdims=True)
        dy = diff * (1.0 / D)
        a = dy * fg_ref[...]
        dx = r * (a - xh * jnp.mean(a * xh, axis=-1, keepdims=True))
        dx_ref[...] = dx
        dd_ref[...] = (dx * g2_ref[...]).astype(BF)
        dfg = jnp.sum(dy * xh, axis=0, keepdims=True)

        @pl.when(i == 0)
        def _():
            dfg_ref[...] = dfg
            loss_ref[...] = jnp.broadcast_to(part, (1, 128))

        @pl.when(i > 0)
        def _():
            dfg_ref[...] += dfg
            loss_ref[...] += jnp.broadcast_to(part, (1, 128))

    blk = pl.BlockSpec((tm, D), lambda i: (i, 0))
    return pl.pallas_call(
        body, name=name, grid=(T // tm,), in_specs=[blk, blk, _row(D), _row(D), blk],
        out_specs=[blk, blk, _row(D), _row(128)],
        out_shape=[jax.ShapeDtypeStruct((T, D), F32), jax.ShapeDtypeStruct((T, D), BF),
                   jax.ShapeDtypeStruct((1, D), F32), jax.ShapeDtypeStruct((1, 128), F32)],
        compiler_params=pltpu.CompilerParams(dimension_semantics=("arbitrary",)),
    )(x1, d, g2, fg, tgt)


def normmod_bwd(x, dh, g, sc, dres, gsrc, gate, *, name, tm=512):
    R = x.shape[0]
    tm = min(tm, R)
    has_res = dres is not None
    fused = isinstance(dh, dict)
    if fused:
        tb, a_stack, tk = dh.get("tb", False), dh.get("a_stack", False), dh["tk"]
        K = 2 * dh["a"].shape[2] if a_stack else dh["a"].shape[1]
        nk = K // tk
        deps = [] if dh.get("dep") is None else [dh["dep"]]
        n_dh = 2 + len(deps)
    else:
        nk, n_dh = 1, 1

    def elementwise(i, dhv, x_ref, g_ref, sc_ref, res_refs, out_refs):
        xv = x_ref[...]
        r = lax.rsqrt(jnp.mean(xv * xv, axis=-1, keepdims=True) + EPS)
        xh = xv * r
        n = xh * g_ref[...]
        dn = dhv * (1.0 + sc_ref[...])
        a = dn * g_ref[...]
        rows = [jnp.sum(dhv, axis=0, keepdims=True), jnp.sum(dhv * n, axis=0, keepdims=True),
                jnp.sum(dn * xh, axis=0, keepdims=True)]
        if has_res:
            dres_ref, gsrc_ref, gate_ref = res_refs
            dx_ref, dxg_ref, st_ref = out_refs
            dr = dres_ref[...]
            dx = dr + r * (a - xh * jnp.mean(a * xh, axis=-1, keepdims=True))
            dx_ref[...] = dx
            dxg_ref[...] = (dx * gate_ref[...]).astype(BF)
            rows.append(jnp.sum(dr * gsrc_ref[...], axis=0, keepdims=True))
        else:
            st_ref, = out_refs
            rows.append(jnp.zeros((1, D), F32))

        @pl.when(i == 0)
        def _():
            for k, row in enumerate(rows):
                st_ref[k:k + 1, :] = row

        @pl.when(i > 0)
        def _():
            for k, row in enumerate(rows):
                st_ref[k:k + 1, :] += row

    def body(*refs):
        x_ref, dh_refs, g_ref, sc_ref = refs[0], refs[1:1 + n_dh], refs[1 + n_dh], refs[2 + n_dh]
        rest = refs[3 + n_dh:]
        res_refs, rest = (rest[:3], rest[3:]) if has_res else ((), rest)
        out_refs = rest[:3] if has_res else rest[:1]
        i = pl.program_id(0)
        if not fused:
            elementwise(i, dh_refs[0][...], x_ref, g_ref, sc_ref, res_refs, out_refs)
            return
        acc = rest[-1]
        k = pl.program_id(1)
        part = lax.dot_general(dh_refs[0][...].astype(BF), dh_refs[1][...].astype(BF),
                               (((1,), (1 if tb else 0,)), ((), ())), preferred_element_type=F32)

        @pl.when(k == 0)
        def _():
            acc[...] = part

        @pl.when(k > 0)
        def _():
            acc[...] += part

        @pl.when(k == nk - 1)
        def _():
            elementwise(i, acc[...], x_ref, g_ref, sc_ref, res_refs, out_refs)

    rowb = lambda w: pl.BlockSpec((1, w), lambda i, *k: (0, 0))
    blk = pl.BlockSpec((tm, D), lambda i, *k: (i, 0))
    st_spec = pl.BlockSpec((4, D), lambda i, *k: (0, 0))
    st_shape = jax.ShapeDtypeStruct((4, D), F32)
    if fused:
        if a_stack:
            nhb = K // 2 // tk
            a_spec = pl.BlockSpec((None, tm, tk), lambda i, k: (k // nhb, i, k % nhb))
        else:
            a_spec = pl.BlockSpec((tm, tk), lambda i, k: (i, k))
        b_spec = pl.BlockSpec((D, tk), lambda i, k: (0, k)) if tb else pl.BlockSpec((tk, D), lambda i, k: (k, 0))
        dh_specs = [a_spec, b_spec] + [pl.BlockSpec(memory_space=pl.ANY)] * len(deps)
        dh_args = [dh["a"], dh["b"]] + deps
        grid, sem = (R // tm, nk), ("arbitrary", "arbitrary")
        scratch = [pltpu.VMEM((tm, D), F32)]
    else:
        dh_specs, dh_args, grid, sem, scratch = [blk], [dh], (R // tm,), ("arbitrary",), []
    cp = pltpu.CompilerParams(dimension_semantics=sem)
    if has_res:
        return pl.pallas_call(
            body, name=name, grid=grid, in_specs=[blk] + dh_specs + [rowb(D), rowb(D), blk, blk, rowb(D)],
            out_specs=[blk, blk, st_spec], scratch_shapes=scratch,
            out_shape=[jax.ShapeDtypeStruct((R, D), F32), jax.ShapeDtypeStruct((R, D), BF), st_shape],
            compiler_params=cp,
        )(x, *dh_args, g, sc, dres, gsrc, gate)
    return pl.pallas_call(
        body, name=name, grid=grid, in_specs=[blk] + dh_specs + [rowb(D), rowb(D)],
        out_specs=st_spec, out_shape=st_shape, scratch_shapes=scratch, compiler_params=cp,
    )(x, *dh_args, g, sc)


def ffn_act_bwd(u0, df, cw, cb, *, name, tc=128):
    nb = DFF // tc

    def body(u_ref, df_ref, wg_ref, wv_ref, bg_ref, bv_ref, du_ref, dw_ref, db_ref):
        wg = [wg_ref[k:k + 1, :] for k in range(3)]
        wv = [wv_ref[k:k + 1, :] for k in range(3)]
        bg, bv = bg_ref[...], bv_ref[...]

        def chunk(r0, first, last, acc):
            xg, xv = _ext_rows(u_ref.at[0], r0, first, last), _ext_rows(u_ref.at[1], r0, first, last)
            dfe = _ext_rows(df_ref, r0, first, last)
            xg_d, xg_u, xv_d, xv_u = _roll_dn(xg), _roll_up(xg), _roll_dn(xv), _roll_up(xv)
            ug = bg + xg_d * wg[0] + xg * wg[1] + xg_u * wg[2]
            uv = bv + xv_d * wv[0] + xv * wv[1] + xv_u * wv[2]
            sig = jax.nn.sigmoid(ug)
            dug = dfe * uv * (sig * (1.0 + ug * (1.0 - sig)))
            duv = dfe * (ug * sig)
            rows = _center_rows(r0, first, last)
            du_ref[0, rows, :] = (_roll_up(dug) * wg[0] + dug * wg[1] + _roll_dn(dug) * wg[2])[_CTR].astype(BF)
            du_ref[1, rows, :] = (_roll_up(duv) * wv[0] + duv * wv[1] + _roll_dn(duv) * wv[2])[_CTR].astype(BF)
            terms = [dug * xg_d, dug * xg, dug * xg_u, dug, duv * xv_d, duv * xv, duv * xv_u, duv]
            return tuple(a + jnp.sum(t[_CTR], axis=0, keepdims=True) for a, t in zip(acc, terms))

        acc = _row_chunks(T, chunk, tuple(jnp.zeros((1, tc), F32) for _ in range(8)))
        for k in range(3):
            dw_ref[0, k:k + 1, :] = acc[k]
            dw_ref[1, k:k + 1, :] = acc[4 + k]
        db_ref[0] = acc[3]
        db_ref[1] = acc[7]

    lo = lambda r: pl.BlockSpec((r, tc), lambda j: (0, j))
    hi = lambda r: pl.BlockSpec((r, tc), lambda j: (0, nb + j))
    st = lambda r: pl.BlockSpec((2, r, tc), lambda j: (0, 0, j))
    return pl.pallas_call(
        body, name=name, grid=(nb,),
        in_specs=[st(T), lo(T), lo(3), hi(3), lo(1), hi(1)],
        out_specs=[st(T), st(3), st(1)],
        out_shape=[jax.ShapeDtypeStruct((2, T, DFF), BF), jax.ShapeDtypeStruct((2, 3, DFF), F32),
                   jax.ShapeDtypeStruct((2, 1, DFF), F32)],
        compiler_params=pltpu.CompilerParams(dimension_semantics=("parallel",)),
    )(u0, df, cw, cw, cb, cb)


def gate_merge_bwd(p, ya, yc, dm, *, name, tm=512):
    def body(ga_ref, gc_ref, ya_ref, yc_ref, dm_ref, dya_ref, dyc_ref, dp_ref):
        sa, sc_ = jax.nn.sigmoid(ga_ref[...]), jax.nn.sigmoid(gc_ref[...])
        dmv = dm_ref[...]
        dya_ref[...] = (dmv * sa).astype(BF)
        dyc_ref[...] = (dmv * sc_).astype(BF)
        dp_ref[:, 0:D] = (dmv * ya_ref[...] * (sa * (1.0 - sa))).astype(BF)
        dp_ref[:, D:2 * D] = (dmv * yc_ref[...] * (sc_ * (1.0 - sc_))).astype(BF)

    blk = pl.BlockSpec((tm, D), lambda i: (i, 0))
    sh = jax.ShapeDtypeStruct((T, D), BF)
    return pl.pallas_call(
        body, name=name, grid=(T // tm,),
        in_specs=[pl.BlockSpec((tm, D), lambda i: (i, O_GA // D)), pl.BlockSpec((tm, D), lambda i: (i, O_GC // D)), blk, blk, blk],
        out_specs=[blk, blk, pl.BlockSpec((tm, 2 * D), lambda i: (i, 0))],
        out_shape=[sh, sh, jax.ShapeDtypeStruct((T, NIN), BF)],
        compiler_params=pltpu.CompilerParams(dimension_semantics=("parallel",)),
    )(p, p, ya, yc, dm)


def convz_bwd(p, dz, cw, cb, dp, *, name):
    o0 = O_CV // (3 * CVB)

    def body(p_ref, dz_ref, w_ref, bias_ref, dp_in, dp_ref, dw_ref, dbias_ref):
        xv, bv, cv = p_ref[:, 0:CVB], p_ref[:, CVB:2 * CVB], p_ref[:, 2 * CVB:3 * CVB]
        ci = cv * xv
        dwc = _conv(ci, w_ref, bias_ref)
        dzv = dz_ref[...]
        ddw = dzv * bv
        dci = _conv_t(ddw, w_ref)
        dp_ref[:, 0:CVB] = (dci * cv).astype(BF)
        dp_ref[:, CVB:2 * CVB] = (dzv * dwc).astype(BF)
        dp_ref[:, 2 * CVB:3 * CVB] = (dci * xv).astype(BF)
        _conv_wgrad(dw_ref, ddw, ci)
        dbias_ref[...] = jnp.sum(ddw, axis=0, keepdims=True)

    own = lambda r: pl.BlockSpec((r, CVB), lambda j: (0, j))
    return pl.pallas_call(
        body, name=name, grid=(CONV // CVB,),
        in_specs=[pl.BlockSpec((T, 3 * CVB), lambda j: (0, o0 + j)), own(T), own(3), own(1),
                  pl.BlockSpec(memory_space=pl.ANY)],
        out_specs=[pl.BlockSpec((T, 3 * CVB), lambda j: (0, o0 + j)), own(3), own(1)],
        out_shape=[jax.ShapeDtypeStruct((T, NIN), BF), jax.ShapeDtypeStruct((3, CONV), F32),
                   jax.ShapeDtypeStruct((1, CONV), F32)],
        input_output_aliases={4: 0},
        compiler_params=pltpu.CompilerParams(dimension_semantics=("parallel",)),
    )(p, dz, cw, cb, dp)


def attn_bwd(q, k, v, do, *, name, tq=1024, kc=768):
    NKC, KC = TKV // kc, kc

    def body(q_ref, k_ref, v_ref, do_ref, dq_ref, dk_ref, dv_ref, s_scr, dp_scr):
        h, i = pl.program_id(0), pl.program_id(1)

        @pl.when(i == 0)
        def _():
            dk_ref[...] = jnp.zeros_like(dk_ref)

        @pl.when((i == 0) & (h % 2 == 0))
        def _():
            dv_ref[...] = jnp.zeros_like(dv_ref)

        qv = q_ref[...]
        dom = jnp.where(_head_mask(h), do_ref[...], jnp.zeros_like(do_ref[...]))
        m = _scores_pass(qv, k_ref, s_scr, kc)
        l = jnp.zeros((tq, 1), F32)
        dsum = jnp.zeros((tq, 1), F32)
        for c in range(NKC):
            cols = slice(c * KC, (c + 1) * KC)
            e = jnp.exp2(s_scr[:, cols] - m)
            s_scr[:, cols] = e
            dp = lax.dot_general(dom, v_ref[cols, :], (((1,), (1,)), ((), ())), preferred_element_type=F32)
            dp_scr[:, cols] = dp
            l = l + jnp.sum(e, axis=-1, keepdims=True)
            dsum = dsum + jnp.sum(e * dp, axis=-1, keepdims=True)
        inv = 1.0 / l
        delta = dsum * inv
        dos = (dom.astype(F32) * inv).astype(BF)
        dq = jnp.zeros((tq, HP), F32)
        for c in range(NKC):
            cols = slice(c * KC, (c + 1) * KC)
            e = s_scr[:, cols]
            ds = (e * (dp_scr[:, cols] - delta) * (inv * SCALE)).astype(BF)
            dq = dq + jnp.dot(ds, k_ref[cols, :], preferred_element_type=F32)
            dk_ref[cols, :] += lax.dot_general(ds, qv, (((0,), (0,)), ((), ())), preferred_element_type=F32)
            dv_ref[cols, :] += lax.dot_general(e.astype(BF), dos, (((0,), (0,)), ((), ())), preferred_element_type=F32)
        dq_ref[...] = dq

    return pl.pallas_call(
        body, name=name, grid=(NH, T // tq),
        in_specs=[pl.BlockSpec((tq, HP), lambda h, i: (i, h)), pl.BlockSpec((TKV, HP), lambda h, i: (0, h)),
                  pl.BlockSpec((TKV, 2 * DV), lambda h, i: (0, h // 2)), pl.BlockSpec((tq, 2 * DV), lambda h, i: (i, h // 2))],
        out_specs=[pl.BlockSpec((tq, HP), lambda h, i: (i, h)), pl.BlockSpec((TKV, HP), lambda h, i: (0, h)),
                   pl.BlockSpec((TKV, 2 * DV), lambda h, i: (0, h // 2))],
        out_shape=[jax.ShapeDtypeStruct((T, NH * HP), F32), jax.ShapeDtypeStruct((TKV, NH * HP), F32),
                   jax.ShapeDtypeStruct((TKV, NH * DV), F32)],
        scratch_shapes=[pltpu.VMEM((tq, TKV), F32), pltpu.VMEM((tq, TKV), F32)],
        compiler_params=pltpu.CompilerParams(dimension_semantics=("arbitrary", "arbitrary")),
    )(q, k, v, do)


def qprep_bwd(p, dq, qg, wq2, cq_t, sq_t, dp, *, name, tm=256):
    qcol = O_Q // 512

    def body(p_ref, dq_ref, g_ref, w_ref, c_ref, s_ref, dp_in, dp_ref, dq2_ref, dg_ref):
        i = pl.program_id(0)
        dqv = dq_ref[...]
        cc = jnp.concatenate([c_ref[...]] * NH, axis=1)
        ss = jnp.concatenate([s_ref[...]] * NH, axis=1)
        dq2 = jnp.concatenate([dqv * cc, dqv * ss], axis=1).astype(BF)
        dq2_ref[...] = dq2
        dcq = lax.dot_general(dq2, w_ref[...], (((1,), (1,)), ((), ())), preferred_element_type=F32)
        pq = p_ref[...]
        r = lax.rsqrt(jnp.sum(pq * pq, axis=-1, keepdims=True) * (1.0 / QL) + EPS)
        xh = pq * r
        a = dcq * g_ref[...]
        dp_ref[...] = (r * (a - xh * (jnp.sum(a * xh, axis=-1, keepdims=True) * (1.0 / QL)))).astype(BF)
        dg = jnp.sum(dcq * xh, axis=0, keepdims=True)

        @pl.when(i == 0)
        def _():
            dg_ref[...] = dg

        @pl.when(i > 0)
        def _():
            dg_ref[...] += dg

    return pl.pallas_call(
        body, name=name, grid=(T // tm,),
        in_specs=[pl.BlockSpec((tm, 512), lambda i: (i, qcol)), pl.BlockSpec((tm, NH * HP), lambda i: (i, 0)), _row(512),
                  pl.BlockSpec((512, 2 * NH * HP), lambda i: (0, 0)),
                  pl.BlockSpec((tm, HP), lambda i: (i, 0)), pl.BlockSpec((tm, HP), lambda i: (i, 0)),
                  pl.BlockSpec(memory_space=pl.ANY)],
        out_specs=[pl.BlockSpec((tm, 512), lambda i: (i, qcol)), pl.BlockSpec((tm, 2 * NH * HP), lambda i: (i, 0)), _row(512)],
        out_shape=[jax.ShapeDtypeStruct((T, NIN), BF), jax.ShapeDtypeStruct((T, 2 * NH * HP), BF),
                   jax.ShapeDtypeStruct((1, 512), F32)],
        input_output_aliases={6: 0},
        compiler_params=pltpu.CompilerParams(dimension_semantics=("arbitrary",)),
    )(p, dq, qg, wq2, cq_t, sq_t, dp)


def kvprep_bwd(pc, p, dk, dv, kvg, wkv2, ck, sk, dp, *, name, tm=256):
    assert tm == TC
    nb = TKV // tm
    kvcol = O_KV // 512

    def body(pc_ref, p_ref, dk_ref, dv_ref, g_ref, w_ref, ck_ref, sk_ref, dp_in, dp_ref, dpc_ref, dkv2_ref, dg_ref):
        i = pl.program_id(0)
        t = jnp.where(i == NLAT, pc_ref[...], p_ref[...])
        pk = t[:, :KVL]
        r = lax.rsqrt(jnp.mean(pk * pk, axis=-1, keepdims=True) + EPS)
        xh = pk * r
        dkv = dk_ref[...]
        dkv2 = jnp.concatenate([dkv, dv_ref[...]], axis=1).astype(BF)
        dkv2_ref[...] = dkv2
        dckv = lax.dot_general(dkv2, w_ref[...], (((1,), (1,)), ((), ())), preferred_element_type=F32)
        a = dckv * g_ref[...]
        dpk = r * (a - xh * jnp.mean(a * xh, axis=-1, keepdims=True))
        dkr = dkv[:, 0:HP]
        for hh in range(1, NH):
            dkr = dkr + dkv[:, hh * HP:(hh + 1) * HP]
        res = jnp.concatenate([dpk, dkr * ck_ref[...], dkr * sk_ref[...]], axis=1).astype(BF)
        dg = jnp.sum(dckv * xh, axis=0, keepdims=True)

        @pl.when(i == 0)
        def _():
            dg_ref[...] = dg

        @pl.when(i > 0)
        def _():
            dg_ref[...] += dg

        @pl.when(i < NLAT)
        def _():
            dp_ref[...] = res

        @pl.when(i == NLAT)
        def _():
            dpc_ref[...] = res

    rb = lambda w: pl.BlockSpec((tm, w), lambda i: (i, 0))
    return pl.pallas_call(
        body, name=name, grid=(nb,),
        in_specs=[pl.BlockSpec((tm, 512), lambda i: (0, 0)),
                  pl.BlockSpec((tm, 512), lambda i: (jnp.minimum(i, NLAT - 1), kvcol)),
                  rb(NH * HP), rb(NH * DV), _row(KVL), pl.BlockSpec((KVL, NH * HP + NH * DV), lambda i: (0, 0)),
                  rb(HP), rb(HP), pl.BlockSpec(memory_space=pl.ANY)],
        out_specs=[pl.BlockSpec((tm, 512), lambda i: (jnp.minimum(i, NLAT - 1), kvcol)),
                   pl.BlockSpec((tm, 512), lambda i: (0, 0)), rb(NH * HP + NH * DV), _row(KVL)],
        out_shape=[jax.ShapeDtypeStruct((T, NIN), BF), jax.ShapeDtypeStruct((TC, 512), BF),
                   jax.ShapeDtypeStruct((TKV, NH * HP + NH * DV), BF), jax.ShapeDtypeStruct((1, KVL), F32)],
        input_output_aliases={8: 0},
        compiler_params=pltpu.CompilerParams(dimension_semantics=("arbitrary",)),
    )(pc, p, dk, dv, kvg, wkv2, ck, sk, dp)


def _pieces(src, width, n):
    out, c = [], src
    while c < src + width:
        k = c // n
        w = min(src + width, (k + 1) * n) - c
        out.append((k, c - k * n, c - src, w))
        c += w
    return out


def _win_moves():
    mv = [(2208, 1024, O_GA), (3232, 1024, O_GC), (0, KVL, O_KV), (256, DR, O_KV + KVL + DN), (288, QL, O_Q)]
    mv += [(256 + _swap_start(g), 8, O_KV + KVL + HP + DN + 8 * g) for g in range(4)]
    for j in range(CONV // CVB):
        base = O_CV + 3 * CVB * j
        mv += [(672 + CVB * j, CVB, base), (1184 + CVB * j, CVB, base + CVB), (1696 + CVB * j, CVB, base + 2 * CVB)]
    return mv


_WIN_ZERO = [(O_KV + KVL, DN), (O_KV + KVL + DN + DR, HP - DN - DR), (O_KV + KVL + HP, DN),
             (O_KV + KVL + HP + DN + DR, HP - DN - DR), (O_Q + QL, 512 - QL)]


def build_win(g, *, name, tm=256):
    def body(g_ref, o_ref):
        for src, w, dst in _win_moves():
            for k, a, off, pw in _pieces(src, w, SH_IN):
                o_ref[:, dst + off:dst + off + pw] = g_ref[k, :, a:a + pw]
        for c0, w in _WIN_ZERO:
            o_ref[:, c0:c0 + w] = jnp.zeros((tm, w), o_ref.dtype)

    return pl.pallas_call(
        body, name=name, grid=(D // tm,), in_specs=[pl.BlockSpec((NDEV, tm, SH_IN), lambda i: (0, i, 0))],
        out_specs=pl.BlockSpec((tm, NIN), lambda i: (i, 0)), out_shape=jax.ShapeDtypeStruct((D, NIN), g.dtype),
        compiler_params=pltpu.CompilerParams(dimension_semantics=("parallel",)),
    )(g)


def shard_win_grad(dwt, dwct, *, name, tc=256):
    def body(dw_ref, dwc_ref, o_ref, kvs):
        kvs[...] = dw_ref[O_KV:O_KV + 512, :] + dwc_ref[...]

        def src(row, w):
            if O_KV <= row < O_KV + 512:
                return kvs[row - O_KV:row - O_KV + w, :]
            return dw_ref[row:row + w, :]

        for s, w, dst in _win_moves():
            if w == 8 or s == 256:
                continue
            for k, a, off, pw in _pieces(s, w, SH_IN):
                o_ref[k, a:a + pw, :] = src(dst + off, pw).astype(o_ref.dtype)
        for g in range(4):
            val = src(O_KV + KVL + DN + 8 * g, 8) + src(O_KV + KVL + HP + DN + _swap_start(g), 8)
            o_ref[0, 256 + 8 * g:256 + 8 * g + 8, :] = val.astype(o_ref.dtype)

    return pl.pallas_call(
        body, name=name, grid=(D // tc,),
        in_specs=[pl.BlockSpec((NIN, tc), lambda j: (0, j)), pl.BlockSpec((512, tc), lambda j: (0, j))],
        out_specs=pl.BlockSpec((NDEV, SH_IN, tc), lambda j: (0, 0, j)),
        out_shape=jax.ShapeDtypeStruct((NDEV, SH_IN, D), BF),
        scratch_shapes=[pltpu.VMEM((512, tc), F32)],
        compiler_params=pltpu.CompilerParams(dimension_semantics=("parallel",)),
    )(dwt, dwct)


def build_wq_wkv(gq, gkv, *, name):
    def body(gq_ref, gkv_ref, q_ref, kv_ref):
        q_ref[...] = jnp.zeros_like(q_ref)
        kv_ref[...] = jnp.zeros_like(kv_ref)
        for h in range(NH):
            q_ref[0:QL, h * HP:h * HP + DN + DR] = gq_ref[h]
            for g in range(4):
                c0 = NH * HP + h * HP + DN + 8 * g
                q_ref[0:QL, c0:c0 + 8] = gq_ref[h, :, DN + _swap_start(g):DN + _swap_start(g) + 8]
            kv_ref[:, h * HP:h * HP + DN] = gkv_ref[h, :, 0:DN]
            kv_ref[:, NH * HP + h * DV:NH * HP + (h + 1) * DV] = gkv_ref[h, :, DN:DN + DV]

    vm = pl.BlockSpec(memory_space=pltpu.VMEM)
    return pl.pallas_call(
        body, name=name, in_specs=[vm, vm], out_specs=[vm, vm],
        out_shape=[jax.ShapeDtypeStruct((512, 2 * NH * HP), gq.dtype), jax.ShapeDtypeStruct((KVL, NH * HP + NH * DV), gq.dtype)],
    )(gq, gkv)


def shard_wq_wkv_grad(dwq2, dwkv2, *, name):
    def body(q_ref, kv_ref, gq_ref, gkv_ref):
        for h in range(NH):
            gq_ref[h, :, 0:DN] = q_ref[0:QL, h * HP:h * HP + DN].astype(BF)
            for g in range(4):
                a = q_ref[0:QL, h * HP + DN + 8 * g:h * HP + DN + 8 * g + 8]
                c0 = NH * HP + h * HP + DN + _swap_start(g)
                gq_ref[h, :, DN + 8 * g:DN + 8 * g + 8] = (a + q_ref[0:QL, c0:c0 + 8]).astype(BF)
            gkv_ref[h, :, 0:DN] = kv_ref[:, h * HP:h * HP + DN].astype(BF)
            gkv_ref[h, :, DN:DN + DV] = kv_ref[:, NH * HP + h * DV:NH * HP + (h + 1) * DV].astype(BF)

    vm = pl.BlockSpec(memory_space=pltpu.VMEM)
    return pl.pallas_call(
        body, name=name, in_specs=[vm, vm], out_specs=[vm, vm],
        out_shape=[jax.ShapeDtypeStruct((NDEV, QL, (DN + DR)), BF), jax.ShapeDtypeStruct((NDEV, KVL, DN + DV), BF)],
    )(dwq2, dwkv2)


def unshard_cols(g, *, name, tm=256):
    _, K, n = g.shape
    tm = _pick(K, tm, 16)

    def body(g_ref, o_ref):
        for k in range(NDEV):
            o_ref[:, k * n:(k + 1) * n] = g_ref[k]

    return pl.pallas_call(
        body, name=name, grid=(K // tm,), in_specs=[pl.BlockSpec((NDEV, tm, n), lambda i: (0, i, 0))],
        out_specs=pl.BlockSpec((tm, NDEV * n), lambda i: (i, 0)), out_shape=jax.ShapeDtypeStruct((K, NDEV * n), g.dtype),
        compiler_params=pltpu.CompilerParams(dimension_semantics=("parallel",)),
    )(g)


def shard_cols(w, *, name, tm=256):
    K, n8 = w.shape
    n = n8 // NDEV
    tm = _pick(K, tm, 16)

    def body(w_ref, o_ref):
        for k in range(NDEV):
            o_ref[k] = w_ref[:, k * n:(k + 1) * n]

    return pl.pallas_call(
        body, name=name, grid=(K // tm,), in_specs=[pl.BlockSpec((tm, n8), lambda i: (i, 0))],
        out_specs=pl.BlockSpec((NDEV, tm, n), lambda i: (0, i, 0)), out_shape=jax.ShapeDtypeStruct((NDEV, K, n), w.dtype),
        compiler_params=pltpu.CompilerParams(dimension_semantics=("parallel",)),
    )(w)


def _rope_tables():
    t = np.arange(T)
    row = (t // GRID_W).astype(np.float32)
    col = (t % GRID_W).astype(np.float32)
    axis_dim = DR // 2
    inv = (np.float32(ROPE_THETA) ** (-np.arange(0, axis_dim, 2, dtype=np.float32) / np.float32(axis_dim))).astype(np.float32)
    ar, ac = (row[:, None] * inv).astype(np.float32), (col[:, None] * inv).astype(np.float32)
    cosv = np.concatenate([np.cos(ar), np.cos(ar), np.cos(ac), np.cos(ac)], axis=1).astype(np.float32)
    sinv = np.concatenate([-np.sin(ar), np.sin(ar), -np.sin(ac), np.sin(ac)], axis=1).astype(np.float32)
    ck = np.zeros((TKV, HP), np.float32)
    sk = np.zeros((TKV, HP), np.float32)
    ck[T:, DN:DN + DR] = 1.0
    ck[:T, DN:DN + DR] = cosv
    sk[:T, DN:DN + DR] = sinv
    cq = np.zeros((T, HP), np.float32)
    cq[:, :DN] = 1.0
    cq[:, DN:DN + DR] = cosv
    return jnp.asarray(ck), jnp.asarray(sk), jnp.asarray(cq), jnp.asarray(sk[:T])


def _local_step(x, ctx, tgt, mod_lat, mod_ctx, n1g, qg, kvg, n2g, fg, conv_w, conv_b, ffn_w, ffn_b, get_w, put_g, dep0):
    sh1, sc1, g1, sh2, sc2, g2 = [mod_lat[:, i * D:(i + 1) * D] for i in range(6)]
    csh1, csc1 = mod_ctx[:, 0:D], mod_ctx[:, D:2 * D]
    ck, sk, cq_t, sq_t = _rope_tables()
    qg_p = jnp.pad(qg, ((0, 0), (0, 512 - QL)))

    hcat = normmod_cat(ctx, x, n1g, csc1, csh1, sc1, sh1, dep0, name="normmod1")
    win = get_w("in", hcat)
    p = mm(hcat, win, M=T, tn=768, name="in_proj")
    pc = mm(hcat, win, M=TC, N=512, a_off=(T, 0), b_off=(0, O_KV), name="in_proj_ctx")
    wq2, wkv2, wao, wco, wo = get_w("mid", p)
    kh, vh, ckv = kvprep(pc, p, kvg, wkv2, ck, sk, name="kvprep")
    qr, cq = qprep(p, qg_p, wq2, cq_t, sq_t, name="qprep")
    o = attn_fwd(qr, kh, vh, name="attn_fwd")
    z = convz(p, conv_w, conv_b, name="convz")
    ya = mm(o, wao, name="attn_out")
    yc = mm(z, wco, name="conv_out")
    merged = gate_merge(p, ya, yc, name="gate_merge")
    a_out = mm(merged, wo, name="o_proj")
    x1, h2 = resid_normmod(x, a_out, g1, n2g, sc2, sh2, name="resid_normmod2")
    wup, wdn = get_w("ffn", h2)
    u0 = mm(h2, wup, tb=True, o_stack=True, tn=1408, name="up_proj")
    f = ffn_act(u0, ffn_w, ffn_b, name="ffn_act")
    dn = mm(f, wdn, tm=512, tk=DFF, name="down_proj")
    dx2, dd, dfg, loss = final_loss(x1, dn, g2, fg, tgt, name="final_loss")

    df = mm(dd, wdn, tb=True, tn=1408, name="down_proj_dx")
    dwdn = mm(f, dd, ta=True, out_dtype=BF, tm=1408, name="down_proj_dw")
    du0, dffn_w, dffn_b = ffn_act_bwd(u0, df, ffn_w, ffn_b, name="ffn_act_bwd")
    dwup = mm(du0, h2, ta=True, a_stack=True, out_dtype=BF, tm=1408, name="up_proj_dw")
    tok = put_g("ffn", dict(dwup=dwup, dwdn=dwdn))
    dx1, da, st2 = normmod_bwd(x1, dict(a=du0, b=wup, a_stack=True, tk=1408, dep=tok), n2g, sc2, dx2, dn, g1,
                               name="up_proj_dx_normmod2_bwd")

    dmerged = mm(da, wo, tb=True, name="o_proj_dx")
    dwo = mm(merged, da, ta=True, out_dtype=BF, tn=512, name="o_proj_dw")
    dya, dyc, dp = gate_merge_bwd(p, ya, yc, dmerged, name="gate_merge_bwd")
    do = mm(dya, wao, tb=True, out_dtype=BF, name="attn_out_dx")
    dwao = mm(o, dya, ta=True, out_dtype=BF, tn=512, name="attn_out_dw")
    dwco = mm(z, dyc, ta=True, out_dtype=BF, tn=512, name="conv_out_dw")
    tok = put_g("mid", dict(dwao=dwao, dwco=dwco, dwo=dwo))
    dz = mm(dyc, wco, tb=True, dep=tok, name="conv_out_dx")
    dp, dconv_w, dconv_b = convz_bwd(p, dz, conv_w, conv_b, dp, name="convz_bwd")
    dq, dk, dv = attn_bwd(qr, kh, vh, do, name="attn_bwd")
    dp, dq2, dqg = qprep_bwd(p, dq, qg_p, wq2, cq_t, sq_t, dp, name="qprep_bwd")
    dwq2 = mm(cq, dq2, ta=True, name="q_up_dw")
    dp, dpc, dkv2, dkvg = kvprep_bwd(pc, p, dk, dv, kvg, wkv2, ck, sk, dp, name="kvprep_bwd")
    dwkv2 = mm(ckv, dkv2, ta=True, name="kv_up_dw")
    tok = put_g("qkv", dict(dwq2=dwq2, dwkv2=dwkv2))

    dwin = mm(dp, hcat, ta=True, K=T, tm=768, dep=tok, name="in_proj_dw")
    dwin_c = mm(dpc, hcat, ta=True, K=TC, b_off=(T, 0), name="in_proj_ctx_dw")
    tok = put_g("in", dict(dwin=dwin, dwin_c=dwin_c))
    dhc = mm(dpc, win, tb=True, N=D, K=512, b_off=(0, O_KV), name="in_proj_ctx_dx")
    dx, _, st1 = normmod_bwd(x, dict(a=dp, b=win, tb=True, tk=1536, dep=tok), n1g, sc1, dx1, a_out, g1,
                             name="in_proj_dx_normmod1_bwd")
    stc = normmod_bwd(ctx, dhc, n1g, csc1, None, None, None, name="normmod1_ctx_bwd")

    zrow = jnp.zeros((1, D), F32)
    dmod_lat = jnp.concatenate([st1[0:1], st1[1:2], st1[3:4], st2[0:1], st2[1:2], st2[3:4]], axis=1)
    dmod_ctx = jnp.concatenate([stc[0:1], stc[1:2], zrow, zrow, zrow, zrow], axis=1)
    return dict(
        loss=loss, dx=dx, dmod_lat=dmod_lat, dmod_ctx=dmod_ctx,
        dn1g=st1[2:3] + stc[2:3], dqg=dqg, dkvg=dkvg, dn2g=st2[2:3], dfg=dfg,
        dconv_w=dconv_w, dconv_b=dconv_b, dffn_w=dffn_w, dffn_b=dffn_b)


def _me():
    x, y, c = lax.axis_index("x"), lax.axis_index("y"), lax.axis_index("c")
    return x, y, c, 4 * x + 2 * y + c


def _peer(x, y, c, k):
    px = 1 - x if k & 4 else x
    py = 1 - y if k & 2 else y
    pc = 1 - c if k & 1 else c
    return (px, py, pc), 4 * px + 2 * py + pc


def _exchange_tiles(src_of_peer, buf, send_sem, recv_sem):
    x, y, c, me = _me()
    for k in range(1, NDEV):
        dev, lin = _peer(x, y, c, k)
        pltpu.make_async_remote_copy(src_ref=src_of_peer(lin), dst_ref=buf.at[me], send_sem=send_sem, recv_sem=recv_sem,
                                     device_id=dev, device_id_type=MESH).start()
    seven = buf.at[pl.ds(0, NDEV - 1)]
    pltpu.make_async_remote_copy(src_ref=seven, dst_ref=seven, send_sem=send_sem, recv_sem=recv_sem,
                                 device_id=(x, y, c), device_id_type=MESH).wait()


def _silu(z):
    return z * jax.nn.sigmoid(z)


def ada_fwd(c, c_ctx, ffn_w, conv_w, w_shard, b_shard, deps, *, name):
    nsh = w_shard.shape[1]
    deps = [d for d in deps if d is not None]

    def body(c_ref, cc_ref, fw_ref, cw_ref, w_ref, b_ref, *rest):
        s_ref, m_ref, mine, res, sems = rest[len(deps):]
        x, y, c, me = _me()
        mine[0:1, :] = _silu(c_ref[...])
        mine[1:2, :] = _silu(cc_ref[...])
        mine[2:5, :] = fw_ref[...]
        mine[5:8, :] = cw_ref[...]
        s_ref[me] = mine[...]
        _exchange_tiles(lambda lin: mine, s_ref, sems.at[0], sems.at[1])
        sall = s_ref[...].reshape(NDEV * 8, D).astype(BF)
        r = jnp.dot(sall, w_ref[...].astype(BF), preferred_element_type=F32) + b_ref[...]
        res[...] = r.reshape(NDEV, 8, nsh)
        m_ref[me] = res[me]
        _exchange_tiles(lambda lin: res.at[lin], m_ref, sems.at[2], sems.at[3])

    vm = pl.BlockSpec(memory_space=pltpu.VMEM)
    return pl.pallas_call(
        body, name=name, in_specs=[vm] * 6 + [pl.BlockSpec(memory_space=pl.ANY)] * len(deps), out_specs=[vm, vm],
        out_shape=[jax.ShapeDtypeStruct((NDEV, 8, D), F32), jax.ShapeDtypeStruct((NDEV, 8, nsh), F32)],
        scratch_shapes=[pltpu.VMEM((8, D), F32), pltpu.VMEM((NDEV, 8, nsh), F32), pltpu.SemaphoreType.DMA((4,))],
    )(c, c_ctx, ffn_w, conv_w, w_shard, b_shard, *deps)


P_DML, P_DMC, P_N1, P_QG, P_KVG, P_CB, P_N2, P_FB, P_FG, P_CW, P_FW, P_LOSS, P_ROWS = 0, 6, 12, 13, 14, 15, 16, 17, 23, 24, 27, 45, 48
FROWS = 3


def sync_small(r, deps, *, name):
    ins = [r["dmod_lat"], r["dmod_ctx"], r["dn1g"], r["dqg"], r["dkvg"], r["dconv_b"], r["dn2g"], r["dffn_b"], r["dfg"],
           r["dconv_w"], r["dffn_w"], r["loss"]]

    def put_wide(p, row0, row, n):
        for j in range(-(-n // D)):
            w = min(D, n - j * D)
            p[row0 + j:row0 + j + 1, 0:w] = row[:, j * D:j * D + w]

    def body(dml, dmc, n1, qg, kvg, cb, n2, fb, fg, cw, fw, loss, *rest):
        a_ref, sum_ref, p, sems = rest[len(deps):]
        x, y, c, me = _me()
        p[...] = jnp.zeros_like(p)
        put_wide(p, P_DML, dml, 6 * D)
        put_wide(p, P_DMC, dmc, 6 * D)
        put_wide(p, P_N1, n1, D)
        put_wide(p, P_QG, qg, 512)
        put_wide(p, P_KVG, kvg, KVL)
        put_wide(p, P_CB, cb, CONV)
        put_wide(p, P_N2, n2, D)
        put_wide(p, P_FG, fg, D)
        put_wide(p, P_LOSS, loss, 128)
        for s in range(2):
            put_wide(p, P_FB + FROWS * s, fb.at[s], DFF)
        for k in range(3):
            put_wide(p, P_CW + k, cw.at[k:k + 1], CONV)
            for s in range(2):
                put_wide(p, P_FW + FROWS * (2 * k + s), fw.at[s, k:k + 1], DFF)
        a_ref[me] = p[...]
        _exchange_tiles(lambda lin: p, a_ref, sems.at[0], sems.at[1])
        acc = a_ref[0]
        for k in range(1, NDEV):
            acc = acc + a_ref[k]
        sum_ref[...] = acc

    vm = pl.BlockSpec(memory_space=pltpu.VMEM)
    return pl.pallas_call(
        body, name=name, in_specs=[vm] * len(ins) + [pl.BlockSpec(memory_space=pl.ANY)] * len(deps), out_specs=[vm, vm],
        out_shape=[jax.ShapeDtypeStruct((NDEV, P_ROWS, D), F32), jax.ShapeDtypeStruct((P_ROWS, D), F32)],
        scratch_shapes=[pltpu.VMEM((P_ROWS, D), F32), pltpu.SemaphoreType.DMA((2,))],
    )(*ins, *deps)


def ada_bwd(s_all, dml, dmc, w_shard, c_ctx, *, name):
    nsh = w_shard.shape[1]

    def body(s_ref, dml_ref, dmc_ref, w_ref, c_ref, dw_ref, gc_ref, s16, dm16, part, buf, sems):
        x, y, c, me = _me()
        s16[...] = jnp.zeros_like(s16)
        dm16[...] = jnp.zeros_like(dm16)
        for k in range(NDEV):
            s16[k:k + 1, :] = s_ref[k, 0:1, :]
        s16[8:9, :] = s_ref[0, 1:2, :]
        dm16[0:8, :] = dml_ref[...]
        dm16[8:9, :] = dmc_ref[...]
        dw_ref[...] = lax.dot_general(s16[...].astype(BF), dm16[...].astype(BF), (((0,), (0,)), ((), ())),
                                      preferred_element_type=F32)
        part[...] = lax.dot_general(dm16[8:16, :].astype(BF), w_ref[...].astype(BF), (((1,), (1,)), ((), ())),
                                    preferred_element_type=F32)
        buf[me] = part[...]
        _exchange_tiles(lambda lin: part, buf, sems.at[0], sems.at[1])
        acc = buf[0]
        for k in range(1, NDEV):
            acc = acc + buf[k]
        z = c_ref[...]
        sg = jax.nn.sigmoid(z)
        gc_ref[...] = acc * (sg * (1.0 + z * (1.0 - sg)))

    vm = pl.BlockSpec(memory_space=pltpu.VMEM)
    return pl.pallas_call(
        body, name=name, in_specs=[vm] * 5, out_specs=[vm, vm],
        out_shape=[jax.ShapeDtypeStruct((D, nsh), F32), jax.ShapeDtypeStruct((8, D), F32)],
        scratch_shapes=[pltpu.VMEM((16, D), F32), pltpu.VMEM((16, nsh), F32), pltpu.VMEM((8, D), F32),
                        pltpu.VMEM((NDEV, 8, D), F32), pltpu.SemaphoreType.DMA((2,))],
    )(s_all, dml, dmc, w_shard, c_ctx)


HBM_SPEC = pl.BlockSpec(memory_space=pltpu.HBM)
SEM_SPEC = pl.BlockSpec(memory_space=pltpu.SEMAPHORE)
EFFECT = pltpu.SideEffectType.DATAFLOW_SIDE_EFFECTING


ALL_PEERS = tuple(range(1, NDEV))
FIRST_HOP = (1, 2, 4, 6)
RELAY = (2, 4, 6)


def _exchange_copies(srcs, lands, send, recv, per_peer, peers):
    x, y, c, me = _me()
    n = len(peers)
    cps = []
    for t in range(len(srcs)):
        for j, k in enumerate(peers):
            dev, lin = _peer(x, y, c, k)
            cps.append(pltpu.make_async_remote_copy(
                src_ref=srcs[t].at[lin] if per_peer else srcs[t], dst_ref=lands[t].at[me],
                send_sem=send.at[n * t + j], recv_sem=recv.at[n * t + j], device_id=dev, device_id_type=MESH))
    return cps


def _relay_copies(lands, send, recv):
    x, y, c, me = _me()
    n = len(RELAY)
    cps = []
    for t in range(len(lands)):
        for j, k in enumerate(RELAY):
            slot = lands[t].at[_peer(x, y, c, k)[1]]
            cps.append(pltpu.make_async_remote_copy(
                src_ref=slot, dst_ref=slot, send_sem=send.at[n * t + j], recv_sem=recv.at[n * t + j],
                device_id=(x, y, 1 - c), device_id_type=MESH))
    return cps


def _own_copies(srcs, lands, own, per_peer):
    me = _me()[3]
    return [pltpu.make_async_copy(srcs[t].at[me] if per_peer else srcs[t], lands[t].at[me], own.at[t])
            for t in range(len(srcs))]


def exchange_start(srcs, *, per_peer, name, dep=None, peers=ALL_PEERS):
    nt = len(srcs)
    ns = len(peers) * nt
    land_shapes = [(a.shape if per_peer else (NDEV,) + a.shape) for a in srcs]
    deps = [] if dep is None else [dep]

    def body(*refs):
        src, land = refs[:nt], refs[nt:2 * nt]
        send, recv, own = refs[2 * nt + len(deps):2 * nt + len(deps) + 3]
        for cp in _exchange_copies(src, land, send, recv, per_peer, peers) + _own_copies(src, land, own, per_peer):
            cp.start()
        refs[-1][...] = jnp.zeros_like(refs[-1])

    hb = lambda a: pltpu.with_memory_space_constraint(a, pltpu.HBM)
    outs = pl.pallas_call(
        body, name=name,
        out_shape=(pltpu.SemaphoreType.DMA((ns,)), pltpu.SemaphoreType.DMA((ns,)), pltpu.SemaphoreType.DMA((nt,)),
                   *[pltpu.HBM(a.shape, a.dtype) for a in srcs], *[pltpu.HBM(s, a.dtype) for s, a in zip(land_shapes, srcs)],
                   jax.ShapeDtypeStruct((8, 128), F32)),
        in_specs=[HBM_SPEC] * (2 * nt) + [pl.BlockSpec(memory_space=pl.ANY)] * len(deps),
        out_specs=(SEM_SPEC, SEM_SPEC, SEM_SPEC, *([HBM_SPEC] * (2 * nt)), pl.BlockSpec(memory_space=pltpu.VMEM)),
        input_output_aliases={i: 3 + i for i in range(2 * nt)},
        compiler_params=pltpu.CompilerParams(has_side_effects=EFFECT),
    )(*[hb(a) for a in srcs], *[hb(lax.empty(s, a.dtype)) for s, a in zip(land_shapes, srcs)], *deps)
    return dict(send=outs[0], recv=outs[1], own=outs[2], src=list(outs[3:3 + nt]), land=list(outs[3 + nt:3 + 2 * nt]),
                token=outs[-1], per_peer=per_peer, peers=peers)


def exchange_wait(h, after, *, name):
    nt = len(h["src"])
    per_peer, peers = h["per_peer"], h["peers"]

    def body(*refs):
        src, land, send, recv, own = refs[:nt], refs[nt:2 * nt], refs[2 * nt], refs[2 * nt + 1], refs[2 * nt + 2]
        for cp in _exchange_copies(src, land, send, recv, per_peer, peers):
            cp.wait_send()
            cp.wait_recv()
        for cp in _own_copies(src, land, own, per_peer):
            cp.wait()

    outs = pl.pallas_call(
        body, name=name,
        out_shape=(*[pltpu.HBM(a.shape, a.dtype) for a in h["src"]], *[pltpu.HBM(a.shape, a.dtype) for a in h["land"]]),
        in_specs=[HBM_SPEC] * (2 * nt) + [SEM_SPEC, SEM_SPEC, SEM_SPEC, pl.BlockSpec(memory_space=pl.ANY)],
        out_specs=tuple([HBM_SPEC] * (2 * nt)),
        input_output_aliases={i: i for i in range(2 * nt)},
        compiler_params=pltpu.CompilerParams(has_side_effects=EFFECT),
    )(*h["src"], *h["land"], h["send"], h["recv"], h["own"], after)
    return list(outs[nt:])


def relay_start(lands, *, name):
    nt = len(lands)
    ns = len(RELAY) * nt

    def body(*refs):
        for cp in _relay_copies(refs[:nt], refs[nt], refs[nt + 1]):
            cp.start()

    outs = pl.pallas_call(
        body, name=name,
        out_shape=(pltpu.SemaphoreType.DMA((ns,)), pltpu.SemaphoreType.DMA((ns,)),
                   *[pltpu.HBM(a.shape, a.dtype) for a in lands]),
        in_specs=[HBM_SPEC] * nt, out_specs=(SEM_SPEC, SEM_SPEC, *([HBM_SPEC] * nt)),
        input_output_aliases={i: 2 + i for i in range(nt)},
        compiler_params=pltpu.CompilerParams(has_side_effects=EFFECT),
    )(*lands)
    return dict(send=outs[0], recv=outs[1], land=list(outs[2:]))


def relay_wait(h, *, name):
    nt = len(h["land"])

    def body(*refs):
        for cp in _relay_copies(refs[:nt], refs[nt], refs[nt + 1]):
            cp.wait_send()
            cp.wait_recv()

    outs = pl.pallas_call(
        body, name=name, out_shape=tuple(pltpu.HBM(a.shape, a.dtype) for a in h["land"]),
        in_specs=[HBM_SPEC] * nt + [SEM_SPEC, SEM_SPEC], out_specs=tuple([HBM_SPEC] * nt),
        input_output_aliases={i: i for i in range(nt)},
        compiler_params=pltpu.CompilerParams(has_side_effects=EFFECT),
    )(*h["land"], h["send"], h["recv"])
    return list(outs)


def _adamw_math(w, g, m, v):
    nm = B1 * m + (1.0 - B1) * g
    nv = B2 * v + (1.0 - B2) * (g * g)
    m_hat = nm / (1.0 - B1 ** STEP)
    v_hat = nv / (1.0 - B2 ** STEP)
    return -LR * (m_hat / (jnp.sqrt(v_hat) + AEPS) + WD * w), nm, nv


def adamw_many(ws, gs, ms, vs, *, name):
    n = len(ws)

    def body(*refs):
        for k in range(n):
            d, nm, nv = _adamw_math(refs[k][...], refs[n + k][...], refs[2 * n + k][...], refs[3 * n + k][...])
            refs[4 * n + k][...] = d
            refs[5 * n + k][...] = nm
            refs[6 * n + k][...] = nv

    vm = pl.BlockSpec(memory_space=pltpu.VMEM)
    sh = [jax.ShapeDtypeStruct(w.shape, F32) for w in ws]
    outs = pl.pallas_call(body, name=name, in_specs=[vm] * (4 * n), out_specs=[vm] * (3 * n), out_shape=sh * 3,
                          )(*ws, *gs, *ms, *vs)
    return outs[:n], outs[n:2 * n], outs[2 * n:]


def adamw(w, g, m, v, *, name, tr=256):
    R, C = w.shape
    tr = _pick(R, tr, 8)

    def body(w_ref, g_ref, m_ref, v_ref, d_ref, nm_ref, nv_ref):
        d_ref[...], nm_ref[...], nv_ref[...] = _adamw_math(w_ref[...], g_ref[...], m_ref[...], v_ref[...])

    blk = pl.BlockSpec((tr, C), lambda i: (i, 0))
    sh = jax.ShapeDtypeStruct((R, C), F32)
    return pl.pallas_call(
        body, name=name, grid=(R // tr,), in_specs=[blk, blk, blk, blk], out_specs=[blk, blk, blk],
        out_shape=[sh, sh, sh], compiler_params=pltpu.CompilerParams(dimension_semantics=("parallel",)),
    )(w, g, m, v)


def adamw_slots(w, slots, m, v, *, name, tr=256):
    unit = w.ndim == 3
    R, C = w.shape[0], w.shape[-1]
    if R % 16 == 0:
        tr = _pick(R, tr, 16)
    else:
        tr = 144

    def body(w_ref, s_ref, m_ref, v_ref, g_ref, d_ref, nm_ref, nv_ref):
        g = s_ref[0].astype(F32)
        for k in range(1, NDEV):
            g = g + s_ref[k].astype(F32)
        g_ref[...] = g
        d_ref[...], nm_ref[...], nv_ref[...] = _adamw_math(w_ref[...], g, m_ref[...], v_ref[...])

    blk = pl.BlockSpec((tr, None, C), lambda i: (i, 0, 0)) if unit else pl.BlockSpec((tr, C), lambda i: (i, 0))
    sh = jax.ShapeDtypeStruct(w.shape, F32)
    return pl.pallas_call(
        body, name=name, grid=(pl.cdiv(R, tr),), in_specs=[blk, pl.BlockSpec((NDEV, tr, C), lambda i: (0, i, 0)), blk, blk],
        out_specs=[blk, blk, blk, blk], out_shape=[sh, sh, sh, sh],
        compiler_params=pltpu.CompilerParams(dimension_semantics=("parallel",)),
    )(w, slots, m, v)


def _padc(a, n=D):
    return jnp.pad(a, ((0, 0), (0, n - a.shape[1])))


def kernel(x, c, ctx, c_ctx, w_ada, b_ada, norm1_g, w_in, q_norm_g, kv_norm_g, w_uq, w_ukv, conv_w, conv_b, w_attn_out, w_conv_out, w_o, norm2_g, w_up, ffn_conv_w, ffn_conv_b, w_down, final_g, loss_target, m_c_ctx, m_w_ada, m_b_ada, m_norm1_g, m_w_in, m_q_norm_g, m_kv_norm_g, m_w_uq, m_w_ukv, m_conv_w, m_conv_b, m_w_attn_out, m_w_conv_out, m_w_o, m_norm2_g, m_w_up, m_ffn_conv_w, m_ffn_conv_b, m_w_down, m_final_g, v_c_ctx, v_w_ada, v_b_ada, v_norm1_g, v_w_in, v_q_norm_g, v_kv_norm_g, v_w_uq, v_w_ukv, v_conv_w, v_conv_b, v_w_attn_out, v_w_conv_out, v_w_o, v_norm2_g, v_w_up, v_ffn_conv_w, v_ffn_conv_b, v_w_down, v_final_g):
    me = 4 * lax.axis_index("x") + 2 * lax.axis_index("y") + lax.axis_index("c")
    W = dict(c_ctx=c_ctx, w_ada=w_ada, b_ada=b_ada, norm1_g=norm1_g, w_in=w_in, q_norm_g=q_norm_g, kv_norm_g=kv_norm_g,
             w_uq=w_uq, w_ukv=w_ukv, conv_w=conv_w, conv_b=conv_b, w_attn_out=w_attn_out, w_conv_out=w_conv_out, w_o=w_o,
             norm2_g=norm2_g, w_up=w_up, ffn_conv_w=ffn_conv_w, ffn_conv_b=ffn_conv_b, w_down=w_down, final_g=final_g)
    M = dict(c_ctx=m_c_ctx, w_ada=m_w_ada, b_ada=m_b_ada, norm1_g=m_norm1_g, w_in=m_w_in, q_norm_g=m_q_norm_g,
             kv_norm_g=m_kv_norm_g, w_uq=m_w_uq, w_ukv=m_w_ukv, conv_w=m_conv_w, conv_b=m_conv_b, w_attn_out=m_w_attn_out,
             w_conv_out=m_w_conv_out, w_o=m_w_o, norm2_g=m_norm2_g, w_up=m_w_up, ffn_conv_w=m_ffn_conv_w,
             ffn_conv_b=m_ffn_conv_b, w_down=m_w_down, final_g=m_final_g)
    V = dict(c_ctx=v_c_ctx, w_ada=v_w_ada, b_ada=v_b_ada, norm1_g=v_norm1_g, w_in=v_w_in, q_norm_g=v_q_norm_g,
             kv_norm_g=v_kv_norm_g, w_uq=v_w_uq, w_ukv=v_w_ukv, conv_w=v_conv_w, conv_b=v_conv_b, w_attn_out=v_w_attn_out,
             w_conv_out=v_w_conv_out, w_o=v_w_o, norm2_g=v_norm2_g, w_up=v_w_up, ffn_conv_w=v_ffn_conv_w,
             ffn_conv_b=v_ffn_conv_b, w_down=v_w_down, final_g=v_final_g)
    names = list(W)
    transposed = ("w_up",)
    as2d = lambda k, a: (a.reshape(1, -1) if a.ndim == 1 else
                         a[0].T if k in transposed else a.reshape(a.shape[-2], a.shape[-1]))
    W2 = {k: as2d(k, a) for k, a in W.items()}
    M2 = {k: as2d(k, a) for k, a in M.items()}
    V2 = {k: as2d(k, a) for k, a in V.items()}
    unit3 = lambda a: jnp.transpose(a, (2, 0, 1))
    W3, M3, V3 = unit3(W["w_in"]), unit3(M["w_in"]), unit3(V["w_in"])
    nsh = W2["w_ada"].shape[1]

    b_sh = lax.dynamic_slice(W2["b_ada"], (0, me * nsh), (1, nsh))
    s_all, m_all = ada_fwd(c, W2["c_ctx"], _padc(W2["ffn_conv_w"]), _padc(W2["conv_w"]), W2["w_ada"], b_sh, [],
                           name="ada_fwd")
    mod_lat = m_all[:, 0, :].reshape(1, 6 * D)
    mod_ctx = m_all[:, 1, :].reshape(1, 6 * D)
    ffn_w_full = s_all[:, 2:5, :2 * DFF // NDEV].transpose(1, 0, 2).reshape(3, 2 * DFF)
    conv_w_full = s_all[:, 5:8, :CONV // NDEV].transpose(1, 0, 2).reshape(3, CONV)

    stage_w = {"in": ["w_in"], "mid": ["w_uq", "w_ukv", "w_attn_out", "w_conv_out", "w_o"], "ffn": ["w_up", "w_down"]}
    two_level = ("in", "mid")
    ag, tok = {}, m_all
    for st, nms in stage_w.items():
        ag[st] = exchange_start([W2[nm].astype(BF) for nm in nms], per_peer=False, dep=tok, name="ag_start_" + st,
                                peers=FIRST_HOP if st in two_level else ALL_PEERS)
        tok = ag[st]["token"]

    def get_w(stage, after):
        lands = exchange_wait(ag[stage], after, name="ag_wait_" + stage)
        if stage in two_level:
            lands = relay_wait(relay_start(lands, name="ag_relay_" + stage), name="ag_relay_wait_" + stage)
        g = dict(zip(stage_w[stage], lands))
        if stage == "in":
            return build_win(g["w_in"], name="build_win")
        if stage == "mid":
            wq2, wkv2 = build_wq_wkv(g["w_uq"], g["w_ukv"], name="build_wq_wkv")
            return (wq2, wkv2, unshard_cols(g["w_attn_out"], name="unshard_w_attn_out"),
                    unshard_cols(g["w_conv_out"], name="unshard_w_conv_out"), g["w_o"].reshape(D, D))
        return g["w_up"].reshape(2 * DFF, D), g["w_down"].reshape(DFF, D)

    stage_g = {"ffn": ["w_up", "w_down"], "mid": ["w_attn_out", "w_conv_out", "w_o"], "qkv": ["w_uq", "w_ukv"],
               "in": ["w_in"]}
    rs = {}

    def put_g(stage, g):
        if stage == "in":
            parts = [shard_win_grad(g["dwin"], g["dwin_c"], name="shard_win_grad")]
        elif stage == "mid":
            parts = [shard_cols(g["dwao"], name="shard_w_attn_out"), shard_cols(g["dwco"], name="shard_w_conv_out"),
                     g["dwo"].reshape(NDEV, D // NDEV, D)]
        elif stage == "qkv":
            parts = list(shard_wq_wkv_grad(g["dwq2"], g["dwkv2"], name="shard_wq_wkv_grad"))
        else:
            parts = [g["dwup"].reshape(NDEV, 2 * DFF // NDEV, D), g["dwdn"].reshape(NDEV, DFF // NDEV, D)]
        rs[stage] = exchange_start(parts, per_peer=True, name="rs_start_" + stage)
        return rs[stage]["token"]

    r = _local_step(x[0], ctx[0], loss_target[0], mod_lat, mod_ctx, W2["norm1_g"], W2["q_norm_g"], W2["kv_norm_g"],
                    W2["norm2_g"], W2["final_g"], conv_w_full, W2["conv_b"], ffn_w_full, W2["ffn_conv_b"], get_w, put_g,
                    ag["ffn"]["token"])

    G, DL, NM, NV = {}, {}, {}, {}

    def finish(stage, after):
        for nm, sl in zip(stage_g[stage], exchange_wait(rs[stage], after, name="rs_wait_" + stage)):
            wmv = (W3, M3, V3) if nm == "w_in" else (W2[nm], M2[nm], V2[nm])
            G[nm], DL[nm], NM[nm], NV[nm] = adamw_slots(wmv[0], sl, wmv[1], wmv[2], name="adamw_" + nm)
            after = DL[nm]
        return after

    after = r["dx"]
    for st in ("ffn", "mid", "qkv"):
        after = finish(st, after)

    a_buf, ssum = sync_small(r, [DL[nm] for st in ("ffn", "mid", "qkv") for nm in stage_g[st]], name="sync_small")
    loss = ssum[P_LOSS, 0]
    G["norm1_g"] = ssum[P_N1:P_N1 + 1]
    G["q_norm_g"] = ssum[P_QG:P_QG + 1, :QL]
    G["kv_norm_g"] = ssum[P_KVG:P_KVG + 1, :KVL]
    G["conv_b"] = ssum[P_CB:P_CB + 1, :CONV]
    G["norm2_g"] = ssum[P_N2:P_N2 + 1]
    G["ffn_conv_b"] = ssum[P_FB:P_FB + 2 * FROWS].reshape(1, 2, FROWS * D)[:, :, :DFF].reshape(1, 2 * DFF)
    G["final_g"] = ssum[P_FG:P_FG + 1]
    G["conv_w"] = lax.dynamic_slice(ssum[P_CW:P_CW + 3, :CONV], (0, me * (CONV // NDEV)), (3, CONV // NDEV))
    fw_full = ssum[P_FW:P_FW + 6 * FROWS].reshape(3, 2, FROWS * D)[:, :, :DFF].reshape(3, 2 * DFF)
    G["ffn_conv_w"] = lax.dynamic_slice(fw_full, (0, me * (2 * DFF // NDEV)), (3, 2 * DFF // NDEV))
    G["b_ada"] = (ssum[P_DML:P_DML + 6] + ssum[P_DMC:P_DMC + 6]).reshape(1, 6 * D)

    dml = lax.dynamic_slice(a_buf[:, P_DML:P_DML + 6, :].reshape(NDEV, 6 * D), (0, me * nsh), (NDEV, nsh))
    dmc = lax.dynamic_slice(ssum[P_DMC:P_DMC + 6].reshape(1, 6 * D), (0, me * nsh), (1, nsh))
    G["w_ada"], gcc = ada_bwd(s_all, dml, dmc, W2["w_ada"], W2["c_ctx"], name="ada_bwd")
    G["c_ctx"] = gcc[0:1]

    DL["w_ada"], NM["w_ada"], NV["w_ada"] = adamw(W2["w_ada"], G["w_ada"], M2["w_ada"], V2["w_ada"], name="adamw_w_ada")
    small = ["c_ctx", "b_ada", "norm1_g", "q_norm_g", "kv_norm_g", "conv_b", "norm2_g", "ffn_conv_b", "final_g", "conv_w",
             "ffn_conv_w"]
    ds, nms, nvs = adamw_many([W2[k] for k in small], [G[k] for k in small], [M2[k] for k in small],
                              [V2[k] for k in small], name="adamw_small")
    for k, nm in enumerate(small):
        DL[nm], NM[nm], NV[nm] = ds[k], nms[k], nvs[k]
    finish("in", ds[0])

    outs = [loss, r["dx"][None]]
    for grp in (G, DL, NM, NV):
        outs += [grp[nm].T[None] if nm in transposed else
                 jnp.transpose(grp[nm], (1, 2, 0)) if nm == "w_in" else grp[nm].reshape(W[nm].shape) for nm in names]
    return tuple(outs)
```

```python
import functools
import numpy as np
import jax
import jax.numpy as jnp
from jax import lax
from jax.experimental import pallas as pl
from jax.experimental.pallas import tpu as pltpu

F32 = jnp.float32
BF = jnp.bfloat16
MESH = pl.DeviceIdType.MESH

D = 1024
T = 2048
TC = 256
TKV = T + TC
GRID_W = 64
NH = 8
DN = 64
DR = 32
DV = 64
QL = 384
KVL = 256
CONV = 512
DFF = 2816
EPS = 1e-6
ROPE_THETA = 10000.0
SCALE = (DN + DR) ** -0.5
NDEV = 8
HP = 128

O_GA, O_GC, O_KV, O_Q, O_CV = 0, 1024, 2048, 2560, 3072
NIN = 4608
CVB = 256
N_IN = 4256
SH_IN = N_IN // NDEV

LR, B1, B2, AEPS, WD, STEP = 0.001, 0.9, 0.999, 1e-08, 0.01, 10


def _pick(n, target, mult=128):
    best = None
    for d in range(mult, min(n, target) + 1, mult):
        if n % d == 0:
            best = d
    return best if best is not None else n


def _swap_start(g):
    return 8 * (g ^ 1)


def mm(a, b, *, ta=False, tb=False, out_dtype=F32, name, tm=1024, tn=1024, tk=2048, M=None, N=None, K=None,
       a_off=(0, 0), b_off=(0, 0), a_stack=False, b_stack=False, o_stack=False, dep=None):
    def dims(arr, stack):
        return (arr.shape[1], 2 * arr.shape[2]) if stack else arr.shape

    ar, ac = dims(a, a_stack)
    br, bc = dims(b, b_stack)
    M = M or ((ac if ta else ar) - a_off[1 if ta else 0])
    K = K or ((ar if ta else ac) - a_off[0 if ta else 1])
    N = N or ((br if tb else bc) - b_off[0 if tb else 1])
    tm = _pick(M, tm, 128 if ta else 16)
    tn = _pick(N // 2 if (o_stack or (b_stack and not tb)) else N, tn, 128)
    tk = _pick(K // 2 if ((a_stack and not ta) or (b_stack and tb)) else K, tk, 128)
    nk = K // tk
    ca = 0 if ta else 1
    cb = 1 if tb else 0

    def body(a_ref, b_ref, *rest):
        o_ref, acc = rest[-2:]
        k = pl.program_id(2)
        part = lax.dot_general(a_ref[...].astype(BF), b_ref[...].astype(BF),
                               (((ca,), (cb,)), ((), ())), preferred_element_type=F32)
        if nk == 1:
            o_ref[...] = part.astype(o_ref.dtype)
        else:
            @pl.when(k == 0)
            def _():
                acc[...] = part

            @pl.when(k > 0)
            def _():
                acc[...] += part

            @pl.when(k == nk - 1)
            def _():
                o_ref[...] = acc[...].astype(o_ref.dtype)

    def spec(blk, rc, off, stack, ncols):
        assert off[0] % blk[0] == 0 and off[1] % blk[1] == 0, (name, blk, off)
        ro, co = off[0] // blk[0], off[1] // blk[1]
        if not stack:
            return pl.BlockSpec(blk, lambda i, j, k: (rc(i, j, k)[0] + ro, rc(i, j, k)[1] + co))
        nhb = ncols // 2 // blk[1]
        return pl.BlockSpec((None,) + blk,
                            lambda i, j, k: ((rc(i, j, k)[1] + co) // nhb, rc(i, j, k)[0] + ro, (rc(i, j, k)[1] + co) % nhb))

    a_spec = spec((tk, tm), lambda i, j, k: (k, i), a_off, a_stack, ac) if ta else \
        spec((tm, tk), lambda i, j, k: (i, k), a_off, a_stack, ac)
    b_spec = spec((tn, tk), lambda i, j, k: (j, k), b_off, b_stack, bc) if tb else \
        spec((tk, tn), lambda i, j, k: (k, j), b_off, b_stack, bc)
    o_spec = spec((tm, tn), lambda i, j, k: (i, j), (0, 0), o_stack, N)
    o_shape = (2, M, N // 2) if o_stack else (M, N)
    deps = [] if dep is None else [dep]
    return pl.pallas_call(
        body, name=name, grid=(M // tm, N // tn, nk),
        in_specs=[a_spec, b_spec] + [pl.BlockSpec(memory_space=pl.ANY)] * len(deps),
        out_specs=o_spec, out_shape=jax.ShapeDtypeStruct(o_shape, out_dtype),
        scratch_shapes=[pltpu.VMEM((tm, tn) if nk > 1 else (8, 128), F32)],
        compiler_params=pltpu.CompilerParams(dimension_semantics=("parallel", "parallel", "arbitrary")),
    )(a, b, *deps)


def _row(width):
    return pl.BlockSpec((1, width), lambda *_: (0, 0))


NLAT = T // TC


def normmod_cat(ctx, x, g, csc, csh, sc, sh, dep, *, name, tm=256):
    assert tm == TC

    def body(c_ref, x_ref, g_ref, csc_ref, csh_ref, sc_ref, sh_ref, dep_ref, h_ref):
        last = pl.program_id(0) == NLAT
        xv = jnp.where(last, c_ref[...], x_ref[...])
        scv = jnp.where(last, csc_ref[...], sc_ref[...])
        shv = jnp.where(last, csh_ref[...], sh_ref[...])
        r = lax.rsqrt(jnp.mean(xv * xv, axis=-1, keepdims=True) + EPS)
        h_ref[...] = ((xv * r * g_ref[...]) * (1.0 + scv) + shv).astype(BF)

    return pl.pallas_call(
        body, name=name, grid=(TKV // tm,),
        in_specs=[pl.BlockSpec((tm, D), lambda i: (0, 0)), pl.BlockSpec((tm, D), lambda i: (jnp.minimum(i, NLAT - 1), 0)),
                  _row(D), _row(D), _row(D), _row(D), _row(D), pl.BlockSpec(memory_space=pl.ANY)],
        out_specs=pl.BlockSpec((tm, D), lambda i: (i, 0)), out_shape=jax.ShapeDtypeStruct((TKV, D), BF),
        compiler_params=pltpu.CompilerParams(dimension_semantics=("parallel",)),
    )(ctx, x, g, csc, csh, sc, sh, dep)


def kvprep(pc, p, kvg, wkv2, ck, sk, *, name, tm=256):
    assert tm == TC
    nb = TKV // tm
    kvcol = O_KV // 512

    def body(pc_ref, p_ref, g_ref, w_ref, ck_ref, sk_ref, k_ref, v_ref, ckv_ref):
        i = pl.program_id(0)
        t = jnp.where(i == NLAT, pc_ref[...], p_ref[...])
        pk = t[:, :KVL]
        r = lax.rsqrt(jnp.mean(pk * pk, axis=-1, keepdims=True) + EPS)
        ckv = (pk * r * g_ref[...]).astype(BF)
        ckv_ref[...] = ckv
        kv2 = jnp.dot(ckv, w_ref[...], preferred_element_type=F32)
        krr = t[:, KVL:KVL + HP] * ck_ref[...] + t[:, KVL + HP:KVL + 2 * HP] * sk_ref[...]
        k_ref[...] = (kv2[:, :NH * HP] + jnp.concatenate([krr] * NH, axis=1)).astype(BF)
        v_ref[...] = kv2[:, NH * HP:].astype(BF)

    return pl.pallas_call(
        body, name=name, grid=(nb,),
        in_specs=[pl.BlockSpec((tm, 512), lambda i: (0, 0)),
                  pl.BlockSpec((tm, 512), lambda i: (jnp.minimum(i, NLAT - 1), kvcol)),
                  _row(KVL), pl.BlockSpec((KVL, NH * HP + NH * DV), lambda i: (0, 0)),
                  pl.BlockSpec((tm, HP), lambda i: (i, 0)), pl.BlockSpec((tm, HP), lambda i: (i, 0))],
        out_specs=[pl.BlockSpec((tm, NH * HP), lambda i: (i, 0)), pl.BlockSpec((tm, NH * DV), lambda i: (i, 0)),
                   pl.BlockSpec((tm, KVL), lambda i: (i, 0))],
        out_shape=[jax.ShapeDtypeStruct((TKV, NH * HP), BF), jax.ShapeDtypeStruct((TKV, NH * DV), BF),
                   jax.ShapeDtypeStruct((TKV, KVL), BF)],
        compiler_params=pltpu.CompilerParams(dimension_semantics=("parallel",)),
    )(pc, p, kvg, wkv2, ck, sk)


def qprep(p, qg, wq2, cq_t, sq_t, *, name, tm=256):
    qcol = O_Q // 512

    def body(p_ref, g_ref, w_ref, c_ref, s_ref, q_ref, cq_ref):
        pq = p_ref[...]
        r = lax.rsqrt(jnp.sum(pq * pq, axis=-1, keepdims=True) * (1.0 / QL) + EPS)
        cq = (pq * r * g_ref[...]).astype(BF)
        cq_ref[...] = cq
        q2 = jnp.dot(cq, w_ref[...], preferred_element_type=F32)
        cc = jnp.concatenate([c_ref[...]] * NH, axis=1)
        ss = jnp.concatenate([s_ref[...]] * NH, axis=1)
        q_ref[...] = (q2[:, :NH * HP] * cc + q2[:, NH * HP:] * ss).astype(BF)

    return pl.pallas_call(
        body, name=name, grid=(T // tm,),
        in_specs=[pl.BlockSpec((tm, 512), lambda i: (i, qcol)), _row(512),
                  pl.BlockSpec((512, 2 * NH * HP), lambda i: (0, 0)),
                  pl.BlockSpec((tm, HP), lambda i: (i, 0)), pl.BlockSpec((tm, HP), lambda i: (i, 0))],
        out_specs=[pl.BlockSpec((tm, NH * HP), lambda i: (i, 0)), pl.BlockSpec((tm, 512), lambda i: (i, 0))],
        out_shape=[jax.ShapeDtypeStruct((T, NH * HP), BF), jax.ShapeDtypeStruct((T, 512), BF)],
        compiler_params=pltpu.CompilerParams(dimension_semantics=("parallel",)),
    )(p, qg, wq2, cq_t, sq_t)


def _head_mask(h):
    lanes = lax.broadcasted_iota(jnp.int32, (1, 2 * DV), 1)
    return (lanes // DV) == (h % 2)


LOG2E = 1.4426950408889634


def _scores_pass(q, k_ref, s_scr, kc):
    m = None
    for c in range(TKV // kc):
        s = lax.dot_general(q, k_ref[c * kc:(c + 1) * kc, :], (((1,), (1,)), ((), ())),
                            preferred_element_type=F32) * (SCALE * LOG2E)
        s_scr[:, c * kc:(c + 1) * kc] = s
        mc = jnp.max(s, axis=-1, keepdims=True)
        m = mc if m is None else jnp.maximum(m, mc)
    return m


def attn_fwd(q, k, v, *, name, tq=512, kc=1152):
    def body(q_ref, k_ref, v_ref, o_ref, s_scr):
        h = pl.program_id(1)
        m = _scores_pass(q_ref[...], k_ref, s_scr, kc)
        l = jnp.zeros((tq, 1), F32)
        acc = jnp.zeros((tq, 2 * DV), F32)
        for c in range(TKV // kc):
            e = jnp.exp2(s_scr[:, c * kc:(c + 1) * kc] - m)
            l = l + jnp.sum(e, axis=-1, keepdims=True)
            acc = acc + jnp.dot(e.astype(BF), v_ref[c * kc:(c + 1) * kc, :], preferred_element_type=F32)
        o2 = jnp.where(_head_mask(h), acc * (1.0 / l), 0.0).astype(BF)

        @pl.when(h % 2 == 0)
        def _():
            o_ref[...] = o2

        @pl.when(h % 2 == 1)
        def _():
            o_ref[...] = o_ref[...] + o2

    return pl.pallas_call(
        body, name=name, grid=(T // tq, NH),
        in_specs=[pl.BlockSpec((tq, HP), lambda i, h: (i, h)), pl.BlockSpec((TKV, HP), lambda i, h: (0, h)),
                  pl.BlockSpec((TKV, 2 * DV), lambda i, h: (0, h // 2))],
        out_specs=pl.BlockSpec((tq, 2 * DV), lambda i, h: (i, h // 2)),
        out_shape=jax.ShapeDtypeStruct((T, NH * DV), BF),
        scratch_shapes=[pltpu.VMEM((tq, TKV), F32)],
        compiler_params=pltpu.CompilerParams(dimension_semantics=("parallel", "arbitrary")),
    )(q, k, v)


def _shift_dn(x):
    n = x.shape[0]
    rows = lax.broadcasted_iota(jnp.int32, (n, 1), 0)
    return jnp.where(rows == 0, 0.0, pltpu.roll(x, 1, axis=0))


def _shift_up(x):
    n = x.shape[0]
    rows = lax.broadcasted_iota(jnp.int32, (n, 1), 0)
    return jnp.where(rows == n - 1, 0.0, pltpu.roll(x, n - 1, axis=0))


def _conv(x, w_ref, b_ref):
    return b_ref[...] + _shift_dn(x) * w_ref[0:1, :] + x * w_ref[1:2, :] + _shift_up(x) * w_ref[2:3, :]


def _conv_t(dy, w_ref):
    return _shift_up(dy) * w_ref[0:1, :] + dy * w_ref[1:2, :] + _shift_dn(dy) * w_ref[2:3, :]


def _conv_wgrad(dw_ref, dy, x):
    dw_ref[0:1, :] = jnp.sum(dy * _shift_dn(x), axis=0, keepdims=True)
    dw_ref[1:2, :] = jnp.sum(dy * x, axis=0, keepdims=True)
    dw_ref[2:3, :] = jnp.sum(dy * _shift_up(x), axis=0, keepdims=True)


def convz(p, cw, cb, *, name):
    o0 = O_CV // (3 * CVB)

    def body(p_ref, w_ref, bias_ref, z_ref):
        xv, bv, cv = p_ref[:, 0:CVB], p_ref[:, CVB:2 * CVB], p_ref[:, 2 * CVB:3 * CVB]
        z_ref[...] = (bv * _conv(cv * xv, w_ref, bias_ref)).astype(BF)

    return pl.pallas_call(
        body, name=name, grid=(CONV // CVB,),
        in_specs=[pl.BlockSpec((T, 3 * CVB), lambda j: (0, o0 + j)), pl.BlockSpec((3, CVB), lambda j: (0, j)),
                  pl.BlockSpec((1, CVB), lambda j: (0, j))],
        out_specs=pl.BlockSpec((T, CVB), lambda j: (0, j)),
        out_shape=jax.ShapeDtypeStruct((T, CONV), BF),
        compiler_params=pltpu.CompilerParams(dimension_semantics=("parallel",)),
    )(p, cw, cb)


def gate_merge(p, ya, yc, *, name, tm=512):
    def body(ga_ref, gc_ref, ya_ref, yc_ref, o_ref):
        o_ref[...] = (jax.nn.sigmoid(ga_ref[...]) * ya_ref[...] + jax.nn.sigmoid(gc_ref[...]) * yc_ref[...]).astype(BF)

    blk = pl.BlockSpec((tm, D), lambda i: (i, 0))
    return pl.pallas_call(
        body, name=name, grid=(T // tm,),
        in_specs=[pl.BlockSpec((tm, D), lambda i: (i, O_GA // D)), pl.BlockSpec((tm, D), lambda i: (i, O_GC // D)), blk, blk],
        out_specs=blk, out_shape=jax.ShapeDtypeStruct((T, D), BF),
        compiler_params=pltpu.CompilerParams(dimension_semantics=("parallel",)),
    )(p, p, ya, yc)


CONV_HALO = 8
CONV_ROWS = 256


def _row_chunks(n, chunk, carry):
    carry = chunk(0, True, False, carry)
    carry = lax.fori_loop(1, n // CONV_ROWS - 1, lambda c, a: chunk(c * CONV_ROWS, False, False, a), carry)
    return chunk(n - CONV_ROWS, False, True, carry)


def _ext_rows(ref, r0, first, last):
    n, w = ref.shape
    zero = jnp.zeros((CONV_HALO, w), ref.dtype)
    if first:
        return jnp.concatenate([zero, ref[0:CONV_ROWS + CONV_HALO, :]], axis=0)
    if last:
        return jnp.concatenate([ref[n - CONV_ROWS - CONV_HALO:n, :], zero], axis=0)
    return ref[pl.ds(pl.multiple_of(r0 - CONV_HALO, 8), CONV_ROWS + 2 * CONV_HALO), :]


def _center_rows(r0, first, last):
    return slice(r0, r0 + CONV_ROWS) if (first or last) else pl.ds(pl.multiple_of(r0, 8), CONV_ROWS)


def _roll_dn(x):
    return pltpu.roll(x, 1, axis=0)


def _roll_up(x):
    return pltpu.roll(x, x.shape[0] - 1, axis=0)


_CTR = slice(CONV_HALO, CONV_HALO + CONV_ROWS)


def ffn_act(u0, cw, cb, *, name, tc=256):
    nb = DFF // tc

    def body(u_ref, wg_ref, wv_ref, bg_ref, bv_ref, f_ref):
        wg = [wg_ref[k:k + 1, :] for k in range(3)]
        wv = [wv_ref[k:k + 1, :] for k in range(3)]
        bg, bv = bg_ref[...], bv_ref[...]

        def chunk(r0, first, last, carry):
            xg, xv = _ext_rows(u_ref.at[0], r0, first, last), _ext_rows(u_ref.at[1], r0, first, last)
            ug = bg + _roll_dn(xg) * wg[0] + xg * wg[1] + _roll_up(xg) * wg[2]
            uv = bv + _roll_dn(xv) * wv[0] + xv * wv[1] + _roll_up(xv) * wv[2]
            f_ref[_center_rows(r0, first, last), :] = (ug * jax.nn.sigmoid(ug) * uv)[_CTR].astype(BF)
            return carry

        _row_chunks(T, chunk, 0)

    return pl.pallas_call(
        body, name=name, grid=(nb,),
        in_specs=[pl.BlockSpec((2, T, tc), lambda j: (0, 0, j)),
                  pl.BlockSpec((3, tc), lambda j: (0, j)), pl.BlockSpec((3, tc), lambda j: (0, nb + j)),
                  pl.BlockSpec((1, tc), lambda j: (0, j)), pl.BlockSpec((1, tc), lambda j: (0, nb + j))],
        out_specs=pl.BlockSpec((T, tc), lambda j: (0, j)),
        out_shape=jax.ShapeDtypeStruct((T, DFF), BF),
        compiler_params=pltpu.CompilerParams(dimension_semantics=("parallel",)),
    )(u0, cw, cw, cb, cb)


def rows_call(lead, ins, in_specs, out_shape, out_specs, fn, *, name, R, tm):
    tb, a_stack, tk = lead.get("tb", False), lead.get("a_stack", False), lead["tk"]
    K = 2 * lead["a"].shape[2] if a_stack else lead["a"].shape[1]
    nk = K // tk
    deps = [] if lead.get("dep") is None else [lead["dep"]]
    n_in = len(ins)

    def body(a_ref, b_ref, *refs):
        refs = refs[len(deps):]
        in_refs, out_refs, acc = refs[:n_in], refs[n_in:-1], refs[-1]
        i, k = pl.program_id(0), pl.program_id(1)
        part = lax.dot_general(a_ref[...].astype(BF), b_ref[...].astype(BF),
                               (((1,), (1 if tb else 0,)), ((), ())), preferred_element_type=F32)
        if nk == 1:
            fn(i, part, in_refs, out_refs)
            return

        @pl.when(k == 0)
        def _():
            acc[...] = part

        @pl.when(k > 0)
        def _():
            acc[...] += part

        @pl.when(k == nk - 1)
        def _():
            fn(i, acc[...], in_refs, out_refs)

    if a_stack:
        nhb = K // 2 // tk
        a_spec = pl.BlockSpec((None, tm, tk), lambda i, k: (k // nhb, i, k % nhb))
    else:
        a_spec = pl.BlockSpec((tm, tk), lambda i, k: (i, k))
    b_spec = pl.BlockSpec((D, tk), lambda i, k: (0, k)) if tb else pl.BlockSpec((tk, D), lambda i, k: (k, 0))
    return pl.pallas_call(
        body, name=name, grid=(R // tm, nk),
        in_specs=[a_spec, b_spec] + [pl.BlockSpec(memory_space=pl.ANY)] * len(deps) + list(in_specs),
        out_specs=out_specs, out_shape=out_shape,
        scratch_shapes=[pltpu.VMEM((tm, D) if nk > 1 else (8, 128), F32)],
        compiler_params=pltpu.CompilerParams(dimension_semantics=("arbitrary", "arbitrary")),
    )(lead["a"], lead["b"], *deps, *ins)


def _rblk(tm, w=D, col=0):
    return pl.BlockSpec((tm, w), lambda i, k: (i, col))


def _rrow(w=D):
    return pl.BlockSpec((1, w), lambda i, k: (0, 0))


def down_final(f, wdn, x1, g2, fg, tgt, *, name, tm=512):
    def fn(i, d, in_refs, out_refs):
        x1_ref, g2_ref, fg_ref, t_ref = in_refs
        d_ref, dx_ref, dd_ref, dfg_ref, loss_ref = out_refs
        d_ref[...] = d
        xv = x1_ref[...] + g2_ref[...] * d
        r = lax.rsqrt(jnp.mean(xv * xv, axis=-1, keepdims=True) + EPS)
        xh = xv * r
        diff = xh * fg_ref[...] - t_ref[...]
        part = 0.5 * jnp.sum(jnp.mean(diff * diff, axis=-1, keepdims=True), axis=0, keepdims=True)
        dy = diff * (1.0 / D)
        a = dy * fg_ref[...]
        dx = r * (a - xh * jnp.mean(a * xh, axis=-1, keepdims=True))
        dx_ref[...] = dx
        dd_ref[...] = (dx * g2_ref[...]).astype(BF)
        dfg = jnp.sum(dy * xh, axis=0, keepdims=True)

        @pl.when(i == 0)
        def _():
            dfg_ref[...] = dfg
            loss_ref[...] = jnp.broadcast_to(part, (1, 128))

        @pl.when(i > 0)
        def _():
            dfg_ref[...] += dfg
            loss_ref[...] += jnp.broadcast_to(part, (1, 128))

    blk = _rblk(tm)
    return rows_call(
        dict(a=f, b=wdn, tk=DFF), [x1, g2, fg, tgt], [blk, _rrow(), _rrow(), blk],
        [jax.ShapeDtypeStruct((T, D), F32), jax.ShapeDtypeStruct((T, D), F32), jax.ShapeDtypeStruct((T, D), BF),
         jax.ShapeDtypeStruct((1, D), F32), jax.ShapeDtypeStruct((1, 128), F32)],
        [blk, blk, blk, _rrow(), _rrow(128)], fn, name=name, R=T, tm=tm)


def oproj_resid(merged, wo, x, gate, g, sc, sh, *, name, tm=512):
    def fn(i, a, in_refs, out_refs):
        x_ref, gate_ref, g_ref, sc_ref, sh_ref = in_refs
        a_ref, x1_ref, h_ref = out_refs
        a_ref[...] = a
        xv = x_ref[...] + gate_ref[...] * a
        x1_ref[...] = xv
        r = lax.rsqrt(jnp.mean(xv * xv, axis=-1, keepdims=True) + EPS)
        h_ref[...] = ((xv * r * g_ref[...]) * (1.0 + sc_ref[...]) + sh_ref[...]).astype(BF)

    blk = _rblk(tm)
    return rows_call(
        dict(a=merged, b=wo, tk=D), [x, gate, g, sc, sh], [blk, _rrow(), _rrow(), _rrow(), _rrow()],
        [jax.ShapeDtypeStruct((T, D), F32), jax.ShapeDtypeStruct((T, D), F32), jax.ShapeDtypeStruct((T, D), BF)],
        [blk, blk, blk], fn, name=name, R=T, tm=tm)


def oproj_dx_gate_bwd(da, wo, p, ya, yc, *, name, tm=512):
    def fn(i, dm, in_refs, out_refs):
        ga_ref, gc_ref, ya_ref, yc_ref = in_refs
        dya_ref, dyc_ref, dp_ref = out_refs
        sa, sc_ = jax.nn.sigmoid(ga_ref[...]), jax.nn.sigmoid(gc_ref[...])
        dya_ref[...] = (dm * sa).astype(BF)
        dyc_ref[...] = (dm * sc_).astype(BF)
        dp_ref[:, 0:D] = (dm * ya_ref[...] * (sa * (1.0 - sa))).astype(BF)
        dp_ref[:, D:2 * D] = (dm * yc_ref[...] * (sc_ * (1.0 - sc_))).astype(BF)

    blk = _rblk(tm)
    sh = jax.ShapeDtypeStruct((T, D), BF)
    return rows_call(
        dict(a=da, b=wo, tb=True, tk=D), [p, p, ya, yc], [_rblk(tm, D, O_GA // D), _rblk(tm, D, O_GC // D), blk, blk],
        [sh, sh, jax.ShapeDtypeStruct((T, NIN), BF)], [blk, blk, _rblk(tm, 2 * D)], fn, name=name, R=T, tm=tm)


def normmod_bwd(x, dh, g, sc, dres, gsrc, gate, *, name, tm=512):
    R = x.shape[0]
    tm = min(tm, R)
    has_res = dres is not None
    fused = isinstance(dh, dict)
    if fused:
        tb, a_stack, tk = dh.get("tb", False), dh.get("a_stack", False), dh["tk"]
        K = 2 * dh["a"].shape[2] if a_stack else dh["a"].shape[1]
        nk = K // tk
        deps = [] if dh.get("dep") is None else [dh["dep"]]
        n_dh = 2 + len(deps)
    else:
        nk, n_dh = 1, 1

    def elementwise(i, dhv, x_ref, g_ref, sc_ref, res_refs, out_refs):
        xv = x_ref[...]
        r = lax.rsqrt(jnp.mean(xv * xv, axis=-1, keepdims=True) + EPS)
        xh = xv * r
        n = xh * g_ref[...]
        dn = dhv * (1.0 + sc_ref[...])
        a = dn * g_ref[...]
        rows = [jnp.sum(dhv, axis=0, keepdims=True), jnp.sum(dhv * n, axis=0, keepdims=True),
                jnp.sum(dn * xh, axis=0, keepdims=True)]
        if has_res:
            dres_ref, gsrc_ref, gate_ref = res_refs
            dx_ref, dxg_ref, st_ref = out_refs
            dr = dres_ref[...]
            dx = dr + r * (a - xh * jnp.mean(a * xh, axis=-1, keepdims=True))
            dx_ref[...] = dx
            dxg_ref[...] = (dx * gate_ref[...]).astype(BF)
            rows.append(jnp.sum(dr * gsrc_ref[...], axis=0, keepdims=True))
        else:
            st_ref, = out_refs
            rows.append(jnp.zeros((1, D), F32))

        @pl.when(i == 0)
        def _():
            for k, row in enumerate(rows):
                st_ref[k:k + 1, :] = row

        @pl.when(i > 0)
        def _():
            for k, row in enumerate(rows):
                st_ref[k:k + 1, :] += row

    def body(*refs):
        x_ref, dh_refs, g_ref, sc_ref = refs[0], refs[1:1 + n_dh], refs[1 + n_dh], refs[2 + n_dh]
        rest = refs[3 + n_dh:]
        res_refs, rest = (rest[:3], rest[3:]) if has_res else ((), rest)
        out_refs = rest[:3] if has_res else rest[:1]
        i = pl.program_id(0)
        if not fused:
            elementwise(i, dh_refs[0][...], x_ref, g_ref, sc_ref, res_refs, out_refs)
            return
        acc = rest[-1]
        k = pl.program_id(1)
        part = lax.dot_general(dh_refs[0][...].astype(BF), dh_refs[1][...].astype(BF),
                               (((1,), (1 if tb else 0,)), ((), ())), preferred_element_type=F32)

        @pl.when(k == 0)
        def _():
            acc[...] = part

        @pl.when(k > 0)
        def _():
            acc[...] += part

        @pl.when(k == nk - 1)
        def _():
            elementwise(i, acc[...], x_ref, g_ref, sc_ref, res_refs, out_refs)

    rowb = lambda w: pl.BlockSpec((1, w), lambda i, *k: (0, 0))
    blk = pl.BlockSpec((tm, D), lambda i, *k: (i, 0))
    st_spec = pl.BlockSpec((4, D), lambda i, *k: (0, 0))
    st_shape = jax.ShapeDtypeStruct((4, D), F32)
    if fused:
        if a_stack:
            nhb = K // 2 // tk
            a_spec = pl.BlockSpec((None, tm, tk), lambda i, k: (k // nhb, i, k % nhb))
        else:
            a_spec = pl.BlockSpec((tm, tk), lambda i, k: (i, k))
        b_spec = pl.BlockSpec((D, tk), lambda i, k: (0, k)) if tb else pl.BlockSpec((tk, D), lambda i, k: (k, 0))
        dh_specs = [a_spec, b_spec] + [pl.BlockSpec(memory_space=pl.ANY)] * len(deps)
        dh_args = [dh["a"], dh["b"]] + deps
        grid, sem = (R // tm, nk), ("arbitrary", "arbitrary")
        scratch = [pltpu.VMEM((tm, D), F32)]
    else:
        dh_specs, dh_args, grid, sem, scratch = [blk], [dh], (R // tm,), ("arbitrary",), []
    cp = pltpu.CompilerParams(dimension_semantics=sem)
    if has_res:
        return pl.pallas_call(
            body, name=name, grid=grid, in_specs=[blk] + dh_specs + [rowb(D), rowb(D), blk, blk, rowb(D)],
            out_specs=[blk, blk, st_spec], scratch_shapes=scratch,
            out_shape=[jax.ShapeDtypeStruct((R, D), F32), jax.ShapeDtypeStruct((R, D), BF), st_shape],
            compiler_params=cp,
        )(x, *dh_args, g, sc, dres, gsrc, gate)
    return pl.pallas_call(
        body, name=name, grid=grid, in_specs=[blk] + dh_specs + [rowb(D), rowb(D)],
        out_specs=st_spec, out_shape=st_shape, scratch_shapes=scratch, compiler_params=cp,
    )(x, *dh_args, g, sc)


def ffn_act_bwd(u0, df, cw, cb, *, name, tc=128):
    nb = DFF // tc

    def body(u_ref, df_ref, wg_ref, wv_ref, bg_ref, bv_ref, du_ref, dw_ref, db_ref):
        wg = [wg_ref[k:k + 1, :] for k in range(3)]
        wv = [wv_ref[k:k + 1, :] for k in range(3)]
        bg, bv = bg_ref[...], bv_ref[...]

        def chunk(r0, first, last, acc):
            xg, xv = _ext_rows(u_ref.at[0], r0, first, last), _ext_rows(u_ref.at[1], r0, first, last)
            dfe = _ext_rows(df_ref, r0, first, last)
            xg_d, xg_u, xv_d, xv_u = _roll_dn(xg), _roll_up(xg), _roll_dn(xv), _roll_up(xv)
            ug = bg + xg_d * wg[0] + xg * wg[1] + xg_u * wg[2]
            uv = bv + xv_d * wv[0] + xv * wv[1] + xv_u * wv[2]
            sig = jax.nn.sigmoid(ug)
            dug = dfe * uv * (sig * (1.0 + ug * (1.0 - sig)))
            duv = dfe * (ug * sig)
            rows = _center_rows(r0, first, last)
            du_ref[0, rows, :] = (_roll_up(dug) * wg[0] + dug * wg[1] + _roll_dn(dug) * wg[2])[_CTR].astype(BF)
            du_ref[1, rows, :] = (_roll_up(duv) * wv[0] + duv * wv[1] + _roll_dn(duv) * wv[2])[_CTR].astype(BF)
            terms = [dug * xg_d, dug * xg, dug * xg_u, dug, duv * xv_d, duv * xv, duv * xv_u, duv]
            return tuple(a + jnp.sum(t[_CTR], axis=0, keepdims=True) for a, t in zip(acc, terms))

        acc = _row_chunks(T, chunk, tuple(jnp.zeros((1, tc), F32) for _ in range(8)))
        for k in range(3):
            dw_ref[0, k:k + 1, :] = acc[k]
            dw_ref[1, k:k + 1, :] = acc[4 + k]
        db_ref[0] = acc[3]
        db_ref[1] = acc[7]

    lo = lambda r: pl.BlockSpec((r, tc), lambda j: (0, j))
    hi = lambda r: pl.BlockSpec((r, tc), lambda j: (0, nb + j))
    st = lambda r: pl.BlockSpec((2, r, tc), lambda j: (0, 0, j))
    return pl.pallas_call(
        body, name=name, grid=(nb,),
        in_specs=[st(T), lo(T), lo(3), hi(3), lo(1), hi(1)],
        out_specs=[st(T), st(3), st(1)],
        out_shape=[jax.ShapeDtypeStruct((2, T, DFF), BF), jax.ShapeDtypeStruct((2, 3, DFF), F32),
                   jax.ShapeDtypeStruct((2, 1, DFF), F32)],
        compiler_params=pltpu.CompilerParams(dimension_semantics=("parallel",)),
    )(u0, df, cw, cw, cb, cb)


def convz_bwd(p, dz, cw, cb, dp, *, name):
    o0 = O_CV // (3 * CVB)

    def body(p_ref, dz_ref, w_ref, bias_ref, dp_in, dp_ref, dw_ref, dbias_ref):
        xv, bv, cv = p_ref[:, 0:CVB], p_ref[:, CVB:2 * CVB], p_ref[:, 2 * CVB:3 * CVB]
        ci = cv * xv
        dwc = _conv(ci, w_ref, bias_ref)
        dzv = dz_ref[...]
        ddw = dzv * bv
        dci = _conv_t(ddw, w_ref)
        dp_ref[:, 0:CVB] = (dci * cv).astype(BF)
        dp_ref[:, CVB:2 * CVB] = (dzv * dwc).astype(BF)
        dp_ref[:, 2 * CVB:3 * CVB] = (dci * xv).astype(BF)
        _conv_wgrad(dw_ref, ddw, ci)
        dbias_ref[...] = jnp.sum(ddw, axis=0, keepdims=True)

    own = lambda r: pl.BlockSpec((r, CVB), lambda j: (0, j))
    return pl.pallas_call(
        body, name=name, grid=(CONV // CVB,),
        in_specs=[pl.BlockSpec((T, 3 * CVB), lambda j: (0, o0 + j)), own(T), own(3), own(1),
                  pl.BlockSpec(memory_space=pl.ANY)],
        out_specs=[pl.BlockSpec((T, 3 * CVB), lambda j: (0, o0 + j)), own(3), own(1)],
        out_shape=[jax.ShapeDtypeStruct((T, NIN), BF), jax.ShapeDtypeStruct((3, CONV), F32),
                   jax.ShapeDtypeStruct((1, CONV), F32)],
        input_output_aliases={4: 0},
        compiler_params=pltpu.CompilerParams(dimension_semantics=("parallel",)),
    )(p, dz, cw, cb, dp)


def attn_bwd(q, k, v, do, *, name, tq=1024, kc=768):
    NKC, KC = TKV // kc, kc

    def body(q_ref, k_ref, v_ref, do_ref, dq_ref, dk_ref, dv_ref, s_scr, dp_scr):
        h, i = pl.program_id(0), pl.program_id(1)

        @pl.when(i == 0)
        def _():
            dk_ref[...] = jnp.zeros_like(dk_ref)

        @pl.when((i == 0) & (h % 2 == 0))
        def _():
            dv_ref[...] = jnp.zeros_like(dv_ref)

        qv = q_ref[...]
        dom = jnp.where(_head_mask(h), do_ref[...], jnp.zeros_like(do_ref[...]))
        m = _scores_pass(qv, k_ref, s_scr, kc)
        l = jnp.zeros((tq, 1), F32)
        dsum = jnp.zeros((tq, 1), F32)
        for c in range(NKC):
            cols = slice(c * KC, (c + 1) * KC)
            e = jnp.exp2(s_scr[:, cols] - m)
            s_scr[:, cols] = e
            dp = lax.dot_general(dom, v_ref[cols, :], (((1,), (1,)), ((), ())), preferred_element_type=F32)
            dp_scr[:, cols] = dp
            l = l + jnp.sum(e, axis=-1, keepdims=True)
            dsum = dsum + jnp.sum(e * dp, axis=-1, keepdims=True)
        inv = 1.0 / l
        delta = dsum * inv
        dos = (dom.astype(F32) * inv).astype(BF)
        dq = jnp.zeros((tq, HP), F32)
        for c in range(NKC):
            cols = slice(c * KC, (c + 1) * KC)
            e = s_scr[:, cols]
            ds = (e * (dp_scr[:, cols] - delta) * (inv * SCALE)).astype(BF)
            dq = dq + jnp.dot(ds, k_ref[cols, :], preferred_element_type=F32)
            dk_ref[cols, :] += lax.dot_general(ds, qv, (((0,), (0,)), ((), ())), preferred_element_type=F32)
            dv_ref[cols, :] += lax.dot_general(e.astype(BF), dos, (((0,), (0,)), ((), ())), preferred_element_type=F32)
        dq_ref[...] = dq

    return pl.pallas_call(
        body, name=name, grid=(NH, T // tq),
        in_specs=[pl.BlockSpec((tq, HP), lambda h, i: (i, h)), pl.BlockSpec((TKV, HP), lambda h, i: (0, h)),
                  pl.BlockSpec((TKV, 2 * DV), lambda h, i: (0, h // 2)), pl.BlockSpec((tq, 2 * DV), lambda h, i: (i, h // 2))],
        out_specs=[pl.BlockSpec((tq, HP), lambda h, i: (i, h)), pl.BlockSpec((TKV, HP), lambda h, i: (0, h)),
                   pl.BlockSpec((TKV, 2 * DV), lambda h, i: (0, h // 2))],
        out_shape=[jax.ShapeDtypeStruct((T, NH * HP), F32), jax.ShapeDtypeStruct((TKV, NH * HP), F32),
                   jax.ShapeDtypeStruct((TKV, NH * DV), F32)],
        scratch_shapes=[pltpu.VMEM((tq, TKV), F32), pltpu.VMEM((tq, TKV), F32)],
        compiler_params=pltpu.CompilerParams(dimension_semantics=("arbitrary", "arbitrary")),
    )(q, k, v, do)


def qprep_bwd(p, dq, qg, wq2, cq_t, sq_t, dp, *, name, tm=256):
    qcol = O_Q // 512

    def body(p_ref, dq_ref, g_ref, w_ref, c_ref, s_ref, dp_in, dp_ref, dq2_ref, dg_ref):
        i = pl.program_id(0)
        dqv = dq_ref[...]
        cc = jnp.concatenate([c_ref[...]] * NH, axis=1)
        ss = jnp.concatenate([s_ref[...]] * NH, axis=1)
        dq2 = jnp.concatenate([dqv * cc, dqv * ss], axis=1).astype(BF)
        dq2_ref[...] = dq2
        dcq = lax.dot_general(dq2, w_ref[...], (((1,), (1,)), ((), ())), preferred_element_type=F32)
        pq = p_ref[...]
        r = lax.rsqrt(jnp.sum(pq * pq, axis=-1, keepdims=True) * (1.0 / QL) + EPS)
        xh = pq * r
        a = dcq * g_ref[...]
        dp_ref[...] = (r * (a - xh * (jnp.sum(a * xh, axis=-1, keepdims=True) * (1.0 / QL)))).astype(BF)
        dg = jnp.sum(dcq * xh, axis=0, keepdims=True)

        @pl.when(i == 0)
        def _():
            dg_ref[...] = dg

        @pl.when(i > 0)
        def _():
            dg_ref[...] += dg

    return pl.pallas_call(
        body, name=name, grid=(T // tm,),
        in_specs=[pl.BlockSpec((tm, 512), lambda i: (i, qcol)), pl.BlockSpec((tm, NH * HP), lambda i: (i, 0)), _row(512),
                  pl.BlockSpec((512, 2 * NH * HP), lambda i: (0, 0)),
                  pl.BlockSpec((tm, HP), lambda i: (i, 0)), pl.BlockSpec((tm, HP), lambda i: (i, 0)),
                  pl.BlockSpec(memory_space=pl.ANY)],
        out_specs=[pl.BlockSpec((tm, 512), lambda i: (i, qcol)), pl.BlockSpec((tm, 2 * NH * HP), lambda i: (i, 0)), _row(512)],
        out_shape=[jax.ShapeDtypeStruct((T, NIN), BF), jax.ShapeDtypeStruct((T, 2 * NH * HP), BF),
                   jax.ShapeDtypeStruct((1, 512), F32)],
        input_output_aliases={6: 0},
        compiler_params=pltpu.CompilerParams(dimension_semantics=("arbitrary",)),
    )(p, dq, qg, wq2, cq_t, sq_t, dp)


def kvprep_bwd(pc, p, dk, dv, kvg, wkv2, ck, sk, dp, *, name, tm=256):
    assert tm == TC
    nb = TKV // tm
    kvcol = O_KV // 512

    def body(pc_ref, p_ref, dk_ref, dv_ref, g_ref, w_ref, ck_ref, sk_ref, dp_in, dp_ref, dpc_ref, dkv2_ref, dg_ref):
        i = pl.program_id(0)
        t = jnp.where(i == NLAT, pc_ref[...], p_ref[...])
        pk = t[:, :KVL]
        r = lax.rsqrt(jnp.mean(pk * pk, axis=-1, keepdims=True) + EPS)
        xh = pk * r
        dkv = dk_ref[...]
        dkv2 = jnp.concatenate([dkv, dv_ref[...]], axis=1).astype(BF)
        dkv2_ref[...] = dkv2
        dckv = lax.dot_general(dkv2, w_ref[...], (((1,), (1,)), ((), ())), preferred_element_type=F32)
        a = dckv * g_ref[...]
        dpk = r * (a - xh * jnp.mean(a * xh, axis=-1, keepdims=True))
        dkr = dkv[:, 0:HP]
        for hh in range(1, NH):
            dkr = dkr + dkv[:, hh * HP:(hh + 1) * HP]
        res = jnp.concatenate([dpk, dkr * ck_ref[...], dkr * sk_ref[...]], axis=1).astype(BF)
        dg = jnp.sum(dckv * xh, axis=0, keepdims=True)

        @pl.when(i == 0)
        def _():
            dg_ref[...] = dg

        @pl.when(i > 0)
        def _():
            dg_ref[...] += dg

        @pl.when(i < NLAT)
        def _():
            dp_ref[...] = res

        @pl.when(i == NLAT)
        def _():
            dpc_ref[...] = res

    rb = lambda w: pl.BlockSpec((tm, w), lambda i: (i, 0))
    return pl.pallas_call(
        body, name=name, grid=(nb,),
        in_specs=[pl.BlockSpec((tm, 512), lambda i: (0, 0)),
                  pl.BlockSpec((tm, 512), lambda i: (jnp.minimum(i, NLAT - 1), kvcol)),
                  rb(NH * HP), rb(NH * DV), _row(KVL), pl.BlockSpec((KVL, NH * HP + NH * DV), lambda i: (0, 0)),
                  rb(HP), rb(HP), pl.BlockSpec(memory_space=pl.ANY)],
        out_specs=[pl.BlockSpec((tm, 512), lambda i: (jnp.minimum(i, NLAT - 1), kvcol)),
                   pl.BlockSpec((tm, 512), lambda i: (0, 0)), rb(NH * HP + NH * DV), _row(KVL)],
        out_shape=[jax.ShapeDtypeStruct((T, NIN), BF), jax.ShapeDtypeStruct((TC, 512), BF),
                   jax.ShapeDtypeStruct((TKV, NH * HP + NH * DV), BF), jax.ShapeDtypeStruct((1, KVL), F32)],
        input_output_aliases={8: 0},
        compiler_params=pltpu.CompilerParams(dimension_semantics=("arbitrary",)),
    )(pc, p, dk, dv, kvg, wkv2, ck, sk, dp)


def _pieces(src, width, n):
    out, c = [], src
    while c < src + width:
        k = c // n
        w = min(src + width, (k + 1) * n) - c
        out.append((k, c - k * n, c - src, w))
        c += w
    return out


def _win_moves():
    mv = [(2208, 1024, O_GA), (3232, 1024, O_GC), (0, KVL, O_KV), (256, DR, O_KV + KVL + DN), (288, QL, O_Q)]
    mv += [(256 + _swap_start(g), 8, O_KV + KVL + HP + DN + 8 * g) for g in range(4)]
    for j in range(CONV // CVB):
        base = O_CV + 3 * CVB * j
        mv += [(672 + CVB * j, CVB, base), (1184 + CVB * j, CVB, base + CVB), (1696 + CVB * j, CVB, base + 2 * CVB)]
    return mv


_WIN_ZERO = [(O_KV + KVL, DN), (O_KV + KVL + DN + DR, HP - DN - DR), (O_KV + KVL + HP, DN),
             (O_KV + KVL + HP + DN + DR, HP - DN - DR), (O_Q + QL, 512 - QL)]


def build_win(g, *, name, tm=256):
    def body(g_ref, o_ref):
        for src, w, dst in _win_moves():
            for k, a, off, pw in _pieces(src, w, SH_IN):
                o_ref[:, dst + off:dst + off + pw] = g_ref[k, :, a:a + pw]
        for c0, w in _WIN_ZERO:
            o_ref[:, c0:c0 + w] = jnp.zeros((tm, w), o_ref.dtype)

    return pl.pallas_call(
        body, name=name, grid=(D // tm,), in_specs=[pl.BlockSpec((NDEV, tm, SH_IN), lambda i: (0, i, 0))],
        out_specs=pl.BlockSpec((tm, NIN), lambda i: (i, 0)), out_shape=jax.ShapeDtypeStruct((D, NIN), g.dtype),
        compiler_params=pltpu.CompilerParams(dimension_semantics=("parallel",)),
    )(g)


def shard_win_grad(dwt, dwct, *, name, tc=256):
    def body(dw_ref, dwc_ref, o_ref, kvs):
        kvs[...] = dw_ref[O_KV:O_KV + 512, :] + dwc_ref[...]

        def src(row, w):
            if O_KV <= row < O_KV + 512:
                return kvs[row - O_KV:row - O_KV + w, :]
            return dw_ref[row:row + w, :]

        for s, w, dst in _win_moves():
            if w == 8 or s == 256:
                continue
            for k, a, off, pw in _pieces(s, w, SH_IN):
                o_ref[k, a:a + pw, :] = src(dst + off, pw).astype(o_ref.dtype)
        for g in range(4):
            val = src(O_KV + KVL + DN + 8 * g, 8) + src(O_KV + KVL + HP + DN + _swap_start(g), 8)
            o_ref[0, 256 + 8 * g:256 + 8 * g + 8, :] = val.astype(o_ref.dtype)

    return pl.pallas_call(
        body, name=name, grid=(D // tc,),
        in_specs=[pl.BlockSpec((NIN, tc), lambda j: (0, j)), pl.BlockSpec((512, tc), lambda j: (0, j))],
        out_specs=pl.BlockSpec((NDEV, SH_IN, tc), lambda j: (0, 0, j)),
        out_shape=jax.ShapeDtypeStruct((NDEV, SH_IN, D), BF),
        scratch_shapes=[pltpu.VMEM((512, tc), F32)],
        compiler_params=pltpu.CompilerParams(dimension_semantics=("parallel",)),
    )(dwt, dwct)


def build_wq_wkv(gq, gkv, *, name):
    def body(gq_ref, gkv_ref, q_ref, kv_ref):
        q_ref[...] = jnp.zeros_like(q_ref)
        kv_ref[...] = jnp.zeros_like(kv_ref)
        for h in range(NH):
            q_ref[0:QL, h * HP:h * HP + DN + DR] = gq_ref[h]
            for g in range(4):
                c0 = NH * HP + h * HP + DN + 8 * g
                q_ref[0:QL, c0:c0 + 8] = gq_ref[h, :, DN + _swap_start(g):DN + _swap_start(g) + 8]
            kv_ref[:, h * HP:h * HP + DN] = gkv_ref[h, :, 0:DN]
            kv_ref[:, NH * HP + h * DV:NH * HP + (h + 1) * DV] = gkv_ref[h, :, DN:DN + DV]

    vm = pl.BlockSpec(memory_space=pltpu.VMEM)
    return pl.pallas_call(
        body, name=name, in_specs=[vm, vm], out_specs=[vm, vm],
        out_shape=[jax.ShapeDtypeStruct((512, 2 * NH * HP), gq.dtype), jax.ShapeDtypeStruct((KVL, NH * HP + NH * DV), gq.dtype)],
    )(gq, gkv)


def shard_wq_wkv_grad(dwq2, dwkv2, *, name):
    def body(q_ref, kv_ref, gq_ref, gkv_ref):
        for h in range(NH):
            gq_ref[h, :, 0:DN] = q_ref[0:QL, h * HP:h * HP + DN].astype(BF)
            for g in range(4):
                a = q_ref[0:QL, h * HP + DN + 8 * g:h * HP + DN + 8 * g + 8]
                c0 = NH * HP + h * HP + DN + _swap_start(g)
                gq_ref[h, :, DN + 8 * g:DN + 8 * g + 8] = (a + q_ref[0:QL, c0:c0 + 8]).astype(BF)
            gkv_ref[h, :, 0:DN] = kv_ref[:, h * HP:h * HP + DN].astype(BF)
            gkv_ref[h, :, DN:DN + DV] = kv_ref[:, NH * HP + h * DV:NH * HP + (h + 1) * DV].astype(BF)

    vm = pl.BlockSpec(memory_space=pltpu.VMEM)
    return pl.pallas_call(
        body, name=name, in_specs=[vm, vm], out_specs=[vm, vm],
        out_shape=[jax.ShapeDtypeStruct((NDEV, QL, (DN + DR)), BF), jax.ShapeDtypeStruct((NDEV, KVL, DN + DV), BF)],
    )(dwq2, dwkv2)


def unshard_cols(g, *, name, tm=256):
    _, K, n = g.shape
    tm = _pick(K, tm, 16)

    def body(g_ref, o_ref):
        for k in range(NDEV):
            o_ref[:, k * n:(k + 1) * n] = g_ref[k]

    return pl.pallas_call(
        body, name=name, grid=(K // tm,), in_specs=[pl.BlockSpec((NDEV, tm, n), lambda i: (0, i, 0))],
        out_specs=pl.BlockSpec((tm, NDEV * n), lambda i: (i, 0)), out_shape=jax.ShapeDtypeStruct((K, NDEV * n), g.dtype),
        compiler_params=pltpu.CompilerParams(dimension_semantics=("parallel",)),
    )(g)


def shard_cols(w, *, name, tm=256):
    K, n8 = w.shape
    n = n8 // NDEV
    tm = _pick(K, tm, 16)

    def body(w_ref, o_ref):
        for k in range(NDEV):
            o_ref[k] = w_ref[:, k * n:(k + 1) * n]

    return pl.pallas_call(
        body, name=name, grid=(K // tm,), in_specs=[pl.BlockSpec((tm, n8), lambda i: (i, 0))],
        out_specs=pl.BlockSpec((NDEV, tm, n), lambda i: (0, i, 0)), out_shape=jax.ShapeDtypeStruct((NDEV, K, n), w.dtype),
        compiler_params=pltpu.CompilerParams(dimension_semantics=("parallel",)),
    )(w)


def _rope_tables():
    t = np.arange(T)
    row = (t // GRID_W).astype(np.float32)
    col = (t % GRID_W).astype(np.float32)
    axis_dim = DR // 2
    inv = (np.float32(ROPE_THETA) ** (-np.arange(0, axis_dim, 2, dtype=np.float32) / np.float32(axis_dim))).astype(np.float32)
    ar, ac = (row[:, None] * inv).astype(np.float32), (col[:, None] * inv).astype(np.float32)
    cosv = np.concatenate([np.cos(ar), np.cos(ar), np.cos(ac), np.cos(ac)], axis=1).astype(np.float32)
    sinv = np.concatenate([-np.sin(ar), np.sin(ar), -np.sin(ac), np.sin(ac)], axis=1).astype(np.float32)
    ck = np.zeros((TKV, HP), np.float32)
    sk = np.zeros((TKV, HP), np.float32)
    ck[T:, DN:DN + DR] = 1.0
    ck[:T, DN:DN + DR] = cosv
    sk[:T, DN:DN + DR] = sinv
    cq = np.zeros((T, HP), np.float32)
    cq[:, :DN] = 1.0
    cq[:, DN:DN + DR] = cosv
    return jnp.asarray(ck), jnp.asarray(sk), jnp.asarray(cq), jnp.asarray(sk[:T])


def _local_step(x, ctx, tgt, mod_lat, mod_ctx, n1g, qg, kvg, n2g, fg, conv_w, conv_b, ffn_w, ffn_b, get_w, put_g, dep0):
    sh1, sc1, g1, sh2, sc2, g2 = [mod_lat[:, i * D:(i + 1) * D] for i in range(6)]
    csh1, csc1 = mod_ctx[:, 0:D], mod_ctx[:, D:2 * D]
    ck, sk, cq_t, sq_t = _rope_tables()
    qg_p = jnp.pad(qg, ((0, 0), (0, 512 - QL)))

    hcat = normmod_cat(ctx, x, n1g, csc1, csh1, sc1, sh1, dep0, name="normmod1")
    win = get_w("in", hcat)
    p = mm(hcat, win, M=T, tn=768, name="in_proj")
    pc = mm(hcat, win, M=TC, N=512, a_off=(T, 0), b_off=(0, O_KV), name="in_proj_ctx")
    wq2, wkv2, wao, wco, wo = get_w("mid", p)
    kh, vh, ckv = kvprep(pc, p, kvg, wkv2, ck, sk, name="kvprep")
    qr, cq = qprep(p, qg_p, wq2, cq_t, sq_t, name="qprep")
    o = attn_fwd(qr, kh, vh, name="attn_fwd")
    z = convz(p, conv_w, conv_b, name="convz")
    ya = mm(o, wao, name="attn_out")
    yc = mm(z, wco, name="conv_out")
    merged = gate_merge(p, ya, yc, name="gate_merge")
    a_out, x1, h2 = oproj_resid(merged, wo, x, g1, n2g, sc2, sh2, name="o_proj_resid_normmod2")
    wup, wdn = get_w("ffn", h2)
    u0 = mm(h2, wup, tb=True, o_stack=True, tn=1408, name="up_proj")
    f = ffn_act(u0, ffn_w, ffn_b, name="ffn_act")
    dn, dx2, dd, dfg, loss = down_final(f, wdn, x1, g2, fg, tgt, name="down_proj_final_loss")

    df = mm(dd, wdn, tb=True, tn=1408, name="down_proj_dx")
    dwdn = mm(f, dd, ta=True, out_dtype=BF, tm=1408, name="down_proj_dw")
    du0, dffn_w, dffn_b = ffn_act_bwd(u0, df, ffn_w, ffn_b, name="ffn_act_bwd")
    dwup = mm(du0, h2, ta=True, a_stack=True, out_dtype=BF, tm=1408, name="up_proj_dw")
    tok = put_g("ffn", dict(dwup=dwup, dwdn=dwdn))
    dx1, da, st2 = normmod_bwd(x1, dict(a=du0, b=wup, a_stack=True, tk=1408, dep=tok), n2g, sc2, dx2, dn, g1,
                               name="up_proj_dx_normmod2_bwd")

    dwo = mm(merged, da, ta=True, out_dtype=BF, tn=512, name="o_proj_dw")
    dya, dyc, dp = oproj_dx_gate_bwd(da, wo, p, ya, yc, name="o_proj_dx_gate_merge_bwd")
    do = mm(dya, wao, tb=True, out_dtype=BF, name="attn_out_dx")
    dwao = mm(o, dya, ta=True, out_dtype=BF, tn=512, name="attn_out_dw")
    dwco = mm(z, dyc, ta=True, out_dtype=BF, tn=512, name="conv_out_dw")
    tok = put_g("mid", dict(dwao=dwao, dwco=dwco, dwo=dwo))
    dz = mm(dyc, wco, tb=True, dep=tok, name="conv_out_dx")
    dp, dconv_w, dconv_b = convz_bwd(p, dz, conv_w, conv_b, dp, name="convz_bwd")
    dq, dk, dv = attn_bwd(qr, kh, vh, do, name="attn_bwd")
    dp, dq2, dqg = qprep_bwd(p, dq, qg_p, wq2, cq_t, sq_t, dp, name="qprep_bwd")
    dwq2 = mm(cq, dq2, ta=True, name="q_up_dw")
    dp, dpc, dkv2, dkvg = kvprep_bwd(pc, p, dk, dv, kvg, wkv2, ck, sk, dp, name="kvprep_bwd")
    dwkv2 = mm(ckv, dkv2, ta=True, name="kv_up_dw")
    tok = put_g("qkv", dict(dwq2=dwq2, dwkv2=dwkv2))

    dwin = mm(dp, hcat, ta=True, K=T, tm=768, dep=tok, name="in_proj_dw")
    dwin_c = mm(dpc, hcat, ta=True, K=TC, b_off=(T, 0), name="in_proj_ctx_dw")
    tok = put_g("in", dict(dwin=dwin, dwin_c=dwin_c))
    dhc = mm(dpc, win, tb=True, N=D, K=512, b_off=(0, O_KV), name="in_proj_ctx_dx")
    dx, _, st1 = normmod_bwd(x, dict(a=dp, b=win, tb=True, tk=1536, dep=tok), n1g, sc1, dx1, a_out, g1,
                             name="in_proj_dx_normmod1_bwd")
    stc = normmod_bwd(ctx, dhc, n1g, csc1, None, None, None, name="normmod1_ctx_bwd")

    zrow = jnp.zeros((1, D), F32)
    dmod_lat = jnp.concatenate([st1[0:1], st1[1:2], st1[3:4], st2[0:1], st2[1:2], st2[3:4]], axis=1)
    dmod_ctx = jnp.concatenate([stc[0:1], stc[1:2], zrow, zrow, zrow, zrow], axis=1)
    return dict(
        loss=loss, dx=dx, dmod_lat=dmod_lat, dmod_ctx=dmod_ctx,
        dn1g=st1[2:3] + stc[2:3], dqg=dqg, dkvg=dkvg, dn2g=st2[2:3], dfg=dfg,
        dconv_w=dconv_w, dconv_b=dconv_b, dffn_w=dffn_w, dffn_b=dffn_b)


def _me():
    x, y, c = lax.axis_index("x"), lax.axis_index("y"), lax.axis_index("c")
    return x, y, c, 4 * x + 2 * y + c


def _peer(x, y, c, k):
    px = 1 - x if k & 4 else x
    py = 1 - y if k & 2 else y
    pc = 1 - c if k & 1 else c
    return (px, py, pc), 4 * px + 2 * py + pc


def _exchange_tiles(src_of_peer, buf, send_sem, recv_sem):
    x, y, c, me = _me()
    for k in range(1, NDEV):
        dev, lin = _peer(x, y, c, k)
        pltpu.make_async_remote_copy(src_ref=src_of_peer(lin), dst_ref=buf.at[me], send_sem=send_sem, recv_sem=recv_sem,
                                     device_id=dev, device_id_type=MESH).start()
    seven = buf.at[pl.ds(0, NDEV - 1)]
    pltpu.make_async_remote_copy(src_ref=seven, dst_ref=seven, send_sem=send_sem, recv_sem=recv_sem,
                                 device_id=(x, y, c), device_id_type=MESH).wait()


def _silu(z):
    return z * jax.nn.sigmoid(z)


def ada_fwd(c, c_ctx, ffn_w, conv_w, w_shard, b_shard, deps, *, name):
    nsh = w_shard.shape[1]
    deps = [d for d in deps if d is not None]

    def body(c_ref, cc_ref, fw_ref, cw_ref, w_ref, b_ref, *rest):
        s_ref, m_ref, mine, res, sems = rest[len(deps):]
        x, y, c, me = _me()
        mine[0:1, :] = _silu(c_ref[...])
        mine[1:2, :] = _silu(cc_ref[...])
        mine[2:5, :] = fw_ref[...]
        mine[5:8, :] = cw_ref[...]
        s_ref[me] = mine[...]
        _exchange_tiles(lambda lin: mine, s_ref, sems.at[0], sems.at[1])
        sall = s_ref[...].reshape(NDEV * 8, D).astype(BF)
        r = jnp.dot(sall, w_ref[...].astype(BF), preferred_element_type=F32) + b_ref[...]
        res[...] = r.reshape(NDEV, 8, nsh)
        m_ref[me] = res[me]
        _exchange_tiles(lambda lin: res.at[lin], m_ref, sems.at[2], sems.at[3])

    vm = pl.BlockSpec(memory_space=pltpu.VMEM)
    return pl.pallas_call(
        body, name=name, in_specs=[vm] * 6 + [pl.BlockSpec(memory_space=pl.ANY)] * len(deps), out_specs=[vm, vm],
        out_shape=[jax.ShapeDtypeStruct((NDEV, 8, D), F32), jax.ShapeDtypeStruct((NDEV, 8, nsh), F32)],
        scratch_shapes=[pltpu.VMEM((8, D), F32), pltpu.VMEM((NDEV, 8, nsh), F32), pltpu.SemaphoreType.DMA((4,))],
    )(c, c_ctx, ffn_w, conv_w, w_shard, b_shard, *deps)


P_DML, P_DMC, P_N1, P_QG, P_KVG, P_CB, P_N2, P_FB, P_FG, P_CW, P_FW, P_LOSS, P_ROWS = 0, 6, 12, 13, 14, 15, 16, 17, 23, 24, 27, 45, 48
FROWS = 3


def sync_small(r, deps, *, name):
    ins = [r["dmod_lat"], r["dmod_ctx"], r["dn1g"], r["dqg"], r["dkvg"], r["dconv_b"], r["dn2g"], r["dffn_b"], r["dfg"],
           r["dconv_w"], r["dffn_w"], r["loss"]]

    def put_wide(p, row0, row, n):
        for j in range(-(-n // D)):
            w = min(D, n - j * D)
            p[row0 + j:row0 + j + 1, 0:w] = row[:, j * D:j * D + w]

    def body(dml, dmc, n1, qg, kvg, cb, n2, fb, fg, cw, fw, loss, *rest):
        a_ref, sum_ref, p, sems = rest[len(deps):]
        x, y, c, me = _me()
        p[...] = jnp.zeros_like(p)
        put_wide(p, P_DML, dml, 6 * D)
        put_wide(p, P_DMC, dmc, 6 * D)
        put_wide(p, P_N1, n1, D)
        put_wide(p, P_QG, qg, 512)
        put_wide(p, P_KVG, kvg, KVL)
        put_wide(p, P_CB, cb, CONV)
        put_wide(p, P_N2, n2, D)
        put_wide(p, P_FG, fg, D)
        put_wide(p, P_LOSS, loss, 128)
        for s in range(2):
            put_wide(p, P_FB + FROWS * s, fb.at[s], DFF)
        for k in range(3):
            put_wide(p, P_CW + k, cw.at[k:k + 1], CONV)
            for s in range(2):
                put_wide(p, P_FW + FROWS * (2 * k + s), fw.at[s, k:k + 1], DFF)
        a_ref[me] = p[...]
        _exchange_tiles(lambda lin: p, a_ref, sems.at[0], sems.at[1])
        acc = a_ref[0]
        for k in range(1, NDEV):
            acc = acc + a_ref[k]
        sum_ref[...] = acc

    vm = pl.BlockSpec(memory_space=pltpu.VMEM)
    return pl.pallas_call(
        body, name=name, in_specs=[vm] * len(ins) + [pl.BlockSpec(memory_space=pl.ANY)] * len(deps), out_specs=[vm, vm],
        out_shape=[jax.ShapeDtypeStruct((NDEV, P_ROWS, D), F32), jax.ShapeDtypeStruct((P_ROWS, D), F32)],
        scratch_shapes=[pltpu.VMEM((P_ROWS, D), F32), pltpu.SemaphoreType.DMA((2,))],
    )(*ins, *deps)


def ada_bwd(s_all, dml, dmc, w_shard, c_ctx, *, name):
    nsh = w_shard.shape[1]

    def body(s_ref, dml_ref, dmc_ref, w_ref, c_ref, dw_ref, gc_ref, s16, dm16, part, buf, sems):
        x, y, c, me = _me()
        s16[...] = jnp.zeros_like(s16)
        dm16[...] = jnp.zeros_like(dm16)
        for k in range(NDEV):
            s16[k:k + 1, :] = s_ref[k, 0:1, :]
        s16[8:9, :] = s_ref[0, 1:2, :]
        dm16[0:8, :] = dml_ref[...]
        dm16[8:9, :] = dmc_ref[...]
        dw_ref[...] = lax.dot_general(s16[...].astype(BF), dm16[...].astype(BF), (((0,), (0,)), ((), ())),
                                      preferred_element_type=F32)
        part[...] = lax.dot_general(dm16[8:16, :].astype(BF), w_ref[...].astype(BF), (((1,), (1,)), ((), ())),
                                    preferred_element_type=F32)
        buf[me] = part[...]
        _exchange_tiles(lambda lin: part, buf, sems.at[0], sems.at[1])
        acc = buf[0]
        for k in range(1, NDEV):
            acc = acc + buf[k]
        z = c_ref[...]
        sg = jax.nn.sigmoid(z)
        gc_ref[...] = acc * (sg * (1.0 + z * (1.0 - sg)))

    vm = pl.BlockSpec(memory_space=pltpu.VMEM)
    return pl.pallas_call(
        body, name=name, in_specs=[vm] * 5, out_specs=[vm, vm],
        out_shape=[jax.ShapeDtypeStruct((D, nsh), F32), jax.ShapeDtypeStruct((8, D), F32)],
        scratch_shapes=[pltpu.VMEM((16, D), F32), pltpu.VMEM((16, nsh), F32), pltpu.VMEM((8, D), F32),
                        pltpu.VMEM((NDEV, 8, D), F32), pltpu.SemaphoreType.DMA((2,))],
    )(s_all, dml, dmc, w_shard, c_ctx)


HBM_SPEC = pl.BlockSpec(memory_space=pltpu.HBM)
SEM_SPEC = pl.BlockSpec(memory_space=pltpu.SEMAPHORE)
EFFECT = pltpu.SideEffectType.DATAFLOW_SIDE_EFFECTING


ALL_PEERS = tuple(range(1, NDEV))
FIRST_HOP = (1, 2, 4, 6)
RELAY = (2, 4, 6)


def _exchange_copies(srcs, lands, send, recv, per_peer, peers):
    x, y, c, me = _me()
    n = len(peers)
    cps = []
    for t in range(len(srcs)):
        for j, k in enumerate(peers):
            dev, lin = _peer(x, y, c, k)
            cps.append(pltpu.make_async_remote_copy(
                src_ref=srcs[t].at[lin] if per_peer else srcs[t], dst_ref=lands[t].at[me],
                send_sem=send.at[n * t + j], recv_sem=recv.at[n * t + j], device_id=dev, device_id_type=MESH))
    return cps


def _relay_copies(lands, send, recv):
    x, y, c, me = _me()
    n = len(RELAY)
    cps = []
    for t in range(len(lands)):
        for j, k in enumerate(RELAY):
            slot = lands[t].at[_peer(x, y, c, k)[1]]
            cps.append(pltpu.make_async_remote_copy(
                src_ref=slot, dst_ref=slot, send_sem=send.at[n * t + j], recv_sem=recv.at[n * t + j],
                device_id=(x, y, 1 - c), device_id_type=MESH))
    return cps


def _own_copies(srcs, lands, own, per_peer):
    me = _me()[3]
    return [pltpu.make_async_copy(srcs[t].at[me] if per_peer else srcs[t], lands[t].at[me], own.at[t])
            for t in range(len(srcs))]


def exchange_start(srcs, *, per_peer, name, dep=None, peers=ALL_PEERS):
    nt = len(srcs)
    ns = len(peers) * nt
    land_shapes = [(a.shape if per_peer else (NDEV,) + a.shape) for a in srcs]
    deps = [] if dep is None else [dep]

    def body(*refs):
        src, land = refs[:nt], refs[nt:2 * nt]
        send, recv, own = refs[2 * nt + len(deps):2 * nt + len(deps) + 3]
        for cp in _exchange_copies(src, land, send, recv, per_peer, peers) + _own_copies(src, land, own, per_peer):
            cp.start()
        refs[-1][...] = jnp.zeros_like(refs[-1])

    hb = lambda a: pltpu.with_memory_space_constraint(a, pltpu.HBM)
    outs = pl.pallas_call(
        body, name=name,
        out_shape=(pltpu.SemaphoreType.DMA((ns,)), pltpu.SemaphoreType.DMA((ns,)), pltpu.SemaphoreType.DMA((nt,)),
                   *[pltpu.HBM(a.shape, a.dtype) for a in srcs], *[pltpu.HBM(s, a.dtype) for s, a in zip(land_shapes, srcs)],
                   jax.ShapeDtypeStruct((8, 128), F32)),
        in_specs=[HBM_SPEC] * (2 * nt) + [pl.BlockSpec(memory_space=pl.ANY)] * len(deps),
        out_specs=(SEM_SPEC, SEM_SPEC, SEM_SPEC, *([HBM_SPEC] * (2 * nt)), pl.BlockSpec(memory_space=pltpu.VMEM)),
        input_output_aliases={i: 3 + i for i in range(2 * nt)},
        compiler_params=pltpu.CompilerParams(has_side_effects=EFFECT),
    )(*[hb(a) for a in srcs], *[hb(lax.empty(s, a.dtype)) for s, a in zip(land_shapes, srcs)], *deps)
    return dict(send=outs[0], recv=outs[1], own=outs[2], src=list(outs[3:3 + nt]), land=list(outs[3 + nt:3 + 2 * nt]),
                token=outs[-1], per_peer=per_peer, peers=peers)


def exchange_wait(h, after, *, name):
    nt = len(h["src"])
    per_peer, peers = h["per_peer"], h["peers"]

    def body(*refs):
        src, land, send, recv, own = refs[:nt], refs[nt:2 * nt], refs[2 * nt], refs[2 * nt + 1], refs[2 * nt + 2]
        for cp in _exchange_copies(src, land, send, recv, per_peer, peers):
            cp.wait_send()
            cp.wait_recv()
        for cp in _own_copies(src, land, own, per_peer):
            cp.wait()

    outs = pl.pallas_call(
        body, name=name,
        out_shape=(*[pltpu.HBM(a.shape, a.dtype) for a in h["src"]], *[pltpu.HBM(a.shape, a.dtype) for a in h["land"]]),
        in_specs=[HBM_SPEC] * (2 * nt) + [SEM_SPEC, SEM_SPEC, SEM_SPEC, pl.BlockSpec(memory_space=pl.ANY)],
        out_specs=tuple([HBM_SPEC] * (2 * nt)),
        input_output_aliases={i: i for i in range(2 * nt)},
        compiler_params=pltpu.CompilerParams(has_side_effects=EFFECT),
    )(*h["src"], *h["land"], h["send"], h["recv"], h["own"], after)
    return list(outs[nt:])


def relay_start(lands, *, name):
    nt = len(lands)
    ns = len(RELAY) * nt

    def body(*refs):
        for cp in _relay_copies(refs[:nt], refs[nt], refs[nt + 1]):
            cp.start()

    outs = pl.pallas_call(
        body, name=name,
        out_shape=(pltpu.SemaphoreType.DMA((ns,)), pltpu.SemaphoreType.DMA((ns,)),
                   *[pltpu.HBM(a.shape, a.dtype) for a in lands]),
        in_specs=[HBM_SPEC] * nt, out_specs=(SEM_SPEC, SEM_SPEC, *([HBM_SPEC] * nt)),
        input_output_aliases={i: 2 + i for i in range(nt)},
        compiler_params=pltpu.CompilerParams(has_side_effects=EFFECT),
    )(*lands)
    return dict(send=outs[0], recv=outs[1], land=list(outs[2:]))


def relay_wait(h, *, name):
    nt = len(h["land"])

    def body(*refs):
        for cp in _relay_copies(refs[:nt], refs[nt], refs[nt + 1]):
            cp.wait_send()
            cp.wait_recv()

    outs = pl.pallas_call(
        body, name=name, out_shape=tuple(pltpu.HBM(a.shape, a.dtype) for a in h["land"]),
        in_specs=[HBM_SPEC] * nt + [SEM_SPEC, SEM_SPEC], out_specs=tuple([HBM_SPEC] * nt),
        input_output_aliases={i: i for i in range(nt)},
        compiler_params=pltpu.CompilerParams(has_side_effects=EFFECT),
    )(*h["land"], h["send"], h["recv"])
    return list(outs)


def _adamw_math(w, g, m, v):
    nm = B1 * m + (1.0 - B1) * g
    nv = B2 * v + (1.0 - B2) * (g * g)
    m_hat = nm / (1.0 - B1 ** STEP)
    v_hat = nv / (1.0 - B2 ** STEP)
    return -LR * (m_hat / (jnp.sqrt(v_hat) + AEPS) + WD * w), nm, nv


def adamw_many(ws, gs, ms, vs, *, name):
    n = len(ws)

    def body(*refs):
        for k in range(n):
            d, nm, nv = _adamw_math(refs[k][...], refs[n + k][...], refs[2 * n + k][...], refs[3 * n + k][...])
            refs[4 * n + k][...] = d
            refs[5 * n + k][...] = nm
            refs[6 * n + k][...] = nv

    vm = pl.BlockSpec(memory_space=pltpu.VMEM)
    sh = [jax.ShapeDtypeStruct(w.shape, F32) for w in ws]
    outs = pl.pallas_call(body, name=name, in_specs=[vm] * (4 * n), out_specs=[vm] * (3 * n), out_shape=sh * 3,
                          )(*ws, *gs, *ms, *vs)
    return outs[:n], outs[n:2 * n], outs[2 * n:]


def adamw(w, g, m, v, *, name, tr=256):
    R, C = w.shape
    tr = _pick(R, tr, 8)

    def body(w_ref, g_ref, m_ref, v_ref, d_ref, nm_ref, nv_ref):
        d_ref[...], nm_ref[...], nv_ref[...] = _adamw_math(w_ref[...], g_ref[...], m_ref[...], v_ref[...])

    blk = pl.BlockSpec((tr, C), lambda i: (i, 0))
    sh = jax.ShapeDtypeStruct((R, C), F32)
    return pl.pallas_call(
        body, name=name, grid=(R // tr,), in_specs=[blk, blk, blk, blk], out_specs=[blk, blk, blk],
        out_shape=[sh, sh, sh], compiler_params=pltpu.CompilerParams(dimension_semantics=("parallel",)),
    )(w, g, m, v)


def adamw_slots(w, slots, m, v, *, name, tr=256):
    unit = w.ndim == 3
    R, C = w.shape[0], w.shape[-1]
    if R % 16 == 0:
        tr = _pick(R, tr, 16)
    else:
        tr = 144

    def body(w_ref, s_ref, m_ref, v_ref, g_ref, d_ref, nm_ref, nv_ref):
        g = s_ref[0].astype(F32)
        for k in range(1, NDEV):
            g = g + s_ref[k].astype(F32)
        g_ref[...] = g
        d_ref[...], nm_ref[...], nv_ref[...] = _adamw_math(w_ref[...], g, m_ref[...], v_ref[...])

    blk = pl.BlockSpec((tr, None, C), lambda i: (i, 0, 0)) if unit else pl.BlockSpec((tr, C), lambda i: (i, 0))
    sh = jax.ShapeDtypeStruct(w.shape, F32)
    return pl.pallas_call(
        body, name=name, grid=(pl.cdiv(R, tr),), in_specs=[blk, pl.BlockSpec((NDEV, tr, C), lambda i: (0, i, 0)), blk, blk],
        out_specs=[blk, blk, blk, blk], out_shape=[sh, sh, sh, sh],
        compiler_params=pltpu.CompilerParams(dimension_semantics=("parallel",)),
    )(w, slots, m, v)


def _padc(a, n=D):
    return jnp.pad(a, ((0, 0), (0, n - a.shape[1])))


def kernel(x, c, ctx, c_ctx, w_ada, b_ada, norm1_g, w_in, q_norm_g, kv_norm_g, w_uq, w_ukv, conv_w, conv_b, w_attn_out, w_conv_out, w_o, norm2_g, w_up, ffn_conv_w, ffn_conv_b, w_down, final_g, loss_target, m_c_ctx, m_w_ada, m_b_ada, m_norm1_g, m_w_in, m_q_norm_g, m_kv_norm_g, m_w_uq, m_w_ukv, m_conv_w, m_conv_b, m_w_attn_out, m_w_conv_out, m_w_o, m_norm2_g, m_w_up, m_ffn_conv_w, m_ffn_conv_b, m_w_down, m_final_g, v_c_ctx, v_w_ada, v_b_ada, v_norm1_g, v_w_in, v_q_norm_g, v_kv_norm_g, v_w_uq, v_w_ukv, v_conv_w, v_conv_b, v_w_attn_out, v_w_conv_out, v_w_o, v_norm2_g, v_w_up, v_ffn_conv_w, v_ffn_conv_b, v_w_down, v_final_g):
    me = 4 * lax.axis_index("x") + 2 * lax.axis_index("y") + lax.axis_index("c")
    W = dict(c_ctx=c_ctx, w_ada=w_ada, b_ada=b_ada, norm1_g=norm1_g, w_in=w_in, q_norm_g=q_norm_g, kv_norm_g=kv_norm_g,
             w_uq=w_uq, w_ukv=w_ukv, conv_w=conv_w, conv_b=conv_b, w_attn_out=w_attn_out, w_conv_out=w_conv_out, w_o=w_o,
             norm2_g=norm2_g, w_up=w_up, ffn_conv_w=ffn_conv_w, ffn_conv_b=ffn_conv_b, w_down=w_down, final_g=final_g)
    M = dict(c_ctx=m_c_ctx, w_ada=m_w_ada, b_ada=m_b_ada, norm1_g=m_norm1_g, w_in=m_w_in, q_norm_g=m_q_norm_g,
             kv_norm_g=m_kv_norm_g, w_uq=m_w_uq, w_ukv=m_w_ukv, conv_w=m_conv_w, conv_b=m_conv_b, w_attn_out=m_w_attn_out,
             w_conv_out=m_w_conv_out, w_o=m_w_o, norm2_g=m_norm2_g, w_up=m_w_up, ffn_conv_w=m_ffn_conv_w,
             ffn_conv_b=m_ffn_conv_b, w_down=m_w_down, final_g=m_final_g)
    V = dict(c_ctx=v_c_ctx, w_ada=v_w_ada, b_ada=v_b_ada, norm1_g=v_norm1_g, w_in=v_w_in, q_norm_g=v_q_norm_g,
             kv_norm_g=v_kv_norm_g, w_uq=v_w_uq, w_ukv=v_w_ukv, conv_w=v_conv_w, conv_b=v_conv_b, w_attn_out=v_w_attn_out,
             w_conv_out=v_w_conv_out, w_o=v_w_o, norm2_g=v_norm2_g, w_up=v_w_up, ffn_conv_w=v_ffn_conv_w,
             ffn_conv_b=v_ffn_conv_b, w_down=v_w_down, final_g=v_final_g)
    names = list(W)
    transposed = ("w_up",)
    as2d = lambda k, a: (a.reshape(1, -1) if a.ndim == 1 else
                         a[0].T if k in transposed else a.reshape(a.shape[-2], a.shape[-1]))
    W2 = {k: as2d(k, a) for k, a in W.items()}
    M2 = {k: as2d(k, a) for k, a in M.items()}
    V2 = {k: as2d(k, a) for k, a in V.items()}
    unit3 = lambda a: jnp.transpose(a, (2, 0, 1))
    W3, M3, V3 = unit3(W["w_in"]), unit3(M["w_in"]), unit3(V["w_in"])
    nsh = W2["w_ada"].shape[1]

    b_sh = lax.dynamic_slice(W2["b_ada"], (0, me * nsh), (1, nsh))
    s_all, m_all = ada_fwd(c, W2["c_ctx"], _padc(W2["ffn_conv_w"]), _padc(W2["conv_w"]), W2["w_ada"], b_sh, [],
                           name="ada_fwd")
    mod_lat = m_all[:, 0, :].reshape(1, 6 * D)
    mod_ctx = m_all[:, 1, :].reshape(1, 6 * D)
    ffn_w_full = s_all[:, 2:5, :2 * DFF // NDEV].transpose(1, 0, 2).reshape(3, 2 * DFF)
    conv_w_full = s_all[:, 5:8, :CONV // NDEV].transpose(1, 0, 2).reshape(3, CONV)

    stage_w = {"in": ["w_in"], "mid": ["w_uq", "w_ukv", "w_attn_out", "w_conv_out", "w_o"], "ffn": ["w_up", "w_down"]}
    two_level = ("in", "mid")
    ag, tok = {}, m_all
    for st, nms in stage_w.items():
        ag[st] = exchange_start([W2[nm].astype(BF) for nm in nms], per_peer=False, dep=tok, name="ag_start_" + st,
                                peers=FIRST_HOP if st in two_level else ALL_PEERS)
        tok = ag[st]["token"]

    def get_w(stage, after):
        lands = exchange_wait(ag[stage], after, name="ag_wait_" + stage)
        if stage in two_level:
            lands = relay_wait(relay_start(lands, name="ag_relay_" + stage), name="ag_relay_wait_" + stage)
        g = dict(zip(stage_w[stage], lands))
        if stage == "in":
            return build_win(g["w_in"], name="build_win")
        if stage == "mid":
            wq2, wkv2 = build_wq_wkv(g["w_uq"], g["w_ukv"], name="build_wq_wkv")
            return (wq2, wkv2, unshard_cols(g["w_attn_out"], name="unshard_w_attn_out"),
                    unshard_cols(g["w_conv_out"], name="unshard_w_conv_out"), g["w_o"].reshape(D, D))
        return g["w_up"].reshape(2 * DFF, D), g["w_down"].reshape(DFF, D)

    stage_g = {"ffn": ["w_up", "w_down"], "mid": ["w_attn_out", "w_conv_out", "w_o"], "qkv": ["w_uq", "w_ukv"],
               "in": ["w_in"]}
    rs = {}

    def put_g(stage, g):
        if stage == "in":
            parts = [shard_win_grad(g["dwin"], g["dwin_c"], name="shard_win_grad")]
        elif stage == "mid":
            parts = [shard_cols(g["dwao"], name="shard_w_attn_out"), shard_cols(g["dwco"], name="shard_w_conv_out"),
                     g["dwo"].reshape(NDEV, D // NDEV, D)]
        elif stage == "qkv":
            parts = list(shard_wq_wkv_grad(g["dwq2"], g["dwkv2"], name="shard_wq_wkv_grad"))
        else:
            parts = [g["dwup"].reshape(NDEV, 2 * DFF // NDEV, D), g["dwdn"].reshape(NDEV, DFF // NDEV, D)]
        rs[stage] = exchange_start(parts, per_peer=True, name="rs_start_" + stage)
        return rs[stage]["token"]

    r = _local_step(x[0], ctx[0], loss_target[0], mod_lat, mod_ctx, W2["norm1_g"], W2["q_norm_g"], W2["kv_norm_g"],
                    W2["norm2_g"], W2["final_g"], conv_w_full, W2["conv_b"], ffn_w_full, W2["ffn_conv_b"], get_w, put_g,
                    ag["ffn"]["token"])

    G, DL, NM, NV = {}, {}, {}, {}

    def finish(stage, after):
        for nm, sl in zip(stage_g[stage], exchange_wait(rs[stage], after, name="rs_wait_" + stage)):
            wmv = (W3, M3, V3) if nm == "w_in" else (W2[nm], M2[nm], V2[nm])
            G[nm], DL[nm], NM[nm], NV[nm] = adamw_slots(wmv[0], sl, wmv[1], wmv[2], name="adamw_" + nm)
            after = DL[nm]
        return after

    after = r["dx"]
    for st in ("ffn", "mid", "qkv"):
        after = finish(st, after)

    a_buf, ssum = sync_small(r, [DL[nm] for st in ("ffn", "mid", "qkv") for nm in stage_g[st]], name="sync_small")
    loss = ssum[P_LOSS, 0]
    G["norm1_g"] = ssum[P_N1:P_N1 + 1]
    G["q_norm_g"] = ssum[P_QG:P_QG + 1, :QL]
    G["kv_norm_g"] = ssum[P_KVG:P_KVG + 1, :KVL]
    G["conv_b"] = ssum[P_CB:P_CB + 1, :CONV]
    G["norm2_g"] = ssum[P_N2:P_N2 + 1]
    G["ffn_conv_b"] = ssum[P_FB:P_FB + 2 * FROWS].reshape(1, 2, FROWS * D)[:, :, :DFF].reshape(1, 2 * DFF)
    G["final_g"] = ssum[P_FG:P_FG + 1]
    G["conv_w"] = lax.dynamic_slice(ssum[P_CW:P_CW + 3, :CONV], (0, me * (CONV // NDEV)), (3, CONV // NDEV))
    fw_full = ssum[P_FW:P_FW + 6 * FROWS].reshape(3, 2, FROWS * D)[:, :, :DFF].reshape(3, 2 * DFF)
    G["ffn_conv_w"] = lax.dynamic_slice(fw_full, (0, me * (2 * DFF // NDEV)), (3, 2 * DFF // NDEV))
    G["b_ada"] = (ssum[P_DML:P_DML + 6] + ssum[P_DMC:P_DMC + 6]).reshape(1, 6 * D)

    dml = lax.dynamic_slice(a_buf[:, P_DML:P_DML + 6, :].reshape(NDEV, 6 * D), (0, me * nsh), (NDEV, nsh))
    dmc = lax.dynamic_slice(ssum[P_DMC:P_DMC + 6].reshape(1, 6 * D), (0, me * nsh), (1, nsh))
    G["w_ada"], gcc = ada_bwd(s_all, dml, dmc, W2["w_ada"], W2["c_ctx"], name="ada_bwd")
    G["c_ctx"] = gcc[0:1]

    DL["w_ada"], NM["w_ada"], NV["w_ada"] = adamw(W2["w_ada"], G["w_ada"], M2["w_ada"], V2["w_ada"], name="adamw_w_ada")
    small = ["c_ctx", "b_ada", "norm1_g", "q_norm_g", "kv_norm_g", "conv_b", "norm2_g", "ffn_conv_b", "final_g", "conv_w",
             "ffn_conv_w"]
    ds, nms, nvs = adamw_many([W2[k] for k in small], [G[k] for k in small], [M2[k] for k in small],
                              [V2[k] for k in small], name="adamw_small")
    for k, nm in enumerate(small):
        DL[nm], NM[nm], NV[nm] = ds[k], nms[k], nvs[k]
    finish("in", ds[0])

    outs = [loss, r["dx"][None]]
    for grp in (G, DL, NM, NV):
        outs += [grp[nm].T[None] if nm in transposed else
                 jnp.transpose(grp[nm], (1, 2, 0)) if nm == "w_in" else grp[nm].reshape(W[nm].shape) for nm in names]
    return tuple(outs)
```

```python
import functools
import numpy as np
import jax
import jax.numpy as jnp
from jax import lax
from jax.experimental import pallas as pl
from jax.experimental.pallas import tpu as pltpu

F32 = jnp.float32
BF = jnp.bfloat16
MESH = pl.DeviceIdType.MESH

D = 1024
T = 2048
TC = 256
TKV = T + TC
GRID_W = 64
NH = 8
DN = 64
DR = 32
DV = 64
QL = 384
KVL = 256
CONV = 512
DFF = 2816
EPS = 1e-6
ROPE_THETA = 10000.0
SCALE = (DN + DR) ** -0.5
NDEV = 8
HP = 128

O_GA, O_GC, O_KV, O_Q, O_CV = 0, 1024, 2048, 2560, 3072
NIN = 4608
CVB = 256
N_IN = 4256
SH_IN = N_IN // NDEV

LR, B1, B2, AEPS, WD, STEP = 0.001, 0.9, 0.999, 1e-08, 0.01, 10


def _pick(n, target, mult=128):
    best = None
    for d in range(mult, min(n, target) + 1, mult):
        if n % d == 0:
            best = d
    return best if best is not None else n


def _swap_start(g):
    return 8 * (g ^ 1)


def mm(a, b, *, ta=False, tb=False, out_dtype=F32, name, tm=1024, tn=1024, tk=2048, M=None, N=None, K=None,
       a_off=(0, 0), b_off=(0, 0), a_stack=False, b_stack=False, o_stack=False, dep=None):
    def dims(arr, stack):
        return (arr.shape[1], 2 * arr.shape[2]) if stack else arr.shape

    ar, ac = dims(a, a_stack)
    br, bc = dims(b, b_stack)
    M = M or ((ac if ta else ar) - a_off[1 if ta else 0])
    K = K or ((ar if ta else ac) - a_off[0 if ta else 1])
    N = N or ((br if tb else bc) - b_off[0 if tb else 1])
    tm = _pick(M, tm, 128 if ta else 16)
    tn = _pick(N // 2 if (o_stack or (b_stack and not tb)) else N, tn, 128)
    tk = _pick(K // 2 if ((a_stack and not ta) or (b_stack and tb)) else K, tk, 128)
    nk = K // tk
    ca = 0 if ta else 1
    cb = 1 if tb else 0

    def body(a_ref, b_ref, *rest):
        o_ref, acc = rest[-2:]
        k = pl.program_id(2)
        part = lax.dot_general(a_ref[...].astype(BF), b_ref[...].astype(BF),
                               (((ca,), (cb,)), ((), ())), preferred_element_type=F32)
        if nk == 1:
            o_ref[...] = part.astype(o_ref.dtype)
        else:
            @pl.when(k == 0)
            def _():
                acc[...] = part

            @pl.when(k > 0)
            def _():
                acc[...] += part

            @pl.when(k == nk - 1)
            def _():
                o_ref[...] = acc[...].astype(o_ref.dtype)

    def spec(blk, rc, off, stack, ncols):
        assert off[0] % blk[0] == 0 and off[1] % blk[1] == 0, (name, blk, off)
        ro, co = off[0] // blk[0], off[1] // blk[1]
        if not stack:
            return pl.BlockSpec(blk, lambda i, j, k: (rc(i, j, k)[0] + ro, rc(i, j, k)[1] + co))
        nhb = ncols // 2 // blk[1]
        return pl.BlockSpec((None,) + blk,
                            lambda i, j, k: ((rc(i, j, k)[1] + co) // nhb, rc(i, j, k)[0] + ro, (rc(i, j, k)[1] + co) % nhb))

    a_spec = spec((tk, tm), lambda i, j, k: (k, i), a_off, a_stack, ac) if ta else \
        spec((tm, tk), lambda i, j, k: (i, k), a_off, a_stack, ac)
    b_spec = spec((tn, tk), lambda i, j, k: (j, k), b_off, b_stack, bc) if tb else \
        spec((tk, tn), lambda i, j, k: (k, j), b_off, b_stack, bc)
    o_spec = spec((tm, tn), lambda i, j, k: (i, j), (0, 0), o_stack, N)
    o_shape = (2, M, N // 2) if o_stack else (M, N)
    deps = [] if dep is None else [dep]
    return pl.pallas_call(
        body, name=name, grid=(M // tm, N // tn, nk),
        in_specs=[a_spec, b_spec] + [pl.BlockSpec(memory_space=pl.ANY)] * len(deps),
        out_specs=o_spec, out_shape=jax.ShapeDtypeStruct(o_shape, out_dtype),
        scratch_shapes=[pltpu.VMEM((tm, tn) if nk > 1 else (8, 128), F32)],
        compiler_params=pltpu.CompilerParams(dimension_semantics=("parallel", "parallel", "arbitrary")),
    )(a, b, *deps)


def _row(width):
    return pl.BlockSpec((1, width), lambda *_: (0, 0))


NLAT = T // TC


def normmod_cat(ctx, x, g, csc, csh, sc, sh, dep, *, name, tm=256):
    assert tm == TC

    def body(c_ref, x_ref, g_ref, csc_ref, csh_ref, sc_ref, sh_ref, dep_ref, h_ref):
        last = pl.program_id(0) == NLAT
        xv = jnp.where(last, c_ref[...], x_ref[...])
        scv = jnp.where(last, csc_ref[...], sc_ref[...])
        shv = jnp.where(last, csh_ref[...], sh_ref[...])
        r = lax.rsqrt(jnp.mean(xv * xv, axis=-1, keepdims=True) + EPS)
        h_ref[...] = ((xv * r * g_ref[...]) * (1.0 + scv) + shv).astype(BF)

    return pl.pallas_call(
        body, name=name, grid=(TKV // tm,),
        in_specs=[pl.BlockSpec((tm, D), lambda i: (0, 0)), pl.BlockSpec((tm, D), lambda i: (jnp.minimum(i, NLAT - 1), 0)),
                  _row(D), _row(D), _row(D), _row(D), _row(D), pl.BlockSpec(memory_space=pl.ANY)],
        out_specs=pl.BlockSpec((tm, D), lambda i: (i, 0)), out_shape=jax.ShapeDtypeStruct((TKV, D), BF),
        compiler_params=pltpu.CompilerParams(dimension_semantics=("parallel",)),
    )(ctx, x, g, csc, csh, sc, sh, dep)


def kvprep(pc, p, kvg, wkv2, ck, sk, *, name, tm=256):
    assert tm == TC
    nb = TKV // tm
    kvcol = O_KV // 512

    def body(pc_ref, p_ref, g_ref, w_ref, ck_ref, sk_ref, k_ref, v_ref, ckv_ref):
        i = pl.program_id(0)
        t = jnp.where(i == NLAT, pc_ref[...], p_ref[...])
        pk = t[:, :KVL]
        r = lax.rsqrt(jnp.mean(pk * pk, axis=-1, keepdims=True) + EPS)
        ckv = (pk * r * g_ref[...]).astype(BF)
        ckv_ref[...] = ckv
        kv2 = jnp.dot(ckv, w_ref[...], preferred_element_type=F32)
        krr = t[:, KVL:KVL + HP] * ck_ref[...] + t[:, KVL + HP:KVL + 2 * HP] * sk_ref[...]
        k_ref[...] = (kv2[:, :NH * HP] + jnp.concatenate([krr] * NH, axis=1)).astype(BF)
        v_ref[...] = kv2[:, NH * HP:].astype(BF)

    return pl.pallas_call(
        body, name=name, grid=(nb,),
        in_specs=[pl.BlockSpec((tm, 512), lambda i: (0, 0)),
                  pl.BlockSpec((tm, 512), lambda i: (jnp.minimum(i, NLAT - 1), kvcol)),
                  _row(KVL), pl.BlockSpec((KVL, NH * HP + NH * DV), lambda i: (0, 0)),
                  pl.BlockSpec((tm, HP), lambda i: (i, 0)), pl.BlockSpec((tm, HP), lambda i: (i, 0))],
        out_specs=[pl.BlockSpec((tm, NH * HP), lambda i: (i, 0)), pl.BlockSpec((tm, NH * DV), lambda i: (i, 0)),
                   pl.BlockSpec((tm, KVL), lambda i: (i, 0))],
        out_shape=[jax.ShapeDtypeStruct((TKV, NH * HP), BF), jax.ShapeDtypeStruct((TKV, NH * DV), BF),
                   jax.ShapeDtypeStruct((TKV, KVL), BF)],
        compiler_params=pltpu.CompilerParams(dimension_semantics=("parallel",)),
    )(pc, p, kvg, wkv2, ck, sk)


def qprep(p, qg, wq2, cq_t, sq_t, *, name, tm=256):
    qcol = O_Q // 512

    def body(p_ref, g_ref, w_ref, c_ref, s_ref, q_ref, cq_ref):
        pq = p_ref[...]
        r = lax.rsqrt(jnp.sum(pq * pq, axis=-1, keepdims=True) * (1.0 / QL) + EPS)
        cq = (pq * r * g_ref[...]).astype(BF)
        cq_ref[...] = cq
        q2 = jnp.dot(cq, w_ref[...], preferred_element_type=F32)
        cc = jnp.concatenate([c_ref[...]] * NH, axis=1)
        ss = jnp.concatenate([s_ref[...]] * NH, axis=1)
        q_ref[...] = (q2[:, :NH * HP] * cc + q2[:, NH * HP:] * ss).astype(BF)

    return pl.pallas_call(
        body, name=name, grid=(T // tm,),
        in_specs=[pl.BlockSpec((tm, 512), lambda i: (i, qcol)), _row(512),
                  pl.BlockSpec((512, 2 * NH * HP), lambda i: (0, 0)),
                  pl.BlockSpec((tm, HP), lambda i: (i, 0)), pl.BlockSpec((tm, HP), lambda i: (i, 0))],
        out_specs=[pl.BlockSpec((tm, NH * HP), lambda i: (i, 0)), pl.BlockSpec((tm, 512), lambda i: (i, 0))],
        out_shape=[jax.ShapeDtypeStruct((T, NH * HP), BF), jax.ShapeDtypeStruct((T, 512), BF)],
        compiler_params=pltpu.CompilerParams(dimension_semantics=("parallel",)),
    )(p, qg, wq2, cq_t, sq_t)


def _head_mask(h):
    lanes = lax.broadcasted_iota(jnp.int32, (1, 2 * DV), 1)
    return (lanes // DV) == (h % 2)


LOG2E = 1.4426950408889634


def _scores_pass(q, k_ref, s_scr, kc):
    m = None
    for c in range(TKV // kc):
        s = lax.dot_general(q, k_ref[c * kc:(c + 1) * kc, :], (((1,), (1,)), ((), ())),
                            preferred_element_type=F32) * (SCALE * LOG2E)
        s_scr[:, c * kc:(c + 1) * kc] = s
        mc = jnp.max(s, axis=-1, keepdims=True)
        m = mc if m is None else jnp.maximum(m, mc)
    return m


def attn_fwd(q, k, v, *, name, tq=512, kc=1152):
    def body(q_ref, k_ref, v_ref, o_ref, s_scr):
        h = pl.program_id(1)
        m = _scores_pass(q_ref[...], k_ref, s_scr, kc)
        l = jnp.zeros((tq, 1), F32)
        acc = jnp.zeros((tq, 2 * DV), F32)
        for c in range(TKV // kc):
            e = jnp.exp2(s_scr[:, c * kc:(c + 1) * kc] - m)
            l = l + jnp.sum(e, axis=-1, keepdims=True)
            acc = acc + jnp.dot(e.astype(BF), v_ref[c * kc:(c + 1) * kc, :], preferred_element_type=F32)
        o2 = jnp.where(_head_mask(h), acc * (1.0 / l), 0.0).astype(BF)

        @pl.when(h % 2 == 0)
        def _():
            o_ref[...] = o2

        @pl.when(h % 2 == 1)
        def _():
            o_ref[...] = o_ref[...] + o2

    return pl.pallas_call(
        body, name=name, grid=(T // tq, NH),
        in_specs=[pl.BlockSpec((tq, HP), lambda i, h: (i, h)), pl.BlockSpec((TKV, HP), lambda i, h: (0, h)),
                  pl.BlockSpec((TKV, 2 * DV), lambda i, h: (0, h // 2))],
        out_specs=pl.BlockSpec((tq, 2 * DV), lambda i, h: (i, h // 2)),
        out_shape=jax.ShapeDtypeStruct((T, NH * DV), BF),
        scratch_shapes=[pltpu.VMEM((tq, TKV), F32)],
        compiler_params=pltpu.CompilerParams(dimension_semantics=("parallel", "arbitrary")),
    )(q, k, v)


def _shift_dn(x):
    n = x.shape[0]
    rows = lax.broadcasted_iota(jnp.int32, (n, 1), 0)
    return jnp.where(rows == 0, 0.0, pltpu.roll(x, 1, axis=0))


def _shift_up(x):
    n = x.shape[0]
    rows = lax.broadcasted_iota(jnp.int32, (n, 1), 0)
    return jnp.where(rows == n - 1, 0.0, pltpu.roll(x, n - 1, axis=0))


def _conv(x, w_ref, b_ref):
    return b_ref[...] + _shift_dn(x) * w_ref[0:1, :] + x * w_ref[1:2, :] + _shift_up(x) * w_ref[2:3, :]


def _conv_t(dy, w_ref):
    return _shift_up(dy) * w_ref[0:1, :] + dy * w_ref[1:2, :] + _shift_dn(dy) * w_ref[2:3, :]


def _conv_wgrad(dw_ref, dy, x):
    dw_ref[0:1, :] = jnp.sum(dy * _shift_dn(x), axis=0, keepdims=True)
    dw_ref[1:2, :] = jnp.sum(dy * x, axis=0, keepdims=True)
    dw_ref[2:3, :] = jnp.sum(dy * _shift_up(x), axis=0, keepdims=True)


def convz(p, cw, cb, *, name):
    o0 = O_CV // (3 * CVB)

    def body(p_ref, w_ref, bias_ref, z_ref):
        xv, bv, cv = p_ref[:, 0:CVB], p_ref[:, CVB:2 * CVB], p_ref[:, 2 * CVB:3 * CVB]
        z_ref[...] = (bv * _conv(cv * xv, w_ref, bias_ref)).astype(BF)

    return pl.pallas_call(
        body, name=name, grid=(CONV // CVB,),
        in_specs=[pl.BlockSpec((T, 3 * CVB), lambda j: (0, o0 + j)), pl.BlockSpec((3, CVB), lambda j: (0, j)),
                  pl.BlockSpec((1, CVB), lambda j: (0, j))],
        out_specs=pl.BlockSpec((T, CVB), lambda j: (0, j)),
        out_shape=jax.ShapeDtypeStruct((T, CONV), BF),
        compiler_params=pltpu.CompilerParams(dimension_semantics=("parallel",)),
    )(p, cw, cb)


def out_proj_merge(o, wao, z, wco, p, *, name, tm=512):
    kin = o.shape[1]

    def body(o_ref, wa_ref, z_ref, wc_ref, ga_ref, gc_ref, ya_ref, yc_ref, m_ref):
        ya = jnp.dot(o_ref[...], wa_ref[...], preferred_element_type=F32)
        yc = jnp.dot(z_ref[...], wc_ref[...], preferred_element_type=F32)
        ya_ref[...] = ya
        yc_ref[...] = yc
        m_ref[...] = (jax.nn.sigmoid(ga_ref[...]) * ya + jax.nn.sigmoid(gc_ref[...]) * yc).astype(BF)

    blk = pl.BlockSpec((tm, D), lambda i: (i, 0))
    act = pl.BlockSpec((tm, kin), lambda i: (i, 0))
    wsp = pl.BlockSpec((kin, D), lambda i: (0, 0))
    sh = jax.ShapeDtypeStruct((T, D), F32)
    return pl.pallas_call(
        body, name=name, grid=(T // tm,),
        in_specs=[act, wsp, act, wsp, pl.BlockSpec((tm, D), lambda i: (i, O_GA // D)),
                  pl.BlockSpec((tm, D), lambda i: (i, O_GC // D))],
        out_specs=[blk, blk, blk], out_shape=[sh, sh, jax.ShapeDtypeStruct((T, D), BF)],
        compiler_params=pltpu.CompilerParams(dimension_semantics=("parallel",)),
    )(o, wao, z, wco, p, p)


CONV_HALO = 8
CONV_ROWS = 256


def _row_chunks(n, chunk, carry):
    carry = chunk(0, True, False, carry)
    carry = lax.fori_loop(1, n // CONV_ROWS - 1, lambda c, a: chunk(c * CONV_ROWS, False, False, a), carry)
    return chunk(n - CONV_ROWS, False, True, carry)


def _ext_rows(ref, r0, first, last):
    n, w = ref.shape
    zero = jnp.zeros((CONV_HALO, w), ref.dtype)
    if first:
        return jnp.concatenate([zero, ref[0:CONV_ROWS + CONV_HALO, :]], axis=0)
    if last:
        return jnp.concatenate([ref[n - CONV_ROWS - CONV_HALO:n, :], zero], axis=0)
    return ref[pl.ds(pl.multiple_of(r0 - CONV_HALO, 8), CONV_ROWS + 2 * CONV_HALO), :]


def _center_rows(r0, first, last):
    return slice(r0, r0 + CONV_ROWS) if (first or last) else pl.ds(pl.multiple_of(r0, 8), CONV_ROWS)


def _roll_dn(x):
    return pltpu.roll(x, 1, axis=0)


def _roll_up(x):
    return pltpu.roll(x, x.shape[0] - 1, axis=0)


_CTR = slice(CONV_HALO, CONV_HALO + CONV_ROWS)


def ffn_act(u0, cw, cb, *, name, tc=256):
    nb = DFF // tc

    def body(u_ref, wg_ref, wv_ref, bg_ref, bv_ref, f_ref):
        wg = [wg_ref[k:k + 1, :] for k in range(3)]
        wv = [wv_ref[k:k + 1, :] for k in range(3)]
        bg, bv = bg_ref[...], bv_ref[...]

        def chunk(r0, first, last, carry):
            xg, xv = _ext_rows(u_ref.at[0], r0, first, last), _ext_rows(u_ref.at[1], r0, first, last)
            ug = bg + _roll_dn(xg) * wg[0] + xg * wg[1] + _roll_up(xg) * wg[2]
            uv = bv + _roll_dn(xv) * wv[0] + xv * wv[1] + _roll_up(xv) * wv[2]
            f_ref[_center_rows(r0, first, last), :] = (ug * jax.nn.sigmoid(ug) * uv)[_CTR].astype(BF)
            return carry

        _row_chunks(T, chunk, 0)

    return pl.pallas_call(
        body, name=name, grid=(nb,),
        in_specs=[pl.BlockSpec((2, T, tc), lambda j: (0, 0, j)),
                  pl.BlockSpec((3, tc), lambda j: (0, j)), pl.BlockSpec((3, tc), lambda j: (0, nb + j)),
                  pl.BlockSpec((1, tc), lambda j: (0, j)), pl.BlockSpec((1, tc), lambda j: (0, nb + j))],
        out_specs=pl.BlockSpec((T, tc), lambda j: (0, j)),
        out_shape=jax.ShapeDtypeStruct((T, DFF), BF),
        compiler_params=pltpu.CompilerParams(dimension_semantics=("parallel",)),
    )(u0, cw, cw, cb, cb)


def rows_call(lead, ins, in_specs, out_shape, out_specs, fn, *, name, R, tm):
    tb, a_stack, tk = lead.get("tb", False), lead.get("a_stack", False), lead["tk"]
    K = 2 * lead["a"].shape[2] if a_stack else lead["a"].shape[1]
    nk = K // tk
    deps = [] if lead.get("dep") is None else [lead["dep"]]
    n_in = len(ins)

    def body(a_ref, b_ref, *refs):
        refs = refs[len(deps):]
        in_refs, out_refs, acc = refs[:n_in], refs[n_in:-1], refs[-1]
        i, k = pl.program_id(0), pl.program_id(1)
        part = lax.dot_general(a_ref[...].astype(BF), b_ref[...].astype(BF),
                               (((1,), (1 if tb else 0,)), ((), ())), preferred_element_type=F32)
        if nk == 1:
            fn(i, part, in_refs, out_refs)
            return

        @pl.when(k == 0)
        def _():
            acc[...] = part

        @pl.when(k > 0)
        def _():
            acc[...] += part

        @pl.when(k == nk - 1)
        def _():
            fn(i, acc[...], in_refs, out_refs)

    if a_stack:
        nhb = K // 2 // tk
        a_spec = pl.BlockSpec((None, tm, tk), lambda i, k: (k // nhb, i, k % nhb))
    else:
        a_spec = pl.BlockSpec((tm, tk), lambda i, k: (i, k))
    b_spec = pl.BlockSpec((D, tk), lambda i, k: (0, k)) if tb else pl.BlockSpec((tk, D), lambda i, k: (k, 0))
    return pl.pallas_call(
        body, name=name, grid=(R // tm, nk),
        in_specs=[a_spec, b_spec] + [pl.BlockSpec(memory_space=pl.ANY)] * len(deps) + list(in_specs),
        out_specs=out_specs, out_shape=out_shape,
        scratch_shapes=[pltpu.VMEM((tm, D) if nk > 1 else (8, 128), F32)],
        compiler_params=pltpu.CompilerParams(dimension_semantics=("arbitrary", "arbitrary")),
    )(lead["a"], lead["b"], *deps, *ins)


def _rblk(tm, w=D, col=0):
    return pl.BlockSpec((tm, w), lambda i, k: (i, col))


def _rrow(w=D):
    return pl.BlockSpec((1, w), lambda i, k: (0, 0))


def down_final(f, wdn, x1, g2, fg, tgt, *, name, tm=512):
    def fn(i, d, in_refs, out_refs):
        x1_ref, g2_ref, fg_ref, t_ref = in_refs
        d_ref, dx_ref, dd_ref, dfg_ref, loss_ref = out_refs
        d_ref[...] = d
        xv = x1_ref[...] + g2_ref[...] * d
        r = lax.rsqrt(jnp.mean(xv * xv, axis=-1, keepdims=True) + EPS)
        xh = xv * r
        diff = xh * fg_ref[...] - t_ref[...]
        part = 0.5 * jnp.sum(jnp.mean(diff * diff, axis=-1, keepdims=True), axis=0, keepdims=True)
        dy = diff * (1.0 / D)
        a = dy * fg_ref[...]
        dx = r * (a - xh * jnp.mean(a * xh, axis=-1, keepdims=True))
        dx_ref[...] = dx
        dd_ref[...] = (dx * g2_ref[...]).astype(BF)
        dfg = jnp.sum(dy * xh, axis=0, keepdims=True)

        @pl.when(i == 0)
        def _():
            dfg_ref[...] = dfg
            loss_ref[...] = jnp.broadcast_to(part, (1, 128))

        @pl.when(i > 0)
        def _():
            dfg_ref[...] += dfg
            loss_ref[...] += jnp.broadcast_to(part, (1, 128))

    blk = _rblk(tm)
    return rows_call(
        dict(a=f, b=wdn, tk=DFF), [x1, g2, fg, tgt], [blk, _rrow(), _rrow(), blk],
        [jax.ShapeDtypeStruct((T, D), F32), jax.ShapeDtypeStruct((T, D), F32), jax.ShapeDtypeStruct((T, D), BF),
         jax.ShapeDtypeStruct((1, D), F32), jax.ShapeDtypeStruct((1, 128), F32)],
        [blk, blk, blk, _rrow(), _rrow(128)], fn, name=name, R=T, tm=tm)


def oproj_resid(merged, wo, x, gate, g, sc, sh, *, name, tm=512):
    def fn(i, a, in_refs, out_refs):
        x_ref, gate_ref, g_ref, sc_ref, sh_ref = in_refs
        a_ref, x1_ref, h_ref = out_refs
        a_ref[...] = a
        xv = x_ref[...] + gate_ref[...] * a
        x1_ref[...] = xv
        r = lax.rsqrt(jnp.mean(xv * xv, axis=-1, keepdims=True) + EPS)
        h_ref[...] = ((xv * r * g_ref[...]) * (1.0 + sc_ref[...]) + sh_ref[...]).astype(BF)

    blk = _rblk(tm)
    return rows_call(
        dict(a=merged, b=wo, tk=D), [x, gate, g, sc, sh], [blk, _rrow(), _rrow(), _rrow(), _rrow()],
        [jax.ShapeDtypeStruct((T, D), F32), jax.ShapeDtypeStruct((T, D), F32), jax.ShapeDtypeStruct((T, D), BF)],
        [blk, blk, blk], fn, name=name, R=T, tm=tm)


def oproj_dx_gate_bwd(da, wo, p, ya, yc, *, name, tm=512):
    def fn(i, dm, in_refs, out_refs):
        ga_ref, gc_ref, ya_ref, yc_ref = in_refs
        dya_ref, dyc_ref, dp_ref = out_refs
        sa, sc_ = jax.nn.sigmoid(ga_ref[...]), jax.nn.sigmoid(gc_ref[...])
        dya_ref[...] = (dm * sa).astype(BF)
        dyc_ref[...] = (dm * sc_).astype(BF)
        dp_ref[:, 0:D] = (dm * ya_ref[...] * (sa * (1.0 - sa))).astype(BF)
        dp_ref[:, D:2 * D] = (dm * yc_ref[...] * (sc_ * (1.0 - sc_))).astype(BF)

    blk = _rblk(tm)
    sh = jax.ShapeDtypeStruct((T, D), BF)
    return rows_call(
        dict(a=da, b=wo, tb=True, tk=D), [p, p, ya, yc], [_rblk(tm, D, O_GA // D), _rblk(tm, D, O_GC // D), blk, blk],
        [sh, sh, jax.ShapeDtypeStruct((T, NIN), BF)], [blk, blk, _rblk(tm, 2 * D)], fn, name=name, R=T, tm=tm)


def normmod_bwd(x, dh, g, sc, dres, gsrc, gate, *, name, tm=512):
    R = x.shape[0]
    tm = min(tm, R)
    has_res = dres is not None
    fused = isinstance(dh, dict)
    if fused:
        tb, a_stack, tk = dh.get("tb", False), dh.get("a_stack", False), dh["tk"]
        K = 2 * dh["a"].shape[2] if a_stack else dh["a"].shape[1]
        nk = K // tk
        deps = [] if dh.get("dep") is None else [dh["dep"]]
        n_dh = 2 + len(deps)
    else:
        nk, n_dh = 1, 1

    def elementwise(i, dhv, x_ref, g_ref, sc_ref, res_refs, out_refs):
        xv = x_ref[...]
        r = lax.rsqrt(jnp.mean(xv * xv, axis=-1, keepdims=True) + EPS)
        xh = xv * r
        n = xh * g_ref[...]
        dn = dhv * (1.0 + sc_ref[...])
        a = dn * g_ref[...]
        rows = [jnp.sum(dhv, axis=0, keepdims=True), jnp.sum(dhv * n, axis=0, keepdims=True),
                jnp.sum(dn * xh, axis=0, keepdims=True)]
        if has_res:
            dres_ref, gsrc_ref, gate_ref = res_refs
            dx_ref, dxg_ref, st_ref = out_refs
            dr = dres_ref[...]
            dx = dr + r * (a - xh * jnp.mean(a * xh, axis=-1, keepdims=True))
            dx_ref[...] = dx
            dxg_ref[...] = (dx * gate_ref[...]).astype(BF)
            rows.append(jnp.sum(dr * gsrc_ref[...], axis=0, keepdims=True))
        else:
            st_ref, = out_refs
            rows.append(jnp.zeros((1, D), F32))

        @pl.when(i == 0)
        def _():
            for k, row in enumerate(rows):
                st_ref[k:k + 1, :] = row

        @pl.when(i > 0)
        def _():
            for k, row in enumerate(rows):
                st_ref[k:k + 1, :] += row

    def body(*refs):
        x_ref, dh_refs, g_ref, sc_ref = refs[0], refs[1:1 + n_dh], refs[1 + n_dh], refs[2 + n_dh]
        rest = refs[3 + n_dh:]
        res_refs, rest = (rest[:3], rest[3:]) if has_res else ((), rest)
        out_refs = rest[:3] if has_res else rest[:1]
        i = pl.program_id(0)
        if not fused:
            elementwise(i, dh_refs[0][...], x_ref, g_ref, sc_ref, res_refs, out_refs)
            return
        acc = rest[-1]
        k = pl.program_id(1)
        part = lax.dot_general(dh_refs[0][...].astype(BF), dh_refs[1][...].astype(BF),
                               (((1,), (1 if tb else 0,)), ((), ())), preferred_element_type=F32)

        @pl.when(k == 0)
        def _():
            acc[...] = part

        @pl.when(k > 0)
        def _():
            acc[...] += part

        @pl.when(k == nk - 1)
        def _():
            elementwise(i, acc[...], x_ref, g_ref, sc_ref, res_refs, out_refs)

    rowb = lambda w: pl.BlockSpec((1, w), lambda i, *k: (0, 0))
    blk = pl.BlockSpec((tm, D), lambda i, *k: (i, 0))
    st_spec = pl.BlockSpec((4, D), lambda i, *k: (0, 0))
    st_shape = jax.ShapeDtypeStruct((4, D), F32)
    if fused:
        if a_stack:
            nhb = K // 2 // tk
            a_spec = pl.BlockSpec((None, tm, tk), lambda i, k: (k // nhb, i, k % nhb))
        else:
            a_spec = pl.BlockSpec((tm, tk), lambda i, k: (i, k))
        b_spec = pl.BlockSpec((D, tk), lambda i, k: (0, k)) if tb else pl.BlockSpec((tk, D), lambda i, k: (k, 0))
        dh_specs = [a_spec, b_spec] + [pl.BlockSpec(memory_space=pl.ANY)] * len(deps)
        dh_args = [dh["a"], dh["b"]] + deps
        grid, sem = (R // tm, nk), ("arbitrary", "arbitrary")
        scratch = [pltpu.VMEM((tm, D), F32)]
    else:
        dh_specs, dh_args, grid, sem, scratch = [blk], [dh], (R // tm,), ("arbitrary",), []
    cp = pltpu.CompilerParams(dimension_semantics=sem)
    if has_res:
        return pl.pallas_call(
            body, name=name, grid=grid, in_specs=[blk] + dh_specs + [rowb(D), rowb(D), blk, blk, rowb(D)],
            out_specs=[blk, blk, st_spec], scratch_shapes=scratch,
            out_shape=[jax.ShapeDtypeStruct((R, D), F32), jax.ShapeDtypeStruct((R, D), BF), st_shape],
            compiler_params=cp,
        )(x, *dh_args, g, sc, dres, gsrc, gate)
    return pl.pallas_call(
        body, name=name, grid=grid, in_specs=[blk] + dh_specs + [rowb(D), rowb(D)],
        out_specs=st_spec, out_shape=st_shape, scratch_shapes=scratch, compiler_params=cp,
    )(x, *dh_args, g, sc)


def ffn_act_bwd(u0, df, cw, cb, *, name, tc=128):
    nb = DFF // tc

    def body(u_ref, df_ref, wg_ref, wv_ref, bg_ref, bv_ref, du_ref, dw_ref, db_ref):
        wg = [wg_ref[k:k + 1, :] for k in range(3)]
        wv = [wv_ref[k:k + 1, :] for k in range(3)]
        bg, bv = bg_ref[...], bv_ref[...]

        def chunk(r0, first, last, acc):
            xg, xv = _ext_rows(u_ref.at[0], r0, first, last), _ext_rows(u_ref.at[1], r0, first, last)
            dfe = _ext_rows(df_ref, r0, first, last)
            xg_d, xg_u, xv_d, xv_u = _roll_dn(xg), _roll_up(xg), _roll_dn(xv), _roll_up(xv)
            ug = bg + xg_d * wg[0] + xg * wg[1] + xg_u * wg[2]
            uv = bv + xv_d * wv[0] + xv * wv[1] + xv_u * wv[2]
            sig = jax.nn.sigmoid(ug)
            dug = dfe * uv * (sig * (1.0 + ug * (1.0 - sig)))
            duv = dfe * (ug * sig)
            rows = _center_rows(r0, first, last)
            du_ref[0, rows, :] = (_roll_up(dug) * wg[0] + dug * wg[1] + _roll_dn(dug) * wg[2])[_CTR].astype(BF)
            du_ref[1, rows, :] = (_roll_up(duv) * wv[0] + duv * wv[1] + _roll_dn(duv) * wv[2])[_CTR].astype(BF)
            terms = [dug * xg_d, dug * xg, dug * xg_u, dug, duv * xv_d, duv * xv, duv * xv_u, duv]
            return tuple(a + jnp.sum(t[_CTR], axis=0, keepdims=True) for a, t in zip(acc, terms))

        acc = _row_chunks(T, chunk, tuple(jnp.zeros((1, tc), F32) for _ in range(8)))
        for k in range(3):
            dw_ref[0, k:k + 1, :] = acc[k]
            dw_ref[1, k:k + 1, :] = acc[4 + k]
        db_ref[0] = acc[3]
        db_ref[1] = acc[7]

    lo = lambda r: pl.BlockSpec((r, tc), lambda j: (0, j))
    hi = lambda r: pl.BlockSpec((r, tc), lambda j: (0, nb + j))
    st = lambda r: pl.BlockSpec((2, r, tc), lambda j: (0, 0, j))
    return pl.pallas_call(
        body, name=name, grid=(nb,),
        in_specs=[st(T), lo(T), lo(3), hi(3), lo(1), hi(1)],
        out_specs=[st(T), st(3), st(1)],
        out_shape=[jax.ShapeDtypeStruct((2, T, DFF), BF), jax.ShapeDtypeStruct((2, 3, DFF), F32),
                   jax.ShapeDtypeStruct((2, 1, DFF), F32)],
        compiler_params=pltpu.CompilerParams(dimension_semantics=("parallel",)),
    )(u0, df, cw, cw, cb, cb)


def convz_bwd(p, dz, cw, cb, dp, *, name):
    o0 = O_CV // (3 * CVB)

    def body(p_ref, dz_ref, w_ref, bias_ref, dp_in, dp_ref, dw_ref, dbias_ref):
        xv, bv, cv = p_ref[:, 0:CVB], p_ref[:, CVB:2 * CVB], p_ref[:, 2 * CVB:3 * CVB]
        ci = cv * xv
        dwc = _conv(ci, w_ref, bias_ref)
        dzv = dz_ref[...]
        ddw = dzv * bv
        dci = _conv_t(ddw, w_ref)
        dp_ref[:, 0:CVB] = (dci * cv).astype(BF)
        dp_ref[:, CVB:2 * CVB] = (dzv * dwc).astype(BF)
        dp_ref[:, 2 * CVB:3 * CVB] = (dci * xv).astype(BF)
        _conv_wgrad(dw_ref, ddw, ci)
        dbias_ref[...] = jnp.sum(ddw, axis=0, keepdims=True)

    own = lambda r: pl.BlockSpec((r, CVB), lambda j: (0, j))
    return pl.pallas_call(
        body, name=name, grid=(CONV // CVB,),
        in_specs=[pl.BlockSpec((T, 3 * CVB), lambda j: (0, o0 + j)), own(T), own(3), own(1),
                  pl.BlockSpec(memory_space=pl.ANY)],
        out_specs=[pl.BlockSpec((T, 3 * CVB), lambda j: (0, o0 + j)), own(3), own(1)],
        out_shape=[jax.ShapeDtypeStruct((T, NIN), BF), jax.ShapeDtypeStruct((3, CONV), F32),
                   jax.ShapeDtypeStruct((1, CONV), F32)],
        input_output_aliases={4: 0},
        compiler_params=pltpu.CompilerParams(dimension_semantics=("parallel",)),
    )(p, dz, cw, cb, dp)


def attn_bwd(q, k, v, do, *, name, tq=1024, kc=768):
    NKC, KC = TKV // kc, kc

    def body(q_ref, k_ref, v_ref, do_ref, dq_ref, dk_ref, dv_ref, s_scr, dp_scr):
        h, i = pl.program_id(0), pl.program_id(1)

        @pl.when(i == 0)
        def _():
            dk_ref[...] = jnp.zeros_like(dk_ref)

        @pl.when((i == 0) & (h % 2 == 0))
        def _():
            dv_ref[...] = jnp.zeros_like(dv_ref)

        qv = q_ref[...]
        dom = jnp.where(_head_mask(h), do_ref[...], jnp.zeros_like(do_ref[...]))
        m = _scores_pass(qv, k_ref, s_scr, kc)
        l = jnp.zeros((tq, 1), F32)
        dsum = jnp.zeros((tq, 1), F32)
        for c in range(NKC):
            cols = slice(c * KC, (c + 1) * KC)
            e = jnp.exp2(s_scr[:, cols] - m)
            s_scr[:, cols] = e
            dp = lax.dot_general(dom, v_ref[cols, :], (((1,), (1,)), ((), ())), preferred_element_type=F32)
            dp_scr[:, cols] = dp
            l = l + jnp.sum(e, axis=-1, keepdims=True)
            dsum = dsum + jnp.sum(e * dp, axis=-1, keepdims=True)
        inv = 1.0 / l
        delta = dsum * inv
        dos = (dom.astype(F32) * inv).astype(BF)
        dq = jnp.zeros((tq, HP), F32)
        for c in range(NKC):
            cols = slice(c * KC, (c + 1) * KC)
            e = s_scr[:, cols]
            ds = (e * (dp_scr[:, cols] - delta) * (inv * SCALE)).astype(BF)
            dq = dq + jnp.dot(ds, k_ref[cols, :], preferred_element_type=F32)
            dk_ref[cols, :] += lax.dot_general(ds, qv, (((0,), (0,)), ((), ())), preferred_element_type=F32)
            dv_ref[cols, :] += lax.dot_general(e.astype(BF), dos, (((0,), (0,)), ((), ())), preferred_element_type=F32)
        dq_ref[...] = dq

    return pl.pallas_call(
        body, name=name, grid=(NH, T // tq),
        in_specs=[pl.BlockSpec((tq, HP), lambda h, i: (i, h)), pl.BlockSpec((TKV, HP), lambda h, i: (0, h)),
                  pl.BlockSpec((TKV, 2 * DV), lambda h, i: (0, h // 2)), pl.BlockSpec((tq, 2 * DV), lambda h, i: (i, h // 2))],
        out_specs=[pl.BlockSpec((tq, HP), lambda h, i: (i, h)), pl.BlockSpec((TKV, HP), lambda h, i: (0, h)),
                   pl.BlockSpec((TKV, 2 * DV), lambda h, i: (0, h // 2))],
        out_shape=[jax.ShapeDtypeStruct((T, NH * HP), F32), jax.ShapeDtypeStruct((TKV, NH * HP), F32),
                   jax.ShapeDtypeStruct((TKV, NH * DV), F32)],
        scratch_shapes=[pltpu.VMEM((tq, TKV), F32), pltpu.VMEM((tq, TKV), F32)],
        compiler_params=pltpu.CompilerParams(dimension_semantics=("arbitrary", "arbitrary")),
    )(q, k, v, do)


def qprep_bwd(p, dq, qg, wq2, cq_t, sq_t, dp, *, name, tm=256):
    qcol = O_Q // 512

    def body(p_ref, dq_ref, g_ref, w_ref, c_ref, s_ref, dp_in, dp_ref, dq2_ref, dg_ref):
        i = pl.program_id(0)
        dqv = dq_ref[...]
        cc = jnp.concatenate([c_ref[...]] * NH, axis=1)
        ss = jnp.concatenate([s_ref[...]] * NH, axis=1)
        dq2 = jnp.concatenate([dqv * cc, dqv * ss], axis=1).astype(BF)
        dq2_ref[...] = dq2
        dcq = lax.dot_general(dq2, w_ref[...], (((1,), (1,)), ((), ())), preferred_element_type=F32)
        pq = p_ref[...]
        r = lax.rsqrt(jnp.sum(pq * pq, axis=-1, keepdims=True) * (1.0 / QL) + EPS)
        xh = pq * r
        a = dcq * g_ref[...]
        dp_ref[...] = (r * (a - xh * (jnp.sum(a * xh, axis=-1, keepdims=True) * (1.0 / QL)))).astype(BF)
        dg = jnp.sum(dcq * xh, axis=0, keepdims=True)

        @pl.when(i == 0)
        def _():
            dg_ref[...] = dg

        @pl.when(i > 0)
        def _():
            dg_ref[...] += dg

    return pl.pallas_call(
        body, name=name, grid=(T // tm,),
        in_specs=[pl.BlockSpec((tm, 512), lambda i: (i, qcol)), pl.BlockSpec((tm, NH * HP), lambda i: (i, 0)), _row(512),
                  pl.BlockSpec((512, 2 * NH * HP), lambda i: (0, 0)),
                  pl.BlockSpec((tm, HP), lambda i: (i, 0)), pl.BlockSpec((tm, HP), lambda i: (i, 0)),
                  pl.BlockSpec(memory_space=pl.ANY)],
        out_specs=[pl.BlockSpec((tm, 512), lambda i: (i, qcol)), pl.BlockSpec((tm, 2 * NH * HP), lambda i: (i, 0)), _row(512)],
        out_shape=[jax.ShapeDtypeStruct((T, NIN), BF), jax.ShapeDtypeStruct((T, 2 * NH * HP), BF),
                   jax.ShapeDtypeStruct((1, 512), F32)],
        input_output_aliases={6: 0},
        compiler_params=pltpu.CompilerParams(dimension_semantics=("arbitrary",)),
    )(p, dq, qg, wq2, cq_t, sq_t, dp)


def kvprep_bwd(pc, p, dk, dv, kvg, wkv2, ck, sk, dp, *, name, tm=256):
    assert tm == TC
    nb = TKV // tm
    kvcol = O_KV // 512

    def body(pc_ref, p_ref, dk_ref, dv_ref, g_ref, w_ref, ck_ref, sk_ref, dp_in, dp_ref, dpc_ref, dkv2_ref, dg_ref):
        i = pl.program_id(0)
        t = jnp.where(i == NLAT, pc_ref[...], p_ref[...])
        pk = t[:, :KVL]
        r = lax.rsqrt(jnp.mean(pk * pk, axis=-1, keepdims=True) + EPS)
        xh = pk * r
        dkv = dk_ref[...]
        dkv2 = jnp.concatenate([dkv, dv_ref[...]], axis=1).astype(BF)
        dkv2_ref[...] = dkv2
        dckv = lax.dot_general(dkv2, w_ref[...], (((1,), (1,)), ((), ())), preferred_element_type=F32)
        a = dckv * g_ref[...]
        dpk = r * (a - xh * jnp.mean(a * xh, axis=-1, keepdims=True))
        dkr = dkv[:, 0:HP]
        for hh in range(1, NH):
            dkr = dkr + dkv[:, hh * HP:(hh + 1) * HP]
        res = jnp.concatenate([dpk, dkr * ck_ref[...], dkr * sk_ref[...]], axis=1).astype(BF)
        dg = jnp.sum(dckv * xh, axis=0, keepdims=True)

        @pl.when(i == 0)
        def _():
            dg_ref[...] = dg

        @pl.when(i > 0)
        def _():
            dg_ref[...] += dg

        @pl.when(i < NLAT)
        def _():
            dp_ref[...] = res

        @pl.when(i == NLAT)
        def _():
            dpc_ref[...] = res

    rb = lambda w: pl.BlockSpec((tm, w), lambda i: (i, 0))
    return pl.pallas_call(
        body, name=name, grid=(nb,),
        in_specs=[pl.BlockSpec((tm, 512), lambda i: (0, 0)),
                  pl.BlockSpec((tm, 512), lambda i: (jnp.minimum(i, NLAT - 1), kvcol)),
                  rb(NH * HP), rb(NH * DV), _row(KVL), pl.BlockSpec((KVL, NH * HP + NH * DV), lambda i: (0, 0)),
                  rb(HP), rb(HP), pl.BlockSpec(memory_space=pl.ANY)],
        out_specs=[pl.BlockSpec((tm, 512), lambda i: (jnp.minimum(i, NLAT - 1), kvcol)),
                   pl.BlockSpec((tm, 512), lambda i: (0, 0)), rb(NH * HP + NH * DV), _row(KVL)],
        out_shape=[jax.ShapeDtypeStruct((T, NIN), BF), jax.ShapeDtypeStruct((TC, 512), BF),
                   jax.ShapeDtypeStruct((TKV, NH * HP + NH * DV), BF), jax.ShapeDtypeStruct((1, KVL), F32)],
        input_output_aliases={8: 0},
        compiler_params=pltpu.CompilerParams(dimension_semantics=("arbitrary",)),
    )(pc, p, dk, dv, kvg, wkv2, ck, sk, dp)


def _pieces(src, width, n):
    out, c = [], src
    while c < src + width:
        k = c // n
        w = min(src + width, (k + 1) * n) - c
        out.append((k, c - k * n, c - src, w))
        c += w
    return out


def _win_moves():
    mv = [(2208, 1024, O_GA), (3232, 1024, O_GC), (0, KVL, O_KV), (256, DR, O_KV + KVL + DN), (288, QL, O_Q)]
    mv += [(256 + _swap_start(g), 8, O_KV + KVL + HP + DN + 8 * g) for g in range(4)]
    for j in range(CONV // CVB):
        base = O_CV + 3 * CVB * j
        mv += [(672 + CVB * j, CVB, base), (1184 + CVB * j, CVB, base + CVB), (1696 + CVB * j, CVB, base + 2 * CVB)]
    return mv


_WIN_ZERO = [(O_KV + KVL, DN), (O_KV + KVL + DN + DR, HP - DN - DR), (O_KV + KVL + HP, DN),
             (O_KV + KVL + HP + DN + DR, HP - DN - DR), (O_Q + QL, 512 - QL)]


def build_win(g, *, name, tm=256):
    def body(g_ref, o_ref):
        for src, w, dst in _win_moves():
            for k, a, off, pw in _pieces(src, w, SH_IN):
                o_ref[:, dst + off:dst + off + pw] = g_ref[k, :, a:a + pw]
        for c0, w in _WIN_ZERO:
            o_ref[:, c0:c0 + w] = jnp.zeros((tm, w), o_ref.dtype)

    return pl.pallas_call(
        body, name=name, grid=(D // tm,), in_specs=[pl.BlockSpec((NDEV, tm, SH_IN), lambda i: (0, i, 0))],
        out_specs=pl.BlockSpec((tm, NIN), lambda i: (i, 0)), out_shape=jax.ShapeDtypeStruct((D, NIN), g.dtype),
        compiler_params=pltpu.CompilerParams(dimension_semantics=("parallel",)),
    )(g)


def shard_win_grad(dwt, dwct, *, name, tc=256):
    def body(dw_ref, dwc_ref, o_ref, kvs):
        kvs[...] = dw_ref[O_KV:O_KV + 512, :] + dwc_ref[...]

        def src(row, w):
            if O_KV <= row < O_KV + 512:
                return kvs[row - O_KV:row - O_KV + w, :]
            return dw_ref[row:row + w, :]

        for s, w, dst in _win_moves():
            if w == 8 or s == 256:
                continue
            for k, a, off, pw in _pieces(s, w, SH_IN):
                o_ref[k, a:a + pw, :] = src(dst + off, pw).astype(o_ref.dtype)
        for g in range(4):
            val = src(O_KV + KVL + DN + 8 * g, 8) + src(O_KV + KVL + HP + DN + _swap_start(g), 8)
            o_ref[0, 256 + 8 * g:256 + 8 * g + 8, :] = val.astype(o_ref.dtype)

    return pl.pallas_call(
        body, name=name, grid=(D // tc,),
        in_specs=[pl.BlockSpec((NIN, tc), lambda j: (0, j)), pl.BlockSpec((512, tc), lambda j: (0, j))],
        out_specs=pl.BlockSpec((NDEV, SH_IN, tc), lambda j: (0, 0, j)),
        out_shape=jax.ShapeDtypeStruct((NDEV, SH_IN, D), BF),
        scratch_shapes=[pltpu.VMEM((512, tc), F32)],
        compiler_params=pltpu.CompilerParams(dimension_semantics=("parallel",)),
    )(dwt, dwct)


def build_wq_wkv(gq, gkv, *, name):
    def body(gq_ref, gkv_ref, q_ref, kv_ref):
        q_ref[...] = jnp.zeros_like(q_ref)
        kv_ref[...] = jnp.zeros_like(kv_ref)
        for h in range(NH):
            q_ref[0:QL, h * HP:h * HP + DN + DR] = gq_ref[h]
            for g in range(4):
                c0 = NH * HP + h * HP + DN + 8 * g
                q_ref[0:QL, c0:c0 + 8] = gq_ref[h, :, DN + _swap_start(g):DN + _swap_start(g) + 8]
            kv_ref[:, h * HP:h * HP + DN] = gkv_ref[h, :, 0:DN]
            kv_ref[:, NH * HP + h * DV:NH * HP + (h + 1) * DV] = gkv_ref[h, :, DN:DN + DV]

    vm = pl.BlockSpec(memory_space=pltpu.VMEM)
    return pl.pallas_call(
        body, name=name, in_specs=[vm, vm], out_specs=[vm, vm],
        out_shape=[jax.ShapeDtypeStruct((512, 2 * NH * HP), gq.dtype), jax.ShapeDtypeStruct((KVL, NH * HP + NH * DV), gq.dtype)],
    )(gq, gkv)


def shard_wq_wkv_grad(dwq2, dwkv2, *, name):
    def body(q_ref, kv_ref, gq_ref, gkv_ref):
        for h in range(NH):
            gq_ref[h, :, 0:DN] = q_ref[0:QL, h * HP:h * HP + DN].astype(BF)
            for g in range(4):
                a = q_ref[0:QL, h * HP + DN + 8 * g:h * HP + DN + 8 * g + 8]
                c0 = NH * HP + h * HP + DN + _swap_start(g)
                gq_ref[h, :, DN + 8 * g:DN + 8 * g + 8] = (a + q_ref[0:QL, c0:c0 + 8]).astype(BF)
            gkv_ref[h, :, 0:DN] = kv_ref[:, h * HP:h * HP + DN].astype(BF)
            gkv_ref[h, :, DN:DN + DV] = kv_ref[:, NH * HP + h * DV:NH * HP + (h + 1) * DV].astype(BF)

    vm = pl.BlockSpec(memory_space=pltpu.VMEM)
    return pl.pallas_call(
        body, name=name, in_specs=[vm, vm], out_specs=[vm, vm],
        out_shape=[jax.ShapeDtypeStruct((NDEV, QL, (DN + DR)), BF), jax.ShapeDtypeStruct((NDEV, KVL, DN + DV), BF)],
    )(dwq2, dwkv2)


def unshard_cols(g, *, name, tm=256):
    _, K, n = g.shape
    tm = _pick(K, tm, 16)

    def body(g_ref, o_ref):
        for k in range(NDEV):
            o_ref[:, k * n:(k + 1) * n] = g_ref[k]

    return pl.pallas_call(
        body, name=name, grid=(K // tm,), in_specs=[pl.BlockSpec((NDEV, tm, n), lambda i: (0, i, 0))],
        out_specs=pl.BlockSpec((tm, NDEV * n), lambda i: (i, 0)), out_shape=jax.ShapeDtypeStruct((K, NDEV * n), g.dtype),
        compiler_params=pltpu.CompilerParams(dimension_semantics=("parallel",)),
    )(g)


def shard_cols(w, *, name, tm=256):
    K, n8 = w.shape
    n = n8 // NDEV
    tm = _pick(K, tm, 16)

    def body(w_ref, o_ref):
        for k in range(NDEV):
            o_ref[k] = w_ref[:, k * n:(k + 1) * n]

    return pl.pallas_call(
        body, name=name, grid=(K // tm,), in_specs=[pl.BlockSpec((tm, n8), lambda i: (i, 0))],
        out_specs=pl.BlockSpec((NDEV, tm, n), lambda i: (0, i, 0)), out_shape=jax.ShapeDtypeStruct((NDEV, K, n), w.dtype),
        compiler_params=pltpu.CompilerParams(dimension_semantics=("parallel",)),
    )(w)


def _rope_tables():
    t = np.arange(T)
    row = (t // GRID_W).astype(np.float32)
    col = (t % GRID_W).astype(np.float32)
    axis_dim = DR // 2
    inv = (np.float32(ROPE_THETA) ** (-np.arange(0, axis_dim, 2, dtype=np.float32) / np.float32(axis_dim))).astype(np.float32)
    ar, ac = (row[:, None] * inv).astype(np.float32), (col[:, None] * inv).astype(np.float32)
    cosv = np.concatenate([np.cos(ar), np.cos(ar), np.cos(ac), np.cos(ac)], axis=1).astype(np.float32)
    sinv = np.concatenate([-np.sin(ar), np.sin(ar), -np.sin(ac), np.sin(ac)], axis=1).astype(np.float32)
    ck = np.zeros((TKV, HP), np.float32)
    sk = np.zeros((TKV, HP), np.float32)
    ck[T:, DN:DN + DR] = 1.0
    ck[:T, DN:DN + DR] = cosv
    sk[:T, DN:DN + DR] = sinv
    cq = np.zeros((T, HP), np.float32)
    cq[:, :DN] = 1.0
    cq[:, DN:DN + DR] = cosv
    return jnp.asarray(ck), jnp.asarray(sk), jnp.asarray(cq), jnp.asarray(sk[:T])


def _local_step(x, ctx, tgt, mod_lat, mod_ctx, n1g, qg, kvg, n2g, fg, conv_w, conv_b, ffn_w, ffn_b, get_w, put_g, dep0):
    sh1, sc1, g1, sh2, sc2, g2 = [mod_lat[:, i * D:(i + 1) * D] for i in range(6)]
    csh1, csc1 = mod_ctx[:, 0:D], mod_ctx[:, D:2 * D]
    ck, sk, cq_t, sq_t = _rope_tables()
    qg_p = jnp.pad(qg, ((0, 0), (0, 512 - QL)))

    hcat = normmod_cat(ctx, x, n1g, csc1, csh1, sc1, sh1, dep0, name="normmod1")
    win = get_w("in", hcat)
    p = mm(hcat, win, M=T, tn=768, name="in_proj")
    pc = mm(hcat, win, M=TC, N=512, a_off=(T, 0), b_off=(0, O_KV), name="in_proj_ctx")
    wq2, wkv2, wao, wco, wo = get_w("mid", p)
    kh, vh, ckv = kvprep(pc, p, kvg, wkv2, ck, sk, name="kvprep")
    qr, cq = qprep(p, qg_p, wq2, cq_t, sq_t, name="qprep")
    o = attn_fwd(qr, kh, vh, name="attn_fwd")
    z = convz(p, conv_w, conv_b, name="convz")
    ya, yc, merged = out_proj_merge(o, wao, z, wco, p, name="attn_conv_out_gate_merge")
    a_out, x1, h2 = oproj_resid(merged, wo, x, g1, n2g, sc2, sh2, name="o_proj_resid_normmod2")
    wup, wdn = get_w("ffn", h2)
    u0 = mm(h2, wup, tb=True, o_stack=True, tn=1408, name="up_proj")
    f = ffn_act(u0, ffn_w, ffn_b, name="ffn_act")
    dn, dx2, dd, dfg, loss = down_final(f, wdn, x1, g2, fg, tgt, name="down_proj_final_loss")

    df = mm(dd, wdn, tb=True, tn=1408, name="down_proj_dx")
    dwdn = mm(f, dd, ta=True, out_dtype=BF, tm=1408, name="down_proj_dw")
    du0, dffn_w, dffn_b = ffn_act_bwd(u0, df, ffn_w, ffn_b, name="ffn_act_bwd")
    dwup = mm(du0, h2, ta=True, a_stack=True, out_dtype=BF, tm=1408, name="up_proj_dw")
    tok = put_g("ffn", dict(dwup=dwup, dwdn=dwdn))
    dx1, da, st2 = normmod_bwd(x1, dict(a=du0, b=wup, a_stack=True, tk=1408, dep=tok), n2g, sc2, dx2, dn, g1,
                               name="up_proj_dx_normmod2_bwd")

    dwo = mm(merged, da, ta=True, out_dtype=BF, tn=512, name="o_proj_dw")
    dya, dyc, dp = oproj_dx_gate_bwd(da, wo, p, ya, yc, name="o_proj_dx_gate_merge_bwd")
    do = mm(dya, wao, tb=True, out_dtype=BF, name="attn_out_dx")
    dwao = mm(o, dya, ta=True, out_dtype=BF, tn=512, name="attn_out_dw")
    dwco = mm(z, dyc, ta=True, out_dtype=BF, tn=512, name="conv_out_dw")
    tok = put_g("mid", dict(dwao=dwao, dwco=dwco, dwo=dwo))
    dz = mm(dyc, wco, tb=True, dep=tok, name="conv_out_dx")
    dp, dconv_w, dconv_b = convz_bwd(p, dz, conv_w, conv_b, dp, name="convz_bwd")
    dq, dk, dv = attn_bwd(qr, kh, vh, do, name="attn_bwd")
    dp, dq2, dqg = qprep_bwd(p, dq, qg_p, wq2, cq_t, sq_t, dp, name="qprep_bwd")
    dwq2 = mm(cq, dq2, ta=True, name="q_up_dw")
    dp, dpc, dkv2, dkvg = kvprep_bwd(pc, p, dk, dv, kvg, wkv2, ck, sk, dp, name="kvprep_bwd")
    dwkv2 = mm(ckv, dkv2, ta=True, name="kv_up_dw")
    tok = put_g("qkv", dict(dwq2=dwq2, dwkv2=dwkv2))

    dwin = mm(dp, hcat, ta=True, K=T, tm=768, dep=tok, name="in_proj_dw")
    dwin_c = mm(dpc, hcat, ta=True, K=TC, b_off=(T, 0), name="in_proj_ctx_dw")
    tok = put_g("in", dict(dwin=dwin, dwin_c=dwin_c))
    dhc = mm(dpc, win, tb=True, N=D, K=512, b_off=(0, O_KV), name="in_proj_ctx_dx")
    dx, _, st1 = normmod_bwd(x, dict(a=dp, b=win, tb=True, tk=1536, dep=tok), n1g, sc1, dx1, a_out, g1,
                             name="in_proj_dx_normmod1_bwd")
    stc = normmod_bwd(ctx, dhc, n1g, csc1, None, None, None, name="normmod1_ctx_bwd")

    zrow = jnp.zeros((1, D), F32)
    dmod_lat = jnp.concatenate([st1[0:1], st1[1:2], st1[3:4], st2[0:1], st2[1:2], st2[3:4]], axis=1)
    dmod_ctx = jnp.concatenate([stc[0:1], stc[1:2], zrow, zrow, zrow, zrow], axis=1)
    return dict(
        loss=loss, dx=dx, dmod_lat=dmod_lat, dmod_ctx=dmod_ctx,
        dn1g=st1[2:3] + stc[2:3], dqg=dqg, dkvg=dkvg, dn2g=st2[2:3], dfg=dfg,
        dconv_w=dconv_w, dconv_b=dconv_b, dffn_w=dffn_w, dffn_b=dffn_b)


def _me():
    x, y, c = lax.axis_index("x"), lax.axis_index("y"), lax.axis_index("c")
    return x, y, c, 4 * x + 2 * y + c


def _peer(x, y, c, k):
    px = 1 - x if k & 4 else x
    py = 1 - y if k & 2 else y
    pc = 1 - c if k & 1 else c
    return (px, py, pc), 4 * px + 2 * py + pc


def _exchange_tiles(src_of_peer, buf, send_sem, recv_sem):
    x, y, c, me = _me()
    for k in range(1, NDEV):
        dev, lin = _peer(x, y, c, k)
        pltpu.make_async_remote_copy(src_ref=src_of_peer(lin), dst_ref=buf.at[me], send_sem=send_sem, recv_sem=recv_sem,
                                     device_id=dev, device_id_type=MESH).start()
    seven = buf.at[pl.ds(0, NDEV - 1)]
    pltpu.make_async_remote_copy(src_ref=seven, dst_ref=seven, send_sem=send_sem, recv_sem=recv_sem,
                                 device_id=(x, y, c), device_id_type=MESH).wait()


def _silu(z):
    return z * jax.nn.sigmoid(z)


def ada_fwd(c, c_ctx, ffn_w, conv_w, w_shard, b_shard, deps, *, name):
    nsh = w_shard.shape[1]
    deps = [d for d in deps if d is not None]

    def body(c_ref, cc_ref, fw_ref, cw_ref, w_ref, b_ref, *rest):
        s_ref, m_ref, mine, res, sems = rest[len(deps):]
        x, y, c, me = _me()
        mine[0:1, :] = _silu(c_ref[...])
        mine[1:2, :] = _silu(cc_ref[...])
        mine[2:5, :] = fw_ref[...]
        mine[5:8, :] = cw_ref[...]
        s_ref[me] = mine[...]
        _exchange_tiles(lambda lin: mine, s_ref, sems.at[0], sems.at[1])
        sall = s_ref[...].reshape(NDEV * 8, D).astype(BF)
        r = jnp.dot(sall, w_ref[...].astype(BF), preferred_element_type=F32) + b_ref[...]
        res[...] = r.reshape(NDEV, 8, nsh)
        m_ref[me] = res[me]
        _exchange_tiles(lambda lin: res.at[lin], m_ref, sems.at[2], sems.at[3])

    vm = pl.BlockSpec(memory_space=pltpu.VMEM)
    return pl.pallas_call(
        body, name=name, in_specs=[vm] * 6 + [pl.BlockSpec(memory_space=pl.ANY)] * len(deps), out_specs=[vm, vm],
        out_shape=[jax.ShapeDtypeStruct((NDEV, 8, D), F32), jax.ShapeDtypeStruct((NDEV, 8, nsh), F32)],
        scratch_shapes=[pltpu.VMEM((8, D), F32), pltpu.VMEM((NDEV, 8, nsh), F32), pltpu.SemaphoreType.DMA((4,))],
    )(c, c_ctx, ffn_w, conv_w, w_shard, b_shard, *deps)


P_DML, P_DMC, P_N1, P_QG, P_KVG, P_CB, P_N2, P_FB, P_FG, P_CW, P_FW, P_LOSS, P_ROWS = 0, 6, 12, 13, 14, 15, 16, 17, 23, 24, 27, 45, 48
FROWS = 3


def sync_small(r, deps, *, name):
    ins = [r["dmod_lat"], r["dmod_ctx"], r["dn1g"], r["dqg"], r["dkvg"], r["dconv_b"], r["dn2g"], r["dffn_b"], r["dfg"],
           r["dconv_w"], r["dffn_w"], r["loss"]]

    def put_wide(p, row0, row, n):
        for j in range(-(-n // D)):
            w = min(D, n - j * D)
            p[row0 + j:row0 + j + 1, 0:w] = row[:, j * D:j * D + w]

    def body(dml, dmc, n1, qg, kvg, cb, n2, fb, fg, cw, fw, loss, *rest):
        a_ref, sum_ref, p, sems = rest[len(deps):]
        x, y, c, me = _me()
        p[...] = jnp.zeros_like(p)
        put_wide(p, P_DML, dml, 6 * D)
        put_wide(p, P_DMC, dmc, 6 * D)
        put_wide(p, P_N1, n1, D)
        put_wide(p, P_QG, qg, 512)
        put_wide(p, P_KVG, kvg, KVL)
        put_wide(p, P_CB, cb, CONV)
        put_wide(p, P_N2, n2, D)
        put_wide(p, P_FG, fg, D)
        put_wide(p, P_LOSS, loss, 128)
        for s in range(2):
            put_wide(p, P_FB + FROWS * s, fb.at[s], DFF)
        for k in range(3):
            put_wide(p, P_CW + k, cw.at[k:k + 1], CONV)
            for s in range(2):
                put_wide(p, P_FW + FROWS * (2 * k + s), fw.at[s, k:k + 1], DFF)
        a_ref[me] = p[...]
        _exchange_tiles(lambda lin: p, a_ref, sems.at[0], sems.at[1])
        acc = a_ref[0]
        for k in range(1, NDEV):
            acc = acc + a_ref[k]
        sum_ref[...] = acc

    vm = pl.BlockSpec(memory_space=pltpu.VMEM)
    return pl.pallas_call(
        body, name=name, in_specs=[vm] * len(ins) + [pl.BlockSpec(memory_space=pl.ANY)] * len(deps), out_specs=[vm, vm],
        out_shape=[jax.ShapeDtypeStruct((NDEV, P_ROWS, D), F32), jax.ShapeDtypeStruct((P_ROWS, D), F32)],
        scratch_shapes=[pltpu.VMEM((P_ROWS, D), F32), pltpu.SemaphoreType.DMA((2,))],
    )(*ins, *deps)


def ada_bwd(s_all, dml, dmc, w_shard, c_ctx, *, name):
    nsh = w_shard.shape[1]

    def body(s_ref, dml_ref, dmc_ref, w_ref, c_ref, dw_ref, gc_ref, s16, dm16, part, buf, sems):
        x, y, c, me = _me()
        s16[...] = jnp.zeros_like(s16)
        dm16[...] = jnp.zeros_like(dm16)
        for k in range(NDEV):
            s16[k:k + 1, :] = s_ref[k, 0:1, :]
        s16[8:9, :] = s_ref[0, 1:2, :]
        dm16[0:8, :] = dml_ref[...]
        dm16[8:9, :] = dmc_ref[...]
        dw_ref[...] = lax.dot_general(s16[...].astype(BF), dm16[...].astype(BF), (((0,), (0,)), ((), ())),
                                      preferred_element_type=F32)
        part[...] = lax.dot_general(dm16[8:16, :].astype(BF), w_ref[...].astype(BF), (((1,), (1,)), ((), ())),
                                    preferred_element_type=F32)
        buf[me] = part[...]
        _exchange_tiles(lambda lin: part, buf, sems.at[0], sems.at[1])
        acc = buf[0]
        for k in range(1, NDEV):
            acc = acc + buf[k]
        z = c_ref[...]
        sg = jax.nn.sigmoid(z)
        gc_ref[...] = acc * (sg * (1.0 + z * (1.0 - sg)))

    vm = pl.BlockSpec(memory_space=pltpu.VMEM)
    return pl.pallas_call(
        body, name=name, in_specs=[vm] * 5, out_specs=[vm, vm],
        out_shape=[jax.ShapeDtypeStruct((D, nsh), F32), jax.ShapeDtypeStruct((8, D), F32)],
        scratch_shapes=[pltpu.VMEM((16, D), F32), pltpu.VMEM((16, nsh), F32), pltpu.VMEM((8, D), F32),
                        pltpu.VMEM((NDEV, 8, D), F32), pltpu.SemaphoreType.DMA((2,))],
    )(s_all, dml, dmc, w_shard, c_ctx)


HBM_SPEC = pl.BlockSpec(memory_space=pltpu.HBM)
SEM_SPEC = pl.BlockSpec(memory_space=pltpu.SEMAPHORE)
EFFECT = pltpu.SideEffectType.DATAFLOW_SIDE_EFFECTING


ALL_PEERS = tuple(range(1, NDEV))
FIRST_HOP = (1, 2, 4, 6)
RELAY = (2, 4, 6)


def _exchange_copies(srcs, lands, send, recv, per_peer, peers):
    x, y, c, me = _me()
    n = len(peers)
    cps = []
    for t in range(len(srcs)):
        for j, k in enumerate(peers):
            dev, lin = _peer(x, y, c, k)
            cps.append(pltpu.make_async_remote_copy(
                src_ref=srcs[t].at[lin] if per_peer else srcs[t], dst_ref=lands[t].at[me],
                send_sem=send.at[n * t + j], recv_sem=recv.at[n * t + j], device_id=dev, device_id_type=MESH))
    return cps


def _relay_copies(lands, send, recv):
    x, y, c, me = _me()
    n = len(RELAY)
    cps = []
    for t in range(len(lands)):
        for j, k in enumerate(RELAY):
            slot = lands[t].at[_peer(x, y, c, k)[1]]
            cps.append(pltpu.make_async_remote_copy(
                src_ref=slot, dst_ref=slot, send_sem=send.at[n * t + j], recv_sem=recv.at[n * t + j],
                device_id=(x, y, 1 - c), device_id_type=MESH))
    return cps


def _own_copies(srcs, lands, own, per_peer):
    me = _me()[3]
    return [pltpu.make_async_copy(srcs[t].at[me] if per_peer else srcs[t], lands[t].at[me], own.at[t])
            for t in range(len(srcs))]


def exchange_start(srcs, *, per_peer, name, dep=None, peers=ALL_PEERS):
    nt = len(srcs)
    ns = len(peers) * nt
    land_shapes = [(a.shape if per_peer else (NDEV,) + a.shape) for a in srcs]
    deps = [] if dep is None else [dep]

    def body(*refs):
        src, land = refs[:nt], refs[nt:2 * nt]
        send, recv, own = refs[2 * nt + len(deps):2 * nt + len(deps) + 3]
        for cp in _exchange_copies(src, land, send, recv, per_peer, peers) + _own_copies(src, land, own, per_peer):
            cp.start()
        refs[-1][...] = jnp.zeros_like(refs[-1])

    hb = lambda a: pltpu.with_memory_space_constraint(a, pltpu.HBM)
    outs = pl.pallas_call(
        body, name=name,
        out_shape=(pltpu.SemaphoreType.DMA((ns,)), pltpu.SemaphoreType.DMA((ns,)), pltpu.SemaphoreType.DMA((nt,)),
                   *[pltpu.HBM(a.shape, a.dtype) for a in srcs], *[pltpu.HBM(s, a.dtype) for s, a in zip(land_shapes, srcs)],
                   jax.ShapeDtypeStruct((8, 128), F32)),
        in_specs=[HBM_SPEC] * (2 * nt) + [pl.BlockSpec(memory_space=pl.ANY)] * len(deps),
        out_specs=(SEM_SPEC, SEM_SPEC, SEM_SPEC, *([HBM_SPEC] * (2 * nt)), pl.BlockSpec(memory_space=pltpu.VMEM)),
        input_output_aliases={i: 3 + i for i in range(2 * nt)},
        compiler_params=pltpu.CompilerParams(has_side_effects=EFFECT),
    )(*[hb(a) for a in srcs], *[hb(lax.empty(s, a.dtype)) for s, a in zip(land_shapes, srcs)], *deps)
    return dict(send=outs[0], recv=outs[1], own=outs[2], src=list(outs[3:3 + nt]), land=list(outs[3 + nt:3 + 2 * nt]),
                token=outs[-1], per_peer=per_peer, peers=peers)


def exchange_wait(h, after, *, name):
    nt = len(h["src"])
    per_peer, peers = h["per_peer"], h["peers"]

    def body(*refs):
        src, land, send, recv, own = refs[:nt], refs[nt:2 * nt], refs[2 * nt], refs[2 * nt + 1], refs[2 * nt + 2]
        for cp in _exchange_copies(src, land, send, recv, per_peer, peers):
            cp.wait_send()
            cp.wait_recv()
        for cp in _own_copies(src, land, own, per_peer):
            cp.wait()

    outs = pl.pallas_call(
        body, name=name,
        out_shape=(*[pltpu.HBM(a.shape, a.dtype) for a in h["src"]], *[pltpu.HBM(a.shape, a.dtype) for a in h["land"]]),
        in_specs=[HBM_SPEC] * (2 * nt) + [SEM_SPEC, SEM_SPEC, SEM_SPEC, pl.BlockSpec(memory_space=pl.ANY)],
        out_specs=tuple([HBM_SPEC] * (2 * nt)),
        input_output_aliases={i: i for i in range(2 * nt)},
        compiler_params=pltpu.CompilerParams(has_side_effects=EFFECT),
    )(*h["src"], *h["land"], h["send"], h["recv"], h["own"], after)
    return list(outs[nt:])


def relay_start(lands, *, name):
    nt = len(lands)
    ns = len(RELAY) * nt

    def body(*refs):
        for cp in _relay_copies(refs[:nt], refs[nt], refs[nt + 1]):
            cp.start()

    outs = pl.pallas_call(
        body, name=name,
        out_shape=(pltpu.SemaphoreType.DMA((ns,)), pltpu.SemaphoreType.DMA((ns,)),
                   *[pltpu.HBM(a.shape, a.dtype) for a in lands]),
        in_specs=[HBM_SPEC] * nt, out_specs=(SEM_SPEC, SEM_SPEC, *([HBM_SPEC] * nt)),
        input_output_aliases={i: 2 + i for i in range(nt)},
        compiler_params=pltpu.CompilerParams(has_side_effects=EFFECT),
    )(*lands)
    return dict(send=outs[0], recv=outs[1], land=list(outs[2:]))


def relay_wait(h, *, name):
    nt = len(h["land"])

    def body(*refs):
        for cp in _relay_copies(refs[:nt], refs[nt], refs[nt + 1]):
            cp.wait_send()
            cp.wait_recv()

    outs = pl.pallas_call(
        body, name=name, out_shape=tuple(pltpu.HBM(a.shape, a.dtype) for a in h["land"]),
        in_specs=[HBM_SPEC] * nt + [SEM_SPEC, SEM_SPEC], out_specs=tuple([HBM_SPEC] * nt),
        input_output_aliases={i: i for i in range(nt)},
        compiler_params=pltpu.CompilerParams(has_side_effects=EFFECT),
    )(*h["land"], h["send"], h["recv"])
    return list(outs)


def _adamw_math(w, g, m, v):
    nm = B1 * m + (1.0 - B1) * g
    nv = B2 * v + (1.0 - B2) * (g * g)
    m_hat = nm / (1.0 - B1 ** STEP)
    v_hat = nv / (1.0 - B2 ** STEP)
    return -LR * (m_hat / (jnp.sqrt(v_hat) + AEPS) + WD * w), nm, nv


def adamw_many(ws, gs, ms, vs, *, name):
    n = len(ws)

    def body(*refs):
        for k in range(n):
            d, nm, nv = _adamw_math(refs[k][...], refs[n + k][...], refs[2 * n + k][...], refs[3 * n + k][...])
            refs[4 * n + k][...] = d
            refs[5 * n + k][...] = nm
            refs[6 * n + k][...] = nv

    vm = pl.BlockSpec(memory_space=pltpu.VMEM)
    sh = [jax.ShapeDtypeStruct(w.shape, F32) for w in ws]
    outs = pl.pallas_call(body, name=name, in_specs=[vm] * (4 * n), out_specs=[vm] * (3 * n), out_shape=sh * 3,
                          )(*ws, *gs, *ms, *vs)
    return outs[:n], outs[n:2 * n], outs[2 * n:]


def adamw(w, g, m, v, *, name, tr=256):
    R, C = w.shape
    tr = _pick(R, tr, 8)

    def body(w_ref, g_ref, m_ref, v_ref, d_ref, nm_ref, nv_ref):
        d_ref[...], nm_ref[...], nv_ref[...] = _adamw_math(w_ref[...], g_ref[...], m_ref[...], v_ref[...])

    blk = pl.BlockSpec((tr, C), lambda i: (i, 0))
    sh = jax.ShapeDtypeStruct((R, C), F32)
    return pl.pallas_call(
        body, name=name, grid=(R // tr,), in_specs=[blk, blk, blk, blk], out_specs=[blk, blk, blk],
        out_shape=[sh, sh, sh], compiler_params=pltpu.CompilerParams(dimension_semantics=("parallel",)),
    )(w, g, m, v)


def adamw_slots(w, slots, m, v, *, name, tr=256):
    unit = w.ndim == 3
    R, C = w.shape[0], w.shape[-1]
    if R % 16 == 0:
        tr = _pick(R, tr, 16)
    else:
        tr = 144

    def body(w_ref, s_ref, m_ref, v_ref, g_ref, d_ref, nm_ref, nv_ref):
        g = s_ref[0].astype(F32)
        for k in range(1, NDEV):
            g = g + s_ref[k].astype(F32)
        g_ref[...] = g
        d_ref[...], nm_ref[...], nv_ref[...] = _adamw_math(w_ref[...], g, m_ref[...], v_ref[...])

    blk = pl.BlockSpec((tr, None, C), lambda i: (i, 0, 0)) if unit else pl.BlockSpec((tr, C), lambda i: (i, 0))
    sh = jax.ShapeDtypeStruct(w.shape, F32)
    return pl.pallas_call(
        body, name=name, grid=(pl.cdiv(R, tr),), in_specs=[blk, pl.BlockSpec((NDEV, tr, C), lambda i: (0, i, 0)), blk, blk],
        out_specs=[blk, blk, blk, blk], out_shape=[sh, sh, sh, sh],
        compiler_params=pltpu.CompilerParams(dimension_semantics=("parallel",)),
    )(w, slots, m, v)


def _padc(a, n=D):
    return jnp.pad(a, ((0, 0), (0, n - a.shape[1])))


def kernel(x, c, ctx, c_ctx, w_ada, b_ada, norm1_g, w_in, q_norm_g, kv_norm_g, w_uq, w_ukv, conv_w, conv_b, w_attn_out, w_conv_out, w_o, norm2_g, w_up, ffn_conv_w, ffn_conv_b, w_down, final_g, loss_target, m_c_ctx, m_w_ada, m_b_ada, m_norm1_g, m_w_in, m_q_norm_g, m_kv_norm_g, m_w_uq, m_w_ukv, m_conv_w, m_conv_b, m_w_attn_out, m_w_conv_out, m_w_o, m_norm2_g, m_w_up, m_ffn_conv_w, m_ffn_conv_b, m_w_down, m_final_g, v_c_ctx, v_w_ada, v_b_ada, v_norm1_g, v_w_in, v_q_norm_g, v_kv_norm_g, v_w_uq, v_w_ukv, v_conv_w, v_conv_b, v_w_attn_out, v_w_conv_out, v_w_o, v_norm2_g, v_w_up, v_ffn_conv_w, v_ffn_conv_b, v_w_down, v_final_g):
    me = 4 * lax.axis_index("x") + 2 * lax.axis_index("y") + lax.axis_index("c")
    W = dict(c_ctx=c_ctx, w_ada=w_ada, b_ada=b_ada, norm1_g=norm1_g, w_in=w_in, q_norm_g=q_norm_g, kv_norm_g=kv_norm_g,
             w_uq=w_uq, w_ukv=w_ukv, conv_w=conv_w, conv_b=conv_b, w_attn_out=w_attn_out, w_conv_out=w_conv_out, w_o=w_o,
             norm2_g=norm2_g, w_up=w_up, ffn_conv_w=ffn_conv_w, ffn_conv_b=ffn_conv_b, w_down=w_down, final_g=final_g)
    M = dict(c_ctx=m_c_ctx, w_ada=m_w_ada, b_ada=m_b_ada, norm1_g=m_norm1_g, w_in=m_w_in, q_norm_g=m_q_norm_g,
             kv_norm_g=m_kv_norm_g, w_uq=m_w_uq, w_ukv=m_w_ukv, conv_w=m_conv_w, conv_b=m_conv_b, w_attn_out=m_w_attn_out,
             w_conv_out=m_w_conv_out, w_o=m_w_o, norm2_g=m_norm2_g, w_up=m_w_up, ffn_conv_w=m_ffn_conv_w,
             ffn_conv_b=m_ffn_conv_b, w_down=m_w_down, final_g=m_final_g)
    V = dict(c_ctx=v_c_ctx, w_ada=v_w_ada, b_ada=v_b_ada, norm1_g=v_norm1_g, w_in=v_w_in, q_norm_g=v_q_norm_g,
             kv_norm_g=v_kv_norm_g, w_uq=v_w_uq, w_ukv=v_w_ukv, conv_w=v_conv_w, conv_b=v_conv_b, w_attn_out=v_w_attn_out,
             w_conv_out=v_w_conv_out, w_o=v_w_o, norm2_g=v_norm2_g, w_up=v_w_up, ffn_conv_w=v_ffn_conv_w,
             ffn_conv_b=v_ffn_conv_b, w_down=v_w_down, final_g=v_final_g)
    names = list(W)
    transposed = ("w_up",)
    as2d = lambda k, a: (a.reshape(1, -1) if a.ndim == 1 else
                         a[0].T if k in transposed else a.reshape(a.shape[-2], a.shape[-1]))
    W2 = {k: as2d(k, a) for k, a in W.items()}
    M2 = {k: as2d(k, a) for k, a in M.items()}
    V2 = {k: as2d(k, a) for k, a in V.items()}
    unit3 = lambda a: jnp.transpose(a, (2, 0, 1))
    W3, M3, V3 = unit3(W["w_in"]), unit3(M["w_in"]), unit3(V["w_in"])
    nsh = W2["w_ada"].shape[1]

    b_sh = lax.dynamic_slice(W2["b_ada"], (0, me * nsh), (1, nsh))
    s_all, m_all = ada_fwd(c, W2["c_ctx"], _padc(W2["ffn_conv_w"]), _padc(W2["conv_w"]), W2["w_ada"], b_sh, [],
                           name="ada_fwd")
    mod_lat = m_all[:, 0, :].reshape(1, 6 * D)
    mod_ctx = m_all[:, 1, :].reshape(1, 6 * D)
    ffn_w_full = s_all[:, 2:5, :2 * DFF // NDEV].transpose(1, 0, 2).reshape(3, 2 * DFF)
    conv_w_full = s_all[:, 5:8, :CONV // NDEV].transpose(1, 0, 2).reshape(3, CONV)

    stage_w = {"in": ["w_in"], "mid": ["w_uq", "w_ukv", "w_attn_out", "w_conv_out", "w_o"], "ffn": ["w_up", "w_down"]}
    two_level = ("in", "mid")
    ag, tok = {}, m_all
    for st, nms in stage_w.items():
        ag[st] = exchange_start([W2[nm].astype(BF) for nm in nms], per_peer=False, dep=tok, name="ag_start_" + st,
                                peers=FIRST_HOP if st in two_level else ALL_PEERS)
        tok = ag[st]["token"]

    def get_w(stage, after):
        lands = exchange_wait(ag[stage], after, name="ag_wait_" + stage)
        if stage in two_level:
            lands = relay_wait(relay_start(lands, name="ag_relay_" + stage), name="ag_relay_wait_" + stage)
        g = dict(zip(stage_w[stage], lands))
        if stage == "in":
            return build_win(g["w_in"], name="build_win")
        if stage == "mid":
            wq2, wkv2 = build_wq_wkv(g["w_uq"], g["w_ukv"], name="build_wq_wkv")
            return (wq2, wkv2, unshard_cols(g["w_attn_out"], name="unshard_w_attn_out"),
                    unshard_cols(g["w_conv_out"], name="unshard_w_conv_out"), g["w_o"].reshape(D, D))
        return g["w_up"].reshape(2 * DFF, D), g["w_down"].reshape(DFF, D)

    stage_g = {"ffn": ["w_up", "w_down"], "mid": ["w_attn_out", "w_conv_out", "w_o"], "qkv": ["w_uq", "w_ukv"],
               "in": ["w_in"]}
    rs = {}

    def put_g(stage, g):
        if stage == "in":
            parts = [shard_win_grad(g["dwin"], g["dwin_c"], name="shard_win_grad")]
        elif stage == "mid":
            parts = [shard_cols(g["dwao"], name="shard_w_attn_out"), shard_cols(g["dwco"], name="shard_w_conv_out"),
                     g["dwo"].reshape(NDEV, D // NDEV, D)]
        elif stage == "qkv":
            parts = list(shard_wq_wkv_grad(g["dwq2"], g["dwkv2"], name="shard_wq_wkv_grad"))
        else:
            parts = [g["dwup"].reshape(NDEV, 2 * DFF // NDEV, D), g["dwdn"].reshape(NDEV, DFF // NDEV, D)]
        rs[stage] = exchange_start(parts, per_peer=True, name="rs_start_" + stage)
        return rs[stage]["token"]

    r = _local_step(x[0], ctx[0], loss_target[0], mod_lat, mod_ctx, W2["norm1_g"], W2["q_norm_g"], W2["kv_norm_g"],
                    W2["norm2_g"], W2["final_g"], conv_w_full, W2["conv_b"], ffn_w_full, W2["ffn_conv_b"], get_w, put_g,
                    ag["ffn"]["token"])

    G, DL, NM, NV = {}, {}, {}, {}

    def finish(stage, after):
        for nm, sl in zip(stage_g[stage], exchange_wait(rs[stage], after, name="rs_wait_" + stage)):
            wmv = (W3, M3, V3) if nm == "w_in" else (W2[nm], M2[nm], V2[nm])
            G[nm], DL[nm], NM[nm], NV[nm] = adamw_slots(wmv[0], sl, wmv[1], wmv[2], name="adamw_" + nm)
            after = DL[nm]
        return after

    after = r["dx"]
    for st in ("ffn", "mid", "qkv"):
        after = finish(st, after)

    a_buf, ssum = sync_small(r, [DL[nm] for st in ("ffn", "mid", "qkv") for nm in stage_g[st]], name="sync_small")
    loss = ssum[P_LOSS, 0]
    G["norm1_g"] = ssum[P_N1:P_N1 + 1]
    G["q_norm_g"] = ssum[P_QG:P_QG + 1, :QL]
    G["kv_norm_g"] = ssum[P_KVG:P_KVG + 1, :KVL]
    G["conv_b"] = ssum[P_CB:P_CB + 1, :CONV]
    G["norm2_g"] = ssum[P_N2:P_N2 + 1]
    G["ffn_conv_b"] = ssum[P_FB:P_FB + 2 * FROWS].reshape(1, 2, FROWS * D)[:, :, :DFF].reshape(1, 2 * DFF)
    G["final_g"] = ssum[P_FG:P_FG + 1]
    G["conv_w"] = lax.dynamic_slice(ssum[P_CW:P_CW + 3, :CONV], (0, me * (CONV // NDEV)), (3, CONV // NDEV))
    fw_full = ssum[P_FW:P_FW + 6 * FROWS].reshape(3, 2, FROWS * D)[:, :, :DFF].reshape(3, 2 * DFF)
    G["ffn_conv_w"] = lax.dynamic_slice(fw_full, (0, me * (2 * DFF // NDEV)), (3, 2 * DFF // NDEV))
    G["b_ada"] = (ssum[P_DML:P_DML + 6] + ssum[P_DMC:P_DMC + 6]).reshape(1, 6 * D)

    dml = lax.dynamic_slice(a_buf[:, P_DML:P_DML + 6, :].reshape(NDEV, 6 * D), (0, me * nsh), (NDEV, nsh))
    dmc = lax.dynamic_slice(ssum[P_DMC:P_DMC + 6].reshape(1, 6 * D), (0, me * nsh), (1, nsh))
    G["w_ada"], gcc = ada_bwd(s_all, dml, dmc, W2["w_ada"], W2["c_ctx"], name="ada_bwd")
    G["c_ctx"] = gcc[0:1]

    DL["w_ada"], NM["w_ada"], NV["w_ada"] = adamw(W2["w_ada"], G["w_ada"], M2["w_ada"], V2["w_ada"], name="adamw_w_ada")
    small = ["c_ctx", "b_ada", "norm1_g", "q_norm_g", "kv_norm_g", "conv_b", "norm2_g", "ffn_conv_b", "final_g", "conv_w",
             "ffn_conv_w"]
    ds, nms, nvs = adamw_many([W2[k] for k in small], [G[k] for k in small], [M2[k] for k in small],
                              [V2[k] for k in small], name="adamw_small")
    for k, nm in enumerate(small):
        DL[nm], NM[nm], NV[nm] = ds[k], nms[k], nvs[k]
    finish("in", ds[0])

    outs = [loss, r["dx"][None]]
    for grp in (G, DL, NM, NV):
        outs += [grp[nm].T[None] if nm in transposed else
                 jnp.transpose(grp[nm], (1, 2, 0)) if nm == "w_in" else grp[nm].reshape(W[nm].shape) for nm in names]
    return tuple(outs)
```

```python
import functools
import numpy as np
import jax
import jax.numpy as jnp
from jax import lax
from jax.experimental import pallas as pl
from jax.experimental.pallas import tpu as pltpu

F32 = jnp.float32
BF = jnp.bfloat16
MESH = pl.DeviceIdType.MESH

D = 1024
T = 2048
TC = 256
TKV = T + TC
GRID_W = 64
NH = 8
DN = 64
DR = 32
DV = 64
QL = 384
KVL = 256
CONV = 512
DFF = 2816
EPS = 1e-6
ROPE_THETA = 10000.0
SCALE = (DN + DR) ** -0.5
NDEV = 8
HP = 128

O_GA, O_GC, O_KV, O_Q, O_CV = 0, 1024, 2048, 2560, 3072
NIN = 4608
CVB = 256
N_IN = 4256
SH_IN = N_IN // NDEV

LR, B1, B2, AEPS, WD, STEP = 0.001, 0.9, 0.999, 1e-08, 0.01, 10


def _pick(n, target, mult=128):
    best = None
    for d in range(mult, min(n, target) + 1, mult):
        if n % d == 0:
            best = d
    return best if best is not None else n


def _swap_start(g):
    return 8 * (g ^ 1)


def mm(a, b, *, ta=False, tb=False, out_dtype=F32, name, tm=1024, tn=1024, tk=2048, M=None, N=None, K=None,
       a_off=(0, 0), b_off=(0, 0), a_stack=False, b_stack=False, o_stack=False, dep=None):
    def dims(arr, stack):
        return (arr.shape[1], 2 * arr.shape[2]) if stack else arr.shape

    ar, ac = dims(a, a_stack)
    br, bc = dims(b, b_stack)
    M = M or ((ac if ta else ar) - a_off[1 if ta else 0])
    K = K or ((ar if ta else ac) - a_off[0 if ta else 1])
    N = N or ((br if tb else bc) - b_off[0 if tb else 1])
    tm = _pick(M, tm, 128 if ta else 16)
    tn = _pick(N // 2 if (o_stack or (b_stack and not tb)) else N, tn, 128)
    tk = _pick(K // 2 if ((a_stack and not ta) or (b_stack and tb)) else K, tk, 128)
    nk = K // tk
    ca = 0 if ta else 1
    cb = 1 if tb else 0

    def body(a_ref, b_ref, *rest):
        o_ref, acc = rest[-2:]
        k = pl.program_id(2)
        part = lax.dot_general(a_ref[...].astype(BF), b_ref[...].astype(BF),
                               (((ca,), (cb,)), ((), ())), preferred_element_type=F32)
        if nk == 1:
            o_ref[...] = part.astype(o_ref.dtype)
        else:
            @pl.when(k == 0)
            def _():
                acc[...] = part

            @pl.when(k > 0)
            def _():
                acc[...] += part

            @pl.when(k == nk - 1)
            def _():
                o_ref[...] = acc[...].astype(o_ref.dtype)

    def spec(blk, rc, off, stack, ncols):
        assert off[0] % blk[0] == 0 and off[1] % blk[1] == 0, (name, blk, off)
        ro, co = off[0] // blk[0], off[1] // blk[1]
        if not stack:
            return pl.BlockSpec(blk, lambda i, j, k: (rc(i, j, k)[0] + ro, rc(i, j, k)[1] + co))
        nhb = ncols // 2 // blk[1]
        return pl.BlockSpec((None,) + blk,
                            lambda i, j, k: ((rc(i, j, k)[1] + co) // nhb, rc(i, j, k)[0] + ro, (rc(i, j, k)[1] + co) % nhb))

    a_spec = spec((tk, tm), lambda i, j, k: (k, i), a_off, a_stack, ac) if ta else \
        spec((tm, tk), lambda i, j, k: (i, k), a_off, a_stack, ac)
    b_spec = spec((tn, tk), lambda i, j, k: (j, k), b_off, b_stack, bc) if tb else \
        spec((tk, tn), lambda i, j, k: (k, j), b_off, b_stack, bc)
    o_spec = spec((tm, tn), lambda i, j, k: (i, j), (0, 0), o_stack, N)
    o_shape = (2, M, N // 2) if o_stack else (M, N)
    deps = [] if dep is None else [dep]
    return pl.pallas_call(
        body, name=name, grid=(M // tm, N // tn, nk),
        in_specs=[a_spec, b_spec] + [pl.BlockSpec(memory_space=pl.ANY)] * len(deps),
        out_specs=o_spec, out_shape=jax.ShapeDtypeStruct(o_shape, out_dtype),
        scratch_shapes=[pltpu.VMEM((tm, tn) if nk > 1 else (8, 128), F32)],
        compiler_params=pltpu.CompilerParams(dimension_semantics=("parallel", "parallel", "arbitrary")),
    )(a, b, *deps)


def _row(width):
    return pl.BlockSpec((1, width), lambda *_: (0, 0))


NLAT = T // TC


def normmod_cat(ctx, x, g, csc, csh, sc, sh, dep, *, name, tm=256):
    assert tm == TC

    def body(c_ref, x_ref, g_ref, csc_ref, csh_ref, sc_ref, sh_ref, dep_ref, h_ref):
        last = pl.program_id(0) == NLAT
        xv = jnp.where(last, c_ref[...], x_ref[...])
        scv = jnp.where(last, csc_ref[...], sc_ref[...])
        shv = jnp.where(last, csh_ref[...], sh_ref[...])
        r = lax.rsqrt(jnp.mean(xv * xv, axis=-1, keepdims=True) + EPS)
        h_ref[...] = ((xv * r * g_ref[...]) * (1.0 + scv) + shv).astype(BF)

    return pl.pallas_call(
        body, name=name, grid=(TKV // tm,),
        in_specs=[pl.BlockSpec((tm, D), lambda i: (0, 0)), pl.BlockSpec((tm, D), lambda i: (jnp.minimum(i, NLAT - 1), 0)),
                  _row(D), _row(D), _row(D), _row(D), _row(D), pl.BlockSpec(memory_space=pl.ANY)],
        out_specs=pl.BlockSpec((tm, D), lambda i: (i, 0)), out_shape=jax.ShapeDtypeStruct((TKV, D), BF),
        compiler_params=pltpu.CompilerParams(dimension_semantics=("parallel",)),
    )(ctx, x, g, csc, csh, sc, sh, dep)


def kvprep(pc, p, kvg, wkv2, ck, sk, *, name, tm=256):
    assert tm == TC
    nb = TKV // tm
    kvcol = O_KV // 512

    def body(pc_ref, p_ref, g_ref, w_ref, ck_ref, sk_ref, k_ref, v_ref, ckv_ref):
        i = pl.program_id(0)
        t = jnp.where(i == NLAT, pc_ref[...], p_ref[...])
        pk = t[:, :KVL]
        r = lax.rsqrt(jnp.mean(pk * pk, axis=-1, keepdims=True) + EPS)
        ckv = (pk * r * g_ref[...]).astype(BF)
        ckv_ref[...] = ckv
        kv2 = jnp.dot(ckv, w_ref[...], preferred_element_type=F32)
        krr = t[:, KVL:KVL + HP] * ck_ref[...] + t[:, KVL + HP:KVL + 2 * HP] * sk_ref[...]
        k_ref[...] = (kv2[:, :NH * HP] + jnp.concatenate([krr] * NH, axis=1)).astype(BF)
        v_ref[...] = kv2[:, NH * HP:].astype(BF)

    return pl.pallas_call(
        body, name=name, grid=(nb,),
        in_specs=[pl.BlockSpec((tm, 512), lambda i: (0, 0)),
                  pl.BlockSpec((tm, 512), lambda i: (jnp.minimum(i, NLAT - 1), kvcol)),
                  _row(KVL), pl.BlockSpec((KVL, NH * HP + NH * DV), lambda i: (0, 0)),
                  pl.BlockSpec((tm, HP), lambda i: (i, 0)), pl.BlockSpec((tm, HP), lambda i: (i, 0))],
        out_specs=[pl.BlockSpec((tm, NH * HP), lambda i: (i, 0)), pl.BlockSpec((tm, NH * DV), lambda i: (i, 0)),
                   pl.BlockSpec((tm, KVL), lambda i: (i, 0))],
        out_shape=[jax.ShapeDtypeStruct((TKV, NH * HP), BF), jax.ShapeDtypeStruct((TKV, NH * DV), BF),
                   jax.ShapeDtypeStruct((TKV, KVL), BF)],
        compiler_params=pltpu.CompilerParams(dimension_semantics=("parallel",)),
    )(pc, p, kvg, wkv2, ck, sk)


def qprep(p, qg, wq2, cq_t, sq_t, *, name, tm=256):
    qcol = O_Q // 512

    def body(p_ref, g_ref, w_ref, c_ref, s_ref, q_ref, cq_ref):
        pq = p_ref[...]
        r = lax.rsqrt(jnp.sum(pq * pq, axis=-1, keepdims=True) * (1.0 / QL) + EPS)
        cq = (pq * r * g_ref[...]).astype(BF)
        cq_ref[...] = cq
        q2 = jnp.dot(cq, w_ref[...], preferred_element_type=F32)
        cc = jnp.concatenate([c_ref[...]] * NH, axis=1)
        ss = jnp.concatenate([s_ref[...]] * NH, axis=1)
        q_ref[...] = (q2[:, :NH * HP] * cc + q2[:, NH * HP:] * ss).astype(BF)

    return pl.pallas_call(
        body, name=name, grid=(T // tm,),
        in_specs=[pl.BlockSpec((tm, 512), lambda i: (i, qcol)), _row(512),
                  pl.BlockSpec((512, 2 * NH * HP), lambda i: (0, 0)),
                  pl.BlockSpec((tm, HP), lambda i: (i, 0)), pl.BlockSpec((tm, HP), lambda i: (i, 0))],
        out_specs=[pl.BlockSpec((tm, NH * HP), lambda i: (i, 0)), pl.BlockSpec((tm, 512), lambda i: (i, 0))],
        out_shape=[jax.ShapeDtypeStruct((T, NH * HP), BF), jax.ShapeDtypeStruct((T, 512), BF)],
        compiler_params=pltpu.CompilerParams(dimension_semantics=("parallel",)),
    )(p, qg, wq2, cq_t, sq_t)


def _head_mask(h):
    lanes = lax.broadcasted_iota(jnp.int32, (1, 2 * DV), 1)
    return (lanes // DV) == (h % 2)


LOG2E = 1.4426950408889634


def _scores_pass(q, k_ref, s_scr, kc):
    m = None
    for c in range(TKV // kc):
        s = lax.dot_general(q, k_ref[c * kc:(c + 1) * kc, :], (((1,), (1,)), ((), ())),
                            preferred_element_type=F32) * (SCALE * LOG2E)
        s_scr[:, c * kc:(c + 1) * kc] = s
        mc = jnp.max(s, axis=-1, keepdims=True)
        m = mc if m is None else jnp.maximum(m, mc)
    return m


def attn_fwd(q, k, v, *, name, tq=512, kc=1152):
    def body(q_ref, k_ref, v_ref, o_ref, s_scr):
        h = pl.program_id(1)
        m = _scores_pass(q_ref[...], k_ref, s_scr, kc)
        l = jnp.zeros((tq, 1), F32)
        acc = jnp.zeros((tq, 2 * DV), F32)
        for c in range(TKV // kc):
            e = jnp.exp2(s_scr[:, c * kc:(c + 1) * kc] - m)
            l = l + jnp.sum(e, axis=-1, keepdims=True)
            acc = acc + jnp.dot(e.astype(BF), v_ref[c * kc:(c + 1) * kc, :], preferred_element_type=F32)
        o2 = jnp.where(_head_mask(h), acc * (1.0 / l), 0.0).astype(BF)

        @pl.when(h % 2 == 0)
        def _():
            o_ref[...] = o2

        @pl.when(h % 2 == 1)
        def _():
            o_ref[...] = o_ref[...] + o2

    return pl.pallas_call(
        body, name=name, grid=(T // tq, NH),
        in_specs=[pl.BlockSpec((tq, HP), lambda i, h: (i, h)), pl.BlockSpec((TKV, HP), lambda i, h: (0, h)),
                  pl.BlockSpec((TKV, 2 * DV), lambda i, h: (0, h // 2))],
        out_specs=pl.BlockSpec((tq, 2 * DV), lambda i, h: (i, h // 2)),
        out_shape=jax.ShapeDtypeStruct((T, NH * DV), BF),
        scratch_shapes=[pltpu.VMEM((tq, TKV), F32)],
        compiler_params=pltpu.CompilerParams(dimension_semantics=("parallel", "arbitrary")),
    )(q, k, v)


def _shift_dn(x):
    n = x.shape[0]
    rows = lax.broadcasted_iota(jnp.int32, (n, 1), 0)
    return jnp.where(rows == 0, 0.0, pltpu.roll(x, 1, axis=0))


def _shift_up(x):
    n = x.shape[0]
    rows = lax.broadcasted_iota(jnp.int32, (n, 1), 0)
    return jnp.where(rows == n - 1, 0.0, pltpu.roll(x, n - 1, axis=0))


def _conv(x, w_ref, b_ref):
    return b_ref[...] + _shift_dn(x) * w_ref[0:1, :] + x * w_ref[1:2, :] + _shift_up(x) * w_ref[2:3, :]


def _conv_t(dy, w_ref):
    return _shift_up(dy) * w_ref[0:1, :] + dy * w_ref[1:2, :] + _shift_dn(dy) * w_ref[2:3, :]


def _conv_wgrad(dw_ref, dy, x):
    dw_ref[0:1, :] = jnp.sum(dy * _shift_dn(x), axis=0, keepdims=True)
    dw_ref[1:2, :] = jnp.sum(dy * x, axis=0, keepdims=True)
    dw_ref[2:3, :] = jnp.sum(dy * _shift_up(x), axis=0, keepdims=True)


def convz(p, cw, cb, *, name):
    o0 = O_CV // (3 * CVB)

    def body(p_ref, w_ref, bias_ref, z_ref):
        xv, bv, cv = p_ref[:, 0:CVB], p_ref[:, CVB:2 * CVB], p_ref[:, 2 * CVB:3 * CVB]
        z_ref[...] = (bv * _conv(cv * xv, w_ref, bias_ref)).astype(BF)

    return pl.pallas_call(
        body, name=name, grid=(CONV // CVB,),
        in_specs=[pl.BlockSpec((T, 3 * CVB), lambda j: (0, o0 + j)), pl.BlockSpec((3, CVB), lambda j: (0, j)),
                  pl.BlockSpec((1, CVB), lambda j: (0, j))],
        out_specs=pl.BlockSpec((T, CVB), lambda j: (0, j)),
        out_shape=jax.ShapeDtypeStruct((T, CONV), BF),
        compiler_params=pltpu.CompilerParams(dimension_semantics=("parallel",)),
    )(p, cw, cb)


def out_proj_merge(o, wao, z, wco, p, *, name, tm=512):
    kin = o.shape[1]

    def body(o_ref, wa_ref, z_ref, wc_ref, ga_ref, gc_ref, ya_ref, yc_ref, m_ref):
        ya = jnp.dot(o_ref[...], wa_ref[...], preferred_element_type=F32)
        yc = jnp.dot(z_ref[...], wc_ref[...], preferred_element_type=F32)
        ya_ref[...] = ya
        yc_ref[...] = yc
        m_ref[...] = (jax.nn.sigmoid(ga_ref[...]) * ya + jax.nn.sigmoid(gc_ref[...]) * yc).astype(BF)

    blk = pl.BlockSpec((tm, D), lambda i: (i, 0))
    act = pl.BlockSpec((tm, kin), lambda i: (i, 0))
    wsp = pl.BlockSpec((kin, D), lambda i: (0, 0))
    sh = jax.ShapeDtypeStruct((T, D), F32)
    return pl.pallas_call(
        body, name=name, grid=(T // tm,),
        in_specs=[act, wsp, act, wsp, pl.BlockSpec((tm, D), lambda i: (i, O_GA // D)),
                  pl.BlockSpec((tm, D), lambda i: (i, O_GC // D))],
        out_specs=[blk, blk, blk], out_shape=[sh, sh, jax.ShapeDtypeStruct((T, D), BF)],
        compiler_params=pltpu.CompilerParams(dimension_semantics=("parallel",)),
    )(o, wao, z, wco, p, p)


CONV_HALO = 8
CONV_ROWS = 256


def _row_chunks(n, chunk, carry):
    carry = chunk(0, True, False, carry)
    carry = lax.fori_loop(1, n // CONV_ROWS - 1, lambda c, a: chunk(c * CONV_ROWS, False, False, a), carry)
    return chunk(n - CONV_ROWS, False, True, carry)


def _ext_rows(ref, r0, first, last):
    n, w = ref.shape
    zero = jnp.zeros((CONV_HALO, w), ref.dtype)
    if first:
        return jnp.concatenate([zero, ref[0:CONV_ROWS + CONV_HALO, :]], axis=0)
    if last:
        return jnp.concatenate([ref[n - CONV_ROWS - CONV_HALO:n, :], zero], axis=0)
    return ref[pl.ds(pl.multiple_of(r0 - CONV_HALO, 8), CONV_ROWS + 2 * CONV_HALO), :]


def _center_rows(r0, first, last):
    return slice(r0, r0 + CONV_ROWS) if (first or last) else pl.ds(pl.multiple_of(r0, 8), CONV_ROWS)


def _roll_dn(x):
    return pltpu.roll(x, 1, axis=0)


def _roll_up(x):
    return pltpu.roll(x, x.shape[0] - 1, axis=0)


_CTR = slice(CONV_HALO, CONV_HALO + CONV_ROWS)


def ffn_act(u0, cw, cb, *, name, tc=256):
    nb = DFF // tc

    def body(u_ref, wg_ref, wv_ref, bg_ref, bv_ref, f_ref):
        wg = [wg_ref[k:k + 1, :] for k in range(3)]
        wv = [wv_ref[k:k + 1, :] for k in range(3)]
        bg, bv = bg_ref[...], bv_ref[...]

        def chunk(r0, first, last, carry):
            xg, xv = _ext_rows(u_ref.at[0], r0, first, last), _ext_rows(u_ref.at[1], r0, first, last)
            ug = bg + _roll_dn(xg) * wg[0] + xg * wg[1] + _roll_up(xg) * wg[2]
            uv = bv + _roll_dn(xv) * wv[0] + xv * wv[1] + _roll_up(xv) * wv[2]
            f_ref[_center_rows(r0, first, last), :] = (ug * jax.nn.sigmoid(ug) * uv)[_CTR].astype(BF)
            return carry

        _row_chunks(T, chunk, 0)

    return pl.pallas_call(
        body, name=name, grid=(nb,),
        in_specs=[pl.BlockSpec((2, T, tc), lambda j: (0, 0, j)),
                  pl.BlockSpec((3, tc), lambda j: (0, j)), pl.BlockSpec((3, tc), lambda j: (0, nb + j)),
                  pl.BlockSpec((1, tc), lambda j: (0, j)), pl.BlockSpec((1, tc), lambda j: (0, nb + j))],
        out_specs=pl.BlockSpec((T, tc), lambda j: (0, j)),
        out_shape=jax.ShapeDtypeStruct((T, DFF), BF),
        compiler_params=pltpu.CompilerParams(dimension_semantics=("parallel",)),
    )(u0, cw, cw, cb, cb)


def rows_call(lead, ins, in_specs, out_shape, out_specs, fn, *, name, R, tm):
    tb, a_stack, tk = lead.get("tb", False), lead.get("a_stack", False), lead["tk"]
    K = 2 * lead["a"].shape[2] if a_stack else lead["a"].shape[1]
    nk = K // tk
    deps = [] if lead.get("dep") is None else [lead["dep"]]
    n_in = len(ins)

    def body(a_ref, b_ref, *refs):
        refs = refs[len(deps):]
        in_refs, out_refs, acc = refs[:n_in], refs[n_in:-1], refs[-1]
        i, k = pl.program_id(0), pl.program_id(1)
        part = lax.dot_general(a_ref[...].astype(BF), b_ref[...].astype(BF),
                               (((1,), (1 if tb else 0,)), ((), ())), preferred_element_type=F32)
        if nk == 1:
            fn(i, part, in_refs, out_refs)
            return

        @pl.when(k == 0)
        def _():
            acc[...] = part

        @pl.when(k > 0)
        def _():
            acc[...] += part

        @pl.when(k == nk - 1)
        def _():
            fn(i, acc[...], in_refs, out_refs)

    if a_stack:
        nhb = K // 2 // tk
        a_spec = pl.BlockSpec((None, tm, tk), lambda i, k: (k // nhb, i, k % nhb))
    else:
        a_spec = pl.BlockSpec((tm, tk), lambda i, k: (i, k))
    b_spec = pl.BlockSpec((D, tk), lambda i, k: (0, k)) if tb else pl.BlockSpec((tk, D), lambda i, k: (k, 0))
    return pl.pallas_call(
        body, name=name, grid=(R // tm, nk),
        in_specs=[a_spec, b_spec] + [pl.BlockSpec(memory_space=pl.ANY)] * len(deps) + list(in_specs),
        out_specs=out_specs, out_shape=out_shape,
        scratch_shapes=[pltpu.VMEM((tm, D) if nk > 1 else (8, 128), F32)],
        compiler_params=pltpu.CompilerParams(dimension_semantics=("arbitrary", "arbitrary")),
    )(lead["a"], lead["b"], *deps, *ins)


def _rblk(tm, w=D, col=0):
    return pl.BlockSpec((tm, w), lambda i, k: (i, col))


def _rrow(w=D):
    return pl.BlockSpec((1, w), lambda i, k: (0, 0))


def down_final(f, wdn, x1, g2, fg, tgt, *, name, tm=512):
    def fn(i, d, in_refs, out_refs):
        x1_ref, g2_ref, fg_ref, t_ref = in_refs
        d_ref, dx_ref, dd_ref, dfg_ref, loss_ref = out_refs
        d_ref[...] = d
        xv = x1_ref[...] + g2_ref[...] * d
        r = lax.rsqrt(jnp.mean(xv * xv, axis=-1, keepdims=True) + EPS)
        xh = xv * r
        diff = xh * fg_ref[...] - t_ref[...]
        part = 0.5 * jnp.sum(jnp.mean(diff * diff, axis=-1, keepdims=True), axis=0, keepdims=True)
        dy = diff * (1.0 / D)
        a = dy * fg_ref[...]
        dx = r * (a - xh * jnp.mean(a * xh, axis=-1, keepdims=True))
        dx_ref[...] = dx
        dd_ref[...] = (dx * g2_ref[...]).astype(BF)
        dfg = jnp.sum(dy * xh, axis=0, keepdims=True)

        @pl.when(i == 0)
        def _():
            dfg_ref[...] = dfg
            loss_ref[...] = jnp.broadcast_to(part, (1, 128))

        @pl.when(i > 0)
        def _():
            dfg_ref[...] += dfg
            loss_ref[...] += jnp.broadcast_to(part, (1, 128))

    blk = _rblk(tm)
    return rows_call(
        dict(a=f, b=wdn, tk=DFF), [x1, g2, fg, tgt], [blk, _rrow(), _rrow(), blk],
        [jax.ShapeDtypeStruct((T, D), F32), jax.ShapeDtypeStruct((T, D), F32), jax.ShapeDtypeStruct((T, D), BF),
         jax.ShapeDtypeStruct((1, D), F32), jax.ShapeDtypeStruct((1, 128), F32)],
        [blk, blk, blk, _rrow(), _rrow(128)], fn, name=name, R=T, tm=tm)


def oproj_resid(merged, wo, x, gate, g, sc, sh, *, name, tm=512):
    def fn(i, a, in_refs, out_refs):
        x_ref, gate_ref, g_ref, sc_ref, sh_ref = in_refs
        a_ref, x1_ref, h_ref = out_refs
        a_ref[...] = a
        xv = x_ref[...] + gate_ref[...] * a
        x1_ref[...] = xv
        r = lax.rsqrt(jnp.mean(xv * xv, axis=-1, keepdims=True) + EPS)
        h_ref[...] = ((xv * r * g_ref[...]) * (1.0 + sc_ref[...]) + sh_ref[...]).astype(BF)

    blk = _rblk(tm)
    return rows_call(
        dict(a=merged, b=wo, tk=D), [x, gate, g, sc, sh], [blk, _rrow(), _rrow(), _rrow(), _rrow()],
        [jax.ShapeDtypeStruct((T, D), F32), jax.ShapeDtypeStruct((T, D), F32), jax.ShapeDtypeStruct((T, D), BF)],
        [blk, blk, blk], fn, name=name, R=T, tm=tm)


def oproj_dx_gate_bwd(da, wo, p, ya, yc, wao, wco, *, name, tm=512):
    kin = wao.shape[0]

    def fn(i, dm, in_refs, out_refs):
        ga_ref, gc_ref, ya_ref, yc_ref, wa_ref, wc_ref = in_refs
        dya_ref, dyc_ref, dp_ref, do_ref, dz_ref = out_refs
        sa, sc_ = jax.nn.sigmoid(ga_ref[...]), jax.nn.sigmoid(gc_ref[...])
        dya, dyc = (dm * sa).astype(BF), (dm * sc_).astype(BF)
        dya_ref[...] = dya
        dyc_ref[...] = dyc
        dp_ref[:, 0:D] = (dm * ya_ref[...] * (sa * (1.0 - sa))).astype(BF)
        dp_ref[:, D:2 * D] = (dm * yc_ref[...] * (sc_ * (1.0 - sc_))).astype(BF)
        nt = (((1,), (1,)), ((), ()))
        do_ref[...] = lax.dot_general(dya, wa_ref[...], nt, preferred_element_type=F32).astype(BF)
        dz_ref[...] = lax.dot_general(dyc, wc_ref[...], nt, preferred_element_type=F32)

    blk = _rblk(tm)
    sh = jax.ShapeDtypeStruct((T, D), BF)
    wsp = pl.BlockSpec((kin, D), lambda i, k: (0, 0))
    return rows_call(
        dict(a=da, b=wo, tb=True, tk=D), [p, p, ya, yc, wao, wco],
        [_rblk(tm, D, O_GA // D), _rblk(tm, D, O_GC // D), blk, blk, wsp, wsp],
        [sh, sh, jax.ShapeDtypeStruct((T, NIN), BF), jax.ShapeDtypeStruct((T, kin), BF), jax.ShapeDtypeStruct((T, kin), F32)],
        [blk, blk, _rblk(tm, 2 * D), _rblk(tm, kin), _rblk(tm, kin)], fn, name=name, R=T, tm=tm)


def normmod_bwd(x, dh, g, sc, dres, gsrc, gate, *, name, tm=512):
    R = x.shape[0]
    tm = min(tm, R)
    has_res = dres is not None
    fused = isinstance(dh, dict)
    if fused:
        tb, a_stack, tk = dh.get("tb", False), dh.get("a_stack", False), dh["tk"]
        K = 2 * dh["a"].shape[2] if a_stack else dh["a"].shape[1]
        nk = K // tk
        deps = [] if dh.get("dep") is None else [dh["dep"]]
        n_dh = 2 + len(deps)
    else:
        nk, n_dh = 1, 1

    def elementwise(i, dhv, x_ref, g_ref, sc_ref, res_refs, out_refs):
        xv = x_ref[...]
        r = lax.rsqrt(jnp.mean(xv * xv, axis=-1, keepdims=True) + EPS)
        xh = xv * r
        n = xh * g_ref[...]
        dn = dhv * (1.0 + sc_ref[...])
        a = dn * g_ref[...]
        rows = [jnp.sum(dhv, axis=0, keepdims=True), jnp.sum(dhv * n, axis=0, keepdims=True),
                jnp.sum(dn * xh, axis=0, keepdims=True)]
        if has_res:
            dres_ref, gsrc_ref, gate_ref = res_refs
            dx_ref, dxg_ref, st_ref = out_refs
            dr = dres_ref[...]
            dx = dr + r * (a - xh * jnp.mean(a * xh, axis=-1, keepdims=True))
            dx_ref[...] = dx
            dxg_ref[...] = (dx * gate_ref[...]).astype(BF)
            rows.append(jnp.sum(dr * gsrc_ref[...], axis=0, keepdims=True))
        else:
            st_ref, = out_refs
            rows.append(jnp.zeros((1, D), F32))

        @pl.when(i == 0)
        def _():
            for k, row in enumerate(rows):
                st_ref[k:k + 1, :] = row

        @pl.when(i > 0)
        def _():
            for k, row in enumerate(rows):
                st_ref[k:k + 1, :] += row

    def body(*refs):
        x_ref, dh_refs, g_ref, sc_ref = refs[0], refs[1:1 + n_dh], refs[1 + n_dh], refs[2 + n_dh]
        rest = refs[3 + n_dh:]
        res_refs, rest = (rest[:3], rest[3:]) if has_res else ((), rest)
        out_refs = rest[:3] if has_res else rest[:1]
        i = pl.program_id(0)
        if not fused:
            elementwise(i, dh_refs[0][...], x_ref, g_ref, sc_ref, res_refs, out_refs)
            return
        acc = rest[-1]
        k = pl.program_id(1)
        part = lax.dot_general(dh_refs[0][...].astype(BF), dh_refs[1][...].astype(BF),
                               (((1,), (1 if tb else 0,)), ((), ())), preferred_element_type=F32)

        @pl.when(k == 0)
        def _():
            acc[...] = part

        @pl.when(k > 0)
        def _():
            acc[...] += part

        @pl.when(k == nk - 1)
        def _():
            elementwise(i, acc[...], x_ref, g_ref, sc_ref, res_refs, out_refs)

    rowb = lambda w: pl.BlockSpec((1, w), lambda i, *k: (0, 0))
    blk = pl.BlockSpec((tm, D), lambda i, *k: (i, 0))
    st_spec = pl.BlockSpec((4, D), lambda i, *k: (0, 0))
    st_shape = jax.ShapeDtypeStruct((4, D), F32)
    if fused:
        if a_stack:
            nhb = K // 2 // tk
            a_spec = pl.BlockSpec((None, tm, tk), lambda i, k: (k // nhb, i, k % nhb))
        else:
            a_spec = pl.BlockSpec((tm, tk), lambda i, k: (i, k))
        b_spec = pl.BlockSpec((D, tk), lambda i, k: (0, k)) if tb else pl.BlockSpec((tk, D), lambda i, k: (k, 0))
        dh_specs = [a_spec, b_spec] + [pl.BlockSpec(memory_space=pl.ANY)] * len(deps)
        dh_args = [dh["a"], dh["b"]] + deps
        grid, sem = (R // tm, nk), ("arbitrary", "arbitrary")
        scratch = [pltpu.VMEM((tm, D), F32)]
    else:
        dh_specs, dh_args, grid, sem, scratch = [blk], [dh], (R // tm,), ("arbitrary",), []
    cp = pltpu.CompilerParams(dimension_semantics=sem)
    if has_res:
        return pl.pallas_call(
            body, name=name, grid=grid, in_specs=[blk] + dh_specs + [rowb(D), rowb(D), blk, blk, rowb(D)],
            out_specs=[blk, blk, st_spec], scratch_shapes=scratch,
            out_shape=[jax.ShapeDtypeStruct((R, D), F32), jax.ShapeDtypeStruct((R, D), BF), st_shape],
            compiler_params=cp,
        )(x, *dh_args, g, sc, dres, gsrc, gate)
    return pl.pallas_call(
        body, name=name, grid=grid, in_specs=[blk] + dh_specs + [rowb(D), rowb(D)],
        out_specs=st_spec, out_shape=st_shape, scratch_shapes=scratch, compiler_params=cp,
    )(x, *dh_args, g, sc)


def ffn_act_bwd(u0, df, cw, cb, *, name, tc=128):
    nb = DFF // tc

    def body(u_ref, df_ref, wg_ref, wv_ref, bg_ref, bv_ref, du_ref, dw_ref, db_ref):
        wg = [wg_ref[k:k + 1, :] for k in range(3)]
        wv = [wv_ref[k:k + 1, :] for k in range(3)]
        bg, bv = bg_ref[...], bv_ref[...]

        def chunk(r0, first, last, acc):
            xg, xv = _ext_rows(u_ref.at[0], r0, first, last), _ext_rows(u_ref.at[1], r0, first, last)
            dfe = _ext_rows(df_ref, r0, first, last)
            xg_d, xg_u, xv_d, xv_u = _roll_dn(xg), _roll_up(xg), _roll_dn(xv), _roll_up(xv)
            ug = bg + xg_d * wg[0] + xg * wg[1] + xg_u * wg[2]
            uv = bv + xv_d * wv[0] + xv * wv[1] + xv_u * wv[2]
            sig = jax.nn.sigmoid(ug)
            dug = dfe * uv * (sig * (1.0 + ug * (1.0 - sig)))
            duv = dfe * (ug * sig)
            rows = _center_rows(r0, first, last)
            du_ref[0, rows, :] = (_roll_up(dug) * wg[0] + dug * wg[1] + _roll_dn(dug) * wg[2])[_CTR].astype(BF)
            du_ref[1, rows, :] = (_roll_up(duv) * wv[0] + duv * wv[1] + _roll_dn(duv) * wv[2])[_CTR].astype(BF)
            terms = [dug * xg_d, dug * xg, dug * xg_u, dug, duv * xv_d, duv * xv, duv * xv_u, duv]
            return tuple(a + jnp.sum(t[_CTR], axis=0, keepdims=True) for a, t in zip(acc, terms))

        acc = _row_chunks(T, chunk, tuple(jnp.zeros((1, tc), F32) for _ in range(8)))
        for k in range(3):
            dw_ref[0, k:k + 1, :] = acc[k]
            dw_ref[1, k:k + 1, :] = acc[4 + k]
        db_ref[0] = acc[3]
        db_ref[1] = acc[7]

    lo = lambda r: pl.BlockSpec((r, tc), lambda j: (0, j))
    hi = lambda r: pl.BlockSpec((r, tc), lambda j: (0, nb + j))
    st = lambda r: pl.BlockSpec((2, r, tc), lambda j: (0, 0, j))
    return pl.pallas_call(
        body, name=name, grid=(nb,),
        in_specs=[st(T), lo(T), lo(3), hi(3), lo(1), hi(1)],
        out_specs=[st(T), st(3), st(1)],
        out_shape=[jax.ShapeDtypeStruct((2, T, DFF), BF), jax.ShapeDtypeStruct((2, 3, DFF), F32),
                   jax.ShapeDtypeStruct((2, 1, DFF), F32)],
        compiler_params=pltpu.CompilerParams(dimension_semantics=("parallel",)),
    )(u0, df, cw, cw, cb, cb)


def convz_bwd(p, dz, cw, cb, dp, *, name):
    o0 = O_CV // (3 * CVB)

    def body(p_ref, dz_ref, w_ref, bias_ref, dp_in, dp_ref, dw_ref, dbias_ref):
        xv, bv, cv = p_ref[:, 0:CVB], p_ref[:, CVB:2 * CVB], p_ref[:, 2 * CVB:3 * CVB]
        ci = cv * xv
        dwc = _conv(ci, w_ref, bias_ref)
        dzv = dz_ref[...]
        ddw = dzv * bv
        dci = _conv_t(ddw, w_ref)
        dp_ref[:, 0:CVB] = (dci * cv).astype(BF)
        dp_ref[:, CVB:2 * CVB] = (dzv * dwc).astype(BF)
        dp_ref[:, 2 * CVB:3 * CVB] = (dci * xv).astype(BF)
        _conv_wgrad(dw_ref, ddw, ci)
        dbias_ref[...] = jnp.sum(ddw, axis=0, keepdims=True)

    own = lambda r: pl.BlockSpec((r, CVB), lambda j: (0, j))
    return pl.pallas_call(
        body, name=name, grid=(CONV // CVB,),
        in_specs=[pl.BlockSpec((T, 3 * CVB), lambda j: (0, o0 + j)), own(T), own(3), own(1),
                  pl.BlockSpec(memory_space=pl.ANY)],
        out_specs=[pl.BlockSpec((T, 3 * CVB), lambda j: (0, o0 + j)), own(3), own(1)],
        out_shape=[jax.ShapeDtypeStruct((T, NIN), BF), jax.ShapeDtypeStruct((3, CONV), F32),
                   jax.ShapeDtypeStruct((1, CONV), F32)],
        input_output_aliases={4: 0},
        compiler_params=pltpu.CompilerParams(dimension_semantics=("parallel",)),
    )(p, dz, cw, cb, dp)


def attn_bwd(q, k, v, do, dep, *, name, tq=1024, kc=768):
    NKC, KC = TKV // kc, kc
    deps = [] if dep is None else [dep]

    def body(q_ref, k_ref, v_ref, do_ref, *rest):
        dq_ref, dk_ref, dv_ref, s_scr, dp_scr = rest[len(deps):]
        h, i = pl.program_id(0), pl.program_id(1)

        @pl.when(i == 0)
        def _():
            dk_ref[...] = jnp.zeros_like(dk_ref)

        @pl.when((i == 0) & (h % 2 == 0))
        def _():
            dv_ref[...] = jnp.zeros_like(dv_ref)

        qv = q_ref[...]
        dom = jnp.where(_head_mask(h), do_ref[...], jnp.zeros_like(do_ref[...]))
        m = _scores_pass(qv, k_ref, s_scr, kc)
        l = jnp.zeros((tq, 1), F32)
        dsum = jnp.zeros((tq, 1), F32)
        for c in range(NKC):
            cols = slice(c * KC, (c + 1) * KC)
            e = jnp.exp2(s_scr[:, cols] - m)
            s_scr[:, cols] = e
            dp = lax.dot_general(dom, v_ref[cols, :], (((1,), (1,)), ((), ())), preferred_element_type=F32)
            dp_scr[:, cols] = dp
            l = l + jnp.sum(e, axis=-1, keepdims=True)
            dsum = dsum + jnp.sum(e * dp, axis=-1, keepdims=True)
        inv = 1.0 / l
        delta = dsum * inv
        dos = (dom.astype(F32) * inv).astype(BF)
        dq = jnp.zeros((tq, HP), F32)
        for c in range(NKC):
            cols = slice(c * KC, (c + 1) * KC)
            e = s_scr[:, cols]
            ds = (e * (dp_scr[:, cols] - delta) * (inv * SCALE)).astype(BF)
            dq = dq + jnp.dot(ds, k_ref[cols, :], preferred_element_type=F32)
            dk_ref[cols, :] += lax.dot_general(ds, qv, (((0,), (0,)), ((), ())), preferred_element_type=F32)
            dv_ref[cols, :] += lax.dot_general(e.astype(BF), dos, (((0,), (0,)), ((), ())), preferred_element_type=F32)
        dq_ref[...] = dq

    return pl.pallas_call(
        body, name=name, grid=(NH, T // tq),
        in_specs=[pl.BlockSpec((tq, HP), lambda h, i: (i, h)), pl.BlockSpec((TKV, HP), lambda h, i: (0, h)),
                  pl.BlockSpec((TKV, 2 * DV), lambda h, i: (0, h // 2)), pl.BlockSpec((tq, 2 * DV), lambda h, i: (i, h // 2)),
                  *([pl.BlockSpec(memory_space=pl.ANY)] * len(deps))],
        out_specs=[pl.BlockSpec((tq, HP), lambda h, i: (i, h)), pl.BlockSpec((TKV, HP), lambda h, i: (0, h)),
                   pl.BlockSpec((TKV, 2 * DV), lambda h, i: (0, h // 2))],
        out_shape=[jax.ShapeDtypeStruct((T, NH * HP), F32), jax.ShapeDtypeStruct((TKV, NH * HP), F32),
                   jax.ShapeDtypeStruct((TKV, NH * DV), F32)],
        scratch_shapes=[pltpu.VMEM((tq, TKV), F32), pltpu.VMEM((tq, TKV), F32)],
        compiler_params=pltpu.CompilerParams(dimension_semantics=("arbitrary", "arbitrary")),
    )(q, k, v, do, *deps)


def qprep_bwd(p, dq, qg, wq2, cq_t, sq_t, dp, *, name, tm=256):
    qcol = O_Q // 512

    def body(p_ref, dq_ref, g_ref, w_ref, c_ref, s_ref, dp_in, dp_ref, dq2_ref, dg_ref):
        i = pl.program_id(0)
        dqv = dq_ref[...]
        cc = jnp.concatenate([c_ref[...]] * NH, axis=1)
        ss = jnp.concatenate([s_ref[...]] * NH, axis=1)
        dq2 = jnp.concatenate([dqv * cc, dqv * ss], axis=1).astype(BF)
        dq2_ref[...] = dq2
        dcq = lax.dot_general(dq2, w_ref[...], (((1,), (1,)), ((), ())), preferred_element_type=F32)
        pq = p_ref[...]
        r = lax.rsqrt(jnp.sum(pq * pq, axis=-1, keepdims=True) * (1.0 / QL) + EPS)
        xh = pq * r
        a = dcq * g_ref[...]
        dp_ref[...] = (r * (a - xh * (jnp.sum(a * xh, axis=-1, keepdims=True) * (1.0 / QL)))).astype(BF)
        dg = jnp.sum(dcq * xh, axis=0, keepdims=True)

        @pl.when(i == 0)
        def _():
            dg_ref[...] = dg

        @pl.when(i > 0)
        def _():
            dg_ref[...] += dg

    return pl.pallas_call(
        body, name=name, grid=(T // tm,),
        in_specs=[pl.BlockSpec((tm, 512), lambda i: (i, qcol)), pl.BlockSpec((tm, NH * HP), lambda i: (i, 0)), _row(512),
                  pl.BlockSpec((512, 2 * NH * HP), lambda i: (0, 0)),
                  pl.BlockSpec((tm, HP), lambda i: (i, 0)), pl.BlockSpec((tm, HP), lambda i: (i, 0)),
                  pl.BlockSpec(memory_space=pl.ANY)],
        out_specs=[pl.BlockSpec((tm, 512), lambda i: (i, qcol)), pl.BlockSpec((tm, 2 * NH * HP), lambda i: (i, 0)), _row(512)],
        out_shape=[jax.ShapeDtypeStruct((T, NIN), BF), jax.ShapeDtypeStruct((T, 2 * NH * HP), BF),
                   jax.ShapeDtypeStruct((1, 512), F32)],
        input_output_aliases={6: 0},
        compiler_params=pltpu.CompilerParams(dimension_semantics=("arbitrary",)),
    )(p, dq, qg, wq2, cq_t, sq_t, dp)


def kvprep_bwd(pc, p, dk, dv, kvg, wkv2, ck, sk, dp, *, name, tm=256):
    assert tm == TC
    nb = TKV // tm
    kvcol = O_KV // 512

    def body(pc_ref, p_ref, dk_ref, dv_ref, g_ref, w_ref, ck_ref, sk_ref, dp_in, dp_ref, dpc_ref, dkv2_ref, dg_ref):
        i = pl.program_id(0)
        t = jnp.where(i == NLAT, pc_ref[...], p_ref[...])
        pk = t[:, :KVL]
        r = lax.rsqrt(jnp.mean(pk * pk, axis=-1, keepdims=True) + EPS)
        xh = pk * r
        dkv = dk_ref[...]
        dkv2 = jnp.concatenate([dkv, dv_ref[...]], axis=1).astype(BF)
        dkv2_ref[...] = dkv2
        dckv = lax.dot_general(dkv2, w_ref[...], (((1,), (1,)), ((), ())), preferred_element_type=F32)
        a = dckv * g_ref[...]
        dpk = r * (a - xh * jnp.mean(a * xh, axis=-1, keepdims=True))
        dkr = dkv[:, 0:HP]
        for hh in range(1, NH):
            dkr = dkr + dkv[:, hh * HP:(hh + 1) * HP]
        res = jnp.concatenate([dpk, dkr * ck_ref[...], dkr * sk_ref[...]], axis=1).astype(BF)
        dg = jnp.sum(dckv * xh, axis=0, keepdims=True)

        @pl.when(i == 0)
        def _():
            dg_ref[...] = dg

        @pl.when(i > 0)
        def _():
            dg_ref[...] += dg

        @pl.when(i < NLAT)
        def _():
            dp_ref[...] = res

        @pl.when(i == NLAT)
        def _():
            dpc_ref[...] = res

    rb = lambda w: pl.BlockSpec((tm, w), lambda i: (i, 0))
    return pl.pallas_call(
        body, name=name, grid=(nb,),
        in_specs=[pl.BlockSpec((tm, 512), lambda i: (0, 0)),
                  pl.BlockSpec((tm, 512), lambda i: (jnp.minimum(i, NLAT - 1), kvcol)),
                  rb(NH * HP), rb(NH * DV), _row(KVL), pl.BlockSpec((KVL, NH * HP + NH * DV), lambda i: (0, 0)),
                  rb(HP), rb(HP), pl.BlockSpec(memory_space=pl.ANY)],
        out_specs=[pl.BlockSpec((tm, 512), lambda i: (jnp.minimum(i, NLAT - 1), kvcol)),
                   pl.BlockSpec((tm, 512), lambda i: (0, 0)), rb(NH * HP + NH * DV), _row(KVL)],
        out_shape=[jax.ShapeDtypeStruct((T, NIN), BF), jax.ShapeDtypeStruct((TC, 512), BF),
                   jax.ShapeDtypeStruct((TKV, NH * HP + NH * DV), BF), jax.ShapeDtypeStruct((1, KVL), F32)],
        input_output_aliases={8: 0},
        compiler_params=pltpu.CompilerParams(dimension_semantics=("arbitrary",)),
    )(pc, p, dk, dv, kvg, wkv2, ck, sk, dp)


def _pieces(src, width, n):
    out, c = [], src
    while c < src + width:
        k = c // n
        w = min(src + width, (k + 1) * n) - c
        out.append((k, c - k * n, c - src, w))
        c += w
    return out


def _win_moves():
    mv = [(2208, 1024, O_GA), (3232, 1024, O_GC), (0, KVL, O_KV), (256, DR, O_KV + KVL + DN), (288, QL, O_Q)]
    mv += [(256 + _swap_start(g), 8, O_KV + KVL + HP + DN + 8 * g) for g in range(4)]
    for j in range(CONV // CVB):
        base = O_CV + 3 * CVB * j
        mv += [(672 + CVB * j, CVB, base), (1184 + CVB * j, CVB, base + CVB), (1696 + CVB * j, CVB, base + 2 * CVB)]
    return mv


_WIN_ZERO = [(O_KV + KVL, DN), (O_KV + KVL + DN + DR, HP - DN - DR), (O_KV + KVL + HP, DN),
             (O_KV + KVL + HP + DN + DR, HP - DN - DR), (O_Q + QL, 512 - QL)]


def build_win(g, *, name, tm=256):
    def body(g_ref, o_ref):
        for src, w, dst in _win_moves():
            for k, a, off, pw in _pieces(src, w, SH_IN):
                o_ref[:, dst + off:dst + off + pw] = g_ref[k, :, a:a + pw]
        for c0, w in _WIN_ZERO:
            o_ref[:, c0:c0 + w] = jnp.zeros((tm, w), o_ref.dtype)

    return pl.pallas_call(
        body, name=name, grid=(D // tm,), in_specs=[pl.BlockSpec((NDEV, tm, SH_IN), lambda i: (0, i, 0))],
        out_specs=pl.BlockSpec((tm, NIN), lambda i: (i, 0)), out_shape=jax.ShapeDtypeStruct((D, NIN), g.dtype),
        compiler_params=pltpu.CompilerParams(dimension_semantics=("parallel",)),
    )(g)


def shard_win_grad(dwt, dwct, *, name, tc=256):
    def body(dw_ref, dwc_ref, o_ref, kvs):
        kvs[...] = dw_ref[O_KV:O_KV + 512, :] + dwc_ref[...]

        def src(row, w):
            if O_KV <= row < O_KV + 512:
                return kvs[row - O_KV:row - O_KV + w, :]
            return dw_ref[row:row + w, :]

        for s, w, dst in _win_moves():
            if w == 8 or s == 256:
                continue
            for k, a, off, pw in _pieces(s, w, SH_IN):
                o_ref[k, a:a + pw, :] = src(dst + off, pw).astype(o_ref.dtype)
        for g in range(4):
            val = src(O_KV + KVL + DN + 8 * g, 8) + src(O_KV + KVL + HP + DN + _swap_start(g), 8)
            o_ref[0, 256 + 8 * g:256 + 8 * g + 8, :] = val.astype(o_ref.dtype)

    return pl.pallas_call(
        body, name=name, grid=(D // tc,),
        in_specs=[pl.BlockSpec((NIN, tc), lambda j: (0, j)), pl.BlockSpec((512, tc), lambda j: (0, j))],
        out_specs=pl.BlockSpec((NDEV, SH_IN, tc), lambda j: (0, 0, j)),
        out_shape=jax.ShapeDtypeStruct((NDEV, SH_IN, D), BF),
        scratch_shapes=[pltpu.VMEM((512, tc), F32)],
        compiler_params=pltpu.CompilerParams(dimension_semantics=("parallel",)),
    )(dwt, dwct)


def build_wq_wkv(gq, gkv, *, name):
    def body(gq_ref, gkv_ref, q_ref, kv_ref):
        q_ref[...] = jnp.zeros_like(q_ref)
        kv_ref[...] = jnp.zeros_like(kv_ref)
        for h in range(NH):
            q_ref[0:QL, h * HP:h * HP + DN + DR] = gq_ref[h]
            for g in range(4):
                c0 = NH * HP + h * HP + DN + 8 * g
                q_ref[0:QL, c0:c0 + 8] = gq_ref[h, :, DN + _swap_start(g):DN + _swap_start(g) + 8]
            kv_ref[:, h * HP:h * HP + DN] = gkv_ref[h, :, 0:DN]
            kv_ref[:, NH * HP + h * DV:NH * HP + (h + 1) * DV] = gkv_ref[h, :, DN:DN + DV]

    vm = pl.BlockSpec(memory_space=pltpu.VMEM)
    return pl.pallas_call(
        body, name=name, in_specs=[vm, vm], out_specs=[vm, vm],
        out_shape=[jax.ShapeDtypeStruct((512, 2 * NH * HP), gq.dtype), jax.ShapeDtypeStruct((KVL, NH * HP + NH * DV), gq.dtype)],
    )(gq, gkv)


def shard_wq_wkv_grad(dwq2, dwkv2, *, name):
    def body(q_ref, kv_ref, gq_ref, gkv_ref):
        for h in range(NH):
            gq_ref[h, :, 0:DN] = q_ref[0:QL, h * HP:h * HP + DN].astype(BF)
            for g in range(4):
                a = q_ref[0:QL, h * HP + DN + 8 * g:h * HP + DN + 8 * g + 8]
                c0 = NH * HP + h * HP + DN + _swap_start(g)
                gq_ref[h, :, DN + 8 * g:DN + 8 * g + 8] = (a + q_ref[0:QL, c0:c0 + 8]).astype(BF)
            gkv_ref[h, :, 0:DN] = kv_ref[:, h * HP:h * HP + DN].astype(BF)
            gkv_ref[h, :, DN:DN + DV] = kv_ref[:, NH * HP + h * DV:NH * HP + (h + 1) * DV].astype(BF)

    vm = pl.BlockSpec(memory_space=pltpu.VMEM)
    return pl.pallas_call(
        body, name=name, in_specs=[vm, vm], out_specs=[vm, vm],
        out_shape=[jax.ShapeDtypeStruct((NDEV, QL, (DN + DR)), BF), jax.ShapeDtypeStruct((NDEV, KVL, DN + DV), BF)],
    )(dwq2, dwkv2)


def unshard_cols(g, *, name, tm=256):
    _, K, n = g.shape
    tm = _pick(K, tm, 16)

    def body(g_ref, o_ref):
        for k in range(NDEV):
            o_ref[:, k * n:(k + 1) * n] = g_ref[k]

    return pl.pallas_call(
        body, name=name, grid=(K // tm,), in_specs=[pl.BlockSpec((NDEV, tm, n), lambda i: (0, i, 0))],
        out_specs=pl.BlockSpec((tm, NDEV * n), lambda i: (i, 0)), out_shape=jax.ShapeDtypeStruct((K, NDEV * n), g.dtype),
        compiler_params=pltpu.CompilerParams(dimension_semantics=("parallel",)),
    )(g)


def shard_cols(w, *, name, tm=256):
    K, n8 = w.shape
    n = n8 // NDEV
    tm = _pick(K, tm, 16)

    def body(w_ref, o_ref):
        for k in range(NDEV):
            o_ref[k] = w_ref[:, k * n:(k + 1) * n]

    return pl.pallas_call(
        body, name=name, grid=(K // tm,), in_specs=[pl.BlockSpec((tm, n8), lambda i: (i, 0))],
        out_specs=pl.BlockSpec((NDEV, tm, n), lambda i: (0, i, 0)), out_shape=jax.ShapeDtypeStruct((NDEV, K, n), w.dtype),
        compiler_params=pltpu.CompilerParams(dimension_semantics=("parallel",)),
    )(w)


def _rope_tables():
    t = np.arange(T)
    row = (t // GRID_W).astype(np.float32)
    col = (t % GRID_W).astype(np.float32)
    axis_dim = DR // 2
    inv = (np.float32(ROPE_THETA) ** (-np.arange(0, axis_dim, 2, dtype=np.float32) / np.float32(axis_dim))).astype(np.float32)
    ar, ac = (row[:, None] * inv).astype(np.float32), (col[:, None] * inv).astype(np.float32)
    cosv = np.concatenate([np.cos(ar), np.cos(ar), np.cos(ac), np.cos(ac)], axis=1).astype(np.float32)
    sinv = np.concatenate([-np.sin(ar), np.sin(ar), -np.sin(ac), np.sin(ac)], axis=1).astype(np.float32)
    ck = np.zeros((TKV, HP), np.float32)
    sk = np.zeros((TKV, HP), np.float32)
    ck[T:, DN:DN + DR] = 1.0
    ck[:T, DN:DN + DR] = cosv
    sk[:T, DN:DN + DR] = sinv
    cq = np.zeros((T, HP), np.float32)
    cq[:, :DN] = 1.0
    cq[:, DN:DN + DR] = cosv
    return jnp.asarray(ck), jnp.asarray(sk), jnp.asarray(cq), jnp.asarray(sk[:T])


def _local_step(x, ctx, tgt, mod_lat, mod_ctx, n1g, qg, kvg, n2g, fg, conv_w, conv_b, ffn_w, ffn_b, get_w, put_g, dep0):
    sh1, sc1, g1, sh2, sc2, g2 = [mod_lat[:, i * D:(i + 1) * D] for i in range(6)]
    csh1, csc1 = mod_ctx[:, 0:D], mod_ctx[:, D:2 * D]
    ck, sk, cq_t, sq_t = _rope_tables()
    qg_p = jnp.pad(qg, ((0, 0), (0, 512 - QL)))

    hcat = normmod_cat(ctx, x, n1g, csc1, csh1, sc1, sh1, dep0, name="normmod1")
    win = get_w("in", hcat)
    p = mm(hcat, win, M=T, tn=768, name="in_proj")
    pc = mm(hcat, win, M=TC, N=512, a_off=(T, 0), b_off=(0, O_KV), name="in_proj_ctx")
    wq2, wkv2, wao, wco, wo = get_w("mid", p)
    kh, vh, ckv = kvprep(pc, p, kvg, wkv2, ck, sk, name="kvprep")
    qr, cq = qprep(p, qg_p, wq2, cq_t, sq_t, name="qprep")
    o = attn_fwd(qr, kh, vh, name="attn_fwd")
    z = convz(p, conv_w, conv_b, name="convz")
    ya, yc, merged = out_proj_merge(o, wao, z, wco, p, name="attn_conv_out_gate_merge")
    a_out, x1, h2 = oproj_resid(merged, wo, x, g1, n2g, sc2, sh2, name="o_proj_resid_normmod2")
    wup, wdn = get_w("ffn", h2)
    u0 = mm(h2, wup, tb=True, o_stack=True, tn=1408, name="up_proj")
    f = ffn_act(u0, ffn_w, ffn_b, name="ffn_act")
    dn, dx2, dd, dfg, loss = down_final(f, wdn, x1, g2, fg, tgt, name="down_proj_final_loss")

    df = mm(dd, wdn, tb=True, tn=1408, name="down_proj_dx")
    dwdn = mm(f, dd, ta=True, out_dtype=BF, tm=1408, name="down_proj_dw")
    du0, dffn_w, dffn_b = ffn_act_bwd(u0, df, ffn_w, ffn_b, name="ffn_act_bwd")
    dwup = mm(du0, h2, ta=True, a_stack=True, out_dtype=BF, tm=1408, name="up_proj_dw")
    tok = put_g("ffn", dict(dwup=dwup, dwdn=dwdn))
    dx1, da, st2 = normmod_bwd(x1, dict(a=du0, b=wup, a_stack=True, tk=1408, dep=tok), n2g, sc2, dx2, dn, g1,
                               name="up_proj_dx_normmod2_bwd")

    dwo = mm(merged, da, ta=True, out_dtype=BF, tn=512, name="o_proj_dw")
    dya, dyc, dp, do, dz = oproj_dx_gate_bwd(da, wo, p, ya, yc, wao, wco, name="o_proj_dx_gate_merge_bwd")
    dwao = mm(o, dya, ta=True, out_dtype=BF, tn=512, name="attn_out_dw")
    dwco = mm(z, dyc, ta=True, out_dtype=BF, tn=512, name="conv_out_dw")
    tok = put_g("mid", dict(dwao=dwao, dwco=dwco, dwo=dwo))
    dp, dconv_w, dconv_b = convz_bwd(p, dz, conv_w, conv_b, dp, name="convz_bwd")
    dq, dk, dv = attn_bwd(qr, kh, vh, do, tok, name="attn_bwd")
    dp, dq2, dqg = qprep_bwd(p, dq, qg_p, wq2, cq_t, sq_t, dp, name="qprep_bwd")
    dwq2 = mm(cq, dq2, ta=True, name="q_up_dw")
    dp, dpc, dkv2, dkvg = kvprep_bwd(pc, p, dk, dv, kvg, wkv2, ck, sk, dp, name="kvprep_bwd")
    dwkv2 = mm(ckv, dkv2, ta=True, name="kv_up_dw")
    tok = put_g("qkv", dict(dwq2=dwq2, dwkv2=dwkv2))

    dwin = mm(dp, hcat, ta=True, K=T, tm=768, dep=tok, name="in_proj_dw")
    dwin_c = mm(dpc, hcat, ta=True, K=TC, b_off=(T, 0), name="in_proj_ctx_dw")
    tok = put_g("in", dict(dwin=dwin, dwin_c=dwin_c))
    dhc = mm(dpc, win, tb=True, N=D, K=512, b_off=(0, O_KV), name="in_proj_ctx_dx")
    dx, _, st1 = normmod_bwd(x, dict(a=dp, b=win, tb=True, tk=1536, dep=tok), n1g, sc1, dx1, a_out, g1,
                             name="in_proj_dx_normmod1_bwd")
    stc = normmod_bwd(ctx, dhc, n1g, csc1, None, None, None, name="normmod1_ctx_bwd")

    zrow = jnp.zeros((1, D), F32)
    dmod_lat = jnp.concatenate([st1[0:1], st1[1:2], st1[3:4], st2[0:1], st2[1:2], st2[3:4]], axis=1)
    dmod_ctx = jnp.concatenate([stc[0:1], stc[1:2], zrow, zrow, zrow, zrow], axis=1)
    return dict(
        loss=loss, dx=dx, dmod_lat=dmod_lat, dmod_ctx=dmod_ctx,
        dn1g=st1[2:3] + stc[2:3], dqg=dqg, dkvg=dkvg, dn2g=st2[2:3], dfg=dfg,
        dconv_w=dconv_w, dconv_b=dconv_b, dffn_w=dffn_w, dffn_b=dffn_b)


def _me():
    x, y, c = lax.axis_index("x"), lax.axis_index("y"), lax.axis_index("c")
    return x, y, c, 4 * x + 2 * y + c


def _peer(x, y, c, k):
    px = 1 - x if k & 4 else x
    py = 1 - y if k & 2 else y
    pc = 1 - c if k & 1 else c
    return (px, py, pc), 4 * px + 2 * py + pc


def _exchange_tiles(src_of_peer, buf, send_sem, recv_sem):
    x, y, c, me = _me()
    for k in range(1, NDEV):
        dev, lin = _peer(x, y, c, k)
        pltpu.make_async_remote_copy(src_ref=src_of_peer(lin), dst_ref=buf.at[me], send_sem=send_sem, recv_sem=recv_sem,
                                     device_id=dev, device_id_type=MESH).start()
    seven = buf.at[pl.ds(0, NDEV - 1)]
    pltpu.make_async_remote_copy(src_ref=seven, dst_ref=seven, send_sem=send_sem, recv_sem=recv_sem,
                                 device_id=(x, y, c), device_id_type=MESH).wait()


def _silu(z):
    return z * jax.nn.sigmoid(z)


def ada_fwd(c, c_ctx, ffn_w, conv_w, w_shard, b_shard, deps, *, name):
    nsh = w_shard.shape[1]
    deps = [d for d in deps if d is not None]

    def body(c_ref, cc_ref, fw_ref, cw_ref, w_ref, b_ref, *rest):
        s_ref, m_ref, mine, res, sems = rest[len(deps):]
        x, y, c, me = _me()
        mine[0:1, :] = _silu(c_ref[...])
        mine[1:2, :] = _silu(cc_ref[...])
        mine[2:5, :] = fw_ref[...]
        mine[5:8, :] = cw_ref[...]
        s_ref[me] = mine[...]
        _exchange_tiles(lambda lin: mine, s_ref, sems.at[0], sems.at[1])
        sall = s_ref[...].reshape(NDEV * 8, D).astype(BF)
        r = jnp.dot(sall, w_ref[...].astype(BF), preferred_element_type=F32) + b_ref[...]
        res[...] = r.reshape(NDEV, 8, nsh)
        m_ref[me] = res[me]
        _exchange_tiles(lambda lin: res.at[lin], m_ref, sems.at[2], sems.at[3])

    vm = pl.BlockSpec(memory_space=pltpu.VMEM)
    return pl.pallas_call(
        body, name=name, in_specs=[vm] * 6 + [pl.BlockSpec(memory_space=pl.ANY)] * len(deps), out_specs=[vm, vm],
        out_shape=[jax.ShapeDtypeStruct((NDEV, 8, D), F32), jax.ShapeDtypeStruct((NDEV, 8, nsh), F32)],
        scratch_shapes=[pltpu.VMEM((8, D), F32), pltpu.VMEM((NDEV, 8, nsh), F32), pltpu.SemaphoreType.DMA((4,))],
    )(c, c_ctx, ffn_w, conv_w, w_shard, b_shard, *deps)


P_DML, P_DMC, P_N1, P_QG, P_KVG, P_CB, P_N2, P_FB, P_FG, P_CW, P_FW, P_LOSS, P_ROWS = 0, 6, 12, 13, 14, 15, 16, 17, 23, 24, 27, 45, 48
FROWS = 3


def sync_small(r, deps, *, name):
    ins = [r["dmod_lat"], r["dmod_ctx"], r["dn1g"], r["dqg"], r["dkvg"], r["dconv_b"], r["dn2g"], r["dffn_b"], r["dfg"],
           r["dconv_w"], r["dffn_w"], r["loss"]]

    def put_wide(p, row0, row, n):
        for j in range(-(-n // D)):
            w = min(D, n - j * D)
            p[row0 + j:row0 + j + 1, 0:w] = row[:, j * D:j * D + w]

    def body(dml, dmc, n1, qg, kvg, cb, n2, fb, fg, cw, fw, loss, *rest):
        a_ref, sum_ref, p, sems = rest[len(deps):]
        x, y, c, me = _me()
        p[...] = jnp.zeros_like(p)
        put_wide(p, P_DML, dml, 6 * D)
        put_wide(p, P_DMC, dmc, 6 * D)
        put_wide(p, P_N1, n1, D)
        put_wide(p, P_QG, qg, 512)
        put_wide(p, P_KVG, kvg, KVL)
        put_wide(p, P_CB, cb, CONV)
        put_wide(p, P_N2, n2, D)
        put_wide(p, P_FG, fg, D)
        put_wide(p, P_LOSS, loss, 128)
        for s in range(2):
            put_wide(p, P_FB + FROWS * s, fb.at[s], DFF)
        for k in range(3):
            put_wide(p, P_CW + k, cw.at[k:k + 1], CONV)
            for s in range(2):
                put_wide(p, P_FW + FROWS * (2 * k + s), fw.at[s, k:k + 1], DFF)
        a_ref[me] = p[...]
        _exchange_tiles(lambda lin: p, a_ref, sems.at[0], sems.at[1])
        acc = a_ref[0]
        for k in range(1, NDEV):
            acc = acc + a_ref[k]
        sum_ref[...] = acc

    vm = pl.BlockSpec(memory_space=pltpu.VMEM)
    return pl.pallas_call(
        body, name=name, in_specs=[vm] * len(ins) + [pl.BlockSpec(memory_space=pl.ANY)] * len(deps), out_specs=[vm, vm],
        out_shape=[jax.ShapeDtypeStruct((NDEV, P_ROWS, D), F32), jax.ShapeDtypeStruct((P_ROWS, D), F32)],
        scratch_shapes=[pltpu.VMEM((P_ROWS, D), F32), pltpu.SemaphoreType.DMA((2,))],
    )(*ins, *deps)


def ada_bwd(s_all, dml, dmc, w_shard, c_ctx, *, name):
    nsh = w_shard.shape[1]

    def body(s_ref, dml_ref, dmc_ref, w_ref, c_ref, dw_ref, gc_ref, s16, dm16, part, buf, sems):
        x, y, c, me = _me()
        s16[...] = jnp.zeros_like(s16)
        dm16[...] = jnp.zeros_like(dm16)
        for k in range(NDEV):
            s16[k:k + 1, :] = s_ref[k, 0:1, :]
        s16[8:9, :] = s_ref[0, 1:2, :]
        dm16[0:8, :] = dml_ref[...]
        dm16[8:9, :] = dmc_ref[...]
        dw_ref[...] = lax.dot_general(s16[...].astype(BF), dm16[...].astype(BF), (((0,), (0,)), ((), ())),
                                      preferred_element_type=F32)
        part[...] = lax.dot_general(dm16[8:16, :].astype(BF), w_ref[...].astype(BF), (((1,), (1,)), ((), ())),
                                    preferred_element_type=F32)
        buf[me] = part[...]
        _exchange_tiles(lambda lin: part, buf, sems.at[0], sems.at[1])
        acc = buf[0]
        for k in range(1, NDEV):
            acc = acc + buf[k]
        z = c_ref[...]
        sg = jax.nn.sigmoid(z)
        gc_ref[...] = acc * (sg * (1.0 + z * (1.0 - sg)))

    vm = pl.BlockSpec(memory_space=pltpu.VMEM)
    return pl.pallas_call(
        body, name=name, in_specs=[vm] * 5, out_specs=[vm, vm],
        out_shape=[jax.ShapeDtypeStruct((D, nsh), F32), jax.ShapeDtypeStruct((8, D), F32)],
        scratch_shapes=[pltpu.VMEM((16, D), F32), pltpu.VMEM((16, nsh), F32), pltpu.VMEM((8, D), F32),
                        pltpu.VMEM((NDEV, 8, D), F32), pltpu.SemaphoreType.DMA((2,))],
    )(s_all, dml, dmc, w_shard, c_ctx)


HBM_SPEC = pl.BlockSpec(memory_space=pltpu.HBM)
SEM_SPEC = pl.BlockSpec(memory_space=pltpu.SEMAPHORE)
EFFECT = pltpu.SideEffectType.DATAFLOW_SIDE_EFFECTING


ALL_PEERS = tuple(range(1, NDEV))
FIRST_HOP = (1, 2, 4, 6)
RELAY = (2, 4, 6)


def _exchange_copies(srcs, lands, send, recv, per_peer, peers):
    x, y, c, me = _me()
    n = len(peers)
    cps = []
    for t in range(len(srcs)):
        for j, k in enumerate(peers):
            dev, lin = _peer(x, y, c, k)
            cps.append(pltpu.make_async_remote_copy(
                src_ref=srcs[t].at[lin] if per_peer else srcs[t], dst_ref=lands[t].at[me],
                send_sem=send.at[n * t + j], recv_sem=recv.at[n * t + j], device_id=dev, device_id_type=MESH))
    return cps


def _relay_copies(lands, send, recv):
    x, y, c, me = _me()
    n = len(RELAY)
    cps = []
    for t in range(len(lands)):
        for j, k in enumerate(RELAY):
            slot = lands[t].at[_peer(x, y, c, k)[1]]
            cps.append(pltpu.make_async_remote_copy(
                src_ref=slot, dst_ref=slot, send_sem=send.at[n * t + j], recv_sem=recv.at[n * t + j],
                device_id=(x, y, 1 - c), device_id_type=MESH))
    return cps


def _own_copies(srcs, lands, own, per_peer):
    me = _me()[3]
    return [pltpu.make_async_copy(srcs[t].at[me] if per_peer else srcs[t], lands[t].at[me], own.at[t])
            for t in range(len(srcs))]


def exchange_start(srcs, *, per_peer, name, dep=None, peers=ALL_PEERS):
    nt = len(srcs)
    ns = len(peers) * nt
    land_shapes = [(a.shape if per_peer else (NDEV,) + a.shape) for a in srcs]
    deps = [] if dep is None else [dep]

    def body(*refs):
        src, land = refs[:nt], refs[nt:2 * nt]
        send, recv, own = refs[2 * nt + len(deps):2 * nt + len(deps) + 3]
        for cp in _exchange_copies(src, land, send, recv, per_peer, peers) + _own_copies(src, land, own, per_peer):
            cp.start()
        refs[-1][...] = jnp.zeros_like(refs[-1])

    hb = lambda a: pltpu.with_memory_space_constraint(a, pltpu.HBM)
    outs = pl.pallas_call(
        body, name=name,
        out_shape=(pltpu.SemaphoreType.DMA((ns,)), pltpu.SemaphoreType.DMA((ns,)), pltpu.SemaphoreType.DMA((nt,)),
                   *[pltpu.HBM(a.shape, a.dtype) for a in srcs], *[pltpu.HBM(s, a.dtype) for s, a in zip(land_shapes, srcs)],
                   jax.ShapeDtypeStruct((8, 128), F32)),
        in_specs=[HBM_SPEC] * (2 * nt) + [pl.BlockSpec(memory_space=pl.ANY)] * len(deps),
        out_specs=(SEM_SPEC, SEM_SPEC, SEM_SPEC, *([HBM_SPEC] * (2 * nt)), pl.BlockSpec(memory_space=pltpu.VMEM)),
        input_output_aliases={i: 3 + i for i in range(2 * nt)},
        compiler_params=pltpu.CompilerParams(has_side_effects=EFFECT),
    )(*[hb(a) for a in srcs], *[hb(lax.empty(s, a.dtype)) for s, a in zip(land_shapes, srcs)], *deps)
    return dict(send=outs[0], recv=outs[1], own=outs[2], src=list(outs[3:3 + nt]), land=list(outs[3 + nt:3 + 2 * nt]),
                token=outs[-1], per_peer=per_peer, peers=peers)


def exchange_wait(h, after, *, name):
    nt = len(h["src"])
    per_peer, peers = h["per_peer"], h["peers"]

    def body(*refs):
        src, land, send, recv, own = refs[:nt], refs[nt:2 * nt], refs[2 * nt], refs[2 * nt + 1], refs[2 * nt + 2]
        for cp in _exchange_copies(src, land, send, recv, per_peer, peers):
            cp.wait_send()
            cp.wait_recv()
        for cp in _own_copies(src, land, own, per_peer):
            cp.wait()

    outs = pl.pallas_call(
        body, name=name,
        out_shape=(*[pltpu.HBM(a.shape, a.dtype) for a in h["src"]], *[pltpu.HBM(a.shape, a.dtype) for a in h["land"]]),
        in_specs=[HBM_SPEC] * (2 * nt) + [SEM_SPEC, SEM_SPEC, SEM_SPEC, pl.BlockSpec(memory_space=pl.ANY)],
        out_specs=tuple([HBM_SPEC] * (2 * nt)),
        input_output_aliases={i: i for i in range(2 * nt)},
        compiler_params=pltpu.CompilerParams(has_side_effects=EFFECT),
    )(*h["src"], *h["land"], h["send"], h["recv"], h["own"], after)
    return list(outs[nt:])


def relay_start(lands, *, name):
    nt = len(lands)
    ns = len(RELAY) * nt

    def body(*refs):
        for cp in _relay_copies(refs[:nt], refs[nt], refs[nt + 1]):
            cp.start()

    outs = pl.pallas_call(
        body, name=name,
        out_shape=(pltpu.SemaphoreType.DMA((ns,)), pltpu.SemaphoreType.DMA((ns,)),
                   *[pltpu.HBM(a.shape, a.dtype) for a in lands]),
        in_specs=[HBM_SPEC] * nt, out_specs=(SEM_SPEC, SEM_SPEC, *([HBM_SPEC] * nt)),
        input_output_aliases={i: 2 + i for i in range(nt)},
        compiler_params=pltpu.CompilerParams(has_side_effects=EFFECT),
    )(*lands)
    return dict(send=outs[0], recv=outs[1], land=list(outs[2:]))


def relay_wait(h, *, name):
    nt = len(h["land"])

    def body(*refs):
        for cp in _relay_copies(refs[:nt], refs[nt], refs[nt + 1]):
            cp.wait_send()
            cp.wait_recv()

    outs = pl.pallas_call(
        body, name=name, out_shape=tuple(pltpu.HBM(a.shape, a.dtype) for a in h["land"]),
        in_specs=[HBM_SPEC] * nt + [SEM_SPEC, SEM_SPEC], out_specs=tuple([HBM_SPEC] * nt),
        input_output_aliases={i: i for i in range(nt)},
        compiler_params=pltpu.CompilerParams(has_side_effects=EFFECT),
    )(*h["land"], h["send"], h["recv"])
    return list(outs)


def _adamw_math(w, g, m, v):
    nm = B1 * m + (1.0 - B1) * g
    nv = B2 * v + (1.0 - B2) * (g * g)
    m_hat = nm / (1.0 - B1 ** STEP)
    v_hat = nv / (1.0 - B2 ** STEP)
    return -LR * (m_hat / (jnp.sqrt(v_hat) + AEPS) + WD * w), nm, nv


def adamw_many(ws, gs, ms, vs, *, name):
    n = len(ws)

    def body(*refs):
        for k in range(n):
            d, nm, nv = _adamw_math(refs[k][...], refs[n + k][...], refs[2 * n + k][...], refs[3 * n + k][...])
            refs[4 * n + k][...] = d
            refs[5 * n + k][...] = nm
            refs[6 * n + k][...] = nv

    vm = pl.BlockSpec(memory_space=pltpu.VMEM)
    sh = [jax.ShapeDtypeStruct(w.shape, F32) for w in ws]
    outs = pl.pallas_call(body, name=name, in_specs=[vm] * (4 * n), out_specs=[vm] * (3 * n), out_shape=sh * 3,
                          )(*ws, *gs, *ms, *vs)
    return outs[:n], outs[n:2 * n], outs[2 * n:]


def adamw(w, g, m, v, *, name, tr=256):
    R, C = w.shape
    tr = _pick(R, tr, 8)

    def body(w_ref, g_ref, m_ref, v_ref, d_ref, nm_ref, nv_ref):
        d_ref[...], nm_ref[...], nv_ref[...] = _adamw_math(w_ref[...], g_ref[...], m_ref[...], v_ref[...])

    blk = pl.BlockSpec((tr, C), lambda i: (i, 0))
    sh = jax.ShapeDtypeStruct((R, C), F32)
    return pl.pallas_call(
        body, name=name, grid=(R // tr,), in_specs=[blk, blk, blk, blk], out_specs=[blk, blk, blk],
        out_shape=[sh, sh, sh], compiler_params=pltpu.CompilerParams(dimension_semantics=("parallel",)),
    )(w, g, m, v)


def adamw_slots(w, slots, m, v, *, name, tr=256):
    unit = w.ndim == 3
    R, C = w.shape[0], w.shape[-1]
    if R % 16 == 0:
        tr = _pick(R, tr, 16)
    else:
        tr = 144

    def body(w_ref, s_ref, m_ref, v_ref, g_ref, d_ref, nm_ref, nv_ref):
        g = s_ref[0].astype(F32)
        for k in range(1, NDEV):
            g = g + s_ref[k].astype(F32)
        g_ref[...] = g
        d_ref[...], nm_ref[...], nv_ref[...] = _adamw_math(w_ref[...], g, m_ref[...], v_ref[...])

    blk = pl.BlockSpec((tr, None, C), lambda i: (i, 0, 0)) if unit else pl.BlockSpec((tr, C), lambda i: (i, 0))
    sh = jax.ShapeDtypeStruct(w.shape, F32)
    return pl.pallas_call(
        body, name=name, grid=(pl.cdiv(R, tr),), in_specs=[blk, pl.BlockSpec((NDEV, tr, C), lambda i: (0, i, 0)), blk, blk],
        out_specs=[blk, blk, blk, blk], out_shape=[sh, sh, sh, sh],
        compiler_params=pltpu.CompilerParams(dimension_semantics=("parallel",)),
    )(w, slots, m, v)


def _padc(a, n=D):
    return jnp.pad(a, ((0, 0), (0, n - a.shape[1])))


def kernel(x, c, ctx, c_ctx, w_ada, b_ada, norm1_g, w_in, q_norm_g, kv_norm_g, w_uq, w_ukv, conv_w, conv_b, w_attn_out, w_conv_out, w_o, norm2_g, w_up, ffn_conv_w, ffn_conv_b, w_down, final_g, loss_target, m_c_ctx, m_w_ada, m_b_ada, m_norm1_g, m_w_in, m_q_norm_g, m_kv_norm_g, m_w_uq, m_w_ukv, m_conv_w, m_conv_b, m_w_attn_out, m_w_conv_out, m_w_o, m_norm2_g, m_w_up, m_ffn_conv_w, m_ffn_conv_b, m_w_down, m_final_g, v_c_ctx, v_w_ada, v_b_ada, v_norm1_g, v_w_in, v_q_norm_g, v_kv_norm_g, v_w_uq, v_w_ukv, v_conv_w, v_conv_b, v_w_attn_out, v_w_conv_out, v_w_o, v_norm2_g, v_w_up, v_ffn_conv_w, v_ffn_conv_b, v_w_down, v_final_g):
    me = 4 * lax.axis_index("x") + 2 * lax.axis_index("y") + lax.axis_index("c")
    W = dict(c_ctx=c_ctx, w_ada=w_ada, b_ada=b_ada, norm1_g=norm1_g, w_in=w_in, q_norm_g=q_norm_g, kv_norm_g=kv_norm_g,
             w_uq=w_uq, w_ukv=w_ukv, conv_w=conv_w, conv_b=conv_b, w_attn_out=w_attn_out, w_conv_out=w_conv_out, w_o=w_o,
             norm2_g=norm2_g, w_up=w_up, ffn_conv_w=ffn_conv_w, ffn_conv_b=ffn_conv_b, w_down=w_down, final_g=final_g)
    M = dict(c_ctx=m_c_ctx, w_ada=m_w_ada, b_ada=m_b_ada, norm1_g=m_norm1_g, w_in=m_w_in, q_norm_g=m_q_norm_g,
             kv_norm_g=m_kv_norm_g, w_uq=m_w_uq, w_ukv=m_w_ukv, conv_w=m_conv_w, conv_b=m_conv_b, w_attn_out=m_w_attn_out,
             w_conv_out=m_w_conv_out, w_o=m_w_o, norm2_g=m_norm2_g, w_up=m_w_up, ffn_conv_w=m_ffn_conv_w,
             ffn_conv_b=m_ffn_conv_b, w_down=m_w_down, final_g=m_final_g)
    V = dict(c_ctx=v_c_ctx, w_ada=v_w_ada, b_ada=v_b_ada, norm1_g=v_norm1_g, w_in=v_w_in, q_norm_g=v_q_norm_g,
             kv_norm_g=v_kv_norm_g, w_uq=v_w_uq, w_ukv=v_w_ukv, conv_w=v_conv_w, conv_b=v_conv_b, w_attn_out=v_w_attn_out,
             w_conv_out=v_w_conv_out, w_o=v_w_o, norm2_g=v_norm2_g, w_up=v_w_up, ffn_conv_w=v_ffn_conv_w,
             ffn_conv_b=v_ffn_conv_b, w_down=v_w_down, final_g=v_final_g)
    names = list(W)
    transposed = ("w_up",)
    as2d = lambda k, a: (a.reshape(1, -1) if a.ndim == 1 else
                         a[0].T if k in transposed else a.reshape(a.shape[-2], a.shape[-1]))
    W2 = {k: as2d(k, a) for k, a in W.items()}
    M2 = {k: as2d(k, a) for k, a in M.items()}
    V2 = {k: as2d(k, a) for k, a in V.items()}
    unit3 = lambda a: jnp.transpose(a, (2, 0, 1))
    W3, M3, V3 = unit3(W["w_in"]), unit3(M["w_in"]), unit3(V["w_in"])
    nsh = W2["w_ada"].shape[1]

    b_sh = lax.dynamic_slice(W2["b_ada"], (0, me * nsh), (1, nsh))
    s_all, m_all = ada_fwd(c, W2["c_ctx"], _padc(W2["ffn_conv_w"]), _padc(W2["conv_w"]), W2["w_ada"], b_sh, [],
                           name="ada_fwd")
    mod_lat = m_all[:, 0, :].reshape(1, 6 * D)
    mod_ctx = m_all[:, 1, :].reshape(1, 6 * D)
    ffn_w_full = s_all[:, 2:5, :2 * DFF // NDEV].transpose(1, 0, 2).reshape(3, 2 * DFF)
    conv_w_full = s_all[:, 5:8, :CONV // NDEV].transpose(1, 0, 2).reshape(3, CONV)

    stage_w = {"in": ["w_in"], "mid": ["w_uq", "w_ukv", "w_attn_out", "w_conv_out", "w_o"], "ffn": ["w_up", "w_down"]}
    two_level = ("in", "mid")
    ag, tok = {}, m_all
    for st, nms in stage_w.items():
        ag[st] = exchange_start([W2[nm].astype(BF) for nm in nms], per_peer=False, dep=tok, name="ag_start_" + st,
                                peers=FIRST_HOP if st in two_level else ALL_PEERS)
        tok = ag[st]["token"]

    def get_w(stage, after):
        lands = exchange_wait(ag[stage], after, name="ag_wait_" + stage)
        if stage in two_level:
            lands = relay_wait(relay_start(lands, name="ag_relay_" + stage), name="ag_relay_wait_" + stage)
        g = dict(zip(stage_w[stage], lands))
        if stage == "in":
            return build_win(g["w_in"], name="build_win")
        if stage == "mid":
            wq2, wkv2 = build_wq_wkv(g["w_uq"], g["w_ukv"], name="build_wq_wkv")
            return (wq2, wkv2, unshard_cols(g["w_attn_out"], name="unshard_w_attn_out"),
                    unshard_cols(g["w_conv_out"], name="unshard_w_conv_out"), g["w_o"].reshape(D, D))
        return g["w_up"].reshape(2 * DFF, D), g["w_down"].reshape(DFF, D)

    stage_g = {"ffn": ["w_up", "w_down"], "mid": ["w_attn_out", "w_conv_out", "w_o"], "qkv": ["w_uq", "w_ukv"],
               "in": ["w_in"]}
    rs = {}

    def put_g(stage, g):
        if stage == "in":
            parts = [shard_win_grad(g["dwin"], g["dwin_c"], name="shard_win_grad")]
        elif stage == "mid":
            parts = [shard_cols(g["dwao"], name="shard_w_attn_out"), shard_cols(g["dwco"], name="shard_w_conv_out"),
                     g["dwo"].reshape(NDEV, D // NDEV, D)]
        elif stage == "qkv":
            parts = list(shard_wq_wkv_grad(g["dwq2"], g["dwkv2"], name="shard_wq_wkv_grad"))
        else:
            parts = [g["dwup"].reshape(NDEV, 2 * DFF // NDEV, D), g["dwdn"].reshape(NDEV, DFF // NDEV, D)]
        rs[stage] = exchange_start(parts, per_peer=True, name="rs_start_" + stage)
        return rs[stage]["token"]

    r = _local_step(x[0], ctx[0], loss_target[0], mod_lat, mod_ctx, W2["norm1_g"], W2["q_norm_g"], W2["kv_norm_g"],
                    W2["norm2_g"], W2["final_g"], conv_w_full, W2["conv_b"], ffn_w_full, W2["ffn_conv_b"], get_w, put_g,
                    ag["ffn"]["token"])

    G, DL, NM, NV = {}, {}, {}, {}

    def finish(stage, after):
        for nm, sl in zip(stage_g[stage], exchange_wait(rs[stage], after, name="rs_wait_" + stage)):
            wmv = (W3, M3, V3) if nm == "w_in" else (W2[nm], M2[nm], V2[nm])
            G[nm], DL[nm], NM[nm], NV[nm] = adamw_slots(wmv[0], sl, wmv[1], wmv[2], name="adamw_" + nm)
            after = DL[nm]
        return after

    after = r["dx"]
    for st in ("ffn", "mid", "qkv"):
        after = finish(st, after)

    a_buf, ssum = sync_small(r, [DL[nm] for st in ("ffn", "mid", "qkv") for nm in stage_g[st]], name="sync_small")
    loss = ssum[P_LOSS, 0]
    G["norm1_g"] = ssum[P_N1:P_N1 + 1]
    G["q_norm_g"] = ssum[P_QG:P_QG + 1, :QL]
    G["kv_norm_g"] = ssum[P_KVG:P_KVG + 1, :KVL]
    G["conv_b"] = ssum[P_CB:P_CB + 1, :CONV]
    G["norm2_g"] = ssum[P_N2:P_N2 + 1]
    G["ffn_conv_b"] = ssum[P_FB:P_FB + 2 * FROWS].reshape(1, 2, FROWS * D)[:, :, :DFF].reshape(1, 2 * DFF)
    G["final_g"] = ssum[P_FG:P_FG + 1]
    G["conv_w"] = lax.dynamic_slice(ssum[P_CW:P_CW + 3, :CONV], (0, me * (CONV // NDEV)), (3, CONV // NDEV))
    fw_full = ssum[P_FW:P_FW + 6 * FROWS].reshape(3, 2, FROWS * D)[:, :, :DFF].reshape(3, 2 * DFF)
    G["ffn_conv_w"] = lax.dynamic_slice(fw_full, (0, me * (2 * DFF // NDEV)), (3, 2 * DFF // NDEV))
    G["b_ada"] = (ssum[P_DML:P_DML + 6] + ssum[P_DMC:P_DMC + 6]).reshape(1, 6 * D)

    dml = lax.dynamic_slice(a_buf[:, P_DML:P_DML + 6, :].reshape(NDEV, 6 * D), (0, me * nsh), (NDEV, nsh))
    dmc = lax.dynamic_slice(ssum[P_DMC:P_DMC + 6].reshape(1, 6 * D), (0, me * nsh), (1, nsh))
    G["w_ada"], gcc = ada_bwd(s_all, dml, dmc, W2["w_ada"], W2["c_ctx"], name="ada_bwd")
    G["c_ctx"] = gcc[0:1]

    DL["w_ada"], NM["w_ada"], NV["w_ada"] = adamw(W2["w_ada"], G["w_ada"], M2["w_ada"], V2["w_ada"], name="adamw_w_ada")
    small = ["c_ctx", "b_ada", "norm1_g", "q_norm_g", "kv_norm_g", "conv_b", "norm2_g", "ffn_conv_b", "final_g", "conv_w",
             "ffn_conv_w"]
    ds, nms, nvs = adamw_many([W2[k] for k in small], [G[k] for k in small], [M2[k] for k in small],
                              [V2[k] for k in small], name="adamw_small")
    for k, nm in enumerate(small):
        DL[nm], NM[nm], NV[nm] = ds[k], nms[k], nvs[k]
    finish("in", ds[0])

    outs = [loss, r["dx"][None]]
    for grp in (G, DL, NM, NV):
        outs += [grp[nm].T[None] if nm in transposed else
                 jnp.transpose(grp[nm], (1, 2, 0)) if nm == "w_in" else grp[nm].reshape(W[nm].shape) for nm in names]
    return tuple(outs)
```

```python
import functools
import numpy as np
import jax
import jax.numpy as jnp
from jax import lax
from jax.experimental import pallas as pl
from jax.experimental.pallas import tpu as pltpu

F32 = jnp.float32
BF = jnp.bfloat16
MESH = pl.DeviceIdType.MESH

D = 1024
T = 2048
TC = 256
TKV = T + TC
GRID_W = 64
NH = 8
DN = 64
DR = 32
DV = 64
QL = 384
KVL = 256
CONV = 512
DFF = 2816
EPS = 1e-6
ROPE_THETA = 10000.0
SCALE = (DN + DR) ** -0.5
NDEV = 8
HP = 128

O_GA, O_GC, O_KV, O_Q, O_CV = 0, 1024, 2048, 2560, 3072
NIN = 4608
CVB = 256
N_IN = 4256
SH_IN = N_IN // NDEV

LR, B1, B2, AEPS, WD, STEP = 0.001, 0.9, 0.999, 1e-08, 0.01, 10


def _pick(n, target, mult=128):
    best = None
    for d in range(mult, min(n, target) + 1, mult):
        if n % d == 0:
            best = d
    return best if best is not None else n


def _swap_start(g):
    return 8 * (g ^ 1)


def mm(a, b, *, ta=False, tb=False, out_dtype=F32, name, tm=1024, tn=1024, tk=2048, M=None, N=None, K=None,
       a_off=(0, 0), b_off=(0, 0), a_stack=False, b_stack=False, o_stack=False, dep=None):
    def dims(arr, stack):
        return (arr.shape[1], 2 * arr.shape[2]) if stack else arr.shape

    ar, ac = dims(a, a_stack)
    br, bc = dims(b, b_stack)
    M = M or ((ac if ta else ar) - a_off[1 if ta else 0])
    K = K or ((ar if ta else ac) - a_off[0 if ta else 1])
    N = N or ((br if tb else bc) - b_off[0 if tb else 1])
    tm = _pick(M, tm, 128 if ta else 16)
    tn = _pick(N // 2 if (o_stack or (b_stack and not tb)) else N, tn, 128)
    tk = _pick(K // 2 if ((a_stack and not ta) or (b_stack and tb)) else K, tk, 128)
    nk = K // tk
    ca = 0 if ta else 1
    cb = 1 if tb else 0

    def body(a_ref, b_ref, *rest):
        o_ref, acc = rest[-2:]
        k = pl.program_id(2)
        part = lax.dot_general(a_ref[...].astype(BF), b_ref[...].astype(BF),
                               (((ca,), (cb,)), ((), ())), preferred_element_type=F32)
        if nk == 1:
            o_ref[...] = part.astype(o_ref.dtype)
        else:
            @pl.when(k == 0)
            def _():
                acc[...] = part

            @pl.when(k > 0)
            def _():
                acc[...] += part

            @pl.when(k == nk - 1)
            def _():
                o_ref[...] = acc[...].astype(o_ref.dtype)

    def spec(blk, rc, off, stack, ncols):
        assert off[0] % blk[0] == 0 and off[1] % blk[1] == 0, (name, blk, off)
        ro, co = off[0] // blk[0], off[1] // blk[1]
        if not stack:
            return pl.BlockSpec(blk, lambda i, j, k: (rc(i, j, k)[0] + ro, rc(i, j, k)[1] + co))
        nhb = ncols // 2 // blk[1]
        return pl.BlockSpec((None,) + blk,
                            lambda i, j, k: ((rc(i, j, k)[1] + co) // nhb, rc(i, j, k)[0] + ro, (rc(i, j, k)[1] + co) % nhb))

    a_spec = spec((tk, tm), lambda i, j, k: (k, i), a_off, a_stack, ac) if ta else \
        spec((tm, tk), lambda i, j, k: (i, k), a_off, a_stack, ac)
    b_spec = spec((tn, tk), lambda i, j, k: (j, k), b_off, b_stack, bc) if tb else \
        spec((tk, tn), lambda i, j, k: (k, j), b_off, b_stack, bc)
    o_spec = spec((tm, tn), lambda i, j, k: (i, j), (0, 0), o_stack, N)
    o_shape = (2, M, N // 2) if o_stack else (M, N)
    deps = [] if dep is None else [dep]
    return pl.pallas_call(
        body, name=name, grid=(M // tm, N // tn, nk),
        in_specs=[a_spec, b_spec] + [pl.BlockSpec(memory_space=pl.ANY)] * len(deps),
        out_specs=o_spec, out_shape=jax.ShapeDtypeStruct(o_shape, out_dtype),
        scratch_shapes=[pltpu.VMEM((tm, tn) if nk > 1 else (8, 128), F32)],
        compiler_params=pltpu.CompilerParams(dimension_semantics=("parallel", "parallel", "arbitrary")),
    )(a, b, *deps)


def _row(width):
    return pl.BlockSpec((1, width), lambda *_: (0, 0))


NLAT = T // TC


def normmod_cat(ctx, x, g, csc, csh, sc, sh, dep, *, name, tm=256):
    assert tm == TC

    def body(c_ref, x_ref, g_ref, csc_ref, csh_ref, sc_ref, sh_ref, dep_ref, h_ref):
        last = pl.program_id(0) == NLAT
        xv = jnp.where(last, c_ref[...], x_ref[...])
        scv = jnp.where(last, csc_ref[...], sc_ref[...])
        shv = jnp.where(last, csh_ref[...], sh_ref[...])
        r = lax.rsqrt(jnp.mean(xv * xv, axis=-1, keepdims=True) + EPS)
        h_ref[...] = ((xv * r * g_ref[...]) * (1.0 + scv) + shv).astype(BF)

    return pl.pallas_call(
        body, name=name, grid=(TKV // tm,),
        in_specs=[pl.BlockSpec((tm, D), lambda i: (0, 0)), pl.BlockSpec((tm, D), lambda i: (jnp.minimum(i, NLAT - 1), 0)),
                  _row(D), _row(D), _row(D), _row(D), _row(D), pl.BlockSpec(memory_space=pl.ANY)],
        out_specs=pl.BlockSpec((tm, D), lambda i: (i, 0)), out_shape=jax.ShapeDtypeStruct((TKV, D), BF),
        compiler_params=pltpu.CompilerParams(dimension_semantics=("parallel",)),
    )(ctx, x, g, csc, csh, sc, sh, dep)


def kvprep(pc, p, kvg, wkv2, ck, sk, *, name, tm=256):
    assert tm == TC
    nb = TKV // tm
    kvcol = O_KV // 512

    def body(pc_ref, p_ref, g_ref, w_ref, ck_ref, sk_ref, k_ref, v_ref, ckv_ref):
        i = pl.program_id(0)
        t = jnp.where(i == NLAT, pc_ref[...], p_ref[...])
        pk = t[:, :KVL]
        r = lax.rsqrt(jnp.mean(pk * pk, axis=-1, keepdims=True) + EPS)
        ckv = (pk * r * g_ref[...]).astype(BF)
        ckv_ref[...] = ckv
        kv2 = jnp.dot(ckv, w_ref[...], preferred_element_type=F32)
        krr = t[:, KVL:KVL + HP] * ck_ref[...] + t[:, KVL + HP:KVL + 2 * HP] * sk_ref[...]
        k_ref[...] = (kv2[:, :NH * HP] + jnp.concatenate([krr] * NH, axis=1)).astype(BF)
        v_ref[...] = kv2[:, NH * HP:].astype(BF)

    return pl.pallas_call(
        body, name=name, grid=(nb,),
        in_specs=[pl.BlockSpec((tm, 512), lambda i: (0, 0)),
                  pl.BlockSpec((tm, 512), lambda i: (jnp.minimum(i, NLAT - 1), kvcol)),
                  _row(KVL), pl.BlockSpec((KVL, NH * HP + NH * DV), lambda i: (0, 0)),
                  pl.BlockSpec((tm, HP), lambda i: (i, 0)), pl.BlockSpec((tm, HP), lambda i: (i, 0))],
        out_specs=[pl.BlockSpec((tm, NH * HP), lambda i: (i, 0)), pl.BlockSpec((tm, NH * DV), lambda i: (i, 0)),
                   pl.BlockSpec((tm, KVL), lambda i: (i, 0))],
        out_shape=[jax.ShapeDtypeStruct((TKV, NH * HP), BF), jax.ShapeDtypeStruct((TKV, NH * DV), BF),
                   jax.ShapeDtypeStruct((TKV, KVL), BF)],
        compiler_params=pltpu.CompilerParams(dimension_semantics=("parallel",)),
    )(pc, p, kvg, wkv2, ck, sk)


def qprep(p, qg, wq2, cq_t, sq_t, *, name, tm=256):
    qcol = O_Q // 512

    def body(p_ref, g_ref, w_ref, c_ref, s_ref, q_ref, cq_ref):
        pq = p_ref[...]
        r = lax.rsqrt(jnp.sum(pq * pq, axis=-1, keepdims=True) * (1.0 / QL) + EPS)
        cq = (pq * r * g_ref[...]).astype(BF)
        cq_ref[...] = cq
        q2 = jnp.dot(cq, w_ref[...], preferred_element_type=F32)
        cc = jnp.concatenate([c_ref[...]] * NH, axis=1)
        ss = jnp.concatenate([s_ref[...]] * NH, axis=1)
        q_ref[...] = (q2[:, :NH * HP] * cc + q2[:, NH * HP:] * ss).astype(BF)

    return pl.pallas_call(
        body, name=name, grid=(T // tm,),
        in_specs=[pl.BlockSpec((tm, 512), lambda i: (i, qcol)), _row(512),
                  pl.BlockSpec((512, 2 * NH * HP), lambda i: (0, 0)),
                  pl.BlockSpec((tm, HP), lambda i: (i, 0)), pl.BlockSpec((tm, HP), lambda i: (i, 0))],
        out_specs=[pl.BlockSpec((tm, NH * HP), lambda i: (i, 0)), pl.BlockSpec((tm, 512), lambda i: (i, 0))],
        out_shape=[jax.ShapeDtypeStruct((T, NH * HP), BF), jax.ShapeDtypeStruct((T, 512), BF)],
        compiler_params=pltpu.CompilerParams(dimension_semantics=("parallel",)),
    )(p, qg, wq2, cq_t, sq_t)


def _head_mask(h):
    lanes = lax.broadcasted_iota(jnp.int32, (1, 2 * DV), 1)
    return (lanes // DV) == (h % 2)


LOG2E = 1.4426950408889634


def attn_fwd(q, k, v, *, name, tq=1024, kc=768):
    def body(q_ref, k_ref, v_ref, o_ref, lse_ref):
        h = pl.program_id(1)
        qv = q_ref[...]
        m = l = acc = None
        for c in range(TKV // kc):
            s = lax.dot_general(qv, k_ref[c * kc:(c + 1) * kc, :], (((1,), (1,)), ((), ())),
                                preferred_element_type=F32) * (SCALE * LOG2E)
            mc = jnp.max(s, axis=-1, keepdims=True)
            if c == 0:
                m = mc
                e = jnp.exp2(s - m)
                l = jnp.sum(e, axis=-1, keepdims=True)
                acc = jnp.dot(e.astype(BF), v_ref[c * kc:(c + 1) * kc, :], preferred_element_type=F32)
            else:
                mn = jnp.maximum(m, mc)
                a = jnp.exp2(m - mn)
                e = jnp.exp2(s - mn)
                l = l * a + jnp.sum(e, axis=-1, keepdims=True)
                acc = acc * a + jnp.dot(e.astype(BF), v_ref[c * kc:(c + 1) * kc, :], preferred_element_type=F32)
                m = mn
        o2 = jnp.where(_head_mask(h), acc * (1.0 / l), 0.0).astype(BF)
        lse_ref[...] = jnp.broadcast_to(m + jnp.log(l) * LOG2E, (tq, HP))

        @pl.when(h % 2 == 0)
        def _():
            o_ref[...] = o2

        @pl.when(h % 2 == 1)
        def _():
            o_ref[...] = o_ref[...] + o2

    return pl.pallas_call(
        body, name=name, grid=(T // tq, NH),
        in_specs=[pl.BlockSpec((tq, HP), lambda i, h: (i, h)), pl.BlockSpec((TKV, HP), lambda i, h: (0, h)),
                  pl.BlockSpec((TKV, 2 * DV), lambda i, h: (0, h // 2))],
        out_specs=[pl.BlockSpec((tq, 2 * DV), lambda i, h: (i, h // 2)), pl.BlockSpec((tq, HP), lambda i, h: (i, h))],
        out_shape=[jax.ShapeDtypeStruct((T, NH * DV), BF), jax.ShapeDtypeStruct((T, NH * HP), F32)],
        compiler_params=pltpu.CompilerParams(dimension_semantics=("parallel", "arbitrary")),
    )(q, k, v)


def _shift_dn(x):
    n = x.shape[0]
    rows = lax.broadcasted_iota(jnp.int32, (n, 1), 0)
    return jnp.where(rows == 0, 0.0, pltpu.roll(x, 1, axis=0))


def _shift_up(x):
    n = x.shape[0]
    rows = lax.broadcasted_iota(jnp.int32, (n, 1), 0)
    return jnp.where(rows == n - 1, 0.0, pltpu.roll(x, n - 1, axis=0))


def _conv(x, w_ref, b_ref):
    return b_ref[...] + _shift_dn(x) * w_ref[0:1, :] + x * w_ref[1:2, :] + _shift_up(x) * w_ref[2:3, :]


def _conv_t(dy, w_ref):
    return _shift_up(dy) * w_ref[0:1, :] + dy * w_ref[1:2, :] + _shift_dn(dy) * w_ref[2:3, :]


def _conv_wgrad(dw_ref, dy, x):
    dw_ref[0:1, :] = jnp.sum(dy * _shift_dn(x), axis=0, keepdims=True)
    dw_ref[1:2, :] = jnp.sum(dy * x, axis=0, keepdims=True)
    dw_ref[2:3, :] = jnp.sum(dy * _shift_up(x), axis=0, keepdims=True)


def convz(p, cw, cb, *, name):
    o0 = O_CV // (3 * CVB)

    def body(p_ref, w_ref, bias_ref, z_ref):
        xv, bv, cv = p_ref[:, 0:CVB], p_ref[:, CVB:2 * CVB], p_ref[:, 2 * CVB:3 * CVB]
        z_ref[...] = (bv * _conv(cv * xv, w_ref, bias_ref)).astype(BF)

    return pl.pallas_call(
        body, name=name, grid=(CONV // CVB,),
        in_specs=[pl.BlockSpec((T, 3 * CVB), lambda j: (0, o0 + j)), pl.BlockSpec((3, CVB), lambda j: (0, j)),
                  pl.BlockSpec((1, CVB), lambda j: (0, j))],
        out_specs=pl.BlockSpec((T, CVB), lambda j: (0, j)),
        out_shape=jax.ShapeDtypeStruct((T, CONV), BF),
        compiler_params=pltpu.CompilerParams(dimension_semantics=("parallel",)),
    )(p, cw, cb)


def out_proj_merge(o, wao, z, wco, p, *, name, tm=512):
    kin = o.shape[1]

    def body(o_ref, wa_ref, z_ref, wc_ref, ga_ref, gc_ref, ya_ref, yc_ref, m_ref):
        ya = jnp.dot(o_ref[...], wa_ref[...], preferred_element_type=F32)
        yc = jnp.dot(z_ref[...], wc_ref[...], preferred_element_type=F32)
        ya_ref[...] = ya
        yc_ref[...] = yc
        m_ref[...] = (jax.nn.sigmoid(ga_ref[...]) * ya + jax.nn.sigmoid(gc_ref[...]) * yc).astype(BF)

    blk = pl.BlockSpec((tm, D), lambda i: (i, 0))
    act = pl.BlockSpec((tm, kin), lambda i: (i, 0))
    wsp = pl.BlockSpec((kin, D), lambda i: (0, 0))
    sh = jax.ShapeDtypeStruct((T, D), F32)
    return pl.pallas_call(
        body, name=name, grid=(T // tm,),
        in_specs=[act, wsp, act, wsp, pl.BlockSpec((tm, D), lambda i: (i, O_GA // D)),
                  pl.BlockSpec((tm, D), lambda i: (i, O_GC // D))],
        out_specs=[blk, blk, blk], out_shape=[sh, sh, jax.ShapeDtypeStruct((T, D), BF)],
        compiler_params=pltpu.CompilerParams(dimension_semantics=("parallel",)),
    )(o, wao, z, wco, p, p)


CONV_HALO = 8
CONV_ROWS = 256


def _row_chunks(n, chunk, carry):
    carry = chunk(0, True, False, carry)
    carry = lax.fori_loop(1, n // CONV_ROWS - 1, lambda c, a: chunk(c * CONV_ROWS, False, False, a), carry)
    return chunk(n - CONV_ROWS, False, True, carry)


def _ext_rows(ref, r0, first, last):
    n, w = ref.shape
    zero = jnp.zeros((CONV_HALO, w), ref.dtype)
    if first:
        return jnp.concatenate([zero, ref[0:CONV_ROWS + CONV_HALO, :]], axis=0)
    if last:
        return jnp.concatenate([ref[n - CONV_ROWS - CONV_HALO:n, :], zero], axis=0)
    return ref[pl.ds(pl.multiple_of(r0 - CONV_HALO, 8), CONV_ROWS + 2 * CONV_HALO), :]


def _center_rows(r0, first, last):
    return slice(r0, r0 + CONV_ROWS) if (first or last) else pl.ds(pl.multiple_of(r0, 8), CONV_ROWS)


def _roll_dn(x):
    return pltpu.roll(x, 1, axis=0)


def _roll_up(x):
    return pltpu.roll(x, x.shape[0] - 1, axis=0)


_CTR = slice(CONV_HALO, CONV_HALO + CONV_ROWS)


def ffn_act(u0, cw, cb, *, name, tc=256):
    nb = DFF // tc

    def body(u_ref, wg_ref, wv_ref, bg_ref, bv_ref, f_ref):
        wg = [wg_ref[k:k + 1, :] for k in range(3)]
        wv = [wv_ref[k:k + 1, :] for k in range(3)]
        bg, bv = bg_ref[...], bv_ref[...]

        def chunk(r0, first, last, carry):
            xg, xv = _ext_rows(u_ref.at[0], r0, first, last), _ext_rows(u_ref.at[1], r0, first, last)
            ug = bg + _roll_dn(xg) * wg[0] + xg * wg[1] + _roll_up(xg) * wg[2]
            uv = bv + _roll_dn(xv) * wv[0] + xv * wv[1] + _roll_up(xv) * wv[2]
            f_ref[_center_rows(r0, first, last), :] = (ug * jax.nn.sigmoid(ug) * uv)[_CTR].astype(BF)
            return carry

        _row_chunks(T, chunk, 0)

    return pl.pallas_call(
        body, name=name, grid=(nb,),
        in_specs=[pl.BlockSpec((2, T, tc), lambda j: (0, 0, j)),
                  pl.BlockSpec((3, tc), lambda j: (0, j)), pl.BlockSpec((3, tc), lambda j: (0, nb + j)),
                  pl.BlockSpec((1, tc), lambda j: (0, j)), pl.BlockSpec((1, tc), lambda j: (0, nb + j))],
        out_specs=pl.BlockSpec((T, tc), lambda j: (0, j)),
        out_shape=jax.ShapeDtypeStruct((T, DFF), BF),
        compiler_params=pltpu.CompilerParams(dimension_semantics=("parallel",)),
    )(u0, cw, cw, cb, cb)


def rows_call(lead, ins, in_specs, out_shape, out_specs, fn, *, name, R, tm):
    tb, a_stack, tk = lead.get("tb", False), lead.get("a_stack", False), lead["tk"]
    K = 2 * lead["a"].shape[2] if a_stack else lead["a"].shape[1]
    nk = K // tk
    deps = [] if lead.get("dep") is None else [lead["dep"]]
    n_in = len(ins)

    def body(a_ref, b_ref, *refs):
        refs = refs[len(deps):]
        in_refs, out_refs, acc = refs[:n_in], refs[n_in:-1], refs[-1]
        i, k = pl.program_id(0), pl.program_id(1)
        part = lax.dot_general(a_ref[...].astype(BF), b_ref[...].astype(BF),
                               (((1,), (1 if tb else 0,)), ((), ())), preferred_element_type=F32)
        if nk == 1:
            fn(i, part, in_refs, out_refs)
            return

        @pl.when(k == 0)
        def _():
            acc[...] = part

        @pl.when(k > 0)
        def _():
            acc[...] += part

        @pl.when(k == nk - 1)
        def _():
            fn(i, acc[...], in_refs, out_refs)

    if a_stack:
        nhb = K // 2 // tk
        a_spec = pl.BlockSpec((None, tm, tk), lambda i, k: (k // nhb, i, k % nhb))
    else:
        a_spec = pl.BlockSpec((tm, tk), lambda i, k: (i, k))
    b_spec = pl.BlockSpec((D, tk), lambda i, k: (0, k)) if tb else pl.BlockSpec((tk, D), lambda i, k: (k, 0))
    return pl.pallas_call(
        body, name=name, grid=(R // tm, nk),
        in_specs=[a_spec, b_spec] + [pl.BlockSpec(memory_space=pl.ANY)] * len(deps) + list(in_specs),
        out_specs=out_specs, out_shape=out_shape,
        scratch_shapes=[pltpu.VMEM((tm, D) if nk > 1 else (8, 128), F32)],
        compiler_params=pltpu.CompilerParams(dimension_semantics=("arbitrary", "arbitrary")),
    )(lead["a"], lead["b"], *deps, *ins)


def _rblk(tm, w=D, col=0):
    return pl.BlockSpec((tm, w), lambda i, k: (i, col))


def _rrow(w=D):
    return pl.BlockSpec((1, w), lambda i, k: (0, 0))


def down_final(f, wdn, x1, g2, fg, tgt, *, name, tm=512):
    def fn(i, d, in_refs, out_refs):
        x1_ref, g2_ref, fg_ref, t_ref = in_refs
        d_ref, dx_ref, dd_ref, dfg_ref, loss_ref = out_refs
        d_ref[...] = d
        xv = x1_ref[...] + g2_ref[...] * d
        r = lax.rsqrt(jnp.mean(xv * xv, axis=-1, keepdims=True) + EPS)
        xh = xv * r
        diff = xh * fg_ref[...] - t_ref[...]
        part = 0.5 * jnp.sum(jnp.mean(diff * diff, axis=-1, keepdims=True), axis=0, keepdims=True)
        dy = diff * (1.0 / D)
        a = dy * fg_ref[...]
        dx = r * (a - xh * jnp.mean(a * xh, axis=-1, keepdims=True))
        dx_ref[...] = dx
        dd_ref[...] = (dx * g2_ref[...]).astype(BF)
        dfg = jnp.sum(dy * xh, axis=0, keepdims=True)

        @pl.when(i == 0)
        def _():
            dfg_ref[...] = dfg
            loss_ref[...] = jnp.broadcast_to(part, (1, 128))

        @pl.when(i > 0)
        def _():
            dfg_ref[...] += dfg
            loss_ref[...] += jnp.broadcast_to(part, (1, 128))

    blk = _rblk(tm)
    return rows_call(
        dict(a=f, b=wdn, tk=DFF), [x1, g2, fg, tgt], [blk, _rrow(), _rrow(), blk],
        [jax.ShapeDtypeStruct((T, D), F32), jax.ShapeDtypeStruct((T, D), F32), jax.ShapeDtypeStruct((T, D), BF),
         jax.ShapeDtypeStruct((1, D), F32), jax.ShapeDtypeStruct((1, 128), F32)],
        [blk, blk, blk, _rrow(), _rrow(128)], fn, name=name, R=T, tm=tm)


def oproj_resid(merged, wo, x, gate, g, sc, sh, *, name, tm=512):
    def fn(i, a, in_refs, out_refs):
        x_ref, gate_ref, g_ref, sc_ref, sh_ref = in_refs
        a_ref, x1_ref, h_ref = out_refs
        a_ref[...] = a
        xv = x_ref[...] + gate_ref[...] * a
        x1_ref[...] = xv
        r = lax.rsqrt(jnp.mean(xv * xv, axis=-1, keepdims=True) + EPS)
        h_ref[...] = ((xv * r * g_ref[...]) * (1.0 + sc_ref[...]) + sh_ref[...]).astype(BF)

    blk = _rblk(tm)
    return rows_call(
        dict(a=merged, b=wo, tk=D), [x, gate, g, sc, sh], [blk, _rrow(), _rrow(), _rrow(), _rrow()],
        [jax.ShapeDtypeStruct((T, D), F32), jax.ShapeDtypeStruct((T, D), F32), jax.ShapeDtypeStruct((T, D), BF)],
        [blk, blk, blk], fn, name=name, R=T, tm=tm)


def oproj_dx_gate_bwd(da, wo, p, ya, yc, wao, wco, *, name, tm=512):
    kin = wao.shape[0]

    def fn(i, dm, in_refs, out_refs):
        ga_ref, gc_ref, ya_ref, yc_ref, wa_ref, wc_ref = in_refs
        dya_ref, dyc_ref, dp_ref, do_ref, dz_ref = out_refs
        sa, sc_ = jax.nn.sigmoid(ga_ref[...]), jax.nn.sigmoid(gc_ref[...])
        dya, dyc = (dm * sa).astype(BF), (dm * sc_).astype(BF)
        dya_ref[...] = dya
        dyc_ref[...] = dyc
        dp_ref[:, 0:D] = (dm * ya_ref[...] * (sa * (1.0 - sa))).astype(BF)
        dp_ref[:, D:2 * D] = (dm * yc_ref[...] * (sc_ * (1.0 - sc_))).astype(BF)
        nt = (((1,), (1,)), ((), ()))
        do_ref[...] = lax.dot_general(dya, wa_ref[...], nt, preferred_element_type=F32).astype(BF)
        dz_ref[...] = lax.dot_general(dyc, wc_ref[...], nt, preferred_element_type=F32)

    blk = _rblk(tm)
    sh = jax.ShapeDtypeStruct((T, D), BF)
    wsp = pl.BlockSpec((kin, D), lambda i, k: (0, 0))
    return rows_call(
        dict(a=da, b=wo, tb=True, tk=D), [p, p, ya, yc, wao, wco],
        [_rblk(tm, D, O_GA // D), _rblk(tm, D, O_GC // D), blk, blk, wsp, wsp],
        [sh, sh, jax.ShapeDtypeStruct((T, NIN), BF), jax.ShapeDtypeStruct((T, kin), BF), jax.ShapeDtypeStruct((T, kin), F32)],
        [blk, blk, _rblk(tm, 2 * D), _rblk(tm, kin), _rblk(tm, kin)], fn, name=name, R=T, tm=tm)


def normmod_bwd(x, dh, g, sc, dres, gsrc, gate, *, name, tm=512):
    R = x.shape[0]
    tm = min(tm, R)
    has_res = dres is not None
    fused = isinstance(dh, dict)
    if fused:
        tb, a_stack, tk = dh.get("tb", False), dh.get("a_stack", False), dh["tk"]
        K = 2 * dh["a"].shape[2] if a_stack else dh["a"].shape[1]
        nk = K // tk
        deps = [] if dh.get("dep") is None else [dh["dep"]]
        n_dh = 2 + len(deps)
    else:
        nk, n_dh = 1, 1

    def elementwise(i, dhv, x_ref, g_ref, sc_ref, res_refs, out_refs):
        xv = x_ref[...]
        r = lax.rsqrt(jnp.mean(xv * xv, axis=-1, keepdims=True) + EPS)
        xh = xv * r
        n = xh * g_ref[...]
        dn = dhv * (1.0 + sc_ref[...])
        a = dn * g_ref[...]
        rows = [jnp.sum(dhv, axis=0, keepdims=True), jnp.sum(dhv * n, axis=0, keepdims=True),
                jnp.sum(dn * xh, axis=0, keepdims=True)]
        if has_res:
            dres_ref, gsrc_ref, gate_ref = res_refs
            dx_ref, dxg_ref, st_ref = out_refs
            dr = dres_ref[...]
            dx = dr + r * (a - xh * jnp.mean(a * xh, axis=-1, keepdims=True))
            dx_ref[...] = dx
            dxg_ref[...] = (dx * gate_ref[...]).astype(BF)
            rows.append(jnp.sum(dr * gsrc_ref[...], axis=0, keepdims=True))
        else:
            st_ref, = out_refs
            rows.append(jnp.zeros((1, D), F32))

        @pl.when(i == 0)
        def _():
            for k, row in enumerate(rows):
                st_ref[k:k + 1, :] = row

        @pl.when(i > 0)
        def _():
            for k, row in enumerate(rows):
                st_ref[k:k + 1, :] += row

    def body(*refs):
        x_ref, dh_refs, g_ref, sc_ref = refs[0], refs[1:1 + n_dh], refs[1 + n_dh], refs[2 + n_dh]
        rest = refs[3 + n_dh:]
        res_refs, rest = (rest[:3], rest[3:]) if has_res else ((), rest)
        out_refs = rest[:3] if has_res else rest[:1]
        i = pl.program_id(0)
        if not fused:
            elementwise(i, dh_refs[0][...], x_ref, g_ref, sc_ref, res_refs, out_refs)
            return
        acc = rest[-1]
        k = pl.program_id(1)
        part = lax.dot_general(dh_refs[0][...].astype(BF), dh_refs[1][...].astype(BF),
                               (((1,), (1 if tb else 0,)), ((), ())), preferred_element_type=F32)

        @pl.when(k == 0)
        def _():
            acc[...] = part

        @pl.when(k > 0)
        def _():
            acc[...] += part

        @pl.when(k == nk - 1)
        def _():
            elementwise(i, acc[...], x_ref, g_ref, sc_ref, res_refs, out_refs)

    rowb = lambda w: pl.BlockSpec((1, w), lambda i, *k: (0, 0))
    blk = pl.BlockSpec((tm, D), lambda i, *k: (i, 0))
    st_spec = pl.BlockSpec((4, D), lambda i, *k: (0, 0))
    st_shape = jax.ShapeDtypeStruct((4, D), F32)
    if fused:
        if a_stack:
            nhb = K // 2 // tk
            a_spec = pl.BlockSpec((None, tm, tk), lambda i, k: (k // nhb, i, k % nhb))
        else:
            a_spec = pl.BlockSpec((tm, tk), lambda i, k: (i, k))
        b_spec = pl.BlockSpec((D, tk), lambda i, k: (0, k)) if tb else pl.BlockSpec((tk, D), lambda i, k: (k, 0))
        dh_specs = [a_spec, b_spec] + [pl.BlockSpec(memory_space=pl.ANY)] * len(deps)
        dh_args = [dh["a"], dh["b"]] + deps
        grid, sem = (R // tm, nk), ("arbitrary", "arbitrary")
        scratch = [pltpu.VMEM((tm, D), F32)]
    else:
        dh_specs, dh_args, grid, sem, scratch = [blk], [dh], (R // tm,), ("arbitrary",), []
    cp = pltpu.CompilerParams(dimension_semantics=sem)
    if has_res:
        return pl.pallas_call(
            body, name=name, grid=grid, in_specs=[blk] + dh_specs + [rowb(D), rowb(D), blk, blk, rowb(D)],
            out_specs=[blk, blk, st_spec], scratch_shapes=scratch,
            out_shape=[jax.ShapeDtypeStruct((R, D), F32), jax.ShapeDtypeStruct((R, D), BF), st_shape],
            compiler_params=cp,
        )(x, *dh_args, g, sc, dres, gsrc, gate)
    return pl.pallas_call(
        body, name=name, grid=grid, in_specs=[blk] + dh_specs + [rowb(D), rowb(D)],
        out_specs=st_spec, out_shape=st_shape, scratch_shapes=scratch, compiler_params=cp,
    )(x, *dh_args, g, sc)


def ffn_act_bwd(u0, df, cw, cb, *, name, tc=128):
    nb = DFF // tc

    def body(u_ref, df_ref, wg_ref, wv_ref, bg_ref, bv_ref, du_ref, dw_ref, db_ref):
        wg = [wg_ref[k:k + 1, :] for k in range(3)]
        wv = [wv_ref[k:k + 1, :] for k in range(3)]
        bg, bv = bg_ref[...], bv_ref[...]

        def chunk(r0, first, last, acc):
            xg, xv = _ext_rows(u_ref.at[0], r0, first, last), _ext_rows(u_ref.at[1], r0, first, last)
            dfe = _ext_rows(df_ref, r0, first, last)
            xg_d, xg_u, xv_d, xv_u = _roll_dn(xg), _roll_up(xg), _roll_dn(xv), _roll_up(xv)
            ug = bg + xg_d * wg[0] + xg * wg[1] + xg_u * wg[2]
            uv = bv + xv_d * wv[0] + xv * wv[1] + xv_u * wv[2]
            sig = jax.nn.sigmoid(ug)
            dug = dfe * uv * (sig * (1.0 + ug * (1.0 - sig)))
            duv = dfe * (ug * sig)
            rows = _center_rows(r0, first, last)
            du_ref[0, rows, :] = (_roll_up(dug) * wg[0] + dug * wg[1] + _roll_dn(dug) * wg[2])[_CTR].astype(BF)
            du_ref[1, rows, :] = (_roll_up(duv) * wv[0] + duv * wv[1] + _roll_dn(duv) * wv[2])[_CTR].astype(BF)
            terms = [dug * xg_d, dug * xg, dug * xg_u, dug, duv * xv_d, duv * xv, duv * xv_u, duv]
            return tuple(a + jnp.sum(t[_CTR], axis=0, keepdims=True) for a, t in zip(acc, terms))

        acc = _row_chunks(T, chunk, tuple(jnp.zeros((1, tc), F32) for _ in range(8)))
        for k in range(3):
            dw_ref[0, k:k + 1, :] = acc[k]
            dw_ref[1, k:k + 1, :] = acc[4 + k]
        db_ref[0] = acc[3]
        db_ref[1] = acc[7]

    lo = lambda r: pl.BlockSpec((r, tc), lambda j: (0, j))
    hi = lambda r: pl.BlockSpec((r, tc), lambda j: (0, nb + j))
    st = lambda r: pl.BlockSpec((2, r, tc), lambda j: (0, 0, j))
    return pl.pallas_call(
        body, name=name, grid=(nb,),
        in_specs=[st(T), lo(T), lo(3), hi(3), lo(1), hi(1)],
        out_specs=[st(T), st(3), st(1)],
        out_shape=[jax.ShapeDtypeStruct((2, T, DFF), BF), jax.ShapeDtypeStruct((2, 3, DFF), F32),
                   jax.ShapeDtypeStruct((2, 1, DFF), F32)],
        compiler_params=pltpu.CompilerParams(dimension_semantics=("parallel",)),
    )(u0, df, cw, cw, cb, cb)


def convz_bwd(p, dz, cw, cb, dp, *, name):
    o0 = O_CV // (3 * CVB)

    def body(p_ref, dz_ref, w_ref, bias_ref, dp_in, dp_ref, dw_ref, dbias_ref):
        xv, bv, cv = p_ref[:, 0:CVB], p_ref[:, CVB:2 * CVB], p_ref[:, 2 * CVB:3 * CVB]
        ci = cv * xv
        dwc = _conv(ci, w_ref, bias_ref)
        dzv = dz_ref[...]
        ddw = dzv * bv
        dci = _conv_t(ddw, w_ref)
        dp_ref[:, 0:CVB] = (dci * cv).astype(BF)
        dp_ref[:, CVB:2 * CVB] = (dzv * dwc).astype(BF)
        dp_ref[:, 2 * CVB:3 * CVB] = (dci * xv).astype(BF)
        _conv_wgrad(dw_ref, ddw, ci)
        dbias_ref[...] = jnp.sum(ddw, axis=0, keepdims=True)

    own = lambda r: pl.BlockSpec((r, CVB), lambda j: (0, j))
    return pl.pallas_call(
        body, name=name, grid=(CONV // CVB,),
        in_specs=[pl.BlockSpec((T, 3 * CVB), lambda j: (0, o0 + j)), own(T), own(3), own(1),
                  pl.BlockSpec(memory_space=pl.ANY)],
        out_specs=[pl.BlockSpec((T, 3 * CVB), lambda j: (0, o0 + j)), own(3), own(1)],
        out_shape=[jax.ShapeDtypeStruct((T, NIN), BF), jax.ShapeDtypeStruct((3, CONV), F32),
                   jax.ShapeDtypeStruct((1, CONV), F32)],
        input_output_aliases={4: 0},
        compiler_params=pltpu.CompilerParams(dimension_semantics=("parallel",)),
    )(p, dz, cw, cb, dp)


def attn_bwd(q, k, v, do, o, lse, dep, *, name, tq=1024, kc=768):
    NKC, KC = TKV // kc, kc
    deps = [] if dep is None else [dep]

    def body(q_ref, k_ref, v_ref, do_ref, o_ref, lse_ref, *rest):
        dq_ref, dk_ref, dv_ref = rest[len(deps):]
        h, i = pl.program_id(0), pl.program_id(1)

        @pl.when(i == 0)
        def _():
            dk_ref[...] = jnp.zeros_like(dk_ref)

        @pl.when((i == 0) & (h % 2 == 0))
        def _():
            dv_ref[...] = jnp.zeros_like(dv_ref)

        qv = q_ref[...]
        dom = jnp.where(_head_mask(h), do_ref[...], jnp.zeros_like(do_ref[...]))
        delta = jnp.sum(dom.astype(F32) * o_ref[...].astype(F32), axis=-1, keepdims=True)
        lse = lse_ref[:, 0:1]
        dq = jnp.zeros((tq, HP), F32)
        for c in range(NKC):
            cols = slice(c * KC, (c + 1) * KC)
            s = lax.dot_general(qv, k_ref[cols, :], (((1,), (1,)), ((), ())),
                                preferred_element_type=F32) * (SCALE * LOG2E)
            pr = jnp.exp2(s - lse)
            dp = lax.dot_general(dom, v_ref[cols, :], (((1,), (1,)), ((), ())), preferred_element_type=F32)
            ds = (pr * (dp - delta) * SCALE).astype(BF)
            dq = dq + jnp.dot(ds, k_ref[cols, :], preferred_element_type=F32)
            dk_ref[cols, :] += lax.dot_general(ds, qv, (((0,), (0,)), ((), ())), preferred_element_type=F32)
            dv_ref[cols, :] += lax.dot_general(pr.astype(BF), dom, (((0,), (0,)), ((), ())), preferred_element_type=F32)
        dq_ref[...] = dq

    return pl.pallas_call(
        body, name=name, grid=(NH, T // tq),
        in_specs=[pl.BlockSpec((tq, HP), lambda h, i: (i, h)), pl.BlockSpec((TKV, HP), lambda h, i: (0, h)),
                  pl.BlockSpec((TKV, 2 * DV), lambda h, i: (0, h // 2)), pl.BlockSpec((tq, 2 * DV), lambda h, i: (i, h // 2)),
                  pl.BlockSpec((tq, 2 * DV), lambda h, i: (i, h // 2)), pl.BlockSpec((tq, HP), lambda h, i: (i, h)),
                  *([pl.BlockSpec(memory_space=pl.ANY)] * len(deps))],
        out_specs=[pl.BlockSpec((tq, HP), lambda h, i: (i, h)), pl.BlockSpec((TKV, HP), lambda h, i: (0, h)),
                   pl.BlockSpec((TKV, 2 * DV), lambda h, i: (0, h // 2))],
        out_shape=[jax.ShapeDtypeStruct((T, NH * HP), F32), jax.ShapeDtypeStruct((TKV, NH * HP), F32),
                   jax.ShapeDtypeStruct((TKV, NH * DV), F32)],
        compiler_params=pltpu.CompilerParams(dimension_semantics=("arbitrary", "arbitrary")),
    )(q, k, v, do, o, lse, *deps)


def qprep_bwd(p, dq, qg, wq2, cq_t, sq_t, dp, *, name, tm=256):
    qcol = O_Q // 512

    def body(p_ref, dq_ref, g_ref, w_ref, c_ref, s_ref, dp_in, dp_ref, dq2_ref, dg_ref):
        i = pl.program_id(0)
        dqv = dq_ref[...]
        cc = jnp.concatenate([c_ref[...]] * NH, axis=1)
        ss = jnp.concatenate([s_ref[...]] * NH, axis=1)
        dq2 = jnp.concatenate([dqv * cc, dqv * ss], axis=1).astype(BF)
        dq2_ref[...] = dq2
        dcq = lax.dot_general(dq2, w_ref[...], (((1,), (1,)), ((), ())), preferred_element_type=F32)
        pq = p_ref[...]
        r = lax.rsqrt(jnp.sum(pq * pq, axis=-1, keepdims=True) * (1.0 / QL) + EPS)
        xh = pq * r
        a = dcq * g_ref[...]
        dp_ref[...] = (r * (a - xh * (jnp.sum(a * xh, axis=-1, keepdims=True) * (1.0 / QL)))).astype(BF)
        dg = jnp.sum(dcq * xh, axis=0, keepdims=True)

        @pl.when(i == 0)
        def _():
            dg_ref[...] = dg

        @pl.when(i > 0)
        def _():
            dg_ref[...] += dg

    return pl.pallas_call(
        body, name=name, grid=(T // tm,),
        in_specs=[pl.BlockSpec((tm, 512), lambda i: (i, qcol)), pl.BlockSpec((tm, NH * HP), lambda i: (i, 0)), _row(512),
                  pl.BlockSpec((512, 2 * NH * HP), lambda i: (0, 0)),
                  pl.BlockSpec((tm, HP), lambda i: (i, 0)), pl.BlockSpec((tm, HP), lambda i: (i, 0)),
                  pl.BlockSpec(memory_space=pl.ANY)],
        out_specs=[pl.BlockSpec((tm, 512), lambda i: (i, qcol)), pl.BlockSpec((tm, 2 * NH * HP), lambda i: (i, 0)), _row(512)],
        out_shape=[jax.ShapeDtypeStruct((T, NIN), BF), jax.ShapeDtypeStruct((T, 2 * NH * HP), BF),
                   jax.ShapeDtypeStruct((1, 512), F32)],
        input_output_aliases={6: 0},
        compiler_params=pltpu.CompilerParams(dimension_semantics=("arbitrary",)),
    )(p, dq, qg, wq2, cq_t, sq_t, dp)


def kvprep_bwd(pc, p, dk, dv, kvg, wkv2, ck, sk, dp, *, name, tm=256):
    assert tm == TC
    nb = TKV // tm
    kvcol = O_KV // 512

    def body(pc_ref, p_ref, dk_ref, dv_ref, g_ref, w_ref, ck_ref, sk_ref, dp_in, dp_ref, dpc_ref, dkv2_ref, dg_ref):
        i = pl.program_id(0)
        t = jnp.where(i == NLAT, pc_ref[...], p_ref[...])
        pk = t[:, :KVL]
        r = lax.rsqrt(jnp.mean(pk * pk, axis=-1, keepdims=True) + EPS)
        xh = pk * r
        dkv = dk_ref[...]
        dkv2 = jnp.concatenate([dkv, dv_ref[...]], axis=1).astype(BF)
        dkv2_ref[...] = dkv2
        dckv = lax.dot_general(dkv2, w_ref[...], (((1,), (1,)), ((), ())), preferred_element_type=F32)
        a = dckv * g_ref[...]
        dpk = r * (a - xh * jnp.mean(a * xh, axis=-1, keepdims=True))
        dkr = dkv[:, 0:HP]
        for hh in range(1, NH):
            dkr = dkr + dkv[:, hh * HP:(hh + 1) * HP]
        res = jnp.concatenate([dpk, dkr * ck_ref[...], dkr * sk_ref[...]], axis=1).astype(BF)
        dg = jnp.sum(dckv * xh, axis=0, keepdims=True)

        @pl.when(i == 0)
        def _():
            dg_ref[...] = dg

        @pl.when(i > 0)
        def _():
            dg_ref[...] += dg

        @pl.when(i < NLAT)
        def _():
            dp_ref[...] = res

        @pl.when(i == NLAT)
        def _():
            dpc_ref[...] = res

    rb = lambda w: pl.BlockSpec((tm, w), lambda i: (i, 0))
    return pl.pallas_call(
        body, name=name, grid=(nb,),
        in_specs=[pl.BlockSpec((tm, 512), lambda i: (0, 0)),
                  pl.BlockSpec((tm, 512), lambda i: (jnp.minimum(i, NLAT - 1), kvcol)),
                  rb(NH * HP), rb(NH * DV), _row(KVL), pl.BlockSpec((KVL, NH * HP + NH * DV), lambda i: (0, 0)),
                  rb(HP), rb(HP), pl.BlockSpec(memory_space=pl.ANY)],
        out_specs=[pl.BlockSpec((tm, 512), lambda i: (jnp.minimum(i, NLAT - 1), kvcol)),
                   pl.BlockSpec((tm, 512), lambda i: (0, 0)), rb(NH * HP + NH * DV), _row(KVL)],
        out_shape=[jax.ShapeDtypeStruct((T, NIN), BF), jax.ShapeDtypeStruct((TC, 512), BF),
                   jax.ShapeDtypeStruct((TKV, NH * HP + NH * DV), BF), jax.ShapeDtypeStruct((1, KVL), F32)],
        input_output_aliases={8: 0},
        compiler_params=pltpu.CompilerParams(dimension_semantics=("arbitrary",)),
    )(pc, p, dk, dv, kvg, wkv2, ck, sk, dp)


def _pieces(src, width, n):
    out, c = [], src
    while c < src + width:
        k = c // n
        w = min(src + width, (k + 1) * n) - c
        out.append((k, c - k * n, c - src, w))
        c += w
    return out


def _win_moves():
    mv = [(2208, 1024, O_GA), (3232, 1024, O_GC), (0, KVL, O_KV), (256, DR, O_KV + KVL + DN), (288, QL, O_Q)]
    mv += [(256 + _swap_start(g), 8, O_KV + KVL + HP + DN + 8 * g) for g in range(4)]
    for j in range(CONV // CVB):
        base = O_CV + 3 * CVB * j
        mv += [(672 + CVB * j, CVB, base), (1184 + CVB * j, CVB, base + CVB), (1696 + CVB * j, CVB, base + 2 * CVB)]
    return mv


_WIN_ZERO = [(O_KV + KVL, DN), (O_KV + KVL + DN + DR, HP - DN - DR), (O_KV + KVL + HP, DN),
             (O_KV + KVL + HP + DN + DR, HP - DN - DR), (O_Q + QL, 512 - QL)]


def build_win(g, *, name, tm=256):
    def body(g_ref, o_ref):
        for src, w, dst in _win_moves():
            for k, a, off, pw in _pieces(src, w, SH_IN):
                o_ref[:, dst + off:dst + off + pw] = g_ref[k, :, a:a + pw]
        for c0, w in _WIN_ZERO:
            o_ref[:, c0:c0 + w] = jnp.zeros((tm, w), o_ref.dtype)

    return pl.pallas_call(
        body, name=name, grid=(D // tm,), in_specs=[pl.BlockSpec((NDEV, tm, SH_IN), lambda i: (0, i, 0))],
        out_specs=pl.BlockSpec((tm, NIN), lambda i: (i, 0)), out_shape=jax.ShapeDtypeStruct((D, NIN), g.dtype),
        compiler_params=pltpu.CompilerParams(dimension_semantics=("parallel",)),
    )(g)


def shard_win_grad(dwt, dwct, *, name, tc=256):
    def body(dw_ref, dwc_ref, o_ref, kvs):
        kvs[...] = dw_ref[O_KV:O_KV + 512, :] + dwc_ref[...]

        def src(row, w):
            if O_KV <= row < O_KV + 512:
                return kvs[row - O_KV:row - O_KV + w, :]
            return dw_ref[row:row + w, :]

        for s, w, dst in _win_moves():
            if w == 8 or s == 256:
                continue
            for k, a, off, pw in _pieces(s, w, SH_IN):
                o_ref[k, a:a + pw, :] = src(dst + off, pw).astype(o_ref.dtype)
        for g in range(4):
            val = src(O_KV + KVL + DN + 8 * g, 8) + src(O_KV + KVL + HP + DN + _swap_start(g), 8)
            o_ref[0, 256 + 8 * g:256 + 8 * g + 8, :] = val.astype(o_ref.dtype)

    return pl.pallas_call(
        body, name=name, grid=(D // tc,),
        in_specs=[pl.BlockSpec((NIN, tc), lambda j: (0, j)), pl.BlockSpec((512, tc), lambda j: (0, j))],
        out_specs=pl.BlockSpec((NDEV, SH_IN, tc), lambda j: (0, 0, j)),
        out_shape=jax.ShapeDtypeStruct((NDEV, SH_IN, D), BF),
        scratch_shapes=[pltpu.VMEM((512, tc), F32)],
        compiler_params=pltpu.CompilerParams(dimension_semantics=("parallel",)),
    )(dwt, dwct)


def build_wq_wkv(gq, gkv, *, name):
    def body(gq_ref, gkv_ref, q_ref, kv_ref):
        q_ref[...] = jnp.zeros_like(q_ref)
        kv_ref[...] = jnp.zeros_like(kv_ref)
        for h in range(NH):
            q_ref[0:QL, h * HP:h * HP + DN + DR] = gq_ref[h]
            for g in range(4):
                c0 = NH * HP + h * HP + DN + 8 * g
                q_ref[0:QL, c0:c0 + 8] = gq_ref[h, :, DN + _swap_start(g):DN + _swap_start(g) + 8]
            kv_ref[:, h * HP:h * HP + DN] = gkv_ref[h, :, 0:DN]
            kv_ref[:, NH * HP + h * DV:NH * HP + (h + 1) * DV] = gkv_ref[h, :, DN:DN + DV]

    vm = pl.BlockSpec(memory_space=pltpu.VMEM)
    return pl.pallas_call(
        body, name=name, in_specs=[vm, vm], out_specs=[vm, vm],
        out_shape=[jax.ShapeDtypeStruct((512, 2 * NH * HP), gq.dtype), jax.ShapeDtypeStruct((KVL, NH * HP + NH * DV), gq.dtype)],
    )(gq, gkv)


def shard_wq_wkv_grad(dwq2, dwkv2, *, name):
    def body(q_ref, kv_ref, gq_ref, gkv_ref):
        for h in range(NH):
            gq_ref[h, :, 0:DN] = q_ref[0:QL, h * HP:h * HP + DN].astype(BF)
            for g in range(4):
                a = q_ref[0:QL, h * HP + DN + 8 * g:h * HP + DN + 8 * g + 8]
                c0 = NH * HP + h * HP + DN + _swap_start(g)
                gq_ref[h, :, DN + 8 * g:DN + 8 * g + 8] = (a + q_ref[0:QL, c0:c0 + 8]).astype(BF)
            gkv_ref[h, :, 0:DN] = kv_ref[:, h * HP:h * HP + DN].astype(BF)
            gkv_ref[h, :, DN:DN + DV] = kv_ref[:, NH * HP + h * DV:NH * HP + (h + 1) * DV].astype(BF)

    vm = pl.BlockSpec(memory_space=pltpu.VMEM)
    return pl.pallas_call(
        body, name=name, in_specs=[vm, vm], out_specs=[vm, vm],
        out_shape=[jax.ShapeDtypeStruct((NDEV, QL, (DN + DR)), BF), jax.ShapeDtypeStruct((NDEV, KVL, DN + DV), BF)],
    )(dwq2, dwkv2)


def unshard_cols(g, *, name, tm=256):
    _, K, n = g.shape
    tm = _pick(K, tm, 16)

    def body(g_ref, o_ref):
        for k in range(NDEV):
            o_ref[:, k * n:(k + 1) * n] = g_ref[k]

    return pl.pallas_call(
        body, name=name, grid=(K // tm,), in_specs=[pl.BlockSpec((NDEV, tm, n), lambda i: (0, i, 0))],
        out_specs=pl.BlockSpec((tm, NDEV * n), lambda i: (i, 0)), out_shape=jax.ShapeDtypeStruct((K, NDEV * n), g.dtype),
        compiler_params=pltpu.CompilerParams(dimension_semantics=("parallel",)),
    )(g)


def shard_cols(w, *, name, tm=256):
    K, n8 = w.shape
    n = n8 // NDEV
    tm = _pick(K, tm, 16)

    def body(w_ref, o_ref):
        for k in range(NDEV):
            o_ref[k] = w_ref[:, k * n:(k + 1) * n]

    return pl.pallas_call(
        body, name=name, grid=(K // tm,), in_specs=[pl.BlockSpec((tm, n8), lambda i: (i, 0))],
        out_specs=pl.BlockSpec((NDEV, tm, n), lambda i: (0, i, 0)), out_shape=jax.ShapeDtypeStruct((NDEV, K, n), w.dtype),
        compiler_params=pltpu.CompilerParams(dimension_semantics=("parallel",)),
    )(w)


def _rope_tables():
    t = np.arange(T)
    row = (t // GRID_W).astype(np.float32)
    col = (t % GRID_W).astype(np.float32)
    axis_dim = DR // 2
    inv = (np.float32(ROPE_THETA) ** (-np.arange(0, axis_dim, 2, dtype=np.float32) / np.float32(axis_dim))).astype(np.float32)
    ar, ac = (row[:, None] * inv).astype(np.float32), (col[:, None] * inv).astype(np.float32)
    cosv = np.concatenate([np.cos(ar), np.cos(ar), np.cos(ac), np.cos(ac)], axis=1).astype(np.float32)
    sinv = np.concatenate([-np.sin(ar), np.sin(ar), -np.sin(ac), np.sin(ac)], axis=1).astype(np.float32)
    ck = np.zeros((TKV, HP), np.float32)
    sk = np.zeros((TKV, HP), np.float32)
    ck[T:, DN:DN + DR] = 1.0
    ck[:T, DN:DN + DR] = cosv
    sk[:T, DN:DN + DR] = sinv
    cq = np.zeros((T, HP), np.float32)
    cq[:, :DN] = 1.0
    cq[:, DN:DN + DR] = cosv
    return jnp.asarray(ck), jnp.asarray(sk), jnp.asarray(cq), jnp.asarray(sk[:T])


def _local_step(x, ctx, tgt, mod_lat, mod_ctx, n1g, qg, kvg, n2g, fg, conv_w, conv_b, ffn_w, ffn_b, get_w, put_g, dep0):
    sh1, sc1, g1, sh2, sc2, g2 = [mod_lat[:, i * D:(i + 1) * D] for i in range(6)]
    csh1, csc1 = mod_ctx[:, 0:D], mod_ctx[:, D:2 * D]
    ck, sk, cq_t, sq_t = _rope_tables()
    qg_p = jnp.pad(qg, ((0, 0), (0, 512 - QL)))

    hcat = normmod_cat(ctx, x, n1g, csc1, csh1, sc1, sh1, dep0, name="normmod1")
    win = get_w("in", hcat)
    p = mm(hcat, win, M=T, tn=768, name="in_proj")
    pc = mm(hcat, win, M=TC, N=512, a_off=(T, 0), b_off=(0, O_KV), name="in_proj_ctx")
    wq2, wkv2, wao, wco, wo = get_w("mid", p)
    kh, vh, ckv = kvprep(pc, p, kvg, wkv2, ck, sk, name="kvprep")
    qr, cq = qprep(p, qg_p, wq2, cq_t, sq_t, name="qprep")
    o, lse = attn_fwd(qr, kh, vh, name="attn_fwd")
    z = convz(p, conv_w, conv_b, name="convz")
    ya, yc, merged = out_proj_merge(o, wao, z, wco, p, name="attn_conv_out_gate_merge")
    a_out, x1, h2 = oproj_resid(merged, wo, x, g1, n2g, sc2, sh2, name="o_proj_resid_normmod2")
    wup, wdn = get_w("ffn", h2)
    u0 = mm(h2, wup, tb=True, o_stack=True, tn=1408, name="up_proj")
    f = ffn_act(u0, ffn_w, ffn_b, name="ffn_act")
    dn, dx2, dd, dfg, loss = down_final(f, wdn, x1, g2, fg, tgt, name="down_proj_final_loss")

    df = mm(dd, wdn, tb=True, tn=1408, name="down_proj_dx")
    dwdn = mm(f, dd, ta=True, out_dtype=BF, tm=1408, name="down_proj_dw")
    du0, dffn_w, dffn_b = ffn_act_bwd(u0, df, ffn_w, ffn_b, name="ffn_act_bwd")
    dwup = mm(du0, h2, ta=True, a_stack=True, out_dtype=BF, tm=1408, name="up_proj_dw")
    tok = put_g("ffn", dict(dwup=dwup, dwdn=dwdn))
    dx1, da, st2 = normmod_bwd(x1, dict(a=du0, b=wup, a_stack=True, tk=1408, dep=tok), n2g, sc2, dx2, dn, g1,
                               name="up_proj_dx_normmod2_bwd")

    dwo = mm(merged, da, ta=True, out_dtype=BF, tn=512, name="o_proj_dw")
    dya, dyc, dp, do, dz = oproj_dx_gate_bwd(da, wo, p, ya, yc, wao, wco, name="o_proj_dx_gate_merge_bwd")
    dwao = mm(o, dya, ta=True, out_dtype=BF, tn=512, name="attn_out_dw")
    dwco = mm(z, dyc, ta=True, out_dtype=BF, tn=512, name="conv_out_dw")
    tok = put_g("mid", dict(dwao=dwao, dwco=dwco, dwo=dwo))
    dp, dconv_w, dconv_b = convz_bwd(p, dz, conv_w, conv_b, dp, name="convz_bwd")
    dq, dk, dv = attn_bwd(qr, kh, vh, do, o, lse, tok, name="attn_bwd")
    dp, dq2, dqg = qprep_bwd(p, dq, qg_p, wq2, cq_t, sq_t, dp, name="qprep_bwd")
    dwq2 = mm(cq, dq2, ta=True, name="q_up_dw")
    dp, dpc, dkv2, dkvg = kvprep_bwd(pc, p, dk, dv, kvg, wkv2, ck, sk, dp, name="kvprep_bwd")
    dwkv2 = mm(ckv, dkv2, ta=True, name="kv_up_dw")
    tok = put_g("qkv", dict(dwq2=dwq2, dwkv2=dwkv2))

    dwin = mm(dp, hcat, ta=True, K=T, tm=768, dep=tok, name="in_proj_dw")
    dwin_c = mm(dpc, hcat, ta=True, K=TC, b_off=(T, 0), name="in_proj_ctx_dw")
    tok = put_g("in", dict(dwin=dwin, dwin_c=dwin_c))
    dhc = mm(dpc, win, tb=True, N=D, K=512, b_off=(0, O_KV), name="in_proj_ctx_dx")
    dx, _, st1 = normmod_bwd(x, dict(a=dp, b=win, tb=True, tk=1536, dep=tok), n1g, sc1, dx1, a_out, g1,
                             name="in_proj_dx_normmod1_bwd")
    stc = normmod_bwd(ctx, dhc, n1g, csc1, None, None, None, name="normmod1_ctx_bwd")

    zrow = jnp.zeros((1, D), F32)
    dmod_lat = jnp.concatenate([st1[0:1], st1[1:2], st1[3:4], st2[0:1], st2[1:2], st2[3:4]], axis=1)
    dmod_ctx = jnp.concatenate([stc[0:1], stc[1:2], zrow, zrow, zrow, zrow], axis=1)
    return dict(
        loss=loss, dx=dx, dmod_lat=dmod_lat, dmod_ctx=dmod_ctx,
        dn1g=st1[2:3] + stc[2:3], dqg=dqg, dkvg=dkvg, dn2g=st2[2:3], dfg=dfg,
        dconv_w=dconv_w, dconv_b=dconv_b, dffn_w=dffn_w, dffn_b=dffn_b)


def _me():
    x, y, c = lax.axis_index("x"), lax.axis_index("y"), lax.axis_index("c")
    return x, y, c, 4 * x + 2 * y + c


def _peer(x, y, c, k):
    px = 1 - x if k & 4 else x
    py = 1 - y if k & 2 else y
    pc = 1 - c if k & 1 else c
    return (px, py, pc), 4 * px + 2 * py + pc


def _exchange_tiles(src_of_peer, buf, send_sem, recv_sem):
    x, y, c, me = _me()
    for k in range(1, NDEV):
        dev, lin = _peer(x, y, c, k)
        pltpu.make_async_remote_copy(src_ref=src_of_peer(lin), dst_ref=buf.at[me], send_sem=send_sem, recv_sem=recv_sem,
                                     device_id=dev, device_id_type=MESH).start()
    seven = buf.at[pl.ds(0, NDEV - 1)]
    pltpu.make_async_remote_copy(src_ref=seven, dst_ref=seven, send_sem=send_sem, recv_sem=recv_sem,
                                 device_id=(x, y, c), device_id_type=MESH).wait()


def _silu(z):
    return z * jax.nn.sigmoid(z)


def ada_fwd(c, c_ctx, ffn_w, conv_w, w_shard, b_shard, deps, *, name):
    nsh = w_shard.shape[1]
    deps = [d for d in deps if d is not None]

    def body(c_ref, cc_ref, fw_ref, cw_ref, w_ref, b_ref, *rest):
        s_ref, m_ref, mine, res, sems = rest[len(deps):]
        x, y, c, me = _me()
        mine[0:1, :] = _silu(c_ref[...])
        mine[1:2, :] = _silu(cc_ref[...])
        mine[2:5, :] = fw_ref[...]
        mine[5:8, :] = cw_ref[...]
        s_ref[me] = mine[...]
        _exchange_tiles(lambda lin: mine, s_ref, sems.at[0], sems.at[1])
        sall = s_ref[...].reshape(NDEV * 8, D).astype(BF)
        r = jnp.dot(sall, w_ref[...].astype(BF), preferred_element_type=F32) + b_ref[...]
        res[...] = r.reshape(NDEV, 8, nsh)
        m_ref[me] = res[me]
        _exchange_tiles(lambda lin: res.at[lin], m_ref, sems.at[2], sems.at[3])

    vm = pl.BlockSpec(memory_space=pltpu.VMEM)
    return pl.pallas_call(
        body, name=name, in_specs=[vm] * 6 + [pl.BlockSpec(memory_space=pl.ANY)] * len(deps), out_specs=[vm, vm],
        out_shape=[jax.ShapeDtypeStruct((NDEV, 8, D), F32), jax.ShapeDtypeStruct((NDEV, 8, nsh), F32)],
        scratch_shapes=[pltpu.VMEM((8, D), F32), pltpu.VMEM((NDEV, 8, nsh), F32), pltpu.SemaphoreType.DMA((4,))],
    )(c, c_ctx, ffn_w, conv_w, w_shard, b_shard, *deps)


P_DML, P_DMC, P_N1, P_QG, P_KVG, P_CB, P_N2, P_FB, P_FG, P_CW, P_FW, P_LOSS, P_ROWS = 0, 6, 12, 13, 14, 15, 16, 17, 23, 24, 27, 45, 48
FROWS = 3


def sync_small(r, deps, *, name):
    ins = [r["dmod_lat"], r["dmod_ctx"], r["dn1g"], r["dqg"], r["dkvg"], r["dconv_b"], r["dn2g"], r["dffn_b"], r["dfg"],
           r["dconv_w"], r["dffn_w"], r["loss"]]

    def put_wide(p, row0, row, n):
        for j in range(-(-n // D)):
            w = min(D, n - j * D)
            p[row0 + j:row0 + j + 1, 0:w] = row[:, j * D:j * D + w]

    def body(dml, dmc, n1, qg, kvg, cb, n2, fb, fg, cw, fw, loss, *rest):
        a_ref, sum_ref, p, sems = rest[len(deps):]
        x, y, c, me = _me()
        p[...] = jnp.zeros_like(p)
        put_wide(p, P_DML, dml, 6 * D)
        put_wide(p, P_DMC, dmc, 6 * D)
        put_wide(p, P_N1, n1, D)
        put_wide(p, P_QG, qg, 512)
        put_wide(p, P_KVG, kvg, KVL)
        put_wide(p, P_CB, cb, CONV)
        put_wide(p, P_N2, n2, D)
        put_wide(p, P_FG, fg, D)
        put_wide(p, P_LOSS, loss, 128)
        for s in range(2):
            put_wide(p, P_FB + FROWS * s, fb.at[s], DFF)
        for k in range(3):
            put_wide(p, P_CW + k, cw.at[k:k + 1], CONV)
            for s in range(2):
                put_wide(p, P_FW + FROWS * (2 * k + s), fw.at[s, k:k + 1], DFF)
        a_ref[me] = p[...]
        _exchange_tiles(lambda lin: p, a_ref, sems.at[0], sems.at[1])
        acc = a_ref[0]
        for k in range(1, NDEV):
            acc = acc + a_ref[k]
        sum_ref[...] = acc

    vm = pl.BlockSpec(memory_space=pltpu.VMEM)
    return pl.pallas_call(
        body, name=name, in_specs=[vm] * len(ins) + [pl.BlockSpec(memory_space=pl.ANY)] * len(deps), out_specs=[vm, vm],
        out_shape=[jax.ShapeDtypeStruct((NDEV, P_ROWS, D), F32), jax.ShapeDtypeStruct((P_ROWS, D), F32)],
        scratch_shapes=[pltpu.VMEM((P_ROWS, D), F32), pltpu.SemaphoreType.DMA((2,))],
    )(*ins, *deps)


def ada_bwd(s_all, dml, dmc, w_shard, c_ctx, *, name):
    nsh = w_shard.shape[1]

    def body(s_ref, dml_ref, dmc_ref, w_ref, c_ref, dw_ref, gc_ref, s16, dm16, part, buf, sems):
        x, y, c, me = _me()
        s16[...] = jnp.zeros_like(s16)
        dm16[...] = jnp.zeros_like(dm16)
        for k in range(NDEV):
            s16[k:k + 1, :] = s_ref[k, 0:1, :]
        s16[8:9, :] = s_ref[0, 1:2, :]
        dm16[0:8, :] = dml_ref[...]
        dm16[8:9, :] = dmc_ref[...]
        dw_ref[...] = lax.dot_general(s16[...].astype(BF), dm16[...].astype(BF), (((0,), (0,)), ((), ())),
                                      preferred_element_type=F32)
        part[...] = lax.dot_general(dm16[8:16, :].astype(BF), w_ref[...].astype(BF), (((1,), (1,)), ((), ())),
                                    preferred_element_type=F32)
        buf[me] = part[...]
        _exchange_tiles(lambda lin: part, buf, sems.at[0], sems.at[1])
        acc = buf[0]
        for k in range(1, NDEV):
            acc = acc + buf[k]
        z = c_ref[...]
        sg = jax.nn.sigmoid(z)
        gc_ref[...] = acc * (sg * (1.0 + z * (1.0 - sg)))

    vm = pl.BlockSpec(memory_space=pltpu.VMEM)
    return pl.pallas_call(
        body, name=name, in_specs=[vm] * 5, out_specs=[vm, vm],
        out_shape=[jax.ShapeDtypeStruct((D, nsh), F32), jax.ShapeDtypeStruct((8, D), F32)],
        scratch_shapes=[pltpu.VMEM((16, D), F32), pltpu.VMEM((16, nsh), F32), pltpu.VMEM((8, D), F32),
                        pltpu.VMEM((NDEV, 8, D), F32), pltpu.SemaphoreType.DMA((2,))],
    )(s_all, dml, dmc, w_shard, c_ctx)


HBM_SPEC = pl.BlockSpec(memory_space=pltpu.HBM)
SEM_SPEC = pl.BlockSpec(memory_space=pltpu.SEMAPHORE)
EFFECT = pltpu.SideEffectType.DATAFLOW_SIDE_EFFECTING


ALL_PEERS = tuple(range(1, NDEV))
FIRST_HOP = (1, 2, 4, 6)
RELAY = (2, 4, 6)


def _exchange_copies(srcs, lands, send, recv, per_peer, peers):
    x, y, c, me = _me()
    n = len(peers)
    cps = []
    for t in range(len(srcs)):
        for j, k in enumerate(peers):
            dev, lin = _peer(x, y, c, k)
            cps.append(pltpu.make_async_remote_copy(
                src_ref=srcs[t].at[lin] if per_peer else srcs[t], dst_ref=lands[t].at[me],
                send_sem=send.at[n * t + j], recv_sem=recv.at[n * t + j], device_id=dev, device_id_type=MESH))
    return cps


def _relay_copies(lands, send, recv):
    x, y, c, me = _me()
    n = len(RELAY)
    cps = []
    for t in range(len(lands)):
        for j, k in enumerate(RELAY):
            slot = lands[t].at[_peer(x, y, c, k)[1]]
            cps.append(pltpu.make_async_remote_copy(
                src_ref=slot, dst_ref=slot, send_sem=send.at[n * t + j], recv_sem=recv.at[n * t + j],
                device_id=(x, y, 1 - c), device_id_type=MESH))
    return cps


def _own_copies(srcs, lands, own, per_peer):
    me = _me()[3]
    return [pltpu.make_async_copy(srcs[t].at[me] if per_peer else srcs[t], lands[t].at[me], own.at[t])
            for t in range(len(srcs))]


def exchange_start(srcs, *, per_peer, name, dep=None, peers=ALL_PEERS):
    nt = len(srcs)
    ns = len(peers) * nt
    land_shapes = [(a.shape if per_peer else (NDEV,) + a.shape) for a in srcs]
    deps = [] if dep is None else [dep]

    def body(*refs):
        src, land = refs[:nt], refs[nt:2 * nt]
        send, recv, own = refs[2 * nt + len(deps):2 * nt + len(deps) + 3]
        for cp in _exchange_copies(src, land, send, recv, per_peer, peers) + _own_copies(src, land, own, per_peer):
            cp.start()
        refs[-1][...] = jnp.zeros_like(refs[-1])

    hb = lambda a: pltpu.with_memory_space_constraint(a, pltpu.HBM)
    outs = pl.pallas_call(
        body, name=name,
        out_shape=(pltpu.SemaphoreType.DMA((ns,)), pltpu.SemaphoreType.DMA((ns,)), pltpu.SemaphoreType.DMA((nt,)),
                   *[pltpu.HBM(a.shape, a.dtype) for a in srcs], *[pltpu.HBM(s, a.dtype) for s, a in zip(land_shapes, srcs)],
                   jax.ShapeDtypeStruct((8, 128), F32)),
        in_specs=[HBM_SPEC] * (2 * nt) + [pl.BlockSpec(memory_space=pl.ANY)] * len(deps),
        out_specs=(SEM_SPEC, SEM_SPEC, SEM_SPEC, *([HBM_SPEC] * (2 * nt)), pl.BlockSpec(memory_space=pltpu.VMEM)),
        input_output_aliases={i: 3 + i for i in range(2 * nt)},
        compiler_params=pltpu.CompilerParams(has_side_effects=EFFECT),
    )(*[hb(a) for a in srcs], *[hb(lax.empty(s, a.dtype)) for s, a in zip(land_shapes, srcs)], *deps)
    return dict(send=outs[0], recv=outs[1], own=outs[2], src=list(outs[3:3 + nt]), land=list(outs[3 + nt:3 + 2 * nt]),
                token=outs[-1], per_peer=per_peer, peers=peers)


def exchange_wait(h, after, *, name):
    nt = len(h["src"])
    per_peer, peers = h["per_peer"], h["peers"]

    def body(*refs):
        src, land, send, recv, own = refs[:nt], refs[nt:2 * nt], refs[2 * nt], refs[2 * nt + 1], refs[2 * nt + 2]
        for cp in _exchange_copies(src, land, send, recv, per_peer, peers):
            cp.wait_send()
            cp.wait_recv()
        for cp in _own_copies(src, land, own, per_peer):
            cp.wait()

    outs = pl.pallas_call(
        body, name=name,
        out_shape=(*[pltpu.HBM(a.shape, a.dtype) for a in h["src"]], *[pltpu.HBM(a.shape, a.dtype) for a in h["land"]]),
        in_specs=[HBM_SPEC] * (2 * nt) + [SEM_SPEC, SEM_SPEC, SEM_SPEC, pl.BlockSpec(memory_space=pl.ANY)],
        out_specs=tuple([HBM_SPEC] * (2 * nt)),
        input_output_aliases={i: i for i in range(2 * nt)},
        compiler_params=pltpu.CompilerParams(has_side_effects=EFFECT),
    )(*h["src"], *h["land"], h["send"], h["recv"], h["own"], after)
    return list(outs[nt:])


def relay_start(lands, *, name):
    nt = len(lands)
    ns = len(RELAY) * nt

    def body(*refs):
        for cp in _relay_copies(refs[:nt], refs[nt], refs[nt + 1]):
            cp.start()

    outs = pl.pallas_call(
        body, name=name,
        out_shape=(pltpu.SemaphoreType.DMA((ns,)), pltpu.SemaphoreType.DMA((ns,)),
                   *[pltpu.HBM(a.shape, a.dtype) for a in lands]),
        in_specs=[HBM_SPEC] * nt, out_specs=(SEM_SPEC, SEM_SPEC, *([HBM_SPEC] * nt)),
        input_output_aliases={i: 2 + i for i in range(nt)},
        compiler_params=pltpu.CompilerParams(has_side_effects=EFFECT),
    )(*lands)
    return dict(send=outs[0], recv=outs[1], land=list(outs[2:]))


def relay_wait(h, *, name):
    nt = len(h["land"])

    def body(*refs):
        for cp in _relay_copies(refs[:nt], refs[nt], refs[nt + 1]):
            cp.wait_send()
            cp.wait_recv()

    outs = pl.pallas_call(
        body, name=name, out_shape=tuple(pltpu.HBM(a.shape, a.dtype) for a in h["land"]),
        in_specs=[HBM_SPEC] * nt + [SEM_SPEC, SEM_SPEC], out_specs=tuple([HBM_SPEC] * nt),
        input_output_aliases={i: i for i in range(nt)},
        compiler_params=pltpu.CompilerParams(has_side_effects=EFFECT),
    )(*h["land"], h["send"], h["recv"])
    return list(outs)


def _adamw_math(w, g, m, v):
    nm = B1 * m + (1.0 - B1) * g
    nv = B2 * v + (1.0 - B2) * (g * g)
    m_hat = nm / (1.0 - B1 ** STEP)
    v_hat = nv / (1.0 - B2 ** STEP)
    return -LR * (m_hat / (jnp.sqrt(v_hat) + AEPS) + WD * w), nm, nv


def adamw_many(ws, gs, ms, vs, *, name):
    n = len(ws)

    def body(*refs):
        for k in range(n):
            d, nm, nv = _adamw_math(refs[k][...], refs[n + k][...], refs[2 * n + k][...], refs[3 * n + k][...])
            refs[4 * n + k][...] = d
            refs[5 * n + k][...] = nm
            refs[6 * n + k][...] = nv

    vm = pl.BlockSpec(memory_space=pltpu.VMEM)
    sh = [jax.ShapeDtypeStruct(w.shape, F32) for w in ws]
    outs = pl.pallas_call(body, name=name, in_specs=[vm] * (4 * n), out_specs=[vm] * (3 * n), out_shape=sh * 3,
                          )(*ws, *gs, *ms, *vs)
    return outs[:n], outs[n:2 * n], outs[2 * n:]


def adamw(w, g, m, v, *, name, tr=256):
    R, C = w.shape
    tr = _pick(R, tr, 8)

    def body(w_ref, g_ref, m_ref, v_ref, d_ref, nm_ref, nv_ref):
        d_ref[...], nm_ref[...], nv_ref[...] = _adamw_math(w_ref[...], g_ref[...], m_ref[...], v_ref[...])

    blk = pl.BlockSpec((tr, C), lambda i: (i, 0))
    sh = jax.ShapeDtypeStruct((R, C), F32)
    return pl.pallas_call(
        body, name=name, grid=(R // tr,), in_specs=[blk, blk, blk, blk], out_specs=[blk, blk, blk],
        out_shape=[sh, sh, sh], compiler_params=pltpu.CompilerParams(dimension_semantics=("parallel",)),
    )(w, g, m, v)


def adamw_slots(w, slots, m, v, *, name, tr=256):
    unit = w.ndim == 3
    R, C = w.shape[0], w.shape[-1]
    if R % 16 == 0:
        tr = _pick(R, tr, 16)
    else:
        tr = 144

    def body(w_ref, s_ref, m_ref, v_ref, g_ref, d_ref, nm_ref, nv_ref):
        g = s_ref[0].astype(F32)
        for k in range(1, NDEV):
            g = g + s_ref[k].astype(F32)
        g_ref[...] = g
        d_ref[...], nm_ref[...], nv_ref[...] = _adamw_math(w_ref[...], g, m_ref[...], v_ref[...])

    blk = pl.BlockSpec((tr, None, C), lambda i: (i, 0, 0)) if unit else pl.BlockSpec((tr, C), lambda i: (i, 0))
    sh = jax.ShapeDtypeStruct(w.shape, F32)
    return pl.pallas_call(
        body, name=name, grid=(pl.cdiv(R, tr),), in_specs=[blk, pl.BlockSpec((NDEV, tr, C), lambda i: (0, i, 0)), blk, blk],
        out_specs=[blk, blk, blk, blk], out_shape=[sh, sh, sh, sh],
        compiler_params=pltpu.CompilerParams(dimension_semantics=("parallel",)),
    )(w, slots, m, v)


def _padc(a, n=D):
    return jnp.pad(a, ((0, 0), (0, n - a.shape[1])))


def kernel(x, c, ctx, c_ctx, w_ada, b_ada, norm1_g, w_in, q_norm_g, kv_norm_g, w_uq, w_ukv, conv_w, conv_b, w_attn_out, w_conv_out, w_o, norm2_g, w_up, ffn_conv_w, ffn_conv_b, w_down, final_g, loss_target, m_c_ctx, m_w_ada, m_b_ada, m_norm1_g, m_w_in, m_q_norm_g, m_kv_norm_g, m_w_uq, m_w_ukv, m_conv_w, m_conv_b, m_w_attn_out, m_w_conv_out, m_w_o, m_norm2_g, m_w_up, m_ffn_conv_w, m_ffn_conv_b, m_w_down, m_final_g, v_c_ctx, v_w_ada, v_b_ada, v_norm1_g, v_w_in, v_q_norm_g, v_kv_norm_g, v_w_uq, v_w_ukv, v_conv_w, v_conv_b, v_w_attn_out, v_w_conv_out, v_w_o, v_norm2_g, v_w_up, v_ffn_conv_w, v_ffn_conv_b, v_w_down, v_final_g):
    me = 4 * lax.axis_index("x") + 2 * lax.axis_index("y") + lax.axis_index("c")
    W = dict(c_ctx=c_ctx, w_ada=w_ada, b_ada=b_ada, norm1_g=norm1_g, w_in=w_in, q_norm_g=q_norm_g, kv_norm_g=kv_norm_g,
             w_uq=w_uq, w_ukv=w_ukv, conv_w=conv_w, conv_b=conv_b, w_attn_out=w_attn_out, w_conv_out=w_conv_out, w_o=w_o,
             norm2_g=norm2_g, w_up=w_up, ffn_conv_w=ffn_conv_w, ffn_conv_b=ffn_conv_b, w_down=w_down, final_g=final_g)
    M = dict(c_ctx=m_c_ctx, w_ada=m_w_ada, b_ada=m_b_ada, norm1_g=m_norm1_g, w_in=m_w_in, q_norm_g=m_q_norm_g,
             kv_norm_g=m_kv_norm_g, w_uq=m_w_uq, w_ukv=m_w_ukv, conv_w=m_conv_w, conv_b=m_conv_b, w_attn_out=m_w_attn_out,
             w_conv_out=m_w_conv_out, w_o=m_w_o, norm2_g=m_norm2_g, w_up=m_w_up, ffn_conv_w=m_ffn_conv_w,
             ffn_conv_b=m_ffn_conv_b, w_down=m_w_down, final_g=m_final_g)
    V = dict(c_ctx=v_c_ctx, w_ada=v_w_ada, b_ada=v_b_ada, norm1_g=v_norm1_g, w_in=v_w_in, q_norm_g=v_q_norm_g,
             kv_norm_g=v_kv_norm_g, w_uq=v_w_uq, w_ukv=v_w_ukv, conv_w=v_conv_w, conv_b=v_conv_b, w_attn_out=v_w_attn_out,
             w_conv_out=v_w_conv_out, w_o=v_w_o, norm2_g=v_norm2_g, w_up=v_w_up, ffn_conv_w=v_ffn_conv_w,
             ffn_conv_b=v_ffn_conv_b, w_down=v_w_down, final_g=v_final_g)
    names = list(W)
    transposed = ("w_up",)
    as2d = lambda k, a: (a.reshape(1, -1) if a.ndim == 1 else
                         a[0].T if k in transposed else a.reshape(a.shape[-2], a.shape[-1]))
    W2 = {k: as2d(k, a) for k, a in W.items()}
    M2 = {k: as2d(k, a) for k, a in M.items()}
    V2 = {k: as2d(k, a) for k, a in V.items()}
    unit3 = lambda a: jnp.transpose(a, (2, 0, 1))
    W3, M3, V3 = unit3(W["w_in"]), unit3(M["w_in"]), unit3(V["w_in"])
    nsh = W2["w_ada"].shape[1]

    b_sh = lax.dynamic_slice(W2["b_ada"], (0, me * nsh), (1, nsh))
    s_all, m_all = ada_fwd(c, W2["c_ctx"], _padc(W2["ffn_conv_w"]), _padc(W2["conv_w"]), W2["w_ada"], b_sh, [],
                           name="ada_fwd")
    mod_lat = m_all[:, 0, :].reshape(1, 6 * D)
    mod_ctx = m_all[:, 1, :].reshape(1, 6 * D)
    ffn_w_full = s_all[:, 2:5, :2 * DFF // NDEV].transpose(1, 0, 2).reshape(3, 2 * DFF)
    conv_w_full = s_all[:, 5:8, :CONV // NDEV].transpose(1, 0, 2).reshape(3, CONV)

    stage_w = {"in": ["w_in"], "mid": ["w_uq", "w_ukv", "w_attn_out", "w_conv_out", "w_o"], "ffn": ["w_up", "w_down"]}
    two_level = ("in", "mid")
    ag, tok = {}, m_all
    for st, nms in stage_w.items():
        ag[st] = exchange_start([W2[nm].astype(BF) for nm in nms], per_peer=False, dep=tok, name="ag_start_" + st,
                                peers=FIRST_HOP if st in two_level else ALL_PEERS)
        tok = ag[st]["token"]

    def get_w(stage, after):
        lands = exchange_wait(ag[stage], after, name="ag_wait_" + stage)
        if stage in two_level:
            lands = relay_wait(relay_start(lands, name="ag_relay_" + stage), name="ag_relay_wait_" + stage)
        g = dict(zip(stage_w[stage], lands))
        if stage == "in":
            return build_win(g["w_in"], name="build_win")
        if stage == "mid":
            wq2, wkv2 = build_wq_wkv(g["w_uq"], g["w_ukv"], name="build_wq_wkv")
            return (wq2, wkv2, unshard_cols(g["w_attn_out"], name="unshard_w_attn_out"),
                    unshard_cols(g["w_conv_out"], name="unshard_w_conv_out"), g["w_o"].reshape(D, D))
        return g["w_up"].reshape(2 * DFF, D), g["w_down"].reshape(DFF, D)

    stage_g = {"ffn": ["w_up", "w_down"], "mid": ["w_attn_out", "w_conv_out", "w_o"], "qkv": ["w_uq", "w_ukv"],
               "in": ["w_in"]}
    rs = {}

    def put_g(stage, g):
        if stage == "in":
            parts = [shard_win_grad(g["dwin"], g["dwin_c"], name="shard_win_grad")]
        elif stage == "mid":
            parts = [shard_cols(g["dwao"], name="shard_w_attn_out"), shard_cols(g["dwco"], name="shard_w_conv_out"),
                     g["dwo"].reshape(NDEV, D // NDEV, D)]
        elif stage == "qkv":
            parts = list(shard_wq_wkv_grad(g["dwq2"], g["dwkv2"], name="shard_wq_wkv_grad"))
        else:
            parts = [g["dwup"].reshape(NDEV, 2 * DFF // NDEV, D), g["dwdn"].reshape(NDEV, DFF // NDEV, D)]
        rs[stage] = exchange_start(parts, per_peer=True, name="rs_start_" + stage)
        return rs[stage]["token"]

    r = _local_step(x[0], ctx[0], loss_target[0], mod_lat, mod_ctx, W2["norm1_g"], W2["q_norm_g"], W2["kv_norm_g"],
                    W2["norm2_g"], W2["final_g"], conv_w_full, W2["conv_b"], ffn_w_full, W2["ffn_conv_b"], get_w, put_g,
                    ag["ffn"]["token"])

    G, DL, NM, NV = {}, {}, {}, {}

    def finish(stage, after):
        for nm, sl in zip(stage_g[stage], exchange_wait(rs[stage], after, name="rs_wait_" + stage)):
            wmv = (W3, M3, V3) if nm == "w_in" else (W2[nm], M2[nm], V2[nm])
            G[nm], DL[nm], NM[nm], NV[nm] = adamw_slots(wmv[0], sl, wmv[1], wmv[2], name="adamw_" + nm)
            after = DL[nm]
        return after

    after = r["dx"]
    for st in ("ffn", "mid", "qkv"):
        after = finish(st, after)

    a_buf, ssum = sync_small(r, [DL[nm] for st in ("ffn", "mid", "qkv") for nm in stage_g[st]], name="sync_small")
    loss = ssum[P_LOSS, 0]
    G["norm1_g"] = ssum[P_N1:P_N1 + 1]
    G["q_norm_g"] = ssum[P_QG:P_QG + 1, :QL]
    G["kv_norm_g"] = ssum[P_KVG:P_KVG + 1, :KVL]
    G["conv_b"] = ssum[P_CB:P_CB + 1, :CONV]
    G["norm2_g"] = ssum[P_N2:P_N2 + 1]
    G["ffn_conv_b"] = ssum[P_FB:P_FB + 2 * FROWS].reshape(1, 2, FROWS * D)[:, :, :DFF].reshape(1, 2 * DFF)
    G["final_g"] = ssum[P_FG:P_FG + 1]
    G["conv_w"] = lax.dynamic_slice(ssum[P_CW:P_CW + 3, :CONV], (0, me * (CONV // NDEV)), (3, CONV // NDEV))
    fw_full = ssum[P_FW:P_FW + 6 * FROWS].reshape(3, 2, FROWS * D)[:, :, :DFF].reshape(3, 2 * DFF)
    G["ffn_conv_w"] = lax.dynamic_slice(fw_full, (0, me * (2 * DFF // NDEV)), (3, 2 * DFF // NDEV))
    G["b_ada"] = (ssum[P_DML:P_DML + 6] + ssum[P_DMC:P_DMC + 6]).reshape(1, 6 * D)

    dml = lax.dynamic_slice(a_buf[:, P_DML:P_DML + 6, :].reshape(NDEV, 6 * D), (0, me * nsh), (NDEV, nsh))
    dmc = lax.dynamic_slice(ssum[P_DMC:P_DMC + 6].reshape(1, 6 * D), (0, me * nsh), (1, nsh))
    G["w_ada"], gcc = ada_bwd(s_all, dml, dmc, W2["w_ada"], W2["c_ctx"], name="ada_bwd")
    G["c_ctx"] = gcc[0:1]

    DL["w_ada"], NM["w_ada"], NV["w_ada"] = adamw(W2["w_ada"], G["w_ada"], M2["w_ada"], V2["w_ada"], name="adamw_w_ada")
    small = ["c_ctx", "b_ada", "norm1_g", "q_norm_g", "kv_norm_g", "conv_b", "norm2_g", "ffn_conv_b", "final_g", "conv_w",
             "ffn_conv_w"]
    ds, nms, nvs = adamw_many([W2[k] for k in small], [G[k] for k in small], [M2[k] for k in small],
                              [V2[k] for k in small], name="adamw_small")
    for k, nm in enumerate(small):
        DL[nm], NM[nm], NV[nm] = ds[k], nms[k], nvs[k]
    finish("in", ds[0])

    outs = [loss, r["dx"][None]]
    for grp in (G, DL, NM, NV):
        outs += [grp[nm].T[None] if nm in transposed else
                 jnp.transpose(grp[nm], (1, 2, 0)) if nm == "w_in" else grp[nm].reshape(W[nm].shape) for nm in names]
    return tuple(outs)
```

```python
import functools
import numpy as np
import jax
import jax.numpy as jnp
from jax import lax
from jax.experimental import pallas as pl
from jax.experimental.pallas import tpu as pltpu

F32 = jnp.float32
BF = jnp.bfloat16
MESH = pl.DeviceIdType.MESH

D = 1024
T = 2048
TC = 256
TKV = T + TC
GRID_W = 64
NH = 8
DN = 64
DR = 32
DV = 64
QL = 384
KVL = 256
CONV = 512
DFF = 2816
EPS = 1e-6
ROPE_THETA = 10000.0
SCALE = (DN + DR) ** -0.5
NDEV = 8
HP = 128

O_GA, O_GC, O_KV, O_Q, O_CV = 0, 1024, 2048, 2560, 3072
NIN = 4608
CVB = 256
N_IN = 4256
SH_IN = N_IN // NDEV

LR, B1, B2, AEPS, WD, STEP = 0.001, 0.9, 0.999, 1e-08, 0.01, 10


def _pick(n, target, mult=128):
    best = None
    for d in range(mult, min(n, target) + 1, mult):
        if n % d == 0:
            best = d
    return best if best is not None else n


def _swap_start(g):
    return 8 * (g ^ 1)


def mm(a, b, *, ta=False, tb=False, out_dtype=F32, name, tm=1024, tn=1024, tk=2048, M=None, N=None, K=None,
       a_off=(0, 0), b_off=(0, 0), a_stack=False, b_stack=False, o_stack=False, dep=None):
    def dims(arr, stack):
        return (arr.shape[1], 2 * arr.shape[2]) if stack else arr.shape

    ar, ac = dims(a, a_stack)
    br, bc = dims(b, b_stack)
    M = M or ((ac if ta else ar) - a_off[1 if ta else 0])
    K = K or ((ar if ta else ac) - a_off[0 if ta else 1])
    N = N or ((br if tb else bc) - b_off[0 if tb else 1])
    tm = _pick(M, tm, 128 if ta else 16)
    tn = _pick(N // 2 if (o_stack or (b_stack and not tb)) else N, tn, 128)
    tk = _pick(K // 2 if ((a_stack and not ta) or (b_stack and tb)) else K, tk, 128)
    nk = K // tk
    ca = 0 if ta else 1
    cb = 1 if tb else 0

    def body(a_ref, b_ref, *rest):
        o_ref, acc = rest[-2:]
        k = pl.program_id(2)
        part = lax.dot_general(a_ref[...].astype(BF), b_ref[...].astype(BF),
                               (((ca,), (cb,)), ((), ())), preferred_element_type=F32)
        if nk == 1:
            o_ref[...] = part.astype(o_ref.dtype)
        else:
            @pl.when(k == 0)
            def _():
                acc[...] = part

            @pl.when(k > 0)
            def _():
                acc[...] += part

            @pl.when(k == nk - 1)
            def _():
                o_ref[...] = acc[...].astype(o_ref.dtype)

    def spec(blk, rc, off, stack, ncols):
        assert off[0] % blk[0] == 0 and off[1] % blk[1] == 0, (name, blk, off)
        ro, co = off[0] // blk[0], off[1] // blk[1]
        if not stack:
            return pl.BlockSpec(blk, lambda i, j, k: (rc(i, j, k)[0] + ro, rc(i, j, k)[1] + co))
        nhb = ncols // 2 // blk[1]
        return pl.BlockSpec((None,) + blk,
                            lambda i, j, k: ((rc(i, j, k)[1] + co) // nhb, rc(i, j, k)[0] + ro, (rc(i, j, k)[1] + co) % nhb))

    a_spec = spec((tk, tm), lambda i, j, k: (k, i), a_off, a_stack, ac) if ta else \
        spec((tm, tk), lambda i, j, k: (i, k), a_off, a_stack, ac)
    b_spec = spec((tn, tk), lambda i, j, k: (j, k), b_off, b_stack, bc) if tb else \
        spec((tk, tn), lambda i, j, k: (k, j), b_off, b_stack, bc)
    o_spec = spec((tm, tn), lambda i, j, k: (i, j), (0, 0), o_stack, N)
    o_shape = (2, M, N // 2) if o_stack else (M, N)
    deps = [] if dep is None else [dep]
    return pl.pallas_call(
        body, name=name, grid=(M // tm, N // tn, nk),
        in_specs=[a_spec, b_spec] + [pl.BlockSpec(memory_space=pl.ANY)] * len(deps),
        out_specs=o_spec, out_shape=jax.ShapeDtypeStruct(o_shape, out_dtype),
        scratch_shapes=[pltpu.VMEM((tm, tn) if nk > 1 else (8, 128), F32)],
        compiler_params=pltpu.CompilerParams(dimension_semantics=("parallel", "parallel", "arbitrary")),
    )(a, b, *deps)


def _row(width):
    return pl.BlockSpec((1, width), lambda *_: (0, 0))


NLAT = T // TC


def normmod_cat(ctx, x, g, csc, csh, sc, sh, dep, *, name, tm=256):
    assert tm == TC

    def body(c_ref, x_ref, g_ref, csc_ref, csh_ref, sc_ref, sh_ref, dep_ref, h_ref):
        last = pl.program_id(0) == NLAT
        xv = jnp.where(last, c_ref[...], x_ref[...])
        scv = jnp.where(last, csc_ref[...], sc_ref[...])
        shv = jnp.where(last, csh_ref[...], sh_ref[...])
        r = lax.rsqrt(jnp.mean(xv * xv, axis=-1, keepdims=True) + EPS)
        h_ref[...] = ((xv * r * g_ref[...]) * (1.0 + scv) + shv).astype(BF)

    return pl.pallas_call(
        body, name=name, grid=(TKV // tm,),
        in_specs=[pl.BlockSpec((tm, D), lambda i: (0, 0)), pl.BlockSpec((tm, D), lambda i: (jnp.minimum(i, NLAT - 1), 0)),
                  _row(D), _row(D), _row(D), _row(D), _row(D), pl.BlockSpec(memory_space=pl.ANY)],
        out_specs=pl.BlockSpec((tm, D), lambda i: (i, 0)), out_shape=jax.ShapeDtypeStruct((TKV, D), BF),
        compiler_params=pltpu.CompilerParams(dimension_semantics=("parallel",)),
    )(ctx, x, g, csc, csh, sc, sh, dep)


def kvprep(pc, p, kvg, wkv2, ck, sk, *, name, tm=256):
    assert tm == TC
    nb = TKV // tm
    kvcol = O_KV // 512

    def body(pc_ref, p_ref, g_ref, w_ref, ck_ref, sk_ref, k_ref, v_ref, ckv_ref):
        i = pl.program_id(0)
        t = jnp.where(i == NLAT, pc_ref[...], p_ref[...])
        pk = t[:, :KVL]
        r = lax.rsqrt(jnp.mean(pk * pk, axis=-1, keepdims=True) + EPS)
        ckv = (pk * r * g_ref[...]).astype(BF)
        ckv_ref[...] = ckv
        kv2 = jnp.dot(ckv, w_ref[...], preferred_element_type=F32)
        krr = t[:, KVL:KVL + HP] * ck_ref[...] + t[:, KVL + HP:KVL + 2 * HP] * sk_ref[...]
        k_ref[...] = (kv2[:, :NH * HP] + jnp.concatenate([krr] * NH, axis=1)).astype(BF)
        v_ref[...] = kv2[:, NH * HP:].astype(BF)

    return pl.pallas_call(
        body, name=name, grid=(nb,),
        in_specs=[pl.BlockSpec((tm, 512), lambda i: (0, 0)),
                  pl.BlockSpec((tm, 512), lambda i: (jnp.minimum(i, NLAT - 1), kvcol)),
                  _row(KVL), pl.BlockSpec((KVL, NH * HP + NH * DV), lambda i: (0, 0)),
                  pl.BlockSpec((tm, HP), lambda i: (i, 0)), pl.BlockSpec((tm, HP), lambda i: (i, 0))],
        out_specs=[pl.BlockSpec((tm, NH * HP), lambda i: (i, 0)), pl.BlockSpec((tm, NH * DV), lambda i: (i, 0)),
                   pl.BlockSpec((tm, KVL), lambda i: (i, 0))],
        out_shape=[jax.ShapeDtypeStruct((TKV, NH * HP), BF), jax.ShapeDtypeStruct((TKV, NH * DV), BF),
                   jax.ShapeDtypeStruct((TKV, KVL), BF)],
        compiler_params=pltpu.CompilerParams(dimension_semantics=("parallel",)),
    )(pc, p, kvg, wkv2, ck, sk)


def qprep(p, qg, wq2, cq_t, sq_t, *, name, tm=256):
    qcol = O_Q // 512

    def body(p_ref, g_ref, w_ref, c_ref, s_ref, q_ref, cq_ref):
        pq = p_ref[...]
        r = lax.rsqrt(jnp.sum(pq * pq, axis=-1, keepdims=True) * (1.0 / QL) + EPS)
        cq = (pq * r * g_ref[...]).astype(BF)
        cq_ref[...] = cq
        q2 = jnp.dot(cq, w_ref[...], preferred_element_type=F32)
        cc = jnp.concatenate([c_ref[...]] * NH, axis=1)
        ss = jnp.concatenate([s_ref[...]] * NH, axis=1)
        q_ref[...] = (q2[:, :NH * HP] * cc + q2[:, NH * HP:] * ss).astype(BF)

    return pl.pallas_call(
        body, name=name, grid=(T // tm,),
        in_specs=[pl.BlockSpec((tm, 512), lambda i: (i, qcol)), _row(512),
                  pl.BlockSpec((512, 2 * NH * HP), lambda i: (0, 0)),
                  pl.BlockSpec((tm, HP), lambda i: (i, 0)), pl.BlockSpec((tm, HP), lambda i: (i, 0))],
        out_specs=[pl.BlockSpec((tm, NH * HP), lambda i: (i, 0)), pl.BlockSpec((tm, 512), lambda i: (i, 0))],
        out_shape=[jax.ShapeDtypeStruct((T, NH * HP), BF), jax.ShapeDtypeStruct((T, 512), BF)],
        compiler_params=pltpu.CompilerParams(dimension_semantics=("parallel",)),
    )(p, qg, wq2, cq_t, sq_t)


def _head_mask(h):
    lanes = lax.broadcasted_iota(jnp.int32, (1, 2 * DV), 1)
    return (lanes // DV) == (h % 2)


LOG2E = 1.4426950408889634


def attn_fwd(q, k, v, *, name, tq=1024, kc=768):
    def body(q_ref, k_ref, v_ref, o_ref, lse_ref):
        h = pl.program_id(1)
        qv = q_ref[...]
        m = l = acc = None
        for c in range(TKV // kc):
            s = lax.dot_general(qv, k_ref[c * kc:(c + 1) * kc, :], (((1,), (1,)), ((), ())),
                                preferred_element_type=F32) * (SCALE * LOG2E)
            mc = jnp.max(s, axis=-1, keepdims=True)
            if c == 0:
                m = mc
                e = jnp.exp2(s - m)
                l = jnp.sum(e, axis=-1, keepdims=True)
                acc = jnp.dot(e.astype(BF), v_ref[c * kc:(c + 1) * kc, :], preferred_element_type=F32)
            else:
                mn = jnp.maximum(m, mc)
                a = jnp.exp2(m - mn)
                e = jnp.exp2(s - mn)
                l = l * a + jnp.sum(e, axis=-1, keepdims=True)
                acc = acc * a + jnp.dot(e.astype(BF), v_ref[c * kc:(c + 1) * kc, :], preferred_element_type=F32)
                m = mn
        o2 = jnp.where(_head_mask(h), acc * (1.0 / l), 0.0).astype(BF)
        lse_ref[...] = jnp.broadcast_to(m + jnp.log(l) * LOG2E, (tq, HP))

        @pl.when(h % 2 == 0)
        def _():
            o_ref[...] = o2

        @pl.when(h % 2 == 1)
        def _():
            o_ref[...] = o_ref[...] + o2

    return pl.pallas_call(
        body, name=name, grid=(T // tq, NH),
        in_specs=[pl.BlockSpec((tq, HP), lambda i, h: (i, h)), pl.BlockSpec((TKV, HP), lambda i, h: (0, h)),
                  pl.BlockSpec((TKV, 2 * DV), lambda i, h: (0, h // 2))],
        out_specs=[pl.BlockSpec((tq, 2 * DV), lambda i, h: (i, h // 2)), pl.BlockSpec((tq, HP), lambda i, h: (i, h))],
        out_shape=[jax.ShapeDtypeStruct((T, NH * DV), BF), jax.ShapeDtypeStruct((T, NH * HP), F32)],
        compiler_params=pltpu.CompilerParams(dimension_semantics=("parallel", "arbitrary")),
    )(q, k, v)


def _shift_dn(x):
    n = x.shape[0]
    rows = lax.broadcasted_iota(jnp.int32, (n, 1), 0)
    return jnp.where(rows == 0, 0.0, pltpu.roll(x, 1, axis=0))


def _shift_up(x):
    n = x.shape[0]
    rows = lax.broadcasted_iota(jnp.int32, (n, 1), 0)
    return jnp.where(rows == n - 1, 0.0, pltpu.roll(x, n - 1, axis=0))


def _conv(x, w_ref, b_ref):
    return b_ref[...] + _shift_dn(x) * w_ref[0:1, :] + x * w_ref[1:2, :] + _shift_up(x) * w_ref[2:3, :]


def _conv_t(dy, w_ref):
    return _shift_up(dy) * w_ref[0:1, :] + dy * w_ref[1:2, :] + _shift_dn(dy) * w_ref[2:3, :]


def _conv_wgrad(dw_ref, dy, x):
    dw_ref[0:1, :] = jnp.sum(dy * _shift_dn(x), axis=0, keepdims=True)
    dw_ref[1:2, :] = jnp.sum(dy * x, axis=0, keepdims=True)
    dw_ref[2:3, :] = jnp.sum(dy * _shift_up(x), axis=0, keepdims=True)


def convz(p, cw, cb, *, name):
    o0 = O_CV // (3 * CVB)

    def body(p_ref, w_ref, bias_ref, z_ref):
        xv, bv, cv = p_ref[:, 0:CVB], p_ref[:, CVB:2 * CVB], p_ref[:, 2 * CVB:3 * CVB]
        z_ref[...] = (bv * _conv(cv * xv, w_ref, bias_ref)).astype(BF)

    return pl.pallas_call(
        body, name=name, grid=(CONV // CVB,),
        in_specs=[pl.BlockSpec((T, 3 * CVB), lambda j: (0, o0 + j)), pl.BlockSpec((3, CVB), lambda j: (0, j)),
                  pl.BlockSpec((1, CVB), lambda j: (0, j))],
        out_specs=pl.BlockSpec((T, CVB), lambda j: (0, j)),
        out_shape=jax.ShapeDtypeStruct((T, CONV), BF),
        compiler_params=pltpu.CompilerParams(dimension_semantics=("parallel",)),
    )(p, cw, cb)


def out_proj_merge(o, wao, z, wco, p, *, name, tm=512):
    kin = o.shape[1]

    def body(o_ref, wa_ref, z_ref, wc_ref, ga_ref, gc_ref, ya_ref, yc_ref, m_ref):
        ya = jnp.dot(o_ref[...], wa_ref[...], preferred_element_type=F32)
        yc = jnp.dot(z_ref[...], wc_ref[...], preferred_element_type=F32)
        ya_ref[...] = ya
        yc_ref[...] = yc
        m_ref[...] = (jax.nn.sigmoid(ga_ref[...]) * ya + jax.nn.sigmoid(gc_ref[...]) * yc).astype(BF)

    blk = pl.BlockSpec((tm, D), lambda i: (i, 0))
    act = pl.BlockSpec((tm, kin), lambda i: (i, 0))
    wsp = pl.BlockSpec((kin, D), lambda i: (0, 0))
    sh = jax.ShapeDtypeStruct((T, D), F32)
    return pl.pallas_call(
        body, name=name, grid=(T // tm,),
        in_specs=[act, wsp, act, wsp, pl.BlockSpec((tm, D), lambda i: (i, O_GA // D)),
                  pl.BlockSpec((tm, D), lambda i: (i, O_GC // D))],
        out_specs=[blk, blk, blk], out_shape=[sh, sh, jax.ShapeDtypeStruct((T, D), BF)],
        compiler_params=pltpu.CompilerParams(dimension_semantics=("parallel",)),
    )(o, wao, z, wco, p, p)


CONV_HALO = 8
CONV_ROWS = 256


def _row_chunks(n, chunk, carry):
    carry = chunk(0, True, False, carry)
    carry = lax.fori_loop(1, n // CONV_ROWS - 1, lambda c, a: chunk(c * CONV_ROWS, False, False, a), carry)
    return chunk(n - CONV_ROWS, False, True, carry)


def _ext_rows(ref, r0, first, last):
    n, w = ref.shape
    zero = jnp.zeros((CONV_HALO, w), ref.dtype)
    if first:
        return jnp.concatenate([zero, ref[0:CONV_ROWS + CONV_HALO, :]], axis=0)
    if last:
        return jnp.concatenate([ref[n - CONV_ROWS - CONV_HALO:n, :], zero], axis=0)
    return ref[pl.ds(pl.multiple_of(r0 - CONV_HALO, 8), CONV_ROWS + 2 * CONV_HALO), :]


def _center_rows(r0, first, last):
    return slice(r0, r0 + CONV_ROWS) if (first or last) else pl.ds(pl.multiple_of(r0, 8), CONV_ROWS)


def _roll_dn(x):
    return pltpu.roll(x, 1, axis=0)


def _roll_up(x):
    return pltpu.roll(x, x.shape[0] - 1, axis=0)


_CTR = slice(CONV_HALO, CONV_HALO + CONV_ROWS)


def ffn_act(u0, cw, cb, *, name, tc=256):
    nb = DFF // tc

    def body(u_ref, wg_ref, wv_ref, bg_ref, bv_ref, f_ref):
        wg = [wg_ref[k:k + 1, :] for k in range(3)]
        wv = [wv_ref[k:k + 1, :] for k in range(3)]
        bg, bv = bg_ref[...], bv_ref[...]

        def chunk(r0, first, last, carry):
            xg, xv = _ext_rows(u_ref.at[0], r0, first, last), _ext_rows(u_ref.at[1], r0, first, last)
            ug = bg + _roll_dn(xg) * wg[0] + xg * wg[1] + _roll_up(xg) * wg[2]
            uv = bv + _roll_dn(xv) * wv[0] + xv * wv[1] + _roll_up(xv) * wv[2]
            f_ref[_center_rows(r0, first, last), :] = (ug * jax.nn.sigmoid(ug) * uv)[_CTR].astype(BF)
            return carry

        _row_chunks(T, chunk, 0)

    return pl.pallas_call(
        body, name=name, grid=(nb,),
        in_specs=[pl.BlockSpec((2, T, tc), lambda j: (0, 0, j)),
                  pl.BlockSpec((3, tc), lambda j: (0, j)), pl.BlockSpec((3, tc), lambda j: (0, nb + j)),
                  pl.BlockSpec((1, tc), lambda j: (0, j)), pl.BlockSpec((1, tc), lambda j: (0, nb + j))],
        out_specs=pl.BlockSpec((T, tc), lambda j: (0, j)),
        out_shape=jax.ShapeDtypeStruct((T, DFF), BF),
        compiler_params=pltpu.CompilerParams(dimension_semantics=("parallel",)),
    )(u0, cw, cw, cb, cb)


def rows_call(lead, ins, in_specs, out_shape, out_specs, fn, *, name, R, tm):
    tb, a_stack, tk = lead.get("tb", False), lead.get("a_stack", False), lead["tk"]
    K = 2 * lead["a"].shape[2] if a_stack else lead["a"].shape[1]
    nk = K // tk
    deps = [] if lead.get("dep") is None else [lead["dep"]]
    n_in = len(ins)

    def body(a_ref, b_ref, *refs):
        refs = refs[len(deps):]
        in_refs, out_refs, acc = refs[:n_in], refs[n_in:-1], refs[-1]
        i, k = pl.program_id(0), pl.program_id(1)
        part = lax.dot_general(a_ref[...].astype(BF), b_ref[...].astype(BF),
                               (((1,), (1 if tb else 0,)), ((), ())), preferred_element_type=F32)
        if nk == 1:
            fn(i, part, in_refs, out_refs)
            return

        @pl.when(k == 0)
        def _():
            acc[...] = part

        @pl.when(k > 0)
        def _():
            acc[...] += part

        @pl.when(k == nk - 1)
        def _():
            fn(i, acc[...], in_refs, out_refs)

    if a_stack:
        nhb = K // 2 // tk
        a_spec = pl.BlockSpec((None, tm, tk), lambda i, k: (k // nhb, i, k % nhb))
    else:
        a_spec = pl.BlockSpec((tm, tk), lambda i, k: (i, k))
    b_spec = pl.BlockSpec((D, tk), lambda i, k: (0, k)) if tb else pl.BlockSpec((tk, D), lambda i, k: (k, 0))
    return pl.pallas_call(
        body, name=name, grid=(R // tm, nk),
        in_specs=[a_spec, b_spec] + [pl.BlockSpec(memory_space=pl.ANY)] * len(deps) + list(in_specs),
        out_specs=out_specs, out_shape=out_shape,
        scratch_shapes=[pltpu.VMEM((tm, D) if nk > 1 else (8, 128), F32)],
        compiler_params=pltpu.CompilerParams(dimension_semantics=("arbitrary", "arbitrary")),
    )(lead["a"], lead["b"], *deps, *ins)


def _rblk(tm, w=D, col=0):
    return pl.BlockSpec((tm, w), lambda i, k: (i, col))


def _rrow(w=D):
    return pl.BlockSpec((1, w), lambda i, k: (0, 0))


def down_final(f, wdn, x1, g2, fg, tgt, *, name, tm=512):
    def fn(i, d, in_refs, out_refs):
        x1_ref, g2_ref, fg_ref, t_ref = in_refs
        d_ref, dx_ref, dd_ref, dfg_ref, loss_ref = out_refs
        d_ref[...] = d
        xv = x1_ref[...] + g2_ref[...] * d
        r = lax.rsqrt(jnp.mean(xv * xv, axis=-1, keepdims=True) + EPS)
        xh = xv * r
        diff = xh * fg_ref[...] - t_ref[...]
        part = 0.5 * jnp.sum(jnp.mean(diff * diff, axis=-1, keepdims=True), axis=0, keepdims=True)
        dy = diff * (1.0 / D)
        a = dy * fg_ref[...]
        dx = r * (a - xh * jnp.mean(a * xh, axis=-1, keepdims=True))
        dx_ref[...] = dx
        dd_ref[...] = (dx * g2_ref[...]).astype(BF)
        dfg = jnp.sum(dy * xh, axis=0, keepdims=True)

        @pl.when(i == 0)
        def _():
            dfg_ref[...] = dfg
            loss_ref[...] = jnp.broadcast_to(part, (1, 128))

        @pl.when(i > 0)
        def _():
            dfg_ref[...] += dfg
            loss_ref[...] += jnp.broadcast_to(part, (1, 128))

    blk = _rblk(tm)
    return rows_call(
        dict(a=f, b=wdn, tk=DFF), [x1, g2, fg, tgt], [blk, _rrow(), _rrow(), blk],
        [jax.ShapeDtypeStruct((T, D), F32), jax.ShapeDtypeStruct((T, D), F32), jax.ShapeDtypeStruct((T, D), BF),
         jax.ShapeDtypeStruct((1, D), F32), jax.ShapeDtypeStruct((1, 128), F32)],
        [blk, blk, blk, _rrow(), _rrow(128)], fn, name=name, R=T, tm=tm)


def oproj_resid(merged, wo, x, gate, g, sc, sh, *, name, tm=512):
    def fn(i, a, in_refs, out_refs):
        x_ref, gate_ref, g_ref, sc_ref, sh_ref = in_refs
        a_ref, x1_ref, h_ref = out_refs
        a_ref[...] = a
        xv = x_ref[...] + gate_ref[...] * a
        x1_ref[...] = xv
        r = lax.rsqrt(jnp.mean(xv * xv, axis=-1, keepdims=True) + EPS)
        h_ref[...] = ((xv * r * g_ref[...]) * (1.0 + sc_ref[...]) + sh_ref[...]).astype(BF)

    blk = _rblk(tm)
    return rows_call(
        dict(a=merged, b=wo, tk=D), [x, gate, g, sc, sh], [blk, _rrow(), _rrow(), _rrow(), _rrow()],
        [jax.ShapeDtypeStruct((T, D), F32), jax.ShapeDtypeStruct((T, D), F32), jax.ShapeDtypeStruct((T, D), BF)],
        [blk, blk, blk], fn, name=name, R=T, tm=tm)


def oproj_dx_gate_bwd(da, wo, p, ya, yc, wao, wco, *, name, tm=512):
    kin = wao.shape[0]

    def fn(i, dm, in_refs, out_refs):
        ga_ref, gc_ref, ya_ref, yc_ref, wa_ref, wc_ref = in_refs
        dya_ref, dyc_ref, dp_ref, do_ref, dz_ref = out_refs
        sa, sc_ = jax.nn.sigmoid(ga_ref[...]), jax.nn.sigmoid(gc_ref[...])
        dya, dyc = (dm * sa).astype(BF), (dm * sc_).astype(BF)
        dya_ref[...] = dya
        dyc_ref[...] = dyc
        dp_ref[:, 0:D] = (dm * ya_ref[...] * (sa * (1.0 - sa))).astype(BF)
        dp_ref[:, D:2 * D] = (dm * yc_ref[...] * (sc_ * (1.0 - sc_))).astype(BF)
        nt = (((1,), (1,)), ((), ()))
        do_ref[...] = lax.dot_general(dya, wa_ref[...], nt, preferred_element_type=F32).astype(BF)
        dz_ref[...] = lax.dot_general(dyc, wc_ref[...], nt, preferred_element_type=F32)

    blk = _rblk(tm)
    sh = jax.ShapeDtypeStruct((T, D), BF)
    wsp = pl.BlockSpec((kin, D), lambda i, k: (0, 0))
    return rows_call(
        dict(a=da, b=wo, tb=True, tk=D), [p, p, ya, yc, wao, wco],
        [_rblk(tm, D, O_GA // D), _rblk(tm, D, O_GC // D), blk, blk, wsp, wsp],
        [sh, sh, jax.ShapeDtypeStruct((T, NIN), BF), jax.ShapeDtypeStruct((T, kin), BF), jax.ShapeDtypeStruct((T, kin), F32)],
        [blk, blk, _rblk(tm, 2 * D), _rblk(tm, kin), _rblk(tm, kin)], fn, name=name, R=T, tm=tm)


def normmod_bwd(x, dh, g, sc, dres, gsrc, gate, *, name, tm=512):
    R = x.shape[0]
    tm = min(tm, R)
    has_res = dres is not None
    fused = isinstance(dh, dict)
    if fused:
        tb, a_stack, tk = dh.get("tb", False), dh.get("a_stack", False), dh["tk"]
        K = 2 * dh["a"].shape[2] if a_stack else dh["a"].shape[1]
        nk = K // tk
        deps = [] if dh.get("dep") is None else [dh["dep"]]
        n_dh = 2 + len(deps)
    else:
        nk, n_dh = 1, 1

    def elementwise(i, dhv, x_ref, g_ref, sc_ref, res_refs, out_refs):
        xv = x_ref[...]
        r = lax.rsqrt(jnp.mean(xv * xv, axis=-1, keepdims=True) + EPS)
        xh = xv * r
        n = xh * g_ref[...]
        dn = dhv * (1.0 + sc_ref[...])
        a = dn * g_ref[...]
        rows = [jnp.sum(dhv, axis=0, keepdims=True), jnp.sum(dhv * n, axis=0, keepdims=True),
                jnp.sum(dn * xh, axis=0, keepdims=True)]
        if has_res:
            dres_ref, gsrc_ref, gate_ref = res_refs
            dx_ref, dxg_ref, st_ref = out_refs
            dr = dres_ref[...]
            dx = dr + r * (a - xh * jnp.mean(a * xh, axis=-1, keepdims=True))
            dx_ref[...] = dx
            dxg_ref[...] = (dx * gate_ref[...]).astype(BF)
            rows.append(jnp.sum(dr * gsrc_ref[...], axis=0, keepdims=True))
        else:
            st_ref, = out_refs
            rows.append(jnp.zeros((1, D), F32))

        @pl.when(i == 0)
        def _():
            for k, row in enumerate(rows):
                st_ref[k:k + 1, :] = row

        @pl.when(i > 0)
        def _():
            for k, row in enumerate(rows):
                st_ref[k:k + 1, :] += row

    def body(*refs):
        x_ref, dh_refs, g_ref, sc_ref = refs[0], refs[1:1 + n_dh], refs[1 + n_dh], refs[2 + n_dh]
        rest = refs[3 + n_dh:]
        res_refs, rest = (rest[:3], rest[3:]) if has_res else ((), rest)
        out_refs = rest[:3] if has_res else rest[:1]
        i = pl.program_id(0)
        if not fused:
            elementwise(i, dh_refs[0][...], x_ref, g_ref, sc_ref, res_refs, out_refs)
            return
        acc = rest[-1]
        k = pl.program_id(1)
        part = lax.dot_general(dh_refs[0][...].astype(BF), dh_refs[1][...].astype(BF),
                               (((1,), (1 if tb else 0,)), ((), ())), preferred_element_type=F32)

        @pl.when(k == 0)
        def _():
            acc[...] = part

        @pl.when(k > 0)
        def _():
            acc[...] += part

        @pl.when(k == nk - 1)
        def _():
            elementwise(i, acc[...], x_ref, g_ref, sc_ref, res_refs, out_refs)

    rowb = lambda w: pl.BlockSpec((1, w), lambda i, *k: (0, 0))
    blk = pl.BlockSpec((tm, D), lambda i, *k: (i, 0))
    st_spec = pl.BlockSpec((4, D), lambda i, *k: (0, 0))
    st_shape = jax.ShapeDtypeStruct((4, D), F32)
    if fused:
        if a_stack:
            nhb = K // 2 // tk
            a_spec = pl.BlockSpec((None, tm, tk), lambda i, k: (k // nhb, i, k % nhb))
        else:
            a_spec = pl.BlockSpec((tm, tk), lambda i, k: (i, k))
        b_spec = pl.BlockSpec((D, tk), lambda i, k: (0, k)) if tb else pl.BlockSpec((tk, D), lambda i, k: (k, 0))
        dh_specs = [a_spec, b_spec] + [pl.BlockSpec(memory_space=pl.ANY)] * len(deps)
        dh_args = [dh["a"], dh["b"]] + deps
        grid, sem = (R // tm, nk), ("arbitrary", "arbitrary")
        scratch = [pltpu.VMEM((tm, D), F32)]
    else:
        dh_specs, dh_args, grid, sem, scratch = [blk], [dh], (R // tm,), ("arbitrary",), []
    cp = pltpu.CompilerParams(dimension_semantics=sem)
    if has_res:
        return pl.pallas_call(
            body, name=name, grid=grid, in_specs=[blk] + dh_specs + [rowb(D), rowb(D), blk, blk, rowb(D)],
            out_specs=[blk, blk, st_spec], scratch_shapes=scratch,
            out_shape=[jax.ShapeDtypeStruct((R, D), F32), jax.ShapeDtypeStruct((R, D), BF), st_shape],
            compiler_params=cp,
        )(x, *dh_args, g, sc, dres, gsrc, gate)
    return pl.pallas_call(
        body, name=name, grid=grid, in_specs=[blk] + dh_specs + [rowb(D), rowb(D)],
        out_specs=st_spec, out_shape=st_shape, scratch_shapes=scratch, compiler_params=cp,
    )(x, *dh_args, g, sc)


def ffn_act_bwd(u0, df, cw, cb, *, name, tc=128):
    nb = DFF // tc

    def body(u_ref, df_ref, wg_ref, wv_ref, bg_ref, bv_ref, du_ref, dw_ref, db_ref):
        wg = [wg_ref[k:k + 1, :] for k in range(3)]
        wv = [wv_ref[k:k + 1, :] for k in range(3)]
        bg, bv = bg_ref[...], bv_ref[...]

        def chunk(r0, first, last, acc):
            xg, xv = _ext_rows(u_ref.at[0], r0, first, last), _ext_rows(u_ref.at[1], r0, first, last)
            dfe = _ext_rows(df_ref, r0, first, last)
            xg_d, xg_u, xv_d, xv_u = _roll_dn(xg), _roll_up(xg), _roll_dn(xv), _roll_up(xv)
            ug = bg + xg_d * wg[0] + xg * wg[1] + xg_u * wg[2]
            uv = bv + xv_d * wv[0] + xv * wv[1] + xv_u * wv[2]
            sig = jax.nn.sigmoid(ug)
            dug = dfe * uv * (sig * (1.0 + ug * (1.0 - sig)))
            duv = dfe * (ug * sig)
            rows = _center_rows(r0, first, last)
            du_ref[0, rows, :] = (_roll_up(dug) * wg[0] + dug * wg[1] + _roll_dn(dug) * wg[2])[_CTR].astype(BF)
            du_ref[1, rows, :] = (_roll_up(duv) * wv[0] + duv * wv[1] + _roll_dn(duv) * wv[2])[_CTR].astype(BF)
            terms = [dug * xg_d, dug * xg, dug * xg_u, dug, duv * xv_d, duv * xv, duv * xv_u, duv]
            return tuple(a + jnp.sum(t[_CTR], axis=0, keepdims=True) for a, t in zip(acc, terms))

        acc = _row_chunks(T, chunk, tuple(jnp.zeros((1, tc), F32) for _ in range(8)))
        for k in range(3):
            dw_ref[0, k:k + 1, :] = acc[k]
            dw_ref[1, k:k + 1, :] = acc[4 + k]
        db_ref[0] = acc[3]
        db_ref[1] = acc[7]

    lo = lambda r: pl.BlockSpec((r, tc), lambda j: (0, j))
    hi = lambda r: pl.BlockSpec((r, tc), lambda j: (0, nb + j))
    st = lambda r: pl.BlockSpec((2, r, tc), lambda j: (0, 0, j))
    return pl.pallas_call(
        body, name=name, grid=(nb,),
        in_specs=[st(T), lo(T), lo(3), hi(3), lo(1), hi(1)],
        out_specs=[st(T), st(3), st(1)],
        out_shape=[jax.ShapeDtypeStruct((2, T, DFF), BF), jax.ShapeDtypeStruct((2, 3, DFF), F32),
                   jax.ShapeDtypeStruct((2, 1, DFF), F32)],
        compiler_params=pltpu.CompilerParams(dimension_semantics=("parallel",)),
    )(u0, df, cw, cw, cb, cb)


def convz_bwd(p, dz, cw, cb, dp, *, name):
    o0 = O_CV // (3 * CVB)

    def body(p_ref, dz_ref, w_ref, bias_ref, dp_in, dp_ref, dw_ref, dbias_ref):
        xv, bv, cv = p_ref[:, 0:CVB], p_ref[:, CVB:2 * CVB], p_ref[:, 2 * CVB:3 * CVB]
        ci = cv * xv
        dwc = _conv(ci, w_ref, bias_ref)
        dzv = dz_ref[...]
        ddw = dzv * bv
        dci = _conv_t(ddw, w_ref)
        dp_ref[:, 0:CVB] = (dci * cv).astype(BF)
        dp_ref[:, CVB:2 * CVB] = (dzv * dwc).astype(BF)
        dp_ref[:, 2 * CVB:3 * CVB] = (dci * xv).astype(BF)
        _conv_wgrad(dw_ref, ddw, ci)
        dbias_ref[...] = jnp.sum(ddw, axis=0, keepdims=True)

    own = lambda r: pl.BlockSpec((r, CVB), lambda j: (0, j))
    return pl.pallas_call(
        body, name=name, grid=(CONV // CVB,),
        in_specs=[pl.BlockSpec((T, 3 * CVB), lambda j: (0, o0 + j)), own(T), own(3), own(1),
                  pl.BlockSpec(memory_space=pl.ANY)],
        out_specs=[pl.BlockSpec((T, 3 * CVB), lambda j: (0, o0 + j)), own(3), own(1)],
        out_shape=[jax.ShapeDtypeStruct((T, NIN), BF), jax.ShapeDtypeStruct((3, CONV), F32),
                   jax.ShapeDtypeStruct((1, CONV), F32)],
        input_output_aliases={4: 0},
        compiler_params=pltpu.CompilerParams(dimension_semantics=("parallel",)),
    )(p, dz, cw, cb, dp)


def attn_bwd(q, k, v, do, o, lse, dep, *, name, tq=1024, kc=768):
    NKC, KC = TKV // kc, kc
    deps = [] if dep is None else [dep]

    def body(q_ref, k_ref, v_ref, do_ref, o_ref, lse_ref, *rest):
        dq_ref, dk_ref, dv_ref = rest[len(deps):]
        h, i = pl.program_id(0), pl.program_id(1)

        @pl.when(i == 0)
        def _():
            dk_ref[...] = jnp.zeros_like(dk_ref)

        @pl.when((i == 0) & (h % 2 == 0))
        def _():
            dv_ref[...] = jnp.zeros_like(dv_ref)

        qv = q_ref[...]
        dom = jnp.where(_head_mask(h), do_ref[...], jnp.zeros_like(do_ref[...]))
        delta = jnp.sum(dom.astype(F32) * o_ref[...].astype(F32), axis=-1, keepdims=True)
        lse = lse_ref[:, 0:1]
        dq = jnp.zeros((tq, HP), F32)
        for c in range(NKC):
            cols = slice(c * KC, (c + 1) * KC)
            s = lax.dot_general(qv, k_ref[cols, :], (((1,), (1,)), ((), ())),
                                preferred_element_type=F32) * (SCALE * LOG2E)
            pr = jnp.exp2(s - lse)
            dp = lax.dot_general(dom, v_ref[cols, :], (((1,), (1,)), ((), ())), preferred_element_type=F32)
            ds = (pr * (dp - delta) * SCALE).astype(BF)
            dq = dq + jnp.dot(ds, k_ref[cols, :], preferred_element_type=F32)
            dk_ref[cols, :] += lax.dot_general(ds, qv, (((0,), (0,)), ((), ())), preferred_element_type=F32)
            dv_ref[cols, :] += lax.dot_general(pr.astype(BF), dom, (((0,), (0,)), ((), ())), preferred_element_type=F32)
        dq_ref[...] = dq

    return pl.pallas_call(
        body, name=name, grid=(NH, T // tq),
        in_specs=[pl.BlockSpec((tq, HP), lambda h, i: (i, h)), pl.BlockSpec((TKV, HP), lambda h, i: (0, h)),
                  pl.BlockSpec((TKV, 2 * DV), lambda h, i: (0, h // 2)), pl.BlockSpec((tq, 2 * DV), lambda h, i: (i, h // 2)),
                  pl.BlockSpec((tq, 2 * DV), lambda h, i: (i, h // 2)), pl.BlockSpec((tq, HP), lambda h, i: (i, h)),
                  *([pl.BlockSpec(memory_space=pl.ANY)] * len(deps))],
        out_specs=[pl.BlockSpec((tq, HP), lambda h, i: (i, h)), pl.BlockSpec((TKV, HP), lambda h, i: (0, h)),
                   pl.BlockSpec((TKV, 2 * DV), lambda h, i: (0, h // 2))],
        out_shape=[jax.ShapeDtypeStruct((T, NH * HP), F32), jax.ShapeDtypeStruct((TKV, NH * HP), F32),
                   jax.ShapeDtypeStruct((TKV, NH * DV), F32)],
        compiler_params=pltpu.CompilerParams(dimension_semantics=("arbitrary", "arbitrary")),
    )(q, k, v, do, o, lse, *deps)


def qprep_bwd(p, dq, qg, wq2, cq_t, sq_t, dp, *, name, tm=256):
    qcol = O_Q // 512

    def body(p_ref, dq_ref, g_ref, w_ref, c_ref, s_ref, dp_in, dp_ref, dq2_ref, dg_ref):
        i = pl.program_id(0)
        dqv = dq_ref[...]
        cc = jnp.concatenate([c_ref[...]] * NH, axis=1)
        ss = jnp.concatenate([s_ref[...]] * NH, axis=1)
        dq2 = jnp.concatenate([dqv * cc, dqv * ss], axis=1).astype(BF)
        dq2_ref[...] = dq2
        dcq = lax.dot_general(dq2, w_ref[...], (((1,), (1,)), ((), ())), preferred_element_type=F32)
        pq = p_ref[...]
        r = lax.rsqrt(jnp.sum(pq * pq, axis=-1, keepdims=True) * (1.0 / QL) + EPS)
        xh = pq * r
        a = dcq * g_ref[...]
        dp_ref[...] = (r * (a - xh * (jnp.sum(a * xh, axis=-1, keepdims=True) * (1.0 / QL)))).astype(BF)
        dg = jnp.sum(dcq * xh, axis=0, keepdims=True)

        @pl.when(i == 0)
        def _():
            dg_ref[...] = dg

        @pl.when(i > 0)
        def _():
            dg_ref[...] += dg

    return pl.pallas_call(
        body, name=name, grid=(T // tm,),
        in_specs=[pl.BlockSpec((tm, 512), lambda i: (i, qcol)), pl.BlockSpec((tm, NH * HP), lambda i: (i, 0)), _row(512),
                  pl.BlockSpec((512, 2 * NH * HP), lambda i: (0, 0)),
                  pl.BlockSpec((tm, HP), lambda i: (i, 0)), pl.BlockSpec((tm, HP), lambda i: (i, 0)),
                  pl.BlockSpec(memory_space=pl.ANY)],
        out_specs=[pl.BlockSpec((tm, 512), lambda i: (i, qcol)), pl.BlockSpec((tm, 2 * NH * HP), lambda i: (i, 0)), _row(512)],
        out_shape=[jax.ShapeDtypeStruct((T, NIN), BF), jax.ShapeDtypeStruct((T, 2 * NH * HP), BF),
                   jax.ShapeDtypeStruct((1, 512), F32)],
        input_output_aliases={6: 0},
        compiler_params=pltpu.CompilerParams(dimension_semantics=("arbitrary",)),
    )(p, dq, qg, wq2, cq_t, sq_t, dp)


def kvprep_bwd(pc, p, dk, dv, kvg, wkv2, ck, sk, dp, *, name, tm=256):
    assert tm == TC
    nb = TKV // tm
    kvcol = O_KV // 512

    def body(pc_ref, p_ref, dk_ref, dv_ref, g_ref, w_ref, ck_ref, sk_ref, dp_in, dp_ref, dpc_ref, dkv2_ref, dg_ref):
        i = pl.program_id(0)
        t = jnp.where(i == NLAT, pc_ref[...], p_ref[...])
        pk = t[:, :KVL]
        r = lax.rsqrt(jnp.mean(pk * pk, axis=-1, keepdims=True) + EPS)
        xh = pk * r
        dkv = dk_ref[...]
        dkv2 = jnp.concatenate([dkv, dv_ref[...]], axis=1).astype(BF)
        dkv2_ref[...] = dkv2
        dckv = lax.dot_general(dkv2, w_ref[...], (((1,), (1,)), ((), ())), preferred_element_type=F32)
        a = dckv * g_ref[...]
        dpk = r * (a - xh * jnp.mean(a * xh, axis=-1, keepdims=True))
        dkr = dkv[:, 0:HP]
        for hh in range(1, NH):
            dkr = dkr + dkv[:, hh * HP:(hh + 1) * HP]
        res = jnp.concatenate([dpk, dkr * ck_ref[...], dkr * sk_ref[...]], axis=1).astype(BF)
        dg = jnp.sum(dckv * xh, axis=0, keepdims=True)

        @pl.when(i == 0)
        def _():
            dg_ref[...] = dg

        @pl.when(i > 0)
        def _():
            dg_ref[...] += dg

        @pl.when(i < NLAT)
        def _():
            dp_ref[...] = res

        @pl.when(i == NLAT)
        def _():
            dpc_ref[...] = res

    rb = lambda w: pl.BlockSpec((tm, w), lambda i: (i, 0))
    return pl.pallas_call(
        body, name=name, grid=(nb,),
        in_specs=[pl.BlockSpec((tm, 512), lambda i: (0, 0)),
                  pl.BlockSpec((tm, 512), lambda i: (jnp.minimum(i, NLAT - 1), kvcol)),
                  rb(NH * HP), rb(NH * DV), _row(KVL), pl.BlockSpec((KVL, NH * HP + NH * DV), lambda i: (0, 0)),
                  rb(HP), rb(HP), pl.BlockSpec(memory_space=pl.ANY)],
        out_specs=[pl.BlockSpec((tm, 512), lambda i: (jnp.minimum(i, NLAT - 1), kvcol)),
                   pl.BlockSpec((tm, 512), lambda i: (0, 0)), rb(NH * HP + NH * DV), _row(KVL)],
        out_shape=[jax.ShapeDtypeStruct((T, NIN), BF), jax.ShapeDtypeStruct((TC, 512), BF),
                   jax.ShapeDtypeStruct((TKV, NH * HP + NH * DV), BF), jax.ShapeDtypeStruct((1, KVL), F32)],
        input_output_aliases={8: 0},
        compiler_params=pltpu.CompilerParams(dimension_semantics=("arbitrary",)),
    )(pc, p, dk, dv, kvg, wkv2, ck, sk, dp)


def _pieces(src, width, n):
    out, c = [], src
    while c < src + width:
        k = c // n
        w = min(src + width, (k + 1) * n) - c
        out.append((k, c - k * n, c - src, w))
        c += w
    return out


def _win_moves():
    mv = [(2208, 1024, O_GA), (3232, 1024, O_GC), (0, KVL, O_KV), (256, DR, O_KV + KVL + DN), (288, QL, O_Q)]
    mv += [(256 + _swap_start(g), 8, O_KV + KVL + HP + DN + 8 * g) for g in range(4)]
    for j in range(CONV // CVB):
        base = O_CV + 3 * CVB * j
        mv += [(672 + CVB * j, CVB, base), (1184 + CVB * j, CVB, base + CVB), (1696 + CVB * j, CVB, base + 2 * CVB)]
    return mv


_WIN_ZERO = [(O_KV + KVL, DN), (O_KV + KVL + DN + DR, HP - DN - DR), (O_KV + KVL + HP, DN),
             (O_KV + KVL + HP + DN + DR, HP - DN - DR), (O_Q + QL, 512 - QL)]


def build_win(g, *, name, tm=256):
    def body(g_ref, o_ref):
        for src, w, dst in _win_moves():
            for k, a, off, pw in _pieces(src, w, SH_IN):
                o_ref[:, dst + off:dst + off + pw] = g_ref[k, :, a:a + pw]
        for c0, w in _WIN_ZERO:
            o_ref[:, c0:c0 + w] = jnp.zeros((tm, w), o_ref.dtype)

    return pl.pallas_call(
        body, name=name, grid=(D // tm,), in_specs=[pl.BlockSpec((NDEV, tm, SH_IN), lambda i: (0, i, 0))],
        out_specs=pl.BlockSpec((tm, NIN), lambda i: (i, 0)), out_shape=jax.ShapeDtypeStruct((D, NIN), g.dtype),
        compiler_params=pltpu.CompilerParams(dimension_semantics=("parallel",)),
    )(g)


def shard_win_grad(dwt, dwct, *, name, tc=256):
    def body(dw_ref, dwc_ref, o_ref, kvs):
        kvs[...] = dw_ref[O_KV:O_KV + 512, :] + dwc_ref[...]

        def src(row, w):
            if O_KV <= row < O_KV + 512:
                return kvs[row - O_KV:row - O_KV + w, :]
            return dw_ref[row:row + w, :]

        for s, w, dst in _win_moves():
            if w == 8 or s == 256:
                continue
            for k, a, off, pw in _pieces(s, w, SH_IN):
                o_ref[k, a:a + pw, :] = src(dst + off, pw).astype(o_ref.dtype)
        for g in range(4):
            val = src(O_KV + KVL + DN + 8 * g, 8) + src(O_KV + KVL + HP + DN + _swap_start(g), 8)
            o_ref[0, 256 + 8 * g:256 + 8 * g + 8, :] = val.astype(o_ref.dtype)

    return pl.pallas_call(
        body, name=name, grid=(D // tc,),
        in_specs=[pl.BlockSpec((NIN, tc), lambda j: (0, j)), pl.BlockSpec((512, tc), lambda j: (0, j))],
        out_specs=pl.BlockSpec((NDEV, SH_IN, tc), lambda j: (0, 0, j)),
        out_shape=jax.ShapeDtypeStruct((NDEV, SH_IN, D), BF),
        scratch_shapes=[pltpu.VMEM((512, tc), F32)],
        compiler_params=pltpu.CompilerParams(dimension_semantics=("parallel",)),
    )(dwt, dwct)


def build_wq_wkv(gq, gkv, *, name):
    def body(gq_ref, gkv_ref, q_ref, kv_ref):
        q_ref[...] = jnp.zeros_like(q_ref)
        kv_ref[...] = jnp.zeros_like(kv_ref)
        for h in range(NH):
            q_ref[0:QL, h * HP:h * HP + DN + DR] = gq_ref[h]
            for g in range(4):
                c0 = NH * HP + h * HP + DN + 8 * g
                q_ref[0:QL, c0:c0 + 8] = gq_ref[h, :, DN + _swap_start(g):DN + _swap_start(g) + 8]
            kv_ref[:, h * HP:h * HP + DN] = gkv_ref[h, :, 0:DN]
            kv_ref[:, NH * HP + h * DV:NH * HP + (h + 1) * DV] = gkv_ref[h, :, DN:DN + DV]

    vm = pl.BlockSpec(memory_space=pltpu.VMEM)
    return pl.pallas_call(
        body, name=name, in_specs=[vm, vm], out_specs=[vm, vm],
        out_shape=[jax.ShapeDtypeStruct((512, 2 * NH * HP), gq.dtype), jax.ShapeDtypeStruct((KVL, NH * HP + NH * DV), gq.dtype)],
    )(gq, gkv)


def shard_wq_wkv_grad(dwq2, dwkv2, *, name):
    def body(q_ref, kv_ref, gq_ref, gkv_ref):
        for h in range(NH):
            gq_ref[h, :, 0:DN] = q_ref[0:QL, h * HP:h * HP + DN].astype(BF)
            for g in range(4):
                a = q_ref[0:QL, h * HP + DN + 8 * g:h * HP + DN + 8 * g + 8]
                c0 = NH * HP + h * HP + DN + _swap_start(g)
                gq_ref[h, :, DN + 8 * g:DN + 8 * g + 8] = (a + q_ref[0:QL, c0:c0 + 8]).astype(BF)
            gkv_ref[h, :, 0:DN] = kv_ref[:, h * HP:h * HP + DN].astype(BF)
            gkv_ref[h, :, DN:DN + DV] = kv_ref[:, NH * HP + h * DV:NH * HP + (h + 1) * DV].astype(BF)

    vm = pl.BlockSpec(memory_space=pltpu.VMEM)
    return pl.pallas_call(
        body, name=name, in_specs=[vm, vm], out_specs=[vm, vm],
        out_shape=[jax.ShapeDtypeStruct((NDEV, QL, (DN + DR)), BF), jax.ShapeDtypeStruct((NDEV, KVL, DN + DV), BF)],
    )(dwq2, dwkv2)


def unshard_cols(g, *, name, tm=256):
    _, K, n = g.shape
    tm = _pick(K, tm, 16)

    def body(g_ref, o_ref):
        for k in range(NDEV):
            o_ref[:, k * n:(k + 1) * n] = g_ref[k]

    return pl.pallas_call(
        body, name=name, grid=(K // tm,), in_specs=[pl.BlockSpec((NDEV, tm, n), lambda i: (0, i, 0))],
        out_specs=pl.BlockSpec((tm, NDEV * n), lambda i: (i, 0)), out_shape=jax.ShapeDtypeStruct((K, NDEV * n), g.dtype),
        compiler_params=pltpu.CompilerParams(dimension_semantics=("parallel",)),
    )(g)


def shard_cols(w, *, name, tm=256):
    K, n8 = w.shape
    n = n8 // NDEV
    tm = _pick(K, tm, 16)

    def body(w_ref, o_ref):
        for k in range(NDEV):
            o_ref[k] = w_ref[:, k * n:(k + 1) * n]

    return pl.pallas_call(
        body, name=name, grid=(K // tm,), in_specs=[pl.BlockSpec((tm, n8), lambda i: (i, 0))],
        out_specs=pl.BlockSpec((NDEV, tm, n), lambda i: (0, i, 0)), out_shape=jax.ShapeDtypeStruct((NDEV, K, n), w.dtype),
        compiler_params=pltpu.CompilerParams(dimension_semantics=("parallel",)),
    )(w)


def _rope_tables():
    t = np.arange(T)
    row = (t // GRID_W).astype(np.float32)
    col = (t % GRID_W).astype(np.float32)
    axis_dim = DR // 2
    inv = (np.float32(ROPE_THETA) ** (-np.arange(0, axis_dim, 2, dtype=np.float32) / np.float32(axis_dim))).astype(np.float32)
    ar, ac = (row[:, None] * inv).astype(np.float32), (col[:, None] * inv).astype(np.float32)
    cosv = np.concatenate([np.cos(ar), np.cos(ar), np.cos(ac), np.cos(ac)], axis=1).astype(np.float32)
    sinv = np.concatenate([-np.sin(ar), np.sin(ar), -np.sin(ac), np.sin(ac)], axis=1).astype(np.float32)
    ck = np.zeros((TKV, HP), np.float32)
    sk = np.zeros((TKV, HP), np.float32)
    ck[T:, DN:DN + DR] = 1.0
    ck[:T, DN:DN + DR] = cosv
    sk[:T, DN:DN + DR] = sinv
    cq = np.zeros((T, HP), np.float32)
    cq[:, :DN] = 1.0
    cq[:, DN:DN + DR] = cosv
    return jnp.asarray(ck), jnp.asarray(sk), jnp.asarray(cq), jnp.asarray(sk[:T])


def _local_step(x, ctx, tgt, mod_lat, mod_ctx, n1g, qg, kvg, n2g, fg, conv_w, conv_b, ffn_w, ffn_b, get_w, put_g, dep0):
    sh1, sc1, g1, sh2, sc2, g2 = [mod_lat[:, i * D:(i + 1) * D] for i in range(6)]
    csh1, csc1 = mod_ctx[:, 0:D], mod_ctx[:, D:2 * D]
    ck, sk, cq_t, sq_t = _rope_tables()
    qg_p = jnp.pad(qg, ((0, 0), (0, 512 - QL)))

    hcat = normmod_cat(ctx, x, n1g, csc1, csh1, sc1, sh1, dep0, name="normmod1")
    win = get_w("in", hcat)
    p = mm(hcat, win, M=T, tn=768, name="in_proj")
    pc = mm(hcat, win, M=TC, N=512, a_off=(T, 0), b_off=(0, O_KV), name="in_proj_ctx")
    wq2, wkv2, wao, wco, wo = get_w("mid", p)
    kh, vh, ckv = kvprep(pc, p, kvg, wkv2, ck, sk, name="kvprep")
    qr, cq = qprep(p, qg_p, wq2, cq_t, sq_t, name="qprep")
    o, lse = attn_fwd(qr, kh, vh, name="attn_fwd")
    z = convz(p, conv_w, conv_b, name="convz")
    ya, yc, merged = out_proj_merge(o, wao, z, wco, p, name="attn_conv_out_gate_merge")
    a_out, x1, h2 = oproj_resid(merged, wo, x, g1, n2g, sc2, sh2, name="o_proj_resid_normmod2")
    wup = get_w("up", h2)
    u0 = mm(h2, wup, tb=True, o_stack=True, tn=1408, name="up_proj")
    f = ffn_act(u0, ffn_w, ffn_b, name="ffn_act")
    wdn = get_w("down", f)
    dn, dx2, dd, dfg, loss = down_final(f, wdn, x1, g2, fg, tgt, name="down_proj_final_loss")

    df = mm(dd, wdn, tb=True, tn=1408, name="down_proj_dx")
    dwdn = mm(f, dd, ta=True, out_dtype=BF, tm=1408, name="down_proj_dw")
    du0, dffn_w, dffn_b = ffn_act_bwd(u0, df, ffn_w, ffn_b, name="ffn_act_bwd")
    dwup = mm(du0, h2, ta=True, a_stack=True, out_dtype=BF, tm=1408, name="up_proj_dw")
    tok = put_g("ffn", dict(dwup=dwup, dwdn=dwdn))
    dx1, da, st2 = normmod_bwd(x1, dict(a=du0, b=wup, a_stack=True, tk=1408, dep=tok), n2g, sc2, dx2, dn, g1,
                               name="up_proj_dx_normmod2_bwd")

    dwo = mm(merged, da, ta=True, out_dtype=BF, tn=512, name="o_proj_dw")
    dya, dyc, dp, do, dz = oproj_dx_gate_bwd(da, wo, p, ya, yc, wao, wco, name="o_proj_dx_gate_merge_bwd")
    dwao = mm(o, dya, ta=True, out_dtype=BF, tn=512, name="attn_out_dw")
    dwco = mm(z, dyc, ta=True, out_dtype=BF, tn=512, name="conv_out_dw")
    tok = put_g("mid", dict(dwao=dwao, dwco=dwco, dwo=dwo))
    dp, dconv_w, dconv_b = convz_bwd(p, dz, conv_w, conv_b, dp, name="convz_bwd")
    dq, dk, dv = attn_bwd(qr, kh, vh, do, o, lse, tok, name="attn_bwd")
    dp, dq2, dqg = qprep_bwd(p, dq, qg_p, wq2, cq_t, sq_t, dp, name="qprep_bwd")
    dwq2 = mm(cq, dq2, ta=True, name="q_up_dw")
    dp, dpc, dkv2, dkvg = kvprep_bwd(pc, p, dk, dv, kvg, wkv2, ck, sk, dp, name="kvprep_bwd")
    dwkv2 = mm(ckv, dkv2, ta=True, name="kv_up_dw")
    tok = put_g("qkv", dict(dwq2=dwq2, dwkv2=dwkv2))

    dwin = mm(dp, hcat, ta=True, K=T, tm=768, dep=tok, name="in_proj_dw")
    dwin_c = mm(dpc, hcat, ta=True, K=TC, b_off=(T, 0), name="in_proj_ctx_dw")
    tok = put_g("in", dict(dwin=dwin, dwin_c=dwin_c))
    dhc = mm(dpc, win, tb=True, N=D, K=512, b_off=(0, O_KV), name="in_proj_ctx_dx")
    dx, _, st1 = normmod_bwd(x, dict(a=dp, b=win, tb=True, tk=1536, dep=tok), n1g, sc1, dx1, a_out, g1,
                             name="in_proj_dx_normmod1_bwd")
    stc = normmod_bwd(ctx, dhc, n1g, csc1, None, None, None, name="normmod1_ctx_bwd")

    zrow = jnp.zeros((1, D), F32)
    dmod_lat = jnp.concatenate([st1[0:1], st1[1:2], st1[3:4], st2[0:1], st2[1:2], st2[3:4]], axis=1)
    dmod_ctx = jnp.concatenate([stc[0:1], stc[1:2], zrow, zrow, zrow, zrow], axis=1)
    return dict(
        loss=loss, dx=dx, dmod_lat=dmod_lat, dmod_ctx=dmod_ctx,
        dn1g=st1[2:3] + stc[2:3], dqg=dqg, dkvg=dkvg, dn2g=st2[2:3], dfg=dfg,
        dconv_w=dconv_w, dconv_b=dconv_b, dffn_w=dffn_w, dffn_b=dffn_b)


def _me():
    x, y, c = lax.axis_index("x"), lax.axis_index("y"), lax.axis_index("c")
    return x, y, c, 4 * x + 2 * y + c


def _peer(x, y, c, k):
    px = 1 - x if k & 4 else x
    py = 1 - y if k & 2 else y
    pc = 1 - c if k & 1 else c
    return (px, py, pc), 4 * px + 2 * py + pc


def _exchange_tiles(src_of_peer, buf, send_sem, recv_sem):
    x, y, c, me = _me()
    for k in range(1, NDEV):
        dev, lin = _peer(x, y, c, k)
        pltpu.make_async_remote_copy(src_ref=src_of_peer(lin), dst_ref=buf.at[me], send_sem=send_sem, recv_sem=recv_sem,
                                     device_id=dev, device_id_type=MESH).start()
    seven = buf.at[pl.ds(0, NDEV - 1)]
    pltpu.make_async_remote_copy(src_ref=seven, dst_ref=seven, send_sem=send_sem, recv_sem=recv_sem,
                                 device_id=(x, y, c), device_id_type=MESH).wait()


def _silu(z):
    return z * jax.nn.sigmoid(z)


def ada_fwd(c, c_ctx, ffn_w, conv_w, w_shard, b_shard, deps, *, name):
    nsh = w_shard.shape[1]
    deps = [d for d in deps if d is not None]

    def body(c_ref, cc_ref, fw_ref, cw_ref, w_ref, b_ref, *rest):
        s_ref, m_ref, mine, res, sems = rest[len(deps):]
        x, y, c, me = _me()
        mine[0:1, :] = _silu(c_ref[...])
        mine[1:2, :] = _silu(cc_ref[...])
        mine[2:5, :] = fw_ref[...]
        mine[5:8, :] = cw_ref[...]
        s_ref[me] = mine[...]
        _exchange_tiles(lambda lin: mine, s_ref, sems.at[0], sems.at[1])
        sall = s_ref[...].reshape(NDEV * 8, D).astype(BF)
        r = jnp.dot(sall, w_ref[...].astype(BF), preferred_element_type=F32) + b_ref[...]
        res[...] = r.reshape(NDEV, 8, nsh)
        m_ref[me] = res[me]
        _exchange_tiles(lambda lin: res.at[lin], m_ref, sems.at[2], sems.at[3])

    vm = pl.BlockSpec(memory_space=pltpu.VMEM)
    return pl.pallas_call(
        body, name=name, in_specs=[vm] * 6 + [pl.BlockSpec(memory_space=pl.ANY)] * len(deps), out_specs=[vm, vm],
        out_shape=[jax.ShapeDtypeStruct((NDEV, 8, D), F32), jax.ShapeDtypeStruct((NDEV, 8, nsh), F32)],
        scratch_shapes=[pltpu.VMEM((8, D), F32), pltpu.VMEM((NDEV, 8, nsh), F32), pltpu.SemaphoreType.DMA((4,))],
    )(c, c_ctx, ffn_w, conv_w, w_shard, b_shard, *deps)


P_DML, P_DMC, P_N1, P_QG, P_KVG, P_CB, P_N2, P_FB, P_FG, P_CW, P_FW, P_LOSS, P_ROWS = 0, 6, 12, 13, 14, 15, 16, 17, 23, 24, 27, 45, 48
FROWS = 3


def sync_small(r, deps, *, name):
    ins = [r["dmod_lat"], r["dmod_ctx"], r["dn1g"], r["dqg"], r["dkvg"], r["dconv_b"], r["dn2g"], r["dffn_b"], r["dfg"],
           r["dconv_w"], r["dffn_w"], r["loss"]]

    def put_wide(p, row0, row, n):
        for j in range(-(-n // D)):
            w = min(D, n - j * D)
            p[row0 + j:row0 + j + 1, 0:w] = row[:, j * D:j * D + w]

    def body(dml, dmc, n1, qg, kvg, cb, n2, fb, fg, cw, fw, loss, *rest):
        a_ref, sum_ref, p, sems = rest[len(deps):]
        x, y, c, me = _me()
        p[...] = jnp.zeros_like(p)
        put_wide(p, P_DML, dml, 6 * D)
        put_wide(p, P_DMC, dmc, 6 * D)
        put_wide(p, P_N1, n1, D)
        put_wide(p, P_QG, qg, 512)
        put_wide(p, P_KVG, kvg, KVL)
        put_wide(p, P_CB, cb, CONV)
        put_wide(p, P_N2, n2, D)
        put_wide(p, P_FG, fg, D)
        put_wide(p, P_LOSS, loss, 128)
        for s in range(2):
            put_wide(p, P_FB + FROWS * s, fb.at[s], DFF)
        for k in range(3):
            put_wide(p, P_CW + k, cw.at[k:k + 1], CONV)
            for s in range(2):
                put_wide(p, P_FW + FROWS * (2 * k + s), fw.at[s, k:k + 1], DFF)
        a_ref[me] = p[...]
        _exchange_tiles(lambda lin: p, a_ref, sems.at[0], sems.at[1])
        acc = a_ref[0]
        for k in range(1, NDEV):
            acc = acc + a_ref[k]
        sum_ref[...] = acc

    vm = pl.BlockSpec(memory_space=pltpu.VMEM)
    return pl.pallas_call(
        body, name=name, in_specs=[vm] * len(ins) + [pl.BlockSpec(memory_space=pl.ANY)] * len(deps), out_specs=[vm, vm],
        out_shape=[jax.ShapeDtypeStruct((NDEV, P_ROWS, D), F32), jax.ShapeDtypeStruct((P_ROWS, D), F32)],
        scratch_shapes=[pltpu.VMEM((P_ROWS, D), F32), pltpu.SemaphoreType.DMA((2,))],
    )(*ins, *deps)


def ada_bwd(s_all, dml, dmc, w_shard, c_ctx, *, name):
    nsh = w_shard.shape[1]

    def body(s_ref, dml_ref, dmc_ref, w_ref, c_ref, dw_ref, gc_ref, s16, dm16, part, buf, sems):
        x, y, c, me = _me()
        s16[...] = jnp.zeros_like(s16)
        dm16[...] = jnp.zeros_like(dm16)
        for k in range(NDEV):
            s16[k:k + 1, :] = s_ref[k, 0:1, :]
        s16[8:9, :] = s_ref[0, 1:2, :]
        dm16[0:8, :] = dml_ref[...]
        dm16[8:9, :] = dmc_ref[...]
        dw_ref[...] = lax.dot_general(s16[...].astype(BF), dm16[...].astype(BF), (((0,), (0,)), ((), ())),
                                      preferred_element_type=F32)
        part[...] = lax.dot_general(dm16[8:16, :].astype(BF), w_ref[...].astype(BF), (((1,), (1,)), ((), ())),
                                    preferred_element_type=F32)
        buf[me] = part[...]
        _exchange_tiles(lambda lin: part, buf, sems.at[0], sems.at[1])
        acc = buf[0]
        for k in range(1, NDEV):
            acc = acc + buf[k]
        z = c_ref[...]
        sg = jax.nn.sigmoid(z)
        gc_ref[...] = acc * (sg * (1.0 + z * (1.0 - sg)))

    vm = pl.BlockSpec(memory_space=pltpu.VMEM)
    return pl.pallas_call(
        body, name=name, in_specs=[vm] * 5, out_specs=[vm, vm],
        out_shape=[jax.ShapeDtypeStruct((D, nsh), F32), jax.ShapeDtypeStruct((8, D), F32)],
        scratch_shapes=[pltpu.VMEM((16, D), F32), pltpu.VMEM((16, nsh), F32), pltpu.VMEM((8, D), F32),
                        pltpu.VMEM((NDEV, 8, D), F32), pltpu.SemaphoreType.DMA((2,))],
    )(s_all, dml, dmc, w_shard, c_ctx)


HBM_SPEC = pl.BlockSpec(memory_space=pltpu.HBM)
SEM_SPEC = pl.BlockSpec(memory_space=pltpu.SEMAPHORE)
EFFECT = pltpu.SideEffectType.DATAFLOW_SIDE_EFFECTING


ALL_PEERS = tuple(range(1, NDEV))
FIRST_HOP = (1, 2, 4, 6)
RELAY = (2, 4, 6)


def _exchange_copies(srcs, lands, send, recv, per_peer, peers):
    x, y, c, me = _me()
    n = len(peers)
    cps = []
    for t in range(len(srcs)):
        for j, k in enumerate(peers):
            dev, lin = _peer(x, y, c, k)
            cps.append(pltpu.make_async_remote_copy(
                src_ref=srcs[t].at[lin] if per_peer else srcs[t], dst_ref=lands[t].at[me],
                send_sem=send.at[n * t + j], recv_sem=recv.at[n * t + j], device_id=dev, device_id_type=MESH))
    return cps


def _relay_copies(lands, send, recv):
    x, y, c, me = _me()
    n = len(RELAY)
    cps = []
    for t in range(len(lands)):
        for j, k in enumerate(RELAY):
            slot = lands[t].at[_peer(x, y, c, k)[1]]
            cps.append(pltpu.make_async_remote_copy(
                src_ref=slot, dst_ref=slot, send_sem=send.at[n * t + j], recv_sem=recv.at[n * t + j],
                device_id=(x, y, 1 - c), device_id_type=MESH))
    return cps


def _own_copies(srcs, lands, own, per_peer):
    me = _me()[3]
    return [pltpu.make_async_copy(srcs[t].at[me] if per_peer else srcs[t], lands[t].at[me], own.at[t])
            for t in range(len(srcs))]


def exchange_start(srcs, *, per_peer, name, dep=None, peers=ALL_PEERS):
    nt = len(srcs)
    ns = len(peers) * nt
    land_shapes = [(a.shape if per_peer else (NDEV,) + a.shape) for a in srcs]
    deps = [] if dep is None else [dep]

    def body(*refs):
        src, land = refs[:nt], refs[nt:2 * nt]
        send, recv, own = refs[2 * nt + len(deps):2 * nt + len(deps) + 3]
        for cp in _exchange_copies(src, land, send, recv, per_peer, peers) + _own_copies(src, land, own, per_peer):
            cp.start()
        refs[-1][...] = jnp.zeros_like(refs[-1])

    hb = lambda a: pltpu.with_memory_space_constraint(a, pltpu.HBM)
    outs = pl.pallas_call(
        body, name=name,
        out_shape=(pltpu.SemaphoreType.DMA((ns,)), pltpu.SemaphoreType.DMA((ns,)), pltpu.SemaphoreType.DMA((nt,)),
                   *[pltpu.HBM(a.shape, a.dtype) for a in srcs], *[pltpu.HBM(s, a.dtype) for s, a in zip(land_shapes, srcs)],
                   jax.ShapeDtypeStruct((8, 128), F32)),
        in_specs=[HBM_SPEC] * (2 * nt) + [pl.BlockSpec(memory_space=pl.ANY)] * len(deps),
        out_specs=(SEM_SPEC, SEM_SPEC, SEM_SPEC, *([HBM_SPEC] * (2 * nt)), pl.BlockSpec(memory_space=pltpu.VMEM)),
        input_output_aliases={i: 3 + i for i in range(2 * nt)},
        compiler_params=pltpu.CompilerParams(has_side_effects=EFFECT),
    )(*[hb(a) for a in srcs], *[hb(lax.empty(s, a.dtype)) for s, a in zip(land_shapes, srcs)], *deps)
    return dict(send=outs[0], recv=outs[1], own=outs[2], src=list(outs[3:3 + nt]), land=list(outs[3 + nt:3 + 2 * nt]),
                token=outs[-1], per_peer=per_peer, peers=peers)


def exchange_wait(h, after, *, name):
    nt = len(h["src"])
    per_peer, peers = h["per_peer"], h["peers"]

    def body(*refs):
        src, land, send, recv, own = refs[:nt], refs[nt:2 * nt], refs[2 * nt], refs[2 * nt + 1], refs[2 * nt + 2]
        for cp in _exchange_copies(src, land, send, recv, per_peer, peers):
            cp.wait_send()
            cp.wait_recv()
        for cp in _own_copies(src, land, own, per_peer):
            cp.wait()

    outs = pl.pallas_call(
        body, name=name,
        out_shape=(*[pltpu.HBM(a.shape, a.dtype) for a in h["src"]], *[pltpu.HBM(a.shape, a.dtype) for a in h["land"]]),
        in_specs=[HBM_SPEC] * (2 * nt) + [SEM_SPEC, SEM_SPEC, SEM_SPEC, pl.BlockSpec(memory_space=pl.ANY)],
        out_specs=tuple([HBM_SPEC] * (2 * nt)),
        input_output_aliases={i: i for i in range(2 * nt)},
        compiler_params=pltpu.CompilerParams(has_side_effects=EFFECT),
    )(*h["src"], *h["land"], h["send"], h["recv"], h["own"], after)
    return list(outs[nt:])


def relay_start(lands, *, name):
    nt = len(lands)
    ns = len(RELAY) * nt

    def body(*refs):
        for cp in _relay_copies(refs[:nt], refs[nt], refs[nt + 1]):
            cp.start()

    outs = pl.pallas_call(
        body, name=name,
        out_shape=(pltpu.SemaphoreType.DMA((ns,)), pltpu.SemaphoreType.DMA((ns,)),
                   *[pltpu.HBM(a.shape, a.dtype) for a in lands]),
        in_specs=[HBM_SPEC] * nt, out_specs=(SEM_SPEC, SEM_SPEC, *([HBM_SPEC] * nt)),
        input_output_aliases={i: 2 + i for i in range(nt)},
        compiler_params=pltpu.CompilerParams(has_side_effects=EFFECT),
    )(*lands)
    return dict(send=outs[0], recv=outs[1], land=list(outs[2:]))


def relay_wait(h, *, name):
    nt = len(h["land"])

    def body(*refs):
        for cp in _relay_copies(refs[:nt], refs[nt], refs[nt + 1]):
            cp.wait_send()
            cp.wait_recv()

    outs = pl.pallas_call(
        body, name=name, out_shape=tuple(pltpu.HBM(a.shape, a.dtype) for a in h["land"]),
        in_specs=[HBM_SPEC] * nt + [SEM_SPEC, SEM_SPEC], out_specs=tuple([HBM_SPEC] * nt),
        input_output_aliases={i: i for i in range(nt)},
        compiler_params=pltpu.CompilerParams(has_side_effects=EFFECT),
    )(*h["land"], h["send"], h["recv"])
    return list(outs)


def _adamw_math(w, g, m, v):
    nm = B1 * m + (1.0 - B1) * g
    nv = B2 * v + (1.0 - B2) * (g * g)
    m_hat = nm / (1.0 - B1 ** STEP)
    v_hat = nv / (1.0 - B2 ** STEP)
    return -LR * (m_hat / (jnp.sqrt(v_hat) + AEPS) + WD * w), nm, nv


def adamw_many(ws, gs, ms, vs, *, name):
    n = len(ws)

    def body(*refs):
        for k in range(n):
            d, nm, nv = _adamw_math(refs[k][...], refs[n + k][...], refs[2 * n + k][...], refs[3 * n + k][...])
            refs[4 * n + k][...] = d
            refs[5 * n + k][...] = nm
            refs[6 * n + k][...] = nv

    vm = pl.BlockSpec(memory_space=pltpu.VMEM)
    sh = [jax.ShapeDtypeStruct(w.shape, F32) for w in ws]
    outs = pl.pallas_call(body, name=name, in_specs=[vm] * (4 * n), out_specs=[vm] * (3 * n), out_shape=sh * 3,
                          )(*ws, *gs, *ms, *vs)
    return outs[:n], outs[n:2 * n], outs[2 * n:]


def adamw(w, g, m, v, *, name, tr=256):
    R, C = w.shape
    tr = _pick(R, tr, 8)

    def body(w_ref, g_ref, m_ref, v_ref, d_ref, nm_ref, nv_ref):
        d_ref[...], nm_ref[...], nv_ref[...] = _adamw_math(w_ref[...], g_ref[...], m_ref[...], v_ref[...])

    blk = pl.BlockSpec((tr, C), lambda i: (i, 0))
    sh = jax.ShapeDtypeStruct((R, C), F32)
    return pl.pallas_call(
        body, name=name, grid=(R // tr,), in_specs=[blk, blk, blk, blk], out_specs=[blk, blk, blk],
        out_shape=[sh, sh, sh], compiler_params=pltpu.CompilerParams(dimension_semantics=("parallel",)),
    )(w, g, m, v)


def adamw_slots(w, slots, m, v, *, name, tr=256):
    unit = w.ndim == 3
    R, C = w.shape[0], w.shape[-1]
    if R % 16 == 0:
        tr = _pick(R, tr, 16)
    else:
        tr = 144

    def body(w_ref, s_ref, m_ref, v_ref, g_ref, d_ref, nm_ref, nv_ref):
        g = s_ref[0].astype(F32)
        for k in range(1, NDEV):
            g = g + s_ref[k].astype(F32)
        g_ref[...] = g
        d_ref[...], nm_ref[...], nv_ref[...] = _adamw_math(w_ref[...], g, m_ref[...], v_ref[...])

    blk = pl.BlockSpec((tr, None, C), lambda i: (i, 0, 0)) if unit else pl.BlockSpec((tr, C), lambda i: (i, 0))
    sh = jax.ShapeDtypeStruct(w.shape, F32)
    return pl.pallas_call(
        body, name=name, grid=(pl.cdiv(R, tr),), in_specs=[blk, pl.BlockSpec((NDEV, tr, C), lambda i: (0, i, 0)), blk, blk],
        out_specs=[blk, blk, blk, blk], out_shape=[sh, sh, sh, sh],
        compiler_params=pltpu.CompilerParams(dimension_semantics=("parallel",)),
    )(w, slots, m, v)


def _padc(a, n=D):
    return jnp.pad(a, ((0, 0), (0, n - a.shape[1])))


def kernel(x, c, ctx, c_ctx, w_ada, b_ada, norm1_g, w_in, q_norm_g, kv_norm_g, w_uq, w_ukv, conv_w, conv_b, w_attn_out, w_conv_out, w_o, norm2_g, w_up, ffn_conv_w, ffn_conv_b, w_down, final_g, loss_target, m_c_ctx, m_w_ada, m_b_ada, m_norm1_g, m_w_in, m_q_norm_g, m_kv_norm_g, m_w_uq, m_w_ukv, m_conv_w, m_conv_b, m_w_attn_out, m_w_conv_out, m_w_o, m_norm2_g, m_w_up, m_ffn_conv_w, m_ffn_conv_b, m_w_down, m_final_g, v_c_ctx, v_w_ada, v_b_ada, v_norm1_g, v_w_in, v_q_norm_g, v_kv_norm_g, v_w_uq, v_w_ukv, v_conv_w, v_conv_b, v_w_attn_out, v_w_conv_out, v_w_o, v_norm2_g, v_w_up, v_ffn_conv_w, v_ffn_conv_b, v_w_down, v_final_g):
    me = 4 * lax.axis_index("x") + 2 * lax.axis_index("y") + lax.axis_index("c")
    W = dict(c_ctx=c_ctx, w_ada=w_ada, b_ada=b_ada, norm1_g=norm1_g, w_in=w_in, q_norm_g=q_norm_g, kv_norm_g=kv_norm_g,
             w_uq=w_uq, w_ukv=w_ukv, conv_w=conv_w, conv_b=conv_b, w_attn_out=w_attn_out, w_conv_out=w_conv_out, w_o=w_o,
             norm2_g=norm2_g, w_up=w_up, ffn_conv_w=ffn_conv_w, ffn_conv_b=ffn_conv_b, w_down=w_down, final_g=final_g)
    M = dict(c_ctx=m_c_ctx, w_ada=m_w_ada, b_ada=m_b_ada, norm1_g=m_norm1_g, w_in=m_w_in, q_norm_g=m_q_norm_g,
             kv_norm_g=m_kv_norm_g, w_uq=m_w_uq, w_ukv=m_w_ukv, conv_w=m_conv_w, conv_b=m_conv_b, w_attn_out=m_w_attn_out,
             w_conv_out=m_w_conv_out, w_o=m_w_o, norm2_g=m_norm2_g, w_up=m_w_up, ffn_conv_w=m_ffn_conv_w,
             ffn_conv_b=m_ffn_conv_b, w_down=m_w_down, final_g=m_final_g)
    V = dict(c_ctx=v_c_ctx, w_ada=v_w_ada, b_ada=v_b_ada, norm1_g=v_norm1_g, w_in=v_w_in, q_norm_g=v_q_norm_g,
             kv_norm_g=v_kv_norm_g, w_uq=v_w_uq, w_ukv=v_w_ukv, conv_w=v_conv_w, conv_b=v_conv_b, w_attn_out=v_w_attn_out,
             w_conv_out=v_w_conv_out, w_o=v_w_o, norm2_g=v_norm2_g, w_up=v_w_up, ffn_conv_w=v_ffn_conv_w,
             ffn_conv_b=v_ffn_conv_b, w_down=v_w_down, final_g=v_final_g)
    names = list(W)
    transposed = ("w_up",)
    as2d = lambda k, a: (a.reshape(1, -1) if a.ndim == 1 else
                         a[0].T if k in transposed else a.reshape(a.shape[-2], a.shape[-1]))
    W2 = {k: as2d(k, a) for k, a in W.items()}
    M2 = {k: as2d(k, a) for k, a in M.items()}
    V2 = {k: as2d(k, a) for k, a in V.items()}
    unit3 = lambda a: jnp.transpose(a, (2, 0, 1))
    W3, M3, V3 = unit3(W["w_in"]), unit3(M["w_in"]), unit3(V["w_in"])
    nsh = W2["w_ada"].shape[1]

    b_sh = lax.dynamic_slice(W2["b_ada"], (0, me * nsh), (1, nsh))
    s_all, m_all = ada_fwd(c, W2["c_ctx"], _padc(W2["ffn_conv_w"]), _padc(W2["conv_w"]), W2["w_ada"], b_sh, [],
                           name="ada_fwd")
    mod_lat = m_all[:, 0, :].reshape(1, 6 * D)
    mod_ctx = m_all[:, 1, :].reshape(1, 6 * D)
    ffn_w_full = s_all[:, 2:5, :2 * DFF // NDEV].transpose(1, 0, 2).reshape(3, 2 * DFF)
    conv_w_full = s_all[:, 5:8, :CONV // NDEV].transpose(1, 0, 2).reshape(3, CONV)

    stage_w = {"in": ["w_in"], "mid": ["w_uq", "w_ukv", "w_attn_out", "w_conv_out", "w_o"], "up": ["w_up"],
               "down": ["w_down"]}
    two_level = ("in", "mid")
    ag, tok = {}, m_all
    for st, nms in stage_w.items():
        ag[st] = exchange_start([W2[nm].astype(BF) for nm in nms], per_peer=False, dep=tok, name="ag_start_" + st,
                                peers=FIRST_HOP if st in two_level else ALL_PEERS)
        tok = ag[st]["token"]

    def get_w(stage, after):
        lands = exchange_wait(ag[stage], after, name="ag_wait_" + stage)
        if stage in two_level:
            lands = relay_wait(relay_start(lands, name="ag_relay_" + stage), name="ag_relay_wait_" + stage)
        g = dict(zip(stage_w[stage], lands))
        if stage == "in":
            return build_win(g["w_in"], name="build_win")
        if stage == "mid":
            wq2, wkv2 = build_wq_wkv(g["w_uq"], g["w_ukv"], name="build_wq_wkv")
            return (wq2, wkv2, unshard_cols(g["w_attn_out"], name="unshard_w_attn_out"),
                    unshard_cols(g["w_conv_out"], name="unshard_w_conv_out"), g["w_o"].reshape(D, D))
        if stage == "up":
            return g["w_up"].reshape(2 * DFF, D)
        return g["w_down"].reshape(DFF, D)

    stage_g = {"ffn": ["w_up", "w_down"], "mid": ["w_attn_out", "w_conv_out", "w_o"], "qkv": ["w_uq", "w_ukv"],
               "in": ["w_in"]}
    rs = {}

    def put_g(stage, g):
        if stage == "in":
            parts = [shard_win_grad(g["dwin"], g["dwin_c"], name="shard_win_grad")]
        elif stage == "mid":
            parts = [shard_cols(g["dwao"], name="shard_w_attn_out"), shard_cols(g["dwco"], name="shard_w_conv_out"),
                     g["dwo"].reshape(NDEV, D // NDEV, D)]
        elif stage == "qkv":
            parts = list(shard_wq_wkv_grad(g["dwq2"], g["dwkv2"], name="shard_wq_wkv_grad"))
        else:
            parts = [g["dwup"].reshape(NDEV, 2 * DFF // NDEV, D), g["dwdn"].reshape(NDEV, DFF // NDEV, D)]
        rs[stage] = exchange_start(parts, per_peer=True, name="rs_start_" + stage)
        return rs[stage]["token"]

    r = _local_step(x[0], ctx[0], loss_target[0], mod_lat, mod_ctx, W2["norm1_g"], W2["q_norm_g"], W2["kv_norm_g"],
                    W2["norm2_g"], W2["final_g"], conv_w_full, W2["conv_b"], ffn_w_full, W2["ffn_conv_b"], get_w, put_g,
                    ag["down"]["token"])

    G, DL, NM, NV = {}, {}, {}, {}

    def finish(stage, after):
        for nm, sl in zip(stage_g[stage], exchange_wait(rs[stage], after, name="rs_wait_" + stage)):
            wmv = (W3, M3, V3) if nm == "w_in" else (W2[nm], M2[nm], V2[nm])
            G[nm], DL[nm], NM[nm], NV[nm] = adamw_slots(wmv[0], sl, wmv[1], wmv[2], name="adamw_" + nm)
            after = DL[nm]
        return after

    after = r["dx"]
    for st in ("ffn", "mid", "qkv"):
        after = finish(st, after)

    a_buf, ssum = sync_small(r, [DL[nm] for st in ("ffn", "mid", "qkv") for nm in stage_g[st]], name="sync_small")
    loss = ssum[P_LOSS, 0]
    G["norm1_g"] = ssum[P_N1:P_N1 + 1]
    G["q_norm_g"] = ssum[P_QG:P_QG + 1, :QL]
    G["kv_norm_g"] = ssum[P_KVG:P_KVG + 1, :KVL]
    G["conv_b"] = ssum[P_CB:P_CB + 1, :CONV]
    G["norm2_g"] = ssum[P_N2:P_N2 + 1]
    G["ffn_conv_b"] = ssum[P_FB:P_FB + 2 * FROWS].reshape(1, 2, FROWS * D)[:, :, :DFF].reshape(1, 2 * DFF)
    G["final_g"] = ssum[P_FG:P_FG + 1]
    G["conv_w"] = lax.dynamic_slice(ssum[P_CW:P_CW + 3, :CONV], (0, me * (CONV // NDEV)), (3, CONV // NDEV))
    fw_full = ssum[P_FW:P_FW + 6 * FROWS].reshape(3, 2, FROWS * D)[:, :, :DFF].reshape(3, 2 * DFF)
    G["ffn_conv_w"] = lax.dynamic_slice(fw_full, (0, me * (2 * DFF // NDEV)), (3, 2 * DFF // NDEV))
    G["b_ada"] = (ssum[P_DML:P_DML + 6] + ssum[P_DMC:P_DMC + 6]).reshape(1, 6 * D)

    dml = lax.dynamic_slice(a_buf[:, P_DML:P_DML + 6, :].reshape(NDEV, 6 * D), (0, me * nsh), (NDEV, nsh))
    dmc = lax.dynamic_slice(ssum[P_DMC:P_DMC + 6].reshape(1, 6 * D), (0, me * nsh), (1, nsh))
    G["w_ada"], gcc = ada_bwd(s_all, dml, dmc, W2["w_ada"], W2["c_ctx"], name="ada_bwd")
    G["c_ctx"] = gcc[0:1]

    DL["w_ada"], NM["w_ada"], NV["w_ada"] = adamw(W2["w_ada"], G["w_ada"], M2["w_ada"], V2["w_ada"], name="adamw_w_ada")
    small = ["c_ctx", "b_ada", "norm1_g", "q_norm_g", "kv_norm_g", "conv_b", "norm2_g", "ffn_conv_b", "final_g", "conv_w",
             "ffn_conv_w"]
    ds, nms, nvs = adamw_many([W2[k] for k in small], [G[k] for k in small], [M2[k] for k in small],
                              [V2[k] for k in small], name="adamw_small")
    for k, nm in enumerate(small):
        DL[nm], NM[nm], NV[nm] = ds[k], nms[k], nvs[k]
    finish("in", ds[0])

    outs = [loss, r["dx"][None]]
    for grp in (G, DL, NM, NV):
        outs += [grp[nm].T[None] if nm in transposed else
                 jnp.transpose(grp[nm], (1, 2, 0)) if nm == "w_in" else grp[nm].reshape(W[nm].shape) for nm in names]
    return tuple(outs)
```

```python
import functools
import numpy as np
import jax
import jax.numpy as jnp
from jax import lax
from jax.experimental import pallas as pl
from jax.experimental.pallas import tpu as pltpu

F32 = jnp.float32
BF = jnp.bfloat16
MESH = pl.DeviceIdType.MESH

D = 1024
T = 2048
TC = 256
TKV = T + TC
GRID_W = 64
NH = 8
DN = 64
DR = 32
DV = 64
QL = 384
KVL = 256
CONV = 512
DFF = 2816
EPS = 1e-6
ROPE_THETA = 10000.0
SCALE = (DN + DR) ** -0.5
NDEV = 8
HP = 128

O_GA, O_GC, O_KV, O_Q, O_CV = 0, 1024, 2048, 2560, 3072
NIN = 4608
CVB = 256
N_IN = 4256
SH_IN = N_IN // NDEV

LR, B1, B2, AEPS, WD, STEP = 0.001, 0.9, 0.999, 1e-08, 0.01, 10


def _pick(n, target, mult=128):
    best = None
    for d in range(mult, min(n, target) + 1, mult):
        if n % d == 0:
            best = d
    return best if best is not None else n


def _swap_start(g):
    return 8 * (g ^ 1)


def mm(a, b, *, ta=False, tb=False, out_dtype=F32, name, tm=1024, tn=1024, tk=2048, M=None, N=None, K=None,
       a_off=(0, 0), b_off=(0, 0), a_stack=False, b_stack=False, o_stack=False, dep=None):
    def dims(arr, stack):
        return (arr.shape[1], 2 * arr.shape[2]) if stack else arr.shape

    ar, ac = dims(a, a_stack)
    br, bc = dims(b, b_stack)
    M = M or ((ac if ta else ar) - a_off[1 if ta else 0])
    K = K or ((ar if ta else ac) - a_off[0 if ta else 1])
    N = N or ((br if tb else bc) - b_off[0 if tb else 1])
    tm = _pick(M, tm, 128 if ta else 16)
    tn = _pick(N // 2 if (o_stack or (b_stack and not tb)) else N, tn, 128)
    tk = _pick(K // 2 if ((a_stack and not ta) or (b_stack and tb)) else K, tk, 128)
    nk = K // tk
    ca = 0 if ta else 1
    cb = 1 if tb else 0

    def body(a_ref, b_ref, *rest):
        o_ref, acc = rest[-2:]
        k = pl.program_id(2)
        part = lax.dot_general(a_ref[...].astype(BF), b_ref[...].astype(BF),
                               (((ca,), (cb,)), ((), ())), preferred_element_type=F32)
        if nk == 1:
            o_ref[...] = part.astype(o_ref.dtype)
        else:
            @pl.when(k == 0)
            def _():
                acc[...] = part

            @pl.when(k > 0)
            def _():
                acc[...] += part

            @pl.when(k == nk - 1)
            def _():
                o_ref[...] = acc[...].astype(o_ref.dtype)

    def spec(blk, rc, off, stack, ncols):
        assert off[0] % blk[0] == 0 and off[1] % blk[1] == 0, (name, blk, off)
        ro, co = off[0] // blk[0], off[1] // blk[1]
        if not stack:
            return pl.BlockSpec(blk, lambda i, j, k: (rc(i, j, k)[0] + ro, rc(i, j, k)[1] + co))
        nhb = ncols // 2 // blk[1]
        return pl.BlockSpec((None,) + blk,
                            lambda i, j, k: ((rc(i, j, k)[1] + co) // nhb, rc(i, j, k)[0] + ro, (rc(i, j, k)[1] + co) % nhb))

    a_spec = spec((tk, tm), lambda i, j, k: (k, i), a_off, a_stack, ac) if ta else \
        spec((tm, tk), lambda i, j, k: (i, k), a_off, a_stack, ac)
    b_spec = spec((tn, tk), lambda i, j, k: (j, k), b_off, b_stack, bc) if tb else \
        spec((tk, tn), lambda i, j, k: (k, j), b_off, b_stack, bc)
    o_spec = spec((tm, tn), lambda i, j, k: (i, j), (0, 0), o_stack, N)
    o_shape = (2, M, N // 2) if o_stack else (M, N)
    deps = [] if dep is None else [dep]
    return pl.pallas_call(
        body, name=name, grid=(M // tm, N // tn, nk),
        in_specs=[a_spec, b_spec] + [pl.BlockSpec(memory_space=pl.ANY)] * len(deps),
        out_specs=o_spec, out_shape=jax.ShapeDtypeStruct(o_shape, out_dtype),
        scratch_shapes=[pltpu.VMEM((tm, tn) if nk > 1 else (8, 128), F32)],
        compiler_params=pltpu.CompilerParams(dimension_semantics=("parallel", "parallel", "arbitrary")),
    )(a, b, *deps)


def _row(width):
    return pl.BlockSpec((1, width), lambda *_: (0, 0))


NLAT = T // TC


def normmod_cat(ctx, x, g, csc, csh, sc, sh, dep, *, name, tm=256):
    assert tm == TC

    def body(c_ref, x_ref, g_ref, csc_ref, csh_ref, sc_ref, sh_ref, dep_ref, h_ref):
        last = pl.program_id(0) == NLAT
        xv = jnp.where(last, c_ref[...], x_ref[...])
        scv = jnp.where(last, csc_ref[...], sc_ref[...])
        shv = jnp.where(last, csh_ref[...], sh_ref[...])
        r = lax.rsqrt(jnp.mean(xv * xv, axis=-1, keepdims=True) + EPS)
        h_ref[...] = ((xv * r * g_ref[...]) * (1.0 + scv) + shv).astype(BF)

    return pl.pallas_call(
        body, name=name, grid=(TKV // tm,),
        in_specs=[pl.BlockSpec((tm, D), lambda i: (0, 0)), pl.BlockSpec((tm, D), lambda i: (jnp.minimum(i, NLAT - 1), 0)),
                  _row(D), _row(D), _row(D), _row(D), _row(D), pl.BlockSpec(memory_space=pl.ANY)],
        out_specs=pl.BlockSpec((tm, D), lambda i: (i, 0)), out_shape=jax.ShapeDtypeStruct((TKV, D), BF),
        compiler_params=pltpu.CompilerParams(dimension_semantics=("parallel",)),
    )(ctx, x, g, csc, csh, sc, sh, dep)


def kvprep(pc, p, kvg, wkv2, ck, sk, *, name, tm=256):
    assert tm == TC
    nb = TKV // tm
    kvcol = O_KV // 512

    def body(pc_ref, p_ref, g_ref, w_ref, ck_ref, sk_ref, k_ref, v_ref, ckv_ref):
        i = pl.program_id(0)
        t = jnp.where(i == NLAT, pc_ref[...], p_ref[...])
        pk = t[:, :KVL]
        r = lax.rsqrt(jnp.mean(pk * pk, axis=-1, keepdims=True) + EPS)
        ckv = (pk * r * g_ref[...]).astype(BF)
        ckv_ref[...] = ckv
        kv2 = jnp.dot(ckv, w_ref[...], preferred_element_type=F32)
        krr = t[:, KVL:KVL + HP] * ck_ref[...] + t[:, KVL + HP:KVL + 2 * HP] * sk_ref[...]
        k_ref[...] = (kv2[:, :NH * HP] + jnp.concatenate([krr] * NH, axis=1)).astype(BF)
        v_ref[...] = kv2[:, NH * HP:].astype(BF)

    return pl.pallas_call(
        body, name=name, grid=(nb,),
        in_specs=[pl.BlockSpec((tm, 512), lambda i: (0, 0)),
                  pl.BlockSpec((tm, 512), lambda i: (jnp.minimum(i, NLAT - 1), kvcol)),
                  _row(KVL), pl.BlockSpec((KVL, NH * HP + NH * DV), lambda i: (0, 0)),
                  pl.BlockSpec((tm, HP), lambda i: (i, 0)), pl.BlockSpec((tm, HP), lambda i: (i, 0))],
        out_specs=[pl.BlockSpec((tm, NH * HP), lambda i: (i, 0)), pl.BlockSpec((tm, NH * DV), lambda i: (i, 0)),
                   pl.BlockSpec((tm, KVL), lambda i: (i, 0))],
        out_shape=[jax.ShapeDtypeStruct((TKV, NH * HP), BF), jax.ShapeDtypeStruct((TKV, NH * DV), BF),
                   jax.ShapeDtypeStruct((TKV, KVL), BF)],
        compiler_params=pltpu.CompilerParams(dimension_semantics=("parallel",)),
    )(pc, p, kvg, wkv2, ck, sk)


def qprep(p, qg, wq2, cq_t, sq_t, *, name, tm=256):
    qcol = O_Q // 512

    def body(p_ref, g_ref, w_ref, c_ref, s_ref, q_ref, cq_ref):
        pq = p_ref[...]
        r = lax.rsqrt(jnp.sum(pq * pq, axis=-1, keepdims=True) * (1.0 / QL) + EPS)
        cq = (pq * r * g_ref[...]).astype(BF)
        cq_ref[...] = cq
        q2 = jnp.dot(cq, w_ref[...], preferred_element_type=F32)
        cc = jnp.concatenate([c_ref[...]] * NH, axis=1)
        ss = jnp.concatenate([s_ref[...]] * NH, axis=1)
        q_ref[...] = (q2[:, :NH * HP] * cc + q2[:, NH * HP:] * ss).astype(BF)

    return pl.pallas_call(
        body, name=name, grid=(T // tm,),
        in_specs=[pl.BlockSpec((tm, 512), lambda i: (i, qcol)), _row(512),
                  pl.BlockSpec((512, 2 * NH * HP), lambda i: (0, 0)),
                  pl.BlockSpec((tm, HP), lambda i: (i, 0)), pl.BlockSpec((tm, HP), lambda i: (i, 0))],
        out_specs=[pl.BlockSpec((tm, NH * HP), lambda i: (i, 0)), pl.BlockSpec((tm, 512), lambda i: (i, 0))],
        out_shape=[jax.ShapeDtypeStruct((T, NH * HP), BF), jax.ShapeDtypeStruct((T, 512), BF)],
        compiler_params=pltpu.CompilerParams(dimension_semantics=("parallel",)),
    )(p, qg, wq2, cq_t, sq_t)


def _head_mask(h):
    lanes = lax.broadcasted_iota(jnp.int32, (1, 2 * DV), 1)
    return (lanes // DV) == (h % 2)


LOG2E = 1.4426950408889634


def attn_fwd(q, k, v, *, name, tq=1024, kc=768):
    def body(q_ref, k_ref, v_ref, o_ref, lse_ref):
        h = pl.program_id(1)
        qv = q_ref[...]
        m = l = acc = None
        for c in range(TKV // kc):
            s = lax.dot_general(qv, k_ref[c * kc:(c + 1) * kc, :], (((1,), (1,)), ((), ())),
                                preferred_element_type=F32) * (SCALE * LOG2E)
            mc = jnp.max(s, axis=-1, keepdims=True)
            if c == 0:
                m = mc
                e = jnp.exp2(s - m)
                l = jnp.sum(e, axis=-1, keepdims=True)
                acc = jnp.dot(e.astype(BF), v_ref[c * kc:(c + 1) * kc, :], preferred_element_type=F32)
            else:
                mn = jnp.maximum(m, mc)
                a = jnp.exp2(m - mn)
                e = jnp.exp2(s - mn)
                l = l * a + jnp.sum(e, axis=-1, keepdims=True)
                acc = acc * a + jnp.dot(e.astype(BF), v_ref[c * kc:(c + 1) * kc, :], preferred_element_type=F32)
                m = mn
        o2 = jnp.where(_head_mask(h), acc * (1.0 / l), 0.0).astype(BF)
        lse_ref[...] = jnp.broadcast_to(m + jnp.log(l) * LOG2E, (tq, HP))

        @pl.when(h % 2 == 0)
        def _():
            o_ref[...] = o2

        @pl.when(h % 2 == 1)
        def _():
            o_ref[...] = o_ref[...] + o2

    return pl.pallas_call(
        body, name=name, grid=(T // tq, NH),
        in_specs=[pl.BlockSpec((tq, HP), lambda i, h: (i, h)), pl.BlockSpec((TKV, HP), lambda i, h: (0, h)),
                  pl.BlockSpec((TKV, 2 * DV), lambda i, h: (0, h // 2))],
        out_specs=[pl.BlockSpec((tq, 2 * DV), lambda i, h: (i, h // 2)), pl.BlockSpec((tq, HP), lambda i, h: (i, h))],
        out_shape=[jax.ShapeDtypeStruct((T, NH * DV), BF), jax.ShapeDtypeStruct((T, NH * HP), F32)],
        compiler_params=pltpu.CompilerParams(dimension_semantics=("parallel", "arbitrary")),
    )(q, k, v)


def _shift_dn(x):
    n = x.shape[0]
    rows = lax.broadcasted_iota(jnp.int32, (n, 1), 0)
    return jnp.where(rows == 0, 0.0, pltpu.roll(x, 1, axis=0))


def _shift_up(x):
    n = x.shape[0]
    rows = lax.broadcasted_iota(jnp.int32, (n, 1), 0)
    return jnp.where(rows == n - 1, 0.0, pltpu.roll(x, n - 1, axis=0))


def _conv(x, w_ref, b_ref):
    return b_ref[...] + _shift_dn(x) * w_ref[0:1, :] + x * w_ref[1:2, :] + _shift_up(x) * w_ref[2:3, :]


def _conv_t(dy, w_ref):
    return _shift_up(dy) * w_ref[0:1, :] + dy * w_ref[1:2, :] + _shift_dn(dy) * w_ref[2:3, :]


def _conv_wgrad(dw_ref, dy, x):
    dw_ref[0:1, :] = jnp.sum(dy * _shift_dn(x), axis=0, keepdims=True)
    dw_ref[1:2, :] = jnp.sum(dy * x, axis=0, keepdims=True)
    dw_ref[2:3, :] = jnp.sum(dy * _shift_up(x), axis=0, keepdims=True)


def convz(p, cw, cb, *, name):
    o0 = O_CV // (3 * CVB)

    def body(p_ref, w_ref, bias_ref, z_ref):
        xv, bv, cv = p_ref[:, 0:CVB], p_ref[:, CVB:2 * CVB], p_ref[:, 2 * CVB:3 * CVB]
        z_ref[...] = (bv * _conv(cv * xv, w_ref, bias_ref)).astype(BF)

    return pl.pallas_call(
        body, name=name, grid=(CONV // CVB,),
        in_specs=[pl.BlockSpec((T, 3 * CVB), lambda j: (0, o0 + j)), pl.BlockSpec((3, CVB), lambda j: (0, j)),
                  pl.BlockSpec((1, CVB), lambda j: (0, j))],
        out_specs=pl.BlockSpec((T, CVB), lambda j: (0, j)),
        out_shape=jax.ShapeDtypeStruct((T, CONV), BF),
        compiler_params=pltpu.CompilerParams(dimension_semantics=("parallel",)),
    )(p, cw, cb)


def out_proj_merge(o, wao, z, wco, p, *, name, tm=512):
    kin = o.shape[1]

    def body(o_ref, wa_ref, z_ref, wc_ref, ga_ref, gc_ref, ya_ref, yc_ref, m_ref):
        ya = jnp.dot(o_ref[...], wa_ref[...], preferred_element_type=F32)
        yc = jnp.dot(z_ref[...], wc_ref[...], preferred_element_type=F32)
        ya_ref[...] = ya
        yc_ref[...] = yc
        m_ref[...] = (jax.nn.sigmoid(ga_ref[...]) * ya + jax.nn.sigmoid(gc_ref[...]) * yc).astype(BF)

    blk = pl.BlockSpec((tm, D), lambda i: (i, 0))
    act = pl.BlockSpec((tm, kin), lambda i: (i, 0))
    wsp = pl.BlockSpec((kin, D), lambda i: (0, 0))
    sh = jax.ShapeDtypeStruct((T, D), F32)
    return pl.pallas_call(
        body, name=name, grid=(T // tm,),
        in_specs=[act, wsp, act, wsp, pl.BlockSpec((tm, D), lambda i: (i, O_GA // D)),
                  pl.BlockSpec((tm, D), lambda i: (i, O_GC // D))],
        out_specs=[blk, blk, blk], out_shape=[sh, sh, jax.ShapeDtypeStruct((T, D), BF)],
        compiler_params=pltpu.CompilerParams(dimension_semantics=("parallel",)),
    )(o, wao, z, wco, p, p)


CONV_HALO = 8
CONV_ROWS = 256


def _row_chunks(n, chunk, carry):
    carry = chunk(0, True, False, carry)
    carry = lax.fori_loop(1, n // CONV_ROWS - 1, lambda c, a: chunk(c * CONV_ROWS, False, False, a), carry)
    return chunk(n - CONV_ROWS, False, True, carry)


def _ext_rows(ref, r0, first, last):
    n, w = ref.shape
    zero = jnp.zeros((CONV_HALO, w), ref.dtype)
    if first:
        return jnp.concatenate([zero, ref[0:CONV_ROWS + CONV_HALO, :]], axis=0)
    if last:
        return jnp.concatenate([ref[n - CONV_ROWS - CONV_HALO:n, :], zero], axis=0)
    return ref[pl.ds(pl.multiple_of(r0 - CONV_HALO, 8), CONV_ROWS + 2 * CONV_HALO), :]


def _center_rows(r0, first, last):
    return slice(r0, r0 + CONV_ROWS) if (first or last) else pl.ds(pl.multiple_of(r0, 8), CONV_ROWS)


def _roll_dn(x):
    return pltpu.roll(x, 1, axis=0)


def _roll_up(x):
    return pltpu.roll(x, x.shape[0] - 1, axis=0)


_CTR = slice(CONV_HALO, CONV_HALO + CONV_ROWS)


def ffn_act(u0, cw, cb, *, name, tc=256):
    nb = DFF // tc

    def body(u_ref, wg_ref, wv_ref, bg_ref, bv_ref, f_ref):
        wg = [wg_ref[k:k + 1, :] for k in range(3)]
        wv = [wv_ref[k:k + 1, :] for k in range(3)]
        bg, bv = bg_ref[...], bv_ref[...]

        def chunk(r0, first, last, carry):
            xg, xv = _ext_rows(u_ref.at[0], r0, first, last), _ext_rows(u_ref.at[1], r0, first, last)
            ug = bg + _roll_dn(xg) * wg[0] + xg * wg[1] + _roll_up(xg) * wg[2]
            uv = bv + _roll_dn(xv) * wv[0] + xv * wv[1] + _roll_up(xv) * wv[2]
            f_ref[_center_rows(r0, first, last), :] = (ug * jax.nn.sigmoid(ug) * uv)[_CTR].astype(BF)
            return carry

        _row_chunks(T, chunk, 0)

    return pl.pallas_call(
        body, name=name, grid=(nb,),
        in_specs=[pl.BlockSpec((2, T, tc), lambda j: (0, 0, j)),
                  pl.BlockSpec((3, tc), lambda j: (0, j)), pl.BlockSpec((3, tc), lambda j: (0, nb + j)),
                  pl.BlockSpec((1, tc), lambda j: (0, j)), pl.BlockSpec((1, tc), lambda j: (0, nb + j))],
        out_specs=pl.BlockSpec((T, tc), lambda j: (0, j)),
        out_shape=jax.ShapeDtypeStruct((T, DFF), BF),
        compiler_params=pltpu.CompilerParams(dimension_semantics=("parallel",)),
    )(u0, cw, cw, cb, cb)


def rows_call(lead, ins, in_specs, out_shape, out_specs, fn, *, name, R, tm):
    tb, a_stack, tk = lead.get("tb", False), lead.get("a_stack", False), lead["tk"]
    K = 2 * lead["a"].shape[2] if a_stack else lead["a"].shape[1]
    nk = K // tk
    deps = [] if lead.get("dep") is None else [lead["dep"]]
    n_in = len(ins)

    def body(a_ref, b_ref, *refs):
        refs = refs[len(deps):]
        in_refs, out_refs, acc = refs[:n_in], refs[n_in:-1], refs[-1]
        i, k = pl.program_id(0), pl.program_id(1)
        part = lax.dot_general(a_ref[...].astype(BF), b_ref[...].astype(BF),
                               (((1,), (1 if tb else 0,)), ((), ())), preferred_element_type=F32)
        if nk == 1:
            fn(i, part, in_refs, out_refs)
            return

        @pl.when(k == 0)
        def _():
            acc[...] = part

        @pl.when(k > 0)
        def _():
            acc[...] += part

        @pl.when(k == nk - 1)
        def _():
            fn(i, acc[...], in_refs, out_refs)

    if a_stack:
        nhb = K // 2 // tk
        a_spec = pl.BlockSpec((None, tm, tk), lambda i, k: (k // nhb, i, k % nhb))
    else:
        a_spec = pl.BlockSpec((tm, tk), lambda i, k: (i, k))
    b_spec = pl.BlockSpec((D, tk), lambda i, k: (0, k)) if tb else pl.BlockSpec((tk, D), lambda i, k: (k, 0))
    return pl.pallas_call(
        body, name=name, grid=(R // tm, nk),
        in_specs=[a_spec, b_spec] + [pl.BlockSpec(memory_space=pl.ANY)] * len(deps) + list(in_specs),
        out_specs=out_specs, out_shape=out_shape,
        scratch_shapes=[pltpu.VMEM((tm, D) if nk > 1 else (8, 128), F32)],
        compiler_params=pltpu.CompilerParams(dimension_semantics=("arbitrary", "arbitrary")),
    )(lead["a"], lead["b"], *deps, *ins)


def _rblk(tm, w=D, col=0):
    return pl.BlockSpec((tm, w), lambda i, k: (i, col))


def _rrow(w=D):
    return pl.BlockSpec((1, w), lambda i, k: (0, 0))


def down_final(f, wdn, x1, g2, fg, tgt, *, name, tm=512):
    def fn(i, d, in_refs, out_refs):
        x1_ref, g2_ref, fg_ref, t_ref = in_refs
        d_ref, dx_ref, dd_ref, dfg_ref, loss_ref = out_refs
        d_ref[...] = d
        xv = x1_ref[...] + g2_ref[...] * d
        r = lax.rsqrt(jnp.mean(xv * xv, axis=-1, keepdims=True) + EPS)
        xh = xv * r
        diff = xh * fg_ref[...] - t_ref[...]
        part = 0.5 * jnp.sum(jnp.mean(diff * diff, axis=-1, keepdims=True), axis=0, keepdims=True)
        dy = diff * (1.0 / D)
        a = dy * fg_ref[...]
        dx = r * (a - xh * jnp.mean(a * xh, axis=-1, keepdims=True))
        dx_ref[...] = dx
        dd_ref[...] = (dx * g2_ref[...]).astype(BF)
        dfg = jnp.sum(dy * xh, axis=0, keepdims=True)

        @pl.when(i == 0)
        def _():
            dfg_ref[...] = dfg
            loss_ref[...] = jnp.broadcast_to(part, (1, 128))

        @pl.when(i > 0)
        def _():
            dfg_ref[...] += dfg
            loss_ref[...] += jnp.broadcast_to(part, (1, 128))

    blk = _rblk(tm)
    return rows_call(
        dict(a=f, b=wdn, tk=DFF), [x1, g2, fg, tgt], [blk, _rrow(), _rrow(), blk],
        [jax.ShapeDtypeStruct((T, D), F32), jax.ShapeDtypeStruct((T, D), F32), jax.ShapeDtypeStruct((T, D), BF),
         jax.ShapeDtypeStruct((1, D), F32), jax.ShapeDtypeStruct((1, 128), F32)],
        [blk, blk, blk, _rrow(), _rrow(128)], fn, name=name, R=T, tm=tm)


def oproj_resid(merged, wo, x, gate, g, sc, sh, *, name, tm=512):
    def fn(i, a, in_refs, out_refs):
        x_ref, gate_ref, g_ref, sc_ref, sh_ref = in_refs
        a_ref, x1_ref, h_ref = out_refs
        a_ref[...] = a
        xv = x_ref[...] + gate_ref[...] * a
        x1_ref[...] = xv
        r = lax.rsqrt(jnp.mean(xv * xv, axis=-1, keepdims=True) + EPS)
        h_ref[...] = ((xv * r * g_ref[...]) * (1.0 + sc_ref[...]) + sh_ref[...]).astype(BF)

    blk = _rblk(tm)
    return rows_call(
        dict(a=merged, b=wo, tk=D), [x, gate, g, sc, sh], [blk, _rrow(), _rrow(), _rrow(), _rrow()],
        [jax.ShapeDtypeStruct((T, D), F32), jax.ShapeDtypeStruct((T, D), F32), jax.ShapeDtypeStruct((T, D), BF)],
        [blk, blk, blk], fn, name=name, R=T, tm=tm)


def oproj_dx_gate_bwd(da, wo, p, ya, yc, wao, wco, *, name, tm=512):
    kin = wao.shape[0]

    def fn(i, dm, in_refs, out_refs):
        ga_ref, gc_ref, ya_ref, yc_ref, wa_ref, wc_ref = in_refs
        dya_ref, dyc_ref, dp_ref, do_ref, dz_ref = out_refs
        sa, sc_ = jax.nn.sigmoid(ga_ref[...]), jax.nn.sigmoid(gc_ref[...])
        dya, dyc = (dm * sa).astype(BF), (dm * sc_).astype(BF)
        dya_ref[...] = dya
        dyc_ref[...] = dyc
        dp_ref[:, 0:D] = (dm * ya_ref[...] * (sa * (1.0 - sa))).astype(BF)
        dp_ref[:, D:2 * D] = (dm * yc_ref[...] * (sc_ * (1.0 - sc_))).astype(BF)
        nt = (((1,), (1,)), ((), ()))
        do_ref[...] = lax.dot_general(dya, wa_ref[...], nt, preferred_element_type=F32).astype(BF)
        dz_ref[...] = lax.dot_general(dyc, wc_ref[...], nt, preferred_element_type=F32)

    blk = _rblk(tm)
    sh = jax.ShapeDtypeStruct((T, D), BF)
    wsp = pl.BlockSpec((kin, D), lambda i, k: (0, 0))
    return rows_call(
        dict(a=da, b=wo, tb=True, tk=D), [p, p, ya, yc, wao, wco],
        [_rblk(tm, D, O_GA // D), _rblk(tm, D, O_GC // D), blk, blk, wsp, wsp],
        [sh, sh, jax.ShapeDtypeStruct((T, NIN), BF), jax.ShapeDtypeStruct((T, kin), BF), jax.ShapeDtypeStruct((T, kin), F32)],
        [blk, blk, _rblk(tm, 2 * D), _rblk(tm, kin), _rblk(tm, kin)], fn, name=name, R=T, tm=tm)


def normmod_bwd(x, dh, g, sc, dres, gsrc, gate, *, name, tm=512):
    R = x.shape[0]
    tm = min(tm, R)
    has_res = dres is not None
    fused = isinstance(dh, dict)
    if fused:
        tb, a_stack, tk = dh.get("tb", False), dh.get("a_stack", False), dh["tk"]
        K = 2 * dh["a"].shape[2] if a_stack else dh["a"].shape[1]
        nk = K // tk
        deps = [] if dh.get("dep") is None else [dh["dep"]]
        n_dh = 2 + len(deps)
    else:
        nk, n_dh = 1, 1

    def elementwise(i, dhv, x_ref, g_ref, sc_ref, res_refs, out_refs):
        xv = x_ref[...]
        r = lax.rsqrt(jnp.mean(xv * xv, axis=-1, keepdims=True) + EPS)
        xh = xv * r
        n = xh * g_ref[...]
        dn = dhv * (1.0 + sc_ref[...])
        a = dn * g_ref[...]
        rows = [jnp.sum(dhv, axis=0, keepdims=True), jnp.sum(dhv * n, axis=0, keepdims=True),
                jnp.sum(dn * xh, axis=0, keepdims=True)]
        if has_res:
            dres_ref, gsrc_ref, gate_ref = res_refs
            dx_ref, dxg_ref, st_ref = out_refs
            dr = dres_ref[...]
            dx = dr + r * (a - xh * jnp.mean(a * xh, axis=-1, keepdims=True))
            dx_ref[...] = dx
            dxg_ref[...] = (dx * gate_ref[...]).astype(BF)
            rows.append(jnp.sum(dr * gsrc_ref[...], axis=0, keepdims=True))
        else:
            st_ref, = out_refs
            rows.append(jnp.zeros((1, D), F32))

        @pl.when(i == 0)
        def _():
            for k, row in enumerate(rows):
                st_ref[k:k + 1, :] = row

        @pl.when(i > 0)
        def _():
            for k, row in enumerate(rows):
                st_ref[k:k + 1, :] += row

    def body(*refs):
        x_ref, dh_refs, g_ref, sc_ref = refs[0], refs[1:1 + n_dh], refs[1 + n_dh], refs[2 + n_dh]
        rest = refs[3 + n_dh:]
        res_refs, rest = (rest[:3], rest[3:]) if has_res else ((), rest)
        out_refs = rest[:3] if has_res else rest[:1]
        i = pl.program_id(0)
        if not fused:
            elementwise(i, dh_refs[0][...], x_ref, g_ref, sc_ref, res_refs, out_refs)
            return
        acc = rest[-1]
        k = pl.program_id(1)
        part = lax.dot_general(dh_refs[0][...].astype(BF), dh_refs[1][...].astype(BF),
                               (((1,), (1 if tb else 0,)), ((), ())), preferred_element_type=F32)

        @pl.when(k == 0)
        def _():
            acc[...] = part

        @pl.when(k > 0)
        def _():
            acc[...] += part

        @pl.when(k == nk - 1)
        def _():
            elementwise(i, acc[...], x_ref, g_ref, sc_ref, res_refs, out_refs)

    rowb = lambda w: pl.BlockSpec((1, w), lambda i, *k: (0, 0))
    blk = pl.BlockSpec((tm, D), lambda i, *k: (i, 0))
    st_spec = pl.BlockSpec((4, D), lambda i, *k: (0, 0))
    st_shape = jax.ShapeDtypeStruct((4, D), F32)
    if fused:
        if a_stack:
            nhb = K // 2 // tk
            a_spec = pl.BlockSpec((None, tm, tk), lambda i, k: (k // nhb, i, k % nhb))
        else:
            a_spec = pl.BlockSpec((tm, tk), lambda i, k: (i, k))
        b_spec = pl.BlockSpec((D, tk), lambda i, k: (0, k)) if tb else pl.BlockSpec((tk, D), lambda i, k: (k, 0))
        dh_specs = [a_spec, b_spec] + [pl.BlockSpec(memory_space=pl.ANY)] * len(deps)
        dh_args = [dh["a"], dh["b"]] + deps
        grid, sem = (R // tm, nk), ("arbitrary", "arbitrary")
        scratch = [pltpu.VMEM((tm, D), F32)]
    else:
        dh_specs, dh_args, grid, sem, scratch = [blk], [dh], (R // tm,), ("arbitrary",), []
    cp = pltpu.CompilerParams(dimension_semantics=sem)
    if has_res:
        return pl.pallas_call(
            body, name=name, grid=grid, in_specs=[blk] + dh_specs + [rowb(D), rowb(D), blk, blk, rowb(D)],
            out_specs=[blk, blk, st_spec], scratch_shapes=scratch,
            out_shape=[jax.ShapeDtypeStruct((R, D), F32), jax.ShapeDtypeStruct((R, D), BF), st_shape],
            compiler_params=cp,
        )(x, *dh_args, g, sc, dres, gsrc, gate)
    return pl.pallas_call(
        body, name=name, grid=grid, in_specs=[blk] + dh_specs + [rowb(D), rowb(D)],
        out_specs=st_spec, out_shape=st_shape, scratch_shapes=scratch, compiler_params=cp,
    )(x, *dh_args, g, sc)


def ffn_act_bwd(u0, df, cw, cb, *, name, tc=128):
    nb = DFF // tc

    def body(u_ref, df_ref, wg_ref, wv_ref, bg_ref, bv_ref, du_ref, dw_ref, db_ref):
        wg = [wg_ref[k:k + 1, :] for k in range(3)]
        wv = [wv_ref[k:k + 1, :] for k in range(3)]
        bg, bv = bg_ref[...], bv_ref[...]

        def chunk(r0, first, last, acc):
            xg, xv = _ext_rows(u_ref.at[0], r0, first, last), _ext_rows(u_ref.at[1], r0, first, last)
            dfe = _ext_rows(df_ref, r0, first, last)
            xg_d, xg_u, xv_d, xv_u = _roll_dn(xg), _roll_up(xg), _roll_dn(xv), _roll_up(xv)
            ug = bg + xg_d * wg[0] + xg * wg[1] + xg_u * wg[2]
            uv = bv + xv_d * wv[0] + xv * wv[1] + xv_u * wv[2]
            sig = jax.nn.sigmoid(ug)
            dug = dfe * uv * (sig * (1.0 + ug * (1.0 - sig)))
            duv = dfe * (ug * sig)
            rows = _center_rows(r0, first, last)
            du_ref[0, rows, :] = (_roll_up(dug) * wg[0] + dug * wg[1] + _roll_dn(dug) * wg[2])[_CTR].astype(BF)
            du_ref[1, rows, :] = (_roll_up(duv) * wv[0] + duv * wv[1] + _roll_dn(duv) * wv[2])[_CTR].astype(BF)
            terms = [dug * xg_d, dug * xg, dug * xg_u, dug, duv * xv_d, duv * xv, duv * xv_u, duv]
            return tuple(a + jnp.sum(t[_CTR], axis=0, keepdims=True) for a, t in zip(acc, terms))

        acc = _row_chunks(T, chunk, tuple(jnp.zeros((1, tc), F32) for _ in range(8)))
        for k in range(3):
            dw_ref[0, k:k + 1, :] = acc[k]
            dw_ref[1, k:k + 1, :] = acc[4 + k]
        db_ref[0] = acc[3]
        db_ref[1] = acc[7]

    lo = lambda r: pl.BlockSpec((r, tc), lambda j: (0, j))
    hi = lambda r: pl.BlockSpec((r, tc), lambda j: (0, nb + j))
    st = lambda r: pl.BlockSpec((2, r, tc), lambda j: (0, 0, j))
    return pl.pallas_call(
        body, name=name, grid=(nb,),
        in_specs=[st(T), lo(T), lo(3), hi(3), lo(1), hi(1)],
        out_specs=[st(T), st(3), st(1)],
        out_shape=[jax.ShapeDtypeStruct((2, T, DFF), BF), jax.ShapeDtypeStruct((2, 3, DFF), F32),
                   jax.ShapeDtypeStruct((2, 1, DFF), F32)],
        compiler_params=pltpu.CompilerParams(dimension_semantics=("parallel",)),
    )(u0, df, cw, cw, cb, cb)


def convz_bwd(p, dz, cw, cb, dp, *, name):
    o0 = O_CV // (3 * CVB)

    def body(p_ref, dz_ref, w_ref, bias_ref, dp_in, dp_ref, dw_ref, dbias_ref):
        xv, bv, cv = p_ref[:, 0:CVB], p_ref[:, CVB:2 * CVB], p_ref[:, 2 * CVB:3 * CVB]
        ci = cv * xv
        dwc = _conv(ci, w_ref, bias_ref)
        dzv = dz_ref[...]
        ddw = dzv * bv
        dci = _conv_t(ddw, w_ref)
        dp_ref[:, 0:CVB] = (dci * cv).astype(BF)
        dp_ref[:, CVB:2 * CVB] = (dzv * dwc).astype(BF)
        dp_ref[:, 2 * CVB:3 * CVB] = (dci * xv).astype(BF)
        _conv_wgrad(dw_ref, ddw, ci)
        dbias_ref[...] = jnp.sum(ddw, axis=0, keepdims=True)

    own = lambda r: pl.BlockSpec((r, CVB), lambda j: (0, j))
    return pl.pallas_call(
        body, name=name, grid=(CONV // CVB,),
        in_specs=[pl.BlockSpec((T, 3 * CVB), lambda j: (0, o0 + j)), own(T), own(3), own(1),
                  pl.BlockSpec(memory_space=pl.ANY)],
        out_specs=[pl.BlockSpec((T, 3 * CVB), lambda j: (0, o0 + j)), own(3), own(1)],
        out_shape=[jax.ShapeDtypeStruct((T, NIN), BF), jax.ShapeDtypeStruct((3, CONV), F32),
                   jax.ShapeDtypeStruct((1, CONV), F32)],
        input_output_aliases={4: 0},
        compiler_params=pltpu.CompilerParams(dimension_semantics=("parallel",)),
    )(p, dz, cw, cb, dp)


def attn_bwd(q, k, v, do, o, lse, dep, *, name, tq=1024, kc=768):
    NKC, KC = TKV // kc, kc
    deps = [] if dep is None else [dep]

    def body(q_ref, k_ref, v_ref, do_ref, o_ref, lse_ref, *rest):
        dq_ref, dk_ref, dv_ref = rest[len(deps):]
        h, i = pl.program_id(0), pl.program_id(1)

        @pl.when(i == 0)
        def _():
            dk_ref[...] = jnp.zeros_like(dk_ref)

        @pl.when((i == 0) & (h % 2 == 0))
        def _():
            dv_ref[...] = jnp.zeros_like(dv_ref)

        qv = q_ref[...]
        dom = jnp.where(_head_mask(h), do_ref[...], jnp.zeros_like(do_ref[...]))
        delta = jnp.sum(dom.astype(F32) * o_ref[...].astype(F32), axis=-1, keepdims=True)
        lse = lse_ref[:, 0:1]
        dq = jnp.zeros((tq, HP), F32)
        for c in range(NKC):
            cols = slice(c * KC, (c + 1) * KC)
            s = lax.dot_general(qv, k_ref[cols, :], (((1,), (1,)), ((), ())),
                                preferred_element_type=F32) * (SCALE * LOG2E)
            pr = jnp.exp2(s - lse)
            dp = lax.dot_general(dom, v_ref[cols, :], (((1,), (1,)), ((), ())), preferred_element_type=F32)
            ds = (pr * (dp - delta) * SCALE).astype(BF)
            dq = dq + jnp.dot(ds, k_ref[cols, :], preferred_element_type=F32)
            dk_ref[cols, :] += lax.dot_general(ds, qv, (((0,), (0,)), ((), ())), preferred_element_type=F32)
            dv_ref[cols, :] += lax.dot_general(pr.astype(BF), dom, (((0,), (0,)), ((), ())), preferred_element_type=F32)
        dq_ref[...] = dq

    return pl.pallas_call(
        body, name=name, grid=(NH, T // tq),
        in_specs=[pl.BlockSpec((tq, HP), lambda h, i: (i, h)), pl.BlockSpec((TKV, HP), lambda h, i: (0, h)),
                  pl.BlockSpec((TKV, 2 * DV), lambda h, i: (0, h // 2)), pl.BlockSpec((tq, 2 * DV), lambda h, i: (i, h // 2)),
                  pl.BlockSpec((tq, 2 * DV), lambda h, i: (i, h // 2)), pl.BlockSpec((tq, HP), lambda h, i: (i, h)),
                  *([pl.BlockSpec(memory_space=pl.ANY)] * len(deps))],
        out_specs=[pl.BlockSpec((tq, HP), lambda h, i: (i, h)), pl.BlockSpec((TKV, HP), lambda h, i: (0, h)),
                   pl.BlockSpec((TKV, 2 * DV), lambda h, i: (0, h // 2))],
        out_shape=[jax.ShapeDtypeStruct((T, NH * HP), F32), jax.ShapeDtypeStruct((TKV, NH * HP), F32),
                   jax.ShapeDtypeStruct((TKV, NH * DV), F32)],
        compiler_params=pltpu.CompilerParams(dimension_semantics=("arbitrary", "arbitrary")),
    )(q, k, v, do, o, lse, *deps)


def qprep_bwd(p, dq, qg, wq2, cq_t, sq_t, dp, *, name, tm=256):
    qcol = O_Q // 512

    def body(p_ref, dq_ref, g_ref, w_ref, c_ref, s_ref, dp_in, dp_ref, dq2_ref, dg_ref):
        i = pl.program_id(0)
        dqv = dq_ref[...]
        cc = jnp.concatenate([c_ref[...]] * NH, axis=1)
        ss = jnp.concatenate([s_ref[...]] * NH, axis=1)
        dq2 = jnp.concatenate([dqv * cc, dqv * ss], axis=1).astype(BF)
        dq2_ref[...] = dq2
        dcq = lax.dot_general(dq2, w_ref[...], (((1,), (1,)), ((), ())), preferred_element_type=F32)
        pq = p_ref[...]
        r = lax.rsqrt(jnp.sum(pq * pq, axis=-1, keepdims=True) * (1.0 / QL) + EPS)
        xh = pq * r
        a = dcq * g_ref[...]
        dp_ref[...] = (r * (a - xh * (jnp.sum(a * xh, axis=-1, keepdims=True) * (1.0 / QL)))).astype(BF)
        dg = jnp.sum(dcq * xh, axis=0, keepdims=True)

        @pl.when(i == 0)
        def _():
            dg_ref[...] = dg

        @pl.when(i > 0)
        def _():
            dg_ref[...] += dg

    return pl.pallas_call(
        body, name=name, grid=(T // tm,),
        in_specs=[pl.BlockSpec((tm, 512), lambda i: (i, qcol)), pl.BlockSpec((tm, NH * HP), lambda i: (i, 0)), _row(512),
                  pl.BlockSpec((512, 2 * NH * HP), lambda i: (0, 0)),
                  pl.BlockSpec((tm, HP), lambda i: (i, 0)), pl.BlockSpec((tm, HP), lambda i: (i, 0)),
                  pl.BlockSpec(memory_space=pl.ANY)],
        out_specs=[pl.BlockSpec((tm, 512), lambda i: (i, qcol)), pl.BlockSpec((tm, 2 * NH * HP), lambda i: (i, 0)), _row(512)],
        out_shape=[jax.ShapeDtypeStruct((T, NIN), BF), jax.ShapeDtypeStruct((T, 2 * NH * HP), BF),
                   jax.ShapeDtypeStruct((1, 512), F32)],
        input_output_aliases={6: 0},
        compiler_params=pltpu.CompilerParams(dimension_semantics=("arbitrary",)),
    )(p, dq, qg, wq2, cq_t, sq_t, dp)


def kvprep_bwd(pc, p, dk, dv, kvg, wkv2, ck, sk, dp, *, name, tm=256):
    assert tm == TC
    nb = TKV // tm
    kvcol = O_KV // 512

    def body(pc_ref, p_ref, dk_ref, dv_ref, g_ref, w_ref, ck_ref, sk_ref, dp_in, dp_ref, dpc_ref, dkv2_ref, dg_ref):
        i = pl.program_id(0)
        t = jnp.where(i == NLAT, pc_ref[...], p_ref[...])
        pk = t[:, :KVL]
        r = lax.rsqrt(jnp.mean(pk * pk, axis=-1, keepdims=True) + EPS)
        xh = pk * r
        dkv = dk_ref[...]
        dkv2 = jnp.concatenate([dkv, dv_ref[...]], axis=1).astype(BF)
        dkv2_ref[...] = dkv2
        dckv = lax.dot_general(dkv2, w_ref[...], (((1,), (1,)), ((), ())), preferred_element_type=F32)
        a = dckv * g_ref[...]
        dpk = r * (a - xh * jnp.mean(a * xh, axis=-1, keepdims=True))
        dkr = dkv[:, 0:HP]
        for hh in range(1, NH):
            dkr = dkr + dkv[:, hh * HP:(hh + 1) * HP]
        res = jnp.concatenate([dpk, dkr * ck_ref[...], dkr * sk_ref[...]], axis=1).astype(BF)
        dg = jnp.sum(dckv * xh, axis=0, keepdims=True)

        @pl.when(i == 0)
        def _():
            dg_ref[...] = dg

        @pl.when(i > 0)
        def _():
            dg_ref[...] += dg

        @pl.when(i < NLAT)
        def _():
            dp_ref[...] = res

        @pl.when(i == NLAT)
        def _():
            dpc_ref[...] = res

    rb = lambda w: pl.BlockSpec((tm, w), lambda i: (i, 0))
    return pl.pallas_call(
        body, name=name, grid=(nb,),
        in_specs=[pl.BlockSpec((tm, 512), lambda i: (0, 0)),
                  pl.BlockSpec((tm, 512), lambda i: (jnp.minimum(i, NLAT - 1), kvcol)),
                  rb(NH * HP), rb(NH * DV), _row(KVL), pl.BlockSpec((KVL, NH * HP + NH * DV), lambda i: (0, 0)),
                  rb(HP), rb(HP), pl.BlockSpec(memory_space=pl.ANY)],
        out_specs=[pl.BlockSpec((tm, 512), lambda i: (jnp.minimum(i, NLAT - 1), kvcol)),
                   pl.BlockSpec((tm, 512), lambda i: (0, 0)), rb(NH * HP + NH * DV), _row(KVL)],
        out_shape=[jax.ShapeDtypeStruct((T, NIN), BF), jax.ShapeDtypeStruct((TC, 512), BF),
                   jax.ShapeDtypeStruct((TKV, NH * HP + NH * DV), BF), jax.ShapeDtypeStruct((1, KVL), F32)],
        input_output_aliases={8: 0},
        compiler_params=pltpu.CompilerParams(dimension_semantics=("arbitrary",)),
    )(pc, p, dk, dv, kvg, wkv2, ck, sk, dp)


def _pieces(src, width, n):
    out, c = [], src
    while c < src + width:
        k = c // n
        w = min(src + width, (k + 1) * n) - c
        out.append((k, c - k * n, c - src, w))
        c += w
    return out


def _win_moves():
    mv = [(2208, 1024, O_GA), (3232, 1024, O_GC), (0, KVL, O_KV), (256, DR, O_KV + KVL + DN), (288, QL, O_Q)]
    mv += [(256 + _swap_start(g), 8, O_KV + KVL + HP + DN + 8 * g) for g in range(4)]
    for j in range(CONV // CVB):
        base = O_CV + 3 * CVB * j
        mv += [(672 + CVB * j, CVB, base), (1184 + CVB * j, CVB, base + CVB), (1696 + CVB * j, CVB, base + 2 * CVB)]
    return mv


_WIN_ZERO = [(O_KV + KVL, DN), (O_KV + KVL + DN + DR, HP - DN - DR), (O_KV + KVL + HP, DN),
             (O_KV + KVL + HP + DN + DR, HP - DN - DR), (O_Q + QL, 512 - QL)]


def build_win(g, *, name, tm=256):
    def body(g_ref, o_ref):
        for src, w, dst in _win_moves():
            for k, a, off, pw in _pieces(src, w, SH_IN):
                o_ref[:, dst + off:dst + off + pw] = g_ref[k, :, a:a + pw]
        for c0, w in _WIN_ZERO:
            o_ref[:, c0:c0 + w] = jnp.zeros((tm, w), o_ref.dtype)

    return pl.pallas_call(
        body, name=name, grid=(D // tm,), in_specs=[pl.BlockSpec((NDEV, tm, SH_IN), lambda i: (0, i, 0))],
        out_specs=pl.BlockSpec((tm, NIN), lambda i: (i, 0)), out_shape=jax.ShapeDtypeStruct((D, NIN), g.dtype),
        compiler_params=pltpu.CompilerParams(dimension_semantics=("parallel",)),
    )(g)


def shard_win_grad(dwt, dwct, *, name, tc=256):
    def body(dw_ref, dwc_ref, o_ref, kvs):
        kvs[...] = dw_ref[O_KV:O_KV + 512, :] + dwc_ref[...]

        def src(row, w):
            if O_KV <= row < O_KV + 512:
                return kvs[row - O_KV:row - O_KV + w, :]
            return dw_ref[row:row + w, :]

        for s, w, dst in _win_moves():
            if w == 8 or s == 256:
                continue
            for k, a, off, pw in _pieces(s, w, SH_IN):
                o_ref[k, a:a + pw, :] = src(dst + off, pw).astype(o_ref.dtype)
        for g in range(4):
            val = src(O_KV + KVL + DN + 8 * g, 8) + src(O_KV + KVL + HP + DN + _swap_start(g), 8)
            o_ref[0, 256 + 8 * g:256 + 8 * g + 8, :] = val.astype(o_ref.dtype)

    return pl.pallas_call(
        body, name=name, grid=(D // tc,),
        in_specs=[pl.BlockSpec((NIN, tc), lambda j: (0, j)), pl.BlockSpec((512, tc), lambda j: (0, j))],
        out_specs=pl.BlockSpec((NDEV, SH_IN, tc), lambda j: (0, 0, j)),
        out_shape=jax.ShapeDtypeStruct((NDEV, SH_IN, D), BF),
        scratch_shapes=[pltpu.VMEM((512, tc), F32)],
        compiler_params=pltpu.CompilerParams(dimension_semantics=("parallel",)),
    )(dwt, dwct)


def build_wq_wkv(gq, gkv, *, name):
    def body(gq_ref, gkv_ref, q_ref, kv_ref):
        q_ref[...] = jnp.zeros_like(q_ref)
        kv_ref[...] = jnp.zeros_like(kv_ref)
        for h in range(NH):
            q_ref[0:QL, h * HP:h * HP + DN + DR] = gq_ref[h]
            for g in range(4):
                c0 = NH * HP + h * HP + DN + 8 * g
                q_ref[0:QL, c0:c0 + 8] = gq_ref[h, :, DN + _swap_start(g):DN + _swap_start(g) + 8]
            kv_ref[:, h * HP:h * HP + DN] = gkv_ref[h, :, 0:DN]
            kv_ref[:, NH * HP + h * DV:NH * HP + (h + 1) * DV] = gkv_ref[h, :, DN:DN + DV]

    vm = pl.BlockSpec(memory_space=pltpu.VMEM)
    return pl.pallas_call(
        body, name=name, in_specs=[vm, vm], out_specs=[vm, vm],
        out_shape=[jax.ShapeDtypeStruct((512, 2 * NH * HP), gq.dtype), jax.ShapeDtypeStruct((KVL, NH * HP + NH * DV), gq.dtype)],
    )(gq, gkv)


def shard_wq_wkv_grad(dwq2, dwkv2, *, name):
    def body(q_ref, kv_ref, gq_ref, gkv_ref):
        for h in range(NH):
            gq_ref[h, :, 0:DN] = q_ref[0:QL, h * HP:h * HP + DN].astype(BF)
            for g in range(4):
                a = q_ref[0:QL, h * HP + DN + 8 * g:h * HP + DN + 8 * g + 8]
                c0 = NH * HP + h * HP + DN + _swap_start(g)
                gq_ref[h, :, DN + 8 * g:DN + 8 * g + 8] = (a + q_ref[0:QL, c0:c0 + 8]).astype(BF)
            gkv_ref[h, :, 0:DN] = kv_ref[:, h * HP:h * HP + DN].astype(BF)
            gkv_ref[h, :, DN:DN + DV] = kv_ref[:, NH * HP + h * DV:NH * HP + (h + 1) * DV].astype(BF)

    vm = pl.BlockSpec(memory_space=pltpu.VMEM)
    return pl.pallas_call(
        body, name=name, in_specs=[vm, vm], out_specs=[vm, vm],
        out_shape=[jax.ShapeDtypeStruct((NDEV, QL, (DN + DR)), BF), jax.ShapeDtypeStruct((NDEV, KVL, DN + DV), BF)],
    )(dwq2, dwkv2)


def unshard_cols(g, *, name, tm=256):
    _, K, n = g.shape
    tm = _pick(K, tm, 16)

    def body(g_ref, o_ref):
        for k in range(NDEV):
            o_ref[:, k * n:(k + 1) * n] = g_ref[k]

    return pl.pallas_call(
        body, name=name, grid=(K // tm,), in_specs=[pl.BlockSpec((NDEV, tm, n), lambda i: (0, i, 0))],
        out_specs=pl.BlockSpec((tm, NDEV * n), lambda i: (i, 0)), out_shape=jax.ShapeDtypeStruct((K, NDEV * n), g.dtype),
        compiler_params=pltpu.CompilerParams(dimension_semantics=("parallel",)),
    )(g)


def shard_cols(w, *, name, tm=256):
    K, n8 = w.shape
    n = n8 // NDEV
    tm = _pick(K, tm, 16)

    def body(w_ref, o_ref):
        for k in range(NDEV):
            o_ref[k] = w_ref[:, k * n:(k + 1) * n]

    return pl.pallas_call(
        body, name=name, grid=(K // tm,), in_specs=[pl.BlockSpec((tm, n8), lambda i: (i, 0))],
        out_specs=pl.BlockSpec((NDEV, tm, n), lambda i: (0, i, 0)), out_shape=jax.ShapeDtypeStruct((NDEV, K, n), w.dtype),
        compiler_params=pltpu.CompilerParams(dimension_semantics=("parallel",)),
    )(w)


def _rope_tables():
    t = np.arange(T)
    row = (t // GRID_W).astype(np.float32)
    col = (t % GRID_W).astype(np.float32)
    axis_dim = DR // 2
    inv = (np.float32(ROPE_THETA) ** (-np.arange(0, axis_dim, 2, dtype=np.float32) / np.float32(axis_dim))).astype(np.float32)
    ar, ac = (row[:, None] * inv).astype(np.float32), (col[:, None] * inv).astype(np.float32)
    cosv = np.concatenate([np.cos(ar), np.cos(ar), np.cos(ac), np.cos(ac)], axis=1).astype(np.float32)
    sinv = np.concatenate([-np.sin(ar), np.sin(ar), -np.sin(ac), np.sin(ac)], axis=1).astype(np.float32)
    ck = np.zeros((TKV, HP), np.float32)
    sk = np.zeros((TKV, HP), np.float32)
    ck[T:, DN:DN + DR] = 1.0
    ck[:T, DN:DN + DR] = cosv
    sk[:T, DN:DN + DR] = sinv
    cq = np.zeros((T, HP), np.float32)
    cq[:, :DN] = 1.0
    cq[:, DN:DN + DR] = cosv
    return jnp.asarray(ck), jnp.asarray(sk), jnp.asarray(cq), jnp.asarray(sk[:T])


def _local_step(x, ctx, tgt, mod_lat, mod_ctx, n1g, qg, kvg, n2g, fg, conv_w, conv_b, ffn_w, ffn_b, get_w, put_g, dep0):
    sh1, sc1, g1, sh2, sc2, g2 = [mod_lat[:, i * D:(i + 1) * D] for i in range(6)]
    csh1, csc1 = mod_ctx[:, 0:D], mod_ctx[:, D:2 * D]
    ck, sk, cq_t, sq_t = _rope_tables()
    qg_p = jnp.pad(qg, ((0, 0), (0, 512 - QL)))

    hcat = normmod_cat(ctx, x, n1g, csc1, csh1, sc1, sh1, dep0, name="normmod1")
    win = get_w("in", hcat)
    p = mm(hcat, win, M=T, tn=768, name="in_proj")
    pc = mm(hcat, win, M=TC, N=512, a_off=(T, 0), b_off=(0, O_KV), name="in_proj_ctx")
    wq2, wkv2, wao, wco, wo = get_w("mid", p)
    kh, vh, ckv = kvprep(pc, p, kvg, wkv2, ck, sk, name="kvprep")
    qr, cq = qprep(p, qg_p, wq2, cq_t, sq_t, name="qprep")
    o, lse = attn_fwd(qr, kh, vh, name="attn_fwd")
    z = convz(p, conv_w, conv_b, name="convz")
    ya, yc, merged = out_proj_merge(o, wao, z, wco, p, name="attn_conv_out_gate_merge")
    a_out, x1, h2 = oproj_resid(merged, wo, x, g1, n2g, sc2, sh2, name="o_proj_resid_normmod2")
    wup = get_w("up", h2)
    u0 = mm(h2, wup, tb=True, o_stack=True, tn=1408, name="up_proj")
    f = ffn_act(u0, ffn_w, ffn_b, name="ffn_act")
    wdn = get_w("down", f)
    dn, dx2, dd, dfg, loss = down_final(f, wdn, x1, g2, fg, tgt, name="down_proj_final_loss")

    df = mm(dd, wdn, tb=True, tn=1408, name="down_proj_dx")
    dwdn = mm(f, dd, ta=True, out_dtype=BF, tm=1408, name="down_proj_dw")
    du0, dffn_w, dffn_b = ffn_act_bwd(u0, df, ffn_w, ffn_b, name="ffn_act_bwd")
    dwup = mm(du0, h2, ta=True, a_stack=True, out_dtype=BF, tm=1408, name="up_proj_dw")
    tok = put_g("ffn", dict(dwup=dwup, dwdn=dwdn))
    dx1, da, st2 = normmod_bwd(x1, dict(a=du0, b=wup, a_stack=True, tk=1408, dep=tok), n2g, sc2, dx2, dn, g1,
                               name="up_proj_dx_normmod2_bwd")

    dwo = mm(merged, da, ta=True, out_dtype=BF, tn=512, name="o_proj_dw")
    dya, dyc, dp, do, dz = oproj_dx_gate_bwd(da, wo, p, ya, yc, wao, wco, name="o_proj_dx_gate_merge_bwd")
    dwao = mm(o, dya, ta=True, out_dtype=BF, tn=512, name="attn_out_dw")
    dwco = mm(z, dyc, ta=True, out_dtype=BF, tn=512, name="conv_out_dw")
    tok = put_g("mid", dict(dwao=dwao, dwco=dwco, dwo=dwo))
    dp, dconv_w, dconv_b = convz_bwd(p, dz, conv_w, conv_b, dp, name="convz_bwd")
    dq, dk, dv = attn_bwd(qr, kh, vh, do, o, lse, tok, name="attn_bwd")
    dp, dq2, dqg = qprep_bwd(p, dq, qg_p, wq2, cq_t, sq_t, dp, name="qprep_bwd")
    dp, dpc, dkv2, dkvg = kvprep_bwd(pc, p, dk, dv, kvg, wkv2, ck, sk, dp, name="kvprep_bwd")

    dwin = mm(dp, hcat, ta=True, K=T, tm=768, name="in_proj_dw")
    dwin_c = mm(dpc, hcat, ta=True, K=TC, b_off=(T, 0), name="in_proj_ctx_dw")
    tok = put_g("in", dict(dwin=dwin, dwin_c=dwin_c))
    dwq2 = mm(cq, dq2, ta=True, dep=tok, name="q_up_dw")
    dwkv2 = mm(ckv, dkv2, ta=True, name="kv_up_dw")
    tok = put_g("qkv", dict(dwq2=dwq2, dwkv2=dwkv2))
    dhc = mm(dpc, win, tb=True, N=D, K=512, b_off=(0, O_KV), name="in_proj_ctx_dx")
    dx, _, st1 = normmod_bwd(x, dict(a=dp, b=win, tb=True, tk=1536, dep=tok), n1g, sc1, dx1, a_out, g1,
                             name="in_proj_dx_normmod1_bwd")
    stc = normmod_bwd(ctx, dhc, n1g, csc1, None, None, None, name="normmod1_ctx_bwd")

    zrow = jnp.zeros((1, D), F32)
    dmod_lat = jnp.concatenate([st1[0:1], st1[1:2], st1[3:4], st2[0:1], st2[1:2], st2[3:4]], axis=1)
    dmod_ctx = jnp.concatenate([stc[0:1], stc[1:2], zrow, zrow, zrow, zrow], axis=1)
    return dict(
        loss=loss, dx=dx, dmod_lat=dmod_lat, dmod_ctx=dmod_ctx,
        dn1g=st1[2:3] + stc[2:3], dqg=dqg, dkvg=dkvg, dn2g=st2[2:3], dfg=dfg,
        dconv_w=dconv_w, dconv_b=dconv_b, dffn_w=dffn_w, dffn_b=dffn_b)


def _me():
    x, y, c = lax.axis_index("x"), lax.axis_index("y"), lax.axis_index("c")
    return x, y, c, 4 * x + 2 * y + c


def _peer(x, y, c, k):
    px = 1 - x if k & 4 else x
    py = 1 - y if k & 2 else y
    pc = 1 - c if k & 1 else c
    return (px, py, pc), 4 * px + 2 * py + pc


def _exchange_tiles(src_of_peer, buf, send_sem, recv_sem):
    x, y, c, me = _me()
    for k in range(1, NDEV):
        dev, lin = _peer(x, y, c, k)
        pltpu.make_async_remote_copy(src_ref=src_of_peer(lin), dst_ref=buf.at[me], send_sem=send_sem, recv_sem=recv_sem,
                                     device_id=dev, device_id_type=MESH).start()
    seven = buf.at[pl.ds(0, NDEV - 1)]
    pltpu.make_async_remote_copy(src_ref=seven, dst_ref=seven, send_sem=send_sem, recv_sem=recv_sem,
                                 device_id=(x, y, c), device_id_type=MESH).wait()


def _silu(z):
    return z * jax.nn.sigmoid(z)


def ada_fwd(c, c_ctx, ffn_w, conv_w, w_shard, b_shard, deps, *, name):
    nsh = w_shard.shape[1]
    deps = [d for d in deps if d is not None]

    def body(c_ref, cc_ref, fw_ref, cw_ref, w_ref, b_ref, *rest):
        s_ref, m_ref, mine, res, sems = rest[len(deps):]
        x, y, c, me = _me()
        mine[0:1, :] = _silu(c_ref[...])
        mine[1:2, :] = _silu(cc_ref[...])
        mine[2:5, :] = fw_ref[...]
        mine[5:8, :] = cw_ref[...]
        s_ref[me] = mine[...]
        _exchange_tiles(lambda lin: mine, s_ref, sems.at[0], sems.at[1])
        sall = s_ref[...].reshape(NDEV * 8, D).astype(BF)
        r = jnp.dot(sall, w_ref[...].astype(BF), preferred_element_type=F32) + b_ref[...]
        res[...] = r.reshape(NDEV, 8, nsh)
        m_ref[me] = res[me]
        _exchange_tiles(lambda lin: res.at[lin], m_ref, sems.at[2], sems.at[3])

    vm = pl.BlockSpec(memory_space=pltpu.VMEM)
    return pl.pallas_call(
        body, name=name, in_specs=[vm] * 6 + [pl.BlockSpec(memory_space=pl.ANY)] * len(deps), out_specs=[vm, vm],
        out_shape=[jax.ShapeDtypeStruct((NDEV, 8, D), F32), jax.ShapeDtypeStruct((NDEV, 8, nsh), F32)],
        scratch_shapes=[pltpu.VMEM((8, D), F32), pltpu.VMEM((NDEV, 8, nsh), F32), pltpu.SemaphoreType.DMA((4,))],
    )(c, c_ctx, ffn_w, conv_w, w_shard, b_shard, *deps)


P_DML, P_DMC, P_N1, P_QG, P_KVG, P_CB, P_N2, P_FB, P_FG, P_CW, P_FW, P_LOSS, P_ROWS = 0, 6, 12, 13, 14, 15, 16, 17, 23, 24, 27, 45, 48
FROWS = 3


def sync_small(r, deps, *, name):
    ins = [r["dmod_lat"], r["dmod_ctx"], r["dn1g"], r["dqg"], r["dkvg"], r["dconv_b"], r["dn2g"], r["dffn_b"], r["dfg"],
           r["dconv_w"], r["dffn_w"], r["loss"]]

    def put_wide(p, row0, row, n):
        for j in range(-(-n // D)):
            w = min(D, n - j * D)
            p[row0 + j:row0 + j + 1, 0:w] = row[:, j * D:j * D + w]

    def body(dml, dmc, n1, qg, kvg, cb, n2, fb, fg, cw, fw, loss, *rest):
        a_ref, sum_ref, p, sems = rest[len(deps):]
        x, y, c, me = _me()
        p[...] = jnp.zeros_like(p)
        put_wide(p, P_DML, dml, 6 * D)
        put_wide(p, P_DMC, dmc, 6 * D)
        put_wide(p, P_N1, n1, D)
        put_wide(p, P_QG, qg, 512)
        put_wide(p, P_KVG, kvg, KVL)
        put_wide(p, P_CB, cb, CONV)
        put_wide(p, P_N2, n2, D)
        put_wide(p, P_FG, fg, D)
        put_wide(p, P_LOSS, loss, 128)
        for s in range(2):
            put_wide(p, P_FB + FROWS * s, fb.at[s], DFF)
        for k in range(3):
            put_wide(p, P_CW + k, cw.at[k:k + 1], CONV)
            for s in range(2):
                put_wide(p, P_FW + FROWS * (2 * k + s), fw.at[s, k:k + 1], DFF)
        a_ref[me] = p[...]
        _exchange_tiles(lambda lin: p, a_ref, sems.at[0], sems.at[1])
        acc = a_ref[0]
        for k in range(1, NDEV):
            acc = acc + a_ref[k]
        sum_ref[...] = acc

    vm = pl.BlockSpec(memory_space=pltpu.VMEM)
    return pl.pallas_call(
        body, name=name, in_specs=[vm] * len(ins) + [pl.BlockSpec(memory_space=pl.ANY)] * len(deps), out_specs=[vm, vm],
        out_shape=[jax.ShapeDtypeStruct((NDEV, P_ROWS, D), F32), jax.ShapeDtypeStruct((P_ROWS, D), F32)],
        scratch_shapes=[pltpu.VMEM((P_ROWS, D), F32), pltpu.SemaphoreType.DMA((2,))],
    )(*ins, *deps)


def ada_bwd(s_all, dml, dmc, w_shard, c_ctx, *, name):
    nsh = w_shard.shape[1]

    def body(s_ref, dml_ref, dmc_ref, w_ref, c_ref, dw_ref, gc_ref, s16, dm16, part, buf, sems):
        x, y, c, me = _me()
        s16[...] = jnp.zeros_like(s16)
        dm16[...] = jnp.zeros_like(dm16)
        for k in range(NDEV):
            s16[k:k + 1, :] = s_ref[k, 0:1, :]
        s16[8:9, :] = s_ref[0, 1:2, :]
        dm16[0:8, :] = dml_ref[...]
        dm16[8:9, :] = dmc_ref[...]
        dw_ref[...] = lax.dot_general(s16[...].astype(BF), dm16[...].astype(BF), (((0,), (0,)), ((), ())),
                                      preferred_element_type=F32)
        part[...] = lax.dot_general(dm16[8:16, :].astype(BF), w_ref[...].astype(BF), (((1,), (1,)), ((), ())),
                                    preferred_element_type=F32)
        buf[me] = part[...]
        _exchange_tiles(lambda lin: part, buf, sems.at[0], sems.at[1])
        acc = buf[0]
        for k in range(1, NDEV):
            acc = acc + buf[k]
        z = c_ref[...]
        sg = jax.nn.sigmoid(z)
        gc_ref[...] = acc * (sg * (1.0 + z * (1.0 - sg)))

    vm = pl.BlockSpec(memory_space=pltpu.VMEM)
    return pl.pallas_call(
        body, name=name, in_specs=[vm] * 5, out_specs=[vm, vm],
        out_shape=[jax.ShapeDtypeStruct((D, nsh), F32), jax.ShapeDtypeStruct((8, D), F32)],
        scratch_shapes=[pltpu.VMEM((16, D), F32), pltpu.VMEM((16, nsh), F32), pltpu.VMEM((8, D), F32),
                        pltpu.VMEM((NDEV, 8, D), F32), pltpu.SemaphoreType.DMA((2,))],
    )(s_all, dml, dmc, w_shard, c_ctx)


HBM_SPEC = pl.BlockSpec(memory_space=pltpu.HBM)
SEM_SPEC = pl.BlockSpec(memory_space=pltpu.SEMAPHORE)
EFFECT = pltpu.SideEffectType.DATAFLOW_SIDE_EFFECTING


ALL_PEERS = tuple(range(1, NDEV))
FIRST_HOP = (1, 2, 4, 6)
RELAY = (2, 4, 6)


def _exchange_copies(srcs, lands, send, recv, per_peer, peers):
    x, y, c, me = _me()
    n = len(peers)
    cps = []
    for t in range(len(srcs)):
        for j, k in enumerate(peers):
            dev, lin = _peer(x, y, c, k)
            cps.append(pltpu.make_async_remote_copy(
                src_ref=srcs[t].at[lin] if per_peer else srcs[t], dst_ref=lands[t].at[me],
                send_sem=send.at[n * t + j], recv_sem=recv.at[n * t + j], device_id=dev, device_id_type=MESH))
    return cps


def _relay_copies(lands, send, recv):
    x, y, c, me = _me()
    n = len(RELAY)
    cps = []
    for t in range(len(lands)):
        for j, k in enumerate(RELAY):
            slot = lands[t].at[_peer(x, y, c, k)[1]]
            cps.append(pltpu.make_async_remote_copy(
                src_ref=slot, dst_ref=slot, send_sem=send.at[n * t + j], recv_sem=recv.at[n * t + j],
                device_id=(x, y, 1 - c), device_id_type=MESH))
    return cps


def _own_copies(srcs, lands, own, per_peer):
    me = _me()[3]
    return [pltpu.make_async_copy(srcs[t].at[me] if per_peer else srcs[t], lands[t].at[me], own.at[t])
            for t in range(len(srcs))]


def exchange_start(srcs, *, per_peer, name, dep=None, peers=ALL_PEERS):
    nt = len(srcs)
    ns = len(peers) * nt
    land_shapes = [(a.shape if per_peer else (NDEV,) + a.shape) for a in srcs]
    deps = [] if dep is None else [dep]

    def body(*refs):
        src, land = refs[:nt], refs[nt:2 * nt]
        send, recv, own = refs[2 * nt + len(deps):2 * nt + len(deps) + 3]
        for cp in _exchange_copies(src, land, send, recv, per_peer, peers) + _own_copies(src, land, own, per_peer):
            cp.start()
        refs[-1][...] = jnp.zeros_like(refs[-1])

    hb = lambda a: pltpu.with_memory_space_constraint(a, pltpu.HBM)
    outs = pl.pallas_call(
        body, name=name,
        out_shape=(pltpu.SemaphoreType.DMA((ns,)), pltpu.SemaphoreType.DMA((ns,)), pltpu.SemaphoreType.DMA((nt,)),
                   *[pltpu.HBM(a.shape, a.dtype) for a in srcs], *[pltpu.HBM(s, a.dtype) for s, a in zip(land_shapes, srcs)],
                   jax.ShapeDtypeStruct((8, 128), F32)),
        in_specs=[HBM_SPEC] * (2 * nt) + [pl.BlockSpec(memory_space=pl.ANY)] * len(deps),
        out_specs=(SEM_SPEC, SEM_SPEC, SEM_SPEC, *([HBM_SPEC] * (2 * nt)), pl.BlockSpec(memory_space=pltpu.VMEM)),
        input_output_aliases={i: 3 + i for i in range(2 * nt)},
        compiler_params=pltpu.CompilerParams(has_side_effects=EFFECT),
    )(*[hb(a) for a in srcs], *[hb(lax.empty(s, a.dtype)) for s, a in zip(land_shapes, srcs)], *deps)
    return dict(send=outs[0], recv=outs[1], own=outs[2], src=list(outs[3:3 + nt]), land=list(outs[3 + nt:3 + 2 * nt]),
                token=outs[-1], per_peer=per_peer, peers=peers)


def exchange_wait(h, after, *, name):
    nt = len(h["src"])
    per_peer, peers = h["per_peer"], h["peers"]

    def body(*refs):
        src, land, send, recv, own = refs[:nt], refs[nt:2 * nt], refs[2 * nt], refs[2 * nt + 1], refs[2 * nt + 2]
        for cp in _exchange_copies(src, land, send, recv, per_peer, peers):
            cp.wait_send()
            cp.wait_recv()
        for cp in _own_copies(src, land, own, per_peer):
            cp.wait()

    outs = pl.pallas_call(
        body, name=name,
        out_shape=(*[pltpu.HBM(a.shape, a.dtype) for a in h["src"]], *[pltpu.HBM(a.shape, a.dtype) for a in h["land"]]),
        in_specs=[HBM_SPEC] * (2 * nt) + [SEM_SPEC, SEM_SPEC, SEM_SPEC, pl.BlockSpec(memory_space=pl.ANY)],
        out_specs=tuple([HBM_SPEC] * (2 * nt)),
        input_output_aliases={i: i for i in range(2 * nt)},
        compiler_params=pltpu.CompilerParams(has_side_effects=EFFECT),
    )(*h["src"], *h["land"], h["send"], h["recv"], h["own"], after)
    return list(outs[nt:])


def relay_start(lands, *, name):
    nt = len(lands)
    ns = len(RELAY) * nt

    def body(*refs):
        for cp in _relay_copies(refs[:nt], refs[nt], refs[nt + 1]):
            cp.start()

    outs = pl.pallas_call(
        body, name=name,
        out_shape=(pltpu.SemaphoreType.DMA((ns,)), pltpu.SemaphoreType.DMA((ns,)),
                   *[pltpu.HBM(a.shape, a.dtype) for a in lands]),
        in_specs=[HBM_SPEC] * nt, out_specs=(SEM_SPEC, SEM_SPEC, *([HBM_SPEC] * nt)),
        input_output_aliases={i: 2 + i for i in range(nt)},
        compiler_params=pltpu.CompilerParams(has_side_effects=EFFECT),
    )(*lands)
    return dict(send=outs[0], recv=outs[1], land=list(outs[2:]))


def relay_wait(h, *, name):
    nt = len(h["land"])

    def body(*refs):
        for cp in _relay_copies(refs[:nt], refs[nt], refs[nt + 1]):
            cp.wait_send()
            cp.wait_recv()

    outs = pl.pallas_call(
        body, name=name, out_shape=tuple(pltpu.HBM(a.shape, a.dtype) for a in h["land"]),
        in_specs=[HBM_SPEC] * nt + [SEM_SPEC, SEM_SPEC], out_specs=tuple([HBM_SPEC] * nt),
        input_output_aliases={i: i for i in range(nt)},
        compiler_params=pltpu.CompilerParams(has_side_effects=EFFECT),
    )(*h["land"], h["send"], h["recv"])
    return list(outs)


def _adamw_math(w, g, m, v):
    nm = B1 * m + (1.0 - B1) * g
    nv = B2 * v + (1.0 - B2) * (g * g)
    m_hat = nm / (1.0 - B1 ** STEP)
    v_hat = nv / (1.0 - B2 ** STEP)
    return -LR * (m_hat / (jnp.sqrt(v_hat) + AEPS) + WD * w), nm, nv


def adamw_many(ws, gs, ms, vs, *, name):
    n = len(ws)

    def body(*refs):
        for k in range(n):
            d, nm, nv = _adamw_math(refs[k][...], refs[n + k][...], refs[2 * n + k][...], refs[3 * n + k][...])
            refs[4 * n + k][...] = d
            refs[5 * n + k][...] = nm
            refs[6 * n + k][...] = nv

    vm = pl.BlockSpec(memory_space=pltpu.VMEM)
    sh = [jax.ShapeDtypeStruct(w.shape, F32) for w in ws]
    outs = pl.pallas_call(body, name=name, in_specs=[vm] * (4 * n), out_specs=[vm] * (3 * n), out_shape=sh * 3,
                          )(*ws, *gs, *ms, *vs)
    return outs[:n], outs[n:2 * n], outs[2 * n:]


def adamw(w, g, m, v, *, name, tr=256):
    R, C = w.shape
    tr = _pick(R, tr, 8)

    def body(w_ref, g_ref, m_ref, v_ref, d_ref, nm_ref, nv_ref):
        d_ref[...], nm_ref[...], nv_ref[...] = _adamw_math(w_ref[...], g_ref[...], m_ref[...], v_ref[...])

    blk = pl.BlockSpec((tr, C), lambda i: (i, 0))
    sh = jax.ShapeDtypeStruct((R, C), F32)
    return pl.pallas_call(
        body, name=name, grid=(R // tr,), in_specs=[blk, blk, blk, blk], out_specs=[blk, blk, blk],
        out_shape=[sh, sh, sh], compiler_params=pltpu.CompilerParams(dimension_semantics=("parallel",)),
    )(w, g, m, v)


def adamw_slots(w, slots, m, v, *, name, tr=256):
    unit = w.ndim == 3
    R, C = w.shape[0], w.shape[-1]
    if R % 16 == 0:
        tr = _pick(R, tr, 16)
    else:
        tr = 144

    def body(w_ref, s_ref, m_ref, v_ref, g_ref, d_ref, nm_ref, nv_ref):
        g = s_ref[0].astype(F32)
        for k in range(1, NDEV):
            g = g + s_ref[k].astype(F32)
        g_ref[...] = g
        d_ref[...], nm_ref[...], nv_ref[...] = _adamw_math(w_ref[...], g, m_ref[...], v_ref[...])

    blk = pl.BlockSpec((tr, None, C), lambda i: (i, 0, 0)) if unit else pl.BlockSpec((tr, C), lambda i: (i, 0))
    sh = jax.ShapeDtypeStruct(w.shape, F32)
    return pl.pallas_call(
        body, name=name, grid=(pl.cdiv(R, tr),), in_specs=[blk, pl.BlockSpec((NDEV, tr, C), lambda i: (0, i, 0)), blk, blk],
        out_specs=[blk, blk, blk, blk], out_shape=[sh, sh, sh, sh],
        compiler_params=pltpu.CompilerParams(dimension_semantics=("parallel",)),
    )(w, slots, m, v)


def _padc(a, n=D):
    return jnp.pad(a, ((0, 0), (0, n - a.shape[1])))


def kernel(x, c, ctx, c_ctx, w_ada, b_ada, norm1_g, w_in, q_norm_g, kv_norm_g, w_uq, w_ukv, conv_w, conv_b, w_attn_out, w_conv_out, w_o, norm2_g, w_up, ffn_conv_w, ffn_conv_b, w_down, final_g, loss_target, m_c_ctx, m_w_ada, m_b_ada, m_norm1_g, m_w_in, m_q_norm_g, m_kv_norm_g, m_w_uq, m_w_ukv, m_conv_w, m_conv_b, m_w_attn_out, m_w_conv_out, m_w_o, m_norm2_g, m_w_up, m_ffn_conv_w, m_ffn_conv_b, m_w_down, m_final_g, v_c_ctx, v_w_ada, v_b_ada, v_norm1_g, v_w_in, v_q_norm_g, v_kv_norm_g, v_w_uq, v_w_ukv, v_conv_w, v_conv_b, v_w_attn_out, v_w_conv_out, v_w_o, v_norm2_g, v_w_up, v_ffn_conv_w, v_ffn_conv_b, v_w_down, v_final_g):
    me = 4 * lax.axis_index("x") + 2 * lax.axis_index("y") + lax.axis_index("c")
    W = dict(c_ctx=c_ctx, w_ada=w_ada, b_ada=b_ada, norm1_g=norm1_g, w_in=w_in, q_norm_g=q_norm_g, kv_norm_g=kv_norm_g,
             w_uq=w_uq, w_ukv=w_ukv, conv_w=conv_w, conv_b=conv_b, w_attn_out=w_attn_out, w_conv_out=w_conv_out, w_o=w_o,
             norm2_g=norm2_g, w_up=w_up, ffn_conv_w=ffn_conv_w, ffn_conv_b=ffn_conv_b, w_down=w_down, final_g=final_g)
    M = dict(c_ctx=m_c_ctx, w_ada=m_w_ada, b_ada=m_b_ada, norm1_g=m_norm1_g, w_in=m_w_in, q_norm_g=m_q_norm_g,
             kv_norm_g=m_kv_norm_g, w_uq=m_w_uq, w_ukv=m_w_ukv, conv_w=m_conv_w, conv_b=m_conv_b, w_attn_out=m_w_attn_out,
             w_conv_out=m_w_conv_out, w_o=m_w_o, norm2_g=m_norm2_g, w_up=m_w_up, ffn_conv_w=m_ffn_conv_w,
             ffn_conv_b=m_ffn_conv_b, w_down=m_w_down, final_g=m_final_g)
    V = dict(c_ctx=v_c_ctx, w_ada=v_w_ada, b_ada=v_b_ada, norm1_g=v_norm1_g, w_in=v_w_in, q_norm_g=v_q_norm_g,
             kv_norm_g=v_kv_norm_g, w_uq=v_w_uq, w_ukv=v_w_ukv, conv_w=v_conv_w, conv_b=v_conv_b, w_attn_out=v_w_attn_out,
             w_conv_out=v_w_conv_out, w_o=v_w_o, norm2_g=v_norm2_g, w_up=v_w_up, ffn_conv_w=v_ffn_conv_w,
             ffn_conv_b=v_ffn_conv_b, w_down=v_w_down, final_g=v_final_g)
    names = list(W)
    transposed = ("w_up",)
    as2d = lambda k, a: (a.reshape(1, -1) if a.ndim == 1 else
                         a[0].T if k in transposed else a.reshape(a.shape[-2], a.shape[-1]))
    W2 = {k: as2d(k, a) for k, a in W.items()}
    M2 = {k: as2d(k, a) for k, a in M.items()}
    V2 = {k: as2d(k, a) for k, a in V.items()}
    unit3 = lambda a: jnp.transpose(a, (2, 0, 1))
    W3, M3, V3 = unit3(W["w_in"]), unit3(M["w_in"]), unit3(V["w_in"])
    nsh = W2["w_ada"].shape[1]

    b_sh = lax.dynamic_slice(W2["b_ada"], (0, me * nsh), (1, nsh))
    s_all, m_all = ada_fwd(c, W2["c_ctx"], _padc(W2["ffn_conv_w"]), _padc(W2["conv_w"]), W2["w_ada"], b_sh, [],
                           name="ada_fwd")
    mod_lat = m_all[:, 0, :].reshape(1, 6 * D)
    mod_ctx = m_all[:, 1, :].reshape(1, 6 * D)
    ffn_w_full = s_all[:, 2:5, :2 * DFF // NDEV].transpose(1, 0, 2).reshape(3, 2 * DFF)
    conv_w_full = s_all[:, 5:8, :CONV // NDEV].transpose(1, 0, 2).reshape(3, CONV)

    stage_w = {"in": ["w_in"], "mid": ["w_uq", "w_ukv", "w_attn_out", "w_conv_out", "w_o"], "up": ["w_up"],
               "down": ["w_down"]}
    two_level = ("in", "mid")
    ag, tok = {}, m_all
    for st, nms in stage_w.items():
        ag[st] = exchange_start([W2[nm].astype(BF) for nm in nms], per_peer=False, dep=tok, name="ag_start_" + st,
                                peers=FIRST_HOP if st in two_level else ALL_PEERS)
        tok = ag[st]["token"]

    def get_w(stage, after):
        lands = exchange_wait(ag[stage], after, name="ag_wait_" + stage)
        if stage in two_level:
            lands = relay_wait(relay_start(lands, name="ag_relay_" + stage), name="ag_relay_wait_" + stage)
        g = dict(zip(stage_w[stage], lands))
        if stage == "in":
            return build_win(g["w_in"], name="build_win")
        if stage == "mid":
            wq2, wkv2 = build_wq_wkv(g["w_uq"], g["w_ukv"], name="build_wq_wkv")
            return (wq2, wkv2, unshard_cols(g["w_attn_out"], name="unshard_w_attn_out"),
                    unshard_cols(g["w_conv_out"], name="unshard_w_conv_out"), g["w_o"].reshape(D, D))
        if stage == "up":
            return g["w_up"].reshape(2 * DFF, D)
        return g["w_down"].reshape(DFF, D)

    stage_g = {"ffn": ["w_up", "w_down"], "mid": ["w_attn_out", "w_conv_out", "w_o"], "qkv": ["w_uq", "w_ukv"],
               "in": ["w_in"]}
    rs = {}

    def put_g(stage, g):
        if stage == "in":
            parts = [shard_win_grad(g["dwin"], g["dwin_c"], name="shard_win_grad")]
        elif stage == "mid":
            parts = [shard_cols(g["dwao"], name="shard_w_attn_out"), shard_cols(g["dwco"], name="shard_w_conv_out"),
                     g["dwo"].reshape(NDEV, D // NDEV, D)]
        elif stage == "qkv":
            parts = list(shard_wq_wkv_grad(g["dwq2"], g["dwkv2"], name="shard_wq_wkv_grad"))
        else:
            parts = [g["dwup"].reshape(NDEV, 2 * DFF // NDEV, D), g["dwdn"].reshape(NDEV, DFF // NDEV, D)]
        rs[stage] = exchange_start(parts, per_peer=True, name="rs_start_" + stage)
        return rs[stage]["token"]

    r = _local_step(x[0], ctx[0], loss_target[0], mod_lat, mod_ctx, W2["norm1_g"], W2["q_norm_g"], W2["kv_norm_g"],
                    W2["norm2_g"], W2["final_g"], conv_w_full, W2["conv_b"], ffn_w_full, W2["ffn_conv_b"], get_w, put_g,
                    ag["down"]["token"])

    G, DL, NM, NV = {}, {}, {}, {}

    def finish(stage, after):
        for nm, sl in zip(stage_g[stage], exchange_wait(rs[stage], after, name="rs_wait_" + stage)):
            wmv = (W3, M3, V3) if nm == "w_in" else (W2[nm], M2[nm], V2[nm])
            G[nm], DL[nm], NM[nm], NV[nm] = adamw_slots(wmv[0], sl, wmv[1], wmv[2], name="adamw_" + nm)
            after = DL[nm]
        return after

    after = r["dx"]
    for st in ("ffn", "mid"):
        after = finish(st, after)

    a_buf, ssum = sync_small(r, [DL[nm] for st in ("ffn", "mid") for nm in stage_g[st]], name="sync_small")
    loss = ssum[P_LOSS, 0]
    G["norm1_g"] = ssum[P_N1:P_N1 + 1]
    G["q_norm_g"] = ssum[P_QG:P_QG + 1, :QL]
    G["kv_norm_g"] = ssum[P_KVG:P_KVG + 1, :KVL]
    G["conv_b"] = ssum[P_CB:P_CB + 1, :CONV]
    G["norm2_g"] = ssum[P_N2:P_N2 + 1]
    G["ffn_conv_b"] = ssum[P_FB:P_FB + 2 * FROWS].reshape(1, 2, FROWS * D)[:, :, :DFF].reshape(1, 2 * DFF)
    G["final_g"] = ssum[P_FG:P_FG + 1]
    G["conv_w"] = lax.dynamic_slice(ssum[P_CW:P_CW + 3, :CONV], (0, me * (CONV // NDEV)), (3, CONV // NDEV))
    fw_full = ssum[P_FW:P_FW + 6 * FROWS].reshape(3, 2, FROWS * D)[:, :, :DFF].reshape(3, 2 * DFF)
    G["ffn_conv_w"] = lax.dynamic_slice(fw_full, (0, me * (2 * DFF // NDEV)), (3, 2 * DFF // NDEV))
    G["b_ada"] = (ssum[P_DML:P_DML + 6] + ssum[P_DMC:P_DMC + 6]).reshape(1, 6 * D)

    dml = lax.dynamic_slice(a_buf[:, P_DML:P_DML + 6, :].reshape(NDEV, 6 * D), (0, me * nsh), (NDEV, nsh))
    dmc = lax.dynamic_slice(ssum[P_DMC:P_DMC + 6].reshape(1, 6 * D), (0, me * nsh), (1, nsh))
    G["w_ada"], gcc = ada_bwd(s_all, dml, dmc, W2["w_ada"], W2["c_ctx"], name="ada_bwd")
    G["c_ctx"] = gcc[0:1]

    DL["w_ada"], NM["w_ada"], NV["w_ada"] = adamw(W2["w_ada"], G["w_ada"], M2["w_ada"], V2["w_ada"], name="adamw_w_ada")
    small = ["c_ctx", "b_ada", "norm1_g", "q_norm_g", "kv_norm_g", "conv_b", "norm2_g", "ffn_conv_b", "final_g", "conv_w",
             "ffn_conv_w"]
    ds, nms, nvs = adamw_many([W2[k] for k in small], [G[k] for k in small], [M2[k] for k in small],
                              [V2[k] for k in small], name="adamw_small")
    for k, nm in enumerate(small):
        DL[nm], NM[nm], NV[nm] = ds[k], nms[k], nvs[k]
    finish("qkv", finish("in", ds[0]))

    outs = [loss, r["dx"][None]]
    for grp in (G, DL, NM, NV):
        outs += [grp[nm].T[None] if nm in transposed else
                 jnp.transpose(grp[nm], (1, 2, 0)) if nm == "w_in" else grp[nm].reshape(W[nm].shape) for nm in names]
    return tuple(outs)
```

```python
import functools
import numpy as np
import jax
import jax.numpy as jnp
from jax import lax
from jax.experimental import pallas as pl
from jax.experimental.pallas import tpu as pltpu

F32 = jnp.float32
BF = jnp.bfloat16
MESH = pl.DeviceIdType.MESH

D = 1024
T = 2048
TC = 256
TKV = T + TC
GRID_W = 64
NH = 8
DN = 64
DR = 32
DV = 64
QL = 384
KVL = 256
CONV = 512
DFF = 2816
EPS = 1e-6
ROPE_THETA = 10000.0
SCALE = (DN + DR) ** -0.5
NDEV = 8
HP = 128

O_GA, O_GC, O_KV, O_Q, O_CV = 0, 1024, 2048, 2560, 3072
NIN = 4608
CVB = 256
N_IN = 4256
SH_IN = N_IN // NDEV

LR, B1, B2, AEPS, WD, STEP = 0.001, 0.9, 0.999, 1e-08, 0.01, 10


def _pick(n, target, mult=128):
    best = None
    for d in range(mult, min(n, target) + 1, mult):
        if n % d == 0:
            best = d
    return best if best is not None else n


def _swap_start(g):
    return 8 * (g ^ 1)


def mm(a, b, *, ta=False, tb=False, out_dtype=F32, name, tm=1024, tn=1024, tk=2048, M=None, N=None, K=None,
       a_off=(0, 0), b_off=(0, 0), a_stack=False, b_stack=False, o_stack=False, dep=None):
    def dims(arr, stack):
        return (arr.shape[1], 2 * arr.shape[2]) if stack else arr.shape

    ar, ac = dims(a, a_stack)
    br, bc = dims(b, b_stack)
    M = M or ((ac if ta else ar) - a_off[1 if ta else 0])
    K = K or ((ar if ta else ac) - a_off[0 if ta else 1])
    N = N or ((br if tb else bc) - b_off[0 if tb else 1])
    tm = _pick(M, tm, 128 if ta else 16)
    tn = _pick(N // 2 if (o_stack or (b_stack and not tb)) else N, tn, 128)
    tk = _pick(K // 2 if ((a_stack and not ta) or (b_stack and tb)) else K, tk, 128)
    nk = K // tk
    ca = 0 if ta else 1
    cb = 1 if tb else 0

    def body(a_ref, b_ref, *rest):
        o_ref, acc = rest[-2:]
        k = pl.program_id(2)
        part = lax.dot_general(a_ref[...].astype(BF), b_ref[...].astype(BF),
                               (((ca,), (cb,)), ((), ())), preferred_element_type=F32)
        if nk == 1:
            o_ref[...] = part.astype(o_ref.dtype)
        else:
            @pl.when(k == 0)
            def _():
                acc[...] = part

            @pl.when(k > 0)
            def _():
                acc[...] += part

            @pl.when(k == nk - 1)
            def _():
                o_ref[...] = acc[...].astype(o_ref.dtype)

    def spec(blk, rc, off, stack, ncols):
        assert off[0] % blk[0] == 0 and off[1] % blk[1] == 0, (name, blk, off)
        ro, co = off[0] // blk[0], off[1] // blk[1]
        if not stack:
            return pl.BlockSpec(blk, lambda i, j, k: (rc(i, j, k)[0] + ro, rc(i, j, k)[1] + co))
        nhb = ncols // 2 // blk[1]
        return pl.BlockSpec((None,) + blk,
                            lambda i, j, k: ((rc(i, j, k)[1] + co) // nhb, rc(i, j, k)[0] + ro, (rc(i, j, k)[1] + co) % nhb))

    a_spec = spec((tk, tm), lambda i, j, k: (k, i), a_off, a_stack, ac) if ta else \
        spec((tm, tk), lambda i, j, k: (i, k), a_off, a_stack, ac)
    b_spec = spec((tn, tk), lambda i, j, k: (j, k), b_off, b_stack, bc) if tb else \
        spec((tk, tn), lambda i, j, k: (k, j), b_off, b_stack, bc)
    o_spec = spec((tm, tn), lambda i, j, k: (i, j), (0, 0), o_stack, N)
    o_shape = (2, M, N // 2) if o_stack else (M, N)
    deps = [] if dep is None else [dep]
    return pl.pallas_call(
        body, name=name, grid=(M // tm, N // tn, nk),
        in_specs=[a_spec, b_spec] + [pl.BlockSpec(memory_space=pl.ANY)] * len(deps),
        out_specs=o_spec, out_shape=jax.ShapeDtypeStruct(o_shape, out_dtype),
        scratch_shapes=[pltpu.VMEM((tm, tn) if nk > 1 else (8, 128), F32)],
        compiler_params=pltpu.CompilerParams(dimension_semantics=("parallel", "parallel", "arbitrary")),
    )(a, b, *deps)


def _row(width):
    return pl.BlockSpec((1, width), lambda *_: (0, 0))


NLAT = T // TC


def normmod_cat(ctx, x, g, csc, csh, sc, sh, dep, *, name, tm=256):
    assert tm == TC

    def body(c_ref, x_ref, g_ref, csc_ref, csh_ref, sc_ref, sh_ref, dep_ref, h_ref):
        last = pl.program_id(0) == NLAT
        xv = jnp.where(last, c_ref[...], x_ref[...])
        scv = jnp.where(last, csc_ref[...], sc_ref[...])
        shv = jnp.where(last, csh_ref[...], sh_ref[...])
        r = lax.rsqrt(jnp.mean(xv * xv, axis=-1, keepdims=True) + EPS)
        h_ref[...] = ((xv * r * g_ref[...]) * (1.0 + scv) + shv).astype(BF)

    return pl.pallas_call(
        body, name=name, grid=(TKV // tm,),
        in_specs=[pl.BlockSpec((tm, D), lambda i: (0, 0)), pl.BlockSpec((tm, D), lambda i: (jnp.minimum(i, NLAT - 1), 0)),
                  _row(D), _row(D), _row(D), _row(D), _row(D), pl.BlockSpec(memory_space=pl.ANY)],
        out_specs=pl.BlockSpec((tm, D), lambda i: (i, 0)), out_shape=jax.ShapeDtypeStruct((TKV, D), BF),
        compiler_params=pltpu.CompilerParams(dimension_semantics=("parallel",)),
    )(ctx, x, g, csc, csh, sc, sh, dep)


def kvprep(pc, p, kvg, wkv2, ck, sk, *, name, tm=256):
    assert tm == TC
    nb = TKV // tm
    kvcol = O_KV // 512

    def body(pc_ref, p_ref, g_ref, w_ref, ck_ref, sk_ref, k_ref, v_ref, ckv_ref):
        i = pl.program_id(0)
        t = jnp.where(i == NLAT, pc_ref[...], p_ref[...])
        pk = t[:, :KVL]
        r = lax.rsqrt(jnp.mean(pk * pk, axis=-1, keepdims=True) + EPS)
        ckv = (pk * r * g_ref[...]).astype(BF)
        ckv_ref[...] = ckv
        kv2 = jnp.dot(ckv, w_ref[...], preferred_element_type=F32)
        krr = t[:, KVL:KVL + HP] * ck_ref[...] + t[:, KVL + HP:KVL + 2 * HP] * sk_ref[...]
        k_ref[...] = (kv2[:, :NH * HP] + jnp.concatenate([krr] * NH, axis=1)).astype(BF)
        v_ref[...] = kv2[:, NH * HP:].astype(BF)

    return pl.pallas_call(
        body, name=name, grid=(nb,),
        in_specs=[pl.BlockSpec((tm, 512), lambda i: (0, 0)),
                  pl.BlockSpec((tm, 512), lambda i: (jnp.minimum(i, NLAT - 1), kvcol)),
                  _row(KVL), pl.BlockSpec((KVL, NH * HP + NH * DV), lambda i: (0, 0)),
                  pl.BlockSpec((tm, HP), lambda i: (i, 0)), pl.BlockSpec((tm, HP), lambda i: (i, 0))],
        out_specs=[pl.BlockSpec((tm, NH * HP), lambda i: (i, 0)), pl.BlockSpec((tm, NH * DV), lambda i: (i, 0)),
                   pl.BlockSpec((tm, KVL), lambda i: (i, 0))],
        out_shape=[jax.ShapeDtypeStruct((TKV, NH * HP), BF), jax.ShapeDtypeStruct((TKV, NH * DV), BF),
                   jax.ShapeDtypeStruct((TKV, KVL), BF)],
        compiler_params=pltpu.CompilerParams(dimension_semantics=("parallel",)),
    )(pc, p, kvg, wkv2, ck, sk)


def qprep(p, qg, wq2, cq_t, sq_t, *, name, tm=256):
    qcol = O_Q // 512

    def body(p_ref, g_ref, w_ref, c_ref, s_ref, q_ref, cq_ref):
        pq = p_ref[...]
        r = lax.rsqrt(jnp.sum(pq * pq, axis=-1, keepdims=True) * (1.0 / QL) + EPS)
        cq = (pq * r * g_ref[...]).astype(BF)
        cq_ref[...] = cq
        q2 = jnp.dot(cq, w_ref[...], preferred_element_type=F32)
        cc = jnp.concatenate([c_ref[...]] * NH, axis=1)
        ss = jnp.concatenate([s_ref[...]] * NH, axis=1)
        q_ref[...] = (q2[:, :NH * HP] * cc + q2[:, NH * HP:] * ss).astype(BF)

    return pl.pallas_call(
        body, name=name, grid=(T // tm,),
        in_specs=[pl.BlockSpec((tm, 512), lambda i: (i, qcol)), _row(512),
                  pl.BlockSpec((512, 2 * NH * HP), lambda i: (0, 0)),
                  pl.BlockSpec((tm, HP), lambda i: (i, 0)), pl.BlockSpec((tm, HP), lambda i: (i, 0))],
        out_specs=[pl.BlockSpec((tm, NH * HP), lambda i: (i, 0)), pl.BlockSpec((tm, 512), lambda i: (i, 0))],
        out_shape=[jax.ShapeDtypeStruct((T, NH * HP), BF), jax.ShapeDtypeStruct((T, 512), BF)],
        compiler_params=pltpu.CompilerParams(dimension_semantics=("parallel",)),
    )(p, qg, wq2, cq_t, sq_t)


def _head_mask(h):
    lanes = lax.broadcasted_iota(jnp.int32, (1, 2 * DV), 1)
    return (lanes // DV) == (h % 2)


LOG2E = 1.4426950408889634


def attn_fwd(q, k, v, *, name, tq=1024, kc=768):
    def body(q_ref, k_ref, v_ref, o_ref, lse_ref):
        h = pl.program_id(1)
        qv = q_ref[...]
        m = l = acc = None
        for c in range(TKV // kc):
            s = lax.dot_general(qv, k_ref[c * kc:(c + 1) * kc, :], (((1,), (1,)), ((), ())),
                                preferred_element_type=F32) * (SCALE * LOG2E)
            mc = jnp.max(s, axis=-1, keepdims=True)
            if c == 0:
                m = mc
                e = jnp.exp2(s - m)
                l = jnp.sum(e, axis=-1, keepdims=True)
                acc = jnp.dot(e.astype(BF), v_ref[c * kc:(c + 1) * kc, :], preferred_element_type=F32)
            else:
                mn = jnp.maximum(m, mc)
                a = jnp.exp2(m - mn)
                e = jnp.exp2(s - mn)
                l = l * a + jnp.sum(e, axis=-1, keepdims=True)
                acc = acc * a + jnp.dot(e.astype(BF), v_ref[c * kc:(c + 1) * kc, :], preferred_element_type=F32)
                m = mn
        o2 = jnp.where(_head_mask(h), acc * (1.0 / l), 0.0).astype(BF)
        lse_ref[...] = jnp.broadcast_to(m + jnp.log(l) * LOG2E, (tq, HP))

        @pl.when(h % 2 == 0)
        def _():
            o_ref[...] = o2

        @pl.when(h % 2 == 1)
        def _():
            o_ref[...] = o_ref[...] + o2

    return pl.pallas_call(
        body, name=name, grid=(T // tq, NH),
        in_specs=[pl.BlockSpec((tq, HP), lambda i, h: (i, h)), pl.BlockSpec((TKV, HP), lambda i, h: (0, h)),
                  pl.BlockSpec((TKV, 2 * DV), lambda i, h: (0, h // 2))],
        out_specs=[pl.BlockSpec((tq, 2 * DV), lambda i, h: (i, h // 2)), pl.BlockSpec((tq, HP), lambda i, h: (i, h))],
        out_shape=[jax.ShapeDtypeStruct((T, NH * DV), BF), jax.ShapeDtypeStruct((T, NH * HP), F32)],
        compiler_params=pltpu.CompilerParams(dimension_semantics=("parallel", "arbitrary")),
    )(q, k, v)


def _shift_dn(x):
    n = x.shape[0]
    rows = lax.broadcasted_iota(jnp.int32, (n, 1), 0)
    return jnp.where(rows == 0, 0.0, pltpu.roll(x, 1, axis=0))


def _shift_up(x):
    n = x.shape[0]
    rows = lax.broadcasted_iota(jnp.int32, (n, 1), 0)
    return jnp.where(rows == n - 1, 0.0, pltpu.roll(x, n - 1, axis=0))


def _conv(x, w_ref, b_ref):
    return b_ref[...] + _shift_dn(x) * w_ref[0:1, :] + x * w_ref[1:2, :] + _shift_up(x) * w_ref[2:3, :]


def _conv_t(dy, w_ref):
    return _shift_up(dy) * w_ref[0:1, :] + dy * w_ref[1:2, :] + _shift_dn(dy) * w_ref[2:3, :]


def _conv_wgrad(dw_ref, dy, x):
    dw_ref[0:1, :] = jnp.sum(dy * _shift_dn(x), axis=0, keepdims=True)
    dw_ref[1:2, :] = jnp.sum(dy * x, axis=0, keepdims=True)
    dw_ref[2:3, :] = jnp.sum(dy * _shift_up(x), axis=0, keepdims=True)


def convz(p, cw, cb, *, name):
    o0 = O_CV // (3 * CVB)

    def body(p_ref, w_ref, bias_ref, z_ref):
        xv, bv, cv = p_ref[:, 0:CVB], p_ref[:, CVB:2 * CVB], p_ref[:, 2 * CVB:3 * CVB]
        z_ref[...] = (bv * _conv(cv * xv, w_ref, bias_ref)).astype(BF)

    return pl.pallas_call(
        body, name=name, grid=(CONV // CVB,),
        in_specs=[pl.BlockSpec((T, 3 * CVB), lambda j: (0, o0 + j)), pl.BlockSpec((3, CVB), lambda j: (0, j)),
                  pl.BlockSpec((1, CVB), lambda j: (0, j))],
        out_specs=pl.BlockSpec((T, CVB), lambda j: (0, j)),
        out_shape=jax.ShapeDtypeStruct((T, CONV), BF),
        compiler_params=pltpu.CompilerParams(dimension_semantics=("parallel",)),
    )(p, cw, cb)


def out_proj_merge(o, wao, z, wco, p, *, name, tm=512):
    kin = o.shape[1]

    def body(o_ref, wa_ref, z_ref, wc_ref, ga_ref, gc_ref, ya_ref, yc_ref, m_ref):
        ya = jnp.dot(o_ref[...], wa_ref[...], preferred_element_type=F32)
        yc = jnp.dot(z_ref[...], wc_ref[...], preferred_element_type=F32)
        ya_ref[...] = ya
        yc_ref[...] = yc
        m_ref[...] = (jax.nn.sigmoid(ga_ref[...]) * ya + jax.nn.sigmoid(gc_ref[...]) * yc).astype(BF)

    blk = pl.BlockSpec((tm, D), lambda i: (i, 0))
    act = pl.BlockSpec((tm, kin), lambda i: (i, 0))
    wsp = pl.BlockSpec((kin, D), lambda i: (0, 0))
    sh = jax.ShapeDtypeStruct((T, D), F32)
    return pl.pallas_call(
        body, name=name, grid=(T // tm,),
        in_specs=[act, wsp, act, wsp, pl.BlockSpec((tm, D), lambda i: (i, O_GA // D)),
                  pl.BlockSpec((tm, D), lambda i: (i, O_GC // D))],
        out_specs=[blk, blk, blk], out_shape=[sh, sh, jax.ShapeDtypeStruct((T, D), BF)],
        compiler_params=pltpu.CompilerParams(dimension_semantics=("parallel",)),
    )(o, wao, z, wco, p, p)


CONV_HALO = 8
CONV_ROWS = 256


def _row_chunks(n, chunk, carry):
    carry = chunk(0, True, False, carry)
    carry = lax.fori_loop(1, n // CONV_ROWS - 1, lambda c, a: chunk(c * CONV_ROWS, False, False, a), carry)
    return chunk(n - CONV_ROWS, False, True, carry)


def _ext_rows(ref, r0, first, last):
    n, w = ref.shape
    zero = jnp.zeros((CONV_HALO, w), ref.dtype)
    if first:
        return jnp.concatenate([zero, ref[0:CONV_ROWS + CONV_HALO, :]], axis=0)
    if last:
        return jnp.concatenate([ref[n - CONV_ROWS - CONV_HALO:n, :], zero], axis=0)
    return ref[pl.ds(pl.multiple_of(r0 - CONV_HALO, 8), CONV_ROWS + 2 * CONV_HALO), :]


def _center_rows(r0, first, last):
    return slice(r0, r0 + CONV_ROWS) if (first or last) else pl.ds(pl.multiple_of(r0, 8), CONV_ROWS)


def _roll_dn(x):
    return pltpu.roll(x, 1, axis=0)


def _roll_up(x):
    return pltpu.roll(x, x.shape[0] - 1, axis=0)


_CTR = slice(CONV_HALO, CONV_HALO + CONV_ROWS)


def ffn_act(u0, cw, cb, *, name, tc=256):
    nb = DFF // tc

    def body(u_ref, wg_ref, wv_ref, bg_ref, bv_ref, f_ref):
        wg = [wg_ref[k:k + 1, :] for k in range(3)]
        wv = [wv_ref[k:k + 1, :] for k in range(3)]
        bg, bv = bg_ref[...], bv_ref[...]

        def chunk(r0, first, last, carry):
            xg, xv = _ext_rows(u_ref.at[0], r0, first, last), _ext_rows(u_ref.at[1], r0, first, last)
            ug = bg + _roll_dn(xg) * wg[0] + xg * wg[1] + _roll_up(xg) * wg[2]
            uv = bv + _roll_dn(xv) * wv[0] + xv * wv[1] + _roll_up(xv) * wv[2]
            f_ref[_center_rows(r0, first, last), :] = (ug * jax.nn.sigmoid(ug) * uv)[_CTR].astype(BF)
            return carry

        _row_chunks(T, chunk, 0)

    return pl.pallas_call(
        body, name=name, grid=(nb,),
        in_specs=[pl.BlockSpec((2, T, tc), lambda j: (0, 0, j)),
                  pl.BlockSpec((3, tc), lambda j: (0, j)), pl.BlockSpec((3, tc), lambda j: (0, nb + j)),
                  pl.BlockSpec((1, tc), lambda j: (0, j)), pl.BlockSpec((1, tc), lambda j: (0, nb + j))],
        out_specs=pl.BlockSpec((T, tc), lambda j: (0, j)),
        out_shape=jax.ShapeDtypeStruct((T, DFF), BF),
        compiler_params=pltpu.CompilerParams(dimension_semantics=("parallel",)),
    )(u0, cw, cw, cb, cb)


def rows_call(lead, ins, in_specs, out_shape, out_specs, fn, *, name, R, tm):
    tb, a_stack, tk = lead.get("tb", False), lead.get("a_stack", False), lead["tk"]
    K = 2 * lead["a"].shape[2] if a_stack else lead["a"].shape[1]
    nk = K // tk
    deps = [] if lead.get("dep") is None else [lead["dep"]]
    n_in = len(ins)

    def body(a_ref, b_ref, *refs):
        refs = refs[len(deps):]
        in_refs, out_refs, acc = refs[:n_in], refs[n_in:-1], refs[-1]
        i, k = pl.program_id(0), pl.program_id(1)
        part = lax.dot_general(a_ref[...].astype(BF), b_ref[...].astype(BF),
                               (((1,), (1 if tb else 0,)), ((), ())), preferred_element_type=F32)
        if nk == 1:
            fn(i, part, in_refs, out_refs)
            return

        @pl.when(k == 0)
        def _():
            acc[...] = part

        @pl.when(k > 0)
        def _():
            acc[...] += part

        @pl.when(k == nk - 1)
        def _():
            fn(i, acc[...], in_refs, out_refs)

    if a_stack:
        nhb = K // 2 // tk
        a_spec = pl.BlockSpec((None, tm, tk), lambda i, k: (k // nhb, i, k % nhb))
    else:
        a_spec = pl.BlockSpec((tm, tk), lambda i, k: (i, k))
    b_spec = pl.BlockSpec((D, tk), lambda i, k: (0, k)) if tb else pl.BlockSpec((tk, D), lambda i, k: (k, 0))
    return pl.pallas_call(
        body, name=name, grid=(R // tm, nk),
        in_specs=[a_spec, b_spec] + [pl.BlockSpec(memory_space=pl.ANY)] * len(deps) + list(in_specs),
        out_specs=out_specs, out_shape=out_shape,
        scratch_shapes=[pltpu.VMEM((tm, D) if nk > 1 else (8, 128), F32)],
        compiler_params=pltpu.CompilerParams(dimension_semantics=("arbitrary", "arbitrary")),
    )(lead["a"], lead["b"], *deps, *ins)


def _rblk(tm, w=D, col=0):
    return pl.BlockSpec((tm, w), lambda i, k: (i, col))


def _rrow(w=D):
    return pl.BlockSpec((1, w), lambda i, k: (0, 0))


def down_final(f, wdn, x1, g2, fg, tgt, *, name, tm=512):
    def fn(i, d, in_refs, out_refs):
        x1_ref, g2_ref, fg_ref, t_ref = in_refs
        d_ref, dx_ref, dd_ref, dfg_ref, loss_ref = out_refs
        d_ref[...] = d
        xv = x1_ref[...] + g2_ref[...] * d
        r = lax.rsqrt(jnp.mean(xv * xv, axis=-1, keepdims=True) + EPS)
        xh = xv * r
        diff = xh * fg_ref[...] - t_ref[...]
        part = 0.5 * jnp.sum(jnp.mean(diff * diff, axis=-1, keepdims=True), axis=0, keepdims=True)
        dy = diff * (1.0 / D)
        a = dy * fg_ref[...]
        dx = r * (a - xh * jnp.mean(a * xh, axis=-1, keepdims=True))
        dx_ref[...] = dx
        dd_ref[...] = (dx * g2_ref[...]).astype(BF)
        dfg = jnp.sum(dy * xh, axis=0, keepdims=True)

        @pl.when(i == 0)
        def _():
            dfg_ref[...] = dfg
            loss_ref[...] = jnp.broadcast_to(part, (1, 128))

        @pl.when(i > 0)
        def _():
            dfg_ref[...] += dfg
            loss_ref[...] += jnp.broadcast_to(part, (1, 128))

    blk = _rblk(tm)
    return rows_call(
        dict(a=f, b=wdn, tk=DFF), [x1, g2, fg, tgt], [blk, _rrow(), _rrow(), blk],
        [jax.ShapeDtypeStruct((T, D), F32), jax.ShapeDtypeStruct((T, D), F32), jax.ShapeDtypeStruct((T, D), BF),
         jax.ShapeDtypeStruct((1, D), F32), jax.ShapeDtypeStruct((1, 128), F32)],
        [blk, blk, blk, _rrow(), _rrow(128)], fn, name=name, R=T, tm=tm)


def oproj_resid(merged, wo, x, gate, g, sc, sh, *, name, tm=512):
    def fn(i, a, in_refs, out_refs):
        x_ref, gate_ref, g_ref, sc_ref, sh_ref = in_refs
        a_ref, x1_ref, h_ref = out_refs
        a_ref[...] = a
        xv = x_ref[...] + gate_ref[...] * a
        x1_ref[...] = xv
        r = lax.rsqrt(jnp.mean(xv * xv, axis=-1, keepdims=True) + EPS)
        h_ref[...] = ((xv * r * g_ref[...]) * (1.0 + sc_ref[...]) + sh_ref[...]).astype(BF)

    blk = _rblk(tm)
    return rows_call(
        dict(a=merged, b=wo, tk=D), [x, gate, g, sc, sh], [blk, _rrow(), _rrow(), _rrow(), _rrow()],
        [jax.ShapeDtypeStruct((T, D), F32), jax.ShapeDtypeStruct((T, D), F32), jax.ShapeDtypeStruct((T, D), BF)],
        [blk, blk, blk], fn, name=name, R=T, tm=tm)


def oproj_dx_gate_bwd(da, wo, p, ya, yc, wao, wco, *, name, tm=512):
    kin = wao.shape[0]

    def fn(i, dm, in_refs, out_refs):
        ga_ref, gc_ref, ya_ref, yc_ref, wa_ref, wc_ref = in_refs
        dya_ref, dyc_ref, dp_ref, do_ref, dz_ref = out_refs
        sa, sc_ = jax.nn.sigmoid(ga_ref[...]), jax.nn.sigmoid(gc_ref[...])
        dya, dyc = (dm * sa).astype(BF), (dm * sc_).astype(BF)
        dya_ref[...] = dya
        dyc_ref[...] = dyc
        dp_ref[:, 0:D] = (dm * ya_ref[...] * (sa * (1.0 - sa))).astype(BF)
        dp_ref[:, D:2 * D] = (dm * yc_ref[...] * (sc_ * (1.0 - sc_))).astype(BF)
        nt = (((1,), (1,)), ((), ()))
        do_ref[...] = lax.dot_general(dya, wa_ref[...], nt, preferred_element_type=F32).astype(BF)
        dz_ref[...] = lax.dot_general(dyc, wc_ref[...], nt, preferred_element_type=F32)

    blk = _rblk(tm)
    sh = jax.ShapeDtypeStruct((T, D), BF)
    wsp = pl.BlockSpec((kin, D), lambda i, k: (0, 0))
    return rows_call(
        dict(a=da, b=wo, tb=True, tk=D), [p, p, ya, yc, wao, wco],
        [_rblk(tm, D, O_GA // D), _rblk(tm, D, O_GC // D), blk, blk, wsp, wsp],
        [sh, sh, jax.ShapeDtypeStruct((T, NIN), BF), jax.ShapeDtypeStruct((T, kin), BF), jax.ShapeDtypeStruct((T, kin), F32)],
        [blk, blk, _rblk(tm, 2 * D), _rblk(tm, kin), _rblk(tm, kin)], fn, name=name, R=T, tm=tm)


def normmod_bwd(x, dh, g, sc, dres, gsrc, gate, *, name, tm=512):
    R = x.shape[0]
    tm = min(tm, R)
    has_res = dres is not None
    fused = isinstance(dh, dict)
    if fused:
        tb, a_stack, tk = dh.get("tb", False), dh.get("a_stack", False), dh["tk"]
        K = 2 * dh["a"].shape[2] if a_stack else dh["a"].shape[1]
        nk = K // tk
        deps = [] if dh.get("dep") is None else [dh["dep"]]
        n_dh = 2 + len(deps)
    else:
        nk, n_dh = 1, 1

    def elementwise(i, dhv, x_ref, g_ref, sc_ref, res_refs, out_refs):
        xv = x_ref[...]
        r = lax.rsqrt(jnp.mean(xv * xv, axis=-1, keepdims=True) + EPS)
        xh = xv * r
        n = xh * g_ref[...]
        dn = dhv * (1.0 + sc_ref[...])
        a = dn * g_ref[...]
        rows = [jnp.sum(dhv, axis=0, keepdims=True), jnp.sum(dhv * n, axis=0, keepdims=True),
                jnp.sum(dn * xh, axis=0, keepdims=True)]
        if has_res:
            dres_ref, gsrc_ref, gate_ref = res_refs
            dx_ref, dxg_ref, st_ref = out_refs
            dr = dres_ref[...]
            dx = dr + r * (a - xh * jnp.mean(a * xh, axis=-1, keepdims=True))
            dx_ref[...] = dx
            dxg_ref[...] = (dx * gate_ref[...]).astype(BF)
            rows.append(jnp.sum(dr * gsrc_ref[...], axis=0, keepdims=True))
        else:
            st_ref, = out_refs
            rows.append(jnp.zeros((1, D), F32))

        @pl.when(i == 0)
        def _():
            for k, row in enumerate(rows):
                st_ref[k:k + 1, :] = row

        @pl.when(i > 0)
        def _():
            for k, row in enumerate(rows):
                st_ref[k:k + 1, :] += row

    def body(*refs):
        x_ref, dh_refs, g_ref, sc_ref = refs[0], refs[1:1 + n_dh], refs[1 + n_dh], refs[2 + n_dh]
        rest = refs[3 + n_dh:]
        res_refs, rest = (rest[:3], rest[3:]) if has_res else ((), rest)
        out_refs = rest[:3] if has_res else rest[:1]
        i = pl.program_id(0)
        if not fused:
            elementwise(i, dh_refs[0][...], x_ref, g_ref, sc_ref, res_refs, out_refs)
            return
        acc = rest[-1]
        k = pl.program_id(1)
        part = lax.dot_general(dh_refs[0][...].astype(BF), dh_refs[1][...].astype(BF),
                               (((1,), (1 if tb else 0,)), ((), ())), preferred_element_type=F32)
        if nk == 1:
            elementwise(i, part, x_ref, g_ref, sc_ref, res_refs, out_refs)
            return

        @pl.when(k == 0)
        def _():
            acc[...] = part

        @pl.when(k > 0)
        def _():
            acc[...] += part

        @pl.when(k == nk - 1)
        def _():
            elementwise(i, acc[...], x_ref, g_ref, sc_ref, res_refs, out_refs)

    rowb = lambda w: pl.BlockSpec((1, w), lambda i, *k: (0, 0))
    blk = pl.BlockSpec((tm, D), lambda i, *k: (i, 0))
    st_spec = pl.BlockSpec((4, D), lambda i, *k: (0, 0))
    st_shape = jax.ShapeDtypeStruct((4, D), F32)
    if fused:
        if a_stack:
            nhb = K // 2 // tk
            a_spec = pl.BlockSpec((None, tm, tk), lambda i, k: (k // nhb, i, k % nhb))
        else:
            a_spec = pl.BlockSpec((tm, tk), lambda i, k: (i, k))
        b_spec = pl.BlockSpec((D, tk), lambda i, k: (0, k)) if tb else pl.BlockSpec((tk, D), lambda i, k: (k, 0))
        dh_specs = [a_spec, b_spec] + [pl.BlockSpec(memory_space=pl.ANY)] * len(deps)
        dh_args = [dh["a"], dh["b"]] + deps
        grid, sem = (R // tm, nk), ("arbitrary", "arbitrary")
        scratch = [pltpu.VMEM((tm, D) if nk > 1 else (8, 128), F32)]
    else:
        dh_specs, dh_args, grid, sem, scratch = [blk], [dh], (R // tm,), ("arbitrary",), []
    cp = pltpu.CompilerParams(dimension_semantics=sem)
    if has_res:
        return pl.pallas_call(
            body, name=name, grid=grid, in_specs=[blk] + dh_specs + [rowb(D), rowb(D), blk, blk, rowb(D)],
            out_specs=[blk, blk, st_spec], scratch_shapes=scratch,
            out_shape=[jax.ShapeDtypeStruct((R, D), F32), jax.ShapeDtypeStruct((R, D), BF), st_shape],
            compiler_params=cp,
        )(x, *dh_args, g, sc, dres, gsrc, gate)
    return pl.pallas_call(
        body, name=name, grid=grid, in_specs=[blk] + dh_specs + [rowb(D), rowb(D)],
        out_specs=st_spec, out_shape=st_shape, scratch_shapes=scratch, compiler_params=cp,
    )(x, *dh_args, g, sc)


def ffn_act_bwd(u0, df, cw, cb, *, name, tc=128):
    nb = DFF // tc

    def body(u_ref, df_ref, wg_ref, wv_ref, bg_ref, bv_ref, du_ref, dw_ref, db_ref):
        wg = [wg_ref[k:k + 1, :] for k in range(3)]
        wv = [wv_ref[k:k + 1, :] for k in range(3)]
        bg, bv = bg_ref[...], bv_ref[...]

        def chunk(r0, first, last, acc):
            xg, xv = _ext_rows(u_ref.at[0], r0, first, last), _ext_rows(u_ref.at[1], r0, first, last)
            dfe = _ext_rows(df_ref, r0, first, last)
            xg_d, xg_u, xv_d, xv_u = _roll_dn(xg), _roll_up(xg), _roll_dn(xv), _roll_up(xv)
            ug = bg + xg_d * wg[0] + xg * wg[1] + xg_u * wg[2]
            uv = bv + xv_d * wv[0] + xv * wv[1] + xv_u * wv[2]
            sig = jax.nn.sigmoid(ug)
            dug = dfe * uv * (sig * (1.0 + ug * (1.0 - sig)))
            duv = dfe * (ug * sig)
            rows = _center_rows(r0, first, last)
            du_ref[0, rows, :] = (_roll_up(dug) * wg[0] + dug * wg[1] + _roll_dn(dug) * wg[2])[_CTR].astype(BF)
            du_ref[1, rows, :] = (_roll_up(duv) * wv[0] + duv * wv[1] + _roll_dn(duv) * wv[2])[_CTR].astype(BF)
            terms = [dug * xg_d, dug * xg, dug * xg_u, dug, duv * xv_d, duv * xv, duv * xv_u, duv]
            return tuple(a + jnp.sum(t[_CTR], axis=0, keepdims=True) for a, t in zip(acc, terms))

        acc = _row_chunks(T, chunk, tuple(jnp.zeros((1, tc), F32) for _ in range(8)))
        for k in range(3):
            dw_ref[0, k:k + 1, :] = acc[k]
            dw_ref[1, k:k + 1, :] = acc[4 + k]
        db_ref[0] = acc[3]
        db_ref[1] = acc[7]

    lo = lambda r: pl.BlockSpec((r, tc), lambda j: (0, j))
    hi = lambda r: pl.BlockSpec((r, tc), lambda j: (0, nb + j))
    st = lambda r: pl.BlockSpec((2, r, tc), lambda j: (0, 0, j))
    return pl.pallas_call(
        body, name=name, grid=(nb,),
        in_specs=[st(T), lo(T), lo(3), hi(3), lo(1), hi(1)],
        out_specs=[st(T), st(3), st(1)],
        out_shape=[jax.ShapeDtypeStruct((2, T, DFF), BF), jax.ShapeDtypeStruct((2, 3, DFF), F32),
                   jax.ShapeDtypeStruct((2, 1, DFF), F32)],
        compiler_params=pltpu.CompilerParams(dimension_semantics=("parallel",)),
    )(u0, df, cw, cw, cb, cb)


def convz_bwd(p, dz, cw, cb, dp, *, name):
    o0 = O_CV // (3 * CVB)

    def body(p_ref, dz_ref, w_ref, bias_ref, dp_in, dp_ref, dw_ref, dbias_ref):
        xv, bv, cv = p_ref[:, 0:CVB], p_ref[:, CVB:2 * CVB], p_ref[:, 2 * CVB:3 * CVB]
        ci = cv * xv
        dwc = _conv(ci, w_ref, bias_ref)
        dzv = dz_ref[...]
        ddw = dzv * bv
        dci = _conv_t(ddw, w_ref)
        dp_ref[:, 0:CVB] = (dci * cv).astype(BF)
        dp_ref[:, CVB:2 * CVB] = (dzv * dwc).astype(BF)
        dp_ref[:, 2 * CVB:3 * CVB] = (dci * xv).astype(BF)
        _conv_wgrad(dw_ref, ddw, ci)
        dbias_ref[...] = jnp.sum(ddw, axis=0, keepdims=True)

    own = lambda r: pl.BlockSpec((r, CVB), lambda j: (0, j))
    return pl.pallas_call(
        body, name=name, grid=(CONV // CVB,),
        in_specs=[pl.BlockSpec((T, 3 * CVB), lambda j: (0, o0 + j)), own(T), own(3), own(1),
                  pl.BlockSpec(memory_space=pl.ANY)],
        out_specs=[pl.BlockSpec((T, 3 * CVB), lambda j: (0, o0 + j)), own(3), own(1)],
        out_shape=[jax.ShapeDtypeStruct((T, NIN), BF), jax.ShapeDtypeStruct((3, CONV), F32),
                   jax.ShapeDtypeStruct((1, CONV), F32)],
        input_output_aliases={4: 0},
        compiler_params=pltpu.CompilerParams(dimension_semantics=("parallel",)),
    )(p, dz, cw, cb, dp)


def attn_bwd(q, k, v, do, o, lse, dep, *, name, tq=1024, kc=768):
    NKC, KC = TKV // kc, kc
    deps = [] if dep is None else [dep]

    def body(q_ref, k_ref, v_ref, do_ref, o_ref, lse_ref, *rest):
        dq_ref, dk_ref, dv_ref = rest[len(deps):]
        h, i = pl.program_id(0), pl.program_id(1)

        @pl.when(i == 0)
        def _():
            dk_ref[...] = jnp.zeros_like(dk_ref)

        @pl.when((i == 0) & (h % 2 == 0))
        def _():
            dv_ref[...] = jnp.zeros_like(dv_ref)

        qv = q_ref[...]
        dom = jnp.where(_head_mask(h), do_ref[...], jnp.zeros_like(do_ref[...]))
        delta = jnp.sum(dom.astype(F32) * o_ref[...].astype(F32), axis=-1, keepdims=True)
        lse = lse_ref[:, 0:1]
        dq = jnp.zeros((tq, HP), F32)
        for c in range(NKC):
            cols = slice(c * KC, (c + 1) * KC)
            s = lax.dot_general(qv, k_ref[cols, :], (((1,), (1,)), ((), ())),
                                preferred_element_type=F32) * (SCALE * LOG2E)
            pr = jnp.exp2(s - lse)
            dp = lax.dot_general(dom, v_ref[cols, :], (((1,), (1,)), ((), ())), preferred_element_type=F32)
            ds = (pr * (dp - delta) * SCALE).astype(BF)
            dq = dq + jnp.dot(ds, k_ref[cols, :], preferred_element_type=F32)
            dk_ref[cols, :] += lax.dot_general(ds, qv, (((0,), (0,)), ((), ())), preferred_element_type=F32)
            dv_ref[cols, :] += lax.dot_general(pr.astype(BF), dom, (((0,), (0,)), ((), ())), preferred_element_type=F32)
        dq_ref[...] = dq

    return pl.pallas_call(
        body, name=name, grid=(NH, T // tq),
        in_specs=[pl.BlockSpec((tq, HP), lambda h, i: (i, h)), pl.BlockSpec((TKV, HP), lambda h, i: (0, h)),
                  pl.BlockSpec((TKV, 2 * DV), lambda h, i: (0, h // 2)), pl.BlockSpec((tq, 2 * DV), lambda h, i: (i, h // 2)),
                  pl.BlockSpec((tq, 2 * DV), lambda h, i: (i, h // 2)), pl.BlockSpec((tq, HP), lambda h, i: (i, h)),
                  *([pl.BlockSpec(memory_space=pl.ANY)] * len(deps))],
        out_specs=[pl.BlockSpec((tq, HP), lambda h, i: (i, h)), pl.BlockSpec((TKV, HP), lambda h, i: (0, h)),
                   pl.BlockSpec((TKV, 2 * DV), lambda h, i: (0, h // 2))],
        out_shape=[jax.ShapeDtypeStruct((T, NH * HP), F32), jax.ShapeDtypeStruct((TKV, NH * HP), F32),
                   jax.ShapeDtypeStruct((TKV, NH * DV), F32)],
        compiler_params=pltpu.CompilerParams(dimension_semantics=("arbitrary", "arbitrary")),
    )(q, k, v, do, o, lse, *deps)


def qprep_bwd(p, dq, qg, wq2, cq_t, sq_t, dp, *, name, tm=256):
    qcol = O_Q // 512

    def body(p_ref, dq_ref, g_ref, w_ref, c_ref, s_ref, dp_in, dp_ref, dq2_ref, dg_ref):
        i = pl.program_id(0)
        dqv = dq_ref[...]
        cc = jnp.concatenate([c_ref[...]] * NH, axis=1)
        ss = jnp.concatenate([s_ref[...]] * NH, axis=1)
        dq2 = jnp.concatenate([dqv * cc, dqv * ss], axis=1).astype(BF)
        dq2_ref[...] = dq2
        dcq = lax.dot_general(dq2, w_ref[...], (((1,), (1,)), ((), ())), preferred_element_type=F32)
        pq = p_ref[...]
        r = lax.rsqrt(jnp.sum(pq * pq, axis=-1, keepdims=True) * (1.0 / QL) + EPS)
        xh = pq * r
        a = dcq * g_ref[...]
        dp_ref[...] = (r * (a - xh * (jnp.sum(a * xh, axis=-1, keepdims=True) * (1.0 / QL)))).astype(BF)
        dg = jnp.sum(dcq * xh, axis=0, keepdims=True)

        @pl.when(i == 0)
        def _():
            dg_ref[...] = dg

        @pl.when(i > 0)
        def _():
            dg_ref[...] += dg

    return pl.pallas_call(
        body, name=name, grid=(T // tm,),
        in_specs=[pl.BlockSpec((tm, 512), lambda i: (i, qcol)), pl.BlockSpec((tm, NH * HP), lambda i: (i, 0)), _row(512),
                  pl.BlockSpec((512, 2 * NH * HP), lambda i: (0, 0)),
                  pl.BlockSpec((tm, HP), lambda i: (i, 0)), pl.BlockSpec((tm, HP), lambda i: (i, 0)),
                  pl.BlockSpec(memory_space=pl.ANY)],
        out_specs=[pl.BlockSpec((tm, 512), lambda i: (i, qcol)), pl.BlockSpec((tm, 2 * NH * HP), lambda i: (i, 0)), _row(512)],
        out_shape=[jax.ShapeDtypeStruct((T, NIN), BF), jax.ShapeDtypeStruct((T, 2 * NH * HP), BF),
                   jax.ShapeDtypeStruct((1, 512), F32)],
        input_output_aliases={6: 0},
        compiler_params=pltpu.CompilerParams(dimension_semantics=("arbitrary",)),
    )(p, dq, qg, wq2, cq_t, sq_t, dp)


def kvprep_bwd(pc, p, dk, dv, kvg, wkv2, ck, sk, dp, *, name, tm=256):
    assert tm == TC
    nb = TKV // tm
    kvcol = O_KV // 512

    def body(pc_ref, p_ref, dk_ref, dv_ref, g_ref, w_ref, ck_ref, sk_ref, dp_in, dp_ref, dpc_ref, dkv2_ref, dg_ref):
        i = pl.program_id(0)
        t = jnp.where(i == NLAT, pc_ref[...], p_ref[...])
        pk = t[:, :KVL]
        r = lax.rsqrt(jnp.mean(pk * pk, axis=-1, keepdims=True) + EPS)
        xh = pk * r
        dkv = dk_ref[...]
        dkv2 = jnp.concatenate([dkv, dv_ref[...]], axis=1).astype(BF)
        dkv2_ref[...] = dkv2
        dckv = lax.dot_general(dkv2, w_ref[...], (((1,), (1,)), ((), ())), preferred_element_type=F32)
        a = dckv * g_ref[...]
        dpk = r * (a - xh * jnp.mean(a * xh, axis=-1, keepdims=True))
        dkr = dkv[:, 0:HP]
        for hh in range(1, NH):
            dkr = dkr + dkv[:, hh * HP:(hh + 1) * HP]
        res = jnp.concatenate([dpk, dkr * ck_ref[...], dkr * sk_ref[...]], axis=1).astype(BF)
        dg = jnp.sum(dckv * xh, axis=0, keepdims=True)

        @pl.when(i == 0)
        def _():
            dg_ref[...] = dg

        @pl.when(i > 0)
        def _():
            dg_ref[...] += dg

        @pl.when(i < NLAT)
        def _():
            dp_ref[...] = res

        @pl.when(i == NLAT)
        def _():
            dpc_ref[...] = res

    rb = lambda w: pl.BlockSpec((tm, w), lambda i: (i, 0))
    return pl.pallas_call(
        body, name=name, grid=(nb,),
        in_specs=[pl.BlockSpec((tm, 512), lambda i: (0, 0)),
                  pl.BlockSpec((tm, 512), lambda i: (jnp.minimum(i, NLAT - 1), kvcol)),
                  rb(NH * HP), rb(NH * DV), _row(KVL), pl.BlockSpec((KVL, NH * HP + NH * DV), lambda i: (0, 0)),
                  rb(HP), rb(HP), pl.BlockSpec(memory_space=pl.ANY)],
        out_specs=[pl.BlockSpec((tm, 512), lambda i: (jnp.minimum(i, NLAT - 1), kvcol)),
                   pl.BlockSpec((tm, 512), lambda i: (0, 0)), rb(NH * HP + NH * DV), _row(KVL)],
        out_shape=[jax.ShapeDtypeStruct((T, NIN), BF), jax.ShapeDtypeStruct((TC, 512), BF),
                   jax.ShapeDtypeStruct((TKV, NH * HP + NH * DV), BF), jax.ShapeDtypeStruct((1, KVL), F32)],
        input_output_aliases={8: 0},
        compiler_params=pltpu.CompilerParams(dimension_semantics=("arbitrary",)),
    )(pc, p, dk, dv, kvg, wkv2, ck, sk, dp)


def _pieces(src, width, n):
    out, c = [], src
    while c < src + width:
        k = c // n
        w = min(src + width, (k + 1) * n) - c
        out.append((k, c - k * n, c - src, w))
        c += w
    return out


def _win_moves():
    mv = [(2208, 1024, O_GA), (3232, 1024, O_GC), (0, KVL, O_KV), (256, DR, O_KV + KVL + DN), (288, QL, O_Q)]
    mv += [(256 + _swap_start(g), 8, O_KV + KVL + HP + DN + 8 * g) for g in range(4)]
    for j in range(CONV // CVB):
        base = O_CV + 3 * CVB * j
        mv += [(672 + CVB * j, CVB, base), (1184 + CVB * j, CVB, base + CVB), (1696 + CVB * j, CVB, base + 2 * CVB)]
    return mv


_WIN_ZERO = [(O_KV + KVL, DN), (O_KV + KVL + DN + DR, HP - DN - DR), (O_KV + KVL + HP, DN),
             (O_KV + KVL + HP + DN + DR, HP - DN - DR), (O_Q + QL, 512 - QL)]


def build_win(g, *, name, tm=256):
    def body(g_ref, o_ref):
        for src, w, dst in _win_moves():
            for k, a, off, pw in _pieces(src, w, SH_IN):
                o_ref[:, dst + off:dst + off + pw] = g_ref[k, :, a:a + pw]
        for c0, w in _WIN_ZERO:
            o_ref[:, c0:c0 + w] = jnp.zeros((tm, w), o_ref.dtype)

    return pl.pallas_call(
        body, name=name, grid=(D // tm,), in_specs=[pl.BlockSpec((NDEV, tm, SH_IN), lambda i: (0, i, 0))],
        out_specs=pl.BlockSpec((tm, NIN), lambda i: (i, 0)), out_shape=jax.ShapeDtypeStruct((D, NIN), g.dtype),
        compiler_params=pltpu.CompilerParams(dimension_semantics=("parallel",)),
    )(g)


def shard_win_grad(dwt, dwct, *, name, tc=256):
    def body(dw_ref, dwc_ref, o_ref, kvs):
        kvs[...] = dw_ref[O_KV:O_KV + 512, :] + dwc_ref[...]

        def src(row, w):
            if O_KV <= row < O_KV + 512:
                return kvs[row - O_KV:row - O_KV + w, :]
            return dw_ref[row:row + w, :]

        for s, w, dst in _win_moves():
            if w == 8 or s == 256:
                continue
            for k, a, off, pw in _pieces(s, w, SH_IN):
                o_ref[k, a:a + pw, :] = src(dst + off, pw).astype(o_ref.dtype)
        for g in range(4):
            val = src(O_KV + KVL + DN + 8 * g, 8) + src(O_KV + KVL + HP + DN + _swap_start(g), 8)
            o_ref[0, 256 + 8 * g:256 + 8 * g + 8, :] = val.astype(o_ref.dtype)

    return pl.pallas_call(
        body, name=name, grid=(D // tc,),
        in_specs=[pl.BlockSpec((NIN, tc), lambda j: (0, j)), pl.BlockSpec((512, tc), lambda j: (0, j))],
        out_specs=pl.BlockSpec((NDEV, SH_IN, tc), lambda j: (0, 0, j)),
        out_shape=jax.ShapeDtypeStruct((NDEV, SH_IN, D), BF),
        scratch_shapes=[pltpu.VMEM((512, tc), F32)],
        compiler_params=pltpu.CompilerParams(dimension_semantics=("parallel",)),
    )(dwt, dwct)


def build_wq_wkv(gq, gkv, *, name):
    def body(gq_ref, gkv_ref, q_ref, kv_ref):
        q_ref[...] = jnp.zeros_like(q_ref)
        kv_ref[...] = jnp.zeros_like(kv_ref)
        for h in range(NH):
            q_ref[0:QL, h * HP:h * HP + DN + DR] = gq_ref[h]
            for g in range(4):
                c0 = NH * HP + h * HP + DN + 8 * g
                q_ref[0:QL, c0:c0 + 8] = gq_ref[h, :, DN + _swap_start(g):DN + _swap_start(g) + 8]
            kv_ref[:, h * HP:h * HP + DN] = gkv_ref[h, :, 0:DN]
            kv_ref[:, NH * HP + h * DV:NH * HP + (h + 1) * DV] = gkv_ref[h, :, DN:DN + DV]

    vm = pl.BlockSpec(memory_space=pltpu.VMEM)
    return pl.pallas_call(
        body, name=name, in_specs=[vm, vm], out_specs=[vm, vm],
        out_shape=[jax.ShapeDtypeStruct((512, 2 * NH * HP), gq.dtype), jax.ShapeDtypeStruct((KVL, NH * HP + NH * DV), gq.dtype)],
    )(gq, gkv)


def shard_wq_wkv_grad(dwq2, dwkv2, *, name):
    def body(q_ref, kv_ref, gq_ref, gkv_ref):
        for h in range(NH):
            gq_ref[h, :, 0:DN] = q_ref[0:QL, h * HP:h * HP + DN].astype(BF)
            for g in range(4):
                a = q_ref[0:QL, h * HP + DN + 8 * g:h * HP + DN + 8 * g + 8]
                c0 = NH * HP + h * HP + DN + _swap_start(g)
                gq_ref[h, :, DN + 8 * g:DN + 8 * g + 8] = (a + q_ref[0:QL, c0:c0 + 8]).astype(BF)
            gkv_ref[h, :, 0:DN] = kv_ref[:, h * HP:h * HP + DN].astype(BF)
            gkv_ref[h, :, DN:DN + DV] = kv_ref[:, NH * HP + h * DV:NH * HP + (h + 1) * DV].astype(BF)

    vm = pl.BlockSpec(memory_space=pltpu.VMEM)
    return pl.pallas_call(
        body, name=name, in_specs=[vm, vm], out_specs=[vm, vm],
        out_shape=[jax.ShapeDtypeStruct((NDEV, QL, (DN + DR)), BF), jax.ShapeDtypeStruct((NDEV, KVL, DN + DV), BF)],
    )(dwq2, dwkv2)


def unshard_cols(g, *, name, tm=256):
    _, K, n = g.shape
    tm = _pick(K, tm, 16)

    def body(g_ref, o_ref):
        for k in range(NDEV):
            o_ref[:, k * n:(k + 1) * n] = g_ref[k]

    return pl.pallas_call(
        body, name=name, grid=(K // tm,), in_specs=[pl.BlockSpec((NDEV, tm, n), lambda i: (0, i, 0))],
        out_specs=pl.BlockSpec((tm, NDEV * n), lambda i: (i, 0)), out_shape=jax.ShapeDtypeStruct((K, NDEV * n), g.dtype),
        compiler_params=pltpu.CompilerParams(dimension_semantics=("parallel",)),
    )(g)


def shard_cols(w, *, name, tm=256):
    K, n8 = w.shape
    n = n8 // NDEV
    tm = _pick(K, tm, 16)

    def body(w_ref, o_ref):
        for k in range(NDEV):
            o_ref[k] = w_ref[:, k * n:(k + 1) * n]

    return pl.pallas_call(
        body, name=name, grid=(K // tm,), in_specs=[pl.BlockSpec((tm, n8), lambda i: (i, 0))],
        out_specs=pl.BlockSpec((NDEV, tm, n), lambda i: (0, i, 0)), out_shape=jax.ShapeDtypeStruct((NDEV, K, n), w.dtype),
        compiler_params=pltpu.CompilerParams(dimension_semantics=("parallel",)),
    )(w)


def _rope_tables():
    t = np.arange(T)
    row = (t // GRID_W).astype(np.float32)
    col = (t % GRID_W).astype(np.float32)
    axis_dim = DR // 2
    inv = (np.float32(ROPE_THETA) ** (-np.arange(0, axis_dim, 2, dtype=np.float32) / np.float32(axis_dim))).astype(np.float32)
    ar, ac = (row[:, None] * inv).astype(np.float32), (col[:, None] * inv).astype(np.float32)
    cosv = np.concatenate([np.cos(ar), np.cos(ar), np.cos(ac), np.cos(ac)], axis=1).astype(np.float32)
    sinv = np.concatenate([-np.sin(ar), np.sin(ar), -np.sin(ac), np.sin(ac)], axis=1).astype(np.float32)
    ck = np.zeros((TKV, HP), np.float32)
    sk = np.zeros((TKV, HP), np.float32)
    ck[T:, DN:DN + DR] = 1.0
    ck[:T, DN:DN + DR] = cosv
    sk[:T, DN:DN + DR] = sinv
    cq = np.zeros((T, HP), np.float32)
    cq[:, :DN] = 1.0
    cq[:, DN:DN + DR] = cosv
    return jnp.asarray(ck), jnp.asarray(sk), jnp.asarray(cq), jnp.asarray(sk[:T])


def _local_step(x, ctx, tgt, mod_lat, mod_ctx, n1g, qg, kvg, n2g, fg, conv_w, conv_b, ffn_w, ffn_b, get_w, put_g, dep0):
    sh1, sc1, g1, sh2, sc2, g2 = [mod_lat[:, i * D:(i + 1) * D] for i in range(6)]
    csh1, csc1 = mod_ctx[:, 0:D], mod_ctx[:, D:2 * D]
    ck, sk, cq_t, sq_t = _rope_tables()
    qg_p = jnp.pad(qg, ((0, 0), (0, 512 - QL)))

    hcat = normmod_cat(ctx, x, n1g, csc1, csh1, sc1, sh1, dep0, name="normmod1")
    win = get_w("in", hcat)
    p = mm(hcat, win, M=T, tn=768, name="in_proj")
    pc = mm(hcat, win, M=TC, N=512, a_off=(T, 0), b_off=(0, O_KV), name="in_proj_ctx")
    wq2, wkv2, wao, wco, wo = get_w("mid", p)
    kh, vh, ckv = kvprep(pc, p, kvg, wkv2, ck, sk, name="kvprep")
    qr, cq = qprep(p, qg_p, wq2, cq_t, sq_t, name="qprep")
    o, lse = attn_fwd(qr, kh, vh, name="attn_fwd")
    z = convz(p, conv_w, conv_b, name="convz")
    ya, yc, merged = out_proj_merge(o, wao, z, wco, p, name="attn_conv_out_gate_merge")
    a_out, x1, h2 = oproj_resid(merged, wo, x, g1, n2g, sc2, sh2, name="o_proj_resid_normmod2")
    wup = get_w("up", h2)
    u0 = mm(h2, wup, tb=True, o_stack=True, tn=1408, name="up_proj")
    f = ffn_act(u0, ffn_w, ffn_b, name="ffn_act")
    wdn = get_w("down", f)
    dn, dx2, dd, dfg, loss = down_final(f, wdn, x1, g2, fg, tgt, name="down_proj_final_loss")

    df = mm(dd, wdn, tb=True, tn=1408, name="down_proj_dx")
    dwdn = mm(f, dd, ta=True, out_dtype=BF, tm=1408, name="down_proj_dw")
    du0, dffn_w, dffn_b = ffn_act_bwd(u0, df, ffn_w, ffn_b, name="ffn_act_bwd")
    dwup = mm(du0, h2, ta=True, a_stack=True, out_dtype=BF, tm=1408, name="up_proj_dw")
    tok = put_g("ffn", dict(dwup=dwup, dwdn=dwdn))
    dx1, da, st2 = normmod_bwd(x1, dict(a=du0, b=wup, a_stack=True, tk=DFF, dep=tok), n2g, sc2, dx2, dn, g1,
                               name="up_proj_dx_normmod2_bwd")

    dwo = mm(merged, da, ta=True, out_dtype=BF, tn=512, name="o_proj_dw")
    dya, dyc, dp, do, dz = oproj_dx_gate_bwd(da, wo, p, ya, yc, wao, wco, name="o_proj_dx_gate_merge_bwd")
    dwao = mm(o, dya, ta=True, out_dtype=BF, tn=512, name="attn_out_dw")
    dwco = mm(z, dyc, ta=True, out_dtype=BF, tn=512, name="conv_out_dw")
    tok = put_g("mid", dict(dwao=dwao, dwco=dwco, dwo=dwo))
    dp, dconv_w, dconv_b = convz_bwd(p, dz, conv_w, conv_b, dp, name="convz_bwd")
    dq, dk, dv = attn_bwd(qr, kh, vh, do, o, lse, tok, name="attn_bwd")
    dp, dq2, dqg = qprep_bwd(p, dq, qg_p, wq2, cq_t, sq_t, dp, name="qprep_bwd")
    dp, dpc, dkv2, dkvg = kvprep_bwd(pc, p, dk, dv, kvg, wkv2, ck, sk, dp, name="kvprep_bwd")

    dwin = mm(dp, hcat, ta=True, K=T, tm=768, name="in_proj_dw")
    dwin_c = mm(dpc, hcat, ta=True, K=TC, b_off=(T, 0), name="in_proj_ctx_dw")
    tok = put_g("in", dict(dwin=dwin, dwin_c=dwin_c))
    dwq2 = mm(cq, dq2, ta=True, dep=tok, name="q_up_dw")
    dwkv2 = mm(ckv, dkv2, ta=True, name="kv_up_dw")
    tok = put_g("qkv", dict(dwq2=dwq2, dwkv2=dwkv2))
    dhc = mm(dpc, win, tb=True, N=D, K=512, b_off=(0, O_KV), name="in_proj_ctx_dx")
    dx, _, st1 = normmod_bwd(x, dict(a=dp, b=win, tb=True, tk=NIN, dep=tok), n1g, sc1, dx1, a_out, g1,
                             name="in_proj_dx_normmod1_bwd")
    stc = normmod_bwd(ctx, dhc, n1g, csc1, None, None, None, name="normmod1_ctx_bwd")

    zrow = jnp.zeros((1, D), F32)
    dmod_lat = jnp.concatenate([st1[0:1], st1[1:2], st1[3:4], st2[0:1], st2[1:2], st2[3:4]], axis=1)
    dmod_ctx = jnp.concatenate([stc[0:1], stc[1:2], zrow, zrow, zrow, zrow], axis=1)
    return dict(
        loss=loss, dx=dx, dmod_lat=dmod_lat, dmod_ctx=dmod_ctx,
        dn1g=st1[2:3] + stc[2:3], dqg=dqg, dkvg=dkvg, dn2g=st2[2:3], dfg=dfg,
        dconv_w=dconv_w, dconv_b=dconv_b, dffn_w=dffn_w, dffn_b=dffn_b)


def _me():
    x, y, c = lax.axis_index("x"), lax.axis_index("y"), lax.axis_index("c")
    return x, y, c, 4 * x + 2 * y + c


def _peer(x, y, c, k):
    px = 1 - x if k & 4 else x
    py = 1 - y if k & 2 else y
    pc = 1 - c if k & 1 else c
    return (px, py, pc), 4 * px + 2 * py + pc


def _exchange_tiles(src_of_peer, buf, send_sem, recv_sem):
    x, y, c, me = _me()
    for k in range(1, NDEV):
        dev, lin = _peer(x, y, c, k)
        pltpu.make_async_remote_copy(src_ref=src_of_peer(lin), dst_ref=buf.at[me], send_sem=send_sem, recv_sem=recv_sem,
                                     device_id=dev, device_id_type=MESH).start()
    seven = buf.at[pl.ds(0, NDEV - 1)]
    pltpu.make_async_remote_copy(src_ref=seven, dst_ref=seven, send_sem=send_sem, recv_sem=recv_sem,
                                 device_id=(x, y, c), device_id_type=MESH).wait()


def _silu(z):
    return z * jax.nn.sigmoid(z)


def ada_fwd(c, c_ctx, ffn_w, conv_w, w_shard, b_shard, deps, *, name):
    nsh = w_shard.shape[1]
    deps = [d for d in deps if d is not None]

    def body(c_ref, cc_ref, fw_ref, cw_ref, w_ref, b_ref, *rest):
        s_ref, m_ref, mine, res, sems = rest[len(deps):]
        x, y, c, me = _me()
        mine[0:1, :] = _silu(c_ref[...])
        mine[1:2, :] = _silu(cc_ref[...])
        mine[2:5, :] = fw_ref[...]
        mine[5:8, :] = cw_ref[...]
        s_ref[me] = mine[...]
        _exchange_tiles(lambda lin: mine, s_ref, sems.at[0], sems.at[1])
        sall = s_ref[...].reshape(NDEV * 8, D).astype(BF)
        r = jnp.dot(sall, w_ref[...].astype(BF), preferred_element_type=F32) + b_ref[...]
        res[...] = r.reshape(NDEV, 8, nsh)
        m_ref[me] = res[me]
        _exchange_tiles(lambda lin: res.at[lin], m_ref, sems.at[2], sems.at[3])

    vm = pl.BlockSpec(memory_space=pltpu.VMEM)
    return pl.pallas_call(
        body, name=name, in_specs=[vm] * 6 + [pl.BlockSpec(memory_space=pl.ANY)] * len(deps), out_specs=[vm, vm],
        out_shape=[jax.ShapeDtypeStruct((NDEV, 8, D), F32), jax.ShapeDtypeStruct((NDEV, 8, nsh), F32)],
        scratch_shapes=[pltpu.VMEM((8, D), F32), pltpu.VMEM((NDEV, 8, nsh), F32), pltpu.SemaphoreType.DMA((4,))],
    )(c, c_ctx, ffn_w, conv_w, w_shard, b_shard, *deps)


P_DML, P_DMC, P_N1, P_QG, P_KVG, P_CB, P_N2, P_FB, P_FG, P_CW, P_FW, P_LOSS, P_ROWS = 0, 6, 12, 13, 14, 15, 16, 17, 23, 24, 27, 45, 48
FROWS = 3


def sync_small(r, deps, *, name):
    ins = [r["dmod_lat"], r["dmod_ctx"], r["dn1g"], r["dqg"], r["dkvg"], r["dconv_b"], r["dn2g"], r["dffn_b"], r["dfg"],
           r["dconv_w"], r["dffn_w"], r["loss"]]

    def put_wide(p, row0, row, n):
        for j in range(-(-n // D)):
            w = min(D, n - j * D)
            p[row0 + j:row0 + j + 1, 0:w] = row[:, j * D:j * D + w]

    def body(dml, dmc, n1, qg, kvg, cb, n2, fb, fg, cw, fw, loss, *rest):
        a_ref, sum_ref, p, sems = rest[len(deps):]
        x, y, c, me = _me()
        p[...] = jnp.zeros_like(p)
        put_wide(p, P_DML, dml, 6 * D)
        put_wide(p, P_DMC, dmc, 6 * D)
        put_wide(p, P_N1, n1, D)
        put_wide(p, P_QG, qg, 512)
        put_wide(p, P_KVG, kvg, KVL)
        put_wide(p, P_CB, cb, CONV)
        put_wide(p, P_N2, n2, D)
        put_wide(p, P_FG, fg, D)
        put_wide(p, P_LOSS, loss, 128)
        for s in range(2):
            put_wide(p, P_FB + FROWS * s, fb.at[s], DFF)
        for k in range(3):
            put_wide(p, P_CW + k, cw.at[k:k + 1], CONV)
            for s in range(2):
                put_wide(p, P_FW + FROWS * (2 * k + s), fw.at[s, k:k + 1], DFF)
        a_ref[me] = p[...]
        _exchange_tiles(lambda lin: p, a_ref, sems.at[0], sems.at[1])
        acc = a_ref[0]
        for k in range(1, NDEV):
            acc = acc + a_ref[k]
        sum_ref[...] = acc

    vm = pl.BlockSpec(memory_space=pltpu.VMEM)
    return pl.pallas_call(
        body, name=name, in_specs=[vm] * len(ins) + [pl.BlockSpec(memory_space=pl.ANY)] * len(deps), out_specs=[vm, vm],
        out_shape=[jax.ShapeDtypeStruct((NDEV, P_ROWS, D), F32), jax.ShapeDtypeStruct((P_ROWS, D), F32)],
        scratch_shapes=[pltpu.VMEM((P_ROWS, D), F32), pltpu.SemaphoreType.DMA((2,))],
    )(*ins, *deps)


def ada_bwd(s_all, dml, dmc, w_shard, c_ctx, *, name):
    nsh = w_shard.shape[1]

    def body(s_ref, dml_ref, dmc_ref, w_ref, c_ref, dw_ref, gc_ref, s16, dm16, part, buf, sems):
        x, y, c, me = _me()
        s16[...] = jnp.zeros_like(s16)
        dm16[...] = jnp.zeros_like(dm16)
        for k in range(NDEV):
            s16[k:k + 1, :] = s_ref[k, 0:1, :]
        s16[8:9, :] = s_ref[0, 1:2, :]
        dm16[0:8, :] = dml_ref[...]
        dm16[8:9, :] = dmc_ref[...]
        dw_ref[...] = lax.dot_general(s16[...].astype(BF), dm16[...].astype(BF), (((0,), (0,)), ((), ())),
                                      preferred_element_type=F32)
        part[...] = lax.dot_general(dm16[8:16, :].astype(BF), w_ref[...].astype(BF), (((1,), (1,)), ((), ())),
                                    preferred_element_type=F32)
        buf[me] = part[...]
        _exchange_tiles(lambda lin: part, buf, sems.at[0], sems.at[1])
        acc = buf[0]
        for k in range(1, NDEV):
            acc = acc + buf[k]
        z = c_ref[...]
        sg = jax.nn.sigmoid(z)
        gc_ref[...] = acc * (sg * (1.0 + z * (1.0 - sg)))

    vm = pl.BlockSpec(memory_space=pltpu.VMEM)
    return pl.pallas_call(
        body, name=name, in_specs=[vm] * 5, out_specs=[vm, vm],
        out_shape=[jax.ShapeDtypeStruct((D, nsh), F32), jax.ShapeDtypeStruct((8, D), F32)],
        scratch_shapes=[pltpu.VMEM((16, D), F32), pltpu.VMEM((16, nsh), F32), pltpu.VMEM((8, D), F32),
                        pltpu.VMEM((NDEV, 8, D), F32), pltpu.SemaphoreType.DMA((2,))],
    )(s_all, dml, dmc, w_shard, c_ctx)


HBM_SPEC = pl.BlockSpec(memory_space=pltpu.HBM)
SEM_SPEC = pl.BlockSpec(memory_space=pltpu.SEMAPHORE)
EFFECT = pltpu.SideEffectType.DATAFLOW_SIDE_EFFECTING


ALL_PEERS = tuple(range(1, NDEV))
FIRST_HOP = (1, 2, 4, 6)
RELAY = (2, 4, 6)


def _exchange_copies(srcs, lands, send, recv, per_peer, peers):
    x, y, c, me = _me()
    n = len(peers)
    cps = []
    for t in range(len(srcs)):
        for j, k in enumerate(peers):
            dev, lin = _peer(x, y, c, k)
            cps.append(pltpu.make_async_remote_copy(
                src_ref=srcs[t].at[lin] if per_peer else srcs[t], dst_ref=lands[t].at[me],
                send_sem=send.at[n * t + j], recv_sem=recv.at[n * t + j], device_id=dev, device_id_type=MESH))
    return cps


def _relay_copies(lands, send, recv):
    x, y, c, me = _me()
    n = len(RELAY)
    cps = []
    for t in range(len(lands)):
        for j, k in enumerate(RELAY):
            slot = lands[t].at[_peer(x, y, c, k)[1]]
            cps.append(pltpu.make_async_remote_copy(
                src_ref=slot, dst_ref=slot, send_sem=send.at[n * t + j], recv_sem=recv.at[n * t + j],
                device_id=(x, y, 1 - c), device_id_type=MESH))
    return cps


def _own_copies(srcs, lands, own, per_peer):
    me = _me()[3]
    return [pltpu.make_async_copy(srcs[t].at[me] if per_peer else srcs[t], lands[t].at[me], own.at[t])
            for t in range(len(srcs))]


def exchange_start(srcs, *, per_peer, name, dep=None, peers=ALL_PEERS):
    nt = len(srcs)
    ns = len(peers) * nt
    land_shapes = [(a.shape if per_peer else (NDEV,) + a.shape) for a in srcs]
    deps = [] if dep is None else [dep]

    def body(*refs):
        src, land = refs[:nt], refs[nt:2 * nt]
        send, recv, own = refs[2 * nt + len(deps):2 * nt + len(deps) + 3]
        for cp in _exchange_copies(src, land, send, recv, per_peer, peers) + _own_copies(src, land, own, per_peer):
            cp.start()
        refs[-1][...] = jnp.zeros_like(refs[-1])

    hb = lambda a: pltpu.with_memory_space_constraint(a, pltpu.HBM)
    outs = pl.pallas_call(
        body, name=name,
        out_shape=(pltpu.SemaphoreType.DMA((ns,)), pltpu.SemaphoreType.DMA((ns,)), pltpu.SemaphoreType.DMA((nt,)),
                   *[pltpu.HBM(a.shape, a.dtype) for a in srcs], *[pltpu.HBM(s, a.dtype) for s, a in zip(land_shapes, srcs)],
                   jax.ShapeDtypeStruct((8, 128), F32)),
        in_specs=[HBM_SPEC] * (2 * nt) + [pl.BlockSpec(memory_space=pl.ANY)] * len(deps),
        out_specs=(SEM_SPEC, SEM_SPEC, SEM_SPEC, *([HBM_SPEC] * (2 * nt)), pl.BlockSpec(memory_space=pltpu.VMEM)),
        input_output_aliases={i: 3 + i for i in range(2 * nt)},
        compiler_params=pltpu.CompilerParams(has_side_effects=EFFECT),
    )(*[hb(a) for a in srcs], *[hb(lax.empty(s, a.dtype)) for s, a in zip(land_shapes, srcs)], *deps)
    return dict(send=outs[0], recv=outs[1], own=outs[2], src=list(outs[3:3 + nt]), land=list(outs[3 + nt:3 + 2 * nt]),
                token=outs[-1], per_peer=per_peer, peers=peers)


def exchange_wait(h, after, *, name):
    nt = len(h["src"])
    per_peer, peers = h["per_peer"], h["peers"]

    def body(*refs):
        src, land, send, recv, own = refs[:nt], refs[nt:2 * nt], refs[2 * nt], refs[2 * nt + 1], refs[2 * nt + 2]
        for cp in _exchange_copies(src, land, send, recv, per_peer, peers):
            cp.wait_send()
            cp.wait_recv()
        for cp in _own_copies(src, land, own, per_peer):
            cp.wait()

    outs = pl.pallas_call(
        body, name=name,
        out_shape=(*[pltpu.HBM(a.shape, a.dtype) for a in h["src"]], *[pltpu.HBM(a.shape, a.dtype) for a in h["land"]]),
        in_specs=[HBM_SPEC] * (2 * nt) + [SEM_SPEC, SEM_SPEC, SEM_SPEC, pl.BlockSpec(memory_space=pl.ANY)],
        out_specs=tuple([HBM_SPEC] * (2 * nt)),
        input_output_aliases={i: i for i in range(2 * nt)},
        compiler_params=pltpu.CompilerParams(has_side_effects=EFFECT),
    )(*h["src"], *h["land"], h["send"], h["recv"], h["own"], after)
    return list(outs[nt:])


def relay_start(lands, *, name):
    nt = len(lands)
    ns = len(RELAY) * nt

    def body(*refs):
        for cp in _relay_copies(refs[:nt], refs[nt], refs[nt + 1]):
            cp.start()

    outs = pl.pallas_call(
        body, name=name,
        out_shape=(pltpu.SemaphoreType.DMA((ns,)), pltpu.SemaphoreType.DMA((ns,)),
                   *[pltpu.HBM(a.shape, a.dtype) for a in lands]),
        in_specs=[HBM_SPEC] * nt, out_specs=(SEM_SPEC, SEM_SPEC, *([HBM_SPEC] * nt)),
        input_output_aliases={i: 2 + i for i in range(nt)},
        compiler_params=pltpu.CompilerParams(has_side_effects=EFFECT),
    )(*lands)
    return dict(send=outs[0], recv=outs[1], land=list(outs[2:]))


def relay_wait(h, *, name):
    nt = len(h["land"])

    def body(*refs):
        for cp in _relay_copies(refs[:nt], refs[nt], refs[nt + 1]):
            cp.wait_send()
            cp.wait_recv()

    outs = pl.pallas_call(
        body, name=name, out_shape=tuple(pltpu.HBM(a.shape, a.dtype) for a in h["land"]),
        in_specs=[HBM_SPEC] * nt + [SEM_SPEC, SEM_SPEC], out_specs=tuple([HBM_SPEC] * nt),
        input_output_aliases={i: i for i in range(nt)},
        compiler_params=pltpu.CompilerParams(has_side_effects=EFFECT),
    )(*h["land"], h["send"], h["recv"])
    return list(outs)


def _adamw_math(w, g, m, v):
    nm = B1 * m + (1.0 - B1) * g
    nv = B2 * v + (1.0 - B2) * (g * g)
    m_hat = nm / (1.0 - B1 ** STEP)
    v_hat = nv / (1.0 - B2 ** STEP)
    return -LR * (m_hat / (jnp.sqrt(v_hat) + AEPS) + WD * w), nm, nv


def adamw_many(ws, gs, ms, vs, *, name):
    n = len(ws)

    def body(*refs):
        for k in range(n):
            d, nm, nv = _adamw_math(refs[k][...], refs[n + k][...], refs[2 * n + k][...], refs[3 * n + k][...])
            refs[4 * n + k][...] = d
            refs[5 * n + k][...] = nm
            refs[6 * n + k][...] = nv

    vm = pl.BlockSpec(memory_space=pltpu.VMEM)
    sh = [jax.ShapeDtypeStruct(w.shape, F32) for w in ws]
    outs = pl.pallas_call(body, name=name, in_specs=[vm] * (4 * n), out_specs=[vm] * (3 * n), out_shape=sh * 3,
                          )(*ws, *gs, *ms, *vs)
    return outs[:n], outs[n:2 * n], outs[2 * n:]


def adamw(w, g, m, v, *, name, tr=256):
    R, C = w.shape
    tr = _pick(R, tr, 8)

    def body(w_ref, g_ref, m_ref, v_ref, d_ref, nm_ref, nv_ref):
        d_ref[...], nm_ref[...], nv_ref[...] = _adamw_math(w_ref[...], g_ref[...], m_ref[...], v_ref[...])

    blk = pl.BlockSpec((tr, C), lambda i: (i, 0))
    sh = jax.ShapeDtypeStruct((R, C), F32)
    return pl.pallas_call(
        body, name=name, grid=(R // tr,), in_specs=[blk, blk, blk, blk], out_specs=[blk, blk, blk],
        out_shape=[sh, sh, sh], compiler_params=pltpu.CompilerParams(dimension_semantics=("parallel",)),
    )(w, g, m, v)


def adamw_slots(w, slots, m, v, *, name, tr=256):
    unit = w.ndim == 3
    R, C = w.shape[0], w.shape[-1]
    if R % 16 == 0:
        tr = _pick(R, tr, 16)
    else:
        tr = 144

    def body(w_ref, s_ref, m_ref, v_ref, g_ref, d_ref, nm_ref, nv_ref):
        g = s_ref[0].astype(F32)
        for k in range(1, NDEV):
            g = g + s_ref[k].astype(F32)
        g_ref[...] = g
        d_ref[...], nm_ref[...], nv_ref[...] = _adamw_math(w_ref[...], g, m_ref[...], v_ref[...])

    blk = pl.BlockSpec((tr, None, C), lambda i: (i, 0, 0)) if unit else pl.BlockSpec((tr, C), lambda i: (i, 0))
    sh = jax.ShapeDtypeStruct(w.shape, F32)
    return pl.pallas_call(
        body, name=name, grid=(pl.cdiv(R, tr),), in_specs=[blk, pl.BlockSpec((NDEV, tr, C), lambda i: (0, i, 0)), blk, blk],
        out_specs=[blk, blk, blk, blk], out_shape=[sh, sh, sh, sh],
        compiler_params=pltpu.CompilerParams(dimension_semantics=("parallel",)),
    )(w, slots, m, v)


def _padc(a, n=D):
    return jnp.pad(a, ((0, 0), (0, n - a.shape[1])))


def kernel(x, c, ctx, c_ctx, w_ada, b_ada, norm1_g, w_in, q_norm_g, kv_norm_g, w_uq, w_ukv, conv_w, conv_b, w_attn_out, w_conv_out, w_o, norm2_g, w_up, ffn_conv_w, ffn_conv_b, w_down, final_g, loss_target, m_c_ctx, m_w_ada, m_b_ada, m_norm1_g, m_w_in, m_q_norm_g, m_kv_norm_g, m_w_uq, m_w_ukv, m_conv_w, m_conv_b, m_w_attn_out, m_w_conv_out, m_w_o, m_norm2_g, m_w_up, m_ffn_conv_w, m_ffn_conv_b, m_w_down, m_final_g, v_c_ctx, v_w_ada, v_b_ada, v_norm1_g, v_w_in, v_q_norm_g, v_kv_norm_g, v_w_uq, v_w_ukv, v_conv_w, v_conv_b, v_w_attn_out, v_w_conv_out, v_w_o, v_norm2_g, v_w_up, v_ffn_conv_w, v_ffn_conv_b, v_w_down, v_final_g):
    me = 4 * lax.axis_index("x") + 2 * lax.axis_index("y") + lax.axis_index("c")
    W = dict(c_ctx=c_ctx, w_ada=w_ada, b_ada=b_ada, norm1_g=norm1_g, w_in=w_in, q_norm_g=q_norm_g, kv_norm_g=kv_norm_g,
             w_uq=w_uq, w_ukv=w_ukv, conv_w=conv_w, conv_b=conv_b, w_attn_out=w_attn_out, w_conv_out=w_conv_out, w_o=w_o,
             norm2_g=norm2_g, w_up=w_up, ffn_conv_w=ffn_conv_w, ffn_conv_b=ffn_conv_b, w_down=w_down, final_g=final_g)
    M = dict(c_ctx=m_c_ctx, w_ada=m_w_ada, b_ada=m_b_ada, norm1_g=m_norm1_g, w_in=m_w_in, q_norm_g=m_q_norm_g,
             kv_norm_g=m_kv_norm_g, w_uq=m_w_uq, w_ukv=m_w_ukv, conv_w=m_conv_w, conv_b=m_conv_b, w_attn_out=m_w_attn_out,
             w_conv_out=m_w_conv_out, w_o=m_w_o, norm2_g=m_norm2_g, w_up=m_w_up, ffn_conv_w=m_ffn_conv_w,
             ffn_conv_b=m_ffn_conv_b, w_down=m_w_down, final_g=m_final_g)
    V = dict(c_ctx=v_c_ctx, w_ada=v_w_ada, b_ada=v_b_ada, norm1_g=v_norm1_g, w_in=v_w_in, q_norm_g=v_q_norm_g,
             kv_norm_g=v_kv_norm_g, w_uq=v_w_uq, w_ukv=v_w_ukv, conv_w=v_conv_w, conv_b=v_conv_b, w_attn_out=v_w_attn_out,
             w_conv_out=v_w_conv_out, w_o=v_w_o, norm2_g=v_norm2_g, w_up=v_w_up, ffn_conv_w=v_ffn_conv_w,
             ffn_conv_b=v_ffn_conv_b, w_down=v_w_down, final_g=v_final_g)
    names = list(W)
    transposed = ("w_up",)
    as2d = lambda k, a: (a.reshape(1, -1) if a.ndim == 1 else
                         a[0].T if k in transposed else a.reshape(a.shape[-2], a.shape[-1]))
    W2 = {k: as2d(k, a) for k, a in W.items()}
    M2 = {k: as2d(k, a) for k, a in M.items()}
    V2 = {k: as2d(k, a) for k, a in V.items()}
    unit3 = lambda a: jnp.transpose(a, (2, 0, 1))
    W3, M3, V3 = unit3(W["w_in"]), unit3(M["w_in"]), unit3(V["w_in"])
    nsh = W2["w_ada"].shape[1]

    b_sh = lax.dynamic_slice(W2["b_ada"], (0, me * nsh), (1, nsh))
    s_all, m_all = ada_fwd(c, W2["c_ctx"], _padc(W2["ffn_conv_w"]), _padc(W2["conv_w"]), W2["w_ada"], b_sh, [],
                           name="ada_fwd")
    mod_lat = m_all[:, 0, :].reshape(1, 6 * D)
    mod_ctx = m_all[:, 1, :].reshape(1, 6 * D)
    ffn_w_full = s_all[:, 2:5, :2 * DFF // NDEV].transpose(1, 0, 2).reshape(3, 2 * DFF)
    conv_w_full = s_all[:, 5:8, :CONV // NDEV].transpose(1, 0, 2).reshape(3, CONV)

    stage_w = {"in": ["w_in"], "mid": ["w_uq", "w_ukv", "w_attn_out", "w_conv_out", "w_o"], "up": ["w_up"],
               "down": ["w_down"]}
    two_level = ("in", "mid")
    ag, tok = {}, m_all
    for st, nms in stage_w.items():
        ag[st] = exchange_start([W2[nm].astype(BF) for nm in nms], per_peer=False, dep=tok, name="ag_start_" + st,
                                peers=FIRST_HOP if st in two_level else ALL_PEERS)
        tok = ag[st]["token"]

    def get_w(stage, after):
        lands = exchange_wait(ag[stage], after, name="ag_wait_" + stage)
        if stage in two_level:
            lands = relay_wait(relay_start(lands, name="ag_relay_" + stage), name="ag_relay_wait_" + stage)
        g = dict(zip(stage_w[stage], lands))
        if stage == "in":
            return build_win(g["w_in"], name="build_win")
        if stage == "mid":
            wq2, wkv2 = build_wq_wkv(g["w_uq"], g["w_ukv"], name="build_wq_wkv")
            return (wq2, wkv2, unshard_cols(g["w_attn_out"], name="unshard_w_attn_out"),
                    unshard_cols(g["w_conv_out"], name="unshard_w_conv_out"), g["w_o"].reshape(D, D))
        if stage == "up":
            return g["w_up"].reshape(2 * DFF, D)
        return g["w_down"].reshape(DFF, D)

    stage_g = {"ffn": ["w_up", "w_down"], "mid": ["w_attn_out", "w_conv_out", "w_o"], "qkv": ["w_uq", "w_ukv"],
               "in": ["w_in"]}
    rs = {}

    def put_g(stage, g):
        if stage == "in":
            parts = [shard_win_grad(g["dwin"], g["dwin_c"], name="shard_win_grad")]
        elif stage == "mid":
            parts = [shard_cols(g["dwao"], name="shard_w_attn_out"), shard_cols(g["dwco"], name="shard_w_conv_out"),
                     g["dwo"].reshape(NDEV, D // NDEV, D)]
        elif stage == "qkv":
            parts = list(shard_wq_wkv_grad(g["dwq2"], g["dwkv2"], name="shard_wq_wkv_grad"))
        else:
            parts = [g["dwup"].reshape(NDEV, 2 * DFF // NDEV, D), g["dwdn"].reshape(NDEV, DFF // NDEV, D)]
        rs[stage] = exchange_start(parts, per_peer=True, name="rs_start_" + stage)
        return rs[stage]["token"]

    r = _local_step(x[0], ctx[0], loss_target[0], mod_lat, mod_ctx, W2["norm1_g"], W2["q_norm_g"], W2["kv_norm_g"],
                    W2["norm2_g"], W2["final_g"], conv_w_full, W2["conv_b"], ffn_w_full, W2["ffn_conv_b"], get_w, put_g,
                    ag["down"]["token"])

    G, DL, NM, NV = {}, {}, {}, {}

    def finish(stage, after):
        for nm, sl in zip(stage_g[stage], exchange_wait(rs[stage], after, name="rs_wait_" + stage)):
            wmv = (W3, M3, V3) if nm == "w_in" else (W2[nm], M2[nm], V2[nm])
            G[nm], DL[nm], NM[nm], NV[nm] = adamw_slots(wmv[0], sl, wmv[1], wmv[2], name="adamw_" + nm)
            after = DL[nm]
        return after

    after = r["dx"]
    for st in ("ffn", "mid"):
        after = finish(st, after)

    a_buf, ssum = sync_small(r, [DL[nm] for st in ("ffn", "mid") for nm in stage_g[st]], name="sync_small")
    loss = ssum[P_LOSS, 0]
    G["norm1_g"] = ssum[P_N1:P_N1 + 1]
    G["q_norm_g"] = ssum[P_QG:P_QG + 1, :QL]
    G["kv_norm_g"] = ssum[P_KVG:P_KVG + 1, :KVL]
    G["conv_b"] = ssum[P_CB:P_CB + 1, :CONV]
    G["norm2_g"] = ssum[P_N2:P_N2 + 1]
    G["ffn_conv_b"] = ssum[P_FB:P_FB + 2 * FROWS].reshape(1, 2, FROWS * D)[:, :, :DFF].reshape(1, 2 * DFF)
    G["final_g"] = ssum[P_FG:P_FG + 1]
    G["conv_w"] = lax.dynamic_slice(ssum[P_CW:P_CW + 3, :CONV], (0, me * (CONV // NDEV)), (3, CONV // NDEV))
    fw_full = ssum[P_FW:P_FW + 6 * FROWS].reshape(3, 2, FROWS * D)[:, :, :DFF].reshape(3, 2 * DFF)
    G["ffn_conv_w"] = lax.dynamic_slice(fw_full, (0, me * (2 * DFF // NDEV)), (3, 2 * DFF // NDEV))
    G["b_ada"] = (ssum[P_DML:P_DML + 6] + ssum[P_DMC:P_DMC + 6]).reshape(1, 6 * D)

    dml = lax.dynamic_slice(a_buf[:, P_DML:P_DML + 6, :].reshape(NDEV, 6 * D), (0, me * nsh), (NDEV, nsh))
    dmc = lax.dynamic_slice(ssum[P_DMC:P_DMC + 6].reshape(1, 6 * D), (0, me * nsh), (1, nsh))
    G["w_ada"], gcc = ada_bwd(s_all, dml, dmc, W2["w_ada"], W2["c_ctx"], name="ada_bwd")
    G["c_ctx"] = gcc[0:1]

    DL["w_ada"], NM["w_ada"], NV["w_ada"] = adamw(W2["w_ada"], G["w_ada"], M2["w_ada"], V2["w_ada"], name="adamw_w_ada")
    small = ["c_ctx", "b_ada", "norm1_g", "q_norm_g", "kv_norm_g", "conv_b", "norm2_g", "ffn_conv_b", "final_g", "conv_w",
             "ffn_conv_w"]
    ds, nms, nvs = adamw_many([W2[k] for k in small], [G[k] for k in small], [M2[k] for k in small],
                              [V2[k] for k in small], name="adamw_small")
    for k, nm in enumerate(small):
        DL[nm], NM[nm], NV[nm] = ds[k], nms[k], nvs[k]
    finish("qkv", finish("in", ds[0]))

    outs = [loss, r["dx"][None]]
    for grp in (G, DL, NM, NV):
        outs += [grp[nm].T[None] if nm in transposed else
                 jnp.transpose(grp[nm], (1, 2, 0)) if nm == "w_in" else grp[nm].reshape(W[nm].shape) for nm in names]
    return tuple(outs)
```

```python
import functools
import numpy as np
import jax
import jax.numpy as jnp
from jax import lax
from jax.experimental import pallas as pl
from jax.experimental.pallas import tpu as pltpu

F32 = jnp.float32
BF = jnp.bfloat16
MESH = pl.DeviceIdType.MESH

D = 1024
T = 2048
TC = 256
TKV = T + TC
GRID_W = 64
NH = 8
DN = 64
DR = 32
DV = 64
QL = 384
KVL = 256
CONV = 512
DFF = 2816
EPS = 1e-6
ROPE_THETA = 10000.0
SCALE = (DN + DR) ** -0.5
NDEV = 8
HP = 128

O_GA, O_GC, O_KV, O_Q, O_CV = 0, 1024, 2048, 2560, 3072
NIN = 4608
CVB = 256
N_IN = 4256
SH_IN = N_IN // NDEV

LR, B1, B2, AEPS, WD, STEP = 0.001, 0.9, 0.999, 1e-08, 0.01, 10


def _pick(n, target, mult=128):
    best = None
    for d in range(mult, min(n, target) + 1, mult):
        if n % d == 0:
            best = d
    return best if best is not None else n


def _swap_start(g):
    return 8 * (g ^ 1)


def mm(a, b, *, ta=False, tb=False, out_dtype=F32, name, tm=1024, tn=1024, tk=2048, M=None, N=None, K=None,
       a_off=(0, 0), b_off=(0, 0), a_stack=False, b_stack=False, o_stack=False, dep=None):
    def dims(arr, stack):
        return (arr.shape[1], 2 * arr.shape[2]) if stack else arr.shape

    ar, ac = dims(a, a_stack)
    br, bc = dims(b, b_stack)
    M = M or ((ac if ta else ar) - a_off[1 if ta else 0])
    K = K or ((ar if ta else ac) - a_off[0 if ta else 1])
    N = N or ((br if tb else bc) - b_off[0 if tb else 1])
    tm = _pick(M, tm, 128 if ta else 16)
    tn = _pick(N // 2 if (o_stack or (b_stack and not tb)) else N, tn, 128)
    tk = _pick(K // 2 if ((a_stack and not ta) or (b_stack and tb)) else K, tk, 128)
    nk = K // tk
    ca = 0 if ta else 1
    cb = 1 if tb else 0

    def body(a_ref, b_ref, *rest):
        o_ref, acc = rest[-2:]
        k = pl.program_id(2)
        part = lax.dot_general(a_ref[...].astype(BF), b_ref[...].astype(BF),
                               (((ca,), (cb,)), ((), ())), preferred_element_type=F32)
        if nk == 1:
            o_ref[...] = part.astype(o_ref.dtype)
        else:
            @pl.when(k == 0)
            def _():
                acc[...] = part

            @pl.when(k > 0)
            def _():
                acc[...] += part

            @pl.when(k == nk - 1)
            def _():
                o_ref[...] = acc[...].astype(o_ref.dtype)

    def spec(blk, rc, off, stack, ncols):
        assert off[0] % blk[0] == 0 and off[1] % blk[1] == 0, (name, blk, off)
        ro, co = off[0] // blk[0], off[1] // blk[1]
        if not stack:
            return pl.BlockSpec(blk, lambda i, j, k: (rc(i, j, k)[0] + ro, rc(i, j, k)[1] + co))
        nhb = ncols // 2 // blk[1]
        return pl.BlockSpec((None,) + blk,
                            lambda i, j, k: ((rc(i, j, k)[1] + co) // nhb, rc(i, j, k)[0] + ro, (rc(i, j, k)[1] + co) % nhb))

    a_spec = spec((tk, tm), lambda i, j, k: (k, i), a_off, a_stack, ac) if ta else \
        spec((tm, tk), lambda i, j, k: (i, k), a_off, a_stack, ac)
    b_spec = spec((tn, tk), lambda i, j, k: (j, k), b_off, b_stack, bc) if tb else \
        spec((tk, tn), lambda i, j, k: (k, j), b_off, b_stack, bc)
    o_spec = spec((tm, tn), lambda i, j, k: (i, j), (0, 0), o_stack, N)
    o_shape = (2, M, N // 2) if o_stack else (M, N)
    deps = [] if dep is None else [dep]
    return pl.pallas_call(
        body, name=name, grid=(M // tm, N // tn, nk),
        in_specs=[a_spec, b_spec] + [pl.BlockSpec(memory_space=pl.ANY)] * len(deps),
        out_specs=o_spec, out_shape=jax.ShapeDtypeStruct(o_shape, out_dtype),
        scratch_shapes=[pltpu.VMEM((tm, tn) if nk > 1 else (8, 128), F32)],
        compiler_params=pltpu.CompilerParams(dimension_semantics=("parallel", "parallel", "arbitrary")),
    )(a, b, *deps)


def _row(width):
    return pl.BlockSpec((1, width), lambda *_: (0, 0))


NLAT = T // TC


def normmod_cat(ctx, x, g, csc, csh, sc, sh, dep, *, name, tm=256):
    assert tm == TC

    def body(c_ref, x_ref, g_ref, csc_ref, csh_ref, sc_ref, sh_ref, dep_ref, h_ref):
        last = pl.program_id(0) == NLAT
        xv = jnp.where(last, c_ref[...], x_ref[...])
        scv = jnp.where(last, csc_ref[...], sc_ref[...])
        shv = jnp.where(last, csh_ref[...], sh_ref[...])
        r = lax.rsqrt(jnp.mean(xv * xv, axis=-1, keepdims=True) + EPS)
        h_ref[...] = ((xv * r * g_ref[...]) * (1.0 + scv) + shv).astype(BF)

    return pl.pallas_call(
        body, name=name, grid=(TKV // tm,),
        in_specs=[pl.BlockSpec((tm, D), lambda i: (0, 0)), pl.BlockSpec((tm, D), lambda i: (jnp.minimum(i, NLAT - 1), 0)),
                  _row(D), _row(D), _row(D), _row(D), _row(D), pl.BlockSpec(memory_space=pl.ANY)],
        out_specs=pl.BlockSpec((tm, D), lambda i: (i, 0)), out_shape=jax.ShapeDtypeStruct((TKV, D), BF),
        compiler_params=pltpu.CompilerParams(dimension_semantics=("parallel",)),
    )(ctx, x, g, csc, csh, sc, sh, dep)


def kvprep(pc, p, kvg, wkv2, ck, sk, *, name, tm=256):
    assert tm == TC
    nb = TKV // tm
    kvcol = O_KV // 512

    def body(pc_ref, p_ref, g_ref, w_ref, ck_ref, sk_ref, k_ref, v_ref, ckv_ref):
        i = pl.program_id(0)
        t = jnp.where(i == NLAT, pc_ref[...], p_ref[...])
        pk = t[:, :KVL]
        r = lax.rsqrt(jnp.mean(pk * pk, axis=-1, keepdims=True) + EPS)
        ckv = (pk * r * g_ref[...]).astype(BF)
        ckv_ref[...] = ckv
        kv2 = jnp.dot(ckv, w_ref[...], preferred_element_type=F32)
        krr = t[:, KVL:KVL + HP] * ck_ref[...] + t[:, KVL + HP:KVL + 2 * HP] * sk_ref[...]
        k_ref[...] = (kv2[:, :NH * HP] + jnp.concatenate([krr] * NH, axis=1)).astype(BF)
        v_ref[...] = kv2[:, NH * HP:].astype(BF)

    return pl.pallas_call(
        body, name=name, grid=(nb,),
        in_specs=[pl.BlockSpec((tm, 512), lambda i: (0, 0)),
                  pl.BlockSpec((tm, 512), lambda i: (jnp.minimum(i, NLAT - 1), kvcol)),
                  _row(KVL), pl.BlockSpec((KVL, NH * HP + NH * DV), lambda i: (0, 0)),
                  pl.BlockSpec((tm, HP), lambda i: (i, 0)), pl.BlockSpec((tm, HP), lambda i: (i, 0))],
        out_specs=[pl.BlockSpec((tm, NH * HP), lambda i: (i, 0)), pl.BlockSpec((tm, NH * DV), lambda i: (i, 0)),
                   pl.BlockSpec((tm, KVL), lambda i: (i, 0))],
        out_shape=[jax.ShapeDtypeStruct((TKV, NH * HP), BF), jax.ShapeDtypeStruct((TKV, NH * DV), BF),
                   jax.ShapeDtypeStruct((TKV, KVL), BF)],
        compiler_params=pltpu.CompilerParams(dimension_semantics=("parallel",)),
    )(pc, p, kvg, wkv2, ck, sk)


def qprep(p, qg, wq2, cq_t, sq_t, *, name, tm=256):
    qcol = O_Q // 512

    def body(p_ref, g_ref, w_ref, c_ref, s_ref, q_ref, cq_ref):
        pq = p_ref[...]
        r = lax.rsqrt(jnp.sum(pq * pq, axis=-1, keepdims=True) * (1.0 / QL) + EPS)
        cq = (pq * r * g_ref[...]).astype(BF)
        cq_ref[...] = cq
        q2 = jnp.dot(cq, w_ref[...], preferred_element_type=F32)
        cc = jnp.concatenate([c_ref[...]] * NH, axis=1)
        ss = jnp.concatenate([s_ref[...]] * NH, axis=1)
        q_ref[...] = (q2[:, :NH * HP] * cc + q2[:, NH * HP:] * ss).astype(BF)

    return pl.pallas_call(
        body, name=name, grid=(T // tm,),
        in_specs=[pl.BlockSpec((tm, 512), lambda i: (i, qcol)), _row(512),
                  pl.BlockSpec((512, 2 * NH * HP), lambda i: (0, 0)),
                  pl.BlockSpec((tm, HP), lambda i: (i, 0)), pl.BlockSpec((tm, HP), lambda i: (i, 0))],
        out_specs=[pl.BlockSpec((tm, NH * HP), lambda i: (i, 0)), pl.BlockSpec((tm, 512), lambda i: (i, 0))],
        out_shape=[jax.ShapeDtypeStruct((T, NH * HP), BF), jax.ShapeDtypeStruct((T, 512), BF)],
        compiler_params=pltpu.CompilerParams(dimension_semantics=("parallel",)),
    )(p, qg, wq2, cq_t, sq_t)


def _head_mask(h):
    lanes = lax.broadcasted_iota(jnp.int32, (1, 2 * DV), 1)
    return (lanes // DV) == (h % 2)


LOG2E = 1.4426950408889634


def attn_fwd(q, k, v, *, name, tq=1024, kc=768):
    def body(q_ref, k_ref, v_ref, o_ref, lse_ref):
        h = pl.program_id(1)
        qv = q_ref[...]
        m = l = acc = None
        for c in range(TKV // kc):
            s = lax.dot_general(qv, k_ref[c * kc:(c + 1) * kc, :], (((1,), (1,)), ((), ())),
                                preferred_element_type=F32) * (SCALE * LOG2E)
            mc = jnp.max(s, axis=-1, keepdims=True)
            if c == 0:
                m = mc
                e = jnp.exp2(s - m)
                l = jnp.sum(e, axis=-1, keepdims=True)
                acc = jnp.dot(e.astype(BF), v_ref[c * kc:(c + 1) * kc, :], preferred_element_type=F32)
            else:
                mn = jnp.maximum(m, mc)
                a = jnp.exp2(m - mn)
                e = jnp.exp2(s - mn)
                l = l * a + jnp.sum(e, axis=-1, keepdims=True)
                acc = acc * a + jnp.dot(e.astype(BF), v_ref[c * kc:(c + 1) * kc, :], preferred_element_type=F32)
                m = mn
        o2 = jnp.where(_head_mask(h), acc * (1.0 / l), 0.0).astype(BF)
        lse_ref[...] = jnp.broadcast_to(m + jnp.log(l) * LOG2E, (tq, HP))

        @pl.when(h % 2 == 0)
        def _():
            o_ref[...] = o2

        @pl.when(h % 2 == 1)
        def _():
            o_ref[...] = o_ref[...] + o2

    return pl.pallas_call(
        body, name=name, grid=(T // tq, NH),
        in_specs=[pl.BlockSpec((tq, HP), lambda i, h: (i, h)), pl.BlockSpec((TKV, HP), lambda i, h: (0, h)),
                  pl.BlockSpec((TKV, 2 * DV), lambda i, h: (0, h // 2))],
        out_specs=[pl.BlockSpec((tq, 2 * DV), lambda i, h: (i, h // 2)), pl.BlockSpec((tq, HP), lambda i, h: (i, h))],
        out_shape=[jax.ShapeDtypeStruct((T, NH * DV), BF), jax.ShapeDtypeStruct((T, NH * HP), F32)],
        compiler_params=pltpu.CompilerParams(dimension_semantics=("parallel", "arbitrary")),
    )(q, k, v)


def _shift_dn(x):
    n = x.shape[0]
    rows = lax.broadcasted_iota(jnp.int32, (n, 1), 0)
    return jnp.where(rows == 0, 0.0, pltpu.roll(x, 1, axis=0))


def _shift_up(x):
    n = x.shape[0]
    rows = lax.broadcasted_iota(jnp.int32, (n, 1), 0)
    return jnp.where(rows == n - 1, 0.0, pltpu.roll(x, n - 1, axis=0))


def _conv(x, w_ref, b_ref):
    return b_ref[...] + _shift_dn(x) * w_ref[0:1, :] + x * w_ref[1:2, :] + _shift_up(x) * w_ref[2:3, :]


def _conv_t(dy, w_ref):
    return _shift_up(dy) * w_ref[0:1, :] + dy * w_ref[1:2, :] + _shift_dn(dy) * w_ref[2:3, :]


def _conv_wgrad(dw_ref, dy, x):
    dw_ref[0:1, :] = jnp.sum(dy * _shift_dn(x), axis=0, keepdims=True)
    dw_ref[1:2, :] = jnp.sum(dy * x, axis=0, keepdims=True)
    dw_ref[2:3, :] = jnp.sum(dy * _shift_up(x), axis=0, keepdims=True)


def convz(p, cw, cb, *, name):
    o0 = O_CV // (3 * CVB)

    def body(p_ref, w_ref, bias_ref, z_ref):
        xv, bv, cv = p_ref[:, 0:CVB], p_ref[:, CVB:2 * CVB], p_ref[:, 2 * CVB:3 * CVB]
        z_ref[...] = (bv * _conv(cv * xv, w_ref, bias_ref)).astype(BF)

    return pl.pallas_call(
        body, name=name, grid=(CONV // CVB,),
        in_specs=[pl.BlockSpec((T, 3 * CVB), lambda j: (0, o0 + j)), pl.BlockSpec((3, CVB), lambda j: (0, j)),
                  pl.BlockSpec((1, CVB), lambda j: (0, j))],
        out_specs=pl.BlockSpec((T, CVB), lambda j: (0, j)),
        out_shape=jax.ShapeDtypeStruct((T, CONV), BF),
        compiler_params=pltpu.CompilerParams(dimension_semantics=("parallel",)),
    )(p, cw, cb)


def out_proj_merge(o, wao, z, wco, p, *, name, tm=512):
    kin = o.shape[1]

    def body(o_ref, wa_ref, z_ref, wc_ref, ga_ref, gc_ref, ya_ref, yc_ref, m_ref):
        ya = jnp.dot(o_ref[...], wa_ref[...], preferred_element_type=F32)
        yc = jnp.dot(z_ref[...], wc_ref[...], preferred_element_type=F32)
        ya_ref[...] = ya
        yc_ref[...] = yc
        m_ref[...] = (jax.nn.sigmoid(ga_ref[...]) * ya + jax.nn.sigmoid(gc_ref[...]) * yc).astype(BF)

    blk = pl.BlockSpec((tm, D), lambda i: (i, 0))
    act = pl.BlockSpec((tm, kin), lambda i: (i, 0))
    wsp = pl.BlockSpec((kin, D), lambda i: (0, 0))
    sh = jax.ShapeDtypeStruct((T, D), F32)
    return pl.pallas_call(
        body, name=name, grid=(T // tm,),
        in_specs=[act, wsp, act, wsp, pl.BlockSpec((tm, D), lambda i: (i, O_GA // D)),
                  pl.BlockSpec((tm, D), lambda i: (i, O_GC // D))],
        out_specs=[blk, blk, blk], out_shape=[sh, sh, jax.ShapeDtypeStruct((T, D), BF)],
        compiler_params=pltpu.CompilerParams(dimension_semantics=("parallel",)),
    )(o, wao, z, wco, p, p)


CONV_HALO = 8
CONV_ROWS = 256


def _row_chunks(n, chunk, carry):
    carry = chunk(0, True, False, carry)
    carry = lax.fori_loop(1, n // CONV_ROWS - 1, lambda c, a: chunk(c * CONV_ROWS, False, False, a), carry)
    return chunk(n - CONV_ROWS, False, True, carry)


def _ext_rows(ref, r0, first, last):
    n, w = ref.shape
    zero = jnp.zeros((CONV_HALO, w), ref.dtype)
    if first:
        return jnp.concatenate([zero, ref[0:CONV_ROWS + CONV_HALO, :]], axis=0)
    if last:
        return jnp.concatenate([ref[n - CONV_ROWS - CONV_HALO:n, :], zero], axis=0)
    return ref[pl.ds(pl.multiple_of(r0 - CONV_HALO, 8), CONV_ROWS + 2 * CONV_HALO), :]


def _center_rows(r0, first, last):
    return slice(r0, r0 + CONV_ROWS) if (first or last) else pl.ds(pl.multiple_of(r0, 8), CONV_ROWS)


def _roll_dn(x):
    return pltpu.roll(x, 1, axis=0)


def _roll_up(x):
    return pltpu.roll(x, x.shape[0] - 1, axis=0)


_CTR = slice(CONV_HALO, CONV_HALO + CONV_ROWS)


def ffn_act(u0, cw, cb, *, name, tc=256):
    nb = DFF // tc

    def body(u_ref, wg_ref, wv_ref, bg_ref, bv_ref, f_ref):
        wg = [wg_ref[k:k + 1, :] for k in range(3)]
        wv = [wv_ref[k:k + 1, :] for k in range(3)]
        bg, bv = bg_ref[...], bv_ref[...]

        def chunk(r0, first, last, carry):
            xg, xv = _ext_rows(u_ref.at[0], r0, first, last), _ext_rows(u_ref.at[1], r0, first, last)
            ug = bg + _roll_dn(xg) * wg[0] + xg * wg[1] + _roll_up(xg) * wg[2]
            uv = bv + _roll_dn(xv) * wv[0] + xv * wv[1] + _roll_up(xv) * wv[2]
            f_ref[_center_rows(r0, first, last), :] = (ug * jax.nn.sigmoid(ug) * uv)[_CTR].astype(BF)
            return carry

        _row_chunks(T, chunk, 0)

    return pl.pallas_call(
        body, name=name, grid=(nb,),
        in_specs=[pl.BlockSpec((2, T, tc), lambda j: (0, 0, j)),
                  pl.BlockSpec((3, tc), lambda j: (0, j)), pl.BlockSpec((3, tc), lambda j: (0, nb + j)),
                  pl.BlockSpec((1, tc), lambda j: (0, j)), pl.BlockSpec((1, tc), lambda j: (0, nb + j))],
        out_specs=pl.BlockSpec((T, tc), lambda j: (0, j)),
        out_shape=jax.ShapeDtypeStruct((T, DFF), BF),
        compiler_params=pltpu.CompilerParams(dimension_semantics=("parallel",)),
    )(u0, cw, cw, cb, cb)


def rows_call(lead, ins, in_specs, out_shape, out_specs, fn, *, name, R, tm):
    tb, a_stack, tk = lead.get("tb", False), lead.get("a_stack", False), lead["tk"]
    K = 2 * lead["a"].shape[2] if a_stack else lead["a"].shape[1]
    nk = K // tk
    deps = [] if lead.get("dep") is None else [lead["dep"]]
    n_in = len(ins)

    def body(a_ref, b_ref, *refs):
        refs = refs[len(deps):]
        in_refs, out_refs, acc = refs[:n_in], refs[n_in:-1], refs[-1]
        i, k = pl.program_id(0), pl.program_id(1)
        part = lax.dot_general(a_ref[...].astype(BF), b_ref[...].astype(BF),
                               (((1,), (1 if tb else 0,)), ((), ())), preferred_element_type=F32)
        if nk == 1:
            fn(i, part, in_refs, out_refs)
            return

        @pl.when(k == 0)
        def _():
            acc[...] = part

        @pl.when(k > 0)
        def _():
            acc[...] += part

        @pl.when(k == nk - 1)
        def _():
            fn(i, acc[...], in_refs, out_refs)

    if a_stack:
        nhb = K // 2 // tk
        a_spec = pl.BlockSpec((None, tm, tk), lambda i, k: (k // nhb, i, k % nhb))
    else:
        a_spec = pl.BlockSpec((tm, tk), lambda i, k: (i, k))
    b_spec = pl.BlockSpec((D, tk), lambda i, k: (0, k)) if tb else pl.BlockSpec((tk, D), lambda i, k: (k, 0))
    return pl.pallas_call(
        body, name=name, grid=(R // tm, nk),
        in_specs=[a_spec, b_spec] + [pl.BlockSpec(memory_space=pl.ANY)] * len(deps) + list(in_specs),
        out_specs=out_specs, out_shape=out_shape,
        scratch_shapes=[pltpu.VMEM((tm, D) if nk > 1 else (8, 128), F32)],
        compiler_params=pltpu.CompilerParams(dimension_semantics=("arbitrary", "arbitrary")),
    )(lead["a"], lead["b"], *deps, *ins)


def _rblk(tm, w=D, col=0):
    return pl.BlockSpec((tm, w), lambda i, k: (i, col))


def _rrow(w=D):
    return pl.BlockSpec((1, w), lambda i, k: (0, 0))


def down_final(f, wdn, x1, g2, fg, tgt, *, name, tm=512):
    def fn(i, d, in_refs, out_refs):
        x1_ref, g2_ref, fg_ref, t_ref = in_refs
        d_ref, dx_ref, dd_ref, dfg_ref, loss_ref = out_refs
        d_ref[...] = d
        xv = x1_ref[...] + g2_ref[...] * d
        r = lax.rsqrt(jnp.mean(xv * xv, axis=-1, keepdims=True) + EPS)
        xh = xv * r
        diff = xh * fg_ref[...] - t_ref[...]
        part = 0.5 * jnp.sum(jnp.mean(diff * diff, axis=-1, keepdims=True), axis=0, keepdims=True)
        dy = diff * (1.0 / D)
        a = dy * fg_ref[...]
        dx = r * (a - xh * jnp.mean(a * xh, axis=-1, keepdims=True))
        dx_ref[...] = dx
        dd_ref[...] = (dx * g2_ref[...]).astype(BF)
        dfg = jnp.sum(dy * xh, axis=0, keepdims=True)

        @pl.when(i == 0)
        def _():
            dfg_ref[...] = dfg
            loss_ref[...] = jnp.broadcast_to(part, (1, 128))

        @pl.when(i > 0)
        def _():
            dfg_ref[...] += dfg
            loss_ref[...] += jnp.broadcast_to(part, (1, 128))

    blk = _rblk(tm)
    return rows_call(
        dict(a=f, b=wdn, tk=DFF), [x1, g2, fg, tgt], [blk, _rrow(), _rrow(), blk],
        [jax.ShapeDtypeStruct((T, D), F32), jax.ShapeDtypeStruct((T, D), F32), jax.ShapeDtypeStruct((T, D), BF),
         jax.ShapeDtypeStruct((1, D), F32), jax.ShapeDtypeStruct((1, 128), F32)],
        [blk, blk, blk, _rrow(), _rrow(128)], fn, name=name, R=T, tm=tm)


def oproj_resid(merged, wo, x, gate, g, sc, sh, *, name, tm=512):
    def fn(i, a, in_refs, out_refs):
        x_ref, gate_ref, g_ref, sc_ref, sh_ref = in_refs
        a_ref, x1_ref, h_ref = out_refs
        a_ref[...] = a
        xv = x_ref[...] + gate_ref[...] * a
        x1_ref[...] = xv
        r = lax.rsqrt(jnp.mean(xv * xv, axis=-1, keepdims=True) + EPS)
        h_ref[...] = ((xv * r * g_ref[...]) * (1.0 + sc_ref[...]) + sh_ref[...]).astype(BF)

    blk = _rblk(tm)
    return rows_call(
        dict(a=merged, b=wo, tk=D), [x, gate, g, sc, sh], [blk, _rrow(), _rrow(), _rrow(), _rrow()],
        [jax.ShapeDtypeStruct((T, D), F32), jax.ShapeDtypeStruct((T, D), F32), jax.ShapeDtypeStruct((T, D), BF)],
        [blk, blk, blk], fn, name=name, R=T, tm=tm)


def oproj_dx_gate_bwd(da, wo, p, ya, yc, wao, wco, *, name, tm=512):
    kin = wao.shape[0]

    def fn(i, dm, in_refs, out_refs):
        ga_ref, gc_ref, ya_ref, yc_ref, wa_ref, wc_ref = in_refs
        dya_ref, dyc_ref, dp_ref, do_ref, dz_ref = out_refs
        sa, sc_ = jax.nn.sigmoid(ga_ref[...]), jax.nn.sigmoid(gc_ref[...])
        dya, dyc = (dm * sa).astype(BF), (dm * sc_).astype(BF)
        dya_ref[...] = dya
        dyc_ref[...] = dyc
        dp_ref[:, 0:D] = (dm * ya_ref[...] * (sa * (1.0 - sa))).astype(BF)
        dp_ref[:, D:2 * D] = (dm * yc_ref[...] * (sc_ * (1.0 - sc_))).astype(BF)
        nt = (((1,), (1,)), ((), ()))
        do_ref[...] = lax.dot_general(dya, wa_ref[...], nt, preferred_element_type=F32).astype(BF)
        dz_ref[...] = lax.dot_general(dyc, wc_ref[...], nt, preferred_element_type=F32)

    blk = _rblk(tm)
    sh = jax.ShapeDtypeStruct((T, D), BF)
    wsp = pl.BlockSpec((kin, D), lambda i, k: (0, 0))
    return rows_call(
        dict(a=da, b=wo, tb=True, tk=D), [p, p, ya, yc, wao, wco],
        [_rblk(tm, D, O_GA // D), _rblk(tm, D, O_GC // D), blk, blk, wsp, wsp],
        [sh, sh, jax.ShapeDtypeStruct((T, NIN), BF), jax.ShapeDtypeStruct((T, kin), BF), jax.ShapeDtypeStruct((T, kin), F32)],
        [blk, blk, _rblk(tm, 2 * D), _rblk(tm, kin), _rblk(tm, kin)], fn, name=name, R=T, tm=tm)


def normmod_bwd(x, dh, g, sc, dres, gsrc, gate, *, name, tm=512):
    R = x.shape[0]
    tm = min(tm, R)
    has_res = dres is not None
    fused = isinstance(dh, dict)
    if fused:
        tb, a_stack, tk = dh.get("tb", False), dh.get("a_stack", False), dh["tk"]
        K = 2 * dh["a"].shape[2] if a_stack else dh["a"].shape[1]
        nk = K // tk
        deps = [] if dh.get("dep") is None else [dh["dep"]]
        n_dh = 2 + len(deps)
    else:
        nk, n_dh = 1, 1

    def elementwise(i, dhv, x_ref, g_ref, sc_ref, res_refs, out_refs):
        xv = x_ref[...]
        r = lax.rsqrt(jnp.mean(xv * xv, axis=-1, keepdims=True) + EPS)
        xh = xv * r
        n = xh * g_ref[...]
        dn = dhv * (1.0 + sc_ref[...])
        a = dn * g_ref[...]
        rows = [jnp.sum(dhv, axis=0, keepdims=True), jnp.sum(dhv * n, axis=0, keepdims=True),
                jnp.sum(dn * xh, axis=0, keepdims=True)]
        if has_res:
            dres_ref, gsrc_ref, gate_ref = res_refs
            dx_ref, dxg_ref, st_ref = out_refs
            dr = dres_ref[...]
            dx = dr + r * (a - xh * jnp.mean(a * xh, axis=-1, keepdims=True))
            dx_ref[...] = dx
            dxg_ref[...] = (dx * gate_ref[...]).astype(BF)
            rows.append(jnp.sum(dr * gsrc_ref[...], axis=0, keepdims=True))
        else:
            st_ref, = out_refs
            rows.append(jnp.zeros((1, D), F32))

        @pl.when(i == 0)
        def _():
            for k, row in enumerate(rows):
                st_ref[k:k + 1, :] = row

        @pl.when(i > 0)
        def _():
            for k, row in enumerate(rows):
                st_ref[k:k + 1, :] += row

    def body(*refs):
        x_ref, dh_refs, g_ref, sc_ref = refs[0], refs[1:1 + n_dh], refs[1 + n_dh], refs[2 + n_dh]
        rest = refs[3 + n_dh:]
        res_refs, rest = (rest[:3], rest[3:]) if has_res else ((), rest)
        out_refs = rest[:3] if has_res else rest[:1]
        i = pl.program_id(0)
        if not fused:
            elementwise(i, dh_refs[0][...], x_ref, g_ref, sc_ref, res_refs, out_refs)
            return
        acc = rest[-1]
        k = pl.program_id(1)
        part = lax.dot_general(dh_refs[0][...].astype(BF), dh_refs[1][...].astype(BF),
                               (((1,), (1 if tb else 0,)), ((), ())), preferred_element_type=F32)
        if nk == 1:
            elementwise(i, part, x_ref, g_ref, sc_ref, res_refs, out_refs)
            return

        @pl.when(k == 0)
        def _():
            acc[...] = part

        @pl.when(k > 0)
        def _():
            acc[...] += part

        @pl.when(k == nk - 1)
        def _():
            elementwise(i, acc[...], x_ref, g_ref, sc_ref, res_refs, out_refs)

    rowb = lambda w: pl.BlockSpec((1, w), lambda i, *k: (0, 0))
    blk = pl.BlockSpec((tm, D), lambda i, *k: (i, 0))
    st_spec = pl.BlockSpec((4, D), lambda i, *k: (0, 0))
    st_shape = jax.ShapeDtypeStruct((4, D), F32)
    if fused:
        if a_stack:
            nhb = K // 2 // tk
            a_spec = pl.BlockSpec((None, tm, tk), lambda i, k: (k // nhb, i, k % nhb))
        else:
            a_spec = pl.BlockSpec((tm, tk), lambda i, k: (i, k))
        b_spec = pl.BlockSpec((D, tk), lambda i, k: (0, k)) if tb else pl.BlockSpec((tk, D), lambda i, k: (k, 0))
        dh_specs = [a_spec, b_spec] + [pl.BlockSpec(memory_space=pl.ANY)] * len(deps)
        dh_args = [dh["a"], dh["b"]] + deps
        grid, sem = (R // tm, nk), ("arbitrary", "arbitrary")
        scratch = [pltpu.VMEM((tm, D) if nk > 1 else (8, 128), F32)]
    else:
        dh_specs, dh_args, grid, sem, scratch = [blk], [dh], (R // tm,), ("arbitrary",), []
    cp = pltpu.CompilerParams(dimension_semantics=sem)
    if has_res:
        return pl.pallas_call(
            body, name=name, grid=grid, in_specs=[blk] + dh_specs + [rowb(D), rowb(D), blk, blk, rowb(D)],
            out_specs=[blk, blk, st_spec], scratch_shapes=scratch,
            out_shape=[jax.ShapeDtypeStruct((R, D), F32), jax.ShapeDtypeStruct((R, D), BF), st_shape],
            compiler_params=cp,
        )(x, *dh_args, g, sc, dres, gsrc, gate)
    return pl.pallas_call(
        body, name=name, grid=grid, in_specs=[blk] + dh_specs + [rowb(D), rowb(D)],
        out_specs=st_spec, out_shape=st_shape, scratch_shapes=scratch, compiler_params=cp,
    )(x, *dh_args, g, sc)


def ffn_act_bwd(u0, df, cw, cb, *, name, tc=128):
    nb = DFF // tc

    def body(u_ref, df_ref, wg_ref, wv_ref, bg_ref, bv_ref, du_ref, dw_ref, db_ref):
        wg = [wg_ref[k:k + 1, :] for k in range(3)]
        wv = [wv_ref[k:k + 1, :] for k in range(3)]
        bg, bv = bg_ref[...], bv_ref[...]

        def chunk(r0, first, last, acc):
            xg, xv = _ext_rows(u_ref.at[0], r0, first, last), _ext_rows(u_ref.at[1], r0, first, last)
            dfe = _ext_rows(df_ref, r0, first, last)
            xg_d, xg_u, xv_d, xv_u = _roll_dn(xg), _roll_up(xg), _roll_dn(xv), _roll_up(xv)
            ug = bg + xg_d * wg[0] + xg * wg[1] + xg_u * wg[2]
            uv = bv + xv_d * wv[0] + xv * wv[1] + xv_u * wv[2]
            sig = jax.nn.sigmoid(ug)
            dug = dfe * uv * (sig * (1.0 + ug * (1.0 - sig)))
            duv = dfe * (ug * sig)
            rows = _center_rows(r0, first, last)
            du_ref[0, rows, :] = (_roll_up(dug) * wg[0] + dug * wg[1] + _roll_dn(dug) * wg[2])[_CTR].astype(BF)
            du_ref[1, rows, :] = (_roll_up(duv) * wv[0] + duv * wv[1] + _roll_dn(duv) * wv[2])[_CTR].astype(BF)
            terms = [dug * xg_d, dug * xg, dug * xg_u, dug, duv * xv_d, duv * xv, duv * xv_u, duv]
            return tuple(a + jnp.sum(t[_CTR], axis=0, keepdims=True) for a, t in zip(acc, terms))

        acc = _row_chunks(T, chunk, tuple(jnp.zeros((1, tc), F32) for _ in range(8)))
        for k in range(3):
            dw_ref[0, k:k + 1, :] = acc[k]
            dw_ref[1, k:k + 1, :] = acc[4 + k]
        db_ref[0] = acc[3]
        db_ref[1] = acc[7]

    lo = lambda r: pl.BlockSpec((r, tc), lambda j: (0, j))
    hi = lambda r: pl.BlockSpec((r, tc), lambda j: (0, nb + j))
    st = lambda r: pl.BlockSpec((2, r, tc), lambda j: (0, 0, j))
    return pl.pallas_call(
        body, name=name, grid=(nb,),
        in_specs=[st(T), lo(T), lo(3), hi(3), lo(1), hi(1)],
        out_specs=[st(T), st(3), st(1)],
        out_shape=[jax.ShapeDtypeStruct((2, T, DFF), BF), jax.ShapeDtypeStruct((2, 3, DFF), F32),
                   jax.ShapeDtypeStruct((2, 1, DFF), F32)],
        compiler_params=pltpu.CompilerParams(dimension_semantics=("parallel",)),
    )(u0, df, cw, cw, cb, cb)


def convz_bwd(p, dz, cw, cb, dp, *, name):
    o0 = O_CV // (3 * CVB)

    def body(p_ref, dz_ref, w_ref, bias_ref, dp_in, dp_ref, dw_ref, dbias_ref):
        xv, bv, cv = p_ref[:, 0:CVB], p_ref[:, CVB:2 * CVB], p_ref[:, 2 * CVB:3 * CVB]
        ci = cv * xv
        dwc = _conv(ci, w_ref, bias_ref)
        dzv = dz_ref[...]
        ddw = dzv * bv
        dci = _conv_t(ddw, w_ref)
        dp_ref[:, 0:CVB] = (dci * cv).astype(BF)
        dp_ref[:, CVB:2 * CVB] = (dzv * dwc).astype(BF)
        dp_ref[:, 2 * CVB:3 * CVB] = (dci * xv).astype(BF)
        _conv_wgrad(dw_ref, ddw, ci)
        dbias_ref[...] = jnp.sum(ddw, axis=0, keepdims=True)

    own = lambda r: pl.BlockSpec((r, CVB), lambda j: (0, j))
    return pl.pallas_call(
        body, name=name, grid=(CONV // CVB,),
        in_specs=[pl.BlockSpec((T, 3 * CVB), lambda j: (0, o0 + j)), own(T), own(3), own(1),
                  pl.BlockSpec(memory_space=pl.ANY)],
        out_specs=[pl.BlockSpec((T, 3 * CVB), lambda j: (0, o0 + j)), own(3), own(1)],
        out_shape=[jax.ShapeDtypeStruct((T, NIN), BF), jax.ShapeDtypeStruct((3, CONV), F32),
                   jax.ShapeDtypeStruct((1, CONV), F32)],
        input_output_aliases={4: 0},
        compiler_params=pltpu.CompilerParams(dimension_semantics=("parallel",)),
    )(p, dz, cw, cb, dp)


def attn_bwd(q, k, v, do, o, lse, dep, *, name, tq=1024, kc=768):
    NKC, KC = TKV // kc, kc
    deps = [] if dep is None else [dep]

    def body(q_ref, k_ref, v_ref, do_ref, o_ref, lse_ref, *rest):
        dq_ref, dk_ref, dv_ref = rest[len(deps):]
        h, i = pl.program_id(0), pl.program_id(1)

        @pl.when(i == 0)
        def _():
            dk_ref[...] = jnp.zeros_like(dk_ref)

        @pl.when((i == 0) & (h % 2 == 0))
        def _():
            dv_ref[...] = jnp.zeros_like(dv_ref)

        qv = q_ref[...]
        dom = jnp.where(_head_mask(h), do_ref[...], jnp.zeros_like(do_ref[...]))
        delta = jnp.sum(dom.astype(F32) * o_ref[...].astype(F32), axis=-1, keepdims=True)
        lse = lse_ref[:, 0:1]
        dq = jnp.zeros((tq, HP), F32)
        for c in range(NKC):
            cols = slice(c * KC, (c + 1) * KC)
            s = lax.dot_general(qv, k_ref[cols, :], (((1,), (1,)), ((), ())),
                                preferred_element_type=F32) * (SCALE * LOG2E)
            pr = jnp.exp2(s - lse)
            dp = lax.dot_general(dom, v_ref[cols, :], (((1,), (1,)), ((), ())), preferred_element_type=F32)
            ds = (pr * (dp - delta) * SCALE).astype(BF)
            dq = dq + jnp.dot(ds, k_ref[cols, :], preferred_element_type=F32)
            dk_ref[cols, :] += lax.dot_general(ds, qv, (((0,), (0,)), ((), ())), preferred_element_type=F32)
            dv_ref[cols, :] += lax.dot_general(pr.astype(BF), dom, (((0,), (0,)), ((), ())), preferred_element_type=F32)
        dq_ref[...] = dq

    return pl.pallas_call(
        body, name=name, grid=(NH, T // tq),
        in_specs=[pl.BlockSpec((tq, HP), lambda h, i: (i, h)), pl.BlockSpec((TKV, HP), lambda h, i: (0, h)),
                  pl.BlockSpec((TKV, 2 * DV), lambda h, i: (0, h // 2)), pl.BlockSpec((tq, 2 * DV), lambda h, i: (i, h // 2)),
                  pl.BlockSpec((tq, 2 * DV), lambda h, i: (i, h // 2)), pl.BlockSpec((tq, HP), lambda h, i: (i, h)),
                  *([pl.BlockSpec(memory_space=pl.ANY)] * len(deps))],
        out_specs=[pl.BlockSpec((tq, HP), lambda h, i: (i, h)), pl.BlockSpec((TKV, HP), lambda h, i: (0, h)),
                   pl.BlockSpec((TKV, 2 * DV), lambda h, i: (0, h // 2))],
        out_shape=[jax.ShapeDtypeStruct((T, NH * HP), F32), jax.ShapeDtypeStruct((TKV, NH * HP), F32),
                   jax.ShapeDtypeStruct((TKV, NH * DV), F32)],
        compiler_params=pltpu.CompilerParams(dimension_semantics=("arbitrary", "arbitrary")),
    )(q, k, v, do, o, lse, *deps)


def qprep_bwd(p, dq, qg, wq2, cq_t, sq_t, dp, *, name, tm=256):
    qcol = O_Q // 512

    def body(p_ref, dq_ref, g_ref, w_ref, c_ref, s_ref, dp_in, dp_ref, dq2_ref, dg_ref):
        i = pl.program_id(0)
        dqv = dq_ref[...]
        cc = jnp.concatenate([c_ref[...]] * NH, axis=1)
        ss = jnp.concatenate([s_ref[...]] * NH, axis=1)
        dq2 = jnp.concatenate([dqv * cc, dqv * ss], axis=1).astype(BF)
        dq2_ref[...] = dq2
        dcq = lax.dot_general(dq2, w_ref[...], (((1,), (1,)), ((), ())), preferred_element_type=F32)
        pq = p_ref[...]
        r = lax.rsqrt(jnp.sum(pq * pq, axis=-1, keepdims=True) * (1.0 / QL) + EPS)
        xh = pq * r
        a = dcq * g_ref[...]
        dp_ref[...] = (r * (a - xh * (jnp.sum(a * xh, axis=-1, keepdims=True) * (1.0 / QL)))).astype(BF)
        dg = jnp.sum(dcq * xh, axis=0, keepdims=True)

        @pl.when(i == 0)
        def _():
            dg_ref[...] = dg

        @pl.when(i > 0)
        def _():
            dg_ref[...] += dg

    return pl.pallas_call(
        body, name=name, grid=(T // tm,),
        in_specs=[pl.BlockSpec((tm, 512), lambda i: (i, qcol)), pl.BlockSpec((tm, NH * HP), lambda i: (i, 0)), _row(512),
                  pl.BlockSpec((512, 2 * NH * HP), lambda i: (0, 0)),
                  pl.BlockSpec((tm, HP), lambda i: (i, 0)), pl.BlockSpec((tm, HP), lambda i: (i, 0)),
                  pl.BlockSpec(memory_space=pl.ANY)],
        out_specs=[pl.BlockSpec((tm, 512), lambda i: (i, qcol)), pl.BlockSpec((tm, 2 * NH * HP), lambda i: (i, 0)), _row(512)],
        out_shape=[jax.ShapeDtypeStruct((T, NIN), BF), jax.ShapeDtypeStruct((T, 2 * NH * HP), BF),
                   jax.ShapeDtypeStruct((1, 512), F32)],
        input_output_aliases={6: 0},
        compiler_params=pltpu.CompilerParams(dimension_semantics=("arbitrary",)),
    )(p, dq, qg, wq2, cq_t, sq_t, dp)


def kvprep_bwd(pc, p, dk, dv, kvg, wkv2, ck, sk, dp, *, name, tm=256):
    assert tm == TC
    nb = TKV // tm
    kvcol = O_KV // 512

    def body(pc_ref, p_ref, dk_ref, dv_ref, g_ref, w_ref, ck_ref, sk_ref, dp_in, dp_ref, dpc_ref, dkv2_ref, dg_ref):
        i = pl.program_id(0)
        t = jnp.where(i == NLAT, pc_ref[...], p_ref[...])
        pk = t[:, :KVL]
        r = lax.rsqrt(jnp.mean(pk * pk, axis=-1, keepdims=True) + EPS)
        xh = pk * r
        dkv = dk_ref[...]
        dkv2 = jnp.concatenate([dkv, dv_ref[...]], axis=1).astype(BF)
        dkv2_ref[...] = dkv2
        dckv = lax.dot_general(dkv2, w_ref[...], (((1,), (1,)), ((), ())), preferred_element_type=F32)
        a = dckv * g_ref[...]
        dpk = r * (a - xh * jnp.mean(a * xh, axis=-1, keepdims=True))
        dkr = dkv[:, 0:HP]
        for hh in range(1, NH):
            dkr = dkr + dkv[:, hh * HP:(hh + 1) * HP]
        res = jnp.concatenate([dpk, dkr * ck_ref[...], dkr * sk_ref[...]], axis=1).astype(BF)
        dg = jnp.sum(dckv * xh, axis=0, keepdims=True)

        @pl.when(i == 0)
        def _():
            dg_ref[...] = dg

        @pl.when(i > 0)
        def _():
            dg_ref[...] += dg

        @pl.when(i < NLAT)
        def _():
            dp_ref[...] = res

        @pl.when(i == NLAT)
        def _():
            dpc_ref[...] = res

    rb = lambda w: pl.BlockSpec((tm, w), lambda i: (i, 0))
    return pl.pallas_call(
        body, name=name, grid=(nb,),
        in_specs=[pl.BlockSpec((tm, 512), lambda i: (0, 0)),
                  pl.BlockSpec((tm, 512), lambda i: (jnp.minimum(i, NLAT - 1), kvcol)),
                  rb(NH * HP), rb(NH * DV), _row(KVL), pl.BlockSpec((KVL, NH * HP + NH * DV), lambda i: (0, 0)),
                  rb(HP), rb(HP), pl.BlockSpec(memory_space=pl.ANY)],
        out_specs=[pl.BlockSpec((tm, 512), lambda i: (jnp.minimum(i, NLAT - 1), kvcol)),
                   pl.BlockSpec((tm, 512), lambda i: (0, 0)), rb(NH * HP + NH * DV), _row(KVL)],
        out_shape=[jax.ShapeDtypeStruct((T, NIN), BF), jax.ShapeDtypeStruct((TC, 512), BF),
                   jax.ShapeDtypeStruct((TKV, NH * HP + NH * DV), BF), jax.ShapeDtypeStruct((1, KVL), F32)],
        input_output_aliases={8: 0},
        compiler_params=pltpu.CompilerParams(dimension_semantics=("arbitrary",)),
    )(pc, p, dk, dv, kvg, wkv2, ck, sk, dp)


def _pieces(src, width, n):
    out, c = [], src
    while c < src + width:
        k = c // n
        w = min(src + width, (k + 1) * n) - c
        out.append((k, c - k * n, c - src, w))
        c += w
    return out


def _win_moves():
    mv = [(2208, 1024, O_GA), (3232, 1024, O_GC), (0, KVL, O_KV), (256, DR, O_KV + KVL + DN), (288, QL, O_Q)]
    mv += [(256 + _swap_start(g), 8, O_KV + KVL + HP + DN + 8 * g) for g in range(4)]
    for j in range(CONV // CVB):
        base = O_CV + 3 * CVB * j
        mv += [(672 + CVB * j, CVB, base), (1184 + CVB * j, CVB, base + CVB), (1696 + CVB * j, CVB, base + 2 * CVB)]
    return mv


_WIN_ZERO = [(O_KV + KVL, DN), (O_KV + KVL + DN + DR, HP - DN - DR), (O_KV + KVL + HP, DN),
             (O_KV + KVL + HP + DN + DR, HP - DN - DR), (O_Q + QL, 512 - QL)]


def build_win(g, *, name, tm=256):
    def body(g_ref, o_ref):
        for src, w, dst in _win_moves():
            for k, a, off, pw in _pieces(src, w, SH_IN):
                o_ref[:, dst + off:dst + off + pw] = g_ref[k, :, a:a + pw]
        for c0, w in _WIN_ZERO:
            o_ref[:, c0:c0 + w] = jnp.zeros((tm, w), o_ref.dtype)

    return pl.pallas_call(
        body, name=name, grid=(D // tm,), in_specs=[pl.BlockSpec((NDEV, tm, SH_IN), lambda i: (0, i, 0))],
        out_specs=pl.BlockSpec((tm, NIN), lambda i: (i, 0)), out_shape=jax.ShapeDtypeStruct((D, NIN), g.dtype),
        compiler_params=pltpu.CompilerParams(dimension_semantics=("parallel",)),
    )(g)


def shard_win_grad(dwt, dwct, *, name, tc=256):
    def body(dw_ref, dwc_ref, o_ref, kvs):
        kvs[...] = dw_ref[O_KV:O_KV + 512, :] + dwc_ref[...]

        def src(row, w):
            if O_KV <= row < O_KV + 512:
                return kvs[row - O_KV:row - O_KV + w, :]
            return dw_ref[row:row + w, :]

        for s, w, dst in _win_moves():
            if w == 8 or s == 256:
                continue
            for k, a, off, pw in _pieces(s, w, SH_IN):
                o_ref[k, a:a + pw, :] = src(dst + off, pw).astype(o_ref.dtype)
        for g in range(4):
            val = src(O_KV + KVL + DN + 8 * g, 8) + src(O_KV + KVL + HP + DN + _swap_start(g), 8)
            o_ref[0, 256 + 8 * g:256 + 8 * g + 8, :] = val.astype(o_ref.dtype)

    return pl.pallas_call(
        body, name=name, grid=(D // tc,),
        in_specs=[pl.BlockSpec((NIN, tc), lambda j: (0, j)), pl.BlockSpec((512, tc), lambda j: (0, j))],
        out_specs=pl.BlockSpec((NDEV, SH_IN, tc), lambda j: (0, 0, j)),
        out_shape=jax.ShapeDtypeStruct((NDEV, SH_IN, D), BF),
        scratch_shapes=[pltpu.VMEM((512, tc), F32)],
        compiler_params=pltpu.CompilerParams(dimension_semantics=("parallel",)),
    )(dwt, dwct)


def _eye(n, m):
    return (lax.broadcasted_iota(jnp.int32, (n, m), 0) == lax.broadcasted_iota(jnp.int32, (n, m), 1)).astype(BF)


_NT = (((1,), (1,)), ((), ()))


def build_wq_wkv(gq, gkv, *, name):
    def body(gq_ref, gkv_ref, q_ref, kv_ref):
        q_ref[...] = jnp.zeros_like(q_ref)
        kv_ref[...] = jnp.zeros_like(kv_ref)
        eye = _eye(QL, QL)
        for h in range(NH):
            qh = lax.dot_general(eye, gq_ref[h], _NT, preferred_element_type=F32).astype(q_ref.dtype)
            q_ref[0:QL, h * HP:h * HP + DN + DR] = qh
            for g in range(4):
                c0 = NH * HP + h * HP + DN + 8 * g
                q_ref[0:QL, c0:c0 + 8] = qh[:, DN + _swap_start(g):DN + _swap_start(g) + 8]
            kv_ref[:, h * HP:h * HP + DN] = gkv_ref[h, :, 0:DN]
            kv_ref[:, NH * HP + h * DV:NH * HP + (h + 1) * DV] = gkv_ref[h, :, DN:DN + DV]

    vm = pl.BlockSpec(memory_space=pltpu.VMEM)
    return pl.pallas_call(
        body, name=name, in_specs=[vm, vm], out_specs=[vm, vm],
        out_shape=[jax.ShapeDtypeStruct((512, 2 * NH * HP), gq.dtype), jax.ShapeDtypeStruct((KVL, NH * HP + NH * DV), gq.dtype)],
    )(gq, gkv)


def shard_wq_wkv_grad(dwq2, dwkv2, *, name):
    def body(q_ref, kv_ref, gq_ref, gkv_ref, xs):
        xs[...] = jnp.zeros_like(xs)
        eye = _eye(DN + DR, HP)
        for h in range(NH):
            xs[:, 0:DN] = q_ref[0:QL, h * HP:h * HP + DN].astype(BF)
            for g in range(4):
                a = q_ref[0:QL, h * HP + DN + 8 * g:h * HP + DN + 8 * g + 8]
                c0 = NH * HP + h * HP + DN + _swap_start(g)
                xs[:, DN + 8 * g:DN + 8 * g + 8] = (a + q_ref[0:QL, c0:c0 + 8]).astype(BF)
            gq_ref[h] = lax.dot_general(eye, xs[...], _NT, preferred_element_type=F32).astype(BF)
            gkv_ref[h, :, 0:DN] = kv_ref[:, h * HP:h * HP + DN].astype(BF)
            gkv_ref[h, :, DN:DN + DV] = kv_ref[:, NH * HP + h * DV:NH * HP + (h + 1) * DV].astype(BF)

    vm = pl.BlockSpec(memory_space=pltpu.VMEM)
    return pl.pallas_call(
        body, name=name, in_specs=[vm, vm], out_specs=[vm, vm],
        out_shape=[jax.ShapeDtypeStruct((NDEV, DN + DR, QL), BF), jax.ShapeDtypeStruct((NDEV, KVL, DN + DV), BF)],
        scratch_shapes=[pltpu.VMEM((QL, HP), BF)],
    )(dwq2, dwkv2)


def unshard_cols(g, *, name, tm=256):
    _, K, n = g.shape
    tm = _pick(K, tm, 16)

    def body(g_ref, o_ref):
        for k in range(NDEV):
            o_ref[:, k * n:(k + 1) * n] = g_ref[k]

    return pl.pallas_call(
        body, name=name, grid=(K // tm,), in_specs=[pl.BlockSpec((NDEV, tm, n), lambda i: (0, i, 0))],
        out_specs=pl.BlockSpec((tm, NDEV * n), lambda i: (i, 0)), out_shape=jax.ShapeDtypeStruct((K, NDEV * n), g.dtype),
        compiler_params=pltpu.CompilerParams(dimension_semantics=("parallel",)),
    )(g)


def shard_cols(w, *, name, tm=256):
    K, n8 = w.shape
    n = n8 // NDEV
    tm = _pick(K, tm, 16)

    def body(w_ref, o_ref):
        for k in range(NDEV):
            o_ref[k] = w_ref[:, k * n:(k + 1) * n]

    return pl.pallas_call(
        body, name=name, grid=(K // tm,), in_specs=[pl.BlockSpec((tm, n8), lambda i: (i, 0))],
        out_specs=pl.BlockSpec((NDEV, tm, n), lambda i: (0, i, 0)), out_shape=jax.ShapeDtypeStruct((NDEV, K, n), w.dtype),
        compiler_params=pltpu.CompilerParams(dimension_semantics=("parallel",)),
    )(w)


def _rope_tables():
    t = np.arange(T)
    row = (t // GRID_W).astype(np.float32)
    col = (t % GRID_W).astype(np.float32)
    axis_dim = DR // 2
    inv = (np.float32(ROPE_THETA) ** (-np.arange(0, axis_dim, 2, dtype=np.float32) / np.float32(axis_dim))).astype(np.float32)
    ar, ac = (row[:, None] * inv).astype(np.float32), (col[:, None] * inv).astype(np.float32)
    cosv = np.concatenate([np.cos(ar), np.cos(ar), np.cos(ac), np.cos(ac)], axis=1).astype(np.float32)
    sinv = np.concatenate([-np.sin(ar), np.sin(ar), -np.sin(ac), np.sin(ac)], axis=1).astype(np.float32)
    ck = np.zeros((TKV, HP), np.float32)
    sk = np.zeros((TKV, HP), np.float32)
    ck[T:, DN:DN + DR] = 1.0
    ck[:T, DN:DN + DR] = cosv
    sk[:T, DN:DN + DR] = sinv
    cq = np.zeros((T, HP), np.float32)
    cq[:, :DN] = 1.0
    cq[:, DN:DN + DR] = cosv
    return jnp.asarray(ck), jnp.asarray(sk), jnp.asarray(cq), jnp.asarray(sk[:T])


def _local_step(x, ctx, tgt, mod_lat, mod_ctx, n1g, qg, kvg, n2g, fg, conv_w, conv_b, ffn_w, ffn_b, get_w, put_g, dep0):
    sh1, sc1, g1, sh2, sc2, g2 = [mod_lat[:, i * D:(i + 1) * D] for i in range(6)]
    csh1, csc1 = mod_ctx[:, 0:D], mod_ctx[:, D:2 * D]
    ck, sk, cq_t, sq_t = _rope_tables()
    qg_p = jnp.pad(qg, ((0, 0), (0, 512 - QL)))

    hcat = normmod_cat(ctx, x, n1g, csc1, csh1, sc1, sh1, dep0, name="normmod1")
    win = get_w("in", hcat)
    p = mm(hcat, win, M=T, tn=768, name="in_proj")
    pc = mm(hcat, win, M=TC, N=512, a_off=(T, 0), b_off=(0, O_KV), name="in_proj_ctx")
    wq2, wkv2, wao, wco, wo = get_w("mid", p)
    kh, vh, ckv = kvprep(pc, p, kvg, wkv2, ck, sk, name="kvprep")
    qr, cq = qprep(p, qg_p, wq2, cq_t, sq_t, name="qprep")
    o, lse = attn_fwd(qr, kh, vh, name="attn_fwd")
    z = convz(p, conv_w, conv_b, name="convz")
    ya, yc, merged = out_proj_merge(o, wao, z, wco, p, name="attn_conv_out_gate_merge")
    a_out, x1, h2 = oproj_resid(merged, wo, x, g1, n2g, sc2, sh2, name="o_proj_resid_normmod2")
    wup = get_w("up", h2)
    u0 = mm(h2, wup, tb=True, o_stack=True, tn=1408, name="up_proj")
    f = ffn_act(u0, ffn_w, ffn_b, name="ffn_act")
    wdn = get_w("down", f)
    dn, dx2, dd, dfg, loss = down_final(f, wdn, x1, g2, fg, tgt, name="down_proj_final_loss")

    df = mm(dd, wdn, tb=True, tn=1408, name="down_proj_dx")
    dwdn = mm(f, dd, ta=True, out_dtype=BF, tm=1408, name="down_proj_dw")
    du0, dffn_w, dffn_b = ffn_act_bwd(u0, df, ffn_w, ffn_b, name="ffn_act_bwd")
    dwup = mm(du0, h2, ta=True, a_stack=True, out_dtype=BF, tm=1408, name="up_proj_dw")
    tok = put_g("ffn", dict(dwup=dwup, dwdn=dwdn))
    dx1, da, st2 = normmod_bwd(x1, dict(a=du0, b=wup, a_stack=True, tk=DFF, dep=tok), n2g, sc2, dx2, dn, g1,
                               name="up_proj_dx_normmod2_bwd")

    dwo = mm(merged, da, ta=True, out_dtype=BF, tn=512, name="o_proj_dw")
    dya, dyc, dp, do, dz = oproj_dx_gate_bwd(da, wo, p, ya, yc, wao, wco, name="o_proj_dx_gate_merge_bwd")
    dwao = mm(o, dya, ta=True, out_dtype=BF, tn=512, name="attn_out_dw")
    dwco = mm(z, dyc, ta=True, out_dtype=BF, tn=512, name="conv_out_dw")
    tok = put_g("mid", dict(dwao=dwao, dwco=dwco, dwo=dwo))
    dp, dconv_w, dconv_b = convz_bwd(p, dz, conv_w, conv_b, dp, name="convz_bwd")
    dq, dk, dv = attn_bwd(qr, kh, vh, do, o, lse, tok, name="attn_bwd")
    dp, dq2, dqg = qprep_bwd(p, dq, qg_p, wq2, cq_t, sq_t, dp, name="qprep_bwd")
    dp, dpc, dkv2, dkvg = kvprep_bwd(pc, p, dk, dv, kvg, wkv2, ck, sk, dp, name="kvprep_bwd")

    dwin = mm(dp, hcat, ta=True, K=T, tm=768, name="in_proj_dw")
    dwin_c = mm(dpc, hcat, ta=True, K=TC, b_off=(T, 0), name="in_proj_ctx_dw")
    tok = put_g("in", dict(dwin=dwin, dwin_c=dwin_c))
    dwq2 = mm(cq, dq2, ta=True, dep=tok, name="q_up_dw")
    dwkv2 = mm(ckv, dkv2, ta=True, name="kv_up_dw")
    tok = put_g("qkv", dict(dwq2=dwq2, dwkv2=dwkv2))
    dhc = mm(dpc, win, tb=True, N=D, K=512, b_off=(0, O_KV), name="in_proj_ctx_dx")
    dx, _, st1 = normmod_bwd(x, dict(a=dp, b=win, tb=True, tk=NIN, dep=tok), n1g, sc1, dx1, a_out, g1,
                             name="in_proj_dx_normmod1_bwd")
    stc = normmod_bwd(ctx, dhc, n1g, csc1, None, None, None, name="normmod1_ctx_bwd")

    zrow = jnp.zeros((1, D), F32)
    dmod_lat = jnp.concatenate([st1[0:1], st1[1:2], st1[3:4], st2[0:1], st2[1:2], st2[3:4]], axis=1)
    dmod_ctx = jnp.concatenate([stc[0:1], stc[1:2], zrow, zrow, zrow, zrow], axis=1)
    return dict(
        loss=loss, dx=dx, dmod_lat=dmod_lat, dmod_ctx=dmod_ctx,
        dn1g=st1[2:3] + stc[2:3], dqg=dqg, dkvg=dkvg, dn2g=st2[2:3], dfg=dfg,
        dconv_w=dconv_w, dconv_b=dconv_b, dffn_w=dffn_w, dffn_b=dffn_b)


def _me():
    x, y, c = lax.axis_index("x"), lax.axis_index("y"), lax.axis_index("c")
    return x, y, c, 4 * x + 2 * y + c


def _peer(x, y, c, k):
    px = 1 - x if k & 4 else x
    py = 1 - y if k & 2 else y
    pc = 1 - c if k & 1 else c
    return (px, py, pc), 4 * px + 2 * py + pc


def _exchange_tiles(src_of_peer, buf, send_sem, recv_sem):
    x, y, c, me = _me()
    for k in range(1, NDEV):
        dev, lin = _peer(x, y, c, k)
        pltpu.make_async_remote_copy(src_ref=src_of_peer(lin), dst_ref=buf.at[me], send_sem=send_sem, recv_sem=recv_sem,
                                     device_id=dev, device_id_type=MESH).start()
    seven = buf.at[pl.ds(0, NDEV - 1)]
    pltpu.make_async_remote_copy(src_ref=seven, dst_ref=seven, send_sem=send_sem, recv_sem=recv_sem,
                                 device_id=(x, y, c), device_id_type=MESH).wait()


def _silu(z):
    return z * jax.nn.sigmoid(z)


def ada_fwd(c, c_ctx, ffn_w, conv_w, w_shard, b_shard, deps, *, name):
    nsh = w_shard.shape[1]
    deps = [d for d in deps if d is not None]

    def body(c_ref, cc_ref, fw_ref, cw_ref, w_ref, b_ref, *rest):
        s_ref, m_ref, mine, res, sems = rest[len(deps):]
        x, y, c, me = _me()
        mine[0:1, :] = _silu(c_ref[...])
        mine[1:2, :] = _silu(cc_ref[...])
        mine[2:5, :] = fw_ref[...]
        mine[5:8, :] = cw_ref[...]
        s_ref[me] = mine[...]
        _exchange_tiles(lambda lin: mine, s_ref, sems.at[0], sems.at[1])
        sall = s_ref[...].reshape(NDEV * 8, D).astype(BF)
        r = jnp.dot(sall, w_ref[...].astype(BF), preferred_element_type=F32) + b_ref[...]
        res[...] = r.reshape(NDEV, 8, nsh)
        m_ref[me] = res[me]
        _exchange_tiles(lambda lin: res.at[lin], m_ref, sems.at[2], sems.at[3])

    vm = pl.BlockSpec(memory_space=pltpu.VMEM)
    return pl.pallas_call(
        body, name=name, in_specs=[vm] * 6 + [pl.BlockSpec(memory_space=pl.ANY)] * len(deps), out_specs=[vm, vm],
        out_shape=[jax.ShapeDtypeStruct((NDEV, 8, D), F32), jax.ShapeDtypeStruct((NDEV, 8, nsh), F32)],
        scratch_shapes=[pltpu.VMEM((8, D), F32), pltpu.VMEM((NDEV, 8, nsh), F32), pltpu.SemaphoreType.DMA((4,))],
    )(c, c_ctx, ffn_w, conv_w, w_shard, b_shard, *deps)


P_DML, P_DMC, P_N1, P_QG, P_KVG, P_CB, P_N2, P_FB, P_FG, P_CW, P_FW, P_LOSS, P_ROWS = 0, 6, 12, 13, 14, 15, 16, 17, 23, 24, 27, 45, 48
FROWS = 3


def sync_small(r, deps, *, name):
    ins = [r["dmod_lat"], r["dmod_ctx"], r["dn1g"], r["dqg"], r["dkvg"], r["dconv_b"], r["dn2g"], r["dffn_b"], r["dfg"],
           r["dconv_w"], r["dffn_w"], r["loss"]]

    def put_wide(p, row0, row, n):
        for j in range(-(-n // D)):
            w = min(D, n - j * D)
            p[row0 + j:row0 + j + 1, 0:w] = row[:, j * D:j * D + w]

    def body(dml, dmc, n1, qg, kvg, cb, n2, fb, fg, cw, fw, loss, *rest):
        a_ref, sum_ref, p, sems = rest[len(deps):]
        x, y, c, me = _me()
        p[...] = jnp.zeros_like(p)
        put_wide(p, P_DML, dml, 6 * D)
        put_wide(p, P_DMC, dmc, 6 * D)
        put_wide(p, P_N1, n1, D)
        put_wide(p, P_QG, qg, 512)
        put_wide(p, P_KVG, kvg, KVL)
        put_wide(p, P_CB, cb, CONV)
        put_wide(p, P_N2, n2, D)
        put_wide(p, P_FG, fg, D)
        put_wide(p, P_LOSS, loss, 128)
        for s in range(2):
            put_wide(p, P_FB + FROWS * s, fb.at[s], DFF)
        for k in range(3):
            put_wide(p, P_CW + k, cw.at[k:k + 1], CONV)
            for s in range(2):
                put_wide(p, P_FW + FROWS * (2 * k + s), fw.at[s, k:k + 1], DFF)
        a_ref[me] = p[...]
        _exchange_tiles(lambda lin: p, a_ref, sems.at[0], sems.at[1])
        acc = a_ref[0]
        for k in range(1, NDEV):
            acc = acc + a_ref[k]
        sum_ref[...] = acc

    vm = pl.BlockSpec(memory_space=pltpu.VMEM)
    return pl.pallas_call(
        body, name=name, in_specs=[vm] * len(ins) + [pl.BlockSpec(memory_space=pl.ANY)] * len(deps), out_specs=[vm, vm],
        out_shape=[jax.ShapeDtypeStruct((NDEV, P_ROWS, D), F32), jax.ShapeDtypeStruct((P_ROWS, D), F32)],
        scratch_shapes=[pltpu.VMEM((P_ROWS, D), F32), pltpu.SemaphoreType.DMA((2,))],
    )(*ins, *deps)


def ada_bwd(s_all, dml, dmc, w_shard, c_ctx, *, name):
    nsh = w_shard.shape[1]

    def body(s_ref, dml_ref, dmc_ref, w_ref, c_ref, dw_ref, gc_ref, s16, dm16, part, buf, sems):
        x, y, c, me = _me()
        s16[...] = jnp.zeros_like(s16)
        dm16[...] = jnp.zeros_like(dm16)
        for k in range(NDEV):
            s16[k:k + 1, :] = s_ref[k, 0:1, :]
        s16[8:9, :] = s_ref[0, 1:2, :]
        dm16[0:8, :] = dml_ref[...]
        dm16[8:9, :] = dmc_ref[...]
        dw_ref[...] = lax.dot_general(s16[...].astype(BF), dm16[...].astype(BF), (((0,), (0,)), ((), ())),
                                      preferred_element_type=F32)
        part[...] = lax.dot_general(dm16[8:16, :].astype(BF), w_ref[...].astype(BF), (((1,), (1,)), ((), ())),
                                    preferred_element_type=F32)
        buf[me] = part[...]
        _exchange_tiles(lambda lin: part, buf, sems.at[0], sems.at[1])
        acc = buf[0]
        for k in range(1, NDEV):
            acc = acc + buf[k]
        z = c_ref[...]
        sg = jax.nn.sigmoid(z)
        gc_ref[...] = acc * (sg * (1.0 + z * (1.0 - sg)))

    vm = pl.BlockSpec(memory_space=pltpu.VMEM)
    return pl.pallas_call(
        body, name=name, in_specs=[vm] * 5, out_specs=[vm, vm],
        out_shape=[jax.ShapeDtypeStruct((D, nsh), F32), jax.ShapeDtypeStruct((8, D), F32)],
        scratch_shapes=[pltpu.VMEM((16, D), F32), pltpu.VMEM((16, nsh), F32), pltpu.VMEM((8, D), F32),
                        pltpu.VMEM((NDEV, 8, D), F32), pltpu.SemaphoreType.DMA((2,))],
    )(s_all, dml, dmc, w_shard, c_ctx)


HBM_SPEC = pl.BlockSpec(memory_space=pltpu.HBM)
SEM_SPEC = pl.BlockSpec(memory_space=pltpu.SEMAPHORE)
EFFECT = pltpu.SideEffectType.DATAFLOW_SIDE_EFFECTING


ALL_PEERS = tuple(range(1, NDEV))
FIRST_HOP = (1, 2, 4, 6)
RELAY = (2, 4, 6)


def _exchange_copies(srcs, lands, send, recv, per_peer, peers):
    x, y, c, me = _me()
    n = len(peers)
    cps = []
    for t in range(len(srcs)):
        for j, k in enumerate(peers):
            dev, lin = _peer(x, y, c, k)
            cps.append(pltpu.make_async_remote_copy(
                src_ref=srcs[t].at[lin] if per_peer else srcs[t], dst_ref=lands[t].at[me],
                send_sem=send.at[n * t + j], recv_sem=recv.at[n * t + j], device_id=dev, device_id_type=MESH))
    return cps


def _relay_copies(lands, send, recv):
    x, y, c, me = _me()
    n = len(RELAY)
    cps = []
    for t in range(len(lands)):
        for j, k in enumerate(RELAY):
            slot = lands[t].at[_peer(x, y, c, k)[1]]
            cps.append(pltpu.make_async_remote_copy(
                src_ref=slot, dst_ref=slot, send_sem=send.at[n * t + j], recv_sem=recv.at[n * t + j],
                device_id=(x, y, 1 - c), device_id_type=MESH))
    return cps


def _own_copies(srcs, lands, own, per_peer):
    me = _me()[3]
    return [pltpu.make_async_copy(srcs[t].at[me] if per_peer else srcs[t], lands[t].at[me], own.at[t])
            for t in range(len(srcs))]


def exchange_start(srcs, *, per_peer, name, dep=None, peers=ALL_PEERS):
    nt = len(srcs)
    ns = len(peers) * nt
    land_shapes = [(a.shape if per_peer else (NDEV,) + a.shape) for a in srcs]
    deps = [] if dep is None else [dep]

    def body(*refs):
        src, land = refs[:nt], refs[nt:2 * nt]
        send, recv, own = refs[2 * nt + len(deps):2 * nt + len(deps) + 3]
        for cp in _exchange_copies(src, land, send, recv, per_peer, peers) + _own_copies(src, land, own, per_peer):
            cp.start()
        refs[-1][...] = jnp.zeros_like(refs[-1])

    hb = lambda a: pltpu.with_memory_space_constraint(a, pltpu.HBM)
    outs = pl.pallas_call(
        body, name=name,
        out_shape=(pltpu.SemaphoreType.DMA((ns,)), pltpu.SemaphoreType.DMA((ns,)), pltpu.SemaphoreType.DMA((nt,)),
                   *[pltpu.HBM(a.shape, a.dtype) for a in srcs], *[pltpu.HBM(s, a.dtype) for s, a in zip(land_shapes, srcs)],
                   jax.ShapeDtypeStruct((8, 128), F32)),
        in_specs=[HBM_SPEC] * (2 * nt) + [pl.BlockSpec(memory_space=pl.ANY)] * len(deps),
        out_specs=(SEM_SPEC, SEM_SPEC, SEM_SPEC, *([HBM_SPEC] * (2 * nt)), pl.BlockSpec(memory_space=pltpu.VMEM)),
        input_output_aliases={i: 3 + i for i in range(2 * nt)},
        compiler_params=pltpu.CompilerParams(has_side_effects=EFFECT),
    )(*[hb(a) for a in srcs], *[hb(lax.empty(s, a.dtype)) for s, a in zip(land_shapes, srcs)], *deps)
    return dict(send=outs[0], recv=outs[1], own=outs[2], src=list(outs[3:3 + nt]), land=list(outs[3 + nt:3 + 2 * nt]),
                token=outs[-1], per_peer=per_peer, peers=peers)


def exchange_wait(h, after, *, name):
    nt = len(h["src"])
    per_peer, peers = h["per_peer"], h["peers"]

    def body(*refs):
        src, land, send, recv, own = refs[:nt], refs[nt:2 * nt], refs[2 * nt], refs[2 * nt + 1], refs[2 * nt + 2]
        for cp in _exchange_copies(src, land, send, recv, per_peer, peers):
            cp.wait_send()
            cp.wait_recv()
        for cp in _own_copies(src, land, own, per_peer):
            cp.wait()

    outs = pl.pallas_call(
        body, name=name,
        out_shape=(*[pltpu.HBM(a.shape, a.dtype) for a in h["src"]], *[pltpu.HBM(a.shape, a.dtype) for a in h["land"]]),
        in_specs=[HBM_SPEC] * (2 * nt) + [SEM_SPEC, SEM_SPEC, SEM_SPEC, pl.BlockSpec(memory_space=pl.ANY)],
        out_specs=tuple([HBM_SPEC] * (2 * nt)),
        input_output_aliases={i: i for i in range(2 * nt)},
        compiler_params=pltpu.CompilerParams(has_side_effects=EFFECT),
    )(*h["src"], *h["land"], h["send"], h["recv"], h["own"], after)
    return list(outs[nt:])


def relay_start(lands, *, name):
    nt = len(lands)
    ns = len(RELAY) * nt

    def body(*refs):
        for cp in _relay_copies(refs[:nt], refs[nt], refs[nt + 1]):
            cp.start()

    outs = pl.pallas_call(
        body, name=name,
        out_shape=(pltpu.SemaphoreType.DMA((ns,)), pltpu.SemaphoreType.DMA((ns,)),
                   *[pltpu.HBM(a.shape, a.dtype) for a in lands]),
        in_specs=[HBM_SPEC] * nt, out_specs=(SEM_SPEC, SEM_SPEC, *([HBM_SPEC] * nt)),
        input_output_aliases={i: 2 + i for i in range(nt)},
        compiler_params=pltpu.CompilerParams(has_side_effects=EFFECT),
    )(*lands)
    return dict(send=outs[0], recv=outs[1], land=list(outs[2:]))


def relay_wait(h, *, name):
    nt = len(h["land"])

    def body(*refs):
        for cp in _relay_copies(refs[:nt], refs[nt], refs[nt + 1]):
            cp.wait_send()
            cp.wait_recv()

    outs = pl.pallas_call(
        body, name=name, out_shape=tuple(pltpu.HBM(a.shape, a.dtype) for a in h["land"]),
        in_specs=[HBM_SPEC] * nt + [SEM_SPEC, SEM_SPEC], out_specs=tuple([HBM_SPEC] * nt),
        input_output_aliases={i: i for i in range(nt)},
        compiler_params=pltpu.CompilerParams(has_side_effects=EFFECT),
    )(*h["land"], h["send"], h["recv"])
    return list(outs)


def _adamw_math(w, g, m, v):
    nm = B1 * m + (1.0 - B1) * g
    nv = B2 * v + (1.0 - B2) * (g * g)
    m_hat = nm / (1.0 - B1 ** STEP)
    v_hat = nv / (1.0 - B2 ** STEP)
    return -LR * (m_hat / (jnp.sqrt(v_hat) + AEPS) + WD * w), nm, nv


def adamw_many(ws, gs, ms, vs, *, name):
    n = len(ws)

    def body(*refs):
        for k in range(n):
            d, nm, nv = _adamw_math(refs[k][...], refs[n + k][...], refs[2 * n + k][...], refs[3 * n + k][...])
            refs[4 * n + k][...] = d
            refs[5 * n + k][...] = nm
            refs[6 * n + k][...] = nv

    vm = pl.BlockSpec(memory_space=pltpu.VMEM)
    sh = [jax.ShapeDtypeStruct(w.shape, F32) for w in ws]
    outs = pl.pallas_call(body, name=name, in_specs=[vm] * (4 * n), out_specs=[vm] * (3 * n), out_shape=sh * 3,
                          )(*ws, *gs, *ms, *vs)
    return outs[:n], outs[n:2 * n], outs[2 * n:]


def adamw(w, g, m, v, *, name, tr=256):
    R, C = w.shape
    tr = _pick(R, tr, 8)

    def body(w_ref, g_ref, m_ref, v_ref, d_ref, nm_ref, nv_ref):
        d_ref[...], nm_ref[...], nv_ref[...] = _adamw_math(w_ref[...], g_ref[...], m_ref[...], v_ref[...])

    blk = pl.BlockSpec((tr, C), lambda i: (i, 0))
    sh = jax.ShapeDtypeStruct((R, C), F32)
    return pl.pallas_call(
        body, name=name, grid=(R // tr,), in_specs=[blk, blk, blk, blk], out_specs=[blk, blk, blk],
        out_shape=[sh, sh, sh], compiler_params=pltpu.CompilerParams(dimension_semantics=("parallel",)),
    )(w, g, m, v)


def adamw_slots(w, slots, m, v, *, name, tr=256):
    unit = w.ndim == 3
    R, C = w.shape[0], w.shape[-1]
    if R % 16 == 0:
        tr = _pick(R, tr, 16)
    else:
        tr = 144

    def body(w_ref, s_ref, m_ref, v_ref, g_ref, d_ref, nm_ref, nv_ref):
        g = s_ref[0].astype(F32)
        for k in range(1, NDEV):
            g = g + s_ref[k].astype(F32)
        g_ref[...] = g
        d_ref[...], nm_ref[...], nv_ref[...] = _adamw_math(w_ref[...], g, m_ref[...], v_ref[...])

    blk = pl.BlockSpec((tr, None, C), lambda i: (i, 0, 0)) if unit else pl.BlockSpec((tr, C), lambda i: (i, 0))
    sh = jax.ShapeDtypeStruct(w.shape, F32)
    return pl.pallas_call(
        body, name=name, grid=(pl.cdiv(R, tr),), in_specs=[blk, pl.BlockSpec((NDEV, tr, C), lambda i: (0, i, 0)), blk, blk],
        out_specs=[blk, blk, blk, blk], out_shape=[sh, sh, sh, sh],
        compiler_params=pltpu.CompilerParams(dimension_semantics=("parallel",)),
    )(w, slots, m, v)


def _padc(a, n=D):
    return jnp.pad(a, ((0, 0), (0, n - a.shape[1])))


def kernel(x, c, ctx, c_ctx, w_ada, b_ada, norm1_g, w_in, q_norm_g, kv_norm_g, w_uq, w_ukv, conv_w, conv_b, w_attn_out, w_conv_out, w_o, norm2_g, w_up, ffn_conv_w, ffn_conv_b, w_down, final_g, loss_target, m_c_ctx, m_w_ada, m_b_ada, m_norm1_g, m_w_in, m_q_norm_g, m_kv_norm_g, m_w_uq, m_w_ukv, m_conv_w, m_conv_b, m_w_attn_out, m_w_conv_out, m_w_o, m_norm2_g, m_w_up, m_ffn_conv_w, m_ffn_conv_b, m_w_down, m_final_g, v_c_ctx, v_w_ada, v_b_ada, v_norm1_g, v_w_in, v_q_norm_g, v_kv_norm_g, v_w_uq, v_w_ukv, v_conv_w, v_conv_b, v_w_attn_out, v_w_conv_out, v_w_o, v_norm2_g, v_w_up, v_ffn_conv_w, v_ffn_conv_b, v_w_down, v_final_g):
    me = 4 * lax.axis_index("x") + 2 * lax.axis_index("y") + lax.axis_index("c")
    W = dict(c_ctx=c_ctx, w_ada=w_ada, b_ada=b_ada, norm1_g=norm1_g, w_in=w_in, q_norm_g=q_norm_g, kv_norm_g=kv_norm_g,
             w_uq=w_uq, w_ukv=w_ukv, conv_w=conv_w, conv_b=conv_b, w_attn_out=w_attn_out, w_conv_out=w_conv_out, w_o=w_o,
             norm2_g=norm2_g, w_up=w_up, ffn_conv_w=ffn_conv_w, ffn_conv_b=ffn_conv_b, w_down=w_down, final_g=final_g)
    M = dict(c_ctx=m_c_ctx, w_ada=m_w_ada, b_ada=m_b_ada, norm1_g=m_norm1_g, w_in=m_w_in, q_norm_g=m_q_norm_g,
             kv_norm_g=m_kv_norm_g, w_uq=m_w_uq, w_ukv=m_w_ukv, conv_w=m_conv_w, conv_b=m_conv_b, w_attn_out=m_w_attn_out,
             w_conv_out=m_w_conv_out, w_o=m_w_o, norm2_g=m_norm2_g, w_up=m_w_up, ffn_conv_w=m_ffn_conv_w,
             ffn_conv_b=m_ffn_conv_b, w_down=m_w_down, final_g=m_final_g)
    V = dict(c_ctx=v_c_ctx, w_ada=v_w_ada, b_ada=v_b_ada, norm1_g=v_norm1_g, w_in=v_w_in, q_norm_g=v_q_norm_g,
             kv_norm_g=v_kv_norm_g, w_uq=v_w_uq, w_ukv=v_w_ukv, conv_w=v_conv_w, conv_b=v_conv_b, w_attn_out=v_w_attn_out,
             w_conv_out=v_w_conv_out, w_o=v_w_o, norm2_g=v_norm2_g, w_up=v_w_up, ffn_conv_w=v_ffn_conv_w,
             ffn_conv_b=v_ffn_conv_b, w_down=v_w_down, final_g=v_final_g)
    names = list(W)
    transposed = ("w_up", "w_uq")
    as2d = lambda k, a: (a.reshape(1, -1) if a.ndim == 1 else
                         a[0].T if k in transposed else a.reshape(a.shape[-2], a.shape[-1]))
    W2 = {k: as2d(k, a) for k, a in W.items()}
    M2 = {k: as2d(k, a) for k, a in M.items()}
    V2 = {k: as2d(k, a) for k, a in V.items()}
    unit3 = lambda a: jnp.transpose(a, (2, 0, 1))
    W3, M3, V3 = unit3(W["w_in"]), unit3(M["w_in"]), unit3(V["w_in"])
    nsh = W2["w_ada"].shape[1]

    b_sh = lax.dynamic_slice(W2["b_ada"], (0, me * nsh), (1, nsh))
    s_all, m_all = ada_fwd(c, W2["c_ctx"], _padc(W2["ffn_conv_w"]), _padc(W2["conv_w"]), W2["w_ada"], b_sh, [],
                           name="ada_fwd")
    mod_lat = m_all[:, 0, :].reshape(1, 6 * D)
    mod_ctx = m_all[:, 1, :].reshape(1, 6 * D)
    ffn_w_full = s_all[:, 2:5, :2 * DFF // NDEV].transpose(1, 0, 2).reshape(3, 2 * DFF)
    conv_w_full = s_all[:, 5:8, :CONV // NDEV].transpose(1, 0, 2).reshape(3, CONV)

    stage_w = {"in": ["w_in"], "mid": ["w_uq", "w_ukv", "w_attn_out", "w_conv_out", "w_o"], "up": ["w_up"],
               "down": ["w_down"]}
    two_level = ("in", "mid")
    ag, tok = {}, m_all
    for st, nms in stage_w.items():
        ag[st] = exchange_start([W2[nm].astype(BF) for nm in nms], per_peer=False, dep=tok, name="ag_start_" + st,
                                peers=FIRST_HOP if st in two_level else ALL_PEERS)
        tok = ag[st]["token"]

    def get_w(stage, after):
        lands = exchange_wait(ag[stage], after, name="ag_wait_" + stage)
        if stage in two_level:
            lands = relay_wait(relay_start(lands, name="ag_relay_" + stage), name="ag_relay_wait_" + stage)
        g = dict(zip(stage_w[stage], lands))
        if stage == "in":
            return build_win(g["w_in"], name="build_win")
        if stage == "mid":
            wq2, wkv2 = build_wq_wkv(g["w_uq"], g["w_ukv"], name="build_wq_wkv")
            return (wq2, wkv2, unshard_cols(g["w_attn_out"], name="unshard_w_attn_out"),
                    unshard_cols(g["w_conv_out"], name="unshard_w_conv_out"), g["w_o"].reshape(D, D))
        if stage == "up":
            return g["w_up"].reshape(2 * DFF, D)
        return g["w_down"].reshape(DFF, D)

    stage_g = {"ffn": ["w_up", "w_down"], "mid": ["w_attn_out", "w_conv_out", "w_o"], "qkv": ["w_uq", "w_ukv"],
               "in": ["w_in"]}
    rs = {}

    def put_g(stage, g):
        if stage == "in":
            parts = [shard_win_grad(g["dwin"], g["dwin_c"], name="shard_win_grad")]
        elif stage == "mid":
            parts = [shard_cols(g["dwao"], name="shard_w_attn_out"), shard_cols(g["dwco"], name="shard_w_conv_out"),
                     g["dwo"].reshape(NDEV, D // NDEV, D)]
        elif stage == "qkv":
            parts = list(shard_wq_wkv_grad(g["dwq2"], g["dwkv2"], name="shard_wq_wkv_grad"))
        else:
            parts = [g["dwup"].reshape(NDEV, 2 * DFF // NDEV, D), g["dwdn"].reshape(NDEV, DFF // NDEV, D)]
        rs[stage] = exchange_start(parts, per_peer=True, name="rs_start_" + stage)
        return rs[stage]["token"]

    r = _local_step(x[0], ctx[0], loss_target[0], mod_lat, mod_ctx, W2["norm1_g"], W2["q_norm_g"], W2["kv_norm_g"],
                    W2["norm2_g"], W2["final_g"], conv_w_full, W2["conv_b"], ffn_w_full, W2["ffn_conv_b"], get_w, put_g,
                    ag["down"]["token"])

    G, DL, NM, NV = {}, {}, {}, {}

    def finish(stage, after):
        for nm, sl in zip(stage_g[stage], exchange_wait(rs[stage], after, name="rs_wait_" + stage)):
            wmv = (W3, M3, V3) if nm == "w_in" else (W2[nm], M2[nm], V2[nm])
            G[nm], DL[nm], NM[nm], NV[nm] = adamw_slots(wmv[0], sl, wmv[1], wmv[2], name="adamw_" + nm)
            after = DL[nm]
        return after

    after = r["dx"]
    for st in ("ffn", "mid"):
        after = finish(st, after)

    a_buf, ssum = sync_small(r, [DL[nm] for st in ("ffn", "mid") for nm in stage_g[st]], name="sync_small")
    loss = ssum[P_LOSS, 0]
    G["norm1_g"] = ssum[P_N1:P_N1 + 1]
    G["q_norm_g"] = ssum[P_QG:P_QG + 1, :QL]
    G["kv_norm_g"] = ssum[P_KVG:P_KVG + 1, :KVL]
    G["conv_b"] = ssum[P_CB:P_CB + 1, :CONV]
    G["norm2_g"] = ssum[P_N2:P_N2 + 1]
    G["ffn_conv_b"] = ssum[P_FB:P_FB + 2 * FROWS].reshape(1, 2, FROWS * D)[:, :, :DFF].reshape(1, 2 * DFF)
    G["final_g"] = ssum[P_FG:P_FG + 1]
    G["conv_w"] = lax.dynamic_slice(ssum[P_CW:P_CW + 3, :CONV], (0, me * (CONV // NDEV)), (3, CONV // NDEV))
    fw_full = ssum[P_FW:P_FW + 6 * FROWS].reshape(3, 2, FROWS * D)[:, :, :DFF].reshape(3, 2 * DFF)
    G["ffn_conv_w"] = lax.dynamic_slice(fw_full, (0, me * (2 * DFF // NDEV)), (3, 2 * DFF // NDEV))
    G["b_ada"] = (ssum[P_DML:P_DML + 6] + ssum[P_DMC:P_DMC + 6]).reshape(1, 6 * D)

    dml = lax.dynamic_slice(a_buf[:, P_DML:P_DML + 6, :].reshape(NDEV, 6 * D), (0, me * nsh), (NDEV, nsh))
    dmc = lax.dynamic_slice(ssum[P_DMC:P_DMC + 6].reshape(1, 6 * D), (0, me * nsh), (1, nsh))
    G["w_ada"], gcc = ada_bwd(s_all, dml, dmc, W2["w_ada"], W2["c_ctx"], name="ada_bwd")
    G["c_ctx"] = gcc[0:1]

    DL["w_ada"], NM["w_ada"], NV["w_ada"] = adamw(W2["w_ada"], G["w_ada"], M2["w_ada"], V2["w_ada"], name="adamw_w_ada")
    small = ["c_ctx", "b_ada", "norm1_g", "q_norm_g", "kv_norm_g", "conv_b", "norm2_g", "ffn_conv_b", "final_g", "conv_w",
             "ffn_conv_w"]
    ds, nms, nvs = adamw_many([W2[k] for k in small], [G[k] for k in small], [M2[k] for k in small],
                              [V2[k] for k in small], name="adamw_small")
    for k, nm in enumerate(small):
        DL[nm], NM[nm], NV[nm] = ds[k], nms[k], nvs[k]
    finish("qkv", finish("in", ds[0]))

    outs = [loss, r["dx"][None]]
    for grp in (G, DL, NM, NV):
        outs += [grp[nm].T[None] if nm in transposed else
                 jnp.transpose(grp[nm], (1, 2, 0)) if nm == "w_in" else grp[nm].reshape(W[nm].shape) for nm in names]
    return tuple(outs)
```

```python
import functools
import numpy as np
import jax
import jax.numpy as jnp
from jax import lax
from jax.experimental import pallas as pl
from jax.experimental.pallas import tpu as pltpu

F32 = jnp.float32
BF = jnp.bfloat16
MESH = pl.DeviceIdType.MESH

D = 1024
T = 2048
TC = 256
TKV = T + TC
GRID_W = 64
NH = 8
DN = 64
DR = 32
DV = 64
QL = 384
KVL = 256
CONV = 512
DFF = 2816
EPS = 1e-6
ROPE_THETA = 10000.0
SCALE = (DN + DR) ** -0.5
NDEV = 8
HP = 128

O_GA, O_GC, O_KV, O_Q, O_CV = 0, 1024, 2048, 2560, 3072
NIN = 4608
CVB = 256
N_IN = 4256
SH_IN = N_IN // NDEV

LR, B1, B2, AEPS, WD, STEP = 0.001, 0.9, 0.999, 1e-08, 0.01, 10


def _pick(n, target, mult=128):
    best = None
    for d in range(mult, min(n, target) + 1, mult):
        if n % d == 0:
            best = d
    return best if best is not None else n


def _swap_start(g):
    return 8 * (g ^ 1)


def mm(a, b, *, ta=False, tb=False, out_dtype=F32, name, tm=1024, tn=1024, tk=2048, M=None, N=None, K=None,
       a_off=(0, 0), b_off=(0, 0), a_stack=False, b_stack=False, o_stack=False, dep=None):
    def dims(arr, stack):
        return (arr.shape[1], 2 * arr.shape[2]) if stack else arr.shape

    ar, ac = dims(a, a_stack)
    br, bc = dims(b, b_stack)
    M = M or ((ac if ta else ar) - a_off[1 if ta else 0])
    K = K or ((ar if ta else ac) - a_off[0 if ta else 1])
    N = N or ((br if tb else bc) - b_off[0 if tb else 1])
    tm = _pick(M, tm, 128 if ta else 16)
    tn = _pick(N // 2 if (o_stack or (b_stack and not tb)) else N, tn, 128)
    tk = _pick(K // 2 if ((a_stack and not ta) or (b_stack and tb)) else K, tk, 128)
    nk = K // tk
    ca = 0 if ta else 1
    cb = 1 if tb else 0

    def body(a_ref, b_ref, *rest):
        o_ref, acc = rest[-2:]
        k = pl.program_id(2)
        part = lax.dot_general(a_ref[...].astype(BF), b_ref[...].astype(BF),
                               (((ca,), (cb,)), ((), ())), preferred_element_type=F32)
        if nk == 1:
            o_ref[...] = part.astype(o_ref.dtype)
        else:
            @pl.when(k == 0)
            def _():
                acc[...] = part

            @pl.when(k > 0)
            def _():
                acc[...] += part

            @pl.when(k == nk - 1)
            def _():
                o_ref[...] = acc[...].astype(o_ref.dtype)

    def spec(blk, rc, off, stack, ncols):
        assert off[0] % blk[0] == 0 and off[1] % blk[1] == 0, (name, blk, off)
        ro, co = off[0] // blk[0], off[1] // blk[1]
        if not stack:
            return pl.BlockSpec(blk, lambda i, j, k: (rc(i, j, k)[0] + ro, rc(i, j, k)[1] + co))
        nhb = ncols // 2 // blk[1]
        return pl.BlockSpec((None,) + blk,
                            lambda i, j, k: ((rc(i, j, k)[1] + co) // nhb, rc(i, j, k)[0] + ro, (rc(i, j, k)[1] + co) % nhb))

    a_spec = spec((tk, tm), lambda i, j, k: (k, i), a_off, a_stack, ac) if ta else \
        spec((tm, tk), lambda i, j, k: (i, k), a_off, a_stack, ac)
    b_spec = spec((tn, tk), lambda i, j, k: (j, k), b_off, b_stack, bc) if tb else \
        spec((tk, tn), lambda i, j, k: (k, j), b_off, b_stack, bc)
    o_spec = spec((tm, tn), lambda i, j, k: (i, j), (0, 0), o_stack, N)
    o_shape = (2, M, N // 2) if o_stack else (M, N)
    deps = [] if dep is None else [dep]
    return pl.pallas_call(
        body, name=name, grid=(M // tm, N // tn, nk),
        in_specs=[a_spec, b_spec] + [pl.BlockSpec(memory_space=pl.ANY)] * len(deps),
        out_specs=o_spec, out_shape=jax.ShapeDtypeStruct(o_shape, out_dtype),
        scratch_shapes=[pltpu.VMEM((tm, tn) if nk > 1 else (8, 128), F32)],
        compiler_params=pltpu.CompilerParams(dimension_semantics=("parallel", "parallel", "arbitrary")),
    )(a, b, *deps)


def _row(width):
    return pl.BlockSpec((1, width), lambda *_: (0, 0))


NLAT = T // TC


def normmod_cat(ctx, x, g, csc, csh, sc, sh, dep, *, name, tm=256):
    assert tm == TC

    def body(c_ref, x_ref, g_ref, csc_ref, csh_ref, sc_ref, sh_ref, dep_ref, h_ref):
        last = pl.program_id(0) == NLAT
        xv = jnp.where(last, c_ref[...], x_ref[...])
        scv = jnp.where(last, csc_ref[...], sc_ref[...])
        shv = jnp.where(last, csh_ref[...], sh_ref[...])
        r = lax.rsqrt(jnp.mean(xv * xv, axis=-1, keepdims=True) + EPS)
        h_ref[...] = ((xv * r * g_ref[...]) * (1.0 + scv) + shv).astype(BF)

    return pl.pallas_call(
        body, name=name, grid=(TKV // tm,),
        in_specs=[pl.BlockSpec((tm, D), lambda i: (0, 0)), pl.BlockSpec((tm, D), lambda i: (jnp.minimum(i, NLAT - 1), 0)),
                  _row(D), _row(D), _row(D), _row(D), _row(D), pl.BlockSpec(memory_space=pl.ANY)],
        out_specs=pl.BlockSpec((tm, D), lambda i: (i, 0)), out_shape=jax.ShapeDtypeStruct((TKV, D), BF),
        compiler_params=pltpu.CompilerParams(dimension_semantics=("parallel",)),
    )(ctx, x, g, csc, csh, sc, sh, dep)


def kvprep(pc, p, kvg, wkv2, ck, sk, *, name, tm=256):
    assert tm == TC
    nb = TKV // tm
    kvcol = O_KV // 512

    def body(pc_ref, p_ref, g_ref, w_ref, ck_ref, sk_ref, k_ref, v_ref, ckv_ref):
        i = pl.program_id(0)
        t = jnp.where(i == NLAT, pc_ref[...], p_ref[...])
        pk = t[:, :KVL]
        r = lax.rsqrt(jnp.mean(pk * pk, axis=-1, keepdims=True) + EPS)
        ckv = (pk * r * g_ref[...]).astype(BF)
        ckv_ref[...] = ckv
        kv2 = jnp.dot(ckv, w_ref[...], preferred_element_type=F32)
        krr = t[:, KVL:KVL + HP] * ck_ref[...] + t[:, KVL + HP:KVL + 2 * HP] * sk_ref[...]
        k_ref[...] = (kv2[:, :NH * HP] + jnp.concatenate([krr] * NH, axis=1)).astype(BF)
        v_ref[...] = kv2[:, NH * HP:].astype(BF)

    return pl.pallas_call(
        body, name=name, grid=(nb,),
        in_specs=[pl.BlockSpec((tm, 512), lambda i: (0, 0)),
                  pl.BlockSpec((tm, 512), lambda i: (jnp.minimum(i, NLAT - 1), kvcol)),
                  _row(KVL), pl.BlockSpec((KVL, NH * HP + NH * DV), lambda i: (0, 0)),
                  pl.BlockSpec((tm, HP), lambda i: (i, 0)), pl.BlockSpec((tm, HP), lambda i: (i, 0))],
        out_specs=[pl.BlockSpec((tm, NH * HP), lambda i: (i, 0)), pl.BlockSpec((tm, NH * DV), lambda i: (i, 0)),
                   pl.BlockSpec((tm, KVL), lambda i: (i, 0))],
        out_shape=[jax.ShapeDtypeStruct((TKV, NH * HP), BF), jax.ShapeDtypeStruct((TKV, NH * DV), BF),
                   jax.ShapeDtypeStruct((TKV, KVL), BF)],
        compiler_params=pltpu.CompilerParams(dimension_semantics=("parallel",)),
    )(pc, p, kvg, wkv2, ck, sk)


def qprep(p, qg, wq2, cq_t, sq_t, *, name, tm=256):
    qcol = O_Q // 512

    def body(p_ref, g_ref, w_ref, c_ref, s_ref, q_ref, cq_ref):
        pq = p_ref[...]
        r = lax.rsqrt(jnp.sum(pq * pq, axis=-1, keepdims=True) * (1.0 / QL) + EPS)
        cq = (pq * r * g_ref[...]).astype(BF)
        cq_ref[...] = cq
        q2 = jnp.dot(cq, w_ref[...], preferred_element_type=F32)
        cc = jnp.concatenate([c_ref[...]] * NH, axis=1)
        ss = jnp.concatenate([s_ref[...]] * NH, axis=1)
        q_ref[...] = (q2[:, :NH * HP] * cc + q2[:, NH * HP:] * ss).astype(BF)

    return pl.pallas_call(
        body, name=name, grid=(T // tm,),
        in_specs=[pl.BlockSpec((tm, 512), lambda i: (i, qcol)), _row(512),
                  pl.BlockSpec((512, 2 * NH * HP), lambda i: (0, 0)),
                  pl.BlockSpec((tm, HP), lambda i: (i, 0)), pl.BlockSpec((tm, HP), lambda i: (i, 0))],
        out_specs=[pl.BlockSpec((tm, NH * HP), lambda i: (i, 0)), pl.BlockSpec((tm, 512), lambda i: (i, 0))],
        out_shape=[jax.ShapeDtypeStruct((T, NH * HP), BF), jax.ShapeDtypeStruct((T, 512), BF)],
        compiler_params=pltpu.CompilerParams(dimension_semantics=("parallel",)),
    )(p, qg, wq2, cq_t, sq_t)


def _head_mask(h):
    lanes = lax.broadcasted_iota(jnp.int32, (1, 2 * DV), 1)
    return (lanes // DV) == (h % 2)


LOG2E = 1.4426950408889634


def attn_fwd(q, k, v, *, name, tq=1024, kc=768):
    def body(q_ref, k_ref, v_ref, o_ref, lse_ref):
        h = pl.program_id(1)
        qv = q_ref[...]
        m = l = acc = None
        for c in range(TKV // kc):
            s = lax.dot_general(qv, k_ref[c * kc:(c + 1) * kc, :], (((1,), (1,)), ((), ())),
                                preferred_element_type=F32) * (SCALE * LOG2E)
            mc = jnp.max(s, axis=-1, keepdims=True)
            if c == 0:
                m = mc
                e = jnp.exp2(s - m)
                l = jnp.sum(e, axis=-1, keepdims=True)
                acc = jnp.dot(e.astype(BF), v_ref[c * kc:(c + 1) * kc, :], preferred_element_type=F32)
            else:
                mn = jnp.maximum(m, mc)
                a = jnp.exp2(m - mn)
                e = jnp.exp2(s - mn)
                l = l * a + jnp.sum(e, axis=-1, keepdims=True)
                acc = acc * a + jnp.dot(e.astype(BF), v_ref[c * kc:(c + 1) * kc, :], preferred_element_type=F32)
                m = mn
        o2 = jnp.where(_head_mask(h), acc * (1.0 / l), 0.0).astype(BF)
        lse_ref[...] = jnp.broadcast_to(m + jnp.log(l) * LOG2E, (tq, HP))

        @pl.when(h % 2 == 0)
        def _():
            o_ref[...] = o2

        @pl.when(h % 2 == 1)
        def _():
            o_ref[...] = o_ref[...] + o2

    return pl.pallas_call(
        body, name=name, grid=(T // tq, NH),
        in_specs=[pl.BlockSpec((tq, HP), lambda i, h: (i, h)), pl.BlockSpec((TKV, HP), lambda i, h: (0, h)),
                  pl.BlockSpec((TKV, 2 * DV), lambda i, h: (0, h // 2))],
        out_specs=[pl.BlockSpec((tq, 2 * DV), lambda i, h: (i, h // 2)), pl.BlockSpec((tq, HP), lambda i, h: (i, h))],
        out_shape=[jax.ShapeDtypeStruct((T, NH * DV), BF), jax.ShapeDtypeStruct((T, NH * HP), F32)],
        compiler_params=pltpu.CompilerParams(dimension_semantics=("parallel", "arbitrary")),
    )(q, k, v)


def _shift_dn(x):
    n = x.shape[0]
    rows = lax.broadcasted_iota(jnp.int32, (n, 1), 0)
    return jnp.where(rows == 0, 0.0, pltpu.roll(x, 1, axis=0))


def _shift_up(x):
    n = x.shape[0]
    rows = lax.broadcasted_iota(jnp.int32, (n, 1), 0)
    return jnp.where(rows == n - 1, 0.0, pltpu.roll(x, n - 1, axis=0))


def _conv(x, w_ref, b_ref):
    return b_ref[...] + _shift_dn(x) * w_ref[0:1, :] + x * w_ref[1:2, :] + _shift_up(x) * w_ref[2:3, :]


def _conv_t(dy, w_ref):
    return _shift_up(dy) * w_ref[0:1, :] + dy * w_ref[1:2, :] + _shift_dn(dy) * w_ref[2:3, :]


def _conv_wgrad(dw_ref, dy, x):
    dw_ref[0:1, :] = jnp.sum(dy * _shift_dn(x), axis=0, keepdims=True)
    dw_ref[1:2, :] = jnp.sum(dy * x, axis=0, keepdims=True)
    dw_ref[2:3, :] = jnp.sum(dy * _shift_up(x), axis=0, keepdims=True)


def convz(p, cw, cb, *, name):
    o0 = O_CV // (3 * CVB)

    def body(p_ref, w_ref, bias_ref, z_ref):
        xv, bv, cv = p_ref[:, 0:CVB], p_ref[:, CVB:2 * CVB], p_ref[:, 2 * CVB:3 * CVB]
        z_ref[...] = (bv * _conv(cv * xv, w_ref, bias_ref)).astype(BF)

    return pl.pallas_call(
        body, name=name, grid=(CONV // CVB,),
        in_specs=[pl.BlockSpec((T, 3 * CVB), lambda j: (0, o0 + j)), pl.BlockSpec((3, CVB), lambda j: (0, j)),
                  pl.BlockSpec((1, CVB), lambda j: (0, j))],
        out_specs=pl.BlockSpec((T, CVB), lambda j: (0, j)),
        out_shape=jax.ShapeDtypeStruct((T, CONV), BF),
        compiler_params=pltpu.CompilerParams(dimension_semantics=("parallel",)),
    )(p, cw, cb)


def out_proj_merge(o, wao, z, wco, p, *, name, tm=512):
    kin = o.shape[1]

    def body(o_ref, wa_ref, z_ref, wc_ref, ga_ref, gc_ref, ya_ref, yc_ref, m_ref):
        ya = jnp.dot(o_ref[...], wa_ref[...], preferred_element_type=F32)
        yc = jnp.dot(z_ref[...], wc_ref[...], preferred_element_type=F32)
        ya_ref[...] = ya
        yc_ref[...] = yc
        m_ref[...] = (jax.nn.sigmoid(ga_ref[...]) * ya + jax.nn.sigmoid(gc_ref[...]) * yc).astype(BF)

    blk = pl.BlockSpec((tm, D), lambda i: (i, 0))
    act = pl.BlockSpec((tm, kin), lambda i: (i, 0))
    wsp = pl.BlockSpec((kin, D), lambda i: (0, 0))
    sh = jax.ShapeDtypeStruct((T, D), F32)
    return pl.pallas_call(
        body, name=name, grid=(T // tm,),
        in_specs=[act, wsp, act, wsp, pl.BlockSpec((tm, D), lambda i: (i, O_GA // D)),
                  pl.BlockSpec((tm, D), lambda i: (i, O_GC // D))],
        out_specs=[blk, blk, blk], out_shape=[sh, sh, jax.ShapeDtypeStruct((T, D), BF)],
        compiler_params=pltpu.CompilerParams(dimension_semantics=("parallel",)),
    )(o, wao, z, wco, p, p)


CONV_HALO = 8
CONV_ROWS = 256


def _row_chunks(n, chunk, carry):
    carry = chunk(0, True, False, carry)
    carry = lax.fori_loop(1, n // CONV_ROWS - 1, lambda c, a: chunk(c * CONV_ROWS, False, False, a), carry)
    return chunk(n - CONV_ROWS, False, True, carry)


def _ext_rows(ref, r0, first, last):
    n, w = ref.shape
    zero = jnp.zeros((CONV_HALO, w), ref.dtype)
    if first:
        return jnp.concatenate([zero, ref[0:CONV_ROWS + CONV_HALO, :]], axis=0)
    if last:
        return jnp.concatenate([ref[n - CONV_ROWS - CONV_HALO:n, :], zero], axis=0)
    return ref[pl.ds(pl.multiple_of(r0 - CONV_HALO, 8), CONV_ROWS + 2 * CONV_HALO), :]


def _center_rows(r0, first, last):
    return slice(r0, r0 + CONV_ROWS) if (first or last) else pl.ds(pl.multiple_of(r0, 8), CONV_ROWS)


def _roll_dn(x):
    return pltpu.roll(x, 1, axis=0)


def _roll_up(x):
    return pltpu.roll(x, x.shape[0] - 1, axis=0)


_CTR = slice(CONV_HALO, CONV_HALO + CONV_ROWS)


def ffn_act(u0, cw, cb, *, name, tc=256):
    nb = DFF // tc

    def body(u_ref, wg_ref, wv_ref, bg_ref, bv_ref, f_ref):
        wg = [wg_ref[k:k + 1, :] for k in range(3)]
        wv = [wv_ref[k:k + 1, :] for k in range(3)]
        bg, bv = bg_ref[...], bv_ref[...]

        def chunk(r0, first, last, carry):
            xg, xv = _ext_rows(u_ref.at[0], r0, first, last), _ext_rows(u_ref.at[1], r0, first, last)
            ug = bg + _roll_dn(xg) * wg[0] + xg * wg[1] + _roll_up(xg) * wg[2]
            uv = bv + _roll_dn(xv) * wv[0] + xv * wv[1] + _roll_up(xv) * wv[2]
            f_ref[_center_rows(r0, first, last), :] = (ug * jax.nn.sigmoid(ug) * uv)[_CTR].astype(BF)
            return carry

        _row_chunks(T, chunk, 0)

    return pl.pallas_call(
        body, name=name, grid=(nb,),
        in_specs=[pl.BlockSpec((2, T, tc), lambda j: (0, 0, j)),
                  pl.BlockSpec((3, tc), lambda j: (0, j)), pl.BlockSpec((3, tc), lambda j: (0, nb + j)),
                  pl.BlockSpec((1, tc), lambda j: (0, j)), pl.BlockSpec((1, tc), lambda j: (0, nb + j))],
        out_specs=pl.BlockSpec((T, tc), lambda j: (0, j)),
        out_shape=jax.ShapeDtypeStruct((T, DFF), BF),
        compiler_params=pltpu.CompilerParams(dimension_semantics=("parallel",)),
    )(u0, cw, cw, cb, cb)


def rows_call(lead, ins, in_specs, out_shape, out_specs, fn, *, name, R, tm):
    tb, a_stack, tk = lead.get("tb", False), lead.get("a_stack", False), lead["tk"]
    K = 2 * lead["a"].shape[2] if a_stack else lead["a"].shape[1]
    nk = K // tk
    deps = [] if lead.get("dep") is None else [lead["dep"]]
    n_in = len(ins)

    def body(a_ref, b_ref, *refs):
        refs = refs[len(deps):]
        in_refs, out_refs, acc = refs[:n_in], refs[n_in:-1], refs[-1]
        i, k = pl.program_id(0), pl.program_id(1)
        part = lax.dot_general(a_ref[...].astype(BF), b_ref[...].astype(BF),
                               (((1,), (1 if tb else 0,)), ((), ())), preferred_element_type=F32)
        if nk == 1:
            fn(i, part, in_refs, out_refs)
            return

        @pl.when(k == 0)
        def _():
            acc[...] = part

        @pl.when(k > 0)
        def _():
            acc[...] += part

        @pl.when(k == nk - 1)
        def _():
            fn(i, acc[...], in_refs, out_refs)

    if a_stack:
        nhb = K // 2 // tk
        a_spec = pl.BlockSpec((None, tm, tk), lambda i, k: (k // nhb, i, k % nhb))
    else:
        a_spec = pl.BlockSpec((tm, tk), lambda i, k: (i, k))
    b_spec = pl.BlockSpec((D, tk), lambda i, k: (0, k)) if tb else pl.BlockSpec((tk, D), lambda i, k: (k, 0))
    return pl.pallas_call(
        body, name=name, grid=(R // tm, nk),
        in_specs=[a_spec, b_spec] + [pl.BlockSpec(memory_space=pl.ANY)] * len(deps) + list(in_specs),
        out_specs=out_specs, out_shape=out_shape,
        scratch_shapes=[pltpu.VMEM((tm, D) if nk > 1 else (8, 128), F32)],
        compiler_params=pltpu.CompilerParams(dimension_semantics=("arbitrary", "arbitrary")),
    )(lead["a"], lead["b"], *deps, *ins)


def _rblk(tm, w=D, col=0):
    return pl.BlockSpec((tm, w), lambda i, k: (i, col))


def _rrow(w=D):
    return pl.BlockSpec((1, w), lambda i, k: (0, 0))


def down_final(f, wdn, x1, g2, fg, tgt, *, name, tm=512):
    def fn(i, d, in_refs, out_refs):
        x1_ref, g2_ref, fg_ref, t_ref = in_refs
        d_ref, dx_ref, dd_ref, dfg_ref, loss_ref = out_refs
        d_ref[...] = d
        xv = x1_ref[...] + g2_ref[...] * d
        r = lax.rsqrt(jnp.mean(xv * xv, axis=-1, keepdims=True) + EPS)
        xh = xv * r
        diff = xh * fg_ref[...] - t_ref[...]
        part = 0.5 * jnp.sum(jnp.mean(diff * diff, axis=-1, keepdims=True), axis=0, keepdims=True)
        dy = diff * (1.0 / D)
        a = dy * fg_ref[...]
        dx = r * (a - xh * jnp.mean(a * xh, axis=-1, keepdims=True))
        dx_ref[...] = dx
        dd_ref[...] = (dx * g2_ref[...]).astype(BF)
        dfg = jnp.sum(dy * xh, axis=0, keepdims=True)

        @pl.when(i == 0)
        def _():
            dfg_ref[...] = dfg
            loss_ref[...] = jnp.broadcast_to(part, (1, 128))

        @pl.when(i > 0)
        def _():
            dfg_ref[...] += dfg
            loss_ref[...] += jnp.broadcast_to(part, (1, 128))

    blk = _rblk(tm)
    return rows_call(
        dict(a=f, b=wdn, tk=DFF), [x1, g2, fg, tgt], [blk, _rrow(), _rrow(), blk],
        [jax.ShapeDtypeStruct((T, D), F32), jax.ShapeDtypeStruct((T, D), F32), jax.ShapeDtypeStruct((T, D), BF),
         jax.ShapeDtypeStruct((1, D), F32), jax.ShapeDtypeStruct((1, 128), F32)],
        [blk, blk, blk, _rrow(), _rrow(128)], fn, name=name, R=T, tm=tm)


def oproj_resid(merged, wo, x, gate, g, sc, sh, *, name, tm=512):
    def fn(i, a, in_refs, out_refs):
        x_ref, gate_ref, g_ref, sc_ref, sh_ref = in_refs
        a_ref, x1_ref, h_ref = out_refs
        a_ref[...] = a
        xv = x_ref[...] + gate_ref[...] * a
        x1_ref[...] = xv
        r = lax.rsqrt(jnp.mean(xv * xv, axis=-1, keepdims=True) + EPS)
        h_ref[...] = ((xv * r * g_ref[...]) * (1.0 + sc_ref[...]) + sh_ref[...]).astype(BF)

    blk = _rblk(tm)
    return rows_call(
        dict(a=merged, b=wo, tk=D), [x, gate, g, sc, sh], [blk, _rrow(), _rrow(), _rrow(), _rrow()],
        [jax.ShapeDtypeStruct((T, D), F32), jax.ShapeDtypeStruct((T, D), F32), jax.ShapeDtypeStruct((T, D), BF)],
        [blk, blk, blk], fn, name=name, R=T, tm=tm)


def oproj_dx_gate_bwd(da, wo, p, ya, yc, wao, wco, *, name, tm=512):
    kin = wao.shape[0]

    def fn(i, dm, in_refs, out_refs):
        ga_ref, gc_ref, ya_ref, yc_ref, wa_ref, wc_ref = in_refs
        dya_ref, dyc_ref, dp_ref, do_ref, dz_ref = out_refs
        sa, sc_ = jax.nn.sigmoid(ga_ref[...]), jax.nn.sigmoid(gc_ref[...])
        dya, dyc = (dm * sa).astype(BF), (dm * sc_).astype(BF)
        dya_ref[...] = dya
        dyc_ref[...] = dyc
        dp_ref[:, 0:D] = (dm * ya_ref[...] * (sa * (1.0 - sa))).astype(BF)
        dp_ref[:, D:2 * D] = (dm * yc_ref[...] * (sc_ * (1.0 - sc_))).astype(BF)
        nt = (((1,), (1,)), ((), ()))
        do_ref[...] = lax.dot_general(dya, wa_ref[...], nt, preferred_element_type=F32).astype(BF)
        dz_ref[...] = lax.dot_general(dyc, wc_ref[...], nt, preferred_element_type=F32)

    blk = _rblk(tm)
    sh = jax.ShapeDtypeStruct((T, D), BF)
    wsp = pl.BlockSpec((kin, D), lambda i, k: (0, 0))
    return rows_call(
        dict(a=da, b=wo, tb=True, tk=D), [p, p, ya, yc, wao, wco],
        [_rblk(tm, D, O_GA // D), _rblk(tm, D, O_GC // D), blk, blk, wsp, wsp],
        [sh, sh, jax.ShapeDtypeStruct((T, NIN), BF), jax.ShapeDtypeStruct((T, kin), BF), jax.ShapeDtypeStruct((T, kin), F32)],
        [blk, blk, _rblk(tm, 2 * D), _rblk(tm, kin), _rblk(tm, kin)], fn, name=name, R=T, tm=tm)


def normmod_bwd(x, dh, g, sc, dres, gsrc, gate, *, name, tm=512):
    R = x.shape[0]
    tm = min(tm, R)
    has_res = dres is not None
    fused = isinstance(dh, dict)
    if fused:
        tb, a_stack, tk = dh.get("tb", False), dh.get("a_stack", False), dh["tk"]
        K = 2 * dh["a"].shape[2] if a_stack else dh["a"].shape[1]
        nk = K // tk
        deps = [] if dh.get("dep") is None else [dh["dep"]]
        n_dh = 2 + len(deps)
    else:
        nk, n_dh = 1, 1

    def elementwise(i, dhv, x_ref, g_ref, sc_ref, res_refs, out_refs):
        xv = x_ref[...]
        r = lax.rsqrt(jnp.mean(xv * xv, axis=-1, keepdims=True) + EPS)
        xh = xv * r
        n = xh * g_ref[...]
        dn = dhv * (1.0 + sc_ref[...])
        a = dn * g_ref[...]
        rows = [jnp.sum(dhv, axis=0, keepdims=True), jnp.sum(dhv * n, axis=0, keepdims=True),
                jnp.sum(dn * xh, axis=0, keepdims=True)]
        if has_res:
            dres_ref, gsrc_ref, gate_ref = res_refs
            dx_ref, dxg_ref, st_ref = out_refs
            dr = dres_ref[...]
            dx = dr + r * (a - xh * jnp.mean(a * xh, axis=-1, keepdims=True))
            dx_ref[...] = dx
            dxg_ref[...] = (dx * gate_ref[...]).astype(BF)
            rows.append(jnp.sum(dr * gsrc_ref[...], axis=0, keepdims=True))
        else:
            st_ref, = out_refs
            rows.append(jnp.zeros((1, D), F32))

        @pl.when(i == 0)
        def _():
            for k, row in enumerate(rows):
                st_ref[k:k + 1, :] = row

        @pl.when(i > 0)
        def _():
            for k, row in enumerate(rows):
                st_ref[k:k + 1, :] += row

    def body(*refs):
        x_ref, dh_refs, g_ref, sc_ref = refs[0], refs[1:1 + n_dh], refs[1 + n_dh], refs[2 + n_dh]
        rest = refs[3 + n_dh:]
        res_refs, rest = (rest[:3], rest[3:]) if has_res else ((), rest)
        out_refs = rest[:3] if has_res else rest[:1]
        i = pl.program_id(0)
        if not fused:
            elementwise(i, dh_refs[0][...], x_ref, g_ref, sc_ref, res_refs, out_refs)
            return
        acc = rest[-1]
        k = pl.program_id(1)
        part = lax.dot_general(dh_refs[0][...].astype(BF), dh_refs[1][...].astype(BF),
                               (((1,), (1 if tb else 0,)), ((), ())), preferred_element_type=F32)
        if nk == 1:
            elementwise(i, part, x_ref, g_ref, sc_ref, res_refs, out_refs)
            return

        @pl.when(k == 0)
        def _():
            acc[...] = part

        @pl.when(k > 0)
        def _():
            acc[...] += part

        @pl.when(k == nk - 1)
        def _():
            elementwise(i, acc[...], x_ref, g_ref, sc_ref, res_refs, out_refs)

    rowb = lambda w: pl.BlockSpec((1, w), lambda i, *k: (0, 0))
    blk = pl.BlockSpec((tm, D), lambda i, *k: (i, 0))
    st_spec = pl.BlockSpec((4, D), lambda i, *k: (0, 0))
    st_shape = jax.ShapeDtypeStruct((4, D), F32)
    if fused:
        if a_stack:
            nhb = K // 2 // tk
            a_spec = pl.BlockSpec((None, tm, tk), lambda i, k: (k // nhb, i, k % nhb))
        else:
            a_spec = pl.BlockSpec((tm, tk), lambda i, k: (i, k))
        b_spec = pl.BlockSpec((D, tk), lambda i, k: (0, k)) if tb else pl.BlockSpec((tk, D), lambda i, k: (k, 0))
        dh_specs = [a_spec, b_spec] + [pl.BlockSpec(memory_space=pl.ANY)] * len(deps)
        dh_args = [dh["a"], dh["b"]] + deps
        grid, sem = (R // tm, nk), ("arbitrary", "arbitrary")
        scratch = [pltpu.VMEM((tm, D) if nk > 1 else (8, 128), F32)]
    else:
        dh_specs, dh_args, grid, sem, scratch = [blk], [dh], (R // tm,), ("arbitrary",), []
    cp = pltpu.CompilerParams(dimension_semantics=sem)
    if has_res:
        return pl.pallas_call(
            body, name=name, grid=grid, in_specs=[blk] + dh_specs + [rowb(D), rowb(D), blk, blk, rowb(D)],
            out_specs=[blk, blk, st_spec], scratch_shapes=scratch,
            out_shape=[jax.ShapeDtypeStruct((R, D), F32), jax.ShapeDtypeStruct((R, D), BF), st_shape],
            compiler_params=cp,
        )(x, *dh_args, g, sc, dres, gsrc, gate)
    return pl.pallas_call(
        body, name=name, grid=grid, in_specs=[blk] + dh_specs + [rowb(D), rowb(D)],
        out_specs=st_spec, out_shape=st_shape, scratch_shapes=scratch, compiler_params=cp,
    )(x, *dh_args, g, sc)


def ffn_act_bwd(u0, df, cw, cb, *, name, tc=128):
    nb = DFF // tc

    def body(u_ref, df_ref, wg_ref, wv_ref, bg_ref, bv_ref, du_ref, dw_ref, db_ref):
        wg = [wg_ref[k:k + 1, :] for k in range(3)]
        wv = [wv_ref[k:k + 1, :] for k in range(3)]
        bg, bv = bg_ref[...], bv_ref[...]

        def chunk(r0, first, last, acc):
            xg, xv = _ext_rows(u_ref.at[0], r0, first, last), _ext_rows(u_ref.at[1], r0, first, last)
            dfe = _ext_rows(df_ref, r0, first, last)
            xg_d, xg_u, xv_d, xv_u = _roll_dn(xg), _roll_up(xg), _roll_dn(xv), _roll_up(xv)
            ug = bg + xg_d * wg[0] + xg * wg[1] + xg_u * wg[2]
            uv = bv + xv_d * wv[0] + xv * wv[1] + xv_u * wv[2]
            sig = jax.nn.sigmoid(ug)
            dug = dfe * uv * (sig * (1.0 + ug * (1.0 - sig)))
            duv = dfe * (ug * sig)
            rows = _center_rows(r0, first, last)
            du_ref[0, rows, :] = (_roll_up(dug) * wg[0] + dug * wg[1] + _roll_dn(dug) * wg[2])[_CTR].astype(BF)
            du_ref[1, rows, :] = (_roll_up(duv) * wv[0] + duv * wv[1] + _roll_dn(duv) * wv[2])[_CTR].astype(BF)
            terms = [dug * xg_d, dug * xg, dug * xg_u, dug, duv * xv_d, duv * xv, duv * xv_u, duv]
            return tuple(a + jnp.sum(t[_CTR], axis=0, keepdims=True) for a, t in zip(acc, terms))

        acc = _row_chunks(T, chunk, tuple(jnp.zeros((1, tc), F32) for _ in range(8)))
        for k in range(3):
            dw_ref[0, k:k + 1, :] = acc[k]
            dw_ref[1, k:k + 1, :] = acc[4 + k]
        db_ref[0] = acc[3]
        db_ref[1] = acc[7]

    lo = lambda r: pl.BlockSpec((r, tc), lambda j: (0, j))
    hi = lambda r: pl.BlockSpec((r, tc), lambda j: (0, nb + j))
    st = lambda r: pl.BlockSpec((2, r, tc), lambda j: (0, 0, j))
    return pl.pallas_call(
        body, name=name, grid=(nb,),
        in_specs=[st(T), lo(T), lo(3), hi(3), lo(1), hi(1)],
        out_specs=[st(T), st(3), st(1)],
        out_shape=[jax.ShapeDtypeStruct((2, T, DFF), BF), jax.ShapeDtypeStruct((2, 3, DFF), F32),
                   jax.ShapeDtypeStruct((2, 1, DFF), F32)],
        compiler_params=pltpu.CompilerParams(dimension_semantics=("parallel",)),
    )(u0, df, cw, cw, cb, cb)


def convz_bwd(p, dz, cw, cb, dp, *, name):
    o0 = O_CV // (3 * CVB)

    def body(p_ref, dz_ref, w_ref, bias_ref, dp_in, dp_ref, dw_ref, dbias_ref):
        xv, bv, cv = p_ref[:, 0:CVB], p_ref[:, CVB:2 * CVB], p_ref[:, 2 * CVB:3 * CVB]
        ci = cv * xv
        dwc = _conv(ci, w_ref, bias_ref)
        dzv = dz_ref[...]
        ddw = dzv * bv
        dci = _conv_t(ddw, w_ref)
        dp_ref[:, 0:CVB] = (dci * cv).astype(BF)
        dp_ref[:, CVB:2 * CVB] = (dzv * dwc).astype(BF)
        dp_ref[:, 2 * CVB:3 * CVB] = (dci * xv).astype(BF)
        _conv_wgrad(dw_ref, ddw, ci)
        dbias_ref[...] = jnp.sum(ddw, axis=0, keepdims=True)

    own = lambda r: pl.BlockSpec((r, CVB), lambda j: (0, j))
    return pl.pallas_call(
        body, name=name, grid=(CONV // CVB,),
        in_specs=[pl.BlockSpec((T, 3 * CVB), lambda j: (0, o0 + j)), own(T), own(3), own(1),
                  pl.BlockSpec(memory_space=pl.ANY)],
        out_specs=[pl.BlockSpec((T, 3 * CVB), lambda j: (0, o0 + j)), own(3), own(1)],
        out_shape=[jax.ShapeDtypeStruct((T, NIN), BF), jax.ShapeDtypeStruct((3, CONV), F32),
                   jax.ShapeDtypeStruct((1, CONV), F32)],
        input_output_aliases={4: 0},
        compiler_params=pltpu.CompilerParams(dimension_semantics=("parallel",)),
    )(p, dz, cw, cb, dp)


def attn_bwd(q, k, v, do, o, lse, dep, *, name, tq=1024, kc=768):
    NKC, KC = TKV // kc, kc
    deps = [] if dep is None else [dep]

    def body(q_ref, k_ref, v_ref, do_ref, o_ref, lse_ref, *rest):
        dq_ref, dk_ref, dv_ref = rest[len(deps):]
        h, i = pl.program_id(0), pl.program_id(1)

        @pl.when(i == 0)
        def _():
            dk_ref[...] = jnp.zeros_like(dk_ref)

        @pl.when((i == 0) & (h % 2 == 0))
        def _():
            dv_ref[...] = jnp.zeros_like(dv_ref)

        qv = q_ref[...]
        dom = jnp.where(_head_mask(h), do_ref[...], jnp.zeros_like(do_ref[...]))
        delta = jnp.sum(dom.astype(F32) * o_ref[...].astype(F32), axis=-1, keepdims=True)
        lse = lse_ref[:, 0:1]
        dq = jnp.zeros((tq, HP), F32)
        for c in range(NKC):
            cols = slice(c * KC, (c + 1) * KC)
            s = lax.dot_general(qv, k_ref[cols, :], (((1,), (1,)), ((), ())),
                                preferred_element_type=F32) * (SCALE * LOG2E)
            pr = jnp.exp2(s - lse)
            dp = lax.dot_general(dom, v_ref[cols, :], (((1,), (1,)), ((), ())), preferred_element_type=F32)
            ds = (pr * (dp - delta) * SCALE).astype(BF)
            dq = dq + jnp.dot(ds, k_ref[cols, :], preferred_element_type=F32)
            dk_ref[cols, :] += lax.dot_general(ds, qv, (((0,), (0,)), ((), ())), preferred_element_type=F32)
            dv_ref[cols, :] += lax.dot_general(pr.astype(BF), dom, (((0,), (0,)), ((), ())), preferred_element_type=F32)
        dq_ref[...] = dq

    return pl.pallas_call(
        body, name=name, grid=(NH, T // tq),
        in_specs=[pl.BlockSpec((tq, HP), lambda h, i: (i, h)), pl.BlockSpec((TKV, HP), lambda h, i: (0, h)),
                  pl.BlockSpec((TKV, 2 * DV), lambda h, i: (0, h // 2)), pl.BlockSpec((tq, 2 * DV), lambda h, i: (i, h // 2)),
                  pl.BlockSpec((tq, 2 * DV), lambda h, i: (i, h // 2)), pl.BlockSpec((tq, HP), lambda h, i: (i, h)),
                  *([pl.BlockSpec(memory_space=pl.ANY)] * len(deps))],
        out_specs=[pl.BlockSpec((tq, HP), lambda h, i: (i, h)), pl.BlockSpec((TKV, HP), lambda h, i: (0, h)),
                   pl.BlockSpec((TKV, 2 * DV), lambda h, i: (0, h // 2))],
        out_shape=[jax.ShapeDtypeStruct((T, NH * HP), F32), jax.ShapeDtypeStruct((TKV, NH * HP), F32),
                   jax.ShapeDtypeStruct((TKV, NH * DV), F32)],
        compiler_params=pltpu.CompilerParams(dimension_semantics=("arbitrary", "arbitrary")),
    )(q, k, v, do, o, lse, *deps)


def qprep_bwd(p, dq, qg, wq2, cq_t, sq_t, dp, *, name, tm=256):
    qcol = O_Q // 512

    def body(p_ref, dq_ref, g_ref, w_ref, c_ref, s_ref, dp_in, dp_ref, dq2_ref, dg_ref):
        i = pl.program_id(0)
        dqv = dq_ref[...]
        cc = jnp.concatenate([c_ref[...]] * NH, axis=1)
        ss = jnp.concatenate([s_ref[...]] * NH, axis=1)
        dq2 = jnp.concatenate([dqv * cc, dqv * ss], axis=1).astype(BF)
        dq2_ref[...] = dq2
        dcq = lax.dot_general(dq2, w_ref[...], (((1,), (1,)), ((), ())), preferred_element_type=F32)
        pq = p_ref[...]
        r = lax.rsqrt(jnp.sum(pq * pq, axis=-1, keepdims=True) * (1.0 / QL) + EPS)
        xh = pq * r
        a = dcq * g_ref[...]
        dp_ref[...] = (r * (a - xh * (jnp.sum(a * xh, axis=-1, keepdims=True) * (1.0 / QL)))).astype(BF)
        dg = jnp.sum(dcq * xh, axis=0, keepdims=True)

        @pl.when(i == 0)
        def _():
            dg_ref[...] = dg

        @pl.when(i > 0)
        def _():
            dg_ref[...] += dg

    return pl.pallas_call(
        body, name=name, grid=(T // tm,),
        in_specs=[pl.BlockSpec((tm, 512), lambda i: (i, qcol)), pl.BlockSpec((tm, NH * HP), lambda i: (i, 0)), _row(512),
                  pl.BlockSpec((512, 2 * NH * HP), lambda i: (0, 0)),
                  pl.BlockSpec((tm, HP), lambda i: (i, 0)), pl.BlockSpec((tm, HP), lambda i: (i, 0)),
                  pl.BlockSpec(memory_space=pl.ANY)],
        out_specs=[pl.BlockSpec((tm, 512), lambda i: (i, qcol)), pl.BlockSpec((tm, 2 * NH * HP), lambda i: (i, 0)), _row(512)],
        out_shape=[jax.ShapeDtypeStruct((T, NIN), BF), jax.ShapeDtypeStruct((T, 2 * NH * HP), BF),
                   jax.ShapeDtypeStruct((1, 512), F32)],
        input_output_aliases={6: 0},
        compiler_params=pltpu.CompilerParams(dimension_semantics=("arbitrary",)),
    )(p, dq, qg, wq2, cq_t, sq_t, dp)


def kvprep_bwd(pc, p, dk, dv, kvg, wkv2, ck, sk, dp, *, name, tm=256):
    assert tm == TC
    nb = TKV // tm
    kvcol = O_KV // 512

    def body(pc_ref, p_ref, dk_ref, dv_ref, g_ref, w_ref, ck_ref, sk_ref, dp_in, dp_ref, dpc_ref, dkv2_ref, dg_ref):
        i = pl.program_id(0)
        t = jnp.where(i == NLAT, pc_ref[...], p_ref[...])
        pk = t[:, :KVL]
        r = lax.rsqrt(jnp.mean(pk * pk, axis=-1, keepdims=True) + EPS)
        xh = pk * r
        dkv = dk_ref[...]
        dkv2 = jnp.concatenate([dkv, dv_ref[...]], axis=1).astype(BF)
        dkv2_ref[...] = dkv2
        dckv = lax.dot_general(dkv2, w_ref[...], (((1,), (1,)), ((), ())), preferred_element_type=F32)
        a = dckv * g_ref[...]
        dpk = r * (a - xh * jnp.mean(a * xh, axis=-1, keepdims=True))
        dkr = dkv[:, 0:HP]
        for hh in range(1, NH):
            dkr = dkr + dkv[:, hh * HP:(hh + 1) * HP]
        res = jnp.concatenate([dpk, dkr * ck_ref[...], dkr * sk_ref[...]], axis=1).astype(BF)
        dg = jnp.sum(dckv * xh, axis=0, keepdims=True)

        @pl.when(i == 0)
        def _():
            dg_ref[...] = dg

        @pl.when(i > 0)
        def _():
            dg_ref[...] += dg

        @pl.when(i < NLAT)
        def _():
            dp_ref[...] = res

        @pl.when(i == NLAT)
        def _():
            dpc_ref[...] = res

    rb = lambda w: pl.BlockSpec((tm, w), lambda i: (i, 0))
    return pl.pallas_call(
        body, name=name, grid=(nb,),
        in_specs=[pl.BlockSpec((tm, 512), lambda i: (0, 0)),
                  pl.BlockSpec((tm, 512), lambda i: (jnp.minimum(i, NLAT - 1), kvcol)),
                  rb(NH * HP), rb(NH * DV), _row(KVL), pl.BlockSpec((KVL, NH * HP + NH * DV), lambda i: (0, 0)),
                  rb(HP), rb(HP), pl.BlockSpec(memory_space=pl.ANY)],
        out_specs=[pl.BlockSpec((tm, 512), lambda i: (jnp.minimum(i, NLAT - 1), kvcol)),
                   pl.BlockSpec((tm, 512), lambda i: (0, 0)), rb(NH * HP + NH * DV), _row(KVL)],
        out_shape=[jax.ShapeDtypeStruct((T, NIN), BF), jax.ShapeDtypeStruct((TC, 512), BF),
                   jax.ShapeDtypeStruct((TKV, NH * HP + NH * DV), BF), jax.ShapeDtypeStruct((1, KVL), F32)],
        input_output_aliases={8: 0},
        compiler_params=pltpu.CompilerParams(dimension_semantics=("arbitrary",)),
    )(pc, p, dk, dv, kvg, wkv2, ck, sk, dp)


def _pieces(src, width, n):
    out, c = [], src
    while c < src + width:
        k = c // n
        w = min(src + width, (k + 1) * n) - c
        out.append((k, c - k * n, c - src, w))
        c += w
    return out


def _win_moves():
    mv = [(2208, 1024, O_GA), (3232, 1024, O_GC), (0, KVL, O_KV), (256, DR, O_KV + KVL + DN), (288, QL, O_Q)]
    mv += [(256 + _swap_start(g), 8, O_KV + KVL + HP + DN + 8 * g) for g in range(4)]
    for j in range(CONV // CVB):
        base = O_CV + 3 * CVB * j
        mv += [(672 + CVB * j, CVB, base), (1184 + CVB * j, CVB, base + CVB), (1696 + CVB * j, CVB, base + 2 * CVB)]
    return mv


_WIN_ZERO = [(O_KV + KVL, DN), (O_KV + KVL + DN + DR, HP - DN - DR), (O_KV + KVL + HP, DN),
             (O_KV + KVL + HP + DN + DR, HP - DN - DR), (O_Q + QL, 512 - QL)]


def build_win(g, *, name, tm=256):
    def body(g_ref, o_ref):
        for src, w, dst in _win_moves():
            for k, a, off, pw in _pieces(src, w, SH_IN):
                o_ref[:, dst + off:dst + off + pw] = g_ref[k, :, a:a + pw]
        for c0, w in _WIN_ZERO:
            o_ref[:, c0:c0 + w] = jnp.zeros((tm, w), o_ref.dtype)

    return pl.pallas_call(
        body, name=name, grid=(D // tm,), in_specs=[pl.BlockSpec((NDEV, tm, SH_IN), lambda i: (0, i, 0))],
        out_specs=pl.BlockSpec((tm, NIN), lambda i: (i, 0)), out_shape=jax.ShapeDtypeStruct((D, NIN), g.dtype),
        compiler_params=pltpu.CompilerParams(dimension_semantics=("parallel",)),
    )(g)


def shard_win_grad(dwt, dwct, *, name, tc=256):
    def body(dw_ref, dwc_ref, o_ref, kvs):
        kvs[...] = dw_ref[O_KV:O_KV + 512, :] + dwc_ref[...]

        def src(row, w):
            if O_KV <= row < O_KV + 512:
                return kvs[row - O_KV:row - O_KV + w, :]
            return dw_ref[row:row + w, :]

        for s, w, dst in _win_moves():
            if w == 8 or s == 256:
                continue
            for k, a, off, pw in _pieces(s, w, SH_IN):
                o_ref[k, a:a + pw, :] = src(dst + off, pw).astype(o_ref.dtype)
        for g in range(4):
            val = src(O_KV + KVL + DN + 8 * g, 8) + src(O_KV + KVL + HP + DN + _swap_start(g), 8)
            o_ref[0, 256 + 8 * g:256 + 8 * g + 8, :] = val.astype(o_ref.dtype)

    return pl.pallas_call(
        body, name=name, grid=(D // tc,),
        in_specs=[pl.BlockSpec((NIN, tc), lambda j: (0, j)), pl.BlockSpec((512, tc), lambda j: (0, j))],
        out_specs=pl.BlockSpec((NDEV, SH_IN, tc), lambda j: (0, 0, j)),
        out_shape=jax.ShapeDtypeStruct((NDEV, SH_IN, D), BF),
        scratch_shapes=[pltpu.VMEM((512, tc), F32)],
        compiler_params=pltpu.CompilerParams(dimension_semantics=("parallel",)),
    )(dwt, dwct)


def _eye(n, m):
    return (lax.broadcasted_iota(jnp.int32, (n, m), 0) == lax.broadcasted_iota(jnp.int32, (n, m), 1)).astype(BF)


_NT = (((1,), (1,)), ((), ()))


def build_wq_wkv(gq, gkv, *, name):
    def body(gq_ref, gkv_ref, q_ref, kv_ref):
        q_ref[...] = jnp.zeros_like(q_ref)
        kv_ref[...] = jnp.zeros_like(kv_ref)
        eye = _eye(QL, QL)
        for h in range(NH):
            qh = lax.dot_general(eye, gq_ref[h], _NT, preferred_element_type=F32).astype(q_ref.dtype)
            q_ref[0:QL, h * HP:h * HP + DN + DR] = qh
            for g in range(4):
                c0 = NH * HP + h * HP + DN + 8 * g
                q_ref[0:QL, c0:c0 + 8] = qh[:, DN + _swap_start(g):DN + _swap_start(g) + 8]
            kv_ref[:, h * HP:h * HP + DN] = gkv_ref[h, :, 0:DN]
            kv_ref[:, NH * HP + h * DV:NH * HP + (h + 1) * DV] = gkv_ref[h, :, DN:DN + DV]

    vm = pl.BlockSpec(memory_space=pltpu.VMEM)
    return pl.pallas_call(
        body, name=name, in_specs=[vm, vm], out_specs=[vm, vm],
        out_shape=[jax.ShapeDtypeStruct((512, 2 * NH * HP), gq.dtype), jax.ShapeDtypeStruct((KVL, NH * HP + NH * DV), gq.dtype)],
    )(gq, gkv)


def shard_wq_wkv_grad(dwq2, dwkv2, *, name):
    def body(q_ref, kv_ref, gq_ref, gkv_ref, xs):
        xs[...] = jnp.zeros_like(xs)
        eye = _eye(DN + DR, HP)
        for h in range(NH):
            xs[:, 0:DN] = q_ref[0:QL, h * HP:h * HP + DN].astype(BF)
            for g in range(4):
                a = q_ref[0:QL, h * HP + DN + 8 * g:h * HP + DN + 8 * g + 8]
                c0 = NH * HP + h * HP + DN + _swap_start(g)
                xs[:, DN + 8 * g:DN + 8 * g + 8] = (a + q_ref[0:QL, c0:c0 + 8]).astype(BF)
            gq_ref[h] = lax.dot_general(eye, xs[...], _NT, preferred_element_type=F32).astype(BF)
            gkv_ref[h, :, 0:DN] = kv_ref[:, h * HP:h * HP + DN].astype(BF)
            gkv_ref[h, :, DN:DN + DV] = kv_ref[:, NH * HP + h * DV:NH * HP + (h + 1) * DV].astype(BF)

    vm = pl.BlockSpec(memory_space=pltpu.VMEM)
    return pl.pallas_call(
        body, name=name, in_specs=[vm, vm], out_specs=[vm, vm],
        out_shape=[jax.ShapeDtypeStruct((NDEV, DN + DR, QL), BF), jax.ShapeDtypeStruct((NDEV, KVL, DN + DV), BF)],
        scratch_shapes=[pltpu.VMEM((QL, HP), BF)],
    )(dwq2, dwkv2)


def unshard_cols(g, *, name, tm=256):
    _, K, n = g.shape
    tm = _pick(K, tm, 16)

    def body(g_ref, o_ref):
        for k in range(NDEV):
            o_ref[:, k * n:(k + 1) * n] = g_ref[k]

    return pl.pallas_call(
        body, name=name, grid=(K // tm,), in_specs=[pl.BlockSpec((NDEV, tm, n), lambda i: (0, i, 0))],
        out_specs=pl.BlockSpec((tm, NDEV * n), lambda i: (i, 0)), out_shape=jax.ShapeDtypeStruct((K, NDEV * n), g.dtype),
        compiler_params=pltpu.CompilerParams(dimension_semantics=("parallel",)),
    )(g)


def shard_cols(w, *, name, tm=256):
    K, n8 = w.shape
    n = n8 // NDEV
    tm = _pick(K, tm, 16)

    def body(w_ref, o_ref):
        for k in range(NDEV):
            o_ref[k] = w_ref[:, k * n:(k + 1) * n]

    return pl.pallas_call(
        body, name=name, grid=(K // tm,), in_specs=[pl.BlockSpec((tm, n8), lambda i: (i, 0))],
        out_specs=pl.BlockSpec((NDEV, tm, n), lambda i: (0, i, 0)), out_shape=jax.ShapeDtypeStruct((NDEV, K, n), w.dtype),
        compiler_params=pltpu.CompilerParams(dimension_semantics=("parallel",)),
    )(w)


def _rope_tables():
    t = np.arange(T)
    row = (t // GRID_W).astype(np.float32)
    col = (t % GRID_W).astype(np.float32)
    axis_dim = DR // 2
    inv = (np.float32(ROPE_THETA) ** (-np.arange(0, axis_dim, 2, dtype=np.float32) / np.float32(axis_dim))).astype(np.float32)
    ar, ac = (row[:, None] * inv).astype(np.float32), (col[:, None] * inv).astype(np.float32)
    cosv = np.concatenate([np.cos(ar), np.cos(ar), np.cos(ac), np.cos(ac)], axis=1).astype(np.float32)
    sinv = np.concatenate([-np.sin(ar), np.sin(ar), -np.sin(ac), np.sin(ac)], axis=1).astype(np.float32)
    ck = np.zeros((TKV, HP), np.float32)
    sk = np.zeros((TKV, HP), np.float32)
    ck[T:, DN:DN + DR] = 1.0
    ck[:T, DN:DN + DR] = cosv
    sk[:T, DN:DN + DR] = sinv
    cq = np.zeros((T, HP), np.float32)
    cq[:, :DN] = 1.0
    cq[:, DN:DN + DR] = cosv
    return jnp.asarray(ck), jnp.asarray(sk), jnp.asarray(cq), jnp.asarray(sk[:T])


def _local_step(x, ctx, tgt, mod_lat, mod_ctx, n1g, qg, kvg, n2g, fg, conv_w, conv_b, ffn_w, ffn_b, get_w, put_g, dep0):
    sh1, sc1, g1, sh2, sc2, g2 = [mod_lat[:, i * D:(i + 1) * D] for i in range(6)]
    csh1, csc1 = mod_ctx[:, 0:D], mod_ctx[:, D:2 * D]
    ck, sk, cq_t, sq_t = _rope_tables()
    qg_p = jnp.pad(qg, ((0, 0), (0, 512 - QL)))

    hcat = normmod_cat(ctx, x, n1g, csc1, csh1, sc1, sh1, dep0, name="normmod1")
    win = get_w("in", hcat)
    p = mm(hcat, win, M=T, tn=768, name="in_proj")
    pc = mm(hcat, win, M=TC, N=512, a_off=(T, 0), b_off=(0, O_KV), name="in_proj_ctx")
    wq2, wkv2, wao, wco, wo = get_w("mid", p)
    kh, vh, ckv = kvprep(pc, p, kvg, wkv2, ck, sk, name="kvprep")
    qr, cq = qprep(p, qg_p, wq2, cq_t, sq_t, name="qprep")
    o, lse = attn_fwd(qr, kh, vh, name="attn_fwd")
    z = convz(p, conv_w, conv_b, name="convz")
    ya, yc, merged = out_proj_merge(o, wao, z, wco, p, name="attn_conv_out_gate_merge")
    a_out, x1, h2 = oproj_resid(merged, wo, x, g1, n2g, sc2, sh2, name="o_proj_resid_normmod2")
    wup = get_w("up", h2)
    u0 = mm(h2, wup, tb=True, o_stack=True, tn=1408, name="up_proj")
    f = ffn_act(u0, ffn_w, ffn_b, name="ffn_act")
    wdn = get_w("down", f)
    dn, dx2, dd, dfg, loss = down_final(f, wdn, x1, g2, fg, tgt, name="down_proj_final_loss")

    df = mm(dd, wdn, tb=True, tn=1408, name="down_proj_dx")
    dwdn = mm(f, dd, ta=True, out_dtype=BF, tm=1408, name="down_proj_dw")
    du0, dffn_w, dffn_b = ffn_act_bwd(u0, df, ffn_w, ffn_b, name="ffn_act_bwd")
    dwup = mm(du0, h2, ta=True, a_stack=True, out_dtype=BF, tm=1408, name="up_proj_dw")
    tok = put_g("ffn", dict(dwup=dwup, dwdn=dwdn))
    dx1, da, st2 = normmod_bwd(x1, dict(a=du0, b=wup, a_stack=True, tk=DFF, dep=tok), n2g, sc2, dx2, dn, g1,
                               name="up_proj_dx_normmod2_bwd")

    dwo = mm(merged, da, ta=True, out_dtype=BF, tn=512, name="o_proj_dw")
    dya, dyc, dp, do, dz = oproj_dx_gate_bwd(da, wo, p, ya, yc, wao, wco, name="o_proj_dx_gate_merge_bwd")
    dwao = mm(o, dya, ta=True, out_dtype=BF, tn=512, name="attn_out_dw")
    dwco = mm(z, dyc, ta=True, out_dtype=BF, tn=512, name="conv_out_dw")
    tok = put_g("mid", dict(dwao=dwao, dwco=dwco, dwo=dwo))
    dp, dconv_w, dconv_b = convz_bwd(p, dz, conv_w, conv_b, dp, name="convz_bwd")
    dq, dk, dv = attn_bwd(qr, kh, vh, do, o, lse, tok, name="attn_bwd")
    dp, dq2, dqg = qprep_bwd(p, dq, qg_p, wq2, cq_t, sq_t, dp, name="qprep_bwd")
    dp, dpc, dkv2, dkvg = kvprep_bwd(pc, p, dk, dv, kvg, wkv2, ck, sk, dp, name="kvprep_bwd")

    dwin = mm(dp, hcat, ta=True, K=T, tm=768, name="in_proj_dw")
    dwin_c = mm(dpc, hcat, ta=True, K=TC, b_off=(T, 0), name="in_proj_ctx_dw")
    tok = put_g("in", dict(dwin=dwin, dwin_c=dwin_c))
    dwq2 = mm(cq, dq2, ta=True, dep=tok, name="q_up_dw")
    dwkv2 = mm(ckv, dkv2, ta=True, name="kv_up_dw")
    tok = put_g("qkv", dict(dwq2=dwq2, dwkv2=dwkv2))
    dhc = mm(dpc, win, tb=True, N=D, K=512, b_off=(0, O_KV), name="in_proj_ctx_dx")
    dx, _, st1 = normmod_bwd(x, dict(a=dp, b=win, tb=True, tk=NIN, dep=tok), n1g, sc1, dx1, a_out, g1,
                             name="in_proj_dx_normmod1_bwd")
    stc = normmod_bwd(ctx, dhc, n1g, csc1, None, None, None, name="normmod1_ctx_bwd")

    zrow = jnp.zeros((1, D), F32)
    dmod_lat = jnp.concatenate([st1[0:1], st1[1:2], st1[3:4], st2[0:1], st2[1:2], st2[3:4]], axis=1)
    dmod_ctx = jnp.concatenate([stc[0:1], stc[1:2], zrow, zrow, zrow, zrow], axis=1)
    return dict(
        loss=loss, dx=dx, dmod_lat=dmod_lat, dmod_ctx=dmod_ctx,
        dn1g=st1[2:3] + stc[2:3], dqg=dqg, dkvg=dkvg, dn2g=st2[2:3], dfg=dfg,
        dconv_w=dconv_w, dconv_b=dconv_b, dffn_w=dffn_w, dffn_b=dffn_b)


def _me():
    x, y, c = lax.axis_index("x"), lax.axis_index("y"), lax.axis_index("c")
    return x, y, c, 4 * x + 2 * y + c


def _peer(x, y, c, k):
    px = 1 - x if k & 4 else x
    py = 1 - y if k & 2 else y
    pc = 1 - c if k & 1 else c
    return (px, py, pc), 4 * px + 2 * py + pc


def _exchange_tiles(src_of_peer, buf, send_sem, recv_sem):
    x, y, c, me = _me()
    for k in range(1, NDEV):
        dev, lin = _peer(x, y, c, k)
        pltpu.make_async_remote_copy(src_ref=src_of_peer(lin), dst_ref=buf.at[me], send_sem=send_sem, recv_sem=recv_sem,
                                     device_id=dev, device_id_type=MESH).start()
    seven = buf.at[pl.ds(0, NDEV - 1)]
    pltpu.make_async_remote_copy(src_ref=seven, dst_ref=seven, send_sem=send_sem, recv_sem=recv_sem,
                                 device_id=(x, y, c), device_id_type=MESH).wait()


def _silu(z):
    return z * jax.nn.sigmoid(z)


def ada_fwd(c, c_ctx, ffn_w, conv_w, w_shard, b_shard, deps, *, name):
    nsh = w_shard.shape[1]
    deps = [d for d in deps if d is not None]

    def body(c_ref, cc_ref, fw_ref, cw_ref, w_ref, b_ref, *rest):
        s_ref, m_ref, mine, res, sems = rest[len(deps):]
        x, y, c, me = _me()
        mine[0:1, :] = _silu(c_ref[...])
        mine[1:2, :] = _silu(cc_ref[...])
        mine[2:5, :] = fw_ref[...]
        mine[5:8, :] = cw_ref[...]
        s_ref[me] = mine[...]
        _exchange_tiles(lambda lin: mine, s_ref, sems.at[0], sems.at[1])
        sall = s_ref[...].reshape(NDEV * 8, D).astype(BF)
        r = jnp.dot(sall, w_ref[...].astype(BF), preferred_element_type=F32) + b_ref[...]
        res[...] = r.reshape(NDEV, 8, nsh)
        m_ref[me] = res[me]
        _exchange_tiles(lambda lin: res.at[lin], m_ref, sems.at[2], sems.at[3])

    vm = pl.BlockSpec(memory_space=pltpu.VMEM)
    return pl.pallas_call(
        body, name=name, in_specs=[vm] * 6 + [pl.BlockSpec(memory_space=pl.ANY)] * len(deps), out_specs=[vm, vm],
        out_shape=[jax.ShapeDtypeStruct((NDEV, 8, D), F32), jax.ShapeDtypeStruct((NDEV, 8, nsh), F32)],
        scratch_shapes=[pltpu.VMEM((8, D), F32), pltpu.VMEM((NDEV, 8, nsh), F32), pltpu.SemaphoreType.DMA((4,))],
    )(c, c_ctx, ffn_w, conv_w, w_shard, b_shard, *deps)


P_DML, P_DMC, P_N1, P_QG, P_KVG, P_CB, P_N2, P_FB, P_FG, P_CW, P_FW, P_LOSS, P_ROWS = 0, 6, 12, 13, 14, 15, 16, 17, 23, 24, 27, 45, 48
FROWS = 3


def pack_small(r, *, name):
    ins = [r["dmod_lat"], r["dmod_ctx"], r["dn1g"], r["dqg"], r["dkvg"], r["dconv_b"], r["dn2g"], r["dffn_b"], r["dfg"],
           r["dconv_w"], r["dffn_w"], r["loss"]]

    def put_wide(p, row0, row, n):
        for j in range(-(-n // D)):
            w = min(D, n - j * D)
            p[row0 + j:row0 + j + 1, 0:w] = row[:, j * D:j * D + w]

    def body(dml, dmc, n1, qg, kvg, cb, n2, fb, fg, cw, fw, loss, p):
        p[...] = jnp.zeros_like(p)
        put_wide(p, P_DML, dml, 6 * D)
        put_wide(p, P_DMC, dmc, 6 * D)
        put_wide(p, P_N1, n1, D)
        put_wide(p, P_QG, qg, 512)
        put_wide(p, P_KVG, kvg, KVL)
        put_wide(p, P_CB, cb, CONV)
        put_wide(p, P_N2, n2, D)
        put_wide(p, P_FG, fg, D)
        put_wide(p, P_LOSS, loss, 128)
        for s in range(2):
            put_wide(p, P_FB + FROWS * s, fb.at[s], DFF)
        for k in range(3):
            put_wide(p, P_CW + k, cw.at[k:k + 1], CONV)
            for s in range(2):
                put_wide(p, P_FW + FROWS * (2 * k + s), fw.at[s, k:k + 1], DFF)

    vm = pl.BlockSpec(memory_space=pltpu.VMEM)
    return pl.pallas_call(
        body, name=name, in_specs=[vm] * len(ins), out_specs=vm, out_shape=jax.ShapeDtypeStruct((P_ROWS, D), F32),
    )(*ins)


def sum_slots(a, *, name):
    def body(a_ref, sum_ref):
        acc = a_ref[0]
        for k in range(1, NDEV):
            acc = acc + a_ref[k]
        sum_ref[...] = acc

    vm = pl.BlockSpec(memory_space=pltpu.VMEM)
    return pl.pallas_call(body, name=name, in_specs=[vm], out_specs=vm, out_shape=jax.ShapeDtypeStruct(a.shape[1:], F32))(a)


def ada_bwd(s_all, dml, dmc, w_shard, c_ctx, *, name):
    nsh = w_shard.shape[1]

    def body(s_ref, dml_ref, dmc_ref, w_ref, c_ref, dw_ref, gc_ref, s16, dm16, part, buf, sems):
        x, y, c, me = _me()
        s16[...] = jnp.zeros_like(s16)
        dm16[...] = jnp.zeros_like(dm16)
        for k in range(NDEV):
            s16[k:k + 1, :] = s_ref[k, 0:1, :]
        s16[8:9, :] = s_ref[0, 1:2, :]
        dm16[0:8, :] = dml_ref[...]
        dm16[8:9, :] = dmc_ref[...]
        dw_ref[...] = lax.dot_general(s16[...].astype(BF), dm16[...].astype(BF), (((0,), (0,)), ((), ())),
                                      preferred_element_type=F32)
        part[...] = lax.dot_general(dm16[8:16, :].astype(BF), w_ref[...].astype(BF), (((1,), (1,)), ((), ())),
                                    preferred_element_type=F32)
        buf[me] = part[...]
        _exchange_tiles(lambda lin: part, buf, sems.at[0], sems.at[1])
        acc = buf[0]
        for k in range(1, NDEV):
            acc = acc + buf[k]
        z = c_ref[...]
        sg = jax.nn.sigmoid(z)
        gc_ref[...] = acc * (sg * (1.0 + z * (1.0 - sg)))

    vm = pl.BlockSpec(memory_space=pltpu.VMEM)
    return pl.pallas_call(
        body, name=name, in_specs=[vm] * 5, out_specs=[vm, vm],
        out_shape=[jax.ShapeDtypeStruct((D, nsh), F32), jax.ShapeDtypeStruct((8, D), F32)],
        scratch_shapes=[pltpu.VMEM((16, D), F32), pltpu.VMEM((16, nsh), F32), pltpu.VMEM((8, D), F32),
                        pltpu.VMEM((NDEV, 8, D), F32), pltpu.SemaphoreType.DMA((2,))],
    )(s_all, dml, dmc, w_shard, c_ctx)


HBM_SPEC = pl.BlockSpec(memory_space=pltpu.HBM)
SEM_SPEC = pl.BlockSpec(memory_space=pltpu.SEMAPHORE)
EFFECT = pltpu.SideEffectType.DATAFLOW_SIDE_EFFECTING


ALL_PEERS = tuple(range(1, NDEV))
FIRST_HOP = (1, 2, 4, 6)
RELAY = (2, 4, 6)


def _exchange_copies(srcs, lands, send, recv, per_peer, peers):
    x, y, c, me = _me()
    n = len(peers)
    cps = []
    for t in range(len(srcs)):
        for j, k in enumerate(peers):
            dev, lin = _peer(x, y, c, k)
            cps.append(pltpu.make_async_remote_copy(
                src_ref=srcs[t].at[lin] if per_peer else srcs[t], dst_ref=lands[t].at[me],
                send_sem=send.at[n * t + j], recv_sem=recv.at[n * t + j], device_id=dev, device_id_type=MESH))
    return cps


def _relay_copies(lands, send, recv):
    x, y, c, me = _me()
    n = len(RELAY)
    cps = []
    for t in range(len(lands)):
        for j, k in enumerate(RELAY):
            slot = lands[t].at[_peer(x, y, c, k)[1]]
            cps.append(pltpu.make_async_remote_copy(
                src_ref=slot, dst_ref=slot, send_sem=send.at[n * t + j], recv_sem=recv.at[n * t + j],
                device_id=(x, y, 1 - c), device_id_type=MESH))
    return cps


def _own_copies(srcs, lands, own, per_peer):
    me = _me()[3]
    return [pltpu.make_async_copy(srcs[t].at[me] if per_peer else srcs[t], lands[t].at[me], own.at[t])
            for t in range(len(srcs))]


def exchange_start(srcs, *, per_peer, name, dep=None, peers=ALL_PEERS):
    nt = len(srcs)
    ns = len(peers) * nt
    land_shapes = [(a.shape if per_peer else (NDEV,) + a.shape) for a in srcs]
    deps = [] if dep is None else [dep]

    def body(*refs):
        src, land = refs[:nt], refs[nt:2 * nt]
        send, recv, own = refs[2 * nt + len(deps):2 * nt + len(deps) + 3]
        for cp in _exchange_copies(src, land, send, recv, per_peer, peers) + _own_copies(src, land, own, per_peer):
            cp.start()
        refs[-1][...] = jnp.zeros_like(refs[-1])

    hb = lambda a: pltpu.with_memory_space_constraint(a, pltpu.HBM)
    outs = pl.pallas_call(
        body, name=name,
        out_shape=(pltpu.SemaphoreType.DMA((ns,)), pltpu.SemaphoreType.DMA((ns,)), pltpu.SemaphoreType.DMA((nt,)),
                   *[pltpu.HBM(a.shape, a.dtype) for a in srcs], *[pltpu.HBM(s, a.dtype) for s, a in zip(land_shapes, srcs)],
                   jax.ShapeDtypeStruct((8, 128), F32)),
        in_specs=[HBM_SPEC] * (2 * nt) + [pl.BlockSpec(memory_space=pl.ANY)] * len(deps),
        out_specs=(SEM_SPEC, SEM_SPEC, SEM_SPEC, *([HBM_SPEC] * (2 * nt)), pl.BlockSpec(memory_space=pltpu.VMEM)),
        input_output_aliases={i: 3 + i for i in range(2 * nt)},
        compiler_params=pltpu.CompilerParams(has_side_effects=EFFECT),
    )(*[hb(a) for a in srcs], *[hb(lax.empty(s, a.dtype)) for s, a in zip(land_shapes, srcs)], *deps)
    return dict(send=outs[0], recv=outs[1], own=outs[2], src=list(outs[3:3 + nt]), land=list(outs[3 + nt:3 + 2 * nt]),
                token=outs[-1], per_peer=per_peer, peers=peers)


def exchange_wait(h, after, *, name):
    nt = len(h["src"])
    per_peer, peers = h["per_peer"], h["peers"]
    after = list(after) if isinstance(after, (list, tuple)) else [after]

    def body(*refs):
        src, land, send, recv, own = refs[:nt], refs[nt:2 * nt], refs[2 * nt], refs[2 * nt + 1], refs[2 * nt + 2]
        for cp in _exchange_copies(src, land, send, recv, per_peer, peers):
            cp.wait_send()
            cp.wait_recv()
        for cp in _own_copies(src, land, own, per_peer):
            cp.wait()

    outs = pl.pallas_call(
        body, name=name,
        out_shape=(*[pltpu.HBM(a.shape, a.dtype) for a in h["src"]], *[pltpu.HBM(a.shape, a.dtype) for a in h["land"]]),
        in_specs=[HBM_SPEC] * (2 * nt) + [SEM_SPEC, SEM_SPEC, SEM_SPEC] + [pl.BlockSpec(memory_space=pl.ANY)] * len(after),
        out_specs=tuple([HBM_SPEC] * (2 * nt)),
        input_output_aliases={i: i for i in range(2 * nt)},
        compiler_params=pltpu.CompilerParams(has_side_effects=EFFECT),
    )(*h["src"], *h["land"], h["send"], h["recv"], h["own"], *after)
    return list(outs[nt:])


def relay_start(lands, *, name):
    nt = len(lands)
    ns = len(RELAY) * nt

    def body(*refs):
        for cp in _relay_copies(refs[:nt], refs[nt], refs[nt + 1]):
            cp.start()

    outs = pl.pallas_call(
        body, name=name,
        out_shape=(pltpu.SemaphoreType.DMA((ns,)), pltpu.SemaphoreType.DMA((ns,)),
                   *[pltpu.HBM(a.shape, a.dtype) for a in lands]),
        in_specs=[HBM_SPEC] * nt, out_specs=(SEM_SPEC, SEM_SPEC, *([HBM_SPEC] * nt)),
        input_output_aliases={i: 2 + i for i in range(nt)},
        compiler_params=pltpu.CompilerParams(has_side_effects=EFFECT),
    )(*lands)
    return dict(send=outs[0], recv=outs[1], land=list(outs[2:]))


def relay_wait(h, *, name):
    nt = len(h["land"])

    def body(*refs):
        for cp in _relay_copies(refs[:nt], refs[nt], refs[nt + 1]):
            cp.wait_send()
            cp.wait_recv()

    outs = pl.pallas_call(
        body, name=name, out_shape=tuple(pltpu.HBM(a.shape, a.dtype) for a in h["land"]),
        in_specs=[HBM_SPEC] * nt + [SEM_SPEC, SEM_SPEC], out_specs=tuple([HBM_SPEC] * nt),
        input_output_aliases={i: i for i in range(nt)},
        compiler_params=pltpu.CompilerParams(has_side_effects=EFFECT),
    )(*h["land"], h["send"], h["recv"])
    return list(outs)


def _adamw_math(w, g, m, v):
    nm = B1 * m + (1.0 - B1) * g
    nv = B2 * v + (1.0 - B2) * (g * g)
    m_hat = nm / (1.0 - B1 ** STEP)
    v_hat = nv / (1.0 - B2 ** STEP)
    return -LR * (m_hat / (jnp.sqrt(v_hat) + AEPS) + WD * w), nm, nv


def adamw_many(ws, gs, ms, vs, *, name):
    n = len(ws)

    def body(*refs):
        for k in range(n):
            d, nm, nv = _adamw_math(refs[k][...], refs[n + k][...], refs[2 * n + k][...], refs[3 * n + k][...])
            refs[4 * n + k][...] = d
            refs[5 * n + k][...] = nm
            refs[6 * n + k][...] = nv

    vm = pl.BlockSpec(memory_space=pltpu.VMEM)
    sh = [jax.ShapeDtypeStruct(w.shape, F32) for w in ws]
    outs = pl.pallas_call(body, name=name, in_specs=[vm] * (4 * n), out_specs=[vm] * (3 * n), out_shape=sh * 3,
                          )(*ws, *gs, *ms, *vs)
    return outs[:n], outs[n:2 * n], outs[2 * n:]


def adamw(w, g, m, v, *, name, tr=256):
    R, C = w.shape
    tr = _pick(R, tr, 8)

    def body(w_ref, g_ref, m_ref, v_ref, d_ref, nm_ref, nv_ref):
        d_ref[...], nm_ref[...], nv_ref[...] = _adamw_math(w_ref[...], g_ref[...], m_ref[...], v_ref[...])

    blk = pl.BlockSpec((tr, C), lambda i: (i, 0))
    sh = jax.ShapeDtypeStruct((R, C), F32)
    return pl.pallas_call(
        body, name=name, grid=(R // tr,), in_specs=[blk, blk, blk, blk], out_specs=[blk, blk, blk],
        out_shape=[sh, sh, sh], compiler_params=pltpu.CompilerParams(dimension_semantics=("parallel",)),
    )(w, g, m, v)


def adamw_slots(w, slots, m, v, *, name, tr=256):
    unit = w.ndim == 3
    R, C = w.shape[0], w.shape[-1]
    if R % 16 == 0:
        tr = _pick(R, tr, 16)
    else:
        tr = 144

    def body(w_ref, s_ref, m_ref, v_ref, g_ref, d_ref, nm_ref, nv_ref):
        g = s_ref[0].astype(F32)
        for k in range(1, NDEV):
            g = g + s_ref[k].astype(F32)
        g_ref[...] = g
        d_ref[...], nm_ref[...], nv_ref[...] = _adamw_math(w_ref[...], g, m_ref[...], v_ref[...])

    blk = pl.BlockSpec((tr, None, C), lambda i: (i, 0, 0)) if unit else pl.BlockSpec((tr, C), lambda i: (i, 0))
    sh = jax.ShapeDtypeStruct(w.shape, F32)
    return pl.pallas_call(
        body, name=name, grid=(pl.cdiv(R, tr),), in_specs=[blk, pl.BlockSpec((NDEV, tr, C), lambda i: (0, i, 0)), blk, blk],
        out_specs=[blk, blk, blk, blk], out_shape=[sh, sh, sh, sh],
        compiler_params=pltpu.CompilerParams(dimension_semantics=("parallel",)),
    )(w, slots, m, v)


def _padc(a, n=D):
    return jnp.pad(a, ((0, 0), (0, n - a.shape[1])))


def kernel(x, c, ctx, c_ctx, w_ada, b_ada, norm1_g, w_in, q_norm_g, kv_norm_g, w_uq, w_ukv, conv_w, conv_b, w_attn_out, w_conv_out, w_o, norm2_g, w_up, ffn_conv_w, ffn_conv_b, w_down, final_g, loss_target, m_c_ctx, m_w_ada, m_b_ada, m_norm1_g, m_w_in, m_q_norm_g, m_kv_norm_g, m_w_uq, m_w_ukv, m_conv_w, m_conv_b, m_w_attn_out, m_w_conv_out, m_w_o, m_norm2_g, m_w_up, m_ffn_conv_w, m_ffn_conv_b, m_w_down, m_final_g, v_c_ctx, v_w_ada, v_b_ada, v_norm1_g, v_w_in, v_q_norm_g, v_kv_norm_g, v_w_uq, v_w_ukv, v_conv_w, v_conv_b, v_w_attn_out, v_w_conv_out, v_w_o, v_norm2_g, v_w_up, v_ffn_conv_w, v_ffn_conv_b, v_w_down, v_final_g):
    me = 4 * lax.axis_index("x") + 2 * lax.axis_index("y") + lax.axis_index("c")
    W = dict(c_ctx=c_ctx, w_ada=w_ada, b_ada=b_ada, norm1_g=norm1_g, w_in=w_in, q_norm_g=q_norm_g, kv_norm_g=kv_norm_g,
             w_uq=w_uq, w_ukv=w_ukv, conv_w=conv_w, conv_b=conv_b, w_attn_out=w_attn_out, w_conv_out=w_conv_out, w_o=w_o,
             norm2_g=norm2_g, w_up=w_up, ffn_conv_w=ffn_conv_w, ffn_conv_b=ffn_conv_b, w_down=w_down, final_g=final_g)
    M = dict(c_ctx=m_c_ctx, w_ada=m_w_ada, b_ada=m_b_ada, norm1_g=m_norm1_g, w_in=m_w_in, q_norm_g=m_q_norm_g,
             kv_norm_g=m_kv_norm_g, w_uq=m_w_uq, w_ukv=m_w_ukv, conv_w=m_conv_w, conv_b=m_conv_b, w_attn_out=m_w_attn_out,
             w_conv_out=m_w_conv_out, w_o=m_w_o, norm2_g=m_norm2_g, w_up=m_w_up, ffn_conv_w=m_ffn_conv_w,
             ffn_conv_b=m_ffn_conv_b, w_down=m_w_down, final_g=m_final_g)
    V = dict(c_ctx=v_c_ctx, w_ada=v_w_ada, b_ada=v_b_ada, norm1_g=v_norm1_g, w_in=v_w_in, q_norm_g=v_q_norm_g,
             kv_norm_g=v_kv_norm_g, w_uq=v_w_uq, w_ukv=v_w_ukv, conv_w=v_conv_w, conv_b=v_conv_b, w_attn_out=v_w_attn_out,
             w_conv_out=v_w_conv_out, w_o=v_w_o, norm2_g=v_norm2_g, w_up=v_w_up, ffn_conv_w=v_ffn_conv_w,
             ffn_conv_b=v_ffn_conv_b, w_down=v_w_down, final_g=v_final_g)
    names = list(W)
    transposed = ("w_up", "w_uq")
    as2d = lambda k, a: (a.reshape(1, -1) if a.ndim == 1 else
                         a[0].T if k in transposed else a.reshape(a.shape[-2], a.shape[-1]))
    W2 = {k: as2d(k, a) for k, a in W.items()}
    M2 = {k: as2d(k, a) for k, a in M.items()}
    V2 = {k: as2d(k, a) for k, a in V.items()}
    unit3 = lambda a: jnp.transpose(a, (2, 0, 1))
    W3, M3, V3 = unit3(W["w_in"]), unit3(M["w_in"]), unit3(V["w_in"])
    nsh = W2["w_ada"].shape[1]

    b_sh = lax.dynamic_slice(W2["b_ada"], (0, me * nsh), (1, nsh))
    s_all, m_all = ada_fwd(c, W2["c_ctx"], _padc(W2["ffn_conv_w"]), _padc(W2["conv_w"]), W2["w_ada"], b_sh, [],
                           name="ada_fwd")
    mod_lat = m_all[:, 0, :].reshape(1, 6 * D)
    mod_ctx = m_all[:, 1, :].reshape(1, 6 * D)
    ffn_w_full = s_all[:, 2:5, :2 * DFF // NDEV].transpose(1, 0, 2).reshape(3, 2 * DFF)
    conv_w_full = s_all[:, 5:8, :CONV // NDEV].transpose(1, 0, 2).reshape(3, CONV)

    stage_w = {"in": ["w_in"], "mid": ["w_uq", "w_ukv", "w_attn_out", "w_conv_out", "w_o"], "up": ["w_up"],
               "down": ["w_down"]}
    two_level = ("in", "mid")
    ag, tok = {}, m_all
    for st, nms in stage_w.items():
        ag[st] = exchange_start([W2[nm].astype(BF) for nm in nms], per_peer=False, dep=tok, name="ag_start_" + st,
                                peers=FIRST_HOP if st in two_level else ALL_PEERS)
        tok = ag[st]["token"]

    def get_w(stage, after):
        lands = exchange_wait(ag[stage], after, name="ag_wait_" + stage)
        if stage in two_level:
            lands = relay_wait(relay_start(lands, name="ag_relay_" + stage), name="ag_relay_wait_" + stage)
        g = dict(zip(stage_w[stage], lands))
        if stage == "in":
            return build_win(g["w_in"], name="build_win")
        if stage == "mid":
            wq2, wkv2 = build_wq_wkv(g["w_uq"], g["w_ukv"], name="build_wq_wkv")
            return (wq2, wkv2, unshard_cols(g["w_attn_out"], name="unshard_w_attn_out"),
                    unshard_cols(g["w_conv_out"], name="unshard_w_conv_out"), g["w_o"].reshape(D, D))
        if stage == "up":
            return g["w_up"].reshape(2 * DFF, D)
        return g["w_down"].reshape(DFF, D)

    stage_g = {"ffn": ["w_up", "w_down"], "mid": ["w_attn_out", "w_conv_out", "w_o"], "qkv": ["w_uq", "w_ukv"],
               "in": ["w_in"]}
    rs = {}

    def put_g(stage, g):
        if stage == "in":
            parts = [shard_win_grad(g["dwin"], g["dwin_c"], name="shard_win_grad")]
        elif stage == "mid":
            parts = [shard_cols(g["dwao"], name="shard_w_attn_out"), shard_cols(g["dwco"], name="shard_w_conv_out"),
                     g["dwo"].reshape(NDEV, D // NDEV, D)]
        elif stage == "qkv":
            parts = list(shard_wq_wkv_grad(g["dwq2"], g["dwkv2"], name="shard_wq_wkv_grad"))
        else:
            parts = [g["dwup"].reshape(NDEV, 2 * DFF // NDEV, D), g["dwdn"].reshape(NDEV, DFF // NDEV, D)]
        rs[stage] = exchange_start(parts, per_peer=True, name="rs_start_" + stage)
        return rs[stage]["token"]

    r = _local_step(x[0], ctx[0], loss_target[0], mod_lat, mod_ctx, W2["norm1_g"], W2["q_norm_g"], W2["kv_norm_g"],
                    W2["norm2_g"], W2["final_g"], conv_w_full, W2["conv_b"], ffn_w_full, W2["ffn_conv_b"], get_w, put_g,
                    ag["down"]["token"])

    G, DL, NM, NV = {}, {}, {}, {}

    def finish(stage, after):
        for nm, sl in zip(stage_g[stage], exchange_wait(rs[stage], after, name="rs_wait_" + stage)):
            wmv = (W3, M3, V3) if nm == "w_in" else (W2[nm], M2[nm], V2[nm])
            G[nm], DL[nm], NM[nm], NV[nm] = adamw_slots(wmv[0], sl, wmv[1], wmv[2], name="adamw_" + nm)
            after = DL[nm]
        return after

    sync = exchange_start([pack_small(r, name="pack_small")], per_peer=False, name="sync_start")
    after = sync["token"]
    for st in ("ffn", "mid", "in", "qkv"):
        after = finish(st, after)
    a_buf, = exchange_wait(sync, [DL[nm] for nms in stage_g.values() for nm in nms], name="sync_wait")
    ssum = sum_slots(a_buf, name="sum_small")
    loss = ssum[P_LOSS, 0]
    G["norm1_g"] = ssum[P_N1:P_N1 + 1]
    G["q_norm_g"] = ssum[P_QG:P_QG + 1, :QL]
    G["kv_norm_g"] = ssum[P_KVG:P_KVG + 1, :KVL]
    G["conv_b"] = ssum[P_CB:P_CB + 1, :CONV]
    G["norm2_g"] = ssum[P_N2:P_N2 + 1]
    G["ffn_conv_b"] = ssum[P_FB:P_FB + 2 * FROWS].reshape(1, 2, FROWS * D)[:, :, :DFF].reshape(1, 2 * DFF)
    G["final_g"] = ssum[P_FG:P_FG + 1]
    G["conv_w"] = lax.dynamic_slice(ssum[P_CW:P_CW + 3, :CONV], (0, me * (CONV // NDEV)), (3, CONV // NDEV))
    fw_full = ssum[P_FW:P_FW + 6 * FROWS].reshape(3, 2, FROWS * D)[:, :, :DFF].reshape(3, 2 * DFF)
    G["ffn_conv_w"] = lax.dynamic_slice(fw_full, (0, me * (2 * DFF // NDEV)), (3, 2 * DFF // NDEV))
    G["b_ada"] = (ssum[P_DML:P_DML + 6] + ssum[P_DMC:P_DMC + 6]).reshape(1, 6 * D)

    dml = lax.dynamic_slice(a_buf[:, P_DML:P_DML + 6, :].reshape(NDEV, 6 * D), (0, me * nsh), (NDEV, nsh))
    dmc = lax.dynamic_slice(ssum[P_DMC:P_DMC + 6].reshape(1, 6 * D), (0, me * nsh), (1, nsh))
    G["w_ada"], gcc = ada_bwd(s_all, dml, dmc, W2["w_ada"], W2["c_ctx"], name="ada_bwd")
    G["c_ctx"] = gcc[0:1]

    DL["w_ada"], NM["w_ada"], NV["w_ada"] = adamw(W2["w_ada"], G["w_ada"], M2["w_ada"], V2["w_ada"], name="adamw_w_ada")
    small = ["c_ctx", "b_ada", "norm1_g", "q_norm_g", "kv_norm_g", "conv_b", "norm2_g", "ffn_conv_b", "final_g", "conv_w",
             "ffn_conv_w"]
    ds, nms, nvs = adamw_many([W2[k] for k in small], [G[k] for k in small], [M2[k] for k in small],
                              [V2[k] for k in small], name="adamw_small")
    for k, nm in enumerate(small):
        DL[nm], NM[nm], NV[nm] = ds[k], nms[k], nvs[k]

    outs = [loss, r["dx"][None]]
    for grp in (G, DL, NM, NV):
        outs += [grp[nm].T[None] if nm in transposed else
                 jnp.transpose(grp[nm], (1, 2, 0)) if nm == "w_in" else grp[nm].reshape(W[nm].shape) for nm in names]
    return tuple(outs)
```

```python
import functools
import numpy as np
import jax
import jax.numpy as jnp
from jax import lax
from jax.experimental import pallas as pl
from jax.experimental.pallas import tpu as pltpu

F32 = jnp.float32
BF = jnp.bfloat16
MESH = pl.DeviceIdType.MESH

D = 1024
T = 2048
TC = 256
TKV = T + TC
GRID_W = 64
NH = 8
DN = 64
DR = 32
DV = 64
QL = 384
KVL = 256
CONV = 512
DFF = 2816
EPS = 1e-6
ROPE_THETA = 10000.0
SCALE = (DN + DR) ** -0.5
NDEV = 8
HP = 128

O_GA, O_GC, O_KV, O_Q, O_CV = 0, 1024, 2048, 2560, 3072
NIN = 4608
CVB = 256
N_IN = 4256
SH_IN = N_IN // NDEV

LR, B1, B2, AEPS, WD, STEP = 0.001, 0.9, 0.999, 1e-08, 0.01, 10


def _pick(n, target, mult=128):
    best = None
    for d in range(mult, min(n, target) + 1, mult):
        if n % d == 0:
            best = d
    return best if best is not None else n


def _swap_start(g):
    return 8 * (g ^ 1)


def mm(a, b, *, ta=False, tb=False, out_dtype=F32, name, tm=1024, tn=1024, tk=2048, M=None, N=None, K=None,
       a_off=(0, 0), b_off=(0, 0), a_stack=False, b_stack=False, o_stack=False, dep=None):
    def dims(arr, stack):
        return (arr.shape[1], 2 * arr.shape[2]) if stack else arr.shape

    ar, ac = dims(a, a_stack)
    br, bc = dims(b, b_stack)
    M = M or ((ac if ta else ar) - a_off[1 if ta else 0])
    K = K or ((ar if ta else ac) - a_off[0 if ta else 1])
    N = N or ((br if tb else bc) - b_off[0 if tb else 1])
    tm = _pick(M, tm, 128 if ta else 16)
    tn = _pick(N // 2 if (o_stack or (b_stack and not tb)) else N, tn, 128)
    tk = _pick(K // 2 if ((a_stack and not ta) or (b_stack and tb)) else K, tk, 128)
    nk = K // tk
    ca = 0 if ta else 1
    cb = 1 if tb else 0

    def body(a_ref, b_ref, *rest):
        o_ref, acc = rest[-2:]
        k = pl.program_id(2)
        part = lax.dot_general(a_ref[...].astype(BF), b_ref[...].astype(BF),
                               (((ca,), (cb,)), ((), ())), preferred_element_type=F32)
        if nk == 1:
            o_ref[...] = part.astype(o_ref.dtype)
        else:
            @pl.when(k == 0)
            def _():
                acc[...] = part

            @pl.when(k > 0)
            def _():
                acc[...] += part

            @pl.when(k == nk - 1)
            def _():
                o_ref[...] = acc[...].astype(o_ref.dtype)

    def spec(blk, rc, off, stack, ncols):
        assert off[0] % blk[0] == 0 and off[1] % blk[1] == 0, (name, blk, off)
        ro, co = off[0] // blk[0], off[1] // blk[1]
        if not stack:
            return pl.BlockSpec(blk, lambda i, j, k: (rc(i, j, k)[0] + ro, rc(i, j, k)[1] + co))
        nhb = ncols // 2 // blk[1]
        return pl.BlockSpec((None,) + blk,
                            lambda i, j, k: ((rc(i, j, k)[1] + co) // nhb, rc(i, j, k)[0] + ro, (rc(i, j, k)[1] + co) % nhb))

    a_spec = spec((tk, tm), lambda i, j, k: (k, i), a_off, a_stack, ac) if ta else \
        spec((tm, tk), lambda i, j, k: (i, k), a_off, a_stack, ac)
    b_spec = spec((tn, tk), lambda i, j, k: (j, k), b_off, b_stack, bc) if tb else \
        spec((tk, tn), lambda i, j, k: (k, j), b_off, b_stack, bc)
    o_spec = spec((tm, tn), lambda i, j, k: (i, j), (0, 0), o_stack, N)
    o_shape = (2, M, N // 2) if o_stack else (M, N)
    deps = [] if dep is None else [dep]
    return pl.pallas_call(
        body, name=name, grid=(M // tm, N // tn, nk),
        in_specs=[a_spec, b_spec] + [pl.BlockSpec(memory_space=pl.ANY)] * len(deps),
        out_specs=o_spec, out_shape=jax.ShapeDtypeStruct(o_shape, out_dtype),
        scratch_shapes=[pltpu.VMEM((tm, tn) if nk > 1 else (8, 128), F32)],
        compiler_params=pltpu.CompilerParams(dimension_semantics=("parallel", "parallel", "arbitrary")),
    )(a, b, *deps)


def _row(width):
    return pl.BlockSpec((1, width), lambda *_: (0, 0))


NLAT = T // TC


def normmod_cat(ctx, x, g, csc, csh, sc, sh, dep, *, name, tm=256):
    assert tm == TC

    def body(c_ref, x_ref, g_ref, csc_ref, csh_ref, sc_ref, sh_ref, dep_ref, h_ref):
        last = pl.program_id(0) == NLAT
        xv = jnp.where(last, c_ref[...], x_ref[...])
        scv = jnp.where(last, csc_ref[...], sc_ref[...])
        shv = jnp.where(last, csh_ref[...], sh_ref[...])
        r = lax.rsqrt(jnp.mean(xv * xv, axis=-1, keepdims=True) + EPS)
        h_ref[...] = ((xv * r * g_ref[...]) * (1.0 + scv) + shv).astype(BF)

    return pl.pallas_call(
        body, name=name, grid=(TKV // tm,),
        in_specs=[pl.BlockSpec((tm, D), lambda i: (0, 0)), pl.BlockSpec((tm, D), lambda i: (jnp.minimum(i, NLAT - 1), 0)),
                  _row(D), _row(D), _row(D), _row(D), _row(D), pl.BlockSpec(memory_space=pl.ANY)],
        out_specs=pl.BlockSpec((tm, D), lambda i: (i, 0)), out_shape=jax.ShapeDtypeStruct((TKV, D), BF),
        compiler_params=pltpu.CompilerParams(dimension_semantics=("parallel",)),
    )(ctx, x, g, csc, csh, sc, sh, dep)


def kvprep(pc, p, kvg, wkv2, ck, sk, *, name, tm=256):
    assert tm == TC
    nb = TKV // tm
    kvcol = O_KV // 512

    def body(pc_ref, p_ref, g_ref, w_ref, ck_ref, sk_ref, k_ref, v_ref, ckv_ref):
        i = pl.program_id(0)
        t = jnp.where(i == NLAT, pc_ref[...], p_ref[...])
        pk = t[:, :KVL]
        r = lax.rsqrt(jnp.mean(pk * pk, axis=-1, keepdims=True) + EPS)
        ckv = (pk * r * g_ref[...]).astype(BF)
        ckv_ref[...] = ckv
        kv2 = jnp.dot(ckv, w_ref[...], preferred_element_type=F32)
        krr = t[:, KVL:KVL + HP] * ck_ref[...] + t[:, KVL + HP:KVL + 2 * HP] * sk_ref[...]
        k_ref[...] = (kv2[:, :NH * HP] + jnp.concatenate([krr] * NH, axis=1)).astype(BF)
        v_ref[...] = kv2[:, NH * HP:].astype(BF)

    return pl.pallas_call(
        body, name=name, grid=(nb,),
        in_specs=[pl.BlockSpec((tm, 512), lambda i: (0, 0)),
                  pl.BlockSpec((tm, 512), lambda i: (jnp.minimum(i, NLAT - 1), kvcol)),
                  _row(KVL), pl.BlockSpec((KVL, NH * HP + NH * DV), lambda i: (0, 0)),
                  pl.BlockSpec((tm, HP), lambda i: (i, 0)), pl.BlockSpec((tm, HP), lambda i: (i, 0))],
        out_specs=[pl.BlockSpec((tm, NH * HP), lambda i: (i, 0)), pl.BlockSpec((tm, NH * DV), lambda i: (i, 0)),
                   pl.BlockSpec((tm, KVL), lambda i: (i, 0))],
        out_shape=[jax.ShapeDtypeStruct((TKV, NH * HP), BF), jax.ShapeDtypeStruct((TKV, NH * DV), BF),
                   jax.ShapeDtypeStruct((TKV, KVL), BF)],
        compiler_params=pltpu.CompilerParams(dimension_semantics=("parallel",)),
    )(pc, p, kvg, wkv2, ck, sk)


def qprep(p, qg, wq2, cq_t, sq_t, *, name, tm=256):
    qcol = O_Q // 512

    def body(p_ref, g_ref, w_ref, c_ref, s_ref, q_ref, cq_ref):
        pq = p_ref[...]
        r = lax.rsqrt(jnp.sum(pq * pq, axis=-1, keepdims=True) * (1.0 / QL) + EPS)
        cq = (pq * r * g_ref[...]).astype(BF)
        cq_ref[...] = cq
        q2 = jnp.dot(cq, w_ref[...], preferred_element_type=F32)
        cc = jnp.concatenate([c_ref[...]] * NH, axis=1)
        ss = jnp.concatenate([s_ref[...]] * NH, axis=1)
        q_ref[...] = (q2[:, :NH * HP] * cc + q2[:, NH * HP:] * ss).astype(BF)

    return pl.pallas_call(
        body, name=name, grid=(T // tm,),
        in_specs=[pl.BlockSpec((tm, 512), lambda i: (i, qcol)), _row(512),
                  pl.BlockSpec((512, 2 * NH * HP), lambda i: (0, 0)),
                  pl.BlockSpec((tm, HP), lambda i: (i, 0)), pl.BlockSpec((tm, HP), lambda i: (i, 0))],
        out_specs=[pl.BlockSpec((tm, NH * HP), lambda i: (i, 0)), pl.BlockSpec((tm, 512), lambda i: (i, 0))],
        out_shape=[jax.ShapeDtypeStruct((T, NH * HP), BF), jax.ShapeDtypeStruct((T, 512), BF)],
        compiler_params=pltpu.CompilerParams(dimension_semantics=("parallel",)),
    )(p, qg, wq2, cq_t, sq_t)


def _head_mask(h):
    lanes = lax.broadcasted_iota(jnp.int32, (1, 2 * DV), 1)
    return (lanes // DV) == (h % 2)


LOG2E = 1.4426950408889634


def attn_fwd(q, k, v, *, name, tq=1024, kc=768):
    def body(q_ref, k_ref, v_ref, o_ref, lse_ref):
        h = pl.program_id(1)
        qv = q_ref[...]
        m = l = acc = None
        for c in range(TKV // kc):
            s = lax.dot_general(qv, k_ref[c * kc:(c + 1) * kc, :], (((1,), (1,)), ((), ())),
                                preferred_element_type=F32) * (SCALE * LOG2E)
            mc = jnp.max(s, axis=-1, keepdims=True)
            if c == 0:
                m = mc
                e = jnp.exp2(s - m)
                l = jnp.sum(e, axis=-1, keepdims=True)
                acc = jnp.dot(e.astype(BF), v_ref[c * kc:(c + 1) * kc, :], preferred_element_type=F32)
            else:
                mn = jnp.maximum(m, mc)
                a = jnp.exp2(m - mn)
                e = jnp.exp2(s - mn)
                l = l * a + jnp.sum(e, axis=-1, keepdims=True)
                acc = acc * a + jnp.dot(e.astype(BF), v_ref[c * kc:(c + 1) * kc, :], preferred_element_type=F32)
                m = mn
        o2 = jnp.where(_head_mask(h), acc * (1.0 / l), 0.0).astype(BF)
        lse_ref[...] = jnp.broadcast_to(m + jnp.log(l) * LOG2E, (tq, HP))

        @pl.when(h % 2 == 0)
        def _():
            o_ref[...] = o2

        @pl.when(h % 2 == 1)
        def _():
            o_ref[...] = o_ref[...] + o2

    return pl.pallas_call(
        body, name=name, grid=(T // tq, NH),
        in_specs=[pl.BlockSpec((tq, HP), lambda i, h: (i, h)), pl.BlockSpec((TKV, HP), lambda i, h: (0, h)),
                  pl.BlockSpec((TKV, 2 * DV), lambda i, h: (0, h // 2))],
        out_specs=[pl.BlockSpec((tq, 2 * DV), lambda i, h: (i, h // 2)), pl.BlockSpec((tq, HP), lambda i, h: (i, h))],
        out_shape=[jax.ShapeDtypeStruct((T, NH * DV), BF), jax.ShapeDtypeStruct((T, NH * HP), F32)],
        compiler_params=pltpu.CompilerParams(dimension_semantics=("parallel", "arbitrary")),
    )(q, k, v)


def _shift_dn(x):
    n = x.shape[0]
    rows = lax.broadcasted_iota(jnp.int32, (n, 1), 0)
    return jnp.where(rows == 0, 0.0, pltpu.roll(x, 1, axis=0))


def _shift_up(x):
    n = x.shape[0]
    rows = lax.broadcasted_iota(jnp.int32, (n, 1), 0)
    return jnp.where(rows == n - 1, 0.0, pltpu.roll(x, n - 1, axis=0))


def _conv(x, w_ref, b_ref):
    return b_ref[...] + _shift_dn(x) * w_ref[0:1, :] + x * w_ref[1:2, :] + _shift_up(x) * w_ref[2:3, :]


def _conv_t(dy, w_ref):
    return _shift_up(dy) * w_ref[0:1, :] + dy * w_ref[1:2, :] + _shift_dn(dy) * w_ref[2:3, :]


def _conv_wgrad(dw_ref, dy, x):
    dw_ref[0:1, :] = jnp.sum(dy * _shift_dn(x), axis=0, keepdims=True)
    dw_ref[1:2, :] = jnp.sum(dy * x, axis=0, keepdims=True)
    dw_ref[2:3, :] = jnp.sum(dy * _shift_up(x), axis=0, keepdims=True)


def convz(p, cw, cb, *, name):
    o0 = O_CV // (3 * CVB)

    def body(p_ref, w_ref, bias_ref, z_ref):
        xv, bv, cv = p_ref[:, 0:CVB], p_ref[:, CVB:2 * CVB], p_ref[:, 2 * CVB:3 * CVB]
        z_ref[...] = (bv * _conv(cv * xv, w_ref, bias_ref)).astype(BF)

    return pl.pallas_call(
        body, name=name, grid=(CONV // CVB,),
        in_specs=[pl.BlockSpec((T, 3 * CVB), lambda j: (0, o0 + j)), pl.BlockSpec((3, CVB), lambda j: (0, j)),
                  pl.BlockSpec((1, CVB), lambda j: (0, j))],
        out_specs=pl.BlockSpec((T, CVB), lambda j: (0, j)),
        out_shape=jax.ShapeDtypeStruct((T, CONV), BF),
        compiler_params=pltpu.CompilerParams(dimension_semantics=("parallel",)),
    )(p, cw, cb)


def out_proj_merge(o, wao, z, wco, p, *, name, tm=512):
    kin = o.shape[1]

    def body(o_ref, wa_ref, z_ref, wc_ref, ga_ref, gc_ref, ya_ref, yc_ref, m_ref):
        ya = jnp.dot(o_ref[...], wa_ref[...], preferred_element_type=F32)
        yc = jnp.dot(z_ref[...], wc_ref[...], preferred_element_type=F32)
        ya_ref[...] = ya
        yc_ref[...] = yc
        m_ref[...] = (jax.nn.sigmoid(ga_ref[...]) * ya + jax.nn.sigmoid(gc_ref[...]) * yc).astype(BF)

    blk = pl.BlockSpec((tm, D), lambda i: (i, 0))
    act = pl.BlockSpec((tm, kin), lambda i: (i, 0))
    wsp = pl.BlockSpec((kin, D), lambda i: (0, 0))
    sh = jax.ShapeDtypeStruct((T, D), F32)
    return pl.pallas_call(
        body, name=name, grid=(T // tm,),
        in_specs=[act, wsp, act, wsp, pl.BlockSpec((tm, D), lambda i: (i, O_GA // D)),
                  pl.BlockSpec((tm, D), lambda i: (i, O_GC // D))],
        out_specs=[blk, blk, blk], out_shape=[sh, sh, jax.ShapeDtypeStruct((T, D), BF)],
        compiler_params=pltpu.CompilerParams(dimension_semantics=("parallel",)),
    )(o, wao, z, wco, p, p)


CONV_HALO = 8
CONV_ROWS = 256


def _row_chunks(n, chunk, carry):
    carry = chunk(0, True, False, carry)
    carry = lax.fori_loop(1, n // CONV_ROWS - 1, lambda c, a: chunk(c * CONV_ROWS, False, False, a), carry)
    return chunk(n - CONV_ROWS, False, True, carry)


def _ext_rows(ref, r0, first, last):
    n, w = ref.shape
    zero = jnp.zeros((CONV_HALO, w), ref.dtype)
    if first:
        return jnp.concatenate([zero, ref[0:CONV_ROWS + CONV_HALO, :]], axis=0)
    if last:
        return jnp.concatenate([ref[n - CONV_ROWS - CONV_HALO:n, :], zero], axis=0)
    return ref[pl.ds(pl.multiple_of(r0 - CONV_HALO, 8), CONV_ROWS + 2 * CONV_HALO), :]


def _center_rows(r0, first, last):
    return slice(r0, r0 + CONV_ROWS) if (first or last) else pl.ds(pl.multiple_of(r0, 8), CONV_ROWS)


def _roll_dn(x):
    return pltpu.roll(x, 1, axis=0)


def _roll_up(x):
    return pltpu.roll(x, x.shape[0] - 1, axis=0)


_CTR = slice(CONV_HALO, CONV_HALO + CONV_ROWS)


def ffn_act(u0, cw, cb, *, name, tc=256):
    nb = DFF // tc

    def body(u_ref, wg_ref, wv_ref, bg_ref, bv_ref, f_ref):
        wg = [wg_ref[k:k + 1, :] for k in range(3)]
        wv = [wv_ref[k:k + 1, :] for k in range(3)]
        bg, bv = bg_ref[...], bv_ref[...]

        def chunk(r0, first, last, carry):
            xg, xv = _ext_rows(u_ref.at[0], r0, first, last), _ext_rows(u_ref.at[1], r0, first, last)
            ug = bg + _roll_dn(xg) * wg[0] + xg * wg[1] + _roll_up(xg) * wg[2]
            uv = bv + _roll_dn(xv) * wv[0] + xv * wv[1] + _roll_up(xv) * wv[2]
            f_ref[_center_rows(r0, first, last), :] = (ug * jax.nn.sigmoid(ug) * uv)[_CTR].astype(BF)
            return carry

        _row_chunks(T, chunk, 0)

    return pl.pallas_call(
        body, name=name, grid=(nb,),
        in_specs=[pl.BlockSpec((2, T, tc), lambda j: (0, 0, j)),
                  pl.BlockSpec((3, tc), lambda j: (0, j)), pl.BlockSpec((3, tc), lambda j: (0, nb + j)),
                  pl.BlockSpec((1, tc), lambda j: (0, j)), pl.BlockSpec((1, tc), lambda j: (0, nb + j))],
        out_specs=pl.BlockSpec((T, tc), lambda j: (0, j)),
        out_shape=jax.ShapeDtypeStruct((T, DFF), BF),
        compiler_params=pltpu.CompilerParams(dimension_semantics=("parallel",)),
    )(u0, cw, cw, cb, cb)


def rows_call(lead, ins, in_specs, out_shape, out_specs, fn, *, name, R, tm):
    tb, a_stack, tk = lead.get("tb", False), lead.get("a_stack", False), lead["tk"]
    K = 2 * lead["a"].shape[2] if a_stack else lead["a"].shape[1]
    nk = K // tk
    deps = [] if lead.get("dep") is None else [lead["dep"]]
    n_in = len(ins)

    def body(a_ref, b_ref, *refs):
        refs = refs[len(deps):]
        in_refs, out_refs, acc = refs[:n_in], refs[n_in:-1], refs[-1]
        i, k = pl.program_id(0), pl.program_id(1)
        part = lax.dot_general(a_ref[...].astype(BF), b_ref[...].astype(BF),
                               (((1,), (1 if tb else 0,)), ((), ())), preferred_element_type=F32)
        if nk == 1:
            fn(i, part, in_refs, out_refs)
            return

        @pl.when(k == 0)
        def _():
            acc[...] = part

        @pl.when(k > 0)
        def _():
            acc[...] += part

        @pl.when(k == nk - 1)
        def _():
            fn(i, acc[...], in_refs, out_refs)

    if a_stack:
        nhb = K // 2 // tk
        a_spec = pl.BlockSpec((None, tm, tk), lambda i, k: (k // nhb, i, k % nhb))
    else:
        a_spec = pl.BlockSpec((tm, tk), lambda i, k: (i, k))
    b_spec = pl.BlockSpec((D, tk), lambda i, k: (0, k)) if tb else pl.BlockSpec((tk, D), lambda i, k: (k, 0))
    return pl.pallas_call(
        body, name=name, grid=(R // tm, nk),
        in_specs=[a_spec, b_spec] + [pl.BlockSpec(memory_space=pl.ANY)] * len(deps) + list(in_specs),
        out_specs=out_specs, out_shape=out_shape,
        scratch_shapes=[pltpu.VMEM((tm, D) if nk > 1 else (8, 128), F32)],
        compiler_params=pltpu.CompilerParams(dimension_semantics=("arbitrary", "arbitrary")),
    )(lead["a"], lead["b"], *deps, *ins)


def _rblk(tm, w=D, col=0):
    return pl.BlockSpec((tm, w), lambda i, k: (i, col))


def _rrow(w=D):
    return pl.BlockSpec((1, w), lambda i, k: (0, 0))


def down_final(f, wdn, x1, g2, fg, tgt, *, name, tm=512):
    def fn(i, d, in_refs, out_refs):
        x1_ref, g2_ref, fg_ref, t_ref = in_refs
        d_ref, dx_ref, dd_ref, dfg_ref, loss_ref = out_refs
        d_ref[...] = d
        xv = x1_ref[...] + g2_ref[...] * d
        r = lax.rsqrt(jnp.mean(xv * xv, axis=-1, keepdims=True) + EPS)
        xh = xv * r
        diff = xh * fg_ref[...] - t_ref[...]
        part = 0.5 * jnp.sum(jnp.mean(diff * diff, axis=-1, keepdims=True), axis=0, keepdims=True)
        dy = diff * (1.0 / D)
        a = dy * fg_ref[...]
        dx = r * (a - xh * jnp.mean(a * xh, axis=-1, keepdims=True))
        dx_ref[...] = dx
        dd_ref[...] = (dx * g2_ref[...]).astype(BF)
        dfg = jnp.sum(dy * xh, axis=0, keepdims=True)

        @pl.when(i == 0)
        def _():
            dfg_ref[...] = dfg
            loss_ref[...] = jnp.broadcast_to(part, (1, 128))

        @pl.when(i > 0)
        def _():
            dfg_ref[...] += dfg
            loss_ref[...] += jnp.broadcast_to(part, (1, 128))

    blk = _rblk(tm)
    return rows_call(
        dict(a=f, b=wdn, tk=DFF), [x1, g2, fg, tgt], [blk, _rrow(), _rrow(), blk],
        [jax.ShapeDtypeStruct((T, D), F32), jax.ShapeDtypeStruct((T, D), F32), jax.ShapeDtypeStruct((T, D), BF),
         jax.ShapeDtypeStruct((1, D), F32), jax.ShapeDtypeStruct((1, 128), F32)],
        [blk, blk, blk, _rrow(), _rrow(128)], fn, name=name, R=T, tm=tm)


def oproj_resid(merged, wo, x, gate, g, sc, sh, *, name, tm=512):
    def fn(i, a, in_refs, out_refs):
        x_ref, gate_ref, g_ref, sc_ref, sh_ref = in_refs
        a_ref, x1_ref, h_ref = out_refs
        a_ref[...] = a
        xv = x_ref[...] + gate_ref[...] * a
        x1_ref[...] = xv
        r = lax.rsqrt(jnp.mean(xv * xv, axis=-1, keepdims=True) + EPS)
        h_ref[...] = ((xv * r * g_ref[...]) * (1.0 + sc_ref[...]) + sh_ref[...]).astype(BF)

    blk = _rblk(tm)
    return rows_call(
        dict(a=merged, b=wo, tk=D), [x, gate, g, sc, sh], [blk, _rrow(), _rrow(), _rrow(), _rrow()],
        [jax.ShapeDtypeStruct((T, D), F32), jax.ShapeDtypeStruct((T, D), F32), jax.ShapeDtypeStruct((T, D), BF)],
        [blk, blk, blk], fn, name=name, R=T, tm=tm)


def oproj_dx_gate_bwd(da, wo, p, ya, yc, wao, wco, *, name, tm=512):
    kin = wao.shape[0]

    def fn(i, dm, in_refs, out_refs):
        ga_ref, gc_ref, ya_ref, yc_ref, wa_ref, wc_ref = in_refs
        dya_ref, dyc_ref, dp_ref, do_ref, dz_ref = out_refs
        sa, sc_ = jax.nn.sigmoid(ga_ref[...]), jax.nn.sigmoid(gc_ref[...])
        dya, dyc = (dm * sa).astype(BF), (dm * sc_).astype(BF)
        dya_ref[...] = dya
        dyc_ref[...] = dyc
        dp_ref[:, 0:D] = (dm * ya_ref[...] * (sa * (1.0 - sa))).astype(BF)
        dp_ref[:, D:2 * D] = (dm * yc_ref[...] * (sc_ * (1.0 - sc_))).astype(BF)
        nt = (((1,), (1,)), ((), ()))
        do_ref[...] = lax.dot_general(dya, wa_ref[...], nt, preferred_element_type=F32).astype(BF)
        dz_ref[...] = lax.dot_general(dyc, wc_ref[...], nt, preferred_element_type=F32)

    blk = _rblk(tm)
    sh = jax.ShapeDtypeStruct((T, D), BF)
    wsp = pl.BlockSpec((kin, D), lambda i, k: (0, 0))
    return rows_call(
        dict(a=da, b=wo, tb=True, tk=D), [p, p, ya, yc, wao, wco],
        [_rblk(tm, D, O_GA // D), _rblk(tm, D, O_GC // D), blk, blk, wsp, wsp],
        [sh, sh, jax.ShapeDtypeStruct((T, NIN), BF), jax.ShapeDtypeStruct((T, kin), BF), jax.ShapeDtypeStruct((T, kin), F32)],
        [blk, blk, _rblk(tm, 2 * D), _rblk(tm, kin), _rblk(tm, kin)], fn, name=name, R=T, tm=tm)


def normmod_bwd(x, dh, g, sc, dres, gsrc, gate, *, name, tm=512):
    R = x.shape[0]
    tm = min(tm, R)
    has_res = dres is not None
    fused = isinstance(dh, dict)
    if fused:
        tb, a_stack, tk = dh.get("tb", False), dh.get("a_stack", False), dh["tk"]
        K = 2 * dh["a"].shape[2] if a_stack else dh["a"].shape[1]
        nk = K // tk
        deps = [] if dh.get("dep") is None else [dh["dep"]]
        n_dh = 2 + len(deps)
    else:
        nk, n_dh = 1, 1

    def elementwise(i, dhv, x_ref, g_ref, sc_ref, res_refs, out_refs):
        xv = x_ref[...]
        r = lax.rsqrt(jnp.mean(xv * xv, axis=-1, keepdims=True) + EPS)
        xh = xv * r
        n = xh * g_ref[...]
        dn = dhv * (1.0 + sc_ref[...])
        a = dn * g_ref[...]
        rows = [jnp.sum(dhv, axis=0, keepdims=True), jnp.sum(dhv * n, axis=0, keepdims=True),
                jnp.sum(dn * xh, axis=0, keepdims=True)]
        if has_res:
            dres_ref, gsrc_ref, gate_ref = res_refs
            dx_ref, dxg_ref, st_ref = out_refs
            dr = dres_ref[...]
            dx = dr + r * (a - xh * jnp.mean(a * xh, axis=-1, keepdims=True))
            dx_ref[...] = dx
            dxg_ref[...] = (dx * gate_ref[...]).astype(BF)
            rows.append(jnp.sum(dr * gsrc_ref[...], axis=0, keepdims=True))
        else:
            st_ref, = out_refs
            rows.append(jnp.zeros((1, D), F32))

        @pl.when(i == 0)
        def _():
            for k, row in enumerate(rows):
                st_ref[k:k + 1, :] = row

        @pl.when(i > 0)
        def _():
            for k, row in enumerate(rows):
                st_ref[k:k + 1, :] += row

    def body(*refs):
        x_ref, dh_refs, g_ref, sc_ref = refs[0], refs[1:1 + n_dh], refs[1 + n_dh], refs[2 + n_dh]
        rest = refs[3 + n_dh:]
        res_refs, rest = (rest[:3], rest[3:]) if has_res else ((), rest)
        out_refs = rest[:3] if has_res else rest[:1]
        i = pl.program_id(0)
        if not fused:
            elementwise(i, dh_refs[0][...], x_ref, g_ref, sc_ref, res_refs, out_refs)
            return
        acc = rest[-1]
        k = pl.program_id(1)
        part = lax.dot_general(dh_refs[0][...].astype(BF), dh_refs[1][...].astype(BF),
                               (((1,), (1 if tb else 0,)), ((), ())), preferred_element_type=F32)
        if nk == 1:
            elementwise(i, part, x_ref, g_ref, sc_ref, res_refs, out_refs)
            return

        @pl.when(k == 0)
        def _():
            acc[...] = part

        @pl.when(k > 0)
        def _():
            acc[...] += part

        @pl.when(k == nk - 1)
        def _():
            elementwise(i, acc[...], x_ref, g_ref, sc_ref, res_refs, out_refs)

    rowb = lambda w: pl.BlockSpec((1, w), lambda i, *k: (0, 0))
    blk = pl.BlockSpec((tm, D), lambda i, *k: (i, 0))
    st_spec = pl.BlockSpec((4, D), lambda i, *k: (0, 0))
    st_shape = jax.ShapeDtypeStruct((4, D), F32)
    if fused:
        if a_stack:
            nhb = K // 2 // tk
            a_spec = pl.BlockSpec((None, tm, tk), lambda i, k: (k // nhb, i, k % nhb))
        else:
            a_spec = pl.BlockSpec((tm, tk), lambda i, k: (i, k))
        b_spec = pl.BlockSpec((D, tk), lambda i, k: (0, k)) if tb else pl.BlockSpec((tk, D), lambda i, k: (k, 0))
        dh_specs = [a_spec, b_spec] + [pl.BlockSpec(memory_space=pl.ANY)] * len(deps)
        dh_args = [dh["a"], dh["b"]] + deps
        grid, sem = (R // tm, nk), ("arbitrary", "arbitrary")
        scratch = [pltpu.VMEM((tm, D) if nk > 1 else (8, 128), F32)]
    else:
        dh_specs, dh_args, grid, sem, scratch = [blk], [dh], (R // tm,), ("arbitrary",), []
    cp = pltpu.CompilerParams(dimension_semantics=sem)
    if has_res:
        return pl.pallas_call(
            body, name=name, grid=grid, in_specs=[blk] + dh_specs + [rowb(D), rowb(D), blk, blk, rowb(D)],
            out_specs=[blk, blk, st_spec], scratch_shapes=scratch,
            out_shape=[jax.ShapeDtypeStruct((R, D), F32), jax.ShapeDtypeStruct((R, D), BF), st_shape],
            compiler_params=cp,
        )(x, *dh_args, g, sc, dres, gsrc, gate)
    return pl.pallas_call(
        body, name=name, grid=grid, in_specs=[blk] + dh_specs + [rowb(D), rowb(D)],
        out_specs=st_spec, out_shape=st_shape, scratch_shapes=scratch, compiler_params=cp,
    )(x, *dh_args, g, sc)


def ffn_act_bwd(u0, df, cw, cb, *, name, tc=128):
    nb = DFF // tc

    def body(u_ref, df_ref, wg_ref, wv_ref, bg_ref, bv_ref, du_ref, dw_ref, db_ref):
        wg = [wg_ref[k:k + 1, :] for k in range(3)]
        wv = [wv_ref[k:k + 1, :] for k in range(3)]
        bg, bv = bg_ref[...], bv_ref[...]

        def chunk(r0, first, last, acc):
            xg, xv = _ext_rows(u_ref.at[0], r0, first, last), _ext_rows(u_ref.at[1], r0, first, last)
            dfe = _ext_rows(df_ref, r0, first, last)
            xg_d, xg_u, xv_d, xv_u = _roll_dn(xg), _roll_up(xg), _roll_dn(xv), _roll_up(xv)
            ug = bg + xg_d * wg[0] + xg * wg[1] + xg_u * wg[2]
            uv = bv + xv_d * wv[0] + xv * wv[1] + xv_u * wv[2]
            sig = jax.nn.sigmoid(ug)
            dug = dfe * uv * (sig * (1.0 + ug * (1.0 - sig)))
            duv = dfe * (ug * sig)
            rows = _center_rows(r0, first, last)
            du_ref[0, rows, :] = (_roll_up(dug) * wg[0] + dug * wg[1] + _roll_dn(dug) * wg[2])[_CTR].astype(BF)
            du_ref[1, rows, :] = (_roll_up(duv) * wv[0] + duv * wv[1] + _roll_dn(duv) * wv[2])[_CTR].astype(BF)
            terms = [dug * xg_d, dug * xg, dug * xg_u, dug, duv * xv_d, duv * xv, duv * xv_u, duv]
            return tuple(a + jnp.sum(t[_CTR], axis=0, keepdims=True) for a, t in zip(acc, terms))

        acc = _row_chunks(T, chunk, tuple(jnp.zeros((1, tc), F32) for _ in range(8)))
        for k in range(3):
            dw_ref[0, k:k + 1, :] = acc[k]
            dw_ref[1, k:k + 1, :] = acc[4 + k]
        db_ref[0] = acc[3]
        db_ref[1] = acc[7]

    lo = lambda r: pl.BlockSpec((r, tc), lambda j: (0, j))
    hi = lambda r: pl.BlockSpec((r, tc), lambda j: (0, nb + j))
    st = lambda r: pl.BlockSpec((2, r, tc), lambda j: (0, 0, j))
    return pl.pallas_call(
        body, name=name, grid=(nb,),
        in_specs=[st(T), lo(T), lo(3), hi(3), lo(1), hi(1)],
        out_specs=[st(T), st(3), st(1)],
        out_shape=[jax.ShapeDtypeStruct((2, T, DFF), BF), jax.ShapeDtypeStruct((2, 3, DFF), F32),
                   jax.ShapeDtypeStruct((2, 1, DFF), F32)],
        compiler_params=pltpu.CompilerParams(dimension_semantics=("parallel",)),
    )(u0, df, cw, cw, cb, cb)


def convz_bwd(p, dz, cw, cb, dp, *, name):
    o0 = O_CV // (3 * CVB)

    def body(p_ref, dz_ref, w_ref, bias_ref, dp_in, dp_ref, dw_ref, dbias_ref):
        xv, bv, cv = p_ref[:, 0:CVB], p_ref[:, CVB:2 * CVB], p_ref[:, 2 * CVB:3 * CVB]
        ci = cv * xv
        dwc = _conv(ci, w_ref, bias_ref)
        dzv = dz_ref[...]
        ddw = dzv * bv
        dci = _conv_t(ddw, w_ref)
        dp_ref[:, 0:CVB] = (dci * cv).astype(BF)
        dp_ref[:, CVB:2 * CVB] = (dzv * dwc).astype(BF)
        dp_ref[:, 2 * CVB:3 * CVB] = (dci * xv).astype(BF)
        _conv_wgrad(dw_ref, ddw, ci)
        dbias_ref[...] = jnp.sum(ddw, axis=0, keepdims=True)

    own = lambda r: pl.BlockSpec((r, CVB), lambda j: (0, j))
    return pl.pallas_call(
        body, name=name, grid=(CONV // CVB,),
        in_specs=[pl.BlockSpec((T, 3 * CVB), lambda j: (0, o0 + j)), own(T), own(3), own(1),
                  pl.BlockSpec(memory_space=pl.ANY)],
        out_specs=[pl.BlockSpec((T, 3 * CVB), lambda j: (0, o0 + j)), own(3), own(1)],
        out_shape=[jax.ShapeDtypeStruct((T, NIN), BF), jax.ShapeDtypeStruct((3, CONV), F32),
                   jax.ShapeDtypeStruct((1, CONV), F32)],
        input_output_aliases={4: 0},
        compiler_params=pltpu.CompilerParams(dimension_semantics=("parallel",)),
    )(p, dz, cw, cb, dp)


def attn_bwd(q, k, v, do, o, lse, dep, *, name, tq=1024, kc=768):
    NKC, KC = TKV // kc, kc
    deps = [] if dep is None else [dep]

    def body(q_ref, k_ref, v_ref, do_ref, o_ref, lse_ref, *rest):
        dq_ref, dk_ref, dv_ref = rest[len(deps):]
        h, i = pl.program_id(0), pl.program_id(1)

        @pl.when(i == 0)
        def _():
            dk_ref[...] = jnp.zeros_like(dk_ref)

        @pl.when((i == 0) & (h % 2 == 0))
        def _():
            dv_ref[...] = jnp.zeros_like(dv_ref)

        qv = q_ref[...]
        dom = jnp.where(_head_mask(h), do_ref[...], jnp.zeros_like(do_ref[...]))
        delta = jnp.sum(dom.astype(F32) * o_ref[...].astype(F32), axis=-1, keepdims=True)
        lse = lse_ref[:, 0:1]
        dq = jnp.zeros((tq, HP), F32)
        for c in range(NKC):
            cols = slice(c * KC, (c + 1) * KC)
            s = lax.dot_general(qv, k_ref[cols, :], (((1,), (1,)), ((), ())),
                                preferred_element_type=F32) * (SCALE * LOG2E)
            pr = jnp.exp2(s - lse)
            dp = lax.dot_general(dom, v_ref[cols, :], (((1,), (1,)), ((), ())), preferred_element_type=F32)
            ds = (pr * (dp - delta) * SCALE).astype(BF)
            dq = dq + jnp.dot(ds, k_ref[cols, :], preferred_element_type=F32)
            dk_ref[cols, :] += lax.dot_general(ds, qv, (((0,), (0,)), ((), ())), preferred_element_type=F32)
            dv_ref[cols, :] += lax.dot_general(pr.astype(BF), dom, (((0,), (0,)), ((), ())), preferred_element_type=F32)
        dq_ref[...] = dq

    return pl.pallas_call(
        body, name=name, grid=(NH, T // tq),
        in_specs=[pl.BlockSpec((tq, HP), lambda h, i: (i, h)), pl.BlockSpec((TKV, HP), lambda h, i: (0, h)),
                  pl.BlockSpec((TKV, 2 * DV), lambda h, i: (0, h // 2)), pl.BlockSpec((tq, 2 * DV), lambda h, i: (i, h // 2)),
                  pl.BlockSpec((tq, 2 * DV), lambda h, i: (i, h // 2)), pl.BlockSpec((tq, HP), lambda h, i: (i, h)),
                  *([pl.BlockSpec(memory_space=pl.ANY)] * len(deps))],
        out_specs=[pl.BlockSpec((tq, HP), lambda h, i: (i, h)), pl.BlockSpec((TKV, HP), lambda h, i: (0, h)),
                   pl.BlockSpec((TKV, 2 * DV), lambda h, i: (0, h // 2))],
        out_shape=[jax.ShapeDtypeStruct((T, NH * HP), F32), jax.ShapeDtypeStruct((TKV, NH * HP), F32),
                   jax.ShapeDtypeStruct((TKV, NH * DV), F32)],
        compiler_params=pltpu.CompilerParams(dimension_semantics=("arbitrary", "arbitrary")),
    )(q, k, v, do, o, lse, *deps)


def qprep_bwd(p, dq, qg, wq2, cq_t, sq_t, dp, *, name, tm=256):
    qcol = O_Q // 512

    def body(p_ref, dq_ref, g_ref, w_ref, c_ref, s_ref, dp_in, dp_ref, dq2_ref, dg_ref):
        i = pl.program_id(0)
        dqv = dq_ref[...]
        cc = jnp.concatenate([c_ref[...]] * NH, axis=1)
        ss = jnp.concatenate([s_ref[...]] * NH, axis=1)
        dq2 = jnp.concatenate([dqv * cc, dqv * ss], axis=1).astype(BF)
        dq2_ref[...] = dq2
        dcq = lax.dot_general(dq2, w_ref[...], (((1,), (1,)), ((), ())), preferred_element_type=F32)
        pq = p_ref[...]
        r = lax.rsqrt(jnp.sum(pq * pq, axis=-1, keepdims=True) * (1.0 / QL) + EPS)
        xh = pq * r
        a = dcq * g_ref[...]
        dp_ref[...] = (r * (a - xh * (jnp.sum(a * xh, axis=-1, keepdims=True) * (1.0 / QL)))).astype(BF)
        dg = jnp.sum(dcq * xh, axis=0, keepdims=True)

        @pl.when(i == 0)
        def _():
            dg_ref[...] = dg

        @pl.when(i > 0)
        def _():
            dg_ref[...] += dg

    return pl.pallas_call(
        body, name=name, grid=(T // tm,),
        in_specs=[pl.BlockSpec((tm, 512), lambda i: (i, qcol)), pl.BlockSpec((tm, NH * HP), lambda i: (i, 0)), _row(512),
                  pl.BlockSpec((512, 2 * NH * HP), lambda i: (0, 0)),
                  pl.BlockSpec((tm, HP), lambda i: (i, 0)), pl.BlockSpec((tm, HP), lambda i: (i, 0)),
                  pl.BlockSpec(memory_space=pl.ANY)],
        out_specs=[pl.BlockSpec((tm, 512), lambda i: (i, qcol)), pl.BlockSpec((tm, 2 * NH * HP), lambda i: (i, 0)), _row(512)],
        out_shape=[jax.ShapeDtypeStruct((T, NIN), BF), jax.ShapeDtypeStruct((T, 2 * NH * HP), BF),
                   jax.ShapeDtypeStruct((1, 512), F32)],
        input_output_aliases={6: 0},
        compiler_params=pltpu.CompilerParams(dimension_semantics=("arbitrary",)),
    )(p, dq, qg, wq2, cq_t, sq_t, dp)


def kvprep_bwd(pc, p, dk, dv, kvg, wkv2, ck, sk, dp, *, name, tm=256):
    assert tm == TC
    nb = TKV // tm
    kvcol = O_KV // 512

    def body(pc_ref, p_ref, dk_ref, dv_ref, g_ref, w_ref, ck_ref, sk_ref, dp_in, dp_ref, dpc_ref, dkv2_ref, dg_ref):
        i = pl.program_id(0)
        t = jnp.where(i == NLAT, pc_ref[...], p_ref[...])
        pk = t[:, :KVL]
        r = lax.rsqrt(jnp.mean(pk * pk, axis=-1, keepdims=True) + EPS)
        xh = pk * r
        dkv = dk_ref[...]
        dkv2 = jnp.concatenate([dkv, dv_ref[...]], axis=1).astype(BF)
        dkv2_ref[...] = dkv2
        dckv = lax.dot_general(dkv2, w_ref[...], (((1,), (1,)), ((), ())), preferred_element_type=F32)
        a = dckv * g_ref[...]
        dpk = r * (a - xh * jnp.mean(a * xh, axis=-1, keepdims=True))
        dkr = dkv[:, 0:HP]
        for hh in range(1, NH):
            dkr = dkr + dkv[:, hh * HP:(hh + 1) * HP]
        res = jnp.concatenate([dpk, dkr * ck_ref[...], dkr * sk_ref[...]], axis=1).astype(BF)
        dg = jnp.sum(dckv * xh, axis=0, keepdims=True)

        @pl.when(i == 0)
        def _():
            dg_ref[...] = dg

        @pl.when(i > 0)
        def _():
            dg_ref[...] += dg

        @pl.when(i < NLAT)
        def _():
            dp_ref[...] = res

        @pl.when(i == NLAT)
        def _():
            dpc_ref[...] = res

    rb = lambda w: pl.BlockSpec((tm, w), lambda i: (i, 0))
    return pl.pallas_call(
        body, name=name, grid=(nb,),
        in_specs=[pl.BlockSpec((tm, 512), lambda i: (0, 0)),
                  pl.BlockSpec((tm, 512), lambda i: (jnp.minimum(i, NLAT - 1), kvcol)),
                  rb(NH * HP), rb(NH * DV), _row(KVL), pl.BlockSpec((KVL, NH * HP + NH * DV), lambda i: (0, 0)),
                  rb(HP), rb(HP), pl.BlockSpec(memory_space=pl.ANY)],
        out_specs=[pl.BlockSpec((tm, 512), lambda i: (jnp.minimum(i, NLAT - 1), kvcol)),
                   pl.BlockSpec((tm, 512), lambda i: (0, 0)), rb(NH * HP + NH * DV), _row(KVL)],
        out_shape=[jax.ShapeDtypeStruct((T, NIN), BF), jax.ShapeDtypeStruct((TC, 512), BF),
                   jax.ShapeDtypeStruct((TKV, NH * HP + NH * DV), BF), jax.ShapeDtypeStruct((1, KVL), F32)],
        input_output_aliases={8: 0},
        compiler_params=pltpu.CompilerParams(dimension_semantics=("arbitrary",)),
    )(pc, p, dk, dv, kvg, wkv2, ck, sk, dp)


def _pieces(src, width, n):
    out, c = [], src
    while c < src + width:
        k = c // n
        w = min(src + width, (k + 1) * n) - c
        out.append((k, c - k * n, c - src, w))
        c += w
    return out


def _win_moves():
    mv = [(2208, 1024, O_GA), (3232, 1024, O_GC), (0, KVL, O_KV), (256, DR, O_KV + KVL + DN), (288, QL, O_Q)]
    mv += [(256 + _swap_start(g), 8, O_KV + KVL + HP + DN + 8 * g) for g in range(4)]
    for j in range(CONV // CVB):
        base = O_CV + 3 * CVB * j
        mv += [(672 + CVB * j, CVB, base), (1184 + CVB * j, CVB, base + CVB), (1696 + CVB * j, CVB, base + 2 * CVB)]
    return mv


_WIN_ZERO = [(O_KV + KVL, DN), (O_KV + KVL + DN + DR, HP - DN - DR), (O_KV + KVL + HP, DN),
             (O_KV + KVL + HP + DN + DR, HP - DN - DR), (O_Q + QL, 512 - QL)]


def build_win(g, *, name, tm=256):
    def body(g_ref, o_ref):
        for src, w, dst in _win_moves():
            for k, a, off, pw in _pieces(src, w, SH_IN):
                o_ref[:, dst + off:dst + off + pw] = g_ref[k, :, a:a + pw]
        for c0, w in _WIN_ZERO:
            o_ref[:, c0:c0 + w] = jnp.zeros((tm, w), o_ref.dtype)

    return pl.pallas_call(
        body, name=name, grid=(D // tm,), in_specs=[pl.BlockSpec((NDEV, tm, SH_IN), lambda i: (0, i, 0))],
        out_specs=pl.BlockSpec((tm, NIN), lambda i: (i, 0)), out_shape=jax.ShapeDtypeStruct((D, NIN), g.dtype),
        compiler_params=pltpu.CompilerParams(dimension_semantics=("parallel",)),
    )(g)


def shard_win_grad(dwt, dwct, *, name, tc=256):
    def body(dw_ref, dwc_ref, o_ref, kvs):
        kvs[...] = dw_ref[O_KV:O_KV + 512, :] + dwc_ref[...]

        def src(row, w):
            if O_KV <= row < O_KV + 512:
                return kvs[row - O_KV:row - O_KV + w, :]
            return dw_ref[row:row + w, :]

        for s, w, dst in _win_moves():
            if w == 8 or s == 256:
                continue
            for k, a, off, pw in _pieces(s, w, SH_IN):
                o_ref[k, a:a + pw, :] = src(dst + off, pw).astype(o_ref.dtype)
        for g in range(4):
            val = src(O_KV + KVL + DN + 8 * g, 8) + src(O_KV + KVL + HP + DN + _swap_start(g), 8)
            o_ref[0, 256 + 8 * g:256 + 8 * g + 8, :] = val.astype(o_ref.dtype)

    return pl.pallas_call(
        body, name=name, grid=(D // tc,),
        in_specs=[pl.BlockSpec((NIN, tc), lambda j: (0, j)), pl.BlockSpec((512, tc), lambda j: (0, j))],
        out_specs=pl.BlockSpec((NDEV, SH_IN, tc), lambda j: (0, 0, j)),
        out_shape=jax.ShapeDtypeStruct((NDEV, SH_IN, D), BF),
        scratch_shapes=[pltpu.VMEM((512, tc), F32)],
        compiler_params=pltpu.CompilerParams(dimension_semantics=("parallel",)),
    )(dwt, dwct)


def _eye(n, m):
    return (lax.broadcasted_iota(jnp.int32, (n, m), 0) == lax.broadcasted_iota(jnp.int32, (n, m), 1)).astype(BF)


_NT = (((1,), (1,)), ((), ()))


def build_wq_wkv(gq, gkv, *, name):
    def body(gq_ref, gkv_ref, q_ref, kv_ref):
        q_ref[...] = jnp.zeros_like(q_ref)
        kv_ref[...] = jnp.zeros_like(kv_ref)
        eye = _eye(QL, QL)
        for h in range(NH):
            qh = lax.dot_general(eye, gq_ref[h], _NT, preferred_element_type=F32).astype(q_ref.dtype)
            q_ref[0:QL, h * HP:h * HP + DN + DR] = qh
            for g in range(4):
                c0 = NH * HP + h * HP + DN + 8 * g
                q_ref[0:QL, c0:c0 + 8] = qh[:, DN + _swap_start(g):DN + _swap_start(g) + 8]
            kv_ref[:, h * HP:h * HP + DN] = gkv_ref[h, :, 0:DN]
            kv_ref[:, NH * HP + h * DV:NH * HP + (h + 1) * DV] = gkv_ref[h, :, DN:DN + DV]

    vm = pl.BlockSpec(memory_space=pltpu.VMEM)
    return pl.pallas_call(
        body, name=name, in_specs=[vm, vm], out_specs=[vm, vm],
        out_shape=[jax.ShapeDtypeStruct((512, 2 * NH * HP), gq.dtype), jax.ShapeDtypeStruct((KVL, NH * HP + NH * DV), gq.dtype)],
    )(gq, gkv)


def shard_wq_wkv_grad(dwq2, dwkv2, *, name):
    def body(q_ref, kv_ref, gq_ref, gkv_ref, xs):
        xs[...] = jnp.zeros_like(xs)
        eye = _eye(DN + DR, HP)
        for h in range(NH):
            xs[:, 0:DN] = q_ref[0:QL, h * HP:h * HP + DN].astype(BF)
            for g in range(4):
                a = q_ref[0:QL, h * HP + DN + 8 * g:h * HP + DN + 8 * g + 8]
                c0 = NH * HP + h * HP + DN + _swap_start(g)
                xs[:, DN + 8 * g:DN + 8 * g + 8] = (a + q_ref[0:QL, c0:c0 + 8]).astype(BF)
            gq_ref[h] = lax.dot_general(eye, xs[...], _NT, preferred_element_type=F32).astype(BF)
            gkv_ref[h, :, 0:DN] = kv_ref[:, h * HP:h * HP + DN].astype(BF)
            gkv_ref[h, :, DN:DN + DV] = kv_ref[:, NH * HP + h * DV:NH * HP + (h + 1) * DV].astype(BF)

    vm = pl.BlockSpec(memory_space=pltpu.VMEM)
    return pl.pallas_call(
        body, name=name, in_specs=[vm, vm], out_specs=[vm, vm],
        out_shape=[jax.ShapeDtypeStruct((NDEV, DN + DR, QL), BF), jax.ShapeDtypeStruct((NDEV, KVL, DN + DV), BF)],
        scratch_shapes=[pltpu.VMEM((QL, HP), BF)],
    )(dwq2, dwkv2)


def unshard_cols(g, *, name, tm=256):
    _, K, n = g.shape
    tm = _pick(K, tm, 16)

    def body(g_ref, o_ref):
        for k in range(NDEV):
            o_ref[:, k * n:(k + 1) * n] = g_ref[k]

    return pl.pallas_call(
        body, name=name, grid=(K // tm,), in_specs=[pl.BlockSpec((NDEV, tm, n), lambda i: (0, i, 0))],
        out_specs=pl.BlockSpec((tm, NDEV * n), lambda i: (i, 0)), out_shape=jax.ShapeDtypeStruct((K, NDEV * n), g.dtype),
        compiler_params=pltpu.CompilerParams(dimension_semantics=("parallel",)),
    )(g)


def shard_cols(w, *, name, tm=256):
    K, n8 = w.shape
    n = n8 // NDEV
    tm = _pick(K, tm, 16)

    def body(w_ref, o_ref):
        for k in range(NDEV):
            o_ref[k] = w_ref[:, k * n:(k + 1) * n]

    return pl.pallas_call(
        body, name=name, grid=(K // tm,), in_specs=[pl.BlockSpec((tm, n8), lambda i: (i, 0))],
        out_specs=pl.BlockSpec((NDEV, tm, n), lambda i: (0, i, 0)), out_shape=jax.ShapeDtypeStruct((NDEV, K, n), w.dtype),
        compiler_params=pltpu.CompilerParams(dimension_semantics=("parallel",)),
    )(w)


def _rope_tables():
    t = np.arange(T)
    row = (t // GRID_W).astype(np.float32)
    col = (t % GRID_W).astype(np.float32)
    axis_dim = DR // 2
    inv = (np.float32(ROPE_THETA) ** (-np.arange(0, axis_dim, 2, dtype=np.float32) / np.float32(axis_dim))).astype(np.float32)
    ar, ac = (row[:, None] * inv).astype(np.float32), (col[:, None] * inv).astype(np.float32)
    cosv = np.concatenate([np.cos(ar), np.cos(ar), np.cos(ac), np.cos(ac)], axis=1).astype(np.float32)
    sinv = np.concatenate([-np.sin(ar), np.sin(ar), -np.sin(ac), np.sin(ac)], axis=1).astype(np.float32)
    ck = np.zeros((TKV, HP), np.float32)
    sk = np.zeros((TKV, HP), np.float32)
    ck[T:, DN:DN + DR] = 1.0
    ck[:T, DN:DN + DR] = cosv
    sk[:T, DN:DN + DR] = sinv
    cq = np.zeros((T, HP), np.float32)
    cq[:, :DN] = 1.0
    cq[:, DN:DN + DR] = cosv
    return jnp.asarray(ck), jnp.asarray(sk), jnp.asarray(cq), jnp.asarray(sk[:T])


def _local_step(x, ctx, tgt, mod_lat, mod_ctx, n1g, qg, kvg, n2g, fg, conv_w, conv_b, ffn_w, ffn_b, get_w, put_g, dep0):
    sh1, sc1, g1, sh2, sc2, g2 = [mod_lat[:, i * D:(i + 1) * D] for i in range(6)]
    csh1, csc1 = mod_ctx[:, 0:D], mod_ctx[:, D:2 * D]
    ck, sk, cq_t, sq_t = _rope_tables()
    qg_p = jnp.pad(qg, ((0, 0), (0, 512 - QL)))

    hcat = normmod_cat(ctx, x, n1g, csc1, csh1, sc1, sh1, dep0, name="normmod1")
    win = get_w("in", hcat)
    p = mm(hcat, win, M=T, tn=768, name="in_proj")
    pc = mm(hcat, win, M=TC, N=512, a_off=(T, 0), b_off=(0, O_KV), name="in_proj_ctx")
    wq2, wkv2, wao, wco, wo = get_w("mid", p)
    kh, vh, ckv = kvprep(pc, p, kvg, wkv2, ck, sk, name="kvprep")
    qr, cq = qprep(p, qg_p, wq2, cq_t, sq_t, name="qprep")
    o, lse = attn_fwd(qr, kh, vh, name="attn_fwd")
    z = convz(p, conv_w, conv_b, name="convz")
    ya, yc, merged = out_proj_merge(o, wao, z, wco, p, name="attn_conv_out_gate_merge")
    a_out, x1, h2 = oproj_resid(merged, wo, x, g1, n2g, sc2, sh2, name="o_proj_resid_normmod2")
    wup = get_w("up", h2)
    u0 = mm(h2, wup, tb=True, o_stack=True, tn=1408, name="up_proj")
    f = ffn_act(u0, ffn_w, ffn_b, name="ffn_act")
    wdn = get_w("down", f)
    dn, dx2, dd, dfg, loss = down_final(f, wdn, x1, g2, fg, tgt, name="down_proj_final_loss")

    df = mm(dd, wdn, tb=True, tn=1408, name="down_proj_dx")
    dwdn = mm(f, dd, ta=True, out_dtype=BF, tm=1408, name="down_proj_dw")
    du0, dffn_w, dffn_b = ffn_act_bwd(u0, df, ffn_w, ffn_b, name="ffn_act_bwd")
    dwup = mm(du0, h2, ta=True, a_stack=True, out_dtype=BF, tm=1408, name="up_proj_dw")
    tok = put_g("ffn", dict(dwup=dwup, dwdn=dwdn))
    dx1, da, st2 = normmod_bwd(x1, dict(a=du0, b=wup, a_stack=True, tk=DFF, dep=tok), n2g, sc2, dx2, dn, g1,
                               name="up_proj_dx_normmod2_bwd")

    dwo = mm(merged, da, ta=True, out_dtype=BF, tn=512, name="o_proj_dw")
    dya, dyc, dp, do, dz = oproj_dx_gate_bwd(da, wo, p, ya, yc, wao, wco, name="o_proj_dx_gate_merge_bwd")
    dwao = mm(o, dya, ta=True, out_dtype=BF, tn=512, name="attn_out_dw")
    dwco = mm(z, dyc, ta=True, out_dtype=BF, tn=512, name="conv_out_dw")
    tok = put_g("mid", dict(dwao=dwao, dwco=dwco, dwo=dwo))
    dp, dconv_w, dconv_b = convz_bwd(p, dz, conv_w, conv_b, dp, name="convz_bwd")
    dq, dk, dv = attn_bwd(qr, kh, vh, do, o, lse, tok, name="attn_bwd")
    dp, dq2, dqg = qprep_bwd(p, dq, qg_p, wq2, cq_t, sq_t, dp, name="qprep_bwd")
    dp, dpc, dkv2, dkvg = kvprep_bwd(pc, p, dk, dv, kvg, wkv2, ck, sk, dp, name="kvprep_bwd")

    dwin = mm(dp, hcat, ta=True, K=T, tm=768, name="in_proj_dw")
    dwin_c = mm(dpc, hcat, ta=True, K=TC, b_off=(T, 0), name="in_proj_ctx_dw")
    tok = put_g("in", dict(dwin=dwin, dwin_c=dwin_c))
    dwq2 = mm(cq, dq2, ta=True, dep=tok, name="q_up_dw")
    dwkv2 = mm(ckv, dkv2, ta=True, name="kv_up_dw")
    tok = put_g("qkv", dict(dwq2=dwq2, dwkv2=dwkv2))
    dhc = mm(dpc, win, tb=True, N=D, K=512, b_off=(0, O_KV), name="in_proj_ctx_dx")
    dx, _, st1 = normmod_bwd(x, dict(a=dp, b=win, tb=True, tk=NIN, dep=tok), n1g, sc1, dx1, a_out, g1,
                             name="in_proj_dx_normmod1_bwd")
    stc = normmod_bwd(ctx, dhc, n1g, csc1, None, None, None, name="normmod1_ctx_bwd")

    zrow = jnp.zeros((1, D), F32)
    dmod_lat = jnp.concatenate([st1[0:1], st1[1:2], st1[3:4], st2[0:1], st2[1:2], st2[3:4]], axis=1)
    dmod_ctx = jnp.concatenate([stc[0:1], stc[1:2], zrow, zrow, zrow, zrow], axis=1)
    return dict(
        loss=loss, dx=dx, dmod_lat=dmod_lat, dmod_ctx=dmod_ctx,
        dn1g=st1[2:3] + stc[2:3], dqg=dqg, dkvg=dkvg, dn2g=st2[2:3], dfg=dfg,
        dconv_w=dconv_w, dconv_b=dconv_b, dffn_w=dffn_w, dffn_b=dffn_b)


def _me():
    x, y, c = lax.axis_index("x"), lax.axis_index("y"), lax.axis_index("c")
    return x, y, c, 4 * x + 2 * y + c


def _peer(x, y, c, k):
    px = 1 - x if k & 4 else x
    py = 1 - y if k & 2 else y
    pc = 1 - c if k & 1 else c
    return (px, py, pc), 4 * px + 2 * py + pc


def _exchange_tiles(src_of_peer, buf, send_sem, recv_sem):
    x, y, c, me = _me()
    for k in range(1, NDEV):
        dev, lin = _peer(x, y, c, k)
        pltpu.make_async_remote_copy(src_ref=src_of_peer(lin), dst_ref=buf.at[me], send_sem=send_sem, recv_sem=recv_sem,
                                     device_id=dev, device_id_type=MESH).start()
    seven = buf.at[pl.ds(0, NDEV - 1)]
    pltpu.make_async_remote_copy(src_ref=seven, dst_ref=seven, send_sem=send_sem, recv_sem=recv_sem,
                                 device_id=(x, y, c), device_id_type=MESH).wait()


def _silu(z):
    return z * jax.nn.sigmoid(z)


def ada_fwd(c, c_ctx, ffn_w, conv_w, w_shard, b_shard, deps, *, name):
    nsh = w_shard.shape[1]
    deps = [d for d in deps if d is not None]

    def body(c_ref, cc_ref, fw_ref, cw_ref, w_ref, b_ref, *rest):
        s_ref, m_ref, mine, res, sems = rest[len(deps):]
        x, y, c, me = _me()
        mine[0:1, :] = _silu(c_ref[...])
        mine[1:2, :] = _silu(cc_ref[...])
        mine[2:5, :] = fw_ref[...]
        mine[5:8, :] = cw_ref[...]
        s_ref[me] = mine[...]
        _exchange_tiles(lambda lin: mine, s_ref, sems.at[0], sems.at[1])
        sall = s_ref[...].reshape(NDEV * 8, D).astype(BF)
        r = jnp.dot(sall, w_ref[...].astype(BF), preferred_element_type=F32) + b_ref[...]
        res[...] = r.reshape(NDEV, 8, nsh)
        m_ref[me] = res[me]
        _exchange_tiles(lambda lin: res.at[lin], m_ref, sems.at[2], sems.at[3])

    vm = pl.BlockSpec(memory_space=pltpu.VMEM)
    return pl.pallas_call(
        body, name=name, in_specs=[vm] * 6 + [pl.BlockSpec(memory_space=pl.ANY)] * len(deps), out_specs=[vm, vm],
        out_shape=[jax.ShapeDtypeStruct((NDEV, 8, D), F32), jax.ShapeDtypeStruct((NDEV, 8, nsh), F32)],
        scratch_shapes=[pltpu.VMEM((8, D), F32), pltpu.VMEM((NDEV, 8, nsh), F32), pltpu.SemaphoreType.DMA((4,))],
    )(c, c_ctx, ffn_w, conv_w, w_shard, b_shard, *deps)


P_DML, P_DMC, P_N1, P_QG, P_KVG, P_CB, P_N2, P_FB, P_FG, P_CW, P_FW, P_LOSS, P_ROWS = 0, 6, 12, 13, 14, 15, 16, 17, 23, 24, 27, 45, 48
FROWS = 3


def pack_small(r, *, name):
    ins = [r["dmod_lat"], r["dmod_ctx"], r["dn1g"], r["dqg"], r["dkvg"], r["dconv_b"], r["dn2g"], r["dffn_b"], r["dfg"],
           r["dconv_w"], r["dffn_w"], r["loss"]]

    def put_wide(p, row0, row, n):
        for j in range(-(-n // D)):
            w = min(D, n - j * D)
            p[row0 + j:row0 + j + 1, 0:w] = row[:, j * D:j * D + w]

    def body(dml, dmc, n1, qg, kvg, cb, n2, fb, fg, cw, fw, loss, p):
        p[...] = jnp.zeros_like(p)
        put_wide(p, P_DML, dml, 6 * D)
        put_wide(p, P_DMC, dmc, 6 * D)
        put_wide(p, P_N1, n1, D)
        put_wide(p, P_QG, qg, 512)
        put_wide(p, P_KVG, kvg, KVL)
        put_wide(p, P_CB, cb, CONV)
        put_wide(p, P_N2, n2, D)
        put_wide(p, P_FG, fg, D)
        put_wide(p, P_LOSS, loss, 128)
        for s in range(2):
            put_wide(p, P_FB + FROWS * s, fb.at[s], DFF)
        for k in range(3):
            put_wide(p, P_CW + k, cw.at[k:k + 1], CONV)
            for s in range(2):
                put_wide(p, P_FW + FROWS * (2 * k + s), fw.at[s, k:k + 1], DFF)

    vm = pl.BlockSpec(memory_space=pltpu.VMEM)
    return pl.pallas_call(
        body, name=name, in_specs=[vm] * len(ins), out_specs=vm, out_shape=jax.ShapeDtypeStruct((P_ROWS, D), F32),
    )(*ins)


def sum_slots(a, *, name):
    def body(a_ref, sum_ref):
        acc = a_ref[0]
        for k in range(1, NDEV):
            acc = acc + a_ref[k]
        sum_ref[...] = acc

    vm = pl.BlockSpec(memory_space=pltpu.VMEM)
    return pl.pallas_call(body, name=name, in_specs=[vm], out_specs=vm, out_shape=jax.ShapeDtypeStruct(a.shape[1:], F32))(a)


def ada_bwd(s_all, dml, dmc, w_shard, c_ctx, *, name):
    nsh = w_shard.shape[1]

    def body(s_ref, dml_ref, dmc_ref, w_ref, c_ref, dw_ref, gc_ref, s16, dm16, part, buf, sems):
        x, y, c, me = _me()
        s16[...] = jnp.zeros_like(s16)
        dm16[...] = jnp.zeros_like(dm16)
        for k in range(NDEV):
            s16[k:k + 1, :] = s_ref[k, 0:1, :]
        s16[8:9, :] = s_ref[0, 1:2, :]
        dm16[0:8, :] = dml_ref[...]
        dm16[8:9, :] = dmc_ref[...]
        dw_ref[...] = lax.dot_general(s16[...].astype(BF), dm16[...].astype(BF), (((0,), (0,)), ((), ())),
                                      preferred_element_type=F32)
        part[...] = lax.dot_general(dm16[8:16, :].astype(BF), w_ref[...].astype(BF), (((1,), (1,)), ((), ())),
                                    preferred_element_type=F32)
        buf[me] = part[...]
        _exchange_tiles(lambda lin: part, buf, sems.at[0], sems.at[1])
        acc = buf[0]
        for k in range(1, NDEV):
            acc = acc + buf[k]
        z = c_ref[...]
        sg = jax.nn.sigmoid(z)
        gc_ref[...] = acc * (sg * (1.0 + z * (1.0 - sg)))

    vm = pl.BlockSpec(memory_space=pltpu.VMEM)
    return pl.pallas_call(
        body, name=name, in_specs=[vm] * 5, out_specs=[vm, vm],
        out_shape=[jax.ShapeDtypeStruct((D, nsh), F32), jax.ShapeDtypeStruct((8, D), F32)],
        scratch_shapes=[pltpu.VMEM((16, D), F32), pltpu.VMEM((16, nsh), F32), pltpu.VMEM((8, D), F32),
                        pltpu.VMEM((NDEV, 8, D), F32), pltpu.SemaphoreType.DMA((2,))],
    )(s_all, dml, dmc, w_shard, c_ctx)


HBM_SPEC = pl.BlockSpec(memory_space=pltpu.HBM)
SEM_SPEC = pl.BlockSpec(memory_space=pltpu.SEMAPHORE)
EFFECT = pltpu.SideEffectType.DATAFLOW_SIDE_EFFECTING


ALL_PEERS = tuple(range(1, NDEV))
FIRST_HOP = (1, 2, 4, 6)
RELAY = (2, 4, 6)


def _exchange_copies(srcs, lands, send, recv, per_peer, peers):
    x, y, c, me = _me()
    n = len(peers)
    cps = []
    for t in range(len(srcs)):
        for j, k in enumerate(peers):
            dev, lin = _peer(x, y, c, k)
            cps.append(pltpu.make_async_remote_copy(
                src_ref=srcs[t].at[lin] if per_peer else srcs[t], dst_ref=lands[t].at[me],
                send_sem=send.at[n * t + j], recv_sem=recv.at[n * t + j], device_id=dev, device_id_type=MESH))
    return cps


def _relay_copies(lands, send, recv):
    x, y, c, me = _me()
    n = len(RELAY)
    cps = []
    for t in range(len(lands)):
        for j, k in enumerate(RELAY):
            slot = lands[t].at[_peer(x, y, c, k)[1]]
            cps.append(pltpu.make_async_remote_copy(
                src_ref=slot, dst_ref=slot, send_sem=send.at[n * t + j], recv_sem=recv.at[n * t + j],
                device_id=(x, y, 1 - c), device_id_type=MESH))
    return cps


def _own_copies(srcs, lands, own, per_peer):
    me = _me()[3]
    return [pltpu.make_async_copy(srcs[t].at[me] if per_peer else srcs[t], lands[t].at[me], own.at[t])
            for t in range(len(srcs))]


def exchange_start(srcs, *, per_peer, name, dep=None, peers=ALL_PEERS):
    nt = len(srcs)
    ns = len(peers) * nt
    land_shapes = [(a.shape if per_peer else (NDEV,) + a.shape) for a in srcs]
    deps = [] if dep is None else [dep]

    def body(*refs):
        src, land = refs[:nt], refs[nt:2 * nt]
        send, recv, own = refs[2 * nt + len(deps):2 * nt + len(deps) + 3]
        for cp in _exchange_copies(src, land, send, recv, per_peer, peers) + _own_copies(src, land, own, per_peer):
            cp.start()
        refs[-1][...] = jnp.zeros_like(refs[-1])

    hb = lambda a: pltpu.with_memory_space_constraint(a, pltpu.HBM)
    outs = pl.pallas_call(
        body, name=name,
        out_shape=(pltpu.SemaphoreType.DMA((ns,)), pltpu.SemaphoreType.DMA((ns,)), pltpu.SemaphoreType.DMA((nt,)),
                   *[pltpu.HBM(a.shape, a.dtype) for a in srcs], *[pltpu.HBM(s, a.dtype) for s, a in zip(land_shapes, srcs)],
                   jax.ShapeDtypeStruct((8, 128), F32)),
        in_specs=[HBM_SPEC] * (2 * nt) + [pl.BlockSpec(memory_space=pl.ANY)] * len(deps),
        out_specs=(SEM_SPEC, SEM_SPEC, SEM_SPEC, *([HBM_SPEC] * (2 * nt)), pl.BlockSpec(memory_space=pltpu.VMEM)),
        input_output_aliases={i: 3 + i for i in range(2 * nt)},
        compiler_params=pltpu.CompilerParams(has_side_effects=EFFECT),
    )(*[hb(a) for a in srcs], *[hb(lax.empty(s, a.dtype)) for s, a in zip(land_shapes, srcs)], *deps)
    return dict(send=outs[0], recv=outs[1], own=outs[2], src=list(outs[3:3 + nt]), land=list(outs[3 + nt:3 + 2 * nt]),
                token=outs[-1], per_peer=per_peer, peers=peers)


def exchange_wait(h, after, *, name):
    nt = len(h["src"])
    per_peer, peers = h["per_peer"], h["peers"]
    after = list(after) if isinstance(after, (list, tuple)) else [after]

    def body(*refs):
        src, land, send, recv, own = refs[:nt], refs[nt:2 * nt], refs[2 * nt], refs[2 * nt + 1], refs[2 * nt + 2]
        for cp in _exchange_copies(src, land, send, recv, per_peer, peers):
            cp.wait_send()
            cp.wait_recv()
        for cp in _own_copies(src, land, own, per_peer):
            cp.wait()

    outs = pl.pallas_call(
        body, name=name,
        out_shape=(*[pltpu.HBM(a.shape, a.dtype) for a in h["src"]], *[pltpu.HBM(a.shape, a.dtype) for a in h["land"]]),
        in_specs=[HBM_SPEC] * (2 * nt) + [SEM_SPEC, SEM_SPEC, SEM_SPEC] + [pl.BlockSpec(memory_space=pl.ANY)] * len(after),
        out_specs=tuple([HBM_SPEC] * (2 * nt)),
        input_output_aliases={i: i for i in range(2 * nt)},
        compiler_params=pltpu.CompilerParams(has_side_effects=EFFECT),
    )(*h["src"], *h["land"], h["send"], h["recv"], h["own"], *after)
    return list(outs[nt:])


def relay_start(lands, *, name):
    nt = len(lands)
    ns = len(RELAY) * nt

    def body(*refs):
        for cp in _relay_copies(refs[:nt], refs[nt], refs[nt + 1]):
            cp.start()

    outs = pl.pallas_call(
        body, name=name,
        out_shape=(pltpu.SemaphoreType.DMA((ns,)), pltpu.SemaphoreType.DMA((ns,)),
                   *[pltpu.HBM(a.shape, a.dtype) for a in lands]),
        in_specs=[HBM_SPEC] * nt, out_specs=(SEM_SPEC, SEM_SPEC, *([HBM_SPEC] * nt)),
        input_output_aliases={i: 2 + i for i in range(nt)},
        compiler_params=pltpu.CompilerParams(has_side_effects=EFFECT),
    )(*lands)
    return dict(send=outs[0], recv=outs[1], land=list(outs[2:]))


def relay_wait(h, *, name):
    nt = len(h["land"])

    def body(*refs):
        for cp in _relay_copies(refs[:nt], refs[nt], refs[nt + 1]):
            cp.wait_send()
            cp.wait_recv()

    outs = pl.pallas_call(
        body, name=name, out_shape=tuple(pltpu.HBM(a.shape, a.dtype) for a in h["land"]),
        in_specs=[HBM_SPEC] * nt + [SEM_SPEC, SEM_SPEC], out_specs=tuple([HBM_SPEC] * nt),
        input_output_aliases={i: i for i in range(nt)},
        compiler_params=pltpu.CompilerParams(has_side_effects=EFFECT),
    )(*h["land"], h["send"], h["recv"])
    return list(outs)


def _adamw_math(w, g, m, v):
    nm = B1 * m + (1.0 - B1) * g
    nv = B2 * v + (1.0 - B2) * (g * g)
    m_hat = nm / (1.0 - B1 ** STEP)
    v_hat = nv / (1.0 - B2 ** STEP)
    return -LR * (m_hat / (jnp.sqrt(v_hat) + AEPS) + WD * w), nm, nv


def adamw_many(ws, gs, ms, vs, *, name):
    n = len(ws)

    def body(*refs):
        for k in range(n):
            d, nm, nv = _adamw_math(refs[k][...], refs[n + k][...], refs[2 * n + k][...], refs[3 * n + k][...])
            refs[4 * n + k][...] = d
            refs[5 * n + k][...] = nm
            refs[6 * n + k][...] = nv

    vm = pl.BlockSpec(memory_space=pltpu.VMEM)
    sh = [jax.ShapeDtypeStruct(w.shape, F32) for w in ws]
    outs = pl.pallas_call(body, name=name, in_specs=[vm] * (4 * n), out_specs=[vm] * (3 * n), out_shape=sh * 3,
                          )(*ws, *gs, *ms, *vs)
    return outs[:n], outs[n:2 * n], outs[2 * n:]


def adamw(w, g, m, v, *, name, tr=256):
    R, C = w.shape
    tr = _pick(R, tr, 8)

    def body(w_ref, g_ref, m_ref, v_ref, d_ref, nm_ref, nv_ref):
        d_ref[...], nm_ref[...], nv_ref[...] = _adamw_math(w_ref[...], g_ref[...], m_ref[...], v_ref[...])

    blk = pl.BlockSpec((tr, C), lambda i: (i, 0))
    sh = jax.ShapeDtypeStruct((R, C), F32)
    return pl.pallas_call(
        body, name=name, grid=(R // tr,), in_specs=[blk, blk, blk, blk], out_specs=[blk, blk, blk],
        out_shape=[sh, sh, sh], compiler_params=pltpu.CompilerParams(dimension_semantics=("parallel",)),
    )(w, g, m, v)


def adamw_slots(w, slots, m, v, *, name, tr=256):
    unit = w.ndim == 3
    R, C = w.shape[0], w.shape[-1]
    if R % 16 == 0:
        tr = _pick(R, tr, 16)
    else:
        tr = 144

    def body(w_ref, s_ref, m_ref, v_ref, g_ref, d_ref, nm_ref, nv_ref):
        g = s_ref[0].astype(F32)
        for k in range(1, NDEV):
            g = g + s_ref[k].astype(F32)
        g_ref[...] = g
        d_ref[...], nm_ref[...], nv_ref[...] = _adamw_math(w_ref[...], g, m_ref[...], v_ref[...])

    blk = pl.BlockSpec((tr, None, C), lambda i: (i, 0, 0)) if unit else pl.BlockSpec((tr, C), lambda i: (i, 0))
    sh = jax.ShapeDtypeStruct(w.shape, F32)
    return pl.pallas_call(
        body, name=name, grid=(pl.cdiv(R, tr),), in_specs=[blk, pl.BlockSpec((NDEV, tr, C), lambda i: (0, i, 0)), blk, blk],
        out_specs=[blk, blk, blk, blk], out_shape=[sh, sh, sh, sh],
        compiler_params=pltpu.CompilerParams(dimension_semantics=("parallel",)),
    )(w, slots, m, v)


def _padc(a, n=D):
    return jnp.pad(a, ((0, 0), (0, n - a.shape[1])))


def kernel(x, c, ctx, c_ctx, w_ada, b_ada, norm1_g, w_in, q_norm_g, kv_norm_g, w_uq, w_ukv, conv_w, conv_b, w_attn_out, w_conv_out, w_o, norm2_g, w_up, ffn_conv_w, ffn_conv_b, w_down, final_g, loss_target, m_c_ctx, m_w_ada, m_b_ada, m_norm1_g, m_w_in, m_q_norm_g, m_kv_norm_g, m_w_uq, m_w_ukv, m_conv_w, m_conv_b, m_w_attn_out, m_w_conv_out, m_w_o, m_norm2_g, m_w_up, m_ffn_conv_w, m_ffn_conv_b, m_w_down, m_final_g, v_c_ctx, v_w_ada, v_b_ada, v_norm1_g, v_w_in, v_q_norm_g, v_kv_norm_g, v_w_uq, v_w_ukv, v_conv_w, v_conv_b, v_w_attn_out, v_w_conv_out, v_w_o, v_norm2_g, v_w_up, v_ffn_conv_w, v_ffn_conv_b, v_w_down, v_final_g):
    me = 4 * lax.axis_index("x") + 2 * lax.axis_index("y") + lax.axis_index("c")
    W = dict(c_ctx=c_ctx, w_ada=w_ada, b_ada=b_ada, norm1_g=norm1_g, w_in=w_in, q_norm_g=q_norm_g, kv_norm_g=kv_norm_g,
             w_uq=w_uq, w_ukv=w_ukv, conv_w=conv_w, conv_b=conv_b, w_attn_out=w_attn_out, w_conv_out=w_conv_out, w_o=w_o,
             norm2_g=norm2_g, w_up=w_up, ffn_conv_w=ffn_conv_w, ffn_conv_b=ffn_conv_b, w_down=w_down, final_g=final_g)
    M = dict(c_ctx=m_c_ctx, w_ada=m_w_ada, b_ada=m_b_ada, norm1_g=m_norm1_g, w_in=m_w_in, q_norm_g=m_q_norm_g,
             kv_norm_g=m_kv_norm_g, w_uq=m_w_uq, w_ukv=m_w_ukv, conv_w=m_conv_w, conv_b=m_conv_b, w_attn_out=m_w_attn_out,
             w_conv_out=m_w_conv_out, w_o=m_w_o, norm2_g=m_norm2_g, w_up=m_w_up, ffn_conv_w=m_ffn_conv_w,
             ffn_conv_b=m_ffn_conv_b, w_down=m_w_down, final_g=m_final_g)
    V = dict(c_ctx=v_c_ctx, w_ada=v_w_ada, b_ada=v_b_ada, norm1_g=v_norm1_g, w_in=v_w_in, q_norm_g=v_q_norm_g,
             kv_norm_g=v_kv_norm_g, w_uq=v_w_uq, w_ukv=v_w_ukv, conv_w=v_conv_w, conv_b=v_conv_b, w_attn_out=v_w_attn_out,
             w_conv_out=v_w_conv_out, w_o=v_w_o, norm2_g=v_norm2_g, w_up=v_w_up, ffn_conv_w=v_ffn_conv_w,
             ffn_conv_b=v_ffn_conv_b, w_down=v_w_down, final_g=v_final_g)
    names = list(W)
    transposed = ("w_up", "w_uq")
    as2d = lambda k, a: (a.reshape(1, -1) if a.ndim == 1 else
                         a[0].T if k in transposed else a.reshape(a.shape[-2], a.shape[-1]))
    W2 = {k: as2d(k, a) for k, a in W.items()}
    M2 = {k: as2d(k, a) for k, a in M.items()}
    V2 = {k: as2d(k, a) for k, a in V.items()}
    unit3 = lambda a: jnp.transpose(a, (2, 0, 1))
    W3, M3, V3 = unit3(W["w_in"]), unit3(M["w_in"]), unit3(V["w_in"])
    nsh = W2["w_ada"].shape[1]

    b_sh = lax.dynamic_slice(W2["b_ada"], (0, me * nsh), (1, nsh))
    s_all, m_all = ada_fwd(c, W2["c_ctx"], _padc(W2["ffn_conv_w"]), _padc(W2["conv_w"]), W2["w_ada"], b_sh, [],
                           name="ada_fwd")
    mod_lat = m_all[:, 0, :].reshape(1, 6 * D)
    mod_ctx = m_all[:, 1, :].reshape(1, 6 * D)
    ffn_w_full = s_all[:, 2:5, :2 * DFF // NDEV].transpose(1, 0, 2).reshape(3, 2 * DFF)
    conv_w_full = s_all[:, 5:8, :CONV // NDEV].transpose(1, 0, 2).reshape(3, CONV)

    stage_w = {"in": ["w_in"], "mid": ["w_uq", "w_ukv", "w_attn_out", "w_conv_out", "w_o"], "up": ["w_up"],
               "down": ["w_down"]}
    two_level = ("in", "mid")
    ag, tok = {}, m_all
    for st, nms in stage_w.items():
        ag[st] = exchange_start([W2[nm].astype(BF) for nm in nms], per_peer=False, dep=tok, name="ag_start_" + st,
                                peers=FIRST_HOP if st in two_level else ALL_PEERS)
        tok = ag[st]["token"]

    def get_w(stage, after):
        lands = exchange_wait(ag[stage], after, name="ag_wait_" + stage)
        if stage in two_level:
            lands = relay_wait(relay_start(lands, name="ag_relay_" + stage), name="ag_relay_wait_" + stage)
        g = dict(zip(stage_w[stage], lands))
        if stage == "in":
            return build_win(g["w_in"], name="build_win")
        if stage == "mid":
            wq2, wkv2 = build_wq_wkv(g["w_uq"], g["w_ukv"], name="build_wq_wkv")
            return (wq2, wkv2, unshard_cols(g["w_attn_out"], name="unshard_w_attn_out"),
                    unshard_cols(g["w_conv_out"], name="unshard_w_conv_out"), g["w_o"].reshape(D, D))
        if stage == "up":
            return g["w_up"].reshape(2 * DFF, D)
        return g["w_down"].reshape(DFF, D)

    stage_g = {"ffn": ["w_up", "w_down"], "mid": ["w_attn_out", "w_conv_out", "w_o"], "qkv": ["w_uq", "w_ukv"],
               "in": ["w_in"]}
    rs = {}

    def put_g(stage, g):
        if stage == "in":
            parts = [shard_win_grad(g["dwin"], g["dwin_c"], name="shard_win_grad")]
        elif stage == "mid":
            parts = [shard_cols(g["dwao"], name="shard_w_attn_out"), shard_cols(g["dwco"], name="shard_w_conv_out"),
                     g["dwo"].reshape(NDEV, D // NDEV, D)]
        elif stage == "qkv":
            parts = list(shard_wq_wkv_grad(g["dwq2"], g["dwkv2"], name="shard_wq_wkv_grad"))
        else:
            parts = [g["dwup"].reshape(NDEV, 2 * DFF // NDEV, D), g["dwdn"].reshape(NDEV, DFF // NDEV, D)]
        rs[stage] = exchange_start(parts, per_peer=True, name="rs_start_" + stage)
        return rs[stage]["token"]

    r = _local_step(x[0], ctx[0], loss_target[0], mod_lat, mod_ctx, W2["norm1_g"], W2["q_norm_g"], W2["kv_norm_g"],
                    W2["norm2_g"], W2["final_g"], conv_w_full, W2["conv_b"], ffn_w_full, W2["ffn_conv_b"], get_w, put_g,
                    ag["down"]["token"])

    G, DL, NM, NV = {}, {}, {}, {}

    def finish(stage, after):
        for nm, sl in zip(stage_g[stage], exchange_wait(rs[stage], after, name="rs_wait_" + stage)):
            wmv = (W3, M3, V3) if nm == "w_in" else (W2[nm], M2[nm], V2[nm])
            G[nm], DL[nm], NM[nm], NV[nm] = adamw_slots(wmv[0], sl, wmv[1], wmv[2], name="adamw_" + nm)
            after = DL[nm]
        return after

    sync = exchange_start([pack_small(r, name="pack_small")], per_peer=False, name="sync_start")
    after = sync["token"]
    for st in ("ffn", "mid", "in", "qkv"):
        after = finish(st, after)
    a_buf, = exchange_wait(sync, [DL[nm] for nms in stage_g.values() for nm in nms], name="sync_wait")
    ssum = sum_slots(a_buf, name="sum_small")
    loss = ssum[P_LOSS, 0]
    G["norm1_g"] = ssum[P_N1:P_N1 + 1]
    G["q_norm_g"] = ssum[P_QG:P_QG + 1, :QL]
    G["kv_norm_g"] = ssum[P_KVG:P_KVG + 1, :KVL]
    G["conv_b"] = ssum[P_CB:P_CB + 1, :CONV]
    G["norm2_g"] = ssum[P_N2:P_N2 + 1]
    G["ffn_conv_b"] = ssum[P_FB:P_FB + 2 * FROWS].reshape(1, 2, FROWS * D)[:, :, :DFF].reshape(1, 2 * DFF)
    G["final_g"] = ssum[P_FG:P_FG + 1]
    G["conv_w"] = lax.dynamic_slice(ssum[P_CW:P_CW + 3, :CONV], (0, me * (CONV // NDEV)), (3, CONV // NDEV))
    fw_full = ssum[P_FW:P_FW + 6 * FROWS].reshape(3, 2, FROWS * D)[:, :, :DFF].reshape(3, 2 * DFF)
    G["ffn_conv_w"] = lax.dynamic_slice(fw_full, (0, me * (2 * DFF // NDEV)), (3, 2 * DFF // NDEV))
    G["b_ada"] = (ssum[P_DML:P_DML + 6] + ssum[P_DMC:P_DMC + 6]).reshape(1, 6 * D)

    dml = lax.dynamic_slice(a_buf[:, P_DML:P_DML + 6, :].reshape(NDEV, 6 * D), (0, me * nsh), (NDEV, nsh))
    dmc = lax.dynamic_slice(ssum[P_DMC:P_DMC + 6].reshape(1, 6 * D), (0, me * nsh), (1, nsh))
    G["w_ada"], gcc = ada_bwd(s_all, dml, dmc, W2["w_ada"], W2["c_ctx"], name="ada_bwd")
    G["c_ctx"] = gcc[0:1]

    DL["w_ada"], NM["w_ada"], NV["w_ada"] = adamw(W2["w_ada"], G["w_ada"], M2["w_ada"], V2["w_ada"], name="adamw_w_ada")
    small = ["c_ctx", "b_ada", "norm1_g", "q_norm_g", "kv_norm_g", "conv_b", "norm2_g", "ffn_conv_b", "final_g", "conv_w",
             "ffn_conv_w"]
    unit_mid = ("conv_w", "ffn_conv_w")
    view = lambda k, a3, a2: jnp.transpose(a3[k], (1, 0, 2)) if k in unit_mid else a2[k]
    for k in unit_mid:
        G[k] = G[k].reshape(3, 1, -1)
    ds, nms, nvs = adamw_many([view(k, W, W2) for k in small], [G[k] for k in small],
                              [view(k, M, M2) for k in small], [view(k, V, V2) for k in small], name="adamw_small")
    for k, nm in enumerate(small):
        DL[nm], NM[nm], NV[nm] = ds[k], nms[k], nvs[k]

    def as_output(nm, a):
        if nm in transposed:
            return a.T[None]
        if nm == "w_in":
            return jnp.transpose(a, (1, 2, 0))
        if nm in unit_mid and a.ndim == 3:
            return jnp.transpose(a, (1, 0, 2))
        return a.reshape(W[nm].shape)

    outs = [loss, r["dx"][None]]
    for grp in (G, DL, NM, NV):
        outs += [as_output(nm, grp[nm]) for nm in names]
    return tuple(outs)
```

```python
import functools
import numpy as np
import jax
import jax.numpy as jnp
from jax import lax
from jax.experimental import pallas as pl
from jax.experimental.pallas import tpu as pltpu

F32 = jnp.float32
BF = jnp.bfloat16
MESH = pl.DeviceIdType.MESH

D = 1024
T = 2048
TC = 256
TKV = T + TC
GRID_W = 64
NH = 8
DN = 64
DR = 32
DV = 64
QL = 384
KVL = 256
CONV = 512
DFF = 2816
EPS = 1e-6
ROPE_THETA = 10000.0
SCALE = (DN + DR) ** -0.5
NDEV = 8
HP = 128

O_GA, O_GC, O_KV, O_Q, O_CV = 0, 1024, 2048, 2560, 3072
NIN = 4608
CVB = 256
N_IN = 4256
SH_IN = N_IN // NDEV

LR, B1, B2, AEPS, WD, STEP = 0.001, 0.9, 0.999, 1e-08, 0.01, 10


def _pick(n, target, mult=128):
    best = None
    for d in range(mult, min(n, target) + 1, mult):
        if n % d == 0:
            best = d
    return best if best is not None else n


def _swap_start(g):
    return 8 * (g ^ 1)


def mm(a, b, *, ta=False, tb=False, out_dtype=F32, name, tm=1024, tn=1024, tk=2048, M=None, N=None, K=None,
       a_off=(0, 0), b_off=(0, 0), a_stack=False, b_stack=False, o_stack=False, dep=None):
    def dims(arr, stack):
        return (arr.shape[1], 2 * arr.shape[2]) if stack else arr.shape

    ar, ac = dims(a, a_stack)
    br, bc = dims(b, b_stack)
    M = M or ((ac if ta else ar) - a_off[1 if ta else 0])
    K = K or ((ar if ta else ac) - a_off[0 if ta else 1])
    N = N or ((br if tb else bc) - b_off[0 if tb else 1])
    tm = _pick(M, tm, 128 if ta else 16)
    tn = _pick(N // 2 if (o_stack or (b_stack and not tb)) else N, tn, 128)
    tk = _pick(K // 2 if ((a_stack and not ta) or (b_stack and tb)) else K, tk, 128)
    nk = K // tk
    ca = 0 if ta else 1
    cb = 1 if tb else 0

    def body(a_ref, b_ref, *rest):
        o_ref, acc = rest[-2:]
        k = pl.program_id(2)
        part = lax.dot_general(a_ref[...].astype(BF), b_ref[...].astype(BF),
                               (((ca,), (cb,)), ((), ())), preferred_element_type=F32)
        if nk == 1:
            o_ref[...] = part.astype(o_ref.dtype)
        else:
            @pl.when(k == 0)
            def _():
                acc[...] = part

            @pl.when(k > 0)
            def _():
                acc[...] += part

            @pl.when(k == nk - 1)
            def _():
                o_ref[...] = acc[...].astype(o_ref.dtype)

    def spec(blk, rc, off, stack, ncols):
        assert off[0] % blk[0] == 0 and off[1] % blk[1] == 0, (name, blk, off)
        ro, co = off[0] // blk[0], off[1] // blk[1]
        if not stack:
            return pl.BlockSpec(blk, lambda i, j, k: (rc(i, j, k)[0] + ro, rc(i, j, k)[1] + co))
        nhb = ncols // 2 // blk[1]
        return pl.BlockSpec((None,) + blk,
                            lambda i, j, k: ((rc(i, j, k)[1] + co) // nhb, rc(i, j, k)[0] + ro, (rc(i, j, k)[1] + co) % nhb))

    a_spec = spec((tk, tm), lambda i, j, k: (k, i), a_off, a_stack, ac) if ta else \
        spec((tm, tk), lambda i, j, k: (i, k), a_off, a_stack, ac)
    b_spec = spec((tn, tk), lambda i, j, k: (j, k), b_off, b_stack, bc) if tb else \
        spec((tk, tn), lambda i, j, k: (k, j), b_off, b_stack, bc)
    o_spec = spec((tm, tn), lambda i, j, k: (i, j), (0, 0), o_stack, N)
    o_shape = (2, M, N // 2) if o_stack else (M, N)
    deps = [] if dep is None else [dep]
    return pl.pallas_call(
        body, name=name, grid=(M // tm, N // tn, nk),
        in_specs=[a_spec, b_spec] + [pl.BlockSpec(memory_space=pl.ANY)] * len(deps),
        out_specs=o_spec, out_shape=jax.ShapeDtypeStruct(o_shape, out_dtype),
        scratch_shapes=[pltpu.VMEM((tm, tn) if nk > 1 else (8, 128), F32)],
        compiler_params=pltpu.CompilerParams(dimension_semantics=("parallel", "parallel", "arbitrary")),
    )(a, b, *deps)


def _row(width):
    return pl.BlockSpec((1, width), lambda *_: (0, 0))


NLAT = T // TC


def normmod_cat(ctx, x, g, csc, csh, sc, sh, dep, *, name, tm=256):
    assert tm == TC

    def body(c_ref, x_ref, g_ref, csc_ref, csh_ref, sc_ref, sh_ref, dep_ref, h_ref):
        last = pl.program_id(0) == NLAT
        xv = jnp.where(last, c_ref[...], x_ref[...])
        scv = jnp.where(last, csc_ref[...], sc_ref[...])
        shv = jnp.where(last, csh_ref[...], sh_ref[...])
        r = lax.rsqrt(jnp.mean(xv * xv, axis=-1, keepdims=True) + EPS)
        h_ref[...] = ((xv * r * g_ref[...]) * (1.0 + scv) + shv).astype(BF)

    return pl.pallas_call(
        body, name=name, grid=(TKV // tm,),
        in_specs=[pl.BlockSpec((tm, D), lambda i: (0, 0)), pl.BlockSpec((tm, D), lambda i: (jnp.minimum(i, NLAT - 1), 0)),
                  _row(D), _row(D), _row(D), _row(D), _row(D), pl.BlockSpec(memory_space=pl.ANY)],
        out_specs=pl.BlockSpec((tm, D), lambda i: (i, 0)), out_shape=jax.ShapeDtypeStruct((TKV, D), BF),
        compiler_params=pltpu.CompilerParams(dimension_semantics=("parallel",)),
    )(ctx, x, g, csc, csh, sc, sh, dep)


def kvprep(pc, p, kvg, wkv2, ck, sk, *, name, tm=256):
    assert tm == TC
    nb = TKV // tm
    kvcol = O_KV // 512

    def body(pc_ref, p_ref, g_ref, w_ref, ck_ref, sk_ref, k_ref, v_ref, ckv_ref):
        i = pl.program_id(0)
        t = jnp.where(i == NLAT, pc_ref[...], p_ref[...])
        pk = t[:, :KVL]
        r = lax.rsqrt(jnp.mean(pk * pk, axis=-1, keepdims=True) + EPS)
        ckv = (pk * r * g_ref[...]).astype(BF)
        ckv_ref[...] = ckv
        kv2 = jnp.dot(ckv, w_ref[...], preferred_element_type=F32)
        krr = t[:, KVL:KVL + HP] * ck_ref[...] + t[:, KVL + HP:KVL + 2 * HP] * sk_ref[...]
        k_ref[...] = (kv2[:, :NH * HP] + jnp.concatenate([krr] * NH, axis=1)).astype(BF)
        v_ref[...] = kv2[:, NH * HP:].astype(BF)

    return pl.pallas_call(
        body, name=name, grid=(nb,),
        in_specs=[pl.BlockSpec((tm, 512), lambda i: (0, 0)),
                  pl.BlockSpec((tm, 512), lambda i: (jnp.minimum(i, NLAT - 1), kvcol)),
                  _row(KVL), pl.BlockSpec((KVL, NH * HP + NH * DV), lambda i: (0, 0)),
                  pl.BlockSpec((tm, HP), lambda i: (i, 0)), pl.BlockSpec((tm, HP), lambda i: (i, 0))],
        out_specs=[pl.BlockSpec((tm, NH * HP), lambda i: (i, 0)), pl.BlockSpec((tm, NH * DV), lambda i: (i, 0)),
                   pl.BlockSpec((tm, KVL), lambda i: (i, 0))],
        out_shape=[jax.ShapeDtypeStruct((TKV, NH * HP), BF), jax.ShapeDtypeStruct((TKV, NH * DV), BF),
                   jax.ShapeDtypeStruct((TKV, KVL), BF)],
        compiler_params=pltpu.CompilerParams(dimension_semantics=("parallel",)),
    )(pc, p, kvg, wkv2, ck, sk)


def qprep(p, qg, wq2, cq_t, sq_t, *, name, tm=256):
    qcol = O_Q // 512

    def body(p_ref, g_ref, w_ref, c_ref, s_ref, q_ref, cq_ref):
        pq = p_ref[...]
        r = lax.rsqrt(jnp.sum(pq * pq, axis=-1, keepdims=True) * (1.0 / QL) + EPS)
        cq = (pq * r * g_ref[...]).astype(BF)
        cq_ref[...] = cq
        q2 = jnp.dot(cq, w_ref[...], preferred_element_type=F32)
        cc = jnp.concatenate([c_ref[...]] * NH, axis=1)
        ss = jnp.concatenate([s_ref[...]] * NH, axis=1)
        q_ref[...] = (q2[:, :NH * HP] * cc + q2[:, NH * HP:] * ss).astype(BF)

    return pl.pallas_call(
        body, name=name, grid=(T // tm,),
        in_specs=[pl.BlockSpec((tm, 512), lambda i: (i, qcol)), _row(512),
                  pl.BlockSpec((512, 2 * NH * HP), lambda i: (0, 0)),
                  pl.BlockSpec((tm, HP), lambda i: (i, 0)), pl.BlockSpec((tm, HP), lambda i: (i, 0))],
        out_specs=[pl.BlockSpec((tm, NH * HP), lambda i: (i, 0)), pl.BlockSpec((tm, 512), lambda i: (i, 0))],
        out_shape=[jax.ShapeDtypeStruct((T, NH * HP), BF), jax.ShapeDtypeStruct((T, 512), BF)],
        compiler_params=pltpu.CompilerParams(dimension_semantics=("parallel",)),
    )(p, qg, wq2, cq_t, sq_t)


def _head_mask(h):
    lanes = lax.broadcasted_iota(jnp.int32, (1, 2 * DV), 1)
    return (lanes // DV) == (h % 2)


LOG2E = 1.4426950408889634


def attn_fwd(q, k, v, *, name, tq=1024, kc=768):
    def body(q_ref, k_ref, v_ref, o_ref, lse_ref):
        h = pl.program_id(1)
        qv = q_ref[...]
        m = l = acc = None
        for c in range(TKV // kc):
            s = lax.dot_general(qv, k_ref[c * kc:(c + 1) * kc, :], (((1,), (1,)), ((), ())),
                                preferred_element_type=F32) * (SCALE * LOG2E)
            mc = jnp.max(s, axis=-1, keepdims=True)
            if c == 0:
                m = mc
                e = jnp.exp2(s - m)
                l = jnp.sum(e, axis=-1, keepdims=True)
                acc = jnp.dot(e.astype(BF), v_ref[c * kc:(c + 1) * kc, :], preferred_element_type=F32)
            else:
                mn = jnp.maximum(m, mc)
                a = jnp.exp2(m - mn)
                e = jnp.exp2(s - mn)
                l = l * a + jnp.sum(e, axis=-1, keepdims=True)
                acc = acc * a + jnp.dot(e.astype(BF), v_ref[c * kc:(c + 1) * kc, :], preferred_element_type=F32)
                m = mn
        o2 = jnp.where(_head_mask(h), acc * (1.0 / l), 0.0).astype(BF)
        lse_ref[...] = jnp.broadcast_to(m + jnp.log(l) * LOG2E, (tq, HP))

        @pl.when(h % 2 == 0)
        def _():
            o_ref[...] = o2

        @pl.when(h % 2 == 1)
        def _():
            o_ref[...] = o_ref[...] + o2

    return pl.pallas_call(
        body, name=name, grid=(T // tq, NH),
        in_specs=[pl.BlockSpec((tq, HP), lambda i, h: (i, h)), pl.BlockSpec((TKV, HP), lambda i, h: (0, h)),
                  pl.BlockSpec((TKV, 2 * DV), lambda i, h: (0, h // 2))],
        out_specs=[pl.BlockSpec((tq, 2 * DV), lambda i, h: (i, h // 2)), pl.BlockSpec((tq, HP), lambda i, h: (i, h))],
        out_shape=[jax.ShapeDtypeStruct((T, NH * DV), BF), jax.ShapeDtypeStruct((T, NH * HP), F32)],
        compiler_params=pltpu.CompilerParams(dimension_semantics=("parallel", "arbitrary")),
    )(q, k, v)


def _shift_dn(x):
    n = x.shape[0]
    rows = lax.broadcasted_iota(jnp.int32, (n, 1), 0)
    return jnp.where(rows == 0, 0.0, pltpu.roll(x, 1, axis=0))


def _shift_up(x):
    n = x.shape[0]
    rows = lax.broadcasted_iota(jnp.int32, (n, 1), 0)
    return jnp.where(rows == n - 1, 0.0, pltpu.roll(x, n - 1, axis=0))


def _conv(x, w_ref, b_ref):
    return b_ref[...] + _shift_dn(x) * w_ref[0:1, :] + x * w_ref[1:2, :] + _shift_up(x) * w_ref[2:3, :]


def _conv_t(dy, w_ref):
    return _shift_up(dy) * w_ref[0:1, :] + dy * w_ref[1:2, :] + _shift_dn(dy) * w_ref[2:3, :]


def _conv_wgrad(dw_ref, dy, x):
    dw_ref[0:1, :] = jnp.sum(dy * _shift_dn(x), axis=0, keepdims=True)
    dw_ref[1:2, :] = jnp.sum(dy * x, axis=0, keepdims=True)
    dw_ref[2:3, :] = jnp.sum(dy * _shift_up(x), axis=0, keepdims=True)


def convz(p, cw, cb, *, name):
    o0 = O_CV // (3 * CVB)

    def body(p_ref, w_ref, bias_ref, z_ref):
        xv, bv, cv = p_ref[:, 0:CVB], p_ref[:, CVB:2 * CVB], p_ref[:, 2 * CVB:3 * CVB]
        z_ref[...] = (bv * _conv(cv * xv, w_ref, bias_ref)).astype(BF)

    return pl.pallas_call(
        body, name=name, grid=(CONV // CVB,),
        in_specs=[pl.BlockSpec((T, 3 * CVB), lambda j: (0, o0 + j)), pl.BlockSpec((3, CVB), lambda j: (0, j)),
                  pl.BlockSpec((1, CVB), lambda j: (0, j))],
        out_specs=pl.BlockSpec((T, CVB), lambda j: (0, j)),
        out_shape=jax.ShapeDtypeStruct((T, CONV), BF),
        compiler_params=pltpu.CompilerParams(dimension_semantics=("parallel",)),
    )(p, cw, cb)


def out_proj_merge(o, wao, z, wco, p, *, name, tm=512):
    kin = o.shape[1]

    def body(o_ref, wa_ref, z_ref, wc_ref, ga_ref, gc_ref, ya_ref, yc_ref, m_ref):
        ya = jnp.dot(o_ref[...], wa_ref[...], preferred_element_type=F32)
        yc = jnp.dot(z_ref[...], wc_ref[...], preferred_element_type=F32)
        ya_ref[...] = ya
        yc_ref[...] = yc
        m_ref[...] = (jax.nn.sigmoid(ga_ref[...]) * ya + jax.nn.sigmoid(gc_ref[...]) * yc).astype(BF)

    blk = pl.BlockSpec((tm, D), lambda i: (i, 0))
    act = pl.BlockSpec((tm, kin), lambda i: (i, 0))
    wsp = pl.BlockSpec((kin, D), lambda i: (0, 0))
    sh = jax.ShapeDtypeStruct((T, D), F32)
    return pl.pallas_call(
        body, name=name, grid=(T // tm,),
        in_specs=[act, wsp, act, wsp, pl.BlockSpec((tm, D), lambda i: (i, O_GA // D)),
                  pl.BlockSpec((tm, D), lambda i: (i, O_GC // D))],
        out_specs=[blk, blk, blk], out_shape=[sh, sh, jax.ShapeDtypeStruct((T, D), BF)],
        compiler_params=pltpu.CompilerParams(dimension_semantics=("parallel",)),
    )(o, wao, z, wco, p, p)


CONV_HALO = 8
CONV_ROWS = 256


def _row_chunks(n, chunk, carry):
    carry = chunk(0, True, False, carry)
    carry = lax.fori_loop(1, n // CONV_ROWS - 1, lambda c, a: chunk(c * CONV_ROWS, False, False, a), carry)
    return chunk(n - CONV_ROWS, False, True, carry)


def _ext_rows(ref, r0, first, last):
    n, w = ref.shape
    zero = jnp.zeros((CONV_HALO, w), ref.dtype)
    if first:
        return jnp.concatenate([zero, ref[0:CONV_ROWS + CONV_HALO, :]], axis=0)
    if last:
        return jnp.concatenate([ref[n - CONV_ROWS - CONV_HALO:n, :], zero], axis=0)
    return ref[pl.ds(pl.multiple_of(r0 - CONV_HALO, 8), CONV_ROWS + 2 * CONV_HALO), :]


def _center_rows(r0, first, last):
    return slice(r0, r0 + CONV_ROWS) if (first or last) else pl.ds(pl.multiple_of(r0, 8), CONV_ROWS)


def _roll_dn(x):
    return pltpu.roll(x, 1, axis=0)


def _roll_up(x):
    return pltpu.roll(x, x.shape[0] - 1, axis=0)


_CTR = slice(CONV_HALO, CONV_HALO + CONV_ROWS)


def ffn_act(u0, cw, cb, *, name, tc=256):
    nb = DFF // tc

    def body(u_ref, wg_ref, wv_ref, bg_ref, bv_ref, f_ref):
        wg = [wg_ref[k:k + 1, :] for k in range(3)]
        wv = [wv_ref[k:k + 1, :] for k in range(3)]
        bg, bv = bg_ref[...], bv_ref[...]

        def chunk(r0, first, last, carry):
            xg, xv = _ext_rows(u_ref.at[0], r0, first, last), _ext_rows(u_ref.at[1], r0, first, last)
            ug = bg + _roll_dn(xg) * wg[0] + xg * wg[1] + _roll_up(xg) * wg[2]
            uv = bv + _roll_dn(xv) * wv[0] + xv * wv[1] + _roll_up(xv) * wv[2]
            f_ref[_center_rows(r0, first, last), :] = (ug * jax.nn.sigmoid(ug) * uv)[_CTR].astype(BF)
            return carry

        _row_chunks(T, chunk, 0)

    return pl.pallas_call(
        body, name=name, grid=(nb,),
        in_specs=[pl.BlockSpec((2, T, tc), lambda j: (0, 0, j)),
                  pl.BlockSpec((3, tc), lambda j: (0, j)), pl.BlockSpec((3, tc), lambda j: (0, nb + j)),
                  pl.BlockSpec((1, tc), lambda j: (0, j)), pl.BlockSpec((1, tc), lambda j: (0, nb + j))],
        out_specs=pl.BlockSpec((T, tc), lambda j: (0, j)),
        out_shape=jax.ShapeDtypeStruct((T, DFF), BF),
        compiler_params=pltpu.CompilerParams(dimension_semantics=("parallel",)),
    )(u0, cw, cw, cb, cb)


def rows_call(lead, ins, in_specs, out_shape, out_specs, fn, *, name, R, tm):
    tb, a_stack, tk = lead.get("tb", False), lead.get("a_stack", False), lead["tk"]
    K = 2 * lead["a"].shape[2] if a_stack else lead["a"].shape[1]
    nk = K // tk
    deps = [] if lead.get("dep") is None else [lead["dep"]]
    n_in = len(ins)

    def body(a_ref, b_ref, *refs):
        refs = refs[len(deps):]
        in_refs, out_refs, acc = refs[:n_in], refs[n_in:-1], refs[-1]
        i, k = pl.program_id(0), pl.program_id(1)
        part = lax.dot_general(a_ref[...].astype(BF), b_ref[...].astype(BF),
                               (((1,), (1 if tb else 0,)), ((), ())), preferred_element_type=F32)
        if nk == 1:
            fn(i, part, in_refs, out_refs)
            return

        @pl.when(k == 0)
        def _():
            acc[...] = part

        @pl.when(k > 0)
        def _():
            acc[...] += part

        @pl.when(k == nk - 1)
        def _():
            fn(i, acc[...], in_refs, out_refs)

    if a_stack:
        nhb = K // 2 // tk
        a_spec = pl.BlockSpec((None, tm, tk), lambda i, k: (k // nhb, i, k % nhb))
    else:
        a_spec = pl.BlockSpec((tm, tk), lambda i, k: (i, k))
    b_spec = pl.BlockSpec((D, tk), lambda i, k: (0, k)) if tb else pl.BlockSpec((tk, D), lambda i, k: (k, 0))
    return pl.pallas_call(
        body, name=name, grid=(R // tm, nk),
        in_specs=[a_spec, b_spec] + [pl.BlockSpec(memory_space=pl.ANY)] * len(deps) + list(in_specs),
        out_specs=out_specs, out_shape=out_shape,
        scratch_shapes=[pltpu.VMEM((tm, D) if nk > 1 else (8, 128), F32)],
        compiler_params=pltpu.CompilerParams(dimension_semantics=("arbitrary", "arbitrary")),
    )(lead["a"], lead["b"], *deps, *ins)


def _rblk(tm, w=D, col=0):
    return pl.BlockSpec((tm, w), lambda i, k: (i, col))


def _rrow(w=D):
    return pl.BlockSpec((1, w), lambda i, k: (0, 0))


def down_final(f, wdn, x1, g2, fg, tgt, *, name, tm=512):
    def fn(i, d, in_refs, out_refs):
        x1_ref, g2_ref, fg_ref, t_ref = in_refs
        d_ref, dx_ref, dd_ref, dfg_ref, loss_ref = out_refs
        d_ref[...] = d
        xv = x1_ref[...] + g2_ref[...] * d
        r = lax.rsqrt(jnp.mean(xv * xv, axis=-1, keepdims=True) + EPS)
        xh = xv * r
        diff = xh * fg_ref[...] - t_ref[...]
        part = 0.5 * jnp.sum(jnp.mean(diff * diff, axis=-1, keepdims=True), axis=0, keepdims=True)
        dy = diff * (1.0 / D)
        a = dy * fg_ref[...]
        dx = r * (a - xh * jnp.mean(a * xh, axis=-1, keepdims=True))
        dx_ref[...] = dx
        dd_ref[...] = (dx * g2_ref[...]).astype(BF)
        dfg = jnp.sum(dy * xh, axis=0, keepdims=True)

        @pl.when(i == 0)
        def _():
            dfg_ref[...] = dfg
            loss_ref[...] = jnp.broadcast_to(part, (1, 128))

        @pl.when(i > 0)
        def _():
            dfg_ref[...] += dfg
            loss_ref[...] += jnp.broadcast_to(part, (1, 128))

    blk = _rblk(tm)
    return rows_call(
        dict(a=f, b=wdn, tk=DFF), [x1, g2, fg, tgt], [blk, _rrow(), _rrow(), blk],
        [jax.ShapeDtypeStruct((T, D), F32), jax.ShapeDtypeStruct((T, D), F32), jax.ShapeDtypeStruct((T, D), BF),
         jax.ShapeDtypeStruct((1, D), F32), jax.ShapeDtypeStruct((1, 128), F32)],
        [blk, blk, blk, _rrow(), _rrow(128)], fn, name=name, R=T, tm=tm)


def oproj_resid(merged, wo, x, gate, g, sc, sh, *, name, tm=512):
    def fn(i, a, in_refs, out_refs):
        x_ref, gate_ref, g_ref, sc_ref, sh_ref = in_refs
        a_ref, x1_ref, h_ref = out_refs
        a_ref[...] = a
        xv = x_ref[...] + gate_ref[...] * a
        x1_ref[...] = xv
        r = lax.rsqrt(jnp.mean(xv * xv, axis=-1, keepdims=True) + EPS)
        h_ref[...] = ((xv * r * g_ref[...]) * (1.0 + sc_ref[...]) + sh_ref[...]).astype(BF)

    blk = _rblk(tm)
    return rows_call(
        dict(a=merged, b=wo, tk=D), [x, gate, g, sc, sh], [blk, _rrow(), _rrow(), _rrow(), _rrow()],
        [jax.ShapeDtypeStruct((T, D), F32), jax.ShapeDtypeStruct((T, D), F32), jax.ShapeDtypeStruct((T, D), BF)],
        [blk, blk, blk], fn, name=name, R=T, tm=tm)


def oproj_dx_gate_bwd(da, wo, p, ya, yc, wao, wco, *, name, tm=512):
    kin = wao.shape[0]

    def fn(i, dm, in_refs, out_refs):
        ga_ref, gc_ref, ya_ref, yc_ref, wa_ref, wc_ref = in_refs
        dya_ref, dyc_ref, dp_ref, do_ref, dz_ref = out_refs
        sa, sc_ = jax.nn.sigmoid(ga_ref[...]), jax.nn.sigmoid(gc_ref[...])
        dya, dyc = (dm * sa).astype(BF), (dm * sc_).astype(BF)
        dya_ref[...] = dya
        dyc_ref[...] = dyc
        dp_ref[:, 0:D] = (dm * ya_ref[...] * (sa * (1.0 - sa))).astype(BF)
        dp_ref[:, D:2 * D] = (dm * yc_ref[...] * (sc_ * (1.0 - sc_))).astype(BF)
        nt = (((1,), (1,)), ((), ()))
        do_ref[...] = lax.dot_general(dya, wa_ref[...], nt, preferred_element_type=F32).astype(BF)
        dz_ref[...] = lax.dot_general(dyc, wc_ref[...], nt, preferred_element_type=F32)

    blk = _rblk(tm)
    sh = jax.ShapeDtypeStruct((T, D), BF)
    wsp = pl.BlockSpec((kin, D), lambda i, k: (0, 0))
    return rows_call(
        dict(a=da, b=wo, tb=True, tk=D), [p, p, ya, yc, wao, wco],
        [_rblk(tm, D, O_GA // D), _rblk(tm, D, O_GC // D), blk, blk, wsp, wsp],
        [sh, sh, jax.ShapeDtypeStruct((T, NIN), BF), jax.ShapeDtypeStruct((T, kin), BF), jax.ShapeDtypeStruct((T, kin), F32)],
        [blk, blk, _rblk(tm, 2 * D), _rblk(tm, kin), _rblk(tm, kin)], fn, name=name, R=T, tm=tm)


def normmod_bwd(x, dh, g, sc, dres, gsrc, gate, *, name, tm=512):
    R = x.shape[0]
    tm = min(tm, R)
    has_res = dres is not None
    fused = isinstance(dh, dict)
    if fused:
        tb, a_stack, tk = dh.get("tb", False), dh.get("a_stack", False), dh["tk"]
        K = 2 * dh["a"].shape[2] if a_stack else dh["a"].shape[1]
        nk = K // tk
        deps = [] if dh.get("dep") is None else [dh["dep"]]
        n_dh = 2 + len(deps)
    else:
        nk, n_dh = 1, 1

    def elementwise(i, dhv, x_ref, g_ref, sc_ref, res_refs, out_refs):
        xv = x_ref[...]
        r = lax.rsqrt(jnp.mean(xv * xv, axis=-1, keepdims=True) + EPS)
        xh = xv * r
        n = xh * g_ref[...]
        dn = dhv * (1.0 + sc_ref[...])
        a = dn * g_ref[...]
        rows = [jnp.sum(dhv, axis=0, keepdims=True), jnp.sum(dhv * n, axis=0, keepdims=True),
                jnp.sum(dn * xh, axis=0, keepdims=True)]
        if has_res:
            dres_ref, gsrc_ref, gate_ref = res_refs
            dx_ref, dxg_ref, st_ref = out_refs
            dr = dres_ref[...]
            dx = dr + r * (a - xh * jnp.mean(a * xh, axis=-1, keepdims=True))
            dx_ref[...] = dx
            dxg_ref[...] = (dx * gate_ref[...]).astype(BF)
            rows.append(jnp.sum(dr * gsrc_ref[...], axis=0, keepdims=True))
        else:
            st_ref, = out_refs
            rows.append(jnp.zeros((1, D), F32))

        @pl.when(i == 0)
        def _():
            for k, row in enumerate(rows):
                st_ref[k:k + 1, :] = row

        @pl.when(i > 0)
        def _():
            for k, row in enumerate(rows):
                st_ref[k:k + 1, :] += row

    def body(*refs):
        x_ref, dh_refs, g_ref, sc_ref = refs[0], refs[1:1 + n_dh], refs[1 + n_dh], refs[2 + n_dh]
        rest = refs[3 + n_dh:]
        res_refs, rest = (rest[:3], rest[3:]) if has_res else ((), rest)
        out_refs = rest[:3] if has_res else rest[:1]
        i = pl.program_id(0)
        if not fused:
            elementwise(i, dh_refs[0][...], x_ref, g_ref, sc_ref, res_refs, out_refs)
            return
        acc = rest[-1]
        k = pl.program_id(1)
        part = lax.dot_general(dh_refs[0][...].astype(BF), dh_refs[1][...].astype(BF),
                               (((1,), (1 if tb else 0,)), ((), ())), preferred_element_type=F32)
        if nk == 1:
            elementwise(i, part, x_ref, g_ref, sc_ref, res_refs, out_refs)
            return

        @pl.when(k == 0)
        def _():
            acc[...] = part

        @pl.when(k > 0)
        def _():
            acc[...] += part

        @pl.when(k == nk - 1)
        def _():
            elementwise(i, acc[...], x_ref, g_ref, sc_ref, res_refs, out_refs)

    rowb = lambda w: pl.BlockSpec((1, w), lambda i, *k: (0, 0))
    blk = pl.BlockSpec((tm, D), lambda i, *k: (i, 0))
    st_spec = pl.BlockSpec((4, D), lambda i, *k: (0, 0))
    st_shape = jax.ShapeDtypeStruct((4, D), F32)
    if fused:
        if a_stack:
            nhb = K // 2 // tk
            a_spec = pl.BlockSpec((None, tm, tk), lambda i, k: (k // nhb, i, k % nhb))
        else:
            a_spec = pl.BlockSpec((tm, tk), lambda i, k: (i, k))
        b_spec = pl.BlockSpec((D, tk), lambda i, k: (0, k)) if tb else pl.BlockSpec((tk, D), lambda i, k: (k, 0))
        dh_specs = [a_spec, b_spec] + [pl.BlockSpec(memory_space=pl.ANY)] * len(deps)
        dh_args = [dh["a"], dh["b"]] + deps
        grid, sem = (R // tm, nk), ("arbitrary", "arbitrary")
        scratch = [pltpu.VMEM((tm, D) if nk > 1 else (8, 128), F32)]
    else:
        dh_specs, dh_args, grid, sem, scratch = [blk], [dh], (R // tm,), ("arbitrary",), []
    cp = pltpu.CompilerParams(dimension_semantics=sem)
    if has_res:
        return pl.pallas_call(
            body, name=name, grid=grid, in_specs=[blk] + dh_specs + [rowb(D), rowb(D), blk, blk, rowb(D)],
            out_specs=[blk, blk, st_spec], scratch_shapes=scratch,
            out_shape=[jax.ShapeDtypeStruct((R, D), F32), jax.ShapeDtypeStruct((R, D), BF), st_shape],
            compiler_params=cp,
        )(x, *dh_args, g, sc, dres, gsrc, gate)
    return pl.pallas_call(
        body, name=name, grid=grid, in_specs=[blk] + dh_specs + [rowb(D), rowb(D)],
        out_specs=st_spec, out_shape=st_shape, scratch_shapes=scratch, compiler_params=cp,
    )(x, *dh_args, g, sc)


def ffn_act_bwd(u0, df, cw, cb, *, name, tc=128):
    nb = DFF // tc

    def body(u_ref, df_ref, wg_ref, wv_ref, bg_ref, bv_ref, du_ref, dw_ref, db_ref):
        wg = [wg_ref[k:k + 1, :] for k in range(3)]
        wv = [wv_ref[k:k + 1, :] for k in range(3)]
        bg, bv = bg_ref[...], bv_ref[...]

        def chunk(r0, first, last, acc):
            xg, xv = _ext_rows(u_ref.at[0], r0, first, last), _ext_rows(u_ref.at[1], r0, first, last)
            dfe = _ext_rows(df_ref, r0, first, last)
            xg_d, xg_u, xv_d, xv_u = _roll_dn(xg), _roll_up(xg), _roll_dn(xv), _roll_up(xv)
            ug = bg + xg_d * wg[0] + xg * wg[1] + xg_u * wg[2]
            uv = bv + xv_d * wv[0] + xv * wv[1] + xv_u * wv[2]
            sig = jax.nn.sigmoid(ug)
            dug = dfe * uv * (sig * (1.0 + ug * (1.0 - sig)))
            duv = dfe * (ug * sig)
            rows = _center_rows(r0, first, last)
            du_ref[0, rows, :] = (_roll_up(dug) * wg[0] + dug * wg[1] + _roll_dn(dug) * wg[2])[_CTR].astype(BF)
            du_ref[1, rows, :] = (_roll_up(duv) * wv[0] + duv * wv[1] + _roll_dn(duv) * wv[2])[_CTR].astype(BF)
            terms = [dug * xg_d, dug * xg, dug * xg_u, dug, duv * xv_d, duv * xv, duv * xv_u, duv]
            return tuple(a + jnp.sum(t[_CTR], axis=0, keepdims=True) for a, t in zip(acc, terms))

        acc = _row_chunks(T, chunk, tuple(jnp.zeros((1, tc), F32) for _ in range(8)))
        for k in range(3):
            dw_ref[0, k:k + 1, :] = acc[k]
            dw_ref[1, k:k + 1, :] = acc[4 + k]
        db_ref[0] = acc[3]
        db_ref[1] = acc[7]

    lo = lambda r: pl.BlockSpec((r, tc), lambda j: (0, j))
    hi = lambda r: pl.BlockSpec((r, tc), lambda j: (0, nb + j))
    st = lambda r: pl.BlockSpec((2, r, tc), lambda j: (0, 0, j))
    return pl.pallas_call(
        body, name=name, grid=(nb,),
        in_specs=[st(T), lo(T), lo(3), hi(3), lo(1), hi(1)],
        out_specs=[st(T), st(3), st(1)],
        out_shape=[jax.ShapeDtypeStruct((2, T, DFF), BF), jax.ShapeDtypeStruct((2, 3, DFF), F32),
                   jax.ShapeDtypeStruct((2, 1, DFF), F32)],
        compiler_params=pltpu.CompilerParams(dimension_semantics=("parallel",)),
    )(u0, df, cw, cw, cb, cb)


def convz_bwd(p, dz, cw, cb, dp, *, name):
    o0 = O_CV // (3 * CVB)

    def body(p_ref, dz_ref, w_ref, bias_ref, dp_in, dp_ref, dw_ref, dbias_ref):
        xv, bv, cv = p_ref[:, 0:CVB], p_ref[:, CVB:2 * CVB], p_ref[:, 2 * CVB:3 * CVB]
        ci = cv * xv
        dwc = _conv(ci, w_ref, bias_ref)
        dzv = dz_ref[...]
        ddw = dzv * bv
        dci = _conv_t(ddw, w_ref)
        dp_ref[:, 0:CVB] = (dci * cv).astype(BF)
        dp_ref[:, CVB:2 * CVB] = (dzv * dwc).astype(BF)
        dp_ref[:, 2 * CVB:3 * CVB] = (dci * xv).astype(BF)
        _conv_wgrad(dw_ref, ddw, ci)
        dbias_ref[...] = jnp.sum(ddw, axis=0, keepdims=True)

    own = lambda r: pl.BlockSpec((r, CVB), lambda j: (0, j))
    return pl.pallas_call(
        body, name=name, grid=(CONV // CVB,),
        in_specs=[pl.BlockSpec((T, 3 * CVB), lambda j: (0, o0 + j)), own(T), own(3), own(1),
                  pl.BlockSpec(memory_space=pl.ANY)],
        out_specs=[pl.BlockSpec((T, 3 * CVB), lambda j: (0, o0 + j)), own(3), own(1)],
        out_shape=[jax.ShapeDtypeStruct((T, NIN), BF), jax.ShapeDtypeStruct((3, CONV), F32),
                   jax.ShapeDtypeStruct((1, CONV), F32)],
        input_output_aliases={4: 0},
        compiler_params=pltpu.CompilerParams(dimension_semantics=("parallel",)),
    )(p, dz, cw, cb, dp)


def attn_bwd(q, k, v, do, o, lse, dep, *, name, tq=1024, kc=768):
    NKC, KC = TKV // kc, kc
    deps = [] if dep is None else [dep]

    def body(q_ref, k_ref, v_ref, do_ref, o_ref, lse_ref, *rest):
        dq_ref, dk_ref, dv_ref = rest[len(deps):]
        h, i = pl.program_id(0), pl.program_id(1)

        @pl.when(i == 0)
        def _():
            dk_ref[...] = jnp.zeros_like(dk_ref)

        @pl.when((i == 0) & (h % 2 == 0))
        def _():
            dv_ref[...] = jnp.zeros_like(dv_ref)

        qv = q_ref[...]
        dom = jnp.where(_head_mask(h), do_ref[...], jnp.zeros_like(do_ref[...]))
        delta = jnp.sum(dom.astype(F32) * o_ref[...].astype(F32), axis=-1, keepdims=True)
        lse = lse_ref[:, 0:1]
        dq = jnp.zeros((tq, HP), F32)
        for c in range(NKC):
            cols = slice(c * KC, (c + 1) * KC)
            s = lax.dot_general(qv, k_ref[cols, :], (((1,), (1,)), ((), ())),
                                preferred_element_type=F32) * (SCALE * LOG2E)
            pr = jnp.exp2(s - lse)
            dp = lax.dot_general(dom, v_ref[cols, :], (((1,), (1,)), ((), ())), preferred_element_type=F32)
            ds = (pr * (dp - delta) * SCALE).astype(BF)
            dq = dq + jnp.dot(ds, k_ref[cols, :], preferred_element_type=F32)
            dk_ref[cols, :] += lax.dot_general(ds, qv, (((0,), (0,)), ((), ())), preferred_element_type=F32)
            dv_ref[cols, :] += lax.dot_general(pr.astype(BF), dom, (((0,), (0,)), ((), ())), preferred_element_type=F32)
        dq_ref[...] = dq

    return pl.pallas_call(
        body, name=name, grid=(NH, T // tq),
        in_specs=[pl.BlockSpec((tq, HP), lambda h, i: (i, h)), pl.BlockSpec((TKV, HP), lambda h, i: (0, h)),
                  pl.BlockSpec((TKV, 2 * DV), lambda h, i: (0, h // 2)), pl.BlockSpec((tq, 2 * DV), lambda h, i: (i, h // 2)),
                  pl.BlockSpec((tq, 2 * DV), lambda h, i: (i, h // 2)), pl.BlockSpec((tq, HP), lambda h, i: (i, h)),
                  *([pl.BlockSpec(memory_space=pl.ANY)] * len(deps))],
        out_specs=[pl.BlockSpec((tq, HP), lambda h, i: (i, h)), pl.BlockSpec((TKV, HP), lambda h, i: (0, h)),
                   pl.BlockSpec((TKV, 2 * DV), lambda h, i: (0, h // 2))],
        out_shape=[jax.ShapeDtypeStruct((T, NH * HP), F32), jax.ShapeDtypeStruct((TKV, NH * HP), F32),
                   jax.ShapeDtypeStruct((TKV, NH * DV), F32)],
        compiler_params=pltpu.CompilerParams(dimension_semantics=("arbitrary", "arbitrary")),
    )(q, k, v, do, o, lse, *deps)


def qprep_bwd(p, dq, qg, wq2, cq_t, sq_t, dp, *, name, tm=256):
    qcol = O_Q // 512

    def body(p_ref, dq_ref, g_ref, w_ref, c_ref, s_ref, dp_in, dp_ref, dq2_ref, dg_ref):
        i = pl.program_id(0)
        dqv = dq_ref[...]
        cc = jnp.concatenate([c_ref[...]] * NH, axis=1)
        ss = jnp.concatenate([s_ref[...]] * NH, axis=1)
        dq2 = jnp.concatenate([dqv * cc, dqv * ss], axis=1).astype(BF)
        dq2_ref[...] = dq2
        dcq = lax.dot_general(dq2, w_ref[...], (((1,), (1,)), ((), ())), preferred_element_type=F32)
        pq = p_ref[...]
        r = lax.rsqrt(jnp.sum(pq * pq, axis=-1, keepdims=True) * (1.0 / QL) + EPS)
        xh = pq * r
        a = dcq * g_ref[...]
        dp_ref[...] = (r * (a - xh * (jnp.sum(a * xh, axis=-1, keepdims=True) * (1.0 / QL)))).astype(BF)
        dg = jnp.sum(dcq * xh, axis=0, keepdims=True)

        @pl.when(i == 0)
        def _():
            dg_ref[...] = dg

        @pl.when(i > 0)
        def _():
            dg_ref[...] += dg

    return pl.pallas_call(
        body, name=name, grid=(T // tm,),
        in_specs=[pl.BlockSpec((tm, 512), lambda i: (i, qcol)), pl.BlockSpec((tm, NH * HP), lambda i: (i, 0)), _row(512),
                  pl.BlockSpec((512, 2 * NH * HP), lambda i: (0, 0)),
                  pl.BlockSpec((tm, HP), lambda i: (i, 0)), pl.BlockSpec((tm, HP), lambda i: (i, 0)),
                  pl.BlockSpec(memory_space=pl.ANY)],
        out_specs=[pl.BlockSpec((tm, 512), lambda i: (i, qcol)), pl.BlockSpec((tm, 2 * NH * HP), lambda i: (i, 0)), _row(512)],
        out_shape=[jax.ShapeDtypeStruct((T, NIN), BF), jax.ShapeDtypeStruct((T, 2 * NH * HP), BF),
                   jax.ShapeDtypeStruct((1, 512), F32)],
        input_output_aliases={6: 0},
        compiler_params=pltpu.CompilerParams(dimension_semantics=("arbitrary",)),
    )(p, dq, qg, wq2, cq_t, sq_t, dp)


def kvprep_bwd(pc, p, dk, dv, kvg, wkv2, ck, sk, dp, *, name, tm=256):
    assert tm == TC
    nb = TKV // tm
    kvcol = O_KV // 512

    def body(pc_ref, p_ref, dk_ref, dv_ref, g_ref, w_ref, ck_ref, sk_ref, dp_in, dp_ref, dpc_ref, dkv2_ref, dg_ref):
        i = pl.program_id(0)
        t = jnp.where(i == NLAT, pc_ref[...], p_ref[...])
        pk = t[:, :KVL]
        r = lax.rsqrt(jnp.mean(pk * pk, axis=-1, keepdims=True) + EPS)
        xh = pk * r
        dkv = dk_ref[...]
        dkv2 = jnp.concatenate([dkv, dv_ref[...]], axis=1).astype(BF)
        dkv2_ref[...] = dkv2
        dckv = lax.dot_general(dkv2, w_ref[...], (((1,), (1,)), ((), ())), preferred_element_type=F32)
        a = dckv * g_ref[...]
        dpk = r * (a - xh * jnp.mean(a * xh, axis=-1, keepdims=True))
        dkr = dkv[:, 0:HP]
        for hh in range(1, NH):
            dkr = dkr + dkv[:, hh * HP:(hh + 1) * HP]
        res = jnp.concatenate([dpk, dkr * ck_ref[...], dkr * sk_ref[...]], axis=1).astype(BF)
        dg = jnp.sum(dckv * xh, axis=0, keepdims=True)

        @pl.when(i == 0)
        def _():
            dg_ref[...] = dg

        @pl.when(i > 0)
        def _():
            dg_ref[...] += dg

        @pl.when(i < NLAT)
        def _():
            dp_ref[...] = res

        @pl.when(i == NLAT)
        def _():
            dpc_ref[...] = res

    rb = lambda w: pl.BlockSpec((tm, w), lambda i: (i, 0))
    return pl.pallas_call(
        body, name=name, grid=(nb,),
        in_specs=[pl.BlockSpec((tm, 512), lambda i: (0, 0)),
                  pl.BlockSpec((tm, 512), lambda i: (jnp.minimum(i, NLAT - 1), kvcol)),
                  rb(NH * HP), rb(NH * DV), _row(KVL), pl.BlockSpec((KVL, NH * HP + NH * DV), lambda i: (0, 0)),
                  rb(HP), rb(HP), pl.BlockSpec(memory_space=pl.ANY)],
        out_specs=[pl.BlockSpec((tm, 512), lambda i: (jnp.minimum(i, NLAT - 1), kvcol)),
                   pl.BlockSpec((tm, 512), lambda i: (0, 0)), rb(NH * HP + NH * DV), _row(KVL)],
        out_shape=[jax.ShapeDtypeStruct((T, NIN), BF), jax.ShapeDtypeStruct((TC, 512), BF),
                   jax.ShapeDtypeStruct((TKV, NH * HP + NH * DV), BF), jax.ShapeDtypeStruct((1, KVL), F32)],
        input_output_aliases={8: 0},
        compiler_params=pltpu.CompilerParams(dimension_semantics=("arbitrary",)),
    )(pc, p, dk, dv, kvg, wkv2, ck, sk, dp)


def _pieces(src, width, n):
    out, c = [], src
    while c < src + width:
        k = c // n
        w = min(src + width, (k + 1) * n) - c
        out.append((k, c - k * n, c - src, w))
        c += w
    return out


def _win_moves():
    mv = [(2208, 1024, O_GA), (3232, 1024, O_GC), (0, KVL, O_KV), (256, DR, O_KV + KVL + DN), (288, QL, O_Q)]
    mv += [(256 + _swap_start(g), 8, O_KV + KVL + HP + DN + 8 * g) for g in range(4)]
    for j in range(CONV // CVB):
        base = O_CV + 3 * CVB * j
        mv += [(672 + CVB * j, CVB, base), (1184 + CVB * j, CVB, base + CVB), (1696 + CVB * j, CVB, base + 2 * CVB)]
    return mv


_WIN_ZERO = [(O_KV + KVL, DN), (O_KV + KVL + DN + DR, HP - DN - DR), (O_KV + KVL + HP, DN),
             (O_KV + KVL + HP + DN + DR, HP - DN - DR), (O_Q + QL, 512 - QL)]


def build_win(g, *, name, tm=256):
    def body(g_ref, o_ref):
        for src, w, dst in _win_moves():
            for k, a, off, pw in _pieces(src, w, SH_IN):
                o_ref[:, dst + off:dst + off + pw] = g_ref[k, :, a:a + pw]
        for c0, w in _WIN_ZERO:
            o_ref[:, c0:c0 + w] = jnp.zeros((tm, w), o_ref.dtype)

    return pl.pallas_call(
        body, name=name, grid=(D // tm,), in_specs=[pl.BlockSpec((NDEV, tm, SH_IN), lambda i: (0, i, 0))],
        out_specs=pl.BlockSpec((tm, NIN), lambda i: (i, 0)), out_shape=jax.ShapeDtypeStruct((D, NIN), g.dtype),
        compiler_params=pltpu.CompilerParams(dimension_semantics=("parallel",)),
    )(g)


def shard_win_grad(dwt, dwct, *, name, tc=256):
    def body(dw_ref, dwc_ref, o_ref, kvs):
        kvs[...] = dw_ref[O_KV:O_KV + 512, :] + dwc_ref[...]

        def src(row, w):
            if O_KV <= row < O_KV + 512:
                return kvs[row - O_KV:row - O_KV + w, :]
            return dw_ref[row:row + w, :]

        for s, w, dst in _win_moves():
            if w == 8 or s == 256:
                continue
            for k, a, off, pw in _pieces(s, w, SH_IN):
                o_ref[k, a:a + pw, :] = src(dst + off, pw).astype(o_ref.dtype)
        for g in range(4):
            val = src(O_KV + KVL + DN + 8 * g, 8) + src(O_KV + KVL + HP + DN + _swap_start(g), 8)
            o_ref[0, 256 + 8 * g:256 + 8 * g + 8, :] = val.astype(o_ref.dtype)

    return pl.pallas_call(
        body, name=name, grid=(D // tc,),
        in_specs=[pl.BlockSpec((NIN, tc), lambda j: (0, j)), pl.BlockSpec((512, tc), lambda j: (0, j))],
        out_specs=pl.BlockSpec((NDEV, SH_IN, tc), lambda j: (0, 0, j)),
        out_shape=jax.ShapeDtypeStruct((NDEV, SH_IN, D), BF),
        scratch_shapes=[pltpu.VMEM((512, tc), F32)],
        compiler_params=pltpu.CompilerParams(dimension_semantics=("parallel",)),
    )(dwt, dwct)


def _eye(n, m):
    return (lax.broadcasted_iota(jnp.int32, (n, m), 0) == lax.broadcasted_iota(jnp.int32, (n, m), 1)).astype(BF)


_NT = (((1,), (1,)), ((), ()))


def build_wq_wkv(gq, gkv, *, name):
    def body(gq_ref, gkv_ref, q_ref, kv_ref):
        q_ref[...] = jnp.zeros_like(q_ref)
        kv_ref[...] = jnp.zeros_like(kv_ref)
        eye = _eye(QL, QL)
        for h in range(NH):
            qh = lax.dot_general(eye, gq_ref[h], _NT, preferred_element_type=F32).astype(q_ref.dtype)
            q_ref[0:QL, h * HP:h * HP + DN + DR] = qh
            for g in range(4):
                c0 = NH * HP + h * HP + DN + 8 * g
                q_ref[0:QL, c0:c0 + 8] = qh[:, DN + _swap_start(g):DN + _swap_start(g) + 8]
            kv_ref[:, h * HP:h * HP + DN] = gkv_ref[h, :, 0:DN]
            kv_ref[:, NH * HP + h * DV:NH * HP + (h + 1) * DV] = gkv_ref[h, :, DN:DN + DV]

    vm = pl.BlockSpec(memory_space=pltpu.VMEM)
    return pl.pallas_call(
        body, name=name, in_specs=[vm, vm], out_specs=[vm, vm],
        out_shape=[jax.ShapeDtypeStruct((512, 2 * NH * HP), gq.dtype), jax.ShapeDtypeStruct((KVL, NH * HP + NH * DV), gq.dtype)],
    )(gq, gkv)


def shard_wq_wkv_grad(dwq2, dwkv2, *, name):
    def body(q_ref, kv_ref, gq_ref, gkv_ref, xs):
        xs[...] = jnp.zeros_like(xs)
        eye = _eye(DN + DR, HP)
        for h in range(NH):
            xs[:, 0:DN] = q_ref[0:QL, h * HP:h * HP + DN].astype(BF)
            for g in range(4):
                a = q_ref[0:QL, h * HP + DN + 8 * g:h * HP + DN + 8 * g + 8]
                c0 = NH * HP + h * HP + DN + _swap_start(g)
                xs[:, DN + 8 * g:DN + 8 * g + 8] = (a + q_ref[0:QL, c0:c0 + 8]).astype(BF)
            gq_ref[h] = lax.dot_general(eye, xs[...], _NT, preferred_element_type=F32).astype(BF)
            gkv_ref[h, :, 0:DN] = kv_ref[:, h * HP:h * HP + DN].astype(BF)
            gkv_ref[h, :, DN:DN + DV] = kv_ref[:, NH * HP + h * DV:NH * HP + (h + 1) * DV].astype(BF)

    vm = pl.BlockSpec(memory_space=pltpu.VMEM)
    return pl.pallas_call(
        body, name=name, in_specs=[vm, vm], out_specs=[vm, vm],
        out_shape=[jax.ShapeDtypeStruct((NDEV, DN + DR, QL), BF), jax.ShapeDtypeStruct((NDEV, KVL, DN + DV), BF)],
        scratch_shapes=[pltpu.VMEM((QL, HP), BF)],
    )(dwq2, dwkv2)


def unshard_cols(g, *, name, tm=256):
    _, K, n = g.shape
    tm = _pick(K, tm, 16)

    def body(g_ref, o_ref):
        for k in range(NDEV):
            o_ref[:, k * n:(k + 1) * n] = g_ref[k]

    return pl.pallas_call(
        body, name=name, grid=(K // tm,), in_specs=[pl.BlockSpec((NDEV, tm, n), lambda i: (0, i, 0))],
        out_specs=pl.BlockSpec((tm, NDEV * n), lambda i: (i, 0)), out_shape=jax.ShapeDtypeStruct((K, NDEV * n), g.dtype),
        compiler_params=pltpu.CompilerParams(dimension_semantics=("parallel",)),
    )(g)


def shard_cols(w, *, name, tm=256):
    K, n8 = w.shape
    n = n8 // NDEV
    tm = _pick(K, tm, 16)

    def body(w_ref, o_ref):
        for k in range(NDEV):
            o_ref[k] = w_ref[:, k * n:(k + 1) * n]

    return pl.pallas_call(
        body, name=name, grid=(K // tm,), in_specs=[pl.BlockSpec((tm, n8), lambda i: (i, 0))],
        out_specs=pl.BlockSpec((NDEV, tm, n), lambda i: (0, i, 0)), out_shape=jax.ShapeDtypeStruct((NDEV, K, n), w.dtype),
        compiler_params=pltpu.CompilerParams(dimension_semantics=("parallel",)),
    )(w)


def _rope_tables():
    t = np.arange(T)
    row = (t // GRID_W).astype(np.float32)
    col = (t % GRID_W).astype(np.float32)
    axis_dim = DR // 2
    inv = (np.float32(ROPE_THETA) ** (-np.arange(0, axis_dim, 2, dtype=np.float32) / np.float32(axis_dim))).astype(np.float32)
    ar, ac = (row[:, None] * inv).astype(np.float32), (col[:, None] * inv).astype(np.float32)
    cosv = np.concatenate([np.cos(ar), np.cos(ar), np.cos(ac), np.cos(ac)], axis=1).astype(np.float32)
    sinv = np.concatenate([-np.sin(ar), np.sin(ar), -np.sin(ac), np.sin(ac)], axis=1).astype(np.float32)
    ck = np.zeros((TKV, HP), np.float32)
    sk = np.zeros((TKV, HP), np.float32)
    ck[T:, DN:DN + DR] = 1.0
    ck[:T, DN:DN + DR] = cosv
    sk[:T, DN:DN + DR] = sinv
    cq = np.zeros((T, HP), np.float32)
    cq[:, :DN] = 1.0
    cq[:, DN:DN + DR] = cosv
    return jnp.asarray(ck), jnp.asarray(sk), jnp.asarray(cq), jnp.asarray(sk[:T])


def _local_step(x, ctx, tgt, mod_lat, mod_ctx, n1g, qg, kvg, n2g, fg, conv_w, conv_b, ffn_w, ffn_b, get_w, put_g, dep0):
    sh1, sc1, g1, sh2, sc2, g2 = [mod_lat[:, i * D:(i + 1) * D] for i in range(6)]
    csh1, csc1 = mod_ctx[:, 0:D], mod_ctx[:, D:2 * D]
    ck, sk, cq_t, sq_t = _rope_tables()
    qg_p = jnp.pad(qg, ((0, 0), (0, 512 - QL)))

    hcat = normmod_cat(ctx, x, n1g, csc1, csh1, sc1, sh1, dep0, name="normmod1")
    win = get_w("in", hcat)
    p = mm(hcat, win, M=T, tn=768, name="in_proj")
    pc = mm(hcat, win, M=TC, N=512, a_off=(T, 0), b_off=(0, O_KV), name="in_proj_ctx")
    wq2, wkv2, wao, wco, wo = get_w("mid", p)
    kh, vh, ckv = kvprep(pc, p, kvg, wkv2, ck, sk, name="kvprep")
    qr, cq = qprep(p, qg_p, wq2, cq_t, sq_t, name="qprep")
    o, lse = attn_fwd(qr, kh, vh, name="attn_fwd")
    z = convz(p, conv_w, conv_b, name="convz")
    ya, yc, merged = out_proj_merge(o, wao, z, wco, p, name="attn_conv_out_gate_merge")
    a_out, x1, h2 = oproj_resid(merged, wo, x, g1, n2g, sc2, sh2, name="o_proj_resid_normmod2")
    wup = get_w("up", h2)
    u0 = mm(h2, wup, tb=True, o_stack=True, tn=1408, name="up_proj")
    f = ffn_act(u0, ffn_w, ffn_b, name="ffn_act")
    wdn = get_w("down", f)
    dn, dx2, dd, dfg, loss = down_final(f, wdn, x1, g2, fg, tgt, name="down_proj_final_loss")

    df = mm(dd, wdn, tb=True, tn=1408, name="down_proj_dx")
    dwdn = mm(f, dd, ta=True, out_dtype=BF, tm=1408, name="down_proj_dw")
    du0, dffn_w, dffn_b = ffn_act_bwd(u0, df, ffn_w, ffn_b, name="ffn_act_bwd")
    dwup = mm(du0, h2, ta=True, a_stack=True, out_dtype=BF, tm=1408, name="up_proj_dw")
    tok = put_g("ffn", dict(dwup=dwup, dwdn=dwdn))
    dx1, da, st2 = normmod_bwd(x1, dict(a=du0, b=wup, a_stack=True, tk=DFF, dep=tok), n2g, sc2, dx2, dn, g1,
                               name="up_proj_dx_normmod2_bwd")

    dwo = mm(merged, da, ta=True, out_dtype=BF, tn=512, name="o_proj_dw")
    dya, dyc, dp, do, dz = oproj_dx_gate_bwd(da, wo, p, ya, yc, wao, wco, name="o_proj_dx_gate_merge_bwd")
    dwao = mm(o, dya, ta=True, out_dtype=BF, tn=512, name="attn_out_dw")
    dwco = mm(z, dyc, ta=True, out_dtype=BF, tn=512, name="conv_out_dw")
    tok = put_g("mid", dict(dwao=dwao, dwco=dwco, dwo=dwo))
    dp, dconv_w, dconv_b = convz_bwd(p, dz, conv_w, conv_b, dp, name="convz_bwd")
    dq, dk, dv = attn_bwd(qr, kh, vh, do, o, lse, tok, name="attn_bwd")
    dp, dq2, dqg = qprep_bwd(p, dq, qg_p, wq2, cq_t, sq_t, dp, name="qprep_bwd")
    dp, dpc, dkv2, dkvg = kvprep_bwd(pc, p, dk, dv, kvg, wkv2, ck, sk, dp, name="kvprep_bwd")

    dwin = mm(dp, hcat, ta=True, K=T, tm=768, name="in_proj_dw")
    dwin_c = mm(dpc, hcat, ta=True, K=TC, b_off=(T, 0), name="in_proj_ctx_dw")
    tok = put_g("in", dict(dwin=dwin, dwin_c=dwin_c))
    dwq2 = mm(cq, dq2, ta=True, dep=tok, name="q_up_dw")
    dwkv2 = mm(ckv, dkv2, ta=True, name="kv_up_dw")
    tok = put_g("qkv", dict(dwq2=dwq2, dwkv2=dwkv2))
    dhc = mm(dpc, win, tb=True, N=D, K=512, b_off=(0, O_KV), name="in_proj_ctx_dx")
    dx, _, st1 = normmod_bwd(x, dict(a=dp, b=win, tb=True, tk=NIN, dep=tok), n1g, sc1, dx1, a_out, g1,
                             name="in_proj_dx_normmod1_bwd")
    stc = normmod_bwd(ctx, dhc, n1g, csc1, None, None, None, name="normmod1_ctx_bwd")

    zrow = jnp.zeros((1, D), F32)
    dmod_lat = jnp.concatenate([st1[0:1], st1[1:2], st1[3:4], st2[0:1], st2[1:2], st2[3:4]], axis=1)
    dmod_ctx = jnp.concatenate([stc[0:1], stc[1:2], zrow, zrow, zrow, zrow], axis=1)
    return dict(
        loss=loss, dx=dx, dmod_lat=dmod_lat, dmod_ctx=dmod_ctx,
        dn1g=st1[2:3] + stc[2:3], dqg=dqg, dkvg=dkvg, dn2g=st2[2:3], dfg=dfg,
        dconv_w=dconv_w, dconv_b=dconv_b, dffn_w=dffn_w, dffn_b=dffn_b)


def _me():
    x, y, c = lax.axis_index("x"), lax.axis_index("y"), lax.axis_index("c")
    return x, y, c, 4 * x + 2 * y + c


def _peer(x, y, c, k):
    px = 1 - x if k & 4 else x
    py = 1 - y if k & 2 else y
    pc = 1 - c if k & 1 else c
    return (px, py, pc), 4 * px + 2 * py + pc


def _exchange_tiles(src_of_peer, buf, send_sem, recv_sem):
    x, y, c, me = _me()
    for k in range(1, NDEV):
        dev, lin = _peer(x, y, c, k)
        pltpu.make_async_remote_copy(src_ref=src_of_peer(lin), dst_ref=buf.at[me], send_sem=send_sem, recv_sem=recv_sem,
                                     device_id=dev, device_id_type=MESH).start()
    seven = buf.at[pl.ds(0, NDEV - 1)]
    pltpu.make_async_remote_copy(src_ref=seven, dst_ref=seven, send_sem=send_sem, recv_sem=recv_sem,
                                 device_id=(x, y, c), device_id_type=MESH).wait()


def _silu(z):
    return z * jax.nn.sigmoid(z)


def ada_fwd(c, c_ctx, ffn_w, conv_w, w_shard, b_ada, deps, *, name):
    nsh = w_shard.shape[1]
    deps = [d for d in deps if d is not None]

    def body(c_ref, cc_ref, fw_ref, cw_ref, w_ref, b_ref, *rest):
        s_ref, m_ref, mine, res, sems = rest[len(deps):]
        x, y, c, me = _me()
        mine[...] = jnp.zeros_like(mine)
        mine[0:1, :] = _silu(c_ref[...])
        mine[1:2, :] = _silu(cc_ref[...])
        for k in range(3):
            mine[2 + k:3 + k, 0:fw_ref.shape[2]] = fw_ref[k]
            mine[5 + k:6 + k, 0:cw_ref.shape[2]] = cw_ref[k]
        s_ref[me] = mine[...]
        _exchange_tiles(lambda lin: mine, s_ref, sems.at[0], sems.at[1])
        sall = s_ref[...].reshape(NDEV * 8, D).astype(BF)
        r = jnp.dot(sall, w_ref[...].astype(BF), preferred_element_type=F32) + b_ref[me]
        res[...] = r.reshape(NDEV, 8, nsh)
        m_ref[me] = res[me]
        _exchange_tiles(lambda lin: res.at[lin], m_ref, sems.at[2], sems.at[3])

    vm = pl.BlockSpec(memory_space=pltpu.VMEM)
    return pl.pallas_call(
        body, name=name, in_specs=[vm] * 6 + [pl.BlockSpec(memory_space=pl.ANY)] * len(deps), out_specs=[vm, vm],
        out_shape=[jax.ShapeDtypeStruct((NDEV, 8, D), F32), jax.ShapeDtypeStruct((NDEV, 8, nsh), F32)],
        scratch_shapes=[pltpu.VMEM((8, D), F32), pltpu.VMEM((NDEV, 8, nsh), F32), pltpu.SemaphoreType.DMA((4,))],
    )(c, c_ctx, ffn_w, conv_w, w_shard, b_ada, *deps)


P_DML, P_DMC, P_N1, P_QG, P_KVG, P_CB, P_N2, P_FB, P_FG, P_CW, P_FW, P_LOSS, P_ROWS = 0, 6, 12, 13, 14, 15, 16, 17, 23, 24, 27, 45, 48
FROWS = 3


def pack_small(r, *, name):
    ins = [r["dmod_lat"], r["dmod_ctx"], r["dn1g"], r["dqg"], r["dkvg"], r["dconv_b"], r["dn2g"], r["dffn_b"], r["dfg"],
           r["dconv_w"], r["dffn_w"], r["loss"]]

    def put_wide(p, row0, row, n):
        for j in range(-(-n // D)):
            w = min(D, n - j * D)
            p[row0 + j:row0 + j + 1, 0:w] = row[:, j * D:j * D + w]

    def body(dml, dmc, n1, qg, kvg, cb, n2, fb, fg, cw, fw, loss, p):
        p[...] = jnp.zeros_like(p)
        put_wide(p, P_DML, dml, 6 * D)
        put_wide(p, P_DMC, dmc, 6 * D)
        put_wide(p, P_N1, n1, D)
        put_wide(p, P_QG, qg, 512)
        put_wide(p, P_KVG, kvg, KVL)
        put_wide(p, P_CB, cb, CONV)
        put_wide(p, P_N2, n2, D)
        put_wide(p, P_FG, fg, D)
        put_wide(p, P_LOSS, loss, 128)
        for s in range(2):
            put_wide(p, P_FB + FROWS * s, fb.at[s], DFF)
        for k in range(3):
            put_wide(p, P_CW + k, cw.at[k:k + 1], CONV)
            for s in range(2):
                put_wide(p, P_FW + FROWS * (2 * k + s), fw.at[s, k:k + 1], DFF)

    vm = pl.BlockSpec(memory_space=pltpu.VMEM)
    return pl.pallas_call(
        body, name=name, in_specs=[vm] * len(ins), out_specs=vm, out_shape=jax.ShapeDtypeStruct((P_ROWS, D), F32),
    )(*ins)


def sum_slots(a, *, name):
    rows = dict(norm1_g=(P_N1, D), q_norm_g=(P_QG, QL), kv_norm_g=(P_KVG, KVL), conv_b=(P_CB, CONV), norm2_g=(P_N2, D),
                final_g=(P_FG, D))

    def body(a_ref, sum_ref, *out):
        acc = a_ref[0]
        for k in range(1, NDEV):
            acc = acc + a_ref[k]
        sum_ref[...] = acc
        for ref, (row, n) in zip(out, rows.values()):
            ref[...] = sum_ref[row:row + 1, 0:n]
        fb, bada = out[len(rows):]
        for s in range(2):
            for j in range(FROWS):
                w = min(D, DFF - j * D)
                row = P_FB + FROWS * s + j
                fb[:, s * DFF + j * D:s * DFF + j * D + w] = sum_ref[row:row + 1, 0:w]
        for j in range(6):
            bada[:, j * D:(j + 1) * D] = sum_ref[P_DML + j:P_DML + j + 1, :] + sum_ref[P_DMC + j:P_DMC + j + 1, :]

    vm = pl.BlockSpec(memory_space=pltpu.VMEM)
    widths = [n for _, n in rows.values()] + [2 * DFF, 6 * D]
    outs = pl.pallas_call(
        body, name=name, in_specs=[vm], out_specs=[vm] * (1 + len(widths)),
        out_shape=[jax.ShapeDtypeStruct(a.shape[1:], F32)] + [jax.ShapeDtypeStruct((1, n), F32) for n in widths])(a)
    return outs[0], dict(zip(list(rows) + ["ffn_conv_b", "b_ada"], outs[1:]))


def ada_bwd(s_all, dml, dmc, w_shard, c_ctx, *, name):
    nsh = w_shard.shape[1]

    def body(s_ref, dml_ref, dmc_ref, w_ref, c_ref, dw_ref, gc_ref, s16, dm16, part, buf, sems):
        x, y, c, me = _me()
        s16[...] = jnp.zeros_like(s16)
        dm16[...] = jnp.zeros_like(dm16)
        for k in range(NDEV):
            s16[k:k + 1, :] = s_ref[k, 0:1, :]
        s16[8:9, :] = s_ref[0, 1:2, :]
        dm16[0:8, :] = dml_ref[...]
        dm16[8:9, :] = dmc_ref[...]
        dw_ref[...] = lax.dot_general(s16[...].astype(BF), dm16[...].astype(BF), (((0,), (0,)), ((), ())),
                                      preferred_element_type=F32)
        part[...] = lax.dot_general(dm16[8:16, :].astype(BF), w_ref[...].astype(BF), (((1,), (1,)), ((), ())),
                                    preferred_element_type=F32)
        buf[me] = part[...]
        _exchange_tiles(lambda lin: part, buf, sems.at[0], sems.at[1])
        acc = buf[0]
        for k in range(1, NDEV):
            acc = acc + buf[k]
        z = c_ref[...]
        sg = jax.nn.sigmoid(z)
        gc_ref[...] = acc * (sg * (1.0 + z * (1.0 - sg)))

    vm = pl.BlockSpec(memory_space=pltpu.VMEM)
    return pl.pallas_call(
        body, name=name, in_specs=[vm] * 5, out_specs=[vm, vm],
        out_shape=[jax.ShapeDtypeStruct((D, nsh), F32), jax.ShapeDtypeStruct((8, D), F32)],
        scratch_shapes=[pltpu.VMEM((16, D), F32), pltpu.VMEM((16, nsh), F32), pltpu.VMEM((8, D), F32),
                        pltpu.VMEM((NDEV, 8, D), F32), pltpu.SemaphoreType.DMA((2,))],
    )(s_all, dml, dmc, w_shard, c_ctx)


HBM_SPEC = pl.BlockSpec(memory_space=pltpu.HBM)
SEM_SPEC = pl.BlockSpec(memory_space=pltpu.SEMAPHORE)
EFFECT = pltpu.SideEffectType.DATAFLOW_SIDE_EFFECTING


ALL_PEERS = tuple(range(1, NDEV))
FIRST_HOP = (1, 2, 4, 6)
RELAY = (2, 4, 6)


def _exchange_copies(srcs, lands, send, recv, per_peer, peers):
    x, y, c, me = _me()
    n = len(peers)
    cps = []
    for t in range(len(srcs)):
        for j, k in enumerate(peers):
            dev, lin = _peer(x, y, c, k)
            cps.append(pltpu.make_async_remote_copy(
                src_ref=srcs[t].at[lin] if per_peer else srcs[t], dst_ref=lands[t].at[me],
                send_sem=send.at[n * t + j], recv_sem=recv.at[n * t + j], device_id=dev, device_id_type=MESH))
    return cps


def _relay_copies(lands, send, recv):
    x, y, c, me = _me()
    n = len(RELAY)
    cps = []
    for t in range(len(lands)):
        for j, k in enumerate(RELAY):
            slot = lands[t].at[_peer(x, y, c, k)[1]]
            cps.append(pltpu.make_async_remote_copy(
                src_ref=slot, dst_ref=slot, send_sem=send.at[n * t + j], recv_sem=recv.at[n * t + j],
                device_id=(x, y, 1 - c), device_id_type=MESH))
    return cps


def _own_copies(srcs, lands, own, per_peer):
    me = _me()[3]
    return [pltpu.make_async_copy(srcs[t].at[me] if per_peer else srcs[t], lands[t].at[me], own.at[t])
            for t in range(len(srcs))]


def exchange_start(srcs, *, per_peer, name, dep=None, peers=ALL_PEERS):
    nt = len(srcs)
    ns = len(peers) * nt
    land_shapes = [(a.shape if per_peer else (NDEV,) + a.shape) for a in srcs]
    deps = [] if dep is None else [dep]

    def body(*refs):
        src, land = refs[:nt], refs[nt:2 * nt]
        send, recv, own = refs[2 * nt + len(deps):2 * nt + len(deps) + 3]
        for cp in _exchange_copies(src, land, send, recv, per_peer, peers) + _own_copies(src, land, own, per_peer):
            cp.start()
        refs[-1][...] = jnp.zeros_like(refs[-1])

    hb = lambda a: pltpu.with_memory_space_constraint(a, pltpu.HBM)
    outs = pl.pallas_call(
        body, name=name,
        out_shape=(pltpu.SemaphoreType.DMA((ns,)), pltpu.SemaphoreType.DMA((ns,)), pltpu.SemaphoreType.DMA((nt,)),
                   *[pltpu.HBM(a.shape, a.dtype) for a in srcs], *[pltpu.HBM(s, a.dtype) for s, a in zip(land_shapes, srcs)],
                   jax.ShapeDtypeStruct((8, 128), F32)),
        in_specs=[HBM_SPEC] * (2 * nt) + [pl.BlockSpec(memory_space=pl.ANY)] * len(deps),
        out_specs=(SEM_SPEC, SEM_SPEC, SEM_SPEC, *([HBM_SPEC] * (2 * nt)), pl.BlockSpec(memory_space=pltpu.VMEM)),
        input_output_aliases={i: 3 + i for i in range(2 * nt)},
        compiler_params=pltpu.CompilerParams(has_side_effects=EFFECT),
    )(*[hb(a) for a in srcs], *[hb(lax.empty(s, a.dtype)) for s, a in zip(land_shapes, srcs)], *deps)
    return dict(send=outs[0], recv=outs[1], own=outs[2], src=list(outs[3:3 + nt]), land=list(outs[3 + nt:3 + 2 * nt]),
                token=outs[-1], per_peer=per_peer, peers=peers)


def exchange_wait(h, after, *, name):
    nt = len(h["src"])
    per_peer, peers = h["per_peer"], h["peers"]
    after = list(after) if isinstance(after, (list, tuple)) else [after]

    def body(*refs):
        src, land, send, recv, own = refs[:nt], refs[nt:2 * nt], refs[2 * nt], refs[2 * nt + 1], refs[2 * nt + 2]
        for cp in _exchange_copies(src, land, send, recv, per_peer, peers):
            cp.wait_send()
            cp.wait_recv()
        for cp in _own_copies(src, land, own, per_peer):
            cp.wait()

    outs = pl.pallas_call(
        body, name=name,
        out_shape=(*[pltpu.HBM(a.shape, a.dtype) for a in h["src"]], *[pltpu.HBM(a.shape, a.dtype) for a in h["land"]]),
        in_specs=[HBM_SPEC] * (2 * nt) + [SEM_SPEC, SEM_SPEC, SEM_SPEC] + [pl.BlockSpec(memory_space=pl.ANY)] * len(after),
        out_specs=tuple([HBM_SPEC] * (2 * nt)),
        input_output_aliases={i: i for i in range(2 * nt)},
        compiler_params=pltpu.CompilerParams(has_side_effects=EFFECT),
    )(*h["src"], *h["land"], h["send"], h["recv"], h["own"], *after)
    return list(outs[nt:])


def relay_start(lands, *, name):
    nt = len(lands)
    ns = len(RELAY) * nt

    def body(*refs):
        for cp in _relay_copies(refs[:nt], refs[nt], refs[nt + 1]):
            cp.start()

    outs = pl.pallas_call(
        body, name=name,
        out_shape=(pltpu.SemaphoreType.DMA((ns,)), pltpu.SemaphoreType.DMA((ns,)),
                   *[pltpu.HBM(a.shape, a.dtype) for a in lands]),
        in_specs=[HBM_SPEC] * nt, out_specs=(SEM_SPEC, SEM_SPEC, *([HBM_SPEC] * nt)),
        input_output_aliases={i: 2 + i for i in range(nt)},
        compiler_params=pltpu.CompilerParams(has_side_effects=EFFECT),
    )(*lands)
    return dict(send=outs[0], recv=outs[1], land=list(outs[2:]))


def relay_wait(h, *, name):
    nt = len(h["land"])

    def body(*refs):
        for cp in _relay_copies(refs[:nt], refs[nt], refs[nt + 1]):
            cp.wait_send()
            cp.wait_recv()

    outs = pl.pallas_call(
        body, name=name, out_shape=tuple(pltpu.HBM(a.shape, a.dtype) for a in h["land"]),
        in_specs=[HBM_SPEC] * nt + [SEM_SPEC, SEM_SPEC], out_specs=tuple([HBM_SPEC] * nt),
        input_output_aliases={i: i for i in range(nt)},
        compiler_params=pltpu.CompilerParams(has_side_effects=EFFECT),
    )(*h["land"], h["send"], h["recv"])
    return list(outs)


def _adamw_math(w, g, m, v):
    nm = B1 * m + (1.0 - B1) * g
    nv = B2 * v + (1.0 - B2) * (g * g)
    m_hat = nm / (1.0 - B1 ** STEP)
    v_hat = nv / (1.0 - B2 ** STEP)
    return -LR * (m_hat / (jnp.sqrt(v_hat) + AEPS) + WD * w), nm, nv


def adamw_many(ws, gs, ms, vs, *, name):
    n = len(ws)

    def body(*refs):
        for k in range(n):
            d, nm, nv = _adamw_math(refs[k][...], refs[n + k][...], refs[2 * n + k][...], refs[3 * n + k][...])
            refs[4 * n + k][...] = d
            refs[5 * n + k][...] = nm
            refs[6 * n + k][...] = nv

    vm = pl.BlockSpec(memory_space=pltpu.VMEM)
    sh = [jax.ShapeDtypeStruct(w.shape, F32) for w in ws]
    outs = pl.pallas_call(body, name=name, in_specs=[vm] * (4 * n), out_specs=[vm] * (3 * n), out_shape=sh * 3,
                          )(*ws, *gs, *ms, *vs)
    return outs[:n], outs[n:2 * n], outs[2 * n:]


def adamw(w, g, m, v, *, name, tr=256):
    R, C = w.shape
    tr = _pick(R, tr, 8)

    def body(w_ref, g_ref, m_ref, v_ref, d_ref, nm_ref, nv_ref):
        d_ref[...], nm_ref[...], nv_ref[...] = _adamw_math(w_ref[...], g_ref[...], m_ref[...], v_ref[...])

    blk = pl.BlockSpec((tr, C), lambda i: (i, 0))
    sh = jax.ShapeDtypeStruct((R, C), F32)
    return pl.pallas_call(
        body, name=name, grid=(R // tr,), in_specs=[blk, blk, blk, blk], out_specs=[blk, blk, blk],
        out_shape=[sh, sh, sh], compiler_params=pltpu.CompilerParams(dimension_semantics=("parallel",)),
    )(w, g, m, v)


def adamw_slots(w, slots, m, v, *, name, tr=256):
    unit = w.ndim == 3
    R, C = w.shape[0], w.shape[-1]
    if R % 16 == 0:
        tr = _pick(R, tr, 16)
    else:
        tr = 144

    def body(w_ref, s_ref, m_ref, v_ref, g_ref, d_ref, nm_ref, nv_ref):
        g = s_ref[0].astype(F32)
        for k in range(1, NDEV):
            g = g + s_ref[k].astype(F32)
        g_ref[...] = g
        d_ref[...], nm_ref[...], nv_ref[...] = _adamw_math(w_ref[...], g, m_ref[...], v_ref[...])

    blk = pl.BlockSpec((tr, None, C), lambda i: (i, 0, 0)) if unit else pl.BlockSpec((tr, C), lambda i: (i, 0))
    sh = jax.ShapeDtypeStruct(w.shape, F32)
    return pl.pallas_call(
        body, name=name, grid=(pl.cdiv(R, tr),), in_specs=[blk, pl.BlockSpec((NDEV, tr, C), lambda i: (0, i, 0)), blk, blk],
        out_specs=[blk, blk, blk, blk], out_shape=[sh, sh, sh, sh],
        compiler_params=pltpu.CompilerParams(dimension_semantics=("parallel",)),
    )(w, slots, m, v)


def kernel(x, c, ctx, c_ctx, w_ada, b_ada, norm1_g, w_in, q_norm_g, kv_norm_g, w_uq, w_ukv, conv_w, conv_b, w_attn_out, w_conv_out, w_o, norm2_g, w_up, ffn_conv_w, ffn_conv_b, w_down, final_g, loss_target, m_c_ctx, m_w_ada, m_b_ada, m_norm1_g, m_w_in, m_q_norm_g, m_kv_norm_g, m_w_uq, m_w_ukv, m_conv_w, m_conv_b, m_w_attn_out, m_w_conv_out, m_w_o, m_norm2_g, m_w_up, m_ffn_conv_w, m_ffn_conv_b, m_w_down, m_final_g, v_c_ctx, v_w_ada, v_b_ada, v_norm1_g, v_w_in, v_q_norm_g, v_kv_norm_g, v_w_uq, v_w_ukv, v_conv_w, v_conv_b, v_w_attn_out, v_w_conv_out, v_w_o, v_norm2_g, v_w_up, v_ffn_conv_w, v_ffn_conv_b, v_w_down, v_final_g):
    me = 4 * lax.axis_index("x") + 2 * lax.axis_index("y") + lax.axis_index("c")
    W = dict(c_ctx=c_ctx, w_ada=w_ada, b_ada=b_ada, norm1_g=norm1_g, w_in=w_in, q_norm_g=q_norm_g, kv_norm_g=kv_norm_g,
             w_uq=w_uq, w_ukv=w_ukv, conv_w=conv_w, conv_b=conv_b, w_attn_out=w_attn_out, w_conv_out=w_conv_out, w_o=w_o,
             norm2_g=norm2_g, w_up=w_up, ffn_conv_w=ffn_conv_w, ffn_conv_b=ffn_conv_b, w_down=w_down, final_g=final_g)
    M = dict(c_ctx=m_c_ctx, w_ada=m_w_ada, b_ada=m_b_ada, norm1_g=m_norm1_g, w_in=m_w_in, q_norm_g=m_q_norm_g,
             kv_norm_g=m_kv_norm_g, w_uq=m_w_uq, w_ukv=m_w_ukv, conv_w=m_conv_w, conv_b=m_conv_b, w_attn_out=m_w_attn_out,
             w_conv_out=m_w_conv_out, w_o=m_w_o, norm2_g=m_norm2_g, w_up=m_w_up, ffn_conv_w=m_ffn_conv_w,
             ffn_conv_b=m_ffn_conv_b, w_down=m_w_down, final_g=m_final_g)
    V = dict(c_ctx=v_c_ctx, w_ada=v_w_ada, b_ada=v_b_ada, norm1_g=v_norm1_g, w_in=v_w_in, q_norm_g=v_q_norm_g,
             kv_norm_g=v_kv_norm_g, w_uq=v_w_uq, w_ukv=v_w_ukv, conv_w=v_conv_w, conv_b=v_conv_b, w_attn_out=v_w_attn_out,
             w_conv_out=v_w_conv_out, w_o=v_w_o, norm2_g=v_norm2_g, w_up=v_w_up, ffn_conv_w=v_ffn_conv_w,
             ffn_conv_b=v_ffn_conv_b, w_down=v_w_down, final_g=v_final_g)
    names = list(W)
    transposed = ("w_up", "w_uq")
    as2d = lambda k, a: (a.reshape(1, -1) if a.ndim == 1 else
                         a[0].T if k in transposed else a.reshape(a.shape[-2], a.shape[-1]))
    W2 = {k: as2d(k, a) for k, a in W.items()}
    M2 = {k: as2d(k, a) for k, a in M.items()}
    V2 = {k: as2d(k, a) for k, a in V.items()}
    unit3 = lambda a: jnp.transpose(a, (2, 0, 1))
    W3, M3, V3 = unit3(W["w_in"]), unit3(M["w_in"]), unit3(V["w_in"])
    nsh = W2["w_ada"].shape[1]

    unit_mid = ("conv_w", "ffn_conv_w")
    mid3 = lambda a: jnp.transpose(a, (1, 0, 2))
    s_all, m_all = ada_fwd(c, W2["c_ctx"], mid3(W["ffn_conv_w"]), mid3(W["conv_w"]), W2["w_ada"],
                           W["b_ada"].reshape(NDEV, 1, nsh), [], name="ada_fwd")
    mod_lat = m_all[:, 0, :].reshape(1, 6 * D)
    mod_ctx = m_all[:, 1, :].reshape(1, 6 * D)
    ffn_w_full = s_all[:, 2:5, :2 * DFF // NDEV].transpose(1, 0, 2).reshape(3, 2 * DFF)
    conv_w_full = s_all[:, 5:8, :CONV // NDEV].transpose(1, 0, 2).reshape(3, CONV)

    stage_w = {"in": ["w_in"], "mid": ["w_uq", "w_ukv", "w_attn_out", "w_conv_out", "w_o"], "up": ["w_up"],
               "down": ["w_down"]}
    two_level = ("in", "mid")
    ag, tok = {}, m_all
    for st, nms in stage_w.items():
        ag[st] = exchange_start([W2[nm].astype(BF) for nm in nms], per_peer=False, dep=tok, name="ag_start_" + st,
                                peers=FIRST_HOP if st in two_level else ALL_PEERS)
        tok = ag[st]["token"]

    def get_w(stage, after):
        lands = exchange_wait(ag[stage], after, name="ag_wait_" + stage)
        if stage in two_level:
            lands = relay_wait(relay_start(lands, name="ag_relay_" + stage), name="ag_relay_wait_" + stage)
        g = dict(zip(stage_w[stage], lands))
        if stage == "in":
            return build_win(g["w_in"], name="build_win")
        if stage == "mid":
            wq2, wkv2 = build_wq_wkv(g["w_uq"], g["w_ukv"], name="build_wq_wkv")
            return (wq2, wkv2, unshard_cols(g["w_attn_out"], name="unshard_w_attn_out"),
                    unshard_cols(g["w_conv_out"], name="unshard_w_conv_out"), g["w_o"].reshape(D, D))
        if stage == "up":
            return g["w_up"].reshape(2 * DFF, D)
        return g["w_down"].reshape(DFF, D)

    stage_g = {"ffn": ["w_up", "w_down"], "mid": ["w_attn_out", "w_conv_out", "w_o"], "qkv": ["w_uq", "w_ukv"],
               "in": ["w_in"]}
    rs = {}

    def put_g(stage, g):
        if stage == "in":
            parts = [shard_win_grad(g["dwin"], g["dwin_c"], name="shard_win_grad")]
        elif stage == "mid":
            parts = [shard_cols(g["dwao"], name="shard_w_attn_out"), shard_cols(g["dwco"], name="shard_w_conv_out"),
                     g["dwo"].reshape(NDEV, D // NDEV, D)]
        elif stage == "qkv":
            parts = list(shard_wq_wkv_grad(g["dwq2"], g["dwkv2"], name="shard_wq_wkv_grad"))
        else:
            parts = [g["dwup"].reshape(NDEV, 2 * DFF // NDEV, D), g["dwdn"].reshape(NDEV, DFF // NDEV, D)]
        rs[stage] = exchange_start(parts, per_peer=True, name="rs_start_" + stage)
        return rs[stage]["token"]

    r = _local_step(x[0], ctx[0], loss_target[0], mod_lat, mod_ctx, W2["norm1_g"], W2["q_norm_g"], W2["kv_norm_g"],
                    W2["norm2_g"], W2["final_g"], conv_w_full, W2["conv_b"], ffn_w_full, W2["ffn_conv_b"], get_w, put_g,
                    ag["down"]["token"])

    G, DL, NM, NV = {}, {}, {}, {}

    def finish(stage, after):
        for nm, sl in zip(stage_g[stage], exchange_wait(rs[stage], after, name="rs_wait_" + stage)):
            wmv = (W3, M3, V3) if nm == "w_in" else (W2[nm], M2[nm], V2[nm])
            G[nm], DL[nm], NM[nm], NV[nm] = adamw_slots(wmv[0], sl, wmv[1], wmv[2], name="adamw_" + nm)
            after = DL[nm]
        return after

    sync = exchange_start([pack_small(r, name="pack_small")], per_peer=False, name="sync_start")
    after = sync["token"]
    for st in ("ffn", "mid", "in", "qkv"):
        after = finish(st, after)
    a_buf, = exchange_wait(sync, [DL[nm] for nms in stage_g.values() for nm in nms], name="sync_wait")
    ssum, g_vec = sum_slots(a_buf, name="sum_small")
    G.update(g_vec)
    loss = ssum[P_LOSS, 0]
    G["conv_w"] = lax.dynamic_slice(ssum[P_CW:P_CW + 3, :CONV], (0, me * (CONV // NDEV)), (3, CONV // NDEV))
    fw_full = ssum[P_FW:P_FW + 6 * FROWS].reshape(3, 2, FROWS * D)[:, :, :DFF].reshape(3, 2 * DFF)
    G["ffn_conv_w"] = lax.dynamic_slice(fw_full, (0, me * (2 * DFF // NDEV)), (3, 2 * DFF // NDEV))

    dml = lax.dynamic_slice(a_buf[:, P_DML:P_DML + 6, :].reshape(NDEV, 6 * D), (0, me * nsh), (NDEV, nsh))
    dmc = lax.dynamic_slice(ssum[P_DMC:P_DMC + 6].reshape(1, 6 * D), (0, me * nsh), (1, nsh))
    G["w_ada"], gcc = ada_bwd(s_all, dml, dmc, W2["w_ada"], W2["c_ctx"], name="ada_bwd")
    G["c_ctx"] = gcc[0:1]

    DL["w_ada"], NM["w_ada"], NV["w_ada"] = adamw(W2["w_ada"], G["w_ada"], M2["w_ada"], V2["w_ada"], name="adamw_w_ada")
    small = ["c_ctx", "b_ada", "norm1_g", "q_norm_g", "kv_norm_g", "conv_b", "norm2_g", "ffn_conv_b", "final_g", "conv_w",
             "ffn_conv_w"]
    view = lambda k, a3, a2: mid3(a3[k]) if k in unit_mid else a2[k]
    for k in unit_mid:
        G[k] = G[k].reshape(3, 1, -1)
    ds, nms, nvs = adamw_many([view(k, W, W2) for k in small], [G[k] for k in small],
                              [view(k, M, M2) for k in small], [view(k, V, V2) for k in small], name="adamw_small")
    for k, nm in enumerate(small):
        DL[nm], NM[nm], NV[nm] = ds[k], nms[k], nvs[k]

    def as_output(nm, a):
        if nm in transposed:
            return a.T[None]
        if nm == "w_in":
            return jnp.transpose(a, (1, 2, 0))
        if nm in unit_mid and a.ndim == 3:
            return jnp.transpose(a, (1, 0, 2))
        return a.reshape(W[nm].shape)

    outs = [loss, r["dx"][None]]
    for grp in (G, DL, NM, NV):
        outs += [as_output(nm, grp[nm]) for nm in names]
    return tuple(outs)
```

```python
import functools
import numpy as np
import jax
import jax.numpy as jnp
from jax import lax
from jax.experimental import pallas as pl
from jax.experimental.pallas import tpu as pltpu

F32 = jnp.float32
BF = jnp.bfloat16
MESH = pl.DeviceIdType.MESH

D = 1024
T = 2048
TC = 256
TKV = T + TC
GRID_W = 64
NH = 8
DN = 64
DR = 32
DV = 64
QL = 384
KVL = 256
CONV = 512
DFF = 2816
EPS = 1e-6
ROPE_THETA = 10000.0
SCALE = (DN + DR) ** -0.5
NDEV = 8
HP = 128

O_GA, O_GC, O_KV, O_Q, O_CV = 0, 1024, 2048, 2560, 3072
NIN = 4608
CVB = 256
N_IN = 4256
SH_IN = N_IN // NDEV

LR, B1, B2, AEPS, WD, STEP = 0.001, 0.9, 0.999, 1e-08, 0.01, 10


def _pick(n, target, mult=128):
    best = None
    for d in range(mult, min(n, target) + 1, mult):
        if n % d == 0:
            best = d
    return best if best is not None else n


def _swap_start(g):
    return 8 * (g ^ 1)


def mm(a, b, *, ta=False, tb=False, out_dtype=F32, name, tm=1024, tn=1024, tk=2048, M=None, N=None, K=None,
       a_off=(0, 0), b_off=(0, 0), a_stack=False, b_stack=False, o_stack=False, dep=None):
    def dims(arr, stack):
        return (arr.shape[1], 2 * arr.shape[2]) if stack else arr.shape

    ar, ac = dims(a, a_stack)
    br, bc = dims(b, b_stack)
    M = M or ((ac if ta else ar) - a_off[1 if ta else 0])
    K = K or ((ar if ta else ac) - a_off[0 if ta else 1])
    N = N or ((br if tb else bc) - b_off[0 if tb else 1])
    tm = _pick(M, tm, 128 if ta else 16)
    tn = _pick(N // 2 if (o_stack or (b_stack and not tb)) else N, tn, 128)
    tk = _pick(K // 2 if ((a_stack and not ta) or (b_stack and tb)) else K, tk, 128)
    nk = K // tk
    ca = 0 if ta else 1
    cb = 1 if tb else 0

    def body(a_ref, b_ref, *rest):
        o_ref, acc = rest[-2:]
        k = pl.program_id(2)
        part = lax.dot_general(a_ref[...].astype(BF), b_ref[...].astype(BF),
                               (((ca,), (cb,)), ((), ())), preferred_element_type=F32)
        if nk == 1:
            o_ref[...] = part.astype(o_ref.dtype)
        else:
            @pl.when(k == 0)
            def _():
                acc[...] = part

            @pl.when(k > 0)
            def _():
                acc[...] += part

            @pl.when(k == nk - 1)
            def _():
                o_ref[...] = acc[...].astype(o_ref.dtype)

    def spec(blk, rc, off, stack, ncols):
        assert off[0] % blk[0] == 0 and off[1] % blk[1] == 0, (name, blk, off)
        ro, co = off[0] // blk[0], off[1] // blk[1]
        if not stack:
            return pl.BlockSpec(blk, lambda i, j, k: (rc(i, j, k)[0] + ro, rc(i, j, k)[1] + co))
        nhb = ncols // 2 // blk[1]
        return pl.BlockSpec((None,) + blk,
                            lambda i, j, k: ((rc(i, j, k)[1] + co) // nhb, rc(i, j, k)[0] + ro, (rc(i, j, k)[1] + co) % nhb))

    a_spec = spec((tk, tm), lambda i, j, k: (k, i), a_off, a_stack, ac) if ta else \
        spec((tm, tk), lambda i, j, k: (i, k), a_off, a_stack, ac)
    b_spec = spec((tn, tk), lambda i, j, k: (j, k), b_off, b_stack, bc) if tb else \
        spec((tk, tn), lambda i, j, k: (k, j), b_off, b_stack, bc)
    o_spec = spec((tm, tn), lambda i, j, k: (i, j), (0, 0), o_stack, N)
    o_shape = (2, M, N // 2) if o_stack else (M, N)
    deps = [] if dep is None else [dep]
    return pl.pallas_call(
        body, name=name, grid=(M // tm, N // tn, nk),
        in_specs=[a_spec, b_spec] + [pl.BlockSpec(memory_space=pl.ANY)] * len(deps),
        out_specs=o_spec, out_shape=jax.ShapeDtypeStruct(o_shape, out_dtype),
        scratch_shapes=[pltpu.VMEM((tm, tn) if nk > 1 else (8, 128), F32)],
        compiler_params=pltpu.CompilerParams(dimension_semantics=("parallel", "parallel", "arbitrary")),
    )(a, b, *deps)


def _row(width):
    return pl.BlockSpec((1, width), lambda *_: (0, 0))


NLAT = T // TC


def normmod_cat(ctx, x, g, csc, csh, sc, sh, dep, *, name, tm=256):
    assert tm == TC

    def body(c_ref, x_ref, g_ref, csc_ref, csh_ref, sc_ref, sh_ref, dep_ref, h_ref):
        last = pl.program_id(0) == NLAT
        xv = jnp.where(last, c_ref[...], x_ref[...])
        scv = jnp.where(last, csc_ref[...], sc_ref[...])
        shv = jnp.where(last, csh_ref[...], sh_ref[...])
        r = lax.rsqrt(jnp.mean(xv * xv, axis=-1, keepdims=True) + EPS)
        h_ref[...] = ((xv * r * g_ref[...]) * (1.0 + scv) + shv).astype(BF)

    return pl.pallas_call(
        body, name=name, grid=(TKV // tm,),
        in_specs=[pl.BlockSpec((tm, D), lambda i: (0, 0)), pl.BlockSpec((tm, D), lambda i: (jnp.minimum(i, NLAT - 1), 0)),
                  _row(D), _row(D), _row(D), _row(D), _row(D), pl.BlockSpec(memory_space=pl.ANY)],
        out_specs=pl.BlockSpec((tm, D), lambda i: (i, 0)), out_shape=jax.ShapeDtypeStruct((TKV, D), BF),
        compiler_params=pltpu.CompilerParams(dimension_semantics=("parallel",)),
    )(ctx, x, g, csc, csh, sc, sh, dep)


def kvprep(pc, p, kvg, wkv2, ck, sk, *, name, tm=256):
    assert tm == TC
    nb = TKV // tm
    kvcol = O_KV // 512

    def body(pc_ref, p_ref, g_ref, w_ref, ck_ref, sk_ref, k_ref, v_ref, ckv_ref):
        i = pl.program_id(0)
        t = jnp.where(i == NLAT, pc_ref[...], p_ref[...])
        pk = t[:, :KVL]
        r = lax.rsqrt(jnp.mean(pk * pk, axis=-1, keepdims=True) + EPS)
        ckv = (pk * r * g_ref[...]).astype(BF)
        ckv_ref[...] = ckv
        kv2 = jnp.dot(ckv, w_ref[...], preferred_element_type=F32)
        krr = t[:, KVL:KVL + HP] * ck_ref[...] + t[:, KVL + HP:KVL + 2 * HP] * sk_ref[...]
        k_ref[...] = (kv2[:, :NH * HP] + jnp.concatenate([krr] * NH, axis=1)).astype(BF)
        v_ref[...] = kv2[:, NH * HP:].astype(BF)

    return pl.pallas_call(
        body, name=name, grid=(nb,),
        in_specs=[pl.BlockSpec((tm, 512), lambda i: (0, 0)),
                  pl.BlockSpec((tm, 512), lambda i: (jnp.minimum(i, NLAT - 1), kvcol)),
                  _row(KVL), pl.BlockSpec((KVL, NH * HP + NH * DV), lambda i: (0, 0)),
                  pl.BlockSpec((tm, HP), lambda i: (i, 0)), pl.BlockSpec((tm, HP), lambda i: (i, 0))],
        out_specs=[pl.BlockSpec((tm, NH * HP), lambda i: (i, 0)), pl.BlockSpec((tm, NH * DV), lambda i: (i, 0)),
                   pl.BlockSpec((tm, KVL), lambda i: (i, 0))],
        out_shape=[jax.ShapeDtypeStruct((TKV, NH * HP), BF), jax.ShapeDtypeStruct((TKV, NH * DV), BF),
                   jax.ShapeDtypeStruct((TKV, KVL), BF)],
        compiler_params=pltpu.CompilerParams(dimension_semantics=("parallel",)),
    )(pc, p, kvg, wkv2, ck, sk)


def qprep(p, qg, wq2, cq_t, sq_t, *, name, tm=256):
    qcol = O_Q // 512

    def body(p_ref, g_ref, w_ref, c_ref, s_ref, q_ref, cq_ref):
        pq = p_ref[...]
        r = lax.rsqrt(jnp.sum(pq * pq, axis=-1, keepdims=True) * (1.0 / QL) + EPS)
        cq = (pq * r * g_ref[...]).astype(BF)
        cq_ref[...] = cq
        q2 = jnp.dot(cq, w_ref[...], preferred_element_type=F32)
        cc = jnp.concatenate([c_ref[...]] * NH, axis=1)
        ss = jnp.concatenate([s_ref[...]] * NH, axis=1)
        q_ref[...] = (q2[:, :NH * HP] * cc + q2[:, NH * HP:] * ss).astype(BF)

    return pl.pallas_call(
        body, name=name, grid=(T // tm,),
        in_specs=[pl.BlockSpec((tm, 512), lambda i: (i, qcol)), _row(512),
                  pl.BlockSpec((512, 2 * NH * HP), lambda i: (0, 0)),
                  pl.BlockSpec((tm, HP), lambda i: (i, 0)), pl.BlockSpec((tm, HP), lambda i: (i, 0))],
        out_specs=[pl.BlockSpec((tm, NH * HP), lambda i: (i, 0)), pl.BlockSpec((tm, 512), lambda i: (i, 0))],
        out_shape=[jax.ShapeDtypeStruct((T, NH * HP), BF), jax.ShapeDtypeStruct((T, 512), BF)],
        compiler_params=pltpu.CompilerParams(dimension_semantics=("parallel",)),
    )(p, qg, wq2, cq_t, sq_t)


def _head_mask(h):
    lanes = lax.broadcasted_iota(jnp.int32, (1, 2 * DV), 1)
    return (lanes // DV) == (h % 2)


LOG2E = 1.4426950408889634


def attn_fwd(q, k, v, *, name, tq=1024, kc=768):
    def body(q_ref, k_ref, v_ref, o_ref, lse_ref):
        h = pl.program_id(1)
        qv = q_ref[...]
        m = l = acc = None
        for c in range(TKV // kc):
            s = lax.dot_general(qv, k_ref[c * kc:(c + 1) * kc, :], (((1,), (1,)), ((), ())),
                                preferred_element_type=F32) * (SCALE * LOG2E)
            mc = jnp.max(s, axis=-1, keepdims=True)
            if c == 0:
                m = mc
                e = jnp.exp2(s - m)
                l = jnp.sum(e, axis=-1, keepdims=True)
                acc = jnp.dot(e.astype(BF), v_ref[c * kc:(c + 1) * kc, :], preferred_element_type=F32)
            else:
                mn = jnp.maximum(m, mc)
                a = jnp.exp2(m - mn)
                e = jnp.exp2(s - mn)
                l = l * a + jnp.sum(e, axis=-1, keepdims=True)
                acc = acc * a + jnp.dot(e.astype(BF), v_ref[c * kc:(c + 1) * kc, :], preferred_element_type=F32)
                m = mn
        o2 = jnp.where(_head_mask(h), acc * (1.0 / l), 0.0).astype(BF)
        lse_ref[...] = jnp.broadcast_to(m + jnp.log(l) * LOG2E, (tq, HP))

        @pl.when(h % 2 == 0)
        def _():
            o_ref[...] = o2

        @pl.when(h % 2 == 1)
        def _():
            o_ref[...] = o_ref[...] + o2

    return pl.pallas_call(
        body, name=name, grid=(T // tq, NH),
        in_specs=[pl.BlockSpec((tq, HP), lambda i, h: (i, h)), pl.BlockSpec((TKV, HP), lambda i, h: (0, h)),
                  pl.BlockSpec((TKV, 2 * DV), lambda i, h: (0, h // 2))],
        out_specs=[pl.BlockSpec((tq, 2 * DV), lambda i, h: (i, h // 2)), pl.BlockSpec((tq, HP), lambda i, h: (i, h))],
        out_shape=[jax.ShapeDtypeStruct((T, NH * DV), BF), jax.ShapeDtypeStruct((T, NH * HP), F32)],
        compiler_params=pltpu.CompilerParams(dimension_semantics=("parallel", "arbitrary")),
    )(q, k, v)


def _shift_dn(x):
    n = x.shape[0]
    rows = lax.broadcasted_iota(jnp.int32, (n, 1), 0)
    return jnp.where(rows == 0, 0.0, pltpu.roll(x, 1, axis=0))


def _shift_up(x):
    n = x.shape[0]
    rows = lax.broadcasted_iota(jnp.int32, (n, 1), 0)
    return jnp.where(rows == n - 1, 0.0, pltpu.roll(x, n - 1, axis=0))


def _conv(x, w_ref, b_ref):
    return b_ref[...] + _shift_dn(x) * w_ref[0:1, :] + x * w_ref[1:2, :] + _shift_up(x) * w_ref[2:3, :]


def _conv_t(dy, w_ref):
    return _shift_up(dy) * w_ref[0:1, :] + dy * w_ref[1:2, :] + _shift_dn(dy) * w_ref[2:3, :]


def _conv_wgrad(dw_ref, dy, x):
    dw_ref[0:1, :] = jnp.sum(dy * _shift_dn(x), axis=0, keepdims=True)
    dw_ref[1:2, :] = jnp.sum(dy * x, axis=0, keepdims=True)
    dw_ref[2:3, :] = jnp.sum(dy * _shift_up(x), axis=0, keepdims=True)


def convz(p, cw, cb, *, name):
    o0 = O_CV // (3 * CVB)

    def body(p_ref, w_ref, bias_ref, z_ref):
        xv, bv, cv = p_ref[:, 0:CVB], p_ref[:, CVB:2 * CVB], p_ref[:, 2 * CVB:3 * CVB]
        z_ref[...] = (bv * _conv(cv * xv, w_ref, bias_ref)).astype(BF)

    return pl.pallas_call(
        body, name=name, grid=(CONV // CVB,),
        in_specs=[pl.BlockSpec((T, 3 * CVB), lambda j: (0, o0 + j)), pl.BlockSpec((3, CVB), lambda j: (0, j)),
                  pl.BlockSpec((1, CVB), lambda j: (0, j))],
        out_specs=pl.BlockSpec((T, CVB), lambda j: (0, j)),
        out_shape=jax.ShapeDtypeStruct((T, CONV), BF),
        compiler_params=pltpu.CompilerParams(dimension_semantics=("parallel",)),
    )(p, cw, cb)


def out_proj_merge(o, wao, z, wco, p, *, name, tm=512):
    kin = o.shape[1]

    def body(o_ref, wa_ref, z_ref, wc_ref, ga_ref, gc_ref, ya_ref, yc_ref, m_ref):
        ya = jnp.dot(o_ref[...], wa_ref[...], preferred_element_type=F32)
        yc = jnp.dot(z_ref[...], wc_ref[...], preferred_element_type=F32)
        ya_ref[...] = ya
        yc_ref[...] = yc
        m_ref[...] = (jax.nn.sigmoid(ga_ref[...]) * ya + jax.nn.sigmoid(gc_ref[...]) * yc).astype(BF)

    blk = pl.BlockSpec((tm, D), lambda i: (i, 0))
    act = pl.BlockSpec((tm, kin), lambda i: (i, 0))
    wsp = pl.BlockSpec((kin, D), lambda i: (0, 0))
    sh = jax.ShapeDtypeStruct((T, D), F32)
    return pl.pallas_call(
        body, name=name, grid=(T // tm,),
        in_specs=[act, wsp, act, wsp, pl.BlockSpec((tm, D), lambda i: (i, O_GA // D)),
                  pl.BlockSpec((tm, D), lambda i: (i, O_GC // D))],
        out_specs=[blk, blk, blk], out_shape=[sh, sh, jax.ShapeDtypeStruct((T, D), BF)],
        compiler_params=pltpu.CompilerParams(dimension_semantics=("parallel",)),
    )(o, wao, z, wco, p, p)


CONV_HALO = 8
CONV_ROWS = 256


def _row_chunks(n, chunk, carry):
    carry = chunk(0, True, False, carry)
    carry = lax.fori_loop(1, n // CONV_ROWS - 1, lambda c, a: chunk(c * CONV_ROWS, False, False, a), carry)
    return chunk(n - CONV_ROWS, False, True, carry)


def _ext_rows(ref, r0, first, last):
    n, w = ref.shape
    zero = jnp.zeros((CONV_HALO, w), ref.dtype)
    if first:
        return jnp.concatenate([zero, ref[0:CONV_ROWS + CONV_HALO, :]], axis=0)
    if last:
        return jnp.concatenate([ref[n - CONV_ROWS - CONV_HALO:n, :], zero], axis=0)
    return ref[pl.ds(pl.multiple_of(r0 - CONV_HALO, 8), CONV_ROWS + 2 * CONV_HALO), :]


def _center_rows(r0, first, last):
    return slice(r0, r0 + CONV_ROWS) if (first or last) else pl.ds(pl.multiple_of(r0, 8), CONV_ROWS)


def _roll_dn(x):
    return pltpu.roll(x, 1, axis=0)


def _roll_up(x):
    return pltpu.roll(x, x.shape[0] - 1, axis=0)


_CTR = slice(CONV_HALO, CONV_HALO + CONV_ROWS)


def ffn_act(u0, cw, cb, *, name, tc=256):
    nb = DFF // tc

    def body(u_ref, wg_ref, wv_ref, bg_ref, bv_ref, f_ref):
        wg = [wg_ref[k:k + 1, :] for k in range(3)]
        wv = [wv_ref[k:k + 1, :] for k in range(3)]
        bg, bv = bg_ref[...], bv_ref[...]

        def chunk(r0, first, last, carry):
            xg, xv = _ext_rows(u_ref.at[0], r0, first, last), _ext_rows(u_ref.at[1], r0, first, last)
            ug = bg + _roll_dn(xg) * wg[0] + xg * wg[1] + _roll_up(xg) * wg[2]
            uv = bv + _roll_dn(xv) * wv[0] + xv * wv[1] + _roll_up(xv) * wv[2]
            f_ref[_center_rows(r0, first, last), :] = (ug * jax.nn.sigmoid(ug) * uv)[_CTR].astype(BF)
            return carry

        _row_chunks(T, chunk, 0)

    return pl.pallas_call(
        body, name=name, grid=(nb,),
        in_specs=[pl.BlockSpec((2, T, tc), lambda j: (0, 0, j)),
                  pl.BlockSpec((3, tc), lambda j: (0, j)), pl.BlockSpec((3, tc), lambda j: (0, nb + j)),
                  pl.BlockSpec((1, tc), lambda j: (0, j)), pl.BlockSpec((1, tc), lambda j: (0, nb + j))],
        out_specs=pl.BlockSpec((T, tc), lambda j: (0, j)),
        out_shape=jax.ShapeDtypeStruct((T, DFF), BF),
        compiler_params=pltpu.CompilerParams(dimension_semantics=("parallel",)),
    )(u0, cw, cw, cb, cb)


def rows_call(lead, ins, in_specs, out_shape, out_specs, fn, *, name, R, tm):
    tb, a_stack, tk = lead.get("tb", False), lead.get("a_stack", False), lead["tk"]
    K = 2 * lead["a"].shape[2] if a_stack else lead["a"].shape[1]
    nk = K // tk
    deps = [] if lead.get("dep") is None else [lead["dep"]]
    n_in = len(ins)

    def body(a_ref, b_ref, *refs):
        refs = refs[len(deps):]
        in_refs, out_refs, acc = refs[:n_in], refs[n_in:-1], refs[-1]
        i, k = pl.program_id(0), pl.program_id(1)
        part = lax.dot_general(a_ref[...].astype(BF), b_ref[...].astype(BF),
                               (((1,), (1 if tb else 0,)), ((), ())), preferred_element_type=F32)
        if nk == 1:
            fn(i, part, in_refs, out_refs)
            return

        @pl.when(k == 0)
        def _():
            acc[...] = part

        @pl.when(k > 0)
        def _():
            acc[...] += part

        @pl.when(k == nk - 1)
        def _():
            fn(i, acc[...], in_refs, out_refs)

    if a_stack:
        nhb = K // 2 // tk
        a_spec = pl.BlockSpec((None, tm, tk), lambda i, k: (k // nhb, i, k % nhb))
    else:
        a_spec = pl.BlockSpec((tm, tk), lambda i, k: (i, k))
    b_spec = pl.BlockSpec((D, tk), lambda i, k: (0, k)) if tb else pl.BlockSpec((tk, D), lambda i, k: (k, 0))
    return pl.pallas_call(
        body, name=name, grid=(R // tm, nk),
        in_specs=[a_spec, b_spec] + [pl.BlockSpec(memory_space=pl.ANY)] * len(deps) + list(in_specs),
        out_specs=out_specs, out_shape=out_shape,
        scratch_shapes=[pltpu.VMEM((tm, D) if nk > 1 else (8, 128), F32)],
        compiler_params=pltpu.CompilerParams(dimension_semantics=("arbitrary", "arbitrary")),
    )(lead["a"], lead["b"], *deps, *ins)


def _rblk(tm, w=D, col=0):
    return pl.BlockSpec((tm, w), lambda i, k: (i, col))


def _rrow(w=D):
    return pl.BlockSpec((1, w), lambda i, k: (0, 0))


def down_final(f, wdn, x1, g2, fg, tgt, *, name, tm=512):
    def fn(i, d, in_refs, out_refs):
        x1_ref, g2_ref, fg_ref, t_ref = in_refs
        d_ref, dx_ref, dd_ref, dfg_ref, loss_ref = out_refs
        d_ref[...] = d
        xv = x1_ref[...] + g2_ref[...] * d
        r = lax.rsqrt(jnp.mean(xv * xv, axis=-1, keepdims=True) + EPS)
        xh = xv * r
        diff = xh * fg_ref[...] - t_ref[...]
        part = 0.5 * jnp.sum(jnp.mean(diff * diff, axis=-1, keepdims=True), axis=0, keepdims=True)
        dy = diff * (1.0 / D)
        a = dy * fg_ref[...]
        dx = r * (a - xh * jnp.mean(a * xh, axis=-1, keepdims=True))
        dx_ref[...] = dx
        dd_ref[...] = (dx * g2_ref[...]).astype(BF)
        dfg = jnp.sum(dy * xh, axis=0, keepdims=True)

        @pl.when(i == 0)
        def _():
            dfg_ref[...] = dfg
            loss_ref[...] = jnp.broadcast_to(part, (1, 128))

        @pl.when(i > 0)
        def _():
            dfg_ref[...] += dfg
            loss_ref[...] += jnp.broadcast_to(part, (1, 128))

    blk = _rblk(tm)
    return rows_call(
        dict(a=f, b=wdn, tk=DFF), [x1, g2, fg, tgt], [blk, _rrow(), _rrow(), blk],
        [jax.ShapeDtypeStruct((T, D), F32), jax.ShapeDtypeStruct((T, D), F32), jax.ShapeDtypeStruct((T, D), BF),
         jax.ShapeDtypeStruct((1, D), F32), jax.ShapeDtypeStruct((1, 128), F32)],
        [blk, blk, blk, _rrow(), _rrow(128)], fn, name=name, R=T, tm=tm)


def oproj_resid(merged, wo, x, gate, g, sc, sh, *, name, tm=512):
    def fn(i, a, in_refs, out_refs):
        x_ref, gate_ref, g_ref, sc_ref, sh_ref = in_refs
        a_ref, x1_ref, h_ref = out_refs
        a_ref[...] = a
        xv = x_ref[...] + gate_ref[...] * a
        x1_ref[...] = xv
        r = lax.rsqrt(jnp.mean(xv * xv, axis=-1, keepdims=True) + EPS)
        h_ref[...] = ((xv * r * g_ref[...]) * (1.0 + sc_ref[...]) + sh_ref[...]).astype(BF)

    blk = _rblk(tm)
    return rows_call(
        dict(a=merged, b=wo, tk=D), [x, gate, g, sc, sh], [blk, _rrow(), _rrow(), _rrow(), _rrow()],
        [jax.ShapeDtypeStruct((T, D), F32), jax.ShapeDtypeStruct((T, D), F32), jax.ShapeDtypeStruct((T, D), BF)],
        [blk, blk, blk], fn, name=name, R=T, tm=tm)


def oproj_dx_gate_bwd(da, wo, p, ya, yc, wao, wco, *, name, tm=512):
    kin = wao.shape[0]

    def fn(i, dm, in_refs, out_refs):
        ga_ref, gc_ref, ya_ref, yc_ref, wa_ref, wc_ref = in_refs
        dya_ref, dyc_ref, dp_ref, do_ref, dz_ref = out_refs
        sa, sc_ = jax.nn.sigmoid(ga_ref[...]), jax.nn.sigmoid(gc_ref[...])
        dya, dyc = (dm * sa).astype(BF), (dm * sc_).astype(BF)
        dya_ref[...] = dya
        dyc_ref[...] = dyc
        dp_ref[:, 0:D] = (dm * ya_ref[...] * (sa * (1.0 - sa))).astype(BF)
        dp_ref[:, D:2 * D] = (dm * yc_ref[...] * (sc_ * (1.0 - sc_))).astype(BF)
        nt = (((1,), (1,)), ((), ()))
        do_ref[...] = lax.dot_general(dya, wa_ref[...], nt, preferred_element_type=F32).astype(BF)
        dz_ref[...] = lax.dot_general(dyc, wc_ref[...], nt, preferred_element_type=F32)

    blk = _rblk(tm)
    sh = jax.ShapeDtypeStruct((T, D), BF)
    wsp = pl.BlockSpec((kin, D), lambda i, k: (0, 0))
    return rows_call(
        dict(a=da, b=wo, tb=True, tk=D), [p, p, ya, yc, wao, wco],
        [_rblk(tm, D, O_GA // D), _rblk(tm, D, O_GC // D), blk, blk, wsp, wsp],
        [sh, sh, jax.ShapeDtypeStruct((T, NIN), BF), jax.ShapeDtypeStruct((T, kin), BF), jax.ShapeDtypeStruct((T, kin), F32)],
        [blk, blk, _rblk(tm, 2 * D), _rblk(tm, kin), _rblk(tm, kin)], fn, name=name, R=T, tm=tm)


def normmod_bwd(x, dh, g, sc, dres, gsrc, gate, *, name, tm=512):
    R = x.shape[0]
    tm = min(tm, R)
    has_res = dres is not None
    fused = isinstance(dh, dict)
    if fused:
        tb, a_stack, tk = dh.get("tb", False), dh.get("a_stack", False), dh["tk"]
        K = 2 * dh["a"].shape[2] if a_stack else dh["a"].shape[1]
        nk = K // tk
        deps = [] if dh.get("dep") is None else [dh["dep"]]
        n_dh = 2 + len(deps)
    else:
        nk, n_dh = 1, 1

    def elementwise(i, dhv, x_ref, g_ref, sc_ref, res_refs, out_refs):
        xv = x_ref[...]
        r = lax.rsqrt(jnp.mean(xv * xv, axis=-1, keepdims=True) + EPS)
        xh = xv * r
        n = xh * g_ref[...]
        dn = dhv * (1.0 + sc_ref[...])
        a = dn * g_ref[...]
        rows = [jnp.sum(dhv, axis=0, keepdims=True), jnp.sum(dhv * n, axis=0, keepdims=True),
                jnp.sum(dn * xh, axis=0, keepdims=True)]
        if has_res:
            dres_ref, gsrc_ref, gate_ref = res_refs
            dx_ref, dxg_ref, st_ref = out_refs
            dr = dres_ref[...]
            dx = dr + r * (a - xh * jnp.mean(a * xh, axis=-1, keepdims=True))
            dx_ref[...] = dx
            dxg_ref[...] = (dx * gate_ref[...]).astype(BF)
            rows.append(jnp.sum(dr * gsrc_ref[...], axis=0, keepdims=True))
        else:
            st_ref, = out_refs
            rows.append(jnp.zeros((1, D), F32))

        @pl.when(i == 0)
        def _():
            for k, row in enumerate(rows):
                st_ref[k:k + 1, :] = row

        @pl.when(i > 0)
        def _():
            for k, row in enumerate(rows):
                st_ref[k:k + 1, :] += row

    def body(*refs):
        x_ref, dh_refs, g_ref, sc_ref = refs[0], refs[1:1 + n_dh], refs[1 + n_dh], refs[2 + n_dh]
        rest = refs[3 + n_dh:]
        res_refs, rest = (rest[:3], rest[3:]) if has_res else ((), rest)
        out_refs = rest[:3] if has_res else rest[:1]
        i = pl.program_id(0)
        if not fused:
            elementwise(i, dh_refs[0][...], x_ref, g_ref, sc_ref, res_refs, out_refs)
            return
        acc = rest[-1]
        k = pl.program_id(1)
        part = lax.dot_general(dh_refs[0][...].astype(BF), dh_refs[1][...].astype(BF),
                               (((1,), (1 if tb else 0,)), ((), ())), preferred_element_type=F32)
        if nk == 1:
            elementwise(i, part, x_ref, g_ref, sc_ref, res_refs, out_refs)
            return

        @pl.when(k == 0)
        def _():
            acc[...] = part

        @pl.when(k > 0)
        def _():
            acc[...] += part

        @pl.when(k == nk - 1)
        def _():
            elementwise(i, acc[...], x_ref, g_ref, sc_ref, res_refs, out_refs)

    rowb = lambda w: pl.BlockSpec((1, w), lambda i, *k: (0, 0))
    blk = pl.BlockSpec((tm, D), lambda i, *k: (i, 0))
    st_spec = pl.BlockSpec((4, D), lambda i, *k: (0, 0))
    st_shape = jax.ShapeDtypeStruct((4, D), F32)
    if fused:
        if a_stack:
            nhb = K // 2 // tk
            a_spec = pl.BlockSpec((None, tm, tk), lambda i, k: (k // nhb, i, k % nhb))
        else:
            a_spec = pl.BlockSpec((tm, tk), lambda i, k: (i, k))
        b_spec = pl.BlockSpec((D, tk), lambda i, k: (0, k)) if tb else pl.BlockSpec((tk, D), lambda i, k: (k, 0))
        dh_specs = [a_spec, b_spec] + [pl.BlockSpec(memory_space=pl.ANY)] * len(deps)
        dh_args = [dh["a"], dh["b"]] + deps
        grid, sem = (R // tm, nk), ("arbitrary", "arbitrary")
        scratch = [pltpu.VMEM((tm, D) if nk > 1 else (8, 128), F32)]
    else:
        dh_specs, dh_args, grid, sem, scratch = [blk], [dh], (R // tm,), ("arbitrary",), []
    cp = pltpu.CompilerParams(dimension_semantics=sem)
    if has_res:
        return pl.pallas_call(
            body, name=name, grid=grid, in_specs=[blk] + dh_specs + [rowb(D), rowb(D), blk, blk, rowb(D)],
            out_specs=[blk, blk, st_spec], scratch_shapes=scratch,
            out_shape=[jax.ShapeDtypeStruct((R, D), F32), jax.ShapeDtypeStruct((R, D), BF), st_shape],
            compiler_params=cp,
        )(x, *dh_args, g, sc, dres, gsrc, gate)
    return pl.pallas_call(
        body, name=name, grid=grid, in_specs=[blk] + dh_specs + [rowb(D), rowb(D)],
        out_specs=st_spec, out_shape=st_shape, scratch_shapes=scratch, compiler_params=cp,
    )(x, *dh_args, g, sc)


def ffn_act_bwd(u0, df, cw, cb, *, name, tc=128):
    nb = DFF // tc

    def body(u_ref, df_ref, wg_ref, wv_ref, bg_ref, bv_ref, du_ref, dw_ref, db_ref):
        wg = [wg_ref[k:k + 1, :] for k in range(3)]
        wv = [wv_ref[k:k + 1, :] for k in range(3)]
        bg, bv = bg_ref[...], bv_ref[...]

        def chunk(r0, first, last, acc):
            xg, xv = _ext_rows(u_ref.at[0], r0, first, last), _ext_rows(u_ref.at[1], r0, first, last)
            dfe = _ext_rows(df_ref, r0, first, last)
            xg_d, xg_u, xv_d, xv_u = _roll_dn(xg), _roll_up(xg), _roll_dn(xv), _roll_up(xv)
            ug = bg + xg_d * wg[0] + xg * wg[1] + xg_u * wg[2]
            uv = bv + xv_d * wv[0] + xv * wv[1] + xv_u * wv[2]
            sig = jax.nn.sigmoid(ug)
            dug = dfe * uv * (sig * (1.0 + ug * (1.0 - sig)))
            duv = dfe * (ug * sig)
            rows = _center_rows(r0, first, last)
            du_ref[0, rows, :] = (_roll_up(dug) * wg[0] + dug * wg[1] + _roll_dn(dug) * wg[2])[_CTR].astype(BF)
            du_ref[1, rows, :] = (_roll_up(duv) * wv[0] + duv * wv[1] + _roll_dn(duv) * wv[2])[_CTR].astype(BF)
            terms = [dug * xg_d, dug * xg, dug * xg_u, dug, duv * xv_d, duv * xv, duv * xv_u, duv]
            return tuple(a + jnp.sum(t[_CTR], axis=0, keepdims=True) for a, t in zip(acc, terms))

        acc = _row_chunks(T, chunk, tuple(jnp.zeros((1, tc), F32) for _ in range(8)))
        for k in range(3):
            dw_ref[0, k:k + 1, :] = acc[k]
            dw_ref[1, k:k + 1, :] = acc[4 + k]
        db_ref[0] = acc[3]
        db_ref[1] = acc[7]

    lo = lambda r: pl.BlockSpec((r, tc), lambda j: (0, j))
    hi = lambda r: pl.BlockSpec((r, tc), lambda j: (0, nb + j))
    st = lambda r: pl.BlockSpec((2, r, tc), lambda j: (0, 0, j))
    return pl.pallas_call(
        body, name=name, grid=(nb,),
        in_specs=[st(T), lo(T), lo(3), hi(3), lo(1), hi(1)],
        out_specs=[st(T), st(3), st(1)],
        out_shape=[jax.ShapeDtypeStruct((2, T, DFF), BF), jax.ShapeDtypeStruct((2, 3, DFF), F32),
                   jax.ShapeDtypeStruct((2, 1, DFF), F32)],
        compiler_params=pltpu.CompilerParams(dimension_semantics=("parallel",)),
    )(u0, df, cw, cw, cb, cb)


def convz_bwd(p, dz, cw, cb, dp, *, name):
    o0 = O_CV // (3 * CVB)

    def body(p_ref, dz_ref, w_ref, bias_ref, dp_in, dp_ref, dw_ref, dbias_ref):
        xv, bv, cv = p_ref[:, 0:CVB], p_ref[:, CVB:2 * CVB], p_ref[:, 2 * CVB:3 * CVB]
        ci = cv * xv
        dwc = _conv(ci, w_ref, bias_ref)
        dzv = dz_ref[...]
        ddw = dzv * bv
        dci = _conv_t(ddw, w_ref)
        dp_ref[:, 0:CVB] = (dci * cv).astype(BF)
        dp_ref[:, CVB:2 * CVB] = (dzv * dwc).astype(BF)
        dp_ref[:, 2 * CVB:3 * CVB] = (dci * xv).astype(BF)
        _conv_wgrad(dw_ref, ddw, ci)
        dbias_ref[...] = jnp.sum(ddw, axis=0, keepdims=True)

    own = lambda r: pl.BlockSpec((r, CVB), lambda j: (0, j))
    return pl.pallas_call(
        body, name=name, grid=(CONV // CVB,),
        in_specs=[pl.BlockSpec((T, 3 * CVB), lambda j: (0, o0 + j)), own(T), own(3), own(1),
                  pl.BlockSpec(memory_space=pl.ANY)],
        out_specs=[pl.BlockSpec((T, 3 * CVB), lambda j: (0, o0 + j)), own(3), own(1)],
        out_shape=[jax.ShapeDtypeStruct((T, NIN), BF), jax.ShapeDtypeStruct((3, CONV), F32),
                   jax.ShapeDtypeStruct((1, CONV), F32)],
        input_output_aliases={4: 0},
        compiler_params=pltpu.CompilerParams(dimension_semantics=("parallel",)),
    )(p, dz, cw, cb, dp)


def attn_bwd(q, k, v, do, o, lse, dep, *, name, tq=1024, kc=768):
    NKC, KC = TKV // kc, kc
    deps = [] if dep is None else [dep]

    def body(q_ref, k_ref, v_ref, do_ref, o_ref, lse_ref, *rest):
        dq_ref, dk_ref, dv_ref = rest[len(deps):]
        h, i = pl.program_id(0), pl.program_id(1)

        @pl.when(i == 0)
        def _():
            dk_ref[...] = jnp.zeros_like(dk_ref)

        @pl.when((i == 0) & (h % 2 == 0))
        def _():
            dv_ref[...] = jnp.zeros_like(dv_ref)

        qv = q_ref[...]
        dom = jnp.where(_head_mask(h), do_ref[...], jnp.zeros_like(do_ref[...]))
        delta = jnp.sum(dom.astype(F32) * o_ref[...].astype(F32), axis=-1, keepdims=True)
        lse = lse_ref[:, 0:1]
        dq = jnp.zeros((tq, HP), F32)
        for c in range(NKC):
            cols = slice(c * KC, (c + 1) * KC)
            s = lax.dot_general(qv, k_ref[cols, :], (((1,), (1,)), ((), ())),
                                preferred_element_type=F32) * (SCALE * LOG2E)
            pr = jnp.exp2(s - lse)
            dp = lax.dot_general(dom, v_ref[cols, :], (((1,), (1,)), ((), ())), preferred_element_type=F32)
            ds = (pr * (dp - delta) * SCALE).astype(BF)
            dq = dq + jnp.dot(ds, k_ref[cols, :], preferred_element_type=F32)
            dk_ref[cols, :] += lax.dot_general(ds, qv, (((0,), (0,)), ((), ())), preferred_element_type=F32)
            dv_ref[cols, :] += lax.dot_general(pr.astype(BF), dom, (((0,), (0,)), ((), ())), preferred_element_type=F32)
        dq_ref[...] = dq

    return pl.pallas_call(
        body, name=name, grid=(NH, T // tq),
        in_specs=[pl.BlockSpec((tq, HP), lambda h, i: (i, h)), pl.BlockSpec((TKV, HP), lambda h, i: (0, h)),
                  pl.BlockSpec((TKV, 2 * DV), lambda h, i: (0, h // 2)), pl.BlockSpec((tq, 2 * DV), lambda h, i: (i, h // 2)),
                  pl.BlockSpec((tq, 2 * DV), lambda h, i: (i, h // 2)), pl.BlockSpec((tq, HP), lambda h, i: (i, h)),
                  *([pl.BlockSpec(memory_space=pl.ANY)] * len(deps))],
        out_specs=[pl.BlockSpec((tq, HP), lambda h, i: (i, h)), pl.BlockSpec((TKV, HP), lambda h, i: (0, h)),
                   pl.BlockSpec((TKV, 2 * DV), lambda h, i: (0, h // 2))],
        out_shape=[jax.ShapeDtypeStruct((T, NH * HP), F32), jax.ShapeDtypeStruct((TKV, NH * HP), F32),
                   jax.ShapeDtypeStruct((TKV, NH * DV), F32)],
        compiler_params=pltpu.CompilerParams(dimension_semantics=("arbitrary", "arbitrary")),
    )(q, k, v, do, o, lse, *deps)


def qprep_bwd(p, dq, qg, wq2, cq_t, sq_t, dp, *, name, tm=256):
    qcol = O_Q // 512

    def body(p_ref, dq_ref, g_ref, w_ref, c_ref, s_ref, dp_in, dp_ref, dq2_ref, dg_ref):
        i = pl.program_id(0)
        dqv = dq_ref[...]
        cc = jnp.concatenate([c_ref[...]] * NH, axis=1)
        ss = jnp.concatenate([s_ref[...]] * NH, axis=1)
        dq2 = jnp.concatenate([dqv * cc, dqv * ss], axis=1).astype(BF)
        dq2_ref[...] = dq2
        dcq = lax.dot_general(dq2, w_ref[...], (((1,), (1,)), ((), ())), preferred_element_type=F32)
        pq = p_ref[...]
        r = lax.rsqrt(jnp.sum(pq * pq, axis=-1, keepdims=True) * (1.0 / QL) + EPS)
        xh = pq * r
        a = dcq * g_ref[...]
        dp_ref[...] = (r * (a - xh * (jnp.sum(a * xh, axis=-1, keepdims=True) * (1.0 / QL)))).astype(BF)
        dg = jnp.sum(dcq * xh, axis=0, keepdims=True)

        @pl.when(i == 0)
        def _():
            dg_ref[...] = dg

        @pl.when(i > 0)
        def _():
            dg_ref[...] += dg

    return pl.pallas_call(
        body, name=name, grid=(T // tm,),
        in_specs=[pl.BlockSpec((tm, 512), lambda i: (i, qcol)), pl.BlockSpec((tm, NH * HP), lambda i: (i, 0)), _row(512),
                  pl.BlockSpec((512, 2 * NH * HP), lambda i: (0, 0)),
                  pl.BlockSpec((tm, HP), lambda i: (i, 0)), pl.BlockSpec((tm, HP), lambda i: (i, 0)),
                  pl.BlockSpec(memory_space=pl.ANY)],
        out_specs=[pl.BlockSpec((tm, 512), lambda i: (i, qcol)), pl.BlockSpec((tm, 2 * NH * HP), lambda i: (i, 0)), _row(512)],
        out_shape=[jax.ShapeDtypeStruct((T, NIN), BF), jax.ShapeDtypeStruct((T, 2 * NH * HP), BF),
                   jax.ShapeDtypeStruct((1, 512), F32)],
        input_output_aliases={6: 0},
        compiler_params=pltpu.CompilerParams(dimension_semantics=("arbitrary",)),
    )(p, dq, qg, wq2, cq_t, sq_t, dp)


def kvprep_bwd(pc, p, dk, dv, kvg, wkv2, ck, sk, dp, *, name, tm=256):
    assert tm == TC
    nb = TKV // tm
    kvcol = O_KV // 512

    def body(pc_ref, p_ref, dk_ref, dv_ref, g_ref, w_ref, ck_ref, sk_ref, dp_in, dp_ref, dpc_ref, dkv2_ref, dg_ref):
        i = pl.program_id(0)
        t = jnp.where(i == NLAT, pc_ref[...], p_ref[...])
        pk = t[:, :KVL]
        r = lax.rsqrt(jnp.mean(pk * pk, axis=-1, keepdims=True) + EPS)
        xh = pk * r
        dkv = dk_ref[...]
        dkv2 = jnp.concatenate([dkv, dv_ref[...]], axis=1).astype(BF)
        dkv2_ref[...] = dkv2
        dckv = lax.dot_general(dkv2, w_ref[...], (((1,), (1,)), ((), ())), preferred_element_type=F32)
        a = dckv * g_ref[...]
        dpk = r * (a - xh * jnp.mean(a * xh, axis=-1, keepdims=True))
        dkr = dkv[:, 0:HP]
        for hh in range(1, NH):
            dkr = dkr + dkv[:, hh * HP:(hh + 1) * HP]
        res = jnp.concatenate([dpk, dkr * ck_ref[...], dkr * sk_ref[...]], axis=1).astype(BF)
        dg = jnp.sum(dckv * xh, axis=0, keepdims=True)

        @pl.when(i == 0)
        def _():
            dg_ref[...] = dg

        @pl.when(i > 0)
        def _():
            dg_ref[...] += dg

        @pl.when(i < NLAT)
        def _():
            dp_ref[...] = res

        @pl.when(i == NLAT)
        def _():
            dpc_ref[...] = res

    rb = lambda w: pl.BlockSpec((tm, w), lambda i: (i, 0))
    return pl.pallas_call(
        body, name=name, grid=(nb,),
        in_specs=[pl.BlockSpec((tm, 512), lambda i: (0, 0)),
                  pl.BlockSpec((tm, 512), lambda i: (jnp.minimum(i, NLAT - 1), kvcol)),
                  rb(NH * HP), rb(NH * DV), _row(KVL), pl.BlockSpec((KVL, NH * HP + NH * DV), lambda i: (0, 0)),
                  rb(HP), rb(HP), pl.BlockSpec(memory_space=pl.ANY)],
        out_specs=[pl.BlockSpec((tm, 512), lambda i: (jnp.minimum(i, NLAT - 1), kvcol)),
                   pl.BlockSpec((tm, 512), lambda i: (0, 0)), rb(NH * HP + NH * DV), _row(KVL)],
        out_shape=[jax.ShapeDtypeStruct((T, NIN), BF), jax.ShapeDtypeStruct((TC, 512), BF),
                   jax.ShapeDtypeStruct((TKV, NH * HP + NH * DV), BF), jax.ShapeDtypeStruct((1, KVL), F32)],
        input_output_aliases={8: 0},
        compiler_params=pltpu.CompilerParams(dimension_semantics=("arbitrary",)),
    )(pc, p, dk, dv, kvg, wkv2, ck, sk, dp)


def _pieces(src, width, n):
    out, c = [], src
    while c < src + width:
        k = c // n
        w = min(src + width, (k + 1) * n) - c
        out.append((k, c - k * n, c - src, w))
        c += w
    return out


def _win_moves():
    mv = [(2208, 1024, O_GA), (3232, 1024, O_GC), (0, KVL, O_KV), (256, DR, O_KV + KVL + DN), (288, QL, O_Q)]
    mv += [(256 + _swap_start(g), 8, O_KV + KVL + HP + DN + 8 * g) for g in range(4)]
    for j in range(CONV // CVB):
        base = O_CV + 3 * CVB * j
        mv += [(672 + CVB * j, CVB, base), (1184 + CVB * j, CVB, base + CVB), (1696 + CVB * j, CVB, base + 2 * CVB)]
    return mv


_WIN_ZERO = [(O_KV + KVL, DN), (O_KV + KVL + DN + DR, HP - DN - DR), (O_KV + KVL + HP, DN),
             (O_KV + KVL + HP + DN + DR, HP - DN - DR), (O_Q + QL, 512 - QL)]


def build_win(g, *, name, tm=256):
    def body(g_ref, o_ref):
        for src, w, dst in _win_moves():
            for k, a, off, pw in _pieces(src, w, SH_IN):
                o_ref[:, dst + off:dst + off + pw] = g_ref[k, :, a:a + pw]
        for c0, w in _WIN_ZERO:
            o_ref[:, c0:c0 + w] = jnp.zeros((tm, w), o_ref.dtype)

    return pl.pallas_call(
        body, name=name, grid=(D // tm,), in_specs=[pl.BlockSpec((NDEV, tm, SH_IN), lambda i: (0, i, 0))],
        out_specs=pl.BlockSpec((tm, NIN), lambda i: (i, 0)), out_shape=jax.ShapeDtypeStruct((D, NIN), g.dtype),
        compiler_params=pltpu.CompilerParams(dimension_semantics=("parallel",)),
    )(g)


def shard_win_grad(dwt, dwct, *, name, tc=256):
    def body(dw_ref, dwc_ref, o_ref, kvs):
        kvs[...] = dw_ref[O_KV:O_KV + 512, :] + dwc_ref[...]

        def src(row, w):
            if O_KV <= row < O_KV + 512:
                return kvs[row - O_KV:row - O_KV + w, :]
            return dw_ref[row:row + w, :]

        for s, w, dst in _win_moves():
            if w == 8 or s == 256:
                continue
            for k, a, off, pw in _pieces(s, w, SH_IN):
                o_ref[k, a:a + pw, :] = src(dst + off, pw).astype(o_ref.dtype)
        for g in range(4):
            val = src(O_KV + KVL + DN + 8 * g, 8) + src(O_KV + KVL + HP + DN + _swap_start(g), 8)
            o_ref[0, 256 + 8 * g:256 + 8 * g + 8, :] = val.astype(o_ref.dtype)

    return pl.pallas_call(
        body, name=name, grid=(D // tc,),
        in_specs=[pl.BlockSpec((NIN, tc), lambda j: (0, j)), pl.BlockSpec((512, tc), lambda j: (0, j))],
        out_specs=pl.BlockSpec((NDEV, SH_IN, tc), lambda j: (0, 0, j)),
        out_shape=jax.ShapeDtypeStruct((NDEV, SH_IN, D), BF),
        scratch_shapes=[pltpu.VMEM((512, tc), F32)],
        compiler_params=pltpu.CompilerParams(dimension_semantics=("parallel",)),
    )(dwt, dwct)


def _eye(n, m):
    return (lax.broadcasted_iota(jnp.int32, (n, m), 0) == lax.broadcasted_iota(jnp.int32, (n, m), 1)).astype(BF)


_NT = (((1,), (1,)), ((), ()))


def build_wq_wkv(gq, gkv, *, name):
    def body(gq_ref, gkv_ref, q_ref, kv_ref):
        q_ref[...] = jnp.zeros_like(q_ref)
        kv_ref[...] = jnp.zeros_like(kv_ref)
        eye = _eye(QL, QL)
        for h in range(NH):
            qh = lax.dot_general(eye, gq_ref[h], _NT, preferred_element_type=F32).astype(q_ref.dtype)
            q_ref[0:QL, h * HP:h * HP + DN + DR] = qh
            for g in range(4):
                c0 = NH * HP + h * HP + DN + 8 * g
                q_ref[0:QL, c0:c0 + 8] = qh[:, DN + _swap_start(g):DN + _swap_start(g) + 8]
            kv_ref[:, h * HP:h * HP + DN] = gkv_ref[h, :, 0:DN]
            kv_ref[:, NH * HP + h * DV:NH * HP + (h + 1) * DV] = gkv_ref[h, :, DN:DN + DV]

    vm = pl.BlockSpec(memory_space=pltpu.VMEM)
    return pl.pallas_call(
        body, name=name, in_specs=[vm, vm], out_specs=[vm, vm],
        out_shape=[jax.ShapeDtypeStruct((512, 2 * NH * HP), gq.dtype), jax.ShapeDtypeStruct((KVL, NH * HP + NH * DV), gq.dtype)],
    )(gq, gkv)


def shard_wq_wkv_grad(dwq2, dwkv2, *, name):
    def body(q_ref, kv_ref, gq_ref, gkv_ref, xs):
        xs[...] = jnp.zeros_like(xs)
        eye = _eye(DN + DR, HP)
        for h in range(NH):
            xs[:, 0:DN] = q_ref[0:QL, h * HP:h * HP + DN].astype(BF)
            for g in range(4):
                a = q_ref[0:QL, h * HP + DN + 8 * g:h * HP + DN + 8 * g + 8]
                c0 = NH * HP + h * HP + DN + _swap_start(g)
                xs[:, DN + 8 * g:DN + 8 * g + 8] = (a + q_ref[0:QL, c0:c0 + 8]).astype(BF)
            gq_ref[h] = lax.dot_general(eye, xs[...], _NT, preferred_element_type=F32).astype(BF)
            gkv_ref[h, :, 0:DN] = kv_ref[:, h * HP:h * HP + DN].astype(BF)
            gkv_ref[h, :, DN:DN + DV] = kv_ref[:, NH * HP + h * DV:NH * HP + (h + 1) * DV].astype(BF)

    vm = pl.BlockSpec(memory_space=pltpu.VMEM)
    return pl.pallas_call(
        body, name=name, in_specs=[vm, vm], out_specs=[vm, vm],
        out_shape=[jax.ShapeDtypeStruct((NDEV, DN + DR, QL), BF), jax.ShapeDtypeStruct((NDEV, KVL, DN + DV), BF)],
        scratch_shapes=[pltpu.VMEM((QL, HP), BF)],
    )(dwq2, dwkv2)


def unshard_cols(g, *, name, tm=256):
    _, K, n = g.shape
    tm = _pick(K, tm, 16)

    def body(g_ref, o_ref):
        for k in range(NDEV):
            o_ref[:, k * n:(k + 1) * n] = g_ref[k]

    return pl.pallas_call(
        body, name=name, grid=(K // tm,), in_specs=[pl.BlockSpec((NDEV, tm, n), lambda i: (0, i, 0))],
        out_specs=pl.BlockSpec((tm, NDEV * n), lambda i: (i, 0)), out_shape=jax.ShapeDtypeStruct((K, NDEV * n), g.dtype),
        compiler_params=pltpu.CompilerParams(dimension_semantics=("parallel",)),
    )(g)


def shard_cols(w, *, name, tm=256):
    K, n8 = w.shape
    n = n8 // NDEV
    tm = _pick(K, tm, 16)

    def body(w_ref, o_ref):
        for k in range(NDEV):
            o_ref[k] = w_ref[:, k * n:(k + 1) * n]

    return pl.pallas_call(
        body, name=name, grid=(K // tm,), in_specs=[pl.BlockSpec((tm, n8), lambda i: (i, 0))],
        out_specs=pl.BlockSpec((NDEV, tm, n), lambda i: (0, i, 0)), out_shape=jax.ShapeDtypeStruct((NDEV, K, n), w.dtype),
        compiler_params=pltpu.CompilerParams(dimension_semantics=("parallel",)),
    )(w)


def _rope_tables():
    t = np.arange(T)
    row = (t // GRID_W).astype(np.float32)
    col = (t % GRID_W).astype(np.float32)
    axis_dim = DR // 2
    inv = (np.float32(ROPE_THETA) ** (-np.arange(0, axis_dim, 2, dtype=np.float32) / np.float32(axis_dim))).astype(np.float32)
    ar, ac = (row[:, None] * inv).astype(np.float32), (col[:, None] * inv).astype(np.float32)
    cosv = np.concatenate([np.cos(ar), np.cos(ar), np.cos(ac), np.cos(ac)], axis=1).astype(np.float32)
    sinv = np.concatenate([-np.sin(ar), np.sin(ar), -np.sin(ac), np.sin(ac)], axis=1).astype(np.float32)
    ck = np.zeros((TKV, HP), np.float32)
    sk = np.zeros((TKV, HP), np.float32)
    ck[T:, DN:DN + DR] = 1.0
    ck[:T, DN:DN + DR] = cosv
    sk[:T, DN:DN + DR] = sinv
    cq = np.zeros((T, HP), np.float32)
    cq[:, :DN] = 1.0
    cq[:, DN:DN + DR] = cosv
    return jnp.asarray(ck), jnp.asarray(sk), jnp.asarray(cq), jnp.asarray(sk[:T])


def _local_step(x, ctx, tgt, mod_lat, mod_ctx, n1g, qg, kvg, n2g, fg, conv_w, conv_b, ffn_w, ffn_b, get_w, put_g, dep0):
    sh1, sc1, g1, sh2, sc2, g2 = [mod_lat[:, i * D:(i + 1) * D] for i in range(6)]
    csh1, csc1 = mod_ctx[:, 0:D], mod_ctx[:, D:2 * D]
    ck, sk, cq_t, sq_t = _rope_tables()
    qg_p = jnp.pad(qg, ((0, 0), (0, 512 - QL)))

    hcat = normmod_cat(ctx, x, n1g, csc1, csh1, sc1, sh1, dep0, name="normmod1")
    win = get_w("in", hcat)
    p = mm(hcat, win, M=T, tn=768, name="in_proj")
    pc = mm(hcat, win, M=TC, N=512, a_off=(T, 0), b_off=(0, O_KV), name="in_proj_ctx")
    wq2, wkv2, wao, wco, wo = get_w("mid", p)
    kh, vh, ckv = kvprep(pc, p, kvg, wkv2, ck, sk, name="kvprep")
    qr, cq = qprep(p, qg_p, wq2, cq_t, sq_t, name="qprep")
    o, lse = attn_fwd(qr, kh, vh, name="attn_fwd")
    z = convz(p, conv_w, conv_b, name="convz")
    ya, yc, merged = out_proj_merge(o, wao, z, wco, p, name="attn_conv_out_gate_merge")
    a_out, x1, h2 = oproj_resid(merged, wo, x, g1, n2g, sc2, sh2, name="o_proj_resid_normmod2")
    wup = get_w("up", h2)
    u0 = mm(h2, wup, tb=True, o_stack=True, tn=1408, name="up_proj")
    f = ffn_act(u0, ffn_w, ffn_b, name="ffn_act")
    wdn = get_w("down", f)
    dn, dx2, dd, dfg, loss = down_final(f, wdn, x1, g2, fg, tgt, name="down_proj_final_loss")

    df = mm(dd, wdn, tb=True, tn=1408, name="down_proj_dx")
    dwdn = mm(f, dd, ta=True, out_dtype=BF, tm=1408, name="down_proj_dw")
    du0, dffn_w, dffn_b = ffn_act_bwd(u0, df, ffn_w, ffn_b, name="ffn_act_bwd")
    dwup = mm(du0, h2, ta=True, a_stack=True, out_dtype=BF, tm=1408, name="up_proj_dw")
    tok = put_g("ffn", dict(dwup=dwup, dwdn=dwdn))
    dx1, da, st2 = normmod_bwd(x1, dict(a=du0, b=wup, a_stack=True, tk=DFF, dep=tok), n2g, sc2, dx2, dn, g1,
                               name="up_proj_dx_normmod2_bwd")

    dwo = mm(merged, da, ta=True, out_dtype=BF, tn=512, name="o_proj_dw")
    dya, dyc, dp, do, dz = oproj_dx_gate_bwd(da, wo, p, ya, yc, wao, wco, name="o_proj_dx_gate_merge_bwd")
    dwao = mm(o, dya, ta=True, out_dtype=BF, tn=512, name="attn_out_dw")
    dwco = mm(z, dyc, ta=True, out_dtype=BF, tn=512, name="conv_out_dw")
    tok = put_g("mid", dict(dwao=dwao, dwco=dwco, dwo=dwo))
    dp, dconv_w, dconv_b = convz_bwd(p, dz, conv_w, conv_b, dp, name="convz_bwd")
    dq, dk, dv = attn_bwd(qr, kh, vh, do, o, lse, tok, name="attn_bwd")
    dp, dq2, dqg = qprep_bwd(p, dq, qg_p, wq2, cq_t, sq_t, dp, name="qprep_bwd")
    dp, dpc, dkv2, dkvg = kvprep_bwd(pc, p, dk, dv, kvg, wkv2, ck, sk, dp, name="kvprep_bwd")

    dwin = mm(dp, hcat, ta=True, K=T, tm=768, name="in_proj_dw")
    dwin_c = mm(dpc, hcat, ta=True, K=TC, b_off=(T, 0), name="in_proj_ctx_dw")
    tok = put_g("in", dict(dwin=dwin, dwin_c=dwin_c))
    dwq2 = mm(cq, dq2, ta=True, dep=tok, name="q_up_dw")
    dwkv2 = mm(ckv, dkv2, ta=True, name="kv_up_dw")
    tok = put_g("qkv", dict(dwq2=dwq2, dwkv2=dwkv2))
    dhc = mm(dpc, win, tb=True, N=D, K=512, b_off=(0, O_KV), name="in_proj_ctx_dx")
    dx, _, st1 = normmod_bwd(x, dict(a=dp, b=win, tb=True, tk=NIN, dep=tok), n1g, sc1, dx1, a_out, g1,
                             name="in_proj_dx_normmod1_bwd")
    stc = normmod_bwd(ctx, dhc, n1g, csc1, None, None, None, name="normmod1_ctx_bwd")

    return dict(loss=loss, dx=dx, st1=st1, st2=st2, stc=stc, dqg=dqg, dkvg=dkvg, dfg=dfg,
                dconv_w=dconv_w, dconv_b=dconv_b, dffn_w=dffn_w, dffn_b=dffn_b)


def _me():
    x, y, c = lax.axis_index("x"), lax.axis_index("y"), lax.axis_index("c")
    return x, y, c, 4 * x + 2 * y + c


def _peer(x, y, c, k):
    px = 1 - x if k & 4 else x
    py = 1 - y if k & 2 else y
    pc = 1 - c if k & 1 else c
    return (px, py, pc), 4 * px + 2 * py + pc


def _exchange_tiles(src_of_peer, buf, send_sem, recv_sem):
    x, y, c, me = _me()
    for k in range(1, NDEV):
        dev, lin = _peer(x, y, c, k)
        pltpu.make_async_remote_copy(src_ref=src_of_peer(lin), dst_ref=buf.at[me], send_sem=send_sem, recv_sem=recv_sem,
                                     device_id=dev, device_id_type=MESH).start()
    seven = buf.at[pl.ds(0, NDEV - 1)]
    pltpu.make_async_remote_copy(src_ref=seven, dst_ref=seven, send_sem=send_sem, recv_sem=recv_sem,
                                 device_id=(x, y, c), device_id_type=MESH).wait()


def _silu(z):
    return z * jax.nn.sigmoid(z)


def ada_fwd(c, c_ctx, ffn_w, conv_w, w_shard, b_ada, deps, *, name):
    nsh, nf, nc = w_shard.shape[1], ffn_w.shape[2], conv_w.shape[2]
    deps = [d for d in deps if d is not None]

    def body(c_ref, cc_ref, fw_ref, cw_ref, w_ref, b_ref, *rest):
        s_ref, ml_ref, mc_ref, fwf_ref, cwf_ref, m_ref, mine, res, sems = rest[len(deps):]
        x, y, c, me = _me()
        mine[...] = jnp.zeros_like(mine)
        mine[0:1, :] = _silu(c_ref[...])
        mine[1:2, :] = _silu(cc_ref[...])
        for k in range(3):
            mine[2 + k:3 + k, 0:fw_ref.shape[2]] = fw_ref[k]
            mine[5 + k:6 + k, 0:cw_ref.shape[2]] = cw_ref[k]
        s_ref[me] = mine[...]
        _exchange_tiles(lambda lin: mine, s_ref, sems.at[0], sems.at[1])
        sall = s_ref[...].reshape(NDEV * 8, D).astype(BF)
        r = jnp.dot(sall, w_ref[...].astype(BF), preferred_element_type=F32) + b_ref[me]
        res[...] = r.reshape(NDEV, 8, nsh)
        m_ref[me] = res[me]
        _exchange_tiles(lambda lin: res.at[lin], m_ref, sems.at[2], sems.at[3])
        for j in range(NDEV):
            ml_ref[:, j * nsh:(j + 1) * nsh] = m_ref[j, 0:1, :]
            mc_ref[:, j * nsh:(j + 1) * nsh] = m_ref[j, 1:2, :]
            fwf_ref[:, j * nf:(j + 1) * nf] = s_ref[j, 2:5, 0:nf]
            cwf_ref[:, j * nc:(j + 1) * nc] = s_ref[j, 5:8, 0:nc]

    vm = pl.BlockSpec(memory_space=pltpu.VMEM)
    return pl.pallas_call(
        body, name=name, in_specs=[vm] * 6 + [pl.BlockSpec(memory_space=pl.ANY)] * len(deps), out_specs=[vm] * 5,
        out_shape=[jax.ShapeDtypeStruct((NDEV, 8, D), F32), jax.ShapeDtypeStruct((1, NDEV * nsh), F32),
                   jax.ShapeDtypeStruct((1, NDEV * nsh), F32), jax.ShapeDtypeStruct((3, NDEV * nf), F32),
                   jax.ShapeDtypeStruct((3, NDEV * nc), F32)],
        scratch_shapes=[pltpu.VMEM((NDEV, 8, nsh), F32), pltpu.VMEM((8, D), F32), pltpu.VMEM((NDEV, 8, nsh), F32),
                        pltpu.SemaphoreType.DMA((4,))],
    )(c, c_ctx, ffn_w, conv_w, w_shard, b_ada, *deps)


P_DML, P_DMC, P_N1, P_QG, P_KVG, P_CB, P_N2, P_FB, P_FG, P_CW, P_FW, P_LOSS, P_ROWS = 0, 6, 12, 13, 14, 15, 16, 17, 23, 24, 27, 45, 48
FROWS = 3


def pack_small(r, *, name):
    ins = [r["st1"], r["st2"], r["stc"], r["dqg"], r["dkvg"], r["dconv_b"], r["dffn_b"], r["dfg"], r["dconv_w"],
           r["dffn_w"], r["loss"]]

    def put_wide(p, row0, row, n):
        for j in range(-(-n // D)):
            w = min(D, n - j * D)
            p[row0 + j:row0 + j + 1, 0:w] = row[:, j * D:j * D + w]

    def body(st1, st2, stc, qg, kvg, cb, fb, fg, cw, fw, loss, p):
        p[...] = jnp.zeros_like(p)
        for j, row in enumerate((st1[0:1, :], st1[1:2, :], st1[3:4, :], st2[0:1, :], st2[1:2, :], st2[3:4, :])):
            p[P_DML + j:P_DML + j + 1, :] = row
        p[P_DMC:P_DMC + 2, :] = stc[0:2, :]
        p[P_N1:P_N1 + 1, :] = st1[2:3, :] + stc[2:3, :]
        p[P_N2:P_N2 + 1, :] = st2[2:3, :]
        put_wide(p, P_QG, qg, 512)
        put_wide(p, P_KVG, kvg, KVL)
        put_wide(p, P_CB, cb, CONV)
        put_wide(p, P_FG, fg, D)
        put_wide(p, P_LOSS, loss, 128)
        for s in range(2):
            put_wide(p, P_FB + FROWS * s, fb.at[s], DFF)
        for k in range(3):
            put_wide(p, P_CW + k, cw.at[k:k + 1], CONV)
            for s in range(2):
                put_wide(p, P_FW + FROWS * (2 * k + s), fw.at[s, k:k + 1], DFF)

    vm = pl.BlockSpec(memory_space=pltpu.VMEM)
    return pl.pallas_call(
        body, name=name, in_specs=[vm] * len(ins), out_specs=vm, out_shape=jax.ShapeDtypeStruct((P_ROWS, D), F32),
    )(*ins)


def sum_slots(a, *, name):
    rows = dict(norm1_g=(P_N1, D), q_norm_g=(P_QG, QL), kv_norm_g=(P_KVG, KVL), conv_b=(P_CB, CONV), norm2_g=(P_N2, D),
                final_g=(P_FG, D))

    def body(a_ref, sum_ref, *out):
        acc = a_ref[0]
        for k in range(1, NDEV):
            acc = acc + a_ref[k]
        sum_ref[...] = acc
        for ref, (row, n) in zip(out, rows.values()):
            ref[...] = sum_ref[row:row + 1, 0:n]
        fb, bada = out[len(rows):]
        for s in range(2):
            for j in range(FROWS):
                w = min(D, DFF - j * D)
                row = P_FB + FROWS * s + j
                fb[:, s * DFF + j * D:s * DFF + j * D + w] = sum_ref[row:row + 1, 0:w]
        for j in range(6):
            bada[:, j * D:(j + 1) * D] = sum_ref[P_DML + j:P_DML + j + 1, :] + sum_ref[P_DMC + j:P_DMC + j + 1, :]

    vm = pl.BlockSpec(memory_space=pltpu.VMEM)
    widths = [n for _, n in rows.values()] + [2 * DFF, 6 * D]
    outs = pl.pallas_call(
        body, name=name, in_specs=[vm], out_specs=[vm] * (1 + len(widths)),
        out_shape=[jax.ShapeDtypeStruct(a.shape[1:], F32)] + [jax.ShapeDtypeStruct((1, n), F32) for n in widths])(a)
    return outs[0], dict(zip(list(rows) + ["ffn_conv_b", "b_ada"], outs[1:]))


def ada_bwd(s_all, dml, dmc, w_shard, c_ctx, *, name):
    nsh = w_shard.shape[1]

    def body(s_ref, dml_ref, dmc_ref, w_ref, c_ref, dw_ref, gc_ref, s16, dm16, part, buf, sems):
        x, y, c, me = _me()
        s16[...] = jnp.zeros_like(s16)
        dm16[...] = jnp.zeros_like(dm16)
        for k in range(NDEV):
            s16[k:k + 1, :] = s_ref[k, 0:1, :]
        s16[8:9, :] = s_ref[0, 1:2, :]
        dm16[0:8, :] = dml_ref[...]
        dm16[8:9, :] = dmc_ref[...]
        dw_ref[...] = lax.dot_general(s16[...].astype(BF), dm16[...].astype(BF), (((0,), (0,)), ((), ())),
                                      preferred_element_type=F32)
        part[...] = lax.dot_general(dm16[8:16, :].astype(BF), w_ref[...].astype(BF), (((1,), (1,)), ((), ())),
                                    preferred_element_type=F32)
        buf[me] = part[...]
        _exchange_tiles(lambda lin: part, buf, sems.at[0], sems.at[1])
        acc = buf[0]
        for k in range(1, NDEV):
            acc = acc + buf[k]
        z = c_ref[...]
        sg = jax.nn.sigmoid(z)
        gc_ref[...] = acc * (sg * (1.0 + z * (1.0 - sg)))

    vm = pl.BlockSpec(memory_space=pltpu.VMEM)
    return pl.pallas_call(
        body, name=name, in_specs=[vm] * 5, out_specs=[vm, vm],
        out_shape=[jax.ShapeDtypeStruct((D, nsh), F32), jax.ShapeDtypeStruct((8, D), F32)],
        scratch_shapes=[pltpu.VMEM((16, D), F32), pltpu.VMEM((16, nsh), F32), pltpu.VMEM((8, D), F32),
                        pltpu.VMEM((NDEV, 8, D), F32), pltpu.SemaphoreType.DMA((2,))],
    )(s_all, dml, dmc, w_shard, c_ctx)


HBM_SPEC = pl.BlockSpec(memory_space=pltpu.HBM)
SEM_SPEC = pl.BlockSpec(memory_space=pltpu.SEMAPHORE)
EFFECT = pltpu.SideEffectType.DATAFLOW_SIDE_EFFECTING


ALL_PEERS = tuple(range(1, NDEV))
FIRST_HOP = (1, 2, 4, 6)
RELAY = (2, 4, 6)


def _exchange_copies(srcs, lands, send, recv, per_peer, peers):
    x, y, c, me = _me()
    n = len(peers)
    cps = []
    for t in range(len(srcs)):
        for j, k in enumerate(peers):
            dev, lin = _peer(x, y, c, k)
            cps.append(pltpu.make_async_remote_copy(
                src_ref=srcs[t].at[lin] if per_peer else srcs[t], dst_ref=lands[t].at[me],
                send_sem=send.at[n * t + j], recv_sem=recv.at[n * t + j], device_id=dev, device_id_type=MESH))
    return cps


def _relay_copies(lands, send, recv):
    x, y, c, me = _me()
    n = len(RELAY)
    cps = []
    for t in range(len(lands)):
        for j, k in enumerate(RELAY):
            slot = lands[t].at[_peer(x, y, c, k)[1]]
            cps.append(pltpu.make_async_remote_copy(
                src_ref=slot, dst_ref=slot, send_sem=send.at[n * t + j], recv_sem=recv.at[n * t + j],
                device_id=(x, y, 1 - c), device_id_type=MESH))
    return cps


def _own_copies(srcs, lands, own, per_peer):
    me = _me()[3]
    return [pltpu.make_async_copy(srcs[t].at[me] if per_peer else srcs[t], lands[t].at[me], own.at[t])
            for t in range(len(srcs))]


def exchange_start(srcs, *, per_peer, name, dep=None, peers=ALL_PEERS):
    nt = len(srcs)
    ns = len(peers) * nt
    land_shapes = [(a.shape if per_peer else (NDEV,) + a.shape) for a in srcs]
    deps = [] if dep is None else [dep]

    def body(*refs):
        src, land = refs[:nt], refs[nt:2 * nt]
        send, recv, own = refs[2 * nt + len(deps):2 * nt + len(deps) + 3]
        for cp in _exchange_copies(src, land, send, recv, per_peer, peers) + _own_copies(src, land, own, per_peer):
            cp.start()
        refs[-1][...] = jnp.zeros_like(refs[-1])

    hb = lambda a: pltpu.with_memory_space_constraint(a, pltpu.HBM)
    outs = pl.pallas_call(
        body, name=name,
        out_shape=(pltpu.SemaphoreType.DMA((ns,)), pltpu.SemaphoreType.DMA((ns,)), pltpu.SemaphoreType.DMA((nt,)),
                   *[pltpu.HBM(a.shape, a.dtype) for a in srcs], *[pltpu.HBM(s, a.dtype) for s, a in zip(land_shapes, srcs)],
                   jax.ShapeDtypeStruct((8, 128), F32)),
        in_specs=[HBM_SPEC] * (2 * nt) + [pl.BlockSpec(memory_space=pl.ANY)] * len(deps),
        out_specs=(SEM_SPEC, SEM_SPEC, SEM_SPEC, *([HBM_SPEC] * (2 * nt)), pl.BlockSpec(memory_space=pltpu.VMEM)),
        input_output_aliases={i: 3 + i for i in range(2 * nt)},
        compiler_params=pltpu.CompilerParams(has_side_effects=EFFECT),
    )(*[hb(a) for a in srcs], *[hb(lax.empty(s, a.dtype)) for s, a in zip(land_shapes, srcs)], *deps)
    return dict(send=outs[0], recv=outs[1], own=outs[2], src=list(outs[3:3 + nt]), land=list(outs[3 + nt:3 + 2 * nt]),
                token=outs[-1], per_peer=per_peer, peers=peers)


def exchange_wait(h, after, *, name):
    nt = len(h["src"])
    per_peer, peers = h["per_peer"], h["peers"]
    after = list(after) if isinstance(after, (list, tuple)) else [after]

    def body(*refs):
        src, land, send, recv, own = refs[:nt], refs[nt:2 * nt], refs[2 * nt], refs[2 * nt + 1], refs[2 * nt + 2]
        for cp in _exchange_copies(src, land, send, recv, per_peer, peers):
            cp.wait_send()
            cp.wait_recv()
        for cp in _own_copies(src, land, own, per_peer):
            cp.wait()

    outs = pl.pallas_call(
        body, name=name,
        out_shape=(*[pltpu.HBM(a.shape, a.dtype) for a in h["src"]], *[pltpu.HBM(a.shape, a.dtype) for a in h["land"]]),
        in_specs=[HBM_SPEC] * (2 * nt) + [SEM_SPEC, SEM_SPEC, SEM_SPEC] + [pl.BlockSpec(memory_space=pl.ANY)] * len(after),
        out_specs=tuple([HBM_SPEC] * (2 * nt)),
        input_output_aliases={i: i for i in range(2 * nt)},
        compiler_params=pltpu.CompilerParams(has_side_effects=EFFECT),
    )(*h["src"], *h["land"], h["send"], h["recv"], h["own"], *after)
    return list(outs[nt:])


def relay_start(lands, *, name):
    nt = len(lands)
    ns = len(RELAY) * nt

    def body(*refs):
        for cp in _relay_copies(refs[:nt], refs[nt], refs[nt + 1]):
            cp.start()

    outs = pl.pallas_call(
        body, name=name,
        out_shape=(pltpu.SemaphoreType.DMA((ns,)), pltpu.SemaphoreType.DMA((ns,)),
                   *[pltpu.HBM(a.shape, a.dtype) for a in lands]),
        in_specs=[HBM_SPEC] * nt, out_specs=(SEM_SPEC, SEM_SPEC, *([HBM_SPEC] * nt)),
        input_output_aliases={i: 2 + i for i in range(nt)},
        compiler_params=pltpu.CompilerParams(has_side_effects=EFFECT),
    )(*lands)
    return dict(send=outs[0], recv=outs[1], land=list(outs[2:]))


def relay_wait(h, *, name):
    nt = len(h["land"])

    def body(*refs):
        for cp in _relay_copies(refs[:nt], refs[nt], refs[nt + 1]):
            cp.wait_send()
            cp.wait_recv()

    outs = pl.pallas_call(
        body, name=name, out_shape=tuple(pltpu.HBM(a.shape, a.dtype) for a in h["land"]),
        in_specs=[HBM_SPEC] * nt + [SEM_SPEC, SEM_SPEC], out_specs=tuple([HBM_SPEC] * nt),
        input_output_aliases={i: i for i in range(nt)},
        compiler_params=pltpu.CompilerParams(has_side_effects=EFFECT),
    )(*h["land"], h["send"], h["recv"])
    return list(outs)


def _adamw_math(w, g, m, v):
    nm = B1 * m + (1.0 - B1) * g
    nv = B2 * v + (1.0 - B2) * (g * g)
    m_hat = nm / (1.0 - B1 ** STEP)
    v_hat = nv / (1.0 - B2 ** STEP)
    return -LR * (m_hat / (jnp.sqrt(v_hat) + AEPS) + WD * w), nm, nv


def adamw_many(ws, gs, ms, vs, *, name):
    n = len(ws)

    def body(*refs):
        for k in range(n):
            d, nm, nv = _adamw_math(refs[k][...], refs[n + k][...], refs[2 * n + k][...], refs[3 * n + k][...])
            refs[4 * n + k][...] = d
            refs[5 * n + k][...] = nm
            refs[6 * n + k][...] = nv

    vm = pl.BlockSpec(memory_space=pltpu.VMEM)
    sh = [jax.ShapeDtypeStruct(w.shape, F32) for w in ws]
    outs = pl.pallas_call(body, name=name, in_specs=[vm] * (4 * n), out_specs=[vm] * (3 * n), out_shape=sh * 3,
                          )(*ws, *gs, *ms, *vs)
    return outs[:n], outs[n:2 * n], outs[2 * n:]


def adamw(w, g, m, v, *, name, tr=256):
    R, C = w.shape
    tr = _pick(R, tr, 8)

    def body(w_ref, g_ref, m_ref, v_ref, d_ref, nm_ref, nv_ref):
        d_ref[...], nm_ref[...], nv_ref[...] = _adamw_math(w_ref[...], g_ref[...], m_ref[...], v_ref[...])

    blk = pl.BlockSpec((tr, C), lambda i: (i, 0))
    sh = jax.ShapeDtypeStruct((R, C), F32)
    return pl.pallas_call(
        body, name=name, grid=(R // tr,), in_specs=[blk, blk, blk, blk], out_specs=[blk, blk, blk],
        out_shape=[sh, sh, sh], compiler_params=pltpu.CompilerParams(dimension_semantics=("parallel",)),
    )(w, g, m, v)


def adamw_slots(w, slots, m, v, *, name, tr=256):
    unit = w.ndim == 3
    R, C = w.shape[0], w.shape[-1]
    if R % 16 == 0:
        tr = _pick(R, tr, 16)
    else:
        tr = 144

    def body(w_ref, s_ref, m_ref, v_ref, g_ref, d_ref, nm_ref, nv_ref):
        g = s_ref[0].astype(F32)
        for k in range(1, NDEV):
            g = g + s_ref[k].astype(F32)
        g_ref[...] = g
        d_ref[...], nm_ref[...], nv_ref[...] = _adamw_math(w_ref[...], g, m_ref[...], v_ref[...])

    blk = pl.BlockSpec((tr, None, C), lambda i: (i, 0, 0)) if unit else pl.BlockSpec((tr, C), lambda i: (i, 0))
    sh = jax.ShapeDtypeStruct(w.shape, F32)
    return pl.pallas_call(
        body, name=name, grid=(pl.cdiv(R, tr),), in_specs=[blk, pl.BlockSpec((NDEV, tr, C), lambda i: (0, i, 0)), blk, blk],
        out_specs=[blk, blk, blk, blk], out_shape=[sh, sh, sh, sh],
        compiler_params=pltpu.CompilerParams(dimension_semantics=("parallel",)),
    )(w, slots, m, v)


def kernel(x, c, ctx, c_ctx, w_ada, b_ada, norm1_g, w_in, q_norm_g, kv_norm_g, w_uq, w_ukv, conv_w, conv_b, w_attn_out, w_conv_out, w_o, norm2_g, w_up, ffn_conv_w, ffn_conv_b, w_down, final_g, loss_target, m_c_ctx, m_w_ada, m_b_ada, m_norm1_g, m_w_in, m_q_norm_g, m_kv_norm_g, m_w_uq, m_w_ukv, m_conv_w, m_conv_b, m_w_attn_out, m_w_conv_out, m_w_o, m_norm2_g, m_w_up, m_ffn_conv_w, m_ffn_conv_b, m_w_down, m_final_g, v_c_ctx, v_w_ada, v_b_ada, v_norm1_g, v_w_in, v_q_norm_g, v_kv_norm_g, v_w_uq, v_w_ukv, v_conv_w, v_conv_b, v_w_attn_out, v_w_conv_out, v_w_o, v_norm2_g, v_w_up, v_ffn_conv_w, v_ffn_conv_b, v_w_down, v_final_g):
    me = 4 * lax.axis_index("x") + 2 * lax.axis_index("y") + lax.axis_index("c")
    W = dict(c_ctx=c_ctx, w_ada=w_ada, b_ada=b_ada, norm1_g=norm1_g, w_in=w_in, q_norm_g=q_norm_g, kv_norm_g=kv_norm_g,
             w_uq=w_uq, w_ukv=w_ukv, conv_w=conv_w, conv_b=conv_b, w_attn_out=w_attn_out, w_conv_out=w_conv_out, w_o=w_o,
             norm2_g=norm2_g, w_up=w_up, ffn_conv_w=ffn_conv_w, ffn_conv_b=ffn_conv_b, w_down=w_down, final_g=final_g)
    M = dict(c_ctx=m_c_ctx, w_ada=m_w_ada, b_ada=m_b_ada, norm1_g=m_norm1_g, w_in=m_w_in, q_norm_g=m_q_norm_g,
             kv_norm_g=m_kv_norm_g, w_uq=m_w_uq, w_ukv=m_w_ukv, conv_w=m_conv_w, conv_b=m_conv_b, w_attn_out=m_w_attn_out,
             w_conv_out=m_w_conv_out, w_o=m_w_o, norm2_g=m_norm2_g, w_up=m_w_up, ffn_conv_w=m_ffn_conv_w,
             ffn_conv_b=m_ffn_conv_b, w_down=m_w_down, final_g=m_final_g)
    V = dict(c_ctx=v_c_ctx, w_ada=v_w_ada, b_ada=v_b_ada, norm1_g=v_norm1_g, w_in=v_w_in, q_norm_g=v_q_norm_g,
             kv_norm_g=v_kv_norm_g, w_uq=v_w_uq, w_ukv=v_w_ukv, conv_w=v_conv_w, conv_b=v_conv_b, w_attn_out=v_w_attn_out,
             w_conv_out=v_w_conv_out, w_o=v_w_o, norm2_g=v_norm2_g, w_up=v_w_up, ffn_conv_w=v_ffn_conv_w,
             ffn_conv_b=v_ffn_conv_b, w_down=v_w_down, final_g=v_final_g)
    names = list(W)
    transposed = ("w_up", "w_uq")
    as2d = lambda k, a: (a.reshape(1, -1) if a.ndim == 1 else
                         a[0].T if k in transposed else a.reshape(a.shape[-2], a.shape[-1]))
    W2 = {k: as2d(k, a) for k, a in W.items()}
    M2 = {k: as2d(k, a) for k, a in M.items()}
    V2 = {k: as2d(k, a) for k, a in V.items()}
    unit3 = lambda a: jnp.transpose(a, (2, 0, 1))
    W3, M3, V3 = unit3(W["w_in"]), unit3(M["w_in"]), unit3(V["w_in"])
    nsh = W2["w_ada"].shape[1]

    unit_mid = ("conv_w", "ffn_conv_w")
    mid3 = lambda a: jnp.transpose(a, (1, 0, 2))
    s_all, mod_lat, mod_ctx, ffn_w_full, conv_w_full = ada_fwd(
        c, W2["c_ctx"], mid3(W["ffn_conv_w"]), mid3(W["conv_w"]), W2["w_ada"], W["b_ada"].reshape(NDEV, 1, nsh), [],
        name="ada_fwd")

    stage_w = {"in": ["w_in"], "mid": ["w_uq", "w_ukv", "w_attn_out", "w_conv_out", "w_o"], "up": ["w_up"],
               "down": ["w_down"]}
    two_level = ("in", "mid")
    ag, tok = {}, mod_lat
    for st, nms in stage_w.items():
        ag[st] = exchange_start([W2[nm].astype(BF) for nm in nms], per_peer=False, dep=tok, name="ag_start_" + st,
                                peers=FIRST_HOP if st in two_level else ALL_PEERS)
        tok = ag[st]["token"]

    def get_w(stage, after):
        lands = exchange_wait(ag[stage], after, name="ag_wait_" + stage)
        if stage in two_level:
            lands = relay_wait(relay_start(lands, name="ag_relay_" + stage), name="ag_relay_wait_" + stage)
        g = dict(zip(stage_w[stage], lands))
        if stage == "in":
            return build_win(g["w_in"], name="build_win")
        if stage == "mid":
            wq2, wkv2 = build_wq_wkv(g["w_uq"], g["w_ukv"], name="build_wq_wkv")
            return (wq2, wkv2, unshard_cols(g["w_attn_out"], name="unshard_w_attn_out"),
                    unshard_cols(g["w_conv_out"], name="unshard_w_conv_out"), g["w_o"].reshape(D, D))
        if stage == "up":
            return g["w_up"].reshape(2 * DFF, D)
        return g["w_down"].reshape(DFF, D)

    stage_g = {"ffn": ["w_up", "w_down"], "mid": ["w_attn_out", "w_conv_out", "w_o"], "qkv": ["w_uq", "w_ukv"],
               "in": ["w_in"]}
    rs = {}

    def put_g(stage, g):
        if stage == "in":
            parts = [shard_win_grad(g["dwin"], g["dwin_c"], name="shard_win_grad")]
        elif stage == "mid":
            parts = [shard_cols(g["dwao"], name="shard_w_attn_out"), shard_cols(g["dwco"], name="shard_w_conv_out"),
                     g["dwo"].reshape(NDEV, D // NDEV, D)]
        elif stage == "qkv":
            parts = list(shard_wq_wkv_grad(g["dwq2"], g["dwkv2"], name="shard_wq_wkv_grad"))
        else:
            parts = [g["dwup"].reshape(NDEV, 2 * DFF // NDEV, D), g["dwdn"].reshape(NDEV, DFF // NDEV, D)]
        rs[stage] = exchange_start(parts, per_peer=True, name="rs_start_" + stage)
        return rs[stage]["token"]

    r = _local_step(x[0], ctx[0], loss_target[0], mod_lat, mod_ctx, W2["norm1_g"], W2["q_norm_g"], W2["kv_norm_g"],
                    W2["norm2_g"], W2["final_g"], conv_w_full, W2["conv_b"], ffn_w_full, W2["ffn_conv_b"], get_w, put_g,
                    ag["down"]["token"])

    G, DL, NM, NV = {}, {}, {}, {}

    def finish(stage, after):
        for nm, sl in zip(stage_g[stage], exchange_wait(rs[stage], after, name="rs_wait_" + stage)):
            wmv = (W3, M3, V3) if nm == "w_in" else (W2[nm], M2[nm], V2[nm])
            G[nm], DL[nm], NM[nm], NV[nm] = adamw_slots(wmv[0], sl, wmv[1], wmv[2], name="adamw_" + nm)
            after = DL[nm]
        return after

    sync = exchange_start([pack_small(r, name="pack_small")], per_peer=False, name="sync_start")
    after = sync["token"]
    for st in ("ffn", "mid", "in", "qkv"):
        after = finish(st, after)
    a_buf, = exchange_wait(sync, [DL[nm] for nms in stage_g.values() for nm in nms], name="sync_wait")
    ssum, g_vec = sum_slots(a_buf, name="sum_small")
    G.update(g_vec)
    loss = ssum[P_LOSS, 0]
    G["conv_w"] = lax.dynamic_slice(ssum[P_CW:P_CW + 3, :CONV], (0, me * (CONV // NDEV)), (3, CONV // NDEV))
    fw_full = ssum[P_FW:P_FW + 6 * FROWS].reshape(3, 2, FROWS * D)[:, :, :DFF].reshape(3, 2 * DFF)
    G["ffn_conv_w"] = lax.dynamic_slice(fw_full, (0, me * (2 * DFF // NDEV)), (3, 2 * DFF // NDEV))

    dml = lax.dynamic_slice(a_buf[:, P_DML:P_DML + 6, :].reshape(NDEV, 6 * D), (0, me * nsh), (NDEV, nsh))
    dmc = lax.dynamic_slice(ssum[P_DMC:P_DMC + 6].reshape(1, 6 * D), (0, me * nsh), (1, nsh))
    G["w_ada"], gcc = ada_bwd(s_all, dml, dmc, W2["w_ada"], W2["c_ctx"], name="ada_bwd")
    G["c_ctx"] = gcc[0:1]

    DL["w_ada"], NM["w_ada"], NV["w_ada"] = adamw(W2["w_ada"], G["w_ada"], M2["w_ada"], V2["w_ada"], name="adamw_w_ada")
    small = ["c_ctx", "b_ada", "norm1_g", "q_norm_g", "kv_norm_g", "conv_b", "norm2_g", "ffn_conv_b", "final_g", "conv_w",
             "ffn_conv_w"]
    view = lambda k, a3, a2: mid3(a3[k]) if k in unit_mid else a2[k]
    for k in unit_mid:
        G[k] = G[k].reshape(3, 1, -1)
    ds, nms, nvs = adamw_many([view(k, W, W2) for k in small], [G[k] for k in small],
                              [view(k, M, M2) for k in small], [view(k, V, V2) for k in small], name="adamw_small")
    for k, nm in enumerate(small):
        DL[nm], NM[nm], NV[nm] = ds[k], nms[k], nvs[k]

    def as_output(nm, a):
        if nm in transposed:
            return a.T[None]
        if nm == "w_in":
            return jnp.transpose(a, (1, 2, 0))
        if nm in unit_mid and a.ndim == 3:
            return jnp.transpose(a, (1, 0, 2))
        return a.reshape(W[nm].shape)

    outs = [loss, r["dx"][None]]
    for grp in (G, DL, NM, NV):
        outs += [as_output(nm, grp[nm]) for nm in names]
    return tuple(outs)
```

```python
import functools
import numpy as np
import jax
import jax.numpy as jnp
from jax import lax
from jax.experimental import pallas as pl
from jax.experimental.pallas import tpu as pltpu

F32 = jnp.float32
BF = jnp.bfloat16
MESH = pl.DeviceIdType.MESH

D = 1024
T = 2048
TC = 256
TKV = T + TC
GRID_W = 64
NH = 8
DN = 64
DR = 32
DV = 64
QL = 384
KVL = 256
CONV = 512
DFF = 2816
EPS = 1e-6
ROPE_THETA = 10000.0
SCALE = (DN + DR) ** -0.5
NDEV = 8
HP = 128

O_GA, O_GC, O_KV, O_Q, O_CV = 0, 1024, 2048, 2560, 3072
NIN = 4608
CVB = 256
N_IN = 4256
SH_IN = N_IN // NDEV

LR, B1, B2, AEPS, WD, STEP = 0.001, 0.9, 0.999, 1e-08, 0.01, 10


def _pick(n, target, mult=128):
    best = None
    for d in range(mult, min(n, target) + 1, mult):
        if n % d == 0:
            best = d
    return best if best is not None else n


def _swap_start(g):
    return 8 * (g ^ 1)


def mm(a, b, *, ta=False, tb=False, out_dtype=F32, name, tm=1024, tn=1024, tk=2048, M=None, N=None, K=None,
       a_off=(0, 0), b_off=(0, 0), a_stack=False, b_stack=False, o_stack=False, dep=None):
    def dims(arr, stack):
        return (arr.shape[1], 2 * arr.shape[2]) if stack else arr.shape

    ar, ac = dims(a, a_stack)
    br, bc = dims(b, b_stack)
    M = M or ((ac if ta else ar) - a_off[1 if ta else 0])
    K = K or ((ar if ta else ac) - a_off[0 if ta else 1])
    N = N or ((br if tb else bc) - b_off[0 if tb else 1])
    tm = _pick(M, tm, 128 if ta else 16)
    tn = _pick(N // 2 if (o_stack or (b_stack and not tb)) else N, tn, 128)
    tk = _pick(K // 2 if ((a_stack and not ta) or (b_stack and tb)) else K, tk, 128)
    nk = K // tk
    ca = 0 if ta else 1
    cb = 1 if tb else 0

    def body(a_ref, b_ref, *rest):
        o_ref, acc = rest[-2:]
        k = pl.program_id(2)
        part = lax.dot_general(a_ref[...].astype(BF), b_ref[...].astype(BF),
                               (((ca,), (cb,)), ((), ())), preferred_element_type=F32)
        if nk == 1:
            o_ref[...] = part.astype(o_ref.dtype)
        else:
            @pl.when(k == 0)
            def _():
                acc[...] = part

            @pl.when(k > 0)
            def _():
                acc[...] += part

            @pl.when(k == nk - 1)
            def _():
                o_ref[...] = acc[...].astype(o_ref.dtype)

    def spec(blk, rc, off, stack, ncols):
        assert off[0] % blk[0] == 0 and off[1] % blk[1] == 0, (name, blk, off)
        ro, co = off[0] // blk[0], off[1] // blk[1]
        if not stack:
            return pl.BlockSpec(blk, lambda i, j, k: (rc(i, j, k)[0] + ro, rc(i, j, k)[1] + co))
        nhb = ncols // 2 // blk[1]
        return pl.BlockSpec((None,) + blk,
                            lambda i, j, k: ((rc(i, j, k)[1] + co) // nhb, rc(i, j, k)[0] + ro, (rc(i, j, k)[1] + co) % nhb))

    a_spec = spec((tk, tm), lambda i, j, k: (k, i), a_off, a_stack, ac) if ta else \
        spec((tm, tk), lambda i, j, k: (i, k), a_off, a_stack, ac)
    b_spec = spec((tn, tk), lambda i, j, k: (j, k), b_off, b_stack, bc) if tb else \
        spec((tk, tn), lambda i, j, k: (k, j), b_off, b_stack, bc)
    o_spec = spec((tm, tn), lambda i, j, k: (i, j), (0, 0), o_stack, N)
    o_shape = (2, M, N // 2) if o_stack else (M, N)
    deps = [] if dep is None else [dep]
    return pl.pallas_call(
        body, name=name, grid=(M // tm, N // tn, nk),
        in_specs=[a_spec, b_spec] + [pl.BlockSpec(memory_space=pl.ANY)] * len(deps),
        out_specs=o_spec, out_shape=jax.ShapeDtypeStruct(o_shape, out_dtype),
        scratch_shapes=[pltpu.VMEM((tm, tn) if nk > 1 else (8, 128), F32)],
        compiler_params=pltpu.CompilerParams(dimension_semantics=("parallel", "parallel", "arbitrary")),
    )(a, b, *deps)


def _row(width):
    return pl.BlockSpec((1, width), lambda *_: (0, 0))


NLAT = T // TC


def normmod_cat(ctx, x, g, csc, csh, sc, sh, dep, *, name, tm=256):
    assert tm == TC

    def body(c_ref, x_ref, g_ref, csc_ref, csh_ref, sc_ref, sh_ref, dep_ref, h_ref):
        last = pl.program_id(0) == NLAT
        xv = jnp.where(last, c_ref[...], x_ref[...])
        scv = jnp.where(last, csc_ref[...], sc_ref[...])
        shv = jnp.where(last, csh_ref[...], sh_ref[...])
        r = lax.rsqrt(jnp.mean(xv * xv, axis=-1, keepdims=True) + EPS)
        h_ref[...] = ((xv * r * g_ref[...]) * (1.0 + scv) + shv).astype(BF)

    return pl.pallas_call(
        body, name=name, grid=(TKV // tm,),
        in_specs=[pl.BlockSpec((tm, D), lambda i: (0, 0)), pl.BlockSpec((tm, D), lambda i: (jnp.minimum(i, NLAT - 1), 0)),
                  _row(D), _row(D), _row(D), _row(D), _row(D), pl.BlockSpec(memory_space=pl.ANY)],
        out_specs=pl.BlockSpec((tm, D), lambda i: (i, 0)), out_shape=jax.ShapeDtypeStruct((TKV, D), BF),
        compiler_params=pltpu.CompilerParams(dimension_semantics=("parallel",)),
    )(ctx, x, g, csc, csh, sc, sh, dep)


def kvprep(pc, p, kvg, wkv2, ck, sk, *, name, tm=256):
    assert tm == TC
    nb = TKV // tm
    kvcol = O_KV // 512

    def body(pc_ref, p_ref, g_ref, w_ref, ck_ref, sk_ref, k_ref, v_ref, ckv_ref):
        i = pl.program_id(0)
        t = jnp.where(i == NLAT, pc_ref[...], p_ref[...])
        pk = t[:, :KVL]
        r = lax.rsqrt(jnp.mean(pk * pk, axis=-1, keepdims=True) + EPS)
        ckv = (pk * r * g_ref[...]).astype(BF)
        ckv_ref[...] = ckv
        kv2 = jnp.dot(ckv, w_ref[...], preferred_element_type=F32)
        krr = t[:, KVL:KVL + HP] * ck_ref[...] + t[:, KVL + HP:KVL + 2 * HP] * sk_ref[...]
        k_ref[...] = (kv2[:, :NH * HP] + jnp.concatenate([krr] * NH, axis=1)).astype(BF)
        v_ref[...] = kv2[:, NH * HP:].astype(BF)

    return pl.pallas_call(
        body, name=name, grid=(nb,),
        in_specs=[pl.BlockSpec((tm, 512), lambda i: (0, 0)),
                  pl.BlockSpec((tm, 512), lambda i: (jnp.minimum(i, NLAT - 1), kvcol)),
                  _row(KVL), pl.BlockSpec((KVL, NH * HP + NH * DV), lambda i: (0, 0)),
                  pl.BlockSpec((tm, HP), lambda i: (i, 0)), pl.BlockSpec((tm, HP), lambda i: (i, 0))],
        out_specs=[pl.BlockSpec((tm, NH * HP), lambda i: (i, 0)), pl.BlockSpec((tm, NH * DV), lambda i: (i, 0)),
                   pl.BlockSpec((tm, KVL), lambda i: (i, 0))],
        out_shape=[jax.ShapeDtypeStruct((TKV, NH * HP), BF), jax.ShapeDtypeStruct((TKV, NH * DV), BF),
                   jax.ShapeDtypeStruct((TKV, KVL), BF)],
        compiler_params=pltpu.CompilerParams(dimension_semantics=("parallel",)),
    )(pc, p, kvg, wkv2, ck, sk)


def qprep(p, qg, wq2, cq_t, sq_t, *, name, tm=256):
    qcol = O_Q // 512

    def body(p_ref, g_ref, w_ref, c_ref, s_ref, q_ref, cq_ref):
        pq = p_ref[...]
        r = lax.rsqrt(jnp.sum(pq * pq, axis=-1, keepdims=True) * (1.0 / QL) + EPS)
        cq = (pq * r * g_ref[...]).astype(BF)
        cq_ref[...] = cq
        q2 = jnp.dot(cq, w_ref[...], preferred_element_type=F32)
        cc = jnp.concatenate([c_ref[...]] * NH, axis=1)
        ss = jnp.concatenate([s_ref[...]] * NH, axis=1)
        q_ref[...] = (q2[:, :NH * HP] * cc + q2[:, NH * HP:] * ss).astype(BF)

    return pl.pallas_call(
        body, name=name, grid=(T // tm,),
        in_specs=[pl.BlockSpec((tm, 512), lambda i: (i, qcol)), _row(512),
                  pl.BlockSpec((512, 2 * NH * HP), lambda i: (0, 0)),
                  pl.BlockSpec((tm, HP), lambda i: (i, 0)), pl.BlockSpec((tm, HP), lambda i: (i, 0))],
        out_specs=[pl.BlockSpec((tm, NH * HP), lambda i: (i, 0)), pl.BlockSpec((tm, 512), lambda i: (i, 0))],
        out_shape=[jax.ShapeDtypeStruct((T, NH * HP), BF), jax.ShapeDtypeStruct((T, 512), BF)],
        compiler_params=pltpu.CompilerParams(dimension_semantics=("parallel",)),
    )(p, qg, wq2, cq_t, sq_t)


def _head_mask(h):
    lanes = lax.broadcasted_iota(jnp.int32, (1, 2 * DV), 1)
    return (lanes // DV) == (h % 2)


LOG2E = 1.4426950408889634


def attn_fwd(q, k, v, *, name, tq=1024, kc=768):
    def body(q_ref, k_ref, v_ref, o_ref, lse_ref):
        h = pl.program_id(1)
        qv = q_ref[...]
        m = l = acc = None
        for c in range(TKV // kc):
            s = lax.dot_general(qv, k_ref[c * kc:(c + 1) * kc, :], (((1,), (1,)), ((), ())),
                                preferred_element_type=F32) * (SCALE * LOG2E)
            mc = jnp.max(s, axis=-1, keepdims=True)
            if c == 0:
                m = mc
                e = jnp.exp2(s - m)
                l = jnp.sum(e, axis=-1, keepdims=True)
                acc = jnp.dot(e.astype(BF), v_ref[c * kc:(c + 1) * kc, :], preferred_element_type=F32)
            else:
                mn = jnp.maximum(m, mc)
                a = jnp.exp2(m - mn)
                e = jnp.exp2(s - mn)
                l = l * a + jnp.sum(e, axis=-1, keepdims=True)
                acc = acc * a + jnp.dot(e.astype(BF), v_ref[c * kc:(c + 1) * kc, :], preferred_element_type=F32)
                m = mn
        o2 = jnp.where(_head_mask(h), acc * (1.0 / l), 0.0).astype(BF)
        lse_ref[...] = jnp.broadcast_to(m + jnp.log(l) * LOG2E, (tq, HP))

        @pl.when(h % 2 == 0)
        def _():
            o_ref[...] = o2

        @pl.when(h % 2 == 1)
        def _():
            o_ref[...] = o_ref[...] + o2

    return pl.pallas_call(
        body, name=name, grid=(T // tq, NH),
        in_specs=[pl.BlockSpec((tq, HP), lambda i, h: (i, h)), pl.BlockSpec((TKV, HP), lambda i, h: (0, h)),
                  pl.BlockSpec((TKV, 2 * DV), lambda i, h: (0, h // 2))],
        out_specs=[pl.BlockSpec((tq, 2 * DV), lambda i, h: (i, h // 2)), pl.BlockSpec((tq, HP), lambda i, h: (i, h))],
        out_shape=[jax.ShapeDtypeStruct((T, NH * DV), BF), jax.ShapeDtypeStruct((T, NH * HP), F32)],
        compiler_params=pltpu.CompilerParams(dimension_semantics=("parallel", "arbitrary")),
    )(q, k, v)


def _shift_dn(x):
    n = x.shape[0]
    rows = lax.broadcasted_iota(jnp.int32, (n, 1), 0)
    return jnp.where(rows == 0, 0.0, pltpu.roll(x, 1, axis=0))


def _shift_up(x):
    n = x.shape[0]
    rows = lax.broadcasted_iota(jnp.int32, (n, 1), 0)
    return jnp.where(rows == n - 1, 0.0, pltpu.roll(x, n - 1, axis=0))


def _conv(x, w_ref, b_ref):
    return b_ref[...] + _shift_dn(x) * w_ref[0:1, :] + x * w_ref[1:2, :] + _shift_up(x) * w_ref[2:3, :]


def _conv_t(dy, w_ref):
    return _shift_up(dy) * w_ref[0:1, :] + dy * w_ref[1:2, :] + _shift_dn(dy) * w_ref[2:3, :]


def _conv_wgrad(dw_ref, dy, x):
    dw_ref[0:1, :] = jnp.sum(dy * _shift_dn(x), axis=0, keepdims=True)
    dw_ref[1:2, :] = jnp.sum(dy * x, axis=0, keepdims=True)
    dw_ref[2:3, :] = jnp.sum(dy * _shift_up(x), axis=0, keepdims=True)


def convz(p, cw, cb, *, name):
    o0 = O_CV // (3 * CVB)

    def body(p_ref, w_ref, bias_ref, z_ref):
        xv, bv, cv = p_ref[:, 0:CVB], p_ref[:, CVB:2 * CVB], p_ref[:, 2 * CVB:3 * CVB]
        z_ref[...] = (bv * _conv(cv * xv, w_ref, bias_ref)).astype(BF)

    return pl.pallas_call(
        body, name=name, grid=(CONV // CVB,),
        in_specs=[pl.BlockSpec((T, 3 * CVB), lambda j: (0, o0 + j)), pl.BlockSpec((3, CVB), lambda j: (0, j)),
                  pl.BlockSpec((1, CVB), lambda j: (0, j))],
        out_specs=pl.BlockSpec((T, CVB), lambda j: (0, j)),
        out_shape=jax.ShapeDtypeStruct((T, CONV), BF),
        compiler_params=pltpu.CompilerParams(dimension_semantics=("parallel",)),
    )(p, cw, cb)


def out_proj_merge(o, wao, z, wco, p, *, name, tm=512):
    kin = o.shape[1]

    def body(o_ref, wa_ref, z_ref, wc_ref, ga_ref, gc_ref, ya_ref, yc_ref, m_ref):
        ya = jnp.dot(o_ref[...], wa_ref[...], preferred_element_type=F32)
        yc = jnp.dot(z_ref[...], wc_ref[...], preferred_element_type=F32)
        ya_ref[...] = ya
        yc_ref[...] = yc
        m_ref[...] = (jax.nn.sigmoid(ga_ref[...]) * ya + jax.nn.sigmoid(gc_ref[...]) * yc).astype(BF)

    blk = pl.BlockSpec((tm, D), lambda i: (i, 0))
    act = pl.BlockSpec((tm, kin), lambda i: (i, 0))
    wsp = pl.BlockSpec((kin, D), lambda i: (0, 0))
    sh = jax.ShapeDtypeStruct((T, D), F32)
    return pl.pallas_call(
        body, name=name, grid=(T // tm,),
        in_specs=[act, wsp, act, wsp, pl.BlockSpec((tm, D), lambda i: (i, O_GA // D)),
                  pl.BlockSpec((tm, D), lambda i: (i, O_GC // D))],
        out_specs=[blk, blk, blk], out_shape=[sh, sh, jax.ShapeDtypeStruct((T, D), BF)],
        compiler_params=pltpu.CompilerParams(dimension_semantics=("parallel",)),
    )(o, wao, z, wco, p, p)


CONV_HALO = 8
CONV_ROWS = 256


def _row_chunks(n, chunk, carry):
    carry = chunk(0, True, False, carry)
    carry = lax.fori_loop(1, n // CONV_ROWS - 1, lambda c, a: chunk(c * CONV_ROWS, False, False, a), carry)
    return chunk(n - CONV_ROWS, False, True, carry)


def _ext_rows(ref, r0, first, last):
    n, w = ref.shape
    zero = jnp.zeros((CONV_HALO, w), ref.dtype)
    if first:
        return jnp.concatenate([zero, ref[0:CONV_ROWS + CONV_HALO, :]], axis=0)
    if last:
        return jnp.concatenate([ref[n - CONV_ROWS - CONV_HALO:n, :], zero], axis=0)
    return ref[pl.ds(pl.multiple_of(r0 - CONV_HALO, 8), CONV_ROWS + 2 * CONV_HALO), :]


def _center_rows(r0, first, last):
    return slice(r0, r0 + CONV_ROWS) if (first or last) else pl.ds(pl.multiple_of(r0, 8), CONV_ROWS)


def _roll_dn(x):
    return pltpu.roll(x, 1, axis=0)


def _roll_up(x):
    return pltpu.roll(x, x.shape[0] - 1, axis=0)


_CTR = slice(CONV_HALO, CONV_HALO + CONV_ROWS)


def ffn_act(u0, cw, cb, *, name, tc=256):
    nb = DFF // tc

    def body(u_ref, wg_ref, wv_ref, bg_ref, bv_ref, f_ref):
        wg = [wg_ref[k:k + 1, :] for k in range(3)]
        wv = [wv_ref[k:k + 1, :] for k in range(3)]
        bg, bv = bg_ref[...], bv_ref[...]

        def chunk(r0, first, last, carry):
            xg, xv = _ext_rows(u_ref.at[0], r0, first, last), _ext_rows(u_ref.at[1], r0, first, last)
            ug = bg + _roll_dn(xg) * wg[0] + xg * wg[1] + _roll_up(xg) * wg[2]
            uv = bv + _roll_dn(xv) * wv[0] + xv * wv[1] + _roll_up(xv) * wv[2]
            f_ref[_center_rows(r0, first, last), :] = (ug * jax.nn.sigmoid(ug) * uv)[_CTR].astype(BF)
            return carry

        _row_chunks(T, chunk, 0)

    return pl.pallas_call(
        body, name=name, grid=(nb,),
        in_specs=[pl.BlockSpec((2, T, tc), lambda j: (0, 0, j)),
                  pl.BlockSpec((3, tc), lambda j: (0, j)), pl.BlockSpec((3, tc), lambda j: (0, nb + j)),
                  pl.BlockSpec((1, tc), lambda j: (0, j)), pl.BlockSpec((1, tc), lambda j: (0, nb + j))],
        out_specs=pl.BlockSpec((T, tc), lambda j: (0, j)),
        out_shape=jax.ShapeDtypeStruct((T, DFF), BF),
        compiler_params=pltpu.CompilerParams(dimension_semantics=("parallel",)),
    )(u0, cw, cw, cb, cb)


def rows_call(lead, ins, in_specs, out_shape, out_specs, fn, *, name, R, tm):
    tb, a_stack, tk = lead.get("tb", False), lead.get("a_stack", False), lead["tk"]
    K = 2 * lead["a"].shape[2] if a_stack else lead["a"].shape[1]
    nk = K // tk
    deps = [] if lead.get("dep") is None else [lead["dep"]]
    n_in = len(ins)

    def body(a_ref, b_ref, *refs):
        refs = refs[len(deps):]
        in_refs, out_refs, acc = refs[:n_in], refs[n_in:-1], refs[-1]
        i, k = pl.program_id(0), pl.program_id(1)
        part = lax.dot_general(a_ref[...].astype(BF), b_ref[...].astype(BF),
                               (((1,), (1 if tb else 0,)), ((), ())), preferred_element_type=F32)
        if nk == 1:
            fn(i, part, in_refs, out_refs)
            return

        @pl.when(k == 0)
        def _():
            acc[...] = part

        @pl.when(k > 0)
        def _():
            acc[...] += part

        @pl.when(k == nk - 1)
        def _():
            fn(i, acc[...], in_refs, out_refs)

    if a_stack:
        nhb = K // 2 // tk
        a_spec = pl.BlockSpec((None, tm, tk), lambda i, k: (k // nhb, i, k % nhb))
    else:
        a_spec = pl.BlockSpec((tm, tk), lambda i, k: (i, k))
    b_spec = pl.BlockSpec((D, tk), lambda i, k: (0, k)) if tb else pl.BlockSpec((tk, D), lambda i, k: (k, 0))
    return pl.pallas_call(
        body, name=name, grid=(R // tm, nk),
        in_specs=[a_spec, b_spec] + [pl.BlockSpec(memory_space=pl.ANY)] * len(deps) + list(in_specs),
        out_specs=out_specs, out_shape=out_shape,
        scratch_shapes=[pltpu.VMEM((tm, D) if nk > 1 else (8, 128), F32)],
        compiler_params=pltpu.CompilerParams(dimension_semantics=("arbitrary", "arbitrary")),
    )(lead["a"], lead["b"], *deps, *ins)


def _rblk(tm, w=D, col=0):
    return pl.BlockSpec((tm, w), lambda i, k: (i, col))


def _rrow(w=D):
    return pl.BlockSpec((1, w), lambda i, k: (0, 0))


def down_final(f, wdn, x1, g2, fg, tgt, *, name, tm=512):
    def fn(i, d, in_refs, out_refs):
        x1_ref, g2_ref, fg_ref, t_ref = in_refs
        d_ref, dx_ref, dd_ref, dfg_ref, loss_ref = out_refs
        d_ref[...] = d
        xv = x1_ref[...] + g2_ref[...] * d
        r = lax.rsqrt(jnp.mean(xv * xv, axis=-1, keepdims=True) + EPS)
        xh = xv * r
        diff = xh * fg_ref[...] - t_ref[...]
        part = 0.5 * jnp.sum(jnp.mean(diff * diff, axis=-1, keepdims=True), axis=0, keepdims=True)
        dy = diff * (1.0 / D)
        a = dy * fg_ref[...]
        dx = r * (a - xh * jnp.mean(a * xh, axis=-1, keepdims=True))
        dx_ref[...] = dx
        dd_ref[...] = (dx * g2_ref[...]).astype(BF)
        dfg = jnp.sum(dy * xh, axis=0, keepdims=True)

        @pl.when(i == 0)
        def _():
            dfg_ref[...] = dfg
            loss_ref[...] = jnp.broadcast_to(part, (1, 128))

        @pl.when(i > 0)
        def _():
            dfg_ref[...] += dfg
            loss_ref[...] += jnp.broadcast_to(part, (1, 128))

    blk = _rblk(tm)
    return rows_call(
        dict(a=f, b=wdn, tk=DFF), [x1, g2, fg, tgt], [blk, _rrow(), _rrow(), blk],
        [jax.ShapeDtypeStruct((T, D), F32), jax.ShapeDtypeStruct((T, D), F32), jax.ShapeDtypeStruct((T, D), BF),
         jax.ShapeDtypeStruct((1, D), F32), jax.ShapeDtypeStruct((1, 128), F32)],
        [blk, blk, blk, _rrow(), _rrow(128)], fn, name=name, R=T, tm=tm)


def oproj_resid(merged, wo, x, gate, g, sc, sh, *, name, tm=512):
    def fn(i, a, in_refs, out_refs):
        x_ref, gate_ref, g_ref, sc_ref, sh_ref = in_refs
        a_ref, x1_ref, h_ref = out_refs
        a_ref[...] = a
        xv = x_ref[...] + gate_ref[...] * a
        x1_ref[...] = xv
        r = lax.rsqrt(jnp.mean(xv * xv, axis=-1, keepdims=True) + EPS)
        h_ref[...] = ((xv * r * g_ref[...]) * (1.0 + sc_ref[...]) + sh_ref[...]).astype(BF)

    blk = _rblk(tm)
    return rows_call(
        dict(a=merged, b=wo, tk=D), [x, gate, g, sc, sh], [blk, _rrow(), _rrow(), _rrow(), _rrow()],
        [jax.ShapeDtypeStruct((T, D), F32), jax.ShapeDtypeStruct((T, D), F32), jax.ShapeDtypeStruct((T, D), BF)],
        [blk, blk, blk], fn, name=name, R=T, tm=tm)


def oproj_dx_gate_bwd(da, wo, p, ya, yc, wao, wco, *, name, tm=512):
    kin = wao.shape[0]

    def fn(i, dm, in_refs, out_refs):
        ga_ref, gc_ref, ya_ref, yc_ref, wa_ref, wc_ref = in_refs
        dya_ref, dyc_ref, dp_ref, do_ref, dz_ref = out_refs
        sa, sc_ = jax.nn.sigmoid(ga_ref[...]), jax.nn.sigmoid(gc_ref[...])
        dya, dyc = (dm * sa).astype(BF), (dm * sc_).astype(BF)
        dya_ref[...] = dya
        dyc_ref[...] = dyc
        dp_ref[:, 0:D] = (dm * ya_ref[...] * (sa * (1.0 - sa))).astype(BF)
        dp_ref[:, D:2 * D] = (dm * yc_ref[...] * (sc_ * (1.0 - sc_))).astype(BF)
        nt = (((1,), (1,)), ((), ()))
        do_ref[...] = lax.dot_general(dya, wa_ref[...], nt, preferred_element_type=F32).astype(BF)
        dz_ref[...] = lax.dot_general(dyc, wc_ref[...], nt, preferred_element_type=F32)

    blk = _rblk(tm)
    sh = jax.ShapeDtypeStruct((T, D), BF)
    wsp = pl.BlockSpec((kin, D), lambda i, k: (0, 0))
    return rows_call(
        dict(a=da, b=wo, tb=True, tk=D), [p, p, ya, yc, wao, wco],
        [_rblk(tm, D, O_GA // D), _rblk(tm, D, O_GC // D), blk, blk, wsp, wsp],
        [sh, sh, jax.ShapeDtypeStruct((T, NIN), BF), jax.ShapeDtypeStruct((T, kin), BF), jax.ShapeDtypeStruct((T, kin), F32)],
        [blk, blk, _rblk(tm, 2 * D), _rblk(tm, kin), _rblk(tm, kin)], fn, name=name, R=T, tm=tm)


def normmod_bwd(x, dh, g, sc, dres, gsrc, gate, *, name, tm=512):
    R = x.shape[0]
    tm = min(tm, R)
    has_res = dres is not None
    fused = isinstance(dh, dict)
    if fused:
        tb, a_stack, tk = dh.get("tb", False), dh.get("a_stack", False), dh["tk"]
        K = 2 * dh["a"].shape[2] if a_stack else dh["a"].shape[1]
        nk = K // tk
        deps = [] if dh.get("dep") is None else [dh["dep"]]
        n_dh = 2 + len(deps)
    else:
        nk, n_dh = 1, 1

    def elementwise(i, dhv, x_ref, g_ref, sc_ref, res_refs, out_refs):
        xv = x_ref[...]
        r = lax.rsqrt(jnp.mean(xv * xv, axis=-1, keepdims=True) + EPS)
        xh = xv * r
        n = xh * g_ref[...]
        dn = dhv * (1.0 + sc_ref[...])
        a = dn * g_ref[...]
        rows = [jnp.sum(dhv, axis=0, keepdims=True), jnp.sum(dhv * n, axis=0, keepdims=True),
                jnp.sum(dn * xh, axis=0, keepdims=True)]
        if has_res:
            dres_ref, gsrc_ref, gate_ref = res_refs
            dx_ref, dxg_ref, st_ref = out_refs
            dr = dres_ref[...]
            dx = dr + r * (a - xh * jnp.mean(a * xh, axis=-1, keepdims=True))
            dx_ref[...] = dx
            dxg_ref[...] = (dx * gate_ref[...]).astype(BF)
            rows.append(jnp.sum(dr * gsrc_ref[...], axis=0, keepdims=True))
        else:
            st_ref, = out_refs
            rows.append(jnp.zeros((1, D), F32))

        @pl.when(i == 0)
        def _():
            for k, row in enumerate(rows):
                st_ref[k:k + 1, :] = row

        @pl.when(i > 0)
        def _():
            for k, row in enumerate(rows):
                st_ref[k:k + 1, :] += row

    def body(*refs):
        x_ref, dh_refs, g_ref, sc_ref = refs[0], refs[1:1 + n_dh], refs[1 + n_dh], refs[2 + n_dh]
        rest = refs[3 + n_dh:]
        res_refs, rest = (rest[:3], rest[3:]) if has_res else ((), rest)
        out_refs = rest[:3] if has_res else rest[:1]
        i = pl.program_id(0)
        if not fused:
            elementwise(i, dh_refs[0][...], x_ref, g_ref, sc_ref, res_refs, out_refs)
            return
        acc = rest[-1]
        k = pl.program_id(1)
        part = lax.dot_general(dh_refs[0][...].astype(BF), dh_refs[1][...].astype(BF),
                               (((1,), (1 if tb else 0,)), ((), ())), preferred_element_type=F32)
        if nk == 1:
            elementwise(i, part, x_ref, g_ref, sc_ref, res_refs, out_refs)
            return

        @pl.when(k == 0)
        def _():
            acc[...] = part

        @pl.when(k > 0)
        def _():
            acc[...] += part

        @pl.when(k == nk - 1)
        def _():
            elementwise(i, acc[...], x_ref, g_ref, sc_ref, res_refs, out_refs)

    rowb = lambda w: pl.BlockSpec((1, w), lambda i, *k: (0, 0))
    blk = pl.BlockSpec((tm, D), lambda i, *k: (i, 0))
    st_spec = pl.BlockSpec((4, D), lambda i, *k: (0, 0))
    st_shape = jax.ShapeDtypeStruct((4, D), F32)
    if fused:
        if a_stack:
            nhb = K // 2 // tk
            a_spec = pl.BlockSpec((None, tm, tk), lambda i, k: (k // nhb, i, k % nhb))
        else:
            a_spec = pl.BlockSpec((tm, tk), lambda i, k: (i, k))
        b_spec = pl.BlockSpec((D, tk), lambda i, k: (0, k)) if tb else pl.BlockSpec((tk, D), lambda i, k: (k, 0))
        dh_specs = [a_spec, b_spec] + [pl.BlockSpec(memory_space=pl.ANY)] * len(deps)
        dh_args = [dh["a"], dh["b"]] + deps
        grid, sem = (R // tm, nk), ("arbitrary", "arbitrary")
        scratch = [pltpu.VMEM((tm, D) if nk > 1 else (8, 128), F32)]
    else:
        dh_specs, dh_args, grid, sem, scratch = [blk], [dh], (R // tm,), ("arbitrary",), []
    cp = pltpu.CompilerParams(dimension_semantics=sem)
    if has_res:
        return pl.pallas_call(
            body, name=name, grid=grid, in_specs=[blk] + dh_specs + [rowb(D), rowb(D), blk, blk, rowb(D)],
            out_specs=[blk, blk, st_spec], scratch_shapes=scratch,
            out_shape=[jax.ShapeDtypeStruct((R, D), F32), jax.ShapeDtypeStruct((R, D), BF), st_shape],
            compiler_params=cp,
        )(x, *dh_args, g, sc, dres, gsrc, gate)
    return pl.pallas_call(
        body, name=name, grid=grid, in_specs=[blk] + dh_specs + [rowb(D), rowb(D)],
        out_specs=st_spec, out_shape=st_shape, scratch_shapes=scratch, compiler_params=cp,
    )(x, *dh_args, g, sc)


def ffn_act_bwd(u0, df, cw, cb, *, name, tc=128):
    nb = DFF // tc

    def body(u_ref, df_ref, wg_ref, wv_ref, bg_ref, bv_ref, du_ref, dw_ref, db_ref):
        wg = [wg_ref[k:k + 1, :] for k in range(3)]
        wv = [wv_ref[k:k + 1, :] for k in range(3)]
        bg, bv = bg_ref[...], bv_ref[...]

        def chunk(r0, first, last, acc):
            xg, xv = _ext_rows(u_ref.at[0], r0, first, last), _ext_rows(u_ref.at[1], r0, first, last)
            dfe = _ext_rows(df_ref, r0, first, last)
            xg_d, xg_u, xv_d, xv_u = _roll_dn(xg), _roll_up(xg), _roll_dn(xv), _roll_up(xv)
            ug = bg + xg_d * wg[0] + xg * wg[1] + xg_u * wg[2]
            uv = bv + xv_d * wv[0] + xv * wv[1] + xv_u * wv[2]
            sig = jax.nn.sigmoid(ug)
            dug = dfe * uv * (sig * (1.0 + ug * (1.0 - sig)))
            duv = dfe * (ug * sig)
            rows = _center_rows(r0, first, last)
            du_ref[0, rows, :] = (_roll_up(dug) * wg[0] + dug * wg[1] + _roll_dn(dug) * wg[2])[_CTR].astype(BF)
            du_ref[1, rows, :] = (_roll_up(duv) * wv[0] + duv * wv[1] + _roll_dn(duv) * wv[2])[_CTR].astype(BF)
            terms = [dug * xg_d, dug * xg, dug * xg_u, dug, duv * xv_d, duv * xv, duv * xv_u, duv]
            return tuple(a + jnp.sum(t[_CTR], axis=0, keepdims=True) for a, t in zip(acc, terms))

        acc = _row_chunks(T, chunk, tuple(jnp.zeros((1, tc), F32) for _ in range(8)))
        for k in range(3):
            dw_ref[0, k:k + 1, :] = acc[k]
            dw_ref[1, k:k + 1, :] = acc[4 + k]
        db_ref[0] = acc[3]
        db_ref[1] = acc[7]

    lo = lambda r: pl.BlockSpec((r, tc), lambda j: (0, j))
    hi = lambda r: pl.BlockSpec((r, tc), lambda j: (0, nb + j))
    st = lambda r: pl.BlockSpec((2, r, tc), lambda j: (0, 0, j))
    return pl.pallas_call(
        body, name=name, grid=(nb,),
        in_specs=[st(T), lo(T), lo(3), hi(3), lo(1), hi(1)],
        out_specs=[st(T), st(3), st(1)],
        out_shape=[jax.ShapeDtypeStruct((2, T, DFF), BF), jax.ShapeDtypeStruct((2, 3, DFF), F32),
                   jax.ShapeDtypeStruct((2, 1, DFF), F32)],
        compiler_params=pltpu.CompilerParams(dimension_semantics=("parallel",)),
    )(u0, df, cw, cw, cb, cb)


def convz_bwd(p, dz, cw, cb, dp, *, name):
    o0 = O_CV // (3 * CVB)

    def body(p_ref, dz_ref, w_ref, bias_ref, dp_in, dp_ref, dw_ref, dbias_ref):
        xv, bv, cv = p_ref[:, 0:CVB], p_ref[:, CVB:2 * CVB], p_ref[:, 2 * CVB:3 * CVB]
        ci = cv * xv
        dwc = _conv(ci, w_ref, bias_ref)
        dzv = dz_ref[...]
        ddw = dzv * bv
        dci = _conv_t(ddw, w_ref)
        dp_ref[:, 0:CVB] = (dci * cv).astype(BF)
        dp_ref[:, CVB:2 * CVB] = (dzv * dwc).astype(BF)
        dp_ref[:, 2 * CVB:3 * CVB] = (dci * xv).astype(BF)
        _conv_wgrad(dw_ref, ddw, ci)
        dbias_ref[...] = jnp.sum(ddw, axis=0, keepdims=True)

    own = lambda r: pl.BlockSpec((r, CVB), lambda j: (0, j))
    return pl.pallas_call(
        body, name=name, grid=(CONV // CVB,),
        in_specs=[pl.BlockSpec((T, 3 * CVB), lambda j: (0, o0 + j)), own(T), own(3), own(1),
                  pl.BlockSpec(memory_space=pl.ANY)],
        out_specs=[pl.BlockSpec((T, 3 * CVB), lambda j: (0, o0 + j)), own(3), own(1)],
        out_shape=[jax.ShapeDtypeStruct((T, NIN), BF), jax.ShapeDtypeStruct((3, CONV), F32),
                   jax.ShapeDtypeStruct((1, CONV), F32)],
        input_output_aliases={4: 0},
        compiler_params=pltpu.CompilerParams(dimension_semantics=("parallel",)),
    )(p, dz, cw, cb, dp)


def attn_bwd(q, k, v, do, o, lse, dep, *, name, tq=1024, kc=768):
    NKC, KC = TKV // kc, kc
    deps = [] if dep is None else [dep]

    def body(q_ref, k_ref, v_ref, do_ref, o_ref, lse_ref, *rest):
        dq_ref, dk_ref, dv_ref = rest[len(deps):]
        h, i = pl.program_id(0), pl.program_id(1)

        @pl.when(i == 0)
        def _():
            dk_ref[...] = jnp.zeros_like(dk_ref)

        @pl.when((i == 0) & (h % 2 == 0))
        def _():
            dv_ref[...] = jnp.zeros_like(dv_ref)

        qv = q_ref[...]
        dom = jnp.where(_head_mask(h), do_ref[...], jnp.zeros_like(do_ref[...]))
        delta = jnp.sum(dom.astype(F32) * o_ref[...].astype(F32), axis=-1, keepdims=True)
        lse = lse_ref[:, 0:1]
        dq = jnp.zeros((tq, HP), F32)
        for c in range(NKC):
            cols = slice(c * KC, (c + 1) * KC)
            s = lax.dot_general(qv, k_ref[cols, :], (((1,), (1,)), ((), ())),
                                preferred_element_type=F32) * (SCALE * LOG2E)
            pr = jnp.exp2(s - lse)
            dp = lax.dot_general(dom, v_ref[cols, :], (((1,), (1,)), ((), ())), preferred_element_type=F32)
            ds = (pr * (dp - delta) * SCALE).astype(BF)
            dq = dq + jnp.dot(ds, k_ref[cols, :], preferred_element_type=F32)
            dk_ref[cols, :] += lax.dot_general(ds, qv, (((0,), (0,)), ((), ())), preferred_element_type=F32)
            dv_ref[cols, :] += lax.dot_general(pr.astype(BF), dom, (((0,), (0,)), ((), ())), preferred_element_type=F32)
        dq_ref[...] = dq

    return pl.pallas_call(
        body, name=name, grid=(NH, T // tq),
        in_specs=[pl.BlockSpec((tq, HP), lambda h, i: (i, h)), pl.BlockSpec((TKV, HP), lambda h, i: (0, h)),
                  pl.BlockSpec((TKV, 2 * DV), lambda h, i: (0, h // 2)), pl.BlockSpec((tq, 2 * DV), lambda h, i: (i, h // 2)),
                  pl.BlockSpec((tq, 2 * DV), lambda h, i: (i, h // 2)), pl.BlockSpec((tq, HP), lambda h, i: (i, h)),
                  *([pl.BlockSpec(memory_space=pl.ANY)] * len(deps))],
        out_specs=[pl.BlockSpec((tq, HP), lambda h, i: (i, h)), pl.BlockSpec((TKV, HP), lambda h, i: (0, h)),
                   pl.BlockSpec((TKV, 2 * DV), lambda h, i: (0, h // 2))],
        out_shape=[jax.ShapeDtypeStruct((T, NH * HP), F32), jax.ShapeDtypeStruct((TKV, NH * HP), F32),
                   jax.ShapeDtypeStruct((TKV, NH * DV), F32)],
        compiler_params=pltpu.CompilerParams(dimension_semantics=("arbitrary", "arbitrary")),
    )(q, k, v, do, o, lse, *deps)


def qprep_bwd(p, dq, qg, wq2, cq_t, sq_t, dp, *, name, tm=256):
    qcol = O_Q // 512

    def body(p_ref, dq_ref, g_ref, w_ref, c_ref, s_ref, dp_in, dp_ref, dq2_ref, dg_ref):
        i = pl.program_id(0)
        dqv = dq_ref[...]
        cc = jnp.concatenate([c_ref[...]] * NH, axis=1)
        ss = jnp.concatenate([s_ref[...]] * NH, axis=1)
        dq2 = jnp.concatenate([dqv * cc, dqv * ss], axis=1).astype(BF)
        dq2_ref[...] = dq2
        dcq = lax.dot_general(dq2, w_ref[...], (((1,), (1,)), ((), ())), preferred_element_type=F32)
        pq = p_ref[...]
        r = lax.rsqrt(jnp.sum(pq * pq, axis=-1, keepdims=True) * (1.0 / QL) + EPS)
        xh = pq * r
        a = dcq * g_ref[...]
        dp_ref[...] = (r * (a - xh * (jnp.sum(a * xh, axis=-1, keepdims=True) * (1.0 / QL)))).astype(BF)
        dg = jnp.sum(dcq * xh, axis=0, keepdims=True)

        @pl.when(i == 0)
        def _():
            dg_ref[...] = dg

        @pl.when(i > 0)
        def _():
            dg_ref[...] += dg

    return pl.pallas_call(
        body, name=name, grid=(T // tm,),
        in_specs=[pl.BlockSpec((tm, 512), lambda i: (i, qcol)), pl.BlockSpec((tm, NH * HP), lambda i: (i, 0)), _row(512),
                  pl.BlockSpec((512, 2 * NH * HP), lambda i: (0, 0)),
                  pl.BlockSpec((tm, HP), lambda i: (i, 0)), pl.BlockSpec((tm, HP), lambda i: (i, 0)),
                  pl.BlockSpec(memory_space=pl.ANY)],
        out_specs=[pl.BlockSpec((tm, 512), lambda i: (i, qcol)), pl.BlockSpec((tm, 2 * NH * HP), lambda i: (i, 0)), _row(512)],
        out_shape=[jax.ShapeDtypeStruct((T, NIN), BF), jax.ShapeDtypeStruct((T, 2 * NH * HP), BF),
                   jax.ShapeDtypeStruct((1, 512), F32)],
        input_output_aliases={6: 0},
        compiler_params=pltpu.CompilerParams(dimension_semantics=("arbitrary",)),
    )(p, dq, qg, wq2, cq_t, sq_t, dp)


def kvprep_bwd(pc, p, dk, dv, kvg, wkv2, ck, sk, dp, *, name, tm=256):
    assert tm == TC
    nb = TKV // tm
    kvcol = O_KV // 512

    def body(pc_ref, p_ref, dk_ref, dv_ref, g_ref, w_ref, ck_ref, sk_ref, dp_in, dp_ref, dpc_ref, dkv2_ref, dg_ref):
        i = pl.program_id(0)
        t = jnp.where(i == NLAT, pc_ref[...], p_ref[...])
        pk = t[:, :KVL]
        r = lax.rsqrt(jnp.mean(pk * pk, axis=-1, keepdims=True) + EPS)
        xh = pk * r
        dkv = dk_ref[...]
        dkv2 = jnp.concatenate([dkv, dv_ref[...]], axis=1).astype(BF)
        dkv2_ref[...] = dkv2
        dckv = lax.dot_general(dkv2, w_ref[...], (((1,), (1,)), ((), ())), preferred_element_type=F32)
        a = dckv * g_ref[...]
        dpk = r * (a - xh * jnp.mean(a * xh, axis=-1, keepdims=True))
        dkr = dkv[:, 0:HP]
        for hh in range(1, NH):
            dkr = dkr + dkv[:, hh * HP:(hh + 1) * HP]
        res = jnp.concatenate([dpk, dkr * ck_ref[...], dkr * sk_ref[...]], axis=1).astype(BF)
        dg = jnp.sum(dckv * xh, axis=0, keepdims=True)

        @pl.when(i == 0)
        def _():
            dg_ref[...] = dg

        @pl.when(i > 0)
        def _():
            dg_ref[...] += dg

        @pl.when(i < NLAT)
        def _():
            dp_ref[...] = res

        @pl.when(i == NLAT)
        def _():
            dpc_ref[...] = res

    rb = lambda w: pl.BlockSpec((tm, w), lambda i: (i, 0))
    return pl.pallas_call(
        body, name=name, grid=(nb,),
        in_specs=[pl.BlockSpec((tm, 512), lambda i: (0, 0)),
                  pl.BlockSpec((tm, 512), lambda i: (jnp.minimum(i, NLAT - 1), kvcol)),
                  rb(NH * HP), rb(NH * DV), _row(KVL), pl.BlockSpec((KVL, NH * HP + NH * DV), lambda i: (0, 0)),
                  rb(HP), rb(HP), pl.BlockSpec(memory_space=pl.ANY)],
        out_specs=[pl.BlockSpec((tm, 512), lambda i: (jnp.minimum(i, NLAT - 1), kvcol)),
                   pl.BlockSpec((tm, 512), lambda i: (0, 0)), rb(NH * HP + NH * DV), _row(KVL)],
        out_shape=[jax.ShapeDtypeStruct((T, NIN), BF), jax.ShapeDtypeStruct((TC, 512), BF),
                   jax.ShapeDtypeStruct((TKV, NH * HP + NH * DV), BF), jax.ShapeDtypeStruct((1, KVL), F32)],
        input_output_aliases={8: 0},
        compiler_params=pltpu.CompilerParams(dimension_semantics=("arbitrary",)),
    )(pc, p, dk, dv, kvg, wkv2, ck, sk, dp)


def _pieces(src, width, n):
    out, c = [], src
    while c < src + width:
        k = c // n
        w = min(src + width, (k + 1) * n) - c
        out.append((k, c - k * n, c - src, w))
        c += w
    return out


def _win_moves():
    mv = [(2208, 1024, O_GA), (3232, 1024, O_GC), (0, KVL, O_KV), (256, DR, O_KV + KVL + DN), (288, QL, O_Q)]
    mv += [(256 + _swap_start(g), 8, O_KV + KVL + HP + DN + 8 * g) for g in range(4)]
    for j in range(CONV // CVB):
        base = O_CV + 3 * CVB * j
        mv += [(672 + CVB * j, CVB, base), (1184 + CVB * j, CVB, base + CVB), (1696 + CVB * j, CVB, base + 2 * CVB)]
    return mv


_WIN_ZERO = [(O_KV + KVL, DN), (O_KV + KVL + DN + DR, HP - DN - DR), (O_KV + KVL + HP, DN),
             (O_KV + KVL + HP + DN + DR, HP - DN - DR), (O_Q + QL, 512 - QL)]


def build_win(g, *, name, tm=256):
    def body(g_ref, o_ref):
        for src, w, dst in _win_moves():
            for k, a, off, pw in _pieces(src, w, SH_IN):
                o_ref[:, dst + off:dst + off + pw] = g_ref[k, :, a:a + pw]
        for c0, w in _WIN_ZERO:
            o_ref[:, c0:c0 + w] = jnp.zeros((tm, w), o_ref.dtype)

    return pl.pallas_call(
        body, name=name, grid=(D // tm,), in_specs=[pl.BlockSpec((NDEV, tm, SH_IN), lambda i: (0, i, 0))],
        out_specs=pl.BlockSpec((tm, NIN), lambda i: (i, 0)), out_shape=jax.ShapeDtypeStruct((D, NIN), g.dtype),
        compiler_params=pltpu.CompilerParams(dimension_semantics=("parallel",)),
    )(g)


def shard_win_grad(dwt, dwct, *, name, tc=256):
    def body(dw_ref, dwc_ref, o_ref, kvs):
        kvs[...] = dw_ref[O_KV:O_KV + 512, :] + dwc_ref[...]

        def src(row, w):
            if O_KV <= row < O_KV + 512:
                return kvs[row - O_KV:row - O_KV + w, :]
            return dw_ref[row:row + w, :]

        for s, w, dst in _win_moves():
            if w == 8 or s == 256:
                continue
            for k, a, off, pw in _pieces(s, w, SH_IN):
                o_ref[k, a:a + pw, :] = src(dst + off, pw).astype(o_ref.dtype)
        for g in range(4):
            val = src(O_KV + KVL + DN + 8 * g, 8) + src(O_KV + KVL + HP + DN + _swap_start(g), 8)
            o_ref[0, 256 + 8 * g:256 + 8 * g + 8, :] = val.astype(o_ref.dtype)

    return pl.pallas_call(
        body, name=name, grid=(D // tc,),
        in_specs=[pl.BlockSpec((NIN, tc), lambda j: (0, j)), pl.BlockSpec((512, tc), lambda j: (0, j))],
        out_specs=pl.BlockSpec((NDEV, SH_IN, tc), lambda j: (0, 0, j)),
        out_shape=jax.ShapeDtypeStruct((NDEV, SH_IN, D), BF),
        scratch_shapes=[pltpu.VMEM((512, tc), F32)],
        compiler_params=pltpu.CompilerParams(dimension_semantics=("parallel",)),
    )(dwt, dwct)


def _eye(n, m):
    return (lax.broadcasted_iota(jnp.int32, (n, m), 0) == lax.broadcasted_iota(jnp.int32, (n, m), 1)).astype(BF)


_NT = (((1,), (1,)), ((), ()))


def build_wq_wkv(gq, gkv, *, name):
    def body(gq_ref, gkv_ref, q_ref, kv_ref):
        q_ref[...] = jnp.zeros_like(q_ref)
        kv_ref[...] = jnp.zeros_like(kv_ref)
        eye = _eye(QL, QL)
        for h in range(NH):
            qh = lax.dot_general(eye, gq_ref[h], _NT, preferred_element_type=F32).astype(q_ref.dtype)
            q_ref[0:QL, h * HP:h * HP + DN + DR] = qh
            for g in range(4):
                c0 = NH * HP + h * HP + DN + 8 * g
                q_ref[0:QL, c0:c0 + 8] = qh[:, DN + _swap_start(g):DN + _swap_start(g) + 8]
            kv_ref[:, h * HP:h * HP + DN] = gkv_ref[h, :, 0:DN]
            kv_ref[:, NH * HP + h * DV:NH * HP + (h + 1) * DV] = gkv_ref[h, :, DN:DN + DV]

    vm = pl.BlockSpec(memory_space=pltpu.VMEM)
    return pl.pallas_call(
        body, name=name, in_specs=[vm, vm], out_specs=[vm, vm],
        out_shape=[jax.ShapeDtypeStruct((512, 2 * NH * HP), gq.dtype), jax.ShapeDtypeStruct((KVL, NH * HP + NH * DV), gq.dtype)],
    )(gq, gkv)


def shard_wq_wkv_grad(dwq2, dwkv2, *, name):
    def body(q_ref, kv_ref, gq_ref, gkv_ref, xs):
        xs[...] = jnp.zeros_like(xs)
        eye = _eye(DN + DR, HP)
        for h in range(NH):
            xs[:, 0:DN] = q_ref[0:QL, h * HP:h * HP + DN].astype(BF)
            for g in range(4):
                a = q_ref[0:QL, h * HP + DN + 8 * g:h * HP + DN + 8 * g + 8]
                c0 = NH * HP + h * HP + DN + _swap_start(g)
                xs[:, DN + 8 * g:DN + 8 * g + 8] = (a + q_ref[0:QL, c0:c0 + 8]).astype(BF)
            gq_ref[h] = lax.dot_general(eye, xs[...], _NT, preferred_element_type=F32).astype(BF)
            gkv_ref[h, :, 0:DN] = kv_ref[:, h * HP:h * HP + DN].astype(BF)
            gkv_ref[h, :, DN:DN + DV] = kv_ref[:, NH * HP + h * DV:NH * HP + (h + 1) * DV].astype(BF)

    vm = pl.BlockSpec(memory_space=pltpu.VMEM)
    return pl.pallas_call(
        body, name=name, in_specs=[vm, vm], out_specs=[vm, vm],
        out_shape=[jax.ShapeDtypeStruct((NDEV, DN + DR, QL), BF), jax.ShapeDtypeStruct((NDEV, KVL, DN + DV), BF)],
        scratch_shapes=[pltpu.VMEM((QL, HP), BF)],
    )(dwq2, dwkv2)


def unshard_cols(g, *, name, tm=256):
    _, K, n = g.shape
    tm = _pick(K, tm, 16)

    def body(g_ref, o_ref):
        for k in range(NDEV):
            o_ref[:, k * n:(k + 1) * n] = g_ref[k]

    return pl.pallas_call(
        body, name=name, grid=(K // tm,), in_specs=[pl.BlockSpec((NDEV, tm, n), lambda i: (0, i, 0))],
        out_specs=pl.BlockSpec((tm, NDEV * n), lambda i: (i, 0)), out_shape=jax.ShapeDtypeStruct((K, NDEV * n), g.dtype),
        compiler_params=pltpu.CompilerParams(dimension_semantics=("parallel",)),
    )(g)


def shard_cols(w, *, name, tm=256):
    K, n8 = w.shape
    n = n8 // NDEV
    tm = _pick(K, tm, 16)

    def body(w_ref, o_ref):
        for k in range(NDEV):
            o_ref[k] = w_ref[:, k * n:(k + 1) * n]

    return pl.pallas_call(
        body, name=name, grid=(K // tm,), in_specs=[pl.BlockSpec((tm, n8), lambda i: (i, 0))],
        out_specs=pl.BlockSpec((NDEV, tm, n), lambda i: (0, i, 0)), out_shape=jax.ShapeDtypeStruct((NDEV, K, n), w.dtype),
        compiler_params=pltpu.CompilerParams(dimension_semantics=("parallel",)),
    )(w)


def _rope_tables():
    t = np.arange(T)
    row = (t // GRID_W).astype(np.float32)
    col = (t % GRID_W).astype(np.float32)
    axis_dim = DR // 2
    inv = (np.float32(ROPE_THETA) ** (-np.arange(0, axis_dim, 2, dtype=np.float32) / np.float32(axis_dim))).astype(np.float32)
    ar, ac = (row[:, None] * inv).astype(np.float32), (col[:, None] * inv).astype(np.float32)
    cosv = np.concatenate([np.cos(ar), np.cos(ar), np.cos(ac), np.cos(ac)], axis=1).astype(np.float32)
    sinv = np.concatenate([-np.sin(ar), np.sin(ar), -np.sin(ac), np.sin(ac)], axis=1).astype(np.float32)
    ck = np.zeros((TKV, HP), np.float32)
    sk = np.zeros((TKV, HP), np.float32)
    ck[T:, DN:DN + DR] = 1.0
    ck[:T, DN:DN + DR] = cosv
    sk[:T, DN:DN + DR] = sinv
    cq = np.zeros((T, HP), np.float32)
    cq[:, :DN] = 1.0
    cq[:, DN:DN + DR] = cosv
    return jnp.asarray(ck), jnp.asarray(sk), jnp.asarray(cq), jnp.asarray(sk[:T])


def _local_step(x, ctx, tgt, mod_lat, mod_ctx, n1g, qg, kvg, n2g, fg, conv_w, conv_b, ffn_w, ffn_b, get_w, put_g, dep0):
    sh1, sc1, g1, sh2, sc2, g2 = [mod_lat[:, i * D:(i + 1) * D] for i in range(6)]
    csh1, csc1 = mod_ctx[:, 0:D], mod_ctx[:, D:2 * D]
    ck, sk, cq_t, sq_t = _rope_tables()
    qg_p = jnp.pad(qg, ((0, 0), (0, 512 - QL)))

    hcat = normmod_cat(ctx, x, n1g, csc1, csh1, sc1, sh1, dep0, name="normmod1")
    win = get_w("in", hcat)
    p = mm(hcat, win, M=T, tn=768, name="in_proj")
    pc = mm(hcat, win, M=TC, N=512, a_off=(T, 0), b_off=(0, O_KV), name="in_proj_ctx")
    wq2, wkv2, wao, wco, wo = get_w("mid", p)
    kh, vh, ckv = kvprep(pc, p, kvg, wkv2, ck, sk, name="kvprep")
    qr, cq = qprep(p, qg_p, wq2, cq_t, sq_t, name="qprep")
    o, lse = attn_fwd(qr, kh, vh, name="attn_fwd")
    z = convz(p, conv_w, conv_b, name="convz")
    ya, yc, merged = out_proj_merge(o, wao, z, wco, p, name="attn_conv_out_gate_merge")
    a_out, x1, h2 = oproj_resid(merged, wo, x, g1, n2g, sc2, sh2, name="o_proj_resid_normmod2")
    wup = get_w("up", h2)
    u0 = mm(h2, wup, tb=True, o_stack=True, tn=1408, name="up_proj")
    f = ffn_act(u0, ffn_w, ffn_b, name="ffn_act")
    wdn = get_w("down", f)
    dn, dx2, dd, dfg, loss = down_final(f, wdn, x1, g2, fg, tgt, name="down_proj_final_loss")

    df = mm(dd, wdn, tb=True, tn=1408, name="down_proj_dx")
    dwdn = mm(f, dd, ta=True, out_dtype=BF, tm=1408, name="down_proj_dw")
    du0, dffn_w, dffn_b = ffn_act_bwd(u0, df, ffn_w, ffn_b, name="ffn_act_bwd")
    dwup = mm(du0, h2, ta=True, a_stack=True, out_dtype=BF, tm=1408, name="up_proj_dw")
    tok = put_g("ffn", dict(dwup=dwup, dwdn=dwdn))
    dx1, da, st2 = normmod_bwd(x1, dict(a=du0, b=wup, a_stack=True, tk=DFF, dep=tok), n2g, sc2, dx2, dn, g1,
                               name="up_proj_dx_normmod2_bwd")

    dwo = mm(merged, da, ta=True, out_dtype=BF, tn=512, name="o_proj_dw")
    dya, dyc, dp, do, dz = oproj_dx_gate_bwd(da, wo, p, ya, yc, wao, wco, name="o_proj_dx_gate_merge_bwd")
    dwao = mm(o, dya, ta=True, out_dtype=BF, tn=512, name="attn_out_dw")
    dwco = mm(z, dyc, ta=True, out_dtype=BF, tn=512, name="conv_out_dw")
    tok = put_g("mid", dict(dwao=dwao, dwco=dwco, dwo=dwo))
    dp, dconv_w, dconv_b = convz_bwd(p, dz, conv_w, conv_b, dp, name="convz_bwd")
    dq, dk, dv = attn_bwd(qr, kh, vh, do, o, lse, tok, name="attn_bwd")
    dp, dq2, dqg = qprep_bwd(p, dq, qg_p, wq2, cq_t, sq_t, dp, name="qprep_bwd")
    dp, dpc, dkv2, dkvg = kvprep_bwd(pc, p, dk, dv, kvg, wkv2, ck, sk, dp, name="kvprep_bwd")

    dwin = mm(dp, hcat, ta=True, K=T, tm=768, name="in_proj_dw")
    dwin_c = mm(dpc, hcat, ta=True, K=TC, b_off=(T, 0), name="in_proj_ctx_dw")
    tok = put_g("in", dict(dwin=dwin, dwin_c=dwin_c))
    dwq2 = mm(cq, dq2, ta=True, dep=tok, name="q_up_dw")
    dwkv2 = mm(ckv, dkv2, ta=True, dep=tok, name="kv_up_dw")
    tok = put_g("qkv", dict(dwq2=dwq2, dwkv2=dwkv2))
    dhc = mm(dpc, win, tb=True, N=D, K=512, b_off=(0, O_KV), dep=tok, name="in_proj_ctx_dx")
    dx, _, st1 = normmod_bwd(x, dict(a=dp, b=win, tb=True, tk=NIN, dep=tok), n1g, sc1, dx1, a_out, g1,
                             name="in_proj_dx_normmod1_bwd")
    stc = normmod_bwd(ctx, dhc, n1g, csc1, None, None, None, name="normmod1_ctx_bwd")

    return dict(loss=loss, dx=dx, st1=st1, st2=st2, stc=stc, dqg=dqg, dkvg=dkvg, dfg=dfg,
                dconv_w=dconv_w, dconv_b=dconv_b, dffn_w=dffn_w, dffn_b=dffn_b)


def _me():
    x, y, c = lax.axis_index("x"), lax.axis_index("y"), lax.axis_index("c")
    return x, y, c, 4 * x + 2 * y + c


def _peer(x, y, c, k):
    px = 1 - x if k & 4 else x
    py = 1 - y if k & 2 else y
    pc = 1 - c if k & 1 else c
    return (px, py, pc), 4 * px + 2 * py + pc


def _exchange_tiles(src_of_peer, buf, send_sem, recv_sem):
    x, y, c, me = _me()
    for k in range(1, NDEV):
        dev, lin = _peer(x, y, c, k)
        pltpu.make_async_remote_copy(src_ref=src_of_peer(lin), dst_ref=buf.at[me], send_sem=send_sem, recv_sem=recv_sem,
                                     device_id=dev, device_id_type=MESH).start()
    seven = buf.at[pl.ds(0, NDEV - 1)]
    pltpu.make_async_remote_copy(src_ref=seven, dst_ref=seven, send_sem=send_sem, recv_sem=recv_sem,
                                 device_id=(x, y, c), device_id_type=MESH).wait()


def _silu(z):
    return z * jax.nn.sigmoid(z)


def ada_fwd(c, c_ctx, ffn_w, conv_w, w_shard, b_ada, deps, *, name):
    nsh, nf, nc = w_shard.shape[1], ffn_w.shape[2], conv_w.shape[2]
    deps = [d for d in deps if d is not None]

    def body(c_ref, cc_ref, fw_ref, cw_ref, w_ref, b_ref, *rest):
        s_ref, ml_ref, mc_ref, fwf_ref, cwf_ref, m_ref, mine, res, sems = rest[len(deps):]
        x, y, c, me = _me()
        mine[...] = jnp.zeros_like(mine)
        mine[0:1, :] = _silu(c_ref[...])
        mine[1:2, :] = _silu(cc_ref[...])
        for k in range(3):
            mine[2 + k:3 + k, 0:fw_ref.shape[2]] = fw_ref[k]
            mine[5 + k:6 + k, 0:cw_ref.shape[2]] = cw_ref[k]
        s_ref[me] = mine[...]
        _exchange_tiles(lambda lin: mine, s_ref, sems.at[0], sems.at[1])
        sall = s_ref[...].reshape(NDEV * 8, D).astype(BF)
        r = jnp.dot(sall, w_ref[...].astype(BF), preferred_element_type=F32) + b_ref[me]
        res[...] = r.reshape(NDEV, 8, nsh)
        m_ref[me] = res[me]
        _exchange_tiles(lambda lin: res.at[lin], m_ref, sems.at[2], sems.at[3])
        for j in range(NDEV):
            ml_ref[:, j * nsh:(j + 1) * nsh] = m_ref[j, 0:1, :]
            mc_ref[:, j * nsh:(j + 1) * nsh] = m_ref[j, 1:2, :]
            fwf_ref[:, j * nf:(j + 1) * nf] = s_ref[j, 2:5, 0:nf]
            cwf_ref[:, j * nc:(j + 1) * nc] = s_ref[j, 5:8, 0:nc]

    vm = pl.BlockSpec(memory_space=pltpu.VMEM)
    return pl.pallas_call(
        body, name=name, in_specs=[vm] * 6 + [pl.BlockSpec(memory_space=pl.ANY)] * len(deps), out_specs=[vm] * 5,
        out_shape=[jax.ShapeDtypeStruct((NDEV, 8, D), F32), jax.ShapeDtypeStruct((1, NDEV * nsh), F32),
                   jax.ShapeDtypeStruct((1, NDEV * nsh), F32), jax.ShapeDtypeStruct((3, NDEV * nf), F32),
                   jax.ShapeDtypeStruct((3, NDEV * nc), F32)],
        scratch_shapes=[pltpu.VMEM((NDEV, 8, nsh), F32), pltpu.VMEM((8, D), F32), pltpu.VMEM((NDEV, 8, nsh), F32),
                        pltpu.SemaphoreType.DMA((4,))],
    )(c, c_ctx, ffn_w, conv_w, w_shard, b_ada, *deps)


P_DML, P_DMC, P_N1, P_QG, P_KVG, P_CB, P_N2, P_FB, P_FG, P_CW, P_FW, P_LOSS, P_ROWS = 0, 6, 12, 13, 14, 15, 16, 17, 23, 24, 27, 45, 48
FROWS = 3


def pack_small(r, *, name):
    ins = [r["st1"], r["st2"], r["stc"], r["dqg"], r["dkvg"], r["dconv_b"], r["dffn_b"], r["dfg"], r["dconv_w"],
           r["dffn_w"], r["loss"]]

    def put_wide(p, row0, row, n):
        for j in range(-(-n // D)):
            w = min(D, n - j * D)
            p[row0 + j:row0 + j + 1, 0:w] = row[:, j * D:j * D + w]

    def body(st1, st2, stc, qg, kvg, cb, fb, fg, cw, fw, loss, p):
        p[...] = jnp.zeros_like(p)
        for j, row in enumerate((st1[0:1, :], st1[1:2, :], st1[3:4, :], st2[0:1, :], st2[1:2, :], st2[3:4, :])):
            p[P_DML + j:P_DML + j + 1, :] = row
        p[P_DMC:P_DMC + 2, :] = stc[0:2, :]
        p[P_N1:P_N1 + 1, :] = st1[2:3, :] + stc[2:3, :]
        p[P_N2:P_N2 + 1, :] = st2[2:3, :]
        put_wide(p, P_QG, qg, 512)
        put_wide(p, P_KVG, kvg, KVL)
        put_wide(p, P_CB, cb, CONV)
        put_wide(p, P_FG, fg, D)
        put_wide(p, P_LOSS, loss, 128)
        for s in range(2):
            put_wide(p, P_FB + FROWS * s, fb.at[s], DFF)
        for k in range(3):
            put_wide(p, P_CW + k, cw.at[k:k + 1], CONV)
            for s in range(2):
                put_wide(p, P_FW + FROWS * (2 * k + s), fw.at[s, k:k + 1], DFF)

    vm = pl.BlockSpec(memory_space=pltpu.VMEM)
    return pl.pallas_call(
        body, name=name, in_specs=[vm] * len(ins), out_specs=vm, out_shape=jax.ShapeDtypeStruct((P_ROWS, D), F32),
    )(*ins)


def sum_slots(a, *, name):
    rows = dict(norm1_g=(P_N1, D), q_norm_g=(P_QG, QL), kv_norm_g=(P_KVG, KVL), conv_b=(P_CB, CONV), norm2_g=(P_N2, D),
                final_g=(P_FG, D))

    def body(a_ref, sum_ref, *out):
        acc = a_ref[0]
        for k in range(1, NDEV):
            acc = acc + a_ref[k]
        sum_ref[...] = acc
        for ref, (row, n) in zip(out, rows.values()):
            ref[...] = sum_ref[row:row + 1, 0:n]
        fb, bada = out[len(rows):]
        for s in range(2):
            for j in range(FROWS):
                w = min(D, DFF - j * D)
                row = P_FB + FROWS * s + j
                fb[:, s * DFF + j * D:s * DFF + j * D + w] = sum_ref[row:row + 1, 0:w]
        for j in range(6):
            bada[:, j * D:(j + 1) * D] = sum_ref[P_DML + j:P_DML + j + 1, :] + sum_ref[P_DMC + j:P_DMC + j + 1, :]

    vm = pl.BlockSpec(memory_space=pltpu.VMEM)
    widths = [n for _, n in rows.values()] + [2 * DFF, 6 * D]
    outs = pl.pallas_call(
        body, name=name, in_specs=[vm], out_specs=[vm] * (1 + len(widths)),
        out_shape=[jax.ShapeDtypeStruct(a.shape[1:], F32)] + [jax.ShapeDtypeStruct((1, n), F32) for n in widths])(a)
    return outs[0], dict(zip(list(rows) + ["ffn_conv_b", "b_ada"], outs[1:]))


def ada_bwd(s_all, dml, dmc, w_shard, c_ctx, *, name):
    nsh = w_shard.shape[1]

    def body(s_ref, dml_ref, dmc_ref, w_ref, c_ref, dw_ref, gc_ref, s16, dm16, part, buf, sems):
        x, y, c, me = _me()
        s16[...] = jnp.zeros_like(s16)
        dm16[...] = jnp.zeros_like(dm16)
        for k in range(NDEV):
            s16[k:k + 1, :] = s_ref[k, 0:1, :]
        s16[8:9, :] = s_ref[0, 1:2, :]
        dm16[0:8, :] = dml_ref[...]
        dm16[8:9, :] = dmc_ref[...]
        dw_ref[...] = lax.dot_general(s16[...].astype(BF), dm16[...].astype(BF), (((0,), (0,)), ((), ())),
                                      preferred_element_type=F32)
        part[...] = lax.dot_general(dm16[8:16, :].astype(BF), w_ref[...].astype(BF), (((1,), (1,)), ((), ())),
                                    preferred_element_type=F32)
        buf[me] = part[...]
        _exchange_tiles(lambda lin: part, buf, sems.at[0], sems.at[1])
        acc = buf[0]
        for k in range(1, NDEV):
            acc = acc + buf[k]
        z = c_ref[...]
        sg = jax.nn.sigmoid(z)
        gc_ref[...] = acc * (sg * (1.0 + z * (1.0 - sg)))

    vm = pl.BlockSpec(memory_space=pltpu.VMEM)
    return pl.pallas_call(
        body, name=name, in_specs=[vm] * 5, out_specs=[vm, vm],
        out_shape=[jax.ShapeDtypeStruct((D, nsh), F32), jax.ShapeDtypeStruct((8, D), F32)],
        scratch_shapes=[pltpu.VMEM((16, D), F32), pltpu.VMEM((16, nsh), F32), pltpu.VMEM((8, D), F32),
                        pltpu.VMEM((NDEV, 8, D), F32), pltpu.SemaphoreType.DMA((2,))],
    )(s_all, dml, dmc, w_shard, c_ctx)


HBM_SPEC = pl.BlockSpec(memory_space=pltpu.HBM)
SEM_SPEC = pl.BlockSpec(memory_space=pltpu.SEMAPHORE)
EFFECT = pltpu.SideEffectType.DATAFLOW_SIDE_EFFECTING


ALL_PEERS = tuple(range(1, NDEV))
FIRST_HOP = (1, 2, 4, 6)
RELAY = (2, 4, 6)


def _exchange_copies(srcs, lands, send, recv, per_peer, peers):
    x, y, c, me = _me()
    n = len(peers)
    cps = []
    for t in range(len(srcs)):
        for j, k in enumerate(peers):
            dev, lin = _peer(x, y, c, k)
            cps.append(pltpu.make_async_remote_copy(
                src_ref=srcs[t].at[lin] if per_peer else srcs[t], dst_ref=lands[t].at[me],
                send_sem=send.at[n * t + j], recv_sem=recv.at[n * t + j], device_id=dev, device_id_type=MESH))
    return cps


def _relay_copies(lands, send, recv):
    x, y, c, me = _me()
    n = len(RELAY)
    cps = []
    for t in range(len(lands)):
        for j, k in enumerate(RELAY):
            slot = lands[t].at[_peer(x, y, c, k)[1]]
            cps.append(pltpu.make_async_remote_copy(
                src_ref=slot, dst_ref=slot, send_sem=send.at[n * t + j], recv_sem=recv.at[n * t + j],
                device_id=(x, y, 1 - c), device_id_type=MESH))
    return cps


def _own_copies(srcs, lands, own, per_peer):
    me = _me()[3]
    return [pltpu.make_async_copy(srcs[t].at[me] if per_peer else srcs[t], lands[t].at[me], own.at[t])
            for t in range(len(srcs))]


def exchange_start(srcs, *, per_peer, name, dep=None, peers=ALL_PEERS):
    nt = len(srcs)
    ns = len(peers) * nt
    land_shapes = [(a.shape if per_peer else (NDEV,) + a.shape) for a in srcs]
    deps = [] if dep is None else [dep]

    def body(*refs):
        src, land = refs[:nt], refs[nt:2 * nt]
        send, recv, own = refs[2 * nt + len(deps):2 * nt + len(deps) + 3]
        for cp in _exchange_copies(src, land, send, recv, per_peer, peers) + _own_copies(src, land, own, per_peer):
            cp.start()
        refs[-1][...] = jnp.zeros_like(refs[-1])

    hb = lambda a: pltpu.with_memory_space_constraint(a, pltpu.HBM)
    outs = pl.pallas_call(
        body, name=name,
        out_shape=(pltpu.SemaphoreType.DMA((ns,)), pltpu.SemaphoreType.DMA((ns,)), pltpu.SemaphoreType.DMA((nt,)),
                   *[pltpu.HBM(a.shape, a.dtype) for a in srcs], *[pltpu.HBM(s, a.dtype) for s, a in zip(land_shapes, srcs)],
                   jax.ShapeDtypeStruct((8, 128), F32)),
        in_specs=[HBM_SPEC] * (2 * nt) + [pl.BlockSpec(memory_space=pl.ANY)] * len(deps),
        out_specs=(SEM_SPEC, SEM_SPEC, SEM_SPEC, *([HBM_SPEC] * (2 * nt)), pl.BlockSpec(memory_space=pltpu.VMEM)),
        input_output_aliases={i: 3 + i for i in range(2 * nt)},
        compiler_params=pltpu.CompilerParams(has_side_effects=EFFECT),
    )(*[hb(a) for a in srcs], *[hb(lax.empty(s, a.dtype)) for s, a in zip(land_shapes, srcs)], *deps)
    return dict(send=outs[0], recv=outs[1], own=outs[2], src=list(outs[3:3 + nt]), land=list(outs[3 + nt:3 + 2 * nt]),
                token=outs[-1], per_peer=per_peer, peers=peers)


def exchange_wait(h, after, *, name):
    nt = len(h["src"])
    per_peer, peers = h["per_peer"], h["peers"]
    after = list(after) if isinstance(after, (list, tuple)) else [after]

    def body(*refs):
        src, land, send, recv, own = refs[:nt], refs[nt:2 * nt], refs[2 * nt], refs[2 * nt + 1], refs[2 * nt + 2]
        for cp in _exchange_copies(src, land, send, recv, per_peer, peers):
            cp.wait_send()
            cp.wait_recv()
        for cp in _own_copies(src, land, own, per_peer):
            cp.wait()

    outs = pl.pallas_call(
        body, name=name,
        out_shape=(*[pltpu.HBM(a.shape, a.dtype) for a in h["src"]], *[pltpu.HBM(a.shape, a.dtype) for a in h["land"]]),
        in_specs=[HBM_SPEC] * (2 * nt) + [SEM_SPEC, SEM_SPEC, SEM_SPEC] + [pl.BlockSpec(memory_space=pl.ANY)] * len(after),
        out_specs=tuple([HBM_SPEC] * (2 * nt)),
        input_output_aliases={i: i for i in range(2 * nt)},
        compiler_params=pltpu.CompilerParams(has_side_effects=EFFECT),
    )(*h["src"], *h["land"], h["send"], h["recv"], h["own"], *after)
    return list(outs[nt:])


def relay_start(lands, *, name):
    nt = len(lands)
    ns = len(RELAY) * nt

    def body(*refs):
        for cp in _relay_copies(refs[:nt], refs[nt], refs[nt + 1]):
            cp.start()

    outs = pl.pallas_call(
        body, name=name,
        out_shape=(pltpu.SemaphoreType.DMA((ns,)), pltpu.SemaphoreType.DMA((ns,)),
                   *[pltpu.HBM(a.shape, a.dtype) for a in lands]),
        in_specs=[HBM_SPEC] * nt, out_specs=(SEM_SPEC, SEM_SPEC, *([HBM_SPEC] * nt)),
        input_output_aliases={i: 2 + i for i in range(nt)},
        compiler_params=pltpu.CompilerParams(has_side_effects=EFFECT),
    )(*lands)
    return dict(send=outs[0], recv=outs[1], land=list(outs[2:]))


def relay_wait(h, *, name):
    nt = len(h["land"])

    def body(*refs):
        for cp in _relay_copies(refs[:nt], refs[nt], refs[nt + 1]):
            cp.wait_send()
            cp.wait_recv()

    outs = pl.pallas_call(
        body, name=name, out_shape=tuple(pltpu.HBM(a.shape, a.dtype) for a in h["land"]),
        in_specs=[HBM_SPEC] * nt + [SEM_SPEC, SEM_SPEC], out_specs=tuple([HBM_SPEC] * nt),
        input_output_aliases={i: i for i in range(nt)},
        compiler_params=pltpu.CompilerParams(has_side_effects=EFFECT),
    )(*h["land"], h["send"], h["recv"])
    return list(outs)


def _adamw_math(w, g, m, v):
    nm = B1 * m + (1.0 - B1) * g
    nv = B2 * v + (1.0 - B2) * (g * g)
    m_hat = nm / (1.0 - B1 ** STEP)
    v_hat = nv / (1.0 - B2 ** STEP)
    return -LR * (m_hat / (jnp.sqrt(v_hat) + AEPS) + WD * w), nm, nv


def adamw_many(ws, gs, ms, vs, *, name):
    n = len(ws)

    def body(*refs):
        for k in range(n):
            d, nm, nv = _adamw_math(refs[k][...], refs[n + k][...], refs[2 * n + k][...], refs[3 * n + k][...])
            refs[4 * n + k][...] = d
            refs[5 * n + k][...] = nm
            refs[6 * n + k][...] = nv

    vm = pl.BlockSpec(memory_space=pltpu.VMEM)
    sh = [jax.ShapeDtypeStruct(w.shape, F32) for w in ws]
    outs = pl.pallas_call(body, name=name, in_specs=[vm] * (4 * n), out_specs=[vm] * (3 * n), out_shape=sh * 3,
                          )(*ws, *gs, *ms, *vs)
    return outs[:n], outs[n:2 * n], outs[2 * n:]


def adamw(w, g, m, v, *, name, tr=256):
    R, C = w.shape
    tr = _pick(R, tr, 8)

    def body(w_ref, g_ref, m_ref, v_ref, d_ref, nm_ref, nv_ref):
        d_ref[...], nm_ref[...], nv_ref[...] = _adamw_math(w_ref[...], g_ref[...], m_ref[...], v_ref[...])

    blk = pl.BlockSpec((tr, C), lambda i: (i, 0))
    sh = jax.ShapeDtypeStruct((R, C), F32)
    return pl.pallas_call(
        body, name=name, grid=(R // tr,), in_specs=[blk, blk, blk, blk], out_specs=[blk, blk, blk],
        out_shape=[sh, sh, sh], compiler_params=pltpu.CompilerParams(dimension_semantics=("parallel",)),
    )(w, g, m, v)


def adamw_slots(w, slots, m, v, *, name, tr=256):
    unit = w.ndim == 3
    R, C = w.shape[0], w.shape[-1]
    if R % 16 == 0:
        tr = _pick(R, tr, 16)
    else:
        tr = 144

    def body(w_ref, s_ref, m_ref, v_ref, g_ref, d_ref, nm_ref, nv_ref):
        g = s_ref[0].astype(F32)
        for k in range(1, NDEV):
            g = g + s_ref[k].astype(F32)
        g_ref[...] = g
        d_ref[...], nm_ref[...], nv_ref[...] = _adamw_math(w_ref[...], g, m_ref[...], v_ref[...])

    blk = pl.BlockSpec((tr, None, C), lambda i: (i, 0, 0)) if unit else pl.BlockSpec((tr, C), lambda i: (i, 0))
    sh = jax.ShapeDtypeStruct(w.shape, F32)
    return pl.pallas_call(
        body, name=name, grid=(pl.cdiv(R, tr),), in_specs=[blk, pl.BlockSpec((NDEV, tr, C), lambda i: (0, i, 0)), blk, blk],
        out_specs=[blk, blk, blk, blk], out_shape=[sh, sh, sh, sh],
        compiler_params=pltpu.CompilerParams(dimension_semantics=("parallel",)),
    )(w, slots, m, v)


def kernel(x, c, ctx, c_ctx, w_ada, b_ada, norm1_g, w_in, q_norm_g, kv_norm_g, w_uq, w_ukv, conv_w, conv_b, w_attn_out, w_conv_out, w_o, norm2_g, w_up, ffn_conv_w, ffn_conv_b, w_down, final_g, loss_target, m_c_ctx, m_w_ada, m_b_ada, m_norm1_g, m_w_in, m_q_norm_g, m_kv_norm_g, m_w_uq, m_w_ukv, m_conv_w, m_conv_b, m_w_attn_out, m_w_conv_out, m_w_o, m_norm2_g, m_w_up, m_ffn_conv_w, m_ffn_conv_b, m_w_down, m_final_g, v_c_ctx, v_w_ada, v_b_ada, v_norm1_g, v_w_in, v_q_norm_g, v_kv_norm_g, v_w_uq, v_w_ukv, v_conv_w, v_conv_b, v_w_attn_out, v_w_conv_out, v_w_o, v_norm2_g, v_w_up, v_ffn_conv_w, v_ffn_conv_b, v_w_down, v_final_g):
    me = 4 * lax.axis_index("x") + 2 * lax.axis_index("y") + lax.axis_index("c")
    W = dict(c_ctx=c_ctx, w_ada=w_ada, b_ada=b_ada, norm1_g=norm1_g, w_in=w_in, q_norm_g=q_norm_g, kv_norm_g=kv_norm_g,
             w_uq=w_uq, w_ukv=w_ukv, conv_w=conv_w, conv_b=conv_b, w_attn_out=w_attn_out, w_conv_out=w_conv_out, w_o=w_o,
             norm2_g=norm2_g, w_up=w_up, ffn_conv_w=ffn_conv_w, ffn_conv_b=ffn_conv_b, w_down=w_down, final_g=final_g)
    M = dict(c_ctx=m_c_ctx, w_ada=m_w_ada, b_ada=m_b_ada, norm1_g=m_norm1_g, w_in=m_w_in, q_norm_g=m_q_norm_g,
             kv_norm_g=m_kv_norm_g, w_uq=m_w_uq, w_ukv=m_w_ukv, conv_w=m_conv_w, conv_b=m_conv_b, w_attn_out=m_w_attn_out,
             w_conv_out=m_w_conv_out, w_o=m_w_o, norm2_g=m_norm2_g, w_up=m_w_up, ffn_conv_w=m_ffn_conv_w,
             ffn_conv_b=m_ffn_conv_b, w_down=m_w_down, final_g=m_final_g)
    V = dict(c_ctx=v_c_ctx, w_ada=v_w_ada, b_ada=v_b_ada, norm1_g=v_norm1_g, w_in=v_w_in, q_norm_g=v_q_norm_g,
             kv_norm_g=v_kv_norm_g, w_uq=v_w_uq, w_ukv=v_w_ukv, conv_w=v_conv_w, conv_b=v_conv_b, w_attn_out=v_w_attn_out,
             w_conv_out=v_w_conv_out, w_o=v_w_o, norm2_g=v_norm2_g, w_up=v_w_up, ffn_conv_w=v_ffn_conv_w,
             ffn_conv_b=v_ffn_conv_b, w_down=v_w_down, final_g=v_final_g)
    names = list(W)
    transposed = ("w_up", "w_uq")
    as2d = lambda k, a: (a.reshape(1, -1) if a.ndim == 1 else
                         a[0].T if k in transposed else a.reshape(a.shape[-2], a.shape[-1]))
    W2 = {k: as2d(k, a) for k, a in W.items()}
    M2 = {k: as2d(k, a) for k, a in M.items()}
    V2 = {k: as2d(k, a) for k, a in V.items()}
    unit3 = lambda a: jnp.transpose(a, (2, 0, 1))
    W3, M3, V3 = unit3(W["w_in"]), unit3(M["w_in"]), unit3(V["w_in"])
    nsh = W2["w_ada"].shape[1]

    unit_mid = ("conv_w", "ffn_conv_w")
    mid3 = lambda a: jnp.transpose(a, (1, 0, 2))
    s_all, mod_lat, mod_ctx, ffn_w_full, conv_w_full = ada_fwd(
        c, W2["c_ctx"], mid3(W["ffn_conv_w"]), mid3(W["conv_w"]), W2["w_ada"], W["b_ada"].reshape(NDEV, 1, nsh), [],
        name="ada_fwd")

    stage_w = {"in": ["w_in"], "mid": ["w_uq", "w_ukv", "w_attn_out", "w_conv_out", "w_o"], "up": ["w_up"],
               "down": ["w_down"]}
    two_level = ("in", "mid")
    ag, tok = {}, mod_lat
    for st, nms in stage_w.items():
        ag[st] = exchange_start([W2[nm].astype(BF) for nm in nms], per_peer=False, dep=tok, name="ag_start_" + st,
                                peers=FIRST_HOP if st in two_level else ALL_PEERS)
        tok = ag[st]["token"]

    def get_w(stage, after):
        lands = exchange_wait(ag[stage], after, name="ag_wait_" + stage)
        if stage in two_level:
            lands = relay_wait(relay_start(lands, name="ag_relay_" + stage), name="ag_relay_wait_" + stage)
        g = dict(zip(stage_w[stage], lands))
        if stage == "in":
            return build_win(g["w_in"], name="build_win")
        if stage == "mid":
            wq2, wkv2 = build_wq_wkv(g["w_uq"], g["w_ukv"], name="build_wq_wkv")
            return (wq2, wkv2, unshard_cols(g["w_attn_out"], name="unshard_w_attn_out"),
                    unshard_cols(g["w_conv_out"], name="unshard_w_conv_out"), g["w_o"].reshape(D, D))
        if stage == "up":
            return g["w_up"].reshape(2 * DFF, D)
        return g["w_down"].reshape(DFF, D)

    stage_g = {"ffn": ["w_up", "w_down"], "mid": ["w_attn_out", "w_conv_out", "w_o"], "qkv": ["w_uq", "w_ukv"],
               "in": ["w_in"]}
    rs = {}

    def put_g(stage, g):
        if stage == "in":
            parts = [shard_win_grad(g["dwin"], g["dwin_c"], name="shard_win_grad")]
        elif stage == "mid":
            parts = [shard_cols(g["dwao"], name="shard_w_attn_out"), shard_cols(g["dwco"], name="shard_w_conv_out"),
                     g["dwo"].reshape(NDEV, D // NDEV, D)]
        elif stage == "qkv":
            parts = list(shard_wq_wkv_grad(g["dwq2"], g["dwkv2"], name="shard_wq_wkv_grad"))
        else:
            parts = [g["dwup"].reshape(NDEV, 2 * DFF // NDEV, D), g["dwdn"].reshape(NDEV, DFF // NDEV, D)]
        rs[stage] = exchange_start(parts, per_peer=True, name="rs_start_" + stage)
        return rs[stage]["token"]

    r = _local_step(x[0], ctx[0], loss_target[0], mod_lat, mod_ctx, W2["norm1_g"], W2["q_norm_g"], W2["kv_norm_g"],
                    W2["norm2_g"], W2["final_g"], conv_w_full, W2["conv_b"], ffn_w_full, W2["ffn_conv_b"], get_w, put_g,
                    ag["down"]["token"])

    G, DL, NM, NV = {}, {}, {}, {}

    def finish(stage, after):
        for nm, sl in zip(stage_g[stage], exchange_wait(rs[stage], after, name="rs_wait_" + stage)):
            wmv = (W3, M3, V3) if nm == "w_in" else (W2[nm], M2[nm], V2[nm])
            G[nm], DL[nm], NM[nm], NV[nm] = adamw_slots(wmv[0], sl, wmv[1], wmv[2], name="adamw_" + nm)
            after = DL[nm]
        return after

    sync = exchange_start([pack_small(r, name="pack_small")], per_peer=False, name="sync_start")
    after = sync["token"]
    for st in ("ffn", "mid", "in", "qkv"):
        after = finish(st, after)
    a_buf, = exchange_wait(sync, [DL[nm] for nms in stage_g.values() for nm in nms], name="sync_wait")
    ssum, g_vec = sum_slots(a_buf, name="sum_small")
    G.update(g_vec)
    loss = ssum[P_LOSS, 0]
    G["conv_w"] = lax.dynamic_slice(ssum[P_CW:P_CW + 3, :CONV], (0, me * (CONV // NDEV)), (3, CONV // NDEV))
    fw_full = ssum[P_FW:P_FW + 6 * FROWS].reshape(3, 2, FROWS * D)[:, :, :DFF].reshape(3, 2 * DFF)
    G["ffn_conv_w"] = lax.dynamic_slice(fw_full, (0, me * (2 * DFF // NDEV)), (3, 2 * DFF // NDEV))

    dml = lax.dynamic_slice(a_buf[:, P_DML:P_DML + 6, :].reshape(NDEV, 6 * D), (0, me * nsh), (NDEV, nsh))
    dmc = lax.dynamic_slice(ssum[P_DMC:P_DMC + 6].reshape(1, 6 * D), (0, me * nsh), (1, nsh))
    G["w_ada"], gcc = ada_bwd(s_all, dml, dmc, W2["w_ada"], W2["c_ctx"], name="ada_bwd")
    G["c_ctx"] = gcc[0:1]

    DL["w_ada"], NM["w_ada"], NV["w_ada"] = adamw(W2["w_ada"], G["w_ada"], M2["w_ada"], V2["w_ada"], name="adamw_w_ada")
    small = ["c_ctx", "b_ada", "norm1_g", "q_norm_g", "kv_norm_g", "conv_b", "norm2_g", "ffn_conv_b", "final_g", "conv_w",
             "ffn_conv_w"]
    view = lambda k, a3, a2: mid3(a3[k]) if k in unit_mid else a2[k]
    for k in unit_mid:
        G[k] = G[k].reshape(3, 1, -1)
    ds, nms, nvs = adamw_many([view(k, W, W2) for k in small], [G[k] for k in small],
                              [view(k, M, M2) for k in small], [view(k, V, V2) for k in small], name="adamw_small")
    for k, nm in enumerate(small):
        DL[nm], NM[nm], NV[nm] = ds[k], nms[k], nvs[k]

    def as_output(nm, a):
        if nm in transposed:
            return a.T[None]
        if nm == "w_in":
            return jnp.transpose(a, (1, 2, 0))
        if nm in unit_mid and a.ndim == 3:
            return jnp.transpose(a, (1, 0, 2))
        return a.reshape(W[nm].shape)

    outs = [loss, r["dx"][None]]
    for grp in (G, DL, NM, NV):
        outs += [as_output(nm, grp[nm]) for nm in names]
    return tuple(outs)
```

```python
import functools
import numpy as np
import jax
import jax.numpy as jnp
from jax import lax
from jax.experimental import pallas as pl
from jax.experimental.pallas import tpu as pltpu

F32 = jnp.float32
BF = jnp.bfloat16
MESH = pl.DeviceIdType.MESH

D = 1024
T = 2048
TC = 256
TKV = T + TC
GRID_W = 64
NH = 8
DN = 64
DR = 32
DV = 64
QL = 384
KVL = 256
CONV = 512
DFF = 2816
EPS = 1e-6
ROPE_THETA = 10000.0
SCALE = (DN + DR) ** -0.5
NDEV = 8
HP = 128

O_GA, O_GC, O_KV, O_Q, O_CV = 0, 1024, 2048, 2560, 3072
NIN = 4608
CVB = 256
N_IN = 4256
SH_IN = N_IN // NDEV

LR, B1, B2, AEPS, WD, STEP = 0.001, 0.9, 0.999, 1e-08, 0.01, 10


def _pick(n, target, mult=128):
    best = None
    for d in range(mult, min(n, target) + 1, mult):
        if n % d == 0:
            best = d
    return best if best is not None else n


def _swap_start(g):
    return 8 * (g ^ 1)


def mm(a, b, *, ta=False, tb=False, out_dtype=F32, name, tm=1024, tn=1024, tk=2048, M=None, N=None, K=None,
       a_off=(0, 0), b_off=(0, 0), a_stack=False, b_stack=False, o_stack=False, dep=None):
    def dims(arr, stack):
        return (arr.shape[1], 2 * arr.shape[2]) if stack else arr.shape

    ar, ac = dims(a, a_stack)
    br, bc = dims(b, b_stack)
    M = M or ((ac if ta else ar) - a_off[1 if ta else 0])
    K = K or ((ar if ta else ac) - a_off[0 if ta else 1])
    N = N or ((br if tb else bc) - b_off[0 if tb else 1])
    tm = _pick(M, tm, 128 if ta else 16)
    tn = _pick(N // 2 if (o_stack or (b_stack and not tb)) else N, tn, 128)
    tk = _pick(K // 2 if ((a_stack and not ta) or (b_stack and tb)) else K, tk, 128)
    nk = K // tk
    ca = 0 if ta else 1
    cb = 1 if tb else 0

    def body(a_ref, b_ref, *rest):
        o_ref, acc = rest[-2:]
        k = pl.program_id(2)
        part = lax.dot_general(a_ref[...].astype(BF), b_ref[...].astype(BF),
                               (((ca,), (cb,)), ((), ())), preferred_element_type=F32)
        if nk == 1:
            o_ref[...] = part.astype(o_ref.dtype)
        else:
            @pl.when(k == 0)
            def _():
                acc[...] = part

            @pl.when(k > 0)
            def _():
                acc[...] += part

            @pl.when(k == nk - 1)
            def _():
                o_ref[...] = acc[...].astype(o_ref.dtype)

    def spec(blk, rc, off, stack, ncols):
        assert off[0] % blk[0] == 0 and off[1] % blk[1] == 0, (name, blk, off)
        ro, co = off[0] // blk[0], off[1] // blk[1]
        if not stack:
            return pl.BlockSpec(blk, lambda i, j, k: (rc(i, j, k)[0] + ro, rc(i, j, k)[1] + co))
        nhb = ncols // 2 // blk[1]
        return pl.BlockSpec((None,) + blk,
                            lambda i, j, k: ((rc(i, j, k)[1] + co) // nhb, rc(i, j, k)[0] + ro, (rc(i, j, k)[1] + co) % nhb))

    a_spec = spec((tk, tm), lambda i, j, k: (k, i), a_off, a_stack, ac) if ta else \
        spec((tm, tk), lambda i, j, k: (i, k), a_off, a_stack, ac)
    b_spec = spec((tn, tk), lambda i, j, k: (j, k), b_off, b_stack, bc) if tb else \
        spec((tk, tn), lambda i, j, k: (k, j), b_off, b_stack, bc)
    o_spec = spec((tm, tn), lambda i, j, k: (i, j), (0, 0), o_stack, N)
    o_shape = (2, M, N // 2) if o_stack else (M, N)
    deps = [] if dep is None else [dep]
    return pl.pallas_call(
        body, name=name, grid=(M // tm, N // tn, nk),
        in_specs=[a_spec, b_spec] + [pl.BlockSpec(memory_space=pl.ANY)] * len(deps),
        out_specs=o_spec, out_shape=jax.ShapeDtypeStruct(o_shape, out_dtype),
        scratch_shapes=[pltpu.VMEM((tm, tn) if nk > 1 else (8, 128), F32)],
        compiler_params=pltpu.CompilerParams(dimension_semantics=("parallel", "parallel", "arbitrary")),
    )(a, b, *deps)


def _row(width):
    return pl.BlockSpec((1, width), lambda *_: (0, 0))


NLAT = T // TC


def normmod_cat(ctx, x, g, csc, csh, sc, sh, dep, *, name, tm=256):
    assert tm == TC

    def body(c_ref, x_ref, g_ref, csc_ref, csh_ref, sc_ref, sh_ref, dep_ref, h_ref):
        last = pl.program_id(0) == NLAT
        xv = jnp.where(last, c_ref[...], x_ref[...])
        scv = jnp.where(last, csc_ref[...], sc_ref[...])
        shv = jnp.where(last, csh_ref[...], sh_ref[...])
        r = lax.rsqrt(jnp.mean(xv * xv, axis=-1, keepdims=True) + EPS)
        h_ref[...] = ((xv * r * g_ref[...]) * (1.0 + scv) + shv).astype(BF)

    return pl.pallas_call(
        body, name=name, grid=(TKV // tm,),
        in_specs=[pl.BlockSpec((tm, D), lambda i: (0, 0)), pl.BlockSpec((tm, D), lambda i: (jnp.minimum(i, NLAT - 1), 0)),
                  _row(D), _row(D), _row(D), _row(D), _row(D), pl.BlockSpec(memory_space=pl.ANY)],
        out_specs=pl.BlockSpec((tm, D), lambda i: (i, 0)), out_shape=jax.ShapeDtypeStruct((TKV, D), BF),
        compiler_params=pltpu.CompilerParams(dimension_semantics=("parallel",)),
    )(ctx, x, g, csc, csh, sc, sh, dep)


def kvprep(pc, p, kvg, wkv2, ck, sk, *, name, tm=256):
    assert tm == TC
    nb = TKV // tm
    kvcol = O_KV // 512

    def body(pc_ref, p_ref, g_ref, w_ref, ck_ref, sk_ref, k_ref, v_ref, ckv_ref):
        i = pl.program_id(0)
        t = jnp.where(i == NLAT, pc_ref[...], p_ref[...])
        pk = t[:, :KVL]
        r = lax.rsqrt(jnp.mean(pk * pk, axis=-1, keepdims=True) + EPS)
        ckv = (pk * r * g_ref[...]).astype(BF)
        ckv_ref[...] = ckv
        kv2 = jnp.dot(ckv, w_ref[...], preferred_element_type=F32)
        krr = t[:, KVL:KVL + HP] * ck_ref[...] + t[:, KVL + HP:KVL + 2 * HP] * sk_ref[...]
        k_ref[...] = (kv2[:, :NH * HP] + jnp.concatenate([krr] * NH, axis=1)).astype(BF)
        v_ref[...] = kv2[:, NH * HP:].astype(BF)

    return pl.pallas_call(
        body, name=name, grid=(nb,),
        in_specs=[pl.BlockSpec((tm, 512), lambda i: (0, 0)),
                  pl.BlockSpec((tm, 512), lambda i: (jnp.minimum(i, NLAT - 1), kvcol)),
                  _row(KVL), pl.BlockSpec((KVL, NH * HP + NH * DV), lambda i: (0, 0)),
                  pl.BlockSpec((tm, HP), lambda i: (i, 0)), pl.BlockSpec((tm, HP), lambda i: (i, 0))],
        out_specs=[pl.BlockSpec((tm, NH * HP), lambda i: (i, 0)), pl.BlockSpec((tm, NH * DV), lambda i: (i, 0)),
                   pl.BlockSpec((tm, KVL), lambda i: (i, 0))],
        out_shape=[jax.ShapeDtypeStruct((TKV, NH * HP), BF), jax.ShapeDtypeStruct((TKV, NH * DV), BF),
                   jax.ShapeDtypeStruct((TKV, KVL), BF)],
        compiler_params=pltpu.CompilerParams(dimension_semantics=("parallel",)),
    )(pc, p, kvg, wkv2, ck, sk)


def qprep(p, qg, wq2, cq_t, sq_t, *, name, tm=256):
    qcol = O_Q // 512

    def body(p_ref, g_ref, w_ref, c_ref, s_ref, q_ref, cq_ref):
        pq = p_ref[...]
        r = lax.rsqrt(jnp.sum(pq * pq, axis=-1, keepdims=True) * (1.0 / QL) + EPS)
        cq = (pq * r * g_ref[...]).astype(BF)
        cq_ref[...] = cq
        q2 = jnp.dot(cq, w_ref[...], preferred_element_type=F32)
        cc = jnp.concatenate([c_ref[...]] * NH, axis=1)
        ss = jnp.concatenate([s_ref[...]] * NH, axis=1)
        q_ref[...] = (q2[:, :NH * HP] * cc + q2[:, NH * HP:] * ss).astype(BF)

    return pl.pallas_call(
        body, name=name, grid=(T // tm,),
        in_specs=[pl.BlockSpec((tm, 512), lambda i: (i, qcol)), _row(512),
                  pl.BlockSpec((512, 2 * NH * HP), lambda i: (0, 0)),
                  pl.BlockSpec((tm, HP), lambda i: (i, 0)), pl.BlockSpec((tm, HP), lambda i: (i, 0))],
        out_specs=[pl.BlockSpec((tm, NH * HP), lambda i: (i, 0)), pl.BlockSpec((tm, 512), lambda i: (i, 0))],
        out_shape=[jax.ShapeDtypeStruct((T, NH * HP), BF), jax.ShapeDtypeStruct((T, 512), BF)],
        compiler_params=pltpu.CompilerParams(dimension_semantics=("parallel",)),
    )(p, qg, wq2, cq_t, sq_t)


def _head_mask(h):
    lanes = lax.broadcasted_iota(jnp.int32, (1, 2 * DV), 1)
    return (lanes // DV) == (h % 2)


LOG2E = 1.4426950408889634


def attn_fwd(q, k, v, *, name, tq=1024, kc=768):
    def body(q_ref, k_ref, v_ref, o_ref, lse_ref):
        h = pl.program_id(1)
        qv = q_ref[...]
        m = l = acc = None
        for c in range(TKV // kc):
            s = lax.dot_general(qv, k_ref[c * kc:(c + 1) * kc, :], (((1,), (1,)), ((), ())),
                                preferred_element_type=F32) * (SCALE * LOG2E)
            mc = jnp.max(s, axis=-1, keepdims=True)
            if c == 0:
                m = mc
                e = jnp.exp2(s - m)
                l = jnp.sum(e, axis=-1, keepdims=True)
                acc = jnp.dot(e.astype(BF), v_ref[c * kc:(c + 1) * kc, :], preferred_element_type=F32)
            else:
                mn = jnp.maximum(m, mc)
                a = jnp.exp2(m - mn)
                e = jnp.exp2(s - mn)
                l = l * a + jnp.sum(e, axis=-1, keepdims=True)
                acc = acc * a + jnp.dot(e.astype(BF), v_ref[c * kc:(c + 1) * kc, :], preferred_element_type=F32)
                m = mn
        o2 = jnp.where(_head_mask(h), acc * (1.0 / l), 0.0).astype(BF)
        lse_ref[...] = jnp.broadcast_to(m + jnp.log(l) * LOG2E, (tq, HP))

        @pl.when(h % 2 == 0)
        def _():
            o_ref[...] = o2

        @pl.when(h % 2 == 1)
        def _():
            o_ref[...] = o_ref[...] + o2

    return pl.pallas_call(
        body, name=name, grid=(T // tq, NH),
        in_specs=[pl.BlockSpec((tq, HP), lambda i, h: (i, h)), pl.BlockSpec((TKV, HP), lambda i, h: (0, h)),
                  pl.BlockSpec((TKV, 2 * DV), lambda i, h: (0, h // 2))],
        out_specs=[pl.BlockSpec((tq, 2 * DV), lambda i, h: (i, h // 2)), pl.BlockSpec((tq, HP), lambda i, h: (i, h))],
        out_shape=[jax.ShapeDtypeStruct((T, NH * DV), BF), jax.ShapeDtypeStruct((T, NH * HP), F32)],
        compiler_params=pltpu.CompilerParams(dimension_semantics=("parallel", "arbitrary")),
    )(q, k, v)


def _shift_dn(x):
    n = x.shape[0]
    rows = lax.broadcasted_iota(jnp.int32, (n, 1), 0)
    return jnp.where(rows == 0, 0.0, pltpu.roll(x, 1, axis=0))


def _shift_up(x):
    n = x.shape[0]
    rows = lax.broadcasted_iota(jnp.int32, (n, 1), 0)
    return jnp.where(rows == n - 1, 0.0, pltpu.roll(x, n - 1, axis=0))


def _conv(x, w_ref, b_ref):
    return b_ref[...] + _shift_dn(x) * w_ref[0:1, :] + x * w_ref[1:2, :] + _shift_up(x) * w_ref[2:3, :]


def _conv_t(dy, w_ref):
    return _shift_up(dy) * w_ref[0:1, :] + dy * w_ref[1:2, :] + _shift_dn(dy) * w_ref[2:3, :]


def _conv_wgrad(dw_ref, dy, x):
    dw_ref[0:1, :] = jnp.sum(dy * _shift_dn(x), axis=0, keepdims=True)
    dw_ref[1:2, :] = jnp.sum(dy * x, axis=0, keepdims=True)
    dw_ref[2:3, :] = jnp.sum(dy * _shift_up(x), axis=0, keepdims=True)


def convz(p, cw, cb, *, name):
    o0 = O_CV // (3 * CVB)

    def body(p_ref, w_ref, bias_ref, z_ref):
        xv, bv, cv = p_ref[:, 0:CVB], p_ref[:, CVB:2 * CVB], p_ref[:, 2 * CVB:3 * CVB]
        z_ref[...] = (bv * _conv(cv * xv, w_ref, bias_ref)).astype(BF)

    return pl.pallas_call(
        body, name=name, grid=(CONV // CVB,),
        in_specs=[pl.BlockSpec((T, 3 * CVB), lambda j: (0, o0 + j)), pl.BlockSpec((3, CVB), lambda j: (0, j)),
                  pl.BlockSpec((1, CVB), lambda j: (0, j))],
        out_specs=pl.BlockSpec((T, CVB), lambda j: (0, j)),
        out_shape=jax.ShapeDtypeStruct((T, CONV), BF),
        compiler_params=pltpu.CompilerParams(dimension_semantics=("parallel",)),
    )(p, cw, cb)


def out_proj_merge(o, wao, z, wco, p, *, name, tm=512):
    kin = o.shape[1]

    def body(o_ref, wa_ref, z_ref, wc_ref, ga_ref, gc_ref, ya_ref, yc_ref, m_ref):
        ya = jnp.dot(o_ref[...], wa_ref[...], preferred_element_type=F32)
        yc = jnp.dot(z_ref[...], wc_ref[...], preferred_element_type=F32)
        ya_ref[...] = ya
        yc_ref[...] = yc
        m_ref[...] = (jax.nn.sigmoid(ga_ref[...]) * ya + jax.nn.sigmoid(gc_ref[...]) * yc).astype(BF)

    blk = pl.BlockSpec((tm, D), lambda i: (i, 0))
    act = pl.BlockSpec((tm, kin), lambda i: (i, 0))
    wsp = pl.BlockSpec((kin, D), lambda i: (0, 0))
    sh = jax.ShapeDtypeStruct((T, D), F32)
    return pl.pallas_call(
        body, name=name, grid=(T // tm,),
        in_specs=[act, wsp, act, wsp, pl.BlockSpec((tm, D), lambda i: (i, O_GA // D)),
                  pl.BlockSpec((tm, D), lambda i: (i, O_GC // D))],
        out_specs=[blk, blk, blk], out_shape=[sh, sh, jax.ShapeDtypeStruct((T, D), BF)],
        compiler_params=pltpu.CompilerParams(dimension_semantics=("parallel",)),
    )(o, wao, z, wco, p, p)


CONV_HALO = 8
CONV_ROWS = 256


def _row_chunks(n, chunk, carry):
    carry = chunk(0, True, False, carry)
    carry = lax.fori_loop(1, n // CONV_ROWS - 1, lambda c, a: chunk(c * CONV_ROWS, False, False, a), carry)
    return chunk(n - CONV_ROWS, False, True, carry)


def _ext_rows(ref, r0, first, last):
    n, w = ref.shape
    zero = jnp.zeros((CONV_HALO, w), ref.dtype)
    if first:
        return jnp.concatenate([zero, ref[0:CONV_ROWS + CONV_HALO, :]], axis=0)
    if last:
        return jnp.concatenate([ref[n - CONV_ROWS - CONV_HALO:n, :], zero], axis=0)
    return ref[pl.ds(pl.multiple_of(r0 - CONV_HALO, 8), CONV_ROWS + 2 * CONV_HALO), :]


def _center_rows(r0, first, last):
    return slice(r0, r0 + CONV_ROWS) if (first or last) else pl.ds(pl.multiple_of(r0, 8), CONV_ROWS)


def _roll_dn(x):
    return pltpu.roll(x, 1, axis=0)


def _roll_up(x):
    return pltpu.roll(x, x.shape[0] - 1, axis=0)


_CTR = slice(CONV_HALO, CONV_HALO + CONV_ROWS)


def ffn_act(u0, cw, cb, *, name, tc=256):
    nb = DFF // tc

    def body(u_ref, wg_ref, wv_ref, bg_ref, bv_ref, f_ref):
        wg = [wg_ref[k:k + 1, :] for k in range(3)]
        wv = [wv_ref[k:k + 1, :] for k in range(3)]
        bg, bv = bg_ref[...], bv_ref[...]

        def chunk(r0, first, last, carry):
            xg, xv = _ext_rows(u_ref.at[0], r0, first, last), _ext_rows(u_ref.at[1], r0, first, last)
            ug = bg + _roll_dn(xg) * wg[0] + xg * wg[1] + _roll_up(xg) * wg[2]
            uv = bv + _roll_dn(xv) * wv[0] + xv * wv[1] + _roll_up(xv) * wv[2]
            f_ref[_center_rows(r0, first, last), :] = (ug * jax.nn.sigmoid(ug) * uv)[_CTR].astype(BF)
            return carry

        _row_chunks(T, chunk, 0)

    return pl.pallas_call(
        body, name=name, grid=(nb,),
        in_specs=[pl.BlockSpec((2, T, tc), lambda j: (0, 0, j)),
                  pl.BlockSpec((3, tc), lambda j: (0, j)), pl.BlockSpec((3, tc), lambda j: (0, nb + j)),
                  pl.BlockSpec((1, tc), lambda j: (0, j)), pl.BlockSpec((1, tc), lambda j: (0, nb + j))],
        out_specs=pl.BlockSpec((T, tc), lambda j: (0, j)),
        out_shape=jax.ShapeDtypeStruct((T, DFF), BF),
        compiler_params=pltpu.CompilerParams(dimension_semantics=("parallel",)),
    )(u0, cw, cw, cb, cb)


def rows_call(lead, ins, in_specs, out_shape, out_specs, fn, *, name, R, tm):
    tb, a_stack, tk = lead.get("tb", False), lead.get("a_stack", False), lead["tk"]
    K = 2 * lead["a"].shape[2] if a_stack else lead["a"].shape[1]
    nk = K // tk
    deps = [] if lead.get("dep") is None else [lead["dep"]]
    n_in = len(ins)

    def body(a_ref, b_ref, *refs):
        refs = refs[len(deps):]
        in_refs, out_refs, acc = refs[:n_in], refs[n_in:-1], refs[-1]
        i, k = pl.program_id(0), pl.program_id(1)
        part = lax.dot_general(a_ref[...].astype(BF), b_ref[...].astype(BF),
                               (((1,), (1 if tb else 0,)), ((), ())), preferred_element_type=F32)
        if nk == 1:
            fn(i, part, in_refs, out_refs)
            return

        @pl.when(k == 0)
        def _():
            acc[...] = part

        @pl.when(k > 0)
        def _():
            acc[...] += part

        @pl.when(k == nk - 1)
        def _():
            fn(i, acc[...], in_refs, out_refs)

    if a_stack:
        nhb = K // 2 // tk
        a_spec = pl.BlockSpec((None, tm, tk), lambda i, k: (k // nhb, i, k % nhb))
    else:
        a_spec = pl.BlockSpec((tm, tk), lambda i, k: (i, k))
    b_spec = pl.BlockSpec((D, tk), lambda i, k: (0, k)) if tb else pl.BlockSpec((tk, D), lambda i, k: (k, 0))
    return pl.pallas_call(
        body, name=name, grid=(R // tm, nk),
        in_specs=[a_spec, b_spec] + [pl.BlockSpec(memory_space=pl.ANY)] * len(deps) + list(in_specs),
        out_specs=out_specs, out_shape=out_shape,
        scratch_shapes=[pltpu.VMEM((tm, D) if nk > 1 else (8, 128), F32)],
        compiler_params=pltpu.CompilerParams(dimension_semantics=("arbitrary", "arbitrary")),
    )(lead["a"], lead["b"], *deps, *ins)


def _rblk(tm, w=D, col=0):
    return pl.BlockSpec((tm, w), lambda i, k: (i, col))


def _rrow(w=D):
    return pl.BlockSpec((1, w), lambda i, k: (0, 0))


def down_final(f, wdn, x1, g2, fg, tgt, *, name, tm=512):
    def fn(i, d, in_refs, out_refs):
        x1_ref, g2_ref, fg_ref, t_ref = in_refs
        d_ref, dx_ref, dd_ref, dfg_ref, loss_ref = out_refs
        d_ref[...] = d
        xv = x1_ref[...] + g2_ref[...] * d
        r = lax.rsqrt(jnp.mean(xv * xv, axis=-1, keepdims=True) + EPS)
        xh = xv * r
        diff = xh * fg_ref[...] - t_ref[...]
        part = 0.5 * jnp.sum(jnp.mean(diff * diff, axis=-1, keepdims=True), axis=0, keepdims=True)
        dy = diff * (1.0 / D)
        a = dy * fg_ref[...]
        dx = r * (a - xh * jnp.mean(a * xh, axis=-1, keepdims=True))
        dx_ref[...] = dx
        dd_ref[...] = (dx * g2_ref[...]).astype(BF)
        dfg = jnp.sum(dy * xh, axis=0, keepdims=True)

        @pl.when(i == 0)
        def _():
            dfg_ref[...] = dfg
            loss_ref[...] = jnp.broadcast_to(part, (1, 128))

        @pl.when(i > 0)
        def _():
            dfg_ref[...] += dfg
            loss_ref[...] += jnp.broadcast_to(part, (1, 128))

    blk = _rblk(tm)
    return rows_call(
        dict(a=f, b=wdn, tk=DFF), [x1, g2, fg, tgt], [blk, _rrow(), _rrow(), blk],
        [jax.ShapeDtypeStruct((T, D), F32), jax.ShapeDtypeStruct((T, D), F32), jax.ShapeDtypeStruct((T, D), BF),
         jax.ShapeDtypeStruct((1, D), F32), jax.ShapeDtypeStruct((1, 128), F32)],
        [blk, blk, blk, _rrow(), _rrow(128)], fn, name=name, R=T, tm=tm)


def oproj_resid(merged, wo, x, gate, g, sc, sh, *, name, tm=512):
    def fn(i, a, in_refs, out_refs):
        x_ref, gate_ref, g_ref, sc_ref, sh_ref = in_refs
        a_ref, x1_ref, h_ref = out_refs
        a_ref[...] = a
        xv = x_ref[...] + gate_ref[...] * a
        x1_ref[...] = xv
        r = lax.rsqrt(jnp.mean(xv * xv, axis=-1, keepdims=True) + EPS)
        h_ref[...] = ((xv * r * g_ref[...]) * (1.0 + sc_ref[...]) + sh_ref[...]).astype(BF)

    blk = _rblk(tm)
    return rows_call(
        dict(a=merged, b=wo, tk=D), [x, gate, g, sc, sh], [blk, _rrow(), _rrow(), _rrow(), _rrow()],
        [jax.ShapeDtypeStruct((T, D), F32), jax.ShapeDtypeStruct((T, D), F32), jax.ShapeDtypeStruct((T, D), BF)],
        [blk, blk, blk], fn, name=name, R=T, tm=tm)


def oproj_dx_gate_bwd(da, wo, p, ya, yc, wao, wco, *, name, tm=512):
    kin = wao.shape[0]

    def fn(i, dm, in_refs, out_refs):
        ga_ref, gc_ref, ya_ref, yc_ref, wa_ref, wc_ref = in_refs
        dya_ref, dyc_ref, dp_ref, do_ref, dz_ref = out_refs
        sa, sc_ = jax.nn.sigmoid(ga_ref[...]), jax.nn.sigmoid(gc_ref[...])
        dya, dyc = (dm * sa).astype(BF), (dm * sc_).astype(BF)
        dya_ref[...] = dya
        dyc_ref[...] = dyc
        dp_ref[:, 0:D] = (dm * ya_ref[...] * (sa * (1.0 - sa))).astype(BF)
        dp_ref[:, D:2 * D] = (dm * yc_ref[...] * (sc_ * (1.0 - sc_))).astype(BF)
        nt = (((1,), (1,)), ((), ()))
        do_ref[...] = lax.dot_general(dya, wa_ref[...], nt, preferred_element_type=F32).astype(BF)
        dz_ref[...] = lax.dot_general(dyc, wc_ref[...], nt, preferred_element_type=F32)

    blk = _rblk(tm)
    sh = jax.ShapeDtypeStruct((T, D), BF)
    wsp = pl.BlockSpec((kin, D), lambda i, k: (0, 0))
    return rows_call(
        dict(a=da, b=wo, tb=True, tk=D), [p, p, ya, yc, wao, wco],
        [_rblk(tm, D, O_GA // D), _rblk(tm, D, O_GC // D), blk, blk, wsp, wsp],
        [sh, sh, jax.ShapeDtypeStruct((T, NIN), BF), jax.ShapeDtypeStruct((T, kin), BF), jax.ShapeDtypeStruct((T, kin), F32)],
        [blk, blk, _rblk(tm, 2 * D), _rblk(tm, kin), _rblk(tm, kin)], fn, name=name, R=T, tm=tm)


def normmod_bwd(x, dh, g, sc, dres, gsrc, gate, *, name, tm=512):
    R = x.shape[0]
    tm = min(tm, R)
    has_res = dres is not None
    fused = isinstance(dh, dict)
    if fused:
        tb, a_stack, tk = dh.get("tb", False), dh.get("a_stack", False), dh["tk"]
        K = 2 * dh["a"].shape[2] if a_stack else dh["a"].shape[1]
        nk = K // tk
        deps = [] if dh.get("dep") is None else [dh["dep"]]
        n_dh = 2 + len(deps)
    else:
        nk, n_dh = 1, 1

    def elementwise(i, dhv, x_ref, g_ref, sc_ref, res_refs, out_refs):
        xv = x_ref[...]
        r = lax.rsqrt(jnp.mean(xv * xv, axis=-1, keepdims=True) + EPS)
        xh = xv * r
        n = xh * g_ref[...]
        dn = dhv * (1.0 + sc_ref[...])
        a = dn * g_ref[...]
        rows = [jnp.sum(dhv, axis=0, keepdims=True), jnp.sum(dhv * n, axis=0, keepdims=True),
                jnp.sum(dn * xh, axis=0, keepdims=True)]
        if has_res:
            dres_ref, gsrc_ref, gate_ref = res_refs
            dx_ref, dxg_ref, st_ref = out_refs
            dr = dres_ref[...]
            dx = dr + r * (a - xh * jnp.mean(a * xh, axis=-1, keepdims=True))
            dx_ref[...] = dx
            dxg_ref[...] = (dx * gate_ref[...]).astype(BF)
            rows.append(jnp.sum(dr * gsrc_ref[...], axis=0, keepdims=True))
        else:
            st_ref, = out_refs
            rows.append(jnp.zeros((1, D), F32))

        @pl.when(i == 0)
        def _():
            for k, row in enumerate(rows):
                st_ref[k:k + 1, :] = row

        @pl.when(i > 0)
        def _():
            for k, row in enumerate(rows):
                st_ref[k:k + 1, :] += row

    def body(*refs):
        x_ref, dh_refs, g_ref, sc_ref = refs[0], refs[1:1 + n_dh], refs[1 + n_dh], refs[2 + n_dh]
        rest = refs[3 + n_dh:]
        res_refs, rest = (rest[:3], rest[3:]) if has_res else ((), rest)
        out_refs = rest[:3] if has_res else rest[:1]
        i = pl.program_id(0)
        if not fused:
            elementwise(i, dh_refs[0][...], x_ref, g_ref, sc_ref, res_refs, out_refs)
            return
        acc = rest[-1]
        k = pl.program_id(1)
        part = lax.dot_general(dh_refs[0][...].astype(BF), dh_refs[1][...].astype(BF),
                               (((1,), (1 if tb else 0,)), ((), ())), preferred_element_type=F32)
        if nk == 1:
            elementwise(i, part, x_ref, g_ref, sc_ref, res_refs, out_refs)
            return

        @pl.when(k == 0)
        def _():
            acc[...] = part

        @pl.when(k > 0)
        def _():
            acc[...] += part

        @pl.when(k == nk - 1)
        def _():
            elementwise(i, acc[...], x_ref, g_ref, sc_ref, res_refs, out_refs)

    rowb = lambda w: pl.BlockSpec((1, w), lambda i, *k: (0, 0))
    blk = pl.BlockSpec((tm, D), lambda i, *k: (i, 0))
    st_spec = pl.BlockSpec((4, D), lambda i, *k: (0, 0))
    st_shape = jax.ShapeDtypeStruct((4, D), F32)
    if fused:
        if a_stack:
            nhb = K // 2 // tk
            a_spec = pl.BlockSpec((None, tm, tk), lambda i, k: (k // nhb, i, k % nhb))
        else:
            a_spec = pl.BlockSpec((tm, tk), lambda i, k: (i, k))
        b_spec = pl.BlockSpec((D, tk), lambda i, k: (0, k)) if tb else pl.BlockSpec((tk, D), lambda i, k: (k, 0))
        dh_specs = [a_spec, b_spec] + [pl.BlockSpec(memory_space=pl.ANY)] * len(deps)
        dh_args = [dh["a"], dh["b"]] + deps
        grid, sem = (R // tm, nk), ("arbitrary", "arbitrary")
        scratch = [pltpu.VMEM((tm, D) if nk > 1 else (8, 128), F32)]
    else:
        dh_specs, dh_args, grid, sem, scratch = [blk], [dh], (R // tm,), ("arbitrary",), []
    cp = pltpu.CompilerParams(dimension_semantics=sem)
    if has_res:
        return pl.pallas_call(
            body, name=name, grid=grid, in_specs=[blk] + dh_specs + [rowb(D), rowb(D), blk, blk, rowb(D)],
            out_specs=[blk, blk, st_spec], scratch_shapes=scratch,
            out_shape=[jax.ShapeDtypeStruct((R, D), F32), jax.ShapeDtypeStruct((R, D), BF), st_shape],
            compiler_params=cp,
        )(x, *dh_args, g, sc, dres, gsrc, gate)
    return pl.pallas_call(
        body, name=name, grid=grid, in_specs=[blk] + dh_specs + [rowb(D), rowb(D)],
        out_specs=st_spec, out_shape=st_shape, scratch_shapes=scratch, compiler_params=cp,
    )(x, *dh_args, g, sc)


def ffn_act_bwd(u0, df, cw, cb, *, name, tc=128):
    nb = DFF // tc

    def body(u_ref, df_ref, wg_ref, wv_ref, bg_ref, bv_ref, du_ref, dw_ref, db_ref):
        wg = [wg_ref[k:k + 1, :] for k in range(3)]
        wv = [wv_ref[k:k + 1, :] for k in range(3)]
        bg, bv = bg_ref[...], bv_ref[...]

        def chunk(r0, first, last, acc):
            xg, xv = _ext_rows(u_ref.at[0], r0, first, last), _ext_rows(u_ref.at[1], r0, first, last)
            dfe = _ext_rows(df_ref, r0, first, last)
            xg_d, xg_u, xv_d, xv_u = _roll_dn(xg), _roll_up(xg), _roll_dn(xv), _roll_up(xv)
            ug = bg + xg_d * wg[0] + xg * wg[1] + xg_u * wg[2]
            uv = bv + xv_d * wv[0] + xv * wv[1] + xv_u * wv[2]
            sig = jax.nn.sigmoid(ug)
            dug = dfe * uv * (sig * (1.0 + ug * (1.0 - sig)))
            duv = dfe * (ug * sig)
            rows = _center_rows(r0, first, last)
            du_ref[0, rows, :] = (_roll_up(dug) * wg[0] + dug * wg[1] + _roll_dn(dug) * wg[2])[_CTR].astype(BF)
            du_ref[1, rows, :] = (_roll_up(duv) * wv[0] + duv * wv[1] + _roll_dn(duv) * wv[2])[_CTR].astype(BF)
            terms = [dug * xg_d, dug * xg, dug * xg_u, dug, duv * xv_d, duv * xv, duv * xv_u, duv]
            return tuple(a + jnp.sum(t[_CTR], axis=0, keepdims=True) for a, t in zip(acc, terms))

        acc = _row_chunks(T, chunk, tuple(jnp.zeros((1, tc), F32) for _ in range(8)))
        for k in range(3):
            dw_ref[0, k:k + 1, :] = acc[k]
            dw_ref[1, k:k + 1, :] = acc[4 + k]
        db_ref[0] = acc[3]
        db_ref[1] = acc[7]

    lo = lambda r: pl.BlockSpec((r, tc), lambda j: (0, j))
    hi = lambda r: pl.BlockSpec((r, tc), lambda j: (0, nb + j))
    st = lambda r: pl.BlockSpec((2, r, tc), lambda j: (0, 0, j))
    return pl.pallas_call(
        body, name=name, grid=(nb,),
        in_specs=[st(T), lo(T), lo(3), hi(3), lo(1), hi(1)],
        out_specs=[st(T), st(3), st(1)],
        out_shape=[jax.ShapeDtypeStruct((2, T, DFF), BF), jax.ShapeDtypeStruct((2, 3, DFF), F32),
                   jax.ShapeDtypeStruct((2, 1, DFF), F32)],
        compiler_params=pltpu.CompilerParams(dimension_semantics=("parallel",)),
    )(u0, df, cw, cw, cb, cb)


def convz_bwd(p, dz, cw, cb, dp, *, name):
    o0 = O_CV // (3 * CVB)

    def body(p_ref, dz_ref, w_ref, bias_ref, dp_in, dp_ref, dw_ref, dbias_ref):
        xv, bv, cv = p_ref[:, 0:CVB], p_ref[:, CVB:2 * CVB], p_ref[:, 2 * CVB:3 * CVB]
        ci = cv * xv
        dwc = _conv(ci, w_ref, bias_ref)
        dzv = dz_ref[...]
        ddw = dzv * bv
        dci = _conv_t(ddw, w_ref)
        dp_ref[:, 0:CVB] = (dci * cv).astype(BF)
        dp_ref[:, CVB:2 * CVB] = (dzv * dwc).astype(BF)
        dp_ref[:, 2 * CVB:3 * CVB] = (dci * xv).astype(BF)
        _conv_wgrad(dw_ref, ddw, ci)
        dbias_ref[...] = jnp.sum(ddw, axis=0, keepdims=True)

    own = lambda r: pl.BlockSpec((r, CVB), lambda j: (0, j))
    return pl.pallas_call(
        body, name=name, grid=(CONV // CVB,),
        in_specs=[pl.BlockSpec((T, 3 * CVB), lambda j: (0, o0 + j)), own(T), own(3), own(1),
                  pl.BlockSpec(memory_space=pl.ANY)],
        out_specs=[pl.BlockSpec((T, 3 * CVB), lambda j: (0, o0 + j)), own(3), own(1)],
        out_shape=[jax.ShapeDtypeStruct((T, NIN), BF), jax.ShapeDtypeStruct((3, CONV), F32),
                   jax.ShapeDtypeStruct((1, CONV), F32)],
        input_output_aliases={4: 0},
        compiler_params=pltpu.CompilerParams(dimension_semantics=("parallel",)),
    )(p, dz, cw, cb, dp)


def attn_bwd(q, k, v, do, o, lse, dep, *, name, tq=1024, kc=768):
    NKC, KC = TKV // kc, kc
    deps = [] if dep is None else [dep]

    def body(q_ref, k_ref, v_ref, do_ref, o_ref, lse_ref, *rest):
        dq_ref, dk_ref, dv_ref = rest[len(deps):]
        h, i = pl.program_id(0), pl.program_id(1)

        @pl.when(i == 0)
        def _():
            dk_ref[...] = jnp.zeros_like(dk_ref)

        @pl.when((i == 0) & (h % 2 == 0))
        def _():
            dv_ref[...] = jnp.zeros_like(dv_ref)

        qv = q_ref[...]
        dom = jnp.where(_head_mask(h), do_ref[...], jnp.zeros_like(do_ref[...]))
        delta = jnp.sum(dom.astype(F32) * o_ref[...].astype(F32), axis=-1, keepdims=True)
        lse = lse_ref[:, 0:1]
        dq = jnp.zeros((tq, HP), F32)
        for c in range(NKC):
            cols = slice(c * KC, (c + 1) * KC)
            s = lax.dot_general(qv, k_ref[cols, :], (((1,), (1,)), ((), ())),
                                preferred_element_type=F32) * (SCALE * LOG2E)
            pr = jnp.exp2(s - lse)
            dp = lax.dot_general(dom, v_ref[cols, :], (((1,), (1,)), ((), ())), preferred_element_type=F32)
            ds = (pr * (dp - delta) * SCALE).astype(BF)
            dq = dq + jnp.dot(ds, k_ref[cols, :], preferred_element_type=F32)
            dk_ref[cols, :] += lax.dot_general(ds, qv, (((0,), (0,)), ((), ())), preferred_element_type=F32)
            dv_ref[cols, :] += lax.dot_general(pr.astype(BF), dom, (((0,), (0,)), ((), ())), preferred_element_type=F32)
        dq_ref[...] = dq

    return pl.pallas_call(
        body, name=name, grid=(NH, T // tq),
        in_specs=[pl.BlockSpec((tq, HP), lambda h, i: (i, h)), pl.BlockSpec((TKV, HP), lambda h, i: (0, h)),
                  pl.BlockSpec((TKV, 2 * DV), lambda h, i: (0, h // 2)), pl.BlockSpec((tq, 2 * DV), lambda h, i: (i, h // 2)),
                  pl.BlockSpec((tq, 2 * DV), lambda h, i: (i, h // 2)), pl.BlockSpec((tq, HP), lambda h, i: (i, h)),
                  *([pl.BlockSpec(memory_space=pl.ANY)] * len(deps))],
        out_specs=[pl.BlockSpec((tq, HP), lambda h, i: (i, h)), pl.BlockSpec((TKV, HP), lambda h, i: (0, h)),
                   pl.BlockSpec((TKV, 2 * DV), lambda h, i: (0, h // 2))],
        out_shape=[jax.ShapeDtypeStruct((T, NH * HP), F32), jax.ShapeDtypeStruct((TKV, NH * HP), F32),
                   jax.ShapeDtypeStruct((TKV, NH * DV), F32)],
        compiler_params=pltpu.CompilerParams(dimension_semantics=("arbitrary", "arbitrary")),
    )(q, k, v, do, o, lse, *deps)


def qprep_bwd(p, dq, qg, wq2, cq_t, sq_t, dp, *, name, tm=256):
    qcol = O_Q // 512

    def body(p_ref, dq_ref, g_ref, w_ref, c_ref, s_ref, dp_in, dp_ref, dq2_ref, dg_ref):
        i = pl.program_id(0)
        dqv = dq_ref[...]
        cc = jnp.concatenate([c_ref[...]] * NH, axis=1)
        ss = jnp.concatenate([s_ref[...]] * NH, axis=1)
        dq2 = jnp.concatenate([dqv * cc, dqv * ss], axis=1).astype(BF)
        dq2_ref[...] = dq2
        dcq = lax.dot_general(dq2, w_ref[...], (((1,), (1,)), ((), ())), preferred_element_type=F32)
        pq = p_ref[...]
        r = lax.rsqrt(jnp.sum(pq * pq, axis=-1, keepdims=True) * (1.0 / QL) + EPS)
        xh = pq * r
        a = dcq * g_ref[...]
        dp_ref[...] = (r * (a - xh * (jnp.sum(a * xh, axis=-1, keepdims=True) * (1.0 / QL)))).astype(BF)
        dg = jnp.sum(dcq * xh, axis=0, keepdims=True)

        @pl.when(i == 0)
        def _():
            dg_ref[...] = dg

        @pl.when(i > 0)
        def _():
            dg_ref[...] += dg

    return pl.pallas_call(
        body, name=name, grid=(T // tm,),
        in_specs=[pl.BlockSpec((tm, 512), lambda i: (i, qcol)), pl.BlockSpec((tm, NH * HP), lambda i: (i, 0)), _row(512),
                  pl.BlockSpec((512, 2 * NH * HP), lambda i: (0, 0)),
                  pl.BlockSpec((tm, HP), lambda i: (i, 0)), pl.BlockSpec((tm, HP), lambda i: (i, 0)),
                  pl.BlockSpec(memory_space=pl.ANY)],
        out_specs=[pl.BlockSpec((tm, 512), lambda i: (i, qcol)), pl.BlockSpec((tm, 2 * NH * HP), lambda i: (i, 0)), _row(512)],
        out_shape=[jax.ShapeDtypeStruct((T, NIN), BF), jax.ShapeDtypeStruct((T, 2 * NH * HP), BF),
                   jax.ShapeDtypeStruct((1, 512), F32)],
        input_output_aliases={6: 0},
        compiler_params=pltpu.CompilerParams(dimension_semantics=("arbitrary",)),
    )(p, dq, qg, wq2, cq_t, sq_t, dp)


def kvprep_bwd(pc, p, dk, dv, kvg, wkv2, ck, sk, dp, *, name, tm=256):
    assert tm == TC
    nb = TKV // tm
    kvcol = O_KV // 512

    def body(pc_ref, p_ref, dk_ref, dv_ref, g_ref, w_ref, ck_ref, sk_ref, dp_in, dp_ref, dpc_ref, dkv2_ref, dg_ref):
        i = pl.program_id(0)
        t = jnp.where(i == NLAT, pc_ref[...], p_ref[...])
        pk = t[:, :KVL]
        r = lax.rsqrt(jnp.mean(pk * pk, axis=-1, keepdims=True) + EPS)
        xh = pk * r
        dkv = dk_ref[...]
        dkv2 = jnp.concatenate([dkv, dv_ref[...]], axis=1).astype(BF)
        dkv2_ref[...] = dkv2
        dckv = lax.dot_general(dkv2, w_ref[...], (((1,), (1,)), ((), ())), preferred_element_type=F32)
        a = dckv * g_ref[...]
        dpk = r * (a - xh * jnp.mean(a * xh, axis=-1, keepdims=True))
        dkr = dkv[:, 0:HP]
        for hh in range(1, NH):
            dkr = dkr + dkv[:, hh * HP:(hh + 1) * HP]
        res = jnp.concatenate([dpk, dkr * ck_ref[...], dkr * sk_ref[...]], axis=1).astype(BF)
        dg = jnp.sum(dckv * xh, axis=0, keepdims=True)

        @pl.when(i == 0)
        def _():
            dg_ref[...] = dg

        @pl.when(i > 0)
        def _():
            dg_ref[...] += dg

        @pl.when(i < NLAT)
        def _():
            dp_ref[...] = res

        @pl.when(i == NLAT)
        def _():
            dpc_ref[...] = res

    rb = lambda w: pl.BlockSpec((tm, w), lambda i: (i, 0))
    return pl.pallas_call(
        body, name=name, grid=(nb,),
        in_specs=[pl.BlockSpec((tm, 512), lambda i: (0, 0)),
                  pl.BlockSpec((tm, 512), lambda i: (jnp.minimum(i, NLAT - 1), kvcol)),
                  rb(NH * HP), rb(NH * DV), _row(KVL), pl.BlockSpec((KVL, NH * HP + NH * DV), lambda i: (0, 0)),
                  rb(HP), rb(HP), pl.BlockSpec(memory_space=pl.ANY)],
        out_specs=[pl.BlockSpec((tm, 512), lambda i: (jnp.minimum(i, NLAT - 1), kvcol)),
                   pl.BlockSpec((tm, 512), lambda i: (0, 0)), rb(NH * HP + NH * DV), _row(KVL)],
        out_shape=[jax.ShapeDtypeStruct((T, NIN), BF), jax.ShapeDtypeStruct((TC, 512), BF),
                   jax.ShapeDtypeStruct((TKV, NH * HP + NH * DV), BF), jax.ShapeDtypeStruct((1, KVL), F32)],
        input_output_aliases={8: 0},
        compiler_params=pltpu.CompilerParams(dimension_semantics=("arbitrary",)),
    )(pc, p, dk, dv, kvg, wkv2, ck, sk, dp)


def _pieces(src, width, n):
    out, c = [], src
    while c < src + width:
        k = c // n
        w = min(src + width, (k + 1) * n) - c
        out.append((k, c - k * n, c - src, w))
        c += w
    return out


def _win_moves():
    mv = [(2208, 1024, O_GA), (3232, 1024, O_GC), (0, KVL, O_KV), (256, DR, O_KV + KVL + DN), (288, QL, O_Q)]
    mv += [(256 + _swap_start(g), 8, O_KV + KVL + HP + DN + 8 * g) for g in range(4)]
    for j in range(CONV // CVB):
        base = O_CV + 3 * CVB * j
        mv += [(672 + CVB * j, CVB, base), (1184 + CVB * j, CVB, base + CVB), (1696 + CVB * j, CVB, base + 2 * CVB)]
    return mv


_WIN_ZERO = [(O_KV + KVL, DN), (O_KV + KVL + DN + DR, HP - DN - DR), (O_KV + KVL + HP, DN),
             (O_KV + KVL + HP + DN + DR, HP - DN - DR), (O_Q + QL, 512 - QL)]


def build_win(g, *, name, tm=256):
    def body(g_ref, o_ref):
        for src, w, dst in _win_moves():
            for k, a, off, pw in _pieces(src, w, SH_IN):
                o_ref[:, dst + off:dst + off + pw] = g_ref[k, :, a:a + pw]
        for c0, w in _WIN_ZERO:
            o_ref[:, c0:c0 + w] = jnp.zeros((tm, w), o_ref.dtype)

    return pl.pallas_call(
        body, name=name, grid=(D // tm,), in_specs=[pl.BlockSpec((NDEV, tm, SH_IN), lambda i: (0, i, 0))],
        out_specs=pl.BlockSpec((tm, NIN), lambda i: (i, 0)), out_shape=jax.ShapeDtypeStruct((D, NIN), g.dtype),
        compiler_params=pltpu.CompilerParams(dimension_semantics=("parallel",)),
    )(g)


def shard_win_grad(dwt, dwct, *, name, col0=0, tc=256):
    n = dwt.shape[1]

    def body(dw_ref, dwc_ref, o_ref, kvs):
        kvs[...] = dw_ref[O_KV:O_KV + 512, :] + dwc_ref[...]

        def src(row, w):
            if O_KV <= row < O_KV + 512:
                return kvs[row - O_KV:row - O_KV + w, :]
            return dw_ref[row:row + w, :]

        for s, w, dst in _win_moves():
            if w == 8 or s == 256:
                continue
            for k, a, off, pw in _pieces(s, w, SH_IN):
                o_ref[k, a:a + pw, :] = src(dst + off, pw).astype(o_ref.dtype)
        for g in range(4):
            val = src(O_KV + KVL + DN + 8 * g, 8) + src(O_KV + KVL + HP + DN + _swap_start(g), 8)
            o_ref[0, 256 + 8 * g:256 + 8 * g + 8, :] = val.astype(o_ref.dtype)

    return pl.pallas_call(
        body, name=name, grid=(n // tc,),
        in_specs=[pl.BlockSpec((NIN, tc), lambda j: (0, j)), pl.BlockSpec((512, tc), lambda j: (0, j + col0 // tc))],
        out_specs=pl.BlockSpec((NDEV, SH_IN, tc), lambda j: (0, 0, j)),
        out_shape=jax.ShapeDtypeStruct((NDEV, SH_IN, n), BF),
        scratch_shapes=[pltpu.VMEM((512, tc), F32)],
        compiler_params=pltpu.CompilerParams(dimension_semantics=("parallel",)),
    )(dwt, dwct)


def _eye(n, m):
    return (lax.broadcasted_iota(jnp.int32, (n, m), 0) == lax.broadcasted_iota(jnp.int32, (n, m), 1)).astype(BF)


_NT = (((1,), (1,)), ((), ()))


def build_wq_wkv(gq, gkv, *, name):
    def body(gq_ref, gkv_ref, q_ref, kv_ref):
        q_ref[...] = jnp.zeros_like(q_ref)
        kv_ref[...] = jnp.zeros_like(kv_ref)
        eye = _eye(QL, QL)
        for h in range(NH):
            qh = lax.dot_general(eye, gq_ref[h], _NT, preferred_element_type=F32).astype(q_ref.dtype)
            q_ref[0:QL, h * HP:h * HP + DN + DR] = qh
            for g in range(4):
                c0 = NH * HP + h * HP + DN + 8 * g
                q_ref[0:QL, c0:c0 + 8] = qh[:, DN + _swap_start(g):DN + _swap_start(g) + 8]
            kv_ref[:, h * HP:h * HP + DN] = gkv_ref[h, :, 0:DN]
            kv_ref[:, NH * HP + h * DV:NH * HP + (h + 1) * DV] = gkv_ref[h, :, DN:DN + DV]

    vm = pl.BlockSpec(memory_space=pltpu.VMEM)
    return pl.pallas_call(
        body, name=name, in_specs=[vm, vm], out_specs=[vm, vm],
        out_shape=[jax.ShapeDtypeStruct((512, 2 * NH * HP), gq.dtype), jax.ShapeDtypeStruct((KVL, NH * HP + NH * DV), gq.dtype)],
    )(gq, gkv)


def shard_wq_wkv_grad(dwq2, dwkv2, *, name):
    def body(q_ref, kv_ref, gq_ref, gkv_ref, xs):
        xs[...] = jnp.zeros_like(xs)
        eye = _eye(DN + DR, HP)
        for h in range(NH):
            xs[:, 0:DN] = q_ref[0:QL, h * HP:h * HP + DN].astype(BF)
            for g in range(4):
                a = q_ref[0:QL, h * HP + DN + 8 * g:h * HP + DN + 8 * g + 8]
                c0 = NH * HP + h * HP + DN + _swap_start(g)
                xs[:, DN + 8 * g:DN + 8 * g + 8] = (a + q_ref[0:QL, c0:c0 + 8]).astype(BF)
            gq_ref[h] = lax.dot_general(eye, xs[...], _NT, preferred_element_type=F32).astype(BF)
            gkv_ref[h, :, 0:DN] = kv_ref[:, h * HP:h * HP + DN].astype(BF)
            gkv_ref[h, :, DN:DN + DV] = kv_ref[:, NH * HP + h * DV:NH * HP + (h + 1) * DV].astype(BF)

    vm = pl.BlockSpec(memory_space=pltpu.VMEM)
    return pl.pallas_call(
        body, name=name, in_specs=[vm, vm], out_specs=[vm, vm],
        out_shape=[jax.ShapeDtypeStruct((NDEV, DN + DR, QL), BF), jax.ShapeDtypeStruct((NDEV, KVL, DN + DV), BF)],
        scratch_shapes=[pltpu.VMEM((QL, HP), BF)],
    )(dwq2, dwkv2)


def unshard_cols(g, *, name, tm=256):
    _, K, n = g.shape
    tm = _pick(K, tm, 16)

    def body(g_ref, o_ref):
        for k in range(NDEV):
            o_ref[:, k * n:(k + 1) * n] = g_ref[k]

    return pl.pallas_call(
        body, name=name, grid=(K // tm,), in_specs=[pl.BlockSpec((NDEV, tm, n), lambda i: (0, i, 0))],
        out_specs=pl.BlockSpec((tm, NDEV * n), lambda i: (i, 0)), out_shape=jax.ShapeDtypeStruct((K, NDEV * n), g.dtype),
        compiler_params=pltpu.CompilerParams(dimension_semantics=("parallel",)),
    )(g)


def shard_cols(w, *, name, tm=256):
    K, n8 = w.shape
    n = n8 // NDEV
    tm = _pick(K, tm, 16)

    def body(w_ref, o_ref):
        for k in range(NDEV):
            o_ref[k] = w_ref[:, k * n:(k + 1) * n]

    return pl.pallas_call(
        body, name=name, grid=(K // tm,), in_specs=[pl.BlockSpec((tm, n8), lambda i: (i, 0))],
        out_specs=pl.BlockSpec((NDEV, tm, n), lambda i: (0, i, 0)), out_shape=jax.ShapeDtypeStruct((NDEV, K, n), w.dtype),
        compiler_params=pltpu.CompilerParams(dimension_semantics=("parallel",)),
    )(w)


def _rope_tables():
    t = np.arange(T)
    row = (t // GRID_W).astype(np.float32)
    col = (t % GRID_W).astype(np.float32)
    axis_dim = DR // 2
    inv = (np.float32(ROPE_THETA) ** (-np.arange(0, axis_dim, 2, dtype=np.float32) / np.float32(axis_dim))).astype(np.float32)
    ar, ac = (row[:, None] * inv).astype(np.float32), (col[:, None] * inv).astype(np.float32)
    cosv = np.concatenate([np.cos(ar), np.cos(ar), np.cos(ac), np.cos(ac)], axis=1).astype(np.float32)
    sinv = np.concatenate([-np.sin(ar), np.sin(ar), -np.sin(ac), np.sin(ac)], axis=1).astype(np.float32)
    ck = np.zeros((TKV, HP), np.float32)
    sk = np.zeros((TKV, HP), np.float32)
    ck[T:, DN:DN + DR] = 1.0
    ck[:T, DN:DN + DR] = cosv
    sk[:T, DN:DN + DR] = sinv
    cq = np.zeros((T, HP), np.float32)
    cq[:, :DN] = 1.0
    cq[:, DN:DN + DR] = cosv
    return jnp.asarray(ck), jnp.asarray(sk), jnp.asarray(cq), jnp.asarray(sk[:T])


def _local_step(x, ctx, tgt, mod_lat, mod_ctx, n1g, qg, kvg, n2g, fg, conv_w, conv_b, ffn_w, ffn_b, get_w, put_g, dep0):
    sh1, sc1, g1, sh2, sc2, g2 = [mod_lat[:, i * D:(i + 1) * D] for i in range(6)]
    csh1, csc1 = mod_ctx[:, 0:D], mod_ctx[:, D:2 * D]
    ck, sk, cq_t, sq_t = _rope_tables()
    qg_p = jnp.pad(qg, ((0, 0), (0, 512 - QL)))

    hcat = normmod_cat(ctx, x, n1g, csc1, csh1, sc1, sh1, dep0, name="normmod1")
    win = get_w("in", hcat)
    p = mm(hcat, win, M=T, tn=768, name="in_proj")
    pc = mm(hcat, win, M=TC, N=512, a_off=(T, 0), b_off=(0, O_KV), name="in_proj_ctx")
    wq2, wkv2, wao, wco, wo = get_w("mid", p)
    kh, vh, ckv = kvprep(pc, p, kvg, wkv2, ck, sk, name="kvprep")
    qr, cq = qprep(p, qg_p, wq2, cq_t, sq_t, name="qprep")
    o, lse = attn_fwd(qr, kh, vh, name="attn_fwd")
    z = convz(p, conv_w, conv_b, name="convz")
    ya, yc, merged = out_proj_merge(o, wao, z, wco, p, name="attn_conv_out_gate_merge")
    a_out, x1, h2 = oproj_resid(merged, wo, x, g1, n2g, sc2, sh2, name="o_proj_resid_normmod2")
    wup = get_w("up", h2)
    u0 = mm(h2, wup, tb=True, o_stack=True, tn=1408, name="up_proj")
    f = ffn_act(u0, ffn_w, ffn_b, name="ffn_act")
    wdn = get_w("down", f)
    dn, dx2, dd, dfg, loss = down_final(f, wdn, x1, g2, fg, tgt, name="down_proj_final_loss")

    df = mm(dd, wdn, tb=True, tn=1408, name="down_proj_dx")
    dwdn = mm(f, dd, ta=True, out_dtype=BF, tm=1408, name="down_proj_dw")
    du0, dffn_w, dffn_b = ffn_act_bwd(u0, df, ffn_w, ffn_b, name="ffn_act_bwd")
    dwup = mm(du0, h2, ta=True, a_stack=True, out_dtype=BF, tm=1408, name="up_proj_dw")
    tok = put_g("ffn", dict(dwup=dwup, dwdn=dwdn))
    dx1, da, st2 = normmod_bwd(x1, dict(a=du0, b=wup, a_stack=True, tk=DFF, dep=tok), n2g, sc2, dx2, dn, g1,
                               name="up_proj_dx_normmod2_bwd")

    dwo = mm(merged, da, ta=True, out_dtype=BF, tn=512, name="o_proj_dw")
    dya, dyc, dp, do, dz = oproj_dx_gate_bwd(da, wo, p, ya, yc, wao, wco, name="o_proj_dx_gate_merge_bwd")
    dwao = mm(o, dya, ta=True, out_dtype=BF, tn=512, name="attn_out_dw")
    dwco = mm(z, dyc, ta=True, out_dtype=BF, tn=512, name="conv_out_dw")
    tok = put_g("mid", dict(dwao=dwao, dwco=dwco, dwo=dwo))
    dp, dconv_w, dconv_b = convz_bwd(p, dz, conv_w, conv_b, dp, name="convz_bwd")
    dq, dk, dv = attn_bwd(qr, kh, vh, do, o, lse, tok, name="attn_bwd")
    dp, dq2, dqg = qprep_bwd(p, dq, qg_p, wq2, cq_t, sq_t, dp, name="qprep_bwd")
    dp, dpc, dkv2, dkvg = kvprep_bwd(pc, p, dk, dv, kvg, wkv2, ck, sk, dp, name="kvprep_bwd")

    dwin_c = mm(dpc, hcat, ta=True, K=TC, b_off=(T, 0), name="in_proj_ctx_dw")
    tok = None
    for half in range(2):
        dwin = mm(dp, hcat, ta=True, K=T, N=D // 2, b_off=(0, half * (D // 2)), tm=768, dep=tok,
                  name=f"in_proj_dw_{half}")
        tok = put_g(f"in{half}", dict(dwin=dwin, dwin_c=dwin_c, col0=half * (D // 2)))
    dwq2 = mm(cq, dq2, ta=True, dep=tok, name="q_up_dw")
    dwkv2 = mm(ckv, dkv2, ta=True, dep=tok, name="kv_up_dw")
    tok = put_g("qkv", dict(dwq2=dwq2, dwkv2=dwkv2))
    dhc = mm(dpc, win, tb=True, N=D, K=512, b_off=(0, O_KV), dep=tok, name="in_proj_ctx_dx")
    dx, _, st1 = normmod_bwd(x, dict(a=dp, b=win, tb=True, tk=NIN, dep=tok), n1g, sc1, dx1, a_out, g1,
                             name="in_proj_dx_normmod1_bwd")
    stc = normmod_bwd(ctx, dhc, n1g, csc1, None, None, None, name="normmod1_ctx_bwd")

    return dict(loss=loss, dx=dx, st1=st1, st2=st2, stc=stc, dqg=dqg, dkvg=dkvg, dfg=dfg,
                dconv_w=dconv_w, dconv_b=dconv_b, dffn_w=dffn_w, dffn_b=dffn_b)


def _me():
    x, y, c = lax.axis_index("x"), lax.axis_index("y"), lax.axis_index("c")
    return x, y, c, 4 * x + 2 * y + c


def _peer(x, y, c, k):
    px = 1 - x if k & 4 else x
    py = 1 - y if k & 2 else y
    pc = 1 - c if k & 1 else c
    return (px, py, pc), 4 * px + 2 * py + pc


def _exchange_tiles(src_of_peer, buf, send_sem, recv_sem):
    x, y, c, me = _me()
    for k in range(1, NDEV):
        dev, lin = _peer(x, y, c, k)
        pltpu.make_async_remote_copy(src_ref=src_of_peer(lin), dst_ref=buf.at[me], send_sem=send_sem, recv_sem=recv_sem,
                                     device_id=dev, device_id_type=MESH).start()
    seven = buf.at[pl.ds(0, NDEV - 1)]
    pltpu.make_async_remote_copy(src_ref=seven, dst_ref=seven, send_sem=send_sem, recv_sem=recv_sem,
                                 device_id=(x, y, c), device_id_type=MESH).wait()


def _silu(z):
    return z * jax.nn.sigmoid(z)


def ada_fwd(c, c_ctx, ffn_w, conv_w, w_shard, b_ada, deps, *, name):
    nsh, nf, nc = w_shard.shape[1], ffn_w.shape[2], conv_w.shape[2]
    deps = [d for d in deps if d is not None]

    def body(c_ref, cc_ref, fw_ref, cw_ref, w_ref, b_ref, *rest):
        s_ref, ml_ref, mc_ref, fwf_ref, cwf_ref, m_ref, mine, res, sems = rest[len(deps):]
        x, y, c, me = _me()
        mine[...] = jnp.zeros_like(mine)
        mine[0:1, :] = _silu(c_ref[...])
        mine[1:2, :] = _silu(cc_ref[...])
        for k in range(3):
            mine[2 + k:3 + k, 0:fw_ref.shape[2]] = fw_ref[k]
            mine[5 + k:6 + k, 0:cw_ref.shape[2]] = cw_ref[k]
        s_ref[me] = mine[...]
        _exchange_tiles(lambda lin: mine, s_ref, sems.at[0], sems.at[1])
        sall = s_ref[...].reshape(NDEV * 8, D).astype(BF)
        r = jnp.dot(sall, w_ref[...].astype(BF), preferred_element_type=F32) + b_ref[me]
        res[...] = r.reshape(NDEV, 8, nsh)
        m_ref[me] = res[me]
        _exchange_tiles(lambda lin: res.at[lin], m_ref, sems.at[2], sems.at[3])
        for j in range(NDEV):
            ml_ref[:, j * nsh:(j + 1) * nsh] = m_ref[j, 0:1, :]
            mc_ref[:, j * nsh:(j + 1) * nsh] = m_ref[j, 1:2, :]
            fwf_ref[:, j * nf:(j + 1) * nf] = s_ref[j, 2:5, 0:nf]
            cwf_ref[:, j * nc:(j + 1) * nc] = s_ref[j, 5:8, 0:nc]

    vm = pl.BlockSpec(memory_space=pltpu.VMEM)
    return pl.pallas_call(
        body, name=name, in_specs=[vm] * 6 + [pl.BlockSpec(memory_space=pl.ANY)] * len(deps), out_specs=[vm] * 5,
        out_shape=[jax.ShapeDtypeStruct((NDEV, 8, D), F32), jax.ShapeDtypeStruct((1, NDEV * nsh), F32),
                   jax.ShapeDtypeStruct((1, NDEV * nsh), F32), jax.ShapeDtypeStruct((3, NDEV * nf), F32),
                   jax.ShapeDtypeStruct((3, NDEV * nc), F32)],
        scratch_shapes=[pltpu.VMEM((NDEV, 8, nsh), F32), pltpu.VMEM((8, D), F32), pltpu.VMEM((NDEV, 8, nsh), F32),
                        pltpu.SemaphoreType.DMA((4,))],
    )(c, c_ctx, ffn_w, conv_w, w_shard, b_ada, *deps)


P_DML, P_DMC, P_N1, P_QG, P_KVG, P_CB, P_N2, P_FB, P_FG, P_CW, P_FW, P_LOSS, P_ROWS = 0, 6, 12, 13, 14, 15, 16, 17, 23, 24, 27, 45, 48
FROWS = 3


def pack_small(r, *, name):
    ins = [r["st1"], r["st2"], r["stc"], r["dqg"], r["dkvg"], r["dconv_b"], r["dffn_b"], r["dfg"], r["dconv_w"],
           r["dffn_w"], r["loss"]]

    def put_wide(p, row0, row, n):
        for j in range(-(-n // D)):
            w = min(D, n - j * D)
            p[row0 + j:row0 + j + 1, 0:w] = row[:, j * D:j * D + w]

    def body(st1, st2, stc, qg, kvg, cb, fb, fg, cw, fw, loss, p):
        p[...] = jnp.zeros_like(p)
        for j, row in enumerate((st1[0:1, :], st1[1:2, :], st1[3:4, :], st2[0:1, :], st2[1:2, :], st2[3:4, :])):
            p[P_DML + j:P_DML + j + 1, :] = row
        p[P_DMC:P_DMC + 2, :] = stc[0:2, :]
        p[P_N1:P_N1 + 1, :] = st1[2:3, :] + stc[2:3, :]
        p[P_N2:P_N2 + 1, :] = st2[2:3, :]
        put_wide(p, P_QG, qg, 512)
        put_wide(p, P_KVG, kvg, KVL)
        put_wide(p, P_CB, cb, CONV)
        put_wide(p, P_FG, fg, D)
        put_wide(p, P_LOSS, loss, 128)
        for s in range(2):
            put_wide(p, P_FB + FROWS * s, fb.at[s], DFF)
        for k in range(3):
            put_wide(p, P_CW + k, cw.at[k:k + 1], CONV)
            for s in range(2):
                put_wide(p, P_FW + FROWS * (2 * k + s), fw.at[s, k:k + 1], DFF)

    vm = pl.BlockSpec(memory_space=pltpu.VMEM)
    return pl.pallas_call(
        body, name=name, in_specs=[vm] * len(ins), out_specs=vm, out_shape=jax.ShapeDtypeStruct((P_ROWS, D), F32),
    )(*ins)


def sum_slots(a, *, name):
    rows = dict(norm1_g=(P_N1, D), q_norm_g=(P_QG, QL), kv_norm_g=(P_KVG, KVL), conv_b=(P_CB, CONV), norm2_g=(P_N2, D),
                final_g=(P_FG, D))

    def body(a_ref, sum_ref, *out):
        acc = a_ref[0]
        for k in range(1, NDEV):
            acc = acc + a_ref[k]
        sum_ref[...] = acc
        for ref, (row, n) in zip(out, rows.values()):
            ref[...] = sum_ref[row:row + 1, 0:n]
        fb, bada = out[len(rows):]
        for s in range(2):
            for j in range(FROWS):
                w = min(D, DFF - j * D)
                row = P_FB + FROWS * s + j
                fb[:, s * DFF + j * D:s * DFF + j * D + w] = sum_ref[row:row + 1, 0:w]
        for j in range(6):
            bada[:, j * D:(j + 1) * D] = sum_ref[P_DML + j:P_DML + j + 1, :] + sum_ref[P_DMC + j:P_DMC + j + 1, :]

    vm = pl.BlockSpec(memory_space=pltpu.VMEM)
    widths = [n for _, n in rows.values()] + [2 * DFF, 6 * D]
    outs = pl.pallas_call(
        body, name=name, in_specs=[vm], out_specs=[vm] * (1 + len(widths)),
        out_shape=[jax.ShapeDtypeStruct(a.shape[1:], F32)] + [jax.ShapeDtypeStruct((1, n), F32) for n in widths])(a)
    return outs[0], dict(zip(list(rows) + ["ffn_conv_b", "b_ada"], outs[1:]))


def ada_bwd(s_all, dml, dmc, w_shard, c_ctx, *, name):
    nsh = w_shard.shape[1]

    def body(s_ref, dml_ref, dmc_ref, w_ref, c_ref, dw_ref, gc_ref, s16, dm16, part, buf, sems):
        x, y, c, me = _me()
        s16[...] = jnp.zeros_like(s16)
        dm16[...] = jnp.zeros_like(dm16)
        for k in range(NDEV):
            s16[k:k + 1, :] = s_ref[k, 0:1, :]
        s16[8:9, :] = s_ref[0, 1:2, :]
        dm16[0:8, :] = dml_ref[...]
        dm16[8:9, :] = dmc_ref[...]
        dw_ref[...] = lax.dot_general(s16[...].astype(BF), dm16[...].astype(BF), (((0,), (0,)), ((), ())),
                                      preferred_element_type=F32)
        part[...] = lax.dot_general(dm16[8:16, :].astype(BF), w_ref[...].astype(BF), (((1,), (1,)), ((), ())),
                                    preferred_element_type=F32)
        buf[me] = part[...]
        _exchange_tiles(lambda lin: part, buf, sems.at[0], sems.at[1])
        acc = buf[0]
        for k in range(1, NDEV):
            acc = acc + buf[k]
        z = c_ref[...]
        sg = jax.nn.sigmoid(z)
        gc_ref[...] = acc * (sg * (1.0 + z * (1.0 - sg)))

    vm = pl.BlockSpec(memory_space=pltpu.VMEM)
    return pl.pallas_call(
        body, name=name, in_specs=[vm] * 5, out_specs=[vm, vm],
        out_shape=[jax.ShapeDtypeStruct((D, nsh), F32), jax.ShapeDtypeStruct((8, D), F32)],
        scratch_shapes=[pltpu.VMEM((16, D), F32), pltpu.VMEM((16, nsh), F32), pltpu.VMEM((8, D), F32),
                        pltpu.VMEM((NDEV, 8, D), F32), pltpu.SemaphoreType.DMA((2,))],
    )(s_all, dml, dmc, w_shard, c_ctx)


HBM_SPEC = pl.BlockSpec(memory_space=pltpu.HBM)
SEM_SPEC = pl.BlockSpec(memory_space=pltpu.SEMAPHORE)
EFFECT = pltpu.SideEffectType.DATAFLOW_SIDE_EFFECTING


ALL_PEERS = tuple(range(1, NDEV))
FIRST_HOP = (1, 2, 4, 6)
RELAY = (2, 4, 6)


def _exchange_copies(srcs, lands, send, recv, per_peer, peers):
    x, y, c, me = _me()
    n = len(peers)
    cps = []
    for t in range(len(srcs)):
        for j, k in enumerate(peers):
            dev, lin = _peer(x, y, c, k)
            cps.append(pltpu.make_async_remote_copy(
                src_ref=srcs[t].at[lin] if per_peer else srcs[t], dst_ref=lands[t].at[me],
                send_sem=send.at[n * t + j], recv_sem=recv.at[n * t + j], device_id=dev, device_id_type=MESH))
    return cps


def _relay_copies(lands, send, recv):
    x, y, c, me = _me()
    n = len(RELAY)
    cps = []
    for t in range(len(lands)):
        for j, k in enumerate(RELAY):
            slot = lands[t].at[_peer(x, y, c, k)[1]]
            cps.append(pltpu.make_async_remote_copy(
                src_ref=slot, dst_ref=slot, send_sem=send.at[n * t + j], recv_sem=recv.at[n * t + j],
                device_id=(x, y, 1 - c), device_id_type=MESH))
    return cps


def _own_copies(srcs, lands, own, per_peer):
    me = _me()[3]
    return [pltpu.make_async_copy(srcs[t].at[me] if per_peer else srcs[t], lands[t].at[me], own.at[t])
            for t in range(len(srcs))]


def exchange_start(srcs, *, per_peer, name, dep=None, peers=ALL_PEERS):
    nt = len(srcs)
    ns = len(peers) * nt
    land_shapes = [(a.shape if per_peer else (NDEV,) + a.shape) for a in srcs]
    deps = [] if dep is None else [dep]

    def body(*refs):
        src, land = refs[:nt], refs[nt:2 * nt]
        send, recv, own = refs[2 * nt + len(deps):2 * nt + len(deps) + 3]
        for cp in _exchange_copies(src, land, send, recv, per_peer, peers) + _own_copies(src, land, own, per_peer):
            cp.start()
        refs[-1][...] = jnp.zeros_like(refs[-1])

    hb = lambda a: pltpu.with_memory_space_constraint(a, pltpu.HBM)
    outs = pl.pallas_call(
        body, name=name,
        out_shape=(pltpu.SemaphoreType.DMA((ns,)), pltpu.SemaphoreType.DMA((ns,)), pltpu.SemaphoreType.DMA((nt,)),
                   *[pltpu.HBM(a.shape, a.dtype) for a in srcs], *[pltpu.HBM(s, a.dtype) for s, a in zip(land_shapes, srcs)],
                   jax.ShapeDtypeStruct((8, 128), F32)),
        in_specs=[HBM_SPEC] * (2 * nt) + [pl.BlockSpec(memory_space=pl.ANY)] * len(deps),
        out_specs=(SEM_SPEC, SEM_SPEC, SEM_SPEC, *([HBM_SPEC] * (2 * nt)), pl.BlockSpec(memory_space=pltpu.VMEM)),
        input_output_aliases={i: 3 + i for i in range(2 * nt)},
        compiler_params=pltpu.CompilerParams(has_side_effects=EFFECT),
    )(*[hb(a) for a in srcs], *[hb(lax.empty(s, a.dtype)) for s, a in zip(land_shapes, srcs)], *deps)
    return dict(send=outs[0], recv=outs[1], own=outs[2], src=list(outs[3:3 + nt]), land=list(outs[3 + nt:3 + 2 * nt]),
                token=outs[-1], per_peer=per_peer, peers=peers)


def exchange_wait(h, after, *, name):
    nt = len(h["src"])
    per_peer, peers = h["per_peer"], h["peers"]
    after = list(after) if isinstance(after, (list, tuple)) else [after]

    def body(*refs):
        src, land, send, recv, own = refs[:nt], refs[nt:2 * nt], refs[2 * nt], refs[2 * nt + 1], refs[2 * nt + 2]
        for cp in _exchange_copies(src, land, send, recv, per_peer, peers):
            cp.wait_send()
            cp.wait_recv()
        for cp in _own_copies(src, land, own, per_peer):
            cp.wait()

    outs = pl.pallas_call(
        body, name=name,
        out_shape=(*[pltpu.HBM(a.shape, a.dtype) for a in h["src"]], *[pltpu.HBM(a.shape, a.dtype) for a in h["land"]]),
        in_specs=[HBM_SPEC] * (2 * nt) + [SEM_SPEC, SEM_SPEC, SEM_SPEC] + [pl.BlockSpec(memory_space=pl.ANY)] * len(after),
        out_specs=tuple([HBM_SPEC] * (2 * nt)),
        input_output_aliases={i: i for i in range(2 * nt)},
        compiler_params=pltpu.CompilerParams(has_side_effects=EFFECT),
    )(*h["src"], *h["land"], h["send"], h["recv"], h["own"], *after)
    return list(outs[nt:])


def relay_start(lands, *, name):
    nt = len(lands)
    ns = len(RELAY) * nt

    def body(*refs):
        for cp in _relay_copies(refs[:nt], refs[nt], refs[nt + 1]):
            cp.start()

    outs = pl.pallas_call(
        body, name=name,
        out_shape=(pltpu.SemaphoreType.DMA((ns,)), pltpu.SemaphoreType.DMA((ns,)),
                   *[pltpu.HBM(a.shape, a.dtype) for a in lands]),
        in_specs=[HBM_SPEC] * nt, out_specs=(SEM_SPEC, SEM_SPEC, *([HBM_SPEC] * nt)),
        input_output_aliases={i: 2 + i for i in range(nt)},
        compiler_params=pltpu.CompilerParams(has_side_effects=EFFECT),
    )(*lands)
    return dict(send=outs[0], recv=outs[1], land=list(outs[2:]))


def relay_wait(h, *, name):
    nt = len(h["land"])

    def body(*refs):
        for cp in _relay_copies(refs[:nt], refs[nt], refs[nt + 1]):
            cp.wait_send()
            cp.wait_recv()

    outs = pl.pallas_call(
        body, name=name, out_shape=tuple(pltpu.HBM(a.shape, a.dtype) for a in h["land"]),
        in_specs=[HBM_SPEC] * nt + [SEM_SPEC, SEM_SPEC], out_specs=tuple([HBM_SPEC] * nt),
        input_output_aliases={i: i for i in range(nt)},
        compiler_params=pltpu.CompilerParams(has_side_effects=EFFECT),
    )(*h["land"], h["send"], h["recv"])
    return list(outs)


def _adamw_math(w, g, m, v):
    nm = B1 * m + (1.0 - B1) * g
    nv = B2 * v + (1.0 - B2) * (g * g)
    m_hat = nm / (1.0 - B1 ** STEP)
    v_hat = nv / (1.0 - B2 ** STEP)
    return -LR * (m_hat / (jnp.sqrt(v_hat) + AEPS) + WD * w), nm, nv


def adamw_many(ws, gs, ms, vs, *, name):
    n = len(ws)

    def body(*refs):
        for k in range(n):
            d, nm, nv = _adamw_math(refs[k][...], refs[n + k][...], refs[2 * n + k][...], refs[3 * n + k][...])
            refs[4 * n + k][...] = d
            refs[5 * n + k][...] = nm
            refs[6 * n + k][...] = nv

    vm = pl.BlockSpec(memory_space=pltpu.VMEM)
    sh = [jax.ShapeDtypeStruct(w.shape, F32) for w in ws]
    outs = pl.pallas_call(body, name=name, in_specs=[vm] * (4 * n), out_specs=[vm] * (3 * n), out_shape=sh * 3,
                          )(*ws, *gs, *ms, *vs)
    return outs[:n], outs[n:2 * n], outs[2 * n:]


def adamw(w, g, m, v, *, name, tr=256):
    R, C = w.shape
    tr = _pick(R, tr, 8)

    def body(w_ref, g_ref, m_ref, v_ref, d_ref, nm_ref, nv_ref):
        d_ref[...], nm_ref[...], nv_ref[...] = _adamw_math(w_ref[...], g_ref[...], m_ref[...], v_ref[...])

    blk = pl.BlockSpec((tr, C), lambda i: (i, 0))
    sh = jax.ShapeDtypeStruct((R, C), F32)
    return pl.pallas_call(
        body, name=name, grid=(R // tr,), in_specs=[blk, blk, blk, blk], out_specs=[blk, blk, blk],
        out_shape=[sh, sh, sh], compiler_params=pltpu.CompilerParams(dimension_semantics=("parallel",)),
    )(w, g, m, v)


def adamw_slots(w, slots, m, v, *, name, tr=256):
    unit = w.ndim == 3
    R, C = w.shape[0], w.shape[-1]
    parts = list(slots) if isinstance(slots, (list, tuple)) else [slots]
    n = len(parts)
    assert sum(s.shape[-1] for s in parts) == C
    if R % 16 == 0:
        tr = _pick(R, tr, 16)
    else:
        tr = 144

    def body(w_ref, *refs):
        s_refs, (m_ref, v_ref, g_ref, d_ref, nm_ref, nv_ref) = refs[:n], refs[n:]
        gs = []
        for s_ref in s_refs:
            g = s_ref[0].astype(F32)
            for k in range(1, NDEV):
                g = g + s_ref[k].astype(F32)
            gs.append(g)
        g = gs[0] if n == 1 else jnp.concatenate(gs, axis=-1)
        g_ref[...] = g
        d_ref[...], nm_ref[...], nv_ref[...] = _adamw_math(w_ref[...], g, m_ref[...], v_ref[...])

    blk = pl.BlockSpec((tr, None, C), lambda i: (i, 0, 0)) if unit else pl.BlockSpec((tr, C), lambda i: (i, 0))
    sh = jax.ShapeDtypeStruct(w.shape, F32)
    return pl.pallas_call(
        body, name=name, grid=(pl.cdiv(R, tr),),
        in_specs=[blk] + [pl.BlockSpec((NDEV, tr, s.shape[-1]), lambda i: (0, i, 0)) for s in parts] + [blk, blk],
        out_specs=[blk, blk, blk, blk], out_shape=[sh, sh, sh, sh],
        compiler_params=pltpu.CompilerParams(dimension_semantics=("parallel",)),
    )(w, *parts, m, v)


def kernel(x, c, ctx, c_ctx, w_ada, b_ada, norm1_g, w_in, q_norm_g, kv_norm_g, w_uq, w_ukv, conv_w, conv_b, w_attn_out, w_conv_out, w_o, norm2_g, w_up, ffn_conv_w, ffn_conv_b, w_down, final_g, loss_target, m_c_ctx, m_w_ada, m_b_ada, m_norm1_g, m_w_in, m_q_norm_g, m_kv_norm_g, m_w_uq, m_w_ukv, m_conv_w, m_conv_b, m_w_attn_out, m_w_conv_out, m_w_o, m_norm2_g, m_w_up, m_ffn_conv_w, m_ffn_conv_b, m_w_down, m_final_g, v_c_ctx, v_w_ada, v_b_ada, v_norm1_g, v_w_in, v_q_norm_g, v_kv_norm_g, v_w_uq, v_w_ukv, v_conv_w, v_conv_b, v_w_attn_out, v_w_conv_out, v_w_o, v_norm2_g, v_w_up, v_ffn_conv_w, v_ffn_conv_b, v_w_down, v_final_g):
    me = 4 * lax.axis_index("x") + 2 * lax.axis_index("y") + lax.axis_index("c")
    W = dict(c_ctx=c_ctx, w_ada=w_ada, b_ada=b_ada, norm1_g=norm1_g, w_in=w_in, q_norm_g=q_norm_g, kv_norm_g=kv_norm_g,
             w_uq=w_uq, w_ukv=w_ukv, conv_w=conv_w, conv_b=conv_b, w_attn_out=w_attn_out, w_conv_out=w_conv_out, w_o=w_o,
             norm2_g=norm2_g, w_up=w_up, ffn_conv_w=ffn_conv_w, ffn_conv_b=ffn_conv_b, w_down=w_down, final_g=final_g)
    M = dict(c_ctx=m_c_ctx, w_ada=m_w_ada, b_ada=m_b_ada, norm1_g=m_norm1_g, w_in=m_w_in, q_norm_g=m_q_norm_g,
             kv_norm_g=m_kv_norm_g, w_uq=m_w_uq, w_ukv=m_w_ukv, conv_w=m_conv_w, conv_b=m_conv_b, w_attn_out=m_w_attn_out,
             w_conv_out=m_w_conv_out, w_o=m_w_o, norm2_g=m_norm2_g, w_up=m_w_up, ffn_conv_w=m_ffn_conv_w,
             ffn_conv_b=m_ffn_conv_b, w_down=m_w_down, final_g=m_final_g)
    V = dict(c_ctx=v_c_ctx, w_ada=v_w_ada, b_ada=v_b_ada, norm1_g=v_norm1_g, w_in=v_w_in, q_norm_g=v_q_norm_g,
             kv_norm_g=v_kv_norm_g, w_uq=v_w_uq, w_ukv=v_w_ukv, conv_w=v_conv_w, conv_b=v_conv_b, w_attn_out=v_w_attn_out,
             w_conv_out=v_w_conv_out, w_o=v_w_o, norm2_g=v_norm2_g, w_up=v_w_up, ffn_conv_w=v_ffn_conv_w,
             ffn_conv_b=v_ffn_conv_b, w_down=v_w_down, final_g=v_final_g)
    names = list(W)
    transposed = ("w_up", "w_uq")
    as2d = lambda k, a: (a.reshape(1, -1) if a.ndim == 1 else
                         a[0].T if k in transposed else a.reshape(a.shape[-2], a.shape[-1]))
    W2 = {k: as2d(k, a) for k, a in W.items()}
    M2 = {k: as2d(k, a) for k, a in M.items()}
    V2 = {k: as2d(k, a) for k, a in V.items()}
    unit3 = lambda a: jnp.transpose(a, (2, 0, 1))
    W3, M3, V3 = unit3(W["w_in"]), unit3(M["w_in"]), unit3(V["w_in"])
    nsh = W2["w_ada"].shape[1]

    unit_mid = ("conv_w", "ffn_conv_w")
    mid3 = lambda a: jnp.transpose(a, (1, 0, 2))
    s_all, mod_lat, mod_ctx, ffn_w_full, conv_w_full = ada_fwd(
        c, W2["c_ctx"], mid3(W["ffn_conv_w"]), mid3(W["conv_w"]), W2["w_ada"], W["b_ada"].reshape(NDEV, 1, nsh), [],
        name="ada_fwd")

    stage_w = {"in": ["w_in"], "mid": ["w_uq", "w_ukv", "w_attn_out", "w_conv_out", "w_o"], "up": ["w_up"],
               "down": ["w_down"]}
    two_level = ("in", "mid")
    ag, tok = {}, mod_lat
    for st, nms in stage_w.items():
        ag[st] = exchange_start([W2[nm].astype(BF) for nm in nms], per_peer=False, dep=tok, name="ag_start_" + st,
                                peers=FIRST_HOP if st in two_level else ALL_PEERS)
        tok = ag[st]["token"]

    def get_w(stage, after):
        lands = exchange_wait(ag[stage], after, name="ag_wait_" + stage)
        if stage in two_level:
            lands = relay_wait(relay_start(lands, name="ag_relay_" + stage), name="ag_relay_wait_" + stage)
        g = dict(zip(stage_w[stage], lands))
        if stage == "in":
            return build_win(g["w_in"], name="build_win")
        if stage == "mid":
            wq2, wkv2 = build_wq_wkv(g["w_uq"], g["w_ukv"], name="build_wq_wkv")
            return (wq2, wkv2, unshard_cols(g["w_attn_out"], name="unshard_w_attn_out"),
                    unshard_cols(g["w_conv_out"], name="unshard_w_conv_out"), g["w_o"].reshape(D, D))
        if stage == "up":
            return g["w_up"].reshape(2 * DFF, D)
        return g["w_down"].reshape(DFF, D)

    stage_g = {"ffn": ["w_up", "w_down"], "mid": ["w_attn_out", "w_conv_out", "w_o"], "qkv": ["w_uq", "w_ukv"],
               "in": ["w_in"]}
    rs = {}

    def put_g(stage, g):
        if stage in ("in0", "in1"):
            parts = [shard_win_grad(g["dwin"], g["dwin_c"], col0=g["col0"], name="shard_win_grad_" + stage[-1])]
        elif stage == "mid":
            parts = [shard_cols(g["dwao"], name="shard_w_attn_out"), shard_cols(g["dwco"], name="shard_w_conv_out"),
                     g["dwo"].reshape(NDEV, D // NDEV, D)]
        elif stage == "qkv":
            parts = list(shard_wq_wkv_grad(g["dwq2"], g["dwkv2"], name="shard_wq_wkv_grad"))
        else:
            parts = [g["dwup"].reshape(NDEV, 2 * DFF // NDEV, D), g["dwdn"].reshape(NDEV, DFF // NDEV, D)]
        rs[stage] = exchange_start(parts, per_peer=True, name="rs_start_" + stage)
        return rs[stage]["token"]

    r = _local_step(x[0], ctx[0], loss_target[0], mod_lat, mod_ctx, W2["norm1_g"], W2["q_norm_g"], W2["kv_norm_g"],
                    W2["norm2_g"], W2["final_g"], conv_w_full, W2["conv_b"], ffn_w_full, W2["ffn_conv_b"], get_w, put_g,
                    ag["down"]["token"])

    G, DL, NM, NV = {}, {}, {}, {}

    def finish(stage, after):
        if stage == "in":
            halves = []
            for h in ("in0", "in1"):
                halves += exchange_wait(rs[h], after, name="rs_wait_" + h)
                after = halves[-1]
            G["w_in"], DL["w_in"], NM["w_in"], NV["w_in"] = adamw_slots(W3, halves, M3, V3, name="adamw_w_in")
            return DL["w_in"]
        for nm, sl in zip(stage_g[stage], exchange_wait(rs[stage], after, name="rs_wait_" + stage)):
            G[nm], DL[nm], NM[nm], NV[nm] = adamw_slots(W2[nm], sl, M2[nm], V2[nm], name="adamw_" + nm)
            after = DL[nm]
        return after

    sync = exchange_start([pack_small(r, name="pack_small")], per_peer=False, name="sync_start")
    after = sync["token"]
    for st in ("ffn", "mid", "in", "qkv"):
        after = finish(st, after)
    a_buf, = exchange_wait(sync, [DL[nm] for nms in stage_g.values() for nm in nms], name="sync_wait")
    ssum, g_vec = sum_slots(a_buf, name="sum_small")
    G.update(g_vec)
    loss = ssum[P_LOSS, 0]
    G["conv_w"] = lax.dynamic_slice(ssum[P_CW:P_CW + 3, :CONV], (0, me * (CONV // NDEV)), (3, CONV // NDEV))
    fw_full = ssum[P_FW:P_FW + 6 * FROWS].reshape(3, 2, FROWS * D)[:, :, :DFF].reshape(3, 2 * DFF)
    G["ffn_conv_w"] = lax.dynamic_slice(fw_full, (0, me * (2 * DFF // NDEV)), (3, 2 * DFF // NDEV))

    dml = lax.dynamic_slice(a_buf[:, P_DML:P_DML + 6, :].reshape(NDEV, 6 * D), (0, me * nsh), (NDEV, nsh))
    dmc = lax.dynamic_slice(ssum[P_DMC:P_DMC + 6].reshape(1, 6 * D), (0, me * nsh), (1, nsh))
    G["w_ada"], gcc = ada_bwd(s_all, dml, dmc, W2["w_ada"], W2["c_ctx"], name="ada_bwd")
    G["c_ctx"] = gcc[0:1]

    DL["w_ada"], NM["w_ada"], NV["w_ada"] = adamw(W2["w_ada"], G["w_ada"], M2["w_ada"], V2["w_ada"], name="adamw_w_ada")
    small = ["c_ctx", "b_ada", "norm1_g", "q_norm_g", "kv_norm_g", "conv_b", "norm2_g", "ffn_conv_b", "final_g", "conv_w",
             "ffn_conv_w"]
    view = lambda k, a3, a2: mid3(a3[k]) if k in unit_mid else a2[k]
    for k in unit_mid:
        G[k] = G[k].reshape(3, 1, -1)
    ds, nms, nvs = adamw_many([view(k, W, W2) for k in small], [G[k] for k in small],
                              [view(k, M, M2) for k in small], [view(k, V, V2) for k in small], name="adamw_small")
    for k, nm in enumerate(small):
        DL[nm], NM[nm], NV[nm] = ds[k], nms[k], nvs[k]

    def as_output(nm, a):
        if nm in transposed:
            return a.T[None]
        if nm == "w_in":
            return jnp.transpose(a, (1, 2, 0))
        if nm in unit_mid and a.ndim == 3:
            return jnp.transpose(a, (1, 0, 2))
        return a.reshape(W[nm].shape)

    outs = [loss, r["dx"][None]]
    for grp in (G, DL, NM, NV):
        outs += [as_output(nm, grp[nm]) for nm in names]
    return tuple(outs)
```

```python
import functools
import numpy as np
import jax
import jax.numpy as jnp
from jax import lax
from jax.experimental import pallas as pl
from jax.experimental.pallas import tpu as pltpu

F32 = jnp.float32
BF = jnp.bfloat16
MESH = pl.DeviceIdType.MESH

D = 1024
T = 2048
TC = 256
TKV = T + TC
GRID_W = 64
NH = 8
DN = 64
DR = 32
DV = 64
QL = 384
KVL = 256
CONV = 512
DFF = 2816
EPS = 1e-6
ROPE_THETA = 10000.0
SCALE = (DN + DR) ** -0.5
NDEV = 8
HP = 128

O_GA, O_GC, O_KV, O_Q, O_CV = 0, 1024, 2048, 2560, 3072
NIN = 4608
CVB = 256
N_IN = 4256
SH_IN = N_IN // NDEV

LR, B1, B2, AEPS, WD, STEP = 0.001, 0.9, 0.999, 1e-08, 0.01, 10


def _pick(n, target, mult=128):
    best = None
    for d in range(mult, min(n, target) + 1, mult):
        if n % d == 0:
            best = d
    return best if best is not None else n


def _swap_start(g):
    return 8 * (g ^ 1)


def mm(a, b, *, ta=False, tb=False, out_dtype=F32, name, tm=1024, tn=1024, tk=2048, M=None, N=None, K=None,
       a_off=(0, 0), b_off=(0, 0), a_stack=False, b_stack=False, o_stack=False, dep=None):
    def dims(arr, stack):
        return (arr.shape[1], 2 * arr.shape[2]) if stack else arr.shape

    ar, ac = dims(a, a_stack)
    br, bc = dims(b, b_stack)
    M = M or ((ac if ta else ar) - a_off[1 if ta else 0])
    K = K or ((ar if ta else ac) - a_off[0 if ta else 1])
    N = N or ((br if tb else bc) - b_off[0 if tb else 1])
    tm = _pick(M, tm, 128 if ta else 16)
    tn = _pick(N // 2 if (o_stack or (b_stack and not tb)) else N, tn, 128)
    tk = _pick(K // 2 if ((a_stack and not ta) or (b_stack and tb)) else K, tk, 128)
    nk = K // tk
    ca = 0 if ta else 1
    cb = 1 if tb else 0

    def body(a_ref, b_ref, *rest):
        o_ref, acc = rest[-2:]
        k = pl.program_id(2)
        part = lax.dot_general(a_ref[...].astype(BF), b_ref[...].astype(BF),
                               (((ca,), (cb,)), ((), ())), preferred_element_type=F32)
        if nk == 1:
            o_ref[...] = part.astype(o_ref.dtype)
        else:
            @pl.when(k == 0)
            def _():
                acc[...] = part

            @pl.when(k > 0)
            def _():
                acc[...] += part

            @pl.when(k == nk - 1)
            def _():
                o_ref[...] = acc[...].astype(o_ref.dtype)

    def spec(blk, rc, off, stack, ncols):
        assert off[0] % blk[0] == 0 and off[1] % blk[1] == 0, (name, blk, off)
        ro, co = off[0] // blk[0], off[1] // blk[1]
        if not stack:
            return pl.BlockSpec(blk, lambda i, j, k: (rc(i, j, k)[0] + ro, rc(i, j, k)[1] + co))
        nhb = ncols // 2 // blk[1]
        return pl.BlockSpec((None,) + blk,
                            lambda i, j, k: ((rc(i, j, k)[1] + co) // nhb, rc(i, j, k)[0] + ro, (rc(i, j, k)[1] + co) % nhb))

    a_spec = spec((tk, tm), lambda i, j, k: (k, i), a_off, a_stack, ac) if ta else \
        spec((tm, tk), lambda i, j, k: (i, k), a_off, a_stack, ac)
    b_spec = spec((tn, tk), lambda i, j, k: (j, k), b_off, b_stack, bc) if tb else \
        spec((tk, tn), lambda i, j, k: (k, j), b_off, b_stack, bc)
    o_spec = spec((tm, tn), lambda i, j, k: (i, j), (0, 0), o_stack, N)
    o_shape = (2, M, N // 2) if o_stack else (M, N)
    deps = [] if dep is None else [dep]
    return pl.pallas_call(
        body, name=name, grid=(M // tm, N // tn, nk),
        in_specs=[a_spec, b_spec] + [pl.BlockSpec(memory_space=pl.ANY)] * len(deps),
        out_specs=o_spec, out_shape=jax.ShapeDtypeStruct(o_shape, out_dtype),
        scratch_shapes=[pltpu.VMEM((tm, tn) if nk > 1 else (8, 128), F32)],
        compiler_params=pltpu.CompilerParams(dimension_semantics=("parallel", "parallel", "arbitrary")),
    )(a, b, *deps)


def _row(width):
    return pl.BlockSpec((1, width), lambda *_: (0, 0))


NLAT = T // TC


def normmod_cat(ctx, x, g, csc, csh, sc, sh, dep, *, name, tm=256):
    assert tm == TC

    def body(c_ref, x_ref, g_ref, csc_ref, csh_ref, sc_ref, sh_ref, dep_ref, h_ref):
        last = pl.program_id(0) == NLAT
        xv = jnp.where(last, c_ref[...], x_ref[...])
        scv = jnp.where(last, csc_ref[...], sc_ref[...])
        shv = jnp.where(last, csh_ref[...], sh_ref[...])
        r = lax.rsqrt(jnp.mean(xv * xv, axis=-1, keepdims=True) + EPS)
        h_ref[...] = ((xv * r * g_ref[...]) * (1.0 + scv) + shv).astype(BF)

    return pl.pallas_call(
        body, name=name, grid=(TKV // tm,),
        in_specs=[pl.BlockSpec((tm, D), lambda i: (0, 0)), pl.BlockSpec((tm, D), lambda i: (jnp.minimum(i, NLAT - 1), 0)),
                  _row(D), _row(D), _row(D), _row(D), _row(D), pl.BlockSpec(memory_space=pl.ANY)],
        out_specs=pl.BlockSpec((tm, D), lambda i: (i, 0)), out_shape=jax.ShapeDtypeStruct((TKV, D), BF),
        compiler_params=pltpu.CompilerParams(dimension_semantics=("parallel",)),
    )(ctx, x, g, csc, csh, sc, sh, dep)


def kvprep(pc, p, kvg, wkv2, ck, sk, *, name, tm=256):
    assert tm == TC
    nb = TKV // tm
    kvcol = O_KV // 512

    def body(pc_ref, p_ref, g_ref, w_ref, ck_ref, sk_ref, k_ref, v_ref, ckv_ref):
        i = pl.program_id(0)
        t = jnp.where(i == NLAT, pc_ref[...], p_ref[...])
        pk = t[:, :KVL]
        r = lax.rsqrt(jnp.mean(pk * pk, axis=-1, keepdims=True) + EPS)
        ckv = (pk * r * g_ref[...]).astype(BF)
        ckv_ref[...] = ckv
        kv2 = jnp.dot(ckv, w_ref[...], preferred_element_type=F32)
        krr = t[:, KVL:KVL + HP] * ck_ref[...] + t[:, KVL + HP:KVL + 2 * HP] * sk_ref[...]
        k_ref[...] = (kv2[:, :NH * HP] + jnp.concatenate([krr] * NH, axis=1)).astype(BF)
        v_ref[...] = kv2[:, NH * HP:].astype(BF)

    return pl.pallas_call(
        body, name=name, grid=(nb,),
        in_specs=[pl.BlockSpec((tm, 512), lambda i: (0, 0)),
                  pl.BlockSpec((tm, 512), lambda i: (jnp.minimum(i, NLAT - 1), kvcol)),
                  _row(KVL), pl.BlockSpec((KVL, NH * HP + NH * DV), lambda i: (0, 0)),
                  pl.BlockSpec((tm, HP), lambda i: (i, 0)), pl.BlockSpec((tm, HP), lambda i: (i, 0))],
        out_specs=[pl.BlockSpec((tm, NH * HP), lambda i: (i, 0)), pl.BlockSpec((tm, NH * DV), lambda i: (i, 0)),
                   pl.BlockSpec((tm, KVL), lambda i: (i, 0))],
        out_shape=[jax.ShapeDtypeStruct((TKV, NH * HP), BF), jax.ShapeDtypeStruct((TKV, NH * DV), BF),
                   jax.ShapeDtypeStruct((TKV, KVL), BF)],
        compiler_params=pltpu.CompilerParams(dimension_semantics=("parallel",)),
    )(pc, p, kvg, wkv2, ck, sk)


def qprep(p, qg, wq2, cq_t, sq_t, *, name, tm=256):
    qcol = O_Q // 512

    def body(p_ref, g_ref, w_ref, c_ref, s_ref, q_ref, cq_ref):
        pq = p_ref[...]
        r = lax.rsqrt(jnp.sum(pq * pq, axis=-1, keepdims=True) * (1.0 / QL) + EPS)
        cq = (pq * r * g_ref[...]).astype(BF)
        cq_ref[...] = cq
        q2 = jnp.dot(cq, w_ref[...], preferred_element_type=F32)
        cc = jnp.concatenate([c_ref[...]] * NH, axis=1)
        ss = jnp.concatenate([s_ref[...]] * NH, axis=1)
        q_ref[...] = (q2[:, :NH * HP] * cc + q2[:, NH * HP:] * ss).astype(BF)

    return pl.pallas_call(
        body, name=name, grid=(T // tm,),
        in_specs=[pl.BlockSpec((tm, 512), lambda i: (i, qcol)), _row(512),
                  pl.BlockSpec((512, 2 * NH * HP), lambda i: (0, 0)),
                  pl.BlockSpec((tm, HP), lambda i: (i, 0)), pl.BlockSpec((tm, HP), lambda i: (i, 0))],
        out_specs=[pl.BlockSpec((tm, NH * HP), lambda i: (i, 0)), pl.BlockSpec((tm, 512), lambda i: (i, 0))],
        out_shape=[jax.ShapeDtypeStruct((T, NH * HP), BF), jax.ShapeDtypeStruct((T, 512), BF)],
        compiler_params=pltpu.CompilerParams(dimension_semantics=("parallel",)),
    )(p, qg, wq2, cq_t, sq_t)


def _head_mask(h):
    lanes = lax.broadcasted_iota(jnp.int32, (1, 2 * DV), 1)
    return (lanes // DV) == (h % 2)


LOG2E = 1.4426950408889634


def attn_fwd(q, k, v, *, name, tq=1024, kc=768):
    def body(q_ref, k_ref, v_ref, o_ref, lse_ref):
        h = pl.program_id(1)
        qv = q_ref[...]
        m = l = acc = None
        for c in range(TKV // kc):
            s = lax.dot_general(qv, k_ref[c * kc:(c + 1) * kc, :], (((1,), (1,)), ((), ())),
                                preferred_element_type=F32) * (SCALE * LOG2E)
            mc = jnp.max(s, axis=-1, keepdims=True)
            if c == 0:
                m = mc
                e = jnp.exp2(s - m)
                l = jnp.sum(e, axis=-1, keepdims=True)
                acc = jnp.dot(e.astype(BF), v_ref[c * kc:(c + 1) * kc, :], preferred_element_type=F32)
            else:
                mn = jnp.maximum(m, mc)
                a = jnp.exp2(m - mn)
                e = jnp.exp2(s - mn)
                l = l * a + jnp.sum(e, axis=-1, keepdims=True)
                acc = acc * a + jnp.dot(e.astype(BF), v_ref[c * kc:(c + 1) * kc, :], preferred_element_type=F32)
                m = mn
        o2 = jnp.where(_head_mask(h), acc * (1.0 / l), 0.0).astype(BF)
        lse_ref[...] = jnp.broadcast_to(m + jnp.log(l) * LOG2E, (tq, HP))

        @pl.when(h % 2 == 0)
        def _():
            o_ref[...] = o2

        @pl.when(h % 2 == 1)
        def _():
            o_ref[...] = o_ref[...] + o2

    return pl.pallas_call(
        body, name=name, grid=(T // tq, NH),
        in_specs=[pl.BlockSpec((tq, HP), lambda i, h: (i, h)), pl.BlockSpec((TKV, HP), lambda i, h: (0, h)),
                  pl.BlockSpec((TKV, 2 * DV), lambda i, h: (0, h // 2))],
        out_specs=[pl.BlockSpec((tq, 2 * DV), lambda i, h: (i, h // 2)), pl.BlockSpec((tq, HP), lambda i, h: (i, h))],
        out_shape=[jax.ShapeDtypeStruct((T, NH * DV), BF), jax.ShapeDtypeStruct((T, NH * HP), F32)],
        compiler_params=pltpu.CompilerParams(dimension_semantics=("parallel", "arbitrary")),
    )(q, k, v)


def _shift_dn(x):
    n = x.shape[0]
    rows = lax.broadcasted_iota(jnp.int32, (n, 1), 0)
    return jnp.where(rows == 0, 0.0, pltpu.roll(x, 1, axis=0))


def _shift_up(x):
    n = x.shape[0]
    rows = lax.broadcasted_iota(jnp.int32, (n, 1), 0)
    return jnp.where(rows == n - 1, 0.0, pltpu.roll(x, n - 1, axis=0))


def _conv(x, w_ref, b_ref):
    return b_ref[...] + _shift_dn(x) * w_ref[0:1, :] + x * w_ref[1:2, :] + _shift_up(x) * w_ref[2:3, :]


def _conv_t(dy, w_ref):
    return _shift_up(dy) * w_ref[0:1, :] + dy * w_ref[1:2, :] + _shift_dn(dy) * w_ref[2:3, :]


def _conv_wgrad(dw_ref, dy, x):
    dw_ref[0:1, :] = jnp.sum(dy * _shift_dn(x), axis=0, keepdims=True)
    dw_ref[1:2, :] = jnp.sum(dy * x, axis=0, keepdims=True)
    dw_ref[2:3, :] = jnp.sum(dy * _shift_up(x), axis=0, keepdims=True)


def convz(p, cw, cb, *, name):
    o0 = O_CV // (3 * CVB)

    def body(p_ref, w_ref, bias_ref, z_ref):
        xv, bv, cv = p_ref[:, 0:CVB], p_ref[:, CVB:2 * CVB], p_ref[:, 2 * CVB:3 * CVB]
        z_ref[...] = (bv * _conv(cv * xv, w_ref, bias_ref)).astype(BF)

    return pl.pallas_call(
        body, name=name, grid=(CONV // CVB,),
        in_specs=[pl.BlockSpec((T, 3 * CVB), lambda j: (0, o0 + j)), pl.BlockSpec((3, CVB), lambda j: (0, j)),
                  pl.BlockSpec((1, CVB), lambda j: (0, j))],
        out_specs=pl.BlockSpec((T, CVB), lambda j: (0, j)),
        out_shape=jax.ShapeDtypeStruct((T, CONV), BF),
        compiler_params=pltpu.CompilerParams(dimension_semantics=("parallel",)),
    )(p, cw, cb)


def out_proj_merge(o, wao, z, wco, p, *, name, tm=512):
    kin = o.shape[1]

    def body(o_ref, wa_ref, z_ref, wc_ref, ga_ref, gc_ref, ya_ref, yc_ref, m_ref):
        ya = jnp.dot(o_ref[...], wa_ref[...], preferred_element_type=F32)
        yc = jnp.dot(z_ref[...], wc_ref[...], preferred_element_type=F32)
        ya_ref[...] = ya
        yc_ref[...] = yc
        m_ref[...] = (jax.nn.sigmoid(ga_ref[...]) * ya + jax.nn.sigmoid(gc_ref[...]) * yc).astype(BF)

    blk = pl.BlockSpec((tm, D), lambda i: (i, 0))
    act = pl.BlockSpec((tm, kin), lambda i: (i, 0))
    wsp = pl.BlockSpec((kin, D), lambda i: (0, 0))
    sh = jax.ShapeDtypeStruct((T, D), F32)
    return pl.pallas_call(
        body, name=name, grid=(T // tm,),
        in_specs=[act, wsp, act, wsp, pl.BlockSpec((tm, D), lambda i: (i, O_GA // D)),
                  pl.BlockSpec((tm, D), lambda i: (i, O_GC // D))],
        out_specs=[blk, blk, blk], out_shape=[sh, sh, jax.ShapeDtypeStruct((T, D), BF)],
        compiler_params=pltpu.CompilerParams(dimension_semantics=("parallel",)),
    )(o, wao, z, wco, p, p)


CONV_HALO = 8
CONV_ROWS = 256


def _row_chunks(n, chunk, carry):
    carry = chunk(0, True, False, carry)
    carry = lax.fori_loop(1, n // CONV_ROWS - 1, lambda c, a: chunk(c * CONV_ROWS, False, False, a), carry)
    return chunk(n - CONV_ROWS, False, True, carry)


def _ext_rows(ref, r0, first, last):
    n, w = ref.shape
    zero = jnp.zeros((CONV_HALO, w), ref.dtype)
    if first:
        return jnp.concatenate([zero, ref[0:CONV_ROWS + CONV_HALO, :]], axis=0)
    if last:
        return jnp.concatenate([ref[n - CONV_ROWS - CONV_HALO:n, :], zero], axis=0)
    return ref[pl.ds(pl.multiple_of(r0 - CONV_HALO, 8), CONV_ROWS + 2 * CONV_HALO), :]


def _center_rows(r0, first, last):
    return slice(r0, r0 + CONV_ROWS) if (first or last) else pl.ds(pl.multiple_of(r0, 8), CONV_ROWS)


def _roll_dn(x):
    return pltpu.roll(x, 1, axis=0)


def _roll_up(x):
    return pltpu.roll(x, x.shape[0] - 1, axis=0)


_CTR = slice(CONV_HALO, CONV_HALO + CONV_ROWS)


def ffn_act(u0, cw, cb, *, name, tc=256):
    nb = DFF // tc

    def body(u_ref, wg_ref, wv_ref, bg_ref, bv_ref, f_ref):
        wg = [wg_ref[k:k + 1, :] for k in range(3)]
        wv = [wv_ref[k:k + 1, :] for k in range(3)]
        bg, bv = bg_ref[...], bv_ref[...]

        def chunk(r0, first, last, carry):
            xg, xv = _ext_rows(u_ref.at[0], r0, first, last), _ext_rows(u_ref.at[1], r0, first, last)
            ug = bg + _roll_dn(xg) * wg[0] + xg * wg[1] + _roll_up(xg) * wg[2]
            uv = bv + _roll_dn(xv) * wv[0] + xv * wv[1] + _roll_up(xv) * wv[2]
            f_ref[_center_rows(r0, first, last), :] = (ug * jax.nn.sigmoid(ug) * uv)[_CTR].astype(BF)
            return carry

        _row_chunks(T, chunk, 0)

    return pl.pallas_call(
        body, name=name, grid=(nb,),
        in_specs=[pl.BlockSpec((2, T, tc), lambda j: (0, 0, j)),
                  pl.BlockSpec((3, tc), lambda j: (0, j)), pl.BlockSpec((3, tc), lambda j: (0, nb + j)),
                  pl.BlockSpec((1, tc), lambda j: (0, j)), pl.BlockSpec((1, tc), lambda j: (0, nb + j))],
        out_specs=pl.BlockSpec((T, tc), lambda j: (0, j)),
        out_shape=jax.ShapeDtypeStruct((T, DFF), BF),
        compiler_params=pltpu.CompilerParams(dimension_semantics=("parallel",)),
    )(u0, cw, cw, cb, cb)


def rows_call(lead, ins, in_specs, out_shape, out_specs, fn, *, name, R, tm):
    tb, a_stack, tk = lead.get("tb", False), lead.get("a_stack", False), lead["tk"]
    K = 2 * lead["a"].shape[2] if a_stack else lead["a"].shape[1]
    nk = K // tk
    deps = [] if lead.get("dep") is None else [lead["dep"]]
    n_in = len(ins)

    def body(a_ref, b_ref, *refs):
        refs = refs[len(deps):]
        in_refs, out_refs, acc = refs[:n_in], refs[n_in:-1], refs[-1]
        i, k = pl.program_id(0), pl.program_id(1)
        part = lax.dot_general(a_ref[...].astype(BF), b_ref[...].astype(BF),
                               (((1,), (1 if tb else 0,)), ((), ())), preferred_element_type=F32)
        if nk == 1:
            fn(i, part, in_refs, out_refs)
            return

        @pl.when(k == 0)
        def _():
            acc[...] = part

        @pl.when(k > 0)
        def _():
            acc[...] += part

        @pl.when(k == nk - 1)
        def _():
            fn(i, acc[...], in_refs, out_refs)

    if a_stack:
        nhb = K // 2 // tk
        a_spec = pl.BlockSpec((None, tm, tk), lambda i, k: (k // nhb, i, k % nhb))
    else:
        a_spec = pl.BlockSpec((tm, tk), lambda i, k: (i, k))
    b_spec = pl.BlockSpec((D, tk), lambda i, k: (0, k)) if tb else pl.BlockSpec((tk, D), lambda i, k: (k, 0))
    return pl.pallas_call(
        body, name=name, grid=(R // tm, nk),
        in_specs=[a_spec, b_spec] + [pl.BlockSpec(memory_space=pl.ANY)] * len(deps) + list(in_specs),
        out_specs=out_specs, out_shape=out_shape,
        scratch_shapes=[pltpu.VMEM((tm, D) if nk > 1 else (8, 128), F32)],
        compiler_params=pltpu.CompilerParams(dimension_semantics=("arbitrary", "arbitrary")),
    )(lead["a"], lead["b"], *deps, *ins)


def _rblk(tm, w=D, col=0):
    return pl.BlockSpec((tm, w), lambda i, k: (i, col))


def _rrow(w=D):
    return pl.BlockSpec((1, w), lambda i, k: (0, 0))


def down_final(f, wdn, x1, g2, fg, tgt, *, name, tm=512):
    def fn(i, d, in_refs, out_refs):
        x1_ref, g2_ref, fg_ref, t_ref = in_refs
        d_ref, dx_ref, dd_ref, dfg_ref, loss_ref = out_refs
        d_ref[...] = d
        xv = x1_ref[...] + g2_ref[...] * d
        r = lax.rsqrt(jnp.mean(xv * xv, axis=-1, keepdims=True) + EPS)
        xh = xv * r
        diff = xh * fg_ref[...] - t_ref[...]
        part = 0.5 * jnp.sum(jnp.mean(diff * diff, axis=-1, keepdims=True), axis=0, keepdims=True)
        dy = diff * (1.0 / D)
        a = dy * fg_ref[...]
        dx = r * (a - xh * jnp.mean(a * xh, axis=-1, keepdims=True))
        dx_ref[...] = dx
        dd_ref[...] = (dx * g2_ref[...]).astype(BF)
        dfg = jnp.sum(dy * xh, axis=0, keepdims=True)

        @pl.when(i == 0)
        def _():
            dfg_ref[...] = dfg
            loss_ref[...] = jnp.broadcast_to(part, (1, 128))

        @pl.when(i > 0)
        def _():
            dfg_ref[...] += dfg
            loss_ref[...] += jnp.broadcast_to(part, (1, 128))

    blk = _rblk(tm)
    return rows_call(
        dict(a=f, b=wdn, tk=DFF), [x1, g2, fg, tgt], [blk, _rrow(), _rrow(), blk],
        [jax.ShapeDtypeStruct((T, D), F32), jax.ShapeDtypeStruct((T, D), F32), jax.ShapeDtypeStruct((T, D), BF),
         jax.ShapeDtypeStruct((1, D), F32), jax.ShapeDtypeStruct((1, 128), F32)],
        [blk, blk, blk, _rrow(), _rrow(128)], fn, name=name, R=T, tm=tm)


def oproj_resid(merged, wo, x, gate, g, sc, sh, *, name, tm=512):
    def fn(i, a, in_refs, out_refs):
        x_ref, gate_ref, g_ref, sc_ref, sh_ref = in_refs
        a_ref, x1_ref, h_ref = out_refs
        a_ref[...] = a
        xv = x_ref[...] + gate_ref[...] * a
        x1_ref[...] = xv
        r = lax.rsqrt(jnp.mean(xv * xv, axis=-1, keepdims=True) + EPS)
        h_ref[...] = ((xv * r * g_ref[...]) * (1.0 + sc_ref[...]) + sh_ref[...]).astype(BF)

    blk = _rblk(tm)
    return rows_call(
        dict(a=merged, b=wo, tk=D), [x, gate, g, sc, sh], [blk, _rrow(), _rrow(), _rrow(), _rrow()],
        [jax.ShapeDtypeStruct((T, D), F32), jax.ShapeDtypeStruct((T, D), F32), jax.ShapeDtypeStruct((T, D), BF)],
        [blk, blk, blk], fn, name=name, R=T, tm=tm)


def oproj_dx_gate_bwd(da, wo, p, ya, yc, wao, wco, *, name, tm=512):
    kin = wao.shape[0]

    def fn(i, dm, in_refs, out_refs):
        ga_ref, gc_ref, ya_ref, yc_ref, wa_ref, wc_ref = in_refs
        dya_ref, dyc_ref, dp_ref, do_ref, dz_ref = out_refs
        sa, sc_ = jax.nn.sigmoid(ga_ref[...]), jax.nn.sigmoid(gc_ref[...])
        dya, dyc = (dm * sa).astype(BF), (dm * sc_).astype(BF)
        dya_ref[...] = dya
        dyc_ref[...] = dyc
        dp_ref[:, 0:D] = (dm * ya_ref[...] * (sa * (1.0 - sa))).astype(BF)
        dp_ref[:, D:2 * D] = (dm * yc_ref[...] * (sc_ * (1.0 - sc_))).astype(BF)
        nt = (((1,), (1,)), ((), ()))
        do_ref[...] = lax.dot_general(dya, wa_ref[...], nt, preferred_element_type=F32).astype(BF)
        dz_ref[...] = lax.dot_general(dyc, wc_ref[...], nt, preferred_element_type=F32)

    blk = _rblk(tm)
    sh = jax.ShapeDtypeStruct((T, D), BF)
    wsp = pl.BlockSpec((kin, D), lambda i, k: (0, 0))
    return rows_call(
        dict(a=da, b=wo, tb=True, tk=D), [p, p, ya, yc, wao, wco],
        [_rblk(tm, D, O_GA // D), _rblk(tm, D, O_GC // D), blk, blk, wsp, wsp],
        [sh, sh, jax.ShapeDtypeStruct((T, NIN), BF), jax.ShapeDtypeStruct((T, kin), BF), jax.ShapeDtypeStruct((T, kin), F32)],
        [blk, blk, _rblk(tm, 2 * D), _rblk(tm, kin), _rblk(tm, kin)], fn, name=name, R=T, tm=tm)


def normmod_bwd(x, dh, g, sc, dres, gsrc, gate, *, name, tm=512):
    R = x.shape[0]
    tm = min(tm, R)
    has_res = dres is not None
    fused = isinstance(dh, dict)
    if fused:
        tb, a_stack, tk = dh.get("tb", False), dh.get("a_stack", False), dh["tk"]
        K = 2 * dh["a"].shape[2] if a_stack else dh["a"].shape[1]
        nk = K // tk
        deps = [] if dh.get("dep") is None else [dh["dep"]]
        n_dh = 2 + len(deps)
    else:
        nk, n_dh = 1, 1

    def elementwise(i, dhv, x_ref, g_ref, sc_ref, res_refs, out_refs):
        xv = x_ref[...]
        r = lax.rsqrt(jnp.mean(xv * xv, axis=-1, keepdims=True) + EPS)
        xh = xv * r
        n = xh * g_ref[...]
        dn = dhv * (1.0 + sc_ref[...])
        a = dn * g_ref[...]
        rows = [jnp.sum(dhv, axis=0, keepdims=True), jnp.sum(dhv * n, axis=0, keepdims=True),
                jnp.sum(dn * xh, axis=0, keepdims=True)]
        if has_res:
            dres_ref, gsrc_ref, gate_ref = res_refs
            dx_ref, dxg_ref, st_ref = out_refs
            dr = dres_ref[...]
            dx = dr + r * (a - xh * jnp.mean(a * xh, axis=-1, keepdims=True))
            dx_ref[...] = dx
            dxg_ref[...] = (dx * gate_ref[...]).astype(BF)
            rows.append(jnp.sum(dr * gsrc_ref[...], axis=0, keepdims=True))
        else:
            st_ref, = out_refs
            rows.append(jnp.zeros((1, D), F32))

        @pl.when(i == 0)
        def _():
            for k, row in enumerate(rows):
                st_ref[k:k + 1, :] = row

        @pl.when(i > 0)
        def _():
            for k, row in enumerate(rows):
                st_ref[k:k + 1, :] += row

    def body(*refs):
        x_ref, dh_refs, g_ref, sc_ref = refs[0], refs[1:1 + n_dh], refs[1 + n_dh], refs[2 + n_dh]
        rest = refs[3 + n_dh:]
        res_refs, rest = (rest[:3], rest[3:]) if has_res else ((), rest)
        out_refs = rest[:3] if has_res else rest[:1]
        i = pl.program_id(0)
        if not fused:
            elementwise(i, dh_refs[0][...], x_ref, g_ref, sc_ref, res_refs, out_refs)
            return
        acc = rest[-1]
        k = pl.program_id(1)
        part = lax.dot_general(dh_refs[0][...].astype(BF), dh_refs[1][...].astype(BF),
                               (((1,), (1 if tb else 0,)), ((), ())), preferred_element_type=F32)
        if nk == 1:
            elementwise(i, part, x_ref, g_ref, sc_ref, res_refs, out_refs)
            return

        @pl.when(k == 0)
        def _():
            acc[...] = part

        @pl.when(k > 0)
        def _():
            acc[...] += part

        @pl.when(k == nk - 1)
        def _():
            elementwise(i, acc[...], x_ref, g_ref, sc_ref, res_refs, out_refs)

    rowb = lambda w: pl.BlockSpec((1, w), lambda i, *k: (0, 0))
    blk = pl.BlockSpec((tm, D), lambda i, *k: (i, 0))
    st_spec = pl.BlockSpec((4, D), lambda i, *k: (0, 0))
    st_shape = jax.ShapeDtypeStruct((4, D), F32)
    if fused:
        if a_stack:
            nhb = K // 2 // tk
            a_spec = pl.BlockSpec((None, tm, tk), lambda i, k: (k // nhb, i, k % nhb))
        else:
            a_spec = pl.BlockSpec((tm, tk), lambda i, k: (i, k))
        b_spec = pl.BlockSpec((D, tk), lambda i, k: (0, k)) if tb else pl.BlockSpec((tk, D), lambda i, k: (k, 0))
        dh_specs = [a_spec, b_spec] + [pl.BlockSpec(memory_space=pl.ANY)] * len(deps)
        dh_args = [dh["a"], dh["b"]] + deps
        grid, sem = (R // tm, nk), ("arbitrary", "arbitrary")
        scratch = [pltpu.VMEM((tm, D) if nk > 1 else (8, 128), F32)]
    else:
        dh_specs, dh_args, grid, sem, scratch = [blk], [dh], (R // tm,), ("arbitrary",), []
    cp = pltpu.CompilerParams(dimension_semantics=sem)
    if has_res:
        return pl.pallas_call(
            body, name=name, grid=grid, in_specs=[blk] + dh_specs + [rowb(D), rowb(D), blk, blk, rowb(D)],
            out_specs=[blk, blk, st_spec], scratch_shapes=scratch,
            out_shape=[jax.ShapeDtypeStruct((R, D), F32), jax.ShapeDtypeStruct((R, D), BF), st_shape],
            compiler_params=cp,
        )(x, *dh_args, g, sc, dres, gsrc, gate)
    return pl.pallas_call(
        body, name=name, grid=grid, in_specs=[blk] + dh_specs + [rowb(D), rowb(D)],
        out_specs=st_spec, out_shape=st_shape, scratch_shapes=scratch, compiler_params=cp,
    )(x, *dh_args, g, sc)


def ffn_act_bwd(u0, df, cw, cb, *, name, tc=128):
    nb = DFF // tc

    def body(u_ref, df_ref, wg_ref, wv_ref, bg_ref, bv_ref, du_ref, dw_ref, db_ref):
        wg = [wg_ref[k:k + 1, :] for k in range(3)]
        wv = [wv_ref[k:k + 1, :] for k in range(3)]
        bg, bv = bg_ref[...], bv_ref[...]

        def chunk(r0, first, last, acc):
            xg, xv = _ext_rows(u_ref.at[0], r0, first, last), _ext_rows(u_ref.at[1], r0, first, last)
            dfe = _ext_rows(df_ref, r0, first, last)
            xg_d, xg_u, xv_d, xv_u = _roll_dn(xg), _roll_up(xg), _roll_dn(xv), _roll_up(xv)
            ug = bg + xg_d * wg[0] + xg * wg[1] + xg_u * wg[2]
            uv = bv + xv_d * wv[0] + xv * wv[1] + xv_u * wv[2]
            sig = jax.nn.sigmoid(ug)
            dug = dfe * uv * (sig * (1.0 + ug * (1.0 - sig)))
            duv = dfe * (ug * sig)
            rows = _center_rows(r0, first, last)
            du_ref[0, rows, :] = (_roll_up(dug) * wg[0] + dug * wg[1] + _roll_dn(dug) * wg[2])[_CTR].astype(BF)
            du_ref[1, rows, :] = (_roll_up(duv) * wv[0] + duv * wv[1] + _roll_dn(duv) * wv[2])[_CTR].astype(BF)
            terms = [dug * xg_d, dug * xg, dug * xg_u, dug, duv * xv_d, duv * xv, duv * xv_u, duv]
            return tuple(a + jnp.sum(t[_CTR], axis=0, keepdims=True) for a, t in zip(acc, terms))

        acc = _row_chunks(T, chunk, tuple(jnp.zeros((1, tc), F32) for _ in range(8)))
        for k in range(3):
            dw_ref[0, k:k + 1, :] = acc[k]
            dw_ref[1, k:k + 1, :] = acc[4 + k]
        db_ref[0] = acc[3]
        db_ref[1] = acc[7]

    lo = lambda r: pl.BlockSpec((r, tc), lambda j: (0, j))
    hi = lambda r: pl.BlockSpec((r, tc), lambda j: (0, nb + j))
    st = lambda r: pl.BlockSpec((2, r, tc), lambda j: (0, 0, j))
    return pl.pallas_call(
        body, name=name, grid=(nb,),
        in_specs=[st(T), lo(T), lo(3), hi(3), lo(1), hi(1)],
        out_specs=[st(T), st(3), st(1)],
        out_shape=[jax.ShapeDtypeStruct((2, T, DFF), BF), jax.ShapeDtypeStruct((2, 3, DFF), F32),
                   jax.ShapeDtypeStruct((2, 1, DFF), F32)],
        compiler_params=pltpu.CompilerParams(dimension_semantics=("parallel",)),
    )(u0, df, cw, cw, cb, cb)


def convz_bwd(p, dz, cw, cb, dp, *, name):
    o0 = O_CV // (3 * CVB)

    def body(p_ref, dz_ref, w_ref, bias_ref, dp_in, dp_ref, dw_ref, dbias_ref):
        xv, bv, cv = p_ref[:, 0:CVB], p_ref[:, CVB:2 * CVB], p_ref[:, 2 * CVB:3 * CVB]
        ci = cv * xv
        dwc = _conv(ci, w_ref, bias_ref)
        dzv = dz_ref[...]
        ddw = dzv * bv
        dci = _conv_t(ddw, w_ref)
        dp_ref[:, 0:CVB] = (dci * cv).astype(BF)
        dp_ref[:, CVB:2 * CVB] = (dzv * dwc).astype(BF)
        dp_ref[:, 2 * CVB:3 * CVB] = (dci * xv).astype(BF)
        _conv_wgrad(dw_ref, ddw, ci)
        dbias_ref[...] = jnp.sum(ddw, axis=0, keepdims=True)

    own = lambda r: pl.BlockSpec((r, CVB), lambda j: (0, j))
    return pl.pallas_call(
        body, name=name, grid=(CONV // CVB,),
        in_specs=[pl.BlockSpec((T, 3 * CVB), lambda j: (0, o0 + j)), own(T), own(3), own(1),
                  pl.BlockSpec(memory_space=pl.ANY)],
        out_specs=[pl.BlockSpec((T, 3 * CVB), lambda j: (0, o0 + j)), own(3), own(1)],
        out_shape=[jax.ShapeDtypeStruct((T, NIN), BF), jax.ShapeDtypeStruct((3, CONV), F32),
                   jax.ShapeDtypeStruct((1, CONV), F32)],
        input_output_aliases={4: 0},
        compiler_params=pltpu.CompilerParams(dimension_semantics=("parallel",)),
    )(p, dz, cw, cb, dp)


def attn_bwd(q, k, v, do, o, lse, dep, *, name, tq=1024, kc=768):
    NKC, KC = TKV // kc, kc
    deps = [] if dep is None else [dep]

    def body(q_ref, k_ref, v_ref, do_ref, o_ref, lse_ref, *rest):
        dq_ref, dk_ref, dv_ref = rest[len(deps):]
        h, i = pl.program_id(0), pl.program_id(1)

        @pl.when(i == 0)
        def _():
            dk_ref[...] = jnp.zeros_like(dk_ref)

        @pl.when((i == 0) & (h % 2 == 0))
        def _():
            dv_ref[...] = jnp.zeros_like(dv_ref)

        qv = q_ref[...]
        dom = jnp.where(_head_mask(h), do_ref[...], jnp.zeros_like(do_ref[...]))
        delta = jnp.sum(dom.astype(F32) * o_ref[...].astype(F32), axis=-1, keepdims=True)
        lse = lse_ref[:, 0:1]
        dq = jnp.zeros((tq, HP), F32)
        for c in range(NKC):
            cols = slice(c * KC, (c + 1) * KC)
            s = lax.dot_general(qv, k_ref[cols, :], (((1,), (1,)), ((), ())),
                                preferred_element_type=F32) * (SCALE * LOG2E)
            pr = jnp.exp2(s - lse)
            dp = lax.dot_general(dom, v_ref[cols, :], (((1,), (1,)), ((), ())), preferred_element_type=F32)
            ds = (pr * (dp - delta) * SCALE).astype(BF)
            dq = dq + jnp.dot(ds, k_ref[cols, :], preferred_element_type=F32)
            dk_ref[cols, :] += lax.dot_general(ds, qv, (((0,), (0,)), ((), ())), preferred_element_type=F32)
            dv_ref[cols, :] += lax.dot_general(pr.astype(BF), dom, (((0,), (0,)), ((), ())), preferred_element_type=F32)
        dq_ref[...] = dq

    return pl.pallas_call(
        body, name=name, grid=(NH, T // tq),
        in_specs=[pl.BlockSpec((tq, HP), lambda h, i: (i, h)), pl.BlockSpec((TKV, HP), lambda h, i: (0, h)),
                  pl.BlockSpec((TKV, 2 * DV), lambda h, i: (0, h // 2)), pl.BlockSpec((tq, 2 * DV), lambda h, i: (i, h // 2)),
                  pl.BlockSpec((tq, 2 * DV), lambda h, i: (i, h // 2)), pl.BlockSpec((tq, HP), lambda h, i: (i, h)),
                  *([pl.BlockSpec(memory_space=pl.ANY)] * len(deps))],
        out_specs=[pl.BlockSpec((tq, HP), lambda h, i: (i, h)), pl.BlockSpec((TKV, HP), lambda h, i: (0, h)),
                   pl.BlockSpec((TKV, 2 * DV), lambda h, i: (0, h // 2))],
        out_shape=[jax.ShapeDtypeStruct((T, NH * HP), F32), jax.ShapeDtypeStruct((TKV, NH * HP), F32),
                   jax.ShapeDtypeStruct((TKV, NH * DV), F32)],
        compiler_params=pltpu.CompilerParams(dimension_semantics=("arbitrary", "arbitrary")),
    )(q, k, v, do, o, lse, *deps)


def qprep_bwd(p, dq, qg, wq2, cq_t, sq_t, dp, *, name, tm=256):
    qcol = O_Q // 512

    def body(p_ref, dq_ref, g_ref, w_ref, c_ref, s_ref, dp_in, dp_ref, dq2_ref, dg_ref):
        i = pl.program_id(0)
        dqv = dq_ref[...]
        cc = jnp.concatenate([c_ref[...]] * NH, axis=1)
        ss = jnp.concatenate([s_ref[...]] * NH, axis=1)
        dq2 = jnp.concatenate([dqv * cc, dqv * ss], axis=1).astype(BF)
        dq2_ref[...] = dq2
        dcq = lax.dot_general(dq2, w_ref[...], (((1,), (1,)), ((), ())), preferred_element_type=F32)
        pq = p_ref[...]
        r = lax.rsqrt(jnp.sum(pq * pq, axis=-1, keepdims=True) * (1.0 / QL) + EPS)
        xh = pq * r
        a = dcq * g_ref[...]
        dp_ref[...] = (r * (a - xh * (jnp.sum(a * xh, axis=-1, keepdims=True) * (1.0 / QL)))).astype(BF)
        dg = jnp.sum(dcq * xh, axis=0, keepdims=True)

        @pl.when(i == 0)
        def _():
            dg_ref[...] = dg

        @pl.when(i > 0)
        def _():
            dg_ref[...] += dg

    return pl.pallas_call(
        body, name=name, grid=(T // tm,),
        in_specs=[pl.BlockSpec((tm, 512), lambda i: (i, qcol)), pl.BlockSpec((tm, NH * HP), lambda i: (i, 0)), _row(512),
                  pl.BlockSpec((512, 2 * NH * HP), lambda i: (0, 0)),
                  pl.BlockSpec((tm, HP), lambda i: (i, 0)), pl.BlockSpec((tm, HP), lambda i: (i, 0)),
                  pl.BlockSpec(memory_space=pl.ANY)],
        out_specs=[pl.BlockSpec((tm, 512), lambda i: (i, qcol)), pl.BlockSpec((tm, 2 * NH * HP), lambda i: (i, 0)), _row(512)],
        out_shape=[jax.ShapeDtypeStruct((T, NIN), BF), jax.ShapeDtypeStruct((T, 2 * NH * HP), BF),
                   jax.ShapeDtypeStruct((1, 512), F32)],
        input_output_aliases={6: 0},
        compiler_params=pltpu.CompilerParams(dimension_semantics=("arbitrary",)),
    )(p, dq, qg, wq2, cq_t, sq_t, dp)


def kvprep_bwd(pc, p, dk, dv, kvg, wkv2, ck, sk, dp, *, name, tm=256):
    assert tm == TC
    nb = TKV // tm
    kvcol = O_KV // 512

    def body(pc_ref, p_ref, dk_ref, dv_ref, g_ref, w_ref, ck_ref, sk_ref, dp_in, dp_ref, dpc_ref, dkv2_ref, dg_ref):
        i = pl.program_id(0)
        t = jnp.where(i == NLAT, pc_ref[...], p_ref[...])
        pk = t[:, :KVL]
        r = lax.rsqrt(jnp.mean(pk * pk, axis=-1, keepdims=True) + EPS)
        xh = pk * r
        dkv = dk_ref[...]
        dkv2 = jnp.concatenate([dkv, dv_ref[...]], axis=1).astype(BF)
        dkv2_ref[...] = dkv2
        dckv = lax.dot_general(dkv2, w_ref[...], (((1,), (1,)), ((), ())), preferred_element_type=F32)
        a = dckv * g_ref[...]
        dpk = r * (a - xh * jnp.mean(a * xh, axis=-1, keepdims=True))
        dkr = dkv[:, 0:HP]
        for hh in range(1, NH):
            dkr = dkr + dkv[:, hh * HP:(hh + 1) * HP]
        res = jnp.concatenate([dpk, dkr * ck_ref[...], dkr * sk_ref[...]], axis=1).astype(BF)
        dg = jnp.sum(dckv * xh, axis=0, keepdims=True)

        @pl.when(i == 0)
        def _():
            dg_ref[...] = dg

        @pl.when(i > 0)
        def _():
            dg_ref[...] += dg

        @pl.when(i < NLAT)
        def _():
            dp_ref[...] = res

        @pl.when(i == NLAT)
        def _():
            dpc_ref[...] = res

    rb = lambda w: pl.BlockSpec((tm, w), lambda i: (i, 0))
    return pl.pallas_call(
        body, name=name, grid=(nb,),
        in_specs=[pl.BlockSpec((tm, 512), lambda i: (0, 0)),
                  pl.BlockSpec((tm, 512), lambda i: (jnp.minimum(i, NLAT - 1), kvcol)),
                  rb(NH * HP), rb(NH * DV), _row(KVL), pl.BlockSpec((KVL, NH * HP + NH * DV), lambda i: (0, 0)),
                  rb(HP), rb(HP), pl.BlockSpec(memory_space=pl.ANY)],
        out_specs=[pl.BlockSpec((tm, 512), lambda i: (jnp.minimum(i, NLAT - 1), kvcol)),
                   pl.BlockSpec((tm, 512), lambda i: (0, 0)), rb(NH * HP + NH * DV), _row(KVL)],
        out_shape=[jax.ShapeDtypeStruct((T, NIN), BF), jax.ShapeDtypeStruct((TC, 512), BF),
                   jax.ShapeDtypeStruct((TKV, NH * HP + NH * DV), BF), jax.ShapeDtypeStruct((1, KVL), F32)],
        input_output_aliases={8: 0},
        compiler_params=pltpu.CompilerParams(dimension_semantics=("arbitrary",)),
    )(pc, p, dk, dv, kvg, wkv2, ck, sk, dp)


def _pieces(src, width, n):
    out, c = [], src
    while c < src + width:
        k = c // n
        w = min(src + width, (k + 1) * n) - c
        out.append((k, c - k * n, c - src, w))
        c += w
    return out


def _win_moves():
    mv = [(2208, 1024, O_GA), (3232, 1024, O_GC), (0, KVL, O_KV), (256, DR, O_KV + KVL + DN), (288, QL, O_Q)]
    mv += [(256 + _swap_start(g), 8, O_KV + KVL + HP + DN + 8 * g) for g in range(4)]
    for j in range(CONV // CVB):
        base = O_CV + 3 * CVB * j
        mv += [(672 + CVB * j, CVB, base), (1184 + CVB * j, CVB, base + CVB), (1696 + CVB * j, CVB, base + 2 * CVB)]
    return mv


_WIN_ZERO = [(O_KV + KVL, DN), (O_KV + KVL + DN + DR, HP - DN - DR), (O_KV + KVL + HP, DN),
             (O_KV + KVL + HP + DN + DR, HP - DN - DR), (O_Q + QL, 512 - QL)]


def build_win(g, *, name, tm=256):
    def body(g_ref, o_ref):
        for src, w, dst in _win_moves():
            for k, a, off, pw in _pieces(src, w, SH_IN):
                o_ref[:, dst + off:dst + off + pw] = g_ref[k, :, a:a + pw]
        for c0, w in _WIN_ZERO:
            o_ref[:, c0:c0 + w] = jnp.zeros((tm, w), o_ref.dtype)

    return pl.pallas_call(
        body, name=name, grid=(D // tm,), in_specs=[pl.BlockSpec((NDEV, tm, SH_IN), lambda i: (0, i, 0))],
        out_specs=pl.BlockSpec((tm, NIN), lambda i: (i, 0)), out_shape=jax.ShapeDtypeStruct((D, NIN), g.dtype),
        compiler_params=pltpu.CompilerParams(dimension_semantics=("parallel",)),
    )(g)


def shard_win_grad(dwt, dwct, *, name, col0=0, tc=256):
    n = dwt.shape[1]

    def body(dw_ref, dwc_ref, o_ref, kvs):
        kvs[...] = dw_ref[O_KV:O_KV + 512, :] + dwc_ref[...]

        def src(row, w):
            if O_KV <= row < O_KV + 512:
                return kvs[row - O_KV:row - O_KV + w, :]
            return dw_ref[row:row + w, :]

        for s, w, dst in _win_moves():
            if w == 8 or s == 256:
                continue
            for k, a, off, pw in _pieces(s, w, SH_IN):
                o_ref[k, a:a + pw, :] = src(dst + off, pw).astype(o_ref.dtype)
        for g in range(4):
            val = src(O_KV + KVL + DN + 8 * g, 8) + src(O_KV + KVL + HP + DN + _swap_start(g), 8)
            o_ref[0, 256 + 8 * g:256 + 8 * g + 8, :] = val.astype(o_ref.dtype)

    return pl.pallas_call(
        body, name=name, grid=(n // tc,),
        in_specs=[pl.BlockSpec((NIN, tc), lambda j: (0, j)), pl.BlockSpec((512, tc), lambda j: (0, j + col0 // tc))],
        out_specs=pl.BlockSpec((NDEV, SH_IN, tc), lambda j: (0, 0, j)),
        out_shape=jax.ShapeDtypeStruct((NDEV, SH_IN, n), BF),
        scratch_shapes=[pltpu.VMEM((512, tc), F32)],
        compiler_params=pltpu.CompilerParams(dimension_semantics=("parallel",)),
    )(dwt, dwct)


def _eye(n, m):
    return (lax.broadcasted_iota(jnp.int32, (n, m), 0) == lax.broadcasted_iota(jnp.int32, (n, m), 1)).astype(BF)


_NT = (((1,), (1,)), ((), ()))


def build_wq_wkv(gq, gkv, *, name):
    def body(gq_ref, gkv_ref, q_ref, kv_ref):
        q_ref[...] = jnp.zeros_like(q_ref)
        kv_ref[...] = jnp.zeros_like(kv_ref)
        eye = _eye(QL, QL)
        for h in range(NH):
            qh = lax.dot_general(eye, gq_ref[h], _NT, preferred_element_type=F32).astype(q_ref.dtype)
            q_ref[0:QL, h * HP:h * HP + DN + DR] = qh
            for g in range(4):
                c0 = NH * HP + h * HP + DN + 8 * g
                q_ref[0:QL, c0:c0 + 8] = qh[:, DN + _swap_start(g):DN + _swap_start(g) + 8]
            kv_ref[:, h * HP:h * HP + DN] = gkv_ref[h, :, 0:DN]
            kv_ref[:, NH * HP + h * DV:NH * HP + (h + 1) * DV] = gkv_ref[h, :, DN:DN + DV]

    vm = pl.BlockSpec(memory_space=pltpu.VMEM)
    return pl.pallas_call(
        body, name=name, in_specs=[vm, vm], out_specs=[vm, vm],
        out_shape=[jax.ShapeDtypeStruct((512, 2 * NH * HP), gq.dtype), jax.ShapeDtypeStruct((KVL, NH * HP + NH * DV), gq.dtype)],
    )(gq, gkv)


def shard_wq_wkv_grad(dwq2, dwkv2, *, name):
    def body(q_ref, kv_ref, gq_ref, gkv_ref, xs):
        xs[...] = jnp.zeros_like(xs)
        eye = _eye(DN + DR, HP)
        for h in range(NH):
            xs[:, 0:DN] = q_ref[0:QL, h * HP:h * HP + DN].astype(BF)
            for g in range(4):
                a = q_ref[0:QL, h * HP + DN + 8 * g:h * HP + DN + 8 * g + 8]
                c0 = NH * HP + h * HP + DN + _swap_start(g)
                xs[:, DN + 8 * g:DN + 8 * g + 8] = (a + q_ref[0:QL, c0:c0 + 8]).astype(BF)
            gq_ref[h] = lax.dot_general(eye, xs[...], _NT, preferred_element_type=F32).astype(BF)
            gkv_ref[h, :, 0:DN] = kv_ref[:, h * HP:h * HP + DN].astype(BF)
            gkv_ref[h, :, DN:DN + DV] = kv_ref[:, NH * HP + h * DV:NH * HP + (h + 1) * DV].astype(BF)

    vm = pl.BlockSpec(memory_space=pltpu.VMEM)
    return pl.pallas_call(
        body, name=name, in_specs=[vm, vm], out_specs=[vm, vm],
        out_shape=[jax.ShapeDtypeStruct((NDEV, DN + DR, QL), BF), jax.ShapeDtypeStruct((NDEV, KVL, DN + DV), BF)],
        scratch_shapes=[pltpu.VMEM((QL, HP), BF)],
    )(dwq2, dwkv2)


def unshard_cols(g, *, name, tm=256):
    _, K, n = g.shape
    tm = _pick(K, tm, 16)

    def body(g_ref, o_ref):
        for k in range(NDEV):
            o_ref[:, k * n:(k + 1) * n] = g_ref[k]

    return pl.pallas_call(
        body, name=name, grid=(K // tm,), in_specs=[pl.BlockSpec((NDEV, tm, n), lambda i: (0, i, 0))],
        out_specs=pl.BlockSpec((tm, NDEV * n), lambda i: (i, 0)), out_shape=jax.ShapeDtypeStruct((K, NDEV * n), g.dtype),
        compiler_params=pltpu.CompilerParams(dimension_semantics=("parallel",)),
    )(g)


def shard_cols(w, *, name, tm=256):
    K, n8 = w.shape
    n = n8 // NDEV
    tm = _pick(K, tm, 16)

    def body(w_ref, o_ref):
        for k in range(NDEV):
            o_ref[k] = w_ref[:, k * n:(k + 1) * n]

    return pl.pallas_call(
        body, name=name, grid=(K // tm,), in_specs=[pl.BlockSpec((tm, n8), lambda i: (i, 0))],
        out_specs=pl.BlockSpec((NDEV, tm, n), lambda i: (0, i, 0)), out_shape=jax.ShapeDtypeStruct((NDEV, K, n), w.dtype),
        compiler_params=pltpu.CompilerParams(dimension_semantics=("parallel",)),
    )(w)


def _rope_tables():
    t = np.arange(T)
    row = (t // GRID_W).astype(np.float32)
    col = (t % GRID_W).astype(np.float32)
    axis_dim = DR // 2
    inv = (np.float32(ROPE_THETA) ** (-np.arange(0, axis_dim, 2, dtype=np.float32) / np.float32(axis_dim))).astype(np.float32)
    ar, ac = (row[:, None] * inv).astype(np.float32), (col[:, None] * inv).astype(np.float32)
    cosv = np.concatenate([np.cos(ar), np.cos(ar), np.cos(ac), np.cos(ac)], axis=1).astype(np.float32)
    sinv = np.concatenate([-np.sin(ar), np.sin(ar), -np.sin(ac), np.sin(ac)], axis=1).astype(np.float32)
    ck = np.zeros((TKV, HP), np.float32)
    sk = np.zeros((TKV, HP), np.float32)
    ck[T:, DN:DN + DR] = 1.0
    ck[:T, DN:DN + DR] = cosv
    sk[:T, DN:DN + DR] = sinv
    cq = np.zeros((T, HP), np.float32)
    cq[:, :DN] = 1.0
    cq[:, DN:DN + DR] = cosv
    return jnp.asarray(ck), jnp.asarray(sk), jnp.asarray(cq), jnp.asarray(sk[:T])


def _local_step(x, ctx, tgt, mod_lat, mod_ctx, n1g, qg, kvg, n2g, fg, conv_w, conv_b, ffn_w, ffn_b, get_w, put_g, dep0):
    sh1, sc1, g1, sh2, sc2, g2 = [mod_lat[:, i * D:(i + 1) * D] for i in range(6)]
    csh1, csc1 = mod_ctx[:, 0:D], mod_ctx[:, D:2 * D]
    ck, sk, cq_t, sq_t = _rope_tables()
    qg_p = jnp.pad(qg, ((0, 0), (0, 512 - QL)))

    hcat = normmod_cat(ctx, x, n1g, csc1, csh1, sc1, sh1, dep0, name="normmod1")
    win = get_w("in", hcat)
    p = mm(hcat, win, M=T, tn=768, name="in_proj")
    pc = mm(hcat, win, M=TC, N=512, a_off=(T, 0), b_off=(0, O_KV), name="in_proj_ctx")
    wq2, wkv2, wao, wco, wo = get_w("mid", p)
    kh, vh, ckv = kvprep(pc, p, kvg, wkv2, ck, sk, name="kvprep")
    qr, cq = qprep(p, qg_p, wq2, cq_t, sq_t, name="qprep")
    o, lse = attn_fwd(qr, kh, vh, name="attn_fwd")
    z = convz(p, conv_w, conv_b, name="convz")
    ya, yc, merged = out_proj_merge(o, wao, z, wco, p, name="attn_conv_out_gate_merge")
    a_out, x1, h2 = oproj_resid(merged, wo, x, g1, n2g, sc2, sh2, name="o_proj_resid_normmod2")
    wup = get_w("up", h2)
    u0 = mm(h2, wup, tb=True, o_stack=True, tn=1408, name="up_proj")
    f = ffn_act(u0, ffn_w, ffn_b, name="ffn_act")
    wdn = get_w("down", f)
    dn, dx2, dd, dfg, loss = down_final(f, wdn, x1, g2, fg, tgt, name="down_proj_final_loss")

    df = mm(dd, wdn, tb=True, tn=1408, name="down_proj_dx")
    dwdn = mm(f, dd, ta=True, out_dtype=BF, tm=1408, name="down_proj_dw")
    du0, dffn_w, dffn_b = ffn_act_bwd(u0, df, ffn_w, ffn_b, name="ffn_act_bwd")
    dwup = mm(du0, h2, ta=True, a_stack=True, out_dtype=BF, tm=1408, name="up_proj_dw")
    tok = put_g("ffn", dict(dwup=dwup, dwdn=dwdn))
    dx1, da, st2 = normmod_bwd(x1, dict(a=du0, b=wup, a_stack=True, tk=DFF, dep=tok), n2g, sc2, dx2, dn, g1,
                               name="up_proj_dx_normmod2_bwd")

    dwo = mm(merged, da, ta=True, out_dtype=BF, tn=512, name="o_proj_dw")
    dya, dyc, dp, do, dz = oproj_dx_gate_bwd(da, wo, p, ya, yc, wao, wco, name="o_proj_dx_gate_merge_bwd")
    dwao = mm(o, dya, ta=True, out_dtype=BF, tn=512, name="attn_out_dw")
    dwco = mm(z, dyc, ta=True, out_dtype=BF, tn=512, name="conv_out_dw")
    tok = put_g("mid", dict(dwao=dwao, dwco=dwco, dwo=dwo))
    dp, dconv_w, dconv_b = convz_bwd(p, dz, conv_w, conv_b, dp, name="convz_bwd")
    dq, dk, dv = attn_bwd(qr, kh, vh, do, o, lse, tok, name="attn_bwd")
    dp, dq2, dqg = qprep_bwd(p, dq, qg_p, wq2, cq_t, sq_t, dp, name="qprep_bwd")
    dp, dpc, dkv2, dkvg = kvprep_bwd(pc, p, dk, dv, kvg, wkv2, ck, sk, dp, name="kvprep_bwd")

    dwin_c = mm(dpc, hcat, ta=True, K=TC, b_off=(T, 0), name="in_proj_ctx_dw")
    tok = None
    for half in range(2):
        dwin = mm(dp, hcat, ta=True, K=T, N=D // 2, b_off=(0, half * (D // 2)), tm=768, dep=tok,
                  name=f"in_proj_dw_{half}")
        tok = put_g(f"in{half}", dict(dwin=dwin, dwin_c=dwin_c, col0=half * (D // 2)))
    dwq2 = mm(cq, dq2, ta=True, dep=tok, name="q_up_dw")
    dwkv2 = mm(ckv, dkv2, ta=True, dep=tok, name="kv_up_dw")
    tok = put_g("qkv", dict(dwq2=dwq2, dwkv2=dwkv2))
    dhc = mm(dpc, win, tb=True, N=D, K=512, b_off=(0, O_KV), dep=tok, name="in_proj_ctx_dx")
    dx, _, st1 = normmod_bwd(x, dict(a=dp, b=win, tb=True, tk=NIN, dep=tok), n1g, sc1, dx1, a_out, g1,
                             name="in_proj_dx_normmod1_bwd")
    stc = normmod_bwd(ctx, dhc, n1g, csc1, None, None, None, name="normmod1_ctx_bwd")

    return dict(loss=loss, dx=dx, st1=st1, st2=st2, stc=stc, dqg=dqg, dkvg=dkvg, dfg=dfg,
                dconv_w=dconv_w, dconv_b=dconv_b, dffn_w=dffn_w, dffn_b=dffn_b)


def _me():
    x, y, c = lax.axis_index("x"), lax.axis_index("y"), lax.axis_index("c")
    return x, y, c, 4 * x + 2 * y + c


def _peer(x, y, c, k):
    px = 1 - x if k & 4 else x
    py = 1 - y if k & 2 else y
    pc = 1 - c if k & 1 else c
    return (px, py, pc), 4 * px + 2 * py + pc


def _exchange_tiles(src_of_peer, buf, send_sem, recv_sem):
    x, y, c, me = _me()
    for k in range(1, NDEV):
        dev, lin = _peer(x, y, c, k)
        pltpu.make_async_remote_copy(src_ref=src_of_peer(lin), dst_ref=buf.at[me], send_sem=send_sem, recv_sem=recv_sem,
                                     device_id=dev, device_id_type=MESH).start()
    seven = buf.at[pl.ds(0, NDEV - 1)]
    pltpu.make_async_remote_copy(src_ref=seven, dst_ref=seven, send_sem=send_sem, recv_sem=recv_sem,
                                 device_id=(x, y, c), device_id_type=MESH).wait()


def _silu(z):
    return z * jax.nn.sigmoid(z)


def ada_fwd(c, c_ctx, ffn_w, conv_w, w_shard, b_ada, deps, *, name):
    nsh, nf, nc = w_shard.shape[1], ffn_w.shape[2], conv_w.shape[2]
    deps = [d for d in deps if d is not None]

    def body(c_ref, cc_ref, fw_ref, cw_ref, w_ref, b_ref, *rest):
        s_ref, ml_ref, mc_ref, fwf_ref, cwf_ref, m_ref, mine, res, sems = rest[len(deps):]
        x, y, c, me = _me()
        mine[...] = jnp.zeros_like(mine)
        mine[0:1, :] = _silu(c_ref[...])
        mine[1:2, :] = _silu(cc_ref[...])
        for k in range(3):
            mine[2 + k:3 + k, 0:fw_ref.shape[2]] = fw_ref[k]
            mine[5 + k:6 + k, 0:cw_ref.shape[2]] = cw_ref[k]
        s_ref[me] = mine[...]
        _exchange_tiles(lambda lin: mine, s_ref, sems.at[0], sems.at[1])
        sall = s_ref[...].reshape(NDEV * 8, D).astype(BF)
        r = jnp.dot(sall, w_ref[...].astype(BF), preferred_element_type=F32) + b_ref[me]
        res[...] = r.reshape(NDEV, 8, nsh)
        m_ref[me] = res[me]
        _exchange_tiles(lambda lin: res.at[lin], m_ref, sems.at[2], sems.at[3])
        for j in range(NDEV):
            ml_ref[:, j * nsh:(j + 1) * nsh] = m_ref[j, 0:1, :]
            mc_ref[:, j * nsh:(j + 1) * nsh] = m_ref[j, 1:2, :]
            fwf_ref[:, j * nf:(j + 1) * nf] = s_ref[j, 2:5, 0:nf]
            cwf_ref[:, j * nc:(j + 1) * nc] = s_ref[j, 5:8, 0:nc]

    vm = pl.BlockSpec(memory_space=pltpu.VMEM)
    return pl.pallas_call(
        body, name=name, in_specs=[vm] * 6 + [pl.BlockSpec(memory_space=pl.ANY)] * len(deps), out_specs=[vm] * 5,
        out_shape=[jax.ShapeDtypeStruct((NDEV, 8, D), F32), jax.ShapeDtypeStruct((1, NDEV * nsh), F32),
                   jax.ShapeDtypeStruct((1, NDEV * nsh), F32), jax.ShapeDtypeStruct((3, NDEV * nf), F32),
                   jax.ShapeDtypeStruct((3, NDEV * nc), F32)],
        scratch_shapes=[pltpu.VMEM((NDEV, 8, nsh), F32), pltpu.VMEM((8, D), F32), pltpu.VMEM((NDEV, 8, nsh), F32),
                        pltpu.SemaphoreType.DMA((4,))],
    )(c, c_ctx, ffn_w, conv_w, w_shard, b_ada, *deps)


P_DML, P_DMC, P_N1, P_QG, P_KVG, P_CB, P_N2, P_FB, P_FG, P_CW, P_FW, P_LOSS, P_ROWS = 0, 8, 16, 17, 18, 19, 20, 21, 27, 28, 31, 49, 56
NSH = 6 * D // NDEV
FROWS = 3


def pack_small(r, *, name):
    ins = [r["st1"], r["st2"], r["stc"], r["dqg"], r["dkvg"], r["dconv_b"], r["dffn_b"], r["dfg"], r["dconv_w"],
           r["dffn_w"], r["loss"]]

    def put_wide(p, row0, row, n):
        for j in range(-(-n // D)):
            w = min(D, n - j * D)
            p[row0 + j:row0 + j + 1, 0:w] = row[:, j * D:j * D + w]

    def body(st1, st2, stc, qg, kvg, cb, fb, fg, cw, fw, loss, p):
        p[...] = jnp.zeros_like(p)
        lat = (st1.at[0:1], st1.at[1:2], st1.at[3:4], st2.at[0:1], st2.at[1:2], st2.at[3:4])
        ctx = (stc.at[0:1], stc.at[1:2])
        for j in range(NDEV):
            done = 0
            while done < NSH:
                q, off = divmod(j * NSH + done, D)
                w = min(D - off, NSH - done)
                p[P_DML + j:P_DML + j + 1, done:done + w] = lat[q][:, off:off + w]
                if q < len(ctx):
                    p[P_DMC + j:P_DMC + j + 1, done:done + w] = ctx[q][:, off:off + w]
                done += w
        p[P_N1:P_N1 + 1, :] = st1[2:3, :] + stc[2:3, :]
        p[P_N2:P_N2 + 1, :] = st2[2:3, :]
        put_wide(p, P_QG, qg, 512)
        put_wide(p, P_KVG, kvg, KVL)
        put_wide(p, P_CB, cb, CONV)
        put_wide(p, P_FG, fg, D)
        put_wide(p, P_LOSS, loss, 128)
        for s in range(2):
            put_wide(p, P_FB + FROWS * s, fb.at[s], DFF)
        for k in range(3):
            put_wide(p, P_CW + k, cw.at[k:k + 1], CONV)
            for s in range(2):
                put_wide(p, P_FW + FROWS * (2 * k + s), fw.at[s, k:k + 1], DFF)

    vm = pl.BlockSpec(memory_space=pltpu.VMEM)
    return pl.pallas_call(
        body, name=name, in_specs=[vm] * len(ins), out_specs=vm, out_shape=jax.ShapeDtypeStruct((P_ROWS, D), F32),
    )(*ins)


def sum_slots(a, *, name):
    rows = dict(norm1_g=(P_N1, D), q_norm_g=(P_QG, QL), kv_norm_g=(P_KVG, KVL), conv_b=(P_CB, CONV), norm2_g=(P_N2, D),
                final_g=(P_FG, D))

    def body(a_ref, sum_ref, *out):
        acc = a_ref[0]
        for k in range(1, NDEV):
            acc = acc + a_ref[k]
        sum_ref[...] = acc
        for ref, (row, n) in zip(out, rows.values()):
            ref[...] = sum_ref[row:row + 1, 0:n]
        fb, bada = out[len(rows):]
        for s in range(2):
            for j in range(FROWS):
                w = min(D, DFF - j * D)
                row = P_FB + FROWS * s + j
                fb[:, s * DFF + j * D:s * DFF + j * D + w] = sum_ref[row:row + 1, 0:w]
        for j in range(NDEV):
            bada[:, j * NSH:(j + 1) * NSH] = (sum_ref[P_DML + j:P_DML + j + 1, 0:NSH]
                                              + sum_ref[P_DMC + j:P_DMC + j + 1, 0:NSH])

    vm = pl.BlockSpec(memory_space=pltpu.VMEM)
    widths = [n for _, n in rows.values()] + [2 * DFF, 6 * D]
    outs = pl.pallas_call(
        body, name=name, in_specs=[vm], out_specs=[vm] * (1 + len(widths)),
        out_shape=[jax.ShapeDtypeStruct(a.shape[1:], F32)] + [jax.ShapeDtypeStruct((1, n), F32) for n in widths])(a)
    return outs[0], dict(zip(list(rows) + ["ffn_conv_b", "b_ada"], outs[1:]))


def ada_bwd(s_all, a_all, a_sum, w_shard, c_ctx, *, name):
    nsh = w_shard.shape[1]
    assert nsh == NSH

    def body(s_ref, a_ref, sum_ref, w_ref, c_ref, dw_ref, gc_ref, s16, dm16, part, buf, sems):
        x, y, c, me = _me()
        s16[...] = jnp.zeros_like(s16)
        dm16[...] = jnp.zeros_like(dm16)
        for k in range(NDEV):
            s16[k:k + 1, :] = s_ref[k, 0:1, :]
            dm16[k:k + 1, :] = a_ref[k, pl.ds(P_DML + me, 1), 0:nsh]
        s16[8:9, :] = s_ref[0, 1:2, :]
        dm16[8:9, :] = sum_ref[pl.ds(P_DMC + me, 1), 0:nsh]
        dw_ref[...] = lax.dot_general(s16[...].astype(BF), dm16[...].astype(BF), (((0,), (0,)), ((), ())),
                                      preferred_element_type=F32)
        part[...] = lax.dot_general(dm16[8:16, :].astype(BF), w_ref[...].astype(BF), (((1,), (1,)), ((), ())),
                                    preferred_element_type=F32)
        buf[me] = part[...]
        _exchange_tiles(lambda lin: part, buf, sems.at[0], sems.at[1])
        acc = buf[0]
        for k in range(1, NDEV):
            acc = acc + buf[k]
        z = c_ref[...]
        sg = jax.nn.sigmoid(z)
        gc_ref[...] = acc * (sg * (1.0 + z * (1.0 - sg)))

    vm = pl.BlockSpec(memory_space=pltpu.VMEM)
    return pl.pallas_call(
        body, name=name, in_specs=[vm] * 5, out_specs=[vm, vm],
        out_shape=[jax.ShapeDtypeStruct((D, nsh), F32), jax.ShapeDtypeStruct((8, D), F32)],
        scratch_shapes=[pltpu.VMEM((16, D), F32), pltpu.VMEM((16, nsh), F32), pltpu.VMEM((8, D), F32),
                        pltpu.VMEM((NDEV, 8, D), F32), pltpu.SemaphoreType.DMA((2,))],
    )(s_all, a_all, a_sum, w_shard, c_ctx)


HBM_SPEC = pl.BlockSpec(memory_space=pltpu.HBM)
SEM_SPEC = pl.BlockSpec(memory_space=pltpu.SEMAPHORE)
EFFECT = pltpu.SideEffectType.DATAFLOW_SIDE_EFFECTING


ALL_PEERS = tuple(range(1, NDEV))
FIRST_HOP = (1, 2, 4, 6)
RELAY = (2, 4, 6)


def _exchange_copies(srcs, lands, send, recv, per_peer, peers):
    x, y, c, me = _me()
    n = len(peers)
    cps = []
    for t in range(len(srcs)):
        for j, k in enumerate(peers):
            dev, lin = _peer(x, y, c, k)
            cps.append(pltpu.make_async_remote_copy(
                src_ref=srcs[t].at[lin] if per_peer else srcs[t], dst_ref=lands[t].at[me],
                send_sem=send.at[n * t + j], recv_sem=recv.at[n * t + j], device_id=dev, device_id_type=MESH))
    return cps


def _relay_copies(lands, send, recv):
    x, y, c, me = _me()
    n = len(RELAY)
    cps = []
    for t in range(len(lands)):
        for j, k in enumerate(RELAY):
            slot = lands[t].at[_peer(x, y, c, k)[1]]
            cps.append(pltpu.make_async_remote_copy(
                src_ref=slot, dst_ref=slot, send_sem=send.at[n * t + j], recv_sem=recv.at[n * t + j],
                device_id=(x, y, 1 - c), device_id_type=MESH))
    return cps


def _own_copies(srcs, lands, own, per_peer):
    me = _me()[3]
    return [pltpu.make_async_copy(srcs[t].at[me] if per_peer else srcs[t], lands[t].at[me], own.at[t])
            for t in range(len(srcs))]


def exchange_start(srcs, *, per_peer, name, dep=None, peers=ALL_PEERS):
    nt = len(srcs)
    ns = len(peers) * nt
    land_shapes = [(a.shape if per_peer else (NDEV,) + a.shape) for a in srcs]
    deps = [] if dep is None else [dep]

    def body(*refs):
        src, land = refs[:nt], refs[nt:2 * nt]
        send, recv, own = refs[2 * nt + len(deps):2 * nt + len(deps) + 3]
        for cp in _exchange_copies(src, land, send, recv, per_peer, peers) + _own_copies(src, land, own, per_peer):
            cp.start()
        refs[-1][...] = jnp.zeros_like(refs[-1])

    hb = lambda a: pltpu.with_memory_space_constraint(a, pltpu.HBM)
    outs = pl.pallas_call(
        body, name=name,
        out_shape=(pltpu.SemaphoreType.DMA((ns,)), pltpu.SemaphoreType.DMA((ns,)), pltpu.SemaphoreType.DMA((nt,)),
                   *[pltpu.HBM(a.shape, a.dtype) for a in srcs], *[pltpu.HBM(s, a.dtype) for s, a in zip(land_shapes, srcs)],
                   jax.ShapeDtypeStruct((8, 128), F32)),
        in_specs=[HBM_SPEC] * (2 * nt) + [pl.BlockSpec(memory_space=pl.ANY)] * len(deps),
        out_specs=(SEM_SPEC, SEM_SPEC, SEM_SPEC, *([HBM_SPEC] * (2 * nt)), pl.BlockSpec(memory_space=pltpu.VMEM)),
        input_output_aliases={i: 3 + i for i in range(2 * nt)},
        compiler_params=pltpu.CompilerParams(has_side_effects=EFFECT),
    )(*[hb(a) for a in srcs], *[hb(lax.empty(s, a.dtype)) for s, a in zip(land_shapes, srcs)], *deps)
    return dict(send=outs[0], recv=outs[1], own=outs[2], src=list(outs[3:3 + nt]), land=list(outs[3 + nt:3 + 2 * nt]),
                token=outs[-1], per_peer=per_peer, peers=peers)


def exchange_wait(h, after, *, name):
    nt = len(h["src"])
    per_peer, peers = h["per_peer"], h["peers"]
    after = list(after) if isinstance(after, (list, tuple)) else [after]

    def body(*refs):
        src, land, send, recv, own = refs[:nt], refs[nt:2 * nt], refs[2 * nt], refs[2 * nt + 1], refs[2 * nt + 2]
        for cp in _exchange_copies(src, land, send, recv, per_peer, peers):
            cp.wait_send()
            cp.wait_recv()
        for cp in _own_copies(src, land, own, per_peer):
            cp.wait()

    outs = pl.pallas_call(
        body, name=name,
        out_shape=(*[pltpu.HBM(a.shape, a.dtype) for a in h["src"]], *[pltpu.HBM(a.shape, a.dtype) for a in h["land"]]),
        in_specs=[HBM_SPEC] * (2 * nt) + [SEM_SPEC, SEM_SPEC, SEM_SPEC] + [pl.BlockSpec(memory_space=pl.ANY)] * len(after),
        out_specs=tuple([HBM_SPEC] * (2 * nt)),
        input_output_aliases={i: i for i in range(2 * nt)},
        compiler_params=pltpu.CompilerParams(has_side_effects=EFFECT),
    )(*h["src"], *h["land"], h["send"], h["recv"], h["own"], *after)
    return list(outs[nt:])


def relay_start(lands, *, name):
    nt = len(lands)
    ns = len(RELAY) * nt

    def body(*refs):
        for cp in _relay_copies(refs[:nt], refs[nt], refs[nt + 1]):
            cp.start()

    outs = pl.pallas_call(
        body, name=name,
        out_shape=(pltpu.SemaphoreType.DMA((ns,)), pltpu.SemaphoreType.DMA((ns,)),
                   *[pltpu.HBM(a.shape, a.dtype) for a in lands]),
        in_specs=[HBM_SPEC] * nt, out_specs=(SEM_SPEC, SEM_SPEC, *([HBM_SPEC] * nt)),
        input_output_aliases={i: 2 + i for i in range(nt)},
        compiler_params=pltpu.CompilerParams(has_side_effects=EFFECT),
    )(*lands)
    return dict(send=outs[0], recv=outs[1], land=list(outs[2:]))


def relay_wait(h, *, name):
    nt = len(h["land"])

    def body(*refs):
        for cp in _relay_copies(refs[:nt], refs[nt], refs[nt + 1]):
            cp.wait_send()
            cp.wait_recv()

    outs = pl.pallas_call(
        body, name=name, out_shape=tuple(pltpu.HBM(a.shape, a.dtype) for a in h["land"]),
        in_specs=[HBM_SPEC] * nt + [SEM_SPEC, SEM_SPEC], out_specs=tuple([HBM_SPEC] * nt),
        input_output_aliases={i: i for i in range(nt)},
        compiler_params=pltpu.CompilerParams(has_side_effects=EFFECT),
    )(*h["land"], h["send"], h["recv"])
    return list(outs)


def _adamw_math(w, g, m, v):
    nm = B1 * m + (1.0 - B1) * g
    nv = B2 * v + (1.0 - B2) * (g * g)
    m_hat = nm / (1.0 - B1 ** STEP)
    v_hat = nv / (1.0 - B2 ** STEP)
    return -LR * (m_hat / (jnp.sqrt(v_hat) + AEPS) + WD * w), nm, nv


def adamw_many(ws, gs, ms, vs, *, name):
    n = len(ws)

    def body(*refs):
        for k in range(n):
            d, nm, nv = _adamw_math(refs[k][...], refs[n + k][...], refs[2 * n + k][...], refs[3 * n + k][...])
            refs[4 * n + k][...] = d
            refs[5 * n + k][...] = nm
            refs[6 * n + k][...] = nv

    vm = pl.BlockSpec(memory_space=pltpu.VMEM)
    sh = [jax.ShapeDtypeStruct(w.shape, F32) for w in ws]
    outs = pl.pallas_call(body, name=name, in_specs=[vm] * (4 * n), out_specs=[vm] * (3 * n), out_shape=sh * 3,
                          )(*ws, *gs, *ms, *vs)
    return outs[:n], outs[n:2 * n], outs[2 * n:]


def adamw(w, g, m, v, *, name, tr=256):
    R, C = w.shape
    tr = _pick(R, tr, 8)

    def body(w_ref, g_ref, m_ref, v_ref, d_ref, nm_ref, nv_ref):
        d_ref[...], nm_ref[...], nv_ref[...] = _adamw_math(w_ref[...], g_ref[...], m_ref[...], v_ref[...])

    blk = pl.BlockSpec((tr, C), lambda i: (i, 0))
    sh = jax.ShapeDtypeStruct((R, C), F32)
    return pl.pallas_call(
        body, name=name, grid=(R // tr,), in_specs=[blk, blk, blk, blk], out_specs=[blk, blk, blk],
        out_shape=[sh, sh, sh], compiler_params=pltpu.CompilerParams(dimension_semantics=("parallel",)),
    )(w, g, m, v)


def adamw_slots(w, slots, m, v, *, name, tr=256):
    unit = w.ndim == 3
    R, C = w.shape[0], w.shape[-1]
    parts = list(slots) if isinstance(slots, (list, tuple)) else [slots]
    n = len(parts)
    assert sum(s.shape[-1] for s in parts) == C
    if R % 16 == 0:
        tr = _pick(R, tr, 16)
    else:
        tr = 144

    def body(w_ref, *refs):
        s_refs, (m_ref, v_ref, g_ref, d_ref, nm_ref, nv_ref) = refs[:n], refs[n:]
        gs = []
        for s_ref in s_refs:
            g = s_ref[0].astype(F32)
            for k in range(1, NDEV):
                g = g + s_ref[k].astype(F32)
            gs.append(g)
        g = gs[0] if n == 1 else jnp.concatenate(gs, axis=-1)
        g_ref[...] = g
        d_ref[...], nm_ref[...], nv_ref[...] = _adamw_math(w_ref[...], g, m_ref[...], v_ref[...])

    blk = pl.BlockSpec((tr, None, C), lambda i: (i, 0, 0)) if unit else pl.BlockSpec((tr, C), lambda i: (i, 0))
    sh = jax.ShapeDtypeStruct(w.shape, F32)
    return pl.pallas_call(
        body, name=name, grid=(pl.cdiv(R, tr),),
        in_specs=[blk] + [pl.BlockSpec((NDEV, tr, s.shape[-1]), lambda i: (0, i, 0)) for s in parts] + [blk, blk],
        out_specs=[blk, blk, blk, blk], out_shape=[sh, sh, sh, sh],
        compiler_params=pltpu.CompilerParams(dimension_semantics=("parallel",)),
    )(w, *parts, m, v)


def kernel(x, c, ctx, c_ctx, w_ada, b_ada, norm1_g, w_in, q_norm_g, kv_norm_g, w_uq, w_ukv, conv_w, conv_b, w_attn_out, w_conv_out, w_o, norm2_g, w_up, ffn_conv_w, ffn_conv_b, w_down, final_g, loss_target, m_c_ctx, m_w_ada, m_b_ada, m_norm1_g, m_w_in, m_q_norm_g, m_kv_norm_g, m_w_uq, m_w_ukv, m_conv_w, m_conv_b, m_w_attn_out, m_w_conv_out, m_w_o, m_norm2_g, m_w_up, m_ffn_conv_w, m_ffn_conv_b, m_w_down, m_final_g, v_c_ctx, v_w_ada, v_b_ada, v_norm1_g, v_w_in, v_q_norm_g, v_kv_norm_g, v_w_uq, v_w_ukv, v_conv_w, v_conv_b, v_w_attn_out, v_w_conv_out, v_w_o, v_norm2_g, v_w_up, v_ffn_conv_w, v_ffn_conv_b, v_w_down, v_final_g):
    me = 4 * lax.axis_index("x") + 2 * lax.axis_index("y") + lax.axis_index("c")
    W = dict(c_ctx=c_ctx, w_ada=w_ada, b_ada=b_ada, norm1_g=norm1_g, w_in=w_in, q_norm_g=q_norm_g, kv_norm_g=kv_norm_g,
             w_uq=w_uq, w_ukv=w_ukv, conv_w=conv_w, conv_b=conv_b, w_attn_out=w_attn_out, w_conv_out=w_conv_out, w_o=w_o,
             norm2_g=norm2_g, w_up=w_up, ffn_conv_w=ffn_conv_w, ffn_conv_b=ffn_conv_b, w_down=w_down, final_g=final_g)
    M = dict(c_ctx=m_c_ctx, w_ada=m_w_ada, b_ada=m_b_ada, norm1_g=m_norm1_g, w_in=m_w_in, q_norm_g=m_q_norm_g,
             kv_norm_g=m_kv_norm_g, w_uq=m_w_uq, w_ukv=m_w_ukv, conv_w=m_conv_w, conv_b=m_conv_b, w_attn_out=m_w_attn_out,
             w_conv_out=m_w_conv_out, w_o=m_w_o, norm2_g=m_norm2_g, w_up=m_w_up, ffn_conv_w=m_ffn_conv_w,
             ffn_conv_b=m_ffn_conv_b, w_down=m_w_down, final_g=m_final_g)
    V = dict(c_ctx=v_c_ctx, w_ada=v_w_ada, b_ada=v_b_ada, norm1_g=v_norm1_g, w_in=v_w_in, q_norm_g=v_q_norm_g,
             kv_norm_g=v_kv_norm_g, w_uq=v_w_uq, w_ukv=v_w_ukv, conv_w=v_conv_w, conv_b=v_conv_b, w_attn_out=v_w_attn_out,
             w_conv_out=v_w_conv_out, w_o=v_w_o, norm2_g=v_norm2_g, w_up=v_w_up, ffn_conv_w=v_ffn_conv_w,
             ffn_conv_b=v_ffn_conv_b, w_down=v_w_down, final_g=v_final_g)
    names = list(W)
    transposed = ("w_up", "w_uq")
    as2d = lambda k, a: (a.reshape(1, -1) if a.ndim == 1 else
                         a[0].T if k in transposed else a.reshape(a.shape[-2], a.shape[-1]))
    W2 = {k: as2d(k, a) for k, a in W.items()}
    M2 = {k: as2d(k, a) for k, a in M.items()}
    V2 = {k: as2d(k, a) for k, a in V.items()}
    unit3 = lambda a: jnp.transpose(a, (2, 0, 1))
    W3, M3, V3 = unit3(W["w_in"]), unit3(M["w_in"]), unit3(V["w_in"])
    nsh = W2["w_ada"].shape[1]

    unit_mid = ("conv_w", "ffn_conv_w")
    mid3 = lambda a: jnp.transpose(a, (1, 0, 2))
    s_all, mod_lat, mod_ctx, ffn_w_full, conv_w_full = ada_fwd(
        c, W2["c_ctx"], mid3(W["ffn_conv_w"]), mid3(W["conv_w"]), W2["w_ada"], W["b_ada"].reshape(NDEV, 1, nsh), [],
        name="ada_fwd")

    stage_w = {"in": ["w_in"], "mid": ["w_uq", "w_ukv", "w_attn_out", "w_conv_out", "w_o"], "up": ["w_up"],
               "down": ["w_down"]}
    two_level = ("in", "mid")
    ag, tok = {}, mod_lat
    for st, nms in stage_w.items():
        ag[st] = exchange_start([W2[nm].astype(BF) for nm in nms], per_peer=False, dep=tok, name="ag_start_" + st,
                                peers=FIRST_HOP if st in two_level else ALL_PEERS)
        tok = ag[st]["token"]

    def get_w(stage, after):
        lands = exchange_wait(ag[stage], after, name="ag_wait_" + stage)
        if stage in two_level:
            lands = relay_wait(relay_start(lands, name="ag_relay_" + stage), name="ag_relay_wait_" + stage)
        g = dict(zip(stage_w[stage], lands))
        if stage == "in":
            return build_win(g["w_in"], name="build_win")
        if stage == "mid":
            wq2, wkv2 = build_wq_wkv(g["w_uq"], g["w_ukv"], name="build_wq_wkv")
            return (wq2, wkv2, unshard_cols(g["w_attn_out"], name="unshard_w_attn_out"),
                    unshard_cols(g["w_conv_out"], name="unshard_w_conv_out"), g["w_o"].reshape(D, D))
        if stage == "up":
            return g["w_up"].reshape(2 * DFF, D)
        return g["w_down"].reshape(DFF, D)

    stage_g = {"ffn": ["w_up", "w_down"], "mid": ["w_attn_out", "w_conv_out", "w_o"], "qkv": ["w_uq", "w_ukv"],
               "in": ["w_in"]}
    rs = {}

    def put_g(stage, g):
        if stage in ("in0", "in1"):
            parts = [shard_win_grad(g["dwin"], g["dwin_c"], col0=g["col0"], name="shard_win_grad_" + stage[-1])]
        elif stage == "mid":
            parts = [shard_cols(g["dwao"], name="shard_w_attn_out"), shard_cols(g["dwco"], name="shard_w_conv_out"),
                     g["dwo"].reshape(NDEV, D // NDEV, D)]
        elif stage == "qkv":
            parts = list(shard_wq_wkv_grad(g["dwq2"], g["dwkv2"], name="shard_wq_wkv_grad"))
        else:
            parts = [g["dwup"].reshape(NDEV, 2 * DFF // NDEV, D), g["dwdn"].reshape(NDEV, DFF // NDEV, D)]
        rs[stage] = exchange_start(parts, per_peer=True, name="rs_start_" + stage)
        return rs[stage]["token"]

    r = _local_step(x[0], ctx[0], loss_target[0], mod_lat, mod_ctx, W2["norm1_g"], W2["q_norm_g"], W2["kv_norm_g"],
                    W2["norm2_g"], W2["final_g"], conv_w_full, W2["conv_b"], ffn_w_full, W2["ffn_conv_b"], get_w, put_g,
                    ag["down"]["token"])

    G, DL, NM, NV = {}, {}, {}, {}

    def finish(stage, after):
        if stage == "in":
            halves = []
            for h in ("in0", "in1"):
                halves += exchange_wait(rs[h], after, name="rs_wait_" + h)
                after = halves[-1]
            G["w_in"], DL["w_in"], NM["w_in"], NV["w_in"] = adamw_slots(W3, halves, M3, V3, name="adamw_w_in")
            return DL["w_in"]
        for nm, sl in zip(stage_g[stage], exchange_wait(rs[stage], after, name="rs_wait_" + stage)):
            G[nm], DL[nm], NM[nm], NV[nm] = adamw_slots(W2[nm], sl, M2[nm], V2[nm], name="adamw_" + nm)
            after = DL[nm]
        return after

    sync = exchange_start([pack_small(r, name="pack_small")], per_peer=False, name="sync_start")
    after = sync["token"]
    for st in ("ffn", "mid", "in", "qkv"):
        after = finish(st, after)
    a_buf, = exchange_wait(sync, [DL[nm] for nms in stage_g.values() for nm in nms], name="sync_wait")
    ssum, g_vec = sum_slots(a_buf, name="sum_small")
    G.update(g_vec)
    loss = ssum[P_LOSS, 0]
    G["conv_w"] = lax.dynamic_slice(ssum[P_CW:P_CW + 3, :CONV], (0, me * (CONV // NDEV)), (3, CONV // NDEV))
    fw_full = ssum[P_FW:P_FW + 6 * FROWS].reshape(3, 2, FROWS * D)[:, :, :DFF].reshape(3, 2 * DFF)
    G["ffn_conv_w"] = lax.dynamic_slice(fw_full, (0, me * (2 * DFF // NDEV)), (3, 2 * DFF // NDEV))

    G["w_ada"], gcc = ada_bwd(s_all, a_buf, ssum, W2["w_ada"], W2["c_ctx"], name="ada_bwd")
    G["c_ctx"] = gcc[0:1]

    DL["w_ada"], NM["w_ada"], NV["w_ada"] = adamw(W2["w_ada"], G["w_ada"], M2["w_ada"], V2["w_ada"], name="adamw_w_ada")
    small = ["c_ctx", "b_ada", "norm1_g", "q_norm_g", "kv_norm_g", "conv_b", "norm2_g", "ffn_conv_b", "final_g", "conv_w",
             "ffn_conv_w"]
    view = lambda k, a3, a2: mid3(a3[k]) if k in unit_mid else a2[k]
    for k in unit_mid:
        G[k] = G[k].reshape(3, 1, -1)
    ds, nms, nvs = adamw_many([view(k, W, W2) for k in small], [G[k] for k in small],
                              [view(k, M, M2) for k in small], [view(k, V, V2) for k in small], name="adamw_small")
    for k, nm in enumerate(small):
        DL[nm], NM[nm], NV[nm] = ds[k], nms[k], nvs[k]

    def as_output(nm, a):
        if nm in transposed:
            return a.T[None]
        if nm == "w_in":
            return jnp.transpose(a, (1, 2, 0))
        if nm in unit_mid and a.ndim == 3:
            return jnp.transpose(a, (1, 0, 2))
        return a.reshape(W[nm].shape)

    outs = [loss, r["dx"][None]]
    for grp in (G, DL, NM, NV):
        outs += [as_output(nm, grp[nm]) for nm in names]
    return tuple(outs)
```

```python
import functools
import numpy as np
import jax
import jax.numpy as jnp
from jax import lax
from jax.experimental import pallas as pl
from jax.experimental.pallas import tpu as pltpu

F32 = jnp.float32
BF = jnp.bfloat16
MESH = pl.DeviceIdType.MESH

D = 1024
T = 2048
TC = 256
TKV = T + TC
GRID_W = 64
NH = 8
DN = 64
DR = 32
DV = 64
QL = 384
KVL = 256
CONV = 512
DFF = 2816
EPS = 1e-6
ROPE_THETA = 10000.0
SCALE = (DN + DR) ** -0.5
NDEV = 8
HP = 128

O_GA, O_GC, O_KV, O_Q, O_CV = 0, 1024, 2048, 2560, 3072
NIN = 4608
CVB = 256
N_IN = 4256
SH_IN = N_IN // NDEV

LR, B1, B2, AEPS, WD, STEP = 0.001, 0.9, 0.999, 1e-08, 0.01, 10


def _pick(n, target, mult=128):
    best = None
    for d in range(mult, min(n, target) + 1, mult):
        if n % d == 0:
            best = d
    return best if best is not None else n


def _swap_start(g):
    return 8 * (g ^ 1)


def mm(a, b, *, ta=False, tb=False, out_dtype=F32, name, tm=1024, tn=1024, tk=2048, M=None, N=None, K=None,
       a_off=(0, 0), b_off=(0, 0), a_stack=False, b_stack=False, o_stack=False, dep=None):
    def dims(arr, stack):
        return (arr.shape[1], 2 * arr.shape[2]) if stack else arr.shape

    ar, ac = dims(a, a_stack)
    br, bc = dims(b, b_stack)
    M = M or ((ac if ta else ar) - a_off[1 if ta else 0])
    K = K or ((ar if ta else ac) - a_off[0 if ta else 1])
    N = N or ((br if tb else bc) - b_off[0 if tb else 1])
    tm = _pick(M, tm, 128 if ta else 16)
    tn = _pick(N // 2 if (o_stack or (b_stack and not tb)) else N, tn, 128)
    tk = _pick(K // 2 if ((a_stack and not ta) or (b_stack and tb)) else K, tk, 128)
    nk = K // tk
    ca = 0 if ta else 1
    cb = 1 if tb else 0

    def body(a_ref, b_ref, *rest):
        o_ref, acc = rest[-2:]
        k = pl.program_id(2)
        part = lax.dot_general(a_ref[...].astype(BF), b_ref[...].astype(BF),
                               (((ca,), (cb,)), ((), ())), preferred_element_type=F32)
        if nk == 1:
            o_ref[...] = part.astype(o_ref.dtype)
        else:
            @pl.when(k == 0)
            def _():
                acc[...] = part

            @pl.when(k > 0)
            def _():
                acc[...] += part

            @pl.when(k == nk - 1)
            def _():
                o_ref[...] = acc[...].astype(o_ref.dtype)

    def spec(blk, rc, off, stack, ncols):
        assert off[0] % blk[0] == 0 and off[1] % blk[1] == 0, (name, blk, off)
        ro, co = off[0] // blk[0], off[1] // blk[1]
        if not stack:
            return pl.BlockSpec(blk, lambda i, j, k: (rc(i, j, k)[0] + ro, rc(i, j, k)[1] + co))
        nhb = ncols // 2 // blk[1]
        return pl.BlockSpec((None,) + blk,
                            lambda i, j, k: ((rc(i, j, k)[1] + co) // nhb, rc(i, j, k)[0] + ro, (rc(i, j, k)[1] + co) % nhb))

    a_spec = spec((tk, tm), lambda i, j, k: (k, i), a_off, a_stack, ac) if ta else \
        spec((tm, tk), lambda i, j, k: (i, k), a_off, a_stack, ac)
    b_spec = spec((tn, tk), lambda i, j, k: (j, k), b_off, b_stack, bc) if tb else \
        spec((tk, tn), lambda i, j, k: (k, j), b_off, b_stack, bc)
    o_spec = spec((tm, tn), lambda i, j, k: (i, j), (0, 0), o_stack, N)
    o_shape = (2, M, N // 2) if o_stack else (M, N)
    deps = [] if dep is None else [dep]
    return pl.pallas_call(
        body, name=name, grid=(M // tm, N // tn, nk),
        in_specs=[a_spec, b_spec] + [pl.BlockSpec(memory_space=pl.ANY)] * len(deps),
        out_specs=o_spec, out_shape=jax.ShapeDtypeStruct(o_shape, out_dtype),
        scratch_shapes=[pltpu.VMEM((tm, tn) if nk > 1 else (8, 128), F32)],
        compiler_params=pltpu.CompilerParams(dimension_semantics=("parallel", "parallel", "arbitrary")),
    )(a, b, *deps)


def _row(width):
    return pl.BlockSpec((1, width), lambda *_: (0, 0))


NLAT = T // TC


def normmod_cat(ctx, x, g, csc, csh, sc, sh, dep, *, name, tm=256):
    assert tm == TC

    def body(c_ref, x_ref, g_ref, csc_ref, csh_ref, sc_ref, sh_ref, dep_ref, h_ref):
        last = pl.program_id(0) == NLAT
        xv = jnp.where(last, c_ref[...], x_ref[...])
        scv = jnp.where(last, csc_ref[...], sc_ref[...])
        shv = jnp.where(last, csh_ref[...], sh_ref[...])
        r = lax.rsqrt(jnp.mean(xv * xv, axis=-1, keepdims=True) + EPS)
        h_ref[...] = ((xv * r * g_ref[...]) * (1.0 + scv) + shv).astype(BF)

    return pl.pallas_call(
        body, name=name, grid=(TKV // tm,),
        in_specs=[pl.BlockSpec((tm, D), lambda i: (0, 0)), pl.BlockSpec((tm, D), lambda i: (jnp.minimum(i, NLAT - 1), 0)),
                  _row(D), _row(D), _row(D), _row(D), _row(D), pl.BlockSpec(memory_space=pl.ANY)],
        out_specs=pl.BlockSpec((tm, D), lambda i: (i, 0)), out_shape=jax.ShapeDtypeStruct((TKV, D), BF),
        compiler_params=pltpu.CompilerParams(dimension_semantics=("parallel",)),
    )(ctx, x, g, csc, csh, sc, sh, dep)


def kvprep(pc, p, kvg, wkv2, ck, sk, *, name, tm=256):
    assert tm == TC
    nb = TKV // tm
    kvcol = O_KV // 512

    def body(pc_ref, p_ref, g_ref, w_ref, ck_ref, sk_ref, k_ref, v_ref, ckv_ref):
        i = pl.program_id(0)
        t = jnp.where(i == NLAT, pc_ref[...], p_ref[...])
        pk = t[:, :KVL]
        r = lax.rsqrt(jnp.mean(pk * pk, axis=-1, keepdims=True) + EPS)
        ckv = (pk * r * g_ref[...]).astype(BF)
        ckv_ref[...] = ckv
        kv2 = jnp.dot(ckv, w_ref[...], preferred_element_type=F32)
        krr = t[:, KVL:KVL + HP] * ck_ref[...] + t[:, KVL + HP:KVL + 2 * HP] * sk_ref[...]
        k_ref[...] = (kv2[:, :NH * HP] + jnp.concatenate([krr] * NH, axis=1)).astype(BF)
        v_ref[...] = kv2[:, NH * HP:].astype(BF)

    return pl.pallas_call(
        body, name=name, grid=(nb,),
        in_specs=[pl.BlockSpec((tm, 512), lambda i: (0, 0)),
                  pl.BlockSpec((tm, 512), lambda i: (jnp.minimum(i, NLAT - 1), kvcol)),
                  _row(KVL), pl.BlockSpec((KVL, NH * HP + NH * DV), lambda i: (0, 0)),
                  pl.BlockSpec((tm, HP), lambda i: (i, 0)), pl.BlockSpec((tm, HP), lambda i: (i, 0))],
        out_specs=[pl.BlockSpec((tm, NH * HP), lambda i: (i, 0)), pl.BlockSpec((tm, NH * DV), lambda i: (i, 0)),
                   pl.BlockSpec((tm, KVL), lambda i: (i, 0))],
        out_shape=[jax.ShapeDtypeStruct((TKV, NH * HP), BF), jax.ShapeDtypeStruct((TKV, NH * DV), BF),
                   jax.ShapeDtypeStruct((TKV, KVL), BF)],
        compiler_params=pltpu.CompilerParams(dimension_semantics=("parallel",)),
    )(pc, p, kvg, wkv2, ck, sk)


def qprep(p, qg, wq2, cq_t, sq_t, *, name, tm=256):
    qcol = O_Q // 512

    def body(p_ref, g_ref, w_ref, c_ref, s_ref, q_ref, cq_ref):
        pq = p_ref[...]
        r = lax.rsqrt(jnp.sum(pq * pq, axis=-1, keepdims=True) * (1.0 / QL) + EPS)
        cq = (pq * r * g_ref[...]).astype(BF)
        cq_ref[...] = cq
        q2 = jnp.dot(cq, w_ref[...], preferred_element_type=F32)
        cc = jnp.concatenate([c_ref[...]] * NH, axis=1)
        ss = jnp.concatenate([s_ref[...]] * NH, axis=1)
        q_ref[...] = (q2[:, :NH * HP] * cc + q2[:, NH * HP:] * ss).astype(BF)

    return pl.pallas_call(
        body, name=name, grid=(T // tm,),
        in_specs=[pl.BlockSpec((tm, 512), lambda i: (i, qcol)), _row(512),
                  pl.BlockSpec((512, 2 * NH * HP), lambda i: (0, 0)),
                  pl.BlockSpec((tm, HP), lambda i: (i, 0)), pl.BlockSpec((tm, HP), lambda i: (i, 0))],
        out_specs=[pl.BlockSpec((tm, NH * HP), lambda i: (i, 0)), pl.BlockSpec((tm, 512), lambda i: (i, 0))],
        out_shape=[jax.ShapeDtypeStruct((T, NH * HP), BF), jax.ShapeDtypeStruct((T, 512), BF)],
        compiler_params=pltpu.CompilerParams(dimension_semantics=("parallel",)),
    )(p, qg, wq2, cq_t, sq_t)


def _head_mask(h):
    lanes = lax.broadcasted_iota(jnp.int32, (1, 2 * DV), 1)
    return (lanes // DV) == (h % 2)


LOG2E = 1.4426950408889634


def attn_fwd(q, k, v, *, name, tq=1024, kc=768):
    def body(q_ref, k_ref, v_ref, o_ref, lse_ref):
        h = pl.program_id(1)
        qv = q_ref[...]
        m = l = acc = None
        for c in range(TKV // kc):
            s = lax.dot_general(qv, k_ref[c * kc:(c + 1) * kc, :], (((1,), (1,)), ((), ())),
                                preferred_element_type=F32) * (SCALE * LOG2E)
            mc = jnp.max(s, axis=-1, keepdims=True)
            if c == 0:
                m = mc
                e = jnp.exp2(s - m)
                l = jnp.sum(e, axis=-1, keepdims=True)
                acc = jnp.dot(e.astype(BF), v_ref[c * kc:(c + 1) * kc, :], preferred_element_type=F32)
            else:
                mn = jnp.maximum(m, mc)
                a = jnp.exp2(m - mn)
                e = jnp.exp2(s - mn)
                l = l * a + jnp.sum(e, axis=-1, keepdims=True)
                acc = acc * a + jnp.dot(e.astype(BF), v_ref[c * kc:(c + 1) * kc, :], preferred_element_type=F32)
                m = mn
        o2 = jnp.where(_head_mask(h), acc * (1.0 / l), 0.0).astype(BF)
        lse_ref[...] = jnp.broadcast_to(m + jnp.log(l) * LOG2E, (tq, HP))

        @pl.when(h % 2 == 0)
        def _():
            o_ref[...] = o2

        @pl.when(h % 2 == 1)
        def _():
            o_ref[...] = o_ref[...] + o2

    return pl.pallas_call(
        body, name=name, grid=(T // tq, NH),
        in_specs=[pl.BlockSpec((tq, HP), lambda i, h: (i, h)), pl.BlockSpec((TKV, HP), lambda i, h: (0, h)),
                  pl.BlockSpec((TKV, 2 * DV), lambda i, h: (0, h // 2))],
        out_specs=[pl.BlockSpec((tq, 2 * DV), lambda i, h: (i, h // 2)), pl.BlockSpec((tq, HP), lambda i, h: (i, h))],
        out_shape=[jax.ShapeDtypeStruct((T, NH * DV), BF), jax.ShapeDtypeStruct((T, NH * HP), F32)],
        compiler_params=pltpu.CompilerParams(dimension_semantics=("parallel", "arbitrary")),
    )(q, k, v)


def _shift_dn(x):
    n = x.shape[0]
    rows = lax.broadcasted_iota(jnp.int32, (n, 1), 0)
    return jnp.where(rows == 0, 0.0, pltpu.roll(x, 1, axis=0))


def _shift_up(x):
    n = x.shape[0]
    rows = lax.broadcasted_iota(jnp.int32, (n, 1), 0)
    return jnp.where(rows == n - 1, 0.0, pltpu.roll(x, n - 1, axis=0))


def _conv(x, w_ref, b_ref):
    return b_ref[...] + _shift_dn(x) * w_ref[0:1, :] + x * w_ref[1:2, :] + _shift_up(x) * w_ref[2:3, :]


def _conv_t(dy, w_ref):
    return _shift_up(dy) * w_ref[0:1, :] + dy * w_ref[1:2, :] + _shift_dn(dy) * w_ref[2:3, :]


def _conv_wgrad(dw_ref, dy, x):
    dw_ref[0:1, :] = jnp.sum(dy * _shift_dn(x), axis=0, keepdims=True)
    dw_ref[1:2, :] = jnp.sum(dy * x, axis=0, keepdims=True)
    dw_ref[2:3, :] = jnp.sum(dy * _shift_up(x), axis=0, keepdims=True)


def convz(p, cw, cb, *, name):
    o0 = O_CV // (3 * CVB)

    def body(p_ref, w_ref, bias_ref, z_ref):
        xv, bv, cv = p_ref[:, 0:CVB], p_ref[:, CVB:2 * CVB], p_ref[:, 2 * CVB:3 * CVB]
        z_ref[...] = (bv * _conv(cv * xv, w_ref, bias_ref)).astype(BF)

    return pl.pallas_call(
        body, name=name, grid=(CONV // CVB,),
        in_specs=[pl.BlockSpec((T, 3 * CVB), lambda j: (0, o0 + j)), pl.BlockSpec((3, CVB), lambda j: (0, j)),
                  pl.BlockSpec((1, CVB), lambda j: (0, j))],
        out_specs=pl.BlockSpec((T, CVB), lambda j: (0, j)),
        out_shape=jax.ShapeDtypeStruct((T, CONV), BF),
        compiler_params=pltpu.CompilerParams(dimension_semantics=("parallel",)),
    )(p, cw, cb)


def out_proj_merge(o, wao, z, wco, p, *, name, tm=512):
    kin = o.shape[1]

    def body(o_ref, wa_ref, z_ref, wc_ref, ga_ref, gc_ref, ya_ref, yc_ref, m_ref):
        ya = jnp.dot(o_ref[...], wa_ref[...], preferred_element_type=F32)
        yc = jnp.dot(z_ref[...], wc_ref[...], preferred_element_type=F32)
        ya_ref[...] = ya
        yc_ref[...] = yc
        m_ref[...] = (jax.nn.sigmoid(ga_ref[...]) * ya + jax.nn.sigmoid(gc_ref[...]) * yc).astype(BF)

    blk = pl.BlockSpec((tm, D), lambda i: (i, 0))
    act = pl.BlockSpec((tm, kin), lambda i: (i, 0))
    wsp = pl.BlockSpec((kin, D), lambda i: (0, 0))
    sh = jax.ShapeDtypeStruct((T, D), F32)
    return pl.pallas_call(
        body, name=name, grid=(T // tm,),
        in_specs=[act, wsp, act, wsp, pl.BlockSpec((tm, D), lambda i: (i, O_GA // D)),
                  pl.BlockSpec((tm, D), lambda i: (i, O_GC // D))],
        out_specs=[blk, blk, blk], out_shape=[sh, sh, jax.ShapeDtypeStruct((T, D), BF)],
        compiler_params=pltpu.CompilerParams(dimension_semantics=("parallel",)),
    )(o, wao, z, wco, p, p)


CONV_HALO = 8
CONV_ROWS = 256


def _row_chunks(n, chunk, carry):
    carry = chunk(0, True, False, carry)
    carry = lax.fori_loop(1, n // CONV_ROWS - 1, lambda c, a: chunk(c * CONV_ROWS, False, False, a), carry)
    return chunk(n - CONV_ROWS, False, True, carry)


def _ext_rows(ref, r0, first, last):
    n, w = ref.shape
    zero = jnp.zeros((CONV_HALO, w), ref.dtype)
    if first:
        return jnp.concatenate([zero, ref[0:CONV_ROWS + CONV_HALO, :]], axis=0)
    if last:
        return jnp.concatenate([ref[n - CONV_ROWS - CONV_HALO:n, :], zero], axis=0)
    return ref[pl.ds(pl.multiple_of(r0 - CONV_HALO, 8), CONV_ROWS + 2 * CONV_HALO), :]


def _center_rows(r0, first, last):
    return slice(r0, r0 + CONV_ROWS) if (first or last) else pl.ds(pl.multiple_of(r0, 8), CONV_ROWS)


def _roll_dn(x):
    return pltpu.roll(x, 1, axis=0)


def _roll_up(x):
    return pltpu.roll(x, x.shape[0] - 1, axis=0)


_CTR = slice(CONV_HALO, CONV_HALO + CONV_ROWS)


def ffn_act(u0, cw, cb, *, name, tc=256):
    nb = DFF // tc

    def body(u_ref, wg_ref, wv_ref, bg_ref, bv_ref, f_ref):
        wg = [wg_ref[k:k + 1, :] for k in range(3)]
        wv = [wv_ref[k:k + 1, :] for k in range(3)]
        bg, bv = bg_ref[...], bv_ref[...]

        def chunk(r0, first, last, carry):
            xg, xv = _ext_rows(u_ref.at[0], r0, first, last), _ext_rows(u_ref.at[1], r0, first, last)
            ug = bg + _roll_dn(xg) * wg[0] + xg * wg[1] + _roll_up(xg) * wg[2]
            uv = bv + _roll_dn(xv) * wv[0] + xv * wv[1] + _roll_up(xv) * wv[2]
            f_ref[_center_rows(r0, first, last), :] = (ug * jax.nn.sigmoid(ug) * uv)[_CTR].astype(BF)
            return carry

        _row_chunks(T, chunk, 0)

    return pl.pallas_call(
        body, name=name, grid=(nb,),
        in_specs=[pl.BlockSpec((2, T, tc), lambda j: (0, 0, j)),
                  pl.BlockSpec((3, tc), lambda j: (0, j)), pl.BlockSpec((3, tc), lambda j: (0, nb + j)),
                  pl.BlockSpec((1, tc), lambda j: (0, j)), pl.BlockSpec((1, tc), lambda j: (0, nb + j))],
        out_specs=pl.BlockSpec((T, tc), lambda j: (0, j)),
        out_shape=jax.ShapeDtypeStruct((T, DFF), BF),
        compiler_params=pltpu.CompilerParams(dimension_semantics=("parallel",)),
    )(u0, cw, cw, cb, cb)


def rows_call(lead, ins, in_specs, out_shape, out_specs, fn, *, name, R, tm):
    tb, a_stack, tk = lead.get("tb", False), lead.get("a_stack", False), lead["tk"]
    K = 2 * lead["a"].shape[2] if a_stack else lead["a"].shape[1]
    nk = K // tk
    deps = [] if lead.get("dep") is None else [lead["dep"]]
    n_in = len(ins)

    def body(a_ref, b_ref, *refs):
        refs = refs[len(deps):]
        in_refs, out_refs, acc = refs[:n_in], refs[n_in:-1], refs[-1]
        i, k = pl.program_id(0), pl.program_id(1)
        part = lax.dot_general(a_ref[...].astype(BF), b_ref[...].astype(BF),
                               (((1,), (1 if tb else 0,)), ((), ())), preferred_element_type=F32)
        if nk == 1:
            fn(i, part, in_refs, out_refs)
            return

        @pl.when(k == 0)
        def _():
            acc[...] = part

        @pl.when(k > 0)
        def _():
            acc[...] += part

        @pl.when(k == nk - 1)
        def _():
            fn(i, acc[...], in_refs, out_refs)

    if a_stack:
        nhb = K // 2 // tk
        a_spec = pl.BlockSpec((None, tm, tk), lambda i, k: (k // nhb, i, k % nhb))
    else:
        a_spec = pl.BlockSpec((tm, tk), lambda i, k: (i, k))
    b_spec = pl.BlockSpec((D, tk), lambda i, k: (0, k)) if tb else pl.BlockSpec((tk, D), lambda i, k: (k, 0))
    return pl.pallas_call(
        body, name=name, grid=(R // tm, nk),
        in_specs=[a_spec, b_spec] + [pl.BlockSpec(memory_space=pl.ANY)] * len(deps) + list(in_specs),
        out_specs=out_specs, out_shape=out_shape,
        scratch_shapes=[pltpu.VMEM((tm, D) if nk > 1 else (8, 128), F32)],
        compiler_params=pltpu.CompilerParams(dimension_semantics=("arbitrary", "arbitrary")),
    )(lead["a"], lead["b"], *deps, *ins)


def _rblk(tm, w=D, col=0):
    return pl.BlockSpec((tm, w), lambda i, k: (i, col))


def _rrow(w=D):
    return pl.BlockSpec((1, w), lambda i, k: (0, 0))


def down_final(f, wdn, x1, g2, fg, tgt, *, name, tm=512):
    def fn(i, d, in_refs, out_refs):
        x1_ref, g2_ref, fg_ref, t_ref = in_refs
        d_ref, dx_ref, dd_ref, dfg_ref, loss_ref = out_refs
        d_ref[...] = d
        xv = x1_ref[...] + g2_ref[...] * d
        r = lax.rsqrt(jnp.mean(xv * xv, axis=-1, keepdims=True) + EPS)
        xh = xv * r
        diff = xh * fg_ref[...] - t_ref[...]
        part = 0.5 * jnp.sum(jnp.mean(diff * diff, axis=-1, keepdims=True), axis=0, keepdims=True)
        dy = diff * (1.0 / D)
        a = dy * fg_ref[...]
        dx = r * (a - xh * jnp.mean(a * xh, axis=-1, keepdims=True))
        dx_ref[...] = dx
        dd_ref[...] = (dx * g2_ref[...]).astype(BF)
        dfg = jnp.sum(dy * xh, axis=0, keepdims=True)

        @pl.when(i == 0)
        def _():
            dfg_ref[...] = dfg
            loss_ref[...] = jnp.broadcast_to(part, (1, 128))

        @pl.when(i > 0)
        def _():
            dfg_ref[...] += dfg
            loss_ref[...] += jnp.broadcast_to(part, (1, 128))

    blk = _rblk(tm)
    return rows_call(
        dict(a=f, b=wdn, tk=DFF), [x1, g2, fg, tgt], [blk, _rrow(), _rrow(), blk],
        [jax.ShapeDtypeStruct((T, D), F32), jax.ShapeDtypeStruct((T, D), F32), jax.ShapeDtypeStruct((T, D), BF),
         jax.ShapeDtypeStruct((1, D), F32), jax.ShapeDtypeStruct((1, 128), F32)],
        [blk, blk, blk, _rrow(), _rrow(128)], fn, name=name, R=T, tm=tm)


def oproj_resid(merged, wo, x, gate, g, sc, sh, *, name, tm=512):
    def fn(i, a, in_refs, out_refs):
        x_ref, gate_ref, g_ref, sc_ref, sh_ref = in_refs
        a_ref, x1_ref, h_ref = out_refs
        a_ref[...] = a
        xv = x_ref[...] + gate_ref[...] * a
        x1_ref[...] = xv
        r = lax.rsqrt(jnp.mean(xv * xv, axis=-1, keepdims=True) + EPS)
        h_ref[...] = ((xv * r * g_ref[...]) * (1.0 + sc_ref[...]) + sh_ref[...]).astype(BF)

    blk = _rblk(tm)
    return rows_call(
        dict(a=merged, b=wo, tk=D), [x, gate, g, sc, sh], [blk, _rrow(), _rrow(), _rrow(), _rrow()],
        [jax.ShapeDtypeStruct((T, D), F32), jax.ShapeDtypeStruct((T, D), F32), jax.ShapeDtypeStruct((T, D), BF)],
        [blk, blk, blk], fn, name=name, R=T, tm=tm)


def oproj_dx_gate_bwd(da, wo, p, ya, yc, wao, wco, *, name, tm=512):
    kin = wao.shape[0]

    def fn(i, dm, in_refs, out_refs):
        ga_ref, gc_ref, ya_ref, yc_ref, wa_ref, wc_ref = in_refs
        dya_ref, dyc_ref, dp_ref, do_ref, dz_ref = out_refs
        sa, sc_ = jax.nn.sigmoid(ga_ref[...]), jax.nn.sigmoid(gc_ref[...])
        dya, dyc = (dm * sa).astype(BF), (dm * sc_).astype(BF)
        dya_ref[...] = dya
        dyc_ref[...] = dyc
        dp_ref[:, 0:D] = (dm * ya_ref[...] * (sa * (1.0 - sa))).astype(BF)
        dp_ref[:, D:2 * D] = (dm * yc_ref[...] * (sc_ * (1.0 - sc_))).astype(BF)
        nt = (((1,), (1,)), ((), ()))
        do_ref[...] = lax.dot_general(dya, wa_ref[...], nt, preferred_element_type=F32).astype(BF)
        dz_ref[...] = lax.dot_general(dyc, wc_ref[...], nt, preferred_element_type=F32)

    blk = _rblk(tm)
    sh = jax.ShapeDtypeStruct((T, D), BF)
    wsp = pl.BlockSpec((kin, D), lambda i, k: (0, 0))
    return rows_call(
        dict(a=da, b=wo, tb=True, tk=D), [p, p, ya, yc, wao, wco],
        [_rblk(tm, D, O_GA // D), _rblk(tm, D, O_GC // D), blk, blk, wsp, wsp],
        [sh, sh, jax.ShapeDtypeStruct((T, NIN), BF), jax.ShapeDtypeStruct((T, kin), BF), jax.ShapeDtypeStruct((T, kin), F32)],
        [blk, blk, _rblk(tm, 2 * D), _rblk(tm, kin), _rblk(tm, kin)], fn, name=name, R=T, tm=tm)


def normmod_bwd(x, dh, g, sc, dres, gsrc, gate, *, name, tm=512):
    R = x.shape[0]
    tm = min(tm, R)
    has_res = dres is not None
    fused = isinstance(dh, dict)
    if fused:
        tb, a_stack, tk = dh.get("tb", False), dh.get("a_stack", False), dh["tk"]
        K = 2 * dh["a"].shape[2] if a_stack else dh["a"].shape[1]
        nk = K // tk
        deps = [] if dh.get("dep") is None else [dh["dep"]]
        n_dh = 2 + len(deps)
    else:
        nk, n_dh = 1, 1

    def elementwise(i, dhv, x_ref, g_ref, sc_ref, res_refs, out_refs):
        xv = x_ref[...]
        r = lax.rsqrt(jnp.mean(xv * xv, axis=-1, keepdims=True) + EPS)
        xh = xv * r
        n = xh * g_ref[...]
        dn = dhv * (1.0 + sc_ref[...])
        a = dn * g_ref[...]
        rows = [jnp.sum(dhv, axis=0, keepdims=True), jnp.sum(dhv * n, axis=0, keepdims=True),
                jnp.sum(dn * xh, axis=0, keepdims=True)]
        if has_res:
            dres_ref, gsrc_ref, gate_ref = res_refs
            dx_ref, dxg_ref, st_ref = out_refs
            dr = dres_ref[...]
            dx = dr + r * (a - xh * jnp.mean(a * xh, axis=-1, keepdims=True))
            dx_ref[...] = dx
            dxg_ref[...] = (dx * gate_ref[...]).astype(BF)
            rows.append(jnp.sum(dr * gsrc_ref[...], axis=0, keepdims=True))
        else:
            st_ref, = out_refs
            rows.append(jnp.zeros((1, D), F32))

        @pl.when(i == 0)
        def _():
            for k, row in enumerate(rows):
                st_ref[k:k + 1, :] = row

        @pl.when(i > 0)
        def _():
            for k, row in enumerate(rows):
                st_ref[k:k + 1, :] += row

    def body(*refs):
        x_ref, dh_refs, g_ref, sc_ref = refs[0], refs[1:1 + n_dh], refs[1 + n_dh], refs[2 + n_dh]
        rest = refs[3 + n_dh:]
        res_refs, rest = (rest[:3], rest[3:]) if has_res else ((), rest)
        out_refs = rest[:3] if has_res else rest[:1]
        i = pl.program_id(0)
        if not fused:
            elementwise(i, dh_refs[0][...], x_ref, g_ref, sc_ref, res_refs, out_refs)
            return
        acc = rest[-1]
        k = pl.program_id(1)
        part = lax.dot_general(dh_refs[0][...].astype(BF), dh_refs[1][...].astype(BF),
                               (((1,), (1 if tb else 0,)), ((), ())), preferred_element_type=F32)
        if nk == 1:
            elementwise(i, part, x_ref, g_ref, sc_ref, res_refs, out_refs)
            return

        @pl.when(k == 0)
        def _():
            acc[...] = part

        @pl.when(k > 0)
        def _():
            acc[...] += part

        @pl.when(k == nk - 1)
        def _():
            elementwise(i, acc[...], x_ref, g_ref, sc_ref, res_refs, out_refs)

    rowb = lambda w: pl.BlockSpec((1, w), lambda i, *k: (0, 0))
    blk = pl.BlockSpec((tm, D), lambda i, *k: (i, 0))
    st_spec = pl.BlockSpec((4, D), lambda i, *k: (0, 0))
    st_shape = jax.ShapeDtypeStruct((4, D), F32)
    if fused:
        if a_stack:
            nhb = K // 2 // tk
            a_spec = pl.BlockSpec((None, tm, tk), lambda i, k: (k // nhb, i, k % nhb))
        else:
            a_spec = pl.BlockSpec((tm, tk), lambda i, k: (i, k))
        b_spec = pl.BlockSpec((D, tk), lambda i, k: (0, k)) if tb else pl.BlockSpec((tk, D), lambda i, k: (k, 0))
        dh_specs = [a_spec, b_spec] + [pl.BlockSpec(memory_space=pl.ANY)] * len(deps)
        dh_args = [dh["a"], dh["b"]] + deps
        grid, sem = (R // tm, nk), ("arbitrary", "arbitrary")
        scratch = [pltpu.VMEM((tm, D) if nk > 1 else (8, 128), F32)]
    else:
        dh_specs, dh_args, grid, sem, scratch = [blk], [dh], (R // tm,), ("arbitrary",), []
    cp = pltpu.CompilerParams(dimension_semantics=sem)
    if has_res:
        return pl.pallas_call(
            body, name=name, grid=grid, in_specs=[blk] + dh_specs + [rowb(D), rowb(D), blk, blk, rowb(D)],
            out_specs=[blk, blk, st_spec], scratch_shapes=scratch,
            out_shape=[jax.ShapeDtypeStruct((R, D), F32), jax.ShapeDtypeStruct((R, D), BF), st_shape],
            compiler_params=cp,
        )(x, *dh_args, g, sc, dres, gsrc, gate)
    return pl.pallas_call(
        body, name=name, grid=grid, in_specs=[blk] + dh_specs + [rowb(D), rowb(D)],
        out_specs=st_spec, out_shape=st_shape, scratch_shapes=scratch, compiler_params=cp,
    )(x, *dh_args, g, sc)


def ffn_act_bwd(u0, df, cw, cb, *, name, tc=128):
    nb = DFF // tc

    def body(u_ref, df_ref, wg_ref, wv_ref, bg_ref, bv_ref, du_ref, dw_ref, db_ref):
        wg = [wg_ref[k:k + 1, :] for k in range(3)]
        wv = [wv_ref[k:k + 1, :] for k in range(3)]
        bg, bv = bg_ref[...], bv_ref[...]

        def chunk(r0, first, last, acc):
            xg, xv = _ext_rows(u_ref.at[0], r0, first, last), _ext_rows(u_ref.at[1], r0, first, last)
            dfe = _ext_rows(df_ref, r0, first, last)
            xg_d, xg_u, xv_d, xv_u = _roll_dn(xg), _roll_up(xg), _roll_dn(xv), _roll_up(xv)
            ug = bg + xg_d * wg[0] + xg * wg[1] + xg_u * wg[2]
            uv = bv + xv_d * wv[0] + xv * wv[1] + xv_u * wv[2]
            sig = jax.nn.sigmoid(ug)
            dug = dfe * uv * (sig * (1.0 + ug * (1.0 - sig)))
            duv = dfe * (ug * sig)
            rows = _center_rows(r0, first, last)
            du_ref[0, rows, :] = (_roll_up(dug) * wg[0] + dug * wg[1] + _roll_dn(dug) * wg[2])[_CTR].astype(BF)
            du_ref[1, rows, :] = (_roll_up(duv) * wv[0] + duv * wv[1] + _roll_dn(duv) * wv[2])[_CTR].astype(BF)
            terms = [dug * xg_d, dug * xg, dug * xg_u, dug, duv * xv_d, duv * xv, duv * xv_u, duv]
            return tuple(a + jnp.sum(t[_CTR], axis=0, keepdims=True) for a, t in zip(acc, terms))

        acc = _row_chunks(T, chunk, tuple(jnp.zeros((1, tc), F32) for _ in range(8)))
        for k in range(3):
            dw_ref[0, k:k + 1, :] = acc[k]
            dw_ref[1, k:k + 1, :] = acc[4 + k]
        db_ref[0] = acc[3]
        db_ref[1] = acc[7]

    lo = lambda r: pl.BlockSpec((r, tc), lambda j: (0, j))
    hi = lambda r: pl.BlockSpec((r, tc), lambda j: (0, nb + j))
    st = lambda r: pl.BlockSpec((2, r, tc), lambda j: (0, 0, j))
    return pl.pallas_call(
        body, name=name, grid=(nb,),
        in_specs=[st(T), lo(T), lo(3), hi(3), lo(1), hi(1)],
        out_specs=[st(T), st(3), st(1)],
        out_shape=[jax.ShapeDtypeStruct((2, T, DFF), BF), jax.ShapeDtypeStruct((2, 3, DFF), F32),
                   jax.ShapeDtypeStruct((2, 1, DFF), F32)],
        compiler_params=pltpu.CompilerParams(dimension_semantics=("parallel",)),
    )(u0, df, cw, cw, cb, cb)


def convz_bwd(p, dz, cw, cb, dp, *, name):
    o0 = O_CV // (3 * CVB)

    def body(p_ref, dz_ref, w_ref, bias_ref, dp_in, dp_ref, dw_ref, dbias_ref):
        xv, bv, cv = p_ref[:, 0:CVB], p_ref[:, CVB:2 * CVB], p_ref[:, 2 * CVB:3 * CVB]
        ci = cv * xv
        dwc = _conv(ci, w_ref, bias_ref)
        dzv = dz_ref[...]
        ddw = dzv * bv
        dci = _conv_t(ddw, w_ref)
        dp_ref[:, 0:CVB] = (dci * cv).astype(BF)
        dp_ref[:, CVB:2 * CVB] = (dzv * dwc).astype(BF)
        dp_ref[:, 2 * CVB:3 * CVB] = (dci * xv).astype(BF)
        _conv_wgrad(dw_ref, ddw, ci)
        dbias_ref[...] = jnp.sum(ddw, axis=0, keepdims=True)

    own = lambda r: pl.BlockSpec((r, CVB), lambda j: (0, j))
    return pl.pallas_call(
        body, name=name, grid=(CONV // CVB,),
        in_specs=[pl.BlockSpec((T, 3 * CVB), lambda j: (0, o0 + j)), own(T), own(3), own(1),
                  pl.BlockSpec(memory_space=pl.ANY)],
        out_specs=[pl.BlockSpec((T, 3 * CVB), lambda j: (0, o0 + j)), own(3), own(1)],
        out_shape=[jax.ShapeDtypeStruct((T, NIN), BF), jax.ShapeDtypeStruct((3, CONV), F32),
                   jax.ShapeDtypeStruct((1, CONV), F32)],
        input_output_aliases={4: 0},
        compiler_params=pltpu.CompilerParams(dimension_semantics=("parallel",)),
    )(p, dz, cw, cb, dp)


def attn_bwd(q, k, v, do, o, lse, dep, *, name, tq=1024, kc=768):
    NKC, KC = TKV // kc, kc
    deps = [] if dep is None else [dep]

    def body(q_ref, k_ref, v_ref, do_ref, o_ref, lse_ref, *rest):
        dq_ref, dk_ref, dv_ref = rest[len(deps):]
        h, i = pl.program_id(0), pl.program_id(1)

        @pl.when(i == 0)
        def _():
            dk_ref[...] = jnp.zeros_like(dk_ref)

        @pl.when((i == 0) & (h % 2 == 0))
        def _():
            dv_ref[...] = jnp.zeros_like(dv_ref)

        qv = q_ref[...]
        dom = jnp.where(_head_mask(h), do_ref[...], jnp.zeros_like(do_ref[...]))
        delta = jnp.sum(dom.astype(F32) * o_ref[...].astype(F32), axis=-1, keepdims=True)
        lse = lse_ref[:, 0:1]
        dq = jnp.zeros((tq, HP), F32)
        for c in range(NKC):
            cols = slice(c * KC, (c + 1) * KC)
            s = lax.dot_general(qv, k_ref[cols, :], (((1,), (1,)), ((), ())),
                                preferred_element_type=F32) * (SCALE * LOG2E)
            pr = jnp.exp2(s - lse)
            dp = lax.dot_general(dom, v_ref[cols, :], (((1,), (1,)), ((), ())), preferred_element_type=F32)
            ds = (pr * (dp - delta) * SCALE).astype(BF)
            dq = dq + jnp.dot(ds, k_ref[cols, :], preferred_element_type=F32)
            dk_ref[cols, :] += lax.dot_general(ds, qv, (((0,), (0,)), ((), ())), preferred_element_type=F32)
            dv_ref[cols, :] += lax.dot_general(pr.astype(BF), dom, (((0,), (0,)), ((), ())), preferred_element_type=F32)
        dq_ref[...] = dq

    return pl.pallas_call(
        body, name=name, grid=(NH, T // tq),
        in_specs=[pl.BlockSpec((tq, HP), lambda h, i: (i, h)), pl.BlockSpec((TKV, HP), lambda h, i: (0, h)),
                  pl.BlockSpec((TKV, 2 * DV), lambda h, i: (0, h // 2)), pl.BlockSpec((tq, 2 * DV), lambda h, i: (i, h // 2)),
                  pl.BlockSpec((tq, 2 * DV), lambda h, i: (i, h // 2)), pl.BlockSpec((tq, HP), lambda h, i: (i, h)),
                  *([pl.BlockSpec(memory_space=pl.ANY)] * len(deps))],
        out_specs=[pl.BlockSpec((tq, HP), lambda h, i: (i, h)), pl.BlockSpec((TKV, HP), lambda h, i: (0, h)),
                   pl.BlockSpec((TKV, 2 * DV), lambda h, i: (0, h // 2))],
        out_shape=[jax.ShapeDtypeStruct((T, NH * HP), F32), jax.ShapeDtypeStruct((TKV, NH * HP), F32),
                   jax.ShapeDtypeStruct((TKV, NH * DV), F32)],
        compiler_params=pltpu.CompilerParams(dimension_semantics=("arbitrary", "arbitrary")),
    )(q, k, v, do, o, lse, *deps)


def qprep_bwd(p, dq, qg, wq2, cq_t, sq_t, dp, *, name, tm=256):
    qcol = O_Q // 512

    def body(p_ref, dq_ref, g_ref, w_ref, c_ref, s_ref, dp_in, dp_ref, dq2_ref, dg_ref):
        i = pl.program_id(0)
        dqv = dq_ref[...]
        cc = jnp.concatenate([c_ref[...]] * NH, axis=1)
        ss = jnp.concatenate([s_ref[...]] * NH, axis=1)
        dq2 = jnp.concatenate([dqv * cc, dqv * ss], axis=1).astype(BF)
        dq2_ref[...] = dq2
        dcq = lax.dot_general(dq2, w_ref[...], (((1,), (1,)), ((), ())), preferred_element_type=F32)
        pq = p_ref[...]
        r = lax.rsqrt(jnp.sum(pq * pq, axis=-1, keepdims=True) * (1.0 / QL) + EPS)
        xh = pq * r
        a = dcq * g_ref[...]
        dp_ref[...] = (r * (a - xh * (jnp.sum(a * xh, axis=-1, keepdims=True) * (1.0 / QL)))).astype(BF)
        dg = jnp.sum(dcq * xh, axis=0, keepdims=True)

        @pl.when(i == 0)
        def _():
            dg_ref[...] = dg

        @pl.when(i > 0)
        def _():
            dg_ref[...] += dg

    return pl.pallas_call(
        body, name=name, grid=(T // tm,),
        in_specs=[pl.BlockSpec((tm, 512), lambda i: (i, qcol)), pl.BlockSpec((tm, NH * HP), lambda i: (i, 0)), _row(512),
                  pl.BlockSpec((512, 2 * NH * HP), lambda i: (0, 0)),
                  pl.BlockSpec((tm, HP), lambda i: (i, 0)), pl.BlockSpec((tm, HP), lambda i: (i, 0)),
                  pl.BlockSpec(memory_space=pl.ANY)],
        out_specs=[pl.BlockSpec((tm, 512), lambda i: (i, qcol)), pl.BlockSpec((tm, 2 * NH * HP), lambda i: (i, 0)), _row(512)],
        out_shape=[jax.ShapeDtypeStruct((T, NIN), BF), jax.ShapeDtypeStruct((T, 2 * NH * HP), BF),
                   jax.ShapeDtypeStruct((1, 512), F32)],
        input_output_aliases={6: 0},
        compiler_params=pltpu.CompilerParams(dimension_semantics=("arbitrary",)),
    )(p, dq, qg, wq2, cq_t, sq_t, dp)


def kvprep_bwd(pc, p, dk, dv, kvg, wkv2, ck, sk, dp, *, name, tm=256):
    assert tm == TC
    nb = TKV // tm
    kvcol = O_KV // 512

    def body(pc_ref, p_ref, dk_ref, dv_ref, g_ref, w_ref, ck_ref, sk_ref, dp_in, dp_ref, dpc_ref, dkv2_ref, dg_ref):
        i = pl.program_id(0)
        t = jnp.where(i == NLAT, pc_ref[...], p_ref[...])
        pk = t[:, :KVL]
        r = lax.rsqrt(jnp.mean(pk * pk, axis=-1, keepdims=True) + EPS)
        xh = pk * r
        dkv = dk_ref[...]
        dkv2 = jnp.concatenate([dkv, dv_ref[...]], axis=1).astype(BF)
        dkv2_ref[...] = dkv2
        dckv = lax.dot_general(dkv2, w_ref[...], (((1,), (1,)), ((), ())), preferred_element_type=F32)
        a = dckv * g_ref[...]
        dpk = r * (a - xh * jnp.mean(a * xh, axis=-1, keepdims=True))
        dkr = dkv[:, 0:HP]
        for hh in range(1, NH):
            dkr = dkr + dkv[:, hh * HP:(hh + 1) * HP]
        res = jnp.concatenate([dpk, dkr * ck_ref[...], dkr * sk_ref[...]], axis=1).astype(BF)
        dg = jnp.sum(dckv * xh, axis=0, keepdims=True)

        @pl.when(i == 0)
        def _():
            dg_ref[...] = dg

        @pl.when(i > 0)
        def _():
            dg_ref[...] += dg

        @pl.when(i < NLAT)
        def _():
            dp_ref[...] = res

        @pl.when(i == NLAT)
        def _():
            dpc_ref[...] = res

    rb = lambda w: pl.BlockSpec((tm, w), lambda i: (i, 0))
    return pl.pallas_call(
        body, name=name, grid=(nb,),
        in_specs=[pl.BlockSpec((tm, 512), lambda i: (0, 0)),
                  pl.BlockSpec((tm, 512), lambda i: (jnp.minimum(i, NLAT - 1), kvcol)),
                  rb(NH * HP), rb(NH * DV), _row(KVL), pl.BlockSpec((KVL, NH * HP + NH * DV), lambda i: (0, 0)),
                  rb(HP), rb(HP), pl.BlockSpec(memory_space=pl.ANY)],
        out_specs=[pl.BlockSpec((tm, 512), lambda i: (jnp.minimum(i, NLAT - 1), kvcol)),
                   pl.BlockSpec((tm, 512), lambda i: (0, 0)), rb(NH * HP + NH * DV), _row(KVL)],
        out_shape=[jax.ShapeDtypeStruct((T, NIN), BF), jax.ShapeDtypeStruct((TC, 512), BF),
                   jax.ShapeDtypeStruct((TKV, NH * HP + NH * DV), BF), jax.ShapeDtypeStruct((1, KVL), F32)],
        input_output_aliases={8: 0},
        compiler_params=pltpu.CompilerParams(dimension_semantics=("arbitrary",)),
    )(pc, p, dk, dv, kvg, wkv2, ck, sk, dp)


def _pieces(src, width, n):
    out, c = [], src
    while c < src + width:
        k = c // n
        w = min(src + width, (k + 1) * n) - c
        out.append((k, c - k * n, c - src, w))
        c += w
    return out


def _win_moves():
    mv = [(2208, 1024, O_GA), (3232, 1024, O_GC), (0, KVL, O_KV), (256, DR, O_KV + KVL + DN), (288, QL, O_Q)]
    mv += [(256 + _swap_start(g), 8, O_KV + KVL + HP + DN + 8 * g) for g in range(4)]
    for j in range(CONV // CVB):
        base = O_CV + 3 * CVB * j
        mv += [(672 + CVB * j, CVB, base), (1184 + CVB * j, CVB, base + CVB), (1696 + CVB * j, CVB, base + 2 * CVB)]
    return mv


_WIN_ZERO = [(O_KV + KVL, DN), (O_KV + KVL + DN + DR, HP - DN - DR), (O_KV + KVL + HP, DN),
             (O_KV + KVL + HP + DN + DR, HP - DN - DR), (O_Q + QL, 512 - QL)]


def build_win(g, *, name, tm=256):
    def body(g_ref, o_ref):
        for src, w, dst in _win_moves():
            for k, a, off, pw in _pieces(src, w, SH_IN):
                o_ref[:, dst + off:dst + off + pw] = g_ref[k, :, a:a + pw]
        for c0, w in _WIN_ZERO:
            o_ref[:, c0:c0 + w] = jnp.zeros((tm, w), o_ref.dtype)

    return pl.pallas_call(
        body, name=name, grid=(D // tm,), in_specs=[pl.BlockSpec((NDEV, tm, SH_IN), lambda i: (0, i, 0))],
        out_specs=pl.BlockSpec((tm, NIN), lambda i: (i, 0)), out_shape=jax.ShapeDtypeStruct((D, NIN), g.dtype),
        compiler_params=pltpu.CompilerParams(dimension_semantics=("parallel",)),
    )(g)


def shard_win_grad(dwt, dwct, *, name, col0=0, tc=256):
    n = dwt.shape[1]

    def body(dw_ref, dwc_ref, o_ref, kvs):
        kvs[...] = dw_ref[O_KV:O_KV + 512, :] + dwc_ref[...]

        def src(row, w):
            if O_KV <= row < O_KV + 512:
                return kvs[row - O_KV:row - O_KV + w, :]
            return dw_ref[row:row + w, :]

        for s, w, dst in _win_moves():
            if w == 8 or s == 256:
                continue
            for k, a, off, pw in _pieces(s, w, SH_IN):
                o_ref[k, a:a + pw, :] = src(dst + off, pw).astype(o_ref.dtype)
        for g in range(4):
            val = src(O_KV + KVL + DN + 8 * g, 8) + src(O_KV + KVL + HP + DN + _swap_start(g), 8)
            o_ref[0, 256 + 8 * g:256 + 8 * g + 8, :] = val.astype(o_ref.dtype)

    return pl.pallas_call(
        body, name=name, grid=(n // tc,),
        in_specs=[pl.BlockSpec((NIN, tc), lambda j: (0, j)), pl.BlockSpec((512, tc), lambda j: (0, j + col0 // tc))],
        out_specs=pl.BlockSpec((NDEV, SH_IN, tc), lambda j: (0, 0, j)),
        out_shape=jax.ShapeDtypeStruct((NDEV, SH_IN, n), BF),
        scratch_shapes=[pltpu.VMEM((512, tc), F32)],
        compiler_params=pltpu.CompilerParams(dimension_semantics=("parallel",)),
    )(dwt, dwct)


def _eye(n, m):
    return (lax.broadcasted_iota(jnp.int32, (n, m), 0) == lax.broadcasted_iota(jnp.int32, (n, m), 1)).astype(BF)


_NT = (((1,), (1,)), ((), ()))


def build_wq_wkv(gq, gkv, *, name):
    def body(gq_ref, gkv_ref, q_ref, kv_ref):
        q_ref[...] = jnp.zeros_like(q_ref)
        kv_ref[...] = jnp.zeros_like(kv_ref)
        eye = _eye(QL, QL)
        for h in range(NH):
            qh = lax.dot_general(eye, gq_ref[h], _NT, preferred_element_type=F32).astype(q_ref.dtype)
            q_ref[0:QL, h * HP:h * HP + DN + DR] = qh
            for g in range(4):
                c0 = NH * HP + h * HP + DN + 8 * g
                q_ref[0:QL, c0:c0 + 8] = qh[:, DN + _swap_start(g):DN + _swap_start(g) + 8]
            kv_ref[:, h * HP:h * HP + DN] = gkv_ref[h, :, 0:DN]
            kv_ref[:, NH * HP + h * DV:NH * HP + (h + 1) * DV] = gkv_ref[h, :, DN:DN + DV]

    vm = pl.BlockSpec(memory_space=pltpu.VMEM)
    return pl.pallas_call(
        body, name=name, in_specs=[vm, vm], out_specs=[vm, vm],
        out_shape=[jax.ShapeDtypeStruct((512, 2 * NH * HP), gq.dtype), jax.ShapeDtypeStruct((KVL, NH * HP + NH * DV), gq.dtype)],
    )(gq, gkv)


def shard_wq_wkv_grad(dwq2, dwkv2, *, name):
    def body(q_ref, kv_ref, gq_ref, gkv_ref, xs):
        xs[...] = jnp.zeros_like(xs)
        eye = _eye(DN + DR, HP)
        for h in range(NH):
            xs[:, 0:DN] = q_ref[0:QL, h * HP:h * HP + DN].astype(BF)
            for g in range(4):
                a = q_ref[0:QL, h * HP + DN + 8 * g:h * HP + DN + 8 * g + 8]
                c0 = NH * HP + h * HP + DN + _swap_start(g)
                xs[:, DN + 8 * g:DN + 8 * g + 8] = (a + q_ref[0:QL, c0:c0 + 8]).astype(BF)
            gq_ref[h] = lax.dot_general(eye, xs[...], _NT, preferred_element_type=F32).astype(BF)
            gkv_ref[h, :, 0:DN] = kv_ref[:, h * HP:h * HP + DN].astype(BF)
            gkv_ref[h, :, DN:DN + DV] = kv_ref[:, NH * HP + h * DV:NH * HP + (h + 1) * DV].astype(BF)

    vm = pl.BlockSpec(memory_space=pltpu.VMEM)
    return pl.pallas_call(
        body, name=name, in_specs=[vm, vm], out_specs=[vm, vm],
        out_shape=[jax.ShapeDtypeStruct((NDEV, DN + DR, QL), BF), jax.ShapeDtypeStruct((NDEV, KVL, DN + DV), BF)],
        scratch_shapes=[pltpu.VMEM((QL, HP), BF)],
    )(dwq2, dwkv2)


def unshard_cols(g, *, name, tm=256):
    _, K, n = g.shape
    tm = _pick(K, tm, 16)

    def body(g_ref, o_ref):
        for k in range(NDEV):
            o_ref[:, k * n:(k + 1) * n] = g_ref[k]

    return pl.pallas_call(
        body, name=name, grid=(K // tm,), in_specs=[pl.BlockSpec((NDEV, tm, n), lambda i: (0, i, 0))],
        out_specs=pl.BlockSpec((tm, NDEV * n), lambda i: (i, 0)), out_shape=jax.ShapeDtypeStruct((K, NDEV * n), g.dtype),
        compiler_params=pltpu.CompilerParams(dimension_semantics=("parallel",)),
    )(g)


def shard_cols(w, *, name, tm=256):
    K, n8 = w.shape
    n = n8 // NDEV
    tm = _pick(K, tm, 16)

    def body(w_ref, o_ref):
        for k in range(NDEV):
            o_ref[k] = w_ref[:, k * n:(k + 1) * n]

    return pl.pallas_call(
        body, name=name, grid=(K // tm,), in_specs=[pl.BlockSpec((tm, n8), lambda i: (i, 0))],
        out_specs=pl.BlockSpec((NDEV, tm, n), lambda i: (0, i, 0)), out_shape=jax.ShapeDtypeStruct((NDEV, K, n), w.dtype),
        compiler_params=pltpu.CompilerParams(dimension_semantics=("parallel",)),
    )(w)


def _rope_tables():
    t = np.arange(T)
    row = (t // GRID_W).astype(np.float32)
    col = (t % GRID_W).astype(np.float32)
    axis_dim = DR // 2
    inv = (np.float32(ROPE_THETA) ** (-np.arange(0, axis_dim, 2, dtype=np.float32) / np.float32(axis_dim))).astype(np.float32)
    ar, ac = (row[:, None] * inv).astype(np.float32), (col[:, None] * inv).astype(np.float32)
    cosv = np.concatenate([np.cos(ar), np.cos(ar), np.cos(ac), np.cos(ac)], axis=1).astype(np.float32)
    sinv = np.concatenate([-np.sin(ar), np.sin(ar), -np.sin(ac), np.sin(ac)], axis=1).astype(np.float32)
    ck = np.zeros((TKV, HP), np.float32)
    sk = np.zeros((TKV, HP), np.float32)
    ck[T:, DN:DN + DR] = 1.0
    ck[:T, DN:DN + DR] = cosv
    sk[:T, DN:DN + DR] = sinv
    cq = np.zeros((T, HP), np.float32)
    cq[:, :DN] = 1.0
    cq[:, DN:DN + DR] = cosv
    return jnp.asarray(ck), jnp.asarray(sk), jnp.asarray(cq), jnp.asarray(sk[:T])


def _local_step(x, ctx, tgt, mod_lat, mod_ctx, n1g, qg, kvg, n2g, fg, conv_w, conv_b, ffn_w, ffn_b, get_w, put_g, dep0):
    sh1, sc1, g1, sh2, sc2, g2 = [mod_lat[:, i * D:(i + 1) * D] for i in range(6)]
    csh1, csc1 = mod_ctx[:, 0:D], mod_ctx[:, D:2 * D]
    ck, sk, cq_t, sq_t = _rope_tables()
    qg_p = jnp.pad(qg, ((0, 0), (0, 512 - QL)))

    hcat = normmod_cat(ctx, x, n1g, csc1, csh1, sc1, sh1, dep0, name="normmod1")
    win = get_w("in", hcat)
    p = mm(hcat, win, M=T, tn=768, name="in_proj")
    pc = mm(hcat, win, M=TC, N=512, a_off=(T, 0), b_off=(0, O_KV), name="in_proj_ctx")
    wq2, wkv2, wao, wco, wo = get_w("mid", p)
    kh, vh, ckv = kvprep(pc, p, kvg, wkv2, ck, sk, name="kvprep")
    qr, cq = qprep(p, qg_p, wq2, cq_t, sq_t, name="qprep")
    o, lse = attn_fwd(qr, kh, vh, name="attn_fwd")
    z = convz(p, conv_w, conv_b, name="convz")
    ya, yc, merged = out_proj_merge(o, wao, z, wco, p, name="attn_conv_out_gate_merge")
    a_out, x1, h2 = oproj_resid(merged, wo, x, g1, n2g, sc2, sh2, name="o_proj_resid_normmod2")
    wup = get_w("up", h2)
    u0 = mm(h2, wup, tb=True, o_stack=True, tn=1408, name="up_proj")
    f = ffn_act(u0, ffn_w, ffn_b, name="ffn_act")
    wdn = get_w("down", f)
    dn, dx2, dd, dfg, loss = down_final(f, wdn, x1, g2, fg, tgt, name="down_proj_final_loss")

    df = mm(dd, wdn, tb=True, tn=1408, name="down_proj_dx")
    dwdn = mm(f, dd, ta=True, out_dtype=BF, tm=1408, name="down_proj_dw")
    du0, dffn_w, dffn_b = ffn_act_bwd(u0, df, ffn_w, ffn_b, name="ffn_act_bwd")
    dwup = mm(du0, h2, ta=True, a_stack=True, out_dtype=BF, tm=1408, name="up_proj_dw")
    tok = put_g("ffn", dict(dwup=dwup, dwdn=dwdn))
    dx1, da, st2 = normmod_bwd(x1, dict(a=du0, b=wup, a_stack=True, tk=DFF, dep=tok), n2g, sc2, dx2, dn, g1,
                               name="up_proj_dx_normmod2_bwd")

    dwo = mm(merged, da, ta=True, out_dtype=BF, tn=512, name="o_proj_dw")
    dya, dyc, dp, do, dz = oproj_dx_gate_bwd(da, wo, p, ya, yc, wao, wco, name="o_proj_dx_gate_merge_bwd")
    dwao = mm(o, dya, ta=True, out_dtype=BF, tn=512, name="attn_out_dw")
    dwco = mm(z, dyc, ta=True, out_dtype=BF, tn=512, name="conv_out_dw")
    tok = put_g("mid", dict(dwao=dwao, dwco=dwco, dwo=dwo))
    dp, dconv_w, dconv_b = convz_bwd(p, dz, conv_w, conv_b, dp, name="convz_bwd")
    dq, dk, dv = attn_bwd(qr, kh, vh, do, o, lse, tok, name="attn_bwd")
    dp, dq2, dqg = qprep_bwd(p, dq, qg_p, wq2, cq_t, sq_t, dp, name="qprep_bwd")
    dp, dpc, dkv2, dkvg = kvprep_bwd(pc, p, dk, dv, kvg, wkv2, ck, sk, dp, name="kvprep_bwd")

    dwin_c = mm(dpc, hcat, ta=True, K=TC, b_off=(T, 0), name="in_proj_ctx_dw")
    tok = None
    for half in range(2):
        dwin = mm(dp, hcat, ta=True, K=T, N=D // 2, b_off=(0, half * (D // 2)), tm=768, dep=tok,
                  name=f"in_proj_dw_{half}")
        tok = put_g(f"in{half}", dict(dwin=dwin, dwin_c=dwin_c, col0=half * (D // 2)))
    dwq2 = mm(cq, dq2, ta=True, dep=tok, name="q_up_dw")
    dwkv2 = mm(ckv, dkv2, ta=True, dep=tok, name="kv_up_dw")
    tok = put_g("qkv", dict(dwq2=dwq2, dwkv2=dwkv2))
    dhc = mm(dpc, win, tb=True, N=D, K=512, b_off=(0, O_KV), dep=tok, name="in_proj_ctx_dx")
    dx, _, st1 = normmod_bwd(x, dict(a=dp, b=win, tb=True, tk=NIN, dep=tok), n1g, sc1, dx1, a_out, g1,
                             name="in_proj_dx_normmod1_bwd")
    stc = normmod_bwd(ctx, dhc, n1g, csc1, None, None, None, name="normmod1_ctx_bwd")

    return dict(loss=loss, dx=dx, st1=st1, st2=st2, stc=stc, dqg=dqg, dkvg=dkvg, dfg=dfg,
                dconv_w=dconv_w, dconv_b=dconv_b, dffn_w=dffn_w, dffn_b=dffn_b)


def _me():
    x, y, c = lax.axis_index("x"), lax.axis_index("y"), lax.axis_index("c")
    return x, y, c, 4 * x + 2 * y + c


def _peer(x, y, c, k):
    px = 1 - x if k & 4 else x
    py = 1 - y if k & 2 else y
    pc = 1 - c if k & 1 else c
    return (px, py, pc), 4 * px + 2 * py + pc


def _exchange_tiles(src_of_peer, buf, send_sem, recv_sem):
    _exchange_tiles_start(src_of_peer, buf, send_sem, recv_sem)
    _exchange_tiles_wait(buf, send_sem, recv_sem)


def _exchange_tiles_start(src_of_peer, buf, send_sem, recv_sem):
    x, y, c, me = _me()
    for k in range(1, NDEV):
        dev, lin = _peer(x, y, c, k)
        pltpu.make_async_remote_copy(src_ref=src_of_peer(lin), dst_ref=buf.at[me], send_sem=send_sem, recv_sem=recv_sem,
                                     device_id=dev, device_id_type=MESH).start()


def _exchange_tiles_wait(buf, send_sem, recv_sem):
    x, y, c, me = _me()
    seven = buf.at[pl.ds(0, NDEV - 1)]
    pltpu.make_async_remote_copy(src_ref=seven, dst_ref=seven, send_sem=send_sem, recv_sem=recv_sem,
                                 device_id=(x, y, c), device_id_type=MESH).wait()


def _silu(z):
    return z * jax.nn.sigmoid(z)


def ada_fwd(c, c_ctx, ffn_w, conv_w, w_shard, b_ada, deps, *, name):
    nsh, nf, nc = w_shard.shape[1], ffn_w.shape[2], conv_w.shape[2]
    deps = [d for d in deps if d is not None]

    def body(c_ref, cc_ref, fw_ref, cw_ref, w_ref, b_ref, *rest):
        s_ref, ml_ref, mc_ref, fwf_ref, cwf_ref, m_ref, mine, res, sems = rest[len(deps):]
        x, y, c, me = _me()
        mine[...] = jnp.zeros_like(mine)
        mine[0:1, :] = _silu(c_ref[...])
        mine[1:2, :] = _silu(cc_ref[...])
        for k in range(3):
            mine[2 + k:3 + k, 0:fw_ref.shape[2]] = fw_ref[k]
            mine[5 + k:6 + k, 0:cw_ref.shape[2]] = cw_ref[k]
        s_ref[me] = mine[...]
        _exchange_tiles(lambda lin: mine, s_ref, sems.at[0], sems.at[1])
        sall = s_ref[...].reshape(NDEV * 8, D).astype(BF)
        r = jnp.dot(sall, w_ref[...].astype(BF), preferred_element_type=F32) + b_ref[me]
        res[...] = r.reshape(NDEV, 8, nsh)
        m_ref[me] = res[me]
        _exchange_tiles(lambda lin: res.at[lin], m_ref, sems.at[2], sems.at[3])
        for j in range(NDEV):
            ml_ref[:, j * nsh:(j + 1) * nsh] = m_ref[j, 0:1, :]
            mc_ref[:, j * nsh:(j + 1) * nsh] = m_ref[j, 1:2, :]
            fwf_ref[:, j * nf:(j + 1) * nf] = s_ref[j, 2:5, 0:nf]
            cwf_ref[:, j * nc:(j + 1) * nc] = s_ref[j, 5:8, 0:nc]

    vm = pl.BlockSpec(memory_space=pltpu.VMEM)
    return pl.pallas_call(
        body, name=name, in_specs=[vm] * 6 + [pl.BlockSpec(memory_space=pl.ANY)] * len(deps), out_specs=[vm] * 5,
        out_shape=[jax.ShapeDtypeStruct((NDEV, 8, D), F32), jax.ShapeDtypeStruct((1, NDEV * nsh), F32),
                   jax.ShapeDtypeStruct((1, NDEV * nsh), F32), jax.ShapeDtypeStruct((3, NDEV * nf), F32),
                   jax.ShapeDtypeStruct((3, NDEV * nc), F32)],
        scratch_shapes=[pltpu.VMEM((NDEV, 8, nsh), F32), pltpu.VMEM((8, D), F32), pltpu.VMEM((NDEV, 8, nsh), F32),
                        pltpu.SemaphoreType.DMA((4,))],
    )(c, c_ctx, ffn_w, conv_w, w_shard, b_ada, *deps)


P_DML, P_DMC, P_N1, P_QG, P_KVG, P_CB, P_N2, P_FB, P_FG, P_CW, P_FW, P_LOSS, P_ROWS = 0, 8, 16, 17, 18, 19, 20, 21, 27, 28, 31, 49, 56
NSH = 6 * D // NDEV
FROWS = 3


def pack_small(r, *, name):
    ins = [r["st1"], r["st2"], r["stc"], r["dqg"], r["dkvg"], r["dconv_b"], r["dffn_b"], r["dfg"], r["dconv_w"],
           r["dffn_w"], r["loss"]]

    def put_wide(p, row0, row, n):
        for j in range(-(-n // D)):
            w = min(D, n - j * D)
            p[row0 + j:row0 + j + 1, 0:w] = row[:, j * D:j * D + w]

    def body(st1, st2, stc, qg, kvg, cb, fb, fg, cw, fw, loss, p):
        p[...] = jnp.zeros_like(p)
        lat = (st1.at[0:1], st1.at[1:2], st1.at[3:4], st2.at[0:1], st2.at[1:2], st2.at[3:4])
        ctx = (stc.at[0:1], stc.at[1:2])
        for j in range(NDEV):
            done = 0
            while done < NSH:
                q, off = divmod(j * NSH + done, D)
                w = min(D - off, NSH - done)
                p[P_DML + j:P_DML + j + 1, done:done + w] = lat[q][:, off:off + w]
                if q < len(ctx):
                    p[P_DMC + j:P_DMC + j + 1, done:done + w] = ctx[q][:, off:off + w]
                done += w
        p[P_N1:P_N1 + 1, :] = st1[2:3, :] + stc[2:3, :]
        p[P_N2:P_N2 + 1, :] = st2[2:3, :]
        put_wide(p, P_QG, qg, 512)
        put_wide(p, P_KVG, kvg, KVL)
        put_wide(p, P_CB, cb, CONV)
        put_wide(p, P_FG, fg, D)
        put_wide(p, P_LOSS, loss, 128)
        for s in range(2):
            put_wide(p, P_FB + FROWS * s, fb.at[s], DFF)
        for k in range(3):
            put_wide(p, P_CW + k, cw.at[k:k + 1], CONV)
            for s in range(2):
                put_wide(p, P_FW + FROWS * (2 * k + s), fw.at[s, k:k + 1], DFF)

    vm = pl.BlockSpec(memory_space=pltpu.VMEM)
    return pl.pallas_call(
        body, name=name, in_specs=[vm] * len(ins), out_specs=vm, out_shape=jax.ShapeDtypeStruct((P_ROWS, D), F32),
    )(*ins)


def sum_slots(a, *, name):
    rows = dict(norm1_g=(P_N1, D), q_norm_g=(P_QG, QL), kv_norm_g=(P_KVG, KVL), conv_b=(P_CB, CONV), norm2_g=(P_N2, D),
                final_g=(P_FG, D))

    def body(a_ref, sum_ref, *out):
        acc = a_ref[0]
        for k in range(1, NDEV):
            acc = acc + a_ref[k]
        sum_ref[...] = acc
        for ref, (row, n) in zip(out, rows.values()):
            ref[...] = sum_ref[row:row + 1, 0:n]
        fb, bada = out[len(rows):]
        for s in range(2):
            for j in range(FROWS):
                w = min(D, DFF - j * D)
                row = P_FB + FROWS * s + j
                fb[:, s * DFF + j * D:s * DFF + j * D + w] = sum_ref[row:row + 1, 0:w]
        for j in range(NDEV):
            bada[:, j * NSH:(j + 1) * NSH] = (sum_ref[P_DML + j:P_DML + j + 1, 0:NSH]
                                              + sum_ref[P_DMC + j:P_DMC + j + 1, 0:NSH])

    vm = pl.BlockSpec(memory_space=pltpu.VMEM)
    widths = [n for _, n in rows.values()] + [2 * DFF, 6 * D]
    outs = pl.pallas_call(
        body, name=name, in_specs=[vm], out_specs=[vm] * (1 + len(widths)),
        out_shape=[jax.ShapeDtypeStruct(a.shape[1:], F32)] + [jax.ShapeDtypeStruct((1, n), F32) for n in widths])(a)
    return outs[0], dict(zip(list(rows) + ["ffn_conv_b", "b_ada"], outs[1:]))


def ada_bwd(s_all, a_all, a_sum, w_shard, c_ctx, *, name):
    nsh = w_shard.shape[1]
    assert nsh == NSH

    def body(s_ref, a_ref, sum_ref, w_ref, c_ref, dw_hbm, gc_ref, s16, dm16, part, buf, dw_ref, sems):
        x, y, c, me = _me()
        s16[...] = jnp.zeros_like(s16)
        dm16[...] = jnp.zeros_like(dm16)
        for k in range(NDEV):
            s16[k:k + 1, :] = s_ref[k, 0:1, :]
            dm16[k:k + 1, :] = a_ref[k, pl.ds(P_DML + me, 1), 0:nsh]
        s16[8:9, :] = s_ref[0, 1:2, :]
        dm16[8:9, :] = sum_ref[pl.ds(P_DMC + me, 1), 0:nsh]
        part[...] = lax.dot_general(dm16[8:16, :].astype(BF), w_ref[...].astype(BF), (((1,), (1,)), ((), ())),
                                    preferred_element_type=F32)
        buf[me] = part[...]
        _exchange_tiles_start(lambda lin: part, buf, sems.at[0], sems.at[1])
        dw_ref[...] = lax.dot_general(s16[...].astype(BF), dm16[...].astype(BF), (((0,), (0,)), ((), ())),
                                      preferred_element_type=F32)
        dw_out = pltpu.make_async_copy(dw_ref, dw_hbm, sems.at[2])
        dw_out.start()
        _exchange_tiles_wait(buf, sems.at[0], sems.at[1])
        acc = buf[0]
        for k in range(1, NDEV):
            acc = acc + buf[k]
        z = c_ref[...]
        sg = jax.nn.sigmoid(z)
        gc_ref[...] = acc * (sg * (1.0 + z * (1.0 - sg)))
        dw_out.wait()

    vm = pl.BlockSpec(memory_space=pltpu.VMEM)
    return pl.pallas_call(
        body, name=name, in_specs=[vm] * 5, out_specs=[HBM_SPEC, vm],
        out_shape=[pltpu.HBM((D, nsh), F32), jax.ShapeDtypeStruct((8, D), F32)],
        scratch_shapes=[pltpu.VMEM((16, D), F32), pltpu.VMEM((16, nsh), F32), pltpu.VMEM((8, D), F32),
                        pltpu.VMEM((NDEV, 8, D), F32), pltpu.VMEM((D, nsh), F32), pltpu.SemaphoreType.DMA((3,))],
    )(s_all, a_all, a_sum, w_shard, c_ctx)


HBM_SPEC = pl.BlockSpec(memory_space=pltpu.HBM)
SEM_SPEC = pl.BlockSpec(memory_space=pltpu.SEMAPHORE)
EFFECT = pltpu.SideEffectType.DATAFLOW_SIDE_EFFECTING


ALL_PEERS = tuple(range(1, NDEV))
FIRST_HOP = (1, 2, 4, 6)
RELAY = (2, 4, 6)


def _exchange_copies(srcs, lands, send, recv, per_peer, peers):
    x, y, c, me = _me()
    n = len(peers)
    cps = []
    for t in range(len(srcs)):
        for j, k in enumerate(peers):
            dev, lin = _peer(x, y, c, k)
            cps.append(pltpu.make_async_remote_copy(
                src_ref=srcs[t].at[lin] if per_peer else srcs[t], dst_ref=lands[t].at[me],
                send_sem=send.at[n * t + j], recv_sem=recv.at[n * t + j], device_id=dev, device_id_type=MESH))
    return cps


def _relay_copies(lands, send, recv):
    x, y, c, me = _me()
    n = len(RELAY)
    cps = []
    for t in range(len(lands)):
        for j, k in enumerate(RELAY):
            slot = lands[t].at[_peer(x, y, c, k)[1]]
            cps.append(pltpu.make_async_remote_copy(
                src_ref=slot, dst_ref=slot, send_sem=send.at[n * t + j], recv_sem=recv.at[n * t + j],
                device_id=(x, y, 1 - c), device_id_type=MESH))
    return cps


def _own_copies(srcs, lands, own, per_peer):
    me = _me()[3]
    return [pltpu.make_async_copy(srcs[t].at[me] if per_peer else srcs[t], lands[t].at[me], own.at[t])
            for t in range(len(srcs))]


def exchange_start(srcs, *, per_peer, name, dep=None, peers=ALL_PEERS):
    nt = len(srcs)
    ns = len(peers) * nt
    land_shapes = [(a.shape if per_peer else (NDEV,) + a.shape) for a in srcs]
    deps = [] if dep is None else [dep]

    def body(*refs):
        src, land = refs[:nt], refs[nt:2 * nt]
        send, recv, own = refs[2 * nt + len(deps):2 * nt + len(deps) + 3]
        for cp in _exchange_copies(src, land, send, recv, per_peer, peers) + _own_copies(src, land, own, per_peer):
            cp.start()
        refs[-1][...] = jnp.zeros_like(refs[-1])

    hb = lambda a: pltpu.with_memory_space_constraint(a, pltpu.HBM)
    outs = pl.pallas_call(
        body, name=name,
        out_shape=(pltpu.SemaphoreType.DMA((ns,)), pltpu.SemaphoreType.DMA((ns,)), pltpu.SemaphoreType.DMA((nt,)),
                   *[pltpu.HBM(a.shape, a.dtype) for a in srcs], *[pltpu.HBM(s, a.dtype) for s, a in zip(land_shapes, srcs)],
                   jax.ShapeDtypeStruct((8, 128), F32)),
        in_specs=[HBM_SPEC] * (2 * nt) + [pl.BlockSpec(memory_space=pl.ANY)] * len(deps),
        out_specs=(SEM_SPEC, SEM_SPEC, SEM_SPEC, *([HBM_SPEC] * (2 * nt)), pl.BlockSpec(memory_space=pltpu.VMEM)),
        input_output_aliases={i: 3 + i for i in range(2 * nt)},
        compiler_params=pltpu.CompilerParams(has_side_effects=EFFECT),
    )(*[hb(a) for a in srcs], *[hb(lax.empty(s, a.dtype)) for s, a in zip(land_shapes, srcs)], *deps)
    return dict(send=outs[0], recv=outs[1], own=outs[2], src=list(outs[3:3 + nt]), land=list(outs[3 + nt:3 + 2 * nt]),
                token=outs[-1], per_peer=per_peer, peers=peers)


def exchange_wait(h, after, *, name):
    nt = len(h["src"])
    per_peer, peers = h["per_peer"], h["peers"]
    after = list(after) if isinstance(after, (list, tuple)) else [after]

    def body(*refs):
        src, land, send, recv, own = refs[:nt], refs[nt:2 * nt], refs[2 * nt], refs[2 * nt + 1], refs[2 * nt + 2]
        for cp in _exchange_copies(src, land, send, recv, per_peer, peers):
            cp.wait_send()
            cp.wait_recv()
        for cp in _own_copies(src, land, own, per_peer):
            cp.wait()

    outs = pl.pallas_call(
        body, name=name,
        out_shape=(*[pltpu.HBM(a.shape, a.dtype) for a in h["src"]], *[pltpu.HBM(a.shape, a.dtype) for a in h["land"]]),
        in_specs=[HBM_SPEC] * (2 * nt) + [SEM_SPEC, SEM_SPEC, SEM_SPEC] + [pl.BlockSpec(memory_space=pl.ANY)] * len(after),
        out_specs=tuple([HBM_SPEC] * (2 * nt)),
        input_output_aliases={i: i for i in range(2 * nt)},
        compiler_params=pltpu.CompilerParams(has_side_effects=EFFECT),
    )(*h["src"], *h["land"], h["send"], h["recv"], h["own"], *after)
    return list(outs[nt:])


def relay_start(lands, *, name):
    nt = len(lands)
    ns = len(RELAY) * nt

    def body(*refs):
        for cp in _relay_copies(refs[:nt], refs[nt], refs[nt + 1]):
            cp.start()

    outs = pl.pallas_call(
        body, name=name,
        out_shape=(pltpu.SemaphoreType.DMA((ns,)), pltpu.SemaphoreType.DMA((ns,)),
                   *[pltpu.HBM(a.shape, a.dtype) for a in lands]),
        in_specs=[HBM_SPEC] * nt, out_specs=(SEM_SPEC, SEM_SPEC, *([HBM_SPEC] * nt)),
        input_output_aliases={i: 2 + i for i in range(nt)},
        compiler_params=pltpu.CompilerParams(has_side_effects=EFFECT),
    )(*lands)
    return dict(send=outs[0], recv=outs[1], land=list(outs[2:]))


def relay_wait(h, *, name):
    nt = len(h["land"])

    def body(*refs):
        for cp in _relay_copies(refs[:nt], refs[nt], refs[nt + 1]):
            cp.wait_send()
            cp.wait_recv()

    outs = pl.pallas_call(
        body, name=name, out_shape=tuple(pltpu.HBM(a.shape, a.dtype) for a in h["land"]),
        in_specs=[HBM_SPEC] * nt + [SEM_SPEC, SEM_SPEC], out_specs=tuple([HBM_SPEC] * nt),
        input_output_aliases={i: i for i in range(nt)},
        compiler_params=pltpu.CompilerParams(has_side_effects=EFFECT),
    )(*h["land"], h["send"], h["recv"])
    return list(outs)


def _adamw_math(w, g, m, v):
    nm = B1 * m + (1.0 - B1) * g
    nv = B2 * v + (1.0 - B2) * (g * g)
    m_hat = nm / (1.0 - B1 ** STEP)
    v_hat = nv / (1.0 - B2 ** STEP)
    return -LR * (m_hat / (jnp.sqrt(v_hat) + AEPS) + WD * w), nm, nv


def adamw_many(ws, gs, ms, vs, *, name):
    n = len(ws)

    def body(*refs):
        for k in range(n):
            d, nm, nv = _adamw_math(refs[k][...], refs[n + k][...], refs[2 * n + k][...], refs[3 * n + k][...])
            refs[4 * n + k][...] = d
            refs[5 * n + k][...] = nm
            refs[6 * n + k][...] = nv

    vm = pl.BlockSpec(memory_space=pltpu.VMEM)
    sh = [jax.ShapeDtypeStruct(w.shape, F32) for w in ws]
    outs = pl.pallas_call(body, name=name, in_specs=[vm] * (4 * n), out_specs=[vm] * (3 * n), out_shape=sh * 3,
                          )(*ws, *gs, *ms, *vs)
    return outs[:n], outs[n:2 * n], outs[2 * n:]


def adamw(w, g, m, v, *, name, tr=256):
    R, C = w.shape
    tr = _pick(R, tr, 8)

    def body(w_ref, g_ref, m_ref, v_ref, d_ref, nm_ref, nv_ref):
        d_ref[...], nm_ref[...], nv_ref[...] = _adamw_math(w_ref[...], g_ref[...], m_ref[...], v_ref[...])

    blk = pl.BlockSpec((tr, C), lambda i: (i, 0))
    sh = jax.ShapeDtypeStruct((R, C), F32)
    return pl.pallas_call(
        body, name=name, grid=(R // tr,), in_specs=[blk, blk, blk, blk], out_specs=[blk, blk, blk],
        out_shape=[sh, sh, sh], compiler_params=pltpu.CompilerParams(dimension_semantics=("parallel",)),
    )(w, g, m, v)


def adamw_slots(w, slots, m, v, *, name, tr=256):
    unit = w.ndim == 3
    R, C = w.shape[0], w.shape[-1]
    parts = list(slots) if isinstance(slots, (list, tuple)) else [slots]
    n = len(parts)
    assert sum(s.shape[-1] for s in parts) == C
    if R % 16 == 0:
        tr = _pick(R, tr, 16)
    else:
        tr = 144

    def body(w_ref, *refs):
        s_refs, (m_ref, v_ref, g_ref, d_ref, nm_ref, nv_ref) = refs[:n], refs[n:]
        gs = []
        for s_ref in s_refs:
            g = s_ref[0].astype(F32)
            for k in range(1, NDEV):
                g = g + s_ref[k].astype(F32)
            gs.append(g)
        g = gs[0] if n == 1 else jnp.concatenate(gs, axis=-1)
        g_ref[...] = g
        d_ref[...], nm_ref[...], nv_ref[...] = _adamw_math(w_ref[...], g, m_ref[...], v_ref[...])

    blk = pl.BlockSpec((tr, None, C), lambda i: (i, 0, 0)) if unit else pl.BlockSpec((tr, C), lambda i: (i, 0))
    sh = jax.ShapeDtypeStruct(w.shape, F32)
    return pl.pallas_call(
        body, name=name, grid=(pl.cdiv(R, tr),),
        in_specs=[blk] + [pl.BlockSpec((NDEV, tr, s.shape[-1]), lambda i: (0, i, 0)) for s in parts] + [blk, blk],
        out_specs=[blk, blk, blk, blk], out_shape=[sh, sh, sh, sh],
        compiler_params=pltpu.CompilerParams(dimension_semantics=("parallel",)),
    )(w, *parts, m, v)


def kernel(x, c, ctx, c_ctx, w_ada, b_ada, norm1_g, w_in, q_norm_g, kv_norm_g, w_uq, w_ukv, conv_w, conv_b, w_attn_out, w_conv_out, w_o, norm2_g, w_up, ffn_conv_w, ffn_conv_b, w_down, final_g, loss_target, m_c_ctx, m_w_ada, m_b_ada, m_norm1_g, m_w_in, m_q_norm_g, m_kv_norm_g, m_w_uq, m_w_ukv, m_conv_w, m_conv_b, m_w_attn_out, m_w_conv_out, m_w_o, m_norm2_g, m_w_up, m_ffn_conv_w, m_ffn_conv_b, m_w_down, m_final_g, v_c_ctx, v_w_ada, v_b_ada, v_norm1_g, v_w_in, v_q_norm_g, v_kv_norm_g, v_w_uq, v_w_ukv, v_conv_w, v_conv_b, v_w_attn_out, v_w_conv_out, v_w_o, v_norm2_g, v_w_up, v_ffn_conv_w, v_ffn_conv_b, v_w_down, v_final_g):
    me = 4 * lax.axis_index("x") + 2 * lax.axis_index("y") + lax.axis_index("c")
    W = dict(c_ctx=c_ctx, w_ada=w_ada, b_ada=b_ada, norm1_g=norm1_g, w_in=w_in, q_norm_g=q_norm_g, kv_norm_g=kv_norm_g,
             w_uq=w_uq, w_ukv=w_ukv, conv_w=conv_w, conv_b=conv_b, w_attn_out=w_attn_out, w_conv_out=w_conv_out, w_o=w_o,
             norm2_g=norm2_g, w_up=w_up, ffn_conv_w=ffn_conv_w, ffn_conv_b=ffn_conv_b, w_down=w_down, final_g=final_g)
    M = dict(c_ctx=m_c_ctx, w_ada=m_w_ada, b_ada=m_b_ada, norm1_g=m_norm1_g, w_in=m_w_in, q_norm_g=m_q_norm_g,
             kv_norm_g=m_kv_norm_g, w_uq=m_w_uq, w_ukv=m_w_ukv, conv_w=m_conv_w, conv_b=m_conv_b, w_attn_out=m_w_attn_out,
             w_conv_out=m_w_conv_out, w_o=m_w_o, norm2_g=m_norm2_g, w_up=m_w_up, ffn_conv_w=m_ffn_conv_w,
             ffn_conv_b=m_ffn_conv_b, w_down=m_w_down, final_g=m_final_g)
    V = dict(c_ctx=v_c_ctx, w_ada=v_w_ada, b_ada=v_b_ada, norm1_g=v_norm1_g, w_in=v_w_in, q_norm_g=v_q_norm_g,
             kv_norm_g=v_kv_norm_g, w_uq=v_w_uq, w_ukv=v_w_ukv, conv_w=v_conv_w, conv_b=v_conv_b, w_attn_out=v_w_attn_out,
             w_conv_out=v_w_conv_out, w_o=v_w_o, norm2_g=v_norm2_g, w_up=v_w_up, ffn_conv_w=v_ffn_conv_w,
             ffn_conv_b=v_ffn_conv_b, w_down=v_w_down, final_g=v_final_g)
    names = list(W)
    transposed = ("w_up", "w_uq")
    as2d = lambda k, a: (a.reshape(1, -1) if a.ndim == 1 else
                         a[0].T if k in transposed else a.reshape(a.shape[-2], a.shape[-1]))
    W2 = {k: as2d(k, a) for k, a in W.items()}
    M2 = {k: as2d(k, a) for k, a in M.items()}
    V2 = {k: as2d(k, a) for k, a in V.items()}
    unit3 = lambda a: jnp.transpose(a, (2, 0, 1))
    W3, M3, V3 = unit3(W["w_in"]), unit3(M["w_in"]), unit3(V["w_in"])
    nsh = W2["w_ada"].shape[1]

    unit_mid = ("conv_w", "ffn_conv_w")
    mid3 = lambda a: jnp.transpose(a, (1, 0, 2))
    s_all, mod_lat, mod_ctx, ffn_w_full, conv_w_full = ada_fwd(
        c, W2["c_ctx"], mid3(W["ffn_conv_w"]), mid3(W["conv_w"]), W2["w_ada"], W["b_ada"].reshape(NDEV, 1, nsh), [],
        name="ada_fwd")

    stage_w = {"in": ["w_in"], "mid": ["w_uq", "w_ukv", "w_attn_out", "w_conv_out", "w_o"], "up": ["w_up"],
               "down": ["w_down"]}
    two_level = ("in", "mid")
    ag, tok = {}, mod_lat
    for st, nms in stage_w.items():
        ag[st] = exchange_start([W2[nm].astype(BF) for nm in nms], per_peer=False, dep=tok, name="ag_start_" + st,
                                peers=FIRST_HOP if st in two_level else ALL_PEERS)
        tok = ag[st]["token"]

    def get_w(stage, after):
        lands = exchange_wait(ag[stage], after, name="ag_wait_" + stage)
        if stage in two_level:
            lands = relay_wait(relay_start(lands, name="ag_relay_" + stage), name="ag_relay_wait_" + stage)
        g = dict(zip(stage_w[stage], lands))
        if stage == "in":
            return build_win(g["w_in"], name="build_win")
        if stage == "mid":
            wq2, wkv2 = build_wq_wkv(g["w_uq"], g["w_ukv"], name="build_wq_wkv")
            return (wq2, wkv2, unshard_cols(g["w_attn_out"], name="unshard_w_attn_out"),
                    unshard_cols(g["w_conv_out"], name="unshard_w_conv_out"), g["w_o"].reshape(D, D))
        if stage == "up":
            return g["w_up"].reshape(2 * DFF, D)
        return g["w_down"].reshape(DFF, D)

    stage_g = {"ffn": ["w_up", "w_down"], "mid": ["w_attn_out", "w_conv_out", "w_o"], "qkv": ["w_uq", "w_ukv"],
               "in": ["w_in"]}
    rs = {}

    def put_g(stage, g):
        if stage in ("in0", "in1"):
            parts = [shard_win_grad(g["dwin"], g["dwin_c"], col0=g["col0"], name="shard_win_grad_" + stage[-1])]
        elif stage == "mid":
            parts = [shard_cols(g["dwao"], name="shard_w_attn_out"), shard_cols(g["dwco"], name="shard_w_conv_out"),
                     g["dwo"].reshape(NDEV, D // NDEV, D)]
        elif stage == "qkv":
            parts = list(shard_wq_wkv_grad(g["dwq2"], g["dwkv2"], name="shard_wq_wkv_grad"))
        else:
            parts = [g["dwup"].reshape(NDEV, 2 * DFF // NDEV, D), g["dwdn"].reshape(NDEV, DFF // NDEV, D)]
        rs[stage] = exchange_start(parts, per_peer=True, name="rs_start_" + stage)
        return rs[stage]["token"]

    r = _local_step(x[0], ctx[0], loss_target[0], mod_lat, mod_ctx, W2["norm1_g"], W2["q_norm_g"], W2["kv_norm_g"],
                    W2["norm2_g"], W2["final_g"], conv_w_full, W2["conv_b"], ffn_w_full, W2["ffn_conv_b"], get_w, put_g,
                    ag["down"]["token"])

    G, DL, NM, NV = {}, {}, {}, {}

    def finish(stage, after):
        if stage == "in":
            halves = []
            for h in ("in0", "in1"):
                halves += exchange_wait(rs[h], after, name="rs_wait_" + h)
                after = halves[-1]
            G["w_in"], DL["w_in"], NM["w_in"], NV["w_in"] = adamw_slots(W3, halves, M3, V3, name="adamw_w_in")
            return DL["w_in"]
        for nm, sl in zip(stage_g[stage], exchange_wait(rs[stage], after, name="rs_wait_" + stage)):
            G[nm], DL[nm], NM[nm], NV[nm] = adamw_slots(W2[nm], sl, M2[nm], V2[nm], name="adamw_" + nm)
            after = DL[nm]
        return after

    sync = exchange_start([pack_small(r, name="pack_small")], per_peer=False, name="sync_start")
    after = sync["token"]
    for st in ("ffn", "mid", "in", "qkv"):
        after = finish(st, after)
    a_buf, = exchange_wait(sync, [DL[nm] for nms in stage_g.values() for nm in nms], name="sync_wait")
    ssum, g_vec = sum_slots(a_buf, name="sum_small")
    G.update(g_vec)
    loss = ssum[P_LOSS, 0]
    G["conv_w"] = lax.dynamic_slice(ssum[P_CW:P_CW + 3, :CONV], (0, me * (CONV // NDEV)), (3, CONV // NDEV))
    fw_full = ssum[P_FW:P_FW + 6 * FROWS].reshape(3, 2, FROWS * D)[:, :, :DFF].reshape(3, 2 * DFF)
    G["ffn_conv_w"] = lax.dynamic_slice(fw_full, (0, me * (2 * DFF // NDEV)), (3, 2 * DFF // NDEV))

    G["w_ada"], gcc = ada_bwd(s_all, a_buf, ssum, W2["w_ada"], W2["c_ctx"], name="ada_bwd")
    G["c_ctx"] = gcc[0:1]

    DL["w_ada"], NM["w_ada"], NV["w_ada"] = adamw(W2["w_ada"], G["w_ada"], M2["w_ada"], V2["w_ada"], name="adamw_w_ada")
    small = ["c_ctx", "b_ada", "norm1_g", "q_norm_g", "kv_norm_g", "conv_b", "norm2_g", "ffn_conv_b", "final_g", "conv_w",
             "ffn_conv_w"]
    view = lambda k, a3, a2: mid3(a3[k]) if k in unit_mid else a2[k]
    for k in unit_mid:
        G[k] = G[k].reshape(3, 1, -1)
    ds, nms, nvs = adamw_many([view(k, W, W2) for k in small], [G[k] for k in small],
                              [view(k, M, M2) for k in small], [view(k, V, V2) for k in small], name="adamw_small")
    for k, nm in enumerate(small):
        DL[nm], NM[nm], NV[nm] = ds[k], nms[k], nvs[k]

    def as_output(nm, a):
        if nm in transposed:
            return a.T[None]
        if nm == "w_in":
            return jnp.transpose(a, (1, 2, 0))
        if nm in unit_mid and a.ndim == 3:
            return jnp.transpose(a, (1, 0, 2))
        return a.reshape(W[nm].shape)

    outs = [loss, r["dx"][None]]
    for grp in (G, DL, NM, NV):
        outs += [as_output(nm, grp[nm]) for nm in names]
    return tuple(outs)
```

```python
import functools
import numpy as np
import jax
import jax.numpy as jnp
from jax import lax
from jax.experimental import pallas as pl
from jax.experimental.pallas import tpu as pltpu

F32 = jnp.float32
BF = jnp.bfloat16
MESH = pl.DeviceIdType.MESH

D = 1024
T = 2048
TC = 256
TKV = T + TC
GRID_W = 64
NH = 8
DN = 64
DR = 32
DV = 64
QL = 384
KVL = 256
CONV = 512
DFF = 2816
EPS = 1e-6
ROPE_THETA = 10000.0
SCALE = (DN + DR) ** -0.5
NDEV = 8
HP = 128

O_GA, O_GC, O_KV, O_Q, O_CV = 0, 1024, 2048, 2560, 3072
NIN = 4608
CVB = 256
N_IN = 4256
SH_IN = N_IN // NDEV

LR, B1, B2, AEPS, WD, STEP = 0.001, 0.9, 0.999, 1e-08, 0.01, 10


def _pick(n, target, mult=128):
    best = None
    for d in range(mult, min(n, target) + 1, mult):
        if n % d == 0:
            best = d
    return best if best is not None else n


def _swap_start(g):
    return 8 * (g ^ 1)


def mm(a, b, *, ta=False, tb=False, out_dtype=F32, name, tm=1024, tn=1024, tk=2048, M=None, N=None, K=None,
       a_off=(0, 0), b_off=(0, 0), a_stack=False, b_stack=False, o_stack=False, dep=None):
    def dims(arr, stack):
        return (arr.shape[1], 2 * arr.shape[2]) if stack else arr.shape

    ar, ac = dims(a, a_stack)
    br, bc = dims(b, b_stack)
    M = M or ((ac if ta else ar) - a_off[1 if ta else 0])
    K = K or ((ar if ta else ac) - a_off[0 if ta else 1])
    N = N or ((br if tb else bc) - b_off[0 if tb else 1])
    tm = _pick(M, tm, 128 if ta else 16)
    tn = _pick(N // 2 if (o_stack or (b_stack and not tb)) else N, tn, 128)
    tk = _pick(K // 2 if ((a_stack and not ta) or (b_stack and tb)) else K, tk, 128)
    nk = K // tk
    ca = 0 if ta else 1
    cb = 1 if tb else 0

    def body(a_ref, b_ref, *rest):
        o_ref, acc = rest[-2:]
        k = pl.program_id(2)
        part = lax.dot_general(a_ref[...].astype(BF), b_ref[...].astype(BF),
                               (((ca,), (cb,)), ((), ())), preferred_element_type=F32)
        if nk == 1:
            o_ref[...] = part.astype(o_ref.dtype)
        else:
            @pl.when(k == 0)
            def _():
                acc[...] = part

            @pl.when(k > 0)
            def _():
                acc[...] += part

            @pl.when(k == nk - 1)
            def _():
                o_ref[...] = acc[...].astype(o_ref.dtype)

    def spec(blk, rc, off, stack, ncols):
        assert off[0] % blk[0] == 0 and off[1] % blk[1] == 0, (name, blk, off)
        ro, co = off[0] // blk[0], off[1] // blk[1]
        if not stack:
            return pl.BlockSpec(blk, lambda i, j, k: (rc(i, j, k)[0] + ro, rc(i, j, k)[1] + co))
        nhb = ncols // 2 // blk[1]
        return pl.BlockSpec((None,) + blk,
                            lambda i, j, k: ((rc(i, j, k)[1] + co) // nhb, rc(i, j, k)[0] + ro, (rc(i, j, k)[1] + co) % nhb))

    a_spec = spec((tk, tm), lambda i, j, k: (k, i), a_off, a_stack, ac) if ta else \
        spec((tm, tk), lambda i, j, k: (i, k), a_off, a_stack, ac)
    b_spec = spec((tn, tk), lambda i, j, k: (j, k), b_off, b_stack, bc) if tb else \
        spec((tk, tn), lambda i, j, k: (k, j), b_off, b_stack, bc)
    o_spec = spec((tm, tn), lambda i, j, k: (i, j), (0, 0), o_stack, N)
    o_shape = (2, M, N // 2) if o_stack else (M, N)
    deps = [] if dep is None else [dep]
    return pl.pallas_call(
        body, name=name, grid=(M // tm, N // tn, nk),
        in_specs=[a_spec, b_spec] + [pl.BlockSpec(memory_space=pl.ANY)] * len(deps),
        out_specs=o_spec, out_shape=jax.ShapeDtypeStruct(o_shape, out_dtype),
        scratch_shapes=[pltpu.VMEM((tm, tn) if nk > 1 else (8, 128), F32)],
        compiler_params=pltpu.CompilerParams(dimension_semantics=("parallel", "parallel", "arbitrary")),
    )(a, b, *deps)


def _row(width):
    return pl.BlockSpec((1, width), lambda *_: (0, 0))


NLAT = T // TC


def normmod_cat(ctx, x, g, csc, csh, sc, sh, dep, *, name, tm=256):
    assert tm == TC

    def body(c_ref, x_ref, g_ref, csc_ref, csh_ref, sc_ref, sh_ref, dep_ref, h_ref):
        last = pl.program_id(0) == NLAT
        xv = jnp.where(last, c_ref[...], x_ref[...])
        scv = jnp.where(last, csc_ref[...], sc_ref[...])
        shv = jnp.where(last, csh_ref[...], sh_ref[...])
        r = lax.rsqrt(jnp.mean(xv * xv, axis=-1, keepdims=True) + EPS)
        h_ref[...] = ((xv * r * g_ref[...]) * (1.0 + scv) + shv).astype(BF)

    return pl.pallas_call(
        body, name=name, grid=(TKV // tm,),
        in_specs=[pl.BlockSpec((tm, D), lambda i: (0, 0)), pl.BlockSpec((tm, D), lambda i: (jnp.minimum(i, NLAT - 1), 0)),
                  _row(D), _row(D), _row(D), _row(D), _row(D), pl.BlockSpec(memory_space=pl.ANY)],
        out_specs=pl.BlockSpec((tm, D), lambda i: (i, 0)), out_shape=jax.ShapeDtypeStruct((TKV, D), BF),
        compiler_params=pltpu.CompilerParams(dimension_semantics=("parallel",)),
    )(ctx, x, g, csc, csh, sc, sh, dep)


def kvprep(pc, p, kvg, wkv2, ck, sk, *, name, tm=256):
    assert tm == TC
    nb = TKV // tm
    kvcol = O_KV // 512

    def body(pc_ref, p_ref, g_ref, w_ref, ck_ref, sk_ref, k_ref, v_ref, ckv_ref):
        i = pl.program_id(0)
        t = jnp.where(i == NLAT, pc_ref[...], p_ref[...])
        pk = t[:, :KVL]
        r = lax.rsqrt(jnp.mean(pk * pk, axis=-1, keepdims=True) + EPS)
        ckv = (pk * r * g_ref[...]).astype(BF)
        ckv_ref[...] = ckv
        kv2 = jnp.dot(ckv, w_ref[...], preferred_element_type=F32)
        krr = t[:, KVL:KVL + HP] * ck_ref[...] + t[:, KVL + HP:KVL + 2 * HP] * sk_ref[...]
        k_ref[...] = (kv2[:, :NH * HP] + jnp.concatenate([krr] * NH, axis=1)).astype(BF)
        v_ref[...] = kv2[:, NH * HP:].astype(BF)

    return pl.pallas_call(
        body, name=name, grid=(nb,),
        in_specs=[pl.BlockSpec((tm, 512), lambda i: (0, 0)),
                  pl.BlockSpec((tm, 512), lambda i: (jnp.minimum(i, NLAT - 1), kvcol)),
                  _row(KVL), pl.BlockSpec((KVL, NH * HP + NH * DV), lambda i: (0, 0)),
                  pl.BlockSpec((tm, HP), lambda i: (i, 0)), pl.BlockSpec((tm, HP), lambda i: (i, 0))],
        out_specs=[pl.BlockSpec((tm, NH * HP), lambda i: (i, 0)), pl.BlockSpec((tm, NH * DV), lambda i: (i, 0)),
                   pl.BlockSpec((tm, KVL), lambda i: (i, 0))],
        out_shape=[jax.ShapeDtypeStruct((TKV, NH * HP), BF), jax.ShapeDtypeStruct((TKV, NH * DV), BF),
                   jax.ShapeDtypeStruct((TKV, KVL), BF)],
        compiler_params=pltpu.CompilerParams(dimension_semantics=("parallel",)),
    )(pc, p, kvg, wkv2, ck, sk)


def qprep(p, qg, wq2, cq_t, sq_t, *, name, tm=256):
    qcol = O_Q // 512

    def body(p_ref, g_ref, w_ref, c_ref, s_ref, q_ref, cq_ref):
        pq = p_ref[...]
        r = lax.rsqrt(jnp.sum(pq * pq, axis=-1, keepdims=True) * (1.0 / QL) + EPS)
        cq = (pq * r * g_ref[...]).astype(BF)
        cq_ref[...] = cq
        q2 = jnp.dot(cq, w_ref[...], preferred_element_type=F32)
        cc = jnp.concatenate([c_ref[...]] * NH, axis=1)
        ss = jnp.concatenate([s_ref[...]] * NH, axis=1)
        q_ref[...] = (q2[:, :NH * HP] * cc + q2[:, NH * HP:] * ss).astype(BF)

    return pl.pallas_call(
        body, name=name, grid=(T // tm,),
        in_specs=[pl.BlockSpec((tm, 512), lambda i: (i, qcol)), _row(512),
                  pl.BlockSpec((512, 2 * NH * HP), lambda i: (0, 0)),
                  pl.BlockSpec((tm, HP), lambda i: (i, 0)), pl.BlockSpec((tm, HP), lambda i: (i, 0))],
        out_specs=[pl.BlockSpec((tm, NH * HP), lambda i: (i, 0)), pl.BlockSpec((tm, 512), lambda i: (i, 0))],
        out_shape=[jax.ShapeDtypeStruct((T, NH * HP), BF), jax.ShapeDtypeStruct((T, 512), BF)],
        compiler_params=pltpu.CompilerParams(dimension_semantics=("parallel",)),
    )(p, qg, wq2, cq_t, sq_t)


def _head_mask(h):
    lanes = lax.broadcasted_iota(jnp.int32, (1, 2 * DV), 1)
    return (lanes // DV) == (h % 2)


LOG2E = 1.4426950408889634


def attn_fwd(q, k, v, *, name, tq=1024, kc=768):
    def body(q_ref, k_ref, v_ref, o_ref, lse_ref):
        h = pl.program_id(1)
        qv = q_ref[...]
        m = l = acc = None
        for c in range(TKV // kc):
            s = lax.dot_general(qv, k_ref[c * kc:(c + 1) * kc, :], (((1,), (1,)), ((), ())),
                                preferred_element_type=F32) * (SCALE * LOG2E)
            mc = jnp.max(s, axis=-1, keepdims=True)
            if c == 0:
                m = mc
                e = jnp.exp2(s - m)
                l = jnp.sum(e, axis=-1, keepdims=True)
                acc = jnp.dot(e.astype(BF), v_ref[c * kc:(c + 1) * kc, :], preferred_element_type=F32)
            else:
                mn = jnp.maximum(m, mc)
                a = jnp.exp2(m - mn)
                e = jnp.exp2(s - mn)
                l = l * a + jnp.sum(e, axis=-1, keepdims=True)
                acc = acc * a + jnp.dot(e.astype(BF), v_ref[c * kc:(c + 1) * kc, :], preferred_element_type=F32)
                m = mn
        o2 = jnp.where(_head_mask(h), acc * (1.0 / l), 0.0).astype(BF)
        lse_ref[...] = jnp.broadcast_to(m + jnp.log(l) * LOG2E, (tq, HP))

        @pl.when(h % 2 == 0)
        def _():
            o_ref[...] = o2

        @pl.when(h % 2 == 1)
        def _():
            o_ref[...] = o_ref[...] + o2

    return pl.pallas_call(
        body, name=name, grid=(T // tq, NH),
        in_specs=[pl.BlockSpec((tq, HP), lambda i, h: (i, h)), pl.BlockSpec((TKV, HP), lambda i, h: (0, h)),
                  pl.BlockSpec((TKV, 2 * DV), lambda i, h: (0, h // 2))],
        out_specs=[pl.BlockSpec((tq, 2 * DV), lambda i, h: (i, h // 2)), pl.BlockSpec((tq, HP), lambda i, h: (i, h))],
        out_shape=[jax.ShapeDtypeStruct((T, NH * DV), BF), jax.ShapeDtypeStruct((T, NH * HP), F32)],
        compiler_params=pltpu.CompilerParams(dimension_semantics=("parallel", "arbitrary")),
    )(q, k, v)


def _shift_dn(x):
    n = x.shape[0]
    rows = lax.broadcasted_iota(jnp.int32, (n, 1), 0)
    return jnp.where(rows == 0, 0.0, pltpu.roll(x, 1, axis=0))


def _shift_up(x):
    n = x.shape[0]
    rows = lax.broadcasted_iota(jnp.int32, (n, 1), 0)
    return jnp.where(rows == n - 1, 0.0, pltpu.roll(x, n - 1, axis=0))


def _conv(x, w_ref, b_ref):
    return b_ref[...] + _shift_dn(x) * w_ref[0:1, :] + x * w_ref[1:2, :] + _shift_up(x) * w_ref[2:3, :]


def _conv_t(dy, w_ref):
    return _shift_up(dy) * w_ref[0:1, :] + dy * w_ref[1:2, :] + _shift_dn(dy) * w_ref[2:3, :]


def _conv_wgrad(dw_ref, dy, x):
    dw_ref[0:1, :] = jnp.sum(dy * _shift_dn(x), axis=0, keepdims=True)
    dw_ref[1:2, :] = jnp.sum(dy * x, axis=0, keepdims=True)
    dw_ref[2:3, :] = jnp.sum(dy * _shift_up(x), axis=0, keepdims=True)


def convz(p, cw, cb, *, name):
    o0 = O_CV // (3 * CVB)

    def body(p_ref, w_ref, bias_ref, z_ref):
        xv, bv, cv = p_ref[:, 0:CVB], p_ref[:, CVB:2 * CVB], p_ref[:, 2 * CVB:3 * CVB]
        z_ref[...] = (bv * _conv(cv * xv, w_ref, bias_ref)).astype(BF)

    return pl.pallas_call(
        body, name=name, grid=(CONV // CVB,),
        in_specs=[pl.BlockSpec((T, 3 * CVB), lambda j: (0, o0 + j)), pl.BlockSpec((3, CVB), lambda j: (0, j)),
                  pl.BlockSpec((1, CVB), lambda j: (0, j))],
        out_specs=pl.BlockSpec((T, CVB), lambda j: (0, j)),
        out_shape=jax.ShapeDtypeStruct((T, CONV), BF),
        compiler_params=pltpu.CompilerParams(dimension_semantics=("parallel",)),
    )(p, cw, cb)


def out_proj_merge(o, wao, z, wco, p, *, name, tm=512):
    kin = o.shape[1]

    def body(o_ref, wa_ref, z_ref, wc_ref, ga_ref, gc_ref, ya_ref, yc_ref, m_ref):
        ya = jnp.dot(o_ref[...], wa_ref[...], preferred_element_type=F32)
        yc = jnp.dot(z_ref[...], wc_ref[...], preferred_element_type=F32)
        ya_ref[...] = ya
        yc_ref[...] = yc
        m_ref[...] = (jax.nn.sigmoid(ga_ref[...]) * ya + jax.nn.sigmoid(gc_ref[...]) * yc).astype(BF)

    blk = pl.BlockSpec((tm, D), lambda i: (i, 0))
    act = pl.BlockSpec((tm, kin), lambda i: (i, 0))
    wsp = pl.BlockSpec((kin, D), lambda i: (0, 0))
    sh = jax.ShapeDtypeStruct((T, D), F32)
    return pl.pallas_call(
        body, name=name, grid=(T // tm,),
        in_specs=[act, wsp, act, wsp, pl.BlockSpec((tm, D), lambda i: (i, O_GA // D)),
                  pl.BlockSpec((tm, D), lambda i: (i, O_GC // D))],
        out_specs=[blk, blk, blk], out_shape=[sh, sh, jax.ShapeDtypeStruct((T, D), BF)],
        compiler_params=pltpu.CompilerParams(dimension_semantics=("parallel",)),
    )(o, wao, z, wco, p, p)


CONV_HALO = 8
CONV_ROWS = 256


def _row_chunks(n, chunk, carry):
    carry = chunk(0, True, False, carry)
    carry = lax.fori_loop(1, n // CONV_ROWS - 1, lambda c, a: chunk(c * CONV_ROWS, False, False, a), carry)
    return chunk(n - CONV_ROWS, False, True, carry)


def _ext_rows(ref, r0, first, last):
    n, w = ref.shape
    zero = jnp.zeros((CONV_HALO, w), ref.dtype)
    if first:
        return jnp.concatenate([zero, ref[0:CONV_ROWS + CONV_HALO, :]], axis=0)
    if last:
        return jnp.concatenate([ref[n - CONV_ROWS - CONV_HALO:n, :], zero], axis=0)
    return ref[pl.ds(pl.multiple_of(r0 - CONV_HALO, 8), CONV_ROWS + 2 * CONV_HALO), :]


def _center_rows(r0, first, last):
    return slice(r0, r0 + CONV_ROWS) if (first or last) else pl.ds(pl.multiple_of(r0, 8), CONV_ROWS)


def _roll_dn(x):
    return pltpu.roll(x, 1, axis=0)


def _roll_up(x):
    return pltpu.roll(x, x.shape[0] - 1, axis=0)


_CTR = slice(CONV_HALO, CONV_HALO + CONV_ROWS)


def ffn_act(u0, cw, cb, *, name, tc=256):
    nb = DFF // tc

    def body(ug_ref, uv_ref, wg_ref, wv_ref, bg_ref, bv_ref, f_ref):
        wg = [wg_ref[k:k + 1, :] for k in range(3)]
        wv = [wv_ref[k:k + 1, :] for k in range(3)]
        bg, bv = bg_ref[...], bv_ref[...]

        def chunk(r0, first, last, carry):
            xg, xv = _ext_rows(ug_ref, r0, first, last), _ext_rows(uv_ref, r0, first, last)
            ug = bg + _roll_dn(xg) * wg[0] + xg * wg[1] + _roll_up(xg) * wg[2]
            uv = bv + _roll_dn(xv) * wv[0] + xv * wv[1] + _roll_up(xv) * wv[2]
            f_ref[_center_rows(r0, first, last), :] = (ug * jax.nn.sigmoid(ug) * uv)[_CTR].astype(BF)
            return carry

        _row_chunks(T, chunk, 0)

    return pl.pallas_call(
        body, name=name, grid=(nb,),
        in_specs=[pl.BlockSpec((None, T, tc), lambda j: (0, 0, j)), pl.BlockSpec((None, T, tc), lambda j: (1, 0, j)),
                  pl.BlockSpec((3, tc), lambda j: (0, j)), pl.BlockSpec((3, tc), lambda j: (0, nb + j)),
                  pl.BlockSpec((1, tc), lambda j: (0, j)), pl.BlockSpec((1, tc), lambda j: (0, nb + j))],
        out_specs=pl.BlockSpec((T, tc), lambda j: (0, j)),
        out_shape=jax.ShapeDtypeStruct((T, DFF), BF),
        compiler_params=pltpu.CompilerParams(dimension_semantics=("parallel",)),
    )(u0, u0, cw, cw, cb, cb)


def rows_call(lead, ins, in_specs, out_shape, out_specs, fn, *, name, R, tm):
    tb, a_stack, tk = lead.get("tb", False), lead.get("a_stack", False), lead["tk"]
    K = 2 * lead["a"].shape[2] if a_stack else lead["a"].shape[1]
    nk = K // tk
    deps = [] if lead.get("dep") is None else [lead["dep"]]
    n_in = len(ins)

    def body(a_ref, b_ref, *refs):
        refs = refs[len(deps):]
        in_refs, out_refs, acc = refs[:n_in], refs[n_in:-1], refs[-1]
        i, k = pl.program_id(0), pl.program_id(1)
        part = lax.dot_general(a_ref[...].astype(BF), b_ref[...].astype(BF),
                               (((1,), (1 if tb else 0,)), ((), ())), preferred_element_type=F32)
        if nk == 1:
            fn(i, part, in_refs, out_refs)
            return

        @pl.when(k == 0)
        def _():
            acc[...] = part

        @pl.when(k > 0)
        def _():
            acc[...] += part

        @pl.when(k == nk - 1)
        def _():
            fn(i, acc[...], in_refs, out_refs)

    if a_stack:
        nhb = K // 2 // tk
        a_spec = pl.BlockSpec((None, tm, tk), lambda i, k: (k // nhb, i, k % nhb))
    else:
        a_spec = pl.BlockSpec((tm, tk), lambda i, k: (i, k))
    b_spec = pl.BlockSpec((D, tk), lambda i, k: (0, k)) if tb else pl.BlockSpec((tk, D), lambda i, k: (k, 0))
    return pl.pallas_call(
        body, name=name, grid=(R // tm, nk),
        in_specs=[a_spec, b_spec] + [pl.BlockSpec(memory_space=pl.ANY)] * len(deps) + list(in_specs),
        out_specs=out_specs, out_shape=out_shape,
        scratch_shapes=[pltpu.VMEM((tm, D) if nk > 1 else (8, 128), F32)],
        compiler_params=pltpu.CompilerParams(dimension_semantics=("arbitrary", "arbitrary")),
    )(lead["a"], lead["b"], *deps, *ins)


def _rblk(tm, w=D, col=0):
    return pl.BlockSpec((tm, w), lambda i, k: (i, col))


def _rrow(w=D):
    return pl.BlockSpec((1, w), lambda i, k: (0, 0))


def down_final(f, wdn, x1, g2, fg, tgt, *, name, tm=512):
    def fn(i, d, in_refs, out_refs):
        x1_ref, g2_ref, fg_ref, t_ref = in_refs
        d_ref, dx_ref, dd_ref, dfg_ref, loss_ref = out_refs
        d_ref[...] = d
        xv = x1_ref[...] + g2_ref[...] * d
        r = lax.rsqrt(jnp.mean(xv * xv, axis=-1, keepdims=True) + EPS)
        xh = xv * r
        diff = xh * fg_ref[...] - t_ref[...]
        part = 0.5 * jnp.sum(jnp.mean(diff * diff, axis=-1, keepdims=True), axis=0, keepdims=True)
        dy = diff * (1.0 / D)
        a = dy * fg_ref[...]
        dx = r * (a - xh * jnp.mean(a * xh, axis=-1, keepdims=True))
        dx_ref[...] = dx
        dd_ref[...] = (dx * g2_ref[...]).astype(BF)
        dfg = jnp.sum(dy * xh, axis=0, keepdims=True)

        @pl.when(i == 0)
        def _():
            dfg_ref[...] = dfg
            loss_ref[...] = jnp.broadcast_to(part, (1, 128))

        @pl.when(i > 0)
        def _():
            dfg_ref[...] += dfg
            loss_ref[...] += jnp.broadcast_to(part, (1, 128))

    blk = _rblk(tm)
    return rows_call(
        dict(a=f, b=wdn, tk=DFF), [x1, g2, fg, tgt], [blk, _rrow(), _rrow(), blk],
        [jax.ShapeDtypeStruct((T, D), F32), jax.ShapeDtypeStruct((T, D), F32), jax.ShapeDtypeStruct((T, D), BF),
         jax.ShapeDtypeStruct((1, D), F32), jax.ShapeDtypeStruct((1, 128), F32)],
        [blk, blk, blk, _rrow(), _rrow(128)], fn, name=name, R=T, tm=tm)


def oproj_resid(merged, wo, x, gate, g, sc, sh, *, name, tm=512):
    def fn(i, a, in_refs, out_refs):
        x_ref, gate_ref, g_ref, sc_ref, sh_ref = in_refs
        a_ref, x1_ref, h_ref = out_refs
        a_ref[...] = a
        xv = x_ref[...] + gate_ref[...] * a
        x1_ref[...] = xv
        r = lax.rsqrt(jnp.mean(xv * xv, axis=-1, keepdims=True) + EPS)
        h_ref[...] = ((xv * r * g_ref[...]) * (1.0 + sc_ref[...]) + sh_ref[...]).astype(BF)

    blk = _rblk(tm)
    return rows_call(
        dict(a=merged, b=wo, tk=D), [x, gate, g, sc, sh], [blk, _rrow(), _rrow(), _rrow(), _rrow()],
        [jax.ShapeDtypeStruct((T, D), F32), jax.ShapeDtypeStruct((T, D), F32), jax.ShapeDtypeStruct((T, D), BF)],
        [blk, blk, blk], fn, name=name, R=T, tm=tm)


def oproj_dx_gate_bwd(da, wo, p, ya, yc, wao, wco, *, name, tm=512):
    kin = wao.shape[0]

    def fn(i, dm, in_refs, out_refs):
        ga_ref, gc_ref, ya_ref, yc_ref, wa_ref, wc_ref = in_refs
        dya_ref, dyc_ref, dp_ref, do_ref, dz_ref = out_refs
        sa, sc_ = jax.nn.sigmoid(ga_ref[...]), jax.nn.sigmoid(gc_ref[...])
        dya, dyc = (dm * sa).astype(BF), (dm * sc_).astype(BF)
        dya_ref[...] = dya
        dyc_ref[...] = dyc
        dp_ref[:, 0:D] = (dm * ya_ref[...] * (sa * (1.0 - sa))).astype(BF)
        dp_ref[:, D:2 * D] = (dm * yc_ref[...] * (sc_ * (1.0 - sc_))).astype(BF)
        nt = (((1,), (1,)), ((), ()))
        do_ref[...] = lax.dot_general(dya, wa_ref[...], nt, preferred_element_type=F32).astype(BF)
        dz_ref[...] = lax.dot_general(dyc, wc_ref[...], nt, preferred_element_type=F32)

    blk = _rblk(tm)
    sh = jax.ShapeDtypeStruct((T, D), BF)
    wsp = pl.BlockSpec((kin, D), lambda i, k: (0, 0))
    return rows_call(
        dict(a=da, b=wo, tb=True, tk=D), [p, p, ya, yc, wao, wco],
        [_rblk(tm, D, O_GA // D), _rblk(tm, D, O_GC // D), blk, blk, wsp, wsp],
        [sh, sh, jax.ShapeDtypeStruct((T, NIN), BF), jax.ShapeDtypeStruct((T, kin), BF), jax.ShapeDtypeStruct((T, kin), F32)],
        [blk, blk, _rblk(tm, 2 * D), _rblk(tm, kin), _rblk(tm, kin)], fn, name=name, R=T, tm=tm)


def normmod_bwd(x, dh, g, sc, dres, gsrc, gate, *, name, tm=512):
    R = x.shape[0]
    tm = min(tm, R)
    has_res = dres is not None
    fused = isinstance(dh, dict)
    if fused:
        tb, a_stack, tk = dh.get("tb", False), dh.get("a_stack", False), dh["tk"]
        K = 2 * dh["a"].shape[2] if a_stack else dh["a"].shape[1]
        nk = K // tk
        deps = [] if dh.get("dep") is None else [dh["dep"]]
        n_dh = 2 + len(deps)
    else:
        nk, n_dh = 1, 1

    def elementwise(i, dhv, x_ref, g_ref, sc_ref, res_refs, out_refs):
        xv = x_ref[...]
        r = lax.rsqrt(jnp.mean(xv * xv, axis=-1, keepdims=True) + EPS)
        xh = xv * r
        n = xh * g_ref[...]
        dn = dhv * (1.0 + sc_ref[...])
        a = dn * g_ref[...]
        rows = [jnp.sum(dhv, axis=0, keepdims=True), jnp.sum(dhv * n, axis=0, keepdims=True),
                jnp.sum(dn * xh, axis=0, keepdims=True)]
        if has_res:
            dres_ref, gsrc_ref, gate_ref = res_refs
            dx_ref, dxg_ref, st_ref = out_refs
            dr = dres_ref[...]
            dx = dr + r * (a - xh * jnp.mean(a * xh, axis=-1, keepdims=True))
            dx_ref[...] = dx
            dxg_ref[...] = (dx * gate_ref[...]).astype(BF)
            rows.append(jnp.sum(dr * gsrc_ref[...], axis=0, keepdims=True))
        else:
            st_ref, = out_refs
            rows.append(jnp.zeros((1, D), F32))

        @pl.when(i == 0)
        def _():
            for k, row in enumerate(rows):
                st_ref[k:k + 1, :] = row

        @pl.when(i > 0)
        def _():
            for k, row in enumerate(rows):
                st_ref[k:k + 1, :] += row

    def body(*refs):
        x_ref, dh_refs, g_ref, sc_ref = refs[0], refs[1:1 + n_dh], refs[1 + n_dh], refs[2 + n_dh]
        rest = refs[3 + n_dh:]
        res_refs, rest = (rest[:3], rest[3:]) if has_res else ((), rest)
        out_refs = rest[:3] if has_res else rest[:1]
        i = pl.program_id(0)
        if not fused:
            elementwise(i, dh_refs[0][...], x_ref, g_ref, sc_ref, res_refs, out_refs)
            return
        acc = rest[-1]
        k = pl.program_id(1)
        part = lax.dot_general(dh_refs[0][...].astype(BF), dh_refs[1][...].astype(BF),
                               (((1,), (1 if tb else 0,)), ((), ())), preferred_element_type=F32)
        if nk == 1:
            elementwise(i, part, x_ref, g_ref, sc_ref, res_refs, out_refs)
            return

        @pl.when(k == 0)
        def _():
            acc[...] = part

        @pl.when(k > 0)
        def _():
            acc[...] += part

        @pl.when(k == nk - 1)
        def _():
            elementwise(i, acc[...], x_ref, g_ref, sc_ref, res_refs, out_refs)

    rowb = lambda w: pl.BlockSpec((1, w), lambda i, *k: (0, 0))
    blk = pl.BlockSpec((tm, D), lambda i, *k: (i, 0))
    st_spec = pl.BlockSpec((4, D), lambda i, *k: (0, 0))
    st_shape = jax.ShapeDtypeStruct((4, D), F32)
    if fused:
        if a_stack:
            nhb = K // 2 // tk
            a_spec = pl.BlockSpec((None, tm, tk), lambda i, k: (k // nhb, i, k % nhb))
        else:
            a_spec = pl.BlockSpec((tm, tk), lambda i, k: (i, k))
        b_spec = pl.BlockSpec((D, tk), lambda i, k: (0, k)) if tb else pl.BlockSpec((tk, D), lambda i, k: (k, 0))
        dh_specs = [a_spec, b_spec] + [pl.BlockSpec(memory_space=pl.ANY)] * len(deps)
        dh_args = [dh["a"], dh["b"]] + deps
        grid, sem = (R // tm, nk), ("arbitrary", "arbitrary")
        scratch = [pltpu.VMEM((tm, D) if nk > 1 else (8, 128), F32)]
    else:
        dh_specs, dh_args, grid, sem, scratch = [blk], [dh], (R // tm,), ("arbitrary",), []
    cp = pltpu.CompilerParams(dimension_semantics=sem)
    if has_res:
        return pl.pallas_call(
            body, name=name, grid=grid, in_specs=[blk] + dh_specs + [rowb(D), rowb(D), blk, blk, rowb(D)],
            out_specs=[blk, blk, st_spec], scratch_shapes=scratch,
            out_shape=[jax.ShapeDtypeStruct((R, D), F32), jax.ShapeDtypeStruct((R, D), BF), st_shape],
            compiler_params=cp,
        )(x, *dh_args, g, sc, dres, gsrc, gate)
    return pl.pallas_call(
        body, name=name, grid=grid, in_specs=[blk] + dh_specs + [rowb(D), rowb(D)],
        out_specs=st_spec, out_shape=st_shape, scratch_shapes=scratch, compiler_params=cp,
    )(x, *dh_args, g, sc)


def ffn_act_bwd(u0, df, cw, cb, *, name, tc=128):
    nb = DFF // tc

    def body(u_ref, df_ref, wg_ref, wv_ref, bg_ref, bv_ref, du_ref, dw_ref, db_ref):
        wg = [wg_ref[k:k + 1, :] for k in range(3)]
        wv = [wv_ref[k:k + 1, :] for k in range(3)]
        bg, bv = bg_ref[...], bv_ref[...]

        def chunk(r0, first, last, acc):
            xg, xv = _ext_rows(u_ref.at[0], r0, first, last), _ext_rows(u_ref.at[1], r0, first, last)
            dfe = _ext_rows(df_ref, r0, first, last)
            xg_d, xg_u, xv_d, xv_u = _roll_dn(xg), _roll_up(xg), _roll_dn(xv), _roll_up(xv)
            ug = bg + xg_d * wg[0] + xg * wg[1] + xg_u * wg[2]
            uv = bv + xv_d * wv[0] + xv * wv[1] + xv_u * wv[2]
            sig = jax.nn.sigmoid(ug)
            dug = dfe * uv * (sig * (1.0 + ug * (1.0 - sig)))
            duv = dfe * (ug * sig)
            rows = _center_rows(r0, first, last)
            du_ref[0, rows, :] = (_roll_up(dug) * wg[0] + dug * wg[1] + _roll_dn(dug) * wg[2])[_CTR].astype(BF)
            du_ref[1, rows, :] = (_roll_up(duv) * wv[0] + duv * wv[1] + _roll_dn(duv) * wv[2])[_CTR].astype(BF)
            terms = [dug * xg_d, dug * xg, dug * xg_u, dug, duv * xv_d, duv * xv, duv * xv_u, duv]
            return tuple(a + jnp.sum(t[_CTR], axis=0, keepdims=True) for a, t in zip(acc, terms))

        acc = _row_chunks(T, chunk, tuple(jnp.zeros((1, tc), F32) for _ in range(8)))
        for k in range(3):
            dw_ref[0, k:k + 1, :] = acc[k]
            dw_ref[1, k:k + 1, :] = acc[4 + k]
        db_ref[0] = acc[3]
        db_ref[1] = acc[7]

    lo = lambda r: pl.BlockSpec((r, tc), lambda j: (0, j))
    hi = lambda r: pl.BlockSpec((r, tc), lambda j: (0, nb + j))
    st = lambda r: pl.BlockSpec((2, r, tc), lambda j: (0, 0, j))
    return pl.pallas_call(
        body, name=name, grid=(nb,),
        in_specs=[st(T), lo(T), lo(3), hi(3), lo(1), hi(1)],
        out_specs=[st(T), st(3), st(1)],
        out_shape=[jax.ShapeDtypeStruct((2, T, DFF), BF), jax.ShapeDtypeStruct((2, 3, DFF), F32),
                   jax.ShapeDtypeStruct((2, 1, DFF), F32)],
        compiler_params=pltpu.CompilerParams(dimension_semantics=("parallel",)),
    )(u0, df, cw, cw, cb, cb)


def convz_bwd(p, dz, cw, cb, dp, *, name):
    o0 = O_CV // (3 * CVB)

    def body(p_ref, dz_ref, w_ref, bias_ref, dp_in, dp_ref, dw_ref, dbias_ref):
        xv, bv, cv = p_ref[:, 0:CVB], p_ref[:, CVB:2 * CVB], p_ref[:, 2 * CVB:3 * CVB]
        ci = cv * xv
        dwc = _conv(ci, w_ref, bias_ref)
        dzv = dz_ref[...]
        ddw = dzv * bv
        dci = _conv_t(ddw, w_ref)
        dp_ref[:, 0:CVB] = (dci * cv).astype(BF)
        dp_ref[:, CVB:2 * CVB] = (dzv * dwc).astype(BF)
        dp_ref[:, 2 * CVB:3 * CVB] = (dci * xv).astype(BF)
        _conv_wgrad(dw_ref, ddw, ci)
        dbias_ref[...] = jnp.sum(ddw, axis=0, keepdims=True)

    own = lambda r: pl.BlockSpec((r, CVB), lambda j: (0, j))
    return pl.pallas_call(
        body, name=name, grid=(CONV // CVB,),
        in_specs=[pl.BlockSpec((T, 3 * CVB), lambda j: (0, o0 + j)), own(T), own(3), own(1),
                  pl.BlockSpec(memory_space=pl.ANY)],
        out_specs=[pl.BlockSpec((T, 3 * CVB), lambda j: (0, o0 + j)), own(3), own(1)],
        out_shape=[jax.ShapeDtypeStruct((T, NIN), BF), jax.ShapeDtypeStruct((3, CONV), F32),
                   jax.ShapeDtypeStruct((1, CONV), F32)],
        input_output_aliases={4: 0},
        compiler_params=pltpu.CompilerParams(dimension_semantics=("parallel",)),
    )(p, dz, cw, cb, dp)


def attn_bwd(q, k, v, do, o, lse, dep, *, name, tq=1024, kc=768):
    NKC, KC = TKV // kc, kc
    deps = [] if dep is None else [dep]

    def body(q_ref, k_ref, v_ref, do_ref, o_ref, lse_ref, *rest):
        dq_ref, dk_ref, dv_ref = rest[len(deps):]
        h, i = pl.program_id(0), pl.program_id(1)

        @pl.when(i == 0)
        def _():
            dk_ref[...] = jnp.zeros_like(dk_ref)

        @pl.when((i == 0) & (h % 2 == 0))
        def _():
            dv_ref[...] = jnp.zeros_like(dv_ref)

        qv = q_ref[...]
        dom = jnp.where(_head_mask(h), do_ref[...], jnp.zeros_like(do_ref[...]))
        delta = jnp.sum(dom.astype(F32) * o_ref[...].astype(F32), axis=-1, keepdims=True)
        lse = lse_ref[:, 0:1]
        dq = jnp.zeros((tq, HP), F32)
        for c in range(NKC):
            cols = slice(c * KC, (c + 1) * KC)
            s = lax.dot_general(qv, k_ref[cols, :], (((1,), (1,)), ((), ())),
                                preferred_element_type=F32) * (SCALE * LOG2E)
            pr = jnp.exp2(s - lse)
            dp = lax.dot_general(dom, v_ref[cols, :], (((1,), (1,)), ((), ())), preferred_element_type=F32)
            ds = (pr * (dp - delta) * SCALE).astype(BF)
            dq = dq + jnp.dot(ds, k_ref[cols, :], preferred_element_type=F32)
            dk_ref[cols, :] += lax.dot_general(ds, qv, (((0,), (0,)), ((), ())), preferred_element_type=F32)
            dv_ref[cols, :] += lax.dot_general(pr.astype(BF), dom, (((0,), (0,)), ((), ())), preferred_element_type=F32)
        dq_ref[...] = dq

    return pl.pallas_call(
        body, name=name, grid=(NH, T // tq),
        in_specs=[pl.BlockSpec((tq, HP), lambda h, i: (i, h)), pl.BlockSpec((TKV, HP), lambda h, i: (0, h)),
                  pl.BlockSpec((TKV, 2 * DV), lambda h, i: (0, h // 2)), pl.BlockSpec((tq, 2 * DV), lambda h, i: (i, h // 2)),
                  pl.BlockSpec((tq, 2 * DV), lambda h, i: (i, h // 2)), pl.BlockSpec((tq, HP), lambda h, i: (i, h)),
                  *([pl.BlockSpec(memory_space=pl.ANY)] * len(deps))],
        out_specs=[pl.BlockSpec((tq, HP), lambda h, i: (i, h)), pl.BlockSpec((TKV, HP), lambda h, i: (0, h)),
                   pl.BlockSpec((TKV, 2 * DV), lambda h, i: (0, h // 2))],
        out_shape=[jax.ShapeDtypeStruct((T, NH * HP), F32), jax.ShapeDtypeStruct((TKV, NH * HP), F32),
                   jax.ShapeDtypeStruct((TKV, NH * DV), F32)],
        compiler_params=pltpu.CompilerParams(dimension_semantics=("arbitrary", "arbitrary")),
    )(q, k, v, do, o, lse, *deps)


def qprep_bwd(p, dq, qg, wq2, cq_t, sq_t, dp, *, name, tm=256):
    qcol = O_Q // 512

    def body(p_ref, dq_ref, g_ref, w_ref, c_ref, s_ref, dp_in, dp_ref, dq2_ref, dg_ref):
        i = pl.program_id(0)
        dqv = dq_ref[...]
        cc = jnp.concatenate([c_ref[...]] * NH, axis=1)
        ss = jnp.concatenate([s_ref[...]] * NH, axis=1)
        dq2 = jnp.concatenate([dqv * cc, dqv * ss], axis=1).astype(BF)
        dq2_ref[...] = dq2
        dcq = lax.dot_general(dq2, w_ref[...], (((1,), (1,)), ((), ())), preferred_element_type=F32)
        pq = p_ref[...]
        r = lax.rsqrt(jnp.sum(pq * pq, axis=-1, keepdims=True) * (1.0 / QL) + EPS)
        xh = pq * r
        a = dcq * g_ref[...]
        dp_ref[...] = (r * (a - xh * (jnp.sum(a * xh, axis=-1, keepdims=True) * (1.0 / QL)))).astype(BF)
        dg = jnp.sum(dcq * xh, axis=0, keepdims=True)

        @pl.when(i == 0)
        def _():
            dg_ref[...] = dg

        @pl.when(i > 0)
        def _():
            dg_ref[...] += dg

    return pl.pallas_call(
        body, name=name, grid=(T // tm,),
        in_specs=[pl.BlockSpec((tm, 512), lambda i: (i, qcol)), pl.BlockSpec((tm, NH * HP), lambda i: (i, 0)), _row(512),
                  pl.BlockSpec((512, 2 * NH * HP), lambda i: (0, 0)),
                  pl.BlockSpec((tm, HP), lambda i: (i, 0)), pl.BlockSpec((tm, HP), lambda i: (i, 0)),
                  pl.BlockSpec(memory_space=pl.ANY)],
        out_specs=[pl.BlockSpec((tm, 512), lambda i: (i, qcol)), pl.BlockSpec((tm, 2 * NH * HP), lambda i: (i, 0)), _row(512)],
        out_shape=[jax.ShapeDtypeStruct((T, NIN), BF), jax.ShapeDtypeStruct((T, 2 * NH * HP), BF),
                   jax.ShapeDtypeStruct((1, 512), F32)],
        input_output_aliases={6: 0},
        compiler_params=pltpu.CompilerParams(dimension_semantics=("arbitrary",)),
    )(p, dq, qg, wq2, cq_t, sq_t, dp)


def kvprep_bwd(pc, p, dk, dv, kvg, wkv2, ck, sk, dp, *, name, tm=256):
    assert tm == TC
    nb = TKV // tm
    kvcol = O_KV // 512

    def body(pc_ref, p_ref, dk_ref, dv_ref, g_ref, w_ref, ck_ref, sk_ref, dp_in, dp_ref, dpc_ref, dkv2_ref, dg_ref):
        i = pl.program_id(0)
        t = jnp.where(i == NLAT, pc_ref[...], p_ref[...])
        pk = t[:, :KVL]
        r = lax.rsqrt(jnp.mean(pk * pk, axis=-1, keepdims=True) + EPS)
        xh = pk * r
        dkv = dk_ref[...]
        dkv2 = jnp.concatenate([dkv, dv_ref[...]], axis=1).astype(BF)
        dkv2_ref[...] = dkv2
        dckv = lax.dot_general(dkv2, w_ref[...], (((1,), (1,)), ((), ())), preferred_element_type=F32)
        a = dckv * g_ref[...]
        dpk = r * (a - xh * jnp.mean(a * xh, axis=-1, keepdims=True))
        dkr = dkv[:, 0:HP]
        for hh in range(1, NH):
            dkr = dkr + dkv[:, hh * HP:(hh + 1) * HP]
        res = jnp.concatenate([dpk, dkr * ck_ref[...], dkr * sk_ref[...]], axis=1).astype(BF)
        dg = jnp.sum(dckv * xh, axis=0, keepdims=True)

        @pl.when(i == 0)
        def _():
            dg_ref[...] = dg

        @pl.when(i > 0)
        def _():
            dg_ref[...] += dg

        @pl.when(i < NLAT)
        def _():
            dp_ref[...] = res

        @pl.when(i == NLAT)
        def _():
            dpc_ref[...] = res

    rb = lambda w: pl.BlockSpec((tm, w), lambda i: (i, 0))
    return pl.pallas_call(
        body, name=name, grid=(nb,),
        in_specs=[pl.BlockSpec((tm, 512), lambda i: (0, 0)),
                  pl.BlockSpec((tm, 512), lambda i: (jnp.minimum(i, NLAT - 1), kvcol)),
                  rb(NH * HP), rb(NH * DV), _row(KVL), pl.BlockSpec((KVL, NH * HP + NH * DV), lambda i: (0, 0)),
                  rb(HP), rb(HP), pl.BlockSpec(memory_space=pl.ANY)],
        out_specs=[pl.BlockSpec((tm, 512), lambda i: (jnp.minimum(i, NLAT - 1), kvcol)),
                   pl.BlockSpec((tm, 512), lambda i: (0, 0)), rb(NH * HP + NH * DV), _row(KVL)],
        out_shape=[jax.ShapeDtypeStruct((T, NIN), BF), jax.ShapeDtypeStruct((TC, 512), BF),
                   jax.ShapeDtypeStruct((TKV, NH * HP + NH * DV), BF), jax.ShapeDtypeStruct((1, KVL), F32)],
        input_output_aliases={8: 0},
        compiler_params=pltpu.CompilerParams(dimension_semantics=("arbitrary",)),
    )(pc, p, dk, dv, kvg, wkv2, ck, sk, dp)


def _pieces(src, width, n):
    out, c = [], src
    while c < src + width:
        k = c // n
        w = min(src + width, (k + 1) * n) - c
        out.append((k, c - k * n, c - src, w))
        c += w
    return out


def _win_moves():
    mv = [(2208, 1024, O_GA), (3232, 1024, O_GC), (0, KVL, O_KV), (256, DR, O_KV + KVL + DN), (288, QL, O_Q)]
    mv += [(256 + _swap_start(g), 8, O_KV + KVL + HP + DN + 8 * g) for g in range(4)]
    for j in range(CONV // CVB):
        base = O_CV + 3 * CVB * j
        mv += [(672 + CVB * j, CVB, base), (1184 + CVB * j, CVB, base + CVB), (1696 + CVB * j, CVB, base + 2 * CVB)]
    return mv


_WIN_ZERO = [(O_KV + KVL, DN), (O_KV + KVL + DN + DR, HP - DN - DR), (O_KV + KVL + HP, DN),
             (O_KV + KVL + HP + DN + DR, HP - DN - DR), (O_Q + QL, 512 - QL)]


def build_win(g, *, name, tm=256):
    def body(g_ref, o_ref):
        for src, w, dst in _win_moves():
            for k, a, off, pw in _pieces(src, w, SH_IN):
                o_ref[:, dst + off:dst + off + pw] = g_ref[k, :, a:a + pw]
        for c0, w in _WIN_ZERO:
            o_ref[:, c0:c0 + w] = jnp.zeros((tm, w), o_ref.dtype)

    return pl.pallas_call(
        body, name=name, grid=(D // tm,), in_specs=[pl.BlockSpec((NDEV, tm, SH_IN), lambda i: (0, i, 0))],
        out_specs=pl.BlockSpec((tm, NIN), lambda i: (i, 0)), out_shape=jax.ShapeDtypeStruct((D, NIN), g.dtype),
        compiler_params=pltpu.CompilerParams(dimension_semantics=("parallel",)),
    )(g)


def shard_win_grad(dwt, dwct, *, name, col0=0, tc=256):
    n = dwt.shape[1]

    def body(dw_ref, dwc_ref, o_ref, kvs):
        kvs[...] = dw_ref[O_KV:O_KV + 512, :] + dwc_ref[...]

        def src(row, w):
            if O_KV <= row < O_KV + 512:
                return kvs[row - O_KV:row - O_KV + w, :]
            return dw_ref[row:row + w, :]

        for s, w, dst in _win_moves():
            if w == 8 or s == 256:
                continue
            for k, a, off, pw in _pieces(s, w, SH_IN):
                o_ref[k, a:a + pw, :] = src(dst + off, pw).astype(o_ref.dtype)
        for g in range(4):
            val = src(O_KV + KVL + DN + 8 * g, 8) + src(O_KV + KVL + HP + DN + _swap_start(g), 8)
            o_ref[0, 256 + 8 * g:256 + 8 * g + 8, :] = val.astype(o_ref.dtype)

    return pl.pallas_call(
        body, name=name, grid=(n // tc,),
        in_specs=[pl.BlockSpec((NIN, tc), lambda j: (0, j)), pl.BlockSpec((512, tc), lambda j: (0, j + col0 // tc))],
        out_specs=pl.BlockSpec((NDEV, SH_IN, tc), lambda j: (0, 0, j)),
        out_shape=jax.ShapeDtypeStruct((NDEV, SH_IN, n), BF),
        scratch_shapes=[pltpu.VMEM((512, tc), F32)],
        compiler_params=pltpu.CompilerParams(dimension_semantics=("parallel",)),
    )(dwt, dwct)


def _eye(n, m):
    return (lax.broadcasted_iota(jnp.int32, (n, m), 0) == lax.broadcasted_iota(jnp.int32, (n, m), 1)).astype(BF)


_NT = (((1,), (1,)), ((), ()))


def build_wq_wkv(gq, gkv, *, name):
    def body(gq_ref, gkv_ref, q_ref, kv_ref):
        q_ref[...] = jnp.zeros_like(q_ref)
        kv_ref[...] = jnp.zeros_like(kv_ref)
        eye = _eye(QL, QL)
        for h in range(NH):
            qh = lax.dot_general(eye, gq_ref[h], _NT, preferred_element_type=F32).astype(q_ref.dtype)
            q_ref[0:QL, h * HP:h * HP + DN + DR] = qh
            for g in range(4):
                c0 = NH * HP + h * HP + DN + 8 * g
                q_ref[0:QL, c0:c0 + 8] = qh[:, DN + _swap_start(g):DN + _swap_start(g) + 8]
            kv_ref[:, h * HP:h * HP + DN] = gkv_ref[h, :, 0:DN]
            kv_ref[:, NH * HP + h * DV:NH * HP + (h + 1) * DV] = gkv_ref[h, :, DN:DN + DV]

    vm = pl.BlockSpec(memory_space=pltpu.VMEM)
    return pl.pallas_call(
        body, name=name, in_specs=[vm, vm], out_specs=[vm, vm],
        out_shape=[jax.ShapeDtypeStruct((512, 2 * NH * HP), gq.dtype), jax.ShapeDtypeStruct((KVL, NH * HP + NH * DV), gq.dtype)],
    )(gq, gkv)


def shard_wq_wkv_grad(dwq2, dwkv2, *, name):
    def body(q_ref, kv_ref, gq_ref, gkv_ref, xs):
        xs[...] = jnp.zeros_like(xs)
        eye = _eye(DN + DR, HP)
        for h in range(NH):
            xs[:, 0:DN] = q_ref[0:QL, h * HP:h * HP + DN].astype(BF)
            for g in range(4):
                a = q_ref[0:QL, h * HP + DN + 8 * g:h * HP + DN + 8 * g + 8]
                c0 = NH * HP + h * HP + DN + _swap_start(g)
                xs[:, DN + 8 * g:DN + 8 * g + 8] = (a + q_ref[0:QL, c0:c0 + 8]).astype(BF)
            gq_ref[h] = lax.dot_general(eye, xs[...], _NT, preferred_element_type=F32).astype(BF)
            gkv_ref[h, :, 0:DN] = kv_ref[:, h * HP:h * HP + DN].astype(BF)
            gkv_ref[h, :, DN:DN + DV] = kv_ref[:, NH * HP + h * DV:NH * HP + (h + 1) * DV].astype(BF)

    vm = pl.BlockSpec(memory_space=pltpu.VMEM)
    return pl.pallas_call(
        body, name=name, in_specs=[vm, vm], out_specs=[vm, vm],
        out_shape=[jax.ShapeDtypeStruct((NDEV, DN + DR, QL), BF), jax.ShapeDtypeStruct((NDEV, KVL, DN + DV), BF)],
        scratch_shapes=[pltpu.VMEM((QL, HP), BF)],
    )(dwq2, dwkv2)


def unshard_cols(g, *, name, tm=256):
    _, K, n = g.shape
    tm = _pick(K, tm, 16)

    def body(g_ref, o_ref):
        for k in range(NDEV):
            o_ref[:, k * n:(k + 1) * n] = g_ref[k]

    return pl.pallas_call(
        body, name=name, grid=(K // tm,), in_specs=[pl.BlockSpec((NDEV, tm, n), lambda i: (0, i, 0))],
        out_specs=pl.BlockSpec((tm, NDEV * n), lambda i: (i, 0)), out_shape=jax.ShapeDtypeStruct((K, NDEV * n), g.dtype),
        compiler_params=pltpu.CompilerParams(dimension_semantics=("parallel",)),
    )(g)


def shard_cols(w, *, name, tm=256):
    K, n8 = w.shape
    n = n8 // NDEV
    tm = _pick(K, tm, 16)

    def body(w_ref, o_ref):
        for k in range(NDEV):
            o_ref[k] = w_ref[:, k * n:(k + 1) * n]

    return pl.pallas_call(
        body, name=name, grid=(K // tm,), in_specs=[pl.BlockSpec((tm, n8), lambda i: (i, 0))],
        out_specs=pl.BlockSpec((NDEV, tm, n), lambda i: (0, i, 0)), out_shape=jax.ShapeDtypeStruct((NDEV, K, n), w.dtype),
        compiler_params=pltpu.CompilerParams(dimension_semantics=("parallel",)),
    )(w)


def _rope_tables():
    t = np.arange(T)
    row = (t // GRID_W).astype(np.float32)
    col = (t % GRID_W).astype(np.float32)
    axis_dim = DR // 2
    inv = (np.float32(ROPE_THETA) ** (-np.arange(0, axis_dim, 2, dtype=np.float32) / np.float32(axis_dim))).astype(np.float32)
    ar, ac = (row[:, None] * inv).astype(np.float32), (col[:, None] * inv).astype(np.float32)
    cosv = np.concatenate([np.cos(ar), np.cos(ar), np.cos(ac), np.cos(ac)], axis=1).astype(np.float32)
    sinv = np.concatenate([-np.sin(ar), np.sin(ar), -np.sin(ac), np.sin(ac)], axis=1).astype(np.float32)
    ck = np.zeros((TKV, HP), np.float32)
    sk = np.zeros((TKV, HP), np.float32)
    ck[T:, DN:DN + DR] = 1.0
    ck[:T, DN:DN + DR] = cosv
    sk[:T, DN:DN + DR] = sinv
    cq = np.zeros((T, HP), np.float32)
    cq[:, :DN] = 1.0
    cq[:, DN:DN + DR] = cosv
    return jnp.asarray(ck), jnp.asarray(sk), jnp.asarray(cq), jnp.asarray(sk[:T])


def _local_step(x, ctx, tgt, mod_lat, mod_ctx, n1g, qg, kvg, n2g, fg, conv_w, conv_b, ffn_w, ffn_b, get_w, put_g, dep0):
    sh1, sc1, g1, sh2, sc2, g2 = [mod_lat[:, i * D:(i + 1) * D] for i in range(6)]
    csh1, csc1 = mod_ctx[:, 0:D], mod_ctx[:, D:2 * D]
    ck, sk, cq_t, sq_t = _rope_tables()
    qg_p = jnp.pad(qg, ((0, 0), (0, 512 - QL)))

    hcat = normmod_cat(ctx, x, n1g, csc1, csh1, sc1, sh1, dep0, name="normmod1")
    win = get_w("in", hcat)
    p = mm(hcat, win, M=T, tn=768, name="in_proj")
    pc = mm(hcat, win, M=TC, N=512, a_off=(T, 0), b_off=(0, O_KV), name="in_proj_ctx")
    wq2, wkv2, wao, wco, wo = get_w("mid", p)
    kh, vh, ckv = kvprep(pc, p, kvg, wkv2, ck, sk, name="kvprep")
    qr, cq = qprep(p, qg_p, wq2, cq_t, sq_t, name="qprep")
    o, lse = attn_fwd(qr, kh, vh, name="attn_fwd")
    z = convz(p, conv_w, conv_b, name="convz")
    ya, yc, merged = out_proj_merge(o, wao, z, wco, p, name="attn_conv_out_gate_merge")
    a_out, x1, h2 = oproj_resid(merged, wo, x, g1, n2g, sc2, sh2, name="o_proj_resid_normmod2")
    wup = get_w("up", h2)
    u0 = mm(h2, wup, tb=True, o_stack=True, tn=1408, name="up_proj")
    f = ffn_act(u0, ffn_w, ffn_b, name="ffn_act")
    wdn = get_w("down", f)
    dn, dx2, dd, dfg, loss = down_final(f, wdn, x1, g2, fg, tgt, name="down_proj_final_loss")

    df = mm(dd, wdn, tb=True, tn=1408, name="down_proj_dx")
    dwdn = mm(f, dd, ta=True, out_dtype=BF, tm=1408, name="down_proj_dw")
    du0, dffn_w, dffn_b = ffn_act_bwd(u0, df, ffn_w, ffn_b, name="ffn_act_bwd")
    dwup = mm(du0, h2, ta=True, a_stack=True, out_dtype=BF, tm=1408, name="up_proj_dw")
    tok = put_g("ffn", dict(dwup=dwup, dwdn=dwdn))
    dx1, da, st2 = normmod_bwd(x1, dict(a=du0, b=wup, a_stack=True, tk=DFF, dep=tok), n2g, sc2, dx2, dn, g1,
                               name="up_proj_dx_normmod2_bwd")

    dwo = mm(merged, da, ta=True, out_dtype=BF, tn=512, name="o_proj_dw")
    dya, dyc, dp, do, dz = oproj_dx_gate_bwd(da, wo, p, ya, yc, wao, wco, name="o_proj_dx_gate_merge_bwd")
    dwao = mm(o, dya, ta=True, out_dtype=BF, tn=512, name="attn_out_dw")
    dwco = mm(z, dyc, ta=True, out_dtype=BF, tn=512, name="conv_out_dw")
    tok = put_g("mid", dict(dwao=dwao, dwco=dwco, dwo=dwo))
    dp, dconv_w, dconv_b = convz_bwd(p, dz, conv_w, conv_b, dp, name="convz_bwd")
    dq, dk, dv = attn_bwd(qr, kh, vh, do, o, lse, tok, name="attn_bwd")
    dp, dq2, dqg = qprep_bwd(p, dq, qg_p, wq2, cq_t, sq_t, dp, name="qprep_bwd")
    dp, dpc, dkv2, dkvg = kvprep_bwd(pc, p, dk, dv, kvg, wkv2, ck, sk, dp, name="kvprep_bwd")

    dwin_c = mm(dpc, hcat, ta=True, K=TC, b_off=(T, 0), name="in_proj_ctx_dw")
    tok = None
    for half in range(2):
        dwin = mm(dp, hcat, ta=True, K=T, N=D // 2, b_off=(0, half * (D // 2)), tm=768, dep=tok,
                  name=f"in_proj_dw_{half}")
        tok = put_g(f"in{half}", dict(dwin=dwin, dwin_c=dwin_c, col0=half * (D // 2)))
    dwq2 = mm(cq, dq2, ta=True, dep=tok, name="q_up_dw")
    dwkv2 = mm(ckv, dkv2, ta=True, dep=tok, name="kv_up_dw")
    tok = put_g("qkv", dict(dwq2=dwq2, dwkv2=dwkv2))
    dhc = mm(dpc, win, tb=True, N=D, K=512, b_off=(0, O_KV), dep=tok, name="in_proj_ctx_dx")
    dx, _, st1 = normmod_bwd(x, dict(a=dp, b=win, tb=True, tk=NIN, dep=tok), n1g, sc1, dx1, a_out, g1,
                             name="in_proj_dx_normmod1_bwd")
    stc = normmod_bwd(ctx, dhc, n1g, csc1, None, None, None, name="normmod1_ctx_bwd")

    return dict(loss=loss, dx=dx, st1=st1, st2=st2, stc=stc, dqg=dqg, dkvg=dkvg, dfg=dfg,
                dconv_w=dconv_w, dconv_b=dconv_b, dffn_w=dffn_w, dffn_b=dffn_b)


def _me():
    x, y, c = lax.axis_index("x"), lax.axis_index("y"), lax.axis_index("c")
    return x, y, c, 4 * x + 2 * y + c


def _peer(x, y, c, k):
    px = 1 - x if k & 4 else x
    py = 1 - y if k & 2 else y
    pc = 1 - c if k & 1 else c
    return (px, py, pc), 4 * px + 2 * py + pc


def _exchange_tiles(src_of_peer, buf, send_sem, recv_sem):
    x, y, c, me = _me()
    for k in range(1, NDEV):
        dev, lin = _peer(x, y, c, k)
        pltpu.make_async_remote_copy(src_ref=src_of_peer(lin), dst_ref=buf.at[me], send_sem=send_sem, recv_sem=recv_sem,
                                     device_id=dev, device_id_type=MESH).start()
    seven = buf.at[pl.ds(0, NDEV - 1)]
    pltpu.make_async_remote_copy(src_ref=seven, dst_ref=seven, send_sem=send_sem, recv_sem=recv_sem,
                                 device_id=(x, y, c), device_id_type=MESH).wait()


def _silu(z):
    return z * jax.nn.sigmoid(z)


def ada_fwd(c, c_ctx, ffn_w, conv_w, w_shard, b_ada, deps, *, name):
    nsh, nf, nc = w_shard.shape[1], ffn_w.shape[2], conv_w.shape[2]
    deps = [d for d in deps if d is not None]

    def body(c_ref, cc_ref, fw_ref, cw_ref, w_ref, b_ref, *rest):
        s_ref, ml_ref, mc_ref, fwf_ref, cwf_ref, m_ref, mine, res, sems = rest[len(deps):]
        x, y, c, me = _me()
        mine[...] = jnp.zeros_like(mine)
        mine[0:1, :] = _silu(c_ref[...])
        mine[1:2, :] = _silu(cc_ref[...])
        for k in range(3):
            mine[2 + k:3 + k, 0:fw_ref.shape[2]] = fw_ref[k]
            mine[5 + k:6 + k, 0:cw_ref.shape[2]] = cw_ref[k]
        s_ref[me] = mine[...]
        _exchange_tiles(lambda lin: mine, s_ref, sems.at[0], sems.at[1])
        sall = s_ref[...].reshape(NDEV * 8, D).astype(BF)
        r = jnp.dot(sall, w_ref[...].astype(BF), preferred_element_type=F32) + b_ref[me]
        res[...] = r.reshape(NDEV, 8, nsh)
        m_ref[me] = res[me]
        _exchange_tiles(lambda lin: res.at[lin], m_ref, sems.at[2], sems.at[3])
        for j in range(NDEV):
            ml_ref[:, j * nsh:(j + 1) * nsh] = m_ref[j, 0:1, :]
            mc_ref[:, j * nsh:(j + 1) * nsh] = m_ref[j, 1:2, :]
            fwf_ref[:, j * nf:(j + 1) * nf] = s_ref[j, 2:5, 0:nf]
            cwf_ref[:, j * nc:(j + 1) * nc] = s_ref[j, 5:8, 0:nc]

    vm = pl.BlockSpec(memory_space=pltpu.VMEM)
    return pl.pallas_call(
        body, name=name, in_specs=[vm] * 6 + [pl.BlockSpec(memory_space=pl.ANY)] * len(deps), out_specs=[vm] * 5,
        out_shape=[jax.ShapeDtypeStruct((NDEV, 8, D), F32), jax.ShapeDtypeStruct((1, NDEV * nsh), F32),
                   jax.ShapeDtypeStruct((1, NDEV * nsh), F32), jax.ShapeDtypeStruct((3, NDEV * nf), F32),
                   jax.ShapeDtypeStruct((3, NDEV * nc), F32)],
        scratch_shapes=[pltpu.VMEM((NDEV, 8, nsh), F32), pltpu.VMEM((8, D), F32), pltpu.VMEM((NDEV, 8, nsh), F32),
                        pltpu.SemaphoreType.DMA((4,))],
    )(c, c_ctx, ffn_w, conv_w, w_shard, b_ada, *deps)


P_DML, P_DMC, P_N1, P_QG, P_KVG, P_CB, P_N2, P_FB, P_FG, P_CW, P_FW, P_LOSS, P_ROWS = 0, 8, 16, 17, 18, 19, 20, 21, 27, 28, 31, 49, 56
NSH = 6 * D // NDEV
FROWS = 3


def pack_small(r, *, name):
    ins = [r["st1"], r["st2"], r["stc"], r["dqg"], r["dkvg"], r["dconv_b"], r["dffn_b"], r["dfg"], r["dconv_w"],
           r["dffn_w"], r["loss"]]

    def put_wide(p, row0, row, n):
        for j in range(-(-n // D)):
            w = min(D, n - j * D)
            p[row0 + j:row0 + j + 1, 0:w] = row[:, j * D:j * D + w]

    def body(st1, st2, stc, qg, kvg, cb, fb, fg, cw, fw, loss, p):
        p[...] = jnp.zeros_like(p)
        lat = (st1.at[0:1], st1.at[1:2], st1.at[3:4], st2.at[0:1], st2.at[1:2], st2.at[3:4])
        ctx = (stc.at[0:1], stc.at[1:2])
        for j in range(NDEV):
            done = 0
            while done < NSH:
                q, off = divmod(j * NSH + done, D)
                w = min(D - off, NSH - done)
                p[P_DML + j:P_DML + j + 1, done:done + w] = lat[q][:, off:off + w]
                if q < len(ctx):
                    p[P_DMC + j:P_DMC + j + 1, done:done + w] = ctx[q][:, off:off + w]
                done += w
        p[P_N1:P_N1 + 1, :] = st1[2:3, :] + stc[2:3, :]
        p[P_N2:P_N2 + 1, :] = st2[2:3, :]
        put_wide(p, P_QG, qg, 512)
        put_wide(p, P_KVG, kvg, KVL)
        put_wide(p, P_CB, cb, CONV)
        put_wide(p, P_FG, fg, D)
        put_wide(p, P_LOSS, loss, 128)
        for s in range(2):
            put_wide(p, P_FB + FROWS * s, fb.at[s], DFF)
        for k in range(3):
            put_wide(p, P_CW + k, cw.at[k:k + 1], CONV)
            for s in range(2):
                put_wide(p, P_FW + FROWS * (2 * k + s), fw.at[s, k:k + 1], DFF)

    vm = pl.BlockSpec(memory_space=pltpu.VMEM)
    return pl.pallas_call(
        body, name=name, in_specs=[vm] * len(ins), out_specs=vm, out_shape=jax.ShapeDtypeStruct((P_ROWS, D), F32),
    )(*ins)


def sum_slots(a, *, name):
    rows = dict(norm1_g=(P_N1, D), q_norm_g=(P_QG, QL), kv_norm_g=(P_KVG, KVL), conv_b=(P_CB, CONV), norm2_g=(P_N2, D),
                final_g=(P_FG, D))

    def body(a_ref, sum_ref, *out):
        acc = a_ref[0]
        for k in range(1, NDEV):
            acc = acc + a_ref[k]
        sum_ref[...] = acc
        for ref, (row, n) in zip(out, rows.values()):
            ref[...] = sum_ref[row:row + 1, 0:n]
        fb, bada = out[len(rows):]
        for s in range(2):
            for j in range(FROWS):
                w = min(D, DFF - j * D)
                row = P_FB + FROWS * s + j
                fb[:, s * DFF + j * D:s * DFF + j * D + w] = sum_ref[row:row + 1, 0:w]
        for j in range(NDEV):
            bada[:, j * NSH:(j + 1) * NSH] = (sum_ref[P_DML + j:P_DML + j + 1, 0:NSH]
                                              + sum_ref[P_DMC + j:P_DMC + j + 1, 0:NSH])

    vm = pl.BlockSpec(memory_space=pltpu.VMEM)
    widths = [n for _, n in rows.values()] + [2 * DFF, 6 * D]
    outs = pl.pallas_call(
        body, name=name, in_specs=[vm], out_specs=[vm] * (1 + len(widths)),
        out_shape=[jax.ShapeDtypeStruct(a.shape[1:], F32)] + [jax.ShapeDtypeStruct((1, n), F32) for n in widths])(a)
    return outs[0], dict(zip(list(rows) + ["ffn_conv_b", "b_ada"], outs[1:]))


def ada_bwd(s_all, a_all, a_sum, w_shard, c_ctx, *, name):
    nsh = w_shard.shape[1]
    assert nsh == NSH

    def body(s_ref, a_ref, sum_ref, w_ref, c_ref, dw_ref, gc_ref, s16, dm16, part, buf, sems):
        x, y, c, me = _me()
        s16[...] = jnp.zeros_like(s16)
        dm16[...] = jnp.zeros_like(dm16)
        for k in range(NDEV):
            s16[k:k + 1, :] = s_ref[k, 0:1, :]
            dm16[k:k + 1, :] = a_ref[k, pl.ds(P_DML + me, 1), 0:nsh]
        s16[8:9, :] = s_ref[0, 1:2, :]
        dm16[8:9, :] = sum_ref[pl.ds(P_DMC + me, 1), 0:nsh]
        dw_ref[...] = lax.dot_general(s16[...].astype(BF), dm16[...].astype(BF), (((0,), (0,)), ((), ())),
                                      preferred_element_type=F32)
        part[...] = lax.dot_general(dm16[8:16, :].astype(BF), w_ref[...].astype(BF), (((1,), (1,)), ((), ())),
                                    preferred_element_type=F32)
        buf[me] = part[...]
        _exchange_tiles(lambda lin: part, buf, sems.at[0], sems.at[1])
        acc = buf[0]
        for k in range(1, NDEV):
            acc = acc + buf[k]
        z = c_ref[...]
        sg = jax.nn.sigmoid(z)
        gc_ref[...] = acc * (sg * (1.0 + z * (1.0 - sg)))

    vm = pl.BlockSpec(memory_space=pltpu.VMEM)
    return pl.pallas_call(
        body, name=name, in_specs=[vm] * 5, out_specs=[vm, vm],
        out_shape=[jax.ShapeDtypeStruct((D, nsh), F32), jax.ShapeDtypeStruct((8, D), F32)],
        scratch_shapes=[pltpu.VMEM((16, D), F32), pltpu.VMEM((16, nsh), F32), pltpu.VMEM((8, D), F32),
                        pltpu.VMEM((NDEV, 8, D), F32), pltpu.SemaphoreType.DMA((2,))],
    )(s_all, a_all, a_sum, w_shard, c_ctx)


HBM_SPEC = pl.BlockSpec(memory_space=pltpu.HBM)
SEM_SPEC = pl.BlockSpec(memory_space=pltpu.SEMAPHORE)
EFFECT = pltpu.SideEffectType.DATAFLOW_SIDE_EFFECTING


ALL_PEERS = tuple(range(1, NDEV))
FIRST_HOP = (1, 2, 4, 6)
RELAY = (2, 4, 6)


def _exchange_copies(srcs, lands, send, recv, per_peer, peers):
    x, y, c, me = _me()
    n = len(peers)
    cps = []
    for t in range(len(srcs)):
        for j, k in enumerate(peers):
            dev, lin = _peer(x, y, c, k)
            cps.append(pltpu.make_async_remote_copy(
                src_ref=srcs[t].at[lin] if per_peer else srcs[t], dst_ref=lands[t].at[me],
                send_sem=send.at[n * t + j], recv_sem=recv.at[n * t + j], device_id=dev, device_id_type=MESH))
    return cps


def _relay_copies(lands, send, recv):
    x, y, c, me = _me()
    n = len(RELAY)
    cps = []
    for t in range(len(lands)):
        for j, k in enumerate(RELAY):
            slot = lands[t].at[_peer(x, y, c, k)[1]]
            cps.append(pltpu.make_async_remote_copy(
                src_ref=slot, dst_ref=slot, send_sem=send.at[n * t + j], recv_sem=recv.at[n * t + j],
                device_id=(x, y, 1 - c), device_id_type=MESH))
    return cps


def _own_copies(srcs, lands, own, per_peer):
    me = _me()[3]
    return [pltpu.make_async_copy(srcs[t].at[me] if per_peer else srcs[t], lands[t].at[me], own.at[t])
            for t in range(len(srcs))]


def exchange_start(srcs, *, per_peer, name, dep=None, peers=ALL_PEERS):
    nt = len(srcs)
    ns = len(peers) * nt
    land_shapes = [(a.shape if per_peer else (NDEV,) + a.shape) for a in srcs]
    deps = [] if dep is None else [dep]

    def body(*refs):
        src, land = refs[:nt], refs[nt:2 * nt]
        send, recv, own = refs[2 * nt + len(deps):2 * nt + len(deps) + 3]
        for cp in _exchange_copies(src, land, send, recv, per_peer, peers) + _own_copies(src, land, own, per_peer):
            cp.start()
        refs[-1][...] = jnp.zeros_like(refs[-1])

    hb = lambda a: pltpu.with_memory_space_constraint(a, pltpu.HBM)
    outs = pl.pallas_call(
        body, name=name,
        out_shape=(pltpu.SemaphoreType.DMA((ns,)), pltpu.SemaphoreType.DMA((ns,)), pltpu.SemaphoreType.DMA((nt,)),
                   *[pltpu.HBM(a.shape, a.dtype) for a in srcs], *[pltpu.HBM(s, a.dtype) for s, a in zip(land_shapes, srcs)],
                   jax.ShapeDtypeStruct((8, 128), F32)),
        in_specs=[HBM_SPEC] * (2 * nt) + [pl.BlockSpec(memory_space=pl.ANY)] * len(deps),
        out_specs=(SEM_SPEC, SEM_SPEC, SEM_SPEC, *([HBM_SPEC] * (2 * nt)), pl.BlockSpec(memory_space=pltpu.VMEM)),
        input_output_aliases={i: 3 + i for i in range(2 * nt)},
        compiler_params=pltpu.CompilerParams(has_side_effects=EFFECT),
    )(*[hb(a) for a in srcs], *[hb(lax.empty(s, a.dtype)) for s, a in zip(land_shapes, srcs)], *deps)
    return dict(send=outs[0], recv=outs[1], own=outs[2], src=list(outs[3:3 + nt]), land=list(outs[3 + nt:3 + 2 * nt]),
                token=outs[-1], per_peer=per_peer, peers=peers)


def exchange_wait(h, after, *, name):
    nt = len(h["src"])
    per_peer, peers = h["per_peer"], h["peers"]
    after = list(after) if isinstance(after, (list, tuple)) else [after]

    def body(*refs):
        src, land, send, recv, own = refs[:nt], refs[nt:2 * nt], refs[2 * nt], refs[2 * nt + 1], refs[2 * nt + 2]
        for cp in _exchange_copies(src, land, send, recv, per_peer, peers):
            cp.wait_send()
            cp.wait_recv()
        for cp in _own_copies(src, land, own, per_peer):
            cp.wait()

    outs = pl.pallas_call(
        body, name=name,
        out_shape=(*[pltpu.HBM(a.shape, a.dtype) for a in h["src"]], *[pltpu.HBM(a.shape, a.dtype) for a in h["land"]]),
        in_specs=[HBM_SPEC] * (2 * nt) + [SEM_SPEC, SEM_SPEC, SEM_SPEC] + [pl.BlockSpec(memory_space=pl.ANY)] * len(after),
        out_specs=tuple([HBM_SPEC] * (2 * nt)),
        input_output_aliases={i: i for i in range(2 * nt)},
        compiler_params=pltpu.CompilerParams(has_side_effects=EFFECT),
    )(*h["src"], *h["land"], h["send"], h["recv"], h["own"], *after)
    return list(outs[nt:])


def relay_start(lands, *, name):
    nt = len(lands)
    ns = len(RELAY) * nt

    def body(*refs):
        for cp in _relay_copies(refs[:nt], refs[nt], refs[nt + 1]):
            cp.start()

    outs = pl.pallas_call(
        body, name=name,
        out_shape=(pltpu.SemaphoreType.DMA((ns,)), pltpu.SemaphoreType.DMA((ns,)),
                   *[pltpu.HBM(a.shape, a.dtype) for a in lands]),
        in_specs=[HBM_SPEC] * nt, out_specs=(SEM_SPEC, SEM_SPEC, *([HBM_SPEC] * nt)),
        input_output_aliases={i: 2 + i for i in range(nt)},
        compiler_params=pltpu.CompilerParams(has_side_effects=EFFECT),
    )(*lands)
    return dict(send=outs[0], recv=outs[1], land=list(outs[2:]))


def relay_wait(h, *, name):
    nt = len(h["land"])

    def body(*refs):
        for cp in _relay_copies(refs[:nt], refs[nt], refs[nt + 1]):
            cp.wait_send()
            cp.wait_recv()

    outs = pl.pallas_call(
        body, name=name, out_shape=tuple(pltpu.HBM(a.shape, a.dtype) for a in h["land"]),
        in_specs=[HBM_SPEC] * nt + [SEM_SPEC, SEM_SPEC], out_specs=tuple([HBM_SPEC] * nt),
        input_output_aliases={i: i for i in range(nt)},
        compiler_params=pltpu.CompilerParams(has_side_effects=EFFECT),
    )(*h["land"], h["send"], h["recv"])
    return list(outs)


def _adamw_math(w, g, m, v):
    nm = B1 * m + (1.0 - B1) * g
    nv = B2 * v + (1.0 - B2) * (g * g)
    m_hat = nm / (1.0 - B1 ** STEP)
    v_hat = nv / (1.0 - B2 ** STEP)
    return -LR * (m_hat / (jnp.sqrt(v_hat) + AEPS) + WD * w), nm, nv


def adamw_many(ws, gs, ms, vs, *, name):
    n = len(ws)

    def body(*refs):
        for k in range(n):
            d, nm, nv = _adamw_math(refs[k][...], refs[n + k][...], refs[2 * n + k][...], refs[3 * n + k][...])
            refs[4 * n + k][...] = d
            refs[5 * n + k][...] = nm
            refs[6 * n + k][...] = nv

    vm = pl.BlockSpec(memory_space=pltpu.VMEM)
    sh = [jax.ShapeDtypeStruct(w.shape, F32) for w in ws]
    outs = pl.pallas_call(body, name=name, in_specs=[vm] * (4 * n), out_specs=[vm] * (3 * n), out_shape=sh * 3,
                          )(*ws, *gs, *ms, *vs)
    return outs[:n], outs[n:2 * n], outs[2 * n:]


def adamw(w, g, m, v, *, name, tr=256):
    R, C = w.shape
    tr = _pick(R, tr, 8)

    def body(w_ref, g_ref, m_ref, v_ref, d_ref, nm_ref, nv_ref):
        d_ref[...], nm_ref[...], nv_ref[...] = _adamw_math(w_ref[...], g_ref[...], m_ref[...], v_ref[...])

    blk = pl.BlockSpec((tr, C), lambda i: (i, 0))
    sh = jax.ShapeDtypeStruct((R, C), F32)
    return pl.pallas_call(
        body, name=name, grid=(R // tr,), in_specs=[blk, blk, blk, blk], out_specs=[blk, blk, blk],
        out_shape=[sh, sh, sh], compiler_params=pltpu.CompilerParams(dimension_semantics=("parallel",)),
    )(w, g, m, v)


def adamw_slots(w, slots, m, v, *, name, tr=256):
    unit = w.ndim == 3
    R, C = w.shape[0], w.shape[-1]
    parts = list(slots) if isinstance(slots, (list, tuple)) else [slots]
    n = len(parts)
    assert sum(s.shape[-1] for s in parts) == C
    if R % 16 == 0:
        tr = _pick(R, tr, 16)
    else:
        tr = 144

    def body(w_ref, *refs):
        s_refs, (m_ref, v_ref, g_ref, d_ref, nm_ref, nv_ref) = refs[:n], refs[n:]
        gs = []
        for s_ref in s_refs:
            g = s_ref[0].astype(F32)
            for k in range(1, NDEV):
                g = g + s_ref[k].astype(F32)
            gs.append(g)
        g = gs[0] if n == 1 else jnp.concatenate(gs, axis=-1)
        g_ref[...] = g
        d_ref[...], nm_ref[...], nv_ref[...] = _adamw_math(w_ref[...], g, m_ref[...], v_ref[...])

    blk = pl.BlockSpec((tr, None, C), lambda i: (i, 0, 0)) if unit else pl.BlockSpec((tr, C), lambda i: (i, 0))
    sh = jax.ShapeDtypeStruct(w.shape, F32)
    return pl.pallas_call(
        body, name=name, grid=(pl.cdiv(R, tr),),
        in_specs=[blk] + [pl.BlockSpec((NDEV, tr, s.shape[-1]), lambda i: (0, i, 0)) for s in parts] + [blk, blk],
        out_specs=[blk, blk, blk, blk], out_shape=[sh, sh, sh, sh],
        compiler_params=pltpu.CompilerParams(dimension_semantics=("parallel",)),
    )(w, *parts, m, v)


def kernel(x, c, ctx, c_ctx, w_ada, b_ada, norm1_g, w_in, q_norm_g, kv_norm_g, w_uq, w_ukv, conv_w, conv_b, w_attn_out, w_conv_out, w_o, norm2_g, w_up, ffn_conv_w, ffn_conv_b, w_down, final_g, loss_target, m_c_ctx, m_w_ada, m_b_ada, m_norm1_g, m_w_in, m_q_norm_g, m_kv_norm_g, m_w_uq, m_w_ukv, m_conv_w, m_conv_b, m_w_attn_out, m_w_conv_out, m_w_o, m_norm2_g, m_w_up, m_ffn_conv_w, m_ffn_conv_b, m_w_down, m_final_g, v_c_ctx, v_w_ada, v_b_ada, v_norm1_g, v_w_in, v_q_norm_g, v_kv_norm_g, v_w_uq, v_w_ukv, v_conv_w, v_conv_b, v_w_attn_out, v_w_conv_out, v_w_o, v_norm2_g, v_w_up, v_ffn_conv_w, v_ffn_conv_b, v_w_down, v_final_g):
    me = 4 * lax.axis_index("x") + 2 * lax.axis_index("y") + lax.axis_index("c")
    W = dict(c_ctx=c_ctx, w_ada=w_ada, b_ada=b_ada, norm1_g=norm1_g, w_in=w_in, q_norm_g=q_norm_g, kv_norm_g=kv_norm_g,
             w_uq=w_uq, w_ukv=w_ukv, conv_w=conv_w, conv_b=conv_b, w_attn_out=w_attn_out, w_conv_out=w_conv_out, w_o=w_o,
             norm2_g=norm2_g, w_up=w_up, ffn_conv_w=ffn_conv_w, ffn_conv_b=ffn_conv_b, w_down=w_down, final_g=final_g)
    M = dict(c_ctx=m_c_ctx, w_ada=m_w_ada, b_ada=m_b_ada, norm1_g=m_norm1_g, w_in=m_w_in, q_norm_g=m_q_norm_g,
             kv_norm_g=m_kv_norm_g, w_uq=m_w_uq, w_ukv=m_w_ukv, conv_w=m_conv_w, conv_b=m_conv_b, w_attn_out=m_w_attn_out,
             w_conv_out=m_w_conv_out, w_o=m_w_o, norm2_g=m_norm2_g, w_up=m_w_up, ffn_conv_w=m_ffn_conv_w,
             ffn_conv_b=m_ffn_conv_b, w_down=m_w_down, final_g=m_final_g)
    V = dict(c_ctx=v_c_ctx, w_ada=v_w_ada, b_ada=v_b_ada, norm1_g=v_norm1_g, w_in=v_w_in, q_norm_g=v_q_norm_g,
             kv_norm_g=v_kv_norm_g, w_uq=v_w_uq, w_ukv=v_w_ukv, conv_w=v_conv_w, conv_b=v_conv_b, w_attn_out=v_w_attn_out,
             w_conv_out=v_w_conv_out, w_o=v_w_o, norm2_g=v_norm2_g, w_up=v_w_up, ffn_conv_w=v_ffn_conv_w,
             ffn_conv_b=v_ffn_conv_b, w_down=v_w_down, final_g=v_final_g)
    names = list(W)
    transposed = ("w_up", "w_uq")
    as2d = lambda k, a: (a.reshape(1, -1) if a.ndim == 1 else
                         a[0].T if k in transposed else a.reshape(a.shape[-2], a.shape[-1]))
    W2 = {k: as2d(k, a) for k, a in W.items()}
    M2 = {k: as2d(k, a) for k, a in M.items()}
    V2 = {k: as2d(k, a) for k, a in V.items()}
    unit3 = lambda a: jnp.transpose(a, (2, 0, 1))
    W3, M3, V3 = unit3(W["w_in"]), unit3(M["w_in"]), unit3(V["w_in"])
    nsh = W2["w_ada"].shape[1]

    unit_mid = ("conv_w", "ffn_conv_w")
    mid3 = lambda a: jnp.transpose(a, (1, 0, 2))
    s_all, mod_lat, mod_ctx, ffn_w_full, conv_w_full = ada_fwd(
        c, W2["c_ctx"], mid3(W["ffn_conv_w"]), mid3(W["conv_w"]), W2["w_ada"], W["b_ada"].reshape(NDEV, 1, nsh), [],
        name="ada_fwd")

    stage_w = {"in": ["w_in"], "mid": ["w_uq", "w_ukv", "w_attn_out", "w_conv_out", "w_o"], "up": ["w_up"],
               "down": ["w_down"]}
    two_level = ("in", "mid")
    ag, tok = {}, mod_lat
    for st, nms in stage_w.items():
        ag[st] = exchange_start([W2[nm].astype(BF) for nm in nms], per_peer=False, dep=tok, name="ag_start_" + st,
                                peers=FIRST_HOP if st in two_level else ALL_PEERS)
        tok = ag[st]["token"]

    def get_w(stage, after):
        lands = exchange_wait(ag[stage], after, name="ag_wait_" + stage)
        if stage in two_level:
            lands = relay_wait(relay_start(lands, name="ag_relay_" + stage), name="ag_relay_wait_" + stage)
        g = dict(zip(stage_w[stage], lands))
        if stage == "in":
            return build_win(g["w_in"], name="build_win")
        if stage == "mid":
            wq2, wkv2 = build_wq_wkv(g["w_uq"], g["w_ukv"], name="build_wq_wkv")
            return (wq2, wkv2, unshard_cols(g["w_attn_out"], name="unshard_w_attn_out"),
                    unshard_cols(g["w_conv_out"], name="unshard_w_conv_out"), g["w_o"].reshape(D, D))
        if stage == "up":
            return g["w_up"].reshape(2 * DFF, D)
        return g["w_down"].reshape(DFF, D)

    stage_g = {"ffn": ["w_up", "w_down"], "mid": ["w_attn_out", "w_conv_out", "w_o"], "qkv": ["w_uq", "w_ukv"],
               "in": ["w_in"]}
    rs = {}

    def put_g(stage, g):
        if stage in ("in0", "in1"):
            parts = [shard_win_grad(g["dwin"], g["dwin_c"], col0=g["col0"], name="shard_win_grad_" + stage[-1])]
        elif stage == "mid":
            parts = [shard_cols(g["dwao"], name="shard_w_attn_out"), shard_cols(g["dwco"], name="shard_w_conv_out"),
                     g["dwo"].reshape(NDEV, D // NDEV, D)]
        elif stage == "qkv":
            parts = list(shard_wq_wkv_grad(g["dwq2"], g["dwkv2"], name="shard_wq_wkv_grad"))
        else:
            parts = [g["dwup"].reshape(NDEV, 2 * DFF // NDEV, D), g["dwdn"].reshape(NDEV, DFF // NDEV, D)]
        rs[stage] = exchange_start(parts, per_peer=True, name="rs_start_" + stage)
        return rs[stage]["token"]

    r = _local_step(x[0], ctx[0], loss_target[0], mod_lat, mod_ctx, W2["norm1_g"], W2["q_norm_g"], W2["kv_norm_g"],
                    W2["norm2_g"], W2["final_g"], conv_w_full, W2["conv_b"], ffn_w_full, W2["ffn_conv_b"], get_w, put_g,
                    ag["down"]["token"])

    G, DL, NM, NV = {}, {}, {}, {}

    def finish(stage, after):
        if stage == "in":
            halves = []
            for h in ("in0", "in1"):
                halves += exchange_wait(rs[h], after, name="rs_wait_" + h)
                after = halves[-1]
            G["w_in"], DL["w_in"], NM["w_in"], NV["w_in"] = adamw_slots(W3, halves, M3, V3, name="adamw_w_in")
            return DL["w_in"]
        for nm, sl in zip(stage_g[stage], exchange_wait(rs[stage], after, name="rs_wait_" + stage)):
            G[nm], DL[nm], NM[nm], NV[nm] = adamw_slots(W2[nm], sl, M2[nm], V2[nm], name="adamw_" + nm)
            after = DL[nm]
        return after

    sync = exchange_start([pack_small(r, name="pack_small")], per_peer=False, name="sync_start")
    after = sync["token"]
    for st in ("ffn", "mid", "in", "qkv"):
        after = finish(st, after)
    a_buf, = exchange_wait(sync, [DL[nm] for nms in stage_g.values() for nm in nms], name="sync_wait")
    ssum, g_vec = sum_slots(a_buf, name="sum_small")
    G.update(g_vec)
    loss = ssum[P_LOSS, 0]
    G["conv_w"] = lax.dynamic_slice(ssum[P_CW:P_CW + 3, :CONV], (0, me * (CONV // NDEV)), (3, CONV // NDEV))
    fw_full = ssum[P_FW:P_FW + 6 * FROWS].reshape(3, 2, FROWS * D)[:, :, :DFF].reshape(3, 2 * DFF)
    G["ffn_conv_w"] = lax.dynamic_slice(fw_full, (0, me * (2 * DFF // NDEV)), (3, 2 * DFF // NDEV))

    G["w_ada"], gcc = ada_bwd(s_all, a_buf, ssum, W2["w_ada"], W2["c_ctx"], name="ada_bwd")
    G["c_ctx"] = gcc[0:1]

    DL["w_ada"], NM["w_ada"], NV["w_ada"] = adamw(W2["w_ada"], G["w_ada"], M2["w_ada"], V2["w_ada"], name="adamw_w_ada")
    small = ["c_ctx", "b_ada", "norm1_g", "q_norm_g", "kv_norm_g", "conv_b", "norm2_g", "ffn_conv_b", "final_g", "conv_w",
             "ffn_conv_w"]
    view = lambda k, a3, a2: mid3(a3[k]) if k in unit_mid else a2[k]
    for k in unit_mid:
        G[k] = G[k].reshape(3, 1, -1)
    ds, nms, nvs = adamw_many([view(k, W, W2) for k in small], [G[k] for k in small],
                              [view(k, M, M2) for k in small], [view(k, V, V2) for k in small], name="adamw_small")
    for k, nm in enumerate(small):
        DL[nm], NM[nm], NV[nm] = ds[k], nms[k], nvs[k]

    def as_output(nm, a):
        if nm in transposed:
            return a.T[None]
        if nm == "w_in":
            return jnp.transpose(a, (1, 2, 0))
        if nm in unit_mid and a.ndim == 3:
            return jnp.transpose(a, (1, 0, 2))
        return a.reshape(W[nm].shape)

    outs = [loss, r["dx"][None]]
    for grp in (G, DL, NM, NV):
        outs += [as_output(nm, grp[nm]) for nm in names]
    return tuple(outs)
```

```python
import functools
import numpy as np
import jax
import jax.numpy as jnp
from jax import lax
from jax.experimental import pallas as pl
from jax.experimental.pallas import tpu as pltpu

F32 = jnp.float32
BF = jnp.bfloat16
MESH = pl.DeviceIdType.MESH

D = 1024
T = 2048
TC = 256
TKV = T + TC
GRID_W = 64
NH = 8
DN = 64
DR = 32
DV = 64
QL = 384
KVL = 256
CONV = 512
DFF = 2816
EPS = 1e-6
ROPE_THETA = 10000.0
SCALE = (DN + DR) ** -0.5
NDEV = 8
HP = 128

O_GA, O_GC, O_KV, O_Q, O_CV = 0, 1024, 2048, 2560, 3072
NIN = 4608
RING = 3
CVB = 256
N_IN = 4256
SH_IN = N_IN // NDEV

LR, B1, B2, AEPS, WD, STEP = 0.001, 0.9, 0.999, 1e-08, 0.01, 10


def _pick(n, target, mult=128):
    best = None
    for d in range(mult, min(n, target) + 1, mult):
        if n % d == 0:
            best = d
    return best if best is not None else n


def _swap_start(g):
    return 8 * (g ^ 1)


def mm(a, b, *, ta=False, tb=False, out_dtype=F32, name, tm=1024, tn=1024, tk=2048, M=None, N=None, K=None,
       a_off=(0, 0), b_off=(0, 0), a_stack=False, b_stack=False, o_stack=False, dep=None, ring=False):
    def dims(arr, stack):
        return (arr.shape[1], 2 * arr.shape[2]) if stack else arr.shape

    ar, ac = dims(a, a_stack)
    br, bc = dims(b, b_stack)
    M = M or ((ac if ta else ar) - a_off[1 if ta else 0])
    K = K or ((ar if ta else ac) - a_off[0 if ta else 1])
    N = N or ((br if tb else bc) - b_off[0 if tb else 1])
    tm = _pick(M, tm, 128 if ta else 16)
    tn = _pick(N // 2 if (o_stack or (b_stack and not tb)) else N, tn, 128)
    tk = _pick(K // 2 if ((a_stack and not ta) or (b_stack and tb)) else K, tk, 128)
    nk = K // tk
    ca = 0 if ta else 1
    cb = 1 if tb else 0

    nj, steps = N // tn, (M // tm) * (N // tn)
    if ring:
        assert tb and nk == 1 and not b_stack and steps >= RING and b_off[0] % tn == 0 and b_off[1] % tk == 0

    def ring_copy(b_hbm, bbuf, sem, step):
        rows = pl.ds(pl.multiple_of((step % nj) * tn + b_off[0], 128), tn)
        return pltpu.make_async_copy(b_hbm.at[rows, pl.ds(b_off[1], tk)], bbuf.at[step % RING], sem.at[step % RING])

    def body(a_ref, b_ref, *rest):
        if ring:
            o_ref, acc, bbuf, sem = rest[-4:]
            s = pl.program_id(0) * nj + pl.program_id(1)

            @pl.when(s == 0)
            def _():
                for t in range(RING - 1):
                    ring_copy(b_ref, bbuf, sem, t).start()

            @pl.when(s + RING - 1 < steps)
            def _():
                ring_copy(b_ref, bbuf, sem, s + RING - 1).start()

            ring_copy(b_ref, bbuf, sem, s).wait()
            o_ref[...] = lax.dot_general(a_ref[...].astype(BF), bbuf[s % RING].astype(BF),
                                         (((ca,), (cb,)), ((), ())), preferred_element_type=F32).astype(o_ref.dtype)
            return
        o_ref, acc = rest[-2:]
        k = pl.program_id(2)
        part = lax.dot_general(a_ref[...].astype(BF), b_ref[...].astype(BF),
                               (((ca,), (cb,)), ((), ())), preferred_element_type=F32)
        if nk == 1:
            o_ref[...] = part.astype(o_ref.dtype)
        else:
            @pl.when(k == 0)
            def _():
                acc[...] = part

            @pl.when(k > 0)
            def _():
                acc[...] += part

            @pl.when(k == nk - 1)
            def _():
                o_ref[...] = acc[...].astype(o_ref.dtype)

    def spec(blk, rc, off, stack, ncols):
        assert off[0] % blk[0] == 0 and off[1] % blk[1] == 0, (name, blk, off)
        ro, co = off[0] // blk[0], off[1] // blk[1]
        if not stack:
            return pl.BlockSpec(blk, lambda i, j, k: (rc(i, j, k)[0] + ro, rc(i, j, k)[1] + co))
        nhb = ncols // 2 // blk[1]
        return pl.BlockSpec((None,) + blk,
                            lambda i, j, k: ((rc(i, j, k)[1] + co) // nhb, rc(i, j, k)[0] + ro, (rc(i, j, k)[1] + co) % nhb))

    a_spec = spec((tk, tm), lambda i, j, k: (k, i), a_off, a_stack, ac) if ta else \
        spec((tm, tk), lambda i, j, k: (i, k), a_off, a_stack, ac)
    b_spec = spec((tn, tk), lambda i, j, k: (j, k), b_off, b_stack, bc) if tb else \
        spec((tk, tn), lambda i, j, k: (k, j), b_off, b_stack, bc)
    o_spec = spec((tm, tn), lambda i, j, k: (i, j), (0, 0), o_stack, N)
    o_shape = (2, M, N // 2) if o_stack else (M, N)
    deps = [] if dep is None else [dep]
    scratch = [pltpu.VMEM((tm, tn) if nk > 1 else (8, 128), F32)]
    if ring:
        b_spec = pl.BlockSpec(memory_space=pl.ANY)
        scratch += [pltpu.VMEM((RING, tn, tk), b.dtype), pltpu.SemaphoreType.DMA((RING,))]
    sem = ("arbitrary",) * 3 if ring else ("parallel", "parallel", "arbitrary")
    return pl.pallas_call(
        body, name=name, grid=(M // tm, N // tn, nk),
        in_specs=[a_spec, b_spec] + [pl.BlockSpec(memory_space=pl.ANY)] * len(deps),
        out_specs=o_spec, out_shape=jax.ShapeDtypeStruct(o_shape, out_dtype), scratch_shapes=scratch,
        compiler_params=pltpu.CompilerParams(dimension_semantics=sem),
    )(a, b, *deps)


def _row(width):
    return pl.BlockSpec((1, width), lambda *_: (0, 0))


NLAT = T // TC


def normmod_cat(ctx, x, g, csc, csh, sc, sh, dep, *, name, tm=256):
    assert tm == TC

    def body(c_ref, x_ref, g_ref, csc_ref, csh_ref, sc_ref, sh_ref, dep_ref, h_ref):
        last = pl.program_id(0) == NLAT
        xv = jnp.where(last, c_ref[...], x_ref[...])
        scv = jnp.where(last, csc_ref[...], sc_ref[...])
        shv = jnp.where(last, csh_ref[...], sh_ref[...])
        r = lax.rsqrt(jnp.mean(xv * xv, axis=-1, keepdims=True) + EPS)
        h_ref[...] = ((xv * r * g_ref[...]) * (1.0 + scv) + shv).astype(BF)

    return pl.pallas_call(
        body, name=name, grid=(TKV // tm,),
        in_specs=[pl.BlockSpec((tm, D), lambda i: (0, 0)), pl.BlockSpec((tm, D), lambda i: (jnp.minimum(i, NLAT - 1), 0)),
                  _row(D), _row(D), _row(D), _row(D), _row(D), pl.BlockSpec(memory_space=pl.ANY)],
        out_specs=pl.BlockSpec((tm, D), lambda i: (i, 0)), out_shape=jax.ShapeDtypeStruct((TKV, D), BF),
        compiler_params=pltpu.CompilerParams(dimension_semantics=("parallel",)),
    )(ctx, x, g, csc, csh, sc, sh, dep)


def kvprep(pc, p, kvg, wkv2, ck, sk, *, name, tm=256):
    assert tm == TC
    nb = TKV // tm
    kvcol = O_KV // 512

    def body(pc_ref, p_ref, g_ref, w_ref, ck_ref, sk_ref, k_ref, v_ref, ckv_ref):
        i = pl.program_id(0)
        t = jnp.where(i == NLAT, pc_ref[...], p_ref[...])
        pk = t[:, :KVL]
        r = lax.rsqrt(jnp.mean(pk * pk, axis=-1, keepdims=True) + EPS)
        ckv = (pk * r * g_ref[...]).astype(BF)
        ckv_ref[...] = ckv
        kv2 = jnp.dot(ckv, w_ref[...], preferred_element_type=F32)
        krr = t[:, KVL:KVL + HP] * ck_ref[...] + t[:, KVL + HP:KVL + 2 * HP] * sk_ref[...]
        k_ref[...] = (kv2[:, :NH * HP] + jnp.concatenate([krr] * NH, axis=1)).astype(BF)
        v_ref[...] = kv2[:, NH * HP:].astype(BF)

    return pl.pallas_call(
        body, name=name, grid=(nb,),
        in_specs=[pl.BlockSpec((tm, 512), lambda i: (0, 0)),
                  pl.BlockSpec((tm, 512), lambda i: (jnp.minimum(i, NLAT - 1), kvcol)),
                  _row(KVL), pl.BlockSpec((KVL, NH * HP + NH * DV), lambda i: (0, 0)),
                  pl.BlockSpec((tm, HP), lambda i: (i, 0)), pl.BlockSpec((tm, HP), lambda i: (i, 0))],
        out_specs=[pl.BlockSpec((tm, NH * HP), lambda i: (i, 0)), pl.BlockSpec((tm, NH * DV), lambda i: (i, 0)),
                   pl.BlockSpec((tm, KVL), lambda i: (i, 0))],
        out_shape=[jax.ShapeDtypeStruct((TKV, NH * HP), BF), jax.ShapeDtypeStruct((TKV, NH * DV), BF),
                   jax.ShapeDtypeStruct((TKV, KVL), BF)],
        compiler_params=pltpu.CompilerParams(dimension_semantics=("parallel",)),
    )(pc, p, kvg, wkv2, ck, sk)


def qprep(p, qg, wq2, cq_t, sq_t, *, name, tm=256):
    qcol = O_Q // 512

    def body(p_ref, g_ref, w_ref, c_ref, s_ref, q_ref, cq_ref):
        pq = p_ref[...]
        r = lax.rsqrt(jnp.sum(pq * pq, axis=-1, keepdims=True) * (1.0 / QL) + EPS)
        cq = (pq * r * g_ref[...]).astype(BF)
        cq_ref[...] = cq
        q2 = jnp.dot(cq, w_ref[...], preferred_element_type=F32)
        cc = jnp.concatenate([c_ref[...]] * NH, axis=1)
        ss = jnp.concatenate([s_ref[...]] * NH, axis=1)
        q_ref[...] = (q2[:, :NH * HP] * cc + q2[:, NH * HP:] * ss).astype(BF)

    return pl.pallas_call(
        body, name=name, grid=(T // tm,),
        in_specs=[pl.BlockSpec((tm, 512), lambda i: (i, qcol)), _row(512),
                  pl.BlockSpec((512, 2 * NH * HP), lambda i: (0, 0)),
                  pl.BlockSpec((tm, HP), lambda i: (i, 0)), pl.BlockSpec((tm, HP), lambda i: (i, 0))],
        out_specs=[pl.BlockSpec((tm, NH * HP), lambda i: (i, 0)), pl.BlockSpec((tm, 512), lambda i: (i, 0))],
        out_shape=[jax.ShapeDtypeStruct((T, NH * HP), BF), jax.ShapeDtypeStruct((T, 512), BF)],
        compiler_params=pltpu.CompilerParams(dimension_semantics=("parallel",)),
    )(p, qg, wq2, cq_t, sq_t)


def _head_mask(h):
    lanes = lax.broadcasted_iota(jnp.int32, (1, 2 * DV), 1)
    return (lanes // DV) == (h % 2)


LOG2E = 1.4426950408889634


def attn_fwd(q, k, v, *, name, tq=1024, kc=768):
    def body(q_ref, k_ref, v_ref, o_ref, lse_ref):
        h = pl.program_id(1)
        qv = q_ref[...]
        m = l = acc = None
        for c in range(TKV // kc):
            s = lax.dot_general(qv, k_ref[c * kc:(c + 1) * kc, :], (((1,), (1,)), ((), ())),
                                preferred_element_type=F32) * (SCALE * LOG2E)
            mc = jnp.max(s, axis=-1, keepdims=True)
            if c == 0:
                m = mc
                e = jnp.exp2(s - m)
                l = jnp.sum(e, axis=-1, keepdims=True)
                acc = jnp.dot(e.astype(BF), v_ref[c * kc:(c + 1) * kc, :], preferred_element_type=F32)
            else:
                mn = jnp.maximum(m, mc)
                a = jnp.exp2(m - mn)
                e = jnp.exp2(s - mn)
                l = l * a + jnp.sum(e, axis=-1, keepdims=True)
                acc = acc * a + jnp.dot(e.astype(BF), v_ref[c * kc:(c + 1) * kc, :], preferred_element_type=F32)
                m = mn
        o2 = jnp.where(_head_mask(h), acc * (1.0 / l), 0.0).astype(BF)
        lse_ref[...] = jnp.broadcast_to(m + jnp.log(l) * LOG2E, (tq, HP))

        @pl.when(h % 2 == 0)
        def _():
            o_ref[...] = o2

        @pl.when(h % 2 == 1)
        def _():
            o_ref[...] = o_ref[...] + o2

    return pl.pallas_call(
        body, name=name, grid=(T // tq, NH),
        in_specs=[pl.BlockSpec((tq, HP), lambda i, h: (i, h)), pl.BlockSpec((TKV, HP), lambda i, h: (0, h)),
                  pl.BlockSpec((TKV, 2 * DV), lambda i, h: (0, h // 2))],
        out_specs=[pl.BlockSpec((tq, 2 * DV), lambda i, h: (i, h // 2)), pl.BlockSpec((tq, HP), lambda i, h: (i, h))],
        out_shape=[jax.ShapeDtypeStruct((T, NH * DV), BF), jax.ShapeDtypeStruct((T, NH * HP), F32)],
        compiler_params=pltpu.CompilerParams(dimension_semantics=("parallel", "arbitrary")),
    )(q, k, v)


def _shift_dn(x):
    n = x.shape[0]
    rows = lax.broadcasted_iota(jnp.int32, (n, 1), 0)
    return jnp.where(rows == 0, 0.0, pltpu.roll(x, 1, axis=0))


def _shift_up(x):
    n = x.shape[0]
    rows = lax.broadcasted_iota(jnp.int32, (n, 1), 0)
    return jnp.where(rows == n - 1, 0.0, pltpu.roll(x, n - 1, axis=0))


def _conv(x, w_ref, b_ref):
    return b_ref[...] + _shift_dn(x) * w_ref[0:1, :] + x * w_ref[1:2, :] + _shift_up(x) * w_ref[2:3, :]


def _conv_t(dy, w_ref):
    return _shift_up(dy) * w_ref[0:1, :] + dy * w_ref[1:2, :] + _shift_dn(dy) * w_ref[2:3, :]


def _conv_wgrad(dw_ref, dy, x):
    dw_ref[0:1, :] = jnp.sum(dy * _shift_dn(x), axis=0, keepdims=True)
    dw_ref[1:2, :] = jnp.sum(dy * x, axis=0, keepdims=True)
    dw_ref[2:3, :] = jnp.sum(dy * _shift_up(x), axis=0, keepdims=True)


def convz(p, cw, cb, *, name):
    o0 = O_CV // (3 * CVB)

    def body(p_ref, w_ref, bias_ref, z_ref):
        xv, bv, cv = p_ref[:, 0:CVB], p_ref[:, CVB:2 * CVB], p_ref[:, 2 * CVB:3 * CVB]
        z_ref[...] = (bv * _conv(cv * xv, w_ref, bias_ref)).astype(BF)

    return pl.pallas_call(
        body, name=name, grid=(CONV // CVB,),
        in_specs=[pl.BlockSpec((T, 3 * CVB), lambda j: (0, o0 + j)), pl.BlockSpec((3, CVB), lambda j: (0, j)),
                  pl.BlockSpec((1, CVB), lambda j: (0, j))],
        out_specs=pl.BlockSpec((T, CVB), lambda j: (0, j)),
        out_shape=jax.ShapeDtypeStruct((T, CONV), BF),
        compiler_params=pltpu.CompilerParams(dimension_semantics=("parallel",)),
    )(p, cw, cb)


def out_proj_merge(o, wao, z, wco, p, *, name, tm=512):
    kin = o.shape[1]

    def body(o_ref, wa_ref, z_ref, wc_ref, ga_ref, gc_ref, ya_ref, yc_ref, m_ref):
        ya = jnp.dot(o_ref[...], wa_ref[...], preferred_element_type=F32)
        yc = jnp.dot(z_ref[...], wc_ref[...], preferred_element_type=F32)
        ya_ref[...] = ya
        yc_ref[...] = yc
        m_ref[...] = (jax.nn.sigmoid(ga_ref[...]) * ya + jax.nn.sigmoid(gc_ref[...]) * yc).astype(BF)

    blk = pl.BlockSpec((tm, D), lambda i: (i, 0))
    act = pl.BlockSpec((tm, kin), lambda i: (i, 0))
    wsp = pl.BlockSpec((kin, D), lambda i: (0, 0))
    sh = jax.ShapeDtypeStruct((T, D), F32)
    return pl.pallas_call(
        body, name=name, grid=(T // tm,),
        in_specs=[act, wsp, act, wsp, pl.BlockSpec((tm, D), lambda i: (i, O_GA // D)),
                  pl.BlockSpec((tm, D), lambda i: (i, O_GC // D))],
        out_specs=[blk, blk, blk], out_shape=[sh, sh, jax.ShapeDtypeStruct((T, D), BF)],
        compiler_params=pltpu.CompilerParams(dimension_semantics=("parallel",)),
    )(o, wao, z, wco, p, p)


CONV_HALO = 8
CONV_ROWS = 256


def _row_chunks(n, chunk, carry):
    carry = chunk(0, True, False, carry)
    carry = lax.fori_loop(1, n // CONV_ROWS - 1, lambda c, a: chunk(c * CONV_ROWS, False, False, a), carry)
    return chunk(n - CONV_ROWS, False, True, carry)


def _ext_rows(ref, r0, first, last):
    n, w = ref.shape
    zero = jnp.zeros((CONV_HALO, w), ref.dtype)
    if first:
        return jnp.concatenate([zero, ref[0:CONV_ROWS + CONV_HALO, :]], axis=0)
    if last:
        return jnp.concatenate([ref[n - CONV_ROWS - CONV_HALO:n, :], zero], axis=0)
    return ref[pl.ds(pl.multiple_of(r0 - CONV_HALO, 8), CONV_ROWS + 2 * CONV_HALO), :]


def _center_rows(r0, first, last):
    return slice(r0, r0 + CONV_ROWS) if (first or last) else pl.ds(pl.multiple_of(r0, 8), CONV_ROWS)


def _roll_dn(x):
    return pltpu.roll(x, 1, axis=0)


def _roll_up(x):
    return pltpu.roll(x, x.shape[0] - 1, axis=0)


_CTR = slice(CONV_HALO, CONV_HALO + CONV_ROWS)


def ffn_act(u0, cw, cb, *, name, tc=256):
    nb = DFF // tc

    def body(u_ref, wg_ref, wv_ref, bg_ref, bv_ref, f_ref):
        wg = [wg_ref[k:k + 1, :] for k in range(3)]
        wv = [wv_ref[k:k + 1, :] for k in range(3)]
        bg, bv = bg_ref[...], bv_ref[...]

        def chunk(r0, first, last, carry):
            xg, xv = _ext_rows(u_ref.at[0], r0, first, last), _ext_rows(u_ref.at[1], r0, first, last)
            ug = bg + _roll_dn(xg) * wg[0] + xg * wg[1] + _roll_up(xg) * wg[2]
            uv = bv + _roll_dn(xv) * wv[0] + xv * wv[1] + _roll_up(xv) * wv[2]
            f_ref[_center_rows(r0, first, last), :] = (ug * jax.nn.sigmoid(ug) * uv)[_CTR].astype(BF)
            return carry

        _row_chunks(T, chunk, 0)

    return pl.pallas_call(
        body, name=name, grid=(nb,),
        in_specs=[pl.BlockSpec((2, T, tc), lambda j: (0, 0, j)),
                  pl.BlockSpec((3, tc), lambda j: (0, j)), pl.BlockSpec((3, tc), lambda j: (0, nb + j)),
                  pl.BlockSpec((1, tc), lambda j: (0, j)), pl.BlockSpec((1, tc), lambda j: (0, nb + j))],
        out_specs=pl.BlockSpec((T, tc), lambda j: (0, j)),
        out_shape=jax.ShapeDtypeStruct((T, DFF), BF),
        compiler_params=pltpu.CompilerParams(dimension_semantics=("parallel",)),
    )(u0, cw, cw, cb, cb)


def rows_call(lead, ins, in_specs, out_shape, out_specs, fn, *, name, R, tm):
    tb, a_stack, tk = lead.get("tb", False), lead.get("a_stack", False), lead["tk"]
    K = 2 * lead["a"].shape[2] if a_stack else lead["a"].shape[1]
    nk = K // tk
    deps = [] if lead.get("dep") is None else [lead["dep"]]
    n_in = len(ins)

    def body(a_ref, b_ref, *refs):
        refs = refs[len(deps):]
        in_refs, out_refs, acc = refs[:n_in], refs[n_in:-1], refs[-1]
        i, k = pl.program_id(0), pl.program_id(1)
        part = lax.dot_general(a_ref[...].astype(BF), b_ref[...].astype(BF),
                               (((1,), (1 if tb else 0,)), ((), ())), preferred_element_type=F32)
        if nk == 1:
            fn(i, part, in_refs, out_refs)
            return

        @pl.when(k == 0)
        def _():
            acc[...] = part

        @pl.when(k > 0)
        def _():
            acc[...] += part

        @pl.when(k == nk - 1)
        def _():
            fn(i, acc[...], in_refs, out_refs)

    if a_stack:
        nhb = K // 2 // tk
        a_spec = pl.BlockSpec((None, tm, tk), lambda i, k: (k // nhb, i, k % nhb))
    else:
        a_spec = pl.BlockSpec((tm, tk), lambda i, k: (i, k))
    b_spec = pl.BlockSpec((D, tk), lambda i, k: (0, k)) if tb else pl.BlockSpec((tk, D), lambda i, k: (k, 0))
    return pl.pallas_call(
        body, name=name, grid=(R // tm, nk),
        in_specs=[a_spec, b_spec] + [pl.BlockSpec(memory_space=pl.ANY)] * len(deps) + list(in_specs),
        out_specs=out_specs, out_shape=out_shape,
        scratch_shapes=[pltpu.VMEM((tm, D) if nk > 1 else (8, 128), F32)],
        compiler_params=pltpu.CompilerParams(dimension_semantics=("arbitrary", "arbitrary")),
    )(lead["a"], lead["b"], *deps, *ins)


def _rblk(tm, w=D, col=0):
    return pl.BlockSpec((tm, w), lambda i, k: (i, col))


def _rrow(w=D):
    return pl.BlockSpec((1, w), lambda i, k: (0, 0))


def down_final(f, wdn, x1, g2, fg, tgt, *, name, tm=512):
    def fn(i, d, in_refs, out_refs):
        x1_ref, g2_ref, fg_ref, t_ref = in_refs
        d_ref, dx_ref, dd_ref, dfg_ref, loss_ref = out_refs
        d_ref[...] = d
        xv = x1_ref[...] + g2_ref[...] * d
        r = lax.rsqrt(jnp.mean(xv * xv, axis=-1, keepdims=True) + EPS)
        xh = xv * r
        diff = xh * fg_ref[...] - t_ref[...]
        part = 0.5 * jnp.sum(jnp.mean(diff * diff, axis=-1, keepdims=True), axis=0, keepdims=True)
        dy = diff * (1.0 / D)
        a = dy * fg_ref[...]
        dx = r * (a - xh * jnp.mean(a * xh, axis=-1, keepdims=True))
        dx_ref[...] = dx
        dd_ref[...] = (dx * g2_ref[...]).astype(BF)
        dfg = jnp.sum(dy * xh, axis=0, keepdims=True)

        @pl.when(i == 0)
        def _():
            dfg_ref[...] = dfg
            loss_ref[...] = jnp.broadcast_to(part, (1, 128))

        @pl.when(i > 0)
        def _():
            dfg_ref[...] += dfg
            loss_ref[...] += jnp.broadcast_to(part, (1, 128))

    blk = _rblk(tm)
    return rows_call(
        dict(a=f, b=wdn, tk=DFF), [x1, g2, fg, tgt], [blk, _rrow(), _rrow(), blk],
        [jax.ShapeDtypeStruct((T, D), F32), jax.ShapeDtypeStruct((T, D), F32), jax.ShapeDtypeStruct((T, D), BF),
         jax.ShapeDtypeStruct((1, D), F32), jax.ShapeDtypeStruct((1, 128), F32)],
        [blk, blk, blk, _rrow(), _rrow(128)], fn, name=name, R=T, tm=tm)


def oproj_resid(merged, wo, x, gate, g, sc, sh, *, name, tm=512):
    def fn(i, a, in_refs, out_refs):
        x_ref, gate_ref, g_ref, sc_ref, sh_ref = in_refs
        a_ref, x1_ref, h_ref = out_refs
        a_ref[...] = a
        xv = x_ref[...] + gate_ref[...] * a
        x1_ref[...] = xv
        r = lax.rsqrt(jnp.mean(xv * xv, axis=-1, keepdims=True) + EPS)
        h_ref[...] = ((xv * r * g_ref[...]) * (1.0 + sc_ref[...]) + sh_ref[...]).astype(BF)

    blk = _rblk(tm)
    return rows_call(
        dict(a=merged, b=wo, tk=D), [x, gate, g, sc, sh], [blk, _rrow(), _rrow(), _rrow(), _rrow()],
        [jax.ShapeDtypeStruct((T, D), F32), jax.ShapeDtypeStruct((T, D), F32), jax.ShapeDtypeStruct((T, D), BF)],
        [blk, blk, blk], fn, name=name, R=T, tm=tm)


def oproj_dx_gate_bwd(da, wo, p, ya, yc, wao, wco, *, name, tm=512):
    kin = wao.shape[0]

    def fn(i, dm, in_refs, out_refs):
        ga_ref, gc_ref, ya_ref, yc_ref, wa_ref, wc_ref = in_refs
        dya_ref, dyc_ref, dp_ref, do_ref, dz_ref = out_refs
        sa, sc_ = jax.nn.sigmoid(ga_ref[...]), jax.nn.sigmoid(gc_ref[...])
        dya, dyc = (dm * sa).astype(BF), (dm * sc_).astype(BF)
        dya_ref[...] = dya
        dyc_ref[...] = dyc
        dp_ref[:, 0:D] = (dm * ya_ref[...] * (sa * (1.0 - sa))).astype(BF)
        dp_ref[:, D:2 * D] = (dm * yc_ref[...] * (sc_ * (1.0 - sc_))).astype(BF)
        nt = (((1,), (1,)), ((), ()))
        do_ref[...] = lax.dot_general(dya, wa_ref[...], nt, preferred_element_type=F32).astype(BF)
        dz_ref[...] = lax.dot_general(dyc, wc_ref[...], nt, preferred_element_type=F32)

    blk = _rblk(tm)
    sh = jax.ShapeDtypeStruct((T, D), BF)
    wsp = pl.BlockSpec((kin, D), lambda i, k: (0, 0))
    return rows_call(
        dict(a=da, b=wo, tb=True, tk=D), [p, p, ya, yc, wao, wco],
        [_rblk(tm, D, O_GA // D), _rblk(tm, D, O_GC // D), blk, blk, wsp, wsp],
        [sh, sh, jax.ShapeDtypeStruct((T, NIN), BF), jax.ShapeDtypeStruct((T, kin), BF), jax.ShapeDtypeStruct((T, kin), F32)],
        [blk, blk, _rblk(tm, 2 * D), _rblk(tm, kin), _rblk(tm, kin)], fn, name=name, R=T, tm=tm)


def normmod_bwd(x, dh, g, sc, dres, gsrc, gate, *, name, tm=512):
    R = x.shape[0]
    tm = min(tm, R)
    has_res = dres is not None
    fused = isinstance(dh, dict)
    if fused:
        tb, a_stack, tk = dh.get("tb", False), dh.get("a_stack", False), dh["tk"]
        K = 2 * dh["a"].shape[2] if a_stack else dh["a"].shape[1]
        nk = K // tk
        deps = [] if dh.get("dep") is None else [dh["dep"]]
        n_dh = 2 + len(deps)
    else:
        nk, n_dh = 1, 1

    def elementwise(i, dhv, x_ref, g_ref, sc_ref, res_refs, out_refs):
        xv = x_ref[...]
        r = lax.rsqrt(jnp.mean(xv * xv, axis=-1, keepdims=True) + EPS)
        xh = xv * r
        n = xh * g_ref[...]
        dn = dhv * (1.0 + sc_ref[...])
        a = dn * g_ref[...]
        rows = [jnp.sum(dhv, axis=0, keepdims=True), jnp.sum(dhv * n, axis=0, keepdims=True),
                jnp.sum(dn * xh, axis=0, keepdims=True)]
        if has_res:
            dres_ref, gsrc_ref, gate_ref = res_refs
            dx_ref, dxg_ref, st_ref = out_refs
            dr = dres_ref[...]
            dx = dr + r * (a - xh * jnp.mean(a * xh, axis=-1, keepdims=True))
            dx_ref[...] = dx
            dxg_ref[...] = (dx * gate_ref[...]).astype(BF)
            rows.append(jnp.sum(dr * gsrc_ref[...], axis=0, keepdims=True))
        else:
            st_ref, = out_refs
            rows.append(jnp.zeros((1, D), F32))

        @pl.when(i == 0)
        def _():
            for k, row in enumerate(rows):
                st_ref[k:k + 1, :] = row

        @pl.when(i > 0)
        def _():
            for k, row in enumerate(rows):
                st_ref[k:k + 1, :] += row

    def body(*refs):
        x_ref, dh_refs, g_ref, sc_ref = refs[0], refs[1:1 + n_dh], refs[1 + n_dh], refs[2 + n_dh]
        rest = refs[3 + n_dh:]
        res_refs, rest = (rest[:3], rest[3:]) if has_res else ((), rest)
        out_refs = rest[:3] if has_res else rest[:1]
        i = pl.program_id(0)
        if not fused:
            elementwise(i, dh_refs[0][...], x_ref, g_ref, sc_ref, res_refs, out_refs)
            return
        acc = rest[-1]
        k = pl.program_id(1)
        part = lax.dot_general(dh_refs[0][...].astype(BF), dh_refs[1][...].astype(BF),
                               (((1,), (1 if tb else 0,)), ((), ())), preferred_element_type=F32)
        if nk == 1:
            elementwise(i, part, x_ref, g_ref, sc_ref, res_refs, out_refs)
            return

        @pl.when(k == 0)
        def _():
            acc[...] = part

        @pl.when(k > 0)
        def _():
            acc[...] += part

        @pl.when(k == nk - 1)
        def _():
            elementwise(i, acc[...], x_ref, g_ref, sc_ref, res_refs, out_refs)

    rowb = lambda w: pl.BlockSpec((1, w), lambda i, *k: (0, 0))
    blk = pl.BlockSpec((tm, D), lambda i, *k: (i, 0))
    st_spec = pl.BlockSpec((4, D), lambda i, *k: (0, 0))
    st_shape = jax.ShapeDtypeStruct((4, D), F32)
    if fused:
        if a_stack:
            nhb = K // 2 // tk
            a_spec = pl.BlockSpec((None, tm, tk), lambda i, k: (k // nhb, i, k % nhb))
        else:
            a_spec = pl.BlockSpec((tm, tk), lambda i, k: (i, k))
        b_spec = pl.BlockSpec((D, tk), lambda i, k: (0, k)) if tb else pl.BlockSpec((tk, D), lambda i, k: (k, 0))
        dh_specs = [a_spec, b_spec] + [pl.BlockSpec(memory_space=pl.ANY)] * len(deps)
        dh_args = [dh["a"], dh["b"]] + deps
        grid, sem = (R // tm, nk), ("arbitrary", "arbitrary")
        scratch = [pltpu.VMEM((tm, D) if nk > 1 else (8, 128), F32)]
    else:
        dh_specs, dh_args, grid, sem, scratch = [blk], [dh], (R // tm,), ("arbitrary",), []
    cp = pltpu.CompilerParams(dimension_semantics=sem)
    if has_res:
        return pl.pallas_call(
            body, name=name, grid=grid, in_specs=[blk] + dh_specs + [rowb(D), rowb(D), blk, blk, rowb(D)],
            out_specs=[blk, blk, st_spec], scratch_shapes=scratch,
            out_shape=[jax.ShapeDtypeStruct((R, D), F32), jax.ShapeDtypeStruct((R, D), BF), st_shape],
            compiler_params=cp,
        )(x, *dh_args, g, sc, dres, gsrc, gate)
    return pl.pallas_call(
        body, name=name, grid=grid, in_specs=[blk] + dh_specs + [rowb(D), rowb(D)],
        out_specs=st_spec, out_shape=st_shape, scratch_shapes=scratch, compiler_params=cp,
    )(x, *dh_args, g, sc)


def ffn_act_bwd(u0, df, cw, cb, *, name, tc=128):
    nb = DFF // tc

    def body(u_ref, df_ref, wg_ref, wv_ref, bg_ref, bv_ref, du_ref, dw_ref, db_ref):
        wg = [wg_ref[k:k + 1, :] for k in range(3)]
        wv = [wv_ref[k:k + 1, :] for k in range(3)]
        bg, bv = bg_ref[...], bv_ref[...]

        def chunk(r0, first, last, acc):
            xg, xv = _ext_rows(u_ref.at[0], r0, first, last), _ext_rows(u_ref.at[1], r0, first, last)
            dfe = _ext_rows(df_ref, r0, first, last)
            xg_d, xg_u, xv_d, xv_u = _roll_dn(xg), _roll_up(xg), _roll_dn(xv), _roll_up(xv)
            ug = bg + xg_d * wg[0] + xg * wg[1] + xg_u * wg[2]
            uv = bv + xv_d * wv[0] + xv * wv[1] + xv_u * wv[2]
            sig = jax.nn.sigmoid(ug)
            dug = dfe * uv * (sig * (1.0 + ug * (1.0 - sig)))
            duv = dfe * (ug * sig)
            rows = _center_rows(r0, first, last)
            du_ref[0, rows, :] = (_roll_up(dug) * wg[0] + dug * wg[1] + _roll_dn(dug) * wg[2])[_CTR].astype(BF)
            du_ref[1, rows, :] = (_roll_up(duv) * wv[0] + duv * wv[1] + _roll_dn(duv) * wv[2])[_CTR].astype(BF)
            terms = [dug * xg_d, dug * xg, dug * xg_u, dug, duv * xv_d, duv * xv, duv * xv_u, duv]
            return tuple(a + jnp.sum(t[_CTR], axis=0, keepdims=True) for a, t in zip(acc, terms))

        acc = _row_chunks(T, chunk, tuple(jnp.zeros((1, tc), F32) for _ in range(8)))
        for k in range(3):
            dw_ref[0, k:k + 1, :] = acc[k]
            dw_ref[1, k:k + 1, :] = acc[4 + k]
        db_ref[0] = acc[3]
        db_ref[1] = acc[7]

    lo = lambda r: pl.BlockSpec((r, tc), lambda j: (0, j))
    hi = lambda r: pl.BlockSpec((r, tc), lambda j: (0, nb + j))
    st = lambda r: pl.BlockSpec((2, r, tc), lambda j: (0, 0, j))
    return pl.pallas_call(
        body, name=name, grid=(nb,),
        in_specs=[st(T), lo(T), lo(3), hi(3), lo(1), hi(1)],
        out_specs=[st(T), st(3), st(1)],
        out_shape=[jax.ShapeDtypeStruct((2, T, DFF), BF), jax.ShapeDtypeStruct((2, 3, DFF), F32),
                   jax.ShapeDtypeStruct((2, 1, DFF), F32)],
        compiler_params=pltpu.CompilerParams(dimension_semantics=("parallel",)),
    )(u0, df, cw, cw, cb, cb)


def convz_bwd(p, dz, cw, cb, dp, *, name):
    o0 = O_CV // (3 * CVB)

    def body(p_ref, dz_ref, w_ref, bias_ref, dp_in, dp_ref, dw_ref, dbias_ref):
        xv, bv, cv = p_ref[:, 0:CVB], p_ref[:, CVB:2 * CVB], p_ref[:, 2 * CVB:3 * CVB]
        ci = cv * xv
        dwc = _conv(ci, w_ref, bias_ref)
        dzv = dz_ref[...]
        ddw = dzv * bv
        dci = _conv_t(ddw, w_ref)
        dp_ref[:, 0:CVB] = (dci * cv).astype(BF)
        dp_ref[:, CVB:2 * CVB] = (dzv * dwc).astype(BF)
        dp_ref[:, 2 * CVB:3 * CVB] = (dci * xv).astype(BF)
        _conv_wgrad(dw_ref, ddw, ci)
        dbias_ref[...] = jnp.sum(ddw, axis=0, keepdims=True)

    own = lambda r: pl.BlockSpec((r, CVB), lambda j: (0, j))
    return pl.pallas_call(
        body, name=name, grid=(CONV // CVB,),
        in_specs=[pl.BlockSpec((T, 3 * CVB), lambda j: (0, o0 + j)), own(T), own(3), own(1),
                  pl.BlockSpec(memory_space=pl.ANY)],
        out_specs=[pl.BlockSpec((T, 3 * CVB), lambda j: (0, o0 + j)), own(3), own(1)],
        out_shape=[jax.ShapeDtypeStruct((T, NIN), BF), jax.ShapeDtypeStruct((3, CONV), F32),
                   jax.ShapeDtypeStruct((1, CONV), F32)],
        input_output_aliases={4: 0},
        compiler_params=pltpu.CompilerParams(dimension_semantics=("parallel",)),
    )(p, dz, cw, cb, dp)


def attn_bwd(q, k, v, do, o, lse, dep, *, name, tq=1024, kc=768):
    NKC, KC = TKV // kc, kc
    deps = [] if dep is None else [dep]

    def body(q_ref, k_ref, v_ref, do_ref, o_ref, lse_ref, *rest):
        dq_ref, dk_ref, dv_ref = rest[len(deps):]
        h, i = pl.program_id(0), pl.program_id(1)

        @pl.when(i == 0)
        def _():
            dk_ref[...] = jnp.zeros_like(dk_ref)

        @pl.when((i == 0) & (h % 2 == 0))
        def _():
            dv_ref[...] = jnp.zeros_like(dv_ref)

        qv = q_ref[...]
        dom = jnp.where(_head_mask(h), do_ref[...], jnp.zeros_like(do_ref[...]))
        delta = jnp.sum(dom.astype(F32) * o_ref[...].astype(F32), axis=-1, keepdims=True)
        lse = lse_ref[:, 0:1]
        dq = jnp.zeros((tq, HP), F32)
        for c in range(NKC):
            cols = slice(c * KC, (c + 1) * KC)
            s = lax.dot_general(qv, k_ref[cols, :], (((1,), (1,)), ((), ())),
                                preferred_element_type=F32) * (SCALE * LOG2E)
            pr = jnp.exp2(s - lse)
            dp = lax.dot_general(dom, v_ref[cols, :], (((1,), (1,)), ((), ())), preferred_element_type=F32)
            ds = (pr * (dp - delta) * SCALE).astype(BF)
            dq = dq + jnp.dot(ds, k_ref[cols, :], preferred_element_type=F32)
            dk_ref[cols, :] += lax.dot_general(ds, qv, (((0,), (0,)), ((), ())), preferred_element_type=F32)
            dv_ref[cols, :] += lax.dot_general(pr.astype(BF), dom, (((0,), (0,)), ((), ())), preferred_element_type=F32)
        dq_ref[...] = dq

    return pl.pallas_call(
        body, name=name, grid=(NH, T // tq),
        in_specs=[pl.BlockSpec((tq, HP), lambda h, i: (i, h)), pl.BlockSpec((TKV, HP), lambda h, i: (0, h)),
                  pl.BlockSpec((TKV, 2 * DV), lambda h, i: (0, h // 2)), pl.BlockSpec((tq, 2 * DV), lambda h, i: (i, h // 2)),
                  pl.BlockSpec((tq, 2 * DV), lambda h, i: (i, h // 2)), pl.BlockSpec((tq, HP), lambda h, i: (i, h)),
                  *([pl.BlockSpec(memory_space=pl.ANY)] * len(deps))],
        out_specs=[pl.BlockSpec((tq, HP), lambda h, i: (i, h)), pl.BlockSpec((TKV, HP), lambda h, i: (0, h)),
                   pl.BlockSpec((TKV, 2 * DV), lambda h, i: (0, h // 2))],
        out_shape=[jax.ShapeDtypeStruct((T, NH * HP), F32), jax.ShapeDtypeStruct((TKV, NH * HP), F32),
                   jax.ShapeDtypeStruct((TKV, NH * DV), F32)],
        compiler_params=pltpu.CompilerParams(dimension_semantics=("arbitrary", "arbitrary")),
    )(q, k, v, do, o, lse, *deps)


def qprep_bwd(p, dq, qg, wq2, cq_t, sq_t, dp, *, name, tm=256):
    qcol = O_Q // 512

    def body(p_ref, dq_ref, g_ref, w_ref, c_ref, s_ref, dp_in, dp_ref, dq2_ref, dg_ref):
        i = pl.program_id(0)
        dqv = dq_ref[...]
        cc = jnp.concatenate([c_ref[...]] * NH, axis=1)
        ss = jnp.concatenate([s_ref[...]] * NH, axis=1)
        dq2 = jnp.concatenate([dqv * cc, dqv * ss], axis=1).astype(BF)
        dq2_ref[...] = dq2
        dcq = lax.dot_general(dq2, w_ref[...], (((1,), (1,)), ((), ())), preferred_element_type=F32)
        pq = p_ref[...]
        r = lax.rsqrt(jnp.sum(pq * pq, axis=-1, keepdims=True) * (1.0 / QL) + EPS)
        xh = pq * r
        a = dcq * g_ref[...]
        dp_ref[...] = (r * (a - xh * (jnp.sum(a * xh, axis=-1, keepdims=True) * (1.0 / QL)))).astype(BF)
        dg = jnp.sum(dcq * xh, axis=0, keepdims=True)

        @pl.when(i == 0)
        def _():
            dg_ref[...] = dg

        @pl.when(i > 0)
        def _():
            dg_ref[...] += dg

    return pl.pallas_call(
        body, name=name, grid=(T // tm,),
        in_specs=[pl.BlockSpec((tm, 512), lambda i: (i, qcol)), pl.BlockSpec((tm, NH * HP), lambda i: (i, 0)), _row(512),
                  pl.BlockSpec((512, 2 * NH * HP), lambda i: (0, 0)),
                  pl.BlockSpec((tm, HP), lambda i: (i, 0)), pl.BlockSpec((tm, HP), lambda i: (i, 0)),
                  pl.BlockSpec(memory_space=pl.ANY)],
        out_specs=[pl.BlockSpec((tm, 512), lambda i: (i, qcol)), pl.BlockSpec((tm, 2 * NH * HP), lambda i: (i, 0)), _row(512)],
        out_shape=[jax.ShapeDtypeStruct((T, NIN), BF), jax.ShapeDtypeStruct((T, 2 * NH * HP), BF),
                   jax.ShapeDtypeStruct((1, 512), F32)],
        input_output_aliases={6: 0},
        compiler_params=pltpu.CompilerParams(dimension_semantics=("arbitrary",)),
    )(p, dq, qg, wq2, cq_t, sq_t, dp)


def kvprep_bwd(pc, p, dk, dv, kvg, wkv2, ck, sk, dp, *, name, tm=256):
    assert tm == TC
    nb = TKV // tm
    kvcol = O_KV // 512

    def body(pc_ref, p_ref, dk_ref, dv_ref, g_ref, w_ref, ck_ref, sk_ref, dp_in, dp_ref, dpc_ref, dkv2_ref, dg_ref):
        i = pl.program_id(0)
        t = jnp.where(i == NLAT, pc_ref[...], p_ref[...])
        pk = t[:, :KVL]
        r = lax.rsqrt(jnp.mean(pk * pk, axis=-1, keepdims=True) + EPS)
        xh = pk * r
        dkv = dk_ref[...]
        dkv2 = jnp.concatenate([dkv, dv_ref[...]], axis=1).astype(BF)
        dkv2_ref[...] = dkv2
        dckv = lax.dot_general(dkv2, w_ref[...], (((1,), (1,)), ((), ())), preferred_element_type=F32)
        a = dckv * g_ref[...]
        dpk = r * (a - xh * jnp.mean(a * xh, axis=-1, keepdims=True))
        dkr = dkv[:, 0:HP]
        for hh in range(1, NH):
            dkr = dkr + dkv[:, hh * HP:(hh + 1) * HP]
        res = jnp.concatenate([dpk, dkr * ck_ref[...], dkr * sk_ref[...]], axis=1).astype(BF)
        dg = jnp.sum(dckv * xh, axis=0, keepdims=True)

        @pl.when(i == 0)
        def _():
            dg_ref[...] = dg

        @pl.when(i > 0)
        def _():
            dg_ref[...] += dg

        @pl.when(i < NLAT)
        def _():
            dp_ref[...] = res

        @pl.when(i == NLAT)
        def _():
            dpc_ref[...] = res

    rb = lambda w: pl.BlockSpec((tm, w), lambda i: (i, 0))
    return pl.pallas_call(
        body, name=name, grid=(nb,),
        in_specs=[pl.BlockSpec((tm, 512), lambda i: (0, 0)),
                  pl.BlockSpec((tm, 512), lambda i: (jnp.minimum(i, NLAT - 1), kvcol)),
                  rb(NH * HP), rb(NH * DV), _row(KVL), pl.BlockSpec((KVL, NH * HP + NH * DV), lambda i: (0, 0)),
                  rb(HP), rb(HP), pl.BlockSpec(memory_space=pl.ANY)],
        out_specs=[pl.BlockSpec((tm, 512), lambda i: (jnp.minimum(i, NLAT - 1), kvcol)),
                   pl.BlockSpec((tm, 512), lambda i: (0, 0)), rb(NH * HP + NH * DV), _row(KVL)],
        out_shape=[jax.ShapeDtypeStruct((T, NIN), BF), jax.ShapeDtypeStruct((TC, 512), BF),
                   jax.ShapeDtypeStruct((TKV, NH * HP + NH * DV), BF), jax.ShapeDtypeStruct((1, KVL), F32)],
        input_output_aliases={8: 0},
        compiler_params=pltpu.CompilerParams(dimension_semantics=("arbitrary",)),
    )(pc, p, dk, dv, kvg, wkv2, ck, sk, dp)


def _pieces(src, width, n):
    out, c = [], src
    while c < src + width:
        k = c // n
        w = min(src + width, (k + 1) * n) - c
        out.append((k, c - k * n, c - src, w))
        c += w
    return out


def _win_moves():
    mv = [(2208, 1024, O_GA), (3232, 1024, O_GC), (0, KVL, O_KV), (256, DR, O_KV + KVL + DN), (288, QL, O_Q)]
    mv += [(256 + _swap_start(g), 8, O_KV + KVL + HP + DN + 8 * g) for g in range(4)]
    for j in range(CONV // CVB):
        base = O_CV + 3 * CVB * j
        mv += [(672 + CVB * j, CVB, base), (1184 + CVB * j, CVB, base + CVB), (1696 + CVB * j, CVB, base + 2 * CVB)]
    return mv


_WIN_ZERO = [(O_KV + KVL, DN), (O_KV + KVL + DN + DR, HP - DN - DR), (O_KV + KVL + HP, DN),
             (O_KV + KVL + HP + DN + DR, HP - DN - DR), (O_Q + QL, 512 - QL)]


def build_win(g, *, name, tm=256):
    def body(g_ref, o_ref):
        for src, w, dst in _win_moves():
            for k, a, off, pw in _pieces(src, w, SH_IN):
                o_ref[:, dst + off:dst + off + pw] = g_ref[k, :, a:a + pw]
        for c0, w in _WIN_ZERO:
            o_ref[:, c0:c0 + w] = jnp.zeros((tm, w), o_ref.dtype)

    return pl.pallas_call(
        body, name=name, grid=(D // tm,), in_specs=[pl.BlockSpec((NDEV, tm, SH_IN), lambda i: (0, i, 0))],
        out_specs=pl.BlockSpec((tm, NIN), lambda i: (i, 0)), out_shape=jax.ShapeDtypeStruct((D, NIN), g.dtype),
        compiler_params=pltpu.CompilerParams(dimension_semantics=("parallel",)),
    )(g)


def shard_win_grad(dwt, dwct, *, name, col0=0, tc=256):
    n = dwt.shape[1]

    def body(dw_ref, dwc_ref, o_ref, kvs):
        kvs[...] = dw_ref[O_KV:O_KV + 512, :] + dwc_ref[...]

        def src(row, w):
            if O_KV <= row < O_KV + 512:
                return kvs[row - O_KV:row - O_KV + w, :]
            return dw_ref[row:row + w, :]

        for s, w, dst in _win_moves():
            if w == 8 or s == 256:
                continue
            for k, a, off, pw in _pieces(s, w, SH_IN):
                o_ref[k, a:a + pw, :] = src(dst + off, pw).astype(o_ref.dtype)
        for g in range(4):
            val = src(O_KV + KVL + DN + 8 * g, 8) + src(O_KV + KVL + HP + DN + _swap_start(g), 8)
            o_ref[0, 256 + 8 * g:256 + 8 * g + 8, :] = val.astype(o_ref.dtype)

    return pl.pallas_call(
        body, name=name, grid=(n // tc,),
        in_specs=[pl.BlockSpec((NIN, tc), lambda j: (0, j)), pl.BlockSpec((512, tc), lambda j: (0, j + col0 // tc))],
        out_specs=pl.BlockSpec((NDEV, SH_IN, tc), lambda j: (0, 0, j)),
        out_shape=jax.ShapeDtypeStruct((NDEV, SH_IN, n), BF),
        scratch_shapes=[pltpu.VMEM((512, tc), F32)],
        compiler_params=pltpu.CompilerParams(dimension_semantics=("parallel",)),
    )(dwt, dwct)


def _eye(n, m):
    return (lax.broadcasted_iota(jnp.int32, (n, m), 0) == lax.broadcasted_iota(jnp.int32, (n, m), 1)).astype(BF)


_NT = (((1,), (1,)), ((), ()))


def build_wq_wkv(gq, gkv, *, name):
    def body(gq_ref, gkv_ref, q_ref, kv_ref):
        q_ref[...] = jnp.zeros_like(q_ref)
        kv_ref[...] = jnp.zeros_like(kv_ref)
        eye = _eye(QL, QL)
        for h in range(NH):
            qh = lax.dot_general(eye, gq_ref[h], _NT, preferred_element_type=F32).astype(q_ref.dtype)
            q_ref[0:QL, h * HP:h * HP + DN + DR] = qh
            for g in range(4):
                c0 = NH * HP + h * HP + DN + 8 * g
                q_ref[0:QL, c0:c0 + 8] = qh[:, DN + _swap_start(g):DN + _swap_start(g) + 8]
            kv_ref[:, h * HP:h * HP + DN] = gkv_ref[h, :, 0:DN]
            kv_ref[:, NH * HP + h * DV:NH * HP + (h + 1) * DV] = gkv_ref[h, :, DN:DN + DV]

    vm = pl.BlockSpec(memory_space=pltpu.VMEM)
    return pl.pallas_call(
        body, name=name, in_specs=[vm, vm], out_specs=[vm, vm],
        out_shape=[jax.ShapeDtypeStruct((512, 2 * NH * HP), gq.dtype), jax.ShapeDtypeStruct((KVL, NH * HP + NH * DV), gq.dtype)],
    )(gq, gkv)


def shard_wq_wkv_grad(dwq2, dwkv2, *, name):
    def body(q_ref, kv_ref, gq_ref, gkv_ref, xs):
        xs[...] = jnp.zeros_like(xs)
        eye = _eye(DN + DR, HP)
        for h in range(NH):
            xs[:, 0:DN] = q_ref[0:QL, h * HP:h * HP + DN].astype(BF)
            for g in range(4):
                a = q_ref[0:QL, h * HP + DN + 8 * g:h * HP + DN + 8 * g + 8]
                c0 = NH * HP + h * HP + DN + _swap_start(g)
                xs[:, DN + 8 * g:DN + 8 * g + 8] = (a + q_ref[0:QL, c0:c0 + 8]).astype(BF)
            gq_ref[h] = lax.dot_general(eye, xs[...], _NT, preferred_element_type=F32).astype(BF)
            gkv_ref[h, :, 0:DN] = kv_ref[:, h * HP:h * HP + DN].astype(BF)
            gkv_ref[h, :, DN:DN + DV] = kv_ref[:, NH * HP + h * DV:NH * HP + (h + 1) * DV].astype(BF)

    vm = pl.BlockSpec(memory_space=pltpu.VMEM)
    return pl.pallas_call(
        body, name=name, in_specs=[vm, vm], out_specs=[vm, vm],
        out_shape=[jax.ShapeDtypeStruct((NDEV, DN + DR, QL), BF), jax.ShapeDtypeStruct((NDEV, KVL, DN + DV), BF)],
        scratch_shapes=[pltpu.VMEM((QL, HP), BF)],
    )(dwq2, dwkv2)


def unshard_cols(g, *, name, tm=256):
    _, K, n = g.shape
    tm = _pick(K, tm, 16)

    def body(g_ref, o_ref):
        for k in range(NDEV):
            o_ref[:, k * n:(k + 1) * n] = g_ref[k]

    return pl.pallas_call(
        body, name=name, grid=(K // tm,), in_specs=[pl.BlockSpec((NDEV, tm, n), lambda i: (0, i, 0))],
        out_specs=pl.BlockSpec((tm, NDEV * n), lambda i: (i, 0)), out_shape=jax.ShapeDtypeStruct((K, NDEV * n), g.dtype),
        compiler_params=pltpu.CompilerParams(dimension_semantics=("parallel",)),
    )(g)


def shard_cols(w, *, name, tm=256):
    K, n8 = w.shape
    n = n8 // NDEV
    tm = _pick(K, tm, 16)

    def body(w_ref, o_ref):
        for k in range(NDEV):
            o_ref[k] = w_ref[:, k * n:(k + 1) * n]

    return pl.pallas_call(
        body, name=name, grid=(K // tm,), in_specs=[pl.BlockSpec((tm, n8), lambda i: (i, 0))],
        out_specs=pl.BlockSpec((NDEV, tm, n), lambda i: (0, i, 0)), out_shape=jax.ShapeDtypeStruct((NDEV, K, n), w.dtype),
        compiler_params=pltpu.CompilerParams(dimension_semantics=("parallel",)),
    )(w)


def _rope_tables():
    t = np.arange(T)
    row = (t // GRID_W).astype(np.float32)
    col = (t % GRID_W).astype(np.float32)
    axis_dim = DR // 2
    inv = (np.float32(ROPE_THETA) ** (-np.arange(0, axis_dim, 2, dtype=np.float32) / np.float32(axis_dim))).astype(np.float32)
    ar, ac = (row[:, None] * inv).astype(np.float32), (col[:, None] * inv).astype(np.float32)
    cosv = np.concatenate([np.cos(ar), np.cos(ar), np.cos(ac), np.cos(ac)], axis=1).astype(np.float32)
    sinv = np.concatenate([-np.sin(ar), np.sin(ar), -np.sin(ac), np.sin(ac)], axis=1).astype(np.float32)
    ck = np.zeros((TKV, HP), np.float32)
    sk = np.zeros((TKV, HP), np.float32)
    ck[T:, DN:DN + DR] = 1.0
    ck[:T, DN:DN + DR] = cosv
    sk[:T, DN:DN + DR] = sinv
    cq = np.zeros((T, HP), np.float32)
    cq[:, :DN] = 1.0
    cq[:, DN:DN + DR] = cosv
    return jnp.asarray(ck), jnp.asarray(sk), jnp.asarray(cq), jnp.asarray(sk[:T])


def _local_step(x, ctx, tgt, mod_lat, mod_ctx, n1g, qg, kvg, n2g, fg, conv_w, conv_b, ffn_w, ffn_b, get_w, put_g, dep0):
    sh1, sc1, g1, sh2, sc2, g2 = [mod_lat[:, i * D:(i + 1) * D] for i in range(6)]
    csh1, csc1 = mod_ctx[:, 0:D], mod_ctx[:, D:2 * D]
    ck, sk, cq_t, sq_t = _rope_tables()
    qg_p = jnp.pad(qg, ((0, 0), (0, 512 - QL)))

    hcat = normmod_cat(ctx, x, n1g, csc1, csh1, sc1, sh1, dep0, name="normmod1")
    win = get_w("in", hcat)
    p = mm(hcat, win, M=T, tn=768, name="in_proj")
    pc = mm(hcat, win, M=TC, N=512, a_off=(T, 0), b_off=(0, O_KV), name="in_proj_ctx")
    wq2, wkv2, wao, wco, wo = get_w("mid", p)
    kh, vh, ckv = kvprep(pc, p, kvg, wkv2, ck, sk, name="kvprep")
    qr, cq = qprep(p, qg_p, wq2, cq_t, sq_t, name="qprep")
    o, lse = attn_fwd(qr, kh, vh, name="attn_fwd")
    z = convz(p, conv_w, conv_b, name="convz")
    ya, yc, merged = out_proj_merge(o, wao, z, wco, p, name="attn_conv_out_gate_merge")
    a_out, x1, h2 = oproj_resid(merged, wo, x, g1, n2g, sc2, sh2, name="o_proj_resid_normmod2")
    wup = get_w("up", h2)
    u0 = mm(h2, wup, tb=True, o_stack=True, tn=1408, ring=True, name="up_proj")
    f = ffn_act(u0, ffn_w, ffn_b, name="ffn_act")
    wdn = get_w("down", f)
    dn, dx2, dd, dfg, loss = down_final(f, wdn, x1, g2, fg, tgt, name="down_proj_final_loss")

    df = mm(dd, wdn, tb=True, tn=1408, ring=True, name="down_proj_dx")
    dwdn = mm(f, dd, ta=True, out_dtype=BF, tm=1408, name="down_proj_dw")
    du0, dffn_w, dffn_b = ffn_act_bwd(u0, df, ffn_w, ffn_b, name="ffn_act_bwd")
    dwup = mm(du0, h2, ta=True, a_stack=True, out_dtype=BF, tm=1408, name="up_proj_dw")
    tok = put_g("ffn", dict(dwup=dwup, dwdn=dwdn))
    dx1, da, st2 = normmod_bwd(x1, dict(a=du0, b=wup, a_stack=True, tk=DFF, dep=tok), n2g, sc2, dx2, dn, g1,
                               name="up_proj_dx_normmod2_bwd")

    dwo = mm(merged, da, ta=True, out_dtype=BF, tn=512, name="o_proj_dw")
    dya, dyc, dp, do, dz = oproj_dx_gate_bwd(da, wo, p, ya, yc, wao, wco, name="o_proj_dx_gate_merge_bwd")
    dwao = mm(o, dya, ta=True, out_dtype=BF, tn=512, name="attn_out_dw")
    dwco = mm(z, dyc, ta=True, out_dtype=BF, tn=512, name="conv_out_dw")
    tok = put_g("mid", dict(dwao=dwao, dwco=dwco, dwo=dwo))
    dp, dconv_w, dconv_b = convz_bwd(p, dz, conv_w, conv_b, dp, name="convz_bwd")
    dq, dk, dv = attn_bwd(qr, kh, vh, do, o, lse, tok, name="attn_bwd")
    dp, dq2, dqg = qprep_bwd(p, dq, qg_p, wq2, cq_t, sq_t, dp, name="qprep_bwd")
    dp, dpc, dkv2, dkvg = kvprep_bwd(pc, p, dk, dv, kvg, wkv2, ck, sk, dp, name="kvprep_bwd")

    dwin_c = mm(dpc, hcat, ta=True, K=TC, b_off=(T, 0), name="in_proj_ctx_dw")
    tok = None
    for half in range(2):
        dwin = mm(dp, hcat, ta=True, K=T, N=D // 2, b_off=(0, half * (D // 2)), tm=768, dep=tok,
                  name=f"in_proj_dw_{half}")
        tok = put_g(f"in{half}", dict(dwin=dwin, dwin_c=dwin_c, col0=half * (D // 2)))
    dwq2 = mm(cq, dq2, ta=True, dep=tok, name="q_up_dw")
    dwkv2 = mm(ckv, dkv2, ta=True, dep=tok, name="kv_up_dw")
    tok = put_g("qkv", dict(dwq2=dwq2, dwkv2=dwkv2))
    dhc = mm(dpc, win, tb=True, N=D, K=512, b_off=(0, O_KV), dep=tok, name="in_proj_ctx_dx")
    dx, _, st1 = normmod_bwd(x, dict(a=dp, b=win, tb=True, tk=NIN, dep=tok), n1g, sc1, dx1, a_out, g1,
                             name="in_proj_dx_normmod1_bwd")
    stc = normmod_bwd(ctx, dhc, n1g, csc1, None, None, None, name="normmod1_ctx_bwd")

    return dict(loss=loss, dx=dx, st1=st1, st2=st2, stc=stc, dqg=dqg, dkvg=dkvg, dfg=dfg,
                dconv_w=dconv_w, dconv_b=dconv_b, dffn_w=dffn_w, dffn_b=dffn_b)


def _me():
    x, y, c = lax.axis_index("x"), lax.axis_index("y"), lax.axis_index("c")
    return x, y, c, 4 * x + 2 * y + c


def _peer(x, y, c, k):
    px = 1 - x if k & 4 else x
    py = 1 - y if k & 2 else y
    pc = 1 - c if k & 1 else c
    return (px, py, pc), 4 * px + 2 * py + pc


def _exchange_tiles(src_of_peer, buf, send_sem, recv_sem):
    x, y, c, me = _me()
    for k in range(1, NDEV):
        dev, lin = _peer(x, y, c, k)
        pltpu.make_async_remote_copy(src_ref=src_of_peer(lin), dst_ref=buf.at[me], send_sem=send_sem, recv_sem=recv_sem,
                                     device_id=dev, device_id_type=MESH).start()
    seven = buf.at[pl.ds(0, NDEV - 1)]
    pltpu.make_async_remote_copy(src_ref=seven, dst_ref=seven, send_sem=send_sem, recv_sem=recv_sem,
                                 device_id=(x, y, c), device_id_type=MESH).wait()


def _silu(z):
    return z * jax.nn.sigmoid(z)


def ada_fwd(c, c_ctx, ffn_w, conv_w, w_shard, b_ada, deps, *, name):
    nsh, nf, nc = w_shard.shape[1], ffn_w.shape[2], conv_w.shape[2]
    deps = [d for d in deps if d is not None]

    def body(c_ref, cc_ref, fw_ref, cw_ref, w_ref, b_ref, *rest):
        s_ref, ml_ref, mc_ref, fwf_ref, cwf_ref, m_ref, mine, res, sems = rest[len(deps):]
        x, y, c, me = _me()
        mine[...] = jnp.zeros_like(mine)
        mine[0:1, :] = _silu(c_ref[...])
        mine[1:2, :] = _silu(cc_ref[...])
        for k in range(3):
            mine[2 + k:3 + k, 0:fw_ref.shape[2]] = fw_ref[k]
            mine[5 + k:6 + k, 0:cw_ref.shape[2]] = cw_ref[k]
        s_ref[me] = mine[...]
        _exchange_tiles(lambda lin: mine, s_ref, sems.at[0], sems.at[1])
        sall = s_ref[...].reshape(NDEV * 8, D).astype(BF)
        r = jnp.dot(sall, w_ref[...].astype(BF), preferred_element_type=F32) + b_ref[me]
        res[...] = r.reshape(NDEV, 8, nsh)
        m_ref[me] = res[me]
        _exchange_tiles(lambda lin: res.at[lin], m_ref, sems.at[2], sems.at[3])
        for j in range(NDEV):
            ml_ref[:, j * nsh:(j + 1) * nsh] = m_ref[j, 0:1, :]
            mc_ref[:, j * nsh:(j + 1) * nsh] = m_ref[j, 1:2, :]
            fwf_ref[:, j * nf:(j + 1) * nf] = s_ref[j, 2:5, 0:nf]
            cwf_ref[:, j * nc:(j + 1) * nc] = s_ref[j, 5:8, 0:nc]

    vm = pl.BlockSpec(memory_space=pltpu.VMEM)
    return pl.pallas_call(
        body, name=name, in_specs=[vm] * 6 + [pl.BlockSpec(memory_space=pl.ANY)] * len(deps), out_specs=[vm] * 5,
        out_shape=[jax.ShapeDtypeStruct((NDEV, 8, D), F32), jax.ShapeDtypeStruct((1, NDEV * nsh), F32),
                   jax.ShapeDtypeStruct((1, NDEV * nsh), F32), jax.ShapeDtypeStruct((3, NDEV * nf), F32),
                   jax.ShapeDtypeStruct((3, NDEV * nc), F32)],
        scratch_shapes=[pltpu.VMEM((NDEV, 8, nsh), F32), pltpu.VMEM((8, D), F32), pltpu.VMEM((NDEV, 8, nsh), F32),
                        pltpu.SemaphoreType.DMA((4,))],
    )(c, c_ctx, ffn_w, conv_w, w_shard, b_ada, *deps)


P_DML, P_DMC, P_N1, P_QG, P_KVG, P_CB, P_N2, P_FB, P_FG, P_CW, P_FW, P_LOSS, P_ROWS = 0, 8, 16, 17, 18, 19, 20, 21, 27, 28, 31, 49, 56
NSH = 6 * D // NDEV
FROWS = 3


def pack_small(r, *, name):
    ins = [r["st1"], r["st2"], r["stc"], r["dqg"], r["dkvg"], r["dconv_b"], r["dffn_b"], r["dfg"], r["dconv_w"],
           r["dffn_w"], r["loss"]]

    def put_wide(p, row0, row, n):
        for j in range(-(-n // D)):
            w = min(D, n - j * D)
            p[row0 + j:row0 + j + 1, 0:w] = row[:, j * D:j * D + w]

    def body(st1, st2, stc, qg, kvg, cb, fb, fg, cw, fw, loss, p):
        p[...] = jnp.zeros_like(p)
        lat = (st1.at[0:1], st1.at[1:2], st1.at[3:4], st2.at[0:1], st2.at[1:2], st2.at[3:4])
        ctx = (stc.at[0:1], stc.at[1:2])
        for j in range(NDEV):
            done = 0
            while done < NSH:
                q, off = divmod(j * NSH + done, D)
                w = min(D - off, NSH - done)
                p[P_DML + j:P_DML + j + 1, done:done + w] = lat[q][:, off:off + w]
                if q < len(ctx):
                    p[P_DMC + j:P_DMC + j + 1, done:done + w] = ctx[q][:, off:off + w]
                done += w
        p[P_N1:P_N1 + 1, :] = st1[2:3, :] + stc[2:3, :]
        p[P_N2:P_N2 + 1, :] = st2[2:3, :]
        put_wide(p, P_QG, qg, 512)
        put_wide(p, P_KVG, kvg, KVL)
        put_wide(p, P_CB, cb, CONV)
        put_wide(p, P_FG, fg, D)
        put_wide(p, P_LOSS, loss, 128)
        for s in range(2):
            put_wide(p, P_FB + FROWS * s, fb.at[s], DFF)
        for k in range(3):
            put_wide(p, P_CW + k, cw.at[k:k + 1], CONV)
            for s in range(2):
                put_wide(p, P_FW + FROWS * (2 * k + s), fw.at[s, k:k + 1], DFF)

    vm = pl.BlockSpec(memory_space=pltpu.VMEM)
    return pl.pallas_call(
        body, name=name, in_specs=[vm] * len(ins), out_specs=vm, out_shape=jax.ShapeDtypeStruct((P_ROWS, D), F32),
    )(*ins)


def sum_slots(a, *, name):
    rows = dict(norm1_g=(P_N1, D), q_norm_g=(P_QG, QL), kv_norm_g=(P_KVG, KVL), conv_b=(P_CB, CONV), norm2_g=(P_N2, D),
                final_g=(P_FG, D))

    def body(a_ref, sum_ref, *out):
        acc = a_ref[0]
        for k in range(1, NDEV):
            acc = acc + a_ref[k]
        sum_ref[...] = acc
        for ref, (row, n) in zip(out, rows.values()):
            ref[...] = sum_ref[row:row + 1, 0:n]
        fb, bada = out[len(rows):]
        for s in range(2):
            for j in range(FROWS):
                w = min(D, DFF - j * D)
                row = P_FB + FROWS * s + j
                fb[:, s * DFF + j * D:s * DFF + j * D + w] = sum_ref[row:row + 1, 0:w]
        for j in range(NDEV):
            bada[:, j * NSH:(j + 1) * NSH] = (sum_ref[P_DML + j:P_DML + j + 1, 0:NSH]
                                              + sum_ref[P_DMC + j:P_DMC + j + 1, 0:NSH])

    vm = pl.BlockSpec(memory_space=pltpu.VMEM)
    widths = [n for _, n in rows.values()] + [2 * DFF, 6 * D]
    outs = pl.pallas_call(
        body, name=name, in_specs=[vm], out_specs=[vm] * (1 + len(widths)),
        out_shape=[jax.ShapeDtypeStruct(a.shape[1:], F32)] + [jax.ShapeDtypeStruct((1, n), F32) for n in widths])(a)
    return outs[0], dict(zip(list(rows) + ["ffn_conv_b", "b_ada"], outs[1:]))


def ada_bwd(s_all, a_all, a_sum, w_shard, c_ctx, *, name):
    nsh = w_shard.shape[1]
    assert nsh == NSH

    def body(s_ref, a_ref, sum_ref, w_ref, c_ref, dw_ref, gc_ref, s16, dm16, part, buf, sems):
        x, y, c, me = _me()
        s16[...] = jnp.zeros_like(s16)
        dm16[...] = jnp.zeros_like(dm16)
        for k in range(NDEV):
            s16[k:k + 1, :] = s_ref[k, 0:1, :]
            dm16[k:k + 1, :] = a_ref[k, pl.ds(P_DML + me, 1), 0:nsh]
        s16[8:9, :] = s_ref[0, 1:2, :]
        dm16[8:9, :] = sum_ref[pl.ds(P_DMC + me, 1), 0:nsh]
        dw_ref[...] = lax.dot_general(s16[...].astype(BF), dm16[...].astype(BF), (((0,), (0,)), ((), ())),
                                      preferred_element_type=F32)
        part[...] = lax.dot_general(dm16[8:16, :].astype(BF), w_ref[...].astype(BF), (((1,), (1,)), ((), ())),
                                    preferred_element_type=F32)
        buf[me] = part[...]
        _exchange_tiles(lambda lin: part, buf, sems.at[0], sems.at[1])
        acc = buf[0]
        for k in range(1, NDEV):
            acc = acc + buf[k]
        z = c_ref[...]
        sg = jax.nn.sigmoid(z)
        gc_ref[...] = acc * (sg * (1.0 + z * (1.0 - sg)))

    vm = pl.BlockSpec(memory_space=pltpu.VMEM)
    return pl.pallas_call(
        body, name=name, in_specs=[vm] * 5, out_specs=[vm, vm],
        out_shape=[jax.ShapeDtypeStruct((D, nsh), F32), jax.ShapeDtypeStruct((8, D), F32)],
        scratch_shapes=[pltpu.VMEM((16, D), F32), pltpu.VMEM((16, nsh), F32), pltpu.VMEM((8, D), F32),
                        pltpu.VMEM((NDEV, 8, D), F32), pltpu.SemaphoreType.DMA((2,))],
    )(s_all, a_all, a_sum, w_shard, c_ctx)


HBM_SPEC = pl.BlockSpec(memory_space=pltpu.HBM)
SEM_SPEC = pl.BlockSpec(memory_space=pltpu.SEMAPHORE)
EFFECT = pltpu.SideEffectType.DATAFLOW_SIDE_EFFECTING


ALL_PEERS = tuple(range(1, NDEV))
FIRST_HOP = (1, 2, 4, 6)
RELAY = (2, 4, 6)


def _exchange_copies(srcs, lands, send, recv, per_peer, peers):
    x, y, c, me = _me()
    n = len(peers)
    cps = []
    for t in range(len(srcs)):
        for j, k in enumerate(peers):
            dev, lin = _peer(x, y, c, k)
            cps.append(pltpu.make_async_remote_copy(
                src_ref=srcs[t].at[lin] if per_peer else srcs[t], dst_ref=lands[t].at[me],
                send_sem=send.at[n * t + j], recv_sem=recv.at[n * t + j], device_id=dev, device_id_type=MESH))
    return cps


def _relay_copies(lands, send, recv):
    x, y, c, me = _me()
    n = len(RELAY)
    cps = []
    for t in range(len(lands)):
        for j, k in enumerate(RELAY):
            slot = lands[t].at[_peer(x, y, c, k)[1]]
            cps.append(pltpu.make_async_remote_copy(
                src_ref=slot, dst_ref=slot, send_sem=send.at[n * t + j], recv_sem=recv.at[n * t + j],
                device_id=(x, y, 1 - c), device_id_type=MESH))
    return cps


def _own_copies(srcs, lands, own, per_peer):
    me = _me()[3]
    return [pltpu.make_async_copy(srcs[t].at[me] if per_peer else srcs[t], lands[t].at[me], own.at[t])
            for t in range(len(srcs))]


def exchange_start(srcs, *, per_peer, name, dep=None, peers=ALL_PEERS):
    nt = len(srcs)
    ns = len(peers) * nt
    land_shapes = [(a.shape if per_peer else (NDEV,) + a.shape) for a in srcs]
    deps = [] if dep is None else [dep]

    def body(*refs):
        src, land = refs[:nt], refs[nt:2 * nt]
        send, recv, own = refs[2 * nt + len(deps):2 * nt + len(deps) + 3]
        for cp in _exchange_copies(src, land, send, recv, per_peer, peers) + _own_copies(src, land, own, per_peer):
            cp.start()
        refs[-1][...] = jnp.zeros_like(refs[-1])

    hb = lambda a: pltpu.with_memory_space_constraint(a, pltpu.HBM)
    outs = pl.pallas_call(
        body, name=name,
        out_shape=(pltpu.SemaphoreType.DMA((ns,)), pltpu.SemaphoreType.DMA((ns,)), pltpu.SemaphoreType.DMA((nt,)),
                   *[pltpu.HBM(a.shape, a.dtype) for a in srcs], *[pltpu.HBM(s, a.dtype) for s, a in zip(land_shapes, srcs)],
                   jax.ShapeDtypeStruct((8, 128), F32)),
        in_specs=[HBM_SPEC] * (2 * nt) + [pl.BlockSpec(memory_space=pl.ANY)] * len(deps),
        out_specs=(SEM_SPEC, SEM_SPEC, SEM_SPEC, *([HBM_SPEC] * (2 * nt)), pl.BlockSpec(memory_space=pltpu.VMEM)),
        input_output_aliases={i: 3 + i for i in range(2 * nt)},
        compiler_params=pltpu.CompilerParams(has_side_effects=EFFECT),
    )(*[hb(a) for a in srcs], *[hb(lax.empty(s, a.dtype)) for s, a in zip(land_shapes, srcs)], *deps)
    return dict(send=outs[0], recv=outs[1], own=outs[2], src=list(outs[3:3 + nt]), land=list(outs[3 + nt:3 + 2 * nt]),
                token=outs[-1], per_peer=per_peer, peers=peers)


def exchange_wait(h, after, *, name):
    nt = len(h["src"])
    per_peer, peers = h["per_peer"], h["peers"]
    after = list(after) if isinstance(after, (list, tuple)) else [after]

    def body(*refs):
        src, land, send, recv, own = refs[:nt], refs[nt:2 * nt], refs[2 * nt], refs[2 * nt + 1], refs[2 * nt + 2]
        for cp in _exchange_copies(src, land, send, recv, per_peer, peers):
            cp.wait_send()
            cp.wait_recv()
        for cp in _own_copies(src, land, own, per_peer):
            cp.wait()

    outs = pl.pallas_call(
        body, name=name,
        out_shape=(*[pltpu.HBM(a.shape, a.dtype) for a in h["src"]], *[pltpu.HBM(a.shape, a.dtype) for a in h["land"]]),
        in_specs=[HBM_SPEC] * (2 * nt) + [SEM_SPEC, SEM_SPEC, SEM_SPEC] + [pl.BlockSpec(memory_space=pl.ANY)] * len(after),
        out_specs=tuple([HBM_SPEC] * (2 * nt)),
        input_output_aliases={i: i for i in range(2 * nt)},
        compiler_params=pltpu.CompilerParams(has_side_effects=EFFECT),
    )(*h["src"], *h["land"], h["send"], h["recv"], h["own"], *after)
    return list(outs[nt:])


def relay_start(lands, *, name):
    nt = len(lands)
    ns = len(RELAY) * nt

    def body(*refs):
        for cp in _relay_copies(refs[:nt], refs[nt], refs[nt + 1]):
            cp.start()

    outs = pl.pallas_call(
        body, name=name,
        out_shape=(pltpu.SemaphoreType.DMA((ns,)), pltpu.SemaphoreType.DMA((ns,)),
                   *[pltpu.HBM(a.shape, a.dtype) for a in lands]),
        in_specs=[HBM_SPEC] * nt, out_specs=(SEM_SPEC, SEM_SPEC, *([HBM_SPEC] * nt)),
        input_output_aliases={i: 2 + i for i in range(nt)},
        compiler_params=pltpu.CompilerParams(has_side_effects=EFFECT),
    )(*lands)
    return dict(send=outs[0], recv=outs[1], land=list(outs[2:]))


def relay_wait(h, *, name):
    nt = len(h["land"])

    def body(*refs):
        for cp in _relay_copies(refs[:nt], refs[nt], refs[nt + 1]):
            cp.wait_send()
            cp.wait_recv()

    outs = pl.pallas_call(
        body, name=name, out_shape=tuple(pltpu.HBM(a.shape, a.dtype) for a in h["land"]),
        in_specs=[HBM_SPEC] * nt + [SEM_SPEC, SEM_SPEC], out_specs=tuple([HBM_SPEC] * nt),
        input_output_aliases={i: i for i in range(nt)},
        compiler_params=pltpu.CompilerParams(has_side_effects=EFFECT),
    )(*h["land"], h["send"], h["recv"])
    return list(outs)


def _adamw_math(w, g, m, v):
    nm = B1 * m + (1.0 - B1) * g
    nv = B2 * v + (1.0 - B2) * (g * g)
    m_hat = nm / (1.0 - B1 ** STEP)
    v_hat = nv / (1.0 - B2 ** STEP)
    return -LR * (m_hat / (jnp.sqrt(v_hat) + AEPS) + WD * w), nm, nv


def adamw_many(ws, gs, ms, vs, *, name):
    n = len(ws)

    def body(*refs):
        for k in range(n):
            d, nm, nv = _adamw_math(refs[k][...], refs[n + k][...], refs[2 * n + k][...], refs[3 * n + k][...])
            refs[4 * n + k][...] = d
            refs[5 * n + k][...] = nm
            refs[6 * n + k][...] = nv

    vm = pl.BlockSpec(memory_space=pltpu.VMEM)
    sh = [jax.ShapeDtypeStruct(w.shape, F32) for w in ws]
    outs = pl.pallas_call(body, name=name, in_specs=[vm] * (4 * n), out_specs=[vm] * (3 * n), out_shape=sh * 3,
                          )(*ws, *gs, *ms, *vs)
    return outs[:n], outs[n:2 * n], outs[2 * n:]


def adamw(w, g, m, v, *, name, tr=256):
    R, C = w.shape
    tr = _pick(R, tr, 8)

    def body(w_ref, g_ref, m_ref, v_ref, d_ref, nm_ref, nv_ref):
        d_ref[...], nm_ref[...], nv_ref[...] = _adamw_math(w_ref[...], g_ref[...], m_ref[...], v_ref[...])

    blk = pl.BlockSpec((tr, C), lambda i: (i, 0))
    sh = jax.ShapeDtypeStruct((R, C), F32)
    return pl.pallas_call(
        body, name=name, grid=(R // tr,), in_specs=[blk, blk, blk, blk], out_specs=[blk, blk, blk],
        out_shape=[sh, sh, sh], compiler_params=pltpu.CompilerParams(dimension_semantics=("parallel",)),
    )(w, g, m, v)


def adamw_slots(w, slots, m, v, *, name, tr=256):
    unit = w.ndim == 3
    R, C = w.shape[0], w.shape[-1]
    parts = list(slots) if isinstance(slots, (list, tuple)) else [slots]
    n = len(parts)
    assert sum(s.shape[-1] for s in parts) == C
    if R % 16 == 0:
        tr = _pick(R, tr, 16)
    else:
        tr = 144

    def body(w_ref, *refs):
        s_refs, (m_ref, v_ref, g_ref, d_ref, nm_ref, nv_ref) = refs[:n], refs[n:]
        gs = []
        for s_ref in s_refs:
            g = s_ref[0].astype(F32)
            for k in range(1, NDEV):
                g = g + s_ref[k].astype(F32)
            gs.append(g)
        g = gs[0] if n == 1 else jnp.concatenate(gs, axis=-1)
        g_ref[...] = g
        d_ref[...], nm_ref[...], nv_ref[...] = _adamw_math(w_ref[...], g, m_ref[...], v_ref[...])

    blk = pl.BlockSpec((tr, None, C), lambda i: (i, 0, 0)) if unit else pl.BlockSpec((tr, C), lambda i: (i, 0))
    sh = jax.ShapeDtypeStruct(w.shape, F32)
    return pl.pallas_call(
        body, name=name, grid=(pl.cdiv(R, tr),),
        in_specs=[blk] + [pl.BlockSpec((NDEV, tr, s.shape[-1]), lambda i: (0, i, 0)) for s in parts] + [blk, blk],
        out_specs=[blk, blk, blk, blk], out_shape=[sh, sh, sh, sh],
        compiler_params=pltpu.CompilerParams(dimension_semantics=("parallel",)),
    )(w, *parts, m, v)


def kernel(x, c, ctx, c_ctx, w_ada, b_ada, norm1_g, w_in, q_norm_g, kv_norm_g, w_uq, w_ukv, conv_w, conv_b, w_attn_out, w_conv_out, w_o, norm2_g, w_up, ffn_conv_w, ffn_conv_b, w_down, final_g, loss_target, m_c_ctx, m_w_ada, m_b_ada, m_norm1_g, m_w_in, m_q_norm_g, m_kv_norm_g, m_w_uq, m_w_ukv, m_conv_w, m_conv_b, m_w_attn_out, m_w_conv_out, m_w_o, m_norm2_g, m_w_up, m_ffn_conv_w, m_ffn_conv_b, m_w_down, m_final_g, v_c_ctx, v_w_ada, v_b_ada, v_norm1_g, v_w_in, v_q_norm_g, v_kv_norm_g, v_w_uq, v_w_ukv, v_conv_w, v_conv_b, v_w_attn_out, v_w_conv_out, v_w_o, v_norm2_g, v_w_up, v_ffn_conv_w, v_ffn_conv_b, v_w_down, v_final_g):
    me = 4 * lax.axis_index("x") + 2 * lax.axis_index("y") + lax.axis_index("c")
    W = dict(c_ctx=c_ctx, w_ada=w_ada, b_ada=b_ada, norm1_g=norm1_g, w_in=w_in, q_norm_g=q_norm_g, kv_norm_g=kv_norm_g,
             w_uq=w_uq, w_ukv=w_ukv, conv_w=conv_w, conv_b=conv_b, w_attn_out=w_attn_out, w_conv_out=w_conv_out, w_o=w_o,
             norm2_g=norm2_g, w_up=w_up, ffn_conv_w=ffn_conv_w, ffn_conv_b=ffn_conv_b, w_down=w_down, final_g=final_g)
    M = dict(c_ctx=m_c_ctx, w_ada=m_w_ada, b_ada=m_b_ada, norm1_g=m_norm1_g, w_in=m_w_in, q_norm_g=m_q_norm_g,
             kv_norm_g=m_kv_norm_g, w_uq=m_w_uq, w_ukv=m_w_ukv, conv_w=m_conv_w, conv_b=m_conv_b, w_attn_out=m_w_attn_out,
             w_conv_out=m_w_conv_out, w_o=m_w_o, norm2_g=m_norm2_g, w_up=m_w_up, ffn_conv_w=m_ffn_conv_w,
             ffn_conv_b=m_ffn_conv_b, w_down=m_w_down, final_g=m_final_g)
    V = dict(c_ctx=v_c_ctx, w_ada=v_w_ada, b_ada=v_b_ada, norm1_g=v_norm1_g, w_in=v_w_in, q_norm_g=v_q_norm_g,
             kv_norm_g=v_kv_norm_g, w_uq=v_w_uq, w_ukv=v_w_ukv, conv_w=v_conv_w, conv_b=v_conv_b, w_attn_out=v_w_attn_out,
             w_conv_out=v_w_conv_out, w_o=v_w_o, norm2_g=v_norm2_g, w_up=v_w_up, ffn_conv_w=v_ffn_conv_w,
             ffn_conv_b=v_ffn_conv_b, w_down=v_w_down, final_g=v_final_g)
    names = list(W)
    transposed = ("w_up", "w_uq")
    as2d = lambda k, a: (a.reshape(1, -1) if a.ndim == 1 else
                         a[0].T if k in transposed else a.reshape(a.shape[-2], a.shape[-1]))
    W2 = {k: as2d(k, a) for k, a in W.items()}
    M2 = {k: as2d(k, a) for k, a in M.items()}
    V2 = {k: as2d(k, a) for k, a in V.items()}
    unit3 = lambda a: jnp.transpose(a, (2, 0, 1))
    W3, M3, V3 = unit3(W["w_in"]), unit3(M["w_in"]), unit3(V["w_in"])
    nsh = W2["w_ada"].shape[1]

    unit_mid = ("conv_w", "ffn_conv_w")
    mid3 = lambda a: jnp.transpose(a, (1, 0, 2))
    s_all, mod_lat, mod_ctx, ffn_w_full, conv_w_full = ada_fwd(
        c, W2["c_ctx"], mid3(W["ffn_conv_w"]), mid3(W["conv_w"]), W2["w_ada"], W["b_ada"].reshape(NDEV, 1, nsh), [],
        name="ada_fwd")

    stage_w = {"in": ["w_in"], "mid": ["w_uq", "w_ukv", "w_attn_out", "w_conv_out", "w_o"], "up": ["w_up"],
               "down": ["w_down"]}
    two_level = ("in", "mid")
    ag, tok = {}, mod_lat
    for st, nms in stage_w.items():
        ag[st] = exchange_start([W2[nm].astype(BF) for nm in nms], per_peer=False, dep=tok, name="ag_start_" + st,
                                peers=FIRST_HOP if st in two_level else ALL_PEERS)
        tok = ag[st]["token"]

    def get_w(stage, after):
        lands = exchange_wait(ag[stage], after, name="ag_wait_" + stage)
        if stage in two_level:
            lands = relay_wait(relay_start(lands, name="ag_relay_" + stage), name="ag_relay_wait_" + stage)
        g = dict(zip(stage_w[stage], lands))
        if stage == "in":
            return build_win(g["w_in"], name="build_win")
        if stage == "mid":
            wq2, wkv2 = build_wq_wkv(g["w_uq"], g["w_ukv"], name="build_wq_wkv")
            return (wq2, wkv2, unshard_cols(g["w_attn_out"], name="unshard_w_attn_out"),
                    unshard_cols(g["w_conv_out"], name="unshard_w_conv_out"), g["w_o"].reshape(D, D))
        if stage == "up":
            return g["w_up"].reshape(2 * DFF, D)
        return g["w_down"].reshape(DFF, D)

    stage_g = {"ffn": ["w_up", "w_down"], "mid": ["w_attn_out", "w_conv_out", "w_o"], "qkv": ["w_uq", "w_ukv"],
               "in": ["w_in"]}
    rs = {}

    def put_g(stage, g):
        if stage in ("in0", "in1"):
            parts = [shard_win_grad(g["dwin"], g["dwin_c"], col0=g["col0"], name="shard_win_grad_" + stage[-1])]
        elif stage == "mid":
            parts = [shard_cols(g["dwao"], name="shard_w_attn_out"), shard_cols(g["dwco"], name="shard_w_conv_out"),
                     g["dwo"].reshape(NDEV, D // NDEV, D)]
        elif stage == "qkv":
            parts = list(shard_wq_wkv_grad(g["dwq2"], g["dwkv2"], name="shard_wq_wkv_grad"))
        else:
            parts = [g["dwup"].reshape(NDEV, 2 * DFF // NDEV, D), g["dwdn"].reshape(NDEV, DFF // NDEV, D)]
        rs[stage] = exchange_start(parts, per_peer=True, name="rs_start_" + stage)
        return rs[stage]["token"]

    r = _local_step(x[0], ctx[0], loss_target[0], mod_lat, mod_ctx, W2["norm1_g"], W2["q_norm_g"], W2["kv_norm_g"],
                    W2["norm2_g"], W2["final_g"], conv_w_full, W2["conv_b"], ffn_w_full, W2["ffn_conv_b"], get_w, put_g,
                    ag["down"]["token"])

    G, DL, NM, NV = {}, {}, {}, {}

    def finish(stage, after):
        if stage == "in":
            halves = []
            for h in ("in0", "in1"):
                halves += exchange_wait(rs[h], after, name="rs_wait_" + h)
                after = halves[-1]
            G["w_in"], DL["w_in"], NM["w_in"], NV["w_in"] = adamw_slots(W3, halves, M3, V3, name="adamw_w_in")
            return DL["w_in"]
        for nm, sl in zip(stage_g[stage], exchange_wait(rs[stage], after, name="rs_wait_" + stage)):
            G[nm], DL[nm], NM[nm], NV[nm] = adamw_slots(W2[nm], sl, M2[nm], V2[nm], name="adamw_" + nm)
            after = DL[nm]
        return after

    sync = exchange_start([pack_small(r, name="pack_small")], per_peer=False, name="sync_start")
    after = sync["token"]
    for st in ("ffn", "mid", "in", "qkv"):
        after = finish(st, after)
    a_buf, = exchange_wait(sync, [DL[nm] for nms in stage_g.values() for nm in nms], name="sync_wait")
    ssum, g_vec = sum_slots(a_buf, name="sum_small")
    G.update(g_vec)
    loss = ssum[P_LOSS, 0]
    G["conv_w"] = lax.dynamic_slice(ssum[P_CW:P_CW + 3, :CONV], (0, me * (CONV // NDEV)), (3, CONV // NDEV))
    fw_full = ssum[P_FW:P_FW + 6 * FROWS].reshape(3, 2, FROWS * D)[:, :, :DFF].reshape(3, 2 * DFF)
    G["ffn_conv_w"] = lax.dynamic_slice(fw_full, (0, me * (2 * DFF // NDEV)), (3, 2 * DFF // NDEV))

    G["w_ada"], gcc = ada_bwd(s_all, a_buf, ssum, W2["w_ada"], W2["c_ctx"], name="ada_bwd")
    G["c_ctx"] = gcc[0:1]

    DL["w_ada"], NM["w_ada"], NV["w_ada"] = adamw(W2["w_ada"], G["w_ada"], M2["w_ada"], V2["w_ada"], name="adamw_w_ada")
    small = ["c_ctx", "b_ada", "norm1_g", "q_norm_g", "kv_norm_g", "conv_b", "norm2_g", "ffn_conv_b", "final_g", "conv_w",
             "ffn_conv_w"]
    view = lambda k, a3, a2: mid3(a3[k]) if k in unit_mid else a2[k]
    for k in unit_mid:
        G[k] = G[k].reshape(3, 1, -1)
    ds, nms, nvs = adamw_many([view(k, W, W2) for k in small], [G[k] for k in small],
                              [view(k, M, M2) for k in small], [view(k, V, V2) for k in small], name="adamw_small")
    for k, nm in enumerate(small):
        DL[nm], NM[nm], NV[nm] = ds[k], nms[k], nvs[k]

    def as_output(nm, a):
        if nm in transposed:
            return a.T[None]
        if nm == "w_in":
            return jnp.transpose(a, (1, 2, 0))
        if nm in unit_mid and a.ndim == 3:
            return jnp.transpose(a, (1, 0, 2))
        return a.reshape(W[nm].shape)

    outs = [loss, r["dx"][None]]
    for grp in (G, DL, NM, NV):
        outs += [as_output(nm, grp[nm]) for nm in names]
    return tuple(outs)
```

```python
import functools
import numpy as np
import jax
import jax.numpy as jnp
from jax import lax
from jax.experimental import pallas as pl
from jax.experimental.pallas import tpu as pltpu

F32 = jnp.float32
BF = jnp.bfloat16
MESH = pl.DeviceIdType.MESH

D = 1024
T = 2048
TC = 256
TKV = T + TC
GRID_W = 64
NH = 8
DN = 64
DR = 32
DV = 64
QL = 384
KVL = 256
CONV = 512
DFF = 2816
EPS = 1e-6
ROPE_THETA = 10000.0
SCALE = (DN + DR) ** -0.5
NDEV = 8
HP = 128

O_GA, O_GC, O_KV, O_Q, O_CV = 0, 1024, 2048, 2560, 3072
NIN = 4608
CVB = 256
N_IN = 4256
SH_IN = N_IN // NDEV

LR, B1, B2, AEPS, WD, STEP = 0.001, 0.9, 0.999, 1e-08, 0.01, 10


def _pick(n, target, mult=128):
    best = None
    for d in range(mult, min(n, target) + 1, mult):
        if n % d == 0:
            best = d
    return best if best is not None else n


def _swap_start(g):
    return 8 * (g ^ 1)


def mm(a, b, *, ta=False, tb=False, out_dtype=F32, name, tm=1024, tn=1024, tk=2048, M=None, N=None, K=None,
       a_off=(0, 0), b_off=(0, 0), a_stack=False, b_stack=False, o_stack=False, dep=None):
    def dims(arr, stack):
        return (arr.shape[1], 2 * arr.shape[2]) if stack else arr.shape

    ar, ac = dims(a, a_stack)
    br, bc = dims(b, b_stack)
    M = M or ((ac if ta else ar) - a_off[1 if ta else 0])
    K = K or ((ar if ta else ac) - a_off[0 if ta else 1])
    N = N or ((br if tb else bc) - b_off[0 if tb else 1])
    tm = _pick(M, tm, 128 if ta else 16)
    tn = _pick(N // 2 if (o_stack or (b_stack and not tb)) else N, tn, 128)
    tk = _pick(K // 2 if ((a_stack and not ta) or (b_stack and tb)) else K, tk, 128)
    nk = K // tk
    ca = 0 if ta else 1
    cb = 1 if tb else 0

    def body(a_ref, b_ref, *rest):
        o_ref, acc = rest[-2:]
        k = pl.program_id(2)
        part = lax.dot_general(a_ref[...].astype(BF), b_ref[...].astype(BF),
                               (((ca,), (cb,)), ((), ())), preferred_element_type=F32)
        if nk == 1:
            o_ref[...] = part.astype(o_ref.dtype)
        else:
            @pl.when(k == 0)
            def _():
                acc[...] = part

            @pl.when(k > 0)
            def _():
                acc[...] += part

            @pl.when(k == nk - 1)
            def _():
                o_ref[...] = acc[...].astype(o_ref.dtype)

    def spec(blk, rc, off, stack, ncols):
        assert off[0] % blk[0] == 0 and off[1] % blk[1] == 0, (name, blk, off)
        ro, co = off[0] // blk[0], off[1] // blk[1]
        if not stack:
            return pl.BlockSpec(blk, lambda i, j, k: (rc(i, j, k)[0] + ro, rc(i, j, k)[1] + co))
        nhb = ncols // 2 // blk[1]
        return pl.BlockSpec((None,) + blk,
                            lambda i, j, k: ((rc(i, j, k)[1] + co) // nhb, rc(i, j, k)[0] + ro, (rc(i, j, k)[1] + co) % nhb))

    a_spec = spec((tk, tm), lambda i, j, k: (k, i), a_off, a_stack, ac) if ta else \
        spec((tm, tk), lambda i, j, k: (i, k), a_off, a_stack, ac)
    b_spec = spec((tn, tk), lambda i, j, k: (j, k), b_off, b_stack, bc) if tb else \
        spec((tk, tn), lambda i, j, k: (k, j), b_off, b_stack, bc)
    o_spec = spec((tm, tn), lambda i, j, k: (i, j), (0, 0), o_stack, N)
    o_shape = (2, M, N // 2) if o_stack else (M, N)
    deps = [] if dep is None else [dep]
    return pl.pallas_call(
        body, name=name, grid=(M // tm, N // tn, nk),
        in_specs=[a_spec, b_spec] + [pl.BlockSpec(memory_space=pl.ANY)] * len(deps),
        out_specs=o_spec, out_shape=jax.ShapeDtypeStruct(o_shape, out_dtype),
        scratch_shapes=[pltpu.VMEM((tm, tn) if nk > 1 else (8, 128), F32)],
        compiler_params=pltpu.CompilerParams(dimension_semantics=("parallel", "parallel", "arbitrary")),
    )(a, b, *deps)


def _row(width):
    return pl.BlockSpec((1, width), lambda *_: (0, 0))


NLAT = T // TC


def normmod_cat(ctx, x, g, csc, csh, sc, sh, dep, *, name, tm=256):
    assert tm == TC

    def body(c_ref, x_ref, g_ref, csc_ref, csh_ref, sc_ref, sh_ref, dep_ref, h_ref):
        last = pl.program_id(0) == NLAT
        xv = jnp.where(last, c_ref[...], x_ref[...])
        scv = jnp.where(last, csc_ref[...], sc_ref[...])
        shv = jnp.where(last, csh_ref[...], sh_ref[...])
        r = lax.rsqrt(jnp.mean(xv * xv, axis=-1, keepdims=True) + EPS)
        h_ref[...] = ((xv * r * g_ref[...]) * (1.0 + scv) + shv).astype(BF)

    return pl.pallas_call(
        body, name=name, grid=(TKV // tm,),
        in_specs=[pl.BlockSpec((tm, D), lambda i: (0, 0)), pl.BlockSpec((tm, D), lambda i: (jnp.minimum(i, NLAT - 1), 0)),
                  _row(D), _row(D), _row(D), _row(D), _row(D), pl.BlockSpec(memory_space=pl.ANY)],
        out_specs=pl.BlockSpec((tm, D), lambda i: (i, 0)), out_shape=jax.ShapeDtypeStruct((TKV, D), BF),
        compiler_params=pltpu.CompilerParams(dimension_semantics=("parallel",)),
    )(ctx, x, g, csc, csh, sc, sh, dep)


def kvprep(pc, p, kvg, wkv2, ck, sk, *, name, tm=256):
    assert tm == TC
    nb = TKV // tm
    kvcol = O_KV // 512

    def body(pc_ref, p_ref, g_ref, w_ref, ck_ref, sk_ref, k_ref, v_ref, ckv_ref):
        i = pl.program_id(0)
        t = jnp.where(i == NLAT, pc_ref[...], p_ref[...])
        pk = t[:, :KVL]
        r = lax.rsqrt(jnp.mean(pk * pk, axis=-1, keepdims=True) + EPS)
        ckv = (pk * r * g_ref[...]).astype(BF)
        ckv_ref[...] = ckv
        kv2 = jnp.dot(ckv, w_ref[...], preferred_element_type=F32)
        krr = t[:, KVL:KVL + HP] * ck_ref[...] + t[:, KVL + HP:KVL + 2 * HP] * sk_ref[...]
        k_ref[...] = (kv2[:, :NH * HP] + jnp.concatenate([krr] * NH, axis=1)).astype(BF)
        v_ref[...] = kv2[:, NH * HP:].astype(BF)

    return pl.pallas_call(
        body, name=name, grid=(nb,),
        in_specs=[pl.BlockSpec((tm, 512), lambda i: (0, 0)),
                  pl.BlockSpec((tm, 512), lambda i: (jnp.minimum(i, NLAT - 1), kvcol)),
                  _row(KVL), pl.BlockSpec((KVL, NH * HP + NH * DV), lambda i: (0, 0)),
                  pl.BlockSpec((tm, HP), lambda i: (i, 0)), pl.BlockSpec((tm, HP), lambda i: (i, 0))],
        out_specs=[pl.BlockSpec((tm, NH * HP), lambda i: (i, 0)), pl.BlockSpec((tm, NH * DV), lambda i: (i, 0)),
                   pl.BlockSpec((tm, KVL), lambda i: (i, 0))],
        out_shape=[jax.ShapeDtypeStruct((TKV, NH * HP), BF), jax.ShapeDtypeStruct((TKV, NH * DV), BF),
                   jax.ShapeDtypeStruct((TKV, KVL), BF)],
        compiler_params=pltpu.CompilerParams(dimension_semantics=("parallel",)),
    )(pc, p, kvg, wkv2, ck, sk)


def qprep(p, qg, wq2, cq_t, sq_t, *, name, tm=256):
    qcol = O_Q // 512

    def body(p_ref, g_ref, w_ref, c_ref, s_ref, q_ref, cq_ref):
        pq = p_ref[...]
        r = lax.rsqrt(jnp.sum(pq * pq, axis=-1, keepdims=True) * (1.0 / QL) + EPS)
        cq = (pq * r * g_ref[...]).astype(BF)
        cq_ref[...] = cq
        q2 = jnp.dot(cq, w_ref[...], preferred_element_type=F32)
        cc = jnp.concatenate([c_ref[...]] * NH, axis=1)
        ss = jnp.concatenate([s_ref[...]] * NH, axis=1)
        q_ref[...] = (q2[:, :NH * HP] * cc + q2[:, NH * HP:] * ss).astype(BF)

    return pl.pallas_call(
        body, name=name, grid=(T // tm,),
        in_specs=[pl.BlockSpec((tm, 512), lambda i: (i, qcol)), _row(512),
                  pl.BlockSpec((512, 2 * NH * HP), lambda i: (0, 0)),
                  pl.BlockSpec((tm, HP), lambda i: (i, 0)), pl.BlockSpec((tm, HP), lambda i: (i, 0))],
        out_specs=[pl.BlockSpec((tm, NH * HP), lambda i: (i, 0)), pl.BlockSpec((tm, 512), lambda i: (i, 0))],
        out_shape=[jax.ShapeDtypeStruct((T, NH * HP), BF), jax.ShapeDtypeStruct((T, 512), BF)],
        compiler_params=pltpu.CompilerParams(dimension_semantics=("parallel",)),
    )(p, qg, wq2, cq_t, sq_t)


def _head_mask(h):
    lanes = lax.broadcasted_iota(jnp.int32, (1, 2 * DV), 1)
    return (lanes // DV) == (h % 2)


LOG2E = 1.4426950408889634


def attn_fwd(q, k, v, *, name, tq=1024, kc=768):
    def body(q_ref, k_ref, v_ref, o_ref, lse_ref):
        h = pl.program_id(1)
        qv = q_ref[...]
        m = l = acc = None
        for c in range(TKV // kc):
            s = lax.dot_general(qv, k_ref[c * kc:(c + 1) * kc, :], (((1,), (1,)), ((), ())),
                                preferred_element_type=F32) * (SCALE * LOG2E)
            mc = jnp.max(s, axis=-1, keepdims=True)
            if c == 0:
                m = mc
                e = jnp.exp2(s - m)
                l = jnp.sum(e, axis=-1, keepdims=True)
                acc = jnp.dot(e.astype(BF), v_ref[c * kc:(c + 1) * kc, :], preferred_element_type=F32)
            else:
                mn = jnp.maximum(m, mc)
                a = jnp.exp2(m - mn)
                e = jnp.exp2(s - mn)
                l = l * a + jnp.sum(e, axis=-1, keepdims=True)
                acc = acc * a + jnp.dot(e.astype(BF), v_ref[c * kc:(c + 1) * kc, :], preferred_element_type=F32)
                m = mn
        o2 = jnp.where(_head_mask(h), acc * (1.0 / l), 0.0).astype(BF)
        lse_ref[...] = jnp.broadcast_to(m + jnp.log(l) * LOG2E, (tq, HP))

        @pl.when(h % 2 == 0)
        def _():
            o_ref[...] = o2

        @pl.when(h % 2 == 1)
        def _():
            o_ref[...] = o_ref[...] + o2

    return pl.pallas_call(
        body, name=name, grid=(T // tq, NH),
        in_specs=[pl.BlockSpec((tq, HP), lambda i, h: (i, h)), pl.BlockSpec((TKV, HP), lambda i, h: (0, h)),
                  pl.BlockSpec((TKV, 2 * DV), lambda i, h: (0, h // 2))],
        out_specs=[pl.BlockSpec((tq, 2 * DV), lambda i, h: (i, h // 2)), pl.BlockSpec((tq, HP), lambda i, h: (i, h))],
        out_shape=[jax.ShapeDtypeStruct((T, NH * DV), BF), jax.ShapeDtypeStruct((T, NH * HP), F32)],
        compiler_params=pltpu.CompilerParams(dimension_semantics=("parallel", "arbitrary")),
    )(q, k, v)


def _shift_dn(x):
    n = x.shape[0]
    rows = lax.broadcasted_iota(jnp.int32, (n, 1), 0)
    return jnp.where(rows == 0, 0.0, pltpu.roll(x, 1, axis=0))


def _shift_up(x):
    n = x.shape[0]
    rows = lax.broadcasted_iota(jnp.int32, (n, 1), 0)
    return jnp.where(rows == n - 1, 0.0, pltpu.roll(x, n - 1, axis=0))


def _conv(x, w_ref, b_ref):
    return b_ref[...] + _shift_dn(x) * w_ref[0:1, :] + x * w_ref[1:2, :] + _shift_up(x) * w_ref[2:3, :]


def _conv_t(dy, w_ref):
    return _shift_up(dy) * w_ref[0:1, :] + dy * w_ref[1:2, :] + _shift_dn(dy) * w_ref[2:3, :]


def _conv_wgrad(dw_ref, dy, x):
    dw_ref[0:1, :] = jnp.sum(dy * _shift_dn(x), axis=0, keepdims=True)
    dw_ref[1:2, :] = jnp.sum(dy * x, axis=0, keepdims=True)
    dw_ref[2:3, :] = jnp.sum(dy * _shift_up(x), axis=0, keepdims=True)


def convz(p, cw, cb, *, name):
    o0 = O_CV // (3 * CVB)

    def body(p_ref, w_ref, bias_ref, z_ref):
        xv, bv, cv = p_ref[:, 0:CVB], p_ref[:, CVB:2 * CVB], p_ref[:, 2 * CVB:3 * CVB]
        z_ref[...] = (bv * _conv(cv * xv, w_ref, bias_ref)).astype(BF)

    return pl.pallas_call(
        body, name=name, grid=(CONV // CVB,),
        in_specs=[pl.BlockSpec((T, 3 * CVB), lambda j: (0, o0 + j)), pl.BlockSpec((3, CVB), lambda j: (0, j)),
                  pl.BlockSpec((1, CVB), lambda j: (0, j))],
        out_specs=pl.BlockSpec((T, CVB), lambda j: (0, j)),
        out_shape=jax.ShapeDtypeStruct((T, CONV), BF),
        compiler_params=pltpu.CompilerParams(dimension_semantics=("parallel",)),
    )(p, cw, cb)


def out_proj_merge(o, wao, z, wco, p, *, name, tm=512):
    kin = o.shape[1]

    def body(o_ref, wa_ref, z_ref, wc_ref, ga_ref, gc_ref, ya_ref, yc_ref, m_ref):
        ya = jnp.dot(o_ref[...], wa_ref[...], preferred_element_type=F32)
        yc = jnp.dot(z_ref[...], wc_ref[...], preferred_element_type=F32)
        ya_ref[...] = ya
        yc_ref[...] = yc
        m_ref[...] = (jax.nn.sigmoid(ga_ref[...]) * ya + jax.nn.sigmoid(gc_ref[...]) * yc).astype(BF)

    blk = pl.BlockSpec((tm, D), lambda i: (i, 0))
    act = pl.BlockSpec((tm, kin), lambda i: (i, 0))
    wsp = pl.BlockSpec((kin, D), lambda i: (0, 0))
    sh = jax.ShapeDtypeStruct((T, D), F32)
    return pl.pallas_call(
        body, name=name, grid=(T // tm,),
        in_specs=[act, wsp, act, wsp, pl.BlockSpec((tm, D), lambda i: (i, O_GA // D)),
                  pl.BlockSpec((tm, D), lambda i: (i, O_GC // D))],
        out_specs=[blk, blk, blk], out_shape=[sh, sh, jax.ShapeDtypeStruct((T, D), BF)],
        compiler_params=pltpu.CompilerParams(dimension_semantics=("parallel",)),
    )(o, wao, z, wco, p, p)


CONV_HALO = 8
CONV_ROWS = 256


def _row_chunks(n, chunk, carry):
    carry = chunk(0, True, False, carry)
    carry = lax.fori_loop(1, n // CONV_ROWS - 1, lambda c, a: chunk(c * CONV_ROWS, False, False, a), carry)
    return chunk(n - CONV_ROWS, False, True, carry)


def _ext_rows(ref, r0, first, last):
    n, w = ref.shape
    zero = jnp.zeros((CONV_HALO, w), ref.dtype)
    if first:
        return jnp.concatenate([zero, ref[0:CONV_ROWS + CONV_HALO, :]], axis=0)
    if last:
        return jnp.concatenate([ref[n - CONV_ROWS - CONV_HALO:n, :], zero], axis=0)
    return ref[pl.ds(pl.multiple_of(r0 - CONV_HALO, 8), CONV_ROWS + 2 * CONV_HALO), :]


def _center_rows(r0, first, last):
    return slice(r0, r0 + CONV_ROWS) if (first or last) else pl.ds(pl.multiple_of(r0, 8), CONV_ROWS)


def _roll_dn(x):
    return pltpu.roll(x, 1, axis=0)


def _roll_up(x):
    return pltpu.roll(x, x.shape[0] - 1, axis=0)


_CTR = slice(CONV_HALO, CONV_HALO + CONV_ROWS)


def ffn_act(u0, cw, cb, *, name, tc=256):
    nb = DFF // tc

    def body(u_ref, wg_ref, wv_ref, bg_ref, bv_ref, f_ref):
        wg = [wg_ref[k:k + 1, :] for k in range(3)]
        wv = [wv_ref[k:k + 1, :] for k in range(3)]
        bg, bv = bg_ref[...], bv_ref[...]

        def chunk(r0, first, last, carry):
            xg, xv = _ext_rows(u_ref.at[0], r0, first, last), _ext_rows(u_ref.at[1], r0, first, last)
            ug = bg + _roll_dn(xg) * wg[0] + xg * wg[1] + _roll_up(xg) * wg[2]
            uv = bv + _roll_dn(xv) * wv[0] + xv * wv[1] + _roll_up(xv) * wv[2]
            f_ref[_center_rows(r0, first, last), :] = (ug * jax.nn.sigmoid(ug) * uv)[_CTR].astype(BF)
            return carry

        _row_chunks(T, chunk, 0)

    return pl.pallas_call(
        body, name=name, grid=(nb,),
        in_specs=[pl.BlockSpec((2, T, tc), lambda j: (0, 0, j)),
                  pl.BlockSpec((3, tc), lambda j: (0, j)), pl.BlockSpec((3, tc), lambda j: (0, nb + j)),
                  pl.BlockSpec((1, tc), lambda j: (0, j)), pl.BlockSpec((1, tc), lambda j: (0, nb + j))],
        out_specs=pl.BlockSpec((T, tc), lambda j: (0, j)),
        out_shape=jax.ShapeDtypeStruct((T, DFF), BF),
        compiler_params=pltpu.CompilerParams(dimension_semantics=("parallel",)),
    )(u0, cw, cw, cb, cb)


def rows_call(lead, ins, in_specs, out_shape, out_specs, fn, *, name, R, tm):
    tb, a_stack, tk = lead.get("tb", False), lead.get("a_stack", False), lead["tk"]
    K = 2 * lead["a"].shape[2] if a_stack else lead["a"].shape[1]
    nk = K // tk
    deps = [] if lead.get("dep") is None else [lead["dep"]]
    n_in = len(ins)

    def body(a_ref, b_ref, *refs):
        refs = refs[len(deps):]
        in_refs, out_refs, acc = refs[:n_in], refs[n_in:-1], refs[-1]
        i, k = pl.program_id(0), pl.program_id(1)
        part = lax.dot_general(a_ref[...].astype(BF), b_ref[...].astype(BF),
                               (((1,), (1 if tb else 0,)), ((), ())), preferred_element_type=F32)
        if nk == 1:
            fn(i, part, in_refs, out_refs)
            return

        @pl.when(k == 0)
        def _():
            acc[...] = part

        @pl.when(k > 0)
        def _():
            acc[...] += part

        @pl.when(k == nk - 1)
        def _():
            fn(i, acc[...], in_refs, out_refs)

    if a_stack:
        nhb = K // 2 // tk
        a_spec = pl.BlockSpec((None, tm, tk), lambda i, k: (k // nhb, i, k % nhb))
    else:
        a_spec = pl.BlockSpec((tm, tk), lambda i, k: (i, k))
    b_spec = pl.BlockSpec((D, tk), lambda i, k: (0, k)) if tb else pl.BlockSpec((tk, D), lambda i, k: (k, 0))
    return pl.pallas_call(
        body, name=name, grid=(R // tm, nk),
        in_specs=[a_spec, b_spec] + [pl.BlockSpec(memory_space=pl.ANY)] * len(deps) + list(in_specs),
        out_specs=out_specs, out_shape=out_shape,
        scratch_shapes=[pltpu.VMEM((tm, D) if nk > 1 else (8, 128), F32)],
        compiler_params=pltpu.CompilerParams(dimension_semantics=("arbitrary", "arbitrary")),
    )(lead["a"], lead["b"], *deps, *ins)


def _rblk(tm, w=D, col=0):
    return pl.BlockSpec((tm, w), lambda i, k: (i, col))


def _rrow(w=D):
    return pl.BlockSpec((1, w), lambda i, k: (0, 0))


def down_final(f, wdn, x1, g2, fg, tgt, *, name, tm=512):
    def fn(i, d, in_refs, out_refs):
        x1_ref, g2_ref, fg_ref, t_ref = in_refs
        d_ref, dx_ref, dd_ref, dfg_ref, loss_ref = out_refs
        d_ref[...] = d
        xv = x1_ref[...] + g2_ref[...] * d
        r = lax.rsqrt(jnp.mean(xv * xv, axis=-1, keepdims=True) + EPS)
        xh = xv * r
        diff = xh * fg_ref[...] - t_ref[...]
        part = 0.5 * jnp.sum(jnp.mean(diff * diff, axis=-1, keepdims=True), axis=0, keepdims=True)
        dy = diff * (1.0 / D)
        a = dy * fg_ref[...]
        dx = r * (a - xh * jnp.mean(a * xh, axis=-1, keepdims=True))
        dx_ref[...] = dx
        dd_ref[...] = (dx * g2_ref[...]).astype(BF)
        dfg = jnp.sum(dy * xh, axis=0, keepdims=True)

        @pl.when(i == 0)
        def _():
            dfg_ref[...] = dfg
            loss_ref[...] = jnp.broadcast_to(part, (1, 128))

        @pl.when(i > 0)
        def _():
            dfg_ref[...] += dfg
            loss_ref[...] += jnp.broadcast_to(part, (1, 128))

    blk = _rblk(tm)
    return rows_call(
        dict(a=f, b=wdn, tk=DFF), [x1, g2, fg, tgt], [blk, _rrow(), _rrow(), blk],
        [jax.ShapeDtypeStruct((T, D), F32), jax.ShapeDtypeStruct((T, D), F32), jax.ShapeDtypeStruct((T, D), BF),
         jax.ShapeDtypeStruct((1, D), F32), jax.ShapeDtypeStruct((1, 128), F32)],
        [blk, blk, blk, _rrow(), _rrow(128)], fn, name=name, R=T, tm=tm)


def oproj_resid(merged, wo, x, gate, g, sc, sh, *, name, tm=512):
    def fn(i, a, in_refs, out_refs):
        x_ref, gate_ref, g_ref, sc_ref, sh_ref = in_refs
        a_ref, x1_ref, h_ref = out_refs
        a_ref[...] = a
        xv = x_ref[...] + gate_ref[...] * a
        x1_ref[...] = xv
        r = lax.rsqrt(jnp.mean(xv * xv, axis=-1, keepdims=True) + EPS)
        h_ref[...] = ((xv * r * g_ref[...]) * (1.0 + sc_ref[...]) + sh_ref[...]).astype(BF)

    blk = _rblk(tm)
    return rows_call(
        dict(a=merged, b=wo, tk=D), [x, gate, g, sc, sh], [blk, _rrow(), _rrow(), _rrow(), _rrow()],
        [jax.ShapeDtypeStruct((T, D), F32), jax.ShapeDtypeStruct((T, D), F32), jax.ShapeDtypeStruct((T, D), BF)],
        [blk, blk, blk], fn, name=name, R=T, tm=tm)


def oproj_dx_gate_bwd(da, wo, p, ya, yc, wao, wco, *, name, tm=512):
    kin = wao.shape[0]

    def fn(i, dm, in_refs, out_refs):
        ga_ref, gc_ref, ya_ref, yc_ref, wa_ref, wc_ref = in_refs
        dya_ref, dyc_ref, dp_ref, do_ref, dz_ref = out_refs
        sa, sc_ = jax.nn.sigmoid(ga_ref[...]), jax.nn.sigmoid(gc_ref[...])
        dya, dyc = (dm * sa).astype(BF), (dm * sc_).astype(BF)
        dya_ref[...] = dya
        dyc_ref[...] = dyc
        dp_ref[:, 0:D] = (dm * ya_ref[...] * (sa * (1.0 - sa))).astype(BF)
        dp_ref[:, D:2 * D] = (dm * yc_ref[...] * (sc_ * (1.0 - sc_))).astype(BF)
        nt = (((1,), (1,)), ((), ()))
        do_ref[...] = lax.dot_general(dya, wa_ref[...], nt, preferred_element_type=F32).astype(BF)
        dz_ref[...] = lax.dot_general(dyc, wc_ref[...], nt, preferred_element_type=F32)

    blk = _rblk(tm)
    sh = jax.ShapeDtypeStruct((T, D), BF)
    wsp = pl.BlockSpec((kin, D), lambda i, k: (0, 0))
    return rows_call(
        dict(a=da, b=wo, tb=True, tk=D), [p, p, ya, yc, wao, wco],
        [_rblk(tm, D, O_GA // D), _rblk(tm, D, O_GC // D), blk, blk, wsp, wsp],
        [sh, sh, jax.ShapeDtypeStruct((T, NIN), BF), jax.ShapeDtypeStruct((T, kin), BF), jax.ShapeDtypeStruct((T, kin), F32)],
        [blk, blk, _rblk(tm, 2 * D), _rblk(tm, kin), _rblk(tm, kin)], fn, name=name, R=T, tm=tm)


def normmod_bwd(x, dh, g, sc, dres, gsrc, gate, *, name, tm=512):
    R = x.shape[0]
    tm = min(tm, R)
    has_res = dres is not None
    fused = isinstance(dh, dict)
    if fused:
        tb, a_stack, tk = dh.get("tb", False), dh.get("a_stack", False), dh["tk"]
        K = 2 * dh["a"].shape[2] if a_stack else dh["a"].shape[1]
        nk = K // tk
        deps = [] if dh.get("dep") is None else [dh["dep"]]
        n_dh = 2 + len(deps)
    else:
        nk, n_dh = 1, 1

    def elementwise(i, dhv, x_ref, g_ref, sc_ref, res_refs, out_refs):
        xv = x_ref[...]
        r = lax.rsqrt(jnp.mean(xv * xv, axis=-1, keepdims=True) + EPS)
        xh = xv * r
        n = xh * g_ref[...]
        dn = dhv * (1.0 + sc_ref[...])
        a = dn * g_ref[...]
        rows = [jnp.sum(dhv, axis=0, keepdims=True), jnp.sum(dhv * n, axis=0, keepdims=True),
                jnp.sum(dn * xh, axis=0, keepdims=True)]
        if has_res:
            dres_ref, gsrc_ref, gate_ref = res_refs
            dx_ref, dxg_ref, st_ref = out_refs
            dr = dres_ref[...]
            dx = dr + r * (a - xh * jnp.mean(a * xh, axis=-1, keepdims=True))
            dx_ref[...] = dx
            dxg_ref[...] = (dx * gate_ref[...]).astype(BF)
            rows.append(jnp.sum(dr * gsrc_ref[...], axis=0, keepdims=True))
        else:
            st_ref, = out_refs
            rows.append(jnp.zeros((1, D), F32))

        @pl.when(i == 0)
        def _():
            for k, row in enumerate(rows):
                st_ref[k:k + 1, :] = row

        @pl.when(i > 0)
        def _():
            for k, row in enumerate(rows):
                st_ref[k:k + 1, :] += row

    def body(*refs):
        x_ref, dh_refs, g_ref, sc_ref = refs[0], refs[1:1 + n_dh], refs[1 + n_dh], refs[2 + n_dh]
        rest = refs[3 + n_dh:]
        res_refs, rest = (rest[:3], rest[3:]) if has_res else ((), rest)
        out_refs = rest[:3] if has_res else rest[:1]
        i = pl.program_id(0)
        if not fused:
            elementwise(i, dh_refs[0][...], x_ref, g_ref, sc_ref, res_refs, out_refs)
            return
        acc = rest[-1]
        k = pl.program_id(1)
        part = lax.dot_general(dh_refs[0][...].astype(BF), dh_refs[1][...].astype(BF),
                               (((1,), (1 if tb else 0,)), ((), ())), preferred_element_type=F32)
        if nk == 1:
            elementwise(i, part, x_ref, g_ref, sc_ref, res_refs, out_refs)
            return

        @pl.when(k == 0)
        def _():
            acc[...] = part

        @pl.when(k > 0)
        def _():
            acc[...] += part

        @pl.when(k == nk - 1)
        def _():
            elementwise(i, acc[...], x_ref, g_ref, sc_ref, res_refs, out_refs)

    rowb = lambda w: pl.BlockSpec((1, w), lambda i, *k: (0, 0))
    blk = pl.BlockSpec((tm, D), lambda i, *k: (i, 0))
    st_spec = pl.BlockSpec((4, D), lambda i, *k: (0, 0))
    st_shape = jax.ShapeDtypeStruct((4, D), F32)
    if fused:
        if a_stack:
            nhb = K // 2 // tk
            a_spec = pl.BlockSpec((None, tm, tk), lambda i, k: (k // nhb, i, k % nhb))
        else:
            a_spec = pl.BlockSpec((tm, tk), lambda i, k: (i, k))
        b_spec = pl.BlockSpec((D, tk), lambda i, k: (0, k)) if tb else pl.BlockSpec((tk, D), lambda i, k: (k, 0))
        dh_specs = [a_spec, b_spec] + [pl.BlockSpec(memory_space=pl.ANY)] * len(deps)
        dh_args = [dh["a"], dh["b"]] + deps
        grid, sem = (R // tm, nk), ("arbitrary", "arbitrary")
        scratch = [pltpu.VMEM((tm, D) if nk > 1 else (8, 128), F32)]
    else:
        dh_specs, dh_args, grid, sem, scratch = [blk], [dh], (R // tm,), ("arbitrary",), []
    cp = pltpu.CompilerParams(dimension_semantics=sem)
    if has_res:
        return pl.pallas_call(
            body, name=name, grid=grid, in_specs=[blk] + dh_specs + [rowb(D), rowb(D), blk, blk, rowb(D)],
            out_specs=[blk, blk, st_spec], scratch_shapes=scratch,
            out_shape=[jax.ShapeDtypeStruct((R, D), F32), jax.ShapeDtypeStruct((R, D), BF), st_shape],
            compiler_params=cp,
        )(x, *dh_args, g, sc, dres, gsrc, gate)
    return pl.pallas_call(
        body, name=name, grid=grid, in_specs=[blk] + dh_specs + [rowb(D), rowb(D)],
        out_specs=st_spec, out_shape=st_shape, scratch_shapes=scratch, compiler_params=cp,
    )(x, *dh_args, g, sc)


def ffn_act_bwd(u0, df, cw, cb, *, name, tc=128):
    nb = DFF // tc

    def body(u_ref, df_ref, wg_ref, wv_ref, bg_ref, bv_ref, du_ref, dw_ref, db_ref):
        wg = [wg_ref[k:k + 1, :] for k in range(3)]
        wv = [wv_ref[k:k + 1, :] for k in range(3)]
        bg, bv = bg_ref[...], bv_ref[...]

        def chunk(r0, first, last, acc):
            xg, xv = _ext_rows(u_ref.at[0], r0, first, last), _ext_rows(u_ref.at[1], r0, first, last)
            dfe = _ext_rows(df_ref, r0, first, last)
            xg_d, xg_u, xv_d, xv_u = _roll_dn(xg), _roll_up(xg), _roll_dn(xv), _roll_up(xv)
            ug = bg + xg_d * wg[0] + xg * wg[1] + xg_u * wg[2]
            uv = bv + xv_d * wv[0] + xv * wv[1] + xv_u * wv[2]
            sig = jax.nn.sigmoid(ug)
            dug = dfe * uv * (sig * (1.0 + ug * (1.0 - sig)))
            duv = dfe * (ug * sig)
            rows = _center_rows(r0, first, last)
            du_ref[0, rows, :] = (_roll_up(dug) * wg[0] + dug * wg[1] + _roll_dn(dug) * wg[2])[_CTR].astype(BF)
            du_ref[1, rows, :] = (_roll_up(duv) * wv[0] + duv * wv[1] + _roll_dn(duv) * wv[2])[_CTR].astype(BF)
            terms = [dug * xg_d, dug * xg, dug * xg_u, dug, duv * xv_d, duv * xv, duv * xv_u, duv]
            return tuple(a + jnp.sum(t[_CTR], axis=0, keepdims=True) for a, t in zip(acc, terms))

        acc = _row_chunks(T, chunk, tuple(jnp.zeros((1, tc), F32) for _ in range(8)))
        for k in range(3):
            dw_ref[0, k:k + 1, :] = acc[k]
            dw_ref[1, k:k + 1, :] = acc[4 + k]
        db_ref[0] = acc[3]
        db_ref[1] = acc[7]

    lo = lambda r: pl.BlockSpec((r, tc), lambda j: (0, j))
    hi = lambda r: pl.BlockSpec((r, tc), lambda j: (0, nb + j))
    st = lambda r: pl.BlockSpec((2, r, tc), lambda j: (0, 0, j))
    return pl.pallas_call(
        body, name=name, grid=(nb,),
        in_specs=[st(T), lo(T), lo(3), hi(3), lo(1), hi(1)],
        out_specs=[st(T), st(3), st(1)],
        out_shape=[jax.ShapeDtypeStruct((2, T, DFF), BF), jax.ShapeDtypeStruct((2, 3, DFF), F32),
                   jax.ShapeDtypeStruct((2, 1, DFF), F32)],
        compiler_params=pltpu.CompilerParams(dimension_semantics=("parallel",)),
    )(u0, df, cw, cw, cb, cb)


def convz_bwd(p, dz, cw, cb, dp, *, name):
    o0 = O_CV // (3 * CVB)

    def body(p_ref, dz_ref, w_ref, bias_ref, dp_in, dp_ref, dw_ref, dbias_ref):
        xv, bv, cv = p_ref[:, 0:CVB], p_ref[:, CVB:2 * CVB], p_ref[:, 2 * CVB:3 * CVB]
        ci = cv * xv
        dwc = _conv(ci, w_ref, bias_ref)
        dzv = dz_ref[...]
        ddw = dzv * bv
        dci = _conv_t(ddw, w_ref)
        dp_ref[:, 0:CVB] = (dci * cv).astype(BF)
        dp_ref[:, CVB:2 * CVB] = (dzv * dwc).astype(BF)
        dp_ref[:, 2 * CVB:3 * CVB] = (dci * xv).astype(BF)
        _conv_wgrad(dw_ref, ddw, ci)
        dbias_ref[...] = jnp.sum(ddw, axis=0, keepdims=True)

    own = lambda r: pl.BlockSpec((r, CVB), lambda j: (0, j))
    return pl.pallas_call(
        body, name=name, grid=(CONV // CVB,),
        in_specs=[pl.BlockSpec((T, 3 * CVB), lambda j: (0, o0 + j)), own(T), own(3), own(1),
                  pl.BlockSpec(memory_space=pl.ANY)],
        out_specs=[pl.BlockSpec((T, 3 * CVB), lambda j: (0, o0 + j)), own(3), own(1)],
        out_shape=[jax.ShapeDtypeStruct((T, NIN), BF), jax.ShapeDtypeStruct((3, CONV), F32),
                   jax.ShapeDtypeStruct((1, CONV), F32)],
        input_output_aliases={4: 0},
        compiler_params=pltpu.CompilerParams(dimension_semantics=("parallel",)),
    )(p, dz, cw, cb, dp)


def attn_bwd(q, k, v, do, o, lse, dep, *, name, tq=1024, kc=768):
    NKC, KC = TKV // kc, kc
    deps = [] if dep is None else [dep]

    def body(q_ref, k_ref, v_ref, do_ref, o_ref, lse_ref, *rest):
        dq_ref, dk_ref, dv_ref = rest[len(deps):]
        h, i = pl.program_id(0), pl.program_id(1)

        @pl.when(i == 0)
        def _():
            dk_ref[...] = jnp.zeros_like(dk_ref)

        @pl.when((i == 0) & (h % 2 == 0))
        def _():
            dv_ref[...] = jnp.zeros_like(dv_ref)

        qv = q_ref[...]
        dom = jnp.where(_head_mask(h), do_ref[...], jnp.zeros_like(do_ref[...]))
        delta = jnp.sum(dom.astype(F32) * o_ref[...].astype(F32), axis=-1, keepdims=True)
        lse = lse_ref[:, 0:1]
        dq = jnp.zeros((tq, HP), F32)
        for c in range(NKC):
            cols = slice(c * KC, (c + 1) * KC)
            s = lax.dot_general(qv, k_ref[cols, :], (((1,), (1,)), ((), ())),
                                preferred_element_type=F32) * (SCALE * LOG2E)
            pr = jnp.exp2(s - lse)
            dp = lax.dot_general(dom, v_ref[cols, :], (((1,), (1,)), ((), ())), preferred_element_type=F32)
            ds = (pr * (dp - delta) * SCALE).astype(BF)
            dq = dq + jnp.dot(ds, k_ref[cols, :], preferred_element_type=F32)
            dk_ref[cols, :] += lax.dot_general(ds, qv, (((0,), (0,)), ((), ())), preferred_element_type=F32)
            dv_ref[cols, :] += lax.dot_general(pr.astype(BF), dom, (((0,), (0,)), ((), ())), preferred_element_type=F32)
        dq_ref[...] = dq

    return pl.pallas_call(
        body, name=name, grid=(NH, T // tq),
        in_specs=[pl.BlockSpec((tq, HP), lambda h, i: (i, h)), pl.BlockSpec((TKV, HP), lambda h, i: (0, h)),
                  pl.BlockSpec((TKV, 2 * DV), lambda h, i: (0, h // 2)), pl.BlockSpec((tq, 2 * DV), lambda h, i: (i, h // 2)),
                  pl.BlockSpec((tq, 2 * DV), lambda h, i: (i, h // 2)), pl.BlockSpec((tq, HP), lambda h, i: (i, h)),
                  *([pl.BlockSpec(memory_space=pl.ANY)] * len(deps))],
        out_specs=[pl.BlockSpec((tq, HP), lambda h, i: (i, h)), pl.BlockSpec((TKV, HP), lambda h, i: (0, h)),
                   pl.BlockSpec((TKV, 2 * DV), lambda h, i: (0, h // 2))],
        out_shape=[jax.ShapeDtypeStruct((T, NH * HP), F32), jax.ShapeDtypeStruct((TKV, NH * HP), F32),
                   jax.ShapeDtypeStruct((TKV, NH * DV), F32)],
        compiler_params=pltpu.CompilerParams(dimension_semantics=("arbitrary", "arbitrary")),
    )(q, k, v, do, o, lse, *deps)


def qprep_bwd(p, dq, qg, wq2, cq_t, sq_t, dp, *, name, tm=256):
    qcol = O_Q // 512

    def body(p_ref, dq_ref, g_ref, w_ref, c_ref, s_ref, dp_in, dp_ref, dq2_ref, dg_ref):
        i = pl.program_id(0)
        dqv = dq_ref[...]
        cc = jnp.concatenate([c_ref[...]] * NH, axis=1)
        ss = jnp.concatenate([s_ref[...]] * NH, axis=1)
        dq2 = jnp.concatenate([dqv * cc, dqv * ss], axis=1).astype(BF)
        dq2_ref[...] = dq2
        dcq = lax.dot_general(dq2, w_ref[...], (((1,), (1,)), ((), ())), preferred_element_type=F32)
        pq = p_ref[...]
        r = lax.rsqrt(jnp.sum(pq * pq, axis=-1, keepdims=True) * (1.0 / QL) + EPS)
        xh = pq * r
        a = dcq * g_ref[...]
        dp_ref[...] = (r * (a - xh * (jnp.sum(a * xh, axis=-1, keepdims=True) * (1.0 / QL)))).astype(BF)
        dg = jnp.sum(dcq * xh, axis=0, keepdims=True)

        @pl.when(i == 0)
        def _():
            dg_ref[...] = dg

        @pl.when(i > 0)
        def _():
            dg_ref[...] += dg

    return pl.pallas_call(
        body, name=name, grid=(T // tm,),
        in_specs=[pl.BlockSpec((tm, 512), lambda i: (i, qcol)), pl.BlockSpec((tm, NH * HP), lambda i: (i, 0)), _row(512),
                  pl.BlockSpec((512, 2 * NH * HP), lambda i: (0, 0)),
                  pl.BlockSpec((tm, HP), lambda i: (i, 0)), pl.BlockSpec((tm, HP), lambda i: (i, 0)),
                  pl.BlockSpec(memory_space=pl.ANY)],
        out_specs=[pl.BlockSpec((tm, 512), lambda i: (i, qcol)), pl.BlockSpec((tm, 2 * NH * HP), lambda i: (i, 0)), _row(512)],
        out_shape=[jax.ShapeDtypeStruct((T, NIN), BF), jax.ShapeDtypeStruct((T, 2 * NH * HP), BF),
                   jax.ShapeDtypeStruct((1, 512), F32)],
        input_output_aliases={6: 0},
        compiler_params=pltpu.CompilerParams(dimension_semantics=("arbitrary",)),
    )(p, dq, qg, wq2, cq_t, sq_t, dp)


def kvprep_bwd(pc, p, dk, dv, kvg, wkv2, ck, sk, dp, *, name, tm=256):
    assert tm == TC
    nb = TKV // tm
    kvcol = O_KV // 512

    def body(pc_ref, p_ref, dk_ref, dv_ref, g_ref, w_ref, ck_ref, sk_ref, dp_in, dp_ref, dpc_ref, dkv2_ref, dg_ref):
        i = pl.program_id(0)
        t = jnp.where(i == NLAT, pc_ref[...], p_ref[...])
        pk = t[:, :KVL]
        r = lax.rsqrt(jnp.mean(pk * pk, axis=-1, keepdims=True) + EPS)
        xh = pk * r
        dkv = dk_ref[...]
        dkv2 = jnp.concatenate([dkv, dv_ref[...]], axis=1).astype(BF)
        dkv2_ref[...] = dkv2
        dckv = lax.dot_general(dkv2, w_ref[...], (((1,), (1,)), ((), ())), preferred_element_type=F32)
        a = dckv * g_ref[...]
        dpk = r * (a - xh * jnp.mean(a * xh, axis=-1, keepdims=True))
        dkr = dkv[:, 0:HP]
        for hh in range(1, NH):
            dkr = dkr + dkv[:, hh * HP:(hh + 1) * HP]
        res = jnp.concatenate([dpk, dkr * ck_ref[...], dkr * sk_ref[...]], axis=1).astype(BF)
        dg = jnp.sum(dckv * xh, axis=0, keepdims=True)

        @pl.when(i == 0)
        def _():
            dg_ref[...] = dg

        @pl.when(i > 0)
        def _():
            dg_ref[...] += dg

        @pl.when(i < NLAT)
        def _():
            dp_ref[...] = res

        @pl.when(i == NLAT)
        def _():
            dpc_ref[...] = res

    rb = lambda w: pl.BlockSpec((tm, w), lambda i: (i, 0))
    return pl.pallas_call(
        body, name=name, grid=(nb,),
        in_specs=[pl.BlockSpec((tm, 512), lambda i: (0, 0)),
                  pl.BlockSpec((tm, 512), lambda i: (jnp.minimum(i, NLAT - 1), kvcol)),
                  rb(NH * HP), rb(NH * DV), _row(KVL), pl.BlockSpec((KVL, NH * HP + NH * DV), lambda i: (0, 0)),
                  rb(HP), rb(HP), pl.BlockSpec(memory_space=pl.ANY)],
        out_specs=[pl.BlockSpec((tm, 512), lambda i: (jnp.minimum(i, NLAT - 1), kvcol)),
                   pl.BlockSpec((tm, 512), lambda i: (0, 0)), rb(NH * HP + NH * DV), _row(KVL)],
        out_shape=[jax.ShapeDtypeStruct((T, NIN), BF), jax.ShapeDtypeStruct((TC, 512), BF),
                   jax.ShapeDtypeStruct((TKV, NH * HP + NH * DV), BF), jax.ShapeDtypeStruct((1, KVL), F32)],
        input_output_aliases={8: 0},
        compiler_params=pltpu.CompilerParams(dimension_semantics=("arbitrary",)),
    )(pc, p, dk, dv, kvg, wkv2, ck, sk, dp)


def _pieces(src, width, n):
    out, c = [], src
    while c < src + width:
        k = c // n
        w = min(src + width, (k + 1) * n) - c
        out.append((k, c - k * n, c - src, w))
        c += w
    return out


def _win_moves():
    mv = [(2208, 1024, O_GA), (3232, 1024, O_GC), (0, KVL, O_KV), (256, DR, O_KV + KVL + DN), (288, QL, O_Q)]
    mv += [(256 + _swap_start(g), 8, O_KV + KVL + HP + DN + 8 * g) for g in range(4)]
    for j in range(CONV // CVB):
        base = O_CV + 3 * CVB * j
        mv += [(672 + CVB * j, CVB, base), (1184 + CVB * j, CVB, base + CVB), (1696 + CVB * j, CVB, base + 2 * CVB)]
    return mv


_WIN_ZERO = [(O_KV + KVL, DN), (O_KV + KVL + DN + DR, HP - DN - DR), (O_KV + KVL + HP, DN),
             (O_KV + KVL + HP + DN + DR, HP - DN - DR), (O_Q + QL, 512 - QL)]


def build_win(g, *, name, tm=256):
    def body(g_ref, o_ref):
        for src, w, dst in _win_moves():
            for k, a, off, pw in _pieces(src, w, SH_IN):
                o_ref[:, dst + off:dst + off + pw] = g_ref[k, :, a:a + pw]
        for c0, w in _WIN_ZERO:
            o_ref[:, c0:c0 + w] = jnp.zeros((tm, w), o_ref.dtype)

    return pl.pallas_call(
        body, name=name, grid=(D // tm,), in_specs=[pl.BlockSpec((NDEV, tm, SH_IN), lambda i: (0, i, 0))],
        out_specs=pl.BlockSpec((tm, NIN), lambda i: (i, 0)), out_shape=jax.ShapeDtypeStruct((D, NIN), g.dtype),
        compiler_params=pltpu.CompilerParams(dimension_semantics=("parallel",)),
    )(g)


def shard_win_grad(dwt, dwct, *, name, col0=0, tc=256):
    n = dwt.shape[1]

    def body(dw_ref, dwc_ref, o_ref, kvs):
        kvs[...] = dw_ref[O_KV:O_KV + 512, :] + dwc_ref[...]

        def src(row, w):
            if O_KV <= row < O_KV + 512:
                return kvs[row - O_KV:row - O_KV + w, :]
            return dw_ref[row:row + w, :]

        for s, w, dst in _win_moves():
            if w == 8 or s == 256:
                continue
            for k, a, off, pw in _pieces(s, w, SH_IN):
                o_ref[k, a:a + pw, :] = src(dst + off, pw).astype(o_ref.dtype)
        for g in range(4):
            val = src(O_KV + KVL + DN + 8 * g, 8) + src(O_KV + KVL + HP + DN + _swap_start(g), 8)
            o_ref[0, 256 + 8 * g:256 + 8 * g + 8, :] = val.astype(o_ref.dtype)

    return pl.pallas_call(
        body, name=name, grid=(n // tc,),
        in_specs=[pl.BlockSpec((NIN, tc), lambda j: (0, j)), pl.BlockSpec((512, tc), lambda j: (0, j + col0 // tc))],
        out_specs=pl.BlockSpec((NDEV, SH_IN, tc), lambda j: (0, 0, j)),
        out_shape=jax.ShapeDtypeStruct((NDEV, SH_IN, n), BF),
        scratch_shapes=[pltpu.VMEM((512, tc), F32)],
        compiler_params=pltpu.CompilerParams(dimension_semantics=("parallel",)),
    )(dwt, dwct)


def _eye(n, m):
    return (lax.broadcasted_iota(jnp.int32, (n, m), 0) == lax.broadcasted_iota(jnp.int32, (n, m), 1)).astype(BF)


_NT = (((1,), (1,)), ((), ()))


def build_wq_wkv(gq, gkv, *, name):
    def body(gq_ref, gkv_ref, q_ref, kv_ref):
        q_ref[...] = jnp.zeros_like(q_ref)
        kv_ref[...] = jnp.zeros_like(kv_ref)
        eye = _eye(QL, QL)
        for h in range(NH):
            qh = lax.dot_general(eye, gq_ref[h], _NT, preferred_element_type=F32).astype(q_ref.dtype)
            q_ref[0:QL, h * HP:h * HP + DN + DR] = qh
            for g in range(4):
                c0 = NH * HP + h * HP + DN + 8 * g
                q_ref[0:QL, c0:c0 + 8] = qh[:, DN + _swap_start(g):DN + _swap_start(g) + 8]
            kv_ref[:, h * HP:h * HP + DN] = gkv_ref[h, :, 0:DN]
            kv_ref[:, NH * HP + h * DV:NH * HP + (h + 1) * DV] = gkv_ref[h, :, DN:DN + DV]

    vm = pl.BlockSpec(memory_space=pltpu.VMEM)
    return pl.pallas_call(
        body, name=name, in_specs=[vm, vm], out_specs=[vm, vm],
        out_shape=[jax.ShapeDtypeStruct((512, 2 * NH * HP), gq.dtype), jax.ShapeDtypeStruct((KVL, NH * HP + NH * DV), gq.dtype)],
    )(gq, gkv)


def shard_wq_wkv_grad(dwq2, dwkv2, *, name):
    def body(q_ref, kv_ref, gq_ref, gkv_ref, xs):
        xs[...] = jnp.zeros_like(xs)
        eye = _eye(DN + DR, HP)
        for h in range(NH):
            xs[:, 0:DN] = q_ref[0:QL, h * HP:h * HP + DN].astype(BF)
            for g in range(4):
                a = q_ref[0:QL, h * HP + DN + 8 * g:h * HP + DN + 8 * g + 8]
                c0 = NH * HP + h * HP + DN + _swap_start(g)
                xs[:, DN + 8 * g:DN + 8 * g + 8] = (a + q_ref[0:QL, c0:c0 + 8]).astype(BF)
            gq_ref[h] = lax.dot_general(eye, xs[...], _NT, preferred_element_type=F32).astype(BF)
            gkv_ref[h, :, 0:DN] = kv_ref[:, h * HP:h * HP + DN].astype(BF)
            gkv_ref[h, :, DN:DN + DV] = kv_ref[:, NH * HP + h * DV:NH * HP + (h + 1) * DV].astype(BF)

    vm = pl.BlockSpec(memory_space=pltpu.VMEM)
    return pl.pallas_call(
        body, name=name, in_specs=[vm, vm], out_specs=[vm, vm],
        out_shape=[jax.ShapeDtypeStruct((NDEV, DN + DR, QL), BF), jax.ShapeDtypeStruct((NDEV, KVL, DN + DV), BF)],
        scratch_shapes=[pltpu.VMEM((QL, HP), BF)],
    )(dwq2, dwkv2)


def unshard_cols(g, *, name, tm=256):
    _, K, n = g.shape
    tm = _pick(K, tm, 16)

    def body(g_ref, o_ref):
        for k in range(NDEV):
            o_ref[:, k * n:(k + 1) * n] = g_ref[k]

    return pl.pallas_call(
        body, name=name, grid=(K // tm,), in_specs=[pl.BlockSpec((NDEV, tm, n), lambda i: (0, i, 0))],
        out_specs=pl.BlockSpec((tm, NDEV * n), lambda i: (i, 0)), out_shape=jax.ShapeDtypeStruct((K, NDEV * n), g.dtype),
        compiler_params=pltpu.CompilerParams(dimension_semantics=("parallel",)),
    )(g)


def shard_cols(w, *, name, tm=256):
    K, n8 = w.shape
    n = n8 // NDEV
    tm = _pick(K, tm, 16)

    def body(w_ref, o_ref):
        for k in range(NDEV):
            o_ref[k] = w_ref[:, k * n:(k + 1) * n]

    return pl.pallas_call(
        body, name=name, grid=(K // tm,), in_specs=[pl.BlockSpec((tm, n8), lambda i: (i, 0))],
        out_specs=pl.BlockSpec((NDEV, tm, n), lambda i: (0, i, 0)), out_shape=jax.ShapeDtypeStruct((NDEV, K, n), w.dtype),
        compiler_params=pltpu.CompilerParams(dimension_semantics=("parallel",)),
    )(w)


def _rope_tables():
    t = np.arange(T)
    row = (t // GRID_W).astype(np.float32)
    col = (t % GRID_W).astype(np.float32)
    axis_dim = DR // 2
    inv = (np.float32(ROPE_THETA) ** (-np.arange(0, axis_dim, 2, dtype=np.float32) / np.float32(axis_dim))).astype(np.float32)
    ar, ac = (row[:, None] * inv).astype(np.float32), (col[:, None] * inv).astype(np.float32)
    cosv = np.concatenate([np.cos(ar), np.cos(ar), np.cos(ac), np.cos(ac)], axis=1).astype(np.float32)
    sinv = np.concatenate([-np.sin(ar), np.sin(ar), -np.sin(ac), np.sin(ac)], axis=1).astype(np.float32)
    ck = np.zeros((TKV, HP), np.float32)
    sk = np.zeros((TKV, HP), np.float32)
    ck[T:, DN:DN + DR] = 1.0
    ck[:T, DN:DN + DR] = cosv
    sk[:T, DN:DN + DR] = sinv
    cq = np.zeros((T, HP), np.float32)
    cq[:, :DN] = 1.0
    cq[:, DN:DN + DR] = cosv
    return jnp.asarray(ck), jnp.asarray(sk), jnp.asarray(cq), jnp.asarray(sk[:T])


def _local_step(x, ctx, tgt, mod_lat, mod_ctx, n1g, qg, kvg, n2g, fg, conv_w, conv_b, ffn_w, ffn_b, get_w, put_g, dep0):
    sh1, sc1, g1, sh2, sc2, g2 = [mod_lat[:, i * D:(i + 1) * D] for i in range(6)]
    csh1, csc1 = mod_ctx[:, 0:D], mod_ctx[:, D:2 * D]
    ck, sk, cq_t, sq_t = _rope_tables()
    qg_p = jnp.pad(qg, ((0, 0), (0, 512 - QL)))

    hcat = normmod_cat(ctx, x, n1g, csc1, csh1, sc1, sh1, dep0, name="normmod1")
    win = get_w("in", hcat)
    p = mm(hcat, win, M=T, tn=768, name="in_proj")
    pc = mm(hcat, win, M=TC, N=512, a_off=(T, 0), b_off=(0, O_KV), name="in_proj_ctx")
    wq2, wkv2, wao, wco, wo = get_w("mid", p)
    kh, vh, ckv = kvprep(pc, p, kvg, wkv2, ck, sk, name="kvprep")
    qr, cq = qprep(p, qg_p, wq2, cq_t, sq_t, name="qprep")
    o, lse = attn_fwd(qr, kh, vh, name="attn_fwd")
    z = convz(p, conv_w, conv_b, name="convz")
    ya, yc, merged = out_proj_merge(o, wao, z, wco, p, name="attn_conv_out_gate_merge")
    a_out, x1, h2 = oproj_resid(merged, wo, x, g1, n2g, sc2, sh2, name="o_proj_resid_normmod2")
    wup = get_w("up", h2)
    u0 = mm(h2, wup, tb=True, o_stack=True, tn=1408, name="up_proj")
    f = ffn_act(u0, ffn_w, ffn_b, name="ffn_act")
    wdn = get_w("down", f)
    dn, dx2, dd, dfg, loss = down_final(f, wdn, x1, g2, fg, tgt, name="down_proj_final_loss")

    df = mm(dd, wdn, tb=True, tn=1408, name="down_proj_dx")
    dwdn = mm(f, dd, ta=True, out_dtype=BF, tm=1408, name="down_proj_dw")
    du0, dffn_w, dffn_b = ffn_act_bwd(u0, df, ffn_w, ffn_b, name="ffn_act_bwd")
    dwup = mm(du0, h2, ta=True, a_stack=True, out_dtype=BF, tm=1408, name="up_proj_dw")
    tok = put_g("ffn", dict(dwup=dwup, dwdn=dwdn))
    dx1, da, st2 = normmod_bwd(x1, dict(a=du0, b=wup, a_stack=True, tk=DFF, dep=tok), n2g, sc2, dx2, dn, g1,
                               name="up_proj_dx_normmod2_bwd")

    dwo = mm(merged, da, ta=True, out_dtype=BF, tn=512, name="o_proj_dw")
    dya, dyc, dp, do, dz = oproj_dx_gate_bwd(da, wo, p, ya, yc, wao, wco, name="o_proj_dx_gate_merge_bwd")
    dwao = mm(o, dya, ta=True, out_dtype=BF, tn=512, name="attn_out_dw")
    dwco = mm(z, dyc, ta=True, out_dtype=BF, tn=512, name="conv_out_dw")
    tok = put_g("mid", dict(dwao=dwao, dwco=dwco, dwo=dwo))
    dp, dconv_w, dconv_b = convz_bwd(p, dz, conv_w, conv_b, dp, name="convz_bwd")
    dq, dk, dv = attn_bwd(qr, kh, vh, do, o, lse, tok, name="attn_bwd")
    dp, dq2, dqg = qprep_bwd(p, dq, qg_p, wq2, cq_t, sq_t, dp, name="qprep_bwd")
    dp, dpc, dkv2, dkvg = kvprep_bwd(pc, p, dk, dv, kvg, wkv2, ck, sk, dp, name="kvprep_bwd")

    dwin_c = mm(dpc, hcat, ta=True, K=TC, b_off=(T, 0), name="in_proj_ctx_dw")
    tok, extra = None, {}
    for half in range(2):
        dwin = mm(dp, hcat, ta=True, K=T, N=D // 2, b_off=(0, half * (D // 2)), tm=768, dep=tok,
                  name=f"in_proj_dw_{half}")
        tok = put_g(f"in{half}", dict(dwin=dwin, dwin_c=dwin_c, col0=half * (D // 2), **extra))
        if half == 0:
            extra = dict(dwq2=mm(cq, dq2, ta=True, dep=tok, name="q_up_dw"),
                         dwkv2=mm(ckv, dkv2, ta=True, dep=tok, name="kv_up_dw"))
    dhc = mm(dpc, win, tb=True, N=D, K=512, b_off=(0, O_KV), dep=tok, name="in_proj_ctx_dx")
    dx, _, st1 = normmod_bwd(x, dict(a=dp, b=win, tb=True, tk=NIN, dep=tok), n1g, sc1, dx1, a_out, g1,
                             name="in_proj_dx_normmod1_bwd")
    stc = normmod_bwd(ctx, dhc, n1g, csc1, None, None, None, name="normmod1_ctx_bwd")

    return dict(loss=loss, dx=dx, st1=st1, st2=st2, stc=stc, dqg=dqg, dkvg=dkvg, dfg=dfg,
                dconv_w=dconv_w, dconv_b=dconv_b, dffn_w=dffn_w, dffn_b=dffn_b)


def _me():
    x, y, c = lax.axis_index("x"), lax.axis_index("y"), lax.axis_index("c")
    return x, y, c, 4 * x + 2 * y + c


def _peer(x, y, c, k):
    px = 1 - x if k & 4 else x
    py = 1 - y if k & 2 else y
    pc = 1 - c if k & 1 else c
    return (px, py, pc), 4 * px + 2 * py + pc


def _exchange_tiles(src_of_peer, buf, send_sem, recv_sem):
    x, y, c, me = _me()
    for k in range(1, NDEV):
        dev, lin = _peer(x, y, c, k)
        pltpu.make_async_remote_copy(src_ref=src_of_peer(lin), dst_ref=buf.at[me], send_sem=send_sem, recv_sem=recv_sem,
                                     device_id=dev, device_id_type=MESH).start()
    seven = buf.at[pl.ds(0, NDEV - 1)]
    pltpu.make_async_remote_copy(src_ref=seven, dst_ref=seven, send_sem=send_sem, recv_sem=recv_sem,
                                 device_id=(x, y, c), device_id_type=MESH).wait()


def _silu(z):
    return z * jax.nn.sigmoid(z)


def ada_fwd(c, c_ctx, ffn_w, conv_w, w_shard, b_ada, deps, *, name):
    nsh, nf, nc = w_shard.shape[1], ffn_w.shape[2], conv_w.shape[2]
    deps = [d for d in deps if d is not None]

    def body(c_ref, cc_ref, fw_ref, cw_ref, w_ref, b_ref, *rest):
        s_ref, ml_ref, mc_ref, fwf_ref, cwf_ref, m_ref, mine, res, sems = rest[len(deps):]
        x, y, c, me = _me()
        mine[...] = jnp.zeros_like(mine)
        mine[0:1, :] = _silu(c_ref[...])
        mine[1:2, :] = _silu(cc_ref[...])
        for k in range(3):
            mine[2 + k:3 + k, 0:fw_ref.shape[2]] = fw_ref[k]
            mine[5 + k:6 + k, 0:cw_ref.shape[2]] = cw_ref[k]
        s_ref[me] = mine[...]
        _exchange_tiles(lambda lin: mine, s_ref, sems.at[0], sems.at[1])
        sall = s_ref[...].reshape(NDEV * 8, D).astype(BF)
        r = jnp.dot(sall, w_ref[...].astype(BF), preferred_element_type=F32) + b_ref[me]
        res[...] = r.reshape(NDEV, 8, nsh)
        m_ref[me] = res[me]
        _exchange_tiles(lambda lin: res.at[lin], m_ref, sems.at[2], sems.at[3])
        for j in range(NDEV):
            ml_ref[:, j * nsh:(j + 1) * nsh] = m_ref[j, 0:1, :]
            mc_ref[:, j * nsh:(j + 1) * nsh] = m_ref[j, 1:2, :]
            fwf_ref[:, j * nf:(j + 1) * nf] = s_ref[j, 2:5, 0:nf]
            cwf_ref[:, j * nc:(j + 1) * nc] = s_ref[j, 5:8, 0:nc]

    vm = pl.BlockSpec(memory_space=pltpu.VMEM)
    return pl.pallas_call(
        body, name=name, in_specs=[vm] * 6 + [pl.BlockSpec(memory_space=pl.ANY)] * len(deps), out_specs=[vm] * 5,
        out_shape=[jax.ShapeDtypeStruct((NDEV, 8, D), F32), jax.ShapeDtypeStruct((1, NDEV * nsh), F32),
                   jax.ShapeDtypeStruct((1, NDEV * nsh), F32), jax.ShapeDtypeStruct((3, NDEV * nf), F32),
                   jax.ShapeDtypeStruct((3, NDEV * nc), F32)],
        scratch_shapes=[pltpu.VMEM((NDEV, 8, nsh), F32), pltpu.VMEM((8, D), F32), pltpu.VMEM((NDEV, 8, nsh), F32),
                        pltpu.SemaphoreType.DMA((4,))],
    )(c, c_ctx, ffn_w, conv_w, w_shard, b_ada, *deps)


P_DML, P_DMC, P_N1, P_QG, P_KVG, P_CB, P_N2, P_FB, P_FG, P_CW, P_FW, P_LOSS, P_ROWS = 0, 8, 16, 17, 18, 19, 20, 21, 27, 28, 31, 49, 56
NSH = 6 * D // NDEV
FROWS = 3


def pack_small(r, *, name):
    ins = [r["st1"], r["st2"], r["stc"], r["dqg"], r["dkvg"], r["dconv_b"], r["dffn_b"], r["dfg"], r["dconv_w"],
           r["dffn_w"], r["loss"]]

    def put_wide(p, row0, row, n):
        for j in range(-(-n // D)):
            w = min(D, n - j * D)
            p[row0 + j:row0 + j + 1, 0:w] = row[:, j * D:j * D + w]

    def body(st1, st2, stc, qg, kvg, cb, fb, fg, cw, fw, loss, p):
        p[...] = jnp.zeros_like(p)
        lat = (st1.at[0:1], st1.at[1:2], st1.at[3:4], st2.at[0:1], st2.at[1:2], st2.at[3:4])
        ctx = (stc.at[0:1], stc.at[1:2])
        for j in range(NDEV):
            done = 0
            while done < NSH:
                q, off = divmod(j * NSH + done, D)
                w = min(D - off, NSH - done)
                p[P_DML + j:P_DML + j + 1, done:done + w] = lat[q][:, off:off + w]
                if q < len(ctx):
                    p[P_DMC + j:P_DMC + j + 1, done:done + w] = ctx[q][:, off:off + w]
                done += w
        p[P_N1:P_N1 + 1, :] = st1[2:3, :] + stc[2:3, :]
        p[P_N2:P_N2 + 1, :] = st2[2:3, :]
        put_wide(p, P_QG, qg, 512)
        put_wide(p, P_KVG, kvg, KVL)
        put_wide(p, P_CB, cb, CONV)
        put_wide(p, P_FG, fg, D)
        put_wide(p, P_LOSS, loss, 128)
        for s in range(2):
            put_wide(p, P_FB + FROWS * s, fb.at[s], DFF)
        for k in range(3):
            put_wide(p, P_CW + k, cw.at[k:k + 1], CONV)
            for s in range(2):
                put_wide(p, P_FW + FROWS * (2 * k + s), fw.at[s, k:k + 1], DFF)

    vm = pl.BlockSpec(memory_space=pltpu.VMEM)
    return pl.pallas_call(
        body, name=name, in_specs=[vm] * len(ins), out_specs=vm, out_shape=jax.ShapeDtypeStruct((P_ROWS, D), F32),
    )(*ins)


def sum_slots(a, *, name):
    rows = dict(norm1_g=(P_N1, D), q_norm_g=(P_QG, QL), kv_norm_g=(P_KVG, KVL), conv_b=(P_CB, CONV), norm2_g=(P_N2, D),
                final_g=(P_FG, D))

    def body(a_ref, sum_ref, *out):
        acc = a_ref[0]
        for k in range(1, NDEV):
            acc = acc + a_ref[k]
        sum_ref[...] = acc
        for ref, (row, n) in zip(out, rows.values()):
            ref[...] = sum_ref[row:row + 1, 0:n]
        fb, bada = out[len(rows):]
        for s in range(2):
            for j in range(FROWS):
                w = min(D, DFF - j * D)
                row = P_FB + FROWS * s + j
                fb[:, s * DFF + j * D:s * DFF + j * D + w] = sum_ref[row:row + 1, 0:w]
        for j in range(NDEV):
            bada[:, j * NSH:(j + 1) * NSH] = (sum_ref[P_DML + j:P_DML + j + 1, 0:NSH]
                                              + sum_ref[P_DMC + j:P_DMC + j + 1, 0:NSH])

    vm = pl.BlockSpec(memory_space=pltpu.VMEM)
    widths = [n for _, n in rows.values()] + [2 * DFF, 6 * D]
    outs = pl.pallas_call(
        body, name=name, in_specs=[vm], out_specs=[vm] * (1 + len(widths)),
        out_shape=[jax.ShapeDtypeStruct(a.shape[1:], F32)] + [jax.ShapeDtypeStruct((1, n), F32) for n in widths])(a)
    return outs[0], dict(zip(list(rows) + ["ffn_conv_b", "b_ada"], outs[1:]))


def ada_bwd(s_all, a_all, a_sum, w_shard, c_ctx, *, name):
    nsh = w_shard.shape[1]
    assert nsh == NSH

    def body(s_ref, a_ref, sum_ref, w_ref, c_ref, dw_ref, gc_ref, s16, dm16, part, buf, sems):
        x, y, c, me = _me()
        s16[...] = jnp.zeros_like(s16)
        dm16[...] = jnp.zeros_like(dm16)
        for k in range(NDEV):
            s16[k:k + 1, :] = s_ref[k, 0:1, :]
            dm16[k:k + 1, :] = a_ref[k, pl.ds(P_DML + me, 1), 0:nsh]
        s16[8:9, :] = s_ref[0, 1:2, :]
        dm16[8:9, :] = sum_ref[pl.ds(P_DMC + me, 1), 0:nsh]
        dw_ref[...] = lax.dot_general(s16[...].astype(BF), dm16[...].astype(BF), (((0,), (0,)), ((), ())),
                                      preferred_element_type=F32)
        part[...] = lax.dot_general(dm16[8:16, :].astype(BF), w_ref[...].astype(BF), (((1,), (1,)), ((), ())),
                                    preferred_element_type=F32)
        buf[me] = part[...]
        _exchange_tiles(lambda lin: part, buf, sems.at[0], sems.at[1])
        acc = buf[0]
        for k in range(1, NDEV):
            acc = acc + buf[k]
        z = c_ref[...]
        sg = jax.nn.sigmoid(z)
        gc_ref[...] = acc * (sg * (1.0 + z * (1.0 - sg)))

    vm = pl.BlockSpec(memory_space=pltpu.VMEM)
    return pl.pallas_call(
        body, name=name, in_specs=[vm] * 5, out_specs=[vm, vm],
        out_shape=[jax.ShapeDtypeStruct((D, nsh), F32), jax.ShapeDtypeStruct((8, D), F32)],
        scratch_shapes=[pltpu.VMEM((16, D), F32), pltpu.VMEM((16, nsh), F32), pltpu.VMEM((8, D), F32),
                        pltpu.VMEM((NDEV, 8, D), F32), pltpu.SemaphoreType.DMA((2,))],
    )(s_all, a_all, a_sum, w_shard, c_ctx)


HBM_SPEC = pl.BlockSpec(memory_space=pltpu.HBM)
SEM_SPEC = pl.BlockSpec(memory_space=pltpu.SEMAPHORE)
EFFECT = pltpu.SideEffectType.DATAFLOW_SIDE_EFFECTING


ALL_PEERS = tuple(range(1, NDEV))
FIRST_HOP = (1, 2, 4, 6)
RELAY = (2, 4, 6)


def _exchange_copies(srcs, lands, send, recv, per_peer, peers):
    x, y, c, me = _me()
    n = len(peers)
    cps = []
    for t in range(len(srcs)):
        for j, k in enumerate(peers):
            dev, lin = _peer(x, y, c, k)
            cps.append(pltpu.make_async_remote_copy(
                src_ref=srcs[t].at[lin] if per_peer else srcs[t], dst_ref=lands[t].at[me],
                send_sem=send.at[n * t + j], recv_sem=recv.at[n * t + j], device_id=dev, device_id_type=MESH))
    return cps


def _relay_copies(lands, send, recv):
    x, y, c, me = _me()
    n = len(RELAY)
    cps = []
    for t in range(len(lands)):
        for j, k in enumerate(RELAY):
            slot = lands[t].at[_peer(x, y, c, k)[1]]
            cps.append(pltpu.make_async_remote_copy(
                src_ref=slot, dst_ref=slot, send_sem=send.at[n * t + j], recv_sem=recv.at[n * t + j],
                device_id=(x, y, 1 - c), device_id_type=MESH))
    return cps


def _own_copies(srcs, lands, own, per_peer):
    me = _me()[3]
    return [pltpu.make_async_copy(srcs[t].at[me] if per_peer else srcs[t], lands[t].at[me], own.at[t])
            for t in range(len(srcs))]


def exchange_start(srcs, *, per_peer, name, dep=None, peers=ALL_PEERS):
    nt = len(srcs)
    ns = len(peers) * nt
    land_shapes = [(a.shape if per_peer else (NDEV,) + a.shape) for a in srcs]
    deps = [] if dep is None else [dep]

    def body(*refs):
        src, land = refs[:nt], refs[nt:2 * nt]
        send, recv, own = refs[2 * nt + len(deps):2 * nt + len(deps) + 3]
        for cp in _exchange_copies(src, land, send, recv, per_peer, peers) + _own_copies(src, land, own, per_peer):
            cp.start()
        refs[-1][...] = jnp.zeros_like(refs[-1])

    hb = lambda a: pltpu.with_memory_space_constraint(a, pltpu.HBM)
    outs = pl.pallas_call(
        body, name=name,
        out_shape=(pltpu.SemaphoreType.DMA((ns,)), pltpu.SemaphoreType.DMA((ns,)), pltpu.SemaphoreType.DMA((nt,)),
                   *[pltpu.HBM(a.shape, a.dtype) for a in srcs], *[pltpu.HBM(s, a.dtype) for s, a in zip(land_shapes, srcs)],
                   jax.ShapeDtypeStruct((8, 128), F32)),
        in_specs=[HBM_SPEC] * (2 * nt) + [pl.BlockSpec(memory_space=pl.ANY)] * len(deps),
        out_specs=(SEM_SPEC, SEM_SPEC, SEM_SPEC, *([HBM_SPEC] * (2 * nt)), pl.BlockSpec(memory_space=pltpu.VMEM)),
        input_output_aliases={i: 3 + i for i in range(2 * nt)},
        compiler_params=pltpu.CompilerParams(has_side_effects=EFFECT),
    )(*[hb(a) for a in srcs], *[hb(lax.empty(s, a.dtype)) for s, a in zip(land_shapes, srcs)], *deps)
    return dict(send=outs[0], recv=outs[1], own=outs[2], src=list(outs[3:3 + nt]), land=list(outs[3 + nt:3 + 2 * nt]),
                token=outs[-1], per_peer=per_peer, peers=peers)


def exchange_wait(h, after, *, name):
    nt = len(h["src"])
    per_peer, peers = h["per_peer"], h["peers"]
    after = list(after) if isinstance(after, (list, tuple)) else [after]

    def body(*refs):
        src, land, send, recv, own = refs[:nt], refs[nt:2 * nt], refs[2 * nt], refs[2 * nt + 1], refs[2 * nt + 2]
        for cp in _exchange_copies(src, land, send, recv, per_peer, peers):
            cp.wait_send()
            cp.wait_recv()
        for cp in _own_copies(src, land, own, per_peer):
            cp.wait()

    outs = pl.pallas_call(
        body, name=name,
        out_shape=(*[pltpu.HBM(a.shape, a.dtype) for a in h["src"]], *[pltpu.HBM(a.shape, a.dtype) for a in h["land"]]),
        in_specs=[HBM_SPEC] * (2 * nt) + [SEM_SPEC, SEM_SPEC, SEM_SPEC] + [pl.BlockSpec(memory_space=pl.ANY)] * len(after),
        out_specs=tuple([HBM_SPEC] * (2 * nt)),
        input_output_aliases={i: i for i in range(2 * nt)},
        compiler_params=pltpu.CompilerParams(has_side_effects=EFFECT),
    )(*h["src"], *h["land"], h["send"], h["recv"], h["own"], *after)
    return list(outs[nt:])


def relay_start(lands, *, name):
    nt = len(lands)
    ns = len(RELAY) * nt

    def body(*refs):
        for cp in _relay_copies(refs[:nt], refs[nt], refs[nt + 1]):
            cp.start()

    outs = pl.pallas_call(
        body, name=name,
        out_shape=(pltpu.SemaphoreType.DMA((ns,)), pltpu.SemaphoreType.DMA((ns,)),
                   *[pltpu.HBM(a.shape, a.dtype) for a in lands]),
        in_specs=[HBM_SPEC] * nt, out_specs=(SEM_SPEC, SEM_SPEC, *([HBM_SPEC] * nt)),
        input_output_aliases={i: 2 + i for i in range(nt)},
        compiler_params=pltpu.CompilerParams(has_side_effects=EFFECT),
    )(*lands)
    return dict(send=outs[0], recv=outs[1], land=list(outs[2:]))


def relay_wait(h, *, name):
    nt = len(h["land"])

    def body(*refs):
        for cp in _relay_copies(refs[:nt], refs[nt], refs[nt + 1]):
            cp.wait_send()
            cp.wait_recv()

    outs = pl.pallas_call(
        body, name=name, out_shape=tuple(pltpu.HBM(a.shape, a.dtype) for a in h["land"]),
        in_specs=[HBM_SPEC] * nt + [SEM_SPEC, SEM_SPEC], out_specs=tuple([HBM_SPEC] * nt),
        input_output_aliases={i: i for i in range(nt)},
        compiler_params=pltpu.CompilerParams(has_side_effects=EFFECT),
    )(*h["land"], h["send"], h["recv"])
    return list(outs)


def _adamw_math(w, g, m, v):
    nm = B1 * m + (1.0 - B1) * g
    nv = B2 * v + (1.0 - B2) * (g * g)
    m_hat = nm / (1.0 - B1 ** STEP)
    v_hat = nv / (1.0 - B2 ** STEP)
    return -LR * (m_hat / (jnp.sqrt(v_hat) + AEPS) + WD * w), nm, nv


def adamw_many(ws, gs, ms, vs, *, name):
    n = len(ws)

    def body(*refs):
        for k in range(n):
            d, nm, nv = _adamw_math(refs[k][...], refs[n + k][...], refs[2 * n + k][...], refs[3 * n + k][...])
            refs[4 * n + k][...] = d
            refs[5 * n + k][...] = nm
            refs[6 * n + k][...] = nv

    vm = pl.BlockSpec(memory_space=pltpu.VMEM)
    sh = [jax.ShapeDtypeStruct(w.shape, F32) for w in ws]
    outs = pl.pallas_call(body, name=name, in_specs=[vm] * (4 * n), out_specs=[vm] * (3 * n), out_shape=sh * 3,
                          )(*ws, *gs, *ms, *vs)
    return outs[:n], outs[n:2 * n], outs[2 * n:]


def adamw(w, g, m, v, *, name, tr=256):
    R, C = w.shape
    tr = _pick(R, tr, 8)

    def body(w_ref, g_ref, m_ref, v_ref, d_ref, nm_ref, nv_ref):
        d_ref[...], nm_ref[...], nv_ref[...] = _adamw_math(w_ref[...], g_ref[...], m_ref[...], v_ref[...])

    blk = pl.BlockSpec((tr, C), lambda i: (i, 0))
    sh = jax.ShapeDtypeStruct((R, C), F32)
    return pl.pallas_call(
        body, name=name, grid=(R // tr,), in_specs=[blk, blk, blk, blk], out_specs=[blk, blk, blk],
        out_shape=[sh, sh, sh], compiler_params=pltpu.CompilerParams(dimension_semantics=("parallel",)),
    )(w, g, m, v)


def adamw_slots(w, slots, m, v, *, name, tr=256):
    unit = w.ndim == 3
    R, C = w.shape[0], w.shape[-1]
    parts = list(slots) if isinstance(slots, (list, tuple)) else [slots]
    n = len(parts)
    assert sum(s.shape[-1] for s in parts) == C
    if R % 16 == 0:
        tr = _pick(R, tr, 16)
    else:
        tr = 144

    def body(w_ref, *refs):
        s_refs, (m_ref, v_ref, g_ref, d_ref, nm_ref, nv_ref) = refs[:n], refs[n:]
        gs = []
        for s_ref in s_refs:
            g = s_ref[0].astype(F32)
            for k in range(1, NDEV):
                g = g + s_ref[k].astype(F32)
            gs.append(g)
        g = gs[0] if n == 1 else jnp.concatenate(gs, axis=-1)
        g_ref[...] = g
        d_ref[...], nm_ref[...], nv_ref[...] = _adamw_math(w_ref[...], g, m_ref[...], v_ref[...])

    blk = pl.BlockSpec((tr, None, C), lambda i: (i, 0, 0)) if unit else pl.BlockSpec((tr, C), lambda i: (i, 0))
    sh = jax.ShapeDtypeStruct(w.shape, F32)
    return pl.pallas_call(
        body, name=name, grid=(pl.cdiv(R, tr),),
        in_specs=[blk] + [pl.BlockSpec((NDEV, tr, s.shape[-1]), lambda i: (0, i, 0)) for s in parts] + [blk, blk],
        out_specs=[blk, blk, blk, blk], out_shape=[sh, sh, sh, sh],
        compiler_params=pltpu.CompilerParams(dimension_semantics=("parallel",)),
    )(w, *parts, m, v)


def kernel(x, c, ctx, c_ctx, w_ada, b_ada, norm1_g, w_in, q_norm_g, kv_norm_g, w_uq, w_ukv, conv_w, conv_b, w_attn_out, w_conv_out, w_o, norm2_g, w_up, ffn_conv_w, ffn_conv_b, w_down, final_g, loss_target, m_c_ctx, m_w_ada, m_b_ada, m_norm1_g, m_w_in, m_q_norm_g, m_kv_norm_g, m_w_uq, m_w_ukv, m_conv_w, m_conv_b, m_w_attn_out, m_w_conv_out, m_w_o, m_norm2_g, m_w_up, m_ffn_conv_w, m_ffn_conv_b, m_w_down, m_final_g, v_c_ctx, v_w_ada, v_b_ada, v_norm1_g, v_w_in, v_q_norm_g, v_kv_norm_g, v_w_uq, v_w_ukv, v_conv_w, v_conv_b, v_w_attn_out, v_w_conv_out, v_w_o, v_norm2_g, v_w_up, v_ffn_conv_w, v_ffn_conv_b, v_w_down, v_final_g):
    me = 4 * lax.axis_index("x") + 2 * lax.axis_index("y") + lax.axis_index("c")
    W = dict(c_ctx=c_ctx, w_ada=w_ada, b_ada=b_ada, norm1_g=norm1_g, w_in=w_in, q_norm_g=q_norm_g, kv_norm_g=kv_norm_g,
             w_uq=w_uq, w_ukv=w_ukv, conv_w=conv_w, conv_b=conv_b, w_attn_out=w_attn_out, w_conv_out=w_conv_out, w_o=w_o,
             norm2_g=norm2_g, w_up=w_up, ffn_conv_w=ffn_conv_w, ffn_conv_b=ffn_conv_b, w_down=w_down, final_g=final_g)
    M = dict(c_ctx=m_c_ctx, w_ada=m_w_ada, b_ada=m_b_ada, norm1_g=m_norm1_g, w_in=m_w_in, q_norm_g=m_q_norm_g,
             kv_norm_g=m_kv_norm_g, w_uq=m_w_uq, w_ukv=m_w_ukv, conv_w=m_conv_w, conv_b=m_conv_b, w_attn_out=m_w_attn_out,
             w_conv_out=m_w_conv_out, w_o=m_w_o, norm2_g=m_norm2_g, w_up=m_w_up, ffn_conv_w=m_ffn_conv_w,
             ffn_conv_b=m_ffn_conv_b, w_down=m_w_down, final_g=m_final_g)
    V = dict(c_ctx=v_c_ctx, w_ada=v_w_ada, b_ada=v_b_ada, norm1_g=v_norm1_g, w_in=v_w_in, q_norm_g=v_q_norm_g,
             kv_norm_g=v_kv_norm_g, w_uq=v_w_uq, w_ukv=v_w_ukv, conv_w=v_conv_w, conv_b=v_conv_b, w_attn_out=v_w_attn_out,
             w_conv_out=v_w_conv_out, w_o=v_w_o, norm2_g=v_norm2_g, w_up=v_w_up, ffn_conv_w=v_ffn_conv_w,
             ffn_conv_b=v_ffn_conv_b, w_down=v_w_down, final_g=v_final_g)
    names = list(W)
    transposed = ("w_up", "w_uq")
    as2d = lambda k, a: (a.reshape(1, -1) if a.ndim == 1 else
                         a[0].T if k in transposed else a.reshape(a.shape[-2], a.shape[-1]))
    W2 = {k: as2d(k, a) for k, a in W.items()}
    M2 = {k: as2d(k, a) for k, a in M.items()}
    V2 = {k: as2d(k, a) for k, a in V.items()}
    unit3 = lambda a: jnp.transpose(a, (2, 0, 1))
    W3, M3, V3 = unit3(W["w_in"]), unit3(M["w_in"]), unit3(V["w_in"])
    nsh = W2["w_ada"].shape[1]

    unit_mid = ("conv_w", "ffn_conv_w")
    mid3 = lambda a: jnp.transpose(a, (1, 0, 2))
    s_all, mod_lat, mod_ctx, ffn_w_full, conv_w_full = ada_fwd(
        c, W2["c_ctx"], mid3(W["ffn_conv_w"]), mid3(W["conv_w"]), W2["w_ada"], W["b_ada"].reshape(NDEV, 1, nsh), [],
        name="ada_fwd")

    stage_w = {"in": ["w_in"], "mid": ["w_uq", "w_ukv", "w_attn_out", "w_conv_out", "w_o"], "up": ["w_up"],
               "down": ["w_down"]}
    two_level = ("in", "mid")
    ag, tok = {}, mod_lat
    for st, nms in stage_w.items():
        ag[st] = exchange_start([W2[nm].astype(BF) for nm in nms], per_peer=False, dep=tok, name="ag_start_" + st,
                                peers=FIRST_HOP if st in two_level else ALL_PEERS)
        tok = ag[st]["token"]

    def get_w(stage, after):
        lands = exchange_wait(ag[stage], after, name="ag_wait_" + stage)
        if stage in two_level:
            lands = relay_wait(relay_start(lands, name="ag_relay_" + stage), name="ag_relay_wait_" + stage)
        g = dict(zip(stage_w[stage], lands))
        if stage == "in":
            return build_win(g["w_in"], name="build_win")
        if stage == "mid":
            wq2, wkv2 = build_wq_wkv(g["w_uq"], g["w_ukv"], name="build_wq_wkv")
            return (wq2, wkv2, unshard_cols(g["w_attn_out"], name="unshard_w_attn_out"),
                    unshard_cols(g["w_conv_out"], name="unshard_w_conv_out"), g["w_o"].reshape(D, D))
        if stage == "up":
            return g["w_up"].reshape(2 * DFF, D)
        return g["w_down"].reshape(DFF, D)

    stage_g = {"ffn": ["w_up", "w_down"], "mid": ["w_attn_out", "w_conv_out", "w_o"], "qkv": ["w_uq", "w_ukv"],
               "in": ["w_in"]}
    rs = {}

    def put_g(stage, g):
        if stage in ("in0", "in1"):
            parts = [shard_win_grad(g["dwin"], g["dwin_c"], col0=g["col0"], name="shard_win_grad_" + stage[-1])]
            if "dwq2" in g:
                parts += list(shard_wq_wkv_grad(g["dwq2"], g["dwkv2"], name="shard_wq_wkv_grad"))
        elif stage == "mid":
            parts = [shard_cols(g["dwao"], name="shard_w_attn_out"), shard_cols(g["dwco"], name="shard_w_conv_out"),
                     g["dwo"].reshape(NDEV, D // NDEV, D)]
        elif stage == "qkv":
            parts = list(shard_wq_wkv_grad(g["dwq2"], g["dwkv2"], name="shard_wq_wkv_grad"))
        else:
            parts = [g["dwup"].reshape(NDEV, 2 * DFF // NDEV, D), g["dwdn"].reshape(NDEV, DFF // NDEV, D)]
        rs[stage] = exchange_start(parts, per_peer=True, name="rs_start_" + stage)
        return rs[stage]["token"]

    r = _local_step(x[0], ctx[0], loss_target[0], mod_lat, mod_ctx, W2["norm1_g"], W2["q_norm_g"], W2["kv_norm_g"],
                    W2["norm2_g"], W2["final_g"], conv_w_full, W2["conv_b"], ffn_w_full, W2["ffn_conv_b"], get_w, put_g,
                    ag["down"]["token"])

    G, DL, NM, NV = {}, {}, {}, {}

    def finish(stage, after):
        if stage == "in":
            lands0 = exchange_wait(rs["in0"], after, name="rs_wait_in0")
            lands1 = exchange_wait(rs["in1"], lands0[0], name="rs_wait_in1")
            G["w_in"], DL["w_in"], NM["w_in"], NV["w_in"] = adamw_slots(W3, [lands0[0], lands1[0]], M3, V3,
                                                                        name="adamw_w_in")
            for nm, sl in zip(stage_g["qkv"], lands1[1:]):
                G[nm], DL[nm], NM[nm], NV[nm] = adamw_slots(W2[nm], sl, M2[nm], V2[nm], name="adamw_" + nm)
            return DL["w_in"]
        for nm, sl in zip(stage_g[stage], exchange_wait(rs[stage], after, name="rs_wait_" + stage)):
            G[nm], DL[nm], NM[nm], NV[nm] = adamw_slots(W2[nm], sl, M2[nm], V2[nm], name="adamw_" + nm)
            after = DL[nm]
        return after

    sync = exchange_start([pack_small(r, name="pack_small")], per_peer=False, name="sync_start")
    after = sync["token"]
    for st in ("ffn", "mid", "in"):
        after = finish(st, after)
    a_buf, = exchange_wait(sync, [DL[nm] for nms in stage_g.values() for nm in nms], name="sync_wait")
    ssum, g_vec = sum_slots(a_buf, name="sum_small")
    G.update(g_vec)
    loss = ssum[P_LOSS, 0]
    G["conv_w"] = lax.dynamic_slice(ssum[P_CW:P_CW + 3, :CONV], (0, me * (CONV // NDEV)), (3, CONV // NDEV))
    fw_full = ssum[P_FW:P_FW + 6 * FROWS].reshape(3, 2, FROWS * D)[:, :, :DFF].reshape(3, 2 * DFF)
    G["ffn_conv_w"] = lax.dynamic_slice(fw_full, (0, me * (2 * DFF // NDEV)), (3, 2 * DFF // NDEV))

    G["w_ada"], gcc = ada_bwd(s_all, a_buf, ssum, W2["w_ada"], W2["c_ctx"], name="ada_bwd")
    G["c_ctx"] = gcc[0:1]

    DL["w_ada"], NM["w_ada"], NV["w_ada"] = adamw(W2["w_ada"], G["w_ada"], M2["w_ada"], V2["w_ada"], name="adamw_w_ada")
    small = ["c_ctx", "b_ada", "norm1_g", "q_norm_g", "kv_norm_g", "conv_b", "norm2_g", "ffn_conv_b", "final_g", "conv_w",
             "ffn_conv_w"]
    view = lambda k, a3, a2: mid3(a3[k]) if k in unit_mid else a2[k]
    for k in unit_mid:
        G[k] = G[k].reshape(3, 1, -1)
    ds, nms, nvs = adamw_many([view(k, W, W2) for k in small], [G[k] for k in small],
                              [view(k, M, M2) for k in small], [view(k, V, V2) for k in small], name="adamw_small")
    for k, nm in enumerate(small):
        DL[nm], NM[nm], NV[nm] = ds[k], nms[k], nvs[k]

    def as_output(nm, a):
        if nm in transposed:
            return a.T[None]
        if nm == "w_in":
            return jnp.transpose(a, (1, 2, 0))
        if nm in unit_mid and a.ndim == 3:
            return jnp.transpose(a, (1, 0, 2))
        return a.reshape(W[nm].shape)

    outs = [loss, r["dx"][None]]
    for grp in (G, DL, NM, NV):
        outs += [as_output(nm, grp[nm]) for nm in names]
    return tuple(outs)
```

```python
import functools
import numpy as np
import jax
import jax.numpy as jnp
from jax import lax
from jax.experimental import pallas as pl
from jax.experimental.pallas import tpu as pltpu

F32 = jnp.float32
BF = jnp.bfloat16
MESH = pl.DeviceIdType.MESH

D = 1024
T = 2048
TC = 256
TKV = T + TC
GRID_W = 64
NH = 8
DN = 64
DR = 32
DV = 64
QL = 384
KVL = 256
CONV = 512
DFF = 2816
EPS = 1e-6
ROPE_THETA = 10000.0
SCALE = (DN + DR) ** -0.5
NDEV = 8
HP = 128

O_GA, O_GC, O_KV, O_Q, O_CV = 0, 1024, 2048, 2560, 3072
NIN = 4608
CVB = 256
N_IN = 4256
SH_IN = N_IN // NDEV

LR, B1, B2, AEPS, WD, STEP = 0.001, 0.9, 0.999, 1e-08, 0.01, 10


def _pick(n, target, mult=128):
    best = None
    for d in range(mult, min(n, target) + 1, mult):
        if n % d == 0:
            best = d
    return best if best is not None else n


def _swap_start(g):
    return 8 * (g ^ 1)


def mm(a, b, *, ta=False, tb=False, out_dtype=F32, name, tm=1024, tn=1024, tk=2048, M=None, N=None, K=None,
       a_off=(0, 0), b_off=(0, 0), a_stack=False, b_stack=False, o_stack=False, dep=None):
    def dims(arr, stack):
        return (arr.shape[1], 2 * arr.shape[2]) if stack else arr.shape

    ar, ac = dims(a, a_stack)
    br, bc = dims(b, b_stack)
    M = M or ((ac if ta else ar) - a_off[1 if ta else 0])
    K = K or ((ar if ta else ac) - a_off[0 if ta else 1])
    N = N or ((br if tb else bc) - b_off[0 if tb else 1])
    tm = _pick(M, tm, 128 if ta else 16)
    tn = _pick(N // 2 if (o_stack or (b_stack and not tb)) else N, tn, 128)
    tk = _pick(K // 2 if ((a_stack and not ta) or (b_stack and tb)) else K, tk, 128)
    nk = K // tk
    ca = 0 if ta else 1
    cb = 1 if tb else 0

    def body(a_ref, b_ref, *rest):
        o_ref, acc = rest[-2:]
        k = pl.program_id(2)
        part = lax.dot_general(a_ref[...].astype(BF), b_ref[...].astype(BF),
                               (((ca,), (cb,)), ((), ())), preferred_element_type=F32)
        if nk == 1:
            o_ref[...] = part.astype(o_ref.dtype)
        else:
            @pl.when(k == 0)
            def _():
                acc[...] = part

            @pl.when(k > 0)
            def _():
                acc[...] += part

            @pl.when(k == nk - 1)
            def _():
                o_ref[...] = acc[...].astype(o_ref.dtype)

    def spec(blk, rc, off, stack, ncols):
        assert off[0] % blk[0] == 0 and off[1] % blk[1] == 0, (name, blk, off)
        ro, co = off[0] // blk[0], off[1] // blk[1]
        if not stack:
            return pl.BlockSpec(blk, lambda i, j, k: (rc(i, j, k)[0] + ro, rc(i, j, k)[1] + co))
        nhb = ncols // 2 // blk[1]
        return pl.BlockSpec((None,) + blk,
                            lambda i, j, k: ((rc(i, j, k)[1] + co) // nhb, rc(i, j, k)[0] + ro, (rc(i, j, k)[1] + co) % nhb))

    a_spec = spec((tk, tm), lambda i, j, k: (k, i), a_off, a_stack, ac) if ta else \
        spec((tm, tk), lambda i, j, k: (i, k), a_off, a_stack, ac)
    b_spec = spec((tn, tk), lambda i, j, k: (j, k), b_off, b_stack, bc) if tb else \
        spec((tk, tn), lambda i, j, k: (k, j), b_off, b_stack, bc)
    o_spec = spec((tm, tn), lambda i, j, k: (i, j), (0, 0), o_stack, N)
    o_shape = (2, M, N // 2) if o_stack else (M, N)
    deps = [] if dep is None else [dep]
    return pl.pallas_call(
        body, name=name, grid=(M // tm, N // tn, nk),
        in_specs=[a_spec, b_spec] + [pl.BlockSpec(memory_space=pl.ANY)] * len(deps),
        out_specs=o_spec, out_shape=jax.ShapeDtypeStruct(o_shape, out_dtype),
        scratch_shapes=[pltpu.VMEM((tm, tn) if nk > 1 else (8, 128), F32)],
        compiler_params=pltpu.CompilerParams(dimension_semantics=("parallel", "parallel", "arbitrary")),
    )(a, b, *deps)


def _row(width):
    return pl.BlockSpec((1, width), lambda *_: (0, 0))


NLAT = T // TC


def normmod_cat(ctx, x, g, csc, csh, sc, sh, dep, *, name, tm=256):
    assert tm == TC

    def body(c_ref, x_ref, g_ref, csc_ref, csh_ref, sc_ref, sh_ref, dep_ref, h_ref):
        last = pl.program_id(0) == NLAT
        xv = jnp.where(last, c_ref[...], x_ref[...])
        scv = jnp.where(last, csc_ref[...], sc_ref[...])
        shv = jnp.where(last, csh_ref[...], sh_ref[...])
        r = lax.rsqrt(jnp.mean(xv * xv, axis=-1, keepdims=True) + EPS)
        h_ref[...] = ((xv * r * g_ref[...]) * (1.0 + scv) + shv).astype(BF)

    return pl.pallas_call(
        body, name=name, grid=(TKV // tm,),
        in_specs=[pl.BlockSpec((tm, D), lambda i: (0, 0)), pl.BlockSpec((tm, D), lambda i: (jnp.minimum(i, NLAT - 1), 0)),
                  _row(D), _row(D), _row(D), _row(D), _row(D), pl.BlockSpec(memory_space=pl.ANY)],
        out_specs=pl.BlockSpec((tm, D), lambda i: (i, 0)), out_shape=jax.ShapeDtypeStruct((TKV, D), BF),
        compiler_params=pltpu.CompilerParams(dimension_semantics=("parallel",)),
    )(ctx, x, g, csc, csh, sc, sh, dep)


def kvprep(pc, p, kvg, wkv2, ck, sk, *, name, tm=256):
    assert tm == TC
    nb = TKV // tm
    kvcol = O_KV // 512

    def body(pc_ref, p_ref, g_ref, w_ref, ck_ref, sk_ref, k_ref, v_ref, ckv_ref):
        i = pl.program_id(0)
        t = jnp.where(i == NLAT, pc_ref[...], p_ref[...])
        pk = t[:, :KVL]
        r = lax.rsqrt(jnp.mean(pk * pk, axis=-1, keepdims=True) + EPS)
        ckv = (pk * r * g_ref[...]).astype(BF)
        ckv_ref[...] = ckv
        kv2 = jnp.dot(ckv, w_ref[...], preferred_element_type=F32)
        krr = t[:, KVL:KVL + HP] * ck_ref[...] + t[:, KVL + HP:KVL + 2 * HP] * sk_ref[...]
        k_ref[...] = (kv2[:, :NH * HP] + jnp.concatenate([krr] * NH, axis=1)).astype(BF)
        v_ref[...] = kv2[:, NH * HP:].astype(BF)

    return pl.pallas_call(
        body, name=name, grid=(nb,),
        in_specs=[pl.BlockSpec((tm, 512), lambda i: (0, 0)),
                  pl.BlockSpec((tm, 512), lambda i: (jnp.minimum(i, NLAT - 1), kvcol)),
                  _row(KVL), pl.BlockSpec((KVL, NH * HP + NH * DV), lambda i: (0, 0)),
                  pl.BlockSpec((tm, HP), lambda i: (i, 0)), pl.BlockSpec((tm, HP), lambda i: (i, 0))],
        out_specs=[pl.BlockSpec((tm, NH * HP), lambda i: (i, 0)), pl.BlockSpec((tm, NH * DV), lambda i: (i, 0)),
                   pl.BlockSpec((tm, KVL), lambda i: (i, 0))],
        out_shape=[jax.ShapeDtypeStruct((TKV, NH * HP), BF), jax.ShapeDtypeStruct((TKV, NH * DV), BF),
                   jax.ShapeDtypeStruct((TKV, KVL), BF)],
        compiler_params=pltpu.CompilerParams(dimension_semantics=("parallel",)),
    )(pc, p, kvg, wkv2, ck, sk)


def qprep(p, qg, wq2, cq_t, sq_t, *, name, tm=256):
    qcol = O_Q // 512

    def body(p_ref, g_ref, w_ref, c_ref, s_ref, q_ref, cq_ref):
        pq = p_ref[...]
        r = lax.rsqrt(jnp.sum(pq * pq, axis=-1, keepdims=True) * (1.0 / QL) + EPS)
        cq = (pq * r * g_ref[...]).astype(BF)
        cq_ref[...] = cq
        q2 = jnp.dot(cq, w_ref[...], preferred_element_type=F32)
        cc = jnp.concatenate([c_ref[...]] * NH, axis=1)
        ss = jnp.concatenate([s_ref[...]] * NH, axis=1)
        q_ref[...] = (q2[:, :NH * HP] * cc + q2[:, NH * HP:] * ss).astype(BF)

    return pl.pallas_call(
        body, name=name, grid=(T // tm,),
        in_specs=[pl.BlockSpec((tm, 512), lambda i: (i, qcol)), _row(512),
                  pl.BlockSpec((512, 2 * NH * HP), lambda i: (0, 0)),
                  pl.BlockSpec((tm, HP), lambda i: (i, 0)), pl.BlockSpec((tm, HP), lambda i: (i, 0))],
        out_specs=[pl.BlockSpec((tm, NH * HP), lambda i: (i, 0)), pl.BlockSpec((tm, 512), lambda i: (i, 0))],
        out_shape=[jax.ShapeDtypeStruct((T, NH * HP), BF), jax.ShapeDtypeStruct((T, 512), BF)],
        compiler_params=pltpu.CompilerParams(dimension_semantics=("parallel",)),
    )(p, qg, wq2, cq_t, sq_t)


def _head_mask(h):
    lanes = lax.broadcasted_iota(jnp.int32, (1, 2 * DV), 1)
    return (lanes // DV) == (h % 2)


LOG2E = 1.4426950408889634


def attn_fwd(q, k, v, *, name, tq=1024, kc=768):
    def body(q_ref, k_ref, v_ref, o_ref, lse_ref):
        h = pl.program_id(1)
        qv = q_ref[...]
        m = l = acc = None
        for c in range(TKV // kc):
            s = lax.dot_general(qv, k_ref[c * kc:(c + 1) * kc, :], (((1,), (1,)), ((), ())),
                                preferred_element_type=F32) * (SCALE * LOG2E)
            mc = jnp.max(s, axis=-1, keepdims=True)
            if c == 0:
                m = mc
                e = jnp.exp2(s - m)
                l = jnp.sum(e, axis=-1, keepdims=True)
                acc = jnp.dot(e.astype(BF), v_ref[c * kc:(c + 1) * kc, :], preferred_element_type=F32)
            else:
                mn = jnp.maximum(m, mc)
                a = jnp.exp2(m - mn)
                e = jnp.exp2(s - mn)
                l = l * a + jnp.sum(e, axis=-1, keepdims=True)
                acc = acc * a + jnp.dot(e.astype(BF), v_ref[c * kc:(c + 1) * kc, :], preferred_element_type=F32)
                m = mn
        o2 = jnp.where(_head_mask(h), acc * (1.0 / l), 0.0).astype(BF)
        lse_ref[...] = jnp.broadcast_to(m + jnp.log(l) * LOG2E, (tq, HP))

        @pl.when(h % 2 == 0)
        def _():
            o_ref[...] = o2

        @pl.when(h % 2 == 1)
        def _():
            o_ref[...] = o_ref[...] + o2

    return pl.pallas_call(
        body, name=name, grid=(T // tq, NH),
        in_specs=[pl.BlockSpec((tq, HP), lambda i, h: (i, h)), pl.BlockSpec((TKV, HP), lambda i, h: (0, h)),
                  pl.BlockSpec((TKV, 2 * DV), lambda i, h: (0, h // 2))],
        out_specs=[pl.BlockSpec((tq, 2 * DV), lambda i, h: (i, h // 2)), pl.BlockSpec((tq, HP), lambda i, h: (i, h))],
        out_shape=[jax.ShapeDtypeStruct((T, NH * DV), BF), jax.ShapeDtypeStruct((T, NH * HP), F32)],
        compiler_params=pltpu.CompilerParams(dimension_semantics=("parallel", "arbitrary")),
    )(q, k, v)


def _shift_dn(x):
    n = x.shape[0]
    rows = lax.broadcasted_iota(jnp.int32, (n, 1), 0)
    return jnp.where(rows == 0, 0.0, pltpu.roll(x, 1, axis=0))


def _shift_up(x):
    n = x.shape[0]
    rows = lax.broadcasted_iota(jnp.int32, (n, 1), 0)
    return jnp.where(rows == n - 1, 0.0, pltpu.roll(x, n - 1, axis=0))


def _conv(x, w_ref, b_ref):
    return b_ref[...] + _shift_dn(x) * w_ref[0:1, :] + x * w_ref[1:2, :] + _shift_up(x) * w_ref[2:3, :]


def _conv_t(dy, w_ref):
    return _shift_up(dy) * w_ref[0:1, :] + dy * w_ref[1:2, :] + _shift_dn(dy) * w_ref[2:3, :]


def _conv_wgrad(dw_ref, dy, x):
    dw_ref[0:1, :] = jnp.sum(dy * _shift_dn(x), axis=0, keepdims=True)
    dw_ref[1:2, :] = jnp.sum(dy * x, axis=0, keepdims=True)
    dw_ref[2:3, :] = jnp.sum(dy * _shift_up(x), axis=0, keepdims=True)


def convz(p, cw, cb, *, name):
    o0 = O_CV // (3 * CVB)

    def body(p_ref, w_ref, bias_ref, z_ref):
        xv, bv, cv = p_ref[:, 0:CVB], p_ref[:, CVB:2 * CVB], p_ref[:, 2 * CVB:3 * CVB]
        z_ref[...] = (bv * _conv(cv * xv, w_ref, bias_ref)).astype(BF)

    return pl.pallas_call(
        body, name=name, grid=(CONV // CVB,),
        in_specs=[pl.BlockSpec((T, 3 * CVB), lambda j: (0, o0 + j)), pl.BlockSpec((3, CVB), lambda j: (0, j)),
                  pl.BlockSpec((1, CVB), lambda j: (0, j))],
        out_specs=pl.BlockSpec((T, CVB), lambda j: (0, j)),
        out_shape=jax.ShapeDtypeStruct((T, CONV), BF),
        compiler_params=pltpu.CompilerParams(dimension_semantics=("parallel",)),
    )(p, cw, cb)


def out_proj_merge(o, wao, z, wco, p, *, name, tm=512):
    kin = o.shape[1]

    def body(o_ref, wa_ref, z_ref, wc_ref, ga_ref, gc_ref, ya_ref, yc_ref, m_ref):
        ya = jnp.dot(o_ref[...], wa_ref[...], preferred_element_type=F32)
        yc = jnp.dot(z_ref[...], wc_ref[...], preferred_element_type=F32)
        ya_ref[...] = ya
        yc_ref[...] = yc
        m_ref[...] = (jax.nn.sigmoid(ga_ref[...]) * ya + jax.nn.sigmoid(gc_ref[...]) * yc).astype(BF)

    blk = pl.BlockSpec((tm, D), lambda i: (i, 0))
    act = pl.BlockSpec((tm, kin), lambda i: (i, 0))
    wsp = pl.BlockSpec((kin, D), lambda i: (0, 0))
    sh = jax.ShapeDtypeStruct((T, D), F32)
    return pl.pallas_call(
        body, name=name, grid=(T // tm,),
        in_specs=[act, wsp, act, wsp, pl.BlockSpec((tm, D), lambda i: (i, O_GA // D)),
                  pl.BlockSpec((tm, D), lambda i: (i, O_GC // D))],
        out_specs=[blk, blk, blk], out_shape=[sh, sh, jax.ShapeDtypeStruct((T, D), BF)],
        compiler_params=pltpu.CompilerParams(dimension_semantics=("parallel",)),
    )(o, wao, z, wco, p, p)


CONV_HALO = 8
CONV_ROWS = 256


def _row_chunks(n, chunk, carry):
    carry = chunk(0, True, False, carry)
    carry = lax.fori_loop(1, n // CONV_ROWS - 1, lambda c, a: chunk(c * CONV_ROWS, False, False, a), carry)
    return chunk(n - CONV_ROWS, False, True, carry)


def _ext_rows(ref, r0, first, last):
    n, w = ref.shape
    zero = jnp.zeros((CONV_HALO, w), ref.dtype)
    if first:
        return jnp.concatenate([zero, ref[0:CONV_ROWS + CONV_HALO, :]], axis=0)
    if last:
        return jnp.concatenate([ref[n - CONV_ROWS - CONV_HALO:n, :], zero], axis=0)
    return ref[pl.ds(pl.multiple_of(r0 - CONV_HALO, 8), CONV_ROWS + 2 * CONV_HALO), :]


def _center_rows(r0, first, last):
    return slice(r0, r0 + CONV_ROWS) if (first or last) else pl.ds(pl.multiple_of(r0, 8), CONV_ROWS)


def _roll_dn(x):
    return pltpu.roll(x, 1, axis=0)


def _roll_up(x):
    return pltpu.roll(x, x.shape[0] - 1, axis=0)


_CTR = slice(CONV_HALO, CONV_HALO + CONV_ROWS)


def ffn_act(u0, cw, cb, *, name, tc=256):
    nb = DFF // tc

    def body(u_ref, wg_ref, wv_ref, bg_ref, bv_ref, f_ref):
        wg = [wg_ref[k:k + 1, :] for k in range(3)]
        wv = [wv_ref[k:k + 1, :] for k in range(3)]
        bg, bv = bg_ref[...], bv_ref[...]

        def chunk(r0, first, last, carry):
            xg, xv = _ext_rows(u_ref.at[0], r0, first, last), _ext_rows(u_ref.at[1], r0, first, last)
            ug = bg + _roll_dn(xg) * wg[0] + xg * wg[1] + _roll_up(xg) * wg[2]
            uv = bv + _roll_dn(xv) * wv[0] + xv * wv[1] + _roll_up(xv) * wv[2]
            f_ref[_center_rows(r0, first, last), :] = (ug * jax.nn.sigmoid(ug) * uv)[_CTR].astype(BF)
            return carry

        _row_chunks(T, chunk, 0)

    return pl.pallas_call(
        body, name=name, grid=(nb,),
        in_specs=[pl.BlockSpec((2, T, tc), lambda j: (0, 0, j)),
                  pl.BlockSpec((3, tc), lambda j: (0, j)), pl.BlockSpec((3, tc), lambda j: (0, nb + j)),
                  pl.BlockSpec((1, tc), lambda j: (0, j)), pl.BlockSpec((1, tc), lambda j: (0, nb + j))],
        out_specs=pl.BlockSpec((T, tc), lambda j: (0, j)),
        out_shape=jax.ShapeDtypeStruct((T, DFF), BF),
        compiler_params=pltpu.CompilerParams(dimension_semantics=("parallel",)),
    )(u0, cw, cw, cb, cb)


def rows_call(lead, ins, in_specs, out_shape, out_specs, fn, *, name, R, tm):
    tb, a_stack, tk = lead.get("tb", False), lead.get("a_stack", False), lead["tk"]
    K = 2 * lead["a"].shape[2] if a_stack else lead["a"].shape[1]
    nk = K // tk
    deps = [] if lead.get("dep") is None else [lead["dep"]]
    n_in = len(ins)

    def body(a_ref, b_ref, *refs):
        refs = refs[len(deps):]
        in_refs, out_refs, acc = refs[:n_in], refs[n_in:-1], refs[-1]
        i, k = pl.program_id(0), pl.program_id(1)
        part = lax.dot_general(a_ref[...].astype(BF), b_ref[...].astype(BF),
                               (((1,), (1 if tb else 0,)), ((), ())), preferred_element_type=F32)
        if nk == 1:
            fn(i, part, in_refs, out_refs)
            return

        @pl.when(k == 0)
        def _():
            acc[...] = part

        @pl.when(k > 0)
        def _():
            acc[...] += part

        @pl.when(k == nk - 1)
        def _():
            fn(i, acc[...], in_refs, out_refs)

    if a_stack:
        nhb = K // 2 // tk
        a_spec = pl.BlockSpec((None, tm, tk), lambda i, k: (k // nhb, i, k % nhb))
    else:
        a_spec = pl.BlockSpec((tm, tk), lambda i, k: (i, k))
    b_spec = pl.BlockSpec((D, tk), lambda i, k: (0, k)) if tb else pl.BlockSpec((tk, D), lambda i, k: (k, 0))
    return pl.pallas_call(
        body, name=name, grid=(R // tm, nk),
        in_specs=[a_spec, b_spec] + [pl.BlockSpec(memory_space=pl.ANY)] * len(deps) + list(in_specs),
        out_specs=out_specs, out_shape=out_shape,
        scratch_shapes=[pltpu.VMEM((tm, D) if nk > 1 else (8, 128), F32)],
        compiler_params=pltpu.CompilerParams(dimension_semantics=("arbitrary", "arbitrary")),
    )(lead["a"], lead["b"], *deps, *ins)


def _rblk(tm, w=D, col=0):
    return pl.BlockSpec((tm, w), lambda i, k: (i, col))


def _rrow(w=D):
    return pl.BlockSpec((1, w), lambda i, k: (0, 0))


def down_final(f, wdn, x1, g2, fg, tgt, *, name, tm=512):
    def fn(i, d, in_refs, out_refs):
        x1_ref, g2_ref, fg_ref, t_ref = in_refs
        d_ref, dx_ref, dd_ref, dfg_ref, loss_ref = out_refs
        d_ref[...] = d
        xv = x1_ref[...] + g2_ref[...] * d
        r = lax.rsqrt(jnp.mean(xv * xv, axis=-1, keepdims=True) + EPS)
        xh = xv * r
        diff = xh * fg_ref[...] - t_ref[...]
        part = 0.5 * jnp.sum(jnp.mean(diff * diff, axis=-1, keepdims=True), axis=0, keepdims=True)
        dy = diff * (1.0 / D)
        a = dy * fg_ref[...]
        dx = r * (a - xh * jnp.mean(a * xh, axis=-1, keepdims=True))
        dx_ref[...] = dx
        dd_ref[...] = (dx * g2_ref[...]).astype(BF)
        dfg = jnp.sum(dy * xh, axis=0, keepdims=True)

        @pl.when(i == 0)
        def _():
            dfg_ref[...] = dfg
            loss_ref[...] = jnp.broadcast_to(part, (1, 128))

        @pl.when(i > 0)
        def _():
            dfg_ref[...] += dfg
            loss_ref[...] += jnp.broadcast_to(part, (1, 128))

    blk = _rblk(tm)
    return rows_call(
        dict(a=f, b=wdn, tk=DFF), [x1, g2, fg, tgt], [blk, _rrow(), _rrow(), blk],
        [jax.ShapeDtypeStruct((T, D), F32), jax.ShapeDtypeStruct((T, D), F32), jax.ShapeDtypeStruct((T, D), BF),
         jax.ShapeDtypeStruct((1, D), F32), jax.ShapeDtypeStruct((1, 128), F32)],
        [blk, blk, blk, _rrow(), _rrow(128)], fn, name=name, R=T, tm=tm)


def oproj_resid(merged, wo, x, gate, g, sc, sh, *, name, tm=512):
    def fn(i, a, in_refs, out_refs):
        x_ref, gate_ref, g_ref, sc_ref, sh_ref = in_refs
        a_ref, x1_ref, h_ref = out_refs
        a_ref[...] = a
        xv = x_ref[...] + gate_ref[...] * a
        x1_ref[...] = xv
        r = lax.rsqrt(jnp.mean(xv * xv, axis=-1, keepdims=True) + EPS)
        h_ref[...] = ((xv * r * g_ref[...]) * (1.0 + sc_ref[...]) + sh_ref[...]).astype(BF)

    blk = _rblk(tm)
    return rows_call(
        dict(a=merged, b=wo, tk=D), [x, gate, g, sc, sh], [blk, _rrow(), _rrow(), _rrow(), _rrow()],
        [jax.ShapeDtypeStruct((T, D), F32), jax.ShapeDtypeStruct((T, D), F32), jax.ShapeDtypeStruct((T, D), BF)],
        [blk, blk, blk], fn, name=name, R=T, tm=tm)


def oproj_dx_gate_bwd(da, wo, p, ya, yc, wao, wco, *, name, tm=512):
    kin = wao.shape[0]

    def fn(i, dm, in_refs, out_refs):
        ga_ref, gc_ref, ya_ref, yc_ref, wa_ref, wc_ref = in_refs
        dya_ref, dyc_ref, dp_ref, do_ref, dz_ref = out_refs
        sa, sc_ = jax.nn.sigmoid(ga_ref[...]), jax.nn.sigmoid(gc_ref[...])
        dya, dyc = (dm * sa).astype(BF), (dm * sc_).astype(BF)
        dya_ref[...] = dya
        dyc_ref[...] = dyc
        dp_ref[:, 0:D] = (dm * ya_ref[...] * (sa * (1.0 - sa))).astype(BF)
        dp_ref[:, D:2 * D] = (dm * yc_ref[...] * (sc_ * (1.0 - sc_))).astype(BF)
        nt = (((1,), (1,)), ((), ()))
        do_ref[...] = lax.dot_general(dya, wa_ref[...], nt, preferred_element_type=F32).astype(BF)
        dz_ref[...] = lax.dot_general(dyc, wc_ref[...], nt, preferred_element_type=F32)

    blk = _rblk(tm)
    sh = jax.ShapeDtypeStruct((T, D), BF)
    wsp = pl.BlockSpec((kin, D), lambda i, k: (0, 0))
    return rows_call(
        dict(a=da, b=wo, tb=True, tk=D), [p, p, ya, yc, wao, wco],
        [_rblk(tm, D, O_GA // D), _rblk(tm, D, O_GC // D), blk, blk, wsp, wsp],
        [sh, sh, jax.ShapeDtypeStruct((T, NIN), BF), jax.ShapeDtypeStruct((T, kin), BF), jax.ShapeDtypeStruct((T, kin), F32)],
        [blk, blk, _rblk(tm, 2 * D), _rblk(tm, kin), _rblk(tm, kin)], fn, name=name, R=T, tm=tm)


def normmod_bwd(x, dh, g, sc, dres, gsrc, gate, *, name, tm=512):
    R = x.shape[0]
    tm = min(tm, R)
    has_res = dres is not None
    fused = isinstance(dh, dict)
    if fused:
        tb, a_stack, tk = dh.get("tb", False), dh.get("a_stack", False), dh["tk"]
        K = 2 * dh["a"].shape[2] if a_stack else dh["a"].shape[1]
        nk = K // tk
        deps = [] if dh.get("dep") is None else [dh["dep"]]
        n_dh = 2 + len(deps)
    else:
        nk, n_dh = 1, 1

    def elementwise(i, dhv, x_ref, g_ref, sc_ref, res_refs, out_refs):
        xv = x_ref[...]
        r = lax.rsqrt(jnp.mean(xv * xv, axis=-1, keepdims=True) + EPS)
        xh = xv * r
        n = xh * g_ref[...]
        dn = dhv * (1.0 + sc_ref[...])
        a = dn * g_ref[...]
        rows = [jnp.sum(dhv, axis=0, keepdims=True), jnp.sum(dhv * n, axis=0, keepdims=True),
                jnp.sum(dn * xh, axis=0, keepdims=True)]
        if has_res:
            dres_ref, gsrc_ref, gate_ref = res_refs
            dx_ref, dxg_ref, st_ref = out_refs
            dr = dres_ref[...]
            dx = dr + r * (a - xh * jnp.mean(a * xh, axis=-1, keepdims=True))
            dx_ref[...] = dx
            dxg_ref[...] = (dx * gate_ref[...]).astype(BF)
            rows.append(jnp.sum(dr * gsrc_ref[...], axis=0, keepdims=True))
        else:
            st_ref, = out_refs
            rows.append(jnp.zeros((1, D), F32))

        @pl.when(i == 0)
        def _():
            for k, row in enumerate(rows):
                st_ref[k:k + 1, :] = row

        @pl.when(i > 0)
        def _():
            for k, row in enumerate(rows):
                st_ref[k:k + 1, :] += row

    def body(*refs):
        x_ref, dh_refs, g_ref, sc_ref = refs[0], refs[1:1 + n_dh], refs[1 + n_dh], refs[2 + n_dh]
        rest = refs[3 + n_dh:]
        res_refs, rest = (rest[:3], rest[3:]) if has_res else ((), rest)
        out_refs = rest[:3] if has_res else rest[:1]
        i = pl.program_id(0)
        if not fused:
            elementwise(i, dh_refs[0][...], x_ref, g_ref, sc_ref, res_refs, out_refs)
            return
        acc = rest[-1]
        k = pl.program_id(1)
        part = lax.dot_general(dh_refs[0][...].astype(BF), dh_refs[1][...].astype(BF),
                               (((1,), (1 if tb else 0,)), ((), ())), preferred_element_type=F32)
        if nk == 1:
            elementwise(i, part, x_ref, g_ref, sc_ref, res_refs, out_refs)
            return

        @pl.when(k == 0)
        def _():
            acc[...] = part

        @pl.when(k > 0)
        def _():
            acc[...] += part

        @pl.when(k == nk - 1)
        def _():
            elementwise(i, acc[...], x_ref, g_ref, sc_ref, res_refs, out_refs)

    rowb = lambda w: pl.BlockSpec((1, w), lambda i, *k: (0, 0))
    blk = pl.BlockSpec((tm, D), lambda i, *k: (i, 0))
    st_spec = pl.BlockSpec((4, D), lambda i, *k: (0, 0))
    st_shape = jax.ShapeDtypeStruct((4, D), F32)
    if fused:
        if a_stack:
            nhb = K // 2 // tk
            a_spec = pl.BlockSpec((None, tm, tk), lambda i, k: (k // nhb, i, k % nhb))
        else:
            a_spec = pl.BlockSpec((tm, tk), lambda i, k: (i, k))
        b_spec = pl.BlockSpec((D, tk), lambda i, k: (0, k)) if tb else pl.BlockSpec((tk, D), lambda i, k: (k, 0))
        dh_specs = [a_spec, b_spec] + [pl.BlockSpec(memory_space=pl.ANY)] * len(deps)
        dh_args = [dh["a"], dh["b"]] + deps
        grid, sem = (R // tm, nk), ("arbitrary", "arbitrary")
        scratch = [pltpu.VMEM((tm, D) if nk > 1 else (8, 128), F32)]
    else:
        dh_specs, dh_args, grid, sem, scratch = [blk], [dh], (R // tm,), ("arbitrary",), []
    cp = pltpu.CompilerParams(dimension_semantics=sem)
    if has_res:
        return pl.pallas_call(
            body, name=name, grid=grid, in_specs=[blk] + dh_specs + [rowb(D), rowb(D), blk, blk, rowb(D)],
            out_specs=[blk, blk, st_spec], scratch_shapes=scratch,
            out_shape=[jax.ShapeDtypeStruct((R, D), F32), jax.ShapeDtypeStruct((R, D), BF), st_shape],
            compiler_params=cp,
        )(x, *dh_args, g, sc, dres, gsrc, gate)
    return pl.pallas_call(
        body, name=name, grid=grid, in_specs=[blk] + dh_specs + [rowb(D), rowb(D)],
        out_specs=st_spec, out_shape=st_shape, scratch_shapes=scratch, compiler_params=cp,
    )(x, *dh_args, g, sc)


def ffn_act_bwd(u0, df, cw, cb, *, name, tc=128):
    nb = DFF // tc

    def body(u_ref, df_ref, wg_ref, wv_ref, bg_ref, bv_ref, du_ref, dw_ref, db_ref):
        wg = [wg_ref[k:k + 1, :] for k in range(3)]
        wv = [wv_ref[k:k + 1, :] for k in range(3)]
        bg, bv = bg_ref[...], bv_ref[...]

        def chunk(r0, first, last, acc):
            xg, xv = _ext_rows(u_ref.at[0], r0, first, last), _ext_rows(u_ref.at[1], r0, first, last)
            dfe = _ext_rows(df_ref, r0, first, last)
            xg_d, xg_u, xv_d, xv_u = _roll_dn(xg), _roll_up(xg), _roll_dn(xv), _roll_up(xv)
            ug = bg + xg_d * wg[0] + xg * wg[1] + xg_u * wg[2]
            uv = bv + xv_d * wv[0] + xv * wv[1] + xv_u * wv[2]
            sig = jax.nn.sigmoid(ug)
            dug = dfe * uv * (sig * (1.0 + ug * (1.0 - sig)))
            duv = dfe * (ug * sig)
            rows = _center_rows(r0, first, last)
            du_ref[0, rows, :] = (_roll_up(dug) * wg[0] + dug * wg[1] + _roll_dn(dug) * wg[2])[_CTR].astype(BF)
            du_ref[1, rows, :] = (_roll_up(duv) * wv[0] + duv * wv[1] + _roll_dn(duv) * wv[2])[_CTR].astype(BF)
            terms = [dug * xg_d, dug * xg, dug * xg_u, dug, duv * xv_d, duv * xv, duv * xv_u, duv]
            return tuple(a + jnp.sum(t[_CTR], axis=0, keepdims=True) for a, t in zip(acc, terms))

        acc = _row_chunks(T, chunk, tuple(jnp.zeros((1, tc), F32) for _ in range(8)))
        for k in range(3):
            dw_ref[0, k:k + 1, :] = acc[k]
            dw_ref[1, k:k + 1, :] = acc[4 + k]
        db_ref[0] = acc[3]
        db_ref[1] = acc[7]

    lo = lambda r: pl.BlockSpec((r, tc), lambda j: (0, j))
    hi = lambda r: pl.BlockSpec((r, tc), lambda j: (0, nb + j))
    st = lambda r: pl.BlockSpec((2, r, tc), lambda j: (0, 0, j))
    return pl.pallas_call(
        body, name=name, grid=(nb,),
        in_specs=[st(T), lo(T), lo(3), hi(3), lo(1), hi(1)],
        out_specs=[st(T), st(3), st(1)],
        out_shape=[jax.ShapeDtypeStruct((2, T, DFF), BF), jax.ShapeDtypeStruct((2, 3, DFF), F32),
                   jax.ShapeDtypeStruct((2, 1, DFF), F32)],
        compiler_params=pltpu.CompilerParams(dimension_semantics=("parallel",)),
    )(u0, df, cw, cw, cb, cb)


def convz_bwd(p, dz, cw, cb, dp, *, name):
    o0 = O_CV // (3 * CVB)

    def body(p_ref, dz_ref, w_ref, bias_ref, dp_in, dp_ref, dw_ref, dbias_ref):
        xv, bv, cv = p_ref[:, 0:CVB], p_ref[:, CVB:2 * CVB], p_ref[:, 2 * CVB:3 * CVB]
        ci = cv * xv
        dwc = _conv(ci, w_ref, bias_ref)
        dzv = dz_ref[...]
        ddw = dzv * bv
        dci = _conv_t(ddw, w_ref)
        dp_ref[:, 0:CVB] = (dci * cv).astype(BF)
        dp_ref[:, CVB:2 * CVB] = (dzv * dwc).astype(BF)
        dp_ref[:, 2 * CVB:3 * CVB] = (dci * xv).astype(BF)
        _conv_wgrad(dw_ref, ddw, ci)
        dbias_ref[...] = jnp.sum(ddw, axis=0, keepdims=True)

    own = lambda r: pl.BlockSpec((r, CVB), lambda j: (0, j))
    return pl.pallas_call(
        body, name=name, grid=(CONV // CVB,),
        in_specs=[pl.BlockSpec((T, 3 * CVB), lambda j: (0, o0 + j)), own(T), own(3), own(1),
                  pl.BlockSpec(memory_space=pl.ANY)],
        out_specs=[pl.BlockSpec((T, 3 * CVB), lambda j: (0, o0 + j)), own(3), own(1)],
        out_shape=[jax.ShapeDtypeStruct((T, NIN), BF), jax.ShapeDtypeStruct((3, CONV), F32),
                   jax.ShapeDtypeStruct((1, CONV), F32)],
        input_output_aliases={4: 0},
        compiler_params=pltpu.CompilerParams(dimension_semantics=("parallel",)),
    )(p, dz, cw, cb, dp)


def attn_bwd(q, k, v, do, o, lse, dep, *, name, tq=1024, kc=768):
    NKC, KC = TKV // kc, kc
    deps = [] if dep is None else [dep]

    def body(q_ref, k_ref, v_ref, do_ref, o_ref, lse_ref, *rest):
        dq_ref, dk_ref, dv_ref = rest[len(deps):]
        h, i = pl.program_id(0), pl.program_id(1)

        @pl.when(i == 0)
        def _():
            dk_ref[...] = jnp.zeros_like(dk_ref)

        @pl.when((i == 0) & (h % 2 == 0))
        def _():
            dv_ref[...] = jnp.zeros_like(dv_ref)

        qv = q_ref[...]
        dom = jnp.where(_head_mask(h), do_ref[...], jnp.zeros_like(do_ref[...]))
        delta = jnp.sum(dom.astype(F32) * o_ref[...].astype(F32), axis=-1, keepdims=True)
        lse = lse_ref[:, 0:1]
        dq = jnp.zeros((tq, HP), F32)
        for c in range(NKC):
            cols = slice(c * KC, (c + 1) * KC)
            s = lax.dot_general(qv, k_ref[cols, :], (((1,), (1,)), ((), ())),
                                preferred_element_type=F32) * (SCALE * LOG2E)
            pr = jnp.exp2(s - lse)
            dp = lax.dot_general(dom, v_ref[cols, :], (((1,), (1,)), ((), ())), preferred_element_type=F32)
            ds = (pr * (dp - delta) * SCALE).astype(BF)
            dq = dq + jnp.dot(ds, k_ref[cols, :], preferred_element_type=F32)
            dk_ref[cols, :] += lax.dot_general(ds, qv, (((0,), (0,)), ((), ())), preferred_element_type=F32)
            dv_ref[cols, :] += lax.dot_general(pr.astype(BF), dom, (((0,), (0,)), ((), ())), preferred_element_type=F32)
        dq_ref[...] = dq

    return pl.pallas_call(
        body, name=name, grid=(NH, T // tq),
        in_specs=[pl.BlockSpec((tq, HP), lambda h, i: (i, h)), pl.BlockSpec((TKV, HP), lambda h, i: (0, h)),
                  pl.BlockSpec((TKV, 2 * DV), lambda h, i: (0, h // 2)), pl.BlockSpec((tq, 2 * DV), lambda h, i: (i, h // 2)),
                  pl.BlockSpec((tq, 2 * DV), lambda h, i: (i, h // 2)), pl.BlockSpec((tq, HP), lambda h, i: (i, h)),
                  *([pl.BlockSpec(memory_space=pl.ANY)] * len(deps))],
        out_specs=[pl.BlockSpec((tq, HP), lambda h, i: (i, h)), pl.BlockSpec((TKV, HP), lambda h, i: (0, h)),
                   pl.BlockSpec((TKV, 2 * DV), lambda h, i: (0, h // 2))],
        out_shape=[jax.ShapeDtypeStruct((T, NH * HP), F32), jax.ShapeDtypeStruct((TKV, NH * HP), F32),
                   jax.ShapeDtypeStruct((TKV, NH * DV), F32)],
        compiler_params=pltpu.CompilerParams(dimension_semantics=("arbitrary", "arbitrary")),
    )(q, k, v, do, o, lse, *deps)


def qprep_bwd(p, dq, qg, wq2, cq_t, sq_t, dp, *, name, tm=256):
    qcol = O_Q // 512

    def body(p_ref, dq_ref, g_ref, w_ref, c_ref, s_ref, dp_in, dp_ref, dq2_ref, dg_ref):
        i = pl.program_id(0)
        dqv = dq_ref[...]
        cc = jnp.concatenate([c_ref[...]] * NH, axis=1)
        ss = jnp.concatenate([s_ref[...]] * NH, axis=1)
        dq2 = jnp.concatenate([dqv * cc, dqv * ss], axis=1).astype(BF)
        dq2_ref[...] = dq2
        dcq = lax.dot_general(dq2, w_ref[...], (((1,), (1,)), ((), ())), preferred_element_type=F32)
        pq = p_ref[...]
        r = lax.rsqrt(jnp.sum(pq * pq, axis=-1, keepdims=True) * (1.0 / QL) + EPS)
        xh = pq * r
        a = dcq * g_ref[...]
        dp_ref[...] = (r * (a - xh * (jnp.sum(a * xh, axis=-1, keepdims=True) * (1.0 / QL)))).astype(BF)
        dg = jnp.sum(dcq * xh, axis=0, keepdims=True)

        @pl.when(i == 0)
        def _():
            dg_ref[...] = dg

        @pl.when(i > 0)
        def _():
            dg_ref[...] += dg

    return pl.pallas_call(
        body, name=name, grid=(T // tm,),
        in_specs=[pl.BlockSpec((tm, 512), lambda i: (i, qcol)), pl.BlockSpec((tm, NH * HP), lambda i: (i, 0)), _row(512),
                  pl.BlockSpec((512, 2 * NH * HP), lambda i: (0, 0)),
                  pl.BlockSpec((tm, HP), lambda i: (i, 0)), pl.BlockSpec((tm, HP), lambda i: (i, 0)),
                  pl.BlockSpec(memory_space=pl.ANY)],
        out_specs=[pl.BlockSpec((tm, 512), lambda i: (i, qcol)), pl.BlockSpec((tm, 2 * NH * HP), lambda i: (i, 0)), _row(512)],
        out_shape=[jax.ShapeDtypeStruct((T, NIN), BF), jax.ShapeDtypeStruct((T, 2 * NH * HP), BF),
                   jax.ShapeDtypeStruct((1, 512), F32)],
        input_output_aliases={6: 0},
        compiler_params=pltpu.CompilerParams(dimension_semantics=("arbitrary",)),
    )(p, dq, qg, wq2, cq_t, sq_t, dp)


def kvprep_bwd(pc, p, dk, dv, kvg, wkv2, ck, sk, dp, *, name, tm=256):
    assert tm == TC
    nb = TKV // tm
    kvcol = O_KV // 512

    def body(pc_ref, p_ref, dk_ref, dv_ref, g_ref, w_ref, ck_ref, sk_ref, dp_in, dp_ref, dpc_ref, dkv2_ref, dg_ref):
        i = pl.program_id(0)
        t = jnp.where(i == NLAT, pc_ref[...], p_ref[...])
        pk = t[:, :KVL]
        r = lax.rsqrt(jnp.mean(pk * pk, axis=-1, keepdims=True) + EPS)
        xh = pk * r
        dkv = dk_ref[...]
        dkv2 = jnp.concatenate([dkv, dv_ref[...]], axis=1).astype(BF)
        dkv2_ref[...] = dkv2
        dckv = lax.dot_general(dkv2, w_ref[...], (((1,), (1,)), ((), ())), preferred_element_type=F32)
        a = dckv * g_ref[...]
        dpk = r * (a - xh * jnp.mean(a * xh, axis=-1, keepdims=True))
        dkr = dkv[:, 0:HP]
        for hh in range(1, NH):
            dkr = dkr + dkv[:, hh * HP:(hh + 1) * HP]
        res = jnp.concatenate([dpk, dkr * ck_ref[...], dkr * sk_ref[...]], axis=1).astype(BF)
        dg = jnp.sum(dckv * xh, axis=0, keepdims=True)

        @pl.when(i == 0)
        def _():
            dg_ref[...] = dg

        @pl.when(i > 0)
        def _():
            dg_ref[...] += dg

        @pl.when(i < NLAT)
        def _():
            dp_ref[...] = res

        @pl.when(i == NLAT)
        def _():
            dpc_ref[...] = res

    rb = lambda w: pl.BlockSpec((tm, w), lambda i: (i, 0))
    return pl.pallas_call(
        body, name=name, grid=(nb,),
        in_specs=[pl.BlockSpec((tm, 512), lambda i: (0, 0)),
                  pl.BlockSpec((tm, 512), lambda i: (jnp.minimum(i, NLAT - 1), kvcol)),
                  rb(NH * HP), rb(NH * DV), _row(KVL), pl.BlockSpec((KVL, NH * HP + NH * DV), lambda i: (0, 0)),
                  rb(HP), rb(HP), pl.BlockSpec(memory_space=pl.ANY)],
        out_specs=[pl.BlockSpec((tm, 512), lambda i: (jnp.minimum(i, NLAT - 1), kvcol)),
                   pl.BlockSpec((tm, 512), lambda i: (0, 0)), rb(NH * HP + NH * DV), _row(KVL)],
        out_shape=[jax.ShapeDtypeStruct((T, NIN), BF), jax.ShapeDtypeStruct((TC, 512), BF),
                   jax.ShapeDtypeStruct((TKV, NH * HP + NH * DV), BF), jax.ShapeDtypeStruct((1, KVL), F32)],
        input_output_aliases={8: 0},
        compiler_params=pltpu.CompilerParams(dimension_semantics=("arbitrary",)),
    )(pc, p, dk, dv, kvg, wkv2, ck, sk, dp)


def _pieces(src, width, n):
    out, c = [], src
    while c < src + width:
        k = c // n
        w = min(src + width, (k + 1) * n) - c
        out.append((k, c - k * n, c - src, w))
        c += w
    return out


def _win_moves():
    mv = [(2208, 1024, O_GA), (3232, 1024, O_GC), (0, KVL, O_KV), (256, DR, O_KV + KVL + DN), (288, QL, O_Q)]
    mv += [(256 + _swap_start(g), 8, O_KV + KVL + HP + DN + 8 * g) for g in range(4)]
    for j in range(CONV // CVB):
        base = O_CV + 3 * CVB * j
        mv += [(672 + CVB * j, CVB, base), (1184 + CVB * j, CVB, base + CVB), (1696 + CVB * j, CVB, base + 2 * CVB)]
    return mv


_WIN_ZERO = [(O_KV + KVL, DN), (O_KV + KVL + DN + DR, HP - DN - DR), (O_KV + KVL + HP, DN),
             (O_KV + KVL + HP + DN + DR, HP - DN - DR), (O_Q + QL, 512 - QL)]


def build_win(g, *, name, tm=256):
    def body(g_ref, o_ref):
        for src, w, dst in _win_moves():
            for k, a, off, pw in _pieces(src, w, SH_IN):
                o_ref[:, dst + off:dst + off + pw] = g_ref[k, :, a:a + pw]
        for c0, w in _WIN_ZERO:
            o_ref[:, c0:c0 + w] = jnp.zeros((tm, w), o_ref.dtype)

    return pl.pallas_call(
        body, name=name, grid=(D // tm,), in_specs=[pl.BlockSpec((NDEV, tm, SH_IN), lambda i: (0, i, 0))],
        out_specs=pl.BlockSpec((tm, NIN), lambda i: (i, 0)), out_shape=jax.ShapeDtypeStruct((D, NIN), g.dtype),
        compiler_params=pltpu.CompilerParams(dimension_semantics=("parallel",)),
    )(g)


def shard_win_grad(dwt, dwct, *, name, col0=0, tc=256):
    n = dwt.shape[1]

    def body(dw_ref, dwc_ref, o_ref, kvs):
        kvs[...] = dw_ref[O_KV:O_KV + 512, :] + dwc_ref[...]

        def src(row, w):
            if O_KV <= row < O_KV + 512:
                return kvs[row - O_KV:row - O_KV + w, :]
            return dw_ref[row:row + w, :]

        for s, w, dst in _win_moves():
            if w == 8 or s == 256:
                continue
            for k, a, off, pw in _pieces(s, w, SH_IN):
                o_ref[k, a:a + pw, :] = src(dst + off, pw).astype(o_ref.dtype)
        for g in range(4):
            val = src(O_KV + KVL + DN + 8 * g, 8) + src(O_KV + KVL + HP + DN + _swap_start(g), 8)
            o_ref[0, 256 + 8 * g:256 + 8 * g + 8, :] = val.astype(o_ref.dtype)

    return pl.pallas_call(
        body, name=name, grid=(n // tc,),
        in_specs=[pl.BlockSpec((NIN, tc), lambda j: (0, j)), pl.BlockSpec((512, tc), lambda j: (0, j + col0 // tc))],
        out_specs=pl.BlockSpec((NDEV, SH_IN, tc), lambda j: (0, 0, j)),
        out_shape=jax.ShapeDtypeStruct((NDEV, SH_IN, n), BF),
        scratch_shapes=[pltpu.VMEM((512, tc), F32)],
        compiler_params=pltpu.CompilerParams(dimension_semantics=("parallel",)),
    )(dwt, dwct)


def _eye(n, m):
    return (lax.broadcasted_iota(jnp.int32, (n, m), 0) == lax.broadcasted_iota(jnp.int32, (n, m), 1)).astype(BF)


_NT = (((1,), (1,)), ((), ()))


def build_wq_wkv(gq, gkv, *, name):
    def body(gq_ref, gkv_ref, q_ref, kv_ref):
        q_ref[...] = jnp.zeros_like(q_ref)
        kv_ref[...] = jnp.zeros_like(kv_ref)
        eye = _eye(QL, QL)
        for h in range(NH):
            qh = lax.dot_general(eye, gq_ref[h], _NT, preferred_element_type=F32).astype(q_ref.dtype)
            q_ref[0:QL, h * HP:h * HP + DN + DR] = qh
            for g in range(4):
                c0 = NH * HP + h * HP + DN + 8 * g
                q_ref[0:QL, c0:c0 + 8] = qh[:, DN + _swap_start(g):DN + _swap_start(g) + 8]
            kv_ref[:, h * HP:h * HP + DN] = gkv_ref[h, :, 0:DN]
            kv_ref[:, NH * HP + h * DV:NH * HP + (h + 1) * DV] = gkv_ref[h, :, DN:DN + DV]

    vm = pl.BlockSpec(memory_space=pltpu.VMEM)
    return pl.pallas_call(
        body, name=name, in_specs=[vm, vm], out_specs=[vm, vm],
        out_shape=[jax.ShapeDtypeStruct((512, 2 * NH * HP), gq.dtype), jax.ShapeDtypeStruct((KVL, NH * HP + NH * DV), gq.dtype)],
    )(gq, gkv)


def shard_wq_wkv_grad(dwq2, dwkv2, *, name):
    def body(q_ref, kv_ref, gq_ref, gkv_ref, xs):
        xs[...] = jnp.zeros_like(xs)
        eye = _eye(DN + DR, HP)
        for h in range(NH):
            xs[:, 0:DN] = q_ref[0:QL, h * HP:h * HP + DN].astype(BF)
            for g in range(4):
                a = q_ref[0:QL, h * HP + DN + 8 * g:h * HP + DN + 8 * g + 8]
                c0 = NH * HP + h * HP + DN + _swap_start(g)
                xs[:, DN + 8 * g:DN + 8 * g + 8] = (a + q_ref[0:QL, c0:c0 + 8]).astype(BF)
            gq_ref[h] = lax.dot_general(eye, xs[...], _NT, preferred_element_type=F32).astype(BF)
            gkv_ref[h, :, 0:DN] = kv_ref[:, h * HP:h * HP + DN].astype(BF)
            gkv_ref[h, :, DN:DN + DV] = kv_ref[:, NH * HP + h * DV:NH * HP + (h + 1) * DV].astype(BF)

    vm = pl.BlockSpec(memory_space=pltpu.VMEM)
    return pl.pallas_call(
        body, name=name, in_specs=[vm, vm], out_specs=[vm, vm],
        out_shape=[jax.ShapeDtypeStruct((NDEV, DN + DR, QL), BF), jax.ShapeDtypeStruct((NDEV, KVL, DN + DV), BF)],
        scratch_shapes=[pltpu.VMEM((QL, HP), BF)],
    )(dwq2, dwkv2)


def unshard_cols(g, *, name, tm=256):
    _, K, n = g.shape
    tm = _pick(K, tm, 16)

    def body(g_ref, o_ref):
        for k in range(NDEV):
            o_ref[:, k * n:(k + 1) * n] = g_ref[k]

    return pl.pallas_call(
        body, name=name, grid=(K // tm,), in_specs=[pl.BlockSpec((NDEV, tm, n), lambda i: (0, i, 0))],
        out_specs=pl.BlockSpec((tm, NDEV * n), lambda i: (i, 0)), out_shape=jax.ShapeDtypeStruct((K, NDEV * n), g.dtype),
        compiler_params=pltpu.CompilerParams(dimension_semantics=("parallel",)),
    )(g)


def shard_cols(w, *, name, tm=256):
    K, n8 = w.shape
    n = n8 // NDEV
    tm = _pick(K, tm, 16)

    def body(w_ref, o_ref):
        for k in range(NDEV):
            o_ref[k] = w_ref[:, k * n:(k + 1) * n]

    return pl.pallas_call(
        body, name=name, grid=(K // tm,), in_specs=[pl.BlockSpec((tm, n8), lambda i: (i, 0))],
        out_specs=pl.BlockSpec((NDEV, tm, n), lambda i: (0, i, 0)), out_shape=jax.ShapeDtypeStruct((NDEV, K, n), w.dtype),
        compiler_params=pltpu.CompilerParams(dimension_semantics=("parallel",)),
    )(w)


def _rope_tables():
    t = np.arange(T)
    row = (t // GRID_W).astype(np.float32)
    col = (t % GRID_W).astype(np.float32)
    axis_dim = DR // 2
    inv = (np.float32(ROPE_THETA) ** (-np.arange(0, axis_dim, 2, dtype=np.float32) / np.float32(axis_dim))).astype(np.float32)
    ar, ac = (row[:, None] * inv).astype(np.float32), (col[:, None] * inv).astype(np.float32)
    cosv = np.concatenate([np.cos(ar), np.cos(ar), np.cos(ac), np.cos(ac)], axis=1).astype(np.float32)
    sinv = np.concatenate([-np.sin(ar), np.sin(ar), -np.sin(ac), np.sin(ac)], axis=1).astype(np.float32)
    ck = np.zeros((TKV, HP), np.float32)
    sk = np.zeros((TKV, HP), np.float32)
    ck[T:, DN:DN + DR] = 1.0
    ck[:T, DN:DN + DR] = cosv
    sk[:T, DN:DN + DR] = sinv
    cq = np.zeros((T, HP), np.float32)
    cq[:, :DN] = 1.0
    cq[:, DN:DN + DR] = cosv
    return jnp.asarray(ck), jnp.asarray(sk), jnp.asarray(cq), jnp.asarray(sk[:T])


def _local_step(x, ctx, tgt, mod_lat, mod_ctx, n1g, qg, kvg, n2g, fg, conv_w, conv_b, ffn_w, ffn_b, get_w, put_g, dep0):
    sh1, sc1, g1, sh2, sc2, g2 = [mod_lat[:, i * D:(i + 1) * D] for i in range(6)]
    csh1, csc1 = mod_ctx[:, 0:D], mod_ctx[:, D:2 * D]
    ck, sk, cq_t, sq_t = _rope_tables()
    qg_p = jnp.pad(qg, ((0, 0), (0, 512 - QL)))

    hcat = normmod_cat(ctx, x, n1g, csc1, csh1, sc1, sh1, dep0, name="normmod1")
    win = get_w("in", hcat)
    p = mm(hcat, win, M=T, tn=768, name="in_proj")
    pc = mm(hcat, win, M=TC, N=512, a_off=(T, 0), b_off=(0, O_KV), name="in_proj_ctx")
    wq2, wkv2, wao, wco, wo = get_w("mid", p)
    kh, vh, ckv = kvprep(pc, p, kvg, wkv2, ck, sk, name="kvprep")
    qr, cq = qprep(p, qg_p, wq2, cq_t, sq_t, name="qprep")
    o, lse = attn_fwd(qr, kh, vh, name="attn_fwd")
    z = convz(p, conv_w, conv_b, name="convz")
    ya, yc, merged = out_proj_merge(o, wao, z, wco, p, name="attn_conv_out_gate_merge")
    a_out, x1, h2 = oproj_resid(merged, wo, x, g1, n2g, sc2, sh2, name="o_proj_resid_normmod2")
    wup = get_w("up", h2)
    u0 = mm(h2, wup, tb=True, o_stack=True, tn=1408, name="up_proj")
    f = ffn_act(u0, ffn_w, ffn_b, name="ffn_act")
    wdn = get_w("down", f)
    dn, dx2, dd, dfg, loss = down_final(f, wdn, x1, g2, fg, tgt, name="down_proj_final_loss")

    df = mm(dd, wdn, tb=True, tn=1408, name="down_proj_dx")
    dwdn = mm(f, dd, ta=True, out_dtype=BF, tm=1408, name="down_proj_dw")
    du0, dffn_w, dffn_b = ffn_act_bwd(u0, df, ffn_w, ffn_b, name="ffn_act_bwd")
    dwup = mm(du0, h2, ta=True, a_stack=True, out_dtype=BF, tm=1408, name="up_proj_dw")
    tok = put_g("ffn", dict(dwup=dwup, dwdn=dwdn))
    dx1, da, st2 = normmod_bwd(x1, dict(a=du0, b=wup, a_stack=True, tk=DFF, dep=tok), n2g, sc2, dx2, dn, g1,
                               name="up_proj_dx_normmod2_bwd")

    dwo = mm(merged, da, ta=True, out_dtype=BF, tn=512, name="o_proj_dw")
    dya, dyc, dp, do, dz = oproj_dx_gate_bwd(da, wo, p, ya, yc, wao, wco, name="o_proj_dx_gate_merge_bwd")
    dwao = mm(o, dya, ta=True, out_dtype=BF, tn=512, name="attn_out_dw")
    dwco = mm(z, dyc, ta=True, out_dtype=BF, tn=512, name="conv_out_dw")
    tok = put_g("mid", dict(dwao=dwao, dwco=dwco, dwo=dwo))
    dp, dconv_w, dconv_b = convz_bwd(p, dz, conv_w, conv_b, dp, name="convz_bwd")
    dq, dk, dv = attn_bwd(qr, kh, vh, do, o, lse, tok, name="attn_bwd")
    dp, dq2, dqg = qprep_bwd(p, dq, qg_p, wq2, cq_t, sq_t, dp, name="qprep_bwd")
    dp, dpc, dkv2, dkvg = kvprep_bwd(pc, p, dk, dv, kvg, wkv2, ck, sk, dp, name="kvprep_bwd")

    dwin_c = mm(dpc, hcat, ta=True, K=TC, b_off=(T, 0), name="in_proj_ctx_dw")
    tok = None
    for half in range(2):
        dwin = mm(dp, hcat, ta=True, K=T, N=D // 2, b_off=(0, half * (D // 2)), tm=1536, dep=tok,
                  name=f"in_proj_dw_{half}")
        tok = put_g(f"in{half}", dict(dwin=dwin, dwin_c=dwin_c, col0=half * (D // 2)))
    dwq2 = mm(cq, dq2, ta=True, dep=tok, name="q_up_dw")
    dwkv2 = mm(ckv, dkv2, ta=True, dep=tok, name="kv_up_dw")
    tok = put_g("qkv", dict(dwq2=dwq2, dwkv2=dwkv2))
    dhc = mm(dpc, win, tb=True, N=D, K=512, b_off=(0, O_KV), dep=tok, name="in_proj_ctx_dx")
    dx, _, st1 = normmod_bwd(x, dict(a=dp, b=win, tb=True, tk=NIN, dep=tok), n1g, sc1, dx1, a_out, g1,
                             name="in_proj_dx_normmod1_bwd")
    stc = normmod_bwd(ctx, dhc, n1g, csc1, None, None, None, name="normmod1_ctx_bwd")

    return dict(loss=loss, dx=dx, st1=st1, st2=st2, stc=stc, dqg=dqg, dkvg=dkvg, dfg=dfg,
                dconv_w=dconv_w, dconv_b=dconv_b, dffn_w=dffn_w, dffn_b=dffn_b)


def _me():
    x, y, c = lax.axis_index("x"), lax.axis_index("y"), lax.axis_index("c")
    return x, y, c, 4 * x + 2 * y + c


def _peer(x, y, c, k):
    px = 1 - x if k & 4 else x
    py = 1 - y if k & 2 else y
    pc = 1 - c if k & 1 else c
    return (px, py, pc), 4 * px + 2 * py + pc


def _exchange_tiles(src_of_peer, buf, send_sem, recv_sem):
    x, y, c, me = _me()
    for k in range(1, NDEV):
        dev, lin = _peer(x, y, c, k)
        pltpu.make_async_remote_copy(src_ref=src_of_peer(lin), dst_ref=buf.at[me], send_sem=send_sem, recv_sem=recv_sem,
                                     device_id=dev, device_id_type=MESH).start()
    seven = buf.at[pl.ds(0, NDEV - 1)]
    pltpu.make_async_remote_copy(src_ref=seven, dst_ref=seven, send_sem=send_sem, recv_sem=recv_sem,
                                 device_id=(x, y, c), device_id_type=MESH).wait()


def _silu(z):
    return z * jax.nn.sigmoid(z)


def ada_fwd(c, c_ctx, ffn_w, conv_w, w_shard, b_ada, deps, *, name):
    nsh, nf, nc = w_shard.shape[1], ffn_w.shape[2], conv_w.shape[2]
    deps = [d for d in deps if d is not None]

    def body(c_ref, cc_ref, fw_ref, cw_ref, w_ref, b_ref, *rest):
        s_ref, ml_ref, mc_ref, fwf_ref, cwf_ref, m_ref, mine, res, sems = rest[len(deps):]
        x, y, c, me = _me()
        mine[...] = jnp.zeros_like(mine)
        mine[0:1, :] = _silu(c_ref[...])
        mine[1:2, :] = _silu(cc_ref[...])
        for k in range(3):
            mine[2 + k:3 + k, 0:fw_ref.shape[2]] = fw_ref[k]
            mine[5 + k:6 + k, 0:cw_ref.shape[2]] = cw_ref[k]
        s_ref[me] = mine[...]
        _exchange_tiles(lambda lin: mine, s_ref, sems.at[0], sems.at[1])
        sall = s_ref[...].reshape(NDEV * 8, D).astype(BF)
        r = jnp.dot(sall, w_ref[...].astype(BF), preferred_element_type=F32) + b_ref[me]
        res[...] = r.reshape(NDEV, 8, nsh)
        m_ref[me] = res[me]
        _exchange_tiles(lambda lin: res.at[lin], m_ref, sems.at[2], sems.at[3])
        for j in range(NDEV):
            ml_ref[:, j * nsh:(j + 1) * nsh] = m_ref[j, 0:1, :]
            mc_ref[:, j * nsh:(j + 1) * nsh] = m_ref[j, 1:2, :]
            fwf_ref[:, j * nf:(j + 1) * nf] = s_ref[j, 2:5, 0:nf]
            cwf_ref[:, j * nc:(j + 1) * nc] = s_ref[j, 5:8, 0:nc]

    vm = pl.BlockSpec(memory_space=pltpu.VMEM)
    return pl.pallas_call(
        body, name=name, in_specs=[vm] * 6 + [pl.BlockSpec(memory_space=pl.ANY)] * len(deps), out_specs=[vm] * 5,
        out_shape=[jax.ShapeDtypeStruct((NDEV, 8, D), F32), jax.ShapeDtypeStruct((1, NDEV * nsh), F32),
                   jax.ShapeDtypeStruct((1, NDEV * nsh), F32), jax.ShapeDtypeStruct((3, NDEV * nf), F32),
                   jax.ShapeDtypeStruct((3, NDEV * nc), F32)],
        scratch_shapes=[pltpu.VMEM((NDEV, 8, nsh), F32), pltpu.VMEM((8, D), F32), pltpu.VMEM((NDEV, 8, nsh), F32),
                        pltpu.SemaphoreType.DMA((4,))],
    )(c, c_ctx, ffn_w, conv_w, w_shard, b_ada, *deps)


P_DML, P_DMC, P_N1, P_QG, P_KVG, P_CB, P_N2, P_FB, P_FG, P_CW, P_FW, P_LOSS, P_ROWS = 0, 8, 16, 17, 18, 19, 20, 21, 27, 28, 31, 49, 56
NSH = 6 * D // NDEV
FROWS = 3


def pack_small(r, *, name):
    ins = [r["st1"], r["st2"], r["stc"], r["dqg"], r["dkvg"], r["dconv_b"], r["dffn_b"], r["dfg"], r["dconv_w"],
           r["dffn_w"], r["loss"]]

    def put_wide(p, row0, row, n):
        for j in range(-(-n // D)):
            w = min(D, n - j * D)
            p[row0 + j:row0 + j + 1, 0:w] = row[:, j * D:j * D + w]

    def body(st1, st2, stc, qg, kvg, cb, fb, fg, cw, fw, loss, p):
        p[...] = jnp.zeros_like(p)
        lat = (st1.at[0:1], st1.at[1:2], st1.at[3:4], st2.at[0:1], st2.at[1:2], st2.at[3:4])
        ctx = (stc.at[0:1], stc.at[1:2])
        for j in range(NDEV):
            done = 0
            while done < NSH:
                q, off = divmod(j * NSH + done, D)
                w = min(D - off, NSH - done)
                p[P_DML + j:P_DML + j + 1, done:done + w] = lat[q][:, off:off + w]
                if q < len(ctx):
                    p[P_DMC + j:P_DMC + j + 1, done:done + w] = ctx[q][:, off:off + w]
                done += w
        p[P_N1:P_N1 + 1, :] = st1[2:3, :] + stc[2:3, :]
        p[P_N2:P_N2 + 1, :] = st2[2:3, :]
        put_wide(p, P_QG, qg, 512)
        put_wide(p, P_KVG, kvg, KVL)
        put_wide(p, P_CB, cb, CONV)
        put_wide(p, P_FG, fg, D)
        put_wide(p, P_LOSS, loss, 128)
        for s in range(2):
            put_wide(p, P_FB + FROWS * s, fb.at[s], DFF)
        for k in range(3):
            put_wide(p, P_CW + k, cw.at[k:k + 1], CONV)
            for s in range(2):
                put_wide(p, P_FW + FROWS * (2 * k + s), fw.at[s, k:k + 1], DFF)

    vm = pl.BlockSpec(memory_space=pltpu.VMEM)
    return pl.pallas_call(
        body, name=name, in_specs=[vm] * len(ins), out_specs=vm, out_shape=jax.ShapeDtypeStruct((P_ROWS, D), F32),
    )(*ins)


def sum_slots(a, *, name):
    rows = dict(norm1_g=(P_N1, D), q_norm_g=(P_QG, QL), kv_norm_g=(P_KVG, KVL), conv_b=(P_CB, CONV), norm2_g=(P_N2, D),
                final_g=(P_FG, D))

    def body(a_ref, sum_ref, *out):
        acc = a_ref[0]
        for k in range(1, NDEV):
            acc = acc + a_ref[k]
        sum_ref[...] = acc
        for ref, (row, n) in zip(out, rows.values()):
            ref[...] = sum_ref[row:row + 1, 0:n]
        fb, bada = out[len(rows):]
        for s in range(2):
            for j in range(FROWS):
                w = min(D, DFF - j * D)
                row = P_FB + FROWS * s + j
                fb[:, s * DFF + j * D:s * DFF + j * D + w] = sum_ref[row:row + 1, 0:w]
        for j in range(NDEV):
            bada[:, j * NSH:(j + 1) * NSH] = (sum_ref[P_DML + j:P_DML + j + 1, 0:NSH]
                                              + sum_ref[P_DMC + j:P_DMC + j + 1, 0:NSH])

    vm = pl.BlockSpec(memory_space=pltpu.VMEM)
    widths = [n for _, n in rows.values()] + [2 * DFF, 6 * D]
    outs = pl.pallas_call(
        body, name=name, in_specs=[vm], out_specs=[vm] * (1 + len(widths)),
        out_shape=[jax.ShapeDtypeStruct(a.shape[1:], F32)] + [jax.ShapeDtypeStruct((1, n), F32) for n in widths])(a)
    return outs[0], dict(zip(list(rows) + ["ffn_conv_b", "b_ada"], outs[1:]))


def ada_bwd(s_all, a_all, a_sum, w_shard, c_ctx, *, name):
    nsh = w_shard.shape[1]
    assert nsh == NSH

    def body(s_ref, a_ref, sum_ref, w_ref, c_ref, dw_ref, gc_ref, s16, dm16, part, buf, sems):
        x, y, c, me = _me()
        s16[...] = jnp.zeros_like(s16)
        dm16[...] = jnp.zeros_like(dm16)
        for k in range(NDEV):
            s16[k:k + 1, :] = s_ref[k, 0:1, :]
            dm16[k:k + 1, :] = a_ref[k, pl.ds(P_DML + me, 1), 0:nsh]
        s16[8:9, :] = s_ref[0, 1:2, :]
        dm16[8:9, :] = sum_ref[pl.ds(P_DMC + me, 1), 0:nsh]
        dw_ref[...] = lax.dot_general(s16[...].astype(BF), dm16[...].astype(BF), (((0,), (0,)), ((), ())),
                                      preferred_element_type=F32)
        part[...] = lax.dot_general(dm16[8:16, :].astype(BF), w_ref[...].astype(BF), (((1,), (1,)), ((), ())),
                                    preferred_element_type=F32)
        buf[me] = part[...]
        _exchange_tiles(lambda lin: part, buf, sems.at[0], sems.at[1])
        acc = buf[0]
        for k in range(1, NDEV):
            acc = acc + buf[k]
        z = c_ref[...]
        sg = jax.nn.sigmoid(z)
        gc_ref[...] = acc * (sg * (1.0 + z * (1.0 - sg)))

    vm = pl.BlockSpec(memory_space=pltpu.VMEM)
    return pl.pallas_call(
        body, name=name, in_specs=[vm] * 5, out_specs=[vm, vm],
        out_shape=[jax.ShapeDtypeStruct((D, nsh), F32), jax.ShapeDtypeStruct((8, D), F32)],
        scratch_shapes=[pltpu.VMEM((16, D), F32), pltpu.VMEM((16, nsh), F32), pltpu.VMEM((8, D), F32),
                        pltpu.VMEM((NDEV, 8, D), F32), pltpu.SemaphoreType.DMA((2,))],
    )(s_all, a_all, a_sum, w_shard, c_ctx)


HBM_SPEC = pl.BlockSpec(memory_space=pltpu.HBM)
SEM_SPEC = pl.BlockSpec(memory_space=pltpu.SEMAPHORE)
EFFECT = pltpu.SideEffectType.DATAFLOW_SIDE_EFFECTING


ALL_PEERS = tuple(range(1, NDEV))
FIRST_HOP = (1, 2, 4, 6)
RELAY = (2, 4, 6)


def _exchange_copies(srcs, lands, send, recv, per_peer, peers):
    x, y, c, me = _me()
    n = len(peers)
    cps = []
    for t in range(len(srcs)):
        for j, k in enumerate(peers):
            dev, lin = _peer(x, y, c, k)
            cps.append(pltpu.make_async_remote_copy(
                src_ref=srcs[t].at[lin] if per_peer else srcs[t], dst_ref=lands[t].at[me],
                send_sem=send.at[n * t + j], recv_sem=recv.at[n * t + j], device_id=dev, device_id_type=MESH))
    return cps


def _relay_copies(lands, send, recv):
    x, y, c, me = _me()
    n = len(RELAY)
    cps = []
    for t in range(len(lands)):
        for j, k in enumerate(RELAY):
            slot = lands[t].at[_peer(x, y, c, k)[1]]
            cps.append(pltpu.make_async_remote_copy(
                src_ref=slot, dst_ref=slot, send_sem=send.at[n * t + j], recv_sem=recv.at[n * t + j],
                device_id=(x, y, 1 - c), device_id_type=MESH))
    return cps


def _own_copies(srcs, lands, own, per_peer):
    me = _me()[3]
    return [pltpu.make_async_copy(srcs[t].at[me] if per_peer else srcs[t], lands[t].at[me], own.at[t])
            for t in range(len(srcs))]


def exchange_start(srcs, *, per_peer, name, dep=None, peers=ALL_PEERS):
    nt = len(srcs)
    ns = len(peers) * nt
    land_shapes = [(a.shape if per_peer else (NDEV,) + a.shape) for a in srcs]
    deps = [] if dep is None else [dep]

    def body(*refs):
        src, land = refs[:nt], refs[nt:2 * nt]
        send, recv, own = refs[2 * nt + len(deps):2 * nt + len(deps) + 3]
        for cp in _exchange_copies(src, land, send, recv, per_peer, peers) + _own_copies(src, land, own, per_peer):
            cp.start()
        refs[-1][...] = jnp.zeros_like(refs[-1])

    hb = lambda a: pltpu.with_memory_space_constraint(a, pltpu.HBM)
    outs = pl.pallas_call(
        body, name=name,
        out_shape=(pltpu.SemaphoreType.DMA((ns,)), pltpu.SemaphoreType.DMA((ns,)), pltpu.SemaphoreType.DMA((nt,)),
                   *[pltpu.HBM(a.shape, a.dtype) for a in srcs], *[pltpu.HBM(s, a.dtype) for s, a in zip(land_shapes, srcs)],
                   jax.ShapeDtypeStruct((8, 128), F32)),
        in_specs=[HBM_SPEC] * (2 * nt) + [pl.BlockSpec(memory_space=pl.ANY)] * len(deps),
        out_specs=(SEM_SPEC, SEM_SPEC, SEM_SPEC, *([HBM_SPEC] * (2 * nt)), pl.BlockSpec(memory_space=pltpu.VMEM)),
        input_output_aliases={i: 3 + i for i in range(2 * nt)},
        compiler_params=pltpu.CompilerParams(has_side_effects=EFFECT),
    )(*[hb(a) for a in srcs], *[hb(lax.empty(s, a.dtype)) for s, a in zip(land_shapes, srcs)], *deps)
    return dict(send=outs[0], recv=outs[1], own=outs[2], src=list(outs[3:3 + nt]), land=list(outs[3 + nt:3 + 2 * nt]),
                token=outs[-1], per_peer=per_peer, peers=peers)


def exchange_wait(h, after, *, name):
    nt = len(h["src"])
    per_peer, peers = h["per_peer"], h["peers"]
    after = list(after) if isinstance(after, (list, tuple)) else [after]

    def body(*refs):
        src, land, send, recv, own = refs[:nt], refs[nt:2 * nt], refs[2 * nt], refs[2 * nt + 1], refs[2 * nt + 2]
        for cp in _exchange_copies(src, land, send, recv, per_peer, peers):
            cp.wait_send()
            cp.wait_recv()
        for cp in _own_copies(src, land, own, per_peer):
            cp.wait()

    outs = pl.pallas_call(
        body, name=name,
        out_shape=(*[pltpu.HBM(a.shape, a.dtype) for a in h["src"]], *[pltpu.HBM(a.shape, a.dtype) for a in h["land"]]),
        in_specs=[HBM_SPEC] * (2 * nt) + [SEM_SPEC, SEM_SPEC, SEM_SPEC] + [pl.BlockSpec(memory_space=pl.ANY)] * len(after),
        out_specs=tuple([HBM_SPEC] * (2 * nt)),
        input_output_aliases={i: i for i in range(2 * nt)},
        compiler_params=pltpu.CompilerParams(has_side_effects=EFFECT),
    )(*h["src"], *h["land"], h["send"], h["recv"], h["own"], *after)
    return list(outs[nt:])


def relay_start(lands, *, name):
    nt = len(lands)
    ns = len(RELAY) * nt

    def body(*refs):
        for cp in _relay_copies(refs[:nt], refs[nt], refs[nt + 1]):
            cp.start()

    outs = pl.pallas_call(
        body, name=name,
        out_shape=(pltpu.SemaphoreType.DMA((ns,)), pltpu.SemaphoreType.DMA((ns,)),
                   *[pltpu.HBM(a.shape, a.dtype) for a in lands]),
        in_specs=[HBM_SPEC] * nt, out_specs=(SEM_SPEC, SEM_SPEC, *([HBM_SPEC] * nt)),
        input_output_aliases={i: 2 + i for i in range(nt)},
        compiler_params=pltpu.CompilerParams(has_side_effects=EFFECT),
    )(*lands)
    return dict(send=outs[0], recv=outs[1], land=list(outs[2:]))


def relay_wait(h, *, name):
    nt = len(h["land"])

    def body(*refs):
        for cp in _relay_copies(refs[:nt], refs[nt], refs[nt + 1]):
            cp.wait_send()
            cp.wait_recv()

    outs = pl.pallas_call(
        body, name=name, out_shape=tuple(pltpu.HBM(a.shape, a.dtype) for a in h["land"]),
        in_specs=[HBM_SPEC] * nt + [SEM_SPEC, SEM_SPEC], out_specs=tuple([HBM_SPEC] * nt),
        input_output_aliases={i: i for i in range(nt)},
        compiler_params=pltpu.CompilerParams(has_side_effects=EFFECT),
    )(*h["land"], h["send"], h["recv"])
    return list(outs)


def _adamw_math(w, g, m, v):
    nm = B1 * m + (1.0 - B1) * g
    nv = B2 * v + (1.0 - B2) * (g * g)
    m_hat = nm / (1.0 - B1 ** STEP)
    v_hat = nv / (1.0 - B2 ** STEP)
    return -LR * (m_hat / (jnp.sqrt(v_hat) + AEPS) + WD * w), nm, nv


def adamw_many(ws, gs, ms, vs, *, name):
    n = len(ws)

    def body(*refs):
        for k in range(n):
            d, nm, nv = _adamw_math(refs[k][...], refs[n + k][...], refs[2 * n + k][...], refs[3 * n + k][...])
            refs[4 * n + k][...] = d
            refs[5 * n + k][...] = nm
            refs[6 * n + k][...] = nv

    vm = pl.BlockSpec(memory_space=pltpu.VMEM)
    sh = [jax.ShapeDtypeStruct(w.shape, F32) for w in ws]
    outs = pl.pallas_call(body, name=name, in_specs=[vm] * (4 * n), out_specs=[vm] * (3 * n), out_shape=sh * 3,
                          )(*ws, *gs, *ms, *vs)
    return outs[:n], outs[n:2 * n], outs[2 * n:]


def adamw(w, g, m, v, *, name, tr=256):
    R, C = w.shape
    tr = _pick(R, tr, 8)

    def body(w_ref, g_ref, m_ref, v_ref, d_ref, nm_ref, nv_ref):
        d_ref[...], nm_ref[...], nv_ref[...] = _adamw_math(w_ref[...], g_ref[...], m_ref[...], v_ref[...])

    blk = pl.BlockSpec((tr, C), lambda i: (i, 0))
    sh = jax.ShapeDtypeStruct((R, C), F32)
    return pl.pallas_call(
        body, name=name, grid=(R // tr,), in_specs=[blk, blk, blk, blk], out_specs=[blk, blk, blk],
        out_shape=[sh, sh, sh], compiler_params=pltpu.CompilerParams(dimension_semantics=("parallel",)),
    )(w, g, m, v)


def adamw_slots(w, slots, m, v, *, name, tr=256):
    unit = w.ndim == 3
    R, C = w.shape[0], w.shape[-1]
    parts = list(slots) if isinstance(slots, (list, tuple)) else [slots]
    n = len(parts)
    assert sum(s.shape[-1] for s in parts) == C
    if R % 16 == 0:
        tr = _pick(R, tr, 16)
    else:
        tr = 144

    def body(w_ref, *refs):
        s_refs, (m_ref, v_ref, g_ref, d_ref, nm_ref, nv_ref) = refs[:n], refs[n:]
        gs = []
        for s_ref in s_refs:
            g = s_ref[0].astype(F32)
            for k in range(1, NDEV):
                g = g + s_ref[k].astype(F32)
            gs.append(g)
        g = gs[0] if n == 1 else jnp.concatenate(gs, axis=-1)
        g_ref[...] = g
        d_ref[...], nm_ref[...], nv_ref[...] = _adamw_math(w_ref[...], g, m_ref[...], v_ref[...])

    blk = pl.BlockSpec((tr, None, C), lambda i: (i, 0, 0)) if unit else pl.BlockSpec((tr, C), lambda i: (i, 0))
    sh = jax.ShapeDtypeStruct(w.shape, F32)
    return pl.pallas_call(
        body, name=name, grid=(pl.cdiv(R, tr),),
        in_specs=[blk] + [pl.BlockSpec((NDEV, tr, s.shape[-1]), lambda i: (0, i, 0)) for s in parts] + [blk, blk],
        out_specs=[blk, blk, blk, blk], out_shape=[sh, sh, sh, sh],
        compiler_params=pltpu.CompilerParams(dimension_semantics=("parallel",)),
    )(w, *parts, m, v)


def kernel(x, c, ctx, c_ctx, w_ada, b_ada, norm1_g, w_in, q_norm_g, kv_norm_g, w_uq, w_ukv, conv_w, conv_b, w_attn_out, w_conv_out, w_o, norm2_g, w_up, ffn_conv_w, ffn_conv_b, w_down, final_g, loss_target, m_c_ctx, m_w_ada, m_b_ada, m_norm1_g, m_w_in, m_q_norm_g, m_kv_norm_g, m_w_uq, m_w_ukv, m_conv_w, m_conv_b, m_w_attn_out, m_w_conv_out, m_w_o, m_norm2_g, m_w_up, m_ffn_conv_w, m_ffn_conv_b, m_w_down, m_final_g, v_c_ctx, v_w_ada, v_b_ada, v_norm1_g, v_w_in, v_q_norm_g, v_kv_norm_g, v_w_uq, v_w_ukv, v_conv_w, v_conv_b, v_w_attn_out, v_w_conv_out, v_w_o, v_norm2_g, v_w_up, v_ffn_conv_w, v_ffn_conv_b, v_w_down, v_final_g):
    me = 4 * lax.axis_index("x") + 2 * lax.axis_index("y") + lax.axis_index("c")
    W = dict(c_ctx=c_ctx, w_ada=w_ada, b_ada=b_ada, norm1_g=norm1_g, w_in=w_in, q_norm_g=q_norm_g, kv_norm_g=kv_norm_g,
             w_uq=w_uq, w_ukv=w_ukv, conv_w=conv_w, conv_b=conv_b, w_attn_out=w_attn_out, w_conv_out=w_conv_out, w_o=w_o,
             norm2_g=norm2_g, w_up=w_up, ffn_conv_w=ffn_conv_w, ffn_conv_b=ffn_conv_b, w_down=w_down, final_g=final_g)
    M = dict(c_ctx=m_c_ctx, w_ada=m_w_ada, b_ada=m_b_ada, norm1_g=m_norm1_g, w_in=m_w_in, q_norm_g=m_q_norm_g,
             kv_norm_g=m_kv_norm_g, w_uq=m_w_uq, w_ukv=m_w_ukv, conv_w=m_conv_w, conv_b=m_conv_b, w_attn_out=m_w_attn_out,
             w_conv_out=m_w_conv_out, w_o=m_w_o, norm2_g=m_norm2_g, w_up=m_w_up, ffn_conv_w=m_ffn_conv_w,
             ffn_conv_b=m_ffn_conv_b, w_down=m_w_down, final_g=m_final_g)
    V = dict(c_ctx=v_c_ctx, w_ada=v_w_ada, b_ada=v_b_ada, norm1_g=v_norm1_g, w_in=v_w_in, q_norm_g=v_q_norm_g,
             kv_norm_g=v_kv_norm_g, w_uq=v_w_uq, w_ukv=v_w_ukv, conv_w=v_conv_w, conv_b=v_conv_b, w_attn_out=v_w_attn_out,
             w_conv_out=v_w_conv_out, w_o=v_w_o, norm2_g=v_norm2_g, w_up=v_w_up, ffn_conv_w=v_ffn_conv_w,
             ffn_conv_b=v_ffn_conv_b, w_down=v_w_down, final_g=v_final_g)
    names = list(W)
    transposed = ("w_up", "w_uq")
    as2d = lambda k, a: (a.reshape(1, -1) if a.ndim == 1 else
                         a[0].T if k in transposed else a.reshape(a.shape[-2], a.shape[-1]))
    W2 = {k: as2d(k, a) for k, a in W.items()}
    M2 = {k: as2d(k, a) for k, a in M.items()}
    V2 = {k: as2d(k, a) for k, a in V.items()}
    unit3 = lambda a: jnp.transpose(a, (2, 0, 1))
    W3, M3, V3 = unit3(W["w_in"]), unit3(M["w_in"]), unit3(V["w_in"])
    nsh = W2["w_ada"].shape[1]

    unit_mid = ("conv_w", "ffn_conv_w")
    mid3 = lambda a: jnp.transpose(a, (1, 0, 2))
    s_all, mod_lat, mod_ctx, ffn_w_full, conv_w_full = ada_fwd(
        c, W2["c_ctx"], mid3(W["ffn_conv_w"]), mid3(W["conv_w"]), W2["w_ada"], W["b_ada"].reshape(NDEV, 1, nsh), [],
        name="ada_fwd")

    stage_w = {"in": ["w_in"], "mid": ["w_uq", "w_ukv", "w_attn_out", "w_conv_out", "w_o"], "up": ["w_up"],
               "down": ["w_down"]}
    two_level = ("in", "mid")
    ag, tok = {}, mod_lat
    for st, nms in stage_w.items():
        ag[st] = exchange_start([W2[nm].astype(BF) for nm in nms], per_peer=False, dep=tok, name="ag_start_" + st,
                                peers=FIRST_HOP if st in two_level else ALL_PEERS)
        tok = ag[st]["token"]

    def get_w(stage, after):
        lands = exchange_wait(ag[stage], after, name="ag_wait_" + stage)
        if stage in two_level:
            lands = relay_wait(relay_start(lands, name="ag_relay_" + stage), name="ag_relay_wait_" + stage)
        g = dict(zip(stage_w[stage], lands))
        if stage == "in":
            return build_win(g["w_in"], name="build_win")
        if stage == "mid":
            wq2, wkv2 = build_wq_wkv(g["w_uq"], g["w_ukv"], name="build_wq_wkv")
            return (wq2, wkv2, unshard_cols(g["w_attn_out"], name="unshard_w_attn_out"),
                    unshard_cols(g["w_conv_out"], name="unshard_w_conv_out"), g["w_o"].reshape(D, D))
        if stage == "up":
            return g["w_up"].reshape(2 * DFF, D)
        return g["w_down"].reshape(DFF, D)

    stage_g = {"ffn": ["w_up", "w_down"], "mid": ["w_attn_out", "w_conv_out", "w_o"], "qkv": ["w_uq", "w_ukv"],
               "in": ["w_in"]}
    rs = {}

    def put_g(stage, g):
        if stage in ("in0", "in1"):
            parts = [shard_win_grad(g["dwin"], g["dwin_c"], col0=g["col0"], name="shard_win_grad_" + stage[-1])]
        elif stage == "mid":
            parts = [shard_cols(g["dwao"], name="shard_w_attn_out"), shard_cols(g["dwco"], name="shard_w_conv_out"),
                     g["dwo"].reshape(NDEV, D // NDEV, D)]
        elif stage == "qkv":
            parts = list(shard_wq_wkv_grad(g["dwq2"], g["dwkv2"], name="shard_wq_wkv_grad"))
        else:
            parts = [g["dwup"].reshape(NDEV, 2 * DFF // NDEV, D), g["dwdn"].reshape(NDEV, DFF // NDEV, D)]
        rs[stage] = exchange_start(parts, per_peer=True, name="rs_start_" + stage)
        return rs[stage]["token"]

    r = _local_step(x[0], ctx[0], loss_target[0], mod_lat, mod_ctx, W2["norm1_g"], W2["q_norm_g"], W2["kv_norm_g"],
                    W2["norm2_g"], W2["final_g"], conv_w_full, W2["conv_b"], ffn_w_full, W2["ffn_conv_b"], get_w, put_g,
                    ag["down"]["token"])

    G, DL, NM, NV = {}, {}, {}, {}

    def finish(stage, after):
        if stage == "in":
            halves = []
            for h in ("in0", "in1"):
                halves += exchange_wait(rs[h], after, name="rs_wait_" + h)
                after = halves[-1]
            G["w_in"], DL["w_in"], NM["w_in"], NV["w_in"] = adamw_slots(W3, halves, M3, V3, name="adamw_w_in")
            return DL["w_in"]
        for nm, sl in zip(stage_g[stage], exchange_wait(rs[stage], after, name="rs_wait_" + stage)):
            G[nm], DL[nm], NM[nm], NV[nm] = adamw_slots(W2[nm], sl, M2[nm], V2[nm], name="adamw_" + nm)
            after = DL[nm]
        return after

    sync = exchange_start([pack_small(r, name="pack_small")], per_peer=False, name="sync_start")
    after = sync["token"]
    for st in ("ffn", "mid", "in", "qkv"):
        after = finish(st, after)
    a_buf, = exchange_wait(sync, [DL[nm] for nms in stage_g.values() for nm in nms], name="sync_wait")
    ssum, g_vec = sum_slots(a_buf, name="sum_small")
    G.update(g_vec)
    loss = ssum[P_LOSS, 0]
    G["conv_w"] = lax.dynamic_slice(ssum[P_CW:P_CW + 3, :CONV], (0, me * (CONV // NDEV)), (3, CONV // NDEV))
    fw_full = ssum[P_FW:P_FW + 6 * FROWS].reshape(3, 2, FROWS * D)[:, :, :DFF].reshape(3, 2 * DFF)
    G["ffn_conv_w"] = lax.dynamic_slice(fw_full, (0, me * (2 * DFF // NDEV)), (3, 2 * DFF // NDEV))

    G["w_ada"], gcc = ada_bwd(s_all, a_buf, ssum, W2["w_ada"], W2["c_ctx"], name="ada_bwd")
    G["c_ctx"] = gcc[0:1]

    DL["w_ada"], NM["w_ada"], NV["w_ada"] = adamw(W2["w_ada"], G["w_ada"], M2["w_ada"], V2["w_ada"], name="adamw_w_ada")
    small = ["c_ctx", "b_ada", "norm1_g", "q_norm_g", "kv_norm_g", "conv_b", "norm2_g", "ffn_conv_b", "final_g", "conv_w",
             "ffn_conv_w"]
    view = lambda k, a3, a2: mid3(a3[k]) if k in unit_mid else a2[k]
    for k in unit_mid:
        G[k] = G[k].reshape(3, 1, -1)
    ds, nms, nvs = adamw_many([view(k, W, W2) for k in small], [G[k] for k in small],
                              [view(k, M, M2) for k in small], [view(k, V, V2) for k in small], name="adamw_small")
    for k, nm in enumerate(small):
        DL[nm], NM[nm], NV[nm] = ds[k], nms[k], nvs[k]

    def as_output(nm, a):
        if nm in transposed:
            return a.T[None]
        if nm == "w_in":
            return jnp.transpose(a, (1, 2, 0))
        if nm in unit_mid and a.ndim == 3:
            return jnp.transpose(a, (1, 0, 2))
        return a.reshape(W[nm].shape)

    outs = [loss, r["dx"][None]]
    for grp in (G, DL, NM, NV):
        outs += [as_output(nm, grp[nm]) for nm in names]
    return tuple(outs)
```

```python
import functools
import numpy as np
import jax
import jax.numpy as jnp
from jax import lax
from jax.experimental import pallas as pl
from jax.experimental.pallas import tpu as pltpu

F32 = jnp.float32
BF = jnp.bfloat16
MESH = pl.DeviceIdType.MESH

D = 1024
T = 2048
TC = 256
TKV = T + TC
GRID_W = 64
NH = 8
DN = 64
DR = 32
DV = 64
QL = 384
KVL = 256
CONV = 512
DFF = 2816
EPS = 1e-6
ROPE_THETA = 10000.0
SCALE = (DN + DR) ** -0.5
NDEV = 8
HP = 128

O_GA, O_GC, O_KV, O_Q, O_CV = 0, 1024, 2048, 2560, 3072
NIN = 4608
CVB = 256
N_IN = 4256
SH_IN = N_IN // NDEV

LR, B1, B2, AEPS, WD, STEP = 0.001, 0.9, 0.999, 1e-08, 0.01, 10


def _pick(n, target, mult=128):
    best = None
    for d in range(mult, min(n, target) + 1, mult):
        if n % d == 0:
            best = d
    return best if best is not None else n


def _swap_start(g):
    return 8 * (g ^ 1)


def mm(a, b, *, ta=False, tb=False, out_dtype=F32, name, tm=1024, tn=1024, tk=2048, M=None, N=None, K=None,
       a_off=(0, 0), b_off=(0, 0), a_stack=False, b_stack=False, o_stack=False, dep=None):
    def dims(arr, stack):
        return (arr.shape[1], 2 * arr.shape[2]) if stack else arr.shape

    ar, ac = dims(a, a_stack)
    br, bc = dims(b, b_stack)
    M = M or ((ac if ta else ar) - a_off[1 if ta else 0])
    K = K or ((ar if ta else ac) - a_off[0 if ta else 1])
    N = N or ((br if tb else bc) - b_off[0 if tb else 1])
    tm = _pick(M, tm, 128 if ta else 16)
    tn = _pick(N // 2 if (o_stack or (b_stack and not tb)) else N, tn, 128)
    tk = _pick(K // 2 if ((a_stack and not ta) or (b_stack and tb)) else K, tk, 128)
    nk = K // tk
    ca = 0 if ta else 1
    cb = 1 if tb else 0

    def body(a_ref, b_ref, *rest):
        o_ref, acc = rest[-2:]
        k = pl.program_id(2)
        part = lax.dot_general(a_ref[...].astype(BF), b_ref[...].astype(BF),
                               (((ca,), (cb,)), ((), ())), preferred_element_type=F32)
        if nk == 1:
            o_ref[...] = part.astype(o_ref.dtype)
        else:
            @pl.when(k == 0)
            def _():
                acc[...] = part

            @pl.when(k > 0)
            def _():
                acc[...] += part

            @pl.when(k == nk - 1)
            def _():
                o_ref[...] = acc[...].astype(o_ref.dtype)

    def spec(blk, rc, off, stack, ncols):
        assert off[0] % blk[0] == 0 and off[1] % blk[1] == 0, (name, blk, off)
        ro, co = off[0] // blk[0], off[1] // blk[1]
        if not stack:
            return pl.BlockSpec(blk, lambda i, j, k: (rc(i, j, k)[0] + ro, rc(i, j, k)[1] + co))
        nhb = ncols // 2 // blk[1]
        return pl.BlockSpec((None,) + blk,
                            lambda i, j, k: ((rc(i, j, k)[1] + co) // nhb, rc(i, j, k)[0] + ro, (rc(i, j, k)[1] + co) % nhb))

    a_spec = spec((tk, tm), lambda i, j, k: (k, i), a_off, a_stack, ac) if ta else \
        spec((tm, tk), lambda i, j, k: (i, k), a_off, a_stack, ac)
    b_spec = spec((tn, tk), lambda i, j, k: (j, k), b_off, b_stack, bc) if tb else \
        spec((tk, tn), lambda i, j, k: (k, j), b_off, b_stack, bc)
    o_spec = spec((tm, tn), lambda i, j, k: (i, j), (0, 0), o_stack, N)
    o_shape = (2, M, N // 2) if o_stack else (M, N)
    deps = [] if dep is None else [dep]
    return pl.pallas_call(
        body, name=name, grid=(M // tm, N // tn, nk),
        in_specs=[a_spec, b_spec] + [pl.BlockSpec(memory_space=pl.ANY)] * len(deps),
        out_specs=o_spec, out_shape=jax.ShapeDtypeStruct(o_shape, out_dtype),
        scratch_shapes=[pltpu.VMEM((tm, tn) if nk > 1 else (8, 128), F32)],
        compiler_params=pltpu.CompilerParams(dimension_semantics=("parallel", "parallel", "arbitrary")),
    )(a, b, *deps)


def _row(width):
    return pl.BlockSpec((1, width), lambda *_: (0, 0))


NLAT = T // TC


def normmod_cat(ctx, x, g, csc, csh, sc, sh, dep, *, name, tm=256):
    assert tm == TC

    def body(c_ref, x_ref, g_ref, csc_ref, csh_ref, sc_ref, sh_ref, dep_ref, h_ref):
        last = pl.program_id(0) == NLAT
        xv = jnp.where(last, c_ref[...], x_ref[...])
        scv = jnp.where(last, csc_ref[...], sc_ref[...])
        shv = jnp.where(last, csh_ref[...], sh_ref[...])
        r = lax.rsqrt(jnp.mean(xv * xv, axis=-1, keepdims=True) + EPS)
        h_ref[...] = ((xv * r * g_ref[...]) * (1.0 + scv) + shv).astype(BF)

    return pl.pallas_call(
        body, name=name, grid=(TKV // tm,),
        in_specs=[pl.BlockSpec((tm, D), lambda i: (0, 0)), pl.BlockSpec((tm, D), lambda i: (jnp.minimum(i, NLAT - 1), 0)),
                  _row(D), _row(D), _row(D), _row(D), _row(D), pl.BlockSpec(memory_space=pl.ANY)],
        out_specs=pl.BlockSpec((tm, D), lambda i: (i, 0)), out_shape=jax.ShapeDtypeStruct((TKV, D), BF),
        compiler_params=pltpu.CompilerParams(dimension_semantics=("parallel",)),
    )(ctx, x, g, csc, csh, sc, sh, dep)


def kvprep(pc, p, kvg, wkv2, ck, sk, *, name, tm=256):
    assert tm == TC
    nb = TKV // tm
    kvcol = O_KV // 512

    def body(pc_ref, p_ref, g_ref, w_ref, ck_ref, sk_ref, k_ref, v_ref, ckv_ref):
        i = pl.program_id(0)
        t = jnp.where(i == NLAT, pc_ref[...], p_ref[...])
        pk = t[:, :KVL]
        r = lax.rsqrt(jnp.mean(pk * pk, axis=-1, keepdims=True) + EPS)
        ckv = (pk * r * g_ref[...]).astype(BF)
        ckv_ref[...] = ckv
        kv2 = jnp.dot(ckv, w_ref[...], preferred_element_type=F32)
        krr = t[:, KVL:KVL + HP] * ck_ref[...] + t[:, KVL + HP:KVL + 2 * HP] * sk_ref[...]
        k_ref[...] = (kv2[:, :NH * HP] + jnp.concatenate([krr] * NH, axis=1)).astype(BF)
        v_ref[...] = kv2[:, NH * HP:].astype(BF)

    return pl.pallas_call(
        body, name=name, grid=(nb,),
        in_specs=[pl.BlockSpec((tm, 512), lambda i: (0, 0)),
                  pl.BlockSpec((tm, 512), lambda i: (jnp.minimum(i, NLAT - 1), kvcol)),
                  _row(KVL), pl.BlockSpec((KVL, NH * HP + NH * DV), lambda i: (0, 0)),
                  pl.BlockSpec((tm, HP), lambda i: (i, 0)), pl.BlockSpec((tm, HP), lambda i: (i, 0))],
        out_specs=[pl.BlockSpec((tm, NH * HP), lambda i: (i, 0)), pl.BlockSpec((tm, NH * DV), lambda i: (i, 0)),
                   pl.BlockSpec((tm, KVL), lambda i: (i, 0))],
        out_shape=[jax.ShapeDtypeStruct((TKV, NH * HP), BF), jax.ShapeDtypeStruct((TKV, NH * DV), BF),
                   jax.ShapeDtypeStruct((TKV, KVL), BF)],
        compiler_params=pltpu.CompilerParams(dimension_semantics=("parallel",)),
    )(pc, p, kvg, wkv2, ck, sk)


def qprep(p, qg, wq2, cq_t, sq_t, *, name, tm=256):
    qcol = O_Q // 512

    def body(p_ref, g_ref, w_ref, c_ref, s_ref, q_ref, cq_ref):
        pq = p_ref[...]
        r = lax.rsqrt(jnp.sum(pq * pq, axis=-1, keepdims=True) * (1.0 / QL) + EPS)
        cq = (pq * r * g_ref[...]).astype(BF)
        cq_ref[...] = cq
        q2 = jnp.dot(cq, w_ref[...], preferred_element_type=F32)
        cc = jnp.concatenate([c_ref[...]] * NH, axis=1)
        ss = jnp.concatenate([s_ref[...]] * NH, axis=1)
        q_ref[...] = (q2[:, :NH * HP] * cc + q2[:, NH * HP:] * ss).astype(BF)

    return pl.pallas_call(
        body, name=name, grid=(T // tm,),
        in_specs=[pl.BlockSpec((tm, 512), lambda i: (i, qcol)), _row(512),
                  pl.BlockSpec((512, 2 * NH * HP), lambda i: (0, 0)),
                  pl.BlockSpec((tm, HP), lambda i: (i, 0)), pl.BlockSpec((tm, HP), lambda i: (i, 0))],
        out_specs=[pl.BlockSpec((tm, NH * HP), lambda i: (i, 0)), pl.BlockSpec((tm, 512), lambda i: (i, 0))],
        out_shape=[jax.ShapeDtypeStruct((T, NH * HP), BF), jax.ShapeDtypeStruct((T, 512), BF)],
        compiler_params=pltpu.CompilerParams(dimension_semantics=("parallel",)),
    )(p, qg, wq2, cq_t, sq_t)


def _head_mask(h):
    lanes = lax.broadcasted_iota(jnp.int32, (1, 2 * DV), 1)
    return (lanes // DV) == (h % 2)


LOG2E = 1.4426950408889634


def attn_fwd(q, k, v, *, name, tq=1024, kc=768):
    def body(q_ref, k_ref, v_ref, o_ref, lse_ref):
        h = pl.program_id(1)
        qv = q_ref[...]
        m = l = acc = None
        for c in range(TKV // kc):
            s = lax.dot_general(qv, k_ref[c * kc:(c + 1) * kc, :], (((1,), (1,)), ((), ())),
                                preferred_element_type=F32) * (SCALE * LOG2E)
            mc = jnp.max(s, axis=-1, keepdims=True)
            if c == 0:
                m = mc
                e = jnp.exp2(s - m)
                l = jnp.sum(e, axis=-1, keepdims=True)
                acc = jnp.dot(e.astype(BF), v_ref[c * kc:(c + 1) * kc, :], preferred_element_type=F32)
            else:
                mn = jnp.maximum(m, mc)
                a = jnp.exp2(m - mn)
                e = jnp.exp2(s - mn)
                l = l * a + jnp.sum(e, axis=-1, keepdims=True)
                acc = acc * a + jnp.dot(e.astype(BF), v_ref[c * kc:(c + 1) * kc, :], preferred_element_type=F32)
                m = mn
        o2 = jnp.where(_head_mask(h), acc * (1.0 / l), 0.0).astype(BF)
        lse_ref[...] = jnp.broadcast_to(m + jnp.log(l) * LOG2E, (tq, HP))

        @pl.when(h % 2 == 0)
        def _():
            o_ref[...] = o2

        @pl.when(h % 2 == 1)
        def _():
            o_ref[...] = o_ref[...] + o2

    return pl.pallas_call(
        body, name=name, grid=(T // tq, NH),
        in_specs=[pl.BlockSpec((tq, HP), lambda i, h: (i, h)), pl.BlockSpec((TKV, HP), lambda i, h: (0, h)),
                  pl.BlockSpec((TKV, 2 * DV), lambda i, h: (0, h // 2))],
        out_specs=[pl.BlockSpec((tq, 2 * DV), lambda i, h: (i, h // 2)), pl.BlockSpec((tq, HP), lambda i, h: (i, h))],
        out_shape=[jax.ShapeDtypeStruct((T, NH * DV), BF), jax.ShapeDtypeStruct((T, NH * HP), F32)],
        compiler_params=pltpu.CompilerParams(dimension_semantics=("parallel", "arbitrary")),
    )(q, k, v)


def _shift_dn(x):
    n = x.shape[0]
    rows = lax.broadcasted_iota(jnp.int32, (n, 1), 0)
    return jnp.where(rows == 0, 0.0, pltpu.roll(x, 1, axis=0))


def _shift_up(x):
    n = x.shape[0]
    rows = lax.broadcasted_iota(jnp.int32, (n, 1), 0)
    return jnp.where(rows == n - 1, 0.0, pltpu.roll(x, n - 1, axis=0))


def _conv(x, w_ref, b_ref):
    return b_ref[...] + _shift_dn(x) * w_ref[0:1, :] + x * w_ref[1:2, :] + _shift_up(x) * w_ref[2:3, :]


def _conv_t(dy, w_ref):
    return _shift_up(dy) * w_ref[0:1, :] + dy * w_ref[1:2, :] + _shift_dn(dy) * w_ref[2:3, :]


def _conv_wgrad(dw_ref, dy, x):
    dw_ref[0:1, :] = jnp.sum(dy * _shift_dn(x), axis=0, keepdims=True)
    dw_ref[1:2, :] = jnp.sum(dy * x, axis=0, keepdims=True)
    dw_ref[2:3, :] = jnp.sum(dy * _shift_up(x), axis=0, keepdims=True)


def convz(p, cw, cb, *, name):
    o0 = O_CV // (3 * CVB)

    def body(p_ref, w_ref, bias_ref, z_ref):
        xv, bv, cv = p_ref[:, 0:CVB], p_ref[:, CVB:2 * CVB], p_ref[:, 2 * CVB:3 * CVB]
        z_ref[...] = (bv * _conv(cv * xv, w_ref, bias_ref)).astype(BF)

    return pl.pallas_call(
        body, name=name, grid=(CONV // CVB,),
        in_specs=[pl.BlockSpec((T, 3 * CVB), lambda j: (0, o0 + j)), pl.BlockSpec((3, CVB), lambda j: (0, j)),
                  pl.BlockSpec((1, CVB), lambda j: (0, j))],
        out_specs=pl.BlockSpec((T, CVB), lambda j: (0, j)),
        out_shape=jax.ShapeDtypeStruct((T, CONV), BF),
        compiler_params=pltpu.CompilerParams(dimension_semantics=("parallel",)),
    )(p, cw, cb)


def out_proj_merge(o, wao, z, wco, p, *, name, tm=512):
    kin = o.shape[1]

    def body(o_ref, wa_ref, z_ref, wc_ref, ga_ref, gc_ref, ya_ref, yc_ref, m_ref):
        ya = jnp.dot(o_ref[...], wa_ref[...], preferred_element_type=F32)
        yc = jnp.dot(z_ref[...], wc_ref[...], preferred_element_type=F32)
        ya_ref[...] = ya
        yc_ref[...] = yc
        m_ref[...] = (jax.nn.sigmoid(ga_ref[...]) * ya + jax.nn.sigmoid(gc_ref[...]) * yc).astype(BF)

    blk = pl.BlockSpec((tm, D), lambda i: (i, 0))
    act = pl.BlockSpec((tm, kin), lambda i: (i, 0))
    wsp = pl.BlockSpec((kin, D), lambda i: (0, 0))
    sh = jax.ShapeDtypeStruct((T, D), F32)
    return pl.pallas_call(
        body, name=name, grid=(T // tm,),
        in_specs=[act, wsp, act, wsp, pl.BlockSpec((tm, D), lambda i: (i, O_GA // D)),
                  pl.BlockSpec((tm, D), lambda i: (i, O_GC // D))],
        out_specs=[blk, blk, blk], out_shape=[sh, sh, jax.ShapeDtypeStruct((T, D), BF)],
        compiler_params=pltpu.CompilerParams(dimension_semantics=("parallel",)),
    )(o, wao, z, wco, p, p)


CONV_HALO = 8
CONV_ROWS = 256


def _row_chunks(n, chunk, carry):
    carry = chunk(0, True, False, carry)
    carry = lax.fori_loop(1, n // CONV_ROWS - 1, lambda c, a: chunk(c * CONV_ROWS, False, False, a), carry)
    return chunk(n - CONV_ROWS, False, True, carry)


def _ext_rows(ref, r0, first, last):
    n, w = ref.shape
    zero = jnp.zeros((CONV_HALO, w), ref.dtype)
    if first:
        return jnp.concatenate([zero, ref[0:CONV_ROWS + CONV_HALO, :]], axis=0)
    if last:
        return jnp.concatenate([ref[n - CONV_ROWS - CONV_HALO:n, :], zero], axis=0)
    return ref[pl.ds(pl.multiple_of(r0 - CONV_HALO, 8), CONV_ROWS + 2 * CONV_HALO), :]


def _center_rows(r0, first, last):
    return slice(r0, r0 + CONV_ROWS) if (first or last) else pl.ds(pl.multiple_of(r0, 8), CONV_ROWS)


def _roll_dn(x):
    return pltpu.roll(x, 1, axis=0)


def _roll_up(x):
    return pltpu.roll(x, x.shape[0] - 1, axis=0)


_CTR = slice(CONV_HALO, CONV_HALO + CONV_ROWS)


def ffn_act(u0, cw, cb, *, name, tc=256):
    nb = DFF // tc

    def body(u_ref, wg_ref, wv_ref, bg_ref, bv_ref, f_ref):
        wg = [wg_ref[k:k + 1, :] for k in range(3)]
        wv = [wv_ref[k:k + 1, :] for k in range(3)]
        bg, bv = bg_ref[...], bv_ref[...]

        def chunk(r0, first, last, carry):
            xg, xv = _ext_rows(u_ref.at[0], r0, first, last), _ext_rows(u_ref.at[1], r0, first, last)
            ug = bg + _roll_dn(xg) * wg[0] + xg * wg[1] + _roll_up(xg) * wg[2]
            uv = bv + _roll_dn(xv) * wv[0] + xv * wv[1] + _roll_up(xv) * wv[2]
            f_ref[_center_rows(r0, first, last), :] = (ug * jax.nn.sigmoid(ug) * uv)[_CTR].astype(BF)
            return carry

        _row_chunks(T, chunk, 0)

    return pl.pallas_call(
        body, name=name, grid=(nb,),
        in_specs=[pl.BlockSpec((2, T, tc), lambda j: (0, 0, j)),
                  pl.BlockSpec((3, tc), lambda j: (0, j)), pl.BlockSpec((3, tc), lambda j: (0, nb + j)),
                  pl.BlockSpec((1, tc), lambda j: (0, j)), pl.BlockSpec((1, tc), lambda j: (0, nb + j))],
        out_specs=pl.BlockSpec((T, tc), lambda j: (0, j)),
        out_shape=jax.ShapeDtypeStruct((T, DFF), BF),
        compiler_params=pltpu.CompilerParams(dimension_semantics=("parallel",)),
    )(u0, cw, cw, cb, cb)


def rows_call(lead, ins, in_specs, out_shape, out_specs, fn, *, name, R, tm):
    tb, a_stack, tk = lead.get("tb", False), lead.get("a_stack", False), lead["tk"]
    K = 2 * lead["a"].shape[2] if a_stack else lead["a"].shape[1]
    nk = K // tk
    deps = [] if lead.get("dep") is None else [lead["dep"]]
    n_in = len(ins)

    def body(a_ref, b_ref, *refs):
        refs = refs[len(deps):]
        in_refs, out_refs, acc = refs[:n_in], refs[n_in:-1], refs[-1]
        i, k = pl.program_id(0), pl.program_id(1)
        part = lax.dot_general(a_ref[...].astype(BF), b_ref[...].astype(BF),
                               (((1,), (1 if tb else 0,)), ((), ())), preferred_element_type=F32)
        if nk == 1:
            fn(i, part, in_refs, out_refs)
            return

        @pl.when(k == 0)
        def _():
            acc[...] = part

        @pl.when(k > 0)
        def _():
            acc[...] += part

        @pl.when(k == nk - 1)
        def _():
            fn(i, acc[...], in_refs, out_refs)

    if a_stack:
        nhb = K // 2 // tk
        a_spec = pl.BlockSpec((None, tm, tk), lambda i, k: (k // nhb, i, k % nhb))
    else:
        a_spec = pl.BlockSpec((tm, tk), lambda i, k: (i, k))
    b_spec = pl.BlockSpec((D, tk), lambda i, k: (0, k)) if tb else pl.BlockSpec((tk, D), lambda i, k: (k, 0))
    return pl.pallas_call(
        body, name=name, grid=(R // tm, nk),
        in_specs=[a_spec, b_spec] + [pl.BlockSpec(memory_space=pl.ANY)] * len(deps) + list(in_specs),
        out_specs=out_specs, out_shape=out_shape,
        scratch_shapes=[pltpu.VMEM((tm, D) if nk > 1 else (8, 128), F32)],
        compiler_params=pltpu.CompilerParams(dimension_semantics=("arbitrary", "arbitrary")),
    )(lead["a"], lead["b"], *deps, *ins)


def _rblk(tm, w=D, col=0):
    return pl.BlockSpec((tm, w), lambda i, k: (i, col))


def _rrow(w=D):
    return pl.BlockSpec((1, w), lambda i, k: (0, 0))


def down_final(f, wdn, x1, g2, fg, tgt, *, name, tm=512):
    def fn(i, d, in_refs, out_refs):
        x1_ref, g2_ref, fg_ref, t_ref = in_refs
        d_ref, dx_ref, dd_ref, dfg_ref, loss_ref = out_refs
        d_ref[...] = d
        xv = x1_ref[...] + g2_ref[...] * d
        r = lax.rsqrt(jnp.mean(xv * xv, axis=-1, keepdims=True) + EPS)
        xh = xv * r
        diff = xh * fg_ref[...] - t_ref[...]
        part = 0.5 * jnp.sum(jnp.mean(diff * diff, axis=-1, keepdims=True), axis=0, keepdims=True)
        dy = diff * (1.0 / D)
        a = dy * fg_ref[...]
        dx = r * (a - xh * jnp.mean(a * xh, axis=-1, keepdims=True))
        dx_ref[...] = dx
        dd_ref[...] = (dx * g2_ref[...]).astype(BF)
        dfg = jnp.sum(dy * xh, axis=0, keepdims=True)

        @pl.when(i == 0)
        def _():
            dfg_ref[...] = dfg
            loss_ref[...] = jnp.broadcast_to(part, (1, 128))

        @pl.when(i > 0)
        def _():
            dfg_ref[...] += dfg
            loss_ref[...] += jnp.broadcast_to(part, (1, 128))

    blk = _rblk(tm)
    return rows_call(
        dict(a=f, b=wdn, tk=DFF), [x1, g2, fg, tgt], [blk, _rrow(), _rrow(), blk],
        [jax.ShapeDtypeStruct((T, D), F32), jax.ShapeDtypeStruct((T, D), F32), jax.ShapeDtypeStruct((T, D), BF),
         jax.ShapeDtypeStruct((1, D), F32), jax.ShapeDtypeStruct((1, 128), F32)],
        [blk, blk, blk, _rrow(), _rrow(128)], fn, name=name, R=T, tm=tm)


def oproj_resid(merged, wo, x, gate, g, sc, sh, *, name, tm=512):
    def fn(i, a, in_refs, out_refs):
        x_ref, gate_ref, g_ref, sc_ref, sh_ref = in_refs
        a_ref, x1_ref, h_ref = out_refs
        a_ref[...] = a
        xv = x_ref[...] + gate_ref[...] * a
        x1_ref[...] = xv
        r = lax.rsqrt(jnp.mean(xv * xv, axis=-1, keepdims=True) + EPS)
        h_ref[...] = ((xv * r * g_ref[...]) * (1.0 + sc_ref[...]) + sh_ref[...]).astype(BF)

    blk = _rblk(tm)
    return rows_call(
        dict(a=merged, b=wo, tk=D), [x, gate, g, sc, sh], [blk, _rrow(), _rrow(), _rrow(), _rrow()],
        [jax.ShapeDtypeStruct((T, D), F32), jax.ShapeDtypeStruct((T, D), F32), jax.ShapeDtypeStruct((T, D), BF)],
        [blk, blk, blk], fn, name=name, R=T, tm=tm)


def oproj_dx_gate_bwd(da, wo, p, ya, yc, wao, wco, *, name, tm=512):
    kin = wao.shape[0]

    def fn(i, dm, in_refs, out_refs):
        ga_ref, gc_ref, ya_ref, yc_ref, wa_ref, wc_ref = in_refs
        dya_ref, dyc_ref, dp_ref, do_ref, dz_ref = out_refs
        sa, sc_ = jax.nn.sigmoid(ga_ref[...]), jax.nn.sigmoid(gc_ref[...])
        dya, dyc = (dm * sa).astype(BF), (dm * sc_).astype(BF)
        dya_ref[...] = dya
        dyc_ref[...] = dyc
        dp_ref[:, 0:D] = (dm * ya_ref[...] * (sa * (1.0 - sa))).astype(BF)
        dp_ref[:, D:2 * D] = (dm * yc_ref[...] * (sc_ * (1.0 - sc_))).astype(BF)
        nt = (((1,), (1,)), ((), ()))
        do_ref[...] = lax.dot_general(dya, wa_ref[...], nt, preferred_element_type=F32).astype(BF)
        dz_ref[...] = lax.dot_general(dyc, wc_ref[...], nt, preferred_element_type=F32)

    blk = _rblk(tm)
    sh = jax.ShapeDtypeStruct((T, D), BF)
    wsp = pl.BlockSpec((kin, D), lambda i, k: (0, 0))
    return rows_call(
        dict(a=da, b=wo, tb=True, tk=D), [p, p, ya, yc, wao, wco],
        [_rblk(tm, D, O_GA // D), _rblk(tm, D, O_GC // D), blk, blk, wsp, wsp],
        [sh, sh, jax.ShapeDtypeStruct((T, NIN), BF), jax.ShapeDtypeStruct((T, kin), BF), jax.ShapeDtypeStruct((T, kin), F32)],
        [blk, blk, _rblk(tm, 2 * D), _rblk(tm, kin), _rblk(tm, kin)], fn, name=name, R=T, tm=tm)


def normmod_bwd(x, dh, g, sc, dres, gsrc, gate, *, name, tm=512):
    R = x.shape[0]
    tm = min(tm, R)
    has_res = dres is not None
    fused = isinstance(dh, dict)
    if fused:
        tb, a_stack, tk = dh.get("tb", False), dh.get("a_stack", False), dh["tk"]
        K = 2 * dh["a"].shape[2] if a_stack else dh["a"].shape[1]
        nk = K // tk
        deps = [] if dh.get("dep") is None else [dh["dep"]]
        n_dh = 2 + len(deps)
    else:
        nk, n_dh = 1, 1

    def elementwise(i, dhv, x_ref, g_ref, sc_ref, res_refs, out_refs):
        xv = x_ref[...]
        r = lax.rsqrt(jnp.mean(xv * xv, axis=-1, keepdims=True) + EPS)
        xh = xv * r
        n = xh * g_ref[...]
        dn = dhv * (1.0 + sc_ref[...])
        a = dn * g_ref[...]
        rows = [jnp.sum(dhv, axis=0, keepdims=True), jnp.sum(dhv * n, axis=0, keepdims=True),
                jnp.sum(dn * xh, axis=0, keepdims=True)]
        if has_res:
            dres_ref, gsrc_ref, gate_ref = res_refs
            dx_ref, dxg_ref, st_ref = out_refs
            dr = dres_ref[...]
            dx = dr + r * (a - xh * jnp.mean(a * xh, axis=-1, keepdims=True))
            dx_ref[...] = dx
            dxg_ref[...] = (dx * gate_ref[...]).astype(BF)
            rows.append(jnp.sum(dr * gsrc_ref[...], axis=0, keepdims=True))
        else:
            st_ref, = out_refs
            rows.append(jnp.zeros((1, D), F32))

        @pl.when(i == 0)
        def _():
            for k, row in enumerate(rows):
                st_ref[k:k + 1, :] = row

        @pl.when(i > 0)
        def _():
            for k, row in enumerate(rows):
                st_ref[k:k + 1, :] += row

    def body(*refs):
        x_ref, dh_refs, g_ref, sc_ref = refs[0], refs[1:1 + n_dh], refs[1 + n_dh], refs[2 + n_dh]
        rest = refs[3 + n_dh:]
        res_refs, rest = (rest[:3], rest[3:]) if has_res else ((), rest)
        out_refs = rest[:3] if has_res else rest[:1]
        i = pl.program_id(0)
        if not fused:
            elementwise(i, dh_refs[0][...], x_ref, g_ref, sc_ref, res_refs, out_refs)
            return
        acc = rest[-1]
        k = pl.program_id(1)
        part = lax.dot_general(dh_refs[0][...].astype(BF), dh_refs[1][...].astype(BF),
                               (((1,), (1 if tb else 0,)), ((), ())), preferred_element_type=F32)
        if nk == 1:
            elementwise(i, part, x_ref, g_ref, sc_ref, res_refs, out_refs)
            return

        @pl.when(k == 0)
        def _():
            acc[...] = part

        @pl.when(k > 0)
        def _():
            acc[...] += part

        @pl.when(k == nk - 1)
        def _():
            elementwise(i, acc[...], x_ref, g_ref, sc_ref, res_refs, out_refs)

    rowb = lambda w: pl.BlockSpec((1, w), lambda i, *k: (0, 0))
    blk = pl.BlockSpec((tm, D), lambda i, *k: (i, 0))
    st_spec = pl.BlockSpec((4, D), lambda i, *k: (0, 0))
    st_shape = jax.ShapeDtypeStruct((4, D), F32)
    if fused:
        if a_stack:
            nhb = K // 2 // tk
            a_spec = pl.BlockSpec((None, tm, tk), lambda i, k: (k // nhb, i, k % nhb))
        else:
            a_spec = pl.BlockSpec((tm, tk), lambda i, k: (i, k))
        b_spec = pl.BlockSpec((D, tk), lambda i, k: (0, k)) if tb else pl.BlockSpec((tk, D), lambda i, k: (k, 0))
        dh_specs = [a_spec, b_spec] + [pl.BlockSpec(memory_space=pl.ANY)] * len(deps)
        dh_args = [dh["a"], dh["b"]] + deps
        grid, sem = (R // tm, nk), ("arbitrary", "arbitrary")
        scratch = [pltpu.VMEM((tm, D) if nk > 1 else (8, 128), F32)]
    else:
        dh_specs, dh_args, grid, sem, scratch = [blk], [dh], (R // tm,), ("arbitrary",), []
    cp = pltpu.CompilerParams(dimension_semantics=sem)
    if has_res:
        return pl.pallas_call(
            body, name=name, grid=grid, in_specs=[blk] + dh_specs + [rowb(D), rowb(D), blk, blk, rowb(D)],
            out_specs=[blk, blk, st_spec], scratch_shapes=scratch,
            out_shape=[jax.ShapeDtypeStruct((R, D), F32), jax.ShapeDtypeStruct((R, D), BF), st_shape],
            compiler_params=cp,
        )(x, *dh_args, g, sc, dres, gsrc, gate)
    return pl.pallas_call(
        body, name=name, grid=grid, in_specs=[blk] + dh_specs + [rowb(D), rowb(D)],
        out_specs=st_spec, out_shape=st_shape, scratch_shapes=scratch, compiler_params=cp,
    )(x, *dh_args, g, sc)


def ffn_act_bwd(u0, df, cw, cb, *, name, tc=128):
    nb = DFF // tc

    def body(u_ref, df_ref, wg_ref, wv_ref, bg_ref, bv_ref, du_ref, dw_ref, db_ref):
        wg = [wg_ref[k:k + 1, :] for k in range(3)]
        wv = [wv_ref[k:k + 1, :] for k in range(3)]
        bg, bv = bg_ref[...], bv_ref[...]

        def chunk(r0, first, last, acc):
            xg, xv = _ext_rows(u_ref.at[0], r0, first, last), _ext_rows(u_ref.at[1], r0, first, last)
            dfe = _ext_rows(df_ref, r0, first, last)
            xg_d, xg_u, xv_d, xv_u = _roll_dn(xg), _roll_up(xg), _roll_dn(xv), _roll_up(xv)
            ug = bg + xg_d * wg[0] + xg * wg[1] + xg_u * wg[2]
            uv = bv + xv_d * wv[0] + xv * wv[1] + xv_u * wv[2]
            sig = jax.nn.sigmoid(ug)
            dug = dfe * uv * (sig * (1.0 + ug * (1.0 - sig)))
            duv = dfe * (ug * sig)
            rows = _center_rows(r0, first, last)
            du_ref[0, rows, :] = (_roll_up(dug) * wg[0] + dug * wg[1] + _roll_dn(dug) * wg[2])[_CTR].astype(BF)
            du_ref[1, rows, :] = (_roll_up(duv) * wv[0] + duv * wv[1] + _roll_dn(duv) * wv[2])[_CTR].astype(BF)
            terms = [dug * xg_d, dug * xg, dug * xg_u, dug, duv * xv_d, duv * xv, duv * xv_u, duv]
            return tuple(a + jnp.sum(t[_CTR], axis=0, keepdims=True) for a, t in zip(acc, terms))

        acc = _row_chunks(T, chunk, tuple(jnp.zeros((1, tc), F32) for _ in range(8)))
        for k in range(3):
            dw_ref[0, k:k + 1, :] = acc[k]
            dw_ref[1, k:k + 1, :] = acc[4 + k]
        db_ref[0] = acc[3]
        db_ref[1] = acc[7]

    lo = lambda r: pl.BlockSpec((r, tc), lambda j: (0, j))
    hi = lambda r: pl.BlockSpec((r, tc), lambda j: (0, nb + j))
    st = lambda r: pl.BlockSpec((2, r, tc), lambda j: (0, 0, j))
    return pl.pallas_call(
        body, name=name, grid=(nb,),
        in_specs=[st(T), lo(T), lo(3), hi(3), lo(1), hi(1)],
        out_specs=[st(T), st(3), st(1)],
        out_shape=[jax.ShapeDtypeStruct((2, T, DFF), BF), jax.ShapeDtypeStruct((2, 3, DFF), F32),
                   jax.ShapeDtypeStruct((2, 1, DFF), F32)],
        compiler_params=pltpu.CompilerParams(dimension_semantics=("parallel",)),
    )(u0, df, cw, cw, cb, cb)


def convz_bwd(p, dz, cw, cb, dp, *, name):
    o0 = O_CV // (3 * CVB)

    def body(p_ref, dz_ref, w_ref, bias_ref, dp_in, dp_ref, dw_ref, dbias_ref):
        xv, bv, cv = p_ref[:, 0:CVB], p_ref[:, CVB:2 * CVB], p_ref[:, 2 * CVB:3 * CVB]
        ci = cv * xv
        dwc = _conv(ci, w_ref, bias_ref)
        dzv = dz_ref[...]
        ddw = dzv * bv
        dci = _conv_t(ddw, w_ref)
        dp_ref[:, 0:CVB] = (dci * cv).astype(BF)
        dp_ref[:, CVB:2 * CVB] = (dzv * dwc).astype(BF)
        dp_ref[:, 2 * CVB:3 * CVB] = (dci * xv).astype(BF)
        _conv_wgrad(dw_ref, ddw, ci)
        dbias_ref[...] = jnp.sum(ddw, axis=0, keepdims=True)

    own = lambda r: pl.BlockSpec((r, CVB), lambda j: (0, j))
    return pl.pallas_call(
        body, name=name, grid=(CONV // CVB,),
        in_specs=[pl.BlockSpec((T, 3 * CVB), lambda j: (0, o0 + j)), own(T), own(3), own(1),
                  pl.BlockSpec(memory_space=pl.ANY)],
        out_specs=[pl.BlockSpec((T, 3 * CVB), lambda j: (0, o0 + j)), own(3), own(1)],
        out_shape=[jax.ShapeDtypeStruct((T, NIN), BF), jax.ShapeDtypeStruct((3, CONV), F32),
                   jax.ShapeDtypeStruct((1, CONV), F32)],
        input_output_aliases={4: 0},
        compiler_params=pltpu.CompilerParams(dimension_semantics=("parallel",)),
    )(p, dz, cw, cb, dp)


def attn_bwd(q, k, v, do, o, lse, dep, *, name, tq=1024, kc=768):
    NKC, KC = TKV // kc, kc
    deps = [] if dep is None else [dep]

    def body(q_ref, k_ref, v_ref, do_ref, o_ref, lse_ref, *rest):
        dq_ref, dk_ref, dv_ref = rest[len(deps):]
        h, i = pl.program_id(0), pl.program_id(1)

        @pl.when(i == 0)
        def _():
            dk_ref[...] = jnp.zeros_like(dk_ref)

        @pl.when((i == 0) & (h % 2 == 0))
        def _():
            dv_ref[...] = jnp.zeros_like(dv_ref)

        qv = q_ref[...]
        dom = jnp.where(_head_mask(h), do_ref[...], jnp.zeros_like(do_ref[...]))
        delta = jnp.sum(dom.astype(F32) * o_ref[...].astype(F32), axis=-1, keepdims=True)
        lse = lse_ref[:, 0:1]
        dq = jnp.zeros((tq, HP), F32)
        for c in range(NKC):
            cols = slice(c * KC, (c + 1) * KC)
            s = lax.dot_general(qv, k_ref[cols, :], (((1,), (1,)), ((), ())),
                                preferred_element_type=F32) * (SCALE * LOG2E)
            pr = jnp.exp2(s - lse)
            dp = lax.dot_general(dom, v_ref[cols, :], (((1,), (1,)), ((), ())), preferred_element_type=F32)
            ds = (pr * (dp - delta) * SCALE).astype(BF)
            dq = dq + jnp.dot(ds, k_ref[cols, :], preferred_element_type=F32)
            dk_ref[cols, :] += lax.dot_general(ds, qv, (((0,), (0,)), ((), ())), preferred_element_type=F32)
            dv_ref[cols, :] += lax.dot_general(pr.astype(BF), dom, (((0,), (0,)), ((), ())), preferred_element_type=F32)
        dq_ref[...] = dq

    return pl.pallas_call(
        body, name=name, grid=(NH, T // tq),
        in_specs=[pl.BlockSpec((tq, HP), lambda h, i: (i, h)), pl.BlockSpec((TKV, HP), lambda h, i: (0, h)),
                  pl.BlockSpec((TKV, 2 * DV), lambda h, i: (0, h // 2)), pl.BlockSpec((tq, 2 * DV), lambda h, i: (i, h // 2)),
                  pl.BlockSpec((tq, 2 * DV), lambda h, i: (i, h // 2)), pl.BlockSpec((tq, HP), lambda h, i: (i, h)),
                  *([pl.BlockSpec(memory_space=pl.ANY)] * len(deps))],
        out_specs=[pl.BlockSpec((tq, HP), lambda h, i: (i, h)), pl.BlockSpec((TKV, HP), lambda h, i: (0, h)),
                   pl.BlockSpec((TKV, 2 * DV), lambda h, i: (0, h // 2))],
        out_shape=[jax.ShapeDtypeStruct((T, NH * HP), F32), jax.ShapeDtypeStruct((TKV, NH * HP), F32),
                   jax.ShapeDtypeStruct((TKV, NH * DV), F32)],
        compiler_params=pltpu.CompilerParams(dimension_semantics=("arbitrary", "arbitrary")),
    )(q, k, v, do, o, lse, *deps)


def qprep_bwd(p, dq, qg, wq2, cq_t, sq_t, dp, *, name, tm=256):
    qcol = O_Q // 512

    def body(p_ref, dq_ref, g_ref, w_ref, c_ref, s_ref, dp_in, dp_ref, dq2_ref, dg_ref):
        i = pl.program_id(0)
        dqv = dq_ref[...]
        cc = jnp.concatenate([c_ref[...]] * NH, axis=1)
        ss = jnp.concatenate([s_ref[...]] * NH, axis=1)
        dq2 = jnp.concatenate([dqv * cc, dqv * ss], axis=1).astype(BF)
        dq2_ref[...] = dq2
        dcq = lax.dot_general(dq2, w_ref[...], (((1,), (1,)), ((), ())), preferred_element_type=F32)
        pq = p_ref[...]
        r = lax.rsqrt(jnp.sum(pq * pq, axis=-1, keepdims=True) * (1.0 / QL) + EPS)
        xh = pq * r
        a = dcq * g_ref[...]
        dp_ref[...] = (r * (a - xh * (jnp.sum(a * xh, axis=-1, keepdims=True) * (1.0 / QL)))).astype(BF)
        dg = jnp.sum(dcq * xh, axis=0, keepdims=True)

        @pl.when(i == 0)
        def _():
            dg_ref[...] = dg

        @pl.when(i > 0)
        def _():
            dg_ref[...] += dg

    return pl.pallas_call(
        body, name=name, grid=(T // tm,),
        in_specs=[pl.BlockSpec((tm, 512), lambda i: (i, qcol)), pl.BlockSpec((tm, NH * HP), lambda i: (i, 0)), _row(512),
                  pl.BlockSpec((512, 2 * NH * HP), lambda i: (0, 0)),
                  pl.BlockSpec((tm, HP), lambda i: (i, 0)), pl.BlockSpec((tm, HP), lambda i: (i, 0)),
                  pl.BlockSpec(memory_space=pl.ANY)],
        out_specs=[pl.BlockSpec((tm, 512), lambda i: (i, qcol)), pl.BlockSpec((tm, 2 * NH * HP), lambda i: (i, 0)), _row(512)],
        out_shape=[jax.ShapeDtypeStruct((T, NIN), BF), jax.ShapeDtypeStruct((T, 2 * NH * HP), BF),
                   jax.ShapeDtypeStruct((1, 512), F32)],
        input_output_aliases={6: 0},
        compiler_params=pltpu.CompilerParams(dimension_semantics=("arbitrary",)),
    )(p, dq, qg, wq2, cq_t, sq_t, dp)


def kvprep_bwd(pc, p, dk, dv, kvg, wkv2, ck, sk, dp, *, name, tm=256):
    assert tm == TC
    nb = TKV // tm
    kvcol = O_KV // 512

    def body(pc_ref, p_ref, dk_ref, dv_ref, g_ref, w_ref, ck_ref, sk_ref, dp_in, dp_ref, dpc_ref, dkv2_ref, dg_ref):
        i = pl.program_id(0)
        t = jnp.where(i == NLAT, pc_ref[...], p_ref[...])
        pk = t[:, :KVL]
        r = lax.rsqrt(jnp.mean(pk * pk, axis=-1, keepdims=True) + EPS)
        xh = pk * r
        dkv = dk_ref[...]
        dkv2 = jnp.concatenate([dkv, dv_ref[...]], axis=1).astype(BF)
        dkv2_ref[...] = dkv2
        dckv = lax.dot_general(dkv2, w_ref[...], (((1,), (1,)), ((), ())), preferred_element_type=F32)
        a = dckv * g_ref[...]
        dpk = r * (a - xh * jnp.mean(a * xh, axis=-1, keepdims=True))
        dkr = dkv[:, 0:HP]
        for hh in range(1, NH):
            dkr = dkr + dkv[:, hh * HP:(hh + 1) * HP]
        res = jnp.concatenate([dpk, dkr * ck_ref[...], dkr * sk_ref[...]], axis=1).astype(BF)
        dg = jnp.sum(dckv * xh, axis=0, keepdims=True)

        @pl.when(i == 0)
        def _():
            dg_ref[...] = dg

        @pl.when(i > 0)
        def _():
            dg_ref[...] += dg

        @pl.when(i < NLAT)
        def _():
            dp_ref[...] = res

        @pl.when(i == NLAT)
        def _():
            dpc_ref[...] = res

    rb = lambda w: pl.BlockSpec((tm, w), lambda i: (i, 0))
    return pl.pallas_call(
        body, name=name, grid=(nb,),
        in_specs=[pl.BlockSpec((tm, 512), lambda i: (0, 0)),
                  pl.BlockSpec((tm, 512), lambda i: (jnp.minimum(i, NLAT - 1), kvcol)),
                  rb(NH * HP), rb(NH * DV), _row(KVL), pl.BlockSpec((KVL, NH * HP + NH * DV), lambda i: (0, 0)),
                  rb(HP), rb(HP), pl.BlockSpec(memory_space=pl.ANY)],
        out_specs=[pl.BlockSpec((tm, 512), lambda i: (jnp.minimum(i, NLAT - 1), kvcol)),
                   pl.BlockSpec((tm, 512), lambda i: (0, 0)), rb(NH * HP + NH * DV), _row(KVL)],
        out_shape=[jax.ShapeDtypeStruct((T, NIN), BF), jax.ShapeDtypeStruct((TC, 512), BF),
                   jax.ShapeDtypeStruct((TKV, NH * HP + NH * DV), BF), jax.ShapeDtypeStruct((1, KVL), F32)],
        input_output_aliases={8: 0},
        compiler_params=pltpu.CompilerParams(dimension_semantics=("arbitrary",)),
    )(pc, p, dk, dv, kvg, wkv2, ck, sk, dp)


def _pieces(src, width, n):
    out, c = [], src
    while c < src + width:
        k = c // n
        w = min(src + width, (k + 1) * n) - c
        out.append((k, c - k * n, c - src, w))
        c += w
    return out


def _win_moves():
    mv = [(2208, 1024, O_GA), (3232, 1024, O_GC), (0, KVL, O_KV), (256, DR, O_KV + KVL + DN), (288, QL, O_Q)]
    mv += [(256 + _swap_start(g), 8, O_KV + KVL + HP + DN + 8 * g) for g in range(4)]
    for j in range(CONV // CVB):
        base = O_CV + 3 * CVB * j
        mv += [(672 + CVB * j, CVB, base), (1184 + CVB * j, CVB, base + CVB), (1696 + CVB * j, CVB, base + 2 * CVB)]
    return mv


_WIN_ZERO = [(O_KV + KVL, DN), (O_KV + KVL + DN + DR, HP - DN - DR), (O_KV + KVL + HP, DN),
             (O_KV + KVL + HP + DN + DR, HP - DN - DR), (O_Q + QL, 512 - QL)]


def build_win(g, *, name, tm=256):
    def body(g_ref, o_ref):
        for src, w, dst in _win_moves():
            for k, a, off, pw in _pieces(src, w, SH_IN):
                o_ref[:, dst + off:dst + off + pw] = g_ref[k, :, a:a + pw]
        for c0, w in _WIN_ZERO:
            o_ref[:, c0:c0 + w] = jnp.zeros((tm, w), o_ref.dtype)

    return pl.pallas_call(
        body, name=name, grid=(D // tm,), in_specs=[pl.BlockSpec((NDEV, tm, SH_IN), lambda i: (0, i, 0))],
        out_specs=pl.BlockSpec((tm, NIN), lambda i: (i, 0)), out_shape=jax.ShapeDtypeStruct((D, NIN), g.dtype),
        compiler_params=pltpu.CompilerParams(dimension_semantics=("parallel",)),
    )(g)


def shard_win_grad(dwt, dwct, *, name, col0=0, tc=256):
    n = dwt.shape[1]

    def body(dw_ref, dwc_ref, o_ref, kvs):
        kvs[...] = dw_ref[O_KV:O_KV + 512, :] + dwc_ref[...]

        def src(row, w):
            if O_KV <= row < O_KV + 512:
                return kvs[row - O_KV:row - O_KV + w, :]
            return dw_ref[row:row + w, :]

        for s, w, dst in _win_moves():
            if w == 8 or s == 256:
                continue
            for k, a, off, pw in _pieces(s, w, SH_IN):
                o_ref[k, a:a + pw, :] = src(dst + off, pw).astype(o_ref.dtype)
        for g in range(4):
            val = src(O_KV + KVL + DN + 8 * g, 8) + src(O_KV + KVL + HP + DN + _swap_start(g), 8)
            o_ref[0, 256 + 8 * g:256 + 8 * g + 8, :] = val.astype(o_ref.dtype)

    return pl.pallas_call(
        body, name=name, grid=(n // tc,),
        in_specs=[pl.BlockSpec((NIN, tc), lambda j: (0, j)), pl.BlockSpec((512, tc), lambda j: (0, j + col0 // tc))],
        out_specs=pl.BlockSpec((NDEV, SH_IN, tc), lambda j: (0, 0, j)),
        out_shape=jax.ShapeDtypeStruct((NDEV, SH_IN, n), BF),
        scratch_shapes=[pltpu.VMEM((512, tc), F32)],
        compiler_params=pltpu.CompilerParams(dimension_semantics=("parallel",)),
    )(dwt, dwct)


def _eye(n, m):
    return (lax.broadcasted_iota(jnp.int32, (n, m), 0) == lax.broadcasted_iota(jnp.int32, (n, m), 1)).astype(BF)


_NT = (((1,), (1,)), ((), ()))


def build_wq_wkv(gq, gkv, *, name):
    def body(gq_ref, gkv_ref, q_ref, kv_ref):
        q_ref[...] = jnp.zeros_like(q_ref)
        kv_ref[...] = jnp.zeros_like(kv_ref)
        eye = _eye(QL, QL)
        for h in range(NH):
            qh = lax.dot_general(eye, gq_ref[h], _NT, preferred_element_type=F32).astype(q_ref.dtype)
            q_ref[0:QL, h * HP:h * HP + DN + DR] = qh
            for g in range(4):
                c0 = NH * HP + h * HP + DN + 8 * g
                q_ref[0:QL, c0:c0 + 8] = qh[:, DN + _swap_start(g):DN + _swap_start(g) + 8]
            kv_ref[:, h * HP:h * HP + DN] = gkv_ref[h, :, 0:DN]
            kv_ref[:, NH * HP + h * DV:NH * HP + (h + 1) * DV] = gkv_ref[h, :, DN:DN + DV]

    vm = pl.BlockSpec(memory_space=pltpu.VMEM)
    return pl.pallas_call(
        body, name=name, in_specs=[vm, vm], out_specs=[vm, vm],
        out_shape=[jax.ShapeDtypeStruct((512, 2 * NH * HP), gq.dtype), jax.ShapeDtypeStruct((KVL, NH * HP + NH * DV), gq.dtype)],
    )(gq, gkv)


def shard_wq_wkv_grad(dwq2, dwkv2, *, name):
    def body(q_ref, kv_ref, gq_ref, gkv_ref, xs):
        xs[...] = jnp.zeros_like(xs)
        eye = _eye(DN + DR, HP)
        for h in range(NH):
            xs[:, 0:DN] = q_ref[0:QL, h * HP:h * HP + DN].astype(BF)
            for g in range(4):
                a = q_ref[0:QL, h * HP + DN + 8 * g:h * HP + DN + 8 * g + 8]
                c0 = NH * HP + h * HP + DN + _swap_start(g)
                xs[:, DN + 8 * g:DN + 8 * g + 8] = (a + q_ref[0:QL, c0:c0 + 8]).astype(BF)
            gq_ref[h] = lax.dot_general(eye, xs[...], _NT, preferred_element_type=F32).astype(BF)
            gkv_ref[h, :, 0:DN] = kv_ref[:, h * HP:h * HP + DN].astype(BF)
            gkv_ref[h, :, DN:DN + DV] = kv_ref[:, NH * HP + h * DV:NH * HP + (h + 1) * DV].astype(BF)

    vm = pl.BlockSpec(memory_space=pltpu.VMEM)
    return pl.pallas_call(
        body, name=name, in_specs=[vm, vm], out_specs=[vm, vm],
        out_shape=[jax.ShapeDtypeStruct((NDEV, DN + DR, QL), BF), jax.ShapeDtypeStruct((NDEV, KVL, DN + DV), BF)],
        scratch_shapes=[pltpu.VMEM((QL, HP), BF)],
    )(dwq2, dwkv2)


def unshard_cols(g, *, name, tm=256):
    _, K, n = g.shape
    tm = _pick(K, tm, 16)

    def body(g_ref, o_ref):
        for k in range(NDEV):
            o_ref[:, k * n:(k + 1) * n] = g_ref[k]

    return pl.pallas_call(
        body, name=name, grid=(K // tm,), in_specs=[pl.BlockSpec((NDEV, tm, n), lambda i: (0, i, 0))],
        out_specs=pl.BlockSpec((tm, NDEV * n), lambda i: (i, 0)), out_shape=jax.ShapeDtypeStruct((K, NDEV * n), g.dtype),
        compiler_params=pltpu.CompilerParams(dimension_semantics=("parallel",)),
    )(g)


def shard_cols(w, *, name, tm=256):
    K, n8 = w.shape
    n = n8 // NDEV
    tm = _pick(K, tm, 16)

    def body(w_ref, o_ref):
        for k in range(NDEV):
            o_ref[k] = w_ref[:, k * n:(k + 1) * n]

    return pl.pallas_call(
        body, name=name, grid=(K // tm,), in_specs=[pl.BlockSpec((tm, n8), lambda i: (i, 0))],
        out_specs=pl.BlockSpec((NDEV, tm, n), lambda i: (0, i, 0)), out_shape=jax.ShapeDtypeStruct((NDEV, K, n), w.dtype),
        compiler_params=pltpu.CompilerParams(dimension_semantics=("parallel",)),
    )(w)


def _rope_tables():
    t = np.arange(T)
    row = (t // GRID_W).astype(np.float32)
    col = (t % GRID_W).astype(np.float32)
    axis_dim = DR // 2
    inv = (np.float32(ROPE_THETA) ** (-np.arange(0, axis_dim, 2, dtype=np.float32) / np.float32(axis_dim))).astype(np.float32)
    ar, ac = (row[:, None] * inv).astype(np.float32), (col[:, None] * inv).astype(np.float32)
    cosv = np.concatenate([np.cos(ar), np.cos(ar), np.cos(ac), np.cos(ac)], axis=1).astype(np.float32)
    sinv = np.concatenate([-np.sin(ar), np.sin(ar), -np.sin(ac), np.sin(ac)], axis=1).astype(np.float32)
    ck = np.zeros((TKV, HP), np.float32)
    sk = np.zeros((TKV, HP), np.float32)
    ck[T:, DN:DN + DR] = 1.0
    ck[:T, DN:DN + DR] = cosv
    sk[:T, DN:DN + DR] = sinv
    cq = np.zeros((T, HP), np.float32)
    cq[:, :DN] = 1.0
    cq[:, DN:DN + DR] = cosv
    return jnp.asarray(ck), jnp.asarray(sk), jnp.asarray(cq), jnp.asarray(sk[:T])


def _local_step(x, ctx, tgt, mod_lat, mod_ctx, n1g, qg, kvg, n2g, fg, conv_w, conv_b, ffn_w, ffn_b, get_w, put_g, dep0):
    sh1, sc1, g1, sh2, sc2, g2 = [mod_lat[:, i * D:(i + 1) * D] for i in range(6)]
    csh1, csc1 = mod_ctx[:, 0:D], mod_ctx[:, D:2 * D]
    ck, sk, cq_t, sq_t = _rope_tables()
    qg_p = jnp.pad(qg, ((0, 0), (0, 512 - QL)))

    hcat = normmod_cat(ctx, x, n1g, csc1, csh1, sc1, sh1, dep0, name="normmod1")
    win = get_w("in", hcat)
    p = mm(hcat, win, M=T, tn=1536, name="in_proj")
    pc = mm(hcat, win, M=TC, N=512, a_off=(T, 0), b_off=(0, O_KV), name="in_proj_ctx")
    wq2, wkv2, wao, wco, wo = get_w("mid", p)
    kh, vh, ckv = kvprep(pc, p, kvg, wkv2, ck, sk, name="kvprep")
    qr, cq = qprep(p, qg_p, wq2, cq_t, sq_t, name="qprep")
    o, lse = attn_fwd(qr, kh, vh, name="attn_fwd")
    z = convz(p, conv_w, conv_b, name="convz")
    ya, yc, merged = out_proj_merge(o, wao, z, wco, p, name="attn_conv_out_gate_merge")
    a_out, x1, h2 = oproj_resid(merged, wo, x, g1, n2g, sc2, sh2, name="o_proj_resid_normmod2")
    wup = get_w("up", h2)
    u0 = mm(h2, wup, tb=True, o_stack=True, tn=1408, name="up_proj")
    f = ffn_act(u0, ffn_w, ffn_b, name="ffn_act")
    wdn = get_w("down", f)
    dn, dx2, dd, dfg, loss = down_final(f, wdn, x1, g2, fg, tgt, name="down_proj_final_loss")

    df = mm(dd, wdn, tb=True, tn=1408, name="down_proj_dx")
    dwdn = mm(f, dd, ta=True, out_dtype=BF, tm=1408, name="down_proj_dw")
    du0, dffn_w, dffn_b = ffn_act_bwd(u0, df, ffn_w, ffn_b, name="ffn_act_bwd")
    dwup = mm(du0, h2, ta=True, a_stack=True, out_dtype=BF, tm=1408, name="up_proj_dw")
    tok = put_g("ffn", dict(dwup=dwup, dwdn=dwdn))
    dx1, da, st2 = normmod_bwd(x1, dict(a=du0, b=wup, a_stack=True, tk=DFF, dep=tok), n2g, sc2, dx2, dn, g1,
                               name="up_proj_dx_normmod2_bwd")

    dwo = mm(merged, da, ta=True, out_dtype=BF, tn=512, name="o_proj_dw")
    dya, dyc, dp, do, dz = oproj_dx_gate_bwd(da, wo, p, ya, yc, wao, wco, name="o_proj_dx_gate_merge_bwd")
    dwao = mm(o, dya, ta=True, out_dtype=BF, tn=512, name="attn_out_dw")
    dwco = mm(z, dyc, ta=True, out_dtype=BF, tn=512, name="conv_out_dw")
    tok = put_g("mid", dict(dwao=dwao, dwco=dwco, dwo=dwo))
    dp, dconv_w, dconv_b = convz_bwd(p, dz, conv_w, conv_b, dp, name="convz_bwd")
    dq, dk, dv = attn_bwd(qr, kh, vh, do, o, lse, tok, name="attn_bwd")
    dp, dq2, dqg = qprep_bwd(p, dq, qg_p, wq2, cq_t, sq_t, dp, name="qprep_bwd")
    dp, dpc, dkv2, dkvg = kvprep_bwd(pc, p, dk, dv, kvg, wkv2, ck, sk, dp, name="kvprep_bwd")

    dwin_c = mm(dpc, hcat, ta=True, K=TC, b_off=(T, 0), name="in_proj_ctx_dw")
    tok = None
    for half in range(2):
        dwin = mm(dp, hcat, ta=True, K=T, N=D // 2, b_off=(0, half * (D // 2)), tm=1536, dep=tok,
                  name=f"in_proj_dw_{half}")
        tok = put_g(f"in{half}", dict(dwin=dwin, dwin_c=dwin_c, col0=half * (D // 2)))
    dwq2 = mm(cq, dq2, ta=True, dep=tok, name="q_up_dw")
    dwkv2 = mm(ckv, dkv2, ta=True, dep=tok, name="kv_up_dw")
    tok = put_g("qkv", dict(dwq2=dwq2, dwkv2=dwkv2))
    dhc = mm(dpc, win, tb=True, N=D, K=512, b_off=(0, O_KV), dep=tok, name="in_proj_ctx_dx")
    dx, _, st1 = normmod_bwd(x, dict(a=dp, b=win, tb=True, tk=NIN, dep=tok), n1g, sc1, dx1, a_out, g1,
                             name="in_proj_dx_normmod1_bwd")
    stc = normmod_bwd(ctx, dhc, n1g, csc1, None, None, None, name="normmod1_ctx_bwd")

    return dict(loss=loss, dx=dx, st1=st1, st2=st2, stc=stc, dqg=dqg, dkvg=dkvg, dfg=dfg,
                dconv_w=dconv_w, dconv_b=dconv_b, dffn_w=dffn_w, dffn_b=dffn_b)


def _me():
    x, y, c = lax.axis_index("x"), lax.axis_index("y"), lax.axis_index("c")
    return x, y, c, 4 * x + 2 * y + c


def _peer(x, y, c, k):
    px = 1 - x if k & 4 else x
    py = 1 - y if k & 2 else y
    pc = 1 - c if k & 1 else c
    return (px, py, pc), 4 * px + 2 * py + pc


def _exchange_tiles(src_of_peer, buf, send_sem, recv_sem):
    x, y, c, me = _me()
    for k in range(1, NDEV):
        dev, lin = _peer(x, y, c, k)
        pltpu.make_async_remote_copy(src_ref=src_of_peer(lin), dst_ref=buf.at[me], send_sem=send_sem, recv_sem=recv_sem,
                                     device_id=dev, device_id_type=MESH).start()
    seven = buf.at[pl.ds(0, NDEV - 1)]
    pltpu.make_async_remote_copy(src_ref=seven, dst_ref=seven, send_sem=send_sem, recv_sem=recv_sem,
                                 device_id=(x, y, c), device_id_type=MESH).wait()


def _silu(z):
    return z * jax.nn.sigmoid(z)


def ada_fwd(c, c_ctx, ffn_w, conv_w, w_shard, b_ada, deps, *, name):
    nsh, nf, nc = w_shard.shape[1], ffn_w.shape[2], conv_w.shape[2]
    deps = [d for d in deps if d is not None]

    def body(c_ref, cc_ref, fw_ref, cw_ref, w_ref, b_ref, *rest):
        s_ref, ml_ref, mc_ref, fwf_ref, cwf_ref, m_ref, mine, res, sems = rest[len(deps):]
        x, y, c, me = _me()
        mine[...] = jnp.zeros_like(mine)
        mine[0:1, :] = _silu(c_ref[...])
        mine[1:2, :] = _silu(cc_ref[...])
        for k in range(3):
            mine[2 + k:3 + k, 0:fw_ref.shape[2]] = fw_ref[k]
            mine[5 + k:6 + k, 0:cw_ref.shape[2]] = cw_ref[k]
        s_ref[me] = mine[...]
        _exchange_tiles(lambda lin: mine, s_ref, sems.at[0], sems.at[1])
        sall = s_ref[...].reshape(NDEV * 8, D).astype(BF)
        r = jnp.dot(sall, w_ref[...].astype(BF), preferred_element_type=F32) + b_ref[me]
        res[...] = r.reshape(NDEV, 8, nsh)
        m_ref[me] = res[me]
        _exchange_tiles(lambda lin: res.at[lin], m_ref, sems.at[2], sems.at[3])
        for j in range(NDEV):
            ml_ref[:, j * nsh:(j + 1) * nsh] = m_ref[j, 0:1, :]
            mc_ref[:, j * nsh:(j + 1) * nsh] = m_ref[j, 1:2, :]
            fwf_ref[:, j * nf:(j + 1) * nf] = s_ref[j, 2:5, 0:nf]
            cwf_ref[:, j * nc:(j + 1) * nc] = s_ref[j, 5:8, 0:nc]

    vm = pl.BlockSpec(memory_space=pltpu.VMEM)
    return pl.pallas_call(
        body, name=name, in_specs=[vm] * 6 + [pl.BlockSpec(memory_space=pl.ANY)] * len(deps), out_specs=[vm] * 5,
        out_shape=[jax.ShapeDtypeStruct((NDEV, 8, D), F32), jax.ShapeDtypeStruct((1, NDEV * nsh), F32),
                   jax.ShapeDtypeStruct((1, NDEV * nsh), F32), jax.ShapeDtypeStruct((3, NDEV * nf), F32),
                   jax.ShapeDtypeStruct((3, NDEV * nc), F32)],
        scratch_shapes=[pltpu.VMEM((NDEV, 8, nsh), F32), pltpu.VMEM((8, D), F32), pltpu.VMEM((NDEV, 8, nsh), F32),
                        pltpu.SemaphoreType.DMA((4,))],
    )(c, c_ctx, ffn_w, conv_w, w_shard, b_ada, *deps)


P_DML, P_DMC, P_N1, P_QG, P_KVG, P_CB, P_N2, P_FB, P_FG, P_CW, P_FW, P_LOSS, P_ROWS = 0, 8, 16, 17, 18, 19, 20, 21, 27, 28, 31, 49, 56
NSH = 6 * D // NDEV
FROWS = 3


def pack_small(r, *, name):
    ins = [r["st1"], r["st2"], r["stc"], r["dqg"], r["dkvg"], r["dconv_b"], r["dffn_b"], r["dfg"], r["dconv_w"],
           r["dffn_w"], r["loss"]]

    def put_wide(p, row0, row, n):
        for j in range(-(-n // D)):
            w = min(D, n - j * D)
            p[row0 + j:row0 + j + 1, 0:w] = row[:, j * D:j * D + w]

    def body(st1, st2, stc, qg, kvg, cb, fb, fg, cw, fw, loss, p):
        p[...] = jnp.zeros_like(p)
        lat = (st1.at[0:1], st1.at[1:2], st1.at[3:4], st2.at[0:1], st2.at[1:2], st2.at[3:4])
        ctx = (stc.at[0:1], stc.at[1:2])
        for j in range(NDEV):
            done = 0
            while done < NSH:
                q, off = divmod(j * NSH + done, D)
                w = min(D - off, NSH - done)
                p[P_DML + j:P_DML + j + 1, done:done + w] = lat[q][:, off:off + w]
                if q < len(ctx):
                    p[P_DMC + j:P_DMC + j + 1, done:done + w] = ctx[q][:, off:off + w]
                done += w
        p[P_N1:P_N1 + 1, :] = st1[2:3, :] + stc[2:3, :]
        p[P_N2:P_N2 + 1, :] = st2[2:3, :]
        put_wide(p, P_QG, qg, 512)
        put_wide(p, P_KVG, kvg, KVL)
        put_wide(p, P_CB, cb, CONV)
        put_wide(p, P_FG, fg, D)
        put_wide(p, P_LOSS, loss, 128)
        for s in range(2):
            put_wide(p, P_FB + FROWS * s, fb.at[s], DFF)
        for k in range(3):
            put_wide(p, P_CW + k, cw.at[k:k + 1], CONV)
            for s in range(2):
                put_wide(p, P_FW + FROWS * (2 * k + s), fw.at[s, k:k + 1], DFF)

    vm = pl.BlockSpec(memory_space=pltpu.VMEM)
    return pl.pallas_call(
        body, name=name, in_specs=[vm] * len(ins), out_specs=vm, out_shape=jax.ShapeDtypeStruct((P_ROWS, D), F32),
    )(*ins)


def sum_slots(a, *, name):
    rows = dict(norm1_g=(P_N1, D), q_norm_g=(P_QG, QL), kv_norm_g=(P_KVG, KVL), conv_b=(P_CB, CONV), norm2_g=(P_N2, D),
                final_g=(P_FG, D))

    def body(a_ref, sum_ref, *out):
        acc = a_ref[0]
        for k in range(1, NDEV):
            acc = acc + a_ref[k]
        sum_ref[...] = acc
        for ref, (row, n) in zip(out, rows.values()):
            ref[...] = sum_ref[row:row + 1, 0:n]
        fb, bada = out[len(rows):]
        for s in range(2):
            for j in range(FROWS):
                w = min(D, DFF - j * D)
                row = P_FB + FROWS * s + j
                fb[:, s * DFF + j * D:s * DFF + j * D + w] = sum_ref[row:row + 1, 0:w]
        for j in range(NDEV):
            bada[:, j * NSH:(j + 1) * NSH] = (sum_ref[P_DML + j:P_DML + j + 1, 0:NSH]
                                              + sum_ref[P_DMC + j:P_DMC + j + 1, 0:NSH])

    vm = pl.BlockSpec(memory_space=pltpu.VMEM)
    widths = [n for _, n in rows.values()] + [2 * DFF, 6 * D]
    outs = pl.pallas_call(
        body, name=name, in_specs=[vm], out_specs=[vm] * (1 + len(widths)),
        out_shape=[jax.ShapeDtypeStruct(a.shape[1:], F32)] + [jax.ShapeDtypeStruct((1, n), F32) for n in widths])(a)
    return outs[0], dict(zip(list(rows) + ["ffn_conv_b", "b_ada"], outs[1:]))


def ada_bwd(s_all, a_all, a_sum, w_shard, c_ctx, *, name):
    nsh = w_shard.shape[1]
    assert nsh == NSH

    def body(s_ref, a_ref, sum_ref, w_ref, c_ref, dw_ref, gc_ref, s16, dm16, part, buf, sems):
        x, y, c, me = _me()
        s16[...] = jnp.zeros_like(s16)
        dm16[...] = jnp.zeros_like(dm16)
        for k in range(NDEV):
            s16[k:k + 1, :] = s_ref[k, 0:1, :]
            dm16[k:k + 1, :] = a_ref[k, pl.ds(P_DML + me, 1), 0:nsh]
        s16[8:9, :] = s_ref[0, 1:2, :]
        dm16[8:9, :] = sum_ref[pl.ds(P_DMC + me, 1), 0:nsh]
        dw_ref[...] = lax.dot_general(s16[...].astype(BF), dm16[...].astype(BF), (((0,), (0,)), ((), ())),
                                      preferred_element_type=F32)
        part[...] = lax.dot_general(dm16[8:16, :].astype(BF), w_ref[...].astype(BF), (((1,), (1,)), ((), ())),
                                    preferred_element_type=F32)
        buf[me] = part[...]
        _exchange_tiles(lambda lin: part, buf, sems.at[0], sems.at[1])
        acc = buf[0]
        for k in range(1, NDEV):
            acc = acc + buf[k]
        z = c_ref[...]
        sg = jax.nn.sigmoid(z)
        gc_ref[...] = acc * (sg * (1.0 + z * (1.0 - sg)))

    vm = pl.BlockSpec(memory_space=pltpu.VMEM)
    return pl.pallas_call(
        body, name=name, in_specs=[vm] * 5, out_specs=[vm, vm],
        out_shape=[jax.ShapeDtypeStruct((D, nsh), F32), jax.ShapeDtypeStruct((8, D), F32)],
        scratch_shapes=[pltpu.VMEM((16, D), F32), pltpu.VMEM((16, nsh), F32), pltpu.VMEM((8, D), F32),
                        pltpu.VMEM((NDEV, 8, D), F32), pltpu.SemaphoreType.DMA((2,))],
    )(s_all, a_all, a_sum, w_shard, c_ctx)


HBM_SPEC = pl.BlockSpec(memory_space=pltpu.HBM)
SEM_SPEC = pl.BlockSpec(memory_space=pltpu.SEMAPHORE)
EFFECT = pltpu.SideEffectType.DATAFLOW_SIDE_EFFECTING


ALL_PEERS = tuple(range(1, NDEV))
FIRST_HOP = (1, 2, 4, 6)
RELAY = (2, 4, 6)


def _exchange_copies(srcs, lands, send, recv, per_peer, peers):
    x, y, c, me = _me()
    n = len(peers)
    cps = []
    for t in range(len(srcs)):
        for j, k in enumerate(peers):
            dev, lin = _peer(x, y, c, k)
            cps.append(pltpu.make_async_remote_copy(
                src_ref=srcs[t].at[lin] if per_peer else srcs[t], dst_ref=lands[t].at[me],
                send_sem=send.at[n * t + j], recv_sem=recv.at[n * t + j], device_id=dev, device_id_type=MESH))
    return cps


def _relay_copies(lands, send, recv):
    x, y, c, me = _me()
    n = len(RELAY)
    cps = []
    for t in range(len(lands)):
        for j, k in enumerate(RELAY):
            slot = lands[t].at[_peer(x, y, c, k)[1]]
            cps.append(pltpu.make_async_remote_copy(
                src_ref=slot, dst_ref=slot, send_sem=send.at[n * t + j], recv_sem=recv.at[n * t + j],
                device_id=(x, y, 1 - c), device_id_type=MESH))
    return cps


def _own_copies(srcs, lands, own, per_peer):
    me = _me()[3]
    return [pltpu.make_async_copy(srcs[t].at[me] if per_peer else srcs[t], lands[t].at[me], own.at[t])
            for t in range(len(srcs))]


def exchange_start(srcs, *, per_peer, name, dep=None, peers=ALL_PEERS):
    nt = len(srcs)
    ns = len(peers) * nt
    land_shapes = [(a.shape if per_peer else (NDEV,) + a.shape) for a in srcs]
    deps = [] if dep is None else [dep]

    def body(*refs):
        src, land = refs[:nt], refs[nt:2 * nt]
        send, recv, own = refs[2 * nt + len(deps):2 * nt + len(deps) + 3]
        for cp in _exchange_copies(src, land, send, recv, per_peer, peers) + _own_copies(src, land, own, per_peer):
            cp.start()
        refs[-1][...] = jnp.zeros_like(refs[-1])

    hb = lambda a: pltpu.with_memory_space_constraint(a, pltpu.HBM)
    outs = pl.pallas_call(
        body, name=name,
        out_shape=(pltpu.SemaphoreType.DMA((ns,)), pltpu.SemaphoreType.DMA((ns,)), pltpu.SemaphoreType.DMA((nt,)),
                   *[pltpu.HBM(a.shape, a.dtype) for a in srcs], *[pltpu.HBM(s, a.dtype) for s, a in zip(land_shapes, srcs)],
                   jax.ShapeDtypeStruct((8, 128), F32)),
        in_specs=[HBM_SPEC] * (2 * nt) + [pl.BlockSpec(memory_space=pl.ANY)] * len(deps),
        out_specs=(SEM_SPEC, SEM_SPEC, SEM_SPEC, *([HBM_SPEC] * (2 * nt)), pl.BlockSpec(memory_space=pltpu.VMEM)),
        input_output_aliases={i: 3 + i for i in range(2 * nt)},
        compiler_params=pltpu.CompilerParams(has_side_effects=EFFECT),
    )(*[hb(a) for a in srcs], *[hb(lax.empty(s, a.dtype)) for s, a in zip(land_shapes, srcs)], *deps)
    return dict(send=outs[0], recv=outs[1], own=outs[2], src=list(outs[3:3 + nt]), land=list(outs[3 + nt:3 + 2 * nt]),
                token=outs[-1], per_peer=per_peer, peers=peers)


def exchange_wait(h, after, *, name):
    nt = len(h["src"])
    per_peer, peers = h["per_peer"], h["peers"]
    after = list(after) if isinstance(after, (list, tuple)) else [after]

    def body(*refs):
        src, land, send, recv, own = refs[:nt], refs[nt:2 * nt], refs[2 * nt], refs[2 * nt + 1], refs[2 * nt + 2]
        for cp in _exchange_copies(src, land, send, recv, per_peer, peers):
            cp.wait_send()
            cp.wait_recv()
        for cp in _own_copies(src, land, own, per_peer):
            cp.wait()

    outs = pl.pallas_call(
        body, name=name,
        out_shape=(*[pltpu.HBM(a.shape, a.dtype) for a in h["src"]], *[pltpu.HBM(a.shape, a.dtype) for a in h["land"]]),
        in_specs=[HBM_SPEC] * (2 * nt) + [SEM_SPEC, SEM_SPEC, SEM_SPEC] + [pl.BlockSpec(memory_space=pl.ANY)] * len(after),
        out_specs=tuple([HBM_SPEC] * (2 * nt)),
        input_output_aliases={i: i for i in range(2 * nt)},
        compiler_params=pltpu.CompilerParams(has_side_effects=EFFECT),
    )(*h["src"], *h["land"], h["send"], h["recv"], h["own"], *after)
    return list(outs[nt:])


def relay_start(lands, *, name):
    nt = len(lands)
    ns = len(RELAY) * nt

    def body(*refs):
        for cp in _relay_copies(refs[:nt], refs[nt], refs[nt + 1]):
            cp.start()

    outs = pl.pallas_call(
        body, name=name,
        out_shape=(pltpu.SemaphoreType.DMA((ns,)), pltpu.SemaphoreType.DMA((ns,)),
                   *[pltpu.HBM(a.shape, a.dtype) for a in lands]),
        in_specs=[HBM_SPEC] * nt, out_specs=(SEM_SPEC, SEM_SPEC, *([HBM_SPEC] * nt)),
        input_output_aliases={i: 2 + i for i in range(nt)},
        compiler_params=pltpu.CompilerParams(has_side_effects=EFFECT),
    )(*lands)
    return dict(send=outs[0], recv=outs[1], land=list(outs[2:]))


def relay_wait(h, *, name):
    nt = len(h["land"])

    def body(*refs):
        for cp in _relay_copies(refs[:nt], refs[nt], refs[nt + 1]):
            cp.wait_send()
            cp.wait_recv()

    outs = pl.pallas_call(
        body, name=name, out_shape=tuple(pltpu.HBM(a.shape, a.dtype) for a in h["land"]),
        in_specs=[HBM_SPEC] * nt + [SEM_SPEC, SEM_SPEC], out_specs=tuple([HBM_SPEC] * nt),
        input_output_aliases={i: i for i in range(nt)},
        compiler_params=pltpu.CompilerParams(has_side_effects=EFFECT),
    )(*h["land"], h["send"], h["recv"])
    return list(outs)


def _adamw_math(w, g, m, v):
    nm = B1 * m + (1.0 - B1) * g
    nv = B2 * v + (1.0 - B2) * (g * g)
    m_hat = nm / (1.0 - B1 ** STEP)
    v_hat = nv / (1.0 - B2 ** STEP)
    return -LR * (m_hat / (jnp.sqrt(v_hat) + AEPS) + WD * w), nm, nv


def adamw_many(ws, gs, ms, vs, *, name):
    n = len(ws)

    def body(*refs):
        for k in range(n):
            d, nm, nv = _adamw_math(refs[k][...], refs[n + k][...], refs[2 * n + k][...], refs[3 * n + k][...])
            refs[4 * n + k][...] = d
            refs[5 * n + k][...] = nm
            refs[6 * n + k][...] = nv

    vm = pl.BlockSpec(memory_space=pltpu.VMEM)
    sh = [jax.ShapeDtypeStruct(w.shape, F32) for w in ws]
    outs = pl.pallas_call(body, name=name, in_specs=[vm] * (4 * n), out_specs=[vm] * (3 * n), out_shape=sh * 3,
                          )(*ws, *gs, *ms, *vs)
    return outs[:n], outs[n:2 * n], outs[2 * n:]


def adamw(w, g, m, v, *, name, tr=256):
    R, C = w.shape
    tr = _pick(R, tr, 8)

    def body(w_ref, g_ref, m_ref, v_ref, d_ref, nm_ref, nv_ref):
        d_ref[...], nm_ref[...], nv_ref[...] = _adamw_math(w_ref[...], g_ref[...], m_ref[...], v_ref[...])

    blk = pl.BlockSpec((tr, C), lambda i: (i, 0))
    sh = jax.ShapeDtypeStruct((R, C), F32)
    return pl.pallas_call(
        body, name=name, grid=(R // tr,), in_specs=[blk, blk, blk, blk], out_specs=[blk, blk, blk],
        out_shape=[sh, sh, sh], compiler_params=pltpu.CompilerParams(dimension_semantics=("parallel",)),
    )(w, g, m, v)


def adamw_slots(w, slots, m, v, *, name, tr=256):
    unit = w.ndim == 3
    R, C = w.shape[0], w.shape[-1]
    parts = list(slots) if isinstance(slots, (list, tuple)) else [slots]
    n = len(parts)
    assert sum(s.shape[-1] for s in parts) == C
    if R % 16 == 0:
        tr = _pick(R, tr, 16)
    else:
        tr = 144

    def body(w_ref, *refs):
        s_refs, (m_ref, v_ref, g_ref, d_ref, nm_ref, nv_ref) = refs[:n], refs[n:]
        gs = []
        for s_ref in s_refs:
            g = s_ref[0].astype(F32)
            for k in range(1, NDEV):
                g = g + s_ref[k].astype(F32)
            gs.append(g)
        g = gs[0] if n == 1 else jnp.concatenate(gs, axis=-1)
        g_ref[...] = g
        d_ref[...], nm_ref[...], nv_ref[...] = _adamw_math(w_ref[...], g, m_ref[...], v_ref[...])

    blk = pl.BlockSpec((tr, None, C), lambda i: (i, 0, 0)) if unit else pl.BlockSpec((tr, C), lambda i: (i, 0))
    sh = jax.ShapeDtypeStruct(w.shape, F32)
    return pl.pallas_call(
        body, name=name, grid=(pl.cdiv(R, tr),),
        in_specs=[blk] + [pl.BlockSpec((NDEV, tr, s.shape[-1]), lambda i: (0, i, 0)) for s in parts] + [blk, blk],
        out_specs=[blk, blk, blk, blk], out_shape=[sh, sh, sh, sh],
        compiler_params=pltpu.CompilerParams(dimension_semantics=("parallel",)),
    )(w, *parts, m, v)


def kernel(x, c, ctx, c_ctx, w_ada, b_ada, norm1_g, w_in, q_norm_g, kv_norm_g, w_uq, w_ukv, conv_w, conv_b, w_attn_out, w_conv_out, w_o, norm2_g, w_up, ffn_conv_w, ffn_conv_b, w_down, final_g, loss_target, m_c_ctx, m_w_ada, m_b_ada, m_norm1_g, m_w_in, m_q_norm_g, m_kv_norm_g, m_w_uq, m_w_ukv, m_conv_w, m_conv_b, m_w_attn_out, m_w_conv_out, m_w_o, m_norm2_g, m_w_up, m_ffn_conv_w, m_ffn_conv_b, m_w_down, m_final_g, v_c_ctx, v_w_ada, v_b_ada, v_norm1_g, v_w_in, v_q_norm_g, v_kv_norm_g, v_w_uq, v_w_ukv, v_conv_w, v_conv_b, v_w_attn_out, v_w_conv_out, v_w_o, v_norm2_g, v_w_up, v_ffn_conv_w, v_ffn_conv_b, v_w_down, v_final_g):
    me = 4 * lax.axis_index("x") + 2 * lax.axis_index("y") + lax.axis_index("c")
    W = dict(c_ctx=c_ctx, w_ada=w_ada, b_ada=b_ada, norm1_g=norm1_g, w_in=w_in, q_norm_g=q_norm_g, kv_norm_g=kv_norm_g,
             w_uq=w_uq, w_ukv=w_ukv, conv_w=conv_w, conv_b=conv_b, w_attn_out=w_attn_out, w_conv_out=w_conv_out, w_o=w_o,
             norm2_g=norm2_g, w_up=w_up, ffn_conv_w=ffn_conv_w, ffn_conv_b=ffn_conv_b, w_down=w_down, final_g=final_g)
    M = dict(c_ctx=m_c_ctx, w_ada=m_w_ada, b_ada=m_b_ada, norm1_g=m_norm1_g, w_in=m_w_in, q_norm_g=m_q_norm_g,
             kv_norm_g=m_kv_norm_g, w_uq=m_w_uq, w_ukv=m_w_ukv, conv_w=m_conv_w, conv_b=m_conv_b, w_attn_out=m_w_attn_out,
             w_conv_out=m_w_conv_out, w_o=m_w_o, norm2_g=m_norm2_g, w_up=m_w_up, ffn_conv_w=m_ffn_conv_w,
             ffn_conv_b=m_ffn_conv_b, w_down=m_w_down, final_g=m_final_g)
    V = dict(c_ctx=v_c_ctx, w_ada=v_w_ada, b_ada=v_b_ada, norm1_g=v_norm1_g, w_in=v_w_in, q_norm_g=v_q_norm_g,
             kv_norm_g=v_kv_norm_g, w_uq=v_w_uq, w_ukv=v_w_ukv, conv_w=v_conv_w, conv_b=v_conv_b, w_attn_out=v_w_attn_out,
             w_conv_out=v_w_conv_out, w_o=v_w_o, norm2_g=v_norm2_g, w_up=v_w_up, ffn_conv_w=v_ffn_conv_w,
             ffn_conv_b=v_ffn_conv_b, w_down=v_w_down, final_g=v_final_g)
    names = list(W)
    transposed = ("w_up", "w_uq")
    as2d = lambda k, a: (a.reshape(1, -1) if a.ndim == 1 else
                         a[0].T if k in transposed else a.reshape(a.shape[-2], a.shape[-1]))
    W2 = {k: as2d(k, a) for k, a in W.items()}
    M2 = {k: as2d(k, a) for k, a in M.items()}
    V2 = {k: as2d(k, a) for k, a in V.items()}
    unit3 = lambda a: jnp.transpose(a, (2, 0, 1))
    W3, M3, V3 = unit3(W["w_in"]), unit3(M["w_in"]), unit3(V["w_in"])
    nsh = W2["w_ada"].shape[1]

    unit_mid = ("conv_w", "ffn_conv_w")
    mid3 = lambda a: jnp.transpose(a, (1, 0, 2))
    s_all, mod_lat, mod_ctx, ffn_w_full, conv_w_full = ada_fwd(
        c, W2["c_ctx"], mid3(W["ffn_conv_w"]), mid3(W["conv_w"]), W2["w_ada"], W["b_ada"].reshape(NDEV, 1, nsh), [],
        name="ada_fwd")

    stage_w = {"in": ["w_in"], "mid": ["w_uq", "w_ukv", "w_attn_out", "w_conv_out", "w_o"], "up": ["w_up"],
               "down": ["w_down"]}
    two_level = ("in", "mid")
    ag, tok = {}, mod_lat
    for st, nms in stage_w.items():
        ag[st] = exchange_start([W2[nm].astype(BF) for nm in nms], per_peer=False, dep=tok, name="ag_start_" + st,
                                peers=FIRST_HOP if st in two_level else ALL_PEERS)
        tok = ag[st]["token"]

    def get_w(stage, after):
        lands = exchange_wait(ag[stage], after, name="ag_wait_" + stage)
        if stage in two_level:
            lands = relay_wait(relay_start(lands, name="ag_relay_" + stage), name="ag_relay_wait_" + stage)
        g = dict(zip(stage_w[stage], lands))
        if stage == "in":
            return build_win(g["w_in"], name="build_win")
        if stage == "mid":
            wq2, wkv2 = build_wq_wkv(g["w_uq"], g["w_ukv"], name="build_wq_wkv")
            return (wq2, wkv2, unshard_cols(g["w_attn_out"], name="unshard_w_attn_out"),
                    unshard_cols(g["w_conv_out"], name="unshard_w_conv_out"), g["w_o"].reshape(D, D))
        if stage == "up":
            return g["w_up"].reshape(2 * DFF, D)
        return g["w_down"].reshape(DFF, D)

    stage_g = {"ffn": ["w_up", "w_down"], "mid": ["w_attn_out", "w_conv_out", "w_o"], "qkv": ["w_uq", "w_ukv"],
               "in": ["w_in"]}
    rs = {}

    def put_g(stage, g):
        if stage in ("in0", "in1"):
            parts = [shard_win_grad(g["dwin"], g["dwin_c"], col0=g["col0"], name="shard_win_grad_" + stage[-1])]
        elif stage == "mid":
            parts = [shard_cols(g["dwao"], name="shard_w_attn_out"), shard_cols(g["dwco"], name="shard_w_conv_out"),
                     g["dwo"].reshape(NDEV, D // NDEV, D)]
        elif stage == "qkv":
            parts = list(shard_wq_wkv_grad(g["dwq2"], g["dwkv2"], name="shard_wq_wkv_grad"))
        else:
            parts = [g["dwup"].reshape(NDEV, 2 * DFF // NDEV, D), g["dwdn"].reshape(NDEV, DFF // NDEV, D)]
        rs[stage] = exchange_start(parts, per_peer=True, name="rs_start_" + stage)
        return rs[stage]["token"]

    r = _local_step(x[0], ctx[0], loss_target[0], mod_lat, mod_ctx, W2["norm1_g"], W2["q_norm_g"], W2["kv_norm_g"],
                    W2["norm2_g"], W2["final_g"], conv_w_full, W2["conv_b"], ffn_w_full, W2["ffn_conv_b"], get_w, put_g,
                    ag["down"]["token"])

    G, DL, NM, NV = {}, {}, {}, {}

    def finish(stage, after):
        if stage == "in":
            halves = []
            for h in ("in0", "in1"):
                halves += exchange_wait(rs[h], after, name="rs_wait_" + h)
                after = halves[-1]
            G["w_in"], DL["w_in"], NM["w_in"], NV["w_in"] = adamw_slots(W3, halves, M3, V3, name="adamw_w_in")
            return DL["w_in"]
        for nm, sl in zip(stage_g[stage], exchange_wait(rs[stage], after, name="rs_wait_" + stage)):
            G[nm], DL[nm], NM[nm], NV[nm] = adamw_slots(W2[nm], sl, M2[nm], V2[nm], name="adamw_" + nm)
            after = DL[nm]
        return after

    sync = exchange_start([pack_small(r, name="pack_small")], per_peer=False, name="sync_start")
    after = sync["token"]
    for st in ("ffn", "mid", "in", "qkv"):
        after = finish(st, after)
    a_buf, = exchange_wait(sync, [DL[nm] for nms in stage_g.values() for nm in nms], name="sync_wait")
    ssum, g_vec = sum_slots(a_buf, name="sum_small")
    G.update(g_vec)
    loss = ssum[P_LOSS, 0]
    G["conv_w"] = lax.dynamic_slice(ssum[P_CW:P_CW + 3, :CONV], (0, me * (CONV // NDEV)), (3, CONV // NDEV))
    fw_full = ssum[P_FW:P_FW + 6 * FROWS].reshape(3, 2, FROWS * D)[:, :, :DFF].reshape(3, 2 * DFF)
    G["ffn_conv_w"] = lax.dynamic_slice(fw_full, (0, me * (2 * DFF // NDEV)), (3, 2 * DFF // NDEV))

    G["w_ada"], gcc = ada_bwd(s_all, a_buf, ssum, W2["w_ada"], W2["c_ctx"], name="ada_bwd")
    G["c_ctx"] = gcc[0:1]

    DL["w_ada"], NM["w_ada"], NV["w_ada"] = adamw(W2["w_ada"], G["w_ada"], M2["w_ada"], V2["w_ada"], name="adamw_w_ada")
    small = ["c_ctx", "b_ada", "norm1_g", "q_norm_g", "kv_norm_g", "conv_b", "norm2_g", "ffn_conv_b", "final_g", "conv_w",
             "ffn_conv_w"]
    view = lambda k, a3, a2: mid3(a3[k]) if k in unit_mid else a2[k]
    for k in unit_mid:
        G[k] = G[k].reshape(3, 1, -1)
    ds, nms, nvs = adamw_many([view(k, W, W2) for k in small], [G[k] for k in small],
                              [view(k, M, M2) for k in small], [view(k, V, V2) for k in small], name="adamw_small")
    for k, nm in enumerate(small):
        DL[nm], NM[nm], NV[nm] = ds[k], nms[k], nvs[k]

    def as_output(nm, a):
        if nm in transposed:
            return a.T[None]
        if nm == "w_in":
            return jnp.transpose(a, (1, 2, 0))
        if nm in unit_mid and a.ndim == 3:
            return jnp.transpose(a, (1, 0, 2))
        return a.reshape(W[nm].shape)

    outs = [loss, r["dx"][None]]
    for grp in (G, DL, NM, NV):
        outs += [as_output(nm, grp[nm]) for nm in names]
    return tuple(outs)
```
